```python
import math
import jax, jax.numpy as jnp
from jax import lax
import numpy as np

D_MODEL = 2048
BATCH = 16
SEQ = 2048
DEPTH = 2

N_META = 16
CHUNK = 128
NORM_EPS = 1e-6
HEAD_NORM_EPS = 1e-5
S5_WIDTH = D_MODEL // 2
S5_GROUP_SIZE = 16
S5_GROUPS = S5_WIDTH // S5_GROUP_SIZE
S5_STATE = 64
MLSTM_WIDTH = 3 * D_MODEL // 2
MLSTM_HEADS = 8
MLSTM_HEAD_DIM = MLSTM_WIDTH // MLSTM_HEADS
MLSTM_CONV = 4
QKV_BLOCK = 4
AB_INNER = S5_WIDTH + MLSTM_WIDTH
AB_IN = 2 * AB_INNER
SSD_INNER = 2 * D_MODEL
SSD_HEAD_DIM = 64
SSD_HEADS = SSD_INNER // SSD_HEAD_DIM
SSD_STATE = 128
SSD_GROUPS = 8
SSD_HPG = SSD_HEADS // SSD_GROUPS
SSD_CONV = 4
SSD_CONV_DIM = SSD_INNER + 2 * SSD_GROUPS * SSD_STATE
SSD_IN = SSD_INNER + SSD_CONV_DIM + SSD_HEADS
N_EVEN = (DEPTH + 1) // 2
N_ODD = DEPTH // 2

kernel_name = 'hybrid_s5_mlstm_ssd_meta'

F32 = jnp.float32


def rmsnorm(x, g):
    xf = x.astype(F32)
    y = xf * lax.rsqrt(jnp.mean(xf * xf, axis=-1, keepdims=True) + NORM_EPS)
    return (y * g.astype(F32)).astype(x.dtype)


def causal_dwconv(x, w, b):
    k, c = w.shape
    y = lax.conv_general_dilated(x, w[:, None, :].astype(x.dtype), window_strides=(1,),
                                 padding=[(k - 1, 0)], dimension_numbers=('NWC', 'WIO', 'NWC'),
                                 feature_group_count=c)
    return y + b.astype(x.dtype)


def run_chunked(step, carry, seqs):
    bsz = seqs[0].shape[0]
    carry, y_meta = step(carry, tuple(a[:, :N_META] for a in seqs))

    def to_chunks(a):
        r = a[:, N_META:]
        r = r.reshape(bsz, r.shape[1] // CHUNK, CHUNK, *r.shape[2:])
        return jnp.moveaxis(r, 1, 0)

    _, y_real = lax.scan(step, carry, tuple(to_chunks(a) for a in seqs))
    y_real = jnp.moveaxis(y_real, 0, 1)
    y_real = y_real.reshape(bsz, -1, *y_real.shape[3:])
    return jnp.concatenate([y_meta, y_real], axis=1)


def cmul(ar, ai, br, bi):
    return ar * br - ai * bi, ar * bi + ai * br


def s5_mixer(u, lam_re, lam_im, log_dt, b_re, b_im, c_re, c_im, d, glu_w, glu_b):
    bsz, t_len, _ = u.shape
    uf = u.astype(F32).reshape(bsz, t_len, S5_GROUPS, S5_GROUP_SIZE)
    dt = jnp.exp(log_dt.astype(F32))[:, None]
    lr, li = lam_re.astype(F32), lam_im.astype(F32)
    mag = jnp.exp(lr * dt)
    ar, ai = mag * jnp.cos(li * dt), mag * jnp.sin(li * dt)
    den = lr * lr + li * li
    qr = ((ar - 1.0) * lr + ai * li) / den
    qi = (ai * lr - (ar - 1.0) * li) / den
    bbr, bbi = cmul(qr[..., None], qi[..., None], b_re.astype(F32), b_im.astype(F32))
    cr, ci = c_re.astype(F32), c_im.astype(F32)

    def combine(e1, e2):
        a1r, a1i, s1r, s1i = e1
        a2r, a2i, s2r, s2i = e2
        pr, pi_ = cmul(a1r, a1i, a2r, a2i)
        tr, ti = cmul(a2r, a2i, s1r, s1i)
        return pr, pi_, tr + s2r, ti + s2i

    def step(carry, inp):
        sr, si = carry
        (uc,) = inp
        bur = jnp.einsum('bcgh,gph->bcgp', uc, bbr)
        bui = jnp.einsum('bcgh,gph->bcgp', uc, bbi)
        shp = bur.shape
        pr, pi_, xr, xi = lax.associative_scan(
            combine, (jnp.broadcast_to(ar, shp), jnp.broadcast_to(ai, shp), bur, bui), axis=1)
        hr, hi = cmul(pr, pi_, sr[:, None], si[:, None])
        xr, xi = xr + hr, xi + hi
        y = jnp.einsum('ghp,bcgp->bcgh', cr, xr) - jnp.einsum('ghp,bcgp->bcgh', ci, xi)
        return (xr[:, -1], xi[:, -1]), y

    zeros = jnp.zeros((bsz, S5_GROUPS, S5_STATE), F32)
    y = run_chunked(step, (zeros, zeros), (uf,))
    y = (y + d.astype(F32).reshape(S5_GROUPS, S5_GROUP_SIZE) * uf).reshape(bsz, t_len, S5_WIDTH)
    g = jax.nn.gelu(y)
    return g * jax.nn.sigmoid(g @ glu_w.astype(F32) + glu_b.astype(F32))


def mlstm_mixer(xm, conv_w, conv_b, wq, wk, wv, w_gate, b_gate, norm_w, skip):
    bsz, t_len, _ = xm.shape
    xf = xm.astype(F32)
    xc = jax.nn.silu(causal_dwconv(xf, conv_w.astype(F32), conv_b.astype(F32)))

    def headwise(a, w):
        nb = w.shape[0]
        return jnp.einsum('btni,nio->btno', a.reshape(bsz, t_len, nb, QKV_BLOCK),
                          w.astype(F32)).reshape(bsz, t_len, MLSTM_WIDTH)

    q = headwise(xc, wq)
    k = headwise(xc, wk)
    v = headwise(xf, wv)
    wg = w_gate.astype(F32)
    gates = (q @ wg[:MLSTM_WIDTH] + k @ wg[MLSTM_WIDTH:2 * MLSTM_WIDTH]
             + v @ wg[2 * MLSTM_WIDTH:] + b_gate.astype(F32))
    ig = gates[..., :MLSTM_HEADS]
    lf = jax.nn.log_sigmoid(gates[..., MLSTM_HEADS:])
    hs = (bsz, t_len, MLSTM_HEADS, MLSTM_HEAD_DIM)
    qh = q.reshape(hs) * (MLSTM_HEAD_DIM ** -0.5)
    kh = k.reshape(hs)
    vh = v.reshape(hs)

    def step(carry, inp):
        cmat, nvec, m = carry
        qc, kc, vc, igc, lfc = inp
        c = qc.shape[1]
        bcum = jnp.cumsum(lfc, axis=1)
        causal = jnp.tril(jnp.ones((c, c), bool))[None, :, :, None]
        dmat = bcum[:, :, None, :] - bcum[:, None, :, :] + igc[:, None, :, :]
        dmat = jnp.where(causal, dmat, -jnp.inf)
        inter = bcum + m[:, None, :]
        mt = jnp.maximum(inter, dmat.max(axis=2))
        wt = jnp.exp(dmat - mt[:, :, None, :])
        w_prev = jnp.exp(inter - mt)
        s = jnp.einsum('bthd,bshd->btsh', qc, kc) * wt
        num = (jnp.einsum('btsh,bshe->bthe', s, vc)
               + w_prev[..., None] * jnp.einsum('bthd,bhde->bthe', qc, cmat))
        den = s.sum(axis=2) + w_prev * jnp.einsum('bthd,bhd->bth', qc, nvec)
        h = num / jnp.maximum(jnp.abs(den), jnp.exp(-mt))[..., None]
        blast = bcum[:, -1]
        g = blast[:, None, :] - bcum + igc
        m_new = jnp.maximum(blast + m, g.max(axis=1))
        decay = jnp.exp(blast + m - m_new)
        wkc = jnp.exp(g - m_new[:, None, :])[..., None] * kc
        c_new = decay[..., None, None] * cmat + jnp.einsum('bshd,bshe->bhde', wkc, vc)
        n_new = decay[..., None] * nvec + wkc.sum(axis=1)
        return (c_new, n_new, m_new), h

    carry0 = (jnp.zeros((bsz, MLSTM_HEADS, MLSTM_HEAD_DIM, MLSTM_HEAD_DIM), F32),
              jnp.zeros((bsz, MLSTM_HEADS, MLSTM_HEAD_DIM), F32),
              jnp.zeros((bsz, MLSTM_HEADS), F32))
    h = run_chunked(step, carry0, (qh, kh, vh, ig, lf))
    mu = jnp.mean(h, axis=-1, keepdims=True)
    var = jnp.mean(jnp.square(h - mu), axis=-1, keepdims=True)
    hn = (h - mu) * lax.rsqrt(var + HEAD_NORM_EPS) * norm_w.astype(F32).reshape(MLSTM_HEADS, MLSTM_HEAD_DIM)
    return hn.reshape(bsz, t_len, MLSTM_WIDTH) + skip.astype(F32) * xc


def ab_layer(x, norm_g, w_in, s5_lambda_re, s5_lambda_im, s5_log_dt, s5_b_re, s5_b_im, s5_c_re,
             s5_c_im, s5_d, s5_glu_w, s5_glu_b, ml_conv_w, ml_conv_b, ml_wq, ml_wk, ml_wv,
             ml_w_gate, ml_b_gate, ml_norm, ml_skip, w_out):
    p = rmsnorm(x, norm_g) @ w_in
    u_a, z_a, x_b, z_b = jnp.split(p, [S5_WIDTH, 2 * S5_WIDTH, 2 * S5_WIDTH + MLSTM_WIDTH], axis=-1)
    y_a = s5_mixer(u_a, s5_lambda_re, s5_lambda_im, s5_log_dt, s5_b_re, s5_b_im, s5_c_re, s5_c_im,
                   s5_d, s5_glu_w, s5_glu_b) * jax.nn.silu(z_a.astype(F32))
    y_b = mlstm_mixer(x_b, ml_conv_w, ml_conv_b, ml_wq, ml_wk, ml_wv, ml_w_gate, ml_b_gate,
                      ml_norm, ml_skip) * jax.nn.silu(z_b.astype(F32))
    y = jnp.concatenate([y_a, y_b], axis=-1).astype(x.dtype)
    return x + y @ w_out


def ssd_layer(x, norm_g, w_in, conv_w, conv_b, dt_bias, a_log, d, gnorm, w_out):
    bsz, t_len, _ = x.shape
    p = (rmsnorm(x, norm_g) @ w_in).astype(F32)
    z, xbc, dt = jnp.split(p, [SSD_INNER, SSD_INNER + SSD_CONV_DIM], axis=-1)
    xbc = jax.nn.silu(causal_dwconv(xbc, conv_w.astype(F32), conv_b.astype(F32)))
    xs, bm, cm = jnp.split(xbc, [SSD_INNER, SSD_INNER + SSD_GROUPS * SSD_STATE], axis=-1)
    xs = xs.reshape(bsz, t_len, SSD_GROUPS, SSD_HPG, SSD_HEAD_DIM)
    bm = bm.reshape(bsz, t_len, SSD_GROUPS, SSD_STATE)
    cm = cm.reshape(bsz, t_len, SSD_GROUPS, SSD_STATE)
    dt = jax.nn.softplus(dt + dt_bias.astype(F32)).reshape(bsz, t_len, SSD_GROUPS, SSD_HPG)
    a = -jnp.exp(a_log.astype(F32)).reshape(SSD_GROUPS, SSD_HPG)

    def step(state, inp):
        xc, dtc, bc, cc = inp
        c = xc.shape[1]
        cum = jnp.cumsum(dtc * a, axis=1)
        causal = jnp.tril(jnp.ones((c, c), bool))[None, :, :, None, None]
        seg = jnp.exp(jnp.where(causal, cum[:, :, None] - cum[:, None], -jnp.inf))
        cb = jnp.einsum('btgn,bsgn->btsg', cc, bc)
        w = cb[..., None] * seg * dtc[:, None]
        y = (jnp.einsum('btsgr,bsgrp->btgrp', w, xc)
             + jnp.exp(cum)[..., None] * jnp.einsum('btgn,bgrpn->btgrp', cc, state))
        last = cum[:, -1]
        dec = jnp.exp(last[:, None] - cum) * dtc
        state = (jnp.exp(last)[..., None, None] * state
                 + jnp.einsum('bsgr,bsgn,bsgrp->bgrpn', dec, bc, xc))
        return state, y

    state0 = jnp.zeros((bsz, SSD_GROUPS, SSD_HPG, SSD_HEAD_DIM, SSD_STATE), F32)
    y = run_chunked(step, state0, (xs, dt, bm, cm))
    y = y + d.astype(F32).reshape(SSD_GROUPS, SSD_HPG, 1) * xs
    y = y.reshape(bsz, t_len, SSD_INNER) * jax.nn.silu(z)
    yg = y.reshape(bsz, t_len, SSD_GROUPS, -1)
    yg = yg * lax.rsqrt(jnp.mean(yg * yg, axis=-1, keepdims=True) + NORM_EPS)
    y = yg.reshape(bsz, t_len, SSD_INNER) * gnorm.astype(F32)
    return x + y.astype(x.dtype) @ w_out


def _fwd_setup_inputs(seed: int = 0) -> dict:
    key = jax.random.key(seed)
    ks = iter(jax.random.split(key, 48))
    nrm = lambda shape, s: jax.random.normal(next(ks), shape, F32) * s
    ne, no = N_EVEN, N_ODD
    lam_im = jnp.pi * jnp.arange(S5_STATE, dtype=F32)
    gate_b = jnp.concatenate([
        nrm((ne, MLSTM_HEADS), 0.1),
        jnp.linspace(3.0, 6.0, MLSTM_HEADS, dtype=F32)[None] + nrm((ne, MLSTM_HEADS), 0.01)], axis=-1)
    dt0 = jnp.exp(jax.random.uniform(next(ks), (no, SSD_HEADS), F32, math.log(1e-3), math.log(1e-1)))
    return {
        'x': nrm((BATCH, SEQ, D_MODEL), 1.0),
        'meta_tokens': nrm((N_META, D_MODEL), 1.0),
        'ab_norm': 1.0 + nrm((ne, D_MODEL), 0.02),
        'ab_w_in': nrm((ne, D_MODEL, AB_IN), D_MODEL ** -0.5),
        's5_lambda_re': -0.5 + nrm((ne, S5_GROUPS, S5_STATE), 0.01),
        's5_lambda_im': lam_im + nrm((ne, S5_GROUPS, S5_STATE), 0.01),
        's5_log_dt': jax.random.uniform(next(ks), (ne, S5_GROUPS), F32, math.log(1e-3), math.log(1e-1)),
        's5_b_re': nrm((ne, S5_GROUPS, S5_STATE, S5_GROUP_SIZE), (2 * S5_GROUP_SIZE) ** -0.5),
        's5_b_im': nrm((ne, S5_GROUPS, S5_STATE, S5_GROUP_SIZE), (2 * S5_GROUP_SIZE) ** -0.5),
        's5_c_re': nrm((ne, S5_GROUPS, S5_GROUP_SIZE, S5_STATE), (2 * S5_STATE) ** -0.5),
        's5_c_im': nrm((ne, S5_GROUPS, S5_GROUP_SIZE, S5_STATE), (2 * S5_STATE) ** -0.5),
        's5_d': nrm((ne, S5_WIDTH), 1.0),
        's5_glu_w': nrm((ne, S5_WIDTH, S5_WIDTH), S5_WIDTH ** -0.5),
        's5_glu_b': nrm((ne, S5_WIDTH), 0.01),
        'ml_conv_w': nrm((ne, MLSTM_CONV, MLSTM_WIDTH), MLSTM_CONV ** -0.5),
        'ml_conv_b': nrm((ne, MLSTM_WIDTH), 0.01),
        'ml_wq': nrm((ne, MLSTM_WIDTH // QKV_BLOCK, QKV_BLOCK, QKV_BLOCK), QKV_BLOCK ** -0.5),
        'ml_wk': nrm((ne, MLSTM_WIDTH // QKV_BLOCK, QKV_BLOCK, QKV_BLOCK), QKV_BLOCK ** -0.5),
        'ml_wv': nrm((ne, MLSTM_WIDTH // QKV_BLOCK, QKV_BLOCK, QKV_BLOCK), QKV_BLOCK ** -0.5),
        'ml_w_gate': nrm((ne, 3 * MLSTM_WIDTH, 2 * MLSTM_HEADS), (3 * MLSTM_WIDTH) ** -0.5),
        'ml_b_gate': gate_b,
        'ml_norm': 1.0 + nrm((ne, MLSTM_WIDTH), 0.02),
        'ml_skip': 1.0 + nrm((ne, MLSTM_WIDTH), 0.02),
        'ab_w_out': nrm((ne, AB_INNER, D_MODEL), AB_INNER ** -0.5),
        'ssd_norm': 1.0 + nrm((no, D_MODEL), 0.02),
        'ssd_w_in': nrm((no, D_MODEL, SSD_IN), D_MODEL ** -0.5),
        'ssd_conv_w': nrm((no, SSD_CONV, SSD_CONV_DIM), SSD_CONV ** -0.5),
        'ssd_conv_b': nrm((no, SSD_CONV_DIM), 0.01),
        'ssd_dt_bias': dt0 + jnp.log(-jnp.expm1(-dt0)),
        'ssd_a_log': jnp.log(jax.random.uniform(next(ks), (no, SSD_HEADS), F32, 1.0, 16.0)),
        'ssd_d': 1.0 + nrm((no, SSD_HEADS), 0.02),
        'ssd_gnorm': 1.0 + nrm((no, SSD_INNER), 0.02),
        'ssd_w_out': nrm((no, SSD_INNER, D_MODEL), SSD_INNER ** -0.5),
        'final_norm': 1.0 + nrm((D_MODEL,), 0.02),
    }


def _fwd_reference(x, meta_tokens, ab_norm, ab_w_in, s5_lambda_re, s5_lambda_im, s5_log_dt, s5_b_re,
              s5_b_im, s5_c_re, s5_c_im, s5_d, s5_glu_w, s5_glu_b, ml_conv_w, ml_conv_b, ml_wq,
              ml_wk, ml_wv, ml_w_gate, ml_b_gate, ml_norm, ml_skip, ab_w_out, ssd_norm, ssd_w_in,
              ssd_conv_w, ssd_conv_b, ssd_dt_bias, ssd_a_log, ssd_d, ssd_gnorm, ssd_w_out, final_norm):
    bsz = x.shape[0]
    meta = jnp.broadcast_to(meta_tokens[None].astype(x.dtype), (bsz, N_META, x.shape[-1]))
    h = jnp.concatenate([meta, x], axis=1)
    for layer in range(DEPTH):
        i = layer // 2
        if layer % 2 == 0:
            h = ab_layer(h, ab_norm[i], ab_w_in[i], s5_lambda_re[i], s5_lambda_im[i], s5_log_dt[i],
                         s5_b_re[i], s5_b_im[i], s5_c_re[i], s5_c_im[i], s5_d[i], s5_glu_w[i],
                         s5_glu_b[i], ml_conv_w[i], ml_conv_b[i], ml_wq[i], ml_wk[i], ml_wv[i],
                         ml_w_gate[i], ml_b_gate[i], ml_norm[i], ml_skip[i], ab_w_out[i])
        else:
            h = ssd_layer(h, ssd_norm[i], ssd_w_in[i], ssd_conv_w[i], ssd_conv_b[i], ssd_dt_bias[i],
                          ssd_a_log[i], ssd_d[i], ssd_gnorm[i], ssd_w_out[i])
    return rmsnorm(h, final_norm)[:, N_META:]


import jax as _jax
import jax.numpy as _jnp

TWIN_FORMAT = 'train_step'
FWD_PARAMS = ['x', 'meta_tokens', 'ab_norm', 'ab_w_in', 's5_lambda_re', 's5_lambda_im', 's5_log_dt', 's5_b_re', 's5_b_im', 's5_c_re', 's5_c_im', 's5_d', 's5_glu_w', 's5_glu_b', 'ml_conv_w', 'ml_conv_b', 'ml_wq', 'ml_wk', 'ml_wv', 'ml_w_gate', 'ml_b_gate', 'ml_norm', 'ml_skip', 'ab_w_out', 'ssd_norm', 'ssd_w_in', 'ssd_conv_w', 'ssd_conv_b', 'ssd_dt_bias', 'ssd_a_log', 'ssd_d', 'ssd_gnorm', 'ssd_w_out', 'final_norm']
TWIN_WEIGHTS = ['meta_tokens', 'ab_norm', 'ab_w_in', 's5_lambda_re', 's5_lambda_im', 's5_log_dt', 's5_b_re', 's5_b_im', 's5_c_re', 's5_c_im', 's5_d', 's5_glu_w', 's5_glu_b', 'ml_conv_w', 'ml_conv_b', 'ml_wq', 'ml_wk', 'ml_wv', 'ml_w_gate', 'ml_b_gate', 'ml_norm', 'ml_skip', 'ab_w_out', 'ssd_norm', 'ssd_w_in', 'ssd_conv_w', 'ssd_conv_b', 'ssd_dt_bias', 'ssd_a_log', 'ssd_d', 'ssd_gnorm', 'ssd_w_out', 'final_norm']
TWIN_DIFF_INPUT = 'x'
TWIN_INPUTS = ['x', 'meta_tokens', 'ab_norm', 'ab_w_in', 's5_lambda_re', 's5_lambda_im', 's5_log_dt', 's5_b_re', 's5_b_im', 's5_c_re', 's5_c_im', 's5_d', 's5_glu_w', 's5_glu_b', 'ml_conv_w', 'ml_conv_b', 'ml_wq', 'ml_wk', 'ml_wv', 'ml_w_gate', 'ml_b_gate', 'ml_norm', 'ml_skip', 'ab_w_out', 'ssd_norm', 'ssd_w_in', 'ssd_conv_w', 'ssd_conv_b', 'ssd_dt_bias', 'ssd_a_log', 'ssd_d', 'ssd_gnorm', 'ssd_w_out', 'final_norm', 'loss_target', 'm_meta_tokens', 'm_ab_norm', 'm_ab_w_in', 'm_s5_lambda_re', 'm_s5_lambda_im', 'm_s5_log_dt', 'm_s5_b_re', 'm_s5_b_im', 'm_s5_c_re', 'm_s5_c_im', 'm_s5_d', 'm_s5_glu_w', 'm_s5_glu_b', 'm_ml_conv_w', 'm_ml_conv_b', 'm_ml_wq', 'm_ml_wk', 'm_ml_wv', 'm_ml_w_gate', 'm_ml_b_gate', 'm_ml_norm', 'm_ml_skip', 'm_ab_w_out', 'm_ssd_norm', 'm_ssd_w_in', 'm_ssd_conv_w', 'm_ssd_conv_b', 'm_ssd_dt_bias', 'm_ssd_a_log', 'm_ssd_d', 'm_ssd_gnorm', 'm_ssd_w_out', 'm_final_norm', 'v_meta_tokens', 'v_ab_norm', 'v_ab_w_in', 'v_s5_lambda_re', 'v_s5_lambda_im', 'v_s5_log_dt', 'v_s5_b_re', 'v_s5_b_im', 'v_s5_c_re', 'v_s5_c_im', 'v_s5_d', 'v_s5_glu_w', 'v_s5_glu_b', 'v_ml_conv_w', 'v_ml_conv_b', 'v_ml_wq', 'v_ml_wk', 'v_ml_wv', 'v_ml_w_gate', 'v_ml_b_gate', 'v_ml_norm', 'v_ml_skip', 'v_ab_w_out', 'v_ssd_norm', 'v_ssd_w_in', 'v_ssd_conv_w', 'v_ssd_conv_b', 'v_ssd_dt_bias', 'v_ssd_a_log', 'v_ssd_d', 'v_ssd_gnorm', 'v_ssd_w_out', 'v_final_norm']
TWIN_OUTPUTS = ['loss', 'grad_x', 'grad_meta_tokens', 'grad_ab_norm', 'grad_ab_w_in', 'grad_s5_lambda_re', 'grad_s5_lambda_im', 'grad_s5_log_dt', 'grad_s5_b_re', 'grad_s5_b_im', 'grad_s5_c_re', 'grad_s5_c_im', 'grad_s5_d', 'grad_s5_glu_w', 'grad_s5_glu_b', 'grad_ml_conv_w', 'grad_ml_conv_b', 'grad_ml_wq', 'grad_ml_wk', 'grad_ml_wv', 'grad_ml_w_gate', 'grad_ml_b_gate', 'grad_ml_norm', 'grad_ml_skip', 'grad_ab_w_out', 'grad_ssd_norm', 'grad_ssd_w_in', 'grad_ssd_conv_w', 'grad_ssd_conv_b', 'grad_ssd_dt_bias', 'grad_ssd_a_log', 'grad_ssd_d', 'grad_ssd_gnorm', 'grad_ssd_w_out', 'grad_final_norm', 'delta_meta_tokens', 'delta_ab_norm', 'delta_ab_w_in', 'delta_s5_lambda_re', 'delta_s5_lambda_im', 'delta_s5_log_dt', 'delta_s5_b_re', 'delta_s5_b_im', 'delta_s5_c_re', 'delta_s5_c_im', 'delta_s5_d', 'delta_s5_glu_w', 'delta_s5_glu_b', 'delta_ml_conv_w', 'delta_ml_conv_b', 'delta_ml_wq', 'delta_ml_wk', 'delta_ml_wv', 'delta_ml_w_gate', 'delta_ml_b_gate', 'delta_ml_norm', 'delta_ml_skip', 'delta_ab_w_out', 'delta_ssd_norm', 'delta_ssd_w_in', 'delta_ssd_conv_w', 'delta_ssd_conv_b', 'delta_ssd_dt_bias', 'delta_ssd_a_log', 'delta_ssd_d', 'delta_ssd_gnorm', 'delta_ssd_w_out', 'delta_final_norm', 'new_m_meta_tokens', 'new_m_ab_norm', 'new_m_ab_w_in', 'new_m_s5_lambda_re', 'new_m_s5_lambda_im', 'new_m_s5_log_dt', 'new_m_s5_b_re', 'new_m_s5_b_im', 'new_m_s5_c_re', 'new_m_s5_c_im', 'new_m_s5_d', 'new_m_s5_glu_w', 'new_m_s5_glu_b', 'new_m_ml_conv_w', 'new_m_ml_conv_b', 'new_m_ml_wq', 'new_m_ml_wk', 'new_m_ml_wv', 'new_m_ml_w_gate', 'new_m_ml_b_gate', 'new_m_ml_norm', 'new_m_ml_skip', 'new_m_ab_w_out', 'new_m_ssd_norm', 'new_m_ssd_w_in', 'new_m_ssd_conv_w', 'new_m_ssd_conv_b', 'new_m_ssd_dt_bias', 'new_m_ssd_a_log', 'new_m_ssd_d', 'new_m_ssd_gnorm', 'new_m_ssd_w_out', 'new_m_final_norm', 'new_v_meta_tokens', 'new_v_ab_norm', 'new_v_ab_w_in', 'new_v_s5_lambda_re', 'new_v_s5_lambda_im', 'new_v_s5_log_dt', 'new_v_s5_b_re', 'new_v_s5_b_im', 'new_v_s5_c_re', 'new_v_s5_c_im', 'new_v_s5_d', 'new_v_s5_glu_w', 'new_v_s5_glu_b', 'new_v_ml_conv_w', 'new_v_ml_conv_b', 'new_v_ml_wq', 'new_v_ml_wk', 'new_v_ml_wv', 'new_v_ml_w_gate', 'new_v_ml_b_gate', 'new_v_ml_norm', 'new_v_ml_skip', 'new_v_ab_w_out', 'new_v_ssd_norm', 'new_v_ssd_w_in', 'new_v_ssd_conv_w', 'new_v_ssd_conv_b', 'new_v_ssd_dt_bias', 'new_v_ssd_a_log', 'new_v_ssd_d', 'new_v_ssd_gnorm', 'new_v_ssd_w_out', 'new_v_final_norm']
TWIN_LEAF_KINDS = {'loss': 'loss', 'grad_x': 'grad_x', 'grad_meta_tokens': 'grad_w', 'grad_ab_norm': 'grad_w', 'grad_ab_w_in': 'grad_w', 'grad_s5_lambda_re': 'grad_w', 'grad_s5_lambda_im': 'grad_w', 'grad_s5_log_dt': 'grad_w', 'grad_s5_b_re': 'grad_w', 'grad_s5_b_im': 'grad_w', 'grad_s5_c_re': 'grad_w', 'grad_s5_c_im': 'grad_w', 'grad_s5_d': 'grad_w', 'grad_s5_glu_w': 'grad_w', 'grad_s5_glu_b': 'grad_w', 'grad_ml_conv_w': 'grad_w', 'grad_ml_conv_b': 'grad_w', 'grad_ml_wq': 'grad_w', 'grad_ml_wk': 'grad_w', 'grad_ml_wv': 'grad_w', 'grad_ml_w_gate': 'grad_w', 'grad_ml_b_gate': 'grad_w', 'grad_ml_norm': 'grad_w', 'grad_ml_skip': 'grad_w', 'grad_ab_w_out': 'grad_w', 'grad_ssd_norm': 'grad_w', 'grad_ssd_w_in': 'grad_w', 'grad_ssd_conv_w': 'grad_w', 'grad_ssd_conv_b': 'grad_w', 'grad_ssd_dt_bias': 'grad_w', 'grad_ssd_a_log': 'grad_w', 'grad_ssd_d': 'grad_w', 'grad_ssd_gnorm': 'grad_w', 'grad_ssd_w_out': 'grad_w', 'grad_final_norm': 'grad_w', 'delta_meta_tokens': 'delta_w', 'delta_ab_norm': 'delta_w', 'delta_ab_w_in': 'delta_w', 'delta_s5_lambda_re': 'delta_w', 'delta_s5_lambda_im': 'delta_w', 'delta_s5_log_dt': 'delta_w', 'delta_s5_b_re': 'delta_w', 'delta_s5_b_im': 'delta_w', 'delta_s5_c_re': 'delta_w', 'delta_s5_c_im': 'delta_w', 'delta_s5_d': 'delta_w', 'delta_s5_glu_w': 'delta_w', 'delta_s5_glu_b': 'delta_w', 'delta_ml_conv_w': 'delta_w', 'delta_ml_conv_b': 'delta_w', 'delta_ml_wq': 'delta_w', 'delta_ml_wk': 'delta_w', 'delta_ml_wv': 'delta_w', 'delta_ml_w_gate': 'delta_w', 'delta_ml_b_gate': 'delta_w', 'delta_ml_norm': 'delta_w', 'delta_ml_skip': 'delta_w', 'delta_ab_w_out': 'delta_w', 'delta_ssd_norm': 'delta_w', 'delta_ssd_w_in': 'delta_w', 'delta_ssd_conv_w': 'delta_w', 'delta_ssd_conv_b': 'delta_w', 'delta_ssd_dt_bias': 'delta_w', 'delta_ssd_a_log': 'delta_w', 'delta_ssd_d': 'delta_w', 'delta_ssd_gnorm': 'delta_w', 'delta_ssd_w_out': 'delta_w', 'delta_final_norm': 'delta_w', 'new_m_meta_tokens': 'new_m', 'new_m_ab_norm': 'new_m', 'new_m_ab_w_in': 'new_m', 'new_m_s5_lambda_re': 'new_m', 'new_m_s5_lambda_im': 'new_m', 'new_m_s5_log_dt': 'new_m', 'new_m_s5_b_re': 'new_m', 'new_m_s5_b_im': 'new_m', 'new_m_s5_c_re': 'new_m', 'new_m_s5_c_im': 'new_m', 'new_m_s5_d': 'new_m', 'new_m_s5_glu_w': 'new_m', 'new_m_s5_glu_b': 'new_m', 'new_m_ml_conv_w': 'new_m', 'new_m_ml_conv_b': 'new_m', 'new_m_ml_wq': 'new_m', 'new_m_ml_wk': 'new_m', 'new_m_ml_wv': 'new_m', 'new_m_ml_w_gate': 'new_m', 'new_m_ml_b_gate': 'new_m', 'new_m_ml_norm': 'new_m', 'new_m_ml_skip': 'new_m', 'new_m_ab_w_out': 'new_m', 'new_m_ssd_norm': 'new_m', 'new_m_ssd_w_in': 'new_m', 'new_m_ssd_conv_w': 'new_m', 'new_m_ssd_conv_b': 'new_m', 'new_m_ssd_dt_bias': 'new_m', 'new_m_ssd_a_log': 'new_m', 'new_m_ssd_d': 'new_m', 'new_m_ssd_gnorm': 'new_m', 'new_m_ssd_w_out': 'new_m', 'new_m_final_norm': 'new_m', 'new_v_meta_tokens': 'new_v', 'new_v_ab_norm': 'new_v', 'new_v_ab_w_in': 'new_v', 'new_v_s5_lambda_re': 'new_v', 'new_v_s5_lambda_im': 'new_v', 'new_v_s5_log_dt': 'new_v', 'new_v_s5_b_re': 'new_v', 'new_v_s5_b_im': 'new_v', 'new_v_s5_c_re': 'new_v', 'new_v_s5_c_im': 'new_v', 'new_v_s5_d': 'new_v', 'new_v_s5_glu_w': 'new_v', 'new_v_s5_glu_b': 'new_v', 'new_v_ml_conv_w': 'new_v', 'new_v_ml_conv_b': 'new_v', 'new_v_ml_wq': 'new_v', 'new_v_ml_wk': 'new_v', 'new_v_ml_wv': 'new_v', 'new_v_ml_w_gate': 'new_v', 'new_v_ml_b_gate': 'new_v', 'new_v_ml_norm': 'new_v', 'new_v_ml_skip': 'new_v', 'new_v_ab_w_out': 'new_v', 'new_v_ssd_norm': 'new_v', 'new_v_ssd_w_in': 'new_v', 'new_v_ssd_conv_w': 'new_v', 'new_v_ssd_conv_b': 'new_v', 'new_v_ssd_dt_bias': 'new_v', 'new_v_ssd_a_log': 'new_v', 'new_v_ssd_d': 'new_v', 'new_v_ssd_gnorm': 'new_v', 'new_v_ssd_w_out': 'new_v', 'new_v_final_norm': 'new_v'}


def _forward(args):
    return _fwd_reference(*[args[k] for k in FWD_PARAMS])


def _output_shape():
    out = _jax.eval_shape(lambda: _forward(_fwd_setup_inputs(0)))
    return out.shape, out.dtype

N_MICROBATCH = 1
ADAM_LR = 0.001
ADAM_B1 = 0.9
ADAM_B2 = 0.999
ADAM_EPS = 1e-08
ADAM_WD = 0.01
ADAM_STEP = 10
PER_EXAMPLE_BATCH_AXIS = {'x': 0, 'loss_target': 0}
SHARED_INPUTS = []
_WEIGHT_DTYPES = {'meta_tokens': _jnp.float32, 'ab_norm': _jnp.float32, 'ab_w_in': _jnp.float32, 's5_lambda_re': _jnp.float32, 's5_lambda_im': _jnp.float32, 's5_log_dt': _jnp.float32, 's5_b_re': _jnp.float32, 's5_b_im': _jnp.float32, 's5_c_re': _jnp.float32, 's5_c_im': _jnp.float32, 's5_d': _jnp.float32, 's5_glu_w': _jnp.float32, 's5_glu_b': _jnp.float32, 'ml_conv_w': _jnp.float32, 'ml_conv_b': _jnp.float32, 'ml_wq': _jnp.float32, 'ml_wk': _jnp.float32, 'ml_wv': _jnp.float32, 'ml_w_gate': _jnp.float32, 'ml_b_gate': _jnp.float32, 'ml_norm': _jnp.float32, 'ml_skip': _jnp.float32, 'ab_w_out': _jnp.float32, 'ssd_norm': _jnp.float32, 'ssd_w_in': _jnp.float32, 'ssd_conv_w': _jnp.float32, 'ssd_conv_b': _jnp.float32, 'ssd_dt_bias': _jnp.float32, 'ssd_a_log': _jnp.float32, 'ssd_d': _jnp.float32, 'ssd_gnorm': _jnp.float32, 'ssd_w_out': _jnp.float32, 'final_norm': _jnp.float32}
MOMENT_SCALE = {'meta_tokens': 5.802023e-03, 'ab_norm': 1.177518e-01, 'ab_w_in': 5.840638e-02, 's5_lambda_re': 7.594457e-04, 's5_lambda_im': 7.369353e-04, 's5_log_dt': 4.986728e-01, 's5_b_re': 4.773515e-04, 's5_b_im': 4.720899e-04, 's5_c_re': 9.438781e-04, 's5_c_im': 9.430893e-04, 's5_d': 1.766913e-02, 's5_glu_w': 4.174752e-03, 's5_glu_b': 6.988508e-03, 'ml_conv_w': 6.828199e-02, 'ml_conv_b': 6.521836e-02, 'ml_wq': 4.565207e-02, 'ml_wk': 4.485631e-02, 'ml_wv': 4.439894e-02, 'ml_w_gate': 3.667534e-01, 'ml_b_gate': 1.356292e-01, 'ml_norm': 4.088240e-02, 'ml_skip': 2.353380e-02, 'ab_w_out': 5.720932e-02, 'ssd_norm': 9.252867e-02, 'ssd_w_in': 3.846916e-02, 'ssd_conv_w': 3.482755e-02, 'ssd_conv_b': 4.954253e-02, 'ssd_dt_bias': 8.461687e-02, 'ssd_a_log': 1.095774e-01, 'ssd_d': 1.759747e-01, 'ssd_gnorm': 4.024246e-02, 'ssd_w_out': 5.804009e-02, 'final_norm': 1.596730e+01}


def _to_microbatches(a, axis):
    t = _jnp.moveaxis(a, axis, 0)
    t = t.reshape((N_MICROBATCH, t.shape[0] // N_MICROBATCH) + t.shape[1:])
    return _jnp.moveaxis(t, 1, axis + 1)


def setup_inputs(seed: int = 0) -> dict:
    inp = _fwd_setup_inputs(seed)
    key = _jax.random.fold_in(_jax.random.key(seed), 7919)
    shape, _ = _output_shape()
    out = dict(inp)
    out["loss_target"] = _jax.random.normal(_jax.random.fold_in(key, 0), shape, _jnp.float32)
    for i, name in enumerate(TWIN_WEIGHTS):
        w = inp[name].astype(_jnp.float32)
        if MOMENT_SCALE is None:
            s = _jnp.sqrt(_jnp.mean(_jnp.square(w)) + 1e-30)
        else:
            s = MOMENT_SCALE[name]
        km, kv = _jax.random.split(_jax.random.fold_in(key, i + 1))
        out[name] = w
        out["m_" + name] = s * _jax.random.normal(km, w.shape, _jnp.float32)
        out["v_" + name] = (s * s) * _jax.random.uniform(kv, w.shape, _jnp.float32, 0.5, 1.5)
    if N_MICROBATCH > 1:
        for name, axis in PER_EXAMPLE_BATCH_AXIS.items():
            out[name] = _to_microbatches(out[name], axis)
    return {'x': out['x'], 'meta_tokens': out['meta_tokens'], 'ab_norm': out['ab_norm'], 'ab_w_in': out['ab_w_in'], 's5_lambda_re': out['s5_lambda_re'], 's5_lambda_im': out['s5_lambda_im'], 's5_log_dt': out['s5_log_dt'], 's5_b_re': out['s5_b_re'], 's5_b_im': out['s5_b_im'], 's5_c_re': out['s5_c_re'], 's5_c_im': out['s5_c_im'], 's5_d': out['s5_d'], 's5_glu_w': out['s5_glu_w'], 's5_glu_b': out['s5_glu_b'], 'ml_conv_w': out['ml_conv_w'], 'ml_conv_b': out['ml_conv_b'], 'ml_wq': out['ml_wq'], 'ml_wk': out['ml_wk'], 'ml_wv': out['ml_wv'], 'ml_w_gate': out['ml_w_gate'], 'ml_b_gate': out['ml_b_gate'], 'ml_norm': out['ml_norm'], 'ml_skip': out['ml_skip'], 'ab_w_out': out['ab_w_out'], 'ssd_norm': out['ssd_norm'], 'ssd_w_in': out['ssd_w_in'], 'ssd_conv_w': out['ssd_conv_w'], 'ssd_conv_b': out['ssd_conv_b'], 'ssd_dt_bias': out['ssd_dt_bias'], 'ssd_a_log': out['ssd_a_log'], 'ssd_d': out['ssd_d'], 'ssd_gnorm': out['ssd_gnorm'], 'ssd_w_out': out['ssd_w_out'], 'final_norm': out['final_norm'], 'loss_target': out['loss_target'], 'm_meta_tokens': out['m_meta_tokens'], 'm_ab_norm': out['m_ab_norm'], 'm_ab_w_in': out['m_ab_w_in'], 'm_s5_lambda_re': out['m_s5_lambda_re'], 'm_s5_lambda_im': out['m_s5_lambda_im'], 'm_s5_log_dt': out['m_s5_log_dt'], 'm_s5_b_re': out['m_s5_b_re'], 'm_s5_b_im': out['m_s5_b_im'], 'm_s5_c_re': out['m_s5_c_re'], 'm_s5_c_im': out['m_s5_c_im'], 'm_s5_d': out['m_s5_d'], 'm_s5_glu_w': out['m_s5_glu_w'], 'm_s5_glu_b': out['m_s5_glu_b'], 'm_ml_conv_w': out['m_ml_conv_w'], 'm_ml_conv_b': out['m_ml_conv_b'], 'm_ml_wq': out['m_ml_wq'], 'm_ml_wk': out['m_ml_wk'], 'm_ml_wv': out['m_ml_wv'], 'm_ml_w_gate': out['m_ml_w_gate'], 'm_ml_b_gate': out['m_ml_b_gate'], 'm_ml_norm': out['m_ml_norm'], 'm_ml_skip': out['m_ml_skip'], 'm_ab_w_out': out['m_ab_w_out'], 'm_ssd_norm': out['m_ssd_norm'], 'm_ssd_w_in': out['m_ssd_w_in'], 'm_ssd_conv_w': out['m_ssd_conv_w'], 'm_ssd_conv_b': out['m_ssd_conv_b'], 'm_ssd_dt_bias': out['m_ssd_dt_bias'], 'm_ssd_a_log': out['m_ssd_a_log'], 'm_ssd_d': out['m_ssd_d'], 'm_ssd_gnorm': out['m_ssd_gnorm'], 'm_ssd_w_out': out['m_ssd_w_out'], 'm_final_norm': out['m_final_norm'], 'v_meta_tokens': out['v_meta_tokens'], 'v_ab_norm': out['v_ab_norm'], 'v_ab_w_in': out['v_ab_w_in'], 'v_s5_lambda_re': out['v_s5_lambda_re'], 'v_s5_lambda_im': out['v_s5_lambda_im'], 'v_s5_log_dt': out['v_s5_log_dt'], 'v_s5_b_re': out['v_s5_b_re'], 'v_s5_b_im': out['v_s5_b_im'], 'v_s5_c_re': out['v_s5_c_re'], 'v_s5_c_im': out['v_s5_c_im'], 'v_s5_d': out['v_s5_d'], 'v_s5_glu_w': out['v_s5_glu_w'], 'v_s5_glu_b': out['v_s5_glu_b'], 'v_ml_conv_w': out['v_ml_conv_w'], 'v_ml_conv_b': out['v_ml_conv_b'], 'v_ml_wq': out['v_ml_wq'], 'v_ml_wk': out['v_ml_wk'], 'v_ml_wv': out['v_ml_wv'], 'v_ml_w_gate': out['v_ml_w_gate'], 'v_ml_b_gate': out['v_ml_b_gate'], 'v_ml_norm': out['v_ml_norm'], 'v_ml_skip': out['v_ml_skip'], 'v_ab_w_out': out['v_ab_w_out'], 'v_ssd_norm': out['v_ssd_norm'], 'v_ssd_w_in': out['v_ssd_w_in'], 'v_ssd_conv_w': out['v_ssd_conv_w'], 'v_ssd_conv_b': out['v_ssd_conv_b'], 'v_ssd_dt_bias': out['v_ssd_dt_bias'], 'v_ssd_a_log': out['v_ssd_a_log'], 'v_ssd_d': out['v_ssd_d'], 'v_ssd_gnorm': out['v_ssd_gnorm'], 'v_ssd_w_out': out['v_ssd_w_out'], 'v_final_norm': out['v_final_norm']}


def _loss(weights, diff, rest, loss_target):
    with _jax.named_scope("forward"):
        args = {**rest, TWIN_DIFF_INPUT: diff, **{k: w.astype(_WEIGHT_DTYPES[k]) for k, w in weights.items()}}
        y = _forward(args)
    with _jax.named_scope("loss_head"):
        err = _jnp.square(y.astype(_jnp.float32) - loss_target)
        return 0.5 * _jnp.sum(_jnp.mean(err, axis=-1)) if err.ndim else 0.5 * err


def _adamw(w, g, m, v):
    m = ADAM_B1 * m + (1.0 - ADAM_B1) * g
    v = ADAM_B2 * v + (1.0 - ADAM_B2) * _jnp.square(g)
    m_hat = m / (1.0 - ADAM_B1 ** ADAM_STEP)
    v_hat = v / (1.0 - ADAM_B2 ** ADAM_STEP)
    delta = -ADAM_LR * (m_hat / (_jnp.sqrt(v_hat) + ADAM_EPS) + ADAM_WD * w)
    return delta, m, v


def reference(x, meta_tokens, ab_norm, ab_w_in, s5_lambda_re, s5_lambda_im, s5_log_dt, s5_b_re, s5_b_im, s5_c_re, s5_c_im, s5_d, s5_glu_w, s5_glu_b, ml_conv_w, ml_conv_b, ml_wq, ml_wk, ml_wv, ml_w_gate, ml_b_gate, ml_norm, ml_skip, ab_w_out, ssd_norm, ssd_w_in, ssd_conv_w, ssd_conv_b, ssd_dt_bias, ssd_a_log, ssd_d, ssd_gnorm, ssd_w_out, final_norm, loss_target, m_meta_tokens, m_ab_norm, m_ab_w_in, m_s5_lambda_re, m_s5_lambda_im, m_s5_log_dt, m_s5_b_re, m_s5_b_im, m_s5_c_re, m_s5_c_im, m_s5_d, m_s5_glu_w, m_s5_glu_b, m_ml_conv_w, m_ml_conv_b, m_ml_wq, m_ml_wk, m_ml_wv, m_ml_w_gate, m_ml_b_gate, m_ml_norm, m_ml_skip, m_ab_w_out, m_ssd_norm, m_ssd_w_in, m_ssd_conv_w, m_ssd_conv_b, m_ssd_dt_bias, m_ssd_a_log, m_ssd_d, m_ssd_gnorm, m_ssd_w_out, m_final_norm, v_meta_tokens, v_ab_norm, v_ab_w_in, v_s5_lambda_re, v_s5_lambda_im, v_s5_log_dt, v_s5_b_re, v_s5_b_im, v_s5_c_re, v_s5_c_im, v_s5_d, v_s5_glu_w, v_s5_glu_b, v_ml_conv_w, v_ml_conv_b, v_ml_wq, v_ml_wk, v_ml_wv, v_ml_w_gate, v_ml_b_gate, v_ml_norm, v_ml_skip, v_ab_w_out, v_ssd_norm, v_ssd_w_in, v_ssd_conv_w, v_ssd_conv_b, v_ssd_dt_bias, v_ssd_a_log, v_ssd_d, v_ssd_gnorm, v_ssd_w_out, v_final_norm):
    given = dict(x=x, meta_tokens=meta_tokens, ab_norm=ab_norm, ab_w_in=ab_w_in, s5_lambda_re=s5_lambda_re, s5_lambda_im=s5_lambda_im, s5_log_dt=s5_log_dt, s5_b_re=s5_b_re, s5_b_im=s5_b_im, s5_c_re=s5_c_re, s5_c_im=s5_c_im, s5_d=s5_d, s5_glu_w=s5_glu_w, s5_glu_b=s5_glu_b, ml_conv_w=ml_conv_w, ml_conv_b=ml_conv_b, ml_wq=ml_wq, ml_wk=ml_wk, ml_wv=ml_wv, ml_w_gate=ml_w_gate, ml_b_gate=ml_b_gate, ml_norm=ml_norm, ml_skip=ml_skip, ab_w_out=ab_w_out, ssd_norm=ssd_norm, ssd_w_in=ssd_w_in, ssd_conv_w=ssd_conv_w, ssd_conv_b=ssd_conv_b, ssd_dt_bias=ssd_dt_bias, ssd_a_log=ssd_a_log, ssd_d=ssd_d, ssd_gnorm=ssd_gnorm, ssd_w_out=ssd_w_out, final_norm=final_norm, loss_target=loss_target, m_meta_tokens=m_meta_tokens, m_ab_norm=m_ab_norm, m_ab_w_in=m_ab_w_in, m_s5_lambda_re=m_s5_lambda_re, m_s5_lambda_im=m_s5_lambda_im, m_s5_log_dt=m_s5_log_dt, m_s5_b_re=m_s5_b_re, m_s5_b_im=m_s5_b_im, m_s5_c_re=m_s5_c_re, m_s5_c_im=m_s5_c_im, m_s5_d=m_s5_d, m_s5_glu_w=m_s5_glu_w, m_s5_glu_b=m_s5_glu_b, m_ml_conv_w=m_ml_conv_w, m_ml_conv_b=m_ml_conv_b, m_ml_wq=m_ml_wq, m_ml_wk=m_ml_wk, m_ml_wv=m_ml_wv, m_ml_w_gate=m_ml_w_gate, m_ml_b_gate=m_ml_b_gate, m_ml_norm=m_ml_norm, m_ml_skip=m_ml_skip, m_ab_w_out=m_ab_w_out, m_ssd_norm=m_ssd_norm, m_ssd_w_in=m_ssd_w_in, m_ssd_conv_w=m_ssd_conv_w, m_ssd_conv_b=m_ssd_conv_b, m_ssd_dt_bias=m_ssd_dt_bias, m_ssd_a_log=m_ssd_a_log, m_ssd_d=m_ssd_d, m_ssd_gnorm=m_ssd_gnorm, m_ssd_w_out=m_ssd_w_out, m_final_norm=m_final_norm, v_meta_tokens=v_meta_tokens, v_ab_norm=v_ab_norm, v_ab_w_in=v_ab_w_in, v_s5_lambda_re=v_s5_lambda_re, v_s5_lambda_im=v_s5_lambda_im, v_s5_log_dt=v_s5_log_dt, v_s5_b_re=v_s5_b_re, v_s5_b_im=v_s5_b_im, v_s5_c_re=v_s5_c_re, v_s5_c_im=v_s5_c_im, v_s5_d=v_s5_d, v_s5_glu_w=v_s5_glu_w, v_s5_glu_b=v_s5_glu_b, v_ml_conv_w=v_ml_conv_w, v_ml_conv_b=v_ml_conv_b, v_ml_wq=v_ml_wq, v_ml_wk=v_ml_wk, v_ml_wv=v_ml_wv, v_ml_w_gate=v_ml_w_gate, v_ml_b_gate=v_ml_b_gate, v_ml_norm=v_ml_norm, v_ml_skip=v_ml_skip, v_ab_w_out=v_ab_w_out, v_ssd_norm=v_ssd_norm, v_ssd_w_in=v_ssd_w_in, v_ssd_conv_w=v_ssd_conv_w, v_ssd_conv_b=v_ssd_conv_b, v_ssd_dt_bias=v_ssd_dt_bias, v_ssd_a_log=v_ssd_a_log, v_ssd_d=v_ssd_d, v_ssd_gnorm=v_ssd_gnorm, v_ssd_w_out=v_ssd_w_out, v_final_norm=v_final_norm)
    weights = {n: given[n] for n in TWIN_WEIGHTS}
    shared = {n: given[n] for n in SHARED_INPUTS}
    per_example = {n: given[n] for n in ['x']}
    grad_fn = _jax.value_and_grad(_loss, argnums=(0, 1))

    def one_microbatch(ex, loss_target):
        ex = dict(ex)
        diff = ex.pop(TWIN_DIFF_INPUT)
        return grad_fn(weights, diff, {**shared, **ex}, loss_target)

    if N_MICROBATCH == 1:
        loss, (grad_w, grad_x) = one_microbatch(per_example, given["loss_target"])
    else:
        def body(carry, xs):
            loss_sum, grad_sum = carry
            l_k, (gw_k, gx_k) = one_microbatch(xs[0], xs[1])
            with _jax.named_scope("update"):
                return (loss_sum + l_k, _jax.tree.map(_jnp.add, grad_sum, gw_k)), gx_k

        init = (_jnp.zeros((), _jnp.float32), _jax.tree.map(_jnp.zeros_like, weights))
        (loss, grad_w), grad_x = _jax.lax.scan(body, init, (per_example, given["loss_target"]))
    with _jax.named_scope("update"):
        delta_w, new_m, new_v = {}, {}, {}
        for n in TWIN_WEIGHTS:
            delta_w[n], new_m[n], new_v[n] = _adamw(weights[n], grad_w[n], given["m_" + n], given["v_" + n])
    return (loss, grad_x, *[grad_w[n] for n in TWIN_WEIGHTS], *[delta_w[n] for n in TWIN_WEIGHTS],
            *[new_m[n] for n in TWIN_WEIGHTS], *[new_v[n] for n in TWIN_WEIGHTS])
```

```python
import functools
import math

import jax
import jax.numpy as jnp
from jax import lax
from jax.experimental import pallas as pl
from jax.experimental.pallas import tpu as pltpu

F32 = jnp.float32
BF16 = jnp.bfloat16

D_MODEL = 2048
SEQ = 2048
N_META = 16
CHUNK = 128
PAD_ROWS = CHUNK - N_META
NORM_EPS = 1e-6
HEAD_NORM_EPS = 1e-5
S5_WIDTH = 1024
S5_GROUPS = 64
S5_GROUP_SIZE = 16
S5_STATE = 64
S5_GB = 8
S5_LANES = S5_GB * S5_STATE
ML_WIDTH = 3072
ML_HEADS = 8
ML_DH = 384
ML_CONV = 4
QKV_BLOCK = 4
SSD_INNER = 4096
SSD_HEADS = 64
SSD_P = 64
SSD_N = 128
SSD_GROUPS = 8
SSD_HPG = 8
SSD_GW = SSD_HPG * SSD_P
N_DEV = 8
ADAM_LR, ADAM_B1, ADAM_B2, ADAM_EPS, ADAM_WD, ADAM_STEP = 0.001, 0.9, 0.999, 1e-08, 0.01, 10
NEG = -1e30
VMEM_CAP = 60 * 1024 * 1024
MESH = pl.DeviceIdType.MESH

NN = (((1,), (0,)), ((), ()))
NT = (((1,), (1,)), ((), ()))
TN = (((0,), (0,)), ((), ()))


def _dot(a, b, dims=NN):
    return lax.dot_general(a, b, dims, preferred_element_type=F32)


def _bf(x):
    return x.astype(BF16)


def _pick(n, cands):
    for c in cands:
        if n % c == 0:
            return c
    return n


def _nbytes(shape, dtype):
    return math.prod(shape) * jnp.dtype(dtype).itemsize


def _pc(body, *, name, grid, in_specs, out_specs, out_shape, scratch=(), vmem=None):
    limit = None if vmem is None else int(min(VMEM_CAP, max(32 * 1024 * 1024, 2 * vmem + (8 << 20))))
    return pl.pallas_call(
        body, name=name, grid=grid, in_specs=in_specs, out_specs=out_specs, out_shape=out_shape,
        scratch_shapes=list(scratch),
        compiler_params=pltpu.CompilerParams(dimension_semantics=("arbitrary",) * len(grid), vmem_limit_bytes=limit))


def _silu(x):
    return x * jax.nn.sigmoid(x)


def _dsilu(x):
    s = jax.nn.sigmoid(x)
    return s * (1.0 + x * (1.0 - s))


def _gelu_and_grad(x):
    c0 = math.sqrt(2.0 / math.pi)
    inner = c0 * (x + 0.044715 * x * x * x)
    t = jnp.tanh(inner)
    g = 0.5 * x * (1.0 + t)
    dg = 0.5 * (1.0 + t) + 0.5 * x * (1.0 - t * t) * c0 * (1.0 + 3 * 0.044715 * x * x)
    return g, dg


def _mm(a, b, mode, *, name, add=None, out_dtype=F32, tn=None, slabs=False):
    if mode == "NN":
        (m, k), (k2, n) = a.shape, b.shape
    elif mode == "NT":
        (m, k), (n, k2) = a.shape, b.shape
    else:
        (k, m), (k2, n) = a.shape, b.shape
    assert k == k2, (a.shape, b.shape, mode)
    tm = _pick(m, (1088, 1024, 768, 512, 384, 256, 128))
    tn = tn or _pick(n, (512, 384, 256, 128))
    tk = _pick(k, (2048, 1088, 1024, 768, 512, 384, 256, 128))
    nk = k // tk
    dims = {"NN": NN, "NT": NT, "TN": TN}[mode]

    def body(*refs):
        a_ref, b_ref = refs[0], refs[1]
        add_ref = refs[2] if add is not None else None
        o_ref, acc_ref = refs[-2], refs[-1]
        kk = pl.program_id(2)

        @pl.when(kk == 0)
        def _():
            acc_ref[...] = jnp.zeros_like(acc_ref)

        acc_ref[...] += _dot(_bf(a_ref[...]), _bf(b_ref[...]), dims)

        @pl.when(kk == nk - 1)
        def _():
            r = acc_ref[...]
            if add_ref is not None:
                r = r + add_ref[...]
            o_ref[...] = r.reshape(o_ref.shape).astype(o_ref.dtype)

    if mode == "NN":
        a_spec = pl.BlockSpec((tm, tk), lambda i, j, kk: (i, kk))
        b_spec = pl.BlockSpec((tk, tn), lambda i, j, kk: (kk, j))
    elif mode == "NT":
        a_spec = pl.BlockSpec((tm, tk), lambda i, j, kk: (i, kk))
        b_spec = pl.BlockSpec((tn, tk), lambda i, j, kk: (j, kk))
    else:
        a_spec = pl.BlockSpec((tk, tm), lambda i, j, kk: (kk, i))
        b_spec = pl.BlockSpec((tk, tn), lambda i, j, kk: (kk, j))
    in_specs = [a_spec, b_spec]
    args = [a, b]
    if add is not None:
        in_specs.append(pl.BlockSpec((tm, tn), lambda i, j, kk: (i, j)))
        args.append(add)
    if slabs:
        out_shape = jax.ShapeDtypeStruct((n // tn, m, tn), out_dtype)
        out_spec = pl.BlockSpec((1, tm, tn), lambda i, j, kk: (j, i, 0))
    else:
        out_shape = jax.ShapeDtypeStruct((m, n), out_dtype)
        out_spec = pl.BlockSpec((tm, tn), lambda i, j, kk: (i, j))
    vmem = (_nbytes((tm, tk), a.dtype) + _nbytes((tk, tn), b.dtype) + _nbytes((tm, tn), out_dtype)
            + (_nbytes((tm, tn), F32) if add is not None else 0)) + _nbytes((tm, tn), F32) // 2
    return _pc(body, name=name, grid=(m // tm, n // tn, nk), in_specs=in_specs, out_specs=out_spec,
               out_shape=out_shape, scratch=[pltpu.VMEM((tm, tn), F32)], vmem=vmem)(*args)


def _rms_fwd(x, g, *, name):
    r, d = x.shape
    tm = _pick(r, (256, 128))

    def body(x_ref, g_ref, o_ref):
        xv = x_ref[...]
        rstd = lax.rsqrt(jnp.mean(xv * xv, axis=1, keepdims=True) + NORM_EPS)
        o_ref[...] = (xv * rstd * g_ref[...]).astype(o_ref.dtype)

    return _pc(body, name=name, grid=(r // tm,),
               in_specs=[pl.BlockSpec((tm, d), lambda i: (i, 0)), pl.BlockSpec((1, d), lambda i: (0, 0))],
               out_specs=pl.BlockSpec((tm, d), lambda i: (i, 0)), out_shape=jax.ShapeDtypeStruct((r, d), BF16),
               vmem=tm * d * 6)(x, g.reshape(1, d))


def _rms_bwd(x, g, dxn, dres, *, name):
    r, d = x.shape
    tm = _pick(r, (256, 128))

    def body(x_ref, g_ref, dxn_ref, dres_ref, dx_ref, dg_ref):
        @pl.when(pl.program_id(0) == 0)
        def _():
            dg_ref[...] = jnp.zeros_like(dg_ref)

        xv = x_ref[...]
        rstd = lax.rsqrt(jnp.mean(xv * xv, axis=1, keepdims=True) + NORM_EPS)
        xh = xv * rstd
        dy = dxn_ref[...]
        dg_ref[...] += jnp.sum(dy * xh, axis=0, keepdims=True)
        dyg = dy * g_ref[...]
        dx_ref[...] = dres_ref[...] + rstd * (dyg - xh * jnp.mean(dyg * xh, axis=1, keepdims=True))

    row = pl.BlockSpec((tm, d), lambda i: (i, 0))
    vec = pl.BlockSpec((1, d), lambda i: (0, 0))
    return _pc(body, name=name, grid=(r // tm,), in_specs=[row, vec, row, row], out_specs=[row, vec],
               out_shape=[jax.ShapeDtypeStruct((r, d), F32), jax.ShapeDtypeStruct((1, d), F32)],
               vmem=tm * d * 16)(x, g.reshape(1, d), dxn, dres)


def _final_loss(h, g, target, bsz, nc):
    d = h.shape[1]

    def body(h_ref, g_ref, t_ref, loss_ref, dh_ref, dg_ref):
        b, c = pl.program_id(0), pl.program_id(1)

        @pl.when((b == 0) & (c == 0))
        def _():
            loss_ref[...] = jnp.zeros_like(loss_ref)
            dg_ref[...] = jnp.zeros_like(dg_ref)

        @pl.when(c == 0)
        def _():
            dh_ref[...] = jnp.zeros_like(dh_ref)

        @pl.when(c > 0)
        def _():
            xv = h_ref[...]
            rstd = lax.rsqrt(jnp.mean(xv * xv, axis=1, keepdims=True) + NORM_EPS)
            xh = xv * rstd
            gv = g_ref[...]
            err = xh * gv - t_ref[0]
            loss_ref[...] += 0.5 * jnp.sum(jnp.mean(err * err, axis=1, keepdims=True))
            dy = err * (1.0 / d)
            dg_ref[...] += jnp.sum(dy * xh, axis=0, keepdims=True)
            dyg = dy * gv
            dh_ref[...] = rstd * (dyg - xh * jnp.mean(dyg * xh, axis=1, keepdims=True))

    row = pl.BlockSpec((CHUNK, d), lambda b, c: (b * nc + c, 0))
    vec = pl.BlockSpec((1, d), lambda b, c: (0, 0))
    return _pc(body, name="final_loss", grid=(bsz, nc),
               in_specs=[row, vec, pl.BlockSpec((1, CHUNK, d), lambda b, c: (b, jnp.maximum(c - 1, 0), 0))],
               out_specs=[pl.BlockSpec((8, 128), lambda b, c: (0, 0)), row, vec],
               out_shape=[jax.ShapeDtypeStruct((8, 128), F32), jax.ShapeDtypeStruct(h.shape, F32),
                          jax.ShapeDtypeStruct((1, d), F32)],
               vmem=CHUNK * d * 16)(h, g.reshape(1, d), target)


def _adamw(w, m, v, gparts, *, name):
    r, c = w.shape
    tr = _pick(r, (256, 128)) if r * c * 4 > (1 << 20) else r

    def body(w_ref, m_ref, v_ref, gp_ref, g_ref, d_ref, nm_ref, nv_ref):
        g = gp_ref[0].astype(F32)
        for j in range(1, N_DEV):
            g = g + gp_ref[j].astype(F32)
        mm = ADAM_B1 * m_ref[...] + (1.0 - ADAM_B1) * g
        vv = ADAM_B2 * v_ref[...] + (1.0 - ADAM_B2) * (g * g)
        m_hat = mm / (1.0 - ADAM_B1 ** ADAM_STEP)
        v_hat = vv / (1.0 - ADAM_B2 ** ADAM_STEP)
        g_ref[...] = g
        d_ref[...] = -ADAM_LR * (m_hat / (jnp.sqrt(v_hat) + ADAM_EPS) + ADAM_WD * w_ref[...])
        nm_ref[...] = mm
        nv_ref[...] = vv

    blk = pl.BlockSpec((tr, c), lambda i: (i, 0))
    out = jax.ShapeDtypeStruct((r, c), F32)
    return _pc(body, name=name, grid=(r // tr,),
               in_specs=[blk, blk, blk, pl.BlockSpec((N_DEV, tr, c), lambda i: (0, i, 0))],
               out_specs=[blk, blk, blk, blk], out_shape=[out, out, out, out],
               vmem=tr * c * (4 * 7 + N_DEV * jnp.dtype(gparts.dtype).itemsize))(w, m, v, gparts)


def _exchange(arrays, kinds, *, name):
    n = len(arrays)

    def body(*refs):
        ins, outs = refs[:n], refs[n:2 * n]
        send_sems, recv_sems, loc_sems = refs[2 * n:]
        x, y, c = lax.axis_index("x"), lax.axis_index("y"), lax.axis_index("c")
        me = 4 * x + 2 * y + c
        local, sends, recvs = [], [], []
        for i in range(n):
            a2a = kinds[i] == "a2a"
            loc = pltpu.make_async_copy(ins[i].at[me] if a2a else ins[i], outs[i].at[me], loc_sems.at[i])
            loc.start()
            local.append(loc)
            for p in (1, 2, 4, 6, 3, 5, 7):
                tx, ty, tc = x ^ ((p >> 2) & 1), y ^ ((p >> 1) & 1), c ^ (p & 1)
                tgt = 4 * tx + 2 * ty + tc
                cp = pltpu.make_async_remote_copy(
                    src_ref=ins[i].at[tgt] if a2a else ins[i], dst_ref=outs[i].at[me],
                    send_sem=send_sems.at[i, p - 1], recv_sem=recv_sems.at[i, p - 1],
                    device_id=(tx, ty, tc), device_id_type=MESH)
                cp.start()
                sends.append(cp)
                recvs.append(pltpu.make_async_remote_copy(
                    src_ref=ins[i].at[tgt] if a2a else ins[i], dst_ref=outs[i].at[tgt],
                    send_sem=send_sems.at[i, p - 1], recv_sem=recv_sems.at[i, p - 1],
                    device_id=(tx, ty, tc), device_id_type=MESH))
        for r in recvs:
            r.wait_recv()
        for s in sends:
            s.wait_send()
        for loc in local:
            loc.wait()

    outs = [jax.ShapeDtypeStruct((N_DEV,) + (a.shape[1:] if k == "a2a" else a.shape), a.dtype)
            for a, k in zip(arrays, kinds)]
    hbm = pl.BlockSpec(memory_space=pltpu.HBM)
    return pl.pallas_call(
        body, name=name, in_specs=[hbm] * n, out_specs=[hbm] * n, out_shape=outs,
        scratch_shapes=[pltpu.SemaphoreType.DMA((n, N_DEV - 1)), pltpu.SemaphoreType.DMA((n, N_DEV - 1)),
                        pltpu.SemaphoreType.DMA((n,))],
    )(*arrays)


def _s5_params(lam_re, lam_im, log_dt, b_re, b_im):
    dt = jnp.exp(log_dt)[:, None]
    mag = jnp.exp(lam_re * dt)
    ar, ai = mag * jnp.cos(lam_im * dt), mag * jnp.sin(lam_im * dt)
    den = lam_re * lam_re + lam_im * lam_im
    qr = ((ar - 1.0) * lam_re + ai * lam_im) / den
    qi = (ai * lam_re - (ar - 1.0) * lam_im) / den
    bbr = qr[..., None] * b_re - qi[..., None] * b_im
    bbi = qr[..., None] * b_im + qi[..., None] * b_re
    return ar, ai, bbr, bbi


def _s5_power_table(ar, ai):
    pr, pi = ar.reshape(1, -1), ai.reshape(1, -1)
    while pr.shape[0] < CHUNK:
        sr, si = pr[-1:], pi[-1:]
        pr, pi = (jnp.concatenate([pr, pr * sr - pi * si], axis=0), jnp.concatenate([pi, pr * si + pi * sr], axis=0))
    return pr, pi


def _blockdiag(w, rows, cols):
    w = w.reshape(S5_GB, S5_GB, rows, cols)
    eye = jnp.eye(S5_GB, dtype=w.dtype)
    return jnp.einsum("abrc,bd->abrdc", w, eye).reshape(S5_GB, S5_GB * rows, S5_GB * cols)


def _blockdiag_extract(w, rows, cols):
    w = w.reshape(S5_GB, S5_GB, rows, S5_GB, cols)
    return jnp.einsum("abrbc->abrc", w).reshape(S5_GROUPS, rows, cols)


def _s5_scan_specs(bsz, nc, rev):
    def chunk(b, c):
        return b * nc + ((nc - 1 - c) if rev else c)

    return dict(
        u=pl.BlockSpec((CHUNK, CHUNK), lambda g, b, c: (chunk(b, c), g)),
        x=pl.BlockSpec((CHUNK, S5_LANES), lambda g, b, c: (chunk(b, c), g)),
        wb=pl.BlockSpec((1, CHUNK, S5_LANES), lambda g, b, c: (g, 0, 0)),
        wc=pl.BlockSpec((1, S5_LANES, CHUNK), lambda g, b, c: (g, 0, 0)),
        tab=pl.BlockSpec((CHUNK, S5_LANES), lambda g, b, c: (0, g)),
        step=pl.BlockSpec((8, S5_LANES), lambda g, b, c: (0, g)),
        d=pl.BlockSpec((1, CHUNK), lambda g, b, c: (0, g)),
        lane=pl.BlockSpec((1, S5_LANES), lambda g, b, c: (0, g)),
        xprev=pl.BlockSpec((8, S5_LANES), lambda g, b, c: (jnp.maximum(chunk(b, c) * (CHUNK // 8) - 1, 0), g)),
    )


def _s5_fwd(u, wbr, wbi, pr, pi, sr, si, wcr, wci, d, bsz, nc):
    r = u.shape[0]
    sp = _s5_scan_specs(bsz, nc, False)

    def body(u_ref, wbr_ref, wbi_ref, pr_ref, pi_ref, sr_ref, si_ref, wcr_ref, wci_ref, d_ref,
             xr_ref, xi_ref, y1_ref, g_ref, cr_s, ci_s):
        @pl.when(pl.program_id(2) == 0)
        def _():
            cr_s[...] = jnp.zeros_like(cr_s)
            ci_s[...] = jnp.zeros_like(ci_s)

        uv = u_ref[...]
        ub = _bf(uv)
        xr, xi = _dot(ub, wbr_ref[0]), _dot(ub, wbi_ref[0])
        row = lax.broadcasted_iota(jnp.int32, (CHUNK, S5_LANES), 0)
        for k in range(7):
            s = 1 << k
            ar, ai = sr_ref[k:k + 1, :], si_ref[k:k + 1, :]
            hr = jnp.where(row >= s, pltpu.roll(xr, s, 0), 0.0)
            hi = jnp.where(row >= s, pltpu.roll(xi, s, 0), 0.0)
            xr, xi = xr + (ar * hr - ai * hi), xi + (ar * hi + ai * hr)
        cr, ci = cr_s[...], ci_s[...]
        tr, ti = pr_ref[...], pi_ref[...]
        xr, xi = xr + (tr * cr - ti * ci), xi + (tr * ci + ti * cr)
        cr_s[...] = xr[CHUNK - 1:CHUNK, :]
        ci_s[...] = xi[CHUNK - 1:CHUNK, :]
        xr_ref[...] = xr
        xi_ref[...] = xi
        y = _dot(_bf(xr), wcr_ref[0]) - _dot(_bf(xi), wci_ref[0]) + d_ref[...] * uv
        y1_ref[...] = y
        g_ref[...] = _bf(_gelu_and_grad(y)[0])

    return _pc(body, name="s5_fwd", grid=(S5_GB, bsz, nc),
               in_specs=[sp["u"], sp["wb"], sp["wb"], sp["tab"], sp["tab"], sp["step"], sp["step"], sp["wc"], sp["wc"],
                         sp["d"]],
               out_specs=[sp["x"], sp["x"], sp["u"], sp["u"]],
               out_shape=[jax.ShapeDtypeStruct((r, S5_GROUPS * S5_STATE), F32)] * 2
               + [jax.ShapeDtypeStruct((r, S5_WIDTH), F32), jax.ShapeDtypeStruct((r, S5_WIDTH), BF16)],
               scratch=[pltpu.VMEM((1, S5_LANES), F32)] * 2, vmem=4 << 20,
               )(u, wbr, wbi, pr, pi, sr, si, wcr, wci, d)


def _s5_post(y1, glu_pre, glu_b, z):
    r, w = y1.shape
    tm = _pick(r, (256, 128))

    def body(y_ref, p_ref, b_ref, z_ref, o_ref):
        g = _gelu_and_grad(y_ref[...])[0]
        o_ref[...] = _bf(g * jax.nn.sigmoid(p_ref[...] + b_ref[...]) * _silu(z_ref[...]))

    row = pl.BlockSpec((tm, w), lambda i: (i, 0))
    return _pc(body, name="s5_post", grid=(r // tm,), in_specs=[row, row, pl.BlockSpec((1, w), lambda i: (0, 0)), row],
               out_specs=row, out_shape=jax.ShapeDtypeStruct((r, w), BF16), vmem=tm * w * 16)(y1, glu_pre, glu_b, z)


def _s5_post_bwd(dya, y1, glu_pre, glu_b, z):
    r, w = y1.shape
    tm = _pick(r, (256, 128))

    def body(dy_ref, y_ref, p_ref, b_ref, z_ref, dz_ref, dp_ref, dg_ref, db_ref):
        @pl.when(pl.program_id(0) == 0)
        def _():
            db_ref[...] = jnp.zeros_like(db_ref)

        g = _gelu_and_grad(y_ref[...])[0]
        s = jax.nn.sigmoid(p_ref[...] + b_ref[...])
        zv = z_ref[...]
        dy = dy_ref[...]
        do = dy * _silu(zv)
        dz_ref[...] = _bf(dy * g * s * _dsilu(zv))
        dp = do * g * s * (1.0 - s)
        dp_ref[...] = _bf(dp)
        db_ref[...] += jnp.sum(dp, axis=0, keepdims=True)
        dg_ref[...] = do * s

    row = pl.BlockSpec((tm, w), lambda i: (i, 0))
    vec = pl.BlockSpec((1, w), lambda i: (0, 0))
    return _pc(body, name="s5_post_bwd", grid=(r // tm,), in_specs=[row, row, row, vec, row],
               out_specs=[row, row, row, vec],
               out_shape=[jax.ShapeDtypeStruct((r, w), BF16), jax.ShapeDtypeStruct((r, w), BF16),
                          jax.ShapeDtypeStruct((r, w), F32), jax.ShapeDtypeStruct((1, w), F32)],
               vmem=tm * w * 24)(dya, y1, glu_pre, glu_b, z)


def _s5_bwd(dg, y1, u, xr, xi, wbr, wbi, qr, qi, sr, si, wcr, wci, d, bsz, nc):
    r = u.shape[0]
    sp = _s5_scan_specs(bsz, nc, True)

    def body(dg_ref, y1_ref, u_ref, xr_ref, xi_ref, xpr_ref, xpi_ref, wbr_ref, wbi_ref, qr_ref, qi_ref, sr_ref, si_ref,
             wcr_ref, wci_ref, d_ref, du_ref, dd_ref, dwcr_ref, dwci_ref, dwbr_ref, dwbi_ref, dar_ref, dai_ref,
             cr_s, ci_s):
        b, c = pl.program_id(1), pl.program_id(2)

        @pl.when((b == 0) & (c == 0))
        def _():
            for ref in (dd_ref, dwcr_ref, dwci_ref, dwbr_ref, dwbi_ref, dar_ref, dai_ref):
                ref[...] = jnp.zeros_like(ref)

        @pl.when(c == 0)
        def _():
            cr_s[...] = jnp.zeros_like(cr_s)
            ci_s[...] = jnp.zeros_like(ci_s)

        uv = u_ref[...]
        ub = _bf(uv)
        dy = dg_ref[...] * _gelu_and_grad(y1_ref[...])[1]
        dd_ref[...] += jnp.sum(dy * uv, axis=0, keepdims=True)
        dyb = _bf(dy)
        xr, xi = xr_ref[...], xi_ref[...]
        dwcr_ref[0] += _dot(_bf(xr), dyb, TN)
        dwci_ref[0] -= _dot(_bf(xi), dyb, TN)
        lr, li = _dot(dyb, wcr_ref[0], NT), -_dot(dyb, wci_ref[0], NT)
        row = lax.broadcasted_iota(jnp.int32, (CHUNK, S5_LANES), 0)
        for k in range(7):
            s = 1 << k
            ar, ai = sr_ref[k:k + 1, :], si_ref[k:k + 1, :]
            hr = jnp.where(row < CHUNK - s, pltpu.roll(lr, CHUNK - s, 0), 0.0)
            hi = jnp.where(row < CHUNK - s, pltpu.roll(li, CHUNK - s, 0), 0.0)
            lr, li = lr + (ar * hr + ai * hi), li + (ar * hi - ai * hr)
        cr, ci = cr_s[...], ci_s[...]
        tr, ti = qr_ref[...], qi_ref[...]
        lr, li = lr + (tr * cr + ti * ci), li + (tr * ci - ti * cr)
        cr_s[...] = lr[0:1, :]
        ci_s[...] = li[0:1, :]
        lrb, lib = _bf(lr), _bf(li)
        du_ref[...] = _bf(_dot(lrb, wbr_ref[0], NT) + _dot(lib, wbi_ref[0], NT) + dy * d_ref[...])
        dwbr_ref[0] += _dot(ub, lrb, TN)
        dwbi_ref[0] += _dot(ub, lib, TN)
        first = c == nc - 1
        pr0 = jnp.where(first, 0.0, xpr_ref[7:8, :])
        pi0 = jnp.where(first, 0.0, xpi_ref[7:8, :])
        xpr = jnp.where(row == 0, pr0, pltpu.roll(xr, 1, 0))
        xpi = jnp.where(row == 0, pi0, pltpu.roll(xi, 1, 0))
        dar_ref[...] += jnp.sum(lr * xpr + li * xpi, axis=0, keepdims=True)
        dai_ref[...] += jnp.sum(li * xpr - lr * xpi, axis=0, keepdims=True)

    st = jax.ShapeDtypeStruct
    return _pc(body, name="s5_bwd", grid=(S5_GB, bsz, nc),
               in_specs=[sp["u"], sp["u"], sp["u"], sp["x"], sp["x"], sp["xprev"], sp["xprev"], sp["wb"], sp["wb"],
                         sp["tab"], sp["tab"], sp["step"], sp["step"], sp["wc"], sp["wc"], sp["d"]],
               out_specs=[sp["u"], sp["d"], sp["wc"], sp["wc"], sp["wb"], sp["wb"], sp["lane"], sp["lane"]],
               out_shape=[st((r, S5_WIDTH), BF16), st((1, S5_WIDTH), F32),
                          st((S5_GB, S5_LANES, CHUNK), F32), st((S5_GB, S5_LANES, CHUNK), F32),
                          st((S5_GB, CHUNK, S5_LANES), F32), st((S5_GB, CHUNK, S5_LANES), F32),
                          st((1, S5_GROUPS * S5_STATE), F32), st((1, S5_GROUPS * S5_STATE), F32)],
               scratch=[pltpu.VMEM((1, S5_LANES), F32)] * 2, vmem=6 << 20,
               )(dg, y1, u, xr, xi, xr, xi, wbr, wbi, qr, qi, sr, si, wcr, wci, d)


def _s5_layer_fwd(u, prm, glu_w, bsz, nc):
    xr, xi, y1, g = _s5_fwd(u, prm["wbr"], prm["wbi"], prm["pr"], prm["pi"], prm["sr"], prm["si"], prm["wcr"],
                            prm["wci"], prm["d"], bsz, nc)
    glu_pre = _mm(g, glu_w, "NN", name="s5_glu")
    return dict(xr=xr, xi=xi, y1=y1, g=g, glu_pre=glu_pre)


def _s5_layer_bwd(dya, u, z, sv, prm, pvjp, glu_w, glu_b, bsz, nc):
    dz, dglu, dg_direct, dglu_b = _s5_post_bwd(dya, sv["y1"], sv["glu_pre"], glu_b, z)
    dg = _mm(dglu, glu_w, "NT", name="s5_dg", add=dg_direct)
    dglu_w = _mm(sv["g"], dglu, "TN", name="s5_dglu_w")
    du, dd, dwcr, dwci, dwbr, dwbi, dar, dai = _s5_bwd(
        dg, sv["y1"], u, sv["xr"], sv["xi"], prm["wbr"], prm["wbi"], prm["qr"], prm["qi"], prm["sr"], prm["si"],
        prm["wcr"], prm["wci"], prm["d"], bsz, nc)
    dbbr = jnp.swapaxes(_blockdiag_extract(dwbr, S5_GROUP_SIZE, S5_STATE), 1, 2)
    dbbi = jnp.swapaxes(_blockdiag_extract(dwbi, S5_GROUP_SIZE, S5_STATE), 1, 2)
    dlr, dli, dldt, dbr, dbi = pvjp((dar.reshape(S5_GROUPS, S5_STATE), dai.reshape(S5_GROUPS, S5_STATE), dbbr, dbbi))
    grads = dict(
        s5_lambda_re=dlr, s5_lambda_im=dli, s5_log_dt=dldt, s5_b_re=dbr, s5_b_im=dbi,
        s5_c_re=jnp.swapaxes(_blockdiag_extract(dwcr, S5_STATE, S5_GROUP_SIZE), 1, 2),
        s5_c_im=jnp.swapaxes(_blockdiag_extract(dwci, S5_STATE, S5_GROUP_SIZE), 1, 2),
        s5_d=dd, s5_glu_w=dglu_w, s5_glu_b=dglu_b)
    return du, dz, grads


def _s5_tables(lam_re, lam_im, log_dt, b_re, b_im, c_re, c_im, d):
    (ar, ai, bbr, bbi), vjp = jax.vjp(_s5_params, lam_re, lam_im, log_dt, b_re, b_im)
    pr, pi = _s5_power_table(lax.stop_gradient(ar), lax.stop_gradient(ai))
    steps = [(1 << k) - 1 for k in range(8)]
    prm = dict(
        wbr=_bf(_blockdiag(jnp.swapaxes(bbr, 1, 2), S5_GROUP_SIZE, S5_STATE)),
        wbi=_bf(_blockdiag(jnp.swapaxes(bbi, 1, 2), S5_GROUP_SIZE, S5_STATE)),
        wcr=_bf(_blockdiag(jnp.swapaxes(c_re, 1, 2), S5_STATE, S5_GROUP_SIZE)),
        wci=_bf(_blockdiag(jnp.swapaxes(c_im, 1, 2), S5_STATE, S5_GROUP_SIZE)),
        pr=pr, pi=pi, qr=pr[::-1], qi=pi[::-1],
        sr=jnp.concatenate([pr[i:i + 1] for i in steps], axis=0),
        si=jnp.concatenate([pi[i:i + 1] for i in steps], axis=0), d=d.reshape(1, S5_WIDTH))
    return prm, vjp


def _tile16(p8):
    return jnp.concatenate([p8] * (CHUNK // 8), axis=0)


def _shift_down(x, halo, s, row):
    return jnp.where(row >= s, pltpu.roll(x, s, 0), pltpu.roll(halo, s, 0))


def _shift_up(x, halo, s, row):
    return jnp.where(row < CHUNK - s, pltpu.roll(x, CHUNK - s, 0), pltpu.roll(halo, CHUNK - s, 0))


def _conv_specs(nc, tw):
    def chunk(b, c):
        return b * nc + c

    return dict(
        x=pl.BlockSpec((CHUNK, tw), lambda j, b, c: (chunk(b, c), j)),
        prev=pl.BlockSpec((8, tw), lambda j, b, c: (jnp.maximum(chunk(b, c) * (CHUNK // 8) - 1, 0), j)),
        nxt=pl.BlockSpec((8, tw), lambda j, b, c: ((b * nc + jnp.minimum(c + 1, nc - 1)) * (CHUNK // 8), j)),
        w=pl.BlockSpec((ML_CONV, tw), lambda j, b, c: (0, j)),
        vec=pl.BlockSpec((1, tw), lambda j, b, c: (0, j)),
    )


def _conv_fwd(x, w, bias, bsz, nc, *, name):
    r, wd = x.shape
    tw = _pick(wd, (512, 384, 256, 128))
    sp = _conv_specs(nc, tw)

    def body(x_ref, p_ref, w_ref, b_ref, o_ref):
        c = pl.program_id(2)
        xv = x_ref[...]
        row = lax.broadcasted_iota(jnp.int32, xv.shape, 0)
        halo = jnp.where(c == 0, 0.0, _tile16(p_ref[...]))
        acc = b_ref[...] + w_ref[3:4, :] * xv
        for s in (1, 2, 3):
            acc = acc + w_ref[3 - s:4 - s, :] * _shift_down(xv, halo, s, row)
        o_ref[...] = acc

    return _pc(body, name=name, grid=(wd // tw, bsz, nc), in_specs=[sp["x"], sp["prev"], sp["w"], sp["vec"]],
               out_specs=sp["x"], out_shape=jax.ShapeDtypeStruct((r, wd), F32), vmem=CHUNK * tw * 16,
               )(x, x, w, bias.reshape(1, wd))


def _conv_bwd(dpre, x, w, bsz, nc, *, name, add=None):
    r, wd = x.shape
    tw = _pick(wd, (512, 384, 256, 128))
    sp = _conv_specs(nc, tw)

    def body(*refs):
        d_ref, n_ref, x_ref, p_ref, w_ref = refs[:5]
        add_ref = refs[5] if add is not None else None
        dx_ref, dw_ref, db_ref = refs[-3:]
        b, c = pl.program_id(1), pl.program_id(2)

        @pl.when((b == 0) & (c == 0))
        def _():
            dw_ref[...] = jnp.zeros_like(dw_ref)
            db_ref[...] = jnp.zeros_like(db_ref)

        dv, xv = d_ref[...], x_ref[...]
        row = lax.broadcasted_iota(jnp.int32, xv.shape, 0)
        dhalo = jnp.where(c == nc - 1, 0.0, _tile16(n_ref[...]))
        xhalo = jnp.where(c == 0, 0.0, _tile16(p_ref[...]))
        dx = w_ref[3:4, :] * dv
        for s in (1, 2, 3):
            dx = dx + w_ref[3 - s:4 - s, :] * _shift_up(dv, dhalo, s, row)
        if add_ref is not None:
            dx = dx + add_ref[...]
        dx_ref[...] = _bf(dx)
        db_ref[...] += jnp.sum(dv, axis=0, keepdims=True)
        dw_ref[3:4, :] += jnp.sum(dv * xv, axis=0, keepdims=True)
        for s in (1, 2, 3):
            dw_ref[3 - s:4 - s, :] += jnp.sum(dv * _shift_down(xv, xhalo, s, row), axis=0, keepdims=True)

    ins = [dpre, dpre, x, x, w] + ([add] if add is not None else [])
    specs = [sp["x"], sp["nxt"], sp["x"], sp["prev"], sp["w"]] + ([sp["x"]] if add is not None else [])
    return _pc(body, name=name, grid=(wd // tw, bsz, nc), in_specs=specs, out_specs=[sp["x"], sp["w"], sp["vec"]],
               out_shape=[jax.ShapeDtypeStruct((r, wd), BF16), jax.ShapeDtypeStruct((ML_CONV, wd), F32),
                          jax.ShapeDtypeStruct((1, wd), F32)], vmem=CHUNK * tw * 24)(*ins)


ML_SCALE = ML_DH ** -0.5


def _headwise_expand(w):
    nb = ML_DH // QKV_BLOCK
    w = w.reshape(ML_HEADS, nb, QKV_BLOCK, QKV_BLOCK)
    eye = jnp.eye(nb, dtype=w.dtype)
    return jnp.einsum("hnio,nm->hnimo", w, eye).reshape(ML_HEADS, ML_DH, ML_DH)


def _headwise_extract(w):
    nb = ML_DH // QKV_BLOCK
    w = w.reshape(ML_HEADS, nb, QKV_BLOCK, nb, QKV_BLOCK)
    return jnp.einsum("hnimo,nm->hnio", w, jnp.eye(nb, dtype=w.dtype)).reshape(ML_HEADS * nb, QKV_BLOCK, QKV_BLOCK)


def _ml_pre(pre, x, wq, wk, wv, wgq, wgk, wgv, bsz, nc):
    r = x.shape[0]
    hrow = pl.BlockSpec((CHUNK, ML_DH), lambda b, c, h: (b * nc + c, h))
    wexp = pl.BlockSpec((1, ML_DH, ML_DH), lambda b, c, h: (h, 0, 0))
    wg = pl.BlockSpec((ML_DH, CHUNK), lambda b, c, h: (h, 0))
    gspec = pl.BlockSpec((CHUNK, CHUNK), lambda b, c, h: (b * nc + c, 0))

    def body(pre_ref, x_ref, wq_ref, wk_ref, wv_ref, gq_ref, gk_ref, gv_ref, q_ref, qs_ref, k_ref, v_ref, gt_ref):
        @pl.when(pl.program_id(2) == 0)
        def _():
            gt_ref[...] = jnp.zeros_like(gt_ref)

        xcb = _bf(_silu(pre_ref[...]))
        q = _dot(xcb, wq_ref[0])
        k = _dot(xcb, wk_ref[0])
        v = _dot(_bf(x_ref[...]), wv_ref[0])
        qb, kb, vb = _bf(q), _bf(k), _bf(v)
        q_ref[...] = qb
        qs_ref[...] = _bf(q * ML_SCALE)
        k_ref[...] = kb
        v_ref[...] = vb
        gt_ref[...] += _dot(qb, gq_ref[...]) + _dot(kb, gk_ref[...]) + _dot(vb, gv_ref[...])

    o = jax.ShapeDtypeStruct((r, ML_WIDTH), BF16)
    return _pc(body, name="ml_pre", grid=(bsz, nc, ML_HEADS),
               in_specs=[hrow, hrow, wexp, wexp, wexp, wg, wg, wg], out_specs=[hrow, hrow, hrow, hrow, gspec],
               out_shape=[o, o, o, o, jax.ShapeDtypeStruct((r, CHUNK), F32)], vmem=4 << 20,
               )(pre, x, wq, wk, wv, wgq, wgk, wgv)


def _cumsum_rows(x, row, rev=False):
    for k in range(7):
        s = 1 << k
        if rev:
            x = x + jnp.where(row < CHUNK - s, pltpu.roll(x, CHUNK - s, 0), 0.0)
        else:
            x = x + jnp.where(row >= s, pltpu.roll(x, s, 0), 0.0)
    return x


def _log_sigmoid(x):
    return jnp.minimum(x, 0.0) - jnp.log(1.0 + jnp.exp(-jnp.abs(x)))


def _ml_core(gates, hd, first, m, qs, k, v, cmat, nvec):
    sq = (CHUNK, CHUNK)
    lane = lax.broadcasted_iota(jnp.int32, sq, 1)
    row = lax.broadcasted_iota(jnp.int32, sq, 0)
    igc = jnp.sum(jnp.where(lane == hd, gates, 0.0), axis=1, keepdims=True)
    fpc = jnp.sum(jnp.where(lane == hd + ML_HEADS, gates, 0.0), axis=1, keepdims=True)
    valid = jnp.logical_or(jnp.logical_not(first), row[:, :1] >= PAD_ROWS)
    igc = jnp.where(valid, igc, NEG)
    lfc = jnp.where(valid, _log_sigmoid(fpc), 0.0)
    bcb = _cumsum_rows(jnp.broadcast_to(lfc, sq), row)
    igb = jnp.broadcast_to(igc, sq)
    dm = jnp.where(lane <= row, bcb - (bcb - igb).T, NEG)
    bc = bcb[:, :1]
    inter = bc + m
    mt = jnp.maximum(inter, jnp.max(dm, axis=1, keepdims=True))
    wt = jnp.exp(dm - mt)
    wprev = jnp.exp(inter - mt)
    s0 = _dot(qs, k, NT)
    s = s0 * wt
    cb = _bf(cmat)
    qc = _dot(qs, cb)
    qf = qs.astype(F32)
    qn = jnp.sum(qf * nvec, axis=1, keepdims=True)
    num = _dot(_bf(s), v) + wprev * qc
    den = jnp.sum(s, axis=1, keepdims=True) + wprev * qn
    emt = jnp.exp(-mt)
    dd = jnp.maximum(jnp.abs(den), emt)
    blast = bcb[CHUNK - 1:CHUNK, :1]
    g = blast - bc + igc
    m_new = jnp.maximum(blast + m, jnp.max(g, axis=0, keepdims=True))
    decay = jnp.exp(blast + m - m_new)
    e = jnp.exp(g - m_new)
    kf = k.astype(F32)
    wk = e * kf
    return dict(lane=lane, row=row, fpc=fpc, valid=valid, wt=wt, wprev=wprev, s=s, cb=cb, qc=qc, qf=qf, qn=qn,
                num=num, den=den, emt=emt, dd=dd, m_new=m_new, decay=decay, e=e, kf=kf, wk=wk)


def _ml_headnorm(h):
    mu = jnp.mean(h, axis=1, keepdims=True)
    hc = h - mu
    rstd = lax.rsqrt(jnp.mean(hc * hc, axis=1, keepdims=True) + HEAD_NORM_EPS)
    return hc * rstd, rstd


def _ml_chunk_specs(nc, rev, head_major):
    def ix(a, b_, c):
        hd, b = (a, b_) if head_major else (b_, a)
        return hd, b, (nc - 1 - c) if rev else c

    def row(a, b_, c):
        hd, b, cc = ix(a, b_, c)
        return b * nc + cc, hd

    def st(a, b_, c):
        hd, b, cc = ix(a, b_, c)
        return (b * ML_HEADS + hd) * nc + cc

    return dict(
        hrow=pl.BlockSpec((CHUNK, ML_DH), row),
        gates=pl.BlockSpec((CHUNK, CHUNK), lambda a, b_, c: (row(a, b_, c)[0], 0)),
        bias=pl.BlockSpec((1, CHUNK), lambda a, b_, c: (0, 0)),
        hvec=pl.BlockSpec((1, ML_DH), lambda a, b_, c: (0, ix(a, b_, c)[0])),
        cs=pl.BlockSpec((1, ML_DH, ML_DH), lambda a, b_, c: (st(a, b_, c), 0, 0)),
        ns=pl.BlockSpec((1, 1, ML_DH), lambda a, b_, c: (st(a, b_, c), 0, 0)),
        ms=pl.BlockSpec((1, 1, CHUNK), lambda a, b_, c: (st(a, b_, c), 0, 0)),
        dgates=pl.BlockSpec((1, CHUNK, CHUNK), lambda a, b_, c: (ix(a, b_, c)[0], row(a, b_, c)[0], 0)),
    )


def _ml_chunk_fwd(qs, k, v, gates, b_gate, pre, z, nw, sk, bsz, nc):
    r = qs.shape[0]
    sp = _ml_chunk_specs(nc, False, False)

    def body(qs_ref, k_ref, v_ref, gt_ref, bg_ref, pre_ref, z_ref, nw_ref, sk_ref,
             h_ref, yb_ref, cs_ref, ns_ref, ms_ref, c_s, n_s, m_s):
        hd, c = pl.program_id(1), pl.program_id(2)

        @pl.when(c == 0)
        def _():
            c_s[...] = jnp.zeros_like(c_s)
            n_s[...] = jnp.zeros_like(n_s)
            m_s[...] = jnp.zeros_like(m_s)

        cmat, nvec, m = c_s[...], n_s[...], m_s[...]
        cs_ref[0] = cmat
        ns_ref[0] = nvec
        ms_ref[0] = jnp.broadcast_to(m, (1, CHUNK))
        v_ = v_ref[...]
        co = _ml_core(gt_ref[...] + bg_ref[...], hd, c == 0, m, qs_ref[...], k_ref[...], v_, cmat, nvec)
        h = co["num"] / co["dd"]
        h_ref[...] = h
        hn, _ = _ml_headnorm(h)
        yb_ref[...] = _bf((hn * nw_ref[...] + sk_ref[...] * _silu(pre_ref[...])) * _silu(z_ref[...]))
        c_s[...] = co["decay"] * cmat + _dot(_bf(co["wk"]), v_, TN)
        n_s[...] = co["decay"] * nvec + jnp.sum(co["wk"], axis=0, keepdims=True)
        m_s[...] = co["m_new"]

    nst = bsz * ML_HEADS * nc
    return _pc(body, name="ml_chunk_fwd", grid=(bsz, ML_HEADS, nc),
               in_specs=[sp["hrow"]] * 3 + [sp["gates"], sp["bias"], sp["hrow"], sp["hrow"], sp["hvec"], sp["hvec"]],
               out_specs=[sp["hrow"], sp["hrow"], sp["cs"], sp["ns"], sp["ms"]],
               out_shape=[jax.ShapeDtypeStruct((r, ML_WIDTH), F32), jax.ShapeDtypeStruct((r, ML_WIDTH), BF16),
                          jax.ShapeDtypeStruct((nst, ML_DH, ML_DH), F32), jax.ShapeDtypeStruct((nst, 1, ML_DH), F32),
                          jax.ShapeDtypeStruct((nst, 1, CHUNK), F32)],
               scratch=[pltpu.VMEM((ML_DH, ML_DH), F32), pltpu.VMEM((1, ML_DH), F32), pltpu.VMEM((1, 1), F32)],
               vmem=6 << 20)(qs, k, v, gates, b_gate, pre, z, nw, sk)


def _ml_chunk_bwd(dyb, qs, k, v, gates, b_gate, pre, z, nw, sk, h, cs, ns, ms, bsz, nc):
    r = qs.shape[0]
    sp = _ml_chunk_specs(nc, True, True)

    def body(dy_ref, qs_ref, k_ref, v_ref, gt_ref, bg_ref, pre_ref, z_ref, nw_ref, sk_ref, h_ref, cs_ref, ns_ref,
             ms_ref, dq_ref, dk_ref, dv_ref, dz_ref, dxc_ref, dgt_ref, dnw_ref, dsk_ref, dc_s, dn_s):
        hd, b, c = pl.program_id(0), pl.program_id(1), pl.program_id(2)

        @pl.when((b == 0) & (c == 0))
        def _():
            dnw_ref[...] = jnp.zeros_like(dnw_ref)
            dsk_ref[...] = jnp.zeros_like(dsk_ref)

        @pl.when(c == 0)
        def _():
            dc_s[...] = jnp.zeros_like(dc_s)
            dn_s[...] = jnp.zeros_like(dn_s)

        qs, k, v = qs_ref[...], k_ref[...], v_ref[...]
        cmat, nvec, m = cs_ref[0], ns_ref[0], ms_ref[0][:, :1]
        co = _ml_core(gt_ref[...] + bg_ref[...], hd, c == nc - 1, m, qs, k, v, cmat, nvec)
        lane, row = co["lane"], co["row"]
        wt, wprev, s, cb, qf = co["wt"], co["wprev"], co["s"], co["cb"], co["qf"]
        h = h_ref[...]
        hn, rstd = _ml_headnorm(h)
        xc = _silu(pre_ref[...])
        zv = z_ref[...]
        nw, sk = nw_ref[...], sk_ref[...]
        dy = dy_ref[...]
        dz_ref[...] = _bf(dy * (hn * nw + sk * xc) * _dsilu(zv))
        do = dy * _silu(zv)
        dsk_ref[...] += jnp.sum(do * xc, axis=0, keepdims=True)
        dnw_ref[...] += jnp.sum(do * hn, axis=0, keepdims=True)
        dxc_ref[...] = do * sk
        dhn = do * nw
        dh = rstd * (dhn - jnp.mean(dhn, axis=1, keepdims=True) - hn * jnp.mean(dhn * hn, axis=1, keepdims=True))
        rinv = 1.0 / co["dd"]
        dnum = dh * rinv
        ddd = -jnp.sum(dh * h, axis=1, keepdims=True) * rinv
        den = co["den"]
        dden = jnp.where(jnp.abs(den) >= co["emt"], ddd * jnp.sign(den), 0.0)
        dnb = _bf(dnum)
        ds = _dot(dnb, v, NT) + dden
        dv = _dot(_bf(s), dnb, TN)
        dnw_ = _bf(dnum * wprev)
        dwn = dden * wprev
        dqs = _dot(dnw_, cb, NT) + dwn * nvec
        dc_out = _dot(qs, dnw_, TN)
        dn_out = jnp.sum(dwn * qf, axis=0, keepdims=True)
        dwprev = jnp.sum(dnum * co["qc"], axis=1, keepdims=True) + dden * co["qn"]
        ds0 = _bf(ds * wt)
        ddm = ds * s
        dqs = dqs + _dot(ds0, k)
        dk = _dot(ds0, qs, TN)
        colc = jnp.sum(ddm.T, axis=1, keepdims=True)
        dbc = dwprev * wprev + jnp.sum(ddm, axis=1, keepdims=True) - colc
        dig = colc
        dcn, dnn = dc_s[...], dn_s[...]
        dcb = _bf(dcn)
        decay, e, kf, wk = co["decay"], co["e"], co["kf"], co["wk"]
        ddecay = (jnp.sum(jnp.sum(dcn * cmat, axis=1, keepdims=True), axis=0, keepdims=True)
                  + jnp.sum(dnn * nvec, axis=1, keepdims=True))
        dwk = _dot(v, dcb, NT) + dnn
        dv = dv + _dot(_bf(wk), dcb)
        dk = dk + e * dwk
        dg = jnp.sum(dwk * kf, axis=1, keepdims=True) * e
        dblast = ddecay * decay + jnp.sum(dg, axis=0, keepdims=True)
        dbc = dbc - dg + jnp.where(row[:, :1] == CHUNK - 1, dblast, 0.0)
        dig = dig + dg
        dc_s[...] = decay * dcn + dc_out
        dn_s[...] = decay * dnn + dn_out
        dlf = _cumsum_rows(jnp.broadcast_to(dbc, (CHUNK, CHUNK)), row, rev=True)[:, :1]
        dfp = dlf * (1.0 - jax.nn.sigmoid(co["fpc"]))
        dig = jnp.where(co["valid"], dig, 0.0)
        dfp = jnp.where(co["valid"], dfp, 0.0)
        dgt_ref[0] = jnp.where(lane == hd, dig, 0.0) + jnp.where(lane == hd + ML_HEADS, dfp, 0.0)
        dq_ref[...] = _bf(dqs * ML_SCALE)
        dk_ref[...] = _bf(dk)
        dv_ref[...] = _bf(dv)

    ob = jax.ShapeDtypeStruct((r, ML_WIDTH), BF16)
    return _pc(body, name="ml_chunk_bwd", grid=(ML_HEADS, bsz, nc),
               in_specs=[sp["hrow"]] * 4 + [sp["gates"], sp["bias"], sp["hrow"], sp["hrow"], sp["hvec"], sp["hvec"],
                                            sp["hrow"], sp["cs"], sp["ns"], sp["ms"]],
               out_specs=[sp["hrow"]] * 5 + [sp["dgates"], sp["hvec"], sp["hvec"]],
               out_shape=[ob, ob, ob, ob, jax.ShapeDtypeStruct((r, ML_WIDTH), F32),
                          jax.ShapeDtypeStruct((ML_HEADS, r, CHUNK), F32),
                          jax.ShapeDtypeStruct((1, ML_WIDTH), F32), jax.ShapeDtypeStruct((1, ML_WIDTH), F32)],
               scratch=[pltpu.VMEM((ML_DH, ML_DH), F32), pltpu.VMEM((1, ML_DH), F32)], vmem=8 << 20,
               )(dyb, qs, k, v, gates, b_gate, pre, z, nw, sk, h, cs, ns, ms)


def _ml_pre_bwd(dq, dk, dv, dgates, dxc_skip, pre, x, q, k, v, wq, wk, wv, wgq, wgk, wgv, bsz, nc):
    r = x.shape[0]
    hrow = pl.BlockSpec((CHUNK, ML_DH), lambda h, b, c: (b * nc + c, h))
    wexp = pl.BlockSpec((1, ML_DH, ML_DH), lambda h, b, c: (h, 0, 0))
    wg = pl.BlockSpec((ML_DH, CHUNK), lambda h, b, c: (h, 0))
    dgs = pl.BlockSpec((ML_HEADS, CHUNK, CHUNK), lambda h, b, c: (0, b * nc + c, 0))
    bgs = pl.BlockSpec((1, 1, CHUNK), lambda h, b, c: (h, 0, 0))

    def body(dq_ref, dk_ref, dv_ref, dg_ref, dxs_ref, pre_ref, x_ref, q_ref, k_ref, v_ref, wq_ref, wk_ref, wv_ref,
             gq_ref, gk_ref, gv_ref, dpre_ref, dxv_ref, dwq_ref, dwk_ref, dwv_ref, dgq_ref, dgk_ref, dgv_ref, dbg_ref):
        b, c = pl.program_id(1), pl.program_id(2)

        @pl.when((b == 0) & (c == 0))
        def _():
            for ref in (dwq_ref, dwk_ref, dwv_ref, dgq_ref, dgk_ref, dgv_ref, dbg_ref):
                ref[...] = jnp.zeros_like(ref)

        dgt = dg_ref[0]
        for j in range(1, ML_HEADS):
            dgt = dgt + dg_ref[j]
        dbg_ref[0] += jnp.sum(dgt, axis=0, keepdims=True)
        dgb = _bf(dgt)
        dqt = _bf(dq_ref[...].astype(F32) + _dot(dgb, gq_ref[...], NT))
        dkt = _bf(dk_ref[...].astype(F32) + _dot(dgb, gk_ref[...], NT))
        dvt = _bf(dv_ref[...].astype(F32) + _dot(dgb, gv_ref[...], NT))
        dgq_ref[...] += _dot(q_ref[...], dgb, TN)
        dgk_ref[...] += _dot(k_ref[...], dgb, TN)
        dgv_ref[...] += _dot(v_ref[...], dgb, TN)
        prev = pre_ref[...]
        xcb = _bf(_silu(prev))
        xb = _bf(x_ref[...])
        dwq_ref[0] += _dot(xcb, dqt, TN)
        dwk_ref[0] += _dot(xcb, dkt, TN)
        dwv_ref[0] += _dot(xb, dvt, TN)
        dxc = _dot(dqt, wq_ref[0], NT) + _dot(dkt, wk_ref[0], NT) + dxs_ref[...]
        dpre_ref[...] = dxc * _dsilu(prev)
        dxv_ref[...] = _dot(dvt, wv_ref[0], NT)

    f = jax.ShapeDtypeStruct((r, ML_WIDTH), F32)
    we = jax.ShapeDtypeStruct((ML_HEADS, ML_DH, ML_DH), F32)
    wgs = jax.ShapeDtypeStruct((ML_WIDTH, CHUNK), F32)
    return _pc(body, name="ml_pre_bwd", grid=(ML_HEADS, bsz, nc),
               in_specs=[hrow, hrow, hrow, dgs, hrow, hrow, hrow, hrow, hrow, hrow, wexp, wexp, wexp, wg, wg, wg],
               out_specs=[hrow, hrow, wexp, wexp, wexp, wg, wg, wg, bgs],
               out_shape=[f, f, we, we, we, wgs, wgs, wgs, jax.ShapeDtypeStruct((ML_HEADS, 1, CHUNK), F32)],
               vmem=8 << 20)(dq, dk, dv, dgates, dxc_skip, pre, x, q, k, v, wq, wk, wv, wgq, wgk, wgv)


def _pad_lanes(w):
    return jnp.pad(w, ((0, 0), (0, CHUNK - w.shape[1])))


def _ml_weights(conv_w, conv_b, wq, wk, wv, w_gate, b_gate, norm_w, skip):
    return dict(
        conv_w=conv_w, conv_b=conv_b,
        wq=_bf(_headwise_expand(wq)), wk=_bf(_headwise_expand(wk)), wv=_bf(_headwise_expand(wv)),
        wgq=_bf(_pad_lanes(w_gate[:ML_WIDTH])), wgk=_bf(_pad_lanes(w_gate[ML_WIDTH:2 * ML_WIDTH])),
        wgv=_bf(_pad_lanes(w_gate[2 * ML_WIDTH:])), b_gate=_pad_lanes(b_gate.reshape(1, -1)),
        norm=norm_w.reshape(1, ML_WIDTH), skip=skip.reshape(1, ML_WIDTH))


def _ml_layer_fwd(x, z, w, bsz, nc):
    pre = _conv_fwd(x, w["conv_w"], w["conv_b"], bsz, nc, name="ml_conv")
    q, qs, k, v, gates = _ml_pre(pre, x, w["wq"], w["wk"], w["wv"], w["wgq"], w["wgk"], w["wgv"], bsz, nc)
    h, yb, cs, ns, ms = _ml_chunk_fwd(qs, k, v, gates, w["b_gate"], pre, z, w["norm"], w["skip"], bsz, nc)
    return yb, dict(pre=pre, q=q, qs=qs, k=k, v=v, gates=gates, h=h, cs=cs, ns=ns, ms=ms)


def _ml_layer_bwd(dyb, x, z, sv, w, bsz, nc):
    dq, dk, dv, dz, dxc, dgates, dnw, dsk = _ml_chunk_bwd(
        dyb, sv["qs"], sv["k"], sv["v"], sv["gates"], w["b_gate"], sv["pre"], z, w["norm"], w["skip"], sv["h"],
        sv["cs"], sv["ns"], sv["ms"], bsz, nc)
    dpre, dxv, dwq, dwk, dwv, dgq, dgk, dgv, dbg = _ml_pre_bwd(
        dq, dk, dv, dgates, dxc, sv["pre"], x, sv["q"], sv["k"], sv["v"], w["wq"], w["wk"], w["wv"], w["wgq"],
        w["wgk"], w["wgv"], bsz, nc)
    dx, dcw, dcb = _conv_bwd(dpre, x, w["conv_w"], bsz, nc, name="ml_conv_bwd", add=dxv)
    ng = 2 * ML_HEADS
    grads = dict(
        ml_conv_w=dcw, ml_conv_b=dcb, ml_wq=_headwise_extract(dwq), ml_wk=_headwise_extract(dwk),
        ml_wv=_headwise_extract(dwv), ml_w_gate=jnp.concatenate([dgq[:, :ng], dgk[:, :ng], dgv[:, :ng]], axis=0),
        ml_b_gate=dbg[0][:, :ng], ml_norm=dnw, ml_skip=dsk)
    return dx, dz, grads


HI = lax.Precision.HIGHEST


def _softplus(x):
    return jnp.maximum(x, 0.0) + jnp.log(1.0 + jnp.exp(-jnp.abs(x)))


def _lane_cumsum(x, lane, rev=False):
    for k in range(7):
        s = 1 << k
        if rev:
            x = x + jnp.where(lane < CHUNK - s, pltpu.roll(x, CHUNK - s, 1), 0.0)
        else:
            x = x + jnp.where(lane >= s, pltpu.roll(x, s, 1), 0.0)
    return x


def _head_sum_matrix():
    r = lax.broadcasted_iota(jnp.int32, (SSD_HPG, SSD_GW), 0)
    l = lax.broadcasted_iota(jnp.int32, (SSD_HPG, SSD_GW), 1)
    return jnp.where(l // SSD_P == r, 1.0, 0.0).astype(F32)


def _ssd_core(xs, bm, cm, dt_raw, dt_bias, a_log, first):
    sq = (CHUNK, CHUNK)
    lane8 = lax.broadcasted_iota(jnp.int32, (SSD_HPG, CHUNK), 1)
    lane = lax.broadcasted_iota(jnp.int32, sq, 1)
    row = lax.broadcasted_iota(jnp.int32, sq, 0)
    low = lane < SSD_P
    valid = jnp.logical_or(jnp.logical_not(first), lane8 >= PAD_ROWS)
    pre = dt_raw + dt_bias
    dt = jnp.where(valid, _softplus(pre), 0.0)
    a = -jnp.exp(a_log)
    cum = _lane_cumsum(dt * a, lane8)
    cb = _dot(_bf(cm), _bf(bm), NT)
    heads = []
    for r in range(SSD_HPG):
        rowb = jnp.broadcast_to(cum[r:r + 1, :], sq)
        colb = rowb.T
        seg = jnp.exp(jnp.where(lane <= row, colb - rowb, NEG))
        dtrow = jnp.broadcast_to(dt[r:r + 1, :], sq)
        lastb = colb[CHUNK - 1:CHUNK, :]
        heads.append(dict(seg=seg, dtrow=dtrow, w=cb * seg * dtrow, ecol=jnp.exp(colb),
                          dec=jnp.exp(lastb - colb) * dtrow.T, elast=jnp.exp(lastb)))

    def pairs(key):
        return jnp.concatenate([jnp.where(low[:heads[0][key].shape[0]], heads[2 * j][key], heads[2 * j + 1][key])
                                for j in range(SSD_HPG // 2)], axis=1)

    return dict(lane8=lane8, low=low, valid=valid, pre=pre, dt=dt, a=a, cum=cum, cb=cb, heads=heads,
                expc=pairs("ecol"), dec=pairs("dec"), elast=pairs("elast"))


def _ssd_specs(nc, rev, group_major):
    def ix(a, b_, c):
        g, b = (a, b_) if group_major else (b_, a)
        return g, b, (nc - 1 - c) if rev else c

    def row(a, b_, c):
        g, b, cc = ix(a, b_, c)
        return b * nc + cc, g

    return dict(
        wide=pl.BlockSpec((CHUNK, SSD_GW), row),
        narrow=pl.BlockSpec((CHUNK, SSD_N), row),
        dtT=pl.BlockSpec((SSD_HPG, CHUNK), lambda a, b_, c: (ix(a, b_, c)[0], row(a, b_, c)[0])),
        hcol=pl.BlockSpec((SSD_HPG, 1), lambda a, b_, c: (ix(a, b_, c)[0], 0)),
        hacc=pl.BlockSpec((SSD_HPG, CHUNK), lambda a, b_, c: (ix(a, b_, c)[0], 0)),
        gvec=pl.BlockSpec((1, SSD_GW), lambda a, b_, c: (0, ix(a, b_, c)[0])),
        state=pl.BlockSpec((1, SSD_N, SSD_GW),
                           lambda a, b_, c: ((ix(a, b_, c)[1] * SSD_GROUPS + ix(a, b_, c)[0]) * nc + ix(a, b_, c)[2], 0, 0)),
    )


def _ssd_chunk_fwd(xs_pre, bm_pre, cm_pre, dt_raw, dt_bias, a_log, d_exp, z, gnorm, bsz, nc):
    r = xs_pre.shape[0]
    sp = _ssd_specs(nc, False, False)

    def body(xs_ref, bm_ref, cm_ref, dt_ref, db_ref, al_ref, d_ref, z_ref, gn_ref, y_ref, yn_ref, st_ref, st_s):
        c = pl.program_id(2)

        @pl.when(c == 0)
        def _():
            st_s[...] = jnp.zeros_like(st_s)

        state = st_s[...]
        st_ref[0] = state
        xs, bm, cm = _silu(xs_ref[...]), _silu(bm_ref[...]), _silu(cm_ref[...])
        co = _ssd_core(xs, bm, cm, dt_ref[...], db_ref[...], al_ref[...], c == 0)
        low, hd = co["low"], co["heads"]
        ys = []
        for j in range(SSD_HPG // 2):
            xp = xs[:, j * CHUNK:(j + 1) * CHUNK]
            lhs = jnp.concatenate([hd[2 * j]["w"], hd[2 * j + 1]["w"]], axis=1)
            rhs = jnp.concatenate([jnp.where(low, xp, 0.0), jnp.where(low, 0.0, xp)], axis=0)
            ys.append(_dot(_bf(lhs), _bf(rhs)))
        cmb = _bf(cm)
        y = jnp.concatenate(ys, axis=1) + co["expc"] * _dot(cmb, _bf(state)) + d_ref[...] * xs
        y_ref[...] = y
        yg = y * _silu(z_ref[...])
        rstd = lax.rsqrt(jnp.mean(yg * yg, axis=1, keepdims=True) + NORM_EPS)
        yn_ref[...] = _bf(yg * rstd * gn_ref[...])
        st_s[...] = co["elast"] * state + _dot(_bf(bm), _bf(xs * co["dec"]), TN)

    nst = bsz * SSD_GROUPS * nc
    return _pc(body, name="ssd_chunk_fwd", grid=(bsz, SSD_GROUPS, nc),
               in_specs=[sp["wide"], sp["narrow"], sp["narrow"], sp["dtT"], sp["hcol"], sp["hcol"], sp["gvec"],
                         sp["wide"], sp["gvec"]],
               out_specs=[sp["wide"], sp["wide"], sp["state"]],
               out_shape=[jax.ShapeDtypeStruct((r, SSD_INNER), F32), jax.ShapeDtypeStruct((r, SSD_INNER), BF16),
                          jax.ShapeDtypeStruct((nst, SSD_N, SSD_GW), F32)],
               scratch=[pltpu.VMEM((SSD_N, SSD_GW), F32)], vmem=6 << 20,
               )(xs_pre, bm_pre, cm_pre, dt_raw, dt_bias, a_log, d_exp, z, gnorm)


def _ssd_chunk_bwd(dyn, xs_pre, bm_pre, cm_pre, dt_raw, dt_bias, a_log, d_exp, z, gnorm, y, states, bsz, nc):
    r = xs_pre.shape[0]
    sp = _ssd_specs(nc, True, True)

    def body(dyn_ref, xs_ref, bm_ref, cm_ref, dt_ref, db_ref, al_ref, d_ref, z_ref, gn_ref, y_ref, st_ref,
             dxs_ref, dbm_ref, dcm_ref, dz_ref, ddt_ref, dgn_ref, dd_ref, dbias_ref, dal_ref, ds_s):
        b, c = pl.program_id(1), pl.program_id(2)

        @pl.when((b == 0) & (c == 0))
        def _():
            for ref in (dgn_ref, dd_ref, dbias_ref, dal_ref):
                ref[...] = jnp.zeros_like(ref)

        @pl.when(c == 0)
        def _():
            ds_s[...] = jnp.zeros_like(ds_s)

        xs_p, bm_p, cm_p = xs_ref[...], bm_ref[...], cm_ref[...]
        xs, bm, cm = _silu(xs_p), _silu(bm_p), _silu(cm_p)
        state = st_ref[0]
        co = _ssd_core(xs, bm, cm, dt_ref[...], db_ref[...], al_ref[...], c == nc - 1)
        low, hd, lane8, cb = co["low"], co["heads"], co["lane8"], co["cb"]
        dt, a, cum = co["dt"], co["a"], co["cum"]
        sub8 = lax.broadcasted_iota(jnp.int32, (SSD_HPG, CHUNK), 0)
        eh = _head_sum_matrix()

        def head_rows(full):
            return lax.dot_general(eh, full, NT, precision=HI, preferred_element_type=F32)

        def head_col(vec):
            return jnp.sum(eh * vec, axis=1, keepdims=True)

        yv, zv, gn = y_ref[...], z_ref[...], gn_ref[...]
        sz = _silu(zv)
        yg = yv * sz
        rstd = lax.rsqrt(jnp.mean(yg * yg, axis=1, keepdims=True) + NORM_EPS)
        yh = yg * rstd
        dyn = dyn_ref[...]
        dgn_ref[...] += jnp.sum(dyn * yh, axis=0, keepdims=True)
        dyh = dyn * gn
        dyg = rstd * (dyh - yh * jnp.mean(dyh * yh, axis=1, keepdims=True))
        dz_ref[...] = _bf(dyg * yv * _dsilu(zv))
        dy = dyg * sz
        dxs = dy * d_ref[...]
        dd_ref[...] += head_col(jnp.sum(dy * xs, axis=0, keepdims=True))
        cmb, bmb, stb = _bf(cm), _bf(bm), _bf(state)
        ysv = _dot(cmb, stb)
        expc = co["expc"]
        dys = _bf(dy * expc)
        dcum = head_rows(dy * ysv * expc)
        dcm = _dot(dys, stb, NT)
        dstate_out = _dot(cmb, dys, TN)
        dcb = jnp.zeros((CHUNK, CHUNK), F32)
        ddt = jnp.zeros((SSD_HPG, CHUNK), F32)
        dxs_pairs = []
        for j in range(SSD_HPG // 2):
            sl = slice(j * CHUNK, (j + 1) * CHUNK)
            dyp, xp = dy[:, sl], _bf(xs[:, sl])
            lhs = _bf(jnp.concatenate([hd[2 * j]["w"], hd[2 * j + 1]["w"]], axis=1))
            both = _dot(lhs, _bf(dyp), TN)
            dxs_pairs.append(jnp.where(low, both[:CHUNK], both[CHUNK:]))
            for q, msk in ((2 * j, low), (2 * j + 1, jnp.logical_not(low))):
                h = hd[q]
                dw = _dot(_bf(jnp.where(msk, dyp, 0.0)), xp, NT)
                dcb = dcb + dw * h["seg"] * h["dtrow"]
                e_ = dw * h["w"]
                dcum_r = jnp.sum(e_.T, axis=0, keepdims=True) - jnp.sum(e_, axis=0, keepdims=True)
                ddt_r = jnp.sum(dw * cb * h["seg"], axis=0, keepdims=True)
                dcum = dcum + jnp.where(sub8 == q, dcum_r, 0.0)
                ddt = ddt + jnp.where(sub8 == q, ddt_r, 0.0)
        dxs = dxs + jnp.concatenate(dxs_pairs, axis=1)
        dcbb = _bf(dcb)
        dcm = dcm + _dot(dcbb, bmb)
        dbm = _dot(dcbb, cmb, TN)
        dsn = ds_s[...]
        dsb = _bf(dsn)
        dec = co["dec"]
        dbm = dbm + _dot(_bf(xs * dec), dsb, NT)
        dxd = _dot(bmb, dsb)
        dxs = dxs + dxd * dec
        ddec = head_rows(dxd * xs)
        last = cum[:, CHUNK - 1:CHUNK]
        erow = jnp.exp(last - cum)
        ddt = ddt + ddec * erow
        dla = ddec * erow * dt
        dlast = (jnp.sum(dla, axis=1, keepdims=True)
                 + head_col(jnp.sum(dsn * state, axis=0, keepdims=True)) * jnp.exp(last))
        dcum = dcum - dla + jnp.where(lane8 == CHUNK - 1, dlast, 0.0)
        ds_s[...] = co["elast"] * dsn + dstate_out
        dda = _lane_cumsum(dcum, lane8, rev=True)
        ddt = jnp.where(co["valid"], ddt + dda * a, 0.0)
        ddt_raw = ddt * jax.nn.sigmoid(co["pre"])
        ddt_ref[...] = ddt_raw
        dbias_ref[...] += jnp.sum(ddt_raw, axis=1, keepdims=True)
        dal_ref[...] += jnp.sum(dda * dt, axis=1, keepdims=True) * a
        dxs_ref[...] = dxs * _dsilu(xs_p)
        dbm_ref[...] = dbm * _dsilu(bm_p)
        dcm_ref[...] = dcm * _dsilu(cm_p)

    st = jax.ShapeDtypeStruct
    hacc = st((SSD_HEADS, CHUNK), F32)
    return _pc(body, name="ssd_chunk_bwd", grid=(SSD_GROUPS, bsz, nc),
               in_specs=[sp["wide"], sp["wide"], sp["narrow"], sp["narrow"], sp["dtT"], sp["hcol"], sp["hcol"],
                         sp["gvec"], sp["wide"], sp["gvec"], sp["wide"], sp["state"]],
               out_specs=[sp["wide"], sp["narrow"], sp["narrow"], sp["wide"], sp["dtT"], sp["gvec"], sp["hacc"],
                          sp["hacc"], sp["hacc"]],
               out_shape=[st((r, SSD_INNER), F32), st((r, SSD_GROUPS * SSD_N), F32), st((r, SSD_GROUPS * SSD_N), F32),
                          st((r, SSD_INNER), BF16), st((SSD_HEADS, r), F32), st((1, SSD_INNER), F32), hacc, hacc, hacc],
               scratch=[pltpu.VMEM((SSD_N, SSD_GW), F32)], vmem=10 << 20,
               )(dyn, xs_pre, bm_pre, cm_pre, dt_raw, dt_bias, a_log, d_exp, z, gnorm, y, states)


SSD_BC = SSD_GROUPS * SSD_N


def _ssd_weights(conv_w, conv_b, dt_bias, a_log, d, gnorm):
    cuts = (0, SSD_INNER, SSD_INNER + SSD_BC, SSD_INNER + 2 * SSD_BC)
    return dict(
        conv_w=[conv_w[:, cuts[i]:cuts[i + 1]] for i in range(3)],
        conv_b=[conv_b[cuts[i]:cuts[i + 1]] for i in range(3)],
        dt_bias=dt_bias.reshape(SSD_HEADS, 1), a_log=a_log.reshape(SSD_HEADS, 1),
        d_exp=jnp.repeat(d.reshape(SSD_HEADS), SSD_P).reshape(1, SSD_INNER), gnorm=gnorm.reshape(1, SSD_INNER))


def _ssd_layer_fwd(z, xs_in, bm_in, cm_in, dt_rows, w, bsz, nc):
    pres = [_conv_fwd(a, w["conv_w"][i], w["conv_b"][i], bsz, nc, name=f"ssd_conv{i}")
            for i, a in enumerate((xs_in, bm_in, cm_in))]
    dt_t = dt_rows[:, :SSD_HEADS].T
    y, yn, states = _ssd_chunk_fwd(pres[0], pres[1], pres[2], dt_t, w["dt_bias"], w["a_log"], w["d_exp"], z,
                                   w["gnorm"], bsz, nc)
    return yn, dict(pres=pres, dt_t=dt_t, y=y, states=states)


def _ssd_layer_bwd(dyn, z, xs_in, bm_in, cm_in, sv, w, bsz, nc):
    pres = sv["pres"]
    dxs_p, dbm_p, dcm_p, dz, ddt_t, dgn, dd, dbias, dal = _ssd_chunk_bwd(
        dyn, pres[0], pres[1], pres[2], sv["dt_t"], w["dt_bias"], w["a_log"], w["d_exp"], z, w["gnorm"], sv["y"],
        sv["states"], bsz, nc)
    outs = [_conv_bwd(dp, a, w["conv_w"][i], bsz, nc, name=f"ssd_conv_bwd{i}")
            for i, (dp, a) in enumerate(((dxs_p, xs_in), (dbm_p, bm_in), (dcm_p, cm_in)))]
    ddt = _bf(_pad_lanes(ddt_t.T))
    grads = dict(
        ssd_conv_w=jnp.concatenate([o[1] for o in outs], axis=1),
        ssd_conv_b=jnp.concatenate([o[2] for o in outs], axis=1),
        ssd_dt_bias=dbias[:, 0], ssd_a_log=dal[:, 0], ssd_d=dd[:, 0], ssd_gnorm=dgn)
    return dz, outs[0][0], outs[1][0], outs[2][0], ddt, grads


WNAMES = ("meta_tokens", "ab_norm", "ab_w_in", "s5_lambda_re", "s5_lambda_im", "s5_log_dt", "s5_b_re", "s5_b_im",
          "s5_c_re", "s5_c_im", "s5_d", "s5_glu_w", "s5_glu_b", "ml_conv_w", "ml_conv_b", "ml_wq", "ml_wk", "ml_wv",
          "ml_w_gate", "ml_b_gate", "ml_norm", "ml_skip", "ab_w_out", "ssd_norm", "ssd_w_in", "ssd_conv_w",
          "ssd_conv_b", "ssd_dt_bias", "ssd_a_log", "ssd_d", "ssd_gnorm", "ssd_w_out", "final_norm")
SHARD_AXIS = dict(meta_tokens=1, ab_w_in=2, s5_glu_w=1, ml_conv_w=2, ml_wq=1, ml_wk=1, ml_wv=1, ml_w_gate=1,
                  ab_w_out=1, ssd_norm=1, ssd_w_in=2, ssd_conv_w=2, ssd_conv_b=1, ssd_gnorm=1, ssd_w_out=1)
BIG = ("ab_w_in", "s5_glu_w", "ab_w_out", "ssd_w_in", "ssd_w_out")
SMALL = tuple(n for n in WNAMES if n in SHARD_AXIS and n not in BIG)
REPL = tuple(n for n in WNAMES if n not in SHARD_AXIS)
PACK_ALIGN = 8 * 128


def _pack(arrs):
    lead = arrs[0][1]
    parts = []
    for a, nlead in arrs:
        f = a.reshape(a.shape[:nlead] + (-1,))
        f = jnp.pad(f, [(0, 0)] * nlead + [(0, (-f.shape[-1]) % PACK_ALIGN)])
        parts.append(f.reshape(f.shape[:nlead] + (-1, 128)))
    return jnp.concatenate(parts, axis=lead)


def _unpack(p, shapes):
    out, r0 = [], 0
    lead = p.shape[:-2]
    for s in shapes:
        n = math.prod(s)
        rows = -(-n // PACK_ALIGN) * 8
        seg = p[..., r0:r0 + rows, :].reshape(lead + (rows * 128,))[..., :n]
        out.append(seg.reshape(lead + tuple(s)))
        r0 += rows
    return out


def _assemble(g, axis):
    m = jnp.moveaxis(g, 0, axis)
    return m.reshape(m.shape[:axis] + (m.shape[axis] * m.shape[axis + 1],) + m.shape[axis + 2:])


def _split(full, axis):
    s = full.shape
    m = full.reshape(s[:axis] + (N_DEV, s[axis] // N_DEV) + s[axis + 1:])
    return jnp.moveaxis(m, axis, 0)


def kernel(x, *rest):
    nw = len(WNAMES)
    w = dict(zip(WNAMES, rest[:nw]))
    loss_target = rest[nw]
    mom = dict(zip(WNAMES, rest[nw + 1:2 * nw + 1]))
    var = dict(zip(WNAMES, rest[2 * nw + 1:3 * nw + 1]))
    bsz = x.shape[0]
    nc = 1 + SEQ // CHUNK
    tp = nc * CHUNK

    big_local = [_bf(w[n][0]) for n in BIG]
    small_local = _pack([(w[n], 0) for n in SMALL])
    gathered = _exchange(big_local + [small_local], ["ag"] * (len(BIG) + 1), name="gather_weights")
    full = {n: _assemble(g[:, None], SHARD_AXIS[n])[0] for n, g in zip(BIG, gathered)}
    for n, g in zip(SMALL, _unpack(gathered[-1], [w[n].shape for n in SMALL])):
        full[n] = _assemble(g, SHARD_AXIS[n])[0] if n != "meta_tokens" else _assemble(g, SHARD_AXIS[n])
    for n in REPL:
        full[n] = w[n][0] if n != "final_norm" else w[n]
    w_in0, w_out0, w_in1, w_out1 = full["ab_w_in"], full["ab_w_out"], full["ssd_w_in"], full["ssd_w_out"]
    cuts0 = (0, S5_WIDTH, 2 * S5_WIDTH, 2 * S5_WIDTH + ML_WIDTH, 2 * (S5_WIDTH + ML_WIDTH))
    w_in0 = [w_in0[:, cuts0[i]:cuts0[i + 1]] for i in range(4)]
    w_out0 = [w_out0[:S5_WIDTH], w_out0[S5_WIDTH:]]
    cuts1 = (0, SSD_INNER, 2 * SSD_INNER, 2 * SSD_INNER + SSD_BC, 2 * SSD_INNER + 2 * SSD_BC)
    w_in1 = [w_in1[:, cuts1[i]:cuts1[i + 1]] for i in range(4)] + [_pad_lanes(w_in1[:, cuts1[4]:])]
    glu_w = full["s5_glu_w"]
    glu_b = full["s5_glu_b"].reshape(1, S5_WIDTH)

    meta = jnp.broadcast_to(full["meta_tokens"][None], (bsz, N_META, D_MODEL))
    h0 = jnp.concatenate([jnp.zeros((bsz, PAD_ROWS, D_MODEL), F32), meta, x], axis=1).reshape(bsz * tp, D_MODEL)
    xn0 = _rms_fwd(h0, full["ab_norm"], name="rms0")
    u, za, xb, zb = [_mm(xn0, wi, "NN", name=f"in0_{i}") for i, wi in enumerate(w_in0)]
    s5p, s5_vjp = _s5_tables(*[full[n] for n in ("s5_lambda_re", "s5_lambda_im", "s5_log_dt", "s5_b_re", "s5_b_im",
                                                   "s5_c_re", "s5_c_im", "s5_d")])
    sv5 = _s5_layer_fwd(u, s5p, glu_w, bsz, nc)
    ya = _s5_post(sv5["y1"], sv5["glu_pre"], glu_b, za)
    mlw = _ml_weights(*[full[n] for n in ("ml_conv_w", "ml_conv_b", "ml_wq", "ml_wk", "ml_wv", "ml_w_gate",
                                           "ml_b_gate", "ml_norm", "ml_skip")])
    yb, svm = _ml_layer_fwd(xb, zb, mlw, bsz, nc)
    h1 = _mm(ya, w_out0[0], "NN", name="out0_a", add=h0)
    h1 = _mm(yb, w_out0[1], "NN", name="out0_b", add=h1)
    xn1 = _rms_fwd(h1, full["ssd_norm"], name="rms1")
    z1, xs_in, bm_in, cm_in, dt_rows = [_mm(xn1, wi, "NN", name=f"in1_{i}") for i, wi in enumerate(w_in1)]
    ssdw = _ssd_weights(*[full[n] for n in ("ssd_conv_w", "ssd_conv_b", "ssd_dt_bias", "ssd_a_log", "ssd_d",
                                             "ssd_gnorm")])
    yn, svs = _ssd_layer_fwd(z1, xs_in, bm_in, cm_in, dt_rows, ssdw, bsz, nc)
    h2 = _mm(yn, w_out1, "NN", name="out1", add=h1)
    loss_part, dh2, dfinal = _final_loss(h2, full["final_norm"], loss_target, bsz, nc)
    loss = lax.psum(loss_part[0, 0], ("x", "y", "c"))

    g = {"final_norm": dfinal}
    dyn = _mm(dh2, w_out1, "NT", name="d_out1")
    g["ssd_w_out"] = _mm(yn, dh2, "TN", name="dw_out1", out_dtype=BF16)
    dz1, dxs, dbm, dcm, ddt, gs = _ssd_layer_bwd(dyn, z1, xs_in, bm_in, cm_in, svs, ssdw, bsz, nc)
    g.update(gs)
    dps1 = (dz1, dxs, dbm, dcm, ddt)
    dxn1 = None
    for i, (dp, wi) in enumerate(zip(dps1, w_in1)):
        dxn1 = _mm(dp, wi, "NT", name=f"d_in1_{i}", add=dxn1)
    dw1 = [_mm(xn1, dp, "TN", name=f"dw_in1_{i}", out_dtype=BF16) for i, dp in enumerate(dps1)]
    g["ssd_w_in"] = jnp.concatenate(dw1[:4] + [dw1[4][:, :SSD_HEADS]], axis=1)
    dh1, g["ssd_norm"] = _rms_bwd(h1, full["ssd_norm"], dxn1, dh2, name="rms1_bwd")
    dya = _mm(dh1, w_out0[0], "NT", name="d_out0_a")
    dyb = _mm(dh1, w_out0[1], "NT", name="d_out0_b")
    g["ab_w_out"] = jnp.concatenate([_mm(ya, dh1, "TN", name="dw_out0_a", out_dtype=BF16),
                                     _mm(yb, dh1, "TN", name="dw_out0_b", out_dtype=BF16)], axis=0)
    du, dza, g5 = _s5_layer_bwd(dya, u, za, sv5, s5p, s5_vjp, glu_w, glu_b, bsz, nc)
    dxb, dzb, gm = _ml_layer_bwd(dyb, xb, zb, svm, mlw, bsz, nc)
    g.update(g5)
    g.update(gm)
    dps0 = (du, dza, dxb, dzb)
    dxn0 = None
    for i, (dp, wi) in enumerate(zip(dps0, w_in0)):
        dxn0 = _mm(dp, wi, "NT", name=f"d_in0_{i}", add=dxn0)
    dw0 = [_mm(xn0, dp, "TN", name=f"dw_in0_{i}", out_dtype=BF16, tn=S5_WIDTH, slabs=True) for i, dp in enumerate(dps0)]
    dw_in0_slabs = jnp.concatenate(dw0, axis=0)
    dh0, g["ab_norm"] = _rms_bwd(h0, full["ab_norm"], dxn0, dh1, name="rms0_bwd")
    dh0 = dh0.reshape(bsz, tp, D_MODEL)
    grad_x = dh0[:, CHUNK:]
    g["meta_tokens"] = jnp.sum(dh0[:, PAD_ROWS:CHUNK], axis=0)

    def local_shape(n):
        return w[n].shape

    def slabs(n):
        gf = g[n].reshape((1,) + tuple(g[n].shape)) if n != "meta_tokens" else g[n]
        full_shape = tuple(d * (N_DEV if i == SHARD_AXIS[n] else 1) for i, d in enumerate(local_shape(n)))
        return _split(gf.reshape(full_shape), SHARD_AXIS[n])

    big_g = [dw_in0_slabs, _bf(slabs("s5_glu_w")[:, 0]), slabs("ab_w_out")[:, 0], slabs("ssd_w_in")[:, 0],
             slabs("ssd_w_out")[:, 0]]
    small_g = _pack([(slabs(n), 1) for n in SMALL])
    repl_g = _pack([(g[n], 0) for n in REPL])
    ex = _exchange(big_g + [small_g, repl_g], ["a2a"] * (len(BIG) + 1) + ["ag"], name="exchange_grads")

    res = {}
    for n, gp in zip(BIG, ex):
        res[n] = _adamw(w[n][0], mom[n][0], var[n][0], gp, name=f"adamw_{n}")
    for names, gp, tag in ((SMALL, ex[-2], "small"), (REPL, ex[-1], "repl")):
        shapes = [local_shape(n) for n in names]
        packs = [_pack([(d[n], 0) for n in names]) for d in (w, mom, var)]
        outs = _adamw(packs[0], packs[1], packs[2], gp, name=f"adamw_{tag}")
        for k, o in enumerate(outs):
            for n, a in zip(names, _unpack(o, shapes)):
                res.setdefault(n, [None] * 4)[k] = a
    outs = [loss, grad_x]
    for k in range(4):
        outs += [res[n][k].reshape(local_shape(n)) for n in WNAMES]
    return tuple(outs)
```

```python
import functools
import math

import jax
import jax.numpy as jnp
from jax import lax
from jax.experimental import pallas as pl
from jax.experimental.pallas import tpu as pltpu

F32 = jnp.float32
BF16 = jnp.bfloat16

D_MODEL = 2048
SEQ = 2048
N_META = 16
CHUNK = 128
PAD_ROWS = CHUNK - N_META
NORM_EPS = 1e-6
HEAD_NORM_EPS = 1e-5
S5_WIDTH = 1024
S5_GROUPS = 64
S5_GROUP_SIZE = 16
S5_STATE = 64
S5_GB = 8
S5_LANES = S5_GB * S5_STATE
ML_WIDTH = 3072
ML_HEADS = 8
ML_DH = 384
ML_CONV = 4
QKV_BLOCK = 4
SSD_INNER = 4096
SSD_HEADS = 64
SSD_P = 64
SSD_N = 128
SSD_GROUPS = 8
SSD_HPG = 8
SSD_GW = SSD_HPG * SSD_P
N_DEV = 8
ADAM_LR, ADAM_B1, ADAM_B2, ADAM_EPS, ADAM_WD, ADAM_STEP = 0.001, 0.9, 0.999, 1e-08, 0.01, 10
NEG = -1e30
VMEM_CAP = 60 * 1024 * 1024
MESH = pl.DeviceIdType.MESH

NN = (((1,), (0,)), ((), ()))
NT = (((1,), (1,)), ((), ()))
TN = (((0,), (0,)), ((), ()))


def _dot(a, b, dims=NN):
    return lax.dot_general(a, b, dims, preferred_element_type=F32)


def _bf(x):
    return x.astype(BF16)


def _pick(n, cands):
    for c in cands:
        if n % c == 0:
            return c
    return n


def _nbytes(shape, dtype):
    return math.prod(shape) * jnp.dtype(dtype).itemsize


def _pc(body, *, name, grid, in_specs, out_specs, out_shape, scratch=(), vmem=None):
    limit = None if vmem is None else int(min(VMEM_CAP, max(32 * 1024 * 1024, 2 * vmem + (8 << 20))))
    return pl.pallas_call(
        body, name=name, grid=grid, in_specs=in_specs, out_specs=out_specs, out_shape=out_shape,
        scratch_shapes=list(scratch),
        compiler_params=pltpu.CompilerParams(dimension_semantics=("arbitrary",) * len(grid), vmem_limit_bytes=limit))


def _silu(x):
    return x * jax.nn.sigmoid(x)


def _dsilu(x):
    s = jax.nn.sigmoid(x)
    return s * (1.0 + x * (1.0 - s))


def _gelu_and_grad(x):
    c0 = math.sqrt(2.0 / math.pi)
    inner = c0 * (x + 0.044715 * x * x * x)
    t = jnp.tanh(inner)
    g = 0.5 * x * (1.0 + t)
    dg = 0.5 * (1.0 + t) + 0.5 * x * (1.0 - t * t) * c0 * (1.0 + 3 * 0.044715 * x * x)
    return g, dg


def _mm(a, b, mode, *, name, add=None, out_dtype=F32, tn=None, slabs=False):
    if mode == "NN":
        (m, k), (k2, n) = a.shape, b.shape
    elif mode == "NT":
        (m, k), (n, k2) = a.shape, b.shape
    else:
        (k, m), (k2, n) = a.shape, b.shape
    assert k == k2, (a.shape, b.shape, mode)
    tm = _pick(m, (1088, 1024, 768, 512, 384, 256, 128))
    tn = tn or _pick(n, (512, 384, 256, 128))
    tk = _pick(k, (2048, 1088, 1024, 768, 512, 384, 256, 128))
    nk = k // tk
    dims = {"NN": NN, "NT": NT, "TN": TN}[mode]

    def body(*refs):
        a_ref, b_ref = refs[0], refs[1]
        add_ref = refs[2] if add is not None else None
        o_ref, acc_ref = refs[-2], refs[-1]
        kk = pl.program_id(2)

        @pl.when(kk == 0)
        def _():
            acc_ref[...] = jnp.zeros_like(acc_ref)

        acc_ref[...] += _dot(_bf(a_ref[...]), _bf(b_ref[...]), dims)

        @pl.when(kk == nk - 1)
        def _():
            r = acc_ref[...]
            if add_ref is not None:
                r = r + add_ref[...]
            o_ref[...] = r.reshape(o_ref.shape).astype(o_ref.dtype)

    if mode == "NN":
        a_spec = pl.BlockSpec((tm, tk), lambda i, j, kk: (i, kk))
        b_spec = pl.BlockSpec((tk, tn), lambda i, j, kk: (kk, j))
    elif mode == "NT":
        a_spec = pl.BlockSpec((tm, tk), lambda i, j, kk: (i, kk))
        b_spec = pl.BlockSpec((tn, tk), lambda i, j, kk: (j, kk))
    else:
        a_spec = pl.BlockSpec((tk, tm), lambda i, j, kk: (kk, i))
        b_spec = pl.BlockSpec((tk, tn), lambda i, j, kk: (kk, j))
    in_specs = [a_spec, b_spec]
    args = [a, b]
    if add is not None:
        in_specs.append(pl.BlockSpec((tm, tn), lambda i, j, kk: (i, j)))
        args.append(add)
    if slabs:
        out_shape = jax.ShapeDtypeStruct((n // tn, m, tn), out_dtype)
        out_spec = pl.BlockSpec((1, tm, tn), lambda i, j, kk: (j, i, 0))
    else:
        out_shape = jax.ShapeDtypeStruct((m, n), out_dtype)
        out_spec = pl.BlockSpec((tm, tn), lambda i, j, kk: (i, j))
    vmem = (_nbytes((tm, tk), a.dtype) + _nbytes((tk, tn), b.dtype) + _nbytes((tm, tn), out_dtype)
            + (_nbytes((tm, tn), F32) if add is not None else 0)) + _nbytes((tm, tn), F32) // 2
    return _pc(body, name=name, grid=(m // tm, n // tn, nk), in_specs=in_specs, out_specs=out_spec,
               out_shape=out_shape, scratch=[pltpu.VMEM((tm, tn), F32)], vmem=vmem)(*args)


def _rms_fwd(x, g, *, name):
    r, d = x.shape
    tm = _pick(r, (256, 128))

    def body(x_ref, g_ref, o_ref):
        xv = x_ref[...]
        rstd = lax.rsqrt(jnp.mean(xv * xv, axis=1, keepdims=True) + NORM_EPS)
        o_ref[...] = (xv * rstd * g_ref[...]).astype(o_ref.dtype)

    return _pc(body, name=name, grid=(r // tm,),
               in_specs=[pl.BlockSpec((tm, d), lambda i: (i, 0)), pl.BlockSpec((1, d), lambda i: (0, 0))],
               out_specs=pl.BlockSpec((tm, d), lambda i: (i, 0)), out_shape=jax.ShapeDtypeStruct((r, d), BF16),
               vmem=tm * d * 6)(x, g.reshape(1, d))


def _rms_bwd(x, g, dxn, dres, *, name):
    r, d = x.shape
    tm = _pick(r, (256, 128))

    def body(x_ref, g_ref, dxn_ref, dres_ref, dx_ref, dg_ref):
        @pl.when(pl.program_id(0) == 0)
        def _():
            dg_ref[...] = jnp.zeros_like(dg_ref)

        xv = x_ref[...]
        rstd = lax.rsqrt(jnp.mean(xv * xv, axis=1, keepdims=True) + NORM_EPS)
        xh = xv * rstd
        dy = dxn_ref[...]
        dg_ref[...] += jnp.sum(dy * xh, axis=0, keepdims=True)
        dyg = dy * g_ref[...]
        dx_ref[...] = dres_ref[...] + rstd * (dyg - xh * jnp.mean(dyg * xh, axis=1, keepdims=True))

    row = pl.BlockSpec((tm, d), lambda i: (i, 0))
    vec = pl.BlockSpec((1, d), lambda i: (0, 0))
    return _pc(body, name=name, grid=(r // tm,), in_specs=[row, vec, row, row], out_specs=[row, vec],
               out_shape=[jax.ShapeDtypeStruct((r, d), F32), jax.ShapeDtypeStruct((1, d), F32)],
               vmem=tm * d * 16)(x, g.reshape(1, d), dxn, dres)


def _final_loss(h, g, target, bsz, nc):
    d = h.shape[1]

    def body(h_ref, g_ref, t_ref, loss_ref, dh_ref, dg_ref):
        b, c = pl.program_id(0), pl.program_id(1)

        @pl.when((b == 0) & (c == 0))
        def _():
            loss_ref[...] = jnp.zeros_like(loss_ref)
            dg_ref[...] = jnp.zeros_like(dg_ref)

        @pl.when(c == 0)
        def _():
            dh_ref[...] = jnp.zeros_like(dh_ref)

        @pl.when(c > 0)
        def _():
            xv = h_ref[...]
            rstd = lax.rsqrt(jnp.mean(xv * xv, axis=1, keepdims=True) + NORM_EPS)
            xh = xv * rstd
            gv = g_ref[...]
            err = xh * gv - t_ref[0]
            loss_ref[...] += 0.5 * jnp.sum(jnp.mean(err * err, axis=1, keepdims=True))
            dy = err * (1.0 / d)
            dg_ref[...] += jnp.sum(dy * xh, axis=0, keepdims=True)
            dyg = dy * gv
            dh_ref[...] = rstd * (dyg - xh * jnp.mean(dyg * xh, axis=1, keepdims=True))

    row = pl.BlockSpec((CHUNK, d), lambda b, c: (b * nc + c, 0))
    vec = pl.BlockSpec((1, d), lambda b, c: (0, 0))
    return _pc(body, name="final_loss", grid=(bsz, nc),
               in_specs=[row, vec, pl.BlockSpec((1, CHUNK, d), lambda b, c: (b, jnp.maximum(c - 1, 0), 0))],
               out_specs=[pl.BlockSpec((8, 128), lambda b, c: (0, 0)), row, vec],
               out_shape=[jax.ShapeDtypeStruct((8, 128), F32), jax.ShapeDtypeStruct(h.shape, F32),
                          jax.ShapeDtypeStruct((1, d), F32)],
               vmem=CHUNK * d * 16)(h, g.reshape(1, d), target)


def _adamw(w, m, v, gparts, *, name):
    r, c = w.shape
    tr = _pick(r, (256, 128)) if r * c * 4 > (1 << 20) else r

    def body(w_ref, m_ref, v_ref, gp_ref, g_ref, d_ref, nm_ref, nv_ref):
        g = gp_ref[0].astype(F32)
        for j in range(1, N_DEV):
            g = g + gp_ref[j].astype(F32)
        mm = ADAM_B1 * m_ref[...] + (1.0 - ADAM_B1) * g
        vv = ADAM_B2 * v_ref[...] + (1.0 - ADAM_B2) * (g * g)
        m_hat = mm / (1.0 - ADAM_B1 ** ADAM_STEP)
        v_hat = vv / (1.0 - ADAM_B2 ** ADAM_STEP)
        g_ref[...] = g
        d_ref[...] = -ADAM_LR * (m_hat / (jnp.sqrt(v_hat) + ADAM_EPS) + ADAM_WD * w_ref[...])
        nm_ref[...] = mm
        nv_ref[...] = vv

    blk = pl.BlockSpec((tr, c), lambda i: (i, 0))
    out = jax.ShapeDtypeStruct((r, c), F32)
    return _pc(body, name=name, grid=(r // tr,),
               in_specs=[blk, blk, blk, pl.BlockSpec((N_DEV, tr, c), lambda i: (0, i, 0))],
               out_specs=[blk, blk, blk, blk], out_shape=[out, out, out, out],
               vmem=tr * c * (4 * 7 + N_DEV * jnp.dtype(gparts.dtype).itemsize))(w, m, v, gparts)


PEERS = (1, 2, 4, 6, 3, 5, 7)
HBM_SPEC = pl.BlockSpec(memory_space=pltpu.HBM)
SEM_SPEC = pl.BlockSpec(memory_space=pltpu.SEMAPHORE)
SIDE_EFFECT = pltpu.SideEffectType.DATAFLOW_SIDE_EFFECTING


def _peer(p):
    x, y, c = lax.axis_index("x"), lax.axis_index("y"), lax.axis_index("c")
    tx, ty, tc = x ^ ((p >> 2) & 1), y ^ ((p >> 1) & 1), c ^ (p & 1)
    return (tx, ty, tc), 4 * tx + 2 * ty + tc


def _place_own(a, kind, *, name):
    def body(in_ref, out_ref, sem):
        me = 4 * lax.axis_index("x") + 2 * lax.axis_index("y") + lax.axis_index("c")
        cp = pltpu.make_async_copy(in_ref.at[me] if kind == "a2a" else in_ref, out_ref.at[me], sem)
        cp.start()
        cp.wait()

    shape = (N_DEV,) + (a.shape[1:] if kind == "a2a" else a.shape)
    return pl.pallas_call(body, name=name, in_specs=[HBM_SPEC], out_specs=HBM_SPEC,
                          out_shape=jax.ShapeDtypeStruct(shape, a.dtype),
                          scratch_shapes=[pltpu.SemaphoreType.DMA(())])(a)


def _exchange_copies(ins, lands, send_sems, recv_sems, kinds, incoming):
    me = 4 * lax.axis_index("x") + 2 * lax.axis_index("y") + lax.axis_index("c")
    copies = []
    for i, kind in enumerate(kinds):
        for p in PEERS:
            dev, tgt = _peer(p)
            k = i * (N_DEV - 1) + p - 1
            copies.append(pltpu.make_async_remote_copy(
                src_ref=ins[i].at[tgt] if kind == "a2a" else ins[i], dst_ref=lands[i].at[tgt if incoming else me],
                send_sem=send_sems.at[k], recv_sem=recv_sems.at[k], device_id=dev, device_id_type=MESH))
    return copies


def _exchange_start(arrays, kinds, *, name):
    n = len(arrays)
    lands = [_place_own(a, k, name=f"{name}_own{i}") for i, (a, k) in enumerate(zip(arrays, kinds))]

    def body(*refs):
        ins, lnd = refs[:n], refs[n:2 * n]
        send_sems, recv_sems = refs[2 * n], refs[2 * n + 1]
        token = refs[-1]
        for cp in _exchange_copies(ins, lnd, send_sems, recv_sems, kinds, False):
            cp.start()
        token[...] = jnp.zeros_like(token)

    sem = pltpu.SemaphoreType.DMA((n * (N_DEV - 1),))
    outs = pl.pallas_call(
        body, name=name, in_specs=[HBM_SPEC] * (2 * n),
        out_specs=[SEM_SPEC, SEM_SPEC] + [HBM_SPEC] * (2 * n) + [pl.BlockSpec(memory_space=pltpu.VMEM)],
        out_shape=[sem, sem] + [pltpu.HBM(a.shape, a.dtype) for a in arrays + lands]
        + [jax.ShapeDtypeStruct((8, 128), F32)],
        input_output_aliases={i: 2 + i for i in range(2 * n)},
        compiler_params=pltpu.CompilerParams(has_side_effects=SIDE_EFFECT),
    )(*[pltpu.with_memory_space_constraint(a, pltpu.HBM) for a in arrays + lands])
    return dict(send=outs[0], recv=outs[1], ins=list(outs[2:2 + n]), lands=list(outs[2 + n:2 + 2 * n]),
                token=outs[-1], kinds=kinds, name=name)


def _exchange_wait(h, after):
    n = len(h["ins"])
    kinds = h["kinds"]

    def body(*refs):
        ins, lnd = refs[:n], refs[n:2 * n]
        send_sems, recv_sems = refs[2 * n], refs[2 * n + 1]
        copies = _exchange_copies(ins, lnd, send_sems, recv_sems, kinds, True)
        for cp in copies:
            cp.wait_recv()
        for cp in copies:
            cp.wait_send()

    arrs = h["ins"] + h["lands"]
    outs = pl.pallas_call(
        body, name=h["name"] + "_wait", in_specs=[HBM_SPEC] * (2 * n) + [SEM_SPEC, SEM_SPEC, pl.BlockSpec(memory_space=pl.ANY)],
        out_specs=[HBM_SPEC] * (2 * n), out_shape=[pltpu.HBM(a.shape, a.dtype) for a in arrs],
        input_output_aliases={i: i for i in range(2 * n)},
        compiler_params=pltpu.CompilerParams(has_side_effects=SIDE_EFFECT),
    )(*arrs, h["send"], h["recv"], after)
    return list(outs[n:])


def _tie(x, dep):
    return lax.optimization_barrier((x, dep))[0]


def _s5_params(lam_re, lam_im, log_dt, b_re, b_im):
    dt = jnp.exp(log_dt)[:, None]
    mag = jnp.exp(lam_re * dt)
    ar, ai = mag * jnp.cos(lam_im * dt), mag * jnp.sin(lam_im * dt)
    den = lam_re * lam_re + lam_im * lam_im
    qr = ((ar - 1.0) * lam_re + ai * lam_im) / den
    qi = (ai * lam_re - (ar - 1.0) * lam_im) / den
    bbr = qr[..., None] * b_re - qi[..., None] * b_im
    bbi = qr[..., None] * b_im + qi[..., None] * b_re
    return ar, ai, bbr, bbi


def _s5_power_table(ar, ai):
    pr, pi = ar.reshape(1, -1), ai.reshape(1, -1)
    while pr.shape[0] < CHUNK:
        sr, si = pr[-1:], pi[-1:]
        pr, pi = (jnp.concatenate([pr, pr * sr - pi * si], axis=0), jnp.concatenate([pi, pr * si + pi * sr], axis=0))
    return pr, pi


def _blockdiag(w, rows, cols):
    w = w.reshape(S5_GB, S5_GB, rows, cols)
    eye = jnp.eye(S5_GB, dtype=w.dtype)
    return jnp.einsum("abrc,bd->abrdc", w, eye).reshape(S5_GB, S5_GB * rows, S5_GB * cols)


def _blockdiag_extract(w, rows, cols):
    w = w.reshape(S5_GB, S5_GB, rows, S5_GB, cols)
    return jnp.einsum("abrbc->abrc", w).reshape(S5_GROUPS, rows, cols)


def _s5_scan_specs(bsz, nc, rev):
    def chunk(b, c):
        return b * nc + ((nc - 1 - c) if rev else c)

    return dict(
        u=pl.BlockSpec((CHUNK, CHUNK), lambda g, b, c: (chunk(b, c), g)),
        x=pl.BlockSpec((CHUNK, S5_LANES), lambda g, b, c: (chunk(b, c), g)),
        wb=pl.BlockSpec((1, CHUNK, S5_LANES), lambda g, b, c: (g, 0, 0)),
        wc=pl.BlockSpec((1, S5_LANES, CHUNK), lambda g, b, c: (g, 0, 0)),
        tab=pl.BlockSpec((CHUNK, S5_LANES), lambda g, b, c: (0, g)),
        step=pl.BlockSpec((8, S5_LANES), lambda g, b, c: (0, g)),
        d=pl.BlockSpec((1, CHUNK), lambda g, b, c: (0, g)),
        lane=pl.BlockSpec((1, S5_LANES), lambda g, b, c: (0, g)),
        xprev=pl.BlockSpec((8, S5_LANES), lambda g, b, c: (jnp.maximum(chunk(b, c) * (CHUNK // 8) - 1, 0), g)),
    )


def _s5_fwd(u, wbr, wbi, pr, pi, sr, si, wcr, wci, d, bsz, nc):
    r = u.shape[0]
    sp = _s5_scan_specs(bsz, nc, False)

    def body(u_ref, wbr_ref, wbi_ref, pr_ref, pi_ref, sr_ref, si_ref, wcr_ref, wci_ref, d_ref,
             xr_ref, xi_ref, y1_ref, g_ref, cr_s, ci_s):
        @pl.when(pl.program_id(2) == 0)
        def _():
            cr_s[...] = jnp.zeros_like(cr_s)
            ci_s[...] = jnp.zeros_like(ci_s)

        uv = u_ref[...]
        ub = _bf(uv)
        xr, xi = _dot(ub, wbr_ref[0]), _dot(ub, wbi_ref[0])
        row = lax.broadcasted_iota(jnp.int32, (CHUNK, S5_LANES), 0)
        for k in range(7):
            s = 1 << k
            ar, ai = sr_ref[k:k + 1, :], si_ref[k:k + 1, :]
            hr = jnp.where(row >= s, pltpu.roll(xr, s, 0), 0.0)
            hi = jnp.where(row >= s, pltpu.roll(xi, s, 0), 0.0)
            xr, xi = xr + (ar * hr - ai * hi), xi + (ar * hi + ai * hr)
        cr, ci = cr_s[...], ci_s[...]
        tr, ti = pr_ref[...], pi_ref[...]
        xr, xi = xr + (tr * cr - ti * ci), xi + (tr * ci + ti * cr)
        cr_s[...] = xr[CHUNK - 1:CHUNK, :]
        ci_s[...] = xi[CHUNK - 1:CHUNK, :]
        xr_ref[...] = xr
        xi_ref[...] = xi
        y = _dot(_bf(xr), wcr_ref[0]) - _dot(_bf(xi), wci_ref[0]) + d_ref[...] * uv
        y1_ref[...] = y
        g_ref[...] = _bf(_gelu_and_grad(y)[0])

    return _pc(body, name="s5_fwd", grid=(S5_GB, bsz, nc),
               in_specs=[sp["u"], sp["wb"], sp["wb"], sp["tab"], sp["tab"], sp["step"], sp["step"], sp["wc"], sp["wc"],
                         sp["d"]],
               out_specs=[sp["x"], sp["x"], sp["u"], sp["u"]],
               out_shape=[jax.ShapeDtypeStruct((r, S5_GROUPS * S5_STATE), F32)] * 2
               + [jax.ShapeDtypeStruct((r, S5_WIDTH), F32), jax.ShapeDtypeStruct((r, S5_WIDTH), BF16)],
               scratch=[pltpu.VMEM((1, S5_LANES), F32)] * 2, vmem=4 << 20,
               )(u, wbr, wbi, pr, pi, sr, si, wcr, wci, d)


def _s5_post(y1, glu_pre, glu_b, z):
    r, w = y1.shape
    tm = _pick(r, (256, 128))

    def body(y_ref, p_ref, b_ref, z_ref, o_ref):
        g = _gelu_and_grad(y_ref[...])[0]
        o_ref[...] = _bf(g * jax.nn.sigmoid(p_ref[...] + b_ref[...]) * _silu(z_ref[...]))

    row = pl.BlockSpec((tm, w), lambda i: (i, 0))
    return _pc(body, name="s5_post", grid=(r // tm,), in_specs=[row, row, pl.BlockSpec((1, w), lambda i: (0, 0)), row],
               out_specs=row, out_shape=jax.ShapeDtypeStruct((r, w), BF16), vmem=tm * w * 16)(y1, glu_pre, glu_b, z)


def _s5_post_bwd(dya, y1, glu_pre, glu_b, z):
    r, w = y1.shape
    tm = _pick(r, (256, 128))

    def body(dy_ref, y_ref, p_ref, b_ref, z_ref, dz_ref, dp_ref, dg_ref, db_ref):
        @pl.when(pl.program_id(0) == 0)
        def _():
            db_ref[...] = jnp.zeros_like(db_ref)

        g = _gelu_and_grad(y_ref[...])[0]
        s = jax.nn.sigmoid(p_ref[...] + b_ref[...])
        zv = z_ref[...]
        dy = dy_ref[...]
        do = dy * _silu(zv)
        dz_ref[...] = _bf(dy * g * s * _dsilu(zv))
        dp = do * g * s * (1.0 - s)
        dp_ref[...] = _bf(dp)
        db_ref[...] += jnp.sum(dp, axis=0, keepdims=True)
        dg_ref[...] = do * s

    row = pl.BlockSpec((tm, w), lambda i: (i, 0))
    vec = pl.BlockSpec((1, w), lambda i: (0, 0))
    return _pc(body, name="s5_post_bwd", grid=(r // tm,), in_specs=[row, row, row, vec, row],
               out_specs=[row, row, row, vec],
               out_shape=[jax.ShapeDtypeStruct((r, w), BF16), jax.ShapeDtypeStruct((r, w), BF16),
                          jax.ShapeDtypeStruct((r, w), F32), jax.ShapeDtypeStruct((1, w), F32)],
               vmem=tm * w * 24)(dya, y1, glu_pre, glu_b, z)


def _s5_bwd(dg, y1, u, xr, xi, wbr, wbi, qr, qi, sr, si, wcr, wci, d, bsz, nc):
    r = u.shape[0]
    sp = _s5_scan_specs(bsz, nc, True)

    def body(dg_ref, y1_ref, u_ref, xr_ref, xi_ref, xpr_ref, xpi_ref, wbr_ref, wbi_ref, qr_ref, qi_ref, sr_ref, si_ref,
             wcr_ref, wci_ref, d_ref, du_ref, dd_ref, dwcr_ref, dwci_ref, dwbr_ref, dwbi_ref, dar_ref, dai_ref,
             cr_s, ci_s):
        b, c = pl.program_id(1), pl.program_id(2)

        @pl.when((b == 0) & (c == 0))
        def _():
            for ref in (dd_ref, dwcr_ref, dwci_ref, dwbr_ref, dwbi_ref, dar_ref, dai_ref):
                ref[...] = jnp.zeros_like(ref)

        @pl.when(c == 0)
        def _():
            cr_s[...] = jnp.zeros_like(cr_s)
            ci_s[...] = jnp.zeros_like(ci_s)

        uv = u_ref[...]
        ub = _bf(uv)
        dy = dg_ref[...] * _gelu_and_grad(y1_ref[...])[1]
        dd_ref[...] += jnp.sum(dy * uv, axis=0, keepdims=True)
        dyb = _bf(dy)
        xr, xi = xr_ref[...], xi_ref[...]
        dwcr_ref[0] += _dot(_bf(xr), dyb, TN)
        dwci_ref[0] -= _dot(_bf(xi), dyb, TN)
        lr, li = _dot(dyb, wcr_ref[0], NT), -_dot(dyb, wci_ref[0], NT)
        row = lax.broadcasted_iota(jnp.int32, (CHUNK, S5_LANES), 0)
        for k in range(7):
            s = 1 << k
            ar, ai = sr_ref[k:k + 1, :], si_ref[k:k + 1, :]
            hr = jnp.where(row < CHUNK - s, pltpu.roll(lr, CHUNK - s, 0), 0.0)
            hi = jnp.where(row < CHUNK - s, pltpu.roll(li, CHUNK - s, 0), 0.0)
            lr, li = lr + (ar * hr + ai * hi), li + (ar * hi - ai * hr)
        cr, ci = cr_s[...], ci_s[...]
        tr, ti = qr_ref[...], qi_ref[...]
        lr, li = lr + (tr * cr + ti * ci), li + (tr * ci - ti * cr)
        cr_s[...] = lr[0:1, :]
        ci_s[...] = li[0:1, :]
        lrb, lib = _bf(lr), _bf(li)
        du_ref[...] = _bf(_dot(lrb, wbr_ref[0], NT) + _dot(lib, wbi_ref[0], NT) + dy * d_ref[...])
        dwbr_ref[0] += _dot(ub, lrb, TN)
        dwbi_ref[0] += _dot(ub, lib, TN)
        first = c == nc - 1
        pr0 = jnp.where(first, 0.0, xpr_ref[7:8, :])
        pi0 = jnp.where(first, 0.0, xpi_ref[7:8, :])
        xpr = jnp.where(row == 0, pr0, pltpu.roll(xr, 1, 0))
        xpi = jnp.where(row == 0, pi0, pltpu.roll(xi, 1, 0))
        dar_ref[...] += jnp.sum(lr * xpr + li * xpi, axis=0, keepdims=True)
        dai_ref[...] += jnp.sum(li * xpr - lr * xpi, axis=0, keepdims=True)

    st = jax.ShapeDtypeStruct
    return _pc(body, name="s5_bwd", grid=(S5_GB, bsz, nc),
               in_specs=[sp["u"], sp["u"], sp["u"], sp["x"], sp["x"], sp["xprev"], sp["xprev"], sp["wb"], sp["wb"],
                         sp["tab"], sp["tab"], sp["step"], sp["step"], sp["wc"], sp["wc"], sp["d"]],
               out_specs=[sp["u"], sp["d"], sp["wc"], sp["wc"], sp["wb"], sp["wb"], sp["lane"], sp["lane"]],
               out_shape=[st((r, S5_WIDTH), BF16), st((1, S5_WIDTH), F32),
                          st((S5_GB, S5_LANES, CHUNK), F32), st((S5_GB, S5_LANES, CHUNK), F32),
                          st((S5_GB, CHUNK, S5_LANES), F32), st((S5_GB, CHUNK, S5_LANES), F32),
                          st((1, S5_GROUPS * S5_STATE), F32), st((1, S5_GROUPS * S5_STATE), F32)],
               scratch=[pltpu.VMEM((1, S5_LANES), F32)] * 2, vmem=6 << 20,
               )(dg, y1, u, xr, xi, xr, xi, wbr, wbi, qr, qi, sr, si, wcr, wci, d)


def _s5_layer_fwd(u, prm, glu_w, bsz, nc):
    xr, xi, y1, g = _s5_fwd(u, prm["wbr"], prm["wbi"], prm["pr"], prm["pi"], prm["sr"], prm["si"], prm["wcr"],
                            prm["wci"], prm["d"], bsz, nc)
    glu_pre = _mm(g, glu_w(y1) if callable(glu_w) else glu_w, "NN", name="s5_glu")
    return dict(xr=xr, xi=xi, y1=y1, g=g, glu_pre=glu_pre)


def _s5_layer_bwd(dya, u, z, sv, prm, pvjp, glu_w, glu_b, bsz, nc):
    dz, dglu, dg_direct, dglu_b = _s5_post_bwd(dya, sv["y1"], sv["glu_pre"], glu_b, z)
    dg = _mm(dglu, glu_w, "NT", name="s5_dg", add=dg_direct)
    dglu_w = _mm(sv["g"], dglu, "TN", name="s5_dglu_w")
    du, dd, dwcr, dwci, dwbr, dwbi, dar, dai = _s5_bwd(
        dg, sv["y1"], u, sv["xr"], sv["xi"], prm["wbr"], prm["wbi"], prm["qr"], prm["qi"], prm["sr"], prm["si"],
        prm["wcr"], prm["wci"], prm["d"], bsz, nc)
    dbbr = jnp.swapaxes(_blockdiag_extract(dwbr, S5_GROUP_SIZE, S5_STATE), 1, 2)
    dbbi = jnp.swapaxes(_blockdiag_extract(dwbi, S5_GROUP_SIZE, S5_STATE), 1, 2)
    dlr, dli, dldt, dbr, dbi = pvjp((dar.reshape(S5_GROUPS, S5_STATE), dai.reshape(S5_GROUPS, S5_STATE), dbbr, dbbi))
    grads = dict(
        s5_lambda_re=dlr, s5_lambda_im=dli, s5_log_dt=dldt, s5_b_re=dbr, s5_b_im=dbi,
        s5_c_re=jnp.swapaxes(_blockdiag_extract(dwcr, S5_STATE, S5_GROUP_SIZE), 1, 2),
        s5_c_im=jnp.swapaxes(_blockdiag_extract(dwci, S5_STATE, S5_GROUP_SIZE), 1, 2),
        s5_d=dd, s5_glu_w=dglu_w, s5_glu_b=dglu_b)
    return du, dz, grads


def _s5_tables(lam_re, lam_im, log_dt, b_re, b_im, c_re, c_im, d):
    (ar, ai, bbr, bbi), vjp = jax.vjp(_s5_params, lam_re, lam_im, log_dt, b_re, b_im)
    pr, pi = _s5_power_table(lax.stop_gradient(ar), lax.stop_gradient(ai))
    steps = [(1 << k) - 1 for k in range(8)]
    prm = dict(
        wbr=_bf(_blockdiag(jnp.swapaxes(bbr, 1, 2), S5_GROUP_SIZE, S5_STATE)),
        wbi=_bf(_blockdiag(jnp.swapaxes(bbi, 1, 2), S5_GROUP_SIZE, S5_STATE)),
        wcr=_bf(_blockdiag(jnp.swapaxes(c_re, 1, 2), S5_STATE, S5_GROUP_SIZE)),
        wci=_bf(_blockdiag(jnp.swapaxes(c_im, 1, 2), S5_STATE, S5_GROUP_SIZE)),
        pr=pr, pi=pi, qr=pr[::-1], qi=pi[::-1],
        sr=jnp.concatenate([pr[i:i + 1] for i in steps], axis=0),
        si=jnp.concatenate([pi[i:i + 1] for i in steps], axis=0), d=d.reshape(1, S5_WIDTH))
    return prm, vjp


def _tile16(p8):
    return jnp.concatenate([p8] * (CHUNK // 8), axis=0)


def _shift_down(x, halo, s, row):
    return jnp.where(row >= s, pltpu.roll(x, s, 0), pltpu.roll(halo, s, 0))


def _shift_up(x, halo, s, row):
    return jnp.where(row < CHUNK - s, pltpu.roll(x, CHUNK - s, 0), pltpu.roll(halo, CHUNK - s, 0))


def _conv_specs(nc, tw):
    def chunk(b, c):
        return b * nc + c

    return dict(
        x=pl.BlockSpec((CHUNK, tw), lambda j, b, c: (chunk(b, c), j)),
        prev=pl.BlockSpec((8, tw), lambda j, b, c: (jnp.maximum(chunk(b, c) * (CHUNK // 8) - 1, 0), j)),
        nxt=pl.BlockSpec((8, tw), lambda j, b, c: ((b * nc + jnp.minimum(c + 1, nc - 1)) * (CHUNK // 8), j)),
        w=pl.BlockSpec((ML_CONV, tw), lambda j, b, c: (0, j)),
        vec=pl.BlockSpec((1, tw), lambda j, b, c: (0, j)),
    )


def _conv_fwd(x, w, bias, bsz, nc, *, name):
    r, wd = x.shape
    tw = _pick(wd, (512, 384, 256, 128))
    sp = _conv_specs(nc, tw)

    def body(x_ref, p_ref, w_ref, b_ref, o_ref):
        c = pl.program_id(2)
        xv = x_ref[...]
        row = lax.broadcasted_iota(jnp.int32, xv.shape, 0)
        halo = jnp.where(c == 0, 0.0, _tile16(p_ref[...]))
        acc = b_ref[...] + w_ref[3:4, :] * xv
        for s in (1, 2, 3):
            acc = acc + w_ref[3 - s:4 - s, :] * _shift_down(xv, halo, s, row)
        o_ref[...] = acc

    return _pc(body, name=name, grid=(wd // tw, bsz, nc), in_specs=[sp["x"], sp["prev"], sp["w"], sp["vec"]],
               out_specs=sp["x"], out_shape=jax.ShapeDtypeStruct((r, wd), F32), vmem=CHUNK * tw * 16,
               )(x, x, w, bias.reshape(1, wd))


def _conv_bwd(dpre, x, w, bsz, nc, *, name, add=None):
    r, wd = x.shape
    tw = _pick(wd, (512, 384, 256, 128))
    sp = _conv_specs(nc, tw)

    def body(*refs):
        d_ref, n_ref, x_ref, p_ref, w_ref = refs[:5]
        add_ref = refs[5] if add is not None else None
        dx_ref, dw_ref, db_ref = refs[-3:]
        b, c = pl.program_id(1), pl.program_id(2)

        @pl.when((b == 0) & (c == 0))
        def _():
            dw_ref[...] = jnp.zeros_like(dw_ref)
            db_ref[...] = jnp.zeros_like(db_ref)

        dv, xv = d_ref[...], x_ref[...]
        row = lax.broadcasted_iota(jnp.int32, xv.shape, 0)
        dhalo = jnp.where(c == nc - 1, 0.0, _tile16(n_ref[...]))
        xhalo = jnp.where(c == 0, 0.0, _tile16(p_ref[...]))
        dx = w_ref[3:4, :] * dv
        for s in (1, 2, 3):
            dx = dx + w_ref[3 - s:4 - s, :] * _shift_up(dv, dhalo, s, row)
        if add_ref is not None:
            dx = dx + add_ref[...]
        dx_ref[...] = _bf(dx)
        db_ref[...] += jnp.sum(dv, axis=0, keepdims=True)
        dw_ref[3:4, :] += jnp.sum(dv * xv, axis=0, keepdims=True)
        for s in (1, 2, 3):
            dw_ref[3 - s:4 - s, :] += jnp.sum(dv * _shift_down(xv, xhalo, s, row), axis=0, keepdims=True)

    ins = [dpre, dpre, x, x, w] + ([add] if add is not None else [])
    specs = [sp["x"], sp["nxt"], sp["x"], sp["prev"], sp["w"]] + ([sp["x"]] if add is not None else [])
    return _pc(body, name=name, grid=(wd // tw, bsz, nc), in_specs=specs, out_specs=[sp["x"], sp["w"], sp["vec"]],
               out_shape=[jax.ShapeDtypeStruct((r, wd), BF16), jax.ShapeDtypeStruct((ML_CONV, wd), F32),
                          jax.ShapeDtypeStruct((1, wd), F32)], vmem=CHUNK * tw * 24)(*ins)


ML_SCALE = ML_DH ** -0.5


def _headwise_expand(w):
    nb = ML_DH // QKV_BLOCK
    w = w.reshape(ML_HEADS, nb, QKV_BLOCK, QKV_BLOCK)
    eye = jnp.eye(nb, dtype=w.dtype)
    return jnp.einsum("hnio,nm->hnimo", w, eye).reshape(ML_HEADS, ML_DH, ML_DH)


def _headwise_extract(w):
    nb = ML_DH // QKV_BLOCK
    w = w.reshape(ML_HEADS, nb, QKV_BLOCK, nb, QKV_BLOCK)
    return jnp.einsum("hnimo,nm->hnio", w, jnp.eye(nb, dtype=w.dtype)).reshape(ML_HEADS * nb, QKV_BLOCK, QKV_BLOCK)


def _ml_pre(pre, x, wq, wk, wv, wgq, wgk, wgv, bsz, nc):
    r = x.shape[0]
    hrow = pl.BlockSpec((CHUNK, ML_DH), lambda b, c, h: (b * nc + c, h))
    wexp = pl.BlockSpec((1, ML_DH, ML_DH), lambda b, c, h: (h, 0, 0))
    wg = pl.BlockSpec((ML_DH, CHUNK), lambda b, c, h: (h, 0))
    gspec = pl.BlockSpec((CHUNK, CHUNK), lambda b, c, h: (b * nc + c, 0))

    def body(pre_ref, x_ref, wq_ref, wk_ref, wv_ref, gq_ref, gk_ref, gv_ref, q_ref, qs_ref, k_ref, v_ref, gt_ref):
        @pl.when(pl.program_id(2) == 0)
        def _():
            gt_ref[...] = jnp.zeros_like(gt_ref)

        xcb = _bf(_silu(pre_ref[...]))
        q = _dot(xcb, wq_ref[0])
        k = _dot(xcb, wk_ref[0])
        v = _dot(_bf(x_ref[...]), wv_ref[0])
        qb, kb, vb = _bf(q), _bf(k), _bf(v)
        q_ref[...] = qb
        qs_ref[...] = _bf(q * ML_SCALE)
        k_ref[...] = kb
        v_ref[...] = vb
        gt_ref[...] += _dot(qb, gq_ref[...]) + _dot(kb, gk_ref[...]) + _dot(vb, gv_ref[...])

    o = jax.ShapeDtypeStruct((r, ML_WIDTH), BF16)
    return _pc(body, name="ml_pre", grid=(bsz, nc, ML_HEADS),
               in_specs=[hrow, hrow, wexp, wexp, wexp, wg, wg, wg], out_specs=[hrow, hrow, hrow, hrow, gspec],
               out_shape=[o, o, o, o, jax.ShapeDtypeStruct((r, CHUNK), F32)], vmem=4 << 20,
               )(pre, x, wq, wk, wv, wgq, wgk, wgv)


def _cumsum_rows(x, row, rev=False):
    for k in range(7):
        s = 1 << k
        if rev:
            x = x + jnp.where(row < CHUNK - s, pltpu.roll(x, CHUNK - s, 0), 0.0)
        else:
            x = x + jnp.where(row >= s, pltpu.roll(x, s, 0), 0.0)
    return x


def _log_sigmoid(x):
    return jnp.minimum(x, 0.0) - jnp.log(1.0 + jnp.exp(-jnp.abs(x)))


def _ml_core(gates, hd, first, m, qs, k, v, cmat, nvec):
    sq = (CHUNK, CHUNK)
    lane = lax.broadcasted_iota(jnp.int32, sq, 1)
    row = lax.broadcasted_iota(jnp.int32, sq, 0)
    igc = jnp.sum(jnp.where(lane == hd, gates, 0.0), axis=1, keepdims=True)
    fpc = jnp.sum(jnp.where(lane == hd + ML_HEADS, gates, 0.0), axis=1, keepdims=True)
    valid = jnp.logical_or(jnp.logical_not(first), row[:, :1] >= PAD_ROWS)
    igc = jnp.where(valid, igc, NEG)
    lfc = jnp.where(valid, _log_sigmoid(fpc), 0.0)
    bcb = _cumsum_rows(jnp.broadcast_to(lfc, sq), row)
    igb = jnp.broadcast_to(igc, sq)
    dm = jnp.where(lane <= row, bcb - (bcb - igb).T, NEG)
    bc = bcb[:, :1]
    inter = bc + m
    mt = jnp.maximum(inter, jnp.max(dm, axis=1, keepdims=True))
    wt = jnp.exp(dm - mt)
    wprev = jnp.exp(inter - mt)
    s0 = _dot(qs, k, NT)
    s = s0 * wt
    cb = _bf(cmat)
    qc = _dot(qs, cb)
    qf = qs.astype(F32)
    qn = jnp.sum(qf * nvec, axis=1, keepdims=True)
    num = _dot(_bf(s), v) + wprev * qc
    den = jnp.sum(s, axis=1, keepdims=True) + wprev * qn
    emt = jnp.exp(-mt)
    dd = jnp.maximum(jnp.abs(den), emt)
    blast = bcb[CHUNK - 1:CHUNK, :1]
    g = blast - bc + igc
    m_new = jnp.maximum(blast + m, jnp.max(g, axis=0, keepdims=True))
    decay = jnp.exp(blast + m - m_new)
    e = jnp.exp(g - m_new)
    kf = k.astype(F32)
    wk = e * kf
    return dict(lane=lane, row=row, fpc=fpc, valid=valid, wt=wt, wprev=wprev, s=s, cb=cb, qc=qc, qf=qf, qn=qn,
                num=num, den=den, emt=emt, dd=dd, m_new=m_new, decay=decay, e=e, kf=kf, wk=wk)


def _ml_headnorm(h):
    mu = jnp.mean(h, axis=1, keepdims=True)
    hc = h - mu
    rstd = lax.rsqrt(jnp.mean(hc * hc, axis=1, keepdims=True) + HEAD_NORM_EPS)
    return hc * rstd, rstd


def _ml_chunk_specs(nc, rev, head_major):
    def ix(a, b_, c):
        hd, b = (a, b_) if head_major else (b_, a)
        return hd, b, (nc - 1 - c) if rev else c

    def row(a, b_, c):
        hd, b, cc = ix(a, b_, c)
        return b * nc + cc, hd

    def st(a, b_, c):
        hd, b, cc = ix(a, b_, c)
        return (b * ML_HEADS + hd) * nc + cc

    return dict(
        hrow=pl.BlockSpec((CHUNK, ML_DH), row),
        gates=pl.BlockSpec((CHUNK, CHUNK), lambda a, b_, c: (row(a, b_, c)[0], 0)),
        bias=pl.BlockSpec((1, CHUNK), lambda a, b_, c: (0, 0)),
        hvec=pl.BlockSpec((1, ML_DH), lambda a, b_, c: (0, ix(a, b_, c)[0])),
        cs=pl.BlockSpec((1, ML_DH, ML_DH), lambda a, b_, c: (st(a, b_, c), 0, 0)),
        ns=pl.BlockSpec((1, 1, ML_DH), lambda a, b_, c: (st(a, b_, c), 0, 0)),
        ms=pl.BlockSpec((1, 1, CHUNK), lambda a, b_, c: (st(a, b_, c), 0, 0)),
        dgates=pl.BlockSpec((1, CHUNK, CHUNK), lambda a, b_, c: (ix(a, b_, c)[0], row(a, b_, c)[0], 0)),
    )


def _ml_chunk_fwd(qs, k, v, gates, b_gate, pre, z, nw, sk, bsz, nc):
    r = qs.shape[0]
    sp = _ml_chunk_specs(nc, False, False)

    def body(qs_ref, k_ref, v_ref, gt_ref, bg_ref, pre_ref, z_ref, nw_ref, sk_ref,
             h_ref, yb_ref, cs_ref, ns_ref, ms_ref, c_s, n_s, m_s):
        hd, c = pl.program_id(1), pl.program_id(2)

        @pl.when(c == 0)
        def _():
            c_s[...] = jnp.zeros_like(c_s)
            n_s[...] = jnp.zeros_like(n_s)
            m_s[...] = jnp.zeros_like(m_s)

        cmat, nvec, m = c_s[...], n_s[...], m_s[...]
        cs_ref[0] = cmat
        ns_ref[0] = nvec
        ms_ref[0] = jnp.broadcast_to(m, (1, CHUNK))
        v_ = v_ref[...]
        co = _ml_core(gt_ref[...] + bg_ref[...], hd, c == 0, m, qs_ref[...], k_ref[...], v_, cmat, nvec)
        h = co["num"] / co["dd"]
        h_ref[...] = h
        hn, _ = _ml_headnorm(h)
        yb_ref[...] = _bf((hn * nw_ref[...] + sk_ref[...] * _silu(pre_ref[...])) * _silu(z_ref[...]))
        c_s[...] = co["decay"] * cmat + _dot(_bf(co["wk"]), v_, TN)
        n_s[...] = co["decay"] * nvec + jnp.sum(co["wk"], axis=0, keepdims=True)
        m_s[...] = co["m_new"]

    nst = bsz * ML_HEADS * nc
    return _pc(body, name="ml_chunk_fwd", grid=(bsz, ML_HEADS, nc),
               in_specs=[sp["hrow"]] * 3 + [sp["gates"], sp["bias"], sp["hrow"], sp["hrow"], sp["hvec"], sp["hvec"]],
               out_specs=[sp["hrow"], sp["hrow"], sp["cs"], sp["ns"], sp["ms"]],
               out_shape=[jax.ShapeDtypeStruct((r, ML_WIDTH), F32), jax.ShapeDtypeStruct((r, ML_WIDTH), BF16),
                          jax.ShapeDtypeStruct((nst, ML_DH, ML_DH), F32), jax.ShapeDtypeStruct((nst, 1, ML_DH), F32),
                          jax.ShapeDtypeStruct((nst, 1, CHUNK), F32)],
               scratch=[pltpu.VMEM((ML_DH, ML_DH), F32), pltpu.VMEM((1, ML_DH), F32), pltpu.VMEM((1, 1), F32)],
               vmem=6 << 20)(qs, k, v, gates, b_gate, pre, z, nw, sk)


def _ml_chunk_bwd(dyb, qs, k, v, gates, b_gate, pre, z, nw, sk, h, cs, ns, ms, bsz, nc):
    r = qs.shape[0]
    sp = _ml_chunk_specs(nc, True, True)

    def body(dy_ref, qs_ref, k_ref, v_ref, gt_ref, bg_ref, pre_ref, z_ref, nw_ref, sk_ref, h_ref, cs_ref, ns_ref,
             ms_ref, dq_ref, dk_ref, dv_ref, dz_ref, dxc_ref, dgt_ref, dnw_ref, dsk_ref, dc_s, dn_s):
        hd, b, c = pl.program_id(0), pl.program_id(1), pl.program_id(2)

        @pl.when((b == 0) & (c == 0))
        def _():
            dnw_ref[...] = jnp.zeros_like(dnw_ref)
            dsk_ref[...] = jnp.zeros_like(dsk_ref)

        @pl.when(c == 0)
        def _():
            dc_s[...] = jnp.zeros_like(dc_s)
            dn_s[...] = jnp.zeros_like(dn_s)

        qs, k, v = qs_ref[...], k_ref[...], v_ref[...]
        cmat, nvec, m = cs_ref[0], ns_ref[0], ms_ref[0][:, :1]
        co = _ml_core(gt_ref[...] + bg_ref[...], hd, c == nc - 1, m, qs, k, v, cmat, nvec)
        lane, row = co["lane"], co["row"]
        wt, wprev, s, cb, qf = co["wt"], co["wprev"], co["s"], co["cb"], co["qf"]
        h = h_ref[...]
        hn, rstd = _ml_headnorm(h)
        xc = _silu(pre_ref[...])
        zv = z_ref[...]
        nw, sk = nw_ref[...], sk_ref[...]
        dy = dy_ref[...]
        dz_ref[...] = _bf(dy * (hn * nw + sk * xc) * _dsilu(zv))
        do = dy * _silu(zv)
        dsk_ref[...] += jnp.sum(do * xc, axis=0, keepdims=True)
        dnw_ref[...] += jnp.sum(do * hn, axis=0, keepdims=True)
        dxc_ref[...] = do * sk
        dhn = do * nw
        dh = rstd * (dhn - jnp.mean(dhn, axis=1, keepdims=True) - hn * jnp.mean(dhn * hn, axis=1, keepdims=True))
        rinv = 1.0 / co["dd"]
        dnum = dh * rinv
        ddd = -jnp.sum(dh * h, axis=1, keepdims=True) * rinv
        den = co["den"]
        dden = jnp.where(jnp.abs(den) >= co["emt"], ddd * jnp.sign(den), 0.0)
        dnb = _bf(dnum)
        ds = _dot(dnb, v, NT) + dden
        dv = _dot(_bf(s), dnb, TN)
        dnw_ = _bf(dnum * wprev)
        dwn = dden * wprev
        dqs = _dot(dnw_, cb, NT) + dwn * nvec
        dc_out = _dot(qs, dnw_, TN)
        dn_out = jnp.sum(dwn * qf, axis=0, keepdims=True)
        dwprev = jnp.sum(dnum * co["qc"], axis=1, keepdims=True) + dden * co["qn"]
        ds0 = _bf(ds * wt)
        ddm = ds * s
        dqs = dqs + _dot(ds0, k)
        dk = _dot(ds0, qs, TN)
        colc = jnp.sum(ddm.T, axis=1, keepdims=True)
        dbc = dwprev * wprev + jnp.sum(ddm, axis=1, keepdims=True) - colc
        dig = colc
        dcn, dnn = dc_s[...], dn_s[...]
        dcb = _bf(dcn)
        decay, e, kf, wk = co["decay"], co["e"], co["kf"], co["wk"]
        ddecay = (jnp.sum(jnp.sum(dcn * cmat, axis=1, keepdims=True), axis=0, keepdims=True)
                  + jnp.sum(dnn * nvec, axis=1, keepdims=True))
        dwk = _dot(v, dcb, NT) + dnn
        dv = dv + _dot(_bf(wk), dcb)
        dk = dk + e * dwk
        dg = jnp.sum(dwk * kf, axis=1, keepdims=True) * e
        dblast = ddecay * decay + jnp.sum(dg, axis=0, keepdims=True)
        dbc = dbc - dg + jnp.where(row[:, :1] == CHUNK - 1, dblast, 0.0)
        dig = dig + dg
        dc_s[...] = decay * dcn + dc_out
        dn_s[...] = decay * dnn + dn_out
        dlf = _cumsum_rows(jnp.broadcast_to(dbc, (CHUNK, CHUNK)), row, rev=True)[:, :1]
        dfp = dlf * (1.0 - jax.nn.sigmoid(co["fpc"]))
        dig = jnp.where(co["valid"], dig, 0.0)
        dfp = jnp.where(co["valid"], dfp, 0.0)
        dgt_ref[0] = jnp.where(lane == hd, dig, 0.0) + jnp.where(lane == hd + ML_HEADS, dfp, 0.0)
        dq_ref[...] = _bf(dqs * ML_SCALE)
        dk_ref[...] = _bf(dk)
        dv_ref[...] = _bf(dv)

    ob = jax.ShapeDtypeStruct((r, ML_WIDTH), BF16)
    return _pc(body, name="ml_chunk_bwd", grid=(ML_HEADS, bsz, nc),
               in_specs=[sp["hrow"]] * 4 + [sp["gates"], sp["bias"], sp["hrow"], sp["hrow"], sp["hvec"], sp["hvec"],
                                            sp["hrow"], sp["cs"], sp["ns"], sp["ms"]],
               out_specs=[sp["hrow"]] * 5 + [sp["dgates"], sp["hvec"], sp["hvec"]],
               out_shape=[ob, ob, ob, ob, jax.ShapeDtypeStruct((r, ML_WIDTH), F32),
                          jax.ShapeDtypeStruct((ML_HEADS, r, CHUNK), F32),
                          jax.ShapeDtypeStruct((1, ML_WIDTH), F32), jax.ShapeDtypeStruct((1, ML_WIDTH), F32)],
               scratch=[pltpu.VMEM((ML_DH, ML_DH), F32), pltpu.VMEM((1, ML_DH), F32)], vmem=8 << 20,
               )(dyb, qs, k, v, gates, b_gate, pre, z, nw, sk, h, cs, ns, ms)


def _ml_pre_bwd(dq, dk, dv, dgates, dxc_skip, pre, x, q, k, v, wq, wk, wv, wgq, wgk, wgv, bsz, nc):
    r = x.shape[0]
    hrow = pl.BlockSpec((CHUNK, ML_DH), lambda h, b, c: (b * nc + c, h))
    wexp = pl.BlockSpec((1, ML_DH, ML_DH), lambda h, b, c: (h, 0, 0))
    wg = pl.BlockSpec((ML_DH, CHUNK), lambda h, b, c: (h, 0))
    dgs = pl.BlockSpec((ML_HEADS, CHUNK, CHUNK), lambda h, b, c: (0, b * nc + c, 0))
    bgs = pl.BlockSpec((1, 1, CHUNK), lambda h, b, c: (h, 0, 0))

    def body(dq_ref, dk_ref, dv_ref, dg_ref, dxs_ref, pre_ref, x_ref, q_ref, k_ref, v_ref, wq_ref, wk_ref, wv_ref,
             gq_ref, gk_ref, gv_ref, dpre_ref, dxv_ref, dwq_ref, dwk_ref, dwv_ref, dgq_ref, dgk_ref, dgv_ref, dbg_ref):
        b, c = pl.program_id(1), pl.program_id(2)

        @pl.when((b == 0) & (c == 0))
        def _():
            for ref in (dwq_ref, dwk_ref, dwv_ref, dgq_ref, dgk_ref, dgv_ref, dbg_ref):
                ref[...] = jnp.zeros_like(ref)

        dgt = dg_ref[0]
        for j in range(1, ML_HEADS):
            dgt = dgt + dg_ref[j]
        dbg_ref[0] += jnp.sum(dgt, axis=0, keepdims=True)
        dgb = _bf(dgt)
        dqt = _bf(dq_ref[...].astype(F32) + _dot(dgb, gq_ref[...], NT))
        dkt = _bf(dk_ref[...].astype(F32) + _dot(dgb, gk_ref[...], NT))
        dvt = _bf(dv_ref[...].astype(F32) + _dot(dgb, gv_ref[...], NT))
        dgq_ref[...] += _dot(q_ref[...], dgb, TN)
        dgk_ref[...] += _dot(k_ref[...], dgb, TN)
        dgv_ref[...] += _dot(v_ref[...], dgb, TN)
        prev = pre_ref[...]
        xcb = _bf(_silu(prev))
        xb = _bf(x_ref[...])
        dwq_ref[0] += _dot(xcb, dqt, TN)
        dwk_ref[0] += _dot(xcb, dkt, TN)
        dwv_ref[0] += _dot(xb, dvt, TN)
        dxc = _dot(dqt, wq_ref[0], NT) + _dot(dkt, wk_ref[0], NT) + dxs_ref[...]
        dpre_ref[...] = dxc * _dsilu(prev)
        dxv_ref[...] = _dot(dvt, wv_ref[0], NT)

    f = jax.ShapeDtypeStruct((r, ML_WIDTH), F32)
    we = jax.ShapeDtypeStruct((ML_HEADS, ML_DH, ML_DH), F32)
    wgs = jax.ShapeDtypeStruct((ML_WIDTH, CHUNK), F32)
    return _pc(body, name="ml_pre_bwd", grid=(ML_HEADS, bsz, nc),
               in_specs=[hrow, hrow, hrow, dgs, hrow, hrow, hrow, hrow, hrow, hrow, wexp, wexp, wexp, wg, wg, wg],
               out_specs=[hrow, hrow, wexp, wexp, wexp, wg, wg, wg, bgs],
               out_shape=[f, f, we, we, we, wgs, wgs, wgs, jax.ShapeDtypeStruct((ML_HEADS, 1, CHUNK), F32)],
               vmem=8 << 20)(dq, dk, dv, dgates, dxc_skip, pre, x, q, k, v, wq, wk, wv, wgq, wgk, wgv)


def _pad_lanes(w):
    return jnp.pad(w, ((0, 0), (0, CHUNK - w.shape[1])))


def _ml_weights(conv_w, conv_b, wq, wk, wv, w_gate, b_gate, norm_w, skip):
    return dict(
        conv_w=conv_w, conv_b=conv_b,
        wq=_bf(_headwise_expand(wq)), wk=_bf(_headwise_expand(wk)), wv=_bf(_headwise_expand(wv)),
        wgq=_bf(_pad_lanes(w_gate[:ML_WIDTH])), wgk=_bf(_pad_lanes(w_gate[ML_WIDTH:2 * ML_WIDTH])),
        wgv=_bf(_pad_lanes(w_gate[2 * ML_WIDTH:])), b_gate=_pad_lanes(b_gate.reshape(1, -1)),
        norm=norm_w.reshape(1, ML_WIDTH), skip=skip.reshape(1, ML_WIDTH))


def _ml_layer_fwd(x, z, w, bsz, nc):
    pre = _conv_fwd(x, w["conv_w"], w["conv_b"], bsz, nc, name="ml_conv")
    q, qs, k, v, gates = _ml_pre(pre, x, w["wq"], w["wk"], w["wv"], w["wgq"], w["wgk"], w["wgv"], bsz, nc)
    h, yb, cs, ns, ms = _ml_chunk_fwd(qs, k, v, gates, w["b_gate"], pre, z, w["norm"], w["skip"], bsz, nc)
    return yb, dict(pre=pre, q=q, qs=qs, k=k, v=v, gates=gates, h=h, cs=cs, ns=ns, ms=ms)


def _ml_layer_bwd(dyb, x, z, sv, w, bsz, nc):
    dq, dk, dv, dz, dxc, dgates, dnw, dsk = _ml_chunk_bwd(
        dyb, sv["qs"], sv["k"], sv["v"], sv["gates"], w["b_gate"], sv["pre"], z, w["norm"], w["skip"], sv["h"],
        sv["cs"], sv["ns"], sv["ms"], bsz, nc)
    dpre, dxv, dwq, dwk, dwv, dgq, dgk, dgv, dbg = _ml_pre_bwd(
        dq, dk, dv, dgates, dxc, sv["pre"], x, sv["q"], sv["k"], sv["v"], w["wq"], w["wk"], w["wv"], w["wgq"],
        w["wgk"], w["wgv"], bsz, nc)
    dx, dcw, dcb = _conv_bwd(dpre, x, w["conv_w"], bsz, nc, name="ml_conv_bwd", add=dxv)
    ng = 2 * ML_HEADS
    grads = dict(
        ml_conv_w=dcw, ml_conv_b=dcb, ml_wq=_headwise_extract(dwq), ml_wk=_headwise_extract(dwk),
        ml_wv=_headwise_extract(dwv), ml_w_gate=jnp.concatenate([dgq[:, :ng], dgk[:, :ng], dgv[:, :ng]], axis=0),
        ml_b_gate=dbg[0][:, :ng], ml_norm=dnw, ml_skip=dsk)
    return dx, dz, grads


HI = lax.Precision.HIGHEST


def _softplus(x):
    return jnp.maximum(x, 0.0) + jnp.log(1.0 + jnp.exp(-jnp.abs(x)))


def _lane_cumsum(x, lane, rev=False):
    for k in range(7):
        s = 1 << k
        if rev:
            x = x + jnp.where(lane < CHUNK - s, pltpu.roll(x, CHUNK - s, 1), 0.0)
        else:
            x = x + jnp.where(lane >= s, pltpu.roll(x, s, 1), 0.0)
    return x


def _head_sum_matrix():
    r = lax.broadcasted_iota(jnp.int32, (SSD_HPG, SSD_GW), 0)
    l = lax.broadcasted_iota(jnp.int32, (SSD_HPG, SSD_GW), 1)
    return jnp.where(l // SSD_P == r, 1.0, 0.0).astype(F32)


def _ssd_core(xs, bm, cm, dt_raw, dt_bias, a_log, first):
    sq = (CHUNK, CHUNK)
    lane8 = lax.broadcasted_iota(jnp.int32, (SSD_HPG, CHUNK), 1)
    lane = lax.broadcasted_iota(jnp.int32, sq, 1)
    row = lax.broadcasted_iota(jnp.int32, sq, 0)
    low = lane < SSD_P
    valid = jnp.logical_or(jnp.logical_not(first), lane8 >= PAD_ROWS)
    pre = dt_raw + dt_bias
    dt = jnp.where(valid, _softplus(pre), 0.0)
    a = -jnp.exp(a_log)
    cum = _lane_cumsum(dt * a, lane8)
    cb = _dot(_bf(cm), _bf(bm), NT)
    heads = []
    for r in range(SSD_HPG):
        rowb = jnp.broadcast_to(cum[r:r + 1, :], sq)
        colb = rowb.T
        seg = jnp.exp(jnp.where(lane <= row, colb - rowb, NEG))
        dtrow = jnp.broadcast_to(dt[r:r + 1, :], sq)
        lastb = colb[CHUNK - 1:CHUNK, :]
        heads.append(dict(seg=seg, dtrow=dtrow, w=cb * seg * dtrow, ecol=jnp.exp(colb),
                          dec=jnp.exp(lastb - colb) * dtrow.T, elast=jnp.exp(lastb)))

    def pairs(key):
        return jnp.concatenate([jnp.where(low[:heads[0][key].shape[0]], heads[2 * j][key], heads[2 * j + 1][key])
                                for j in range(SSD_HPG // 2)], axis=1)

    return dict(lane8=lane8, low=low, valid=valid, pre=pre, dt=dt, a=a, cum=cum, cb=cb, heads=heads,
                expc=pairs("ecol"), dec=pairs("dec"), elast=pairs("elast"))


def _ssd_specs(nc, rev, group_major):
    def ix(a, b_, c):
        g, b = (a, b_) if group_major else (b_, a)
        return g, b, (nc - 1 - c) if rev else c

    def row(a, b_, c):
        g, b, cc = ix(a, b_, c)
        return b * nc + cc, g

    return dict(
        wide=pl.BlockSpec((CHUNK, SSD_GW), row),
        narrow=pl.BlockSpec((CHUNK, SSD_N), row),
        dtT=pl.BlockSpec((SSD_HPG, CHUNK), lambda a, b_, c: (ix(a, b_, c)[0], row(a, b_, c)[0])),
        hcol=pl.BlockSpec((SSD_HPG, 1), lambda a, b_, c: (ix(a, b_, c)[0], 0)),
        hacc=pl.BlockSpec((SSD_HPG, CHUNK), lambda a, b_, c: (ix(a, b_, c)[0], 0)),
        gvec=pl.BlockSpec((1, SSD_GW), lambda a, b_, c: (0, ix(a, b_, c)[0])),
        state=pl.BlockSpec((1, SSD_N, SSD_GW),
                           lambda a, b_, c: ((ix(a, b_, c)[1] * SSD_GROUPS + ix(a, b_, c)[0]) * nc + ix(a, b_, c)[2], 0, 0)),
    )


def _ssd_chunk_fwd(xs_pre, bm_pre, cm_pre, dt_raw, dt_bias, a_log, d_exp, z, gnorm, bsz, nc):
    r = xs_pre.shape[0]
    sp = _ssd_specs(nc, False, False)

    def body(xs_ref, bm_ref, cm_ref, dt_ref, db_ref, al_ref, d_ref, z_ref, gn_ref, y_ref, yn_ref, st_ref, st_s):
        c = pl.program_id(2)

        @pl.when(c == 0)
        def _():
            st_s[...] = jnp.zeros_like(st_s)

        state = st_s[...]
        st_ref[0] = state
        xs, bm, cm = _silu(xs_ref[...]), _silu(bm_ref[...]), _silu(cm_ref[...])
        co = _ssd_core(xs, bm, cm, dt_ref[...], db_ref[...], al_ref[...], c == 0)
        low, hd = co["low"], co["heads"]
        ys = []
        for j in range(SSD_HPG // 2):
            xp = xs[:, j * CHUNK:(j + 1) * CHUNK]
            lhs = jnp.concatenate([hd[2 * j]["w"], hd[2 * j + 1]["w"]], axis=1)
            rhs = jnp.concatenate([jnp.where(low, xp, 0.0), jnp.where(low, 0.0, xp)], axis=0)
            ys.append(_dot(_bf(lhs), _bf(rhs)))
        cmb = _bf(cm)
        y = jnp.concatenate(ys, axis=1) + co["expc"] * _dot(cmb, _bf(state)) + d_ref[...] * xs
        y_ref[...] = y
        yg = y * _silu(z_ref[...])
        rstd = lax.rsqrt(jnp.mean(yg * yg, axis=1, keepdims=True) + NORM_EPS)
        yn_ref[...] = _bf(yg * rstd * gn_ref[...])
        st_s[...] = co["elast"] * state + _dot(_bf(bm), _bf(xs * co["dec"]), TN)

    nst = bsz * SSD_GROUPS * nc
    return _pc(body, name="ssd_chunk_fwd", grid=(bsz, SSD_GROUPS, nc),
               in_specs=[sp["wide"], sp["narrow"], sp["narrow"], sp["dtT"], sp["hcol"], sp["hcol"], sp["gvec"],
                         sp["wide"], sp["gvec"]],
               out_specs=[sp["wide"], sp["wide"], sp["state"]],
               out_shape=[jax.ShapeDtypeStruct((r, SSD_INNER), F32), jax.ShapeDtypeStruct((r, SSD_INNER), BF16),
                          jax.ShapeDtypeStruct((nst, SSD_N, SSD_GW), F32)],
               scratch=[pltpu.VMEM((SSD_N, SSD_GW), F32)], vmem=6 << 20,
               )(xs_pre, bm_pre, cm_pre, dt_raw, dt_bias, a_log, d_exp, z, gnorm)


def _ssd_chunk_bwd(dyn, xs_pre, bm_pre, cm_pre, dt_raw, dt_bias, a_log, d_exp, z, gnorm, y, states, bsz, nc):
    r = xs_pre.shape[0]
    sp = _ssd_specs(nc, True, True)

    def body(dyn_ref, xs_ref, bm_ref, cm_ref, dt_ref, db_ref, al_ref, d_ref, z_ref, gn_ref, y_ref, st_ref,
             dxs_ref, dbm_ref, dcm_ref, dz_ref, ddt_ref, dgn_ref, dd_ref, dbias_ref, dal_ref, ds_s):
        b, c = pl.program_id(1), pl.program_id(2)

        @pl.when((b == 0) & (c == 0))
        def _():
            for ref in (dgn_ref, dd_ref, dbias_ref, dal_ref):
                ref[...] = jnp.zeros_like(ref)

        @pl.when(c == 0)
        def _():
            ds_s[...] = jnp.zeros_like(ds_s)

        xs_p, bm_p, cm_p = xs_ref[...], bm_ref[...], cm_ref[...]
        xs, bm, cm = _silu(xs_p), _silu(bm_p), _silu(cm_p)
        state = st_ref[0]
        co = _ssd_core(xs, bm, cm, dt_ref[...], db_ref[...], al_ref[...], c == nc - 1)
        low, hd, lane8, cb = co["low"], co["heads"], co["lane8"], co["cb"]
        dt, a, cum = co["dt"], co["a"], co["cum"]
        sub8 = lax.broadcasted_iota(jnp.int32, (SSD_HPG, CHUNK), 0)
        eh = _head_sum_matrix()

        def head_rows(full):
            return lax.dot_general(eh, full, NT, precision=HI, preferred_element_type=F32)

        def head_col(vec):
            return jnp.sum(eh * vec, axis=1, keepdims=True)

        yv, zv, gn = y_ref[...], z_ref[...], gn_ref[...]
        sz = _silu(zv)
        yg = yv * sz
        rstd = lax.rsqrt(jnp.mean(yg * yg, axis=1, keepdims=True) + NORM_EPS)
        yh = yg * rstd
        dyn = dyn_ref[...]
        dgn_ref[...] += jnp.sum(dyn * yh, axis=0, keepdims=True)
        dyh = dyn * gn
        dyg = rstd * (dyh - yh * jnp.mean(dyh * yh, axis=1, keepdims=True))
        dz_ref[...] = _bf(dyg * yv * _dsilu(zv))
        dy = dyg * sz
        dxs = dy * d_ref[...]
        dd_ref[...] += head_col(jnp.sum(dy * xs, axis=0, keepdims=True))
        cmb, bmb, stb = _bf(cm), _bf(bm), _bf(state)
        ysv = _dot(cmb, stb)
        expc = co["expc"]
        dys = _bf(dy * expc)
        dcum = head_rows(dy * ysv * expc)
        dcm = _dot(dys, stb, NT)
        dstate_out = _dot(cmb, dys, TN)
        dcb = jnp.zeros((CHUNK, CHUNK), F32)
        ddt = jnp.zeros((SSD_HPG, CHUNK), F32)
        dxs_pairs = []
        for j in range(SSD_HPG // 2):
            sl = slice(j * CHUNK, (j + 1) * CHUNK)
            dyp, xp = dy[:, sl], _bf(xs[:, sl])
            lhs = _bf(jnp.concatenate([hd[2 * j]["w"], hd[2 * j + 1]["w"]], axis=1))
            both = _dot(lhs, _bf(dyp), TN)
            dxs_pairs.append(jnp.where(low, both[:CHUNK], both[CHUNK:]))
            for q, msk in ((2 * j, low), (2 * j + 1, jnp.logical_not(low))):
                h = hd[q]
                dw = _dot(_bf(jnp.where(msk, dyp, 0.0)), xp, NT)
                dcb = dcb + dw * h["seg"] * h["dtrow"]
                e_ = dw * h["w"]
                dcum_r = jnp.sum(e_.T, axis=0, keepdims=True) - jnp.sum(e_, axis=0, keepdims=True)
                ddt_r = jnp.sum(dw * cb * h["seg"], axis=0, keepdims=True)
                dcum = dcum + jnp.where(sub8 == q, dcum_r, 0.0)
                ddt = ddt + jnp.where(sub8 == q, ddt_r, 0.0)
        dxs = dxs + jnp.concatenate(dxs_pairs, axis=1)
        dcbb = _bf(dcb)
        dcm = dcm + _dot(dcbb, bmb)
        dbm = _dot(dcbb, cmb, TN)
        dsn = ds_s[...]
        dsb = _bf(dsn)
        dec = co["dec"]
        dbm = dbm + _dot(_bf(xs * dec), dsb, NT)
        dxd = _dot(bmb, dsb)
        dxs = dxs + dxd * dec
        ddec = head_rows(dxd * xs)
        last = cum[:, CHUNK - 1:CHUNK]
        erow = jnp.exp(last - cum)
        ddt = ddt + ddec * erow
        dla = ddec * erow * dt
        dlast = (jnp.sum(dla, axis=1, keepdims=True)
                 + head_col(jnp.sum(dsn * state, axis=0, keepdims=True)) * jnp.exp(last))
        dcum = dcum - dla + jnp.where(lane8 == CHUNK - 1, dlast, 0.0)
        ds_s[...] = co["elast"] * dsn + dstate_out
        dda = _lane_cumsum(dcum, lane8, rev=True)
        ddt = jnp.where(co["valid"], ddt + dda * a, 0.0)
        ddt_raw = ddt * jax.nn.sigmoid(co["pre"])
        ddt_ref[...] = ddt_raw
        dbias_ref[...] += jnp.sum(ddt_raw, axis=1, keepdims=True)
        dal_ref[...] += jnp.sum(dda * dt, axis=1, keepdims=True) * a
        dxs_ref[...] = dxs * _dsilu(xs_p)
        dbm_ref[...] = dbm * _dsilu(bm_p)
        dcm_ref[...] = dcm * _dsilu(cm_p)

    st = jax.ShapeDtypeStruct
    hacc = st((SSD_HEADS, CHUNK), F32)
    return _pc(body, name="ssd_chunk_bwd", grid=(SSD_GROUPS, bsz, nc),
               in_specs=[sp["wide"], sp["wide"], sp["narrow"], sp["narrow"], sp["dtT"], sp["hcol"], sp["hcol"],
                         sp["gvec"], sp["wide"], sp["gvec"], sp["wide"], sp["state"]],
               out_specs=[sp["wide"], sp["narrow"], sp["narrow"], sp["wide"], sp["dtT"], sp["gvec"], sp["hacc"],
                          sp["hacc"], sp["hacc"]],
               out_shape=[st((r, SSD_INNER), F32), st((r, SSD_GROUPS * SSD_N), F32), st((r, SSD_GROUPS * SSD_N), F32),
                          st((r, SSD_INNER), BF16), st((SSD_HEADS, r), F32), st((1, SSD_INNER), F32), hacc, hacc, hacc],
               scratch=[pltpu.VMEM((SSD_N, SSD_GW), F32)], vmem=10 << 20,
               )(dyn, xs_pre, bm_pre, cm_pre, dt_raw, dt_bias, a_log, d_exp, z, gnorm, y, states)


SSD_BC = SSD_GROUPS * SSD_N


def _ssd_weights(conv_w, conv_b, dt_bias, a_log, d, gnorm):
    cuts = (0, SSD_INNER, SSD_INNER + SSD_BC, SSD_INNER + 2 * SSD_BC)
    return dict(
        conv_w=[conv_w[:, cuts[i]:cuts[i + 1]] for i in range(3)],
        conv_b=[conv_b[cuts[i]:cuts[i + 1]] for i in range(3)],
        dt_bias=dt_bias.reshape(SSD_HEADS, 1), a_log=a_log.reshape(SSD_HEADS, 1),
        d_exp=jnp.repeat(d.reshape(SSD_HEADS), SSD_P).reshape(1, SSD_INNER), gnorm=gnorm.reshape(1, SSD_INNER))


def _ssd_layer_fwd(z, xs_in, bm_in, cm_in, dt_rows, w, bsz, nc):
    pres = [_conv_fwd(a, w["conv_w"][i], w["conv_b"][i], bsz, nc, name=f"ssd_conv{i}")
            for i, a in enumerate((xs_in, bm_in, cm_in))]
    dt_t = dt_rows[:, :SSD_HEADS].T
    y, yn, states = _ssd_chunk_fwd(pres[0], pres[1], pres[2], dt_t, w["dt_bias"], w["a_log"], w["d_exp"], z,
                                   w["gnorm"], bsz, nc)
    return yn, dict(pres=pres, dt_t=dt_t, y=y, states=states)


def _ssd_layer_bwd(dyn, z, xs_in, bm_in, cm_in, sv, w, bsz, nc):
    pres = sv["pres"]
    dxs_p, dbm_p, dcm_p, dz, ddt_t, dgn, dd, dbias, dal = _ssd_chunk_bwd(
        dyn, pres[0], pres[1], pres[2], sv["dt_t"], w["dt_bias"], w["a_log"], w["d_exp"], z, w["gnorm"], sv["y"],
        sv["states"], bsz, nc)
    outs = [_conv_bwd(dp, a, w["conv_w"][i], bsz, nc, name=f"ssd_conv_bwd{i}")
            for i, (dp, a) in enumerate(((dxs_p, xs_in), (dbm_p, bm_in), (dcm_p, cm_in)))]
    ddt = _bf(_pad_lanes(ddt_t.T))
    grads = dict(
        ssd_conv_w=jnp.concatenate([o[1] for o in outs], axis=1),
        ssd_conv_b=jnp.concatenate([o[2] for o in outs], axis=1),
        ssd_dt_bias=dbias[:, 0], ssd_a_log=dal[:, 0], ssd_d=dd[:, 0], ssd_gnorm=dgn)
    return dz, outs[0][0], outs[1][0], outs[2][0], ddt, grads


WNAMES = ("meta_tokens", "ab_norm", "ab_w_in", "s5_lambda_re", "s5_lambda_im", "s5_log_dt", "s5_b_re", "s5_b_im",
          "s5_c_re", "s5_c_im", "s5_d", "s5_glu_w", "s5_glu_b", "ml_conv_w", "ml_conv_b", "ml_wq", "ml_wk", "ml_wv",
          "ml_w_gate", "ml_b_gate", "ml_norm", "ml_skip", "ab_w_out", "ssd_norm", "ssd_w_in", "ssd_conv_w",
          "ssd_conv_b", "ssd_dt_bias", "ssd_a_log", "ssd_d", "ssd_gnorm", "ssd_w_out", "final_norm")
SHARD_AXIS = dict(meta_tokens=1, ab_w_in=2, s5_glu_w=1, ml_conv_w=2, ml_wq=1, ml_wk=1, ml_wv=1, ml_w_gate=1,
                  ab_w_out=1, ssd_norm=1, ssd_w_in=2, ssd_conv_w=2, ssd_conv_b=1, ssd_gnorm=1, ssd_w_out=1)
BIG = ("ab_w_in", "s5_glu_w", "ab_w_out", "ssd_w_in", "ssd_w_out")
SMALL = tuple(n for n in WNAMES if n in SHARD_AXIS and n not in BIG)
REPL = tuple(n for n in WNAMES if n not in SHARD_AXIS)
PACK_ALIGN = 8 * 128


def _pack(arrs):
    lead = arrs[0][1]
    parts = []
    for a, nlead in arrs:
        f = a.reshape(a.shape[:nlead] + (-1,))
        f = jnp.pad(f, [(0, 0)] * nlead + [(0, (-f.shape[-1]) % PACK_ALIGN)])
        parts.append(f.reshape(f.shape[:nlead] + (-1, 128)))
    return jnp.concatenate(parts, axis=lead)


def _unpack(p, shapes):
    out, r0 = [], 0
    lead = p.shape[:-2]
    for s in shapes:
        n = math.prod(s)
        rows = -(-n // PACK_ALIGN) * 8
        seg = p[..., r0:r0 + rows, :].reshape(lead + (rows * 128,))[..., :n]
        out.append(seg.reshape(lead + tuple(s)))
        r0 += rows
    return out


def _assemble(g, axis):
    m = jnp.moveaxis(g, 0, axis)
    return m.reshape(m.shape[:axis] + (m.shape[axis] * m.shape[axis + 1],) + m.shape[axis + 2:])


def _split(full, axis):
    s = full.shape
    m = full.reshape(s[:axis] + (N_DEV, s[axis] // N_DEV) + s[axis + 1:])
    return jnp.moveaxis(m, axis, 0)


def kernel(x, *rest):
    nw = len(WNAMES)
    w = dict(zip(WNAMES, rest[:nw]))
    loss_target = rest[nw]
    mom = dict(zip(WNAMES, rest[nw + 1:2 * nw + 1]))
    var = dict(zip(WNAMES, rest[2 * nw + 1:3 * nw + 1]))
    bsz = x.shape[0]
    nc = 1 + SEQ // CHUNK
    tp = nc * CHUNK

    local = {n: _bf(w[n][0]) for n in BIG}
    small_local = _pack([(w[n], 0) for n in SMALL])
    ga = _exchange_start([local["ab_w_in"], small_local], ["ag", "ag"], name="gather_a")
    gb = _exchange_start([_tie(local["s5_glu_w"], ga["token"]), local["ab_w_out"]], ["ag", "ag"], name="gather_b")
    gc = _exchange_start([_tie(local["ssd_w_in"], gb["token"]), local["ssd_w_out"]], ["ag", "ag"], name="gather_c")
    got_a = _exchange_wait(ga, gc["token"])

    def assemble_big(n, got):
        return _assemble(got[:, None], SHARD_AXIS[n])[0]

    full = {"ab_w_in": assemble_big("ab_w_in", got_a[0])}
    for n, g in zip(SMALL, _unpack(got_a[1], [w[n].shape for n in SMALL])):
        full[n] = _assemble(g, SHARD_AXIS[n])[0] if n != "meta_tokens" else _assemble(g, SHARD_AXIS[n])
    for n in REPL:
        full[n] = w[n][0] if n != "final_norm" else w[n]
    cuts0 = (0, S5_WIDTH, 2 * S5_WIDTH, 2 * S5_WIDTH + ML_WIDTH, 2 * (S5_WIDTH + ML_WIDTH))
    w_in0 = [full["ab_w_in"][:, cuts0[i]:cuts0[i + 1]] for i in range(4)]
    glu_b = full["s5_glu_b"].reshape(1, S5_WIDTH)

    meta = jnp.broadcast_to(full["meta_tokens"][None], (bsz, N_META, D_MODEL))
    h0 = jnp.concatenate([jnp.zeros((bsz, PAD_ROWS, D_MODEL), F32), meta, x], axis=1).reshape(bsz * tp, D_MODEL)
    xn0 = _rms_fwd(h0, full["ab_norm"], name="rms0")
    u, za, xb, zb = [_mm(xn0, wi, "NN", name=f"in0_{i}") for i, wi in enumerate(w_in0)]
    s5p, s5_vjp = _s5_tables(*[full[n] for n in ("s5_lambda_re", "s5_lambda_im", "s5_log_dt", "s5_b_re", "s5_b_im",
                                                   "s5_c_re", "s5_c_im", "s5_d")])
    got_b = []

    def glu_w_after(scan_out):
        got_b.extend(_exchange_wait(gb, scan_out))
        return assemble_big("s5_glu_w", got_b[0])

    sv5 = _s5_layer_fwd(u, s5p, glu_w_after, bsz, nc)
    glu_w = assemble_big("s5_glu_w", got_b[0])
    w_out0 = assemble_big("ab_w_out", got_b[1])
    w_out0 = [w_out0[:S5_WIDTH], w_out0[S5_WIDTH:]]
    ya = _s5_post(sv5["y1"], sv5["glu_pre"], glu_b, za)
    mlw = _ml_weights(*[full[n] for n in ("ml_conv_w", "ml_conv_b", "ml_wq", "ml_wk", "ml_wv", "ml_w_gate",
                                           "ml_b_gate", "ml_norm", "ml_skip")])
    yb, svm = _ml_layer_fwd(xb, zb, mlw, bsz, nc)
    h1 = _mm(ya, w_out0[0], "NN", name="out0_a", add=h0)
    h1 = _mm(yb, w_out0[1], "NN", name="out0_b", add=h1)
    got_c = _exchange_wait(gc, h1)
    w_in1, w_out1 = assemble_big("ssd_w_in", got_c[0]), assemble_big("ssd_w_out", got_c[1])
    cuts1 = (0, SSD_INNER, 2 * SSD_INNER, 2 * SSD_INNER + SSD_BC, 2 * SSD_INNER + 2 * SSD_BC)
    w_in1 = [w_in1[:, cuts1[i]:cuts1[i + 1]] for i in range(4)] + [_pad_lanes(w_in1[:, cuts1[4]:])]
    xn1 = _rms_fwd(h1, full["ssd_norm"], name="rms1")
    z1, xs_in, bm_in, cm_in, dt_rows = [_mm(xn1, wi, "NN", name=f"in1_{i}") for i, wi in enumerate(w_in1)]
    ssdw = _ssd_weights(*[full[n] for n in ("ssd_conv_w", "ssd_conv_b", "ssd_dt_bias", "ssd_a_log", "ssd_d",
                                             "ssd_gnorm")])
    yn, svs = _ssd_layer_fwd(z1, xs_in, bm_in, cm_in, dt_rows, ssdw, bsz, nc)
    h2 = _mm(yn, w_out1, "NN", name="out1", add=h1)
    loss_part, dh2, dfinal = _final_loss(h2, full["final_norm"], loss_target, bsz, nc)
    loss = lax.psum(loss_part[0, 0], ("x", "y", "c"))

    g = {"final_norm": dfinal}
    dyn = _mm(dh2, w_out1, "NT", name="d_out1")
    g["ssd_w_out"] = _mm(yn, dh2, "TN", name="dw_out1", out_dtype=BF16)
    dz1, dxs, dbm, dcm, ddt, gs = _ssd_layer_bwd(dyn, z1, xs_in, bm_in, cm_in, svs, ssdw, bsz, nc)
    g.update(gs)
    dps1 = (dz1, dxs, dbm, dcm, ddt)
    dxn1 = None
    for i, (dp, wi) in enumerate(zip(dps1, w_in1)):
        dxn1 = _mm(dp, wi, "NT", name=f"d_in1_{i}", add=dxn1)
    dw1 = [_mm(xn1, dp, "TN", name=f"dw_in1_{i}", out_dtype=BF16) for i, dp in enumerate(dps1)]
    g["ssd_w_in"] = jnp.concatenate(dw1[:4] + [dw1[4][:, :SSD_HEADS]], axis=1)

    def local_shape(n):
        return w[n].shape

    def slabs(n):
        gf = g[n].reshape((1,) + tuple(g[n].shape)) if n != "meta_tokens" else g[n]
        full_shape = tuple(d * (N_DEV if i == SHARD_AXIS[n] else 1) for i, d in enumerate(local_shape(n)))
        return _split(gf.reshape(full_shape), SHARD_AXIS[n])

    x1 = _exchange_start([slabs("ssd_w_in")[:, 0], slabs("ssd_w_out")[:, 0]], ["a2a", "a2a"], name="grads_1")
    dh1, g["ssd_norm"] = _rms_bwd(h1, full["ssd_norm"], dxn1, _tie(dh2, x1["token"]), name="rms1_bwd")
    dya = _mm(dh1, w_out0[0], "NT", name="d_out0_a")
    dyb = _mm(dh1, w_out0[1], "NT", name="d_out0_b")
    g["ab_w_out"] = jnp.concatenate([_mm(ya, dh1, "TN", name="dw_out0_a", out_dtype=BF16),
                                     _mm(yb, dh1, "TN", name="dw_out0_b", out_dtype=BF16)], axis=0)
    du, dza, g5 = _s5_layer_bwd(dya, u, za, sv5, s5p, s5_vjp, glu_w, glu_b, bsz, nc)
    g.update(g5)
    x2 = _exchange_start([slabs("ab_w_out")[:, 0], _bf(slabs("s5_glu_w")[:, 0])], ["a2a", "a2a"], name="grads_2")
    dxb, dzb, gm = _ml_layer_bwd(_tie(dyb, x2["token"]), xb, zb, svm, mlw, bsz, nc)
    g.update(gm)
    dps0 = (du, dza, dxb, dzb)
    dxn0 = None
    for i, (dp, wi) in enumerate(zip(dps0, w_in0)):
        dxn0 = _mm(dp, wi, "NT", name=f"d_in0_{i}", add=dxn0)
    dw0 = [_mm(xn0, dp, "TN", name=f"dw_in0_{i}", out_dtype=BF16, tn=S5_WIDTH, slabs=True) for i, dp in enumerate(dps0)]
    dw_in0_slabs = jnp.concatenate(dw0, axis=0)
    dh0, g["ab_norm"] = _rms_bwd(h0, full["ab_norm"], dxn0, dh1, name="rms0_bwd")
    dh0 = dh0.reshape(bsz, tp, D_MODEL)
    grad_x = dh0[:, CHUNK:]
    g["meta_tokens"] = jnp.sum(dh0[:, PAD_ROWS:CHUNK], axis=0)

    small_g = _pack([(slabs(n), 1) for n in SMALL])
    repl_g = _pack([(g[n], 0) for n in REPL])
    x3 = _exchange_start([dw_in0_slabs, small_g, repl_g], ["a2a", "a2a", "ag"], name="grads_3")

    def update_big(n, gp):
        return _adamw(w[n][0], mom[n][0], var[n][0], gp, name=f"adamw_{n}")

    res = {}
    ex1 = _exchange_wait(x1, x3["token"])
    res["ssd_w_in"], res["ssd_w_out"] = update_big("ssd_w_in", ex1[0]), update_big("ssd_w_out", ex1[1])
    ex2 = _exchange_wait(x2, res["ssd_w_out"][0])
    res["ab_w_out"], res["s5_glu_w"] = update_big("ab_w_out", ex2[0]), update_big("s5_glu_w", ex2[1])
    ex3 = _exchange_wait(x3, res["s5_glu_w"][0])
    res["ab_w_in"] = update_big("ab_w_in", ex3[0])
    for names, gp, tag in ((SMALL, ex3[1], "small"), (REPL, ex3[2], "repl")):
        shapes = [local_shape(n) for n in names]
        packs = [_pack([(d[n], 0) for n in names]) for d in (w, mom, var)]
        outs = _adamw(packs[0], packs[1], packs[2], gp, name=f"adamw_{tag}")
        for k, o in enumerate(outs):
            for n, a in zip(names, _unpack(o, shapes)):
                res.setdefault(n, [None] * 4)[k] = a
    outs = [loss, grad_x]
    for k in range(4):
        outs += [res[n][k].reshape(local_shape(n)) for n in WNAMES]
    return tuple(outs)
```

```python
import functools
import math

import jax
import jax.numpy as jnp
from jax import lax
from jax.experimental import pallas as pl
from jax.experimental.pallas import tpu as pltpu

F32 = jnp.float32
BF16 = jnp.bfloat16

D_MODEL = 2048
SEQ = 2048
N_META = 16
CHUNK = 128
PAD_ROWS = CHUNK - N_META
NORM_EPS = 1e-6
HEAD_NORM_EPS = 1e-5
S5_WIDTH = 1024
S5_GROUPS = 64
S5_GROUP_SIZE = 16
S5_STATE = 64
S5_GB = 8
S5_LANES = S5_GB * S5_STATE
ML_WIDTH = 3072
ML_HEADS = 8
ML_DH = 384
ML_CONV = 4
QKV_BLOCK = 4
SSD_INNER = 4096
SSD_HEADS = 64
SSD_P = 64
SSD_N = 128
SSD_GROUPS = 8
SSD_HPG = 8
SSD_GW = SSD_HPG * SSD_P
N_DEV = 8
ADAM_LR, ADAM_B1, ADAM_B2, ADAM_EPS, ADAM_WD, ADAM_STEP = 0.001, 0.9, 0.999, 1e-08, 0.01, 10
NEG = -1e30
VMEM_CAP = 60 * 1024 * 1024
MESH = pl.DeviceIdType.MESH

NN = (((1,), (0,)), ((), ()))
NT = (((1,), (1,)), ((), ()))
TN = (((0,), (0,)), ((), ()))


def _dot(a, b, dims=NN):
    return lax.dot_general(a, b, dims, preferred_element_type=F32)


def _bf(x):
    return x.astype(BF16)


def _pick(n, cands):
    for c in cands:
        if n % c == 0:
            return c
    return n


def _nbytes(shape, dtype):
    return math.prod(shape) * jnp.dtype(dtype).itemsize


ANY_SPEC = pl.BlockSpec(memory_space=pl.ANY)


def _pc(body, *, name, grid, in_specs, out_specs, out_shape, scratch=(), vmem=None, dep=None):
    limit = None if vmem is None else int(min(VMEM_CAP, max(32 * 1024 * 1024, 2 * vmem + (8 << 20))))
    n_in = len(in_specs)
    if dep is not None:
        inner = body

        def body(*refs):
            inner(*refs[:n_in], *refs[n_in + 1:])

        in_specs = list(in_specs) + [ANY_SPEC]
    call = pl.pallas_call(
        body, name=name, grid=grid, in_specs=in_specs, out_specs=out_specs, out_shape=out_shape,
        scratch_shapes=list(scratch),
        compiler_params=pltpu.CompilerParams(dimension_semantics=("arbitrary",) * len(grid), vmem_limit_bytes=limit))
    return call if dep is None else (lambda *args: call(*args, dep))


def _silu(x):
    return x * jax.nn.sigmoid(x)


def _dsilu(x):
    s = jax.nn.sigmoid(x)
    return s * (1.0 + x * (1.0 - s))


def _gelu_and_grad(x):
    c0 = math.sqrt(2.0 / math.pi)
    inner = c0 * (x + 0.044715 * x * x * x)
    t = jnp.tanh(inner)
    g = 0.5 * x * (1.0 + t)
    dg = 0.5 * (1.0 + t) + 0.5 * x * (1.0 - t * t) * c0 * (1.0 + 3 * 0.044715 * x * x)
    return g, dg


def _mm(a, b, mode, *, name, add=None, out_dtype=F32, tn=None, slabs=False):
    if mode == "NN":
        (m, k), (k2, n) = a.shape, b.shape
    elif mode == "NT":
        (m, k), (n, k2) = a.shape, b.shape
    else:
        (k, m), (k2, n) = a.shape, b.shape
    assert k == k2, (a.shape, b.shape, mode)
    tm = _pick(m, (1088, 1024, 768, 512, 384, 256, 128))
    tn = tn or _pick(n, (512, 384, 256, 128))
    tk = _pick(k, (2048, 1088, 1024, 768, 512, 384, 256, 128))
    nk = k // tk
    dims = {"NN": NN, "NT": NT, "TN": TN}[mode]

    def body(*refs):
        a_ref, b_ref = refs[0], refs[1]
        add_ref = refs[2] if add is not None else None
        o_ref, acc_ref = refs[-2], refs[-1]
        kk = pl.program_id(2)

        @pl.when(kk == 0)
        def _():
            acc_ref[...] = jnp.zeros_like(acc_ref)

        acc_ref[...] += _dot(_bf(a_ref[...]), _bf(b_ref[...]), dims)

        @pl.when(kk == nk - 1)
        def _():
            r = acc_ref[...]
            if add_ref is not None:
                r = r + add_ref[...]
            o_ref[...] = r.reshape(o_ref.shape).astype(o_ref.dtype)

    if mode == "NN":
        a_spec = pl.BlockSpec((tm, tk), lambda i, j, kk: (i, kk))
        b_spec = pl.BlockSpec((tk, tn), lambda i, j, kk: (kk, j))
    elif mode == "NT":
        a_spec = pl.BlockSpec((tm, tk), lambda i, j, kk: (i, kk))
        b_spec = pl.BlockSpec((tn, tk), lambda i, j, kk: (j, kk))
    else:
        a_spec = pl.BlockSpec((tk, tm), lambda i, j, kk: (kk, i))
        b_spec = pl.BlockSpec((tk, tn), lambda i, j, kk: (kk, j))
    in_specs = [a_spec, b_spec]
    args = [a, b]
    if add is not None:
        in_specs.append(pl.BlockSpec((tm, tn), lambda i, j, kk: (i, j)))
        args.append(add)
    if slabs:
        out_shape = jax.ShapeDtypeStruct((n // tn, m, tn), out_dtype)
        out_spec = pl.BlockSpec((1, tm, tn), lambda i, j, kk: (j, i, 0))
    else:
        out_shape = jax.ShapeDtypeStruct((m, n), out_dtype)
        out_spec = pl.BlockSpec((tm, tn), lambda i, j, kk: (i, j))
    vmem = (_nbytes((tm, tk), a.dtype) + _nbytes((tk, tn), b.dtype) + _nbytes((tm, tn), out_dtype)
            + (_nbytes((tm, tn), F32) if add is not None else 0)) + _nbytes((tm, tn), F32) // 2
    return _pc(body, name=name, grid=(m // tm, n // tn, nk), in_specs=in_specs, out_specs=out_spec,
               out_shape=out_shape, scratch=[pltpu.VMEM((tm, tn), F32)], vmem=vmem)(*args)


def _rms_fwd(x, g, *, name):
    r, d = x.shape
    tm = _pick(r, (256, 128))

    def body(x_ref, g_ref, o_ref):
        xv = x_ref[...]
        rstd = lax.rsqrt(jnp.mean(xv * xv, axis=1, keepdims=True) + NORM_EPS)
        o_ref[...] = (xv * rstd * g_ref[...]).astype(o_ref.dtype)

    return _pc(body, name=name, grid=(r // tm,),
               in_specs=[pl.BlockSpec((tm, d), lambda i: (i, 0)), pl.BlockSpec((1, d), lambda i: (0, 0))],
               out_specs=pl.BlockSpec((tm, d), lambda i: (i, 0)), out_shape=jax.ShapeDtypeStruct((r, d), BF16),
               vmem=tm * d * 6)(x, g.reshape(1, d))


def _rms_bwd(x, g, dxn, dres, *, name, dep=None):
    r, d = x.shape
    tm = _pick(r, (256, 128))

    def body(x_ref, g_ref, dxn_ref, dres_ref, dx_ref, dg_ref):
        @pl.when(pl.program_id(0) == 0)
        def _():
            dg_ref[...] = jnp.zeros_like(dg_ref)

        xv = x_ref[...]
        rstd = lax.rsqrt(jnp.mean(xv * xv, axis=1, keepdims=True) + NORM_EPS)
        xh = xv * rstd
        dy = dxn_ref[...]
        dg_ref[...] += jnp.sum(dy * xh, axis=0, keepdims=True)
        dyg = dy * g_ref[...]
        dx_ref[...] = dres_ref[...] + rstd * (dyg - xh * jnp.mean(dyg * xh, axis=1, keepdims=True))

    row = pl.BlockSpec((tm, d), lambda i: (i, 0))
    vec = pl.BlockSpec((1, d), lambda i: (0, 0))
    return _pc(body, name=name, grid=(r // tm,), in_specs=[row, vec, row, row], out_specs=[row, vec],
               out_shape=[jax.ShapeDtypeStruct((r, d), F32), jax.ShapeDtypeStruct((1, d), F32)],
               vmem=tm * d * 16, dep=dep)(x, g.reshape(1, d), dxn, dres)


def _final_loss(h, g, target, bsz, nc):
    d = h.shape[1]

    def body(h_ref, g_ref, t_ref, loss_ref, dh_ref, dg_ref):
        b, c = pl.program_id(0), pl.program_id(1)

        @pl.when((b == 0) & (c == 0))
        def _():
            loss_ref[...] = jnp.zeros_like(loss_ref)
            dg_ref[...] = jnp.zeros_like(dg_ref)

        @pl.when(c == 0)
        def _():
            dh_ref[...] = jnp.zeros_like(dh_ref)

        @pl.when(c > 0)
        def _():
            xv = h_ref[...]
            rstd = lax.rsqrt(jnp.mean(xv * xv, axis=1, keepdims=True) + NORM_EPS)
            xh = xv * rstd
            gv = g_ref[...]
            err = xh * gv - t_ref[0]
            loss_ref[...] += 0.5 * jnp.sum(jnp.mean(err * err, axis=1, keepdims=True))
            dy = err * (1.0 / d)
            dg_ref[...] += jnp.sum(dy * xh, axis=0, keepdims=True)
            dyg = dy * gv
            dh_ref[...] = rstd * (dyg - xh * jnp.mean(dyg * xh, axis=1, keepdims=True))

    row = pl.BlockSpec((CHUNK, d), lambda b, c: (b * nc + c, 0))
    vec = pl.BlockSpec((1, d), lambda b, c: (0, 0))
    return _pc(body, name="final_loss", grid=(bsz, nc),
               in_specs=[row, vec, pl.BlockSpec((1, CHUNK, d), lambda b, c: (b, jnp.maximum(c - 1, 0), 0))],
               out_specs=[pl.BlockSpec((8, 128), lambda b, c: (0, 0)), row, vec],
               out_shape=[jax.ShapeDtypeStruct((8, 128), F32), jax.ShapeDtypeStruct(h.shape, F32),
                          jax.ShapeDtypeStruct((1, d), F32)],
               vmem=CHUNK * d * 16)(h, g.reshape(1, d), target)


def _adamw(w, m, v, gparts, *, name):
    r, c = w.shape
    tr = _pick(r, (256, 128)) if r * c * 4 > (1 << 20) else r

    def body(w_ref, m_ref, v_ref, gp_ref, g_ref, d_ref, nm_ref, nv_ref):
        g = gp_ref[0].astype(F32)
        for j in range(1, N_DEV):
            g = g + gp_ref[j].astype(F32)
        mm = ADAM_B1 * m_ref[...] + (1.0 - ADAM_B1) * g
        vv = ADAM_B2 * v_ref[...] + (1.0 - ADAM_B2) * (g * g)
        m_hat = mm / (1.0 - ADAM_B1 ** ADAM_STEP)
        v_hat = vv / (1.0 - ADAM_B2 ** ADAM_STEP)
        g_ref[...] = g
        d_ref[...] = -ADAM_LR * (m_hat / (jnp.sqrt(v_hat) + ADAM_EPS) + ADAM_WD * w_ref[...])
        nm_ref[...] = mm
        nv_ref[...] = vv

    blk = pl.BlockSpec((tr, c), lambda i: (i, 0))
    out = jax.ShapeDtypeStruct((r, c), F32)
    return _pc(body, name=name, grid=(r // tr,),
               in_specs=[blk, blk, blk, pl.BlockSpec((N_DEV, tr, c), lambda i: (0, i, 0))],
               out_specs=[blk, blk, blk, blk], out_shape=[out, out, out, out],
               vmem=tr * c * (4 * 7 + N_DEV * jnp.dtype(gparts.dtype).itemsize))(w, m, v, gparts)


PEERS = (1, 2, 4, 6, 3, 5, 7)
HBM_SPEC = pl.BlockSpec(memory_space=pltpu.HBM)
SEM_SPEC = pl.BlockSpec(memory_space=pltpu.SEMAPHORE)
SIDE_EFFECT = pltpu.SideEffectType.DATAFLOW_SIDE_EFFECTING


def _peer(p):
    x, y, c = lax.axis_index("x"), lax.axis_index("y"), lax.axis_index("c")
    tx, ty, tc = x ^ ((p >> 2) & 1), y ^ ((p >> 1) & 1), c ^ (p & 1)
    return (tx, ty, tc), 4 * tx + 2 * ty + tc


def _place_own(a, kind, *, name):
    rows, cols = a.shape[-2:]
    tr = _pick(rows, (512, 256, 128, 64, 32, 16))
    me = (4 * lax.axis_index("x") + 2 * lax.axis_index("y") + lax.axis_index("c")).astype(jnp.int32).reshape(1)

    def body(me_ref, in_ref, out_ref):
        out_ref[...] = in_ref[...].reshape(out_ref.shape)

    if kind == "a2a":
        in_spec = pl.BlockSpec((1, tr, cols), lambda i, me_ref: (me_ref[0], i, 0))
    else:
        in_spec = pl.BlockSpec((tr, cols), lambda i, me_ref: (i, 0))
    return pl.pallas_call(
        body, name=name, out_shape=jax.ShapeDtypeStruct((N_DEV, rows, cols), a.dtype),
        grid_spec=pltpu.PrefetchScalarGridSpec(
            num_scalar_prefetch=1, grid=(rows // tr,), in_specs=[in_spec],
            out_specs=pl.BlockSpec((1, tr, cols), lambda i, me_ref: (me_ref[0], i, 0))))(me, a)


def _exchange_copies(ins, lands, send_sems, recv_sems, kinds, incoming):
    me = 4 * lax.axis_index("x") + 2 * lax.axis_index("y") + lax.axis_index("c")
    copies = []
    for i, kind in enumerate(kinds):
        for p in PEERS:
            dev, tgt = _peer(p)
            k = i * (N_DEV - 1) + p - 1
            copies.append(pltpu.make_async_remote_copy(
                src_ref=ins[i].at[tgt] if kind == "a2a" else ins[i], dst_ref=lands[i].at[tgt if incoming else me],
                send_sem=send_sems.at[k], recv_sem=recv_sems.at[k], device_id=dev, device_id_type=MESH))
    return copies


def _exchange_start(arrays, kinds, *, name, dep=None):
    n = len(arrays)
    lands = [_place_own(a, k, name=f"{name}_own{i}") for i, (a, k) in enumerate(zip(arrays, kinds))]
    extra = [] if dep is None else [dep]

    def body(*refs):
        ins, lnd = refs[:n], refs[n:2 * n]
        send_sems, recv_sems = refs[2 * n + len(extra)], refs[2 * n + len(extra) + 1]
        token = refs[-1]
        for cp in _exchange_copies(ins, lnd, send_sems, recv_sems, kinds, False):
            cp.start()
        token[...] = jnp.zeros_like(token)

    sem = pltpu.SemaphoreType.DMA((n * (N_DEV - 1),))
    outs = pl.pallas_call(
        body, name=name, in_specs=[HBM_SPEC] * (2 * n) + [ANY_SPEC] * len(extra),
        out_specs=[SEM_SPEC, SEM_SPEC] + [HBM_SPEC] * (2 * n) + [pl.BlockSpec(memory_space=pltpu.VMEM)],
        out_shape=[sem, sem] + [pltpu.HBM(a.shape, a.dtype) for a in arrays + lands]
        + [jax.ShapeDtypeStruct((8, 128), F32)],
        input_output_aliases={i: 2 + i for i in range(2 * n)},
        compiler_params=pltpu.CompilerParams(has_side_effects=SIDE_EFFECT),
    )(*[pltpu.with_memory_space_constraint(a, pltpu.HBM) for a in arrays + lands], *extra)
    return dict(send=outs[0], recv=outs[1], ins=list(outs[2:2 + n]), lands=list(outs[2 + n:2 + 2 * n]),
                token=outs[-1], kinds=kinds, name=name)


def _exchange_wait(h, after):
    n = len(h["ins"])
    kinds = h["kinds"]

    def body(*refs):
        ins, lnd = refs[:n], refs[n:2 * n]
        send_sems, recv_sems = refs[2 * n], refs[2 * n + 1]
        copies = _exchange_copies(ins, lnd, send_sems, recv_sems, kinds, True)
        for cp in copies:
            cp.wait_recv()
        for cp in copies:
            cp.wait_send()

    arrs = h["ins"] + h["lands"]
    outs = pl.pallas_call(
        body, name=h["name"] + "_wait", in_specs=[HBM_SPEC] * (2 * n) + [SEM_SPEC, SEM_SPEC, pl.BlockSpec(memory_space=pl.ANY)],
        out_specs=[HBM_SPEC] * (2 * n), out_shape=[pltpu.HBM(a.shape, a.dtype) for a in arrs],
        input_output_aliases={i: i for i in range(2 * n)},
        compiler_params=pltpu.CompilerParams(has_side_effects=SIDE_EFFECT),
    )(*arrs, h["send"], h["recv"], after)
    return list(outs[n:])


def _s5_params(lam_re, lam_im, log_dt, b_re, b_im):
    dt = jnp.exp(log_dt)[:, None]
    mag = jnp.exp(lam_re * dt)
    ar, ai = mag * jnp.cos(lam_im * dt), mag * jnp.sin(lam_im * dt)
    den = lam_re * lam_re + lam_im * lam_im
    qr = ((ar - 1.0) * lam_re + ai * lam_im) / den
    qi = (ai * lam_re - (ar - 1.0) * lam_im) / den
    bbr = qr[..., None] * b_re - qi[..., None] * b_im
    bbi = qr[..., None] * b_im + qi[..., None] * b_re
    return ar, ai, bbr, bbi


def _s5_power_table(ar, ai):
    pr, pi = ar.reshape(1, -1), ai.reshape(1, -1)
    while pr.shape[0] < CHUNK:
        sr, si = pr[-1:], pi[-1:]
        pr, pi = (jnp.concatenate([pr, pr * sr - pi * si], axis=0), jnp.concatenate([pi, pr * si + pi * sr], axis=0))
    return pr, pi


def _blockdiag(w, rows, cols):
    w = w.reshape(S5_GB, S5_GB, rows, cols)
    eye = jnp.eye(S5_GB, dtype=w.dtype)
    return jnp.einsum("abrc,bd->abrdc", w, eye).reshape(S5_GB, S5_GB * rows, S5_GB * cols)


def _blockdiag_extract(w, rows, cols):
    w = w.reshape(S5_GB, S5_GB, rows, S5_GB, cols)
    return jnp.einsum("abrbc->abrc", w).reshape(S5_GROUPS, rows, cols)


def _s5_scan_specs(bsz, nc, rev):
    def chunk(b, c):
        return b * nc + ((nc - 1 - c) if rev else c)

    return dict(
        u=pl.BlockSpec((CHUNK, CHUNK), lambda g, b, c: (chunk(b, c), g)),
        x=pl.BlockSpec((CHUNK, S5_LANES), lambda g, b, c: (chunk(b, c), g)),
        wb=pl.BlockSpec((1, CHUNK, S5_LANES), lambda g, b, c: (g, 0, 0)),
        wc=pl.BlockSpec((1, S5_LANES, CHUNK), lambda g, b, c: (g, 0, 0)),
        tab=pl.BlockSpec((CHUNK, S5_LANES), lambda g, b, c: (0, g)),
        step=pl.BlockSpec((8, S5_LANES), lambda g, b, c: (0, g)),
        d=pl.BlockSpec((1, CHUNK), lambda g, b, c: (0, g)),
        lane=pl.BlockSpec((1, S5_LANES), lambda g, b, c: (0, g)),
        xprev=pl.BlockSpec((8, S5_LANES), lambda g, b, c: (jnp.maximum(chunk(b, c) * (CHUNK // 8) - 1, 0), g)),
    )


def _s5_fwd(u, wbr, wbi, pr, pi, sr, si, wcr, wci, d, bsz, nc):
    r = u.shape[0]
    sp = _s5_scan_specs(bsz, nc, False)

    def body(u_ref, wbr_ref, wbi_ref, pr_ref, pi_ref, sr_ref, si_ref, wcr_ref, wci_ref, d_ref,
             xr_ref, xi_ref, y1_ref, g_ref, cr_s, ci_s):
        @pl.when(pl.program_id(2) == 0)
        def _():
            cr_s[...] = jnp.zeros_like(cr_s)
            ci_s[...] = jnp.zeros_like(ci_s)

        uv = u_ref[...]
        ub = _bf(uv)
        xr, xi = _dot(ub, wbr_ref[0]), _dot(ub, wbi_ref[0])
        row = lax.broadcasted_iota(jnp.int32, (CHUNK, S5_LANES), 0)
        for k in range(7):
            s = 1 << k
            ar, ai = sr_ref[k:k + 1, :], si_ref[k:k + 1, :]
            hr = jnp.where(row >= s, pltpu.roll(xr, s, 0), 0.0)
            hi = jnp.where(row >= s, pltpu.roll(xi, s, 0), 0.0)
            xr, xi = xr + (ar * hr - ai * hi), xi + (ar * hi + ai * hr)
        cr, ci = cr_s[...], ci_s[...]
        tr, ti = pr_ref[...], pi_ref[...]
        xr, xi = xr + (tr * cr - ti * ci), xi + (tr * ci + ti * cr)
        cr_s[...] = xr[CHUNK - 1:CHUNK, :]
        ci_s[...] = xi[CHUNK - 1:CHUNK, :]
        xr_ref[...] = xr
        xi_ref[...] = xi
        y = _dot(_bf(xr), wcr_ref[0]) - _dot(_bf(xi), wci_ref[0]) + d_ref[...] * uv
        y1_ref[...] = y
        g_ref[...] = _bf(_gelu_and_grad(y)[0])

    return _pc(body, name="s5_fwd", grid=(S5_GB, bsz, nc),
               in_specs=[sp["u"], sp["wb"], sp["wb"], sp["tab"], sp["tab"], sp["step"], sp["step"], sp["wc"], sp["wc"],
                         sp["d"]],
               out_specs=[sp["x"], sp["x"], sp["u"], sp["u"]],
               out_shape=[jax.ShapeDtypeStruct((r, S5_GROUPS * S5_STATE), F32)] * 2
               + [jax.ShapeDtypeStruct((r, S5_WIDTH), F32), jax.ShapeDtypeStruct((r, S5_WIDTH), BF16)],
               scratch=[pltpu.VMEM((1, S5_LANES), F32)] * 2, vmem=4 << 20,
               )(u, wbr, wbi, pr, pi, sr, si, wcr, wci, d)


def _s5_post(y1, glu_pre, glu_b, z):
    r, w = y1.shape
    tm = _pick(r, (256, 128))

    def body(y_ref, p_ref, b_ref, z_ref, o_ref):
        g = _gelu_and_grad(y_ref[...])[0]
        o_ref[...] = _bf(g * jax.nn.sigmoid(p_ref[...] + b_ref[...]) * _silu(z_ref[...]))

    row = pl.BlockSpec((tm, w), lambda i: (i, 0))
    return _pc(body, name="s5_post", grid=(r // tm,), in_specs=[row, row, pl.BlockSpec((1, w), lambda i: (0, 0)), row],
               out_specs=row, out_shape=jax.ShapeDtypeStruct((r, w), BF16), vmem=tm * w * 16)(y1, glu_pre, glu_b, z)


def _s5_post_bwd(dya, y1, glu_pre, glu_b, z):
    r, w = y1.shape
    tm = _pick(r, (256, 128))

    def body(dy_ref, y_ref, p_ref, b_ref, z_ref, dz_ref, dp_ref, dg_ref, db_ref):
        @pl.when(pl.program_id(0) == 0)
        def _():
            db_ref[...] = jnp.zeros_like(db_ref)

        g = _gelu_and_grad(y_ref[...])[0]
        s = jax.nn.sigmoid(p_ref[...] + b_ref[...])
        zv = z_ref[...]
        dy = dy_ref[...]
        do = dy * _silu(zv)
        dz_ref[...] = _bf(dy * g * s * _dsilu(zv))
        dp = do * g * s * (1.0 - s)
        dp_ref[...] = _bf(dp)
        db_ref[...] += jnp.sum(dp, axis=0, keepdims=True)
        dg_ref[...] = do * s

    row = pl.BlockSpec((tm, w), lambda i: (i, 0))
    vec = pl.BlockSpec((1, w), lambda i: (0, 0))
    return _pc(body, name="s5_post_bwd", grid=(r // tm,), in_specs=[row, row, row, vec, row],
               out_specs=[row, row, row, vec],
               out_shape=[jax.ShapeDtypeStruct((r, w), BF16), jax.ShapeDtypeStruct((r, w), BF16),
                          jax.ShapeDtypeStruct((r, w), F32), jax.ShapeDtypeStruct((1, w), F32)],
               vmem=tm * w * 24)(dya, y1, glu_pre, glu_b, z)


def _s5_bwd(dg, y1, u, xr, xi, wbr, wbi, qr, qi, sr, si, wcr, wci, d, bsz, nc):
    r = u.shape[0]
    sp = _s5_scan_specs(bsz, nc, True)

    def body(dg_ref, y1_ref, u_ref, xr_ref, xi_ref, xpr_ref, xpi_ref, wbr_ref, wbi_ref, qr_ref, qi_ref, sr_ref, si_ref,
             wcr_ref, wci_ref, d_ref, du_ref, dd_ref, dwcr_ref, dwci_ref, dwbr_ref, dwbi_ref, dar_ref, dai_ref,
             cr_s, ci_s):
        b, c = pl.program_id(1), pl.program_id(2)

        @pl.when((b == 0) & (c == 0))
        def _():
            for ref in (dd_ref, dwcr_ref, dwci_ref, dwbr_ref, dwbi_ref, dar_ref, dai_ref):
                ref[...] = jnp.zeros_like(ref)

        @pl.when(c == 0)
        def _():
            cr_s[...] = jnp.zeros_like(cr_s)
            ci_s[...] = jnp.zeros_like(ci_s)

        uv = u_ref[...]
        ub = _bf(uv)
        dy = dg_ref[...] * _gelu_and_grad(y1_ref[...])[1]
        dd_ref[...] += jnp.sum(dy * uv, axis=0, keepdims=True)
        dyb = _bf(dy)
        xr, xi = xr_ref[...], xi_ref[...]
        dwcr_ref[0] += _dot(_bf(xr), dyb, TN)
        dwci_ref[0] -= _dot(_bf(xi), dyb, TN)
        lr, li = _dot(dyb, wcr_ref[0], NT), -_dot(dyb, wci_ref[0], NT)
        row = lax.broadcasted_iota(jnp.int32, (CHUNK, S5_LANES), 0)
        for k in range(7):
            s = 1 << k
            ar, ai = sr_ref[k:k + 1, :], si_ref[k:k + 1, :]
            hr = jnp.where(row < CHUNK - s, pltpu.roll(lr, CHUNK - s, 0), 0.0)
            hi = jnp.where(row < CHUNK - s, pltpu.roll(li, CHUNK - s, 0), 0.0)
            lr, li = lr + (ar * hr + ai * hi), li + (ar * hi - ai * hr)
        cr, ci = cr_s[...], ci_s[...]
        tr, ti = qr_ref[...], qi_ref[...]
        lr, li = lr + (tr * cr + ti * ci), li + (tr * ci - ti * cr)
        cr_s[...] = lr[0:1, :]
        ci_s[...] = li[0:1, :]
        lrb, lib = _bf(lr), _bf(li)
        du_ref[...] = _bf(_dot(lrb, wbr_ref[0], NT) + _dot(lib, wbi_ref[0], NT) + dy * d_ref[...])
        dwbr_ref[0] += _dot(ub, lrb, TN)
        dwbi_ref[0] += _dot(ub, lib, TN)
        first = c == nc - 1
        pr0 = jnp.where(first, 0.0, xpr_ref[7:8, :])
        pi0 = jnp.where(first, 0.0, xpi_ref[7:8, :])
        xpr = jnp.where(row == 0, pr0, pltpu.roll(xr, 1, 0))
        xpi = jnp.where(row == 0, pi0, pltpu.roll(xi, 1, 0))
        dar_ref[...] += jnp.sum(lr * xpr + li * xpi, axis=0, keepdims=True)
        dai_ref[...] += jnp.sum(li * xpr - lr * xpi, axis=0, keepdims=True)

    st = jax.ShapeDtypeStruct
    return _pc(body, name="s5_bwd", grid=(S5_GB, bsz, nc),
               in_specs=[sp["u"], sp["u"], sp["u"], sp["x"], sp["x"], sp["xprev"], sp["xprev"], sp["wb"], sp["wb"],
                         sp["tab"], sp["tab"], sp["step"], sp["step"], sp["wc"], sp["wc"], sp["d"]],
               out_specs=[sp["u"], sp["d"], sp["wc"], sp["wc"], sp["wb"], sp["wb"], sp["lane"], sp["lane"]],
               out_shape=[st((r, S5_WIDTH), BF16), st((1, S5_WIDTH), F32),
                          st((S5_GB, S5_LANES, CHUNK), F32), st((S5_GB, S5_LANES, CHUNK), F32),
                          st((S5_GB, CHUNK, S5_LANES), F32), st((S5_GB, CHUNK, S5_LANES), F32),
                          st((1, S5_GROUPS * S5_STATE), F32), st((1, S5_GROUPS * S5_STATE), F32)],
               scratch=[pltpu.VMEM((1, S5_LANES), F32)] * 2, vmem=6 << 20,
               )(dg, y1, u, xr, xi, xr, xi, wbr, wbi, qr, qi, sr, si, wcr, wci, d)


def _s5_layer_fwd(u, prm, glu_w, bsz, nc):
    xr, xi, y1, g = _s5_fwd(u, prm["wbr"], prm["wbi"], prm["pr"], prm["pi"], prm["sr"], prm["si"], prm["wcr"],
                            prm["wci"], prm["d"], bsz, nc)
    glu_pre = _mm(g, glu_w(y1) if callable(glu_w) else glu_w, "NN", name="s5_glu")
    return dict(xr=xr, xi=xi, y1=y1, g=g, glu_pre=glu_pre)


def _s5_layer_bwd(dya, u, z, sv, prm, pvjp, glu_w, glu_b, bsz, nc):
    dz, dglu, dg_direct, dglu_b = _s5_post_bwd(dya, sv["y1"], sv["glu_pre"], glu_b, z)
    dg = _mm(dglu, glu_w, "NT", name="s5_dg", add=dg_direct)
    dglu_w = _mm(sv["g"], dglu, "TN", name="s5_dglu_w")
    du, dd, dwcr, dwci, dwbr, dwbi, dar, dai = _s5_bwd(
        dg, sv["y1"], u, sv["xr"], sv["xi"], prm["wbr"], prm["wbi"], prm["qr"], prm["qi"], prm["sr"], prm["si"],
        prm["wcr"], prm["wci"], prm["d"], bsz, nc)
    dbbr = jnp.swapaxes(_blockdiag_extract(dwbr, S5_GROUP_SIZE, S5_STATE), 1, 2)
    dbbi = jnp.swapaxes(_blockdiag_extract(dwbi, S5_GROUP_SIZE, S5_STATE), 1, 2)
    dlr, dli, dldt, dbr, dbi = pvjp((dar.reshape(S5_GROUPS, S5_STATE), dai.reshape(S5_GROUPS, S5_STATE), dbbr, dbbi))
    grads = dict(
        s5_lambda_re=dlr, s5_lambda_im=dli, s5_log_dt=dldt, s5_b_re=dbr, s5_b_im=dbi,
        s5_c_re=jnp.swapaxes(_blockdiag_extract(dwcr, S5_STATE, S5_GROUP_SIZE), 1, 2),
        s5_c_im=jnp.swapaxes(_blockdiag_extract(dwci, S5_STATE, S5_GROUP_SIZE), 1, 2),
        s5_d=dd, s5_glu_w=dglu_w, s5_glu_b=dglu_b)
    return du, dz, grads


def _s5_tables(lam_re, lam_im, log_dt, b_re, b_im, c_re, c_im, d):
    (ar, ai, bbr, bbi), vjp = jax.vjp(_s5_params, lam_re, lam_im, log_dt, b_re, b_im)
    pr, pi = _s5_power_table(lax.stop_gradient(ar), lax.stop_gradient(ai))
    steps = [(1 << k) - 1 for k in range(8)]
    prm = dict(
        wbr=_bf(_blockdiag(jnp.swapaxes(bbr, 1, 2), S5_GROUP_SIZE, S5_STATE)),
        wbi=_bf(_blockdiag(jnp.swapaxes(bbi, 1, 2), S5_GROUP_SIZE, S5_STATE)),
        wcr=_bf(_blockdiag(jnp.swapaxes(c_re, 1, 2), S5_STATE, S5_GROUP_SIZE)),
        wci=_bf(_blockdiag(jnp.swapaxes(c_im, 1, 2), S5_STATE, S5_GROUP_SIZE)),
        pr=pr, pi=pi, qr=pr[::-1], qi=pi[::-1],
        sr=jnp.concatenate([pr[i:i + 1] for i in steps], axis=0),
        si=jnp.concatenate([pi[i:i + 1] for i in steps], axis=0), d=d.reshape(1, S5_WIDTH))
    return prm, vjp


def _tile16(p8):
    return jnp.concatenate([p8] * (CHUNK // 8), axis=0)


def _shift_down(x, halo, s, row):
    return jnp.where(row >= s, pltpu.roll(x, s, 0), pltpu.roll(halo, s, 0))


def _shift_up(x, halo, s, row):
    return jnp.where(row < CHUNK - s, pltpu.roll(x, CHUNK - s, 0), pltpu.roll(halo, CHUNK - s, 0))


def _conv_specs(nc, tw):
    def chunk(b, c):
        return b * nc + c

    return dict(
        x=pl.BlockSpec((CHUNK, tw), lambda j, b, c: (chunk(b, c), j)),
        prev=pl.BlockSpec((8, tw), lambda j, b, c: (jnp.maximum(chunk(b, c) * (CHUNK // 8) - 1, 0), j)),
        nxt=pl.BlockSpec((8, tw), lambda j, b, c: ((b * nc + jnp.minimum(c + 1, nc - 1)) * (CHUNK // 8), j)),
        w=pl.BlockSpec((ML_CONV, tw), lambda j, b, c: (0, j)),
        vec=pl.BlockSpec((1, tw), lambda j, b, c: (0, j)),
    )


def _conv_fwd(x, w, bias, bsz, nc, *, name):
    r, wd = x.shape
    tw = _pick(wd, (512, 384, 256, 128))
    sp = _conv_specs(nc, tw)

    def body(x_ref, p_ref, w_ref, b_ref, o_ref):
        c = pl.program_id(2)
        xv = x_ref[...]
        row = lax.broadcasted_iota(jnp.int32, xv.shape, 0)
        halo = jnp.where(c == 0, 0.0, _tile16(p_ref[...]))
        acc = b_ref[...] + w_ref[3:4, :] * xv
        for s in (1, 2, 3):
            acc = acc + w_ref[3 - s:4 - s, :] * _shift_down(xv, halo, s, row)
        o_ref[...] = acc

    return _pc(body, name=name, grid=(wd // tw, bsz, nc), in_specs=[sp["x"], sp["prev"], sp["w"], sp["vec"]],
               out_specs=sp["x"], out_shape=jax.ShapeDtypeStruct((r, wd), F32), vmem=CHUNK * tw * 16,
               )(x, x, w, bias.reshape(1, wd))


def _conv_bwd(dpre, x, w, bsz, nc, *, name, add=None):
    r, wd = x.shape
    tw = _pick(wd, (512, 384, 256, 128))
    sp = _conv_specs(nc, tw)

    def body(*refs):
        d_ref, n_ref, x_ref, p_ref, w_ref = refs[:5]
        add_ref = refs[5] if add is not None else None
        dx_ref, dw_ref, db_ref = refs[-3:]
        b, c = pl.program_id(1), pl.program_id(2)

        @pl.when((b == 0) & (c == 0))
        def _():
            dw_ref[...] = jnp.zeros_like(dw_ref)
            db_ref[...] = jnp.zeros_like(db_ref)

        dv, xv = d_ref[...], x_ref[...]
        row = lax.broadcasted_iota(jnp.int32, xv.shape, 0)
        dhalo = jnp.where(c == nc - 1, 0.0, _tile16(n_ref[...]))
        xhalo = jnp.where(c == 0, 0.0, _tile16(p_ref[...]))
        dx = w_ref[3:4, :] * dv
        for s in (1, 2, 3):
            dx = dx + w_ref[3 - s:4 - s, :] * _shift_up(dv, dhalo, s, row)
        if add_ref is not None:
            dx = dx + add_ref[...]
        dx_ref[...] = _bf(dx)
        db_ref[...] += jnp.sum(dv, axis=0, keepdims=True)
        dw_ref[3:4, :] += jnp.sum(dv * xv, axis=0, keepdims=True)
        for s in (1, 2, 3):
            dw_ref[3 - s:4 - s, :] += jnp.sum(dv * _shift_down(xv, xhalo, s, row), axis=0, keepdims=True)

    ins = [dpre, dpre, x, x, w] + ([add] if add is not None else [])
    specs = [sp["x"], sp["nxt"], sp["x"], sp["prev"], sp["w"]] + ([sp["x"]] if add is not None else [])
    return _pc(body, name=name, grid=(wd // tw, bsz, nc), in_specs=specs, out_specs=[sp["x"], sp["w"], sp["vec"]],
               out_shape=[jax.ShapeDtypeStruct((r, wd), BF16), jax.ShapeDtypeStruct((ML_CONV, wd), F32),
                          jax.ShapeDtypeStruct((1, wd), F32)], vmem=CHUNK * tw * 24)(*ins)


ML_SCALE = ML_DH ** -0.5


def _headwise_expand(w):
    tiled = jnp.tile(w.reshape(ML_HEADS, ML_DH, QKV_BLOCK), (1, 1, ML_DH // QKV_BLOCK))
    blk = jnp.arange(ML_DH) // QKV_BLOCK
    return jnp.where(blk[:, None] == blk[None, :], tiled, 0.0)


def _headwise_extract(w):
    return w[:, :, :QKV_BLOCK].reshape(ML_HEADS * ML_DH // QKV_BLOCK, QKV_BLOCK, QKV_BLOCK)


def _ml_pre(pre, x, wq, wk, wv, wgq, wgk, wgv, bsz, nc):
    r = x.shape[0]
    hrow = pl.BlockSpec((CHUNK, ML_DH), lambda b, c, h: (b * nc + c, h))
    wexp = pl.BlockSpec((1, ML_DH, ML_DH), lambda b, c, h: (h, 0, 0))
    wg = pl.BlockSpec((ML_DH, CHUNK), lambda b, c, h: (h, 0))
    gspec = pl.BlockSpec((CHUNK, CHUNK), lambda b, c, h: (b * nc + c, 0))

    def body(pre_ref, x_ref, wq_ref, wk_ref, wv_ref, gq_ref, gk_ref, gv_ref, q_ref, qs_ref, k_ref, v_ref, gt_ref):
        @pl.when(pl.program_id(2) == 0)
        def _():
            gt_ref[...] = jnp.zeros_like(gt_ref)

        xcb = _bf(_silu(pre_ref[...]))
        q = _dot(xcb, wq_ref[0])
        k = _dot(xcb, wk_ref[0])
        v = _dot(_bf(x_ref[...]), wv_ref[0])
        qb, kb, vb = _bf(q), _bf(k), _bf(v)
        q_ref[...] = qb
        qs_ref[...] = _bf(q * ML_SCALE)
        k_ref[...] = kb
        v_ref[...] = vb
        gt_ref[...] += _dot(qb, gq_ref[...]) + _dot(kb, gk_ref[...]) + _dot(vb, gv_ref[...])

    o = jax.ShapeDtypeStruct((r, ML_WIDTH), BF16)
    return _pc(body, name="ml_pre", grid=(bsz, nc, ML_HEADS),
               in_specs=[hrow, hrow, wexp, wexp, wexp, wg, wg, wg], out_specs=[hrow, hrow, hrow, hrow, gspec],
               out_shape=[o, o, o, o, jax.ShapeDtypeStruct((r, CHUNK), F32)], vmem=4 << 20,
               )(pre, x, wq, wk, wv, wgq, wgk, wgv)


def _cumsum_rows(x, row, rev=False):
    for k in range(7):
        s = 1 << k
        if rev:
            x = x + jnp.where(row < CHUNK - s, pltpu.roll(x, CHUNK - s, 0), 0.0)
        else:
            x = x + jnp.where(row >= s, pltpu.roll(x, s, 0), 0.0)
    return x


def _log_sigmoid(x):
    return jnp.minimum(x, 0.0) - jnp.log(1.0 + jnp.exp(-jnp.abs(x)))


def _ml_core(gates, hd, first, m, qs, k, v, cmat, nvec):
    sq = (CHUNK, CHUNK)
    lane = lax.broadcasted_iota(jnp.int32, sq, 1)
    row = lax.broadcasted_iota(jnp.int32, sq, 0)
    igc = jnp.sum(jnp.where(lane == hd, gates, 0.0), axis=1, keepdims=True)
    fpc = jnp.sum(jnp.where(lane == hd + ML_HEADS, gates, 0.0), axis=1, keepdims=True)
    valid = jnp.logical_or(jnp.logical_not(first), row[:, :1] >= PAD_ROWS)
    igc = jnp.where(valid, igc, NEG)
    lfc = jnp.where(valid, _log_sigmoid(fpc), 0.0)
    bcb = _cumsum_rows(jnp.broadcast_to(lfc, sq), row)
    igb = jnp.broadcast_to(igc, sq)
    dm = jnp.where(lane <= row, bcb - (bcb - igb).T, NEG)
    bc = bcb[:, :1]
    inter = bc + m
    mt = jnp.maximum(inter, jnp.max(dm, axis=1, keepdims=True))
    wt = jnp.exp(dm - mt)
    wprev = jnp.exp(inter - mt)
    s0 = _dot(qs, k, NT)
    s = s0 * wt
    cb = _bf(cmat)
    qc = _dot(qs, cb)
    qf = qs.astype(F32)
    qn = jnp.sum(qf * nvec, axis=1, keepdims=True)
    num = _dot(_bf(s), v) + wprev * qc
    den = jnp.sum(s, axis=1, keepdims=True) + wprev * qn
    emt = jnp.exp(-mt)
    dd = jnp.maximum(jnp.abs(den), emt)
    blast = bcb[CHUNK - 1:CHUNK, :1]
    g = blast - bc + igc
    m_new = jnp.maximum(blast + m, jnp.max(g, axis=0, keepdims=True))
    decay = jnp.exp(blast + m - m_new)
    e = jnp.exp(g - m_new)
    kf = k.astype(F32)
    wk = e * kf
    return dict(lane=lane, row=row, fpc=fpc, valid=valid, wt=wt, wprev=wprev, s=s, cb=cb, qc=qc, qf=qf, qn=qn,
                num=num, den=den, emt=emt, dd=dd, m_new=m_new, decay=decay, e=e, kf=kf, wk=wk)


def _ml_headnorm(h):
    mu = jnp.mean(h, axis=1, keepdims=True)
    hc = h - mu
    rstd = lax.rsqrt(jnp.mean(hc * hc, axis=1, keepdims=True) + HEAD_NORM_EPS)
    return hc * rstd, rstd


def _ml_chunk_specs(nc, rev, head_major):
    def ix(a, b_, c):
        hd, b = (a, b_) if head_major else (b_, a)
        return hd, b, (nc - 1 - c) if rev else c

    def row(a, b_, c):
        hd, b, cc = ix(a, b_, c)
        return b * nc + cc, hd

    def st(a, b_, c):
        hd, b, cc = ix(a, b_, c)
        return (b * ML_HEADS + hd) * nc + cc

    return dict(
        hrow=pl.BlockSpec((CHUNK, ML_DH), row),
        gates=pl.BlockSpec((CHUNK, CHUNK), lambda a, b_, c: (row(a, b_, c)[0], 0)),
        bias=pl.BlockSpec((1, CHUNK), lambda a, b_, c: (0, 0)),
        hvec=pl.BlockSpec((1, ML_DH), lambda a, b_, c: (0, ix(a, b_, c)[0])),
        cs=pl.BlockSpec((1, ML_DH, ML_DH), lambda a, b_, c: (st(a, b_, c), 0, 0)),
        ns=pl.BlockSpec((1, 1, ML_DH), lambda a, b_, c: (st(a, b_, c), 0, 0)),
        ms=pl.BlockSpec((1, 1, CHUNK), lambda a, b_, c: (st(a, b_, c), 0, 0)),
        dgates=pl.BlockSpec((1, CHUNK, CHUNK), lambda a, b_, c: (ix(a, b_, c)[0], row(a, b_, c)[0], 0)),
    )


def _ml_chunk_fwd(qs, k, v, gates, b_gate, pre, z, nw, sk, bsz, nc):
    r = qs.shape[0]
    sp = _ml_chunk_specs(nc, False, False)

    def body(qs_ref, k_ref, v_ref, gt_ref, bg_ref, pre_ref, z_ref, nw_ref, sk_ref,
             h_ref, yb_ref, cs_ref, ns_ref, ms_ref, c_s, n_s, m_s):
        hd, c = pl.program_id(1), pl.program_id(2)

        @pl.when(c == 0)
        def _():
            c_s[...] = jnp.zeros_like(c_s)
            n_s[...] = jnp.zeros_like(n_s)
            m_s[...] = jnp.zeros_like(m_s)

        cmat, nvec, m = c_s[...], n_s[...], m_s[...]
        cs_ref[0] = cmat
        ns_ref[0] = nvec
        ms_ref[0] = jnp.broadcast_to(m, (1, CHUNK))
        v_ = v_ref[...]
        co = _ml_core(gt_ref[...] + bg_ref[...], hd, c == 0, m, qs_ref[...], k_ref[...], v_, cmat, nvec)
        h = co["num"] / co["dd"]
        h_ref[...] = h
        hn, _ = _ml_headnorm(h)
        yb_ref[...] = _bf((hn * nw_ref[...] + sk_ref[...] * _silu(pre_ref[...])) * _silu(z_ref[...]))
        c_s[...] = co["decay"] * cmat + _dot(_bf(co["wk"]), v_, TN)
        n_s[...] = co["decay"] * nvec + jnp.sum(co["wk"], axis=0, keepdims=True)
        m_s[...] = co["m_new"]

    nst = bsz * ML_HEADS * nc
    return _pc(body, name="ml_chunk_fwd", grid=(bsz, ML_HEADS, nc),
               in_specs=[sp["hrow"]] * 3 + [sp["gates"], sp["bias"], sp["hrow"], sp["hrow"], sp["hvec"], sp["hvec"]],
               out_specs=[sp["hrow"], sp["hrow"], sp["cs"], sp["ns"], sp["ms"]],
               out_shape=[jax.ShapeDtypeStruct((r, ML_WIDTH), F32), jax.ShapeDtypeStruct((r, ML_WIDTH), BF16),
                          jax.ShapeDtypeStruct((nst, ML_DH, ML_DH), F32), jax.ShapeDtypeStruct((nst, 1, ML_DH), F32),
                          jax.ShapeDtypeStruct((nst, 1, CHUNK), F32)],
               scratch=[pltpu.VMEM((ML_DH, ML_DH), F32), pltpu.VMEM((1, ML_DH), F32), pltpu.VMEM((1, 1), F32)],
               vmem=6 << 20)(qs, k, v, gates, b_gate, pre, z, nw, sk)


def _ml_chunk_bwd(dyb, qs, k, v, gates, b_gate, pre, z, nw, sk, h, cs, ns, ms, bsz, nc, dep=None):
    r = qs.shape[0]
    sp = _ml_chunk_specs(nc, True, True)

    def body(dy_ref, qs_ref, k_ref, v_ref, gt_ref, bg_ref, pre_ref, z_ref, nw_ref, sk_ref, h_ref, cs_ref, ns_ref,
             ms_ref, dq_ref, dk_ref, dv_ref, dz_ref, dxc_ref, dgt_ref, dnw_ref, dsk_ref, dc_s, dn_s):
        hd, b, c = pl.program_id(0), pl.program_id(1), pl.program_id(2)

        @pl.when((b == 0) & (c == 0))
        def _():
            dnw_ref[...] = jnp.zeros_like(dnw_ref)
            dsk_ref[...] = jnp.zeros_like(dsk_ref)

        @pl.when(c == 0)
        def _():
            dc_s[...] = jnp.zeros_like(dc_s)
            dn_s[...] = jnp.zeros_like(dn_s)

        qs, k, v = qs_ref[...], k_ref[...], v_ref[...]
        cmat, nvec, m = cs_ref[0], ns_ref[0], ms_ref[0][:, :1]
        co = _ml_core(gt_ref[...] + bg_ref[...], hd, c == nc - 1, m, qs, k, v, cmat, nvec)
        lane, row = co["lane"], co["row"]
        wt, wprev, s, cb, qf = co["wt"], co["wprev"], co["s"], co["cb"], co["qf"]
        h = h_ref[...]
        hn, rstd = _ml_headnorm(h)
        xc = _silu(pre_ref[...])
        zv = z_ref[...]
        nw, sk = nw_ref[...], sk_ref[...]
        dy = dy_ref[...]
        dz_ref[...] = _bf(dy * (hn * nw + sk * xc) * _dsilu(zv))
        do = dy * _silu(zv)
        dsk_ref[...] += jnp.sum(do * xc, axis=0, keepdims=True)
        dnw_ref[...] += jnp.sum(do * hn, axis=0, keepdims=True)
        dxc_ref[...] = do * sk
        dhn = do * nw
        dh = rstd * (dhn - jnp.mean(dhn, axis=1, keepdims=True) - hn * jnp.mean(dhn * hn, axis=1, keepdims=True))
        rinv = 1.0 / co["dd"]
        dnum = dh * rinv
        ddd = -jnp.sum(dh * h, axis=1, keepdims=True) * rinv
        den = co["den"]
        dden = jnp.where(jnp.abs(den) >= co["emt"], ddd * jnp.sign(den), 0.0)
        dnb = _bf(dnum)
        ds = _dot(dnb, v, NT) + dden
        dv = _dot(_bf(s), dnb, TN)
        dnw_ = _bf(dnum * wprev)
        dwn = dden * wprev
        dqs = _dot(dnw_, cb, NT) + dwn * nvec
        dc_out = _dot(qs, dnw_, TN)
        dn_out = jnp.sum(dwn * qf, axis=0, keepdims=True)
        dwprev = jnp.sum(dnum * co["qc"], axis=1, keepdims=True) + dden * co["qn"]
        ds0 = _bf(ds * wt)
        ddm = ds * s
        dqs = dqs + _dot(ds0, k)
        dk = _dot(ds0, qs, TN)
        colc = jnp.sum(ddm.T, axis=1, keepdims=True)
        dbc = dwprev * wprev + jnp.sum(ddm, axis=1, keepdims=True) - colc
        dig = colc
        dcn, dnn = dc_s[...], dn_s[...]
        dcb = _bf(dcn)
        decay, e, kf, wk = co["decay"], co["e"], co["kf"], co["wk"]
        ddecay = (jnp.sum(jnp.sum(dcn * cmat, axis=1, keepdims=True), axis=0, keepdims=True)
                  + jnp.sum(dnn * nvec, axis=1, keepdims=True))
        dwk = _dot(v, dcb, NT) + dnn
        dv = dv + _dot(_bf(wk), dcb)
        dk = dk + e * dwk
        dg = jnp.sum(dwk * kf, axis=1, keepdims=True) * e
        dblast = ddecay * decay + jnp.sum(dg, axis=0, keepdims=True)
        dbc = dbc - dg + jnp.where(row[:, :1] == CHUNK - 1, dblast, 0.0)
        dig = dig + dg
        dc_s[...] = decay * dcn + dc_out
        dn_s[...] = decay * dnn + dn_out
        dlf = _cumsum_rows(jnp.broadcast_to(dbc, (CHUNK, CHUNK)), row, rev=True)[:, :1]
        dfp = dlf * (1.0 - jax.nn.sigmoid(co["fpc"]))
        dig = jnp.where(co["valid"], dig, 0.0)
        dfp = jnp.where(co["valid"], dfp, 0.0)
        dgt_ref[0] = jnp.where(lane == hd, dig, 0.0) + jnp.where(lane == hd + ML_HEADS, dfp, 0.0)
        dq_ref[...] = _bf(dqs * ML_SCALE)
        dk_ref[...] = _bf(dk)
        dv_ref[...] = _bf(dv)

    ob = jax.ShapeDtypeStruct((r, ML_WIDTH), BF16)
    return _pc(body, name="ml_chunk_bwd", grid=(ML_HEADS, bsz, nc),
               in_specs=[sp["hrow"]] * 4 + [sp["gates"], sp["bias"], sp["hrow"], sp["hrow"], sp["hvec"], sp["hvec"],
                                            sp["hrow"], sp["cs"], sp["ns"], sp["ms"]],
               out_specs=[sp["hrow"]] * 5 + [sp["dgates"], sp["hvec"], sp["hvec"]],
               out_shape=[ob, ob, ob, ob, jax.ShapeDtypeStruct((r, ML_WIDTH), F32),
                          jax.ShapeDtypeStruct((ML_HEADS, r, CHUNK), F32),
                          jax.ShapeDtypeStruct((1, ML_WIDTH), F32), jax.ShapeDtypeStruct((1, ML_WIDTH), F32)],
               scratch=[pltpu.VMEM((ML_DH, ML_DH), F32), pltpu.VMEM((1, ML_DH), F32)], vmem=8 << 20, dep=dep,
               )(dyb, qs, k, v, gates, b_gate, pre, z, nw, sk, h, cs, ns, ms)


def _ml_pre_bwd(dq, dk, dv, dgates, dxc_skip, pre, x, q, k, v, wq, wk, wv, wgq, wgk, wgv, bsz, nc):
    r = x.shape[0]
    hrow = pl.BlockSpec((CHUNK, ML_DH), lambda h, b, c: (b * nc + c, h))
    wexp = pl.BlockSpec((1, ML_DH, ML_DH), lambda h, b, c: (h, 0, 0))
    wcmp = pl.BlockSpec((1, ML_DH, CHUNK), lambda h, b, c: (h, 0, 0))
    wg = pl.BlockSpec((ML_DH, CHUNK), lambda h, b, c: (h, 0))
    dgs = pl.BlockSpec((ML_HEADS, CHUNK, CHUNK), lambda h, b, c: (0, b * nc + c, 0))
    bgs = pl.BlockSpec((1, 1, CHUNK), lambda h, b, c: (h, 0, 0))

    def body(dq_ref, dk_ref, dv_ref, dg_ref, dxs_ref, pre_ref, x_ref, q_ref, k_ref, v_ref, wq_ref, wk_ref, wv_ref,
             gq_ref, gk_ref, gv_ref, dpre_ref, dxv_ref, cq_ref, ck_ref, cv_ref, dgq_ref, dgk_ref, dgv_ref, dbg_ref,
             dwq_ref, dwk_ref, dwv_ref):
        b, c = pl.program_id(1), pl.program_id(2)

        @pl.when((b == 0) & (c == 0))
        def _():
            for ref in (dwq_ref, dwk_ref, dwv_ref, dgq_ref, dgk_ref, dgv_ref, dbg_ref):
                ref[...] = jnp.zeros_like(ref)

        dgt = dg_ref[0]
        for j in range(1, ML_HEADS):
            dgt = dgt + dg_ref[j]
        dbg_ref[0] += jnp.sum(dgt, axis=0, keepdims=True)
        dgb = _bf(dgt)
        dqt = _bf(dq_ref[...].astype(F32) + _dot(dgb, gq_ref[...], NT))
        dkt = _bf(dk_ref[...].astype(F32) + _dot(dgb, gk_ref[...], NT))
        dvt = _bf(dv_ref[...].astype(F32) + _dot(dgb, gv_ref[...], NT))
        dgq_ref[...] += _dot(q_ref[...], dgb, TN)
        dgk_ref[...] += _dot(k_ref[...], dgb, TN)
        dgv_ref[...] += _dot(v_ref[...], dgb, TN)
        prev = pre_ref[...]
        xcb = _bf(_silu(prev))
        xb = _bf(x_ref[...])
        dwq_ref[...] += _dot(xcb, dqt, TN)
        dwk_ref[...] += _dot(xcb, dkt, TN)
        dwv_ref[...] += _dot(xb, dvt, TN)
        dxc = _dot(dqt, wq_ref[0], NT) + _dot(dkt, wk_ref[0], NT) + dxs_ref[...]
        dpre_ref[...] = dxc * _dsilu(prev)
        dxv_ref[...] = _dot(dvt, wv_ref[0], NT)

        @pl.when((b == bsz - 1) & (c == nc - 1))
        def _():
            rr = lax.broadcasted_iota(jnp.int32, (ML_DH, ML_DH), 0)
            cc = lax.broadcasted_iota(jnp.int32, (ML_DH, ML_DH), 1)
            diag = rr // QKV_BLOCK == cc // QKV_BLOCK
            fc = lax.broadcasted_iota(jnp.int32, (ML_DH, CHUNK), 0)
            fo = lax.broadcasted_iota(jnp.int32, (ML_DH, CHUNK), 1)
            fold = jnp.where(fc % QKV_BLOCK == fo, 1.0, 0.0).astype(F32)
            for src, dst in ((dwq_ref, cq_ref), (dwk_ref, ck_ref), (dwv_ref, cv_ref)):
                dst[0] = jnp.dot(jnp.where(diag, src[...], 0.0), fold, precision=HI, preferred_element_type=F32)

    f = jax.ShapeDtypeStruct((r, ML_WIDTH), F32)
    wc = jax.ShapeDtypeStruct((ML_HEADS, ML_DH, CHUNK), F32)
    wgs = jax.ShapeDtypeStruct((ML_WIDTH, CHUNK), F32)
    return _pc(body, name="ml_pre_bwd", grid=(ML_HEADS, bsz, nc),
               in_specs=[hrow, hrow, hrow, dgs, hrow, hrow, hrow, hrow, hrow, hrow, wexp, wexp, wexp, wg, wg, wg],
               out_specs=[hrow, hrow, wcmp, wcmp, wcmp, wg, wg, wg, bgs],
               out_shape=[f, f, wc, wc, wc, wgs, wgs, wgs, jax.ShapeDtypeStruct((ML_HEADS, 1, CHUNK), F32)],
               scratch=[pltpu.VMEM((ML_DH, ML_DH), F32)] * 3,
               vmem=8 << 20)(dq, dk, dv, dgates, dxc_skip, pre, x, q, k, v, wq, wk, wv, wgq, wgk, wgv)


def _pad_lanes(w):
    return jnp.pad(w, ((0, 0), (0, CHUNK - w.shape[1])))


def _ml_weights(conv_w, conv_b, wq, wk, wv, w_gate, b_gate, norm_w, skip):
    return dict(
        conv_w=conv_w, conv_b=conv_b,
        wq=_bf(_headwise_expand(wq)), wk=_bf(_headwise_expand(wk)), wv=_bf(_headwise_expand(wv)),
        wgq=_bf(_pad_lanes(w_gate[:ML_WIDTH])), wgk=_bf(_pad_lanes(w_gate[ML_WIDTH:2 * ML_WIDTH])),
        wgv=_bf(_pad_lanes(w_gate[2 * ML_WIDTH:])), b_gate=_pad_lanes(b_gate.reshape(1, -1)),
        norm=norm_w.reshape(1, ML_WIDTH), skip=skip.reshape(1, ML_WIDTH))


def _ml_layer_fwd(x, z, w, bsz, nc):
    pre = _conv_fwd(x, w["conv_w"], w["conv_b"], bsz, nc, name="ml_conv")
    q, qs, k, v, gates = _ml_pre(pre, x, w["wq"], w["wk"], w["wv"], w["wgq"], w["wgk"], w["wgv"], bsz, nc)
    h, yb, cs, ns, ms = _ml_chunk_fwd(qs, k, v, gates, w["b_gate"], pre, z, w["norm"], w["skip"], bsz, nc)
    return yb, dict(pre=pre, q=q, qs=qs, k=k, v=v, gates=gates, h=h, cs=cs, ns=ns, ms=ms)


def _ml_layer_bwd(dyb, x, z, sv, w, bsz, nc, dep=None):
    dq, dk, dv, dz, dxc, dgates, dnw, dsk = _ml_chunk_bwd(
        dyb, sv["qs"], sv["k"], sv["v"], sv["gates"], w["b_gate"], sv["pre"], z, w["norm"], w["skip"], sv["h"],
        sv["cs"], sv["ns"], sv["ms"], bsz, nc, dep=dep)
    dpre, dxv, dwq, dwk, dwv, dgq, dgk, dgv, dbg = _ml_pre_bwd(
        dq, dk, dv, dgates, dxc, sv["pre"], x, sv["q"], sv["k"], sv["v"], w["wq"], w["wk"], w["wv"], w["wgq"],
        w["wgk"], w["wgv"], bsz, nc)
    dx, dcw, dcb = _conv_bwd(dpre, x, w["conv_w"], bsz, nc, name="ml_conv_bwd", add=dxv)
    ng = 2 * ML_HEADS
    grads = dict(
        ml_conv_w=dcw, ml_conv_b=dcb, ml_wq=_headwise_extract(dwq), ml_wk=_headwise_extract(dwk),
        ml_wv=_headwise_extract(dwv), ml_w_gate=jnp.concatenate([dgq[:, :ng], dgk[:, :ng], dgv[:, :ng]], axis=0),
        ml_b_gate=dbg[0][:, :ng], ml_norm=dnw, ml_skip=dsk)
    return dx, dz, grads


HI = lax.Precision.HIGHEST


def _softplus(x):
    return jnp.maximum(x, 0.0) + jnp.log(1.0 + jnp.exp(-jnp.abs(x)))


def _lane_cumsum(x, lane, rev=False):
    for k in range(7):
        s = 1 << k
        if rev:
            x = x + jnp.where(lane < CHUNK - s, pltpu.roll(x, CHUNK - s, 1), 0.0)
        else:
            x = x + jnp.where(lane >= s, pltpu.roll(x, s, 1), 0.0)
    return x


def _head_sum_matrix():
    r = lax.broadcasted_iota(jnp.int32, (SSD_HPG, SSD_GW), 0)
    l = lax.broadcasted_iota(jnp.int32, (SSD_HPG, SSD_GW), 1)
    return jnp.where(l // SSD_P == r, 1.0, 0.0).astype(F32)


def _ssd_core(xs, bm, cm, dt_raw, dt_bias, a_log, first):
    sq = (CHUNK, CHUNK)
    lane8 = lax.broadcasted_iota(jnp.int32, (SSD_HPG, CHUNK), 1)
    lane = lax.broadcasted_iota(jnp.int32, sq, 1)
    row = lax.broadcasted_iota(jnp.int32, sq, 0)
    low = lane < SSD_P
    valid = jnp.logical_or(jnp.logical_not(first), lane8 >= PAD_ROWS)
    pre = dt_raw + dt_bias
    dt = jnp.where(valid, _softplus(pre), 0.0)
    a = -jnp.exp(a_log)
    cum = _lane_cumsum(dt * a, lane8)
    cb = _dot(_bf(cm), _bf(bm), NT)
    heads = []
    for r in range(SSD_HPG):
        rowb = jnp.broadcast_to(cum[r:r + 1, :], sq)
        colb = rowb.T
        seg = jnp.exp(jnp.where(lane <= row, colb - rowb, NEG))
        dtrow = jnp.broadcast_to(dt[r:r + 1, :], sq)
        lastb = colb[CHUNK - 1:CHUNK, :]
        heads.append(dict(seg=seg, dtrow=dtrow, w=cb * seg * dtrow, ecol=jnp.exp(colb),
                          dec=jnp.exp(lastb - colb) * dtrow.T, elast=jnp.exp(lastb)))

    def pairs(key):
        return jnp.concatenate([jnp.where(low[:heads[0][key].shape[0]], heads[2 * j][key], heads[2 * j + 1][key])
                                for j in range(SSD_HPG // 2)], axis=1)

    return dict(lane8=lane8, low=low, valid=valid, pre=pre, dt=dt, a=a, cum=cum, cb=cb, heads=heads,
                expc=pairs("ecol"), dec=pairs("dec"), elast=pairs("elast"))


def _ssd_specs(nc, rev, group_major):
    def ix(a, b_, c):
        g, b = (a, b_) if group_major else (b_, a)
        return g, b, (nc - 1 - c) if rev else c

    def row(a, b_, c):
        g, b, cc = ix(a, b_, c)
        return b * nc + cc, g

    return dict(
        wide=pl.BlockSpec((CHUNK, SSD_GW), row),
        narrow=pl.BlockSpec((CHUNK, SSD_N), row),
        dtT=pl.BlockSpec((SSD_HPG, CHUNK), lambda a, b_, c: (ix(a, b_, c)[0], row(a, b_, c)[0])),
        hcol=pl.BlockSpec((SSD_HPG, 1), lambda a, b_, c: (ix(a, b_, c)[0], 0)),
        hacc=pl.BlockSpec((SSD_HPG, CHUNK), lambda a, b_, c: (ix(a, b_, c)[0], 0)),
        gvec=pl.BlockSpec((1, SSD_GW), lambda a, b_, c: (0, ix(a, b_, c)[0])),
        state=pl.BlockSpec((1, SSD_N, SSD_GW),
                           lambda a, b_, c: ((ix(a, b_, c)[1] * SSD_GROUPS + ix(a, b_, c)[0]) * nc + ix(a, b_, c)[2], 0, 0)),
    )


def _ssd_chunk_fwd(xs_pre, bm_pre, cm_pre, dt_raw, dt_bias, a_log, d_exp, z, gnorm, bsz, nc):
    r = xs_pre.shape[0]
    sp = _ssd_specs(nc, False, False)

    def body(xs_ref, bm_ref, cm_ref, dt_ref, db_ref, al_ref, d_ref, z_ref, gn_ref, y_ref, yn_ref, st_ref, st_s):
        c = pl.program_id(2)

        @pl.when(c == 0)
        def _():
            st_s[...] = jnp.zeros_like(st_s)

        state = st_s[...]
        st_ref[0] = state
        xs, bm, cm = _silu(xs_ref[...]), _silu(bm_ref[...]), _silu(cm_ref[...])
        co = _ssd_core(xs, bm, cm, dt_ref[...], db_ref[...], al_ref[...], c == 0)
        low, hd = co["low"], co["heads"]
        ys = []
        for j in range(SSD_HPG // 2):
            xp = xs[:, j * CHUNK:(j + 1) * CHUNK]
            lhs = jnp.concatenate([hd[2 * j]["w"], hd[2 * j + 1]["w"]], axis=1)
            rhs = jnp.concatenate([jnp.where(low, xp, 0.0), jnp.where(low, 0.0, xp)], axis=0)
            ys.append(_dot(_bf(lhs), _bf(rhs)))
        cmb = _bf(cm)
        y = jnp.concatenate(ys, axis=1) + co["expc"] * _dot(cmb, _bf(state)) + d_ref[...] * xs
        y_ref[...] = y
        yg = y * _silu(z_ref[...])
        rstd = lax.rsqrt(jnp.mean(yg * yg, axis=1, keepdims=True) + NORM_EPS)
        yn_ref[...] = _bf(yg * rstd * gn_ref[...])
        st_s[...] = co["elast"] * state + _dot(_bf(bm), _bf(xs * co["dec"]), TN)

    nst = bsz * SSD_GROUPS * nc
    return _pc(body, name="ssd_chunk_fwd", grid=(bsz, SSD_GROUPS, nc),
               in_specs=[sp["wide"], sp["narrow"], sp["narrow"], sp["dtT"], sp["hcol"], sp["hcol"], sp["gvec"],
                         sp["wide"], sp["gvec"]],
               out_specs=[sp["wide"], sp["wide"], sp["state"]],
               out_shape=[jax.ShapeDtypeStruct((r, SSD_INNER), F32), jax.ShapeDtypeStruct((r, SSD_INNER), BF16),
                          jax.ShapeDtypeStruct((nst, SSD_N, SSD_GW), F32)],
               scratch=[pltpu.VMEM((SSD_N, SSD_GW), F32)], vmem=6 << 20,
               )(xs_pre, bm_pre, cm_pre, dt_raw, dt_bias, a_log, d_exp, z, gnorm)


def _ssd_chunk_bwd(dyn, xs_pre, bm_pre, cm_pre, dt_raw, dt_bias, a_log, d_exp, z, gnorm, y, states, bsz, nc):
    r = xs_pre.shape[0]
    sp = _ssd_specs(nc, True, True)

    def body(dyn_ref, xs_ref, bm_ref, cm_ref, dt_ref, db_ref, al_ref, d_ref, z_ref, gn_ref, y_ref, st_ref,
             dxs_ref, dbm_ref, dcm_ref, dz_ref, ddt_ref, dgn_ref, dd_ref, dbias_ref, dal_ref, ds_s):
        b, c = pl.program_id(1), pl.program_id(2)

        @pl.when((b == 0) & (c == 0))
        def _():
            for ref in (dgn_ref, dd_ref, dbias_ref, dal_ref):
                ref[...] = jnp.zeros_like(ref)

        @pl.when(c == 0)
        def _():
            ds_s[...] = jnp.zeros_like(ds_s)

        xs_p, bm_p, cm_p = xs_ref[...], bm_ref[...], cm_ref[...]
        xs, bm, cm = _silu(xs_p), _silu(bm_p), _silu(cm_p)
        state = st_ref[0]
        co = _ssd_core(xs, bm, cm, dt_ref[...], db_ref[...], al_ref[...], c == nc - 1)
        low, hd, lane8, cb = co["low"], co["heads"], co["lane8"], co["cb"]
        dt, a, cum = co["dt"], co["a"], co["cum"]
        sub8 = lax.broadcasted_iota(jnp.int32, (SSD_HPG, CHUNK), 0)
        eh = _head_sum_matrix()

        def head_rows(full):
            return lax.dot_general(eh, full, NT, precision=HI, preferred_element_type=F32)

        def head_col(vec):
            return jnp.sum(eh * vec, axis=1, keepdims=True)

        yv, zv, gn = y_ref[...], z_ref[...], gn_ref[...]
        sz = _silu(zv)
        yg = yv * sz
        rstd = lax.rsqrt(jnp.mean(yg * yg, axis=1, keepdims=True) + NORM_EPS)
        yh = yg * rstd
        dyn = dyn_ref[...]
        dgn_ref[...] += jnp.sum(dyn * yh, axis=0, keepdims=True)
        dyh = dyn * gn
        dyg = rstd * (dyh - yh * jnp.mean(dyh * yh, axis=1, keepdims=True))
        dz_ref[...] = _bf(dyg * yv * _dsilu(zv))
        dy = dyg * sz
        dxs = dy * d_ref[...]
        dd_ref[...] += head_col(jnp.sum(dy * xs, axis=0, keepdims=True))
        cmb, bmb, stb = _bf(cm), _bf(bm), _bf(state)
        ysv = _dot(cmb, stb)
        expc = co["expc"]
        dys = _bf(dy * expc)
        dcum = head_rows(dy * ysv * expc)
        dcm = _dot(dys, stb, NT)
        dstate_out = _dot(cmb, dys, TN)
        dcb = jnp.zeros((CHUNK, CHUNK), F32)
        ddt = jnp.zeros((SSD_HPG, CHUNK), F32)
        dxs_pairs = []
        for j in range(SSD_HPG // 2):
            sl = slice(j * CHUNK, (j + 1) * CHUNK)
            dyp, xp = dy[:, sl], _bf(xs[:, sl])
            lhs = _bf(jnp.concatenate([hd[2 * j]["w"], hd[2 * j + 1]["w"]], axis=1))
            both = _dot(lhs, _bf(dyp), TN)
            dxs_pairs.append(jnp.where(low, both[:CHUNK], both[CHUNK:]))
            for q, msk in ((2 * j, low), (2 * j + 1, jnp.logical_not(low))):
                h = hd[q]
                dw = _dot(_bf(jnp.where(msk, dyp, 0.0)), xp, NT)
                dcb = dcb + dw * h["seg"] * h["dtrow"]
                e_ = dw * h["w"]
                dcum_r = jnp.sum(e_.T, axis=0, keepdims=True) - jnp.sum(e_, axis=0, keepdims=True)
                ddt_r = jnp.sum(dw * cb * h["seg"], axis=0, keepdims=True)
                dcum = dcum + jnp.where(sub8 == q, dcum_r, 0.0)
                ddt = ddt + jnp.where(sub8 == q, ddt_r, 0.0)
        dxs = dxs + jnp.concatenate(dxs_pairs, axis=1)
        dcbb = _bf(dcb)
        dcm = dcm + _dot(dcbb, bmb)
        dbm = _dot(dcbb, cmb, TN)
        dsn = ds_s[...]
        dsb = _bf(dsn)
        dec = co["dec"]
        dbm = dbm + _dot(_bf(xs * dec), dsb, NT)
        dxd = _dot(bmb, dsb)
        dxs = dxs + dxd * dec
        ddec = head_rows(dxd * xs)
        last = cum[:, CHUNK - 1:CHUNK]
        erow = jnp.exp(last - cum)
        ddt = ddt + ddec * erow
        dla = ddec * erow * dt
        dlast = (jnp.sum(dla, axis=1, keepdims=True)
                 + head_col(jnp.sum(dsn * state, axis=0, keepdims=True)) * jnp.exp(last))
        dcum = dcum - dla + jnp.where(lane8 == CHUNK - 1, dlast, 0.0)
        ds_s[...] = co["elast"] * dsn + dstate_out
        dda = _lane_cumsum(dcum, lane8, rev=True)
        ddt = jnp.where(co["valid"], ddt + dda * a, 0.0)
        ddt_raw = ddt * jax.nn.sigmoid(co["pre"])
        ddt_ref[...] = ddt_raw
        dbias_ref[...] += jnp.sum(ddt_raw, axis=1, keepdims=True)
        dal_ref[...] += jnp.sum(dda * dt, axis=1, keepdims=True) * a
        dxs_ref[...] = dxs * _dsilu(xs_p)
        dbm_ref[...] = dbm * _dsilu(bm_p)
        dcm_ref[...] = dcm * _dsilu(cm_p)

    st = jax.ShapeDtypeStruct
    hacc = st((SSD_HEADS, CHUNK), F32)
    return _pc(body, name="ssd_chunk_bwd", grid=(SSD_GROUPS, bsz, nc),
               in_specs=[sp["wide"], sp["wide"], sp["narrow"], sp["narrow"], sp["dtT"], sp["hcol"], sp["hcol"],
                         sp["gvec"], sp["wide"], sp["gvec"], sp["wide"], sp["state"]],
               out_specs=[sp["wide"], sp["narrow"], sp["narrow"], sp["wide"], sp["dtT"], sp["gvec"], sp["hacc"],
                          sp["hacc"], sp["hacc"]],
               out_shape=[st((r, SSD_INNER), F32), st((r, SSD_GROUPS * SSD_N), F32), st((r, SSD_GROUPS * SSD_N), F32),
                          st((r, SSD_INNER), BF16), st((SSD_HEADS, r), F32), st((1, SSD_INNER), F32), hacc, hacc, hacc],
               scratch=[pltpu.VMEM((SSD_N, SSD_GW), F32)], vmem=10 << 20,
               )(dyn, xs_pre, bm_pre, cm_pre, dt_raw, dt_bias, a_log, d_exp, z, gnorm, y, states)


SSD_BC = SSD_GROUPS * SSD_N


def _ssd_weights(conv_w, conv_b, dt_bias, a_log, d, gnorm):
    cuts = (0, SSD_INNER, SSD_INNER + SSD_BC, SSD_INNER + 2 * SSD_BC)
    return dict(
        conv_w=[conv_w[:, cuts[i]:cuts[i + 1]] for i in range(3)],
        conv_b=[conv_b[cuts[i]:cuts[i + 1]] for i in range(3)],
        dt_bias=dt_bias.reshape(SSD_HEADS, 1), a_log=a_log.reshape(SSD_HEADS, 1),
        d_exp=jnp.repeat(d.reshape(SSD_HEADS), SSD_P).reshape(1, SSD_INNER), gnorm=gnorm.reshape(1, SSD_INNER))


def _ssd_layer_fwd(z, xs_in, bm_in, cm_in, dt_rows, w, bsz, nc):
    pres = [_conv_fwd(a, w["conv_w"][i], w["conv_b"][i], bsz, nc, name=f"ssd_conv{i}")
            for i, a in enumerate((xs_in, bm_in, cm_in))]
    dt_t = dt_rows[:, :SSD_HEADS].T
    y, yn, states = _ssd_chunk_fwd(pres[0], pres[1], pres[2], dt_t, w["dt_bias"], w["a_log"], w["d_exp"], z,
                                   w["gnorm"], bsz, nc)
    return yn, dict(pres=pres, dt_t=dt_t, y=y, states=states)


def _ssd_layer_bwd(dyn, z, xs_in, bm_in, cm_in, sv, w, bsz, nc):
    pres = sv["pres"]
    dxs_p, dbm_p, dcm_p, dz, ddt_t, dgn, dd, dbias, dal = _ssd_chunk_bwd(
        dyn, pres[0], pres[1], pres[2], sv["dt_t"], w["dt_bias"], w["a_log"], w["d_exp"], z, w["gnorm"], sv["y"],
        sv["states"], bsz, nc)
    outs = [_conv_bwd(dp, a, w["conv_w"][i], bsz, nc, name=f"ssd_conv_bwd{i}")
            for i, (dp, a) in enumerate(((dxs_p, xs_in), (dbm_p, bm_in), (dcm_p, cm_in)))]
    ddt = _bf(_pad_lanes(ddt_t.T))
    grads = dict(
        ssd_conv_w=jnp.concatenate([o[1] for o in outs], axis=1),
        ssd_conv_b=jnp.concatenate([o[2] for o in outs], axis=1),
        ssd_dt_bias=dbias[:, 0], ssd_a_log=dal[:, 0], ssd_d=dd[:, 0], ssd_gnorm=dgn)
    return dz, outs[0][0], outs[1][0], outs[2][0], ddt, grads


WNAMES = ("meta_tokens", "ab_norm", "ab_w_in", "s5_lambda_re", "s5_lambda_im", "s5_log_dt", "s5_b_re", "s5_b_im",
          "s5_c_re", "s5_c_im", "s5_d", "s5_glu_w", "s5_glu_b", "ml_conv_w", "ml_conv_b", "ml_wq", "ml_wk", "ml_wv",
          "ml_w_gate", "ml_b_gate", "ml_norm", "ml_skip", "ab_w_out", "ssd_norm", "ssd_w_in", "ssd_conv_w",
          "ssd_conv_b", "ssd_dt_bias", "ssd_a_log", "ssd_d", "ssd_gnorm", "ssd_w_out", "final_norm")
SHARD_AXIS = dict(meta_tokens=1, ab_w_in=2, s5_glu_w=1, ml_conv_w=2, ml_wq=1, ml_wk=1, ml_wv=1, ml_w_gate=1,
                  ab_w_out=1, ssd_norm=1, ssd_w_in=2, ssd_conv_w=2, ssd_conv_b=1, ssd_gnorm=1, ssd_w_out=1)
BIG = ("ab_w_in", "s5_glu_w", "ab_w_out", "ssd_w_in", "ssd_w_out")
SMALL = tuple(n for n in WNAMES if n in SHARD_AXIS and n not in BIG)
REPL = tuple(n for n in WNAMES if n not in SHARD_AXIS)
PACK_ALIGN = 8 * 128


def _pack(arrs):
    lead = arrs[0][1]
    parts = []
    for a, nlead in arrs:
        f = a.reshape(a.shape[:nlead] + (-1,))
        f = jnp.pad(f, [(0, 0)] * nlead + [(0, (-f.shape[-1]) % PACK_ALIGN)])
        parts.append(f.reshape(f.shape[:nlead] + (-1, 128)))
    return jnp.concatenate(parts, axis=lead)


def _unpack(p, shapes):
    out, r0 = [], 0
    lead = p.shape[:-2]
    for s in shapes:
        n = math.prod(s)
        rows = -(-n // PACK_ALIGN) * 8
        seg = p[..., r0:r0 + rows, :].reshape(lead + (rows * 128,))[..., :n]
        out.append(seg.reshape(lead + tuple(s)))
        r0 += rows
    return out


def _assemble(g, axis):
    m = jnp.moveaxis(g, 0, axis)
    return m.reshape(m.shape[:axis] + (m.shape[axis] * m.shape[axis + 1],) + m.shape[axis + 2:])


def _split(full, axis):
    s = full.shape
    m = full.reshape(s[:axis] + (N_DEV, s[axis] // N_DEV) + s[axis + 1:])
    return jnp.moveaxis(m, axis, 0)


def kernel(x, *rest):
    nw = len(WNAMES)
    w = dict(zip(WNAMES, rest[:nw]))
    loss_target = rest[nw]
    mom = dict(zip(WNAMES, rest[nw + 1:2 * nw + 1]))
    var = dict(zip(WNAMES, rest[2 * nw + 1:3 * nw + 1]))
    bsz = x.shape[0]
    nc = 1 + SEQ // CHUNK
    tp = nc * CHUNK

    local = {n: _bf(w[n][0]) for n in BIG}
    small_local = _pack([(w[n], 0) for n in SMALL])
    ga = _exchange_start([local["ab_w_in"], small_local], ["ag", "ag"], name="gather_a")
    got_a = _exchange_wait(ga, ga["token"])
    gb = _exchange_start([local["s5_glu_w"], local["ab_w_out"]], ["ag", "ag"], name="gather_b", dep=got_a[1])
    gc = _exchange_start([local["ssd_w_in"], local["ssd_w_out"]], ["ag", "ag"], name="gather_c", dep=gb["token"])

    def assemble_big(n, got):
        return _assemble(got[:, None], SHARD_AXIS[n])[0]

    full = {"ab_w_in": assemble_big("ab_w_in", got_a[0])}
    for n, g in zip(SMALL, _unpack(got_a[1], [w[n].shape for n in SMALL])):
        full[n] = _assemble(g, SHARD_AXIS[n])[0] if n != "meta_tokens" else _assemble(g, SHARD_AXIS[n])
    for n in REPL:
        full[n] = w[n][0] if n != "final_norm" else w[n]
    cuts0 = (0, S5_WIDTH, 2 * S5_WIDTH, 2 * S5_WIDTH + ML_WIDTH, 2 * (S5_WIDTH + ML_WIDTH))
    w_in0 = [full["ab_w_in"][:, cuts0[i]:cuts0[i + 1]] for i in range(4)]
    glu_b = full["s5_glu_b"].reshape(1, S5_WIDTH)

    meta = jnp.broadcast_to(full["meta_tokens"][None], (bsz, N_META, D_MODEL))
    h0 = jnp.concatenate([jnp.zeros((bsz, PAD_ROWS, D_MODEL), F32), meta, x], axis=1).reshape(bsz * tp, D_MODEL)
    xn0 = _rms_fwd(h0, full["ab_norm"], name="rms0")
    u, za, xb, zb = [_mm(xn0, wi, "NN", name=f"in0_{i}") for i, wi in enumerate(w_in0)]
    s5p, s5_vjp = _s5_tables(*[full[n] for n in ("s5_lambda_re", "s5_lambda_im", "s5_log_dt", "s5_b_re", "s5_b_im",
                                                   "s5_c_re", "s5_c_im", "s5_d")])
    got_b = []

    def glu_w_after(scan_out):
        got_b.extend(_exchange_wait(gb, scan_out))
        return assemble_big("s5_glu_w", got_b[0])

    sv5 = _s5_layer_fwd(u, s5p, glu_w_after, bsz, nc)
    glu_w = assemble_big("s5_glu_w", got_b[0])
    w_out0 = assemble_big("ab_w_out", got_b[1])
    w_out0 = [w_out0[:S5_WIDTH], w_out0[S5_WIDTH:]]
    ya = _s5_post(sv5["y1"], sv5["glu_pre"], glu_b, za)
    mlw = _ml_weights(*[full[n] for n in ("ml_conv_w", "ml_conv_b", "ml_wq", "ml_wk", "ml_wv", "ml_w_gate",
                                           "ml_b_gate", "ml_norm", "ml_skip")])
    yb, svm = _ml_layer_fwd(xb, zb, mlw, bsz, nc)
    h1 = _mm(ya, w_out0[0], "NN", name="out0_a", add=h0)
    h1 = _mm(yb, w_out0[1], "NN", name="out0_b", add=h1)
    got_c = _exchange_wait(gc, h1)
    w_in1, w_out1 = assemble_big("ssd_w_in", got_c[0]), assemble_big("ssd_w_out", got_c[1])
    cuts1 = (0, SSD_INNER, 2 * SSD_INNER, 2 * SSD_INNER + SSD_BC, 2 * SSD_INNER + 2 * SSD_BC)
    w_in1 = [w_in1[:, cuts1[i]:cuts1[i + 1]] for i in range(4)] + [_pad_lanes(w_in1[:, cuts1[4]:])]
    xn1 = _rms_fwd(h1, full["ssd_norm"], name="rms1")
    z1, xs_in, bm_in, cm_in, dt_rows = [_mm(xn1, wi, "NN", name=f"in1_{i}") for i, wi in enumerate(w_in1)]
    ssdw = _ssd_weights(*[full[n] for n in ("ssd_conv_w", "ssd_conv_b", "ssd_dt_bias", "ssd_a_log", "ssd_d",
                                             "ssd_gnorm")])
    yn, svs = _ssd_layer_fwd(z1, xs_in, bm_in, cm_in, dt_rows, ssdw, bsz, nc)
    h2 = _mm(yn, w_out1, "NN", name="out1", add=h1)
    loss_part, dh2, dfinal = _final_loss(h2, full["final_norm"], loss_target, bsz, nc)
    loss = lax.psum(loss_part[0, 0], ("x", "y", "c"))

    g = {"final_norm": dfinal}
    dyn = _mm(dh2, w_out1, "NT", name="d_out1")
    g["ssd_w_out"] = _mm(yn, dh2, "TN", name="dw_out1", out_dtype=BF16)
    dz1, dxs, dbm, dcm, ddt, gs = _ssd_layer_bwd(dyn, z1, xs_in, bm_in, cm_in, svs, ssdw, bsz, nc)
    g.update(gs)
    dps1 = (dz1, dxs, dbm, dcm, ddt)
    dxn1 = None
    for i, (dp, wi) in enumerate(zip(dps1, w_in1)):
        dxn1 = _mm(dp, wi, "NT", name=f"d_in1_{i}", add=dxn1)
    dw1 = [_mm(xn1, dp, "TN", name=f"dw_in1_{i}", out_dtype=BF16) for i, dp in enumerate(dps1)]
    g["ssd_w_in"] = jnp.concatenate(dw1[:4] + [dw1[4][:, :SSD_HEADS]], axis=1)

    def local_shape(n):
        return w[n].shape

    def slabs(n):
        gf = g[n].reshape((1,) + tuple(g[n].shape)) if n != "meta_tokens" else g[n]
        full_shape = tuple(d * (N_DEV if i == SHARD_AXIS[n] else 1) for i, d in enumerate(local_shape(n)))
        return _split(gf.reshape(full_shape), SHARD_AXIS[n])

    x1 = _exchange_start([slabs("ssd_w_in")[:, 0], slabs("ssd_w_out")[:, 0]], ["a2a", "a2a"], name="grads_1")
    dh1, g["ssd_norm"] = _rms_bwd(h1, full["ssd_norm"], dxn1, dh2, name="rms1_bwd", dep=x1["token"])
    dya = _mm(dh1, w_out0[0], "NT", name="d_out0_a")
    dyb = _mm(dh1, w_out0[1], "NT", name="d_out0_b")
    g["ab_w_out"] = jnp.concatenate([_mm(ya, dh1, "TN", name="dw_out0_a", out_dtype=BF16),
                                     _mm(yb, dh1, "TN", name="dw_out0_b", out_dtype=BF16)], axis=0)
    du, dza, g5 = _s5_layer_bwd(dya, u, za, sv5, s5p, s5_vjp, glu_w, glu_b, bsz, nc)
    g.update(g5)
    x2 = _exchange_start([slabs("ab_w_out")[:, 0], _bf(slabs("s5_glu_w")[:, 0])], ["a2a", "a2a"], name="grads_2")
    dxb, dzb, gm = _ml_layer_bwd(dyb, xb, zb, svm, mlw, bsz, nc, dep=x2["token"])
    g.update(gm)
    dps0 = (du, dza, dxb, dzb)
    dxn0 = None
    for i, (dp, wi) in enumerate(zip(dps0, w_in0)):
        dxn0 = _mm(dp, wi, "NT", name=f"d_in0_{i}", add=dxn0)
    dw0 = [_mm(xn0, dp, "TN", name=f"dw_in0_{i}", out_dtype=BF16, tn=S5_WIDTH, slabs=True) for i, dp in enumerate(dps0)]
    dw_in0_slabs = jnp.concatenate(dw0, axis=0)
    dh0, g["ab_norm"] = _rms_bwd(h0, full["ab_norm"], dxn0, dh1, name="rms0_bwd")
    dh0 = dh0.reshape(bsz, tp, D_MODEL)
    grad_x = dh0[:, CHUNK:]
    g["meta_tokens"] = jnp.sum(dh0[:, PAD_ROWS:CHUNK], axis=0)

    small_g = _pack([(slabs(n), 1) for n in SMALL])
    repl_g = _pack([(g[n], 0) for n in REPL])
    x3 = _exchange_start([dw_in0_slabs, small_g, repl_g], ["a2a", "a2a", "ag"], name="grads_3")

    def update_big(n, gp):
        return _adamw(w[n][0], mom[n][0], var[n][0], gp, name=f"adamw_{n}")

    res = {}
    ex1 = _exchange_wait(x1, x3["token"])
    res["ssd_w_in"], res["ssd_w_out"] = update_big("ssd_w_in", ex1[0]), update_big("ssd_w_out", ex1[1])
    ex2 = _exchange_wait(x2, res["ssd_w_out"][0])
    res["ab_w_out"], res["s5_glu_w"] = update_big("ab_w_out", ex2[0]), update_big("s5_glu_w", ex2[1])
    ex3 = _exchange_wait(x3, res["s5_glu_w"][0])
    res["ab_w_in"] = update_big("ab_w_in", ex3[0])
    for names, gp, tag in ((SMALL, ex3[1], "small"), (REPL, ex3[2], "repl")):
        shapes = [local_shape(n) for n in names]
        packs = [_pack([(d[n], 0) for n in names]) for d in (w, mom, var)]
        outs = _adamw(packs[0], packs[1], packs[2], gp, name=f"adamw_{tag}")
        for k, o in enumerate(outs):
            for n, a in zip(names, _unpack(o, shapes)):
                res.setdefault(n, [None] * 4)[k] = a
    outs = [loss, grad_x]
    for k in range(4):
        outs += [res[n][k].reshape(local_shape(n)) for n in WNAMES]
    return tuple(outs)
```

```python
import functools
import math

import jax
import jax.numpy as jnp
from jax import lax
from jax.experimental import pallas as pl
from jax.experimental.pallas import tpu as pltpu

F32 = jnp.float32
BF16 = jnp.bfloat16

D_MODEL = 2048
SEQ = 2048
N_META = 16
CHUNK = 128
PAD_ROWS = CHUNK - N_META
NORM_EPS = 1e-6
HEAD_NORM_EPS = 1e-5
S5_WIDTH = 1024
S5_GROUPS = 64
S5_GROUP_SIZE = 16
S5_STATE = 64
S5_GB = 8
S5_LANES = S5_GB * S5_STATE
ML_WIDTH = 3072
ML_HEADS = 8
ML_DH = 384
ML_CONV = 4
QKV_BLOCK = 4
SSD_INNER = 4096
SSD_HEADS = 64
SSD_P = 64
SSD_N = 128
SSD_GROUPS = 8
SSD_HPG = 8
SSD_GW = SSD_HPG * SSD_P
N_DEV = 8
ADAM_LR, ADAM_B1, ADAM_B2, ADAM_EPS, ADAM_WD, ADAM_STEP = 0.001, 0.9, 0.999, 1e-08, 0.01, 10
NEG = -1e30
VMEM_CAP = 60 * 1024 * 1024
MESH = pl.DeviceIdType.MESH

NN = (((1,), (0,)), ((), ()))
NT = (((1,), (1,)), ((), ()))
TN = (((0,), (0,)), ((), ()))


def _dot(a, b, dims=NN):
    return lax.dot_general(a, b, dims, preferred_element_type=F32)


def _bf(x):
    return x.astype(BF16)


def _pick(n, cands):
    for c in cands:
        if n % c == 0:
            return c
    return n


def _nbytes(shape, dtype):
    return math.prod(shape) * jnp.dtype(dtype).itemsize


ANY_SPEC = pl.BlockSpec(memory_space=pl.ANY)


def _pc(body, *, name, grid, in_specs, out_specs, out_shape, scratch=(), vmem=None, dep=None):
    limit = None if vmem is None else int(min(VMEM_CAP, max(32 * 1024 * 1024, 2 * vmem + (8 << 20))))
    n_in = len(in_specs)
    if dep is not None:
        inner = body

        def body(*refs):
            inner(*refs[:n_in], *refs[n_in + 1:])

        in_specs = list(in_specs) + [ANY_SPEC]
    call = pl.pallas_call(
        body, name=name, grid=grid, in_specs=in_specs, out_specs=out_specs, out_shape=out_shape,
        scratch_shapes=list(scratch),
        compiler_params=pltpu.CompilerParams(dimension_semantics=("arbitrary",) * len(grid), vmem_limit_bytes=limit))
    return call if dep is None else (lambda *args: call(*args, dep))


def _silu(x):
    return x * jax.nn.sigmoid(x)


def _dsilu(x):
    s = jax.nn.sigmoid(x)
    return s * (1.0 + x * (1.0 - s))


def _gelu_and_grad(x):
    c0 = math.sqrt(2.0 / math.pi)
    inner = c0 * (x + 0.044715 * x * x * x)
    t = jnp.tanh(inner)
    g = 0.5 * x * (1.0 + t)
    dg = 0.5 * (1.0 + t) + 0.5 * x * (1.0 - t * t) * c0 * (1.0 + 3 * 0.044715 * x * x)
    return g, dg


def _mm(a, b, mode, *, name, add=None, out_dtype=F32, tn=None, slabs=False):
    if mode == "NN":
        (m, k), (k2, n) = a.shape, b.shape
    elif mode == "NT":
        (m, k), (n, k2) = a.shape, b.shape
    else:
        (k, m), (k2, n) = a.shape, b.shape
    assert k == k2, (a.shape, b.shape, mode)
    tm = _pick(m, (1088, 1024, 768, 512, 384, 256, 128))
    tn = tn or _pick(n, (512, 384, 256, 128))
    tk = _pick(k, (2048, 1088, 1024, 768, 512, 384, 256, 128))
    nk = k // tk
    dims = {"NN": NN, "NT": NT, "TN": TN}[mode]

    def body(*refs):
        a_ref, b_ref = refs[0], refs[1]
        add_ref = refs[2] if add is not None else None
        o_ref, acc_ref = refs[-2], refs[-1]
        kk = pl.program_id(2)

        @pl.when(kk == 0)
        def _():
            acc_ref[...] = jnp.zeros_like(acc_ref)

        acc_ref[...] += _dot(_bf(a_ref[...]), _bf(b_ref[...]), dims)

        @pl.when(kk == nk - 1)
        def _():
            r = acc_ref[...]
            if add_ref is not None:
                r = r + add_ref[...]
            o_ref[...] = r.reshape(o_ref.shape).astype(o_ref.dtype)

    if mode == "NN":
        a_spec = pl.BlockSpec((tm, tk), lambda i, j, kk: (i, kk))
        b_spec = pl.BlockSpec((tk, tn), lambda i, j, kk: (kk, j))
    elif mode == "NT":
        a_spec = pl.BlockSpec((tm, tk), lambda i, j, kk: (i, kk))
        b_spec = pl.BlockSpec((tn, tk), lambda i, j, kk: (j, kk))
    else:
        a_spec = pl.BlockSpec((tk, tm), lambda i, j, kk: (kk, i))
        b_spec = pl.BlockSpec((tk, tn), lambda i, j, kk: (kk, j))
    in_specs = [a_spec, b_spec]
    args = [a, b]
    if add is not None:
        in_specs.append(pl.BlockSpec((tm, tn), lambda i, j, kk: (i, j)))
        args.append(add)
    if slabs:
        out_shape = jax.ShapeDtypeStruct((n // tn, m, tn), out_dtype)
        out_spec = pl.BlockSpec((1, tm, tn), lambda i, j, kk: (j, i, 0))
    else:
        out_shape = jax.ShapeDtypeStruct((m, n), out_dtype)
        out_spec = pl.BlockSpec((tm, tn), lambda i, j, kk: (i, j))
    vmem = (_nbytes((tm, tk), a.dtype) + _nbytes((tk, tn), b.dtype) + _nbytes((tm, tn), out_dtype)
            + (_nbytes((tm, tn), F32) if add is not None else 0)) + _nbytes((tm, tn), F32) // 2
    return _pc(body, name=name, grid=(m // tm, n // tn, nk), in_specs=in_specs, out_specs=out_spec,
               out_shape=out_shape, scratch=[pltpu.VMEM((tm, tn), F32)], vmem=vmem)(*args)


def _rms_fwd(x, g, *, name):
    r, d = x.shape
    tm = _pick(r, (256, 128))

    def body(x_ref, g_ref, o_ref):
        xv = x_ref[...]
        rstd = lax.rsqrt(jnp.mean(xv * xv, axis=1, keepdims=True) + NORM_EPS)
        o_ref[...] = (xv * rstd * g_ref[...]).astype(o_ref.dtype)

    return _pc(body, name=name, grid=(r // tm,),
               in_specs=[pl.BlockSpec((tm, d), lambda i: (i, 0)), pl.BlockSpec((1, d), lambda i: (0, 0))],
               out_specs=pl.BlockSpec((tm, d), lambda i: (i, 0)), out_shape=jax.ShapeDtypeStruct((r, d), BF16),
               vmem=tm * d * 6)(x, g.reshape(1, d))


def _rms_bwd(x, g, dxn, dres, *, name, dep=None):
    r, d = x.shape
    tm = _pick(r, (256, 128))

    def body(x_ref, g_ref, dxn_ref, dres_ref, dx_ref, dg_ref):
        @pl.when(pl.program_id(0) == 0)
        def _():
            dg_ref[...] = jnp.zeros_like(dg_ref)

        xv = x_ref[...]
        rstd = lax.rsqrt(jnp.mean(xv * xv, axis=1, keepdims=True) + NORM_EPS)
        xh = xv * rstd
        dy = dxn_ref[...]
        dg_ref[...] += jnp.sum(dy * xh, axis=0, keepdims=True)
        dyg = dy * g_ref[...]
        dx_ref[...] = dres_ref[...] + rstd * (dyg - xh * jnp.mean(dyg * xh, axis=1, keepdims=True))

    row = pl.BlockSpec((tm, d), lambda i: (i, 0))
    vec = pl.BlockSpec((1, d), lambda i: (0, 0))
    return _pc(body, name=name, grid=(r // tm,), in_specs=[row, vec, row, row], out_specs=[row, vec],
               out_shape=[jax.ShapeDtypeStruct((r, d), F32), jax.ShapeDtypeStruct((1, d), F32)],
               vmem=tm * d * 16, dep=dep)(x, g.reshape(1, d), dxn, dres)


def _final_loss(h, g, target, bsz, nc):
    d = h.shape[1]

    def body(h_ref, g_ref, t_ref, loss_ref, dh_ref, dg_ref):
        b, c = pl.program_id(0), pl.program_id(1)

        @pl.when((b == 0) & (c == 0))
        def _():
            loss_ref[...] = jnp.zeros_like(loss_ref)
            dg_ref[...] = jnp.zeros_like(dg_ref)

        @pl.when(c == 0)
        def _():
            dh_ref[...] = jnp.zeros_like(dh_ref)

        @pl.when(c > 0)
        def _():
            xv = h_ref[...]
            rstd = lax.rsqrt(jnp.mean(xv * xv, axis=1, keepdims=True) + NORM_EPS)
            xh = xv * rstd
            gv = g_ref[...]
            err = xh * gv - t_ref[0]
            loss_ref[...] += 0.5 * jnp.sum(jnp.mean(err * err, axis=1, keepdims=True))
            dy = err * (1.0 / d)
            dg_ref[...] += jnp.sum(dy * xh, axis=0, keepdims=True)
            dyg = dy * gv
            dh_ref[...] = rstd * (dyg - xh * jnp.mean(dyg * xh, axis=1, keepdims=True))

    row = pl.BlockSpec((CHUNK, d), lambda b, c: (b * nc + c, 0))
    vec = pl.BlockSpec((1, d), lambda b, c: (0, 0))
    return _pc(body, name="final_loss", grid=(bsz, nc),
               in_specs=[row, vec, pl.BlockSpec((1, CHUNK, d), lambda b, c: (b, jnp.maximum(c - 1, 0), 0))],
               out_specs=[pl.BlockSpec((8, 128), lambda b, c: (0, 0)), row, vec],
               out_shape=[jax.ShapeDtypeStruct((8, 128), F32), jax.ShapeDtypeStruct(h.shape, F32),
                          jax.ShapeDtypeStruct((1, d), F32)],
               vmem=CHUNK * d * 16)(h, g.reshape(1, d), target)


def _adamw(w, m, v, gparts, *, name):
    r, c = w.shape
    tr = _pick(r, (256, 128)) if r * c * 4 > (1 << 20) else r

    def body(w_ref, m_ref, v_ref, gp_ref, g_ref, d_ref, nm_ref, nv_ref):
        g = gp_ref[0].astype(F32)
        for j in range(1, N_DEV):
            g = g + gp_ref[j].astype(F32)
        mm = ADAM_B1 * m_ref[...] + (1.0 - ADAM_B1) * g
        vv = ADAM_B2 * v_ref[...] + (1.0 - ADAM_B2) * (g * g)
        m_hat = mm / (1.0 - ADAM_B1 ** ADAM_STEP)
        v_hat = vv / (1.0 - ADAM_B2 ** ADAM_STEP)
        g_ref[...] = g
        d_ref[...] = -ADAM_LR * (m_hat / (jnp.sqrt(v_hat) + ADAM_EPS) + ADAM_WD * w_ref[...])
        nm_ref[...] = mm
        nv_ref[...] = vv

    blk = pl.BlockSpec((tr, c), lambda i: (i, 0))
    out = jax.ShapeDtypeStruct((r, c), F32)
    return _pc(body, name=name, grid=(r // tr,),
               in_specs=[blk, blk, blk, pl.BlockSpec((N_DEV, tr, c), lambda i: (0, i, 0))],
               out_specs=[blk, blk, blk, blk], out_shape=[out, out, out, out],
               vmem=tr * c * (4 * 7 + N_DEV * jnp.dtype(gparts.dtype).itemsize))(w, m, v, gparts)


PEERS = (1, 2, 4, 6, 3, 5, 7)
HBM_SPEC = pl.BlockSpec(memory_space=pltpu.HBM)
SEM_SPEC = pl.BlockSpec(memory_space=pltpu.SEMAPHORE)
SIDE_EFFECT = pltpu.SideEffectType.DATAFLOW_SIDE_EFFECTING


def _peer(p):
    x, y, c = lax.axis_index("x"), lax.axis_index("y"), lax.axis_index("c")
    tx, ty, tc = x ^ ((p >> 2) & 1), y ^ ((p >> 1) & 1), c ^ (p & 1)
    return (tx, ty, tc), 4 * tx + 2 * ty + tc


def _place_own(a, kind, *, name):
    rows, cols = a.shape[-2:]
    tr = _pick(rows, (512, 256, 128, 64, 32, 16))
    me = (4 * lax.axis_index("x") + 2 * lax.axis_index("y") + lax.axis_index("c")).astype(jnp.int32).reshape(1)

    def body(me_ref, in_ref, out_ref):
        out_ref[...] = in_ref[...].reshape(out_ref.shape)

    if kind == "a2a":
        in_spec = pl.BlockSpec((1, tr, cols), lambda i, me_ref: (me_ref[0], i, 0))
    else:
        in_spec = pl.BlockSpec((tr, cols), lambda i, me_ref: (i, 0))
    return pl.pallas_call(
        body, name=name, out_shape=jax.ShapeDtypeStruct((N_DEV, rows, cols), a.dtype),
        grid_spec=pltpu.PrefetchScalarGridSpec(
            num_scalar_prefetch=1, grid=(rows // tr,), in_specs=[in_spec],
            out_specs=pl.BlockSpec((1, tr, cols), lambda i, me_ref: (me_ref[0], i, 0))))(me, a)


def _exchange_copies(ins, lands, send_sems, recv_sems, kinds, incoming):
    me = 4 * lax.axis_index("x") + 2 * lax.axis_index("y") + lax.axis_index("c")
    copies = []
    for i, kind in enumerate(kinds):
        for p in PEERS:
            dev, tgt = _peer(p)
            k = i * (N_DEV - 1) + p - 1
            copies.append(pltpu.make_async_remote_copy(
                src_ref=ins[i].at[tgt] if kind == "a2a" else ins[i], dst_ref=lands[i].at[tgt if incoming else me],
                send_sem=send_sems.at[k], recv_sem=recv_sems.at[k], device_id=dev, device_id_type=MESH))
    return copies


def _exchange_start(arrays, kinds, *, name, dep=None):
    n = len(arrays)
    lands = [_place_own(a, k, name=f"{name}_own{i}") for i, (a, k) in enumerate(zip(arrays, kinds))]
    extra = [] if dep is None else [dep]

    def body(*refs):
        ins, lnd = refs[:n], refs[n:2 * n]
        send_sems, recv_sems = refs[2 * n + len(extra)], refs[2 * n + len(extra) + 1]
        token = refs[-1]
        for cp in _exchange_copies(ins, lnd, send_sems, recv_sems, kinds, False):
            cp.start()
        token[...] = jnp.zeros_like(token)

    sem = pltpu.SemaphoreType.DMA((n * (N_DEV - 1),))
    outs = pl.pallas_call(
        body, name=name, in_specs=[HBM_SPEC] * (2 * n) + [ANY_SPEC] * len(extra),
        out_specs=[SEM_SPEC, SEM_SPEC] + [HBM_SPEC] * (2 * n) + [pl.BlockSpec(memory_space=pltpu.VMEM)],
        out_shape=[sem, sem] + [pltpu.HBM(a.shape, a.dtype) for a in arrays + lands]
        + [jax.ShapeDtypeStruct((8, 128), F32)],
        input_output_aliases={i: 2 + i for i in range(2 * n)},
        compiler_params=pltpu.CompilerParams(has_side_effects=SIDE_EFFECT),
    )(*[pltpu.with_memory_space_constraint(a, pltpu.HBM) for a in arrays + lands], *extra)
    return dict(send=outs[0], recv=outs[1], ins=list(outs[2:2 + n]), lands=list(outs[2 + n:2 + 2 * n]),
                token=outs[-1], kinds=kinds, name=name)


def _exchange_wait(h, after):
    n = len(h["ins"])
    kinds = h["kinds"]

    def body(*refs):
        ins, lnd = refs[:n], refs[n:2 * n]
        send_sems, recv_sems = refs[2 * n], refs[2 * n + 1]
        copies = _exchange_copies(ins, lnd, send_sems, recv_sems, kinds, True)
        for cp in copies:
            cp.wait_recv()
        for cp in copies:
            cp.wait_send()

    arrs = h["ins"] + h["lands"]
    after = list(after) if isinstance(after, (list, tuple)) else [after]
    outs = pl.pallas_call(
        body, name=h["name"] + "_wait", in_specs=[HBM_SPEC] * (2 * n) + [SEM_SPEC, SEM_SPEC] + [ANY_SPEC] * len(after),
        out_specs=[HBM_SPEC] * (2 * n), out_shape=[pltpu.HBM(a.shape, a.dtype) for a in arrs],
        input_output_aliases={i: i for i in range(2 * n)},
        compiler_params=pltpu.CompilerParams(has_side_effects=SIDE_EFFECT),
    )(*arrs, h["send"], h["recv"], *after)
    return list(outs[n:])


def _s5_params(lam_re, lam_im, log_dt, b_re, b_im):
    dt = jnp.exp(log_dt)[:, None]
    mag = jnp.exp(lam_re * dt)
    ar, ai = mag * jnp.cos(lam_im * dt), mag * jnp.sin(lam_im * dt)
    den = lam_re * lam_re + lam_im * lam_im
    qr = ((ar - 1.0) * lam_re + ai * lam_im) / den
    qi = (ai * lam_re - (ar - 1.0) * lam_im) / den
    bbr = qr[..., None] * b_re - qi[..., None] * b_im
    bbi = qr[..., None] * b_im + qi[..., None] * b_re
    return ar, ai, bbr, bbi


def _s5_power_table(ar, ai):
    pr, pi = ar.reshape(1, -1), ai.reshape(1, -1)
    while pr.shape[0] < CHUNK:
        sr, si = pr[-1:], pi[-1:]
        pr, pi = (jnp.concatenate([pr, pr * sr - pi * si], axis=0), jnp.concatenate([pi, pr * si + pi * sr], axis=0))
    return pr, pi


def _blockdiag(w, rows, cols):
    w = w.reshape(S5_GB, S5_GB, rows, cols)
    eye = jnp.eye(S5_GB, dtype=w.dtype)
    return jnp.einsum("abrc,bd->abrdc", w, eye).reshape(S5_GB, S5_GB * rows, S5_GB * cols)


def _blockdiag_extract(w, rows, cols):
    w = w.reshape(S5_GB, S5_GB, rows, S5_GB, cols)
    return jnp.einsum("abrbc->abrc", w).reshape(S5_GROUPS, rows, cols)


def _s5_scan_specs(bsz, nc, rev):
    def chunk(b, c):
        return b * nc + ((nc - 1 - c) if rev else c)

    return dict(
        u=pl.BlockSpec((CHUNK, CHUNK), lambda g, b, c: (chunk(b, c), g)),
        x=pl.BlockSpec((CHUNK, S5_LANES), lambda g, b, c: (chunk(b, c), g)),
        wb=pl.BlockSpec((1, CHUNK, S5_LANES), lambda g, b, c: (g, 0, 0)),
        wc=pl.BlockSpec((1, S5_LANES, CHUNK), lambda g, b, c: (g, 0, 0)),
        tab=pl.BlockSpec((CHUNK, S5_LANES), lambda g, b, c: (0, g)),
        step=pl.BlockSpec((8, S5_LANES), lambda g, b, c: (0, g)),
        d=pl.BlockSpec((1, CHUNK), lambda g, b, c: (0, g)),
        lane=pl.BlockSpec((1, S5_LANES), lambda g, b, c: (0, g)),
        xprev=pl.BlockSpec((8, S5_LANES), lambda g, b, c: (jnp.maximum(chunk(b, c) * (CHUNK // 8) - 1, 0), g)),
    )


def _s5_fwd(u, wbr, wbi, pr, pi, sr, si, wcr, wci, d, bsz, nc):
    r = u.shape[0]
    sp = _s5_scan_specs(bsz, nc, False)

    def body(u_ref, wbr_ref, wbi_ref, pr_ref, pi_ref, sr_ref, si_ref, wcr_ref, wci_ref, d_ref,
             xr_ref, xi_ref, y1_ref, g_ref, cr_s, ci_s):
        @pl.when(pl.program_id(2) == 0)
        def _():
            cr_s[...] = jnp.zeros_like(cr_s)
            ci_s[...] = jnp.zeros_like(ci_s)

        uv = u_ref[...]
        ub = _bf(uv)
        xr, xi = _dot(ub, wbr_ref[0]), _dot(ub, wbi_ref[0])
        row = lax.broadcasted_iota(jnp.int32, (CHUNK, S5_LANES), 0)
        for k in range(7):
            s = 1 << k
            ar, ai = sr_ref[k:k + 1, :], si_ref[k:k + 1, :]
            hr = jnp.where(row >= s, pltpu.roll(xr, s, 0), 0.0)
            hi = jnp.where(row >= s, pltpu.roll(xi, s, 0), 0.0)
            xr, xi = xr + (ar * hr - ai * hi), xi + (ar * hi + ai * hr)
        cr, ci = cr_s[...], ci_s[...]
        tr, ti = pr_ref[...], pi_ref[...]
        xr, xi = xr + (tr * cr - ti * ci), xi + (tr * ci + ti * cr)
        cr_s[...] = xr[CHUNK - 1:CHUNK, :]
        ci_s[...] = xi[CHUNK - 1:CHUNK, :]
        xr_ref[...] = xr
        xi_ref[...] = xi
        y = _dot(_bf(xr), wcr_ref[0]) - _dot(_bf(xi), wci_ref[0]) + d_ref[...] * uv
        y1_ref[...] = y
        g_ref[...] = _bf(_gelu_and_grad(y)[0])

    return _pc(body, name="s5_fwd", grid=(S5_GB, bsz, nc),
               in_specs=[sp["u"], sp["wb"], sp["wb"], sp["tab"], sp["tab"], sp["step"], sp["step"], sp["wc"], sp["wc"],
                         sp["d"]],
               out_specs=[sp["x"], sp["x"], sp["u"], sp["u"]],
               out_shape=[jax.ShapeDtypeStruct((r, S5_GROUPS * S5_STATE), F32)] * 2
               + [jax.ShapeDtypeStruct((r, S5_WIDTH), F32), jax.ShapeDtypeStruct((r, S5_WIDTH), BF16)],
               scratch=[pltpu.VMEM((1, S5_LANES), F32)] * 2, vmem=4 << 20,
               )(u, wbr, wbi, pr, pi, sr, si, wcr, wci, d)


def _s5_post(y1, glu_pre, glu_b, z):
    r, w = y1.shape
    tm = _pick(r, (256, 128))

    def body(y_ref, p_ref, b_ref, z_ref, o_ref):
        g = _gelu_and_grad(y_ref[...])[0]
        o_ref[...] = _bf(g * jax.nn.sigmoid(p_ref[...] + b_ref[...]) * _silu(z_ref[...]))

    row = pl.BlockSpec((tm, w), lambda i: (i, 0))
    return _pc(body, name="s5_post", grid=(r // tm,), in_specs=[row, row, pl.BlockSpec((1, w), lambda i: (0, 0)), row],
               out_specs=row, out_shape=jax.ShapeDtypeStruct((r, w), BF16), vmem=tm * w * 16)(y1, glu_pre, glu_b, z)


def _s5_post_bwd(dya, y1, glu_pre, glu_b, z):
    r, w = y1.shape
    tm = _pick(r, (256, 128))

    def body(dy_ref, y_ref, p_ref, b_ref, z_ref, dz_ref, dp_ref, dg_ref, db_ref):
        @pl.when(pl.program_id(0) == 0)
        def _():
            db_ref[...] = jnp.zeros_like(db_ref)

        g = _gelu_and_grad(y_ref[...])[0]
        s = jax.nn.sigmoid(p_ref[...] + b_ref[...])
        zv = z_ref[...]
        dy = dy_ref[...]
        do = dy * _silu(zv)
        dz_ref[...] = _bf(dy * g * s * _dsilu(zv))
        dp = do * g * s * (1.0 - s)
        dp_ref[...] = _bf(dp)
        db_ref[...] += jnp.sum(dp, axis=0, keepdims=True)
        dg_ref[...] = do * s

    row = pl.BlockSpec((tm, w), lambda i: (i, 0))
    vec = pl.BlockSpec((1, w), lambda i: (0, 0))
    return _pc(body, name="s5_post_bwd", grid=(r // tm,), in_specs=[row, row, row, vec, row],
               out_specs=[row, row, row, vec],
               out_shape=[jax.ShapeDtypeStruct((r, w), BF16), jax.ShapeDtypeStruct((r, w), BF16),
                          jax.ShapeDtypeStruct((r, w), F32), jax.ShapeDtypeStruct((1, w), F32)],
               vmem=tm * w * 24)(dya, y1, glu_pre, glu_b, z)


def _s5_bwd(dg, y1, u, xr, xi, wbr, wbi, qr, qi, sr, si, wcr, wci, d, bsz, nc):
    r = u.shape[0]
    sp = _s5_scan_specs(bsz, nc, True)

    def body(dg_ref, y1_ref, u_ref, xr_ref, xi_ref, xpr_ref, xpi_ref, wbr_ref, wbi_ref, qr_ref, qi_ref, sr_ref, si_ref,
             wcr_ref, wci_ref, d_ref, du_ref, dd_ref, dwcr_ref, dwci_ref, dwbr_ref, dwbi_ref, dar_ref, dai_ref,
             cr_s, ci_s):
        b, c = pl.program_id(1), pl.program_id(2)

        @pl.when((b == 0) & (c == 0))
        def _():
            for ref in (dd_ref, dwcr_ref, dwci_ref, dwbr_ref, dwbi_ref, dar_ref, dai_ref):
                ref[...] = jnp.zeros_like(ref)

        @pl.when(c == 0)
        def _():
            cr_s[...] = jnp.zeros_like(cr_s)
            ci_s[...] = jnp.zeros_like(ci_s)

        uv = u_ref[...]
        ub = _bf(uv)
        dy = dg_ref[...] * _gelu_and_grad(y1_ref[...])[1]
        dd_ref[...] += jnp.sum(dy * uv, axis=0, keepdims=True)
        dyb = _bf(dy)
        xr, xi = xr_ref[...], xi_ref[...]
        dwcr_ref[0] += _dot(_bf(xr), dyb, TN)
        dwci_ref[0] -= _dot(_bf(xi), dyb, TN)
        lr, li = _dot(dyb, wcr_ref[0], NT), -_dot(dyb, wci_ref[0], NT)
        row = lax.broadcasted_iota(jnp.int32, (CHUNK, S5_LANES), 0)
        for k in range(7):
            s = 1 << k
            ar, ai = sr_ref[k:k + 1, :], si_ref[k:k + 1, :]
            hr = jnp.where(row < CHUNK - s, pltpu.roll(lr, CHUNK - s, 0), 0.0)
            hi = jnp.where(row < CHUNK - s, pltpu.roll(li, CHUNK - s, 0), 0.0)
            lr, li = lr + (ar * hr + ai * hi), li + (ar * hi - ai * hr)
        cr, ci = cr_s[...], ci_s[...]
        tr, ti = qr_ref[...], qi_ref[...]
        lr, li = lr + (tr * cr + ti * ci), li + (tr * ci - ti * cr)
        cr_s[...] = lr[0:1, :]
        ci_s[...] = li[0:1, :]
        lrb, lib = _bf(lr), _bf(li)
        du_ref[...] = _bf(_dot(lrb, wbr_ref[0], NT) + _dot(lib, wbi_ref[0], NT) + dy * d_ref[...])
        dwbr_ref[0] += _dot(ub, lrb, TN)
        dwbi_ref[0] += _dot(ub, lib, TN)
        first = c == nc - 1
        pr0 = jnp.where(first, 0.0, xpr_ref[7:8, :])
        pi0 = jnp.where(first, 0.0, xpi_ref[7:8, :])
        xpr = jnp.where(row == 0, pr0, pltpu.roll(xr, 1, 0))
        xpi = jnp.where(row == 0, pi0, pltpu.roll(xi, 1, 0))
        dar_ref[...] += jnp.sum(lr * xpr + li * xpi, axis=0, keepdims=True)
        dai_ref[...] += jnp.sum(li * xpr - lr * xpi, axis=0, keepdims=True)

    st = jax.ShapeDtypeStruct
    return _pc(body, name="s5_bwd", grid=(S5_GB, bsz, nc),
               in_specs=[sp["u"], sp["u"], sp["u"], sp["x"], sp["x"], sp["xprev"], sp["xprev"], sp["wb"], sp["wb"],
                         sp["tab"], sp["tab"], sp["step"], sp["step"], sp["wc"], sp["wc"], sp["d"]],
               out_specs=[sp["u"], sp["d"], sp["wc"], sp["wc"], sp["wb"], sp["wb"], sp["lane"], sp["lane"]],
               out_shape=[st((r, S5_WIDTH), BF16), st((1, S5_WIDTH), F32),
                          st((S5_GB, S5_LANES, CHUNK), F32), st((S5_GB, S5_LANES, CHUNK), F32),
                          st((S5_GB, CHUNK, S5_LANES), F32), st((S5_GB, CHUNK, S5_LANES), F32),
                          st((1, S5_GROUPS * S5_STATE), F32), st((1, S5_GROUPS * S5_STATE), F32)],
               scratch=[pltpu.VMEM((1, S5_LANES), F32)] * 2, vmem=6 << 20,
               )(dg, y1, u, xr, xi, xr, xi, wbr, wbi, qr, qi, sr, si, wcr, wci, d)


def _s5_layer_fwd(u, prm, glu_w, bsz, nc):
    xr, xi, y1, g = _s5_fwd(u, prm["wbr"], prm["wbi"], prm["pr"], prm["pi"], prm["sr"], prm["si"], prm["wcr"],
                            prm["wci"], prm["d"], bsz, nc)
    glu_pre = _mm(g, glu_w(y1) if callable(glu_w) else glu_w, "NN", name="s5_glu")
    return dict(xr=xr, xi=xi, y1=y1, g=g, glu_pre=glu_pre)


def _s5_layer_bwd(dya, u, z, sv, prm, pvjp, glu_w, glu_b, bsz, nc):
    dz, dglu, dg_direct, dglu_b = _s5_post_bwd(dya, sv["y1"], sv["glu_pre"], glu_b, z)
    dg = _mm(dglu, glu_w, "NT", name="s5_dg", add=dg_direct)
    dglu_w = _mm(sv["g"], dglu, "TN", name="s5_dglu_w")
    du, dd, dwcr, dwci, dwbr, dwbi, dar, dai = _s5_bwd(
        dg, sv["y1"], u, sv["xr"], sv["xi"], prm["wbr"], prm["wbi"], prm["qr"], prm["qi"], prm["sr"], prm["si"],
        prm["wcr"], prm["wci"], prm["d"], bsz, nc)
    dbbr = jnp.swapaxes(_blockdiag_extract(dwbr, S5_GROUP_SIZE, S5_STATE), 1, 2)
    dbbi = jnp.swapaxes(_blockdiag_extract(dwbi, S5_GROUP_SIZE, S5_STATE), 1, 2)
    dlr, dli, dldt, dbr, dbi = pvjp((dar.reshape(S5_GROUPS, S5_STATE), dai.reshape(S5_GROUPS, S5_STATE), dbbr, dbbi))
    grads = dict(
        s5_lambda_re=dlr, s5_lambda_im=dli, s5_log_dt=dldt, s5_b_re=dbr, s5_b_im=dbi,
        s5_c_re=jnp.swapaxes(_blockdiag_extract(dwcr, S5_STATE, S5_GROUP_SIZE), 1, 2),
        s5_c_im=jnp.swapaxes(_blockdiag_extract(dwci, S5_STATE, S5_GROUP_SIZE), 1, 2),
        s5_d=dd, s5_glu_w=dglu_w, s5_glu_b=dglu_b)
    return du, dz, grads


def _s5_tables(lam_re, lam_im, log_dt, b_re, b_im, c_re, c_im, d):
    (ar, ai, bbr, bbi), vjp = jax.vjp(_s5_params, lam_re, lam_im, log_dt, b_re, b_im)
    pr, pi = _s5_power_table(lax.stop_gradient(ar), lax.stop_gradient(ai))
    steps = [(1 << k) - 1 for k in range(8)]
    prm = dict(
        wbr=_bf(_blockdiag(jnp.swapaxes(bbr, 1, 2), S5_GROUP_SIZE, S5_STATE)),
        wbi=_bf(_blockdiag(jnp.swapaxes(bbi, 1, 2), S5_GROUP_SIZE, S5_STATE)),
        wcr=_bf(_blockdiag(jnp.swapaxes(c_re, 1, 2), S5_STATE, S5_GROUP_SIZE)),
        wci=_bf(_blockdiag(jnp.swapaxes(c_im, 1, 2), S5_STATE, S5_GROUP_SIZE)),
        pr=pr, pi=pi, qr=pr[::-1], qi=pi[::-1],
        sr=jnp.concatenate([pr[i:i + 1] for i in steps], axis=0),
        si=jnp.concatenate([pi[i:i + 1] for i in steps], axis=0), d=d.reshape(1, S5_WIDTH))
    return prm, vjp


def _tile16(p8):
    return jnp.concatenate([p8] * (CHUNK // 8), axis=0)


def _shift_down(x, halo, s, row):
    return jnp.where(row >= s, pltpu.roll(x, s, 0), pltpu.roll(halo, s, 0))


def _shift_up(x, halo, s, row):
    return jnp.where(row < CHUNK - s, pltpu.roll(x, CHUNK - s, 0), pltpu.roll(halo, CHUNK - s, 0))


def _conv_specs(nc, tw):
    def chunk(b, c):
        return b * nc + c

    return dict(
        x=pl.BlockSpec((CHUNK, tw), lambda j, b, c: (chunk(b, c), j)),
        prev=pl.BlockSpec((8, tw), lambda j, b, c: (jnp.maximum(chunk(b, c) * (CHUNK // 8) - 1, 0), j)),
        nxt=pl.BlockSpec((8, tw), lambda j, b, c: ((b * nc + jnp.minimum(c + 1, nc - 1)) * (CHUNK // 8), j)),
        w=pl.BlockSpec((ML_CONV, tw), lambda j, b, c: (0, j)),
        vec=pl.BlockSpec((1, tw), lambda j, b, c: (0, j)),
    )


def _conv_fwd(x, w, bias, bsz, nc, *, name):
    r, wd = x.shape
    tw = _pick(wd, (2048, 1536, 1024, 512, 384, 256, 128))
    sp = _conv_specs(nc, tw)

    def body(x_ref, p_ref, w_ref, b_ref, o_ref):
        c = pl.program_id(2)
        xv = x_ref[...]
        row = lax.broadcasted_iota(jnp.int32, xv.shape, 0)
        halo = jnp.where(c == 0, 0.0, _tile16(p_ref[...]))
        acc = b_ref[...] + w_ref[3:4, :] * xv
        for s in (1, 2, 3):
            acc = acc + w_ref[3 - s:4 - s, :] * _shift_down(xv, halo, s, row)
        o_ref[...] = acc

    return _pc(body, name=name, grid=(wd // tw, bsz, nc), in_specs=[sp["x"], sp["prev"], sp["w"], sp["vec"]],
               out_specs=sp["x"], out_shape=jax.ShapeDtypeStruct((r, wd), F32), vmem=CHUNK * tw * 16,
               )(x, x, w, bias.reshape(1, wd))


def _conv_bwd(dpre, x, w, bsz, nc, *, name, add=None):
    r, wd = x.shape
    tw = _pick(wd, (2048, 1536, 1024, 512, 384, 256, 128))
    sp = _conv_specs(nc, tw)

    def body(*refs):
        d_ref, n_ref, x_ref, p_ref, w_ref = refs[:5]
        add_ref = refs[5] if add is not None else None
        dx_ref, dw_ref, db_ref = refs[-3:]
        b, c = pl.program_id(1), pl.program_id(2)

        @pl.when((b == 0) & (c == 0))
        def _():
            dw_ref[...] = jnp.zeros_like(dw_ref)
            db_ref[...] = jnp.zeros_like(db_ref)

        dv, xv = d_ref[...], x_ref[...]
        row = lax.broadcasted_iota(jnp.int32, xv.shape, 0)
        dhalo = jnp.where(c == nc - 1, 0.0, _tile16(n_ref[...]))
        xhalo = jnp.where(c == 0, 0.0, _tile16(p_ref[...]))
        dx = w_ref[3:4, :] * dv
        for s in (1, 2, 3):
            dx = dx + w_ref[3 - s:4 - s, :] * _shift_up(dv, dhalo, s, row)
        if add_ref is not None:
            dx = dx + add_ref[...]
        dx_ref[...] = _bf(dx)
        db_ref[...] += jnp.sum(dv, axis=0, keepdims=True)
        dw_ref[3:4, :] += jnp.sum(dv * xv, axis=0, keepdims=True)
        for s in (1, 2, 3):
            dw_ref[3 - s:4 - s, :] += jnp.sum(dv * _shift_down(xv, xhalo, s, row), axis=0, keepdims=True)

    ins = [dpre, dpre, x, x, w] + ([add] if add is not None else [])
    specs = [sp["x"], sp["nxt"], sp["x"], sp["prev"], sp["w"]] + ([sp["x"]] if add is not None else [])
    return _pc(body, name=name, grid=(wd // tw, bsz, nc), in_specs=specs, out_specs=[sp["x"], sp["w"], sp["vec"]],
               out_shape=[jax.ShapeDtypeStruct((r, wd), BF16), jax.ShapeDtypeStruct((ML_CONV, wd), F32),
                          jax.ShapeDtypeStruct((1, wd), F32)], vmem=CHUNK * tw * 24)(*ins)


ML_SCALE = ML_DH ** -0.5


def _headwise_expand(w):
    tiled = jnp.tile(w.reshape(ML_HEADS, ML_DH, QKV_BLOCK), (1, 1, ML_DH // QKV_BLOCK))
    blk = jnp.arange(ML_DH) // QKV_BLOCK
    return jnp.where(blk[:, None] == blk[None, :], tiled, 0.0)


def _headwise_extract(w):
    return w[:, :, :QKV_BLOCK].reshape(ML_HEADS * ML_DH // QKV_BLOCK, QKV_BLOCK, QKV_BLOCK)


def _ml_pre(pre, x, wq, wk, wv, wgq, wgk, wgv, bsz, nc):
    r = x.shape[0]
    hrow = pl.BlockSpec((CHUNK, ML_DH), lambda b, c, h: (b * nc + c, h))
    wexp = pl.BlockSpec((1, ML_DH, ML_DH), lambda b, c, h: (h, 0, 0))
    wg = pl.BlockSpec((ML_DH, CHUNK), lambda b, c, h: (h, 0))
    gspec = pl.BlockSpec((CHUNK, CHUNK), lambda b, c, h: (b * nc + c, 0))

    def body(pre_ref, x_ref, wq_ref, wk_ref, wv_ref, gq_ref, gk_ref, gv_ref, q_ref, qs_ref, k_ref, v_ref, gt_ref):
        @pl.when(pl.program_id(2) == 0)
        def _():
            gt_ref[...] = jnp.zeros_like(gt_ref)

        xcb = _bf(_silu(pre_ref[...]))
        q = _dot(xcb, wq_ref[0])
        k = _dot(xcb, wk_ref[0])
        v = _dot(_bf(x_ref[...]), wv_ref[0])
        qb, kb, vb = _bf(q), _bf(k), _bf(v)
        q_ref[...] = qb
        qs_ref[...] = _bf(q * ML_SCALE)
        k_ref[...] = kb
        v_ref[...] = vb
        gt_ref[...] += _dot(qb, gq_ref[...]) + _dot(kb, gk_ref[...]) + _dot(vb, gv_ref[...])

    o = jax.ShapeDtypeStruct((r, ML_WIDTH), BF16)
    return _pc(body, name="ml_pre", grid=(bsz, nc, ML_HEADS),
               in_specs=[hrow, hrow, wexp, wexp, wexp, wg, wg, wg], out_specs=[hrow, hrow, hrow, hrow, gspec],
               out_shape=[o, o, o, o, jax.ShapeDtypeStruct((r, CHUNK), F32)], vmem=4 << 20,
               )(pre, x, wq, wk, wv, wgq, wgk, wgv)


def _cumsum_rows(x, row, rev=False):
    for k in range(7):
        s = 1 << k
        if rev:
            x = x + jnp.where(row < CHUNK - s, pltpu.roll(x, CHUNK - s, 0), 0.0)
        else:
            x = x + jnp.where(row >= s, pltpu.roll(x, s, 0), 0.0)
    return x


def _log_sigmoid(x):
    return jnp.minimum(x, 0.0) - jnp.log(1.0 + jnp.exp(-jnp.abs(x)))


def _ml_core(gates, hd, first, m, qs, k, v, cmat, nvec):
    sq = (CHUNK, CHUNK)
    lane = lax.broadcasted_iota(jnp.int32, sq, 1)
    row = lax.broadcasted_iota(jnp.int32, sq, 0)
    igc = jnp.sum(jnp.where(lane == hd, gates, 0.0), axis=1, keepdims=True)
    fpc = jnp.sum(jnp.where(lane == hd + ML_HEADS, gates, 0.0), axis=1, keepdims=True)
    valid = jnp.logical_or(jnp.logical_not(first), row[:, :1] >= PAD_ROWS)
    igc = jnp.where(valid, igc, NEG)
    lfc = jnp.where(valid, _log_sigmoid(fpc), 0.0)
    bcb = _cumsum_rows(jnp.broadcast_to(lfc, sq), row)
    igb = jnp.broadcast_to(igc, sq)
    dm = jnp.where(lane <= row, bcb - (bcb - igb).T, NEG)
    bc = bcb[:, :1]
    inter = bc + m
    mt = jnp.maximum(inter, jnp.max(dm, axis=1, keepdims=True))
    wt = jnp.exp(dm - mt)
    wprev = jnp.exp(inter - mt)
    s0 = _dot(qs, k, NT)
    s = s0 * wt
    cb = _bf(cmat)
    qc = _dot(qs, cb)
    qf = qs.astype(F32)
    qn = jnp.sum(qf * nvec, axis=1, keepdims=True)
    num = _dot(_bf(s), v) + wprev * qc
    den = jnp.sum(s, axis=1, keepdims=True) + wprev * qn
    emt = jnp.exp(-mt)
    dd = jnp.maximum(jnp.abs(den), emt)
    blast = bcb[CHUNK - 1:CHUNK, :1]
    g = blast - bc + igc
    m_new = jnp.maximum(blast + m, jnp.max(g, axis=0, keepdims=True))
    decay = jnp.exp(blast + m - m_new)
    e = jnp.exp(g - m_new)
    kf = k.astype(F32)
    wk = e * kf
    return dict(lane=lane, row=row, fpc=fpc, valid=valid, wt=wt, wprev=wprev, s=s, cb=cb, qc=qc, qf=qf, qn=qn,
                num=num, den=den, emt=emt, dd=dd, m_new=m_new, decay=decay, e=e, kf=kf, wk=wk)


def _ml_headnorm(h):
    mu = jnp.mean(h, axis=1, keepdims=True)
    hc = h - mu
    rstd = lax.rsqrt(jnp.mean(hc * hc, axis=1, keepdims=True) + HEAD_NORM_EPS)
    return hc * rstd, rstd


def _ml_chunk_specs(nc, rev, head_major):
    def ix(a, b_, c):
        hd, b = (a, b_) if head_major else (b_, a)
        return hd, b, (nc - 1 - c) if rev else c

    def row(a, b_, c):
        hd, b, cc = ix(a, b_, c)
        return b * nc + cc, hd

    def st(a, b_, c):
        hd, b, cc = ix(a, b_, c)
        return (b * ML_HEADS + hd) * nc + cc

    return dict(
        hrow=pl.BlockSpec((CHUNK, ML_DH), row),
        gates=pl.BlockSpec((CHUNK, CHUNK), lambda a, b_, c: (row(a, b_, c)[0], 0)),
        bias=pl.BlockSpec((1, CHUNK), lambda a, b_, c: (0, 0)),
        hvec=pl.BlockSpec((1, ML_DH), lambda a, b_, c: (0, ix(a, b_, c)[0])),
        cs=pl.BlockSpec((1, ML_DH, ML_DH), lambda a, b_, c: (st(a, b_, c), 0, 0)),
        ns=pl.BlockSpec((1, 1, ML_DH), lambda a, b_, c: (st(a, b_, c), 0, 0)),
        ms=pl.BlockSpec((1, 1, CHUNK), lambda a, b_, c: (st(a, b_, c), 0, 0)),
        dgates=pl.BlockSpec((1, CHUNK, CHUNK), lambda a, b_, c: (ix(a, b_, c)[0], row(a, b_, c)[0], 0)),
    )


def _ml_chunk_fwd(qs, k, v, gates, b_gate, pre, z, nw, sk, bsz, nc):
    r = qs.shape[0]
    sp = _ml_chunk_specs(nc, False, False)

    def body(qs_ref, k_ref, v_ref, gt_ref, bg_ref, pre_ref, z_ref, nw_ref, sk_ref,
             h_ref, yb_ref, cs_ref, ns_ref, ms_ref, c_s, n_s, m_s):
        hd, c = pl.program_id(1), pl.program_id(2)

        @pl.when(c == 0)
        def _():
            c_s[...] = jnp.zeros_like(c_s)
            n_s[...] = jnp.zeros_like(n_s)
            m_s[...] = jnp.zeros_like(m_s)

        cmat, nvec, m = c_s[...], n_s[...], m_s[...]
        cs_ref[0] = cmat
        ns_ref[0] = nvec
        ms_ref[0] = jnp.broadcast_to(m, (1, CHUNK))
        v_ = v_ref[...]
        co = _ml_core(gt_ref[...] + bg_ref[...], hd, c == 0, m, qs_ref[...], k_ref[...], v_, cmat, nvec)
        h = co["num"] / co["dd"]
        h_ref[...] = h
        hn, _ = _ml_headnorm(h)
        yb_ref[...] = _bf((hn * nw_ref[...] + sk_ref[...] * _silu(pre_ref[...])) * _silu(z_ref[...]))
        c_s[...] = co["decay"] * cmat + _dot(_bf(co["wk"]), v_, TN)
        n_s[...] = co["decay"] * nvec + jnp.sum(co["wk"], axis=0, keepdims=True)
        m_s[...] = co["m_new"]

    nst = bsz * ML_HEADS * nc
    return _pc(body, name="ml_chunk_fwd", grid=(bsz, ML_HEADS, nc),
               in_specs=[sp["hrow"]] * 3 + [sp["gates"], sp["bias"], sp["hrow"], sp["hrow"], sp["hvec"], sp["hvec"]],
               out_specs=[sp["hrow"], sp["hrow"], sp["cs"], sp["ns"], sp["ms"]],
               out_shape=[jax.ShapeDtypeStruct((r, ML_WIDTH), F32), jax.ShapeDtypeStruct((r, ML_WIDTH), BF16),
                          jax.ShapeDtypeStruct((nst, ML_DH, ML_DH), F32), jax.ShapeDtypeStruct((nst, 1, ML_DH), F32),
                          jax.ShapeDtypeStruct((nst, 1, CHUNK), F32)],
               scratch=[pltpu.VMEM((ML_DH, ML_DH), F32), pltpu.VMEM((1, ML_DH), F32), pltpu.VMEM((1, 1), F32)],
               vmem=6 << 20)(qs, k, v, gates, b_gate, pre, z, nw, sk)


def _ml_chunk_bwd(dyb, qs, k, v, gates, b_gate, pre, z, nw, sk, h, cs, ns, ms, bsz, nc, dep=None):
    r = qs.shape[0]
    sp = _ml_chunk_specs(nc, True, True)

    def body(dy_ref, qs_ref, k_ref, v_ref, gt_ref, bg_ref, pre_ref, z_ref, nw_ref, sk_ref, h_ref, cs_ref, ns_ref,
             ms_ref, dq_ref, dk_ref, dv_ref, dz_ref, dxc_ref, dgt_ref, dnw_ref, dsk_ref, dc_s, dn_s):
        hd, b, c = pl.program_id(0), pl.program_id(1), pl.program_id(2)

        @pl.when((b == 0) & (c == 0))
        def _():
            dnw_ref[...] = jnp.zeros_like(dnw_ref)
            dsk_ref[...] = jnp.zeros_like(dsk_ref)

        @pl.when(c == 0)
        def _():
            dc_s[...] = jnp.zeros_like(dc_s)
            dn_s[...] = jnp.zeros_like(dn_s)

        qs, k, v = qs_ref[...], k_ref[...], v_ref[...]
        cmat, nvec, m = cs_ref[0], ns_ref[0], ms_ref[0][:, :1]
        co = _ml_core(gt_ref[...] + bg_ref[...], hd, c == nc - 1, m, qs, k, v, cmat, nvec)
        lane, row = co["lane"], co["row"]
        wt, wprev, s, cb, qf = co["wt"], co["wprev"], co["s"], co["cb"], co["qf"]
        h = h_ref[...]
        hn, rstd = _ml_headnorm(h)
        xc = _silu(pre_ref[...])
        zv = z_ref[...]
        nw, sk = nw_ref[...], sk_ref[...]
        dy = dy_ref[...]
        dz_ref[...] = _bf(dy * (hn * nw + sk * xc) * _dsilu(zv))
        do = dy * _silu(zv)
        dsk_ref[...] += jnp.sum(do * xc, axis=0, keepdims=True)
        dnw_ref[...] += jnp.sum(do * hn, axis=0, keepdims=True)
        dxc_ref[...] = do * sk
        dhn = do * nw
        dh = rstd * (dhn - jnp.mean(dhn, axis=1, keepdims=True) - hn * jnp.mean(dhn * hn, axis=1, keepdims=True))
        rinv = 1.0 / co["dd"]
        dnum = dh * rinv
        ddd = -jnp.sum(dh * h, axis=1, keepdims=True) * rinv
        den = co["den"]
        dden = jnp.where(jnp.abs(den) >= co["emt"], ddd * jnp.sign(den), 0.0)
        dnb = _bf(dnum)
        ds = _dot(dnb, v, NT) + dden
        dv = _dot(_bf(s), dnb, TN)
        dnw_ = _bf(dnum * wprev)
        dwn = dden * wprev
        dqs = _dot(dnw_, cb, NT) + dwn * nvec
        dc_out = _dot(qs, dnw_, TN)
        dn_out = jnp.sum(dwn * qf, axis=0, keepdims=True)
        dwprev = jnp.sum(dnum * co["qc"], axis=1, keepdims=True) + dden * co["qn"]
        ds0 = _bf(ds * wt)
        ddm = ds * s
        dqs = dqs + _dot(ds0, k)
        dk = _dot(ds0, qs, TN)
        colc = jnp.sum(ddm.T, axis=1, keepdims=True)
        dbc = dwprev * wprev + jnp.sum(ddm, axis=1, keepdims=True) - colc
        dig = colc
        dcn, dnn = dc_s[...], dn_s[...]
        dcb = _bf(dcn)
        decay, e, kf, wk = co["decay"], co["e"], co["kf"], co["wk"]
        ddecay = (jnp.sum(jnp.sum(dcn * cmat, axis=1, keepdims=True), axis=0, keepdims=True)
                  + jnp.sum(dnn * nvec, axis=1, keepdims=True))
        dwk = _dot(v, dcb, NT) + dnn
        dv = dv + _dot(_bf(wk), dcb)
        dk = dk + e * dwk
        dg = jnp.sum(dwk * kf, axis=1, keepdims=True) * e
        dblast = ddecay * decay + jnp.sum(dg, axis=0, keepdims=True)
        dbc = dbc - dg + jnp.where(row[:, :1] == CHUNK - 1, dblast, 0.0)
        dig = dig + dg
        dc_s[...] = decay * dcn + dc_out
        dn_s[...] = decay * dnn + dn_out
        dlf = _cumsum_rows(jnp.broadcast_to(dbc, (CHUNK, CHUNK)), row, rev=True)[:, :1]
        dfp = dlf * (1.0 - jax.nn.sigmoid(co["fpc"]))
        dig = jnp.where(co["valid"], dig, 0.0)
        dfp = jnp.where(co["valid"], dfp, 0.0)
        dgt_ref[0] = jnp.where(lane == hd, dig, 0.0) + jnp.where(lane == hd + ML_HEADS, dfp, 0.0)
        dq_ref[...] = _bf(dqs * ML_SCALE)
        dk_ref[...] = _bf(dk)
        dv_ref[...] = _bf(dv)

    ob = jax.ShapeDtypeStruct((r, ML_WIDTH), BF16)
    return _pc(body, name="ml_chunk_bwd", grid=(ML_HEADS, bsz, nc),
               in_specs=[sp["hrow"]] * 4 + [sp["gates"], sp["bias"], sp["hrow"], sp["hrow"], sp["hvec"], sp["hvec"],
                                            sp["hrow"], sp["cs"], sp["ns"], sp["ms"]],
               out_specs=[sp["hrow"]] * 5 + [sp["dgates"], sp["hvec"], sp["hvec"]],
               out_shape=[ob, ob, ob, ob, jax.ShapeDtypeStruct((r, ML_WIDTH), F32),
                          jax.ShapeDtypeStruct((ML_HEADS, r, CHUNK), F32),
                          jax.ShapeDtypeStruct((1, ML_WIDTH), F32), jax.ShapeDtypeStruct((1, ML_WIDTH), F32)],
               scratch=[pltpu.VMEM((ML_DH, ML_DH), F32), pltpu.VMEM((1, ML_DH), F32)], vmem=8 << 20, dep=dep,
               )(dyb, qs, k, v, gates, b_gate, pre, z, nw, sk, h, cs, ns, ms)


def _ml_pre_bwd(dq, dk, dv, dgates, dxc_skip, pre, x, q, k, v, wq, wk, wv, wgq, wgk, wgv, bsz, nc):
    r = x.shape[0]
    hrow = pl.BlockSpec((CHUNK, ML_DH), lambda h, b, c: (b * nc + c, h))
    wexp = pl.BlockSpec((1, ML_DH, ML_DH), lambda h, b, c: (h, 0, 0))
    wcmp = pl.BlockSpec((1, ML_DH, CHUNK), lambda h, b, c: (h, 0, 0))
    wg = pl.BlockSpec((ML_DH, CHUNK), lambda h, b, c: (h, 0))
    dgs = pl.BlockSpec((ML_HEADS, CHUNK, CHUNK), lambda h, b, c: (0, b * nc + c, 0))
    bgs = pl.BlockSpec((1, 1, CHUNK), lambda h, b, c: (h, 0, 0))

    def body(dq_ref, dk_ref, dv_ref, dg_ref, dxs_ref, pre_ref, x_ref, q_ref, k_ref, v_ref, wq_ref, wk_ref, wv_ref,
             gq_ref, gk_ref, gv_ref, dpre_ref, dxv_ref, cq_ref, ck_ref, cv_ref, dgq_ref, dgk_ref, dgv_ref, dbg_ref,
             dwq_ref, dwk_ref, dwv_ref):
        b, c = pl.program_id(1), pl.program_id(2)

        @pl.when((b == 0) & (c == 0))
        def _():
            for ref in (dwq_ref, dwk_ref, dwv_ref, dgq_ref, dgk_ref, dgv_ref, dbg_ref):
                ref[...] = jnp.zeros_like(ref)

        dgt = dg_ref[0]
        for j in range(1, ML_HEADS):
            dgt = dgt + dg_ref[j]
        dbg_ref[0] += jnp.sum(dgt, axis=0, keepdims=True)
        dgb = _bf(dgt)
        dqt = _bf(dq_ref[...].astype(F32) + _dot(dgb, gq_ref[...], NT))
        dkt = _bf(dk_ref[...].astype(F32) + _dot(dgb, gk_ref[...], NT))
        dvt = _bf(dv_ref[...].astype(F32) + _dot(dgb, gv_ref[...], NT))
        dgq_ref[...] += _dot(q_ref[...], dgb, TN)
        dgk_ref[...] += _dot(k_ref[...], dgb, TN)
        dgv_ref[...] += _dot(v_ref[...], dgb, TN)
        prev = pre_ref[...]
        xcb = _bf(_silu(prev))
        xb = _bf(x_ref[...])
        dwq_ref[...] += _dot(xcb, dqt, TN)
        dwk_ref[...] += _dot(xcb, dkt, TN)
        dwv_ref[...] += _dot(xb, dvt, TN)
        dxc = _dot(dqt, wq_ref[0], NT) + _dot(dkt, wk_ref[0], NT) + dxs_ref[...]
        dpre_ref[...] = dxc * _dsilu(prev)
        dxv_ref[...] = _dot(dvt, wv_ref[0], NT)

        @pl.when((b == bsz - 1) & (c == nc - 1))
        def _():
            rr = lax.broadcasted_iota(jnp.int32, (ML_DH, ML_DH), 0)
            cc = lax.broadcasted_iota(jnp.int32, (ML_DH, ML_DH), 1)
            diag = rr // QKV_BLOCK == cc // QKV_BLOCK
            fc = lax.broadcasted_iota(jnp.int32, (ML_DH, CHUNK), 0)
            fo = lax.broadcasted_iota(jnp.int32, (ML_DH, CHUNK), 1)
            fold = jnp.where(fc % QKV_BLOCK == fo, 1.0, 0.0).astype(F32)
            for src, dst in ((dwq_ref, cq_ref), (dwk_ref, ck_ref), (dwv_ref, cv_ref)):
                dst[0] = jnp.dot(jnp.where(diag, src[...], 0.0), fold, precision=HI, preferred_element_type=F32)

    f = jax.ShapeDtypeStruct((r, ML_WIDTH), F32)
    wc = jax.ShapeDtypeStruct((ML_HEADS, ML_DH, CHUNK), F32)
    wgs = jax.ShapeDtypeStruct((ML_WIDTH, CHUNK), F32)
    return _pc(body, name="ml_pre_bwd", grid=(ML_HEADS, bsz, nc),
               in_specs=[hrow, hrow, hrow, dgs, hrow, hrow, hrow, hrow, hrow, hrow, wexp, wexp, wexp, wg, wg, wg],
               out_specs=[hrow, hrow, wcmp, wcmp, wcmp, wg, wg, wg, bgs],
               out_shape=[f, f, wc, wc, wc, wgs, wgs, wgs, jax.ShapeDtypeStruct((ML_HEADS, 1, CHUNK), F32)],
               scratch=[pltpu.VMEM((ML_DH, ML_DH), F32)] * 3,
               vmem=8 << 20)(dq, dk, dv, dgates, dxc_skip, pre, x, q, k, v, wq, wk, wv, wgq, wgk, wgv)


def _pad_lanes(w):
    return jnp.pad(w, ((0, 0), (0, CHUNK - w.shape[1])))


def _ml_weights(conv_w, conv_b, wq, wk, wv, w_gate, b_gate, norm_w, skip):
    return dict(
        conv_w=conv_w, conv_b=conv_b,
        wq=_bf(_headwise_expand(wq)), wk=_bf(_headwise_expand(wk)), wv=_bf(_headwise_expand(wv)),
        wgq=_bf(_pad_lanes(w_gate[:ML_WIDTH])), wgk=_bf(_pad_lanes(w_gate[ML_WIDTH:2 * ML_WIDTH])),
        wgv=_bf(_pad_lanes(w_gate[2 * ML_WIDTH:])), b_gate=_pad_lanes(b_gate.reshape(1, -1)),
        norm=norm_w.reshape(1, ML_WIDTH), skip=skip.reshape(1, ML_WIDTH))


def _ml_layer_fwd(x, z, w, bsz, nc):
    pre = _conv_fwd(x, w["conv_w"], w["conv_b"], bsz, nc, name="ml_conv")
    q, qs, k, v, gates = _ml_pre(pre, x, w["wq"], w["wk"], w["wv"], w["wgq"], w["wgk"], w["wgv"], bsz, nc)
    h, yb, cs, ns, ms = _ml_chunk_fwd(qs, k, v, gates, w["b_gate"], pre, z, w["norm"], w["skip"], bsz, nc)
    return yb, dict(pre=pre, q=q, qs=qs, k=k, v=v, gates=gates, h=h, cs=cs, ns=ns, ms=ms)


def _ml_layer_bwd(dyb, x, z, sv, w, bsz, nc, dep=None):
    dq, dk, dv, dz, dxc, dgates, dnw, dsk = _ml_chunk_bwd(
        dyb, sv["qs"], sv["k"], sv["v"], sv["gates"], w["b_gate"], sv["pre"], z, w["norm"], w["skip"], sv["h"],
        sv["cs"], sv["ns"], sv["ms"], bsz, nc, dep=dep)
    dpre, dxv, dwq, dwk, dwv, dgq, dgk, dgv, dbg = _ml_pre_bwd(
        dq, dk, dv, dgates, dxc, sv["pre"], x, sv["q"], sv["k"], sv["v"], w["wq"], w["wk"], w["wv"], w["wgq"],
        w["wgk"], w["wgv"], bsz, nc)
    dx, dcw, dcb = _conv_bwd(dpre, x, w["conv_w"], bsz, nc, name="ml_conv_bwd", add=dxv)
    ng = 2 * ML_HEADS
    grads = dict(
        ml_conv_w=dcw, ml_conv_b=dcb, ml_wq=_headwise_extract(dwq), ml_wk=_headwise_extract(dwk),
        ml_wv=_headwise_extract(dwv), ml_w_gate=jnp.concatenate([dgq[:, :ng], dgk[:, :ng], dgv[:, :ng]], axis=0),
        ml_b_gate=dbg[0][:, :ng], ml_norm=dnw, ml_skip=dsk)
    return dx, dz, grads


HI = lax.Precision.HIGHEST


def _softplus(x):
    return jnp.maximum(x, 0.0) + jnp.log(1.0 + jnp.exp(-jnp.abs(x)))


def _lane_cumsum(x, lane, rev=False):
    for k in range(7):
        s = 1 << k
        if rev:
            x = x + jnp.where(lane < CHUNK - s, pltpu.roll(x, CHUNK - s, 1), 0.0)
        else:
            x = x + jnp.where(lane >= s, pltpu.roll(x, s, 1), 0.0)
    return x


def _head_sum_matrix():
    r = lax.broadcasted_iota(jnp.int32, (SSD_HPG, SSD_GW), 0)
    l = lax.broadcasted_iota(jnp.int32, (SSD_HPG, SSD_GW), 1)
    return jnp.where(l // SSD_P == r, 1.0, 0.0).astype(F32)


def _ssd_core(xs, bm, cm, dt_raw, dt_bias, a_log, first):
    sq = (CHUNK, CHUNK)
    lane8 = lax.broadcasted_iota(jnp.int32, (SSD_HPG, CHUNK), 1)
    lane = lax.broadcasted_iota(jnp.int32, sq, 1)
    row = lax.broadcasted_iota(jnp.int32, sq, 0)
    low = lane < SSD_P
    valid = jnp.logical_or(jnp.logical_not(first), lane8 >= PAD_ROWS)
    pre = dt_raw + dt_bias
    dt = jnp.where(valid, _softplus(pre), 0.0)
    a = -jnp.exp(a_log)
    cum = _lane_cumsum(dt * a, lane8)
    cb = _dot(_bf(cm), _bf(bm), NT)
    heads = []
    for r in range(SSD_HPG):
        rowb = jnp.broadcast_to(cum[r:r + 1, :], sq)
        colb = rowb.T
        seg = jnp.exp(jnp.where(lane <= row, colb - rowb, NEG))
        dtrow = jnp.broadcast_to(dt[r:r + 1, :], sq)
        lastb = colb[CHUNK - 1:CHUNK, :]
        heads.append(dict(seg=seg, dtrow=dtrow, w=cb * seg * dtrow, ecol=jnp.exp(colb),
                          dec=jnp.exp(lastb - colb) * dtrow.T, elast=jnp.exp(lastb)))

    def pairs(key):
        return jnp.concatenate([jnp.where(low[:heads[0][key].shape[0]], heads[2 * j][key], heads[2 * j + 1][key])
                                for j in range(SSD_HPG // 2)], axis=1)

    return dict(lane8=lane8, low=low, valid=valid, pre=pre, dt=dt, a=a, cum=cum, cb=cb, heads=heads,
                expc=pairs("ecol"), dec=pairs("dec"), elast=pairs("elast"))


def _ssd_specs(nc, rev, group_major):
    def ix(a, b_, c):
        g, b = (a, b_) if group_major else (b_, a)
        return g, b, (nc - 1 - c) if rev else c

    def row(a, b_, c):
        g, b, cc = ix(a, b_, c)
        return b * nc + cc, g

    return dict(
        wide=pl.BlockSpec((CHUNK, SSD_GW), row),
        narrow=pl.BlockSpec((CHUNK, SSD_N), row),
        dtT=pl.BlockSpec((SSD_HPG, CHUNK), lambda a, b_, c: (ix(a, b_, c)[0], row(a, b_, c)[0])),
        hcol=pl.BlockSpec((SSD_HPG, 1), lambda a, b_, c: (ix(a, b_, c)[0], 0)),
        hacc=pl.BlockSpec((SSD_HPG, CHUNK), lambda a, b_, c: (ix(a, b_, c)[0], 0)),
        gvec=pl.BlockSpec((1, SSD_GW), lambda a, b_, c: (0, ix(a, b_, c)[0])),
        state=pl.BlockSpec((1, SSD_N, SSD_GW),
                           lambda a, b_, c: ((ix(a, b_, c)[1] * SSD_GROUPS + ix(a, b_, c)[0]) * nc + ix(a, b_, c)[2], 0, 0)),
    )


def _ssd_chunk_fwd(xs_pre, bm_pre, cm_pre, dt_raw, dt_bias, a_log, d_exp, z, gnorm, bsz, nc):
    r = xs_pre.shape[0]
    sp = _ssd_specs(nc, False, False)

    def body(xs_ref, bm_ref, cm_ref, dt_ref, db_ref, al_ref, d_ref, z_ref, gn_ref, y_ref, yn_ref, st_ref, st_s):
        c = pl.program_id(2)

        @pl.when(c == 0)
        def _():
            st_s[...] = jnp.zeros_like(st_s)

        state = st_s[...]
        st_ref[0] = state
        xs, bm, cm = _silu(xs_ref[...]), _silu(bm_ref[...]), _silu(cm_ref[...])
        co = _ssd_core(xs, bm, cm, dt_ref[...], db_ref[...], al_ref[...], c == 0)
        low, hd = co["low"], co["heads"]
        ys = []
        for j in range(SSD_HPG // 2):
            xp = xs[:, j * CHUNK:(j + 1) * CHUNK]
            lhs = jnp.concatenate([hd[2 * j]["w"], hd[2 * j + 1]["w"]], axis=1)
            rhs = jnp.concatenate([jnp.where(low, xp, 0.0), jnp.where(low, 0.0, xp)], axis=0)
            ys.append(_dot(_bf(lhs), _bf(rhs)))
        cmb = _bf(cm)
        y = jnp.concatenate(ys, axis=1) + co["expc"] * _dot(cmb, _bf(state)) + d_ref[...] * xs
        y_ref[...] = y
        yg = y * _silu(z_ref[...])
        rstd = lax.rsqrt(jnp.mean(yg * yg, axis=1, keepdims=True) + NORM_EPS)
        yn_ref[...] = _bf(yg * rstd * gn_ref[...])
        st_s[...] = co["elast"] * state + _dot(_bf(bm), _bf(xs * co["dec"]), TN)

    nst = bsz * SSD_GROUPS * nc
    return _pc(body, name="ssd_chunk_fwd", grid=(bsz, SSD_GROUPS, nc),
               in_specs=[sp["wide"], sp["narrow"], sp["narrow"], sp["dtT"], sp["hcol"], sp["hcol"], sp["gvec"],
                         sp["wide"], sp["gvec"]],
               out_specs=[sp["wide"], sp["wide"], sp["state"]],
               out_shape=[jax.ShapeDtypeStruct((r, SSD_INNER), F32), jax.ShapeDtypeStruct((r, SSD_INNER), BF16),
                          jax.ShapeDtypeStruct((nst, SSD_N, SSD_GW), F32)],
               scratch=[pltpu.VMEM((SSD_N, SSD_GW), F32)], vmem=6 << 20,
               )(xs_pre, bm_pre, cm_pre, dt_raw, dt_bias, a_log, d_exp, z, gnorm)


def _ssd_chunk_bwd(dyn, xs_pre, bm_pre, cm_pre, dt_raw, dt_bias, a_log, d_exp, z, gnorm, y, states, bsz, nc):
    r = xs_pre.shape[0]
    sp = _ssd_specs(nc, True, True)

    def body(dyn_ref, xs_ref, bm_ref, cm_ref, dt_ref, db_ref, al_ref, d_ref, z_ref, gn_ref, y_ref, st_ref,
             dxs_ref, dbm_ref, dcm_ref, dz_ref, ddt_ref, dgn_ref, dd_ref, dbias_ref, dal_ref, ds_s):
        b, c = pl.program_id(1), pl.program_id(2)

        @pl.when((b == 0) & (c == 0))
        def _():
            for ref in (dgn_ref, dd_ref, dbias_ref, dal_ref):
                ref[...] = jnp.zeros_like(ref)

        @pl.when(c == 0)
        def _():
            ds_s[...] = jnp.zeros_like(ds_s)

        xs_p, bm_p, cm_p = xs_ref[...], bm_ref[...], cm_ref[...]
        xs, bm, cm = _silu(xs_p), _silu(bm_p), _silu(cm_p)
        state = st_ref[0]
        co = _ssd_core(xs, bm, cm, dt_ref[...], db_ref[...], al_ref[...], c == nc - 1)
        low, hd, lane8, cb = co["low"], co["heads"], co["lane8"], co["cb"]
        dt, a, cum = co["dt"], co["a"], co["cum"]
        sub8 = lax.broadcasted_iota(jnp.int32, (SSD_HPG, CHUNK), 0)
        eh = _head_sum_matrix()

        def head_rows(full):
            return lax.dot_general(eh, full, NT, precision=HI, preferred_element_type=F32)

        def head_col(vec):
            return jnp.sum(eh * vec, axis=1, keepdims=True)

        yv, zv, gn = y_ref[...], z_ref[...], gn_ref[...]
        sz = _silu(zv)
        yg = yv * sz
        rstd = lax.rsqrt(jnp.mean(yg * yg, axis=1, keepdims=True) + NORM_EPS)
        yh = yg * rstd
        dyn = dyn_ref[...]
        dgn_ref[...] += jnp.sum(dyn * yh, axis=0, keepdims=True)
        dyh = dyn * gn
        dyg = rstd * (dyh - yh * jnp.mean(dyh * yh, axis=1, keepdims=True))
        dz_ref[...] = _bf(dyg * yv * _dsilu(zv))
        dy = dyg * sz
        dxs = dy * d_ref[...]
        dd_ref[...] += head_col(jnp.sum(dy * xs, axis=0, keepdims=True))
        cmb, bmb, stb = _bf(cm), _bf(bm), _bf(state)
        ysv = _dot(cmb, stb)
        expc = co["expc"]
        dys = _bf(dy * expc)
        dcum = head_rows(dy * ysv * expc)
        dcm = _dot(dys, stb, NT)
        dstate_out = _dot(cmb, dys, TN)
        dcb = jnp.zeros((CHUNK, CHUNK), F32)
        ddt = jnp.zeros((SSD_HPG, CHUNK), F32)
        dxs_pairs = []
        for j in range(SSD_HPG // 2):
            sl = slice(j * CHUNK, (j + 1) * CHUNK)
            dyp, xp = dy[:, sl], _bf(xs[:, sl])
            lhs = _bf(jnp.concatenate([hd[2 * j]["w"], hd[2 * j + 1]["w"]], axis=1))
            both = _dot(lhs, _bf(dyp), TN)
            dxs_pairs.append(jnp.where(low, both[:CHUNK], both[CHUNK:]))
            for q, msk in ((2 * j, low), (2 * j + 1, jnp.logical_not(low))):
                h = hd[q]
                dw = _dot(_bf(jnp.where(msk, dyp, 0.0)), xp, NT)
                dcb = dcb + dw * h["seg"] * h["dtrow"]
                e_ = dw * h["w"]
                dcum_r = jnp.sum(e_.T, axis=0, keepdims=True) - jnp.sum(e_, axis=0, keepdims=True)
                ddt_r = jnp.sum(dw * cb * h["seg"], axis=0, keepdims=True)
                dcum = dcum + jnp.where(sub8 == q, dcum_r, 0.0)
                ddt = ddt + jnp.where(sub8 == q, ddt_r, 0.0)
        dxs = dxs + jnp.concatenate(dxs_pairs, axis=1)
        dcbb = _bf(dcb)
        dcm = dcm + _dot(dcbb, bmb)
        dbm = _dot(dcbb, cmb, TN)
        dsn = ds_s[...]
        dsb = _bf(dsn)
        dec = co["dec"]
        dbm = dbm + _dot(_bf(xs * dec), dsb, NT)
        dxd = _dot(bmb, dsb)
        dxs = dxs + dxd * dec
        ddec = head_rows(dxd * xs)
        last = cum[:, CHUNK - 1:CHUNK]
        erow = jnp.exp(last - cum)
        ddt = ddt + ddec * erow
        dla = ddec * erow * dt
        dlast = (jnp.sum(dla, axis=1, keepdims=True)
                 + head_col(jnp.sum(dsn * state, axis=0, keepdims=True)) * jnp.exp(last))
        dcum = dcum - dla + jnp.where(lane8 == CHUNK - 1, dlast, 0.0)
        ds_s[...] = co["elast"] * dsn + dstate_out
        dda = _lane_cumsum(dcum, lane8, rev=True)
        ddt = jnp.where(co["valid"], ddt + dda * a, 0.0)
        ddt_raw = ddt * jax.nn.sigmoid(co["pre"])
        ddt_ref[...] = ddt_raw
        dbias_ref[...] += jnp.sum(ddt_raw, axis=1, keepdims=True)
        dal_ref[...] += jnp.sum(dda * dt, axis=1, keepdims=True) * a
        dxs_ref[...] = dxs * _dsilu(xs_p)
        dbm_ref[...] = dbm * _dsilu(bm_p)
        dcm_ref[...] = dcm * _dsilu(cm_p)

    st = jax.ShapeDtypeStruct
    hacc = st((SSD_HEADS, CHUNK), F32)
    return _pc(body, name="ssd_chunk_bwd", grid=(SSD_GROUPS, bsz, nc),
               in_specs=[sp["wide"], sp["wide"], sp["narrow"], sp["narrow"], sp["dtT"], sp["hcol"], sp["hcol"],
                         sp["gvec"], sp["wide"], sp["gvec"], sp["wide"], sp["state"]],
               out_specs=[sp["wide"], sp["narrow"], sp["narrow"], sp["wide"], sp["dtT"], sp["gvec"], sp["hacc"],
                          sp["hacc"], sp["hacc"]],
               out_shape=[st((r, SSD_INNER), F32), st((r, SSD_GROUPS * SSD_N), F32), st((r, SSD_GROUPS * SSD_N), F32),
                          st((r, SSD_INNER), BF16), st((SSD_HEADS, r), F32), st((1, SSD_INNER), F32), hacc, hacc, hacc],
               scratch=[pltpu.VMEM((SSD_N, SSD_GW), F32)], vmem=10 << 20,
               )(dyn, xs_pre, bm_pre, cm_pre, dt_raw, dt_bias, a_log, d_exp, z, gnorm, y, states)


SSD_BC = SSD_GROUPS * SSD_N


def _ssd_weights(conv_w, conv_b, dt_bias, a_log, d, gnorm):
    cuts = (0, SSD_INNER, SSD_INNER + SSD_BC, SSD_INNER + 2 * SSD_BC)
    return dict(
        conv_w=[conv_w[:, cuts[i]:cuts[i + 1]] for i in range(3)],
        conv_b=[conv_b[cuts[i]:cuts[i + 1]] for i in range(3)],
        dt_bias=dt_bias.reshape(SSD_HEADS, 1), a_log=a_log.reshape(SSD_HEADS, 1),
        d_exp=jnp.repeat(d.reshape(SSD_HEADS), SSD_P).reshape(1, SSD_INNER), gnorm=gnorm.reshape(1, SSD_INNER))


def _ssd_layer_fwd(z, xs_in, bm_in, cm_in, dt_rows, w, bsz, nc):
    pres = [_conv_fwd(a, w["conv_w"][i], w["conv_b"][i], bsz, nc, name=f"ssd_conv{i}")
            for i, a in enumerate((xs_in, bm_in, cm_in))]
    dt_t = dt_rows[:, :SSD_HEADS].T
    y, yn, states = _ssd_chunk_fwd(pres[0], pres[1], pres[2], dt_t, w["dt_bias"], w["a_log"], w["d_exp"], z,
                                   w["gnorm"], bsz, nc)
    return yn, dict(pres=pres, dt_t=dt_t, y=y, states=states)


def _ssd_layer_bwd(dyn, z, xs_in, bm_in, cm_in, sv, w, bsz, nc):
    pres = sv["pres"]
    dxs_p, dbm_p, dcm_p, dz, ddt_t, dgn, dd, dbias, dal = _ssd_chunk_bwd(
        dyn, pres[0], pres[1], pres[2], sv["dt_t"], w["dt_bias"], w["a_log"], w["d_exp"], z, w["gnorm"], sv["y"],
        sv["states"], bsz, nc)
    outs = [_conv_bwd(dp, a, w["conv_w"][i], bsz, nc, name=f"ssd_conv_bwd{i}")
            for i, (dp, a) in enumerate(((dxs_p, xs_in), (dbm_p, bm_in), (dcm_p, cm_in)))]
    ddt = _bf(_pad_lanes(ddt_t.T))
    grads = dict(
        ssd_conv_w=jnp.concatenate([o[1] for o in outs], axis=1),
        ssd_conv_b=jnp.concatenate([o[2] for o in outs], axis=1),
        ssd_dt_bias=dbias[:, 0], ssd_a_log=dal[:, 0], ssd_d=dd[:, 0], ssd_gnorm=dgn)
    return dz, outs[0][0], outs[1][0], outs[2][0], ddt, grads


WNAMES = ("meta_tokens", "ab_norm", "ab_w_in", "s5_lambda_re", "s5_lambda_im", "s5_log_dt", "s5_b_re", "s5_b_im",
          "s5_c_re", "s5_c_im", "s5_d", "s5_glu_w", "s5_glu_b", "ml_conv_w", "ml_conv_b", "ml_wq", "ml_wk", "ml_wv",
          "ml_w_gate", "ml_b_gate", "ml_norm", "ml_skip", "ab_w_out", "ssd_norm", "ssd_w_in", "ssd_conv_w",
          "ssd_conv_b", "ssd_dt_bias", "ssd_a_log", "ssd_d", "ssd_gnorm", "ssd_w_out", "final_norm")
SHARD_AXIS = dict(meta_tokens=1, ab_w_in=2, s5_glu_w=1, ml_conv_w=2, ml_wq=1, ml_wk=1, ml_wv=1, ml_w_gate=1,
                  ab_w_out=1, ssd_norm=1, ssd_w_in=2, ssd_conv_w=2, ssd_conv_b=1, ssd_gnorm=1, ssd_w_out=1)
BIG = ("ab_w_in", "s5_glu_w", "ab_w_out", "ssd_w_in", "ssd_w_out")
SMALL = tuple(n for n in WNAMES if n in SHARD_AXIS and n not in BIG)
REPL = tuple(n for n in WNAMES if n not in SHARD_AXIS)
PACK_ALIGN = 8 * 128


def _pack(arrs):
    lead = arrs[0][1]
    parts = []
    for a, nlead in arrs:
        f = a.reshape(a.shape[:nlead] + (-1,))
        f = jnp.pad(f, [(0, 0)] * nlead + [(0, (-f.shape[-1]) % PACK_ALIGN)])
        parts.append(f.reshape(f.shape[:nlead] + (-1, 128)))
    return jnp.concatenate(parts, axis=lead)


def _unpack(p, shapes):
    out, r0 = [], 0
    lead = p.shape[:-2]
    for s in shapes:
        n = math.prod(s)
        rows = -(-n // PACK_ALIGN) * 8
        seg = p[..., r0:r0 + rows, :].reshape(lead + (rows * 128,))[..., :n]
        out.append(seg.reshape(lead + tuple(s)))
        r0 += rows
    return out


def _assemble(g, axis):
    m = jnp.moveaxis(g, 0, axis)
    return m.reshape(m.shape[:axis] + (m.shape[axis] * m.shape[axis + 1],) + m.shape[axis + 2:])


def _split(full, axis):
    s = full.shape
    m = full.reshape(s[:axis] + (N_DEV, s[axis] // N_DEV) + s[axis + 1:])
    return jnp.moveaxis(m, axis, 0)


def kernel(x, *rest):
    nw = len(WNAMES)
    w = dict(zip(WNAMES, rest[:nw]))
    loss_target = rest[nw]
    mom = dict(zip(WNAMES, rest[nw + 1:2 * nw + 1]))
    var = dict(zip(WNAMES, rest[2 * nw + 1:3 * nw + 1]))
    bsz = x.shape[0]
    nc = 1 + SEQ // CHUNK
    tp = nc * CHUNK

    local = {n: _bf(w[n][0]) for n in BIG}
    small_local = _pack([(w[n], 0) for n in SMALL])
    gs = _exchange_start([small_local], ["ag"], name="gather_s")
    ga = _exchange_start([local["ab_w_in"]], ["ag"], name="gather_a", dep=gs["token"])
    got_s = _exchange_wait(gs, ga["token"])

    def assemble_big(n, got):
        return _assemble(got[:, None], SHARD_AXIS[n])[0]

    full = {}
    for n, g in zip(SMALL, _unpack(got_s[0], [w[n].shape for n in SMALL])):
        full[n] = _assemble(g, SHARD_AXIS[n])[0] if n != "meta_tokens" else _assemble(g, SHARD_AXIS[n])
    for n in REPL:
        full[n] = w[n][0] if n != "final_norm" else w[n]
    glu_b = full["s5_glu_b"].reshape(1, S5_WIDTH)
    meta = jnp.broadcast_to(full["meta_tokens"][None], (bsz, N_META, D_MODEL))
    h0 = jnp.concatenate([jnp.zeros((bsz, PAD_ROWS, D_MODEL), F32), meta, x], axis=1).reshape(bsz * tp, D_MODEL)
    xn0 = _rms_fwd(h0, full["ab_norm"], name="rms0")
    s5p, s5_vjp = _s5_tables(*[full[n] for n in ("s5_lambda_re", "s5_lambda_im", "s5_log_dt", "s5_b_re", "s5_b_im",
                                                   "s5_c_re", "s5_c_im", "s5_d")])
    mlw = _ml_weights(*[full[n] for n in ("ml_conv_w", "ml_conv_b", "ml_wq", "ml_wk", "ml_wv", "ml_w_gate",
                                           "ml_b_gate", "ml_norm", "ml_skip")])
    got_a = _exchange_wait(ga, [xn0, s5p["wbr"], s5p["wcr"], s5p["qr"], s5p["sr"], mlw["wq"], mlw["wk"], mlw["wv"],
                                mlw["wgq"]])
    gb = _exchange_start([local["s5_glu_w"], local["ab_w_out"]], ["ag", "ag"], name="gather_b", dep=got_a[0])
    gc = _exchange_start([local["ssd_w_in"], local["ssd_w_out"]], ["ag", "ag"], name="gather_c", dep=gb["token"])
    full["ab_w_in"] = assemble_big("ab_w_in", got_a[0])
    cuts0 = (0, S5_WIDTH, 2 * S5_WIDTH, 2 * S5_WIDTH + ML_WIDTH, 2 * (S5_WIDTH + ML_WIDTH))
    w_in0 = [full["ab_w_in"][:, cuts0[i]:cuts0[i + 1]] for i in range(4)]

    u, za, xb, zb = [_mm(xn0, wi, "NN", name=f"in0_{i}") for i, wi in enumerate(w_in0)]
    got_b = []

    def glu_w_after(scan_out):
        got_b.extend(_exchange_wait(gb, scan_out))
        return assemble_big("s5_glu_w", got_b[0])

    sv5 = _s5_layer_fwd(u, s5p, glu_w_after, bsz, nc)
    glu_w = assemble_big("s5_glu_w", got_b[0])
    w_out0 = assemble_big("ab_w_out", got_b[1])
    w_out0 = [w_out0[:S5_WIDTH], w_out0[S5_WIDTH:]]
    ya = _s5_post(sv5["y1"], sv5["glu_pre"], glu_b, za)
    yb, svm = _ml_layer_fwd(xb, zb, mlw, bsz, nc)
    h1 = _mm(ya, w_out0[0], "NN", name="out0_a", add=h0)
    h1 = _mm(yb, w_out0[1], "NN", name="out0_b", add=h1)
    got_c = _exchange_wait(gc, h1)
    w_in1, w_out1 = assemble_big("ssd_w_in", got_c[0]), assemble_big("ssd_w_out", got_c[1])
    cuts1 = (0, SSD_INNER, 2 * SSD_INNER, 2 * SSD_INNER + SSD_BC, 2 * SSD_INNER + 2 * SSD_BC)
    w_in1 = [w_in1[:, cuts1[i]:cuts1[i + 1]] for i in range(4)] + [_pad_lanes(w_in1[:, cuts1[4]:])]
    xn1 = _rms_fwd(h1, full["ssd_norm"], name="rms1")
    z1, xs_in, bm_in, cm_in, dt_rows = [_mm(xn1, wi, "NN", name=f"in1_{i}") for i, wi in enumerate(w_in1)]
    ssdw = _ssd_weights(*[full[n] for n in ("ssd_conv_w", "ssd_conv_b", "ssd_dt_bias", "ssd_a_log", "ssd_d",
                                             "ssd_gnorm")])
    yn, svs = _ssd_layer_fwd(z1, xs_in, bm_in, cm_in, dt_rows, ssdw, bsz, nc)
    h2 = _mm(yn, w_out1, "NN", name="out1", add=h1)
    loss_part, dh2, dfinal = _final_loss(h2, full["final_norm"], loss_target, bsz, nc)
    loss = lax.psum(loss_part[0, 0], ("x", "y", "c"))

    g = {"final_norm": dfinal}
    dyn = _mm(dh2, w_out1, "NT", name="d_out1")
    g["ssd_w_out"] = _mm(yn, dh2, "TN", name="dw_out1", out_dtype=BF16)
    dz1, dxs, dbm, dcm, ddt, gs = _ssd_layer_bwd(dyn, z1, xs_in, bm_in, cm_in, svs, ssdw, bsz, nc)
    g.update(gs)
    dps1 = (dz1, dxs, dbm, dcm, ddt)
    dxn1 = None
    for i, (dp, wi) in enumerate(zip(dps1, w_in1)):
        dxn1 = _mm(dp, wi, "NT", name=f"d_in1_{i}", add=dxn1)
    dw1 = [_mm(xn1, dp, "TN", name=f"dw_in1_{i}", out_dtype=BF16) for i, dp in enumerate(dps1)]
    g["ssd_w_in"] = jnp.concatenate(dw1[:4] + [dw1[4][:, :SSD_HEADS]], axis=1)

    def local_shape(n):
        return w[n].shape

    def slabs(n):
        gf = g[n].reshape((1,) + tuple(g[n].shape)) if n != "meta_tokens" else g[n]
        full_shape = tuple(d * (N_DEV if i == SHARD_AXIS[n] else 1) for i, d in enumerate(local_shape(n)))
        return _split(gf.reshape(full_shape), SHARD_AXIS[n])

    x1 = _exchange_start([slabs("ssd_w_in")[:, 0], slabs("ssd_w_out")[:, 0]], ["a2a", "a2a"], name="grads_1")
    dh1, g["ssd_norm"] = _rms_bwd(h1, full["ssd_norm"], dxn1, dh2, name="rms1_bwd", dep=x1["token"])
    dya = _mm(dh1, w_out0[0], "NT", name="d_out0_a")
    dyb = _mm(dh1, w_out0[1], "NT", name="d_out0_b")
    g["ab_w_out"] = jnp.concatenate([_mm(ya, dh1, "TN", name="dw_out0_a", out_dtype=BF16),
                                     _mm(yb, dh1, "TN", name="dw_out0_b", out_dtype=BF16)], axis=0)
    du, dza, g5 = _s5_layer_bwd(dya, u, za, sv5, s5p, s5_vjp, glu_w, glu_b, bsz, nc)
    g.update(g5)
    x2 = _exchange_start([slabs("ab_w_out")[:, 0], _bf(slabs("s5_glu_w")[:, 0])], ["a2a", "a2a"], name="grads_2")
    dxb, dzb, gm = _ml_layer_bwd(dyb, xb, zb, svm, mlw, bsz, nc, dep=x2["token"])
    g.update(gm)
    dps0 = (du, dza, dxb, dzb)
    dxn0 = None
    for i, (dp, wi) in enumerate(zip(dps0, w_in0)):
        dxn0 = _mm(dp, wi, "NT", name=f"d_in0_{i}", add=dxn0)
    dw0 = [_mm(xn0, dp, "TN", name=f"dw_in0_{i}", out_dtype=BF16, tn=S5_WIDTH, slabs=True) for i, dp in enumerate(dps0)]
    dw_in0_slabs = jnp.concatenate(dw0, axis=0)
    dh0, g["ab_norm"] = _rms_bwd(h0, full["ab_norm"], dxn0, dh1, name="rms0_bwd")
    dh0 = dh0.reshape(bsz, tp, D_MODEL)
    grad_x = dh0[:, CHUNK:]
    g["meta_tokens"] = jnp.sum(dh0[:, PAD_ROWS:CHUNK], axis=0)

    small_g = _pack([(slabs(n), 1) for n in SMALL])
    repl_g = _pack([(g[n], 0) for n in REPL])
    x3 = _exchange_start([dw_in0_slabs, small_g, repl_g], ["a2a", "a2a", "ag"], name="grads_3")

    def update_big(n, gp):
        return _adamw(w[n][0], mom[n][0], var[n][0], gp, name=f"adamw_{n}")

    res = {}
    ex1 = _exchange_wait(x1, x3["token"])
    res["ssd_w_in"], res["ssd_w_out"] = update_big("ssd_w_in", ex1[0]), update_big("ssd_w_out", ex1[1])
    ex2 = _exchange_wait(x2, res["ssd_w_out"][0])
    res["ab_w_out"], res["s5_glu_w"] = update_big("ab_w_out", ex2[0]), update_big("s5_glu_w", ex2[1])
    ex3 = _exchange_wait(x3, res["s5_glu_w"][0])
    res["ab_w_in"] = update_big("ab_w_in", ex3[0])
    for names, gp, tag in ((SMALL, ex3[1], "small"), (REPL, ex3[2], "repl")):
        shapes = [local_shape(n) for n in names]
        packs = [_pack([(d[n], 0) for n in names]) for d in (w, mom, var)]
        outs = _adamw(packs[0], packs[1], packs[2], gp, name=f"adamw_{tag}")
        for k, o in enumerate(outs):
            for n, a in zip(names, _unpack(o, shapes)):
                res.setdefault(n, [None] * 4)[k] = a
    outs = [loss, grad_x]
    for k in range(4):
        outs += [res[n][k].reshape(local_shape(n)) for n in WNAMES]
    return tuple(outs)
```

```python
import functools
import math

import jax
import jax.numpy as jnp
from jax import lax
from jax.experimental import pallas as pl
from jax.experimental.pallas import tpu as pltpu

F32 = jnp.float32
BF16 = jnp.bfloat16

D_MODEL = 2048
SEQ = 2048
N_META = 16
CHUNK = 128
PAD_ROWS = CHUNK - N_META
NORM_EPS = 1e-6
HEAD_NORM_EPS = 1e-5
S5_WIDTH = 1024
S5_GROUPS = 64
S5_GROUP_SIZE = 16
S5_STATE = 64
S5_GB = 8
S5_LANES = S5_GB * S5_STATE
ML_WIDTH = 3072
ML_HEADS = 8
ML_DH = 384
ML_CONV = 4
QKV_BLOCK = 4
SSD_INNER = 4096
SSD_HEADS = 64
SSD_P = 64
SSD_N = 128
SSD_GROUPS = 8
SSD_HPG = 8
SSD_GW = SSD_HPG * SSD_P
N_DEV = 8
ADAM_LR, ADAM_B1, ADAM_B2, ADAM_EPS, ADAM_WD, ADAM_STEP = 0.001, 0.9, 0.999, 1e-08, 0.01, 10
NEG = -1e30
VMEM_CAP = 60 * 1024 * 1024
MESH = pl.DeviceIdType.MESH

NN = (((1,), (0,)), ((), ()))
NT = (((1,), (1,)), ((), ()))
TN = (((0,), (0,)), ((), ()))


def _dot(a, b, dims=NN):
    return lax.dot_general(a, b, dims, preferred_element_type=F32)


def _bf(x):
    return x.astype(BF16)


def _pick(n, cands):
    for c in cands:
        if n % c == 0:
            return c
    return n


def _nbytes(shape, dtype):
    return math.prod(shape) * jnp.dtype(dtype).itemsize


ANY_SPEC = pl.BlockSpec(memory_space=pl.ANY)


def _pc(body, *, name, grid, in_specs, out_specs, out_shape, scratch=(), vmem=None, dep=None):
    limit = None if vmem is None else int(min(VMEM_CAP, max(32 * 1024 * 1024, 2 * vmem + (8 << 20))))
    n_in = len(in_specs)
    if dep is not None:
        inner = body

        def body(*refs):
            inner(*refs[:n_in], *refs[n_in + 1:])

        in_specs = list(in_specs) + [ANY_SPEC]
    call = pl.pallas_call(
        body, name=name, grid=grid, in_specs=in_specs, out_specs=out_specs, out_shape=out_shape,
        scratch_shapes=list(scratch),
        compiler_params=pltpu.CompilerParams(dimension_semantics=("arbitrary",) * len(grid), vmem_limit_bytes=limit))
    return call if dep is None else (lambda *args: call(*args, dep))


def _silu(x):
    return x * jax.nn.sigmoid(x)


def _dsilu(x):
    s = jax.nn.sigmoid(x)
    return s * (1.0 + x * (1.0 - s))


def _gelu_and_grad(x):
    c0 = math.sqrt(2.0 / math.pi)
    inner = c0 * (x + 0.044715 * x * x * x)
    t = jnp.tanh(inner)
    g = 0.5 * x * (1.0 + t)
    dg = 0.5 * (1.0 + t) + 0.5 * x * (1.0 - t * t) * c0 * (1.0 + 3 * 0.044715 * x * x)
    return g, dg


def _mm(a, b, mode, *, name, add=None, out_dtype=F32, tn=None, slabs=False):
    if mode == "NN":
        (m, k), (k2, n) = a.shape, b.shape
    elif mode == "NT":
        (m, k), (n, k2) = a.shape, b.shape
    else:
        (k, m), (k2, n) = a.shape, b.shape
    assert k == k2, (a.shape, b.shape, mode)
    tm = _pick(m, (1088, 1024, 768, 512, 384, 256, 128))
    tn = tn or _pick(n, (512, 384, 256, 128))
    tk = _pick(k, (2048, 1088, 1024, 768, 512, 384, 256, 128))
    nk = k // tk
    dims = {"NN": NN, "NT": NT, "TN": TN}[mode]

    def body(*refs):
        a_ref, b_ref = refs[0], refs[1]
        add_ref = refs[2] if add is not None else None
        o_ref, acc_ref = refs[-2], refs[-1]
        kk = pl.program_id(2)

        @pl.when(kk == 0)
        def _():
            acc_ref[...] = jnp.zeros_like(acc_ref)

        acc_ref[...] += _dot(_bf(a_ref[...]), _bf(b_ref[...]), dims)

        @pl.when(kk == nk - 1)
        def _():
            r = acc_ref[...]
            if add_ref is not None:
                r = r + add_ref[...]
            o_ref[...] = r.reshape(o_ref.shape).astype(o_ref.dtype)

    if mode == "NN":
        a_spec = pl.BlockSpec((tm, tk), lambda i, j, kk: (i, kk))
        b_spec = pl.BlockSpec((tk, tn), lambda i, j, kk: (kk, j))
    elif mode == "NT":
        a_spec = pl.BlockSpec((tm, tk), lambda i, j, kk: (i, kk))
        b_spec = pl.BlockSpec((tn, tk), lambda i, j, kk: (j, kk))
    else:
        a_spec = pl.BlockSpec((tk, tm), lambda i, j, kk: (kk, i))
        b_spec = pl.BlockSpec((tk, tn), lambda i, j, kk: (kk, j))
    in_specs = [a_spec, b_spec]
    args = [a, b]
    if add is not None:
        in_specs.append(pl.BlockSpec((tm, tn), lambda i, j, kk: (i, j)))
        args.append(add)
    if slabs:
        out_shape = jax.ShapeDtypeStruct((n // tn, m, tn), out_dtype)
        out_spec = pl.BlockSpec((1, tm, tn), lambda i, j, kk: (j, i, 0))
    else:
        out_shape = jax.ShapeDtypeStruct((m, n), out_dtype)
        out_spec = pl.BlockSpec((tm, tn), lambda i, j, kk: (i, j))
    vmem = (_nbytes((tm, tk), a.dtype) + _nbytes((tk, tn), b.dtype) + _nbytes((tm, tn), out_dtype)
            + (_nbytes((tm, tn), F32) if add is not None else 0)) + _nbytes((tm, tn), F32) // 2
    return _pc(body, name=name, grid=(m // tm, n // tn, nk), in_specs=in_specs, out_specs=out_spec,
               out_shape=out_shape, scratch=[pltpu.VMEM((tm, tn), F32)], vmem=vmem)(*args)


def _rms_fwd(x, g, *, name):
    r, d = x.shape
    tm = _pick(r, (256, 128))

    def body(x_ref, g_ref, o_ref):
        xv = x_ref[...]
        rstd = lax.rsqrt(jnp.mean(xv * xv, axis=1, keepdims=True) + NORM_EPS)
        o_ref[...] = (xv * rstd * g_ref[...]).astype(o_ref.dtype)

    return _pc(body, name=name, grid=(r // tm,),
               in_specs=[pl.BlockSpec((tm, d), lambda i: (i, 0)), pl.BlockSpec((1, d), lambda i: (0, 0))],
               out_specs=pl.BlockSpec((tm, d), lambda i: (i, 0)), out_shape=jax.ShapeDtypeStruct((r, d), BF16),
               vmem=tm * d * 6)(x, g.reshape(1, d))


def _rms_bwd(x, g, dxn, dres, *, name, dep=None):
    r, d = x.shape
    tm = _pick(r, (256, 128))

    def body(x_ref, g_ref, dxn_ref, dres_ref, dx_ref, dg_ref):
        @pl.when(pl.program_id(0) == 0)
        def _():
            dg_ref[...] = jnp.zeros_like(dg_ref)

        xv = x_ref[...]
        rstd = lax.rsqrt(jnp.mean(xv * xv, axis=1, keepdims=True) + NORM_EPS)
        xh = xv * rstd
        dy = dxn_ref[...]
        dg_ref[...] += jnp.sum(dy * xh, axis=0, keepdims=True)
        dyg = dy * g_ref[...]
        dx_ref[...] = dres_ref[...] + rstd * (dyg - xh * jnp.mean(dyg * xh, axis=1, keepdims=True))

    row = pl.BlockSpec((tm, d), lambda i: (i, 0))
    vec = pl.BlockSpec((1, d), lambda i: (0, 0))
    return _pc(body, name=name, grid=(r // tm,), in_specs=[row, vec, row, row], out_specs=[row, vec],
               out_shape=[jax.ShapeDtypeStruct((r, d), F32), jax.ShapeDtypeStruct((1, d), F32)],
               vmem=tm * d * 16, dep=dep)(x, g.reshape(1, d), dxn, dres)


def _final_loss(h, g, target, bsz, nc):
    d = h.shape[1]

    def body(h_ref, g_ref, t_ref, loss_ref, dh_ref, dg_ref):
        b, c = pl.program_id(0), pl.program_id(1)

        @pl.when((b == 0) & (c == 0))
        def _():
            loss_ref[...] = jnp.zeros_like(loss_ref)
            dg_ref[...] = jnp.zeros_like(dg_ref)

        @pl.when(c == 0)
        def _():
            dh_ref[...] = jnp.zeros_like(dh_ref)

        @pl.when(c > 0)
        def _():
            xv = h_ref[...]
            rstd = lax.rsqrt(jnp.mean(xv * xv, axis=1, keepdims=True) + NORM_EPS)
            xh = xv * rstd
            gv = g_ref[...]
            err = xh * gv - t_ref[0]
            loss_ref[...] += 0.5 * jnp.sum(jnp.mean(err * err, axis=1, keepdims=True))
            dy = err * (1.0 / d)
            dg_ref[...] += jnp.sum(dy * xh, axis=0, keepdims=True)
            dyg = dy * gv
            dh_ref[...] = rstd * (dyg - xh * jnp.mean(dyg * xh, axis=1, keepdims=True))

    row = pl.BlockSpec((CHUNK, d), lambda b, c: (b * nc + c, 0))
    vec = pl.BlockSpec((1, d), lambda b, c: (0, 0))
    return _pc(body, name="final_loss", grid=(bsz, nc),
               in_specs=[row, vec, pl.BlockSpec((1, CHUNK, d), lambda b, c: (b, jnp.maximum(c - 1, 0), 0))],
               out_specs=[pl.BlockSpec((8, 128), lambda b, c: (0, 0)), row, vec],
               out_shape=[jax.ShapeDtypeStruct((8, 128), F32), jax.ShapeDtypeStruct(h.shape, F32),
                          jax.ShapeDtypeStruct((1, d), F32)],
               vmem=CHUNK * d * 16)(h, g.reshape(1, d), target)


def _adamw(w, m, v, gparts, *, name):
    r, c = w.shape
    tr = _pick(r, (256, 128)) if r * c * 4 > (1 << 20) else r

    def body(w_ref, m_ref, v_ref, gp_ref, g_ref, d_ref, nm_ref, nv_ref):
        g = gp_ref[0].astype(F32)
        for j in range(1, N_DEV):
            g = g + gp_ref[j].astype(F32)
        mm = ADAM_B1 * m_ref[...] + (1.0 - ADAM_B1) * g
        vv = ADAM_B2 * v_ref[...] + (1.0 - ADAM_B2) * (g * g)
        m_hat = mm / (1.0 - ADAM_B1 ** ADAM_STEP)
        v_hat = vv / (1.0 - ADAM_B2 ** ADAM_STEP)
        g_ref[...] = g
        d_ref[...] = -ADAM_LR * (m_hat / (jnp.sqrt(v_hat) + ADAM_EPS) + ADAM_WD * w_ref[...])
        nm_ref[...] = mm
        nv_ref[...] = vv

    blk = pl.BlockSpec((tr, c), lambda i: (i, 0))
    out = jax.ShapeDtypeStruct((r, c), F32)
    return _pc(body, name=name, grid=(r // tr,),
               in_specs=[blk, blk, blk, pl.BlockSpec((N_DEV, tr, c), lambda i: (0, i, 0))],
               out_specs=[blk, blk, blk, blk], out_shape=[out, out, out, out],
               vmem=tr * c * (4 * 7 + N_DEV * jnp.dtype(gparts.dtype).itemsize))(w, m, v, gparts)


PEERS = (1, 2, 4, 6, 3, 5, 7)
HBM_SPEC = pl.BlockSpec(memory_space=pltpu.HBM)
SEM_SPEC = pl.BlockSpec(memory_space=pltpu.SEMAPHORE)
SIDE_EFFECT = pltpu.SideEffectType.DATAFLOW_SIDE_EFFECTING


def _peer(p):
    x, y, c = lax.axis_index("x"), lax.axis_index("y"), lax.axis_index("c")
    tx, ty, tc = x ^ ((p >> 2) & 1), y ^ ((p >> 1) & 1), c ^ (p & 1)
    return (tx, ty, tc), 4 * tx + 2 * ty + tc


def _place_own(a, kind, *, name):
    rows, cols = a.shape[-2:]
    tr = _pick(rows, (512, 256, 128, 64, 32, 16))
    me = (4 * lax.axis_index("x") + 2 * lax.axis_index("y") + lax.axis_index("c")).astype(jnp.int32).reshape(1)

    def body(me_ref, in_ref, out_ref):
        out_ref[...] = in_ref[...].reshape(out_ref.shape)

    if kind == "a2a":
        in_spec = pl.BlockSpec((1, tr, cols), lambda i, me_ref: (me_ref[0], i, 0))
    else:
        in_spec = pl.BlockSpec((tr, cols), lambda i, me_ref: (i, 0))
    return pl.pallas_call(
        body, name=name, out_shape=jax.ShapeDtypeStruct((N_DEV, rows, cols), a.dtype),
        grid_spec=pltpu.PrefetchScalarGridSpec(
            num_scalar_prefetch=1, grid=(rows // tr,), in_specs=[in_spec],
            out_specs=pl.BlockSpec((1, tr, cols), lambda i, me_ref: (me_ref[0], i, 0))))(me, a)


def _exchange_copies(ins, lands, send_sems, recv_sems, kinds, incoming):
    me = 4 * lax.axis_index("x") + 2 * lax.axis_index("y") + lax.axis_index("c")
    copies = []
    for i, kind in enumerate(kinds):
        for p in PEERS:
            dev, tgt = _peer(p)
            k = i * (N_DEV - 1) + p - 1
            copies.append(pltpu.make_async_remote_copy(
                src_ref=ins[i].at[tgt] if kind == "a2a" else ins[i], dst_ref=lands[i].at[tgt if incoming else me],
                send_sem=send_sems.at[k], recv_sem=recv_sems.at[k], device_id=dev, device_id_type=MESH))
    return copies


def _exchange_start(arrays, kinds, *, name, dep=None):
    n = len(arrays)
    lands = [_place_own(a, k, name=f"{name}_own{i}") for i, (a, k) in enumerate(zip(arrays, kinds))]
    extra = [] if dep is None else [dep]

    def body(*refs):
        ins, lnd = refs[:n], refs[n:2 * n]
        send_sems, recv_sems = refs[2 * n + len(extra)], refs[2 * n + len(extra) + 1]
        token = refs[-1]
        for cp in _exchange_copies(ins, lnd, send_sems, recv_sems, kinds, False):
            cp.start()
        token[...] = jnp.zeros_like(token)

    sem = pltpu.SemaphoreType.DMA((n * (N_DEV - 1),))
    outs = pl.pallas_call(
        body, name=name, in_specs=[HBM_SPEC] * (2 * n) + [ANY_SPEC] * len(extra),
        out_specs=[SEM_SPEC, SEM_SPEC] + [HBM_SPEC] * (2 * n) + [pl.BlockSpec(memory_space=pltpu.VMEM)],
        out_shape=[sem, sem] + [pltpu.HBM(a.shape, a.dtype) for a in arrays + lands]
        + [jax.ShapeDtypeStruct((8, 128), F32)],
        input_output_aliases={i: 2 + i for i in range(2 * n)},
        compiler_params=pltpu.CompilerParams(has_side_effects=SIDE_EFFECT),
    )(*[pltpu.with_memory_space_constraint(a, pltpu.HBM) for a in arrays + lands], *extra)
    return dict(send=outs[0], recv=outs[1], ins=list(outs[2:2 + n]), lands=list(outs[2 + n:2 + 2 * n]),
                token=outs[-1], kinds=kinds, name=name)


def _exchange_wait(h, after):
    n = len(h["ins"])
    kinds = h["kinds"]

    def body(*refs):
        ins, lnd = refs[:n], refs[n:2 * n]
        send_sems, recv_sems = refs[2 * n], refs[2 * n + 1]
        copies = _exchange_copies(ins, lnd, send_sems, recv_sems, kinds, True)
        for cp in copies:
            cp.wait_recv()
        for cp in copies:
            cp.wait_send()

    arrs = h["ins"] + h["lands"]
    after = list(after) if isinstance(after, (list, tuple)) else [after]
    outs = pl.pallas_call(
        body, name=h["name"] + "_wait", in_specs=[HBM_SPEC] * (2 * n) + [SEM_SPEC, SEM_SPEC] + [ANY_SPEC] * len(after),
        out_specs=[HBM_SPEC] * (2 * n), out_shape=[pltpu.HBM(a.shape, a.dtype) for a in arrs],
        input_output_aliases={i: i for i in range(2 * n)},
        compiler_params=pltpu.CompilerParams(has_side_effects=SIDE_EFFECT),
    )(*arrs, h["send"], h["recv"], *after)
    return list(outs[n:])


def _s5_params(lam_re, lam_im, log_dt, b_re, b_im):
    dt = jnp.exp(log_dt)[:, None]
    mag = jnp.exp(lam_re * dt)
    ar, ai = mag * jnp.cos(lam_im * dt), mag * jnp.sin(lam_im * dt)
    den = lam_re * lam_re + lam_im * lam_im
    qr = ((ar - 1.0) * lam_re + ai * lam_im) / den
    qi = (ai * lam_re - (ar - 1.0) * lam_im) / den
    bbr = qr[..., None] * b_re - qi[..., None] * b_im
    bbi = qr[..., None] * b_im + qi[..., None] * b_re
    return ar, ai, bbr, bbi


def _s5_power_table(ar, ai):
    pr, pi = ar.reshape(1, -1), ai.reshape(1, -1)
    while pr.shape[0] < CHUNK:
        sr, si = pr[-1:], pi[-1:]
        pr, pi = (jnp.concatenate([pr, pr * sr - pi * si], axis=0), jnp.concatenate([pi, pr * si + pi * sr], axis=0))
    return pr, pi


def _blockdiag(w, rows, cols):
    w = w.reshape(S5_GB, S5_GB, rows, cols)
    eye = jnp.eye(S5_GB, dtype=w.dtype)
    return jnp.einsum("abrc,bd->abrdc", w, eye).reshape(S5_GB, S5_GB * rows, S5_GB * cols)


def _blockdiag_extract(w, rows, cols):
    w = w.reshape(S5_GB, S5_GB, rows, S5_GB, cols)
    return jnp.einsum("abrbc->abrc", w).reshape(S5_GROUPS, rows, cols)


def _s5_scan_specs(bsz, nc, rev):
    def chunk(b, c):
        return b * nc + ((nc - 1 - c) if rev else c)

    return dict(
        u=pl.BlockSpec((CHUNK, CHUNK), lambda g, b, c: (chunk(b, c), g)),
        x=pl.BlockSpec((CHUNK, S5_LANES), lambda g, b, c: (chunk(b, c), g)),
        wb=pl.BlockSpec((1, CHUNK, S5_LANES), lambda g, b, c: (g, 0, 0)),
        wc=pl.BlockSpec((1, S5_LANES, CHUNK), lambda g, b, c: (g, 0, 0)),
        tab=pl.BlockSpec((CHUNK, S5_LANES), lambda g, b, c: (0, g)),
        step=pl.BlockSpec((8, S5_LANES), lambda g, b, c: (0, g)),
        d=pl.BlockSpec((1, CHUNK), lambda g, b, c: (0, g)),
        lane=pl.BlockSpec((1, S5_LANES), lambda g, b, c: (0, g)),
        xprev=pl.BlockSpec((8, S5_LANES), lambda g, b, c: (jnp.maximum(chunk(b, c) * (CHUNK // 8) - 1, 0), g)),
    )


def _s5_fwd(u, wbr, wbi, pr, pi, sr, si, wcr, wci, d, bsz, nc):
    r = u.shape[0]
    sp = _s5_scan_specs(bsz, nc, False)

    def body(u_ref, wbr_ref, wbi_ref, pr_ref, pi_ref, sr_ref, si_ref, wcr_ref, wci_ref, d_ref,
             xr_ref, xi_ref, y1_ref, g_ref, cr_s, ci_s):
        @pl.when(pl.program_id(2) == 0)
        def _():
            cr_s[...] = jnp.zeros_like(cr_s)
            ci_s[...] = jnp.zeros_like(ci_s)

        uv = u_ref[...]
        ub = _bf(uv)
        xr, xi = _dot(ub, wbr_ref[0]), _dot(ub, wbi_ref[0])
        row = lax.broadcasted_iota(jnp.int32, (CHUNK, S5_LANES), 0)
        for k in range(7):
            s = 1 << k
            ar, ai = sr_ref[k:k + 1, :], si_ref[k:k + 1, :]
            hr = jnp.where(row >= s, pltpu.roll(xr, s, 0), 0.0)
            hi = jnp.where(row >= s, pltpu.roll(xi, s, 0), 0.0)
            xr, xi = xr + (ar * hr - ai * hi), xi + (ar * hi + ai * hr)
        cr, ci = cr_s[...], ci_s[...]
        tr, ti = pr_ref[...], pi_ref[...]
        xr, xi = xr + (tr * cr - ti * ci), xi + (tr * ci + ti * cr)
        cr_s[...] = xr[CHUNK - 1:CHUNK, :]
        ci_s[...] = xi[CHUNK - 1:CHUNK, :]
        xr_ref[...] = xr
        xi_ref[...] = xi
        y = _dot(_bf(xr), wcr_ref[0]) - _dot(_bf(xi), wci_ref[0]) + d_ref[...] * uv
        y1_ref[...] = y
        g_ref[...] = _bf(_gelu_and_grad(y)[0])

    return _pc(body, name="s5_fwd", grid=(S5_GB, bsz, nc),
               in_specs=[sp["u"], sp["wb"], sp["wb"], sp["tab"], sp["tab"], sp["step"], sp["step"], sp["wc"], sp["wc"],
                         sp["d"]],
               out_specs=[sp["x"], sp["x"], sp["u"], sp["u"]],
               out_shape=[jax.ShapeDtypeStruct((r, S5_GROUPS * S5_STATE), F32)] * 2
               + [jax.ShapeDtypeStruct((r, S5_WIDTH), F32), jax.ShapeDtypeStruct((r, S5_WIDTH), BF16)],
               scratch=[pltpu.VMEM((1, S5_LANES), F32)] * 2, vmem=4 << 20,
               )(u, wbr, wbi, pr, pi, sr, si, wcr, wci, d)


def _s5_post(y1, glu_pre, glu_b, z):
    r, w = y1.shape
    tm = _pick(r, (256, 128))

    def body(y_ref, p_ref, b_ref, z_ref, o_ref):
        g = _gelu_and_grad(y_ref[...])[0]
        o_ref[...] = _bf(g * jax.nn.sigmoid(p_ref[...] + b_ref[...]) * _silu(z_ref[...]))

    row = pl.BlockSpec((tm, w), lambda i: (i, 0))
    return _pc(body, name="s5_post", grid=(r // tm,), in_specs=[row, row, pl.BlockSpec((1, w), lambda i: (0, 0)), row],
               out_specs=row, out_shape=jax.ShapeDtypeStruct((r, w), BF16), vmem=tm * w * 16)(y1, glu_pre, glu_b, z)


def _s5_post_bwd(dya, y1, glu_pre, glu_b, z):
    r, w = y1.shape
    tm = _pick(r, (256, 128))

    def body(dy_ref, y_ref, p_ref, b_ref, z_ref, dz_ref, dp_ref, dg_ref, db_ref):
        @pl.when(pl.program_id(0) == 0)
        def _():
            db_ref[...] = jnp.zeros_like(db_ref)

        g = _gelu_and_grad(y_ref[...])[0]
        s = jax.nn.sigmoid(p_ref[...] + b_ref[...])
        zv = z_ref[...]
        dy = dy_ref[...]
        do = dy * _silu(zv)
        dz_ref[...] = _bf(dy * g * s * _dsilu(zv))
        dp = do * g * s * (1.0 - s)
        dp_ref[...] = _bf(dp)
        db_ref[...] += jnp.sum(dp, axis=0, keepdims=True)
        dg_ref[...] = do * s

    row = pl.BlockSpec((tm, w), lambda i: (i, 0))
    vec = pl.BlockSpec((1, w), lambda i: (0, 0))
    return _pc(body, name="s5_post_bwd", grid=(r // tm,), in_specs=[row, row, row, vec, row],
               out_specs=[row, row, row, vec],
               out_shape=[jax.ShapeDtypeStruct((r, w), BF16), jax.ShapeDtypeStruct((r, w), BF16),
                          jax.ShapeDtypeStruct((r, w), F32), jax.ShapeDtypeStruct((1, w), F32)],
               vmem=tm * w * 24)(dya, y1, glu_pre, glu_b, z)


def _s5_bwd(dg, y1, u, xr, xi, wbr, wbi, qr, qi, sr, si, wcr, wci, d, bsz, nc):
    r = u.shape[0]
    sp = _s5_scan_specs(bsz, nc, True)

    def body(dg_ref, y1_ref, u_ref, xr_ref, xi_ref, xpr_ref, xpi_ref, wbr_ref, wbi_ref, qr_ref, qi_ref, sr_ref, si_ref,
             wcr_ref, wci_ref, d_ref, du_ref, dd_ref, dwcr_ref, dwci_ref, dwbr_ref, dwbi_ref, dar_ref, dai_ref,
             cr_s, ci_s):
        b, c = pl.program_id(1), pl.program_id(2)

        @pl.when((b == 0) & (c == 0))
        def _():
            for ref in (dd_ref, dwcr_ref, dwci_ref, dwbr_ref, dwbi_ref, dar_ref, dai_ref):
                ref[...] = jnp.zeros_like(ref)

        @pl.when(c == 0)
        def _():
            cr_s[...] = jnp.zeros_like(cr_s)
            ci_s[...] = jnp.zeros_like(ci_s)

        uv = u_ref[...]
        ub = _bf(uv)
        dy = dg_ref[...] * _gelu_and_grad(y1_ref[...])[1]
        dd_ref[...] += jnp.sum(dy * uv, axis=0, keepdims=True)
        dyb = _bf(dy)
        xr, xi = xr_ref[...], xi_ref[...]
        dwcr_ref[0] += _dot(_bf(xr), dyb, TN)
        dwci_ref[0] -= _dot(_bf(xi), dyb, TN)
        lr, li = _dot(dyb, wcr_ref[0], NT), -_dot(dyb, wci_ref[0], NT)
        row = lax.broadcasted_iota(jnp.int32, (CHUNK, S5_LANES), 0)
        for k in range(7):
            s = 1 << k
            ar, ai = sr_ref[k:k + 1, :], si_ref[k:k + 1, :]
            hr = jnp.where(row < CHUNK - s, pltpu.roll(lr, CHUNK - s, 0), 0.0)
            hi = jnp.where(row < CHUNK - s, pltpu.roll(li, CHUNK - s, 0), 0.0)
            lr, li = lr + (ar * hr + ai * hi), li + (ar * hi - ai * hr)
        cr, ci = cr_s[...], ci_s[...]
        tr, ti = qr_ref[...], qi_ref[...]
        lr, li = lr + (tr * cr + ti * ci), li + (tr * ci - ti * cr)
        cr_s[...] = lr[0:1, :]
        ci_s[...] = li[0:1, :]
        lrb, lib = _bf(lr), _bf(li)
        du_ref[...] = _bf(_dot(lrb, wbr_ref[0], NT) + _dot(lib, wbi_ref[0], NT) + dy * d_ref[...])
        dwbr_ref[0] += _dot(ub, lrb, TN)
        dwbi_ref[0] += _dot(ub, lib, TN)
        first = c == nc - 1
        pr0 = jnp.where(first, 0.0, xpr_ref[7:8, :])
        pi0 = jnp.where(first, 0.0, xpi_ref[7:8, :])
        xpr = jnp.where(row == 0, pr0, pltpu.roll(xr, 1, 0))
        xpi = jnp.where(row == 0, pi0, pltpu.roll(xi, 1, 0))
        dar_ref[...] += jnp.sum(lr * xpr + li * xpi, axis=0, keepdims=True)
        dai_ref[...] += jnp.sum(li * xpr - lr * xpi, axis=0, keepdims=True)

    st = jax.ShapeDtypeStruct
    return _pc(body, name="s5_bwd", grid=(S5_GB, bsz, nc),
               in_specs=[sp["u"], sp["u"], sp["u"], sp["x"], sp["x"], sp["xprev"], sp["xprev"], sp["wb"], sp["wb"],
                         sp["tab"], sp["tab"], sp["step"], sp["step"], sp["wc"], sp["wc"], sp["d"]],
               out_specs=[sp["u"], sp["d"], sp["wc"], sp["wc"], sp["wb"], sp["wb"], sp["lane"], sp["lane"]],
               out_shape=[st((r, S5_WIDTH), BF16), st((1, S5_WIDTH), F32),
                          st((S5_GB, S5_LANES, CHUNK), F32), st((S5_GB, S5_LANES, CHUNK), F32),
                          st((S5_GB, CHUNK, S5_LANES), F32), st((S5_GB, CHUNK, S5_LANES), F32),
                          st((1, S5_GROUPS * S5_STATE), F32), st((1, S5_GROUPS * S5_STATE), F32)],
               scratch=[pltpu.VMEM((1, S5_LANES), F32)] * 2, vmem=6 << 20,
               )(dg, y1, u, xr, xi, xr, xi, wbr, wbi, qr, qi, sr, si, wcr, wci, d)


def _s5_layer_fwd(u, prm, glu_w, bsz, nc):
    xr, xi, y1, g = _s5_fwd(u, prm["wbr"], prm["wbi"], prm["pr"], prm["pi"], prm["sr"], prm["si"], prm["wcr"],
                            prm["wci"], prm["d"], bsz, nc)
    glu_pre = _mm(g, glu_w(y1) if callable(glu_w) else glu_w, "NN", name="s5_glu")
    return dict(xr=xr, xi=xi, y1=y1, g=g, glu_pre=glu_pre)


def _s5_layer_bwd(dya, u, z, sv, prm, pvjp, glu_w, glu_b, bsz, nc):
    dz, dglu, dg_direct, dglu_b = _s5_post_bwd(dya, sv["y1"], sv["glu_pre"], glu_b, z)
    dg = _mm(dglu, glu_w, "NT", name="s5_dg", add=dg_direct)
    dglu_w = _mm(sv["g"], dglu, "TN", name="s5_dglu_w")
    du, dd, dwcr, dwci, dwbr, dwbi, dar, dai = _s5_bwd(
        dg, sv["y1"], u, sv["xr"], sv["xi"], prm["wbr"], prm["wbi"], prm["qr"], prm["qi"], prm["sr"], prm["si"],
        prm["wcr"], prm["wci"], prm["d"], bsz, nc)
    dbbr = jnp.swapaxes(_blockdiag_extract(dwbr, S5_GROUP_SIZE, S5_STATE), 1, 2)
    dbbi = jnp.swapaxes(_blockdiag_extract(dwbi, S5_GROUP_SIZE, S5_STATE), 1, 2)
    dlr, dli, dldt, dbr, dbi = pvjp((dar.reshape(S5_GROUPS, S5_STATE), dai.reshape(S5_GROUPS, S5_STATE), dbbr, dbbi))
    grads = dict(
        s5_lambda_re=dlr, s5_lambda_im=dli, s5_log_dt=dldt, s5_b_re=dbr, s5_b_im=dbi,
        s5_c_re=jnp.swapaxes(_blockdiag_extract(dwcr, S5_STATE, S5_GROUP_SIZE), 1, 2),
        s5_c_im=jnp.swapaxes(_blockdiag_extract(dwci, S5_STATE, S5_GROUP_SIZE), 1, 2),
        s5_d=dd, s5_glu_w=dglu_w, s5_glu_b=dglu_b)
    return du, dz, grads


def _s5_tables(lam_re, lam_im, log_dt, b_re, b_im, c_re, c_im, d):
    (ar, ai, bbr, bbi), vjp = jax.vjp(_s5_params, lam_re, lam_im, log_dt, b_re, b_im)
    pr, pi = _s5_power_table(lax.stop_gradient(ar), lax.stop_gradient(ai))
    steps = [(1 << k) - 1 for k in range(8)]
    prm = dict(
        wbr=_bf(_blockdiag(jnp.swapaxes(bbr, 1, 2), S5_GROUP_SIZE, S5_STATE)),
        wbi=_bf(_blockdiag(jnp.swapaxes(bbi, 1, 2), S5_GROUP_SIZE, S5_STATE)),
        wcr=_bf(_blockdiag(jnp.swapaxes(c_re, 1, 2), S5_STATE, S5_GROUP_SIZE)),
        wci=_bf(_blockdiag(jnp.swapaxes(c_im, 1, 2), S5_STATE, S5_GROUP_SIZE)),
        pr=pr, pi=pi, qr=pr[::-1], qi=pi[::-1],
        sr=jnp.concatenate([pr[i:i + 1] for i in steps], axis=0),
        si=jnp.concatenate([pi[i:i + 1] for i in steps], axis=0), d=d.reshape(1, S5_WIDTH))
    return prm, vjp


def _tile16(p8):
    return jnp.concatenate([p8] * (CHUNK // 8), axis=0)


def _shift_down(x, halo, s, row):
    return jnp.where(row >= s, pltpu.roll(x, s, 0), pltpu.roll(halo, s, 0))


def _shift_up(x, halo, s, row):
    return jnp.where(row < CHUNK - s, pltpu.roll(x, CHUNK - s, 0), pltpu.roll(halo, CHUNK - s, 0))


def _conv_specs(nc, tw):
    def chunk(b, c):
        return b * nc + c

    return dict(
        x=pl.BlockSpec((CHUNK, tw), lambda j, b, c: (chunk(b, c), j)),
        prev=pl.BlockSpec((8, tw), lambda j, b, c: (jnp.maximum(chunk(b, c) * (CHUNK // 8) - 1, 0), j)),
        nxt=pl.BlockSpec((8, tw), lambda j, b, c: ((b * nc + jnp.minimum(c + 1, nc - 1)) * (CHUNK // 8), j)),
        w=pl.BlockSpec((ML_CONV, tw), lambda j, b, c: (0, j)),
        vec=pl.BlockSpec((1, tw), lambda j, b, c: (0, j)),
    )


def _conv_fwd(x, w, bias, bsz, nc, *, name):
    r, wd = x.shape
    tw = _pick(wd, (2048, 1536, 1024, 512, 384, 256, 128))
    sp = _conv_specs(nc, tw)

    def body(x_ref, p_ref, w_ref, b_ref, o_ref):
        c = pl.program_id(2)
        xv = x_ref[...]
        row = lax.broadcasted_iota(jnp.int32, xv.shape, 0)
        halo = jnp.where(c == 0, 0.0, _tile16(p_ref[...]))
        acc = b_ref[...] + w_ref[3:4, :] * xv
        for s in (1, 2, 3):
            acc = acc + w_ref[3 - s:4 - s, :] * _shift_down(xv, halo, s, row)
        o_ref[...] = acc

    return _pc(body, name=name, grid=(wd // tw, bsz, nc), in_specs=[sp["x"], sp["prev"], sp["w"], sp["vec"]],
               out_specs=sp["x"], out_shape=jax.ShapeDtypeStruct((r, wd), F32), vmem=CHUNK * tw * 16,
               )(x, x, w, bias.reshape(1, wd))


def _conv_bwd(dpre, x, w, bsz, nc, *, name, add=None):
    r, wd = x.shape
    tw = _pick(wd, (2048, 1536, 1024, 512, 384, 256, 128))
    sp = _conv_specs(nc, tw)

    def body(*refs):
        d_ref, n_ref, x_ref, p_ref, w_ref = refs[:5]
        add_ref = refs[5] if add is not None else None
        dx_ref, dw_ref, db_ref = refs[-3:]
        b, c = pl.program_id(1), pl.program_id(2)

        @pl.when((b == 0) & (c == 0))
        def _():
            dw_ref[...] = jnp.zeros_like(dw_ref)
            db_ref[...] = jnp.zeros_like(db_ref)

        dv, xv = d_ref[...], x_ref[...]
        row = lax.broadcasted_iota(jnp.int32, xv.shape, 0)
        dhalo = jnp.where(c == nc - 1, 0.0, _tile16(n_ref[...]))
        xhalo = jnp.where(c == 0, 0.0, _tile16(p_ref[...]))
        dx = w_ref[3:4, :] * dv
        for s in (1, 2, 3):
            dx = dx + w_ref[3 - s:4 - s, :] * _shift_up(dv, dhalo, s, row)
        if add_ref is not None:
            dx = dx + add_ref[...]
        dx_ref[...] = _bf(dx)
        db_ref[...] += jnp.sum(dv, axis=0, keepdims=True)
        dw_ref[3:4, :] += jnp.sum(dv * xv, axis=0, keepdims=True)
        for s in (1, 2, 3):
            dw_ref[3 - s:4 - s, :] += jnp.sum(dv * _shift_down(xv, xhalo, s, row), axis=0, keepdims=True)

    ins = [dpre, dpre, x, x, w] + ([add] if add is not None else [])
    specs = [sp["x"], sp["nxt"], sp["x"], sp["prev"], sp["w"]] + ([sp["x"]] if add is not None else [])
    return _pc(body, name=name, grid=(wd // tw, bsz, nc), in_specs=specs, out_specs=[sp["x"], sp["w"], sp["vec"]],
               out_shape=[jax.ShapeDtypeStruct((r, wd), BF16), jax.ShapeDtypeStruct((ML_CONV, wd), F32),
                          jax.ShapeDtypeStruct((1, wd), F32)], vmem=CHUNK * tw * 24)(*ins)


ML_SCALE = ML_DH ** -0.5


def _headwise_expand(w):
    tiled = jnp.tile(w.reshape(ML_HEADS, ML_DH, QKV_BLOCK), (1, 1, ML_DH // QKV_BLOCK))
    blk = jnp.arange(ML_DH) // QKV_BLOCK
    return jnp.where(blk[:, None] == blk[None, :], tiled, 0.0)


def _headwise_extract(w):
    return w[:, :, :QKV_BLOCK].reshape(ML_HEADS * ML_DH // QKV_BLOCK, QKV_BLOCK, QKV_BLOCK)


def _ml_pre(pre, x, wq, wk, wv, wgq, wgk, wgv, bsz, nc):
    r = x.shape[0]
    tr = _pick(r, (256, 128))
    hrow = pl.BlockSpec((tr, ML_DH), lambda h, i: (i, h))
    wexp = pl.BlockSpec((1, ML_DH, ML_DH), lambda h, i: (h, 0, 0))
    wg = pl.BlockSpec((ML_DH, CHUNK), lambda h, i: (h, 0))

    def body(pre_ref, x_ref, wq_ref, wk_ref, wv_ref, gq_ref, gk_ref, gv_ref, q_ref, qs_ref, k_ref, v_ref, gt_ref):
        xcb = _bf(_silu(pre_ref[...]))
        q = _dot(xcb, wq_ref[0])
        k = _dot(xcb, wk_ref[0])
        v = _dot(_bf(x_ref[...]), wv_ref[0])
        qb, kb, vb = _bf(q), _bf(k), _bf(v)
        q_ref[...] = qb
        qs_ref[...] = _bf(q * ML_SCALE)
        k_ref[...] = kb
        v_ref[...] = vb
        gt_ref[0] = _dot(qb, gq_ref[...]) + _dot(kb, gk_ref[...]) + _dot(vb, gv_ref[...])

    o = jax.ShapeDtypeStruct((r, ML_WIDTH), BF16)
    q, qs, k, v, gates8 = _pc(
        body, name="ml_pre", grid=(ML_HEADS, r // tr),
        in_specs=[hrow, hrow, wexp, wexp, wexp, wg, wg, wg],
        out_specs=[hrow, hrow, hrow, hrow, pl.BlockSpec((1, tr, CHUNK), lambda h, i: (h, i, 0))],
        out_shape=[o, o, o, o, jax.ShapeDtypeStruct((ML_HEADS, r, CHUNK), F32)], vmem=6 << 20,
    )(pre, x, wq, wk, wv, wgq, wgk, wgv)

    def sum_body(g_ref, o_ref):
        acc = g_ref[0]
        for j in range(1, ML_HEADS):
            acc = acc + g_ref[j]
        o_ref[...] = acc

    gates = _pc(sum_body, name="ml_gates_sum", grid=(r // tr,),
                in_specs=[pl.BlockSpec((ML_HEADS, tr, CHUNK), lambda i: (0, i, 0))],
                out_specs=pl.BlockSpec((tr, CHUNK), lambda i: (i, 0)),
                out_shape=jax.ShapeDtypeStruct((r, CHUNK), F32), vmem=2 << 20)(gates8)
    return q, qs, k, v, gates


def _tri(rev):
    r = lax.broadcasted_iota(jnp.int32, (CHUNK, CHUNK), 0)
    c = lax.broadcasted_iota(jnp.int32, (CHUNK, CHUNK), 1)
    return jnp.where((c >= r) if rev else (c <= r), 1.0, 0.0).astype(F32)


def _cumsum_rows(x, row, rev=False):
    for k in range(7):
        s = 1 << k
        if rev:
            x = x + jnp.where(row < CHUNK - s, pltpu.roll(x, CHUNK - s, 0), 0.0)
        else:
            x = x + jnp.where(row >= s, pltpu.roll(x, s, 0), 0.0)
    return x


def _log_sigmoid(x):
    return jnp.minimum(x, 0.0) - jnp.log(1.0 + jnp.exp(-jnp.abs(x)))


def _ml_core(gates, hd, first, m, qs, k, v, cmat, nvec):
    sq = (CHUNK, CHUNK)
    lane = lax.broadcasted_iota(jnp.int32, sq, 1)
    row = lax.broadcasted_iota(jnp.int32, sq, 0)
    igc = jnp.sum(jnp.where(lane == hd, gates, 0.0), axis=1, keepdims=True)
    fpc = jnp.sum(jnp.where(lane == hd + ML_HEADS, gates, 0.0), axis=1, keepdims=True)
    valid = jnp.logical_or(jnp.logical_not(first), row[:, :1] >= PAD_ROWS)
    igc = jnp.where(valid, igc, NEG)
    lfc = jnp.where(valid, _log_sigmoid(fpc), 0.0)
    bcb = _cumsum_rows(jnp.broadcast_to(lfc, sq), row)
    igb = jnp.broadcast_to(igc, sq)
    dm = jnp.where(lane <= row, bcb - (bcb - igb).T, NEG)
    bc = bcb[:, :1]
    inter = bc + m
    mt = jnp.maximum(inter, jnp.max(dm, axis=1, keepdims=True))
    wt = jnp.exp(dm - mt)
    wprev = jnp.exp(inter - mt)
    s0 = _dot(qs, k, NT)
    s = s0 * wt
    cb = _bf(cmat)
    qc = _dot(qs, cb)
    qf = qs.astype(F32)
    qn = jnp.sum(qf * nvec, axis=1, keepdims=True)
    num = _dot(_bf(s), v) + wprev * qc
    den = jnp.sum(s, axis=1, keepdims=True) + wprev * qn
    emt = jnp.exp(-mt)
    dd = jnp.maximum(jnp.abs(den), emt)
    blast = bcb[CHUNK - 1:CHUNK, :1]
    g = blast - bc + igc
    m_new = jnp.maximum(blast + m, jnp.max(g, axis=0, keepdims=True))
    decay = jnp.exp(blast + m - m_new)
    e = jnp.exp(g - m_new)
    kf = k.astype(F32)
    wk = e * kf
    return dict(lane=lane, row=row, fpc=fpc, valid=valid, wt=wt, wprev=wprev, s=s, cb=cb, qc=qc, qf=qf, qn=qn,
                num=num, den=den, emt=emt, dd=dd, m_new=m_new, decay=decay, e=e, kf=kf, wk=wk)


def _ml_headnorm(h):
    mu = jnp.mean(h, axis=1, keepdims=True)
    hc = h - mu
    rstd = lax.rsqrt(jnp.mean(hc * hc, axis=1, keepdims=True) + HEAD_NORM_EPS)
    return hc * rstd, rstd


def _ml_chunk_specs(nc, rev, head_major):
    def ix(a, b_, c):
        hd, b = (a, b_) if head_major else (b_, a)
        return hd, b, (nc - 1 - c) if rev else c

    def row(a, b_, c):
        hd, b, cc = ix(a, b_, c)
        return b * nc + cc, hd

    def st(a, b_, c):
        hd, b, cc = ix(a, b_, c)
        return (b * ML_HEADS + hd) * nc + cc

    return dict(
        hrow=pl.BlockSpec((CHUNK, ML_DH), row),
        gates=pl.BlockSpec((CHUNK, CHUNK), lambda a, b_, c: (row(a, b_, c)[0], 0)),
        bias=pl.BlockSpec((1, CHUNK), lambda a, b_, c: (0, 0)),
        hvec=pl.BlockSpec((1, ML_DH), lambda a, b_, c: (0, ix(a, b_, c)[0])),
        cs=pl.BlockSpec((1, ML_DH, ML_DH), lambda a, b_, c: (st(a, b_, c), 0, 0)),
        ns=pl.BlockSpec((1, 1, ML_DH), lambda a, b_, c: (st(a, b_, c), 0, 0)),
        ms=pl.BlockSpec((1, 1, CHUNK), lambda a, b_, c: (st(a, b_, c), 0, 0)),
        dgates=pl.BlockSpec((1, CHUNK, CHUNK), lambda a, b_, c: (ix(a, b_, c)[0], row(a, b_, c)[0], 0)),
    )


def _ml_chunk_fwd(qs, k, v, gates, b_gate, pre, z, nw, sk, bsz, nc):
    r = qs.shape[0]
    sp = _ml_chunk_specs(nc, False, False)

    def body(qs_ref, k_ref, v_ref, gt_ref, bg_ref, pre_ref, z_ref, nw_ref, sk_ref,
             h_ref, yb_ref, cs_ref, ns_ref, ms_ref, c_s, n_s, m_s):
        hd, c = pl.program_id(1), pl.program_id(2)

        @pl.when(c == 0)
        def _():
            c_s[...] = jnp.zeros_like(c_s)
            n_s[...] = jnp.zeros_like(n_s)
            m_s[...] = jnp.zeros_like(m_s)

        cmat, nvec, m = c_s[...], n_s[...], m_s[...]
        cs_ref[0] = cmat
        ns_ref[0] = nvec
        ms_ref[0] = jnp.broadcast_to(m, (1, CHUNK))
        v_ = v_ref[...]
        co = _ml_core(gt_ref[...] + bg_ref[...], hd, c == 0, m, qs_ref[...], k_ref[...], v_, cmat, nvec)
        h = co["num"] / co["dd"]
        h_ref[...] = h
        hn, _ = _ml_headnorm(h)
        yb_ref[...] = _bf((hn * nw_ref[...] + sk_ref[...] * _silu(pre_ref[...])) * _silu(z_ref[...]))
        c_s[...] = co["decay"] * cmat + _dot(_bf(co["wk"]), v_, TN)
        n_s[...] = co["decay"] * nvec + jnp.sum(co["wk"], axis=0, keepdims=True)
        m_s[...] = co["m_new"]

    nst = bsz * ML_HEADS * nc
    return _pc(body, name="ml_chunk_fwd", grid=(bsz, ML_HEADS, nc),
               in_specs=[sp["hrow"]] * 3 + [sp["gates"], sp["bias"], sp["hrow"], sp["hrow"], sp["hvec"], sp["hvec"]],
               out_specs=[sp["hrow"], sp["hrow"], sp["cs"], sp["ns"], sp["ms"]],
               out_shape=[jax.ShapeDtypeStruct((r, ML_WIDTH), F32), jax.ShapeDtypeStruct((r, ML_WIDTH), BF16),
                          jax.ShapeDtypeStruct((nst, ML_DH, ML_DH), F32), jax.ShapeDtypeStruct((nst, 1, ML_DH), F32),
                          jax.ShapeDtypeStruct((nst, 1, CHUNK), F32)],
               scratch=[pltpu.VMEM((ML_DH, ML_DH), F32), pltpu.VMEM((1, ML_DH), F32), pltpu.VMEM((1, 1), F32)],
               vmem=6 << 20)(qs, k, v, gates, b_gate, pre, z, nw, sk)


def _ml_chunk_bwd(dyb, qs, k, v, gates, b_gate, pre, z, nw, sk, h, cs, ns, ms, bsz, nc, dep=None):
    r = qs.shape[0]
    sp = _ml_chunk_specs(nc, True, True)

    def body(dy_ref, qs_ref, k_ref, v_ref, gt_ref, bg_ref, pre_ref, z_ref, nw_ref, sk_ref, h_ref, cs_ref, ns_ref,
             ms_ref, dq_ref, dk_ref, dv_ref, dz_ref, dxc_ref, dgt_ref, dnw_ref, dsk_ref, dc_s, dn_s):
        hd, b, c = pl.program_id(0), pl.program_id(1), pl.program_id(2)

        @pl.when((b == 0) & (c == 0))
        def _():
            dnw_ref[...] = jnp.zeros_like(dnw_ref)
            dsk_ref[...] = jnp.zeros_like(dsk_ref)

        @pl.when(c == 0)
        def _():
            dc_s[...] = jnp.zeros_like(dc_s)
            dn_s[...] = jnp.zeros_like(dn_s)

        qs, k, v = qs_ref[...], k_ref[...], v_ref[...]
        cmat, nvec, m = cs_ref[0], ns_ref[0], ms_ref[0][:, :1]
        co = _ml_core(gt_ref[...] + bg_ref[...], hd, c == nc - 1, m, qs, k, v, cmat, nvec)
        lane, row = co["lane"], co["row"]
        wt, wprev, s, cb, qf = co["wt"], co["wprev"], co["s"], co["cb"], co["qf"]
        h = h_ref[...]
        hn, rstd = _ml_headnorm(h)
        xc = _silu(pre_ref[...])
        zv = z_ref[...]
        nw, sk = nw_ref[...], sk_ref[...]
        dy = dy_ref[...]
        dz_ref[...] = _bf(dy * (hn * nw + sk * xc) * _dsilu(zv))
        do = dy * _silu(zv)
        dsk_ref[...] += jnp.sum(do * xc, axis=0, keepdims=True)
        dnw_ref[...] += jnp.sum(do * hn, axis=0, keepdims=True)
        dxc_ref[...] = do * sk
        dhn = do * nw
        dh = rstd * (dhn - jnp.mean(dhn, axis=1, keepdims=True) - hn * jnp.mean(dhn * hn, axis=1, keepdims=True))
        rinv = 1.0 / co["dd"]
        dnum = dh * rinv
        ddd = -jnp.sum(dh * h, axis=1, keepdims=True) * rinv
        den = co["den"]
        dden = jnp.where(jnp.abs(den) >= co["emt"], ddd * jnp.sign(den), 0.0)
        dnb = _bf(dnum)
        ds = _dot(dnb, v, NT) + dden
        dv = _dot(_bf(s), dnb, TN)
        dnw_ = _bf(dnum * wprev)
        dwn = dden * wprev
        dqs = _dot(dnw_, cb, NT) + dwn * nvec
        dc_out = _dot(qs, dnw_, TN)
        dn_out = jnp.sum(dwn * qf, axis=0, keepdims=True)
        dwprev = jnp.sum(dnum * co["qc"], axis=1, keepdims=True) + dden * co["qn"]
        ds0 = _bf(ds * wt)
        ddm = ds * s
        dqs = dqs + _dot(ds0, k)
        dk = _dot(ds0, qs, TN)
        colc = jnp.sum(ddm.T, axis=1, keepdims=True)
        dbc = dwprev * wprev + jnp.sum(ddm, axis=1, keepdims=True) - colc
        dig = colc
        dcn, dnn = dc_s[...], dn_s[...]
        dcb = _bf(dcn)
        decay, e, kf, wk = co["decay"], co["e"], co["kf"], co["wk"]
        ddecay = (jnp.sum(jnp.sum(dcn * cmat, axis=1, keepdims=True), axis=0, keepdims=True)
                  + jnp.sum(dnn * nvec, axis=1, keepdims=True))
        dwk = _dot(v, dcb, NT) + dnn
        dv = dv + _dot(_bf(wk), dcb)
        dk = dk + e * dwk
        dg = jnp.sum(dwk * kf, axis=1, keepdims=True) * e
        dblast = ddecay * decay + jnp.sum(dg, axis=0, keepdims=True)
        dbc = dbc - dg + jnp.where(row[:, :1] == CHUNK - 1, dblast, 0.0)
        dig = dig + dg
        dc_s[...] = decay * dcn + dc_out
        dn_s[...] = decay * dnn + dn_out
        dlf = _cumsum_rows(jnp.broadcast_to(dbc, (CHUNK, CHUNK)), row, rev=True)[:, :1]
        dfp = dlf * (1.0 - jax.nn.sigmoid(co["fpc"]))
        dig = jnp.where(co["valid"], dig, 0.0)
        dfp = jnp.where(co["valid"], dfp, 0.0)
        dgt_ref[0] = jnp.where(lane == hd, dig, 0.0) + jnp.where(lane == hd + ML_HEADS, dfp, 0.0)
        dq_ref[...] = _bf(dqs * ML_SCALE)
        dk_ref[...] = _bf(dk)
        dv_ref[...] = _bf(dv)

    ob = jax.ShapeDtypeStruct((r, ML_WIDTH), BF16)
    return _pc(body, name="ml_chunk_bwd", grid=(ML_HEADS, bsz, nc),
               in_specs=[sp["hrow"]] * 4 + [sp["gates"], sp["bias"], sp["hrow"], sp["hrow"], sp["hvec"], sp["hvec"],
                                            sp["hrow"], sp["cs"], sp["ns"], sp["ms"]],
               out_specs=[sp["hrow"]] * 5 + [sp["dgates"], sp["hvec"], sp["hvec"]],
               out_shape=[ob, ob, ob, ob, jax.ShapeDtypeStruct((r, ML_WIDTH), F32),
                          jax.ShapeDtypeStruct((ML_HEADS, r, CHUNK), F32),
                          jax.ShapeDtypeStruct((1, ML_WIDTH), F32), jax.ShapeDtypeStruct((1, ML_WIDTH), F32)],
               scratch=[pltpu.VMEM((ML_DH, ML_DH), F32), pltpu.VMEM((1, ML_DH), F32)], vmem=8 << 20, dep=dep,
               )(dyb, qs, k, v, gates, b_gate, pre, z, nw, sk, h, cs, ns, ms)


def _ml_pre_bwd(dq, dk, dv, dgates, dxc_skip, pre, x, q, k, v, wq, wk, wv, wgq, wgk, wgv, bsz, nc):
    r = x.shape[0]
    hrow = pl.BlockSpec((CHUNK, ML_DH), lambda h, b, c: (b * nc + c, h))
    wexp = pl.BlockSpec((1, ML_DH, ML_DH), lambda h, b, c: (h, 0, 0))
    wcmp = pl.BlockSpec((1, ML_DH, CHUNK), lambda h, b, c: (h, 0, 0))
    wg = pl.BlockSpec((ML_DH, CHUNK), lambda h, b, c: (h, 0))
    dgs = pl.BlockSpec((ML_HEADS, CHUNK, CHUNK), lambda h, b, c: (0, b * nc + c, 0))
    bgs = pl.BlockSpec((1, 1, CHUNK), lambda h, b, c: (h, 0, 0))

    def body(dq_ref, dk_ref, dv_ref, dg_ref, dxs_ref, pre_ref, x_ref, q_ref, k_ref, v_ref, wq_ref, wk_ref, wv_ref,
             gq_ref, gk_ref, gv_ref, dpre_ref, dxv_ref, cq_ref, ck_ref, cv_ref, dgq_ref, dgk_ref, dgv_ref, dbg_ref,
             dwq_ref, dwk_ref, dwv_ref):
        b, c = pl.program_id(1), pl.program_id(2)

        @pl.when((b == 0) & (c == 0))
        def _():
            for ref in (dwq_ref, dwk_ref, dwv_ref, dgq_ref, dgk_ref, dgv_ref, dbg_ref):
                ref[...] = jnp.zeros_like(ref)

        dgt = dg_ref[0]
        for j in range(1, ML_HEADS):
            dgt = dgt + dg_ref[j]
        dbg_ref[0] += jnp.sum(dgt, axis=0, keepdims=True)
        dgb = _bf(dgt)
        dqt = _bf(dq_ref[...].astype(F32) + _dot(dgb, gq_ref[...], NT))
        dkt = _bf(dk_ref[...].astype(F32) + _dot(dgb, gk_ref[...], NT))
        dvt = _bf(dv_ref[...].astype(F32) + _dot(dgb, gv_ref[...], NT))
        dgq_ref[...] += _dot(q_ref[...], dgb, TN)
        dgk_ref[...] += _dot(k_ref[...], dgb, TN)
        dgv_ref[...] += _dot(v_ref[...], dgb, TN)
        prev = pre_ref[...]
        xcb = _bf(_silu(prev))
        xb = _bf(x_ref[...])
        dwq_ref[...] += _dot(xcb, dqt, TN)
        dwk_ref[...] += _dot(xcb, dkt, TN)
        dwv_ref[...] += _dot(xb, dvt, TN)
        dxc = _dot(dqt, wq_ref[0], NT) + _dot(dkt, wk_ref[0], NT) + dxs_ref[...]
        dpre_ref[...] = dxc * _dsilu(prev)
        dxv_ref[...] = _dot(dvt, wv_ref[0], NT)

        @pl.when((b == bsz - 1) & (c == nc - 1))
        def _():
            rr = lax.broadcasted_iota(jnp.int32, (ML_DH, ML_DH), 0)
            cc = lax.broadcasted_iota(jnp.int32, (ML_DH, ML_DH), 1)
            diag = rr // QKV_BLOCK == cc // QKV_BLOCK
            fc = lax.broadcasted_iota(jnp.int32, (ML_DH, CHUNK), 0)
            fo = lax.broadcasted_iota(jnp.int32, (ML_DH, CHUNK), 1)
            fold = jnp.where(fc % QKV_BLOCK == fo, 1.0, 0.0).astype(F32)
            for src, dst in ((dwq_ref, cq_ref), (dwk_ref, ck_ref), (dwv_ref, cv_ref)):
                dst[0] = jnp.dot(jnp.where(diag, src[...], 0.0), fold, precision=HI, preferred_element_type=F32)

    f = jax.ShapeDtypeStruct((r, ML_WIDTH), F32)
    wc = jax.ShapeDtypeStruct((ML_HEADS, ML_DH, CHUNK), F32)
    wgs = jax.ShapeDtypeStruct((ML_WIDTH, CHUNK), F32)
    return _pc(body, name="ml_pre_bwd", grid=(ML_HEADS, bsz, nc),
               in_specs=[hrow, hrow, hrow, dgs, hrow, hrow, hrow, hrow, hrow, hrow, wexp, wexp, wexp, wg, wg, wg],
               out_specs=[hrow, hrow, wcmp, wcmp, wcmp, wg, wg, wg, bgs],
               out_shape=[f, f, wc, wc, wc, wgs, wgs, wgs, jax.ShapeDtypeStruct((ML_HEADS, 1, CHUNK), F32)],
               scratch=[pltpu.VMEM((ML_DH, ML_DH), F32)] * 3,
               vmem=8 << 20)(dq, dk, dv, dgates, dxc_skip, pre, x, q, k, v, wq, wk, wv, wgq, wgk, wgv)


def _pad_lanes(w):
    return jnp.pad(w, ((0, 0), (0, CHUNK - w.shape[1])))


def _ml_weights(conv_w, conv_b, wq, wk, wv, w_gate, b_gate, norm_w, skip):
    return dict(
        conv_w=conv_w, conv_b=conv_b,
        wq=_bf(_headwise_expand(wq)), wk=_bf(_headwise_expand(wk)), wv=_bf(_headwise_expand(wv)),
        wgq=_bf(_pad_lanes(w_gate[:ML_WIDTH])), wgk=_bf(_pad_lanes(w_gate[ML_WIDTH:2 * ML_WIDTH])),
        wgv=_bf(_pad_lanes(w_gate[2 * ML_WIDTH:])), b_gate=_pad_lanes(b_gate.reshape(1, -1)),
        norm=norm_w.reshape(1, ML_WIDTH), skip=skip.reshape(1, ML_WIDTH))


def _ml_layer_fwd(x, z, w, bsz, nc):
    pre = _conv_fwd(x, w["conv_w"], w["conv_b"], bsz, nc, name="ml_conv")
    q, qs, k, v, gates = _ml_pre(pre, x, w["wq"], w["wk"], w["wv"], w["wgq"], w["wgk"], w["wgv"], bsz, nc)
    h, yb, cs, ns, ms = _ml_chunk_fwd(qs, k, v, gates, w["b_gate"], pre, z, w["norm"], w["skip"], bsz, nc)
    return yb, dict(pre=pre, q=q, qs=qs, k=k, v=v, gates=gates, h=h, cs=cs, ns=ns, ms=ms)


def _ml_layer_bwd(dyb, x, z, sv, w, bsz, nc, dep=None):
    dq, dk, dv, dz, dxc, dgates, dnw, dsk = _ml_chunk_bwd(
        dyb, sv["qs"], sv["k"], sv["v"], sv["gates"], w["b_gate"], sv["pre"], z, w["norm"], w["skip"], sv["h"],
        sv["cs"], sv["ns"], sv["ms"], bsz, nc, dep=dep)
    dpre, dxv, dwq, dwk, dwv, dgq, dgk, dgv, dbg = _ml_pre_bwd(
        dq, dk, dv, dgates, dxc, sv["pre"], x, sv["q"], sv["k"], sv["v"], w["wq"], w["wk"], w["wv"], w["wgq"],
        w["wgk"], w["wgv"], bsz, nc)
    dx, dcw, dcb = _conv_bwd(dpre, x, w["conv_w"], bsz, nc, name="ml_conv_bwd", add=dxv)
    ng = 2 * ML_HEADS
    grads = dict(
        ml_conv_w=dcw, ml_conv_b=dcb, ml_wq=_headwise_extract(dwq), ml_wk=_headwise_extract(dwk),
        ml_wv=_headwise_extract(dwv), ml_w_gate=jnp.concatenate([dgq[:, :ng], dgk[:, :ng], dgv[:, :ng]], axis=0),
        ml_b_gate=dbg[0][:, :ng], ml_norm=dnw, ml_skip=dsk)
    return dx, dz, grads


HI = lax.Precision.HIGHEST


def _softplus(x):
    return jnp.maximum(x, 0.0) + jnp.log(1.0 + jnp.exp(-jnp.abs(x)))


def _lane_cumsum(x, lane, rev=False):
    del lane
    return jnp.dot(x, _tri(not rev), precision=lax.Precision.HIGHEST, preferred_element_type=F32)


def _head_sum_matrix():
    r = lax.broadcasted_iota(jnp.int32, (SSD_HPG, SSD_GW), 0)
    l = lax.broadcasted_iota(jnp.int32, (SSD_HPG, SSD_GW), 1)
    return jnp.where(l // SSD_P == r, 1.0, 0.0).astype(F32)


def _ssd_core(xs, bm, cm, dt_raw, dt_bias, a_log, first):
    sq = (CHUNK, CHUNK)
    lane8 = lax.broadcasted_iota(jnp.int32, (SSD_HPG, CHUNK), 1)
    lane = lax.broadcasted_iota(jnp.int32, sq, 1)
    row = lax.broadcasted_iota(jnp.int32, sq, 0)
    low = lane < SSD_P
    valid = jnp.logical_or(jnp.logical_not(first), lane8 >= PAD_ROWS)
    pre = dt_raw + dt_bias
    dt = jnp.where(valid, _softplus(pre), 0.0)
    a = -jnp.exp(a_log)
    cum = _lane_cumsum(dt * a, lane8)
    cb = _dot(_bf(cm), _bf(bm), NT)
    heads = []
    for r in range(SSD_HPG):
        rowb = jnp.broadcast_to(cum[r:r + 1, :], sq)
        colb = rowb.T
        seg = jnp.exp(jnp.where(lane <= row, colb - rowb, NEG))
        dtrow = jnp.broadcast_to(dt[r:r + 1, :], sq)
        lastb = colb[CHUNK - 1:CHUNK, :]
        heads.append(dict(seg=seg, dtrow=dtrow, w=cb * seg * dtrow, ecol=jnp.exp(colb),
                          dec=jnp.exp(lastb - colb) * dtrow.T, elast=jnp.exp(lastb)))

    def pairs(key):
        return jnp.concatenate([jnp.where(low[:heads[0][key].shape[0]], heads[2 * j][key], heads[2 * j + 1][key])
                                for j in range(SSD_HPG // 2)], axis=1)

    return dict(lane8=lane8, low=low, valid=valid, pre=pre, dt=dt, a=a, cum=cum, cb=cb, heads=heads,
                expc=pairs("ecol"), dec=pairs("dec"), elast=pairs("elast"))


def _ssd_specs(nc, rev, group_major):
    def ix(a, b_, c):
        g, b = (a, b_) if group_major else (b_, a)
        return g, b, (nc - 1 - c) if rev else c

    def row(a, b_, c):
        g, b, cc = ix(a, b_, c)
        return b * nc + cc, g

    return dict(
        wide=pl.BlockSpec((CHUNK, SSD_GW), row),
        narrow=pl.BlockSpec((CHUNK, SSD_N), row),
        dtT=pl.BlockSpec((SSD_HPG, CHUNK), lambda a, b_, c: (ix(a, b_, c)[0], row(a, b_, c)[0])),
        hcol=pl.BlockSpec((SSD_HPG, 1), lambda a, b_, c: (ix(a, b_, c)[0], 0)),
        hacc=pl.BlockSpec((SSD_HPG, CHUNK), lambda a, b_, c: (ix(a, b_, c)[0], 0)),
        gvec=pl.BlockSpec((1, SSD_GW), lambda a, b_, c: (0, ix(a, b_, c)[0])),
        state=pl.BlockSpec((1, SSD_N, SSD_GW),
                           lambda a, b_, c: ((ix(a, b_, c)[1] * SSD_GROUPS + ix(a, b_, c)[0]) * nc + ix(a, b_, c)[2], 0, 0)),
    )


def _ssd_chunk_fwd(xs_pre, bm_pre, cm_pre, dt_raw, dt_bias, a_log, d_exp, z, gnorm, bsz, nc):
    r = xs_pre.shape[0]
    sp = _ssd_specs(nc, False, False)

    def body(xs_ref, bm_ref, cm_ref, dt_ref, db_ref, al_ref, d_ref, z_ref, gn_ref, y_ref, yn_ref, st_ref, st_s):
        c = pl.program_id(2)

        @pl.when(c == 0)
        def _():
            st_s[...] = jnp.zeros_like(st_s)

        state = st_s[...]
        st_ref[0] = state
        xs, bm, cm = _silu(xs_ref[...]), _silu(bm_ref[...]), _silu(cm_ref[...])
        co = _ssd_core(xs, bm, cm, dt_ref[...], db_ref[...], al_ref[...], c == 0)
        low, hd = co["low"], co["heads"]
        ys = []
        for j in range(SSD_HPG // 2):
            xp = xs[:, j * CHUNK:(j + 1) * CHUNK]
            lhs = jnp.concatenate([hd[2 * j]["w"], hd[2 * j + 1]["w"]], axis=1)
            rhs = jnp.concatenate([jnp.where(low, xp, 0.0), jnp.where(low, 0.0, xp)], axis=0)
            ys.append(_dot(_bf(lhs), _bf(rhs)))
        cmb = _bf(cm)
        y = jnp.concatenate(ys, axis=1) + co["expc"] * _dot(cmb, _bf(state)) + d_ref[...] * xs
        y_ref[...] = y
        yg = y * _silu(z_ref[...])
        rstd = lax.rsqrt(jnp.mean(yg * yg, axis=1, keepdims=True) + NORM_EPS)
        yn_ref[...] = _bf(yg * rstd * gn_ref[...])
        st_s[...] = co["elast"] * state + _dot(_bf(bm), _bf(xs * co["dec"]), TN)

    nst = bsz * SSD_GROUPS * nc
    return _pc(body, name="ssd_chunk_fwd", grid=(bsz, SSD_GROUPS, nc),
               in_specs=[sp["wide"], sp["narrow"], sp["narrow"], sp["dtT"], sp["hcol"], sp["hcol"], sp["gvec"],
                         sp["wide"], sp["gvec"]],
               out_specs=[sp["wide"], sp["wide"], sp["state"]],
               out_shape=[jax.ShapeDtypeStruct((r, SSD_INNER), F32), jax.ShapeDtypeStruct((r, SSD_INNER), BF16),
                          jax.ShapeDtypeStruct((nst, SSD_N, SSD_GW), F32)],
               scratch=[pltpu.VMEM((SSD_N, SSD_GW), F32)], vmem=6 << 20,
               )(xs_pre, bm_pre, cm_pre, dt_raw, dt_bias, a_log, d_exp, z, gnorm)


def _ssd_chunk_bwd(dyn, xs_pre, bm_pre, cm_pre, dt_raw, dt_bias, a_log, d_exp, z, gnorm, y, states, bsz, nc):
    r = xs_pre.shape[0]
    sp = _ssd_specs(nc, True, True)

    def body(dyn_ref, xs_ref, bm_ref, cm_ref, dt_ref, db_ref, al_ref, d_ref, z_ref, gn_ref, y_ref, st_ref,
             dxs_ref, dbm_ref, dcm_ref, dz_ref, ddt_ref, dgn_ref, dd_ref, dbias_ref, dal_ref, ds_s):
        b, c = pl.program_id(1), pl.program_id(2)

        @pl.when((b == 0) & (c == 0))
        def _():
            for ref in (dgn_ref, dd_ref, dbias_ref, dal_ref):
                ref[...] = jnp.zeros_like(ref)

        @pl.when(c == 0)
        def _():
            ds_s[...] = jnp.zeros_like(ds_s)

        xs_p, bm_p, cm_p = xs_ref[...], bm_ref[...], cm_ref[...]
        xs, bm, cm = _silu(xs_p), _silu(bm_p), _silu(cm_p)
        state = st_ref[0]
        co = _ssd_core(xs, bm, cm, dt_ref[...], db_ref[...], al_ref[...], c == nc - 1)
        low, hd, lane8, cb = co["low"], co["heads"], co["lane8"], co["cb"]
        dt, a, cum = co["dt"], co["a"], co["cum"]
        sub8 = lax.broadcasted_iota(jnp.int32, (SSD_HPG, CHUNK), 0)
        eh = _head_sum_matrix()

        def head_rows(full):
            return lax.dot_general(eh, full, NT, precision=HI, preferred_element_type=F32)

        def head_col(vec):
            return jnp.sum(eh * vec, axis=1, keepdims=True)

        yv, zv, gn = y_ref[...], z_ref[...], gn_ref[...]
        sz = _silu(zv)
        yg = yv * sz
        rstd = lax.rsqrt(jnp.mean(yg * yg, axis=1, keepdims=True) + NORM_EPS)
        yh = yg * rstd
        dyn = dyn_ref[...]
        dgn_ref[...] += jnp.sum(dyn * yh, axis=0, keepdims=True)
        dyh = dyn * gn
        dyg = rstd * (dyh - yh * jnp.mean(dyh * yh, axis=1, keepdims=True))
        dz_ref[...] = _bf(dyg * yv * _dsilu(zv))
        dy = dyg * sz
        dxs = dy * d_ref[...]
        dd_ref[...] += head_col(jnp.sum(dy * xs, axis=0, keepdims=True))
        cmb, bmb, stb = _bf(cm), _bf(bm), _bf(state)
        ysv = _dot(cmb, stb)
        expc = co["expc"]
        dys = _bf(dy * expc)
        dcum = head_rows(dy * ysv * expc)
        dcm = _dot(dys, stb, NT)
        dstate_out = _dot(cmb, dys, TN)
        dcb = jnp.zeros((CHUNK, CHUNK), F32)
        ddt = jnp.zeros((SSD_HPG, CHUNK), F32)
        dxs_pairs = []
        for j in range(SSD_HPG // 2):
            sl = slice(j * CHUNK, (j + 1) * CHUNK)
            dyp, xp = dy[:, sl], _bf(xs[:, sl])
            lhs = _bf(jnp.concatenate([hd[2 * j]["w"], hd[2 * j + 1]["w"]], axis=1))
            both = _dot(lhs, _bf(dyp), TN)
            dxs_pairs.append(jnp.where(low, both[:CHUNK], both[CHUNK:]))
            for q, msk in ((2 * j, low), (2 * j + 1, jnp.logical_not(low))):
                h = hd[q]
                dw = _dot(_bf(jnp.where(msk, dyp, 0.0)), xp, NT)
                dcb = dcb + dw * h["seg"] * h["dtrow"]
                e_ = dw * h["w"]
                dcum_r = jnp.sum(e_.T, axis=0, keepdims=True) - jnp.sum(e_, axis=0, keepdims=True)
                ddt_r = jnp.sum(dw * cb * h["seg"], axis=0, keepdims=True)
                dcum = dcum + jnp.where(sub8 == q, dcum_r, 0.0)
                ddt = ddt + jnp.where(sub8 == q, ddt_r, 0.0)
        dxs = dxs + jnp.concatenate(dxs_pairs, axis=1)
        dcbb = _bf(dcb)
        dcm = dcm + _dot(dcbb, bmb)
        dbm = _dot(dcbb, cmb, TN)
        dsn = ds_s[...]
        dsb = _bf(dsn)
        dec = co["dec"]
        dbm = dbm + _dot(_bf(xs * dec), dsb, NT)
        dxd = _dot(bmb, dsb)
        dxs = dxs + dxd * dec
        ddec = head_rows(dxd * xs)
        last = cum[:, CHUNK - 1:CHUNK]
        erow = jnp.exp(last - cum)
        ddt = ddt + ddec * erow
        dla = ddec * erow * dt
        dlast = (jnp.sum(dla, axis=1, keepdims=True)
                 + head_col(jnp.sum(dsn * state, axis=0, keepdims=True)) * jnp.exp(last))
        dcum = dcum - dla + jnp.where(lane8 == CHUNK - 1, dlast, 0.0)
        ds_s[...] = co["elast"] * dsn + dstate_out
        dda = _lane_cumsum(dcum, lane8, rev=True)
        ddt = jnp.where(co["valid"], ddt + dda * a, 0.0)
        ddt_raw = ddt * jax.nn.sigmoid(co["pre"])
        ddt_ref[...] = ddt_raw
        dbias_ref[...] += jnp.sum(ddt_raw, axis=1, keepdims=True)
        dal_ref[...] += jnp.sum(dda * dt, axis=1, keepdims=True) * a
        dxs_ref[...] = dxs * _dsilu(xs_p)
        dbm_ref[...] = dbm * _dsilu(bm_p)
        dcm_ref[...] = dcm * _dsilu(cm_p)

    st = jax.ShapeDtypeStruct
    hacc = st((SSD_HEADS, CHUNK), F32)
    return _pc(body, name="ssd_chunk_bwd", grid=(SSD_GROUPS, bsz, nc),
               in_specs=[sp["wide"], sp["wide"], sp["narrow"], sp["narrow"], sp["dtT"], sp["hcol"], sp["hcol"],
                         sp["gvec"], sp["wide"], sp["gvec"], sp["wide"], sp["state"]],
               out_specs=[sp["wide"], sp["narrow"], sp["narrow"], sp["wide"], sp["dtT"], sp["gvec"], sp["hacc"],
                          sp["hacc"], sp["hacc"]],
               out_shape=[st((r, SSD_INNER), F32), st((r, SSD_GROUPS * SSD_N), F32), st((r, SSD_GROUPS * SSD_N), F32),
                          st((r, SSD_INNER), BF16), st((SSD_HEADS, r), F32), st((1, SSD_INNER), F32), hacc, hacc, hacc],
               scratch=[pltpu.VMEM((SSD_N, SSD_GW), F32)], vmem=10 << 20,
               )(dyn, xs_pre, bm_pre, cm_pre, dt_raw, dt_bias, a_log, d_exp, z, gnorm, y, states)


SSD_BC = SSD_GROUPS * SSD_N


def _ssd_weights(conv_w, conv_b, dt_bias, a_log, d, gnorm):
    cuts = (0, SSD_INNER, SSD_INNER + SSD_BC, SSD_INNER + 2 * SSD_BC)
    return dict(
        conv_w=[conv_w[:, cuts[i]:cuts[i + 1]] for i in range(3)],
        conv_b=[conv_b[cuts[i]:cuts[i + 1]] for i in range(3)],
        dt_bias=dt_bias.reshape(SSD_HEADS, 1), a_log=a_log.reshape(SSD_HEADS, 1),
        d_exp=jnp.repeat(d.reshape(SSD_HEADS), SSD_P).reshape(1, SSD_INNER), gnorm=gnorm.reshape(1, SSD_INNER))


def _ssd_layer_fwd(z, xs_in, bm_in, cm_in, dt_rows, w, bsz, nc):
    pres = [_conv_fwd(a, w["conv_w"][i], w["conv_b"][i], bsz, nc, name=f"ssd_conv{i}")
            for i, a in enumerate((xs_in, bm_in, cm_in))]
    dt_t = dt_rows[:, :SSD_HEADS].T
    y, yn, states = _ssd_chunk_fwd(pres[0], pres[1], pres[2], dt_t, w["dt_bias"], w["a_log"], w["d_exp"], z,
                                   w["gnorm"], bsz, nc)
    return yn, dict(pres=pres, dt_t=dt_t, y=y, states=states)


def _ssd_layer_bwd(dyn, z, xs_in, bm_in, cm_in, sv, w, bsz, nc):
    pres = sv["pres"]
    dxs_p, dbm_p, dcm_p, dz, ddt_t, dgn, dd, dbias, dal = _ssd_chunk_bwd(
        dyn, pres[0], pres[1], pres[2], sv["dt_t"], w["dt_bias"], w["a_log"], w["d_exp"], z, w["gnorm"], sv["y"],
        sv["states"], bsz, nc)
    outs = [_conv_bwd(dp, a, w["conv_w"][i], bsz, nc, name=f"ssd_conv_bwd{i}")
            for i, (dp, a) in enumerate(((dxs_p, xs_in), (dbm_p, bm_in), (dcm_p, cm_in)))]
    ddt = _bf(_pad_lanes(ddt_t.T))
    grads = dict(
        ssd_conv_w=jnp.concatenate([o[1] for o in outs], axis=1),
        ssd_conv_b=jnp.concatenate([o[2] for o in outs], axis=1),
        ssd_dt_bias=dbias[:, 0], ssd_a_log=dal[:, 0], ssd_d=dd[:, 0], ssd_gnorm=dgn)
    return dz, outs[0][0], outs[1][0], outs[2][0], ddt, grads


WNAMES = ("meta_tokens", "ab_norm", "ab_w_in", "s5_lambda_re", "s5_lambda_im", "s5_log_dt", "s5_b_re", "s5_b_im",
          "s5_c_re", "s5_c_im", "s5_d", "s5_glu_w", "s5_glu_b", "ml_conv_w", "ml_conv_b", "ml_wq", "ml_wk", "ml_wv",
          "ml_w_gate", "ml_b_gate", "ml_norm", "ml_skip", "ab_w_out", "ssd_norm", "ssd_w_in", "ssd_conv_w",
          "ssd_conv_b", "ssd_dt_bias", "ssd_a_log", "ssd_d", "ssd_gnorm", "ssd_w_out", "final_norm")
SHARD_AXIS = dict(meta_tokens=1, ab_w_in=2, s5_glu_w=1, ml_conv_w=2, ml_wq=1, ml_wk=1, ml_wv=1, ml_w_gate=1,
                  ab_w_out=1, ssd_norm=1, ssd_w_in=2, ssd_conv_w=2, ssd_conv_b=1, ssd_gnorm=1, ssd_w_out=1)
BIG = ("ab_w_in", "s5_glu_w", "ab_w_out", "ssd_w_in", "ssd_w_out")
SMALL = tuple(n for n in WNAMES if n in SHARD_AXIS and n not in BIG)
REPL = tuple(n for n in WNAMES if n not in SHARD_AXIS)
PACK_ALIGN = 8 * 128


def _pack(arrs):
    lead = arrs[0][1]
    parts = []
    for a, nlead in arrs:
        f = a.reshape(a.shape[:nlead] + (-1,))
        f = jnp.pad(f, [(0, 0)] * nlead + [(0, (-f.shape[-1]) % PACK_ALIGN)])
        parts.append(f.reshape(f.shape[:nlead] + (-1, 128)))
    return jnp.concatenate(parts, axis=lead)


def _unpack(p, shapes):
    out, r0 = [], 0
    lead = p.shape[:-2]
    for s in shapes:
        n = math.prod(s)
        rows = -(-n // PACK_ALIGN) * 8
        seg = p[..., r0:r0 + rows, :].reshape(lead + (rows * 128,))[..., :n]
        out.append(seg.reshape(lead + tuple(s)))
        r0 += rows
    return out


def _assemble(g, axis):
    m = jnp.moveaxis(g, 0, axis)
    return m.reshape(m.shape[:axis] + (m.shape[axis] * m.shape[axis + 1],) + m.shape[axis + 2:])


def _split(full, axis):
    s = full.shape
    m = full.reshape(s[:axis] + (N_DEV, s[axis] // N_DEV) + s[axis + 1:])
    return jnp.moveaxis(m, axis, 0)


def kernel(x, *rest):
    nw = len(WNAMES)
    w = dict(zip(WNAMES, rest[:nw]))
    loss_target = rest[nw]
    mom = dict(zip(WNAMES, rest[nw + 1:2 * nw + 1]))
    var = dict(zip(WNAMES, rest[2 * nw + 1:3 * nw + 1]))
    bsz = x.shape[0]
    nc = 1 + SEQ // CHUNK
    tp = nc * CHUNK

    local = {n: _bf(w[n][0]) for n in BIG}
    small_local = _pack([(w[n], 0) for n in SMALL])
    gs = _exchange_start([small_local], ["ag"], name="gather_s")
    ga = _exchange_start([local["ab_w_in"]], ["ag"], name="gather_a", dep=gs["token"])
    got_s = _exchange_wait(gs, ga["token"])

    def assemble_big(n, got):
        return _assemble(got[:, None], SHARD_AXIS[n])[0]

    full = {}
    for n, g in zip(SMALL, _unpack(got_s[0], [w[n].shape for n in SMALL])):
        full[n] = _assemble(g, SHARD_AXIS[n])[0] if n != "meta_tokens" else _assemble(g, SHARD_AXIS[n])
    for n in REPL:
        full[n] = w[n][0] if n != "final_norm" else w[n]
    glu_b = full["s5_glu_b"].reshape(1, S5_WIDTH)
    meta = jnp.broadcast_to(full["meta_tokens"][None], (bsz, N_META, D_MODEL))
    h0 = jnp.concatenate([jnp.zeros((bsz, PAD_ROWS, D_MODEL), F32), meta, x], axis=1).reshape(bsz * tp, D_MODEL)
    xn0 = _rms_fwd(h0, full["ab_norm"], name="rms0")
    s5p, s5_vjp = _s5_tables(*[full[n] for n in ("s5_lambda_re", "s5_lambda_im", "s5_log_dt", "s5_b_re", "s5_b_im",
                                                   "s5_c_re", "s5_c_im", "s5_d")])
    mlw = _ml_weights(*[full[n] for n in ("ml_conv_w", "ml_conv_b", "ml_wq", "ml_wk", "ml_wv", "ml_w_gate",
                                           "ml_b_gate", "ml_norm", "ml_skip")])
    got_a = _exchange_wait(ga, [xn0, s5p["wbr"], s5p["wcr"], s5p["qr"], s5p["sr"], mlw["wq"], mlw["wk"], mlw["wv"],
                                mlw["wgq"]])
    gb = _exchange_start([local["s5_glu_w"], local["ab_w_out"]], ["ag", "ag"], name="gather_b", dep=got_a[0])
    gc = _exchange_start([local["ssd_w_in"], local["ssd_w_out"]], ["ag", "ag"], name="gather_c", dep=gb["token"])
    full["ab_w_in"] = assemble_big("ab_w_in", got_a[0])
    cuts0 = (0, S5_WIDTH, 2 * S5_WIDTH, 2 * S5_WIDTH + ML_WIDTH, 2 * (S5_WIDTH + ML_WIDTH))
    w_in0 = [full["ab_w_in"][:, cuts0[i]:cuts0[i + 1]] for i in range(4)]

    u, za, xb, zb = [_mm(xn0, wi, "NN", name=f"in0_{i}") for i, wi in enumerate(w_in0)]
    got_b = []

    def glu_w_after(scan_out):
        got_b.extend(_exchange_wait(gb, scan_out))
        return assemble_big("s5_glu_w", got_b[0])

    sv5 = _s5_layer_fwd(u, s5p, glu_w_after, bsz, nc)
    glu_w = assemble_big("s5_glu_w", got_b[0])
    w_out0 = assemble_big("ab_w_out", got_b[1])
    w_out0 = [w_out0[:S5_WIDTH], w_out0[S5_WIDTH:]]
    ya = _s5_post(sv5["y1"], sv5["glu_pre"], glu_b, za)
    yb, svm = _ml_layer_fwd(xb, zb, mlw, bsz, nc)
    h1 = _mm(ya, w_out0[0], "NN", name="out0_a", add=h0)
    h1 = _mm(yb, w_out0[1], "NN", name="out0_b", add=h1)
    got_c = _exchange_wait(gc, h1)
    w_in1, w_out1 = assemble_big("ssd_w_in", got_c[0]), assemble_big("ssd_w_out", got_c[1])
    cuts1 = (0, SSD_INNER, 2 * SSD_INNER, 2 * SSD_INNER + SSD_BC, 2 * SSD_INNER + 2 * SSD_BC)
    w_in1 = [w_in1[:, cuts1[i]:cuts1[i + 1]] for i in range(4)] + [_pad_lanes(w_in1[:, cuts1[4]:])]
    xn1 = _rms_fwd(h1, full["ssd_norm"], name="rms1")
    z1, xs_in, bm_in, cm_in, dt_rows = [_mm(xn1, wi, "NN", name=f"in1_{i}") for i, wi in enumerate(w_in1)]
    ssdw = _ssd_weights(*[full[n] for n in ("ssd_conv_w", "ssd_conv_b", "ssd_dt_bias", "ssd_a_log", "ssd_d",
                                             "ssd_gnorm")])
    yn, svs = _ssd_layer_fwd(z1, xs_in, bm_in, cm_in, dt_rows, ssdw, bsz, nc)
    h2 = _mm(yn, w_out1, "NN", name="out1", add=h1)
    loss_part, dh2, dfinal = _final_loss(h2, full["final_norm"], loss_target, bsz, nc)
    loss = lax.psum(loss_part[0, 0], ("x", "y", "c"))

    g = {"final_norm": dfinal}
    dyn = _mm(dh2, w_out1, "NT", name="d_out1")
    g["ssd_w_out"] = _mm(yn, dh2, "TN", name="dw_out1", out_dtype=BF16)
    dz1, dxs, dbm, dcm, ddt, gs = _ssd_layer_bwd(dyn, z1, xs_in, bm_in, cm_in, svs, ssdw, bsz, nc)
    g.update(gs)
    dps1 = (dz1, dxs, dbm, dcm, ddt)
    dxn1 = None
    for i, (dp, wi) in enumerate(zip(dps1, w_in1)):
        dxn1 = _mm(dp, wi, "NT", name=f"d_in1_{i}", add=dxn1)
    dw1 = [_mm(xn1, dp, "TN", name=f"dw_in1_{i}", out_dtype=BF16) for i, dp in enumerate(dps1)]
    g["ssd_w_in"] = jnp.concatenate(dw1[:4] + [dw1[4][:, :SSD_HEADS]], axis=1)

    def local_shape(n):
        return w[n].shape

    def slabs(n):
        gf = g[n].reshape((1,) + tuple(g[n].shape)) if n != "meta_tokens" else g[n]
        full_shape = tuple(d * (N_DEV if i == SHARD_AXIS[n] else 1) for i, d in enumerate(local_shape(n)))
        return _split(gf.reshape(full_shape), SHARD_AXIS[n])

    x1 = _exchange_start([slabs("ssd_w_in")[:, 0], slabs("ssd_w_out")[:, 0]], ["a2a", "a2a"], name="grads_1")
    dh1, g["ssd_norm"] = _rms_bwd(h1, full["ssd_norm"], dxn1, dh2, name="rms1_bwd", dep=x1["token"])
    dya = _mm(dh1, w_out0[0], "NT", name="d_out0_a")
    dyb = _mm(dh1, w_out0[1], "NT", name="d_out0_b")
    g["ab_w_out"] = jnp.concatenate([_mm(ya, dh1, "TN", name="dw_out0_a", out_dtype=BF16),
                                     _mm(yb, dh1, "TN", name="dw_out0_b", out_dtype=BF16)], axis=0)
    du, dza, g5 = _s5_layer_bwd(dya, u, za, sv5, s5p, s5_vjp, glu_w, glu_b, bsz, nc)
    g.update(g5)
    x2 = _exchange_start([slabs("ab_w_out")[:, 0], _bf(slabs("s5_glu_w")[:, 0])], ["a2a", "a2a"], name="grads_2")
    dxb, dzb, gm = _ml_layer_bwd(dyb, xb, zb, svm, mlw, bsz, nc, dep=x2["token"])
    g.update(gm)
    dps0 = (du, dza, dxb, dzb)
    dxn0 = None
    for i, (dp, wi) in enumerate(zip(dps0, w_in0)):
        dxn0 = _mm(dp, wi, "NT", name=f"d_in0_{i}", add=dxn0)
    dw0 = [_mm(xn0, dp, "TN", name=f"dw_in0_{i}", out_dtype=BF16, tn=S5_WIDTH, slabs=True) for i, dp in enumerate(dps0)]
    dw_in0_slabs = jnp.concatenate(dw0, axis=0)
    dh0, g["ab_norm"] = _rms_bwd(h0, full["ab_norm"], dxn0, dh1, name="rms0_bwd")
    dh0 = dh0.reshape(bsz, tp, D_MODEL)
    grad_x = dh0[:, CHUNK:]
    g["meta_tokens"] = jnp.sum(dh0[:, PAD_ROWS:CHUNK], axis=0)

    small_g = _pack([(slabs(n), 1) for n in SMALL])
    repl_g = _pack([(g[n], 0) for n in REPL])
    x3 = _exchange_start([dw_in0_slabs, small_g, repl_g], ["a2a", "a2a", "ag"], name="grads_3")

    def update_big(n, gp):
        return _adamw(w[n][0], mom[n][0], var[n][0], gp, name=f"adamw_{n}")

    res = {}
    ex1 = _exchange_wait(x1, x3["token"])
    res["ssd_w_in"], res["ssd_w_out"] = update_big("ssd_w_in", ex1[0]), update_big("ssd_w_out", ex1[1])
    ex2 = _exchange_wait(x2, res["ssd_w_out"][0])
    res["ab_w_out"], res["s5_glu_w"] = update_big("ab_w_out", ex2[0]), update_big("s5_glu_w", ex2[1])
    ex3 = _exchange_wait(x3, [res[n][0] for n in ("ssd_w_in", "ssd_w_out", "ab_w_out", "s5_glu_w")])
    res["ab_w_in"] = update_big("ab_w_in", ex3[0])
    for names, gp, tag in ((SMALL, ex3[1], "small"), (REPL, ex3[2], "repl")):
        shapes = [local_shape(n) for n in names]
        packs = [_pack([(d[n], 0) for n in names]) for d in (w, mom, var)]
        outs = _adamw(packs[0], packs[1], packs[2], gp, name=f"adamw_{tag}")
        for k, o in enumerate(outs):
            for n, a in zip(names, _unpack(o, shapes)):
                res.setdefault(n, [None] * 4)[k] = a
    outs = [loss, grad_x]
    for k in range(4):
        outs += [res[n][k].reshape(local_shape(n)) for n in WNAMES]
    return tuple(outs)
```

```python
import functools
import math

import jax
import jax.numpy as jnp
from jax import lax
from jax.experimental import pallas as pl
from jax.experimental.pallas import tpu as pltpu

F32 = jnp.float32
BF16 = jnp.bfloat16

D_MODEL = 2048
SEQ = 2048
N_META = 16
CHUNK = 128
PAD_ROWS = CHUNK - N_META
NORM_EPS = 1e-6
HEAD_NORM_EPS = 1e-5
S5_WIDTH = 1024
S5_GROUPS = 64
S5_GROUP_SIZE = 16
S5_STATE = 64
S5_GB = 8
S5_LANES = S5_GB * S5_STATE
ML_WIDTH = 3072
ML_HEADS = 8
ML_DH = 384
ML_CONV = 4
QKV_BLOCK = 4
SSD_INNER = 4096
SSD_HEADS = 64
SSD_P = 64
SSD_N = 128
SSD_GROUPS = 8
SSD_HPG = 8
SSD_GW = SSD_HPG * SSD_P
N_DEV = 8
ADAM_LR, ADAM_B1, ADAM_B2, ADAM_EPS, ADAM_WD, ADAM_STEP = 0.001, 0.9, 0.999, 1e-08, 0.01, 10
NEG = -1e30
VMEM_CAP = 60 * 1024 * 1024
MESH = pl.DeviceIdType.MESH

NN = (((1,), (0,)), ((), ()))
NT = (((1,), (1,)), ((), ()))
TN = (((0,), (0,)), ((), ()))


def _dot(a, b, dims=NN):
    return lax.dot_general(a, b, dims, preferred_element_type=F32)


def _bf(x):
    return x.astype(BF16)


def _pick(n, cands):
    for c in cands:
        if n % c == 0:
            return c
    return n


def _nbytes(shape, dtype):
    return math.prod(shape) * jnp.dtype(dtype).itemsize


ANY_SPEC = pl.BlockSpec(memory_space=pl.ANY)


def _pc(body, *, name, grid, in_specs, out_specs, out_shape, scratch=(), vmem=None, dep=None):
    limit = None if vmem is None else int(min(VMEM_CAP, max(32 * 1024 * 1024, 2 * vmem + (8 << 20))))
    n_in = len(in_specs)
    if dep is not None:
        inner = body

        def body(*refs):
            inner(*refs[:n_in], *refs[n_in + 1:])

        in_specs = list(in_specs) + [ANY_SPEC]
    call = pl.pallas_call(
        body, name=name, grid=grid, in_specs=in_specs, out_specs=out_specs, out_shape=out_shape,
        scratch_shapes=list(scratch),
        compiler_params=pltpu.CompilerParams(dimension_semantics=("arbitrary",) * len(grid), vmem_limit_bytes=limit))
    return call if dep is None else (lambda *args: call(*args, dep))


def _silu(x):
    return x * jax.nn.sigmoid(x)


def _dsilu(x):
    s = jax.nn.sigmoid(x)
    return s * (1.0 + x * (1.0 - s))


def _gelu_and_grad(x):
    c0 = math.sqrt(2.0 / math.pi)
    inner = c0 * (x + 0.044715 * x * x * x)
    t = jnp.tanh(inner)
    g = 0.5 * x * (1.0 + t)
    dg = 0.5 * (1.0 + t) + 0.5 * x * (1.0 - t * t) * c0 * (1.0 + 3 * 0.044715 * x * x)
    return g, dg


def _mm(a, b, mode, *, name, add=None, out_dtype=F32, tn=None, slabs=False):
    if mode == "NN":
        (m, k), (k2, n) = a.shape, b.shape
    elif mode == "NT":
        (m, k), (n, k2) = a.shape, b.shape
    else:
        (k, m), (k2, n) = a.shape, b.shape
    assert k == k2, (a.shape, b.shape, mode)
    tm = _pick(m, (1088, 1024, 768, 512, 384, 256, 128))
    tn = tn or _pick(n, (512, 384, 256, 128))
    tk = _pick(k, (2048, 1088, 1024, 768, 512, 384, 256, 128))
    nk = k // tk
    dims = {"NN": NN, "NT": NT, "TN": TN}[mode]

    def body(*refs):
        a_ref, b_ref = refs[0], refs[1]
        add_ref = refs[2] if add is not None else None
        o_ref, acc_ref = refs[-2], refs[-1]
        kk = pl.program_id(2)

        @pl.when(kk == 0)
        def _():
            acc_ref[...] = jnp.zeros_like(acc_ref)

        acc_ref[...] += _dot(_bf(a_ref[...]), _bf(b_ref[...]), dims)

        @pl.when(kk == nk - 1)
        def _():
            r = acc_ref[...]
            if add_ref is not None:
                r = r + add_ref[...]
            o_ref[...] = r.reshape(o_ref.shape).astype(o_ref.dtype)

    if mode == "NN":
        a_spec = pl.BlockSpec((tm, tk), lambda i, j, kk: (i, kk))
        b_spec = pl.BlockSpec((tk, tn), lambda i, j, kk: (kk, j))
    elif mode == "NT":
        a_spec = pl.BlockSpec((tm, tk), lambda i, j, kk: (i, kk))
        b_spec = pl.BlockSpec((tn, tk), lambda i, j, kk: (j, kk))
    else:
        a_spec = pl.BlockSpec((tk, tm), lambda i, j, kk: (kk, i))
        b_spec = pl.BlockSpec((tk, tn), lambda i, j, kk: (kk, j))
    in_specs = [a_spec, b_spec]
    args = [a, b]
    if add is not None:
        in_specs.append(pl.BlockSpec((tm, tn), lambda i, j, kk: (i, j)))
        args.append(add)
    if slabs:
        out_shape = jax.ShapeDtypeStruct((n // tn, m, tn), out_dtype)
        out_spec = pl.BlockSpec((1, tm, tn), lambda i, j, kk: (j, i, 0))
    else:
        out_shape = jax.ShapeDtypeStruct((m, n), out_dtype)
        out_spec = pl.BlockSpec((tm, tn), lambda i, j, kk: (i, j))
    vmem = (_nbytes((tm, tk), a.dtype) + _nbytes((tk, tn), b.dtype) + _nbytes((tm, tn), out_dtype)
            + (_nbytes((tm, tn), F32) if add is not None else 0)) + _nbytes((tm, tn), F32) // 2
    return _pc(body, name=name, grid=(m // tm, n // tn, nk), in_specs=in_specs, out_specs=out_spec,
               out_shape=out_shape, scratch=[pltpu.VMEM((tm, tn), F32)], vmem=vmem)(*args)


def _rms_fwd(x, g, *, name):
    r, d = x.shape
    tm = _pick(r, (256, 128))

    def body(x_ref, g_ref, o_ref):
        xv = x_ref[...]
        rstd = lax.rsqrt(jnp.mean(xv * xv, axis=1, keepdims=True) + NORM_EPS)
        o_ref[...] = (xv * rstd * g_ref[...]).astype(o_ref.dtype)

    return _pc(body, name=name, grid=(r // tm,),
               in_specs=[pl.BlockSpec((tm, d), lambda i: (i, 0)), pl.BlockSpec((1, d), lambda i: (0, 0))],
               out_specs=pl.BlockSpec((tm, d), lambda i: (i, 0)), out_shape=jax.ShapeDtypeStruct((r, d), BF16),
               vmem=tm * d * 6)(x, g.reshape(1, d))


def _rms_bwd(x, g, dxn, dres, *, name, dep=None):
    r, d = x.shape
    tm = _pick(r, (256, 128))

    def body(x_ref, g_ref, dxn_ref, dres_ref, dx_ref, dg_ref):
        @pl.when(pl.program_id(0) == 0)
        def _():
            dg_ref[...] = jnp.zeros_like(dg_ref)

        xv = x_ref[...]
        rstd = lax.rsqrt(jnp.mean(xv * xv, axis=1, keepdims=True) + NORM_EPS)
        xh = xv * rstd
        dy = dxn_ref[...]
        dg_ref[...] += jnp.sum(dy * xh, axis=0, keepdims=True)
        dyg = dy * g_ref[...]
        dx_ref[...] = dres_ref[...] + rstd * (dyg - xh * jnp.mean(dyg * xh, axis=1, keepdims=True))

    row = pl.BlockSpec((tm, d), lambda i: (i, 0))
    vec = pl.BlockSpec((1, d), lambda i: (0, 0))
    return _pc(body, name=name, grid=(r // tm,), in_specs=[row, vec, row, row], out_specs=[row, vec],
               out_shape=[jax.ShapeDtypeStruct((r, d), F32), jax.ShapeDtypeStruct((1, d), F32)],
               vmem=tm * d * 16, dep=dep)(x, g.reshape(1, d), dxn, dres)


def _final_loss(h, g, target, bsz, nc):
    d = h.shape[1]

    def body(h_ref, g_ref, t_ref, loss_ref, dh_ref, dg_ref):
        b, c = pl.program_id(0), pl.program_id(1)

        @pl.when((b == 0) & (c == 0))
        def _():
            loss_ref[...] = jnp.zeros_like(loss_ref)
            dg_ref[...] = jnp.zeros_like(dg_ref)

        @pl.when(c == 0)
        def _():
            dh_ref[...] = jnp.zeros_like(dh_ref)

        @pl.when(c > 0)
        def _():
            xv = h_ref[...]
            rstd = lax.rsqrt(jnp.mean(xv * xv, axis=1, keepdims=True) + NORM_EPS)
            xh = xv * rstd
            gv = g_ref[...]
            err = xh * gv - t_ref[0]
            loss_ref[...] += 0.5 * jnp.sum(jnp.mean(err * err, axis=1, keepdims=True))
            dy = err * (1.0 / d)
            dg_ref[...] += jnp.sum(dy * xh, axis=0, keepdims=True)
            dyg = dy * gv
            dh_ref[...] = rstd * (dyg - xh * jnp.mean(dyg * xh, axis=1, keepdims=True))

    row = pl.BlockSpec((CHUNK, d), lambda b, c: (b * nc + c, 0))
    vec = pl.BlockSpec((1, d), lambda b, c: (0, 0))
    return _pc(body, name="final_loss", grid=(bsz, nc),
               in_specs=[row, vec, pl.BlockSpec((1, CHUNK, d), lambda b, c: (b, jnp.maximum(c - 1, 0), 0))],
               out_specs=[pl.BlockSpec((8, 128), lambda b, c: (0, 0)), row, vec],
               out_shape=[jax.ShapeDtypeStruct((8, 128), F32), jax.ShapeDtypeStruct(h.shape, F32),
                          jax.ShapeDtypeStruct((1, d), F32)],
               vmem=CHUNK * d * 16)(h, g.reshape(1, d), target)


def _adamw(w, m, v, gparts, *, name):
    r, c = w.shape
    tr = _pick(r, (256, 128)) if r * c * 4 > (1 << 20) else r

    def body(w_ref, m_ref, v_ref, gp_ref, g_ref, d_ref, nm_ref, nv_ref):
        g = gp_ref[0].astype(F32)
        for j in range(1, N_DEV):
            g = g + gp_ref[j].astype(F32)
        mm = ADAM_B1 * m_ref[...] + (1.0 - ADAM_B1) * g
        vv = ADAM_B2 * v_ref[...] + (1.0 - ADAM_B2) * (g * g)
        m_hat = mm / (1.0 - ADAM_B1 ** ADAM_STEP)
        v_hat = vv / (1.0 - ADAM_B2 ** ADAM_STEP)
        g_ref[...] = g
        d_ref[...] = -ADAM_LR * (m_hat / (jnp.sqrt(v_hat) + ADAM_EPS) + ADAM_WD * w_ref[...])
        nm_ref[...] = mm
        nv_ref[...] = vv

    blk = pl.BlockSpec((tr, c), lambda i: (i, 0))
    out = jax.ShapeDtypeStruct((r, c), F32)
    return _pc(body, name=name, grid=(r // tr,),
               in_specs=[blk, blk, blk, pl.BlockSpec((N_DEV, tr, c), lambda i: (0, i, 0))],
               out_specs=[blk, blk, blk, blk], out_shape=[out, out, out, out],
               vmem=tr * c * (4 * 7 + N_DEV * jnp.dtype(gparts.dtype).itemsize))(w, m, v, gparts)


PEERS = (1, 2, 4, 6, 3, 5, 7)
HBM_SPEC = pl.BlockSpec(memory_space=pltpu.HBM)
SEM_SPEC = pl.BlockSpec(memory_space=pltpu.SEMAPHORE)
SIDE_EFFECT = pltpu.SideEffectType.DATAFLOW_SIDE_EFFECTING


def _peer(p):
    x, y, c = lax.axis_index("x"), lax.axis_index("y"), lax.axis_index("c")
    tx, ty, tc = x ^ ((p >> 2) & 1), y ^ ((p >> 1) & 1), c ^ (p & 1)
    return (tx, ty, tc), 4 * tx + 2 * ty + tc


def _place_own(a, kind, *, name):
    rows, cols = a.shape[-2:]
    tr = _pick(rows, (512, 256, 128, 64, 32, 16))
    me = (4 * lax.axis_index("x") + 2 * lax.axis_index("y") + lax.axis_index("c")).astype(jnp.int32).reshape(1)

    def body(me_ref, in_ref, out_ref):
        out_ref[...] = in_ref[...].reshape(out_ref.shape)

    if kind == "a2a":
        in_spec = pl.BlockSpec((1, tr, cols), lambda i, me_ref: (me_ref[0], i, 0))
    else:
        in_spec = pl.BlockSpec((tr, cols), lambda i, me_ref: (i, 0))
    return pl.pallas_call(
        body, name=name, out_shape=jax.ShapeDtypeStruct((N_DEV, rows, cols), a.dtype),
        grid_spec=pltpu.PrefetchScalarGridSpec(
            num_scalar_prefetch=1, grid=(rows // tr,), in_specs=[in_spec],
            out_specs=pl.BlockSpec((1, tr, cols), lambda i, me_ref: (me_ref[0], i, 0))))(me, a)


def _exchange_copies(ins, lands, send_sems, recv_sems, kinds, incoming):
    me = 4 * lax.axis_index("x") + 2 * lax.axis_index("y") + lax.axis_index("c")
    copies = []
    for i, kind in enumerate(kinds):
        for p in PEERS:
            dev, tgt = _peer(p)
            k = i * (N_DEV - 1) + p - 1
            copies.append(pltpu.make_async_remote_copy(
                src_ref=ins[i].at[tgt] if kind == "a2a" else ins[i], dst_ref=lands[i].at[tgt if incoming else me],
                send_sem=send_sems.at[k], recv_sem=recv_sems.at[k], device_id=dev, device_id_type=MESH))
    return copies


def _exchange_start(arrays, kinds, *, name, dep=None):
    n = len(arrays)
    lands = [_place_own(a, k, name=f"{name}_own{i}") for i, (a, k) in enumerate(zip(arrays, kinds))]
    extra = [] if dep is None else [dep]

    def body(*refs):
        ins, lnd = refs[:n], refs[n:2 * n]
        send_sems, recv_sems = refs[2 * n + len(extra)], refs[2 * n + len(extra) + 1]
        token = refs[-1]
        for cp in _exchange_copies(ins, lnd, send_sems, recv_sems, kinds, False):
            cp.start()
        token[...] = jnp.zeros_like(token)

    sem = pltpu.SemaphoreType.DMA((n * (N_DEV - 1),))
    outs = pl.pallas_call(
        body, name=name, in_specs=[HBM_SPEC] * (2 * n) + [ANY_SPEC] * len(extra),
        out_specs=[SEM_SPEC, SEM_SPEC] + [HBM_SPEC] * (2 * n) + [pl.BlockSpec(memory_space=pltpu.VMEM)],
        out_shape=[sem, sem] + [pltpu.HBM(a.shape, a.dtype) for a in arrays + lands]
        + [jax.ShapeDtypeStruct((8, 128), F32)],
        input_output_aliases={i: 2 + i for i in range(2 * n)},
        compiler_params=pltpu.CompilerParams(has_side_effects=SIDE_EFFECT),
    )(*[pltpu.with_memory_space_constraint(a, pltpu.HBM) for a in arrays + lands], *extra)
    return dict(send=outs[0], recv=outs[1], ins=list(outs[2:2 + n]), lands=list(outs[2 + n:2 + 2 * n]),
                token=outs[-1], kinds=kinds, name=name)


def _exchange_wait(h, after):
    n = len(h["ins"])
    kinds = h["kinds"]

    def body(*refs):
        ins, lnd = refs[:n], refs[n:2 * n]
        send_sems, recv_sems = refs[2 * n], refs[2 * n + 1]
        copies = _exchange_copies(ins, lnd, send_sems, recv_sems, kinds, True)
        for cp in copies:
            cp.wait_recv()
        for cp in copies:
            cp.wait_send()

    arrs = h["ins"] + h["lands"]
    after = list(after) if isinstance(after, (list, tuple)) else [after]
    outs = pl.pallas_call(
        body, name=h["name"] + "_wait", in_specs=[HBM_SPEC] * (2 * n) + [SEM_SPEC, SEM_SPEC] + [ANY_SPEC] * len(after),
        out_specs=[HBM_SPEC] * (2 * n), out_shape=[pltpu.HBM(a.shape, a.dtype) for a in arrs],
        input_output_aliases={i: i for i in range(2 * n)},
        compiler_params=pltpu.CompilerParams(has_side_effects=SIDE_EFFECT),
    )(*arrs, h["send"], h["recv"], *after)
    return list(outs[n:])


def _s5_params(lam_re, lam_im, log_dt, b_re, b_im):
    dt = jnp.exp(log_dt)[:, None]
    mag = jnp.exp(lam_re * dt)
    ar, ai = mag * jnp.cos(lam_im * dt), mag * jnp.sin(lam_im * dt)
    den = lam_re * lam_re + lam_im * lam_im
    qr = ((ar - 1.0) * lam_re + ai * lam_im) / den
    qi = (ai * lam_re - (ar - 1.0) * lam_im) / den
    bbr = qr[..., None] * b_re - qi[..., None] * b_im
    bbi = qr[..., None] * b_im + qi[..., None] * b_re
    return ar, ai, bbr, bbi


def _s5_power_table(ar, ai):
    pr, pi = ar.reshape(1, -1), ai.reshape(1, -1)
    while pr.shape[0] < CHUNK:
        sr, si = pr[-1:], pi[-1:]
        pr, pi = (jnp.concatenate([pr, pr * sr - pi * si], axis=0), jnp.concatenate([pi, pr * si + pi * sr], axis=0))
    return pr, pi


def _blockdiag(w, rows, cols):
    w = w.reshape(S5_GB, S5_GB, rows, cols)
    eye = jnp.eye(S5_GB, dtype=w.dtype)
    return jnp.einsum("abrc,bd->abrdc", w, eye).reshape(S5_GB, S5_GB * rows, S5_GB * cols)


def _blockdiag_extract(w, rows, cols):
    w = w.reshape(S5_GB, S5_GB, rows, S5_GB, cols)
    return jnp.einsum("abrbc->abrc", w).reshape(S5_GROUPS, rows, cols)


def _s5_scan_specs(bsz, nc, rev):
    def chunk(b, c):
        return b * nc + ((nc - 1 - c) if rev else c)

    return dict(
        u=pl.BlockSpec((CHUNK, CHUNK), lambda g, b, c: (chunk(b, c), g)),
        x=pl.BlockSpec((CHUNK, S5_LANES), lambda g, b, c: (chunk(b, c), g)),
        wb=pl.BlockSpec((1, CHUNK, S5_LANES), lambda g, b, c: (g, 0, 0)),
        wc=pl.BlockSpec((1, S5_LANES, CHUNK), lambda g, b, c: (g, 0, 0)),
        tab=pl.BlockSpec((CHUNK, S5_LANES), lambda g, b, c: (0, g)),
        step=pl.BlockSpec((8, S5_LANES), lambda g, b, c: (0, g)),
        d=pl.BlockSpec((1, CHUNK), lambda g, b, c: (0, g)),
        lane=pl.BlockSpec((1, S5_LANES), lambda g, b, c: (0, g)),
        xprev=pl.BlockSpec((8, S5_LANES), lambda g, b, c: (jnp.maximum(chunk(b, c) * (CHUNK // 8) - 1, 0), g)),
    )


def _s5_fwd(u, wbr, wbi, pr, pi, sr, si, wcr, wci, d, bsz, nc):
    r = u.shape[0]
    sp = _s5_scan_specs(bsz, nc, False)

    def body(u_ref, wbr_ref, wbi_ref, pr_ref, pi_ref, sr_ref, si_ref, wcr_ref, wci_ref, d_ref,
             xr_ref, xi_ref, y1_ref, g_ref, cr_s, ci_s):
        @pl.when(pl.program_id(2) == 0)
        def _():
            cr_s[...] = jnp.zeros_like(cr_s)
            ci_s[...] = jnp.zeros_like(ci_s)

        uv = u_ref[...]
        ub = _bf(uv)
        xr, xi = _dot(ub, wbr_ref[0]), _dot(ub, wbi_ref[0])
        sub = lax.broadcasted_iota(jnp.int32, (CHUNK, S5_LANES), 0) % 8
        for k in range(3):
            s = 1 << k
            ar, ai = sr_ref[k:k + 1, :], si_ref[k:k + 1, :]
            hr = jnp.where(sub >= s, pltpu.roll(xr, s, 0), 0.0)
            hi = jnp.where(sub >= s, pltpu.roll(xi, s, 0), 0.0)
            xr, xi = xr + (ar * hr - ai * hi), xi + (ar * hi + ai * hr)
        cr, ci = cr_s[...], ci_s[...]
        tr, ti = pr_ref[0:8, :], pi_ref[0:8, :]
        outr, outi = [], []
        for g8 in range(CHUNK // 8):
            br, bi = xr[8 * g8:8 * g8 + 8, :], xi[8 * g8:8 * g8 + 8, :]
            br, bi = br + (tr * cr - ti * ci), bi + (tr * ci + ti * cr)
            cr, ci = br[7:8, :], bi[7:8, :]
            outr.append(br)
            outi.append(bi)
        xr, xi = jnp.concatenate(outr, axis=0), jnp.concatenate(outi, axis=0)
        cr_s[...] = cr
        ci_s[...] = ci
        xr_ref[...] = xr
        xi_ref[...] = xi
        y = _dot(_bf(xr), wcr_ref[0]) - _dot(_bf(xi), wci_ref[0]) + d_ref[...] * uv
        y1_ref[...] = y
        g_ref[...] = _bf(_gelu_and_grad(y)[0])

    return _pc(body, name="s5_fwd", grid=(S5_GB, bsz, nc),
               in_specs=[sp["u"], sp["wb"], sp["wb"], sp["tab"], sp["tab"], sp["step"], sp["step"], sp["wc"], sp["wc"],
                         sp["d"]],
               out_specs=[sp["x"], sp["x"], sp["u"], sp["u"]],
               out_shape=[jax.ShapeDtypeStruct((r, S5_GROUPS * S5_STATE), F32)] * 2
               + [jax.ShapeDtypeStruct((r, S5_WIDTH), F32), jax.ShapeDtypeStruct((r, S5_WIDTH), BF16)],
               scratch=[pltpu.VMEM((1, S5_LANES), F32)] * 2, vmem=4 << 20,
               )(u, wbr, wbi, pr, pi, sr, si, wcr, wci, d)


def _s5_post(y1, glu_pre, glu_b, z):
    r, w = y1.shape
    tm = _pick(r, (256, 128))

    def body(y_ref, p_ref, b_ref, z_ref, o_ref):
        g = _gelu_and_grad(y_ref[...])[0]
        o_ref[...] = _bf(g * jax.nn.sigmoid(p_ref[...] + b_ref[...]) * _silu(z_ref[...]))

    row = pl.BlockSpec((tm, w), lambda i: (i, 0))
    return _pc(body, name="s5_post", grid=(r // tm,), in_specs=[row, row, pl.BlockSpec((1, w), lambda i: (0, 0)), row],
               out_specs=row, out_shape=jax.ShapeDtypeStruct((r, w), BF16), vmem=tm * w * 16)(y1, glu_pre, glu_b, z)


def _s5_post_bwd(dya, y1, glu_pre, glu_b, z):
    r, w = y1.shape
    tm = _pick(r, (256, 128))

    def body(dy_ref, y_ref, p_ref, b_ref, z_ref, dz_ref, dp_ref, dg_ref, db_ref):
        @pl.when(pl.program_id(0) == 0)
        def _():
            db_ref[...] = jnp.zeros_like(db_ref)

        g = _gelu_and_grad(y_ref[...])[0]
        s = jax.nn.sigmoid(p_ref[...] + b_ref[...])
        zv = z_ref[...]
        dy = dy_ref[...]
        do = dy * _silu(zv)
        dz_ref[...] = _bf(dy * g * s * _dsilu(zv))
        dp = do * g * s * (1.0 - s)
        dp_ref[...] = _bf(dp)
        db_ref[...] += jnp.sum(dp, axis=0, keepdims=True)
        dg_ref[...] = do * s

    row = pl.BlockSpec((tm, w), lambda i: (i, 0))
    vec = pl.BlockSpec((1, w), lambda i: (0, 0))
    return _pc(body, name="s5_post_bwd", grid=(r // tm,), in_specs=[row, row, row, vec, row],
               out_specs=[row, row, row, vec],
               out_shape=[jax.ShapeDtypeStruct((r, w), BF16), jax.ShapeDtypeStruct((r, w), BF16),
                          jax.ShapeDtypeStruct((r, w), F32), jax.ShapeDtypeStruct((1, w), F32)],
               vmem=tm * w * 24)(dya, y1, glu_pre, glu_b, z)


def _s5_bwd(dg, y1, u, xr, xi, wbr, wbi, qr, qi, sr, si, wcr, wci, d, bsz, nc):
    r = u.shape[0]
    sp = _s5_scan_specs(bsz, nc, True)

    def body(dg_ref, y1_ref, u_ref, xr_ref, xi_ref, xpr_ref, xpi_ref, wbr_ref, wbi_ref, qr_ref, qi_ref, sr_ref, si_ref,
             wcr_ref, wci_ref, d_ref, du_ref, dd_ref, dwcr_ref, dwci_ref, dwbr_ref, dwbi_ref, dar_ref, dai_ref,
             cr_s, ci_s):
        b, c = pl.program_id(1), pl.program_id(2)

        @pl.when((b == 0) & (c == 0))
        def _():
            for ref in (dd_ref, dwcr_ref, dwci_ref, dwbr_ref, dwbi_ref, dar_ref, dai_ref):
                ref[...] = jnp.zeros_like(ref)

        @pl.when(c == 0)
        def _():
            cr_s[...] = jnp.zeros_like(cr_s)
            ci_s[...] = jnp.zeros_like(ci_s)

        uv = u_ref[...]
        ub = _bf(uv)
        dy = dg_ref[...] * _gelu_and_grad(y1_ref[...])[1]
        dd_ref[...] += jnp.sum(dy * uv, axis=0, keepdims=True)
        dyb = _bf(dy)
        xr, xi = xr_ref[...], xi_ref[...]
        dwcr_ref[0] += _dot(_bf(xr), dyb, TN)
        dwci_ref[0] -= _dot(_bf(xi), dyb, TN)
        lr, li = _dot(dyb, wcr_ref[0], NT), -_dot(dyb, wci_ref[0], NT)
        row = lax.broadcasted_iota(jnp.int32, (CHUNK, S5_LANES), 0)
        sub = row % 8
        for k in range(3):
            s = 1 << k
            ar, ai = sr_ref[k:k + 1, :], si_ref[k:k + 1, :]
            hr = jnp.where(sub < 8 - s, pltpu.roll(lr, CHUNK - s, 0), 0.0)
            hi = jnp.where(sub < 8 - s, pltpu.roll(li, CHUNK - s, 0), 0.0)
            lr, li = lr + (ar * hr + ai * hi), li + (ar * hi - ai * hr)
        cr, ci = cr_s[...], ci_s[...]
        tr, ti = qr_ref[CHUNK - 8:CHUNK, :], qi_ref[CHUNK - 8:CHUNK, :]
        outr, outi = [], []
        for g8 in reversed(range(CHUNK // 8)):
            br, bi = lr[8 * g8:8 * g8 + 8, :], li[8 * g8:8 * g8 + 8, :]
            br, bi = br + (tr * cr + ti * ci), bi + (tr * ci - ti * cr)
            cr, ci = br[0:1, :], bi[0:1, :]
            outr.append(br)
            outi.append(bi)
        lr, li = jnp.concatenate(outr[::-1], axis=0), jnp.concatenate(outi[::-1], axis=0)
        cr_s[...] = cr
        ci_s[...] = ci
        lrb, lib = _bf(lr), _bf(li)
        du_ref[...] = _bf(_dot(lrb, wbr_ref[0], NT) + _dot(lib, wbi_ref[0], NT) + dy * d_ref[...])
        dwbr_ref[0] += _dot(ub, lrb, TN)
        dwbi_ref[0] += _dot(ub, lib, TN)
        first = c == nc - 1
        pr0 = jnp.where(first, 0.0, xpr_ref[7:8, :])
        pi0 = jnp.where(first, 0.0, xpi_ref[7:8, :])
        xpr = jnp.where(row == 0, pr0, pltpu.roll(xr, 1, 0))
        xpi = jnp.where(row == 0, pi0, pltpu.roll(xi, 1, 0))
        dar_ref[...] += jnp.sum(lr * xpr + li * xpi, axis=0, keepdims=True)
        dai_ref[...] += jnp.sum(li * xpr - lr * xpi, axis=0, keepdims=True)

    st = jax.ShapeDtypeStruct
    return _pc(body, name="s5_bwd", grid=(S5_GB, bsz, nc),
               in_specs=[sp["u"], sp["u"], sp["u"], sp["x"], sp["x"], sp["xprev"], sp["xprev"], sp["wb"], sp["wb"],
                         sp["tab"], sp["tab"], sp["step"], sp["step"], sp["wc"], sp["wc"], sp["d"]],
               out_specs=[sp["u"], sp["d"], sp["wc"], sp["wc"], sp["wb"], sp["wb"], sp["lane"], sp["lane"]],
               out_shape=[st((r, S5_WIDTH), BF16), st((1, S5_WIDTH), F32),
                          st((S5_GB, S5_LANES, CHUNK), F32), st((S5_GB, S5_LANES, CHUNK), F32),
                          st((S5_GB, CHUNK, S5_LANES), F32), st((S5_GB, CHUNK, S5_LANES), F32),
                          st((1, S5_GROUPS * S5_STATE), F32), st((1, S5_GROUPS * S5_STATE), F32)],
               scratch=[pltpu.VMEM((1, S5_LANES), F32)] * 2, vmem=6 << 20,
               )(dg, y1, u, xr, xi, xr, xi, wbr, wbi, qr, qi, sr, si, wcr, wci, d)


def _s5_layer_fwd(u, prm, glu_w, bsz, nc):
    xr, xi, y1, g = _s5_fwd(u, prm["wbr"], prm["wbi"], prm["pr"], prm["pi"], prm["sr"], prm["si"], prm["wcr"],
                            prm["wci"], prm["d"], bsz, nc)
    glu_pre = _mm(g, glu_w(y1) if callable(glu_w) else glu_w, "NN", name="s5_glu")
    return dict(xr=xr, xi=xi, y1=y1, g=g, glu_pre=glu_pre)


def _s5_layer_bwd(dya, u, z, sv, prm, pvjp, glu_w, glu_b, bsz, nc):
    dz, dglu, dg_direct, dglu_b = _s5_post_bwd(dya, sv["y1"], sv["glu_pre"], glu_b, z)
    dg = _mm(dglu, glu_w, "NT", name="s5_dg", add=dg_direct)
    dglu_w = _mm(sv["g"], dglu, "TN", name="s5_dglu_w")
    du, dd, dwcr, dwci, dwbr, dwbi, dar, dai = _s5_bwd(
        dg, sv["y1"], u, sv["xr"], sv["xi"], prm["wbr"], prm["wbi"], prm["qr"], prm["qi"], prm["sr"], prm["si"],
        prm["wcr"], prm["wci"], prm["d"], bsz, nc)
    dbbr = jnp.swapaxes(_blockdiag_extract(dwbr, S5_GROUP_SIZE, S5_STATE), 1, 2)
    dbbi = jnp.swapaxes(_blockdiag_extract(dwbi, S5_GROUP_SIZE, S5_STATE), 1, 2)
    dlr, dli, dldt, dbr, dbi = pvjp((dar.reshape(S5_GROUPS, S5_STATE), dai.reshape(S5_GROUPS, S5_STATE), dbbr, dbbi))
    grads = dict(
        s5_lambda_re=dlr, s5_lambda_im=dli, s5_log_dt=dldt, s5_b_re=dbr, s5_b_im=dbi,
        s5_c_re=jnp.swapaxes(_blockdiag_extract(dwcr, S5_STATE, S5_GROUP_SIZE), 1, 2),
        s5_c_im=jnp.swapaxes(_blockdiag_extract(dwci, S5_STATE, S5_GROUP_SIZE), 1, 2),
        s5_d=dd, s5_glu_w=dglu_w, s5_glu_b=dglu_b)
    return du, dz, grads


def _s5_tables(lam_re, lam_im, log_dt, b_re, b_im, c_re, c_im, d):
    (ar, ai, bbr, bbi), vjp = jax.vjp(_s5_params, lam_re, lam_im, log_dt, b_re, b_im)
    pr, pi = _s5_power_table(lax.stop_gradient(ar), lax.stop_gradient(ai))
    steps = [(1 << k) - 1 for k in range(8)]
    prm = dict(
        wbr=_bf(_blockdiag(jnp.swapaxes(bbr, 1, 2), S5_GROUP_SIZE, S5_STATE)),
        wbi=_bf(_blockdiag(jnp.swapaxes(bbi, 1, 2), S5_GROUP_SIZE, S5_STATE)),
        wcr=_bf(_blockdiag(jnp.swapaxes(c_re, 1, 2), S5_STATE, S5_GROUP_SIZE)),
        wci=_bf(_blockdiag(jnp.swapaxes(c_im, 1, 2), S5_STATE, S5_GROUP_SIZE)),
        pr=pr, pi=pi, qr=pr[::-1], qi=pi[::-1],
        sr=jnp.concatenate([pr[i:i + 1] for i in steps], axis=0),
        si=jnp.concatenate([pi[i:i + 1] for i in steps], axis=0), d=d.reshape(1, S5_WIDTH))
    return prm, vjp


def _tile16(p8):
    return jnp.concatenate([p8] * (CHUNK // 8), axis=0)


def _shift_down(x, halo, s, row):
    return jnp.where(row >= s, pltpu.roll(x, s, 0), pltpu.roll(halo, s, 0))


def _shift_up(x, halo, s, row):
    return jnp.where(row < CHUNK - s, pltpu.roll(x, CHUNK - s, 0), pltpu.roll(halo, CHUNK - s, 0))


def _conv_specs(nc, tw):
    def chunk(b, c):
        return b * nc + c

    return dict(
        x=pl.BlockSpec((CHUNK, tw), lambda j, b, c: (chunk(b, c), j)),
        prev=pl.BlockSpec((8, tw), lambda j, b, c: (jnp.maximum(chunk(b, c) * (CHUNK // 8) - 1, 0), j)),
        nxt=pl.BlockSpec((8, tw), lambda j, b, c: ((b * nc + jnp.minimum(c + 1, nc - 1)) * (CHUNK // 8), j)),
        w=pl.BlockSpec((ML_CONV, tw), lambda j, b, c: (0, j)),
        vec=pl.BlockSpec((1, tw), lambda j, b, c: (0, j)),
    )


def _conv_fwd(x, w, bias, bsz, nc, *, name):
    r, wd = x.shape
    tw = _pick(wd, (2048, 1536, 1024, 512, 384, 256, 128))
    sp = _conv_specs(nc, tw)

    def body(x_ref, p_ref, w_ref, b_ref, o_ref):
        c = pl.program_id(2)
        xv = x_ref[...]
        row = lax.broadcasted_iota(jnp.int32, xv.shape, 0)
        halo = jnp.where(c == 0, 0.0, _tile16(p_ref[...]))
        acc = b_ref[...] + w_ref[3:4, :] * xv
        for s in (1, 2, 3):
            acc = acc + w_ref[3 - s:4 - s, :] * _shift_down(xv, halo, s, row)
        o_ref[...] = acc

    return _pc(body, name=name, grid=(wd // tw, bsz, nc), in_specs=[sp["x"], sp["prev"], sp["w"], sp["vec"]],
               out_specs=sp["x"], out_shape=jax.ShapeDtypeStruct((r, wd), F32), vmem=CHUNK * tw * 16,
               )(x, x, w, bias.reshape(1, wd))


def _conv_bwd(dpre, x, w, bsz, nc, *, name, add=None):
    r, wd = x.shape
    tw = _pick(wd, (2048, 1536, 1024, 512, 384, 256, 128))
    sp = _conv_specs(nc, tw)

    def body(*refs):
        d_ref, n_ref, x_ref, p_ref, w_ref = refs[:5]
        add_ref = refs[5] if add is not None else None
        dx_ref, dw_ref, db_ref = refs[-3:]
        b, c = pl.program_id(1), pl.program_id(2)

        @pl.when((b == 0) & (c == 0))
        def _():
            dw_ref[...] = jnp.zeros_like(dw_ref)
            db_ref[...] = jnp.zeros_like(db_ref)

        dv, xv = d_ref[...], x_ref[...]
        row = lax.broadcasted_iota(jnp.int32, xv.shape, 0)
        dhalo = jnp.where(c == nc - 1, 0.0, _tile16(n_ref[...]))
        xhalo = jnp.where(c == 0, 0.0, _tile16(p_ref[...]))
        dx = w_ref[3:4, :] * dv
        for s in (1, 2, 3):
            dx = dx + w_ref[3 - s:4 - s, :] * _shift_up(dv, dhalo, s, row)
        if add_ref is not None:
            dx = dx + add_ref[...]
        dx_ref[...] = _bf(dx)
        db_ref[...] += jnp.sum(dv, axis=0, keepdims=True)
        dw_ref[3:4, :] += jnp.sum(dv * xv, axis=0, keepdims=True)
        for s in (1, 2, 3):
            dw_ref[3 - s:4 - s, :] += jnp.sum(dv * _shift_down(xv, xhalo, s, row), axis=0, keepdims=True)

    ins = [dpre, dpre, x, x, w] + ([add] if add is not None else [])
    specs = [sp["x"], sp["nxt"], sp["x"], sp["prev"], sp["w"]] + ([sp["x"]] if add is not None else [])
    return _pc(body, name=name, grid=(wd // tw, bsz, nc), in_specs=specs, out_specs=[sp["x"], sp["w"], sp["vec"]],
               out_shape=[jax.ShapeDtypeStruct((r, wd), BF16), jax.ShapeDtypeStruct((ML_CONV, wd), F32),
                          jax.ShapeDtypeStruct((1, wd), F32)], vmem=CHUNK * tw * 24)(*ins)


ML_SCALE = ML_DH ** -0.5


def _headwise_expand(w):
    tiled = jnp.tile(w.reshape(ML_HEADS, ML_DH, QKV_BLOCK), (1, 1, ML_DH // QKV_BLOCK))
    blk = jnp.arange(ML_DH) // QKV_BLOCK
    return jnp.where(blk[:, None] == blk[None, :], tiled, 0.0)


def _headwise_extract(w):
    return w[:, :, :QKV_BLOCK].reshape(ML_HEADS * ML_DH // QKV_BLOCK, QKV_BLOCK, QKV_BLOCK)


def _ml_pre(pre, x, wq, wk, wv, wgq, wgk, wgv, bsz, nc):
    r = x.shape[0]
    tr = _pick(r, (256, 128))
    hrow = pl.BlockSpec((tr, ML_DH), lambda h, i: (i, h))
    wexp = pl.BlockSpec((1, ML_DH, ML_DH), lambda h, i: (h, 0, 0))
    wg = pl.BlockSpec((ML_DH, CHUNK), lambda h, i: (h, 0))

    def body(pre_ref, x_ref, wq_ref, wk_ref, wv_ref, gq_ref, gk_ref, gv_ref, q_ref, qs_ref, k_ref, v_ref, gt_ref):
        xcb = _bf(_silu(pre_ref[...]))
        q = _dot(xcb, wq_ref[0])
        k = _dot(xcb, wk_ref[0])
        v = _dot(_bf(x_ref[...]), wv_ref[0])
        qb, kb, vb = _bf(q), _bf(k), _bf(v)
        q_ref[...] = qb
        qs_ref[...] = _bf(q * ML_SCALE)
        k_ref[...] = kb
        v_ref[...] = vb
        gt_ref[0] = _dot(qb, gq_ref[...]) + _dot(kb, gk_ref[...]) + _dot(vb, gv_ref[...])

    o = jax.ShapeDtypeStruct((r, ML_WIDTH), BF16)
    q, qs, k, v, gates8 = _pc(
        body, name="ml_pre", grid=(ML_HEADS, r // tr),
        in_specs=[hrow, hrow, wexp, wexp, wexp, wg, wg, wg],
        out_specs=[hrow, hrow, hrow, hrow, pl.BlockSpec((1, tr, CHUNK), lambda h, i: (h, i, 0))],
        out_shape=[o, o, o, o, jax.ShapeDtypeStruct((ML_HEADS, r, CHUNK), F32)], vmem=6 << 20,
    )(pre, x, wq, wk, wv, wgq, wgk, wgv)

    def sum_body(g_ref, o_ref):
        acc = g_ref[0]
        for j in range(1, ML_HEADS):
            acc = acc + g_ref[j]
        o_ref[...] = acc

    gates = _pc(sum_body, name="ml_gates_sum", grid=(r // tr,),
                in_specs=[pl.BlockSpec((ML_HEADS, tr, CHUNK), lambda i: (0, i, 0))],
                out_specs=pl.BlockSpec((tr, CHUNK), lambda i: (i, 0)),
                out_shape=jax.ShapeDtypeStruct((r, CHUNK), F32), vmem=2 << 20)(gates8)
    return q, qs, k, v, gates


def _tri(rev):
    r = lax.broadcasted_iota(jnp.int32, (CHUNK, CHUNK), 0)
    c = lax.broadcasted_iota(jnp.int32, (CHUNK, CHUNK), 1)
    return jnp.where((c >= r) if rev else (c <= r), 1.0, 0.0).astype(F32)


def _cumsum_rows(x, row, rev=False):
    for k in range(7):
        s = 1 << k
        if rev:
            x = x + jnp.where(row < CHUNK - s, pltpu.roll(x, CHUNK - s, 0), 0.0)
        else:
            x = x + jnp.where(row >= s, pltpu.roll(x, s, 0), 0.0)
    return x


def _log_sigmoid(x):
    return jnp.minimum(x, 0.0) - jnp.log(1.0 + jnp.exp(-jnp.abs(x)))


def _ml_core(gates, hd, first, m, qs, k, v, cmat, nvec):
    sq = (CHUNK, CHUNK)
    lane = lax.broadcasted_iota(jnp.int32, sq, 1)
    row = lax.broadcasted_iota(jnp.int32, sq, 0)
    igc = jnp.sum(jnp.where(lane == hd, gates, 0.0), axis=1, keepdims=True)
    fpc = jnp.sum(jnp.where(lane == hd + ML_HEADS, gates, 0.0), axis=1, keepdims=True)
    valid = jnp.logical_or(jnp.logical_not(first), row[:, :1] >= PAD_ROWS)
    igc = jnp.where(valid, igc, NEG)
    lfc = jnp.where(valid, _log_sigmoid(fpc), 0.0)
    bcb = _cumsum_rows(jnp.broadcast_to(lfc, sq), row)
    igb = jnp.broadcast_to(igc, sq)
    dm = jnp.where(lane <= row, bcb - (bcb - igb).T, NEG)
    bc = bcb[:, :1]
    inter = bc + m
    mt = jnp.maximum(inter, jnp.max(dm, axis=1, keepdims=True))
    wt = jnp.exp(dm - mt)
    wprev = jnp.exp(inter - mt)
    s0 = _dot(qs, k, NT)
    s = s0 * wt
    cb = _bf(cmat)
    qc = _dot(qs, cb)
    qf = qs.astype(F32)
    qn = jnp.sum(qf * nvec, axis=1, keepdims=True)
    num = _dot(_bf(s), v) + wprev * qc
    den = jnp.sum(s, axis=1, keepdims=True) + wprev * qn
    emt = jnp.exp(-mt)
    dd = jnp.maximum(jnp.abs(den), emt)
    blast = bcb[CHUNK - 1:CHUNK, :1]
    g = blast - bc + igc
    m_new = jnp.maximum(blast + m, jnp.max(g, axis=0, keepdims=True))
    decay = jnp.exp(blast + m - m_new)
    e = jnp.exp(g - m_new)
    kf = k.astype(F32)
    wk = e * kf
    return dict(lane=lane, row=row, fpc=fpc, valid=valid, wt=wt, wprev=wprev, s=s, cb=cb, qc=qc, qf=qf, qn=qn,
                num=num, den=den, emt=emt, dd=dd, m_new=m_new, decay=decay, e=e, kf=kf, wk=wk)


def _ml_headnorm(h):
    mu = jnp.mean(h, axis=1, keepdims=True)
    hc = h - mu
    rstd = lax.rsqrt(jnp.mean(hc * hc, axis=1, keepdims=True) + HEAD_NORM_EPS)
    return hc * rstd, rstd


def _ml_chunk_specs(nc, rev, bsz):
    def cc(c):
        return (nc - 1 - c) if rev else c

    return dict(
        hrow=pl.BlockSpec((bsz, CHUNK, ML_DH), lambda hd, c: (0, cc(c), hd)),
        gates=pl.BlockSpec((bsz, CHUNK, CHUNK), lambda hd, c: (0, cc(c), 0)),
        bias=pl.BlockSpec((1, CHUNK), lambda hd, c: (0, 0)),
        hvec=pl.BlockSpec((1, ML_DH), lambda hd, c: (0, hd)),
        cs=pl.BlockSpec((bsz, 1, ML_DH, ML_DH), lambda hd, c: (0, hd * nc + cc(c), 0, 0)),
        ns=pl.BlockSpec((bsz, 1, 1, ML_DH), lambda hd, c: (0, hd * nc + cc(c), 0, 0)),
        ms=pl.BlockSpec((bsz, 1, 1, CHUNK), lambda hd, c: (0, hd * nc + cc(c), 0, 0)),
        dgates=pl.BlockSpec((1, bsz, CHUNK, CHUNK), lambda hd, c: (hd, 0, cc(c), 0)),
    )


def _seq(a, bsz):
    return a.reshape(bsz, a.shape[0] // bsz, a.shape[1])


def _ml_chunk_fwd(qs, k, v, gates, b_gate, pre, z, nw, sk, bsz, nc):
    r = qs.shape[0]
    tp = r // bsz
    sp = _ml_chunk_specs(nc, False, bsz)

    def body(qs_all, k_all, v_all, gt_all, bg_ref, pre_all, z_all, nw_ref, sk_ref,
             h_all, yb_all, cs_all, ns_all, ms_all, c_sall, n_sall, m_sall):
        hd, c = pl.program_id(0), pl.program_id(1)

        @pl.when(c == 0)
        def _():
            c_sall[...] = jnp.zeros_like(c_sall)
            n_sall[...] = jnp.zeros_like(n_sall)
            m_sall[...] = jnp.zeros_like(m_sall)

        for bi in range(bsz):
            one(hd, c, qs_all.at[bi], k_all.at[bi], v_all.at[bi], gt_all.at[bi], bg_ref, pre_all.at[bi], z_all.at[bi],
                nw_ref, sk_ref, h_all.at[bi], yb_all.at[bi], cs_all.at[bi], ns_all.at[bi], ms_all.at[bi],
                c_sall.at[bi], n_sall.at[bi], m_sall.at[bi])

    def one(hd, c, qs_ref, k_ref, v_ref, gt_ref, bg_ref, pre_ref, z_ref, nw_ref, sk_ref,
            h_ref, yb_ref, cs_ref, ns_ref, ms_ref, c_s, n_s, m_s):
        cmat, nvec, m = c_s[...], n_s[...], m_s[...]
        cs_ref[0] = cmat
        ns_ref[0] = nvec
        ms_ref[0] = jnp.broadcast_to(m, (1, CHUNK))
        v_ = v_ref[...]
        co = _ml_core(gt_ref[...] + bg_ref[...], hd, c == 0, m, qs_ref[...], k_ref[...], v_, cmat, nvec)
        h = co["num"] / co["dd"]
        h_ref[...] = h
        hn, _ = _ml_headnorm(h)
        yb_ref[...] = _bf((hn * nw_ref[...] + sk_ref[...] * _silu(pre_ref[...])) * _silu(z_ref[...]))
        c_s[...] = co["decay"] * cmat + _dot(_bf(co["wk"]), v_, TN)
        n_s[...] = co["decay"] * nvec + jnp.sum(co["wk"], axis=0, keepdims=True)
        m_s[...] = co["m_new"]

    nst = ML_HEADS * nc
    h, yb, cs, ns, ms = _pc(
        body, name="ml_chunk_fwd", grid=(ML_HEADS, nc),
        in_specs=[sp["hrow"]] * 3 + [sp["gates"], sp["bias"], sp["hrow"], sp["hrow"], sp["hvec"], sp["hvec"]],
        out_specs=[sp["hrow"], sp["hrow"], sp["cs"], sp["ns"], sp["ms"]],
        out_shape=[jax.ShapeDtypeStruct((bsz, tp, ML_WIDTH), F32), jax.ShapeDtypeStruct((bsz, tp, ML_WIDTH), BF16),
                   jax.ShapeDtypeStruct((bsz, nst, ML_DH, ML_DH), F32),
                   jax.ShapeDtypeStruct((bsz, nst, 1, ML_DH), F32), jax.ShapeDtypeStruct((bsz, nst, 1, CHUNK), F32)],
        scratch=[pltpu.VMEM((bsz, ML_DH, ML_DH), F32), pltpu.VMEM((bsz, 1, ML_DH), F32),
                 pltpu.VMEM((bsz, 1, 1), F32)],
        vmem=12 << 20)(*[_seq(a, bsz) for a in (qs, k, v, gates)], b_gate, _seq(pre, bsz), _seq(z, bsz), nw, sk)
    return h.reshape(r, ML_WIDTH), yb.reshape(r, ML_WIDTH), cs, ns, ms


def _ml_chunk_bwd(dyb, qs, k, v, gates, b_gate, pre, z, nw, sk, h, cs, ns, ms, bsz, nc, dep=None):
    r = qs.shape[0]
    tp = r // bsz
    sp = _ml_chunk_specs(nc, True, bsz)

    def body(dy_all, qs_all, k_all, v_all, gt_all, bg_ref, pre_all, z_all, nw_ref, sk_ref, h_all, cs_all, ns_all,
             ms_all, dq_all, dk_all, dv_all, dz_all, dxc_all, dgt_all, dnw_ref, dsk_ref, dc_sall, dn_sall):
        hd, c = pl.program_id(0), pl.program_id(1)

        @pl.when(c == 0)
        def _():
            for ref in (dnw_ref, dsk_ref, dc_sall, dn_sall):
                ref[...] = jnp.zeros_like(ref)

        for bi in range(bsz):
            one(hd, c, dy_all.at[bi], qs_all.at[bi], k_all.at[bi], v_all.at[bi], gt_all.at[bi], bg_ref,
                pre_all.at[bi], z_all.at[bi], nw_ref, sk_ref, h_all.at[bi], cs_all.at[bi], ns_all.at[bi],
                ms_all.at[bi], dq_all.at[bi], dk_all.at[bi], dv_all.at[bi], dz_all.at[bi], dxc_all.at[bi],
                dgt_all.at[0, bi], dnw_ref, dsk_ref, dc_sall.at[bi], dn_sall.at[bi])

    def one(hd, c, dy_ref, qs_ref, k_ref, v_ref, gt_ref, bg_ref, pre_ref, z_ref, nw_ref, sk_ref, h_ref, cs_ref, ns_ref,
            ms_ref, dq_ref, dk_ref, dv_ref, dz_ref, dxc_ref, dgt_ref, dnw_ref, dsk_ref, dc_s, dn_s):

        qs, k, v = qs_ref[...], k_ref[...], v_ref[...]
        cmat, nvec, m = cs_ref[0], ns_ref[0], ms_ref[0][:, :1]
        co = _ml_core(gt_ref[...] + bg_ref[...], hd, c == nc - 1, m, qs, k, v, cmat, nvec)
        lane, row = co["lane"], co["row"]
        wt, wprev, s, cb, qf = co["wt"], co["wprev"], co["s"], co["cb"], co["qf"]
        h = h_ref[...]
        hn, rstd = _ml_headnorm(h)
        xc = _silu(pre_ref[...])
        zv = z_ref[...]
        nw, sk = nw_ref[...], sk_ref[...]
        dy = dy_ref[...]
        dz_ref[...] = _bf(dy * (hn * nw + sk * xc) * _dsilu(zv))
        do = dy * _silu(zv)
        dsk_ref[...] += jnp.sum(do * xc, axis=0, keepdims=True)
        dnw_ref[...] += jnp.sum(do * hn, axis=0, keepdims=True)
        dxc_ref[...] = do * sk
        dhn = do * nw
        dh = rstd * (dhn - jnp.mean(dhn, axis=1, keepdims=True) - hn * jnp.mean(dhn * hn, axis=1, keepdims=True))
        rinv = 1.0 / co["dd"]
        dnum = dh * rinv
        ddd = -jnp.sum(dh * h, axis=1, keepdims=True) * rinv
        den = co["den"]
        dden = jnp.where(jnp.abs(den) >= co["emt"], ddd * jnp.sign(den), 0.0)
        dnb = _bf(dnum)
        ds = _dot(dnb, v, NT) + dden
        dv = _dot(_bf(s), dnb, TN)
        dnw_ = _bf(dnum * wprev)
        dwn = dden * wprev
        dqs = _dot(dnw_, cb, NT) + dwn * nvec
        dc_out = _dot(qs, dnw_, TN)
        dn_out = jnp.sum(dwn * qf, axis=0, keepdims=True)
        dwprev = jnp.sum(dnum * co["qc"], axis=1, keepdims=True) + dden * co["qn"]
        ds0 = _bf(ds * wt)
        ddm = ds * s
        dqs = dqs + _dot(ds0, k)
        dk = _dot(ds0, qs, TN)
        colc = jnp.sum(ddm.T, axis=1, keepdims=True)
        dbc = dwprev * wprev + jnp.sum(ddm, axis=1, keepdims=True) - colc
        dig = colc
        dcn, dnn = dc_s[...], dn_s[...]
        dcb = _bf(dcn)
        decay, e, kf, wk = co["decay"], co["e"], co["kf"], co["wk"]
        ddecay = (jnp.sum(jnp.sum(dcn * cmat, axis=1, keepdims=True), axis=0, keepdims=True)
                  + jnp.sum(dnn * nvec, axis=1, keepdims=True))
        dwk = _dot(v, dcb, NT) + dnn
        dv = dv + _dot(_bf(wk), dcb)
        dk = dk + e * dwk
        dg = jnp.sum(dwk * kf, axis=1, keepdims=True) * e
        dblast = ddecay * decay + jnp.sum(dg, axis=0, keepdims=True)
        dbc = dbc - dg + jnp.where(row[:, :1] == CHUNK - 1, dblast, 0.0)
        dig = dig + dg
        dc_s[...] = decay * dcn + dc_out
        dn_s[...] = decay * dnn + dn_out
        dlf = _cumsum_rows(jnp.broadcast_to(dbc, (CHUNK, CHUNK)), row, rev=True)[:, :1]
        dfp = dlf * (1.0 - jax.nn.sigmoid(co["fpc"]))
        dig = jnp.where(co["valid"], dig, 0.0)
        dfp = jnp.where(co["valid"], dfp, 0.0)
        dgt_ref[...] = jnp.where(lane == hd, dig, 0.0) + jnp.where(lane == hd + ML_HEADS, dfp, 0.0)
        dq_ref[...] = _bf(dqs * ML_SCALE)
        dk_ref[...] = _bf(dk)
        dv_ref[...] = _bf(dv)

    ob = jax.ShapeDtypeStruct((bsz, tp, ML_WIDTH), BF16)
    dq, dk, dv, dz, dxc, dgt, dnw, dsk = _pc(
        body, name="ml_chunk_bwd", grid=(ML_HEADS, nc),
        in_specs=[sp["hrow"]] * 4 + [sp["gates"], sp["bias"], sp["hrow"], sp["hrow"], sp["hvec"], sp["hvec"],
                                     sp["hrow"], sp["cs"], sp["ns"], sp["ms"]],
        out_specs=[sp["hrow"]] * 5 + [sp["dgates"], sp["hvec"], sp["hvec"]],
        out_shape=[ob, ob, ob, ob, jax.ShapeDtypeStruct((bsz, tp, ML_WIDTH), F32),
                   jax.ShapeDtypeStruct((ML_HEADS, bsz, tp, CHUNK), F32),
                   jax.ShapeDtypeStruct((1, ML_WIDTH), F32), jax.ShapeDtypeStruct((1, ML_WIDTH), F32)],
        scratch=[pltpu.VMEM((bsz, ML_DH, ML_DH), F32), pltpu.VMEM((bsz, 1, ML_DH), F32)], vmem=16 << 20, dep=dep,
    )(*[_seq(a, bsz) for a in (dyb, qs, k, v, gates)], b_gate, _seq(pre, bsz), _seq(z, bsz), nw, sk, _seq(h, bsz),
      cs, ns, ms)
    return (dq.reshape(r, ML_WIDTH), dk.reshape(r, ML_WIDTH), dv.reshape(r, ML_WIDTH), dz.reshape(r, ML_WIDTH),
            dxc.reshape(r, ML_WIDTH), dgt.reshape(ML_HEADS, r, CHUNK), dnw, dsk)


def _ml_pre_bwd(dq, dk, dv, dgates, dxc_skip, pre, x, q, k, v, wq, wk, wv, wgq, wgk, wgv, bsz, nc):
    r = x.shape[0]
    tr = _pick(r, (256, 128))
    nt = r // tr
    hrow = pl.BlockSpec((tr, ML_DH), lambda h, i: (i, h))
    wexp = pl.BlockSpec((1, ML_DH, ML_DH), lambda h, i: (h, 0, 0))
    wcmp = pl.BlockSpec((1, ML_DH, CHUNK), lambda h, i: (h, 0, 0))
    wg = pl.BlockSpec((ML_DH, CHUNK), lambda h, i: (h, 0))
    dgs = pl.BlockSpec((ML_HEADS, tr, CHUNK), lambda h, i: (0, i, 0))
    bgs = pl.BlockSpec((1, 1, CHUNK), lambda h, i: (h, 0, 0))

    def body(dq_ref, dk_ref, dv_ref, dg_ref, dxs_ref, pre_ref, x_ref, q_ref, k_ref, v_ref, wq_ref, wk_ref, wv_ref,
             gq_ref, gk_ref, gv_ref, dpre_ref, dxv_ref, cq_ref, ck_ref, cv_ref, dgq_ref, dgk_ref, dgv_ref, dbg_ref,
             dwq_ref, dwk_ref, dwv_ref):
        i = pl.program_id(1)

        @pl.when(i == 0)
        def _():
            for ref in (dwq_ref, dwk_ref, dwv_ref, dgq_ref, dgk_ref, dgv_ref, dbg_ref):
                ref[...] = jnp.zeros_like(ref)

        dgt = dg_ref[0]
        for j in range(1, ML_HEADS):
            dgt = dgt + dg_ref[j]
        dbg_ref[0] += jnp.sum(dgt, axis=0, keepdims=True)
        dgb = _bf(dgt)
        dqt = _bf(dq_ref[...].astype(F32) + _dot(dgb, gq_ref[...], NT))
        dkt = _bf(dk_ref[...].astype(F32) + _dot(dgb, gk_ref[...], NT))
        dvt = _bf(dv_ref[...].astype(F32) + _dot(dgb, gv_ref[...], NT))
        dgq_ref[...] += _dot(q_ref[...], dgb, TN)
        dgk_ref[...] += _dot(k_ref[...], dgb, TN)
        dgv_ref[...] += _dot(v_ref[...], dgb, TN)
        prev = pre_ref[...]
        xcb = _bf(_silu(prev))
        xb = _bf(x_ref[...])
        dwq_ref[...] += _dot(xcb, dqt, TN)
        dwk_ref[...] += _dot(xcb, dkt, TN)
        dwv_ref[...] += _dot(xb, dvt, TN)
        dxc = _dot(dqt, wq_ref[0], NT) + _dot(dkt, wk_ref[0], NT) + dxs_ref[...]
        dpre_ref[...] = dxc * _dsilu(prev)
        dxv_ref[...] = _dot(dvt, wv_ref[0], NT)

        @pl.when(i == nt - 1)
        def _():
            rr = lax.broadcasted_iota(jnp.int32, (ML_DH, ML_DH), 0)
            cc = lax.broadcasted_iota(jnp.int32, (ML_DH, ML_DH), 1)
            diag = rr // QKV_BLOCK == cc // QKV_BLOCK
            fc = lax.broadcasted_iota(jnp.int32, (ML_DH, CHUNK), 0)
            fo = lax.broadcasted_iota(jnp.int32, (ML_DH, CHUNK), 1)
            fold = jnp.where(fc % QKV_BLOCK == fo, 1.0, 0.0).astype(F32)
            for src, dst in ((dwq_ref, cq_ref), (dwk_ref, ck_ref), (dwv_ref, cv_ref)):
                dst[0] = jnp.dot(jnp.where(diag, src[...], 0.0), fold, precision=HI, preferred_element_type=F32)

    f = jax.ShapeDtypeStruct((r, ML_WIDTH), F32)
    wc = jax.ShapeDtypeStruct((ML_HEADS, ML_DH, CHUNK), F32)
    wgs = jax.ShapeDtypeStruct((ML_WIDTH, CHUNK), F32)
    return _pc(body, name="ml_pre_bwd", grid=(ML_HEADS, nt),
               in_specs=[hrow, hrow, hrow, dgs, hrow, hrow, hrow, hrow, hrow, hrow, wexp, wexp, wexp, wg, wg, wg],
               out_specs=[hrow, hrow, wcmp, wcmp, wcmp, wg, wg, wg, bgs],
               out_shape=[f, f, wc, wc, wc, wgs, wgs, wgs, jax.ShapeDtypeStruct((ML_HEADS, 1, CHUNK), F32)],
               scratch=[pltpu.VMEM((ML_DH, ML_DH), F32)] * 3,
               vmem=8 << 20)(dq, dk, dv, dgates, dxc_skip, pre, x, q, k, v, wq, wk, wv, wgq, wgk, wgv)


def _pad_lanes(w):
    return jnp.pad(w, ((0, 0), (0, CHUNK - w.shape[1])))


def _ml_weights(conv_w, conv_b, wq, wk, wv, w_gate, b_gate, norm_w, skip):
    return dict(
        conv_w=conv_w, conv_b=conv_b,
        wq=_bf(_headwise_expand(wq)), wk=_bf(_headwise_expand(wk)), wv=_bf(_headwise_expand(wv)),
        wgq=_bf(_pad_lanes(w_gate[:ML_WIDTH])), wgk=_bf(_pad_lanes(w_gate[ML_WIDTH:2 * ML_WIDTH])),
        wgv=_bf(_pad_lanes(w_gate[2 * ML_WIDTH:])), b_gate=_pad_lanes(b_gate.reshape(1, -1)),
        norm=norm_w.reshape(1, ML_WIDTH), skip=skip.reshape(1, ML_WIDTH))


def _ml_layer_fwd(x, z, w, bsz, nc):
    pre = _conv_fwd(x, w["conv_w"], w["conv_b"], bsz, nc, name="ml_conv")
    q, qs, k, v, gates = _ml_pre(pre, x, w["wq"], w["wk"], w["wv"], w["wgq"], w["wgk"], w["wgv"], bsz, nc)
    h, yb, cs, ns, ms = _ml_chunk_fwd(qs, k, v, gates, w["b_gate"], pre, z, w["norm"], w["skip"], bsz, nc)
    return yb, dict(pre=pre, q=q, qs=qs, k=k, v=v, gates=gates, h=h, cs=cs, ns=ns, ms=ms)


def _ml_layer_bwd(dyb, x, z, sv, w, bsz, nc, dep=None):
    dq, dk, dv, dz, dxc, dgates, dnw, dsk = _ml_chunk_bwd(
        dyb, sv["qs"], sv["k"], sv["v"], sv["gates"], w["b_gate"], sv["pre"], z, w["norm"], w["skip"], sv["h"],
        sv["cs"], sv["ns"], sv["ms"], bsz, nc, dep=dep)
    dpre, dxv, dwq, dwk, dwv, dgq, dgk, dgv, dbg = _ml_pre_bwd(
        dq, dk, dv, dgates, dxc, sv["pre"], x, sv["q"], sv["k"], sv["v"], w["wq"], w["wk"], w["wv"], w["wgq"],
        w["wgk"], w["wgv"], bsz, nc)
    dx, dcw, dcb = _conv_bwd(dpre, x, w["conv_w"], bsz, nc, name="ml_conv_bwd", add=dxv)
    ng = 2 * ML_HEADS
    grads = dict(
        ml_conv_w=dcw, ml_conv_b=dcb, ml_wq=_headwise_extract(dwq), ml_wk=_headwise_extract(dwk),
        ml_wv=_headwise_extract(dwv), ml_w_gate=jnp.concatenate([dgq[:, :ng], dgk[:, :ng], dgv[:, :ng]], axis=0),
        ml_b_gate=dbg[0][:, :ng], ml_norm=dnw, ml_skip=dsk)
    return dx, dz, grads


HI = lax.Precision.HIGHEST


def _softplus(x):
    return jnp.maximum(x, 0.0) + jnp.log(1.0 + jnp.exp(-jnp.abs(x)))


def _lane_cumsum(x, lane, rev=False):
    del lane
    return jnp.dot(x, _tri(not rev), precision=lax.Precision.HIGHEST, preferred_element_type=F32)


def _head_sum_matrix():
    r = lax.broadcasted_iota(jnp.int32, (SSD_HPG, SSD_GW), 0)
    l = lax.broadcasted_iota(jnp.int32, (SSD_HPG, SSD_GW), 1)
    return jnp.where(l // SSD_P == r, 1.0, 0.0).astype(F32)


def _ssd_core(xs, bm, cm, dt_raw, dt_bias, a_log, first):
    sq = (CHUNK, CHUNK)
    lane8 = lax.broadcasted_iota(jnp.int32, (SSD_HPG, CHUNK), 1)
    lane = lax.broadcasted_iota(jnp.int32, sq, 1)
    row = lax.broadcasted_iota(jnp.int32, sq, 0)
    low = lane < SSD_P
    valid = jnp.logical_or(jnp.logical_not(first), lane8 >= PAD_ROWS)
    pre = dt_raw + dt_bias
    dt = jnp.where(valid, _softplus(pre), 0.0)
    a = -jnp.exp(a_log)
    cum = _lane_cumsum(dt * a, lane8)
    cb = _dot(_bf(cm), _bf(bm), NT)
    heads = []
    for r in range(SSD_HPG):
        rowb = jnp.broadcast_to(cum[r:r + 1, :], sq)
        colb = rowb.T
        seg = jnp.exp(jnp.where(lane <= row, colb - rowb, NEG))
        dtrow = jnp.broadcast_to(dt[r:r + 1, :], sq)
        lastb = colb[CHUNK - 1:CHUNK, :]
        heads.append(dict(seg=seg, dtrow=dtrow, w=cb * seg * dtrow, ecol=jnp.exp(colb),
                          dec=jnp.exp(lastb - colb) * dtrow.T, elast=jnp.exp(lastb)))

    def pairs(key):
        return jnp.concatenate([jnp.where(low[:heads[0][key].shape[0]], heads[2 * j][key], heads[2 * j + 1][key])
                                for j in range(SSD_HPG // 2)], axis=1)

    return dict(lane8=lane8, low=low, valid=valid, pre=pre, dt=dt, a=a, cum=cum, cb=cb, heads=heads,
                expc=pairs("ecol"), dec=pairs("dec"), elast=pairs("elast"))


def _ssd_specs(nc, rev, bsz):
    def cc(c):
        return (nc - 1 - c) if rev else c

    return dict(
        wide=pl.BlockSpec((bsz, CHUNK, SSD_GW), lambda g, c: (0, cc(c), g)),
        narrow=pl.BlockSpec((bsz, CHUNK, SSD_N), lambda g, c: (0, cc(c), g)),
        dtT=pl.BlockSpec((bsz, SSD_HPG, CHUNK), lambda g, c: (0, g, cc(c))),
        hcol=pl.BlockSpec((SSD_HPG, 1), lambda g, c: (g, 0)),
        hacc=pl.BlockSpec((SSD_HPG, CHUNK), lambda g, c: (g, 0)),
        gvec=pl.BlockSpec((1, SSD_GW), lambda g, c: (0, g)),
        state=pl.BlockSpec((bsz, 1, SSD_N, SSD_GW), lambda g, c: (0, g * nc + cc(c), 0, 0)),
    )


def _ssd_chunk_fwd(xs_pre, bm_pre, cm_pre, dt_raw, dt_bias, a_log, d_exp, z, gnorm, bsz, nc):
    tp = xs_pre.shape[1]
    sp = _ssd_specs(nc, False, bsz)

    def body(xs_all, bm_all, cm_all, dt_all, db_ref, al_ref, d_ref, z_all, gn_ref, y_all, yn_all, st_all, st_sall):
        c = pl.program_id(1)

        @pl.when(c == 0)
        def _():
            st_sall[...] = jnp.zeros_like(st_sall)

        for bi in range(bsz):
            one(c, xs_all.at[bi], bm_all.at[bi], cm_all.at[bi], dt_all.at[bi], db_ref, al_ref, d_ref, z_all.at[bi],
                gn_ref, y_all.at[bi], yn_all.at[bi], st_all.at[bi], st_sall.at[bi])

    def one(c, xs_ref, bm_ref, cm_ref, dt_ref, db_ref, al_ref, d_ref, z_ref, gn_ref, y_ref, yn_ref, st_ref, st_s):
        state = st_s[...]
        st_ref[0] = state
        xs, bm, cm = _silu(xs_ref[...]), _silu(bm_ref[...]), _silu(cm_ref[...])
        co = _ssd_core(xs, bm, cm, dt_ref[...], db_ref[...], al_ref[...], c == 0)
        low, hd = co["low"], co["heads"]
        ys = []
        for j in range(SSD_HPG // 2):
            xp = xs[:, j * CHUNK:(j + 1) * CHUNK]
            lhs = jnp.concatenate([hd[2 * j]["w"], hd[2 * j + 1]["w"]], axis=1)
            rhs = jnp.concatenate([jnp.where(low, xp, 0.0), jnp.where(low, 0.0, xp)], axis=0)
            ys.append(_dot(_bf(lhs), _bf(rhs)))
        cmb = _bf(cm)
        y = jnp.concatenate(ys, axis=1) + co["expc"] * _dot(cmb, _bf(state)) + d_ref[...] * xs
        y_ref[...] = y
        yg = y * _silu(z_ref[...])
        rstd = lax.rsqrt(jnp.mean(yg * yg, axis=1, keepdims=True) + NORM_EPS)
        yn_ref[...] = _bf(yg * rstd * gn_ref[...])
        st_s[...] = co["elast"] * state + _dot(_bf(bm), _bf(xs * co["dec"]), TN)

    return _pc(body, name="ssd_chunk_fwd", grid=(SSD_GROUPS, nc),
               in_specs=[sp["wide"], sp["narrow"], sp["narrow"], sp["dtT"], sp["hcol"], sp["hcol"], sp["gvec"],
                         sp["wide"], sp["gvec"]],
               out_specs=[sp["wide"], sp["wide"], sp["state"]],
               out_shape=[jax.ShapeDtypeStruct((bsz, tp, SSD_INNER), F32),
                          jax.ShapeDtypeStruct((bsz, tp, SSD_INNER), BF16),
                          jax.ShapeDtypeStruct((bsz, SSD_GROUPS * nc, SSD_N, SSD_GW), F32)],
               scratch=[pltpu.VMEM((bsz, SSD_N, SSD_GW), F32)], vmem=12 << 20,
               )(xs_pre, bm_pre, cm_pre, dt_raw, dt_bias, a_log, d_exp, z, gnorm)


def _ssd_chunk_bwd(dyn, xs_pre, bm_pre, cm_pre, dt_raw, dt_bias, a_log, d_exp, z, gnorm, y, states, bsz, nc):
    tp = xs_pre.shape[1]
    sp = _ssd_specs(nc, True, bsz)

    def body(dyn_all, xs_all, bm_all, cm_all, dt_all, db_ref, al_ref, d_ref, z_all, gn_ref, y_all, st_all,
             dxs_all, dbm_all, dcm_all, dz_all, ddt_all, dgn_ref, dd_ref, dbias_ref, dal_ref, ds_sall):
        c = pl.program_id(1)

        @pl.when(c == 0)
        def _():
            for ref in (dgn_ref, dd_ref, dbias_ref, dal_ref, ds_sall):
                ref[...] = jnp.zeros_like(ref)

        for bi in range(bsz):
            one(c, dyn_all.at[bi], xs_all.at[bi], bm_all.at[bi], cm_all.at[bi], dt_all.at[bi], db_ref, al_ref, d_ref,
                z_all.at[bi], gn_ref, y_all.at[bi], st_all.at[bi], dxs_all.at[bi], dbm_all.at[bi], dcm_all.at[bi],
                dz_all.at[bi], ddt_all.at[bi], dgn_ref, dd_ref, dbias_ref, dal_ref, ds_sall.at[bi])

    def one(c, dyn_ref, xs_ref, bm_ref, cm_ref, dt_ref, db_ref, al_ref, d_ref, z_ref, gn_ref, y_ref, st_ref,
            dxs_ref, dbm_ref, dcm_ref, dz_ref, ddt_ref, dgn_ref, dd_ref, dbias_ref, dal_ref, ds_s):
        xs_p, bm_p, cm_p = xs_ref[...], bm_ref[...], cm_ref[...]
        xs, bm, cm = _silu(xs_p), _silu(bm_p), _silu(cm_p)
        state = st_ref[0]
        co = _ssd_core(xs, bm, cm, dt_ref[...], db_ref[...], al_ref[...], c == nc - 1)
        low, hd, lane8, cb = co["low"], co["heads"], co["lane8"], co["cb"]
        dt, a, cum = co["dt"], co["a"], co["cum"]
        sub8 = lax.broadcasted_iota(jnp.int32, (SSD_HPG, CHUNK), 0)
        eh = _head_sum_matrix()

        def head_rows(full):
            return lax.dot_general(eh, full, NT, precision=HI, preferred_element_type=F32)

        def head_col(vec):
            return jnp.sum(eh * vec, axis=1, keepdims=True)

        yv, zv, gn = y_ref[...], z_ref[...], gn_ref[...]
        sz = _silu(zv)
        yg = yv * sz
        rstd = lax.rsqrt(jnp.mean(yg * yg, axis=1, keepdims=True) + NORM_EPS)
        yh = yg * rstd
        dyn = dyn_ref[...]
        dgn_ref[...] += jnp.sum(dyn * yh, axis=0, keepdims=True)
        dyh = dyn * gn
        dyg = rstd * (dyh - yh * jnp.mean(dyh * yh, axis=1, keepdims=True))
        dz_ref[...] = _bf(dyg * yv * _dsilu(zv))
        dy = dyg * sz
        dxs = dy * d_ref[...]
        dd_ref[...] += head_col(jnp.sum(dy * xs, axis=0, keepdims=True))
        cmb, bmb, stb = _bf(cm), _bf(bm), _bf(state)
        ysv = _dot(cmb, stb)
        expc = co["expc"]
        dys = _bf(dy * expc)
        dcum = head_rows(dy * ysv * expc)
        dcm = _dot(dys, stb, NT)
        dstate_out = _dot(cmb, dys, TN)
        dcb = jnp.zeros((CHUNK, CHUNK), F32)
        ddt = jnp.zeros((SSD_HPG, CHUNK), F32)
        dxs_pairs = []
        for j in range(SSD_HPG // 2):
            sl = slice(j * CHUNK, (j + 1) * CHUNK)
            dyp, xp = dy[:, sl], _bf(xs[:, sl])
            lhs = _bf(jnp.concatenate([hd[2 * j]["w"], hd[2 * j + 1]["w"]], axis=1))
            both = _dot(lhs, _bf(dyp), TN)
            dxs_pairs.append(jnp.where(low, both[:CHUNK], both[CHUNK:]))
            for q, msk in ((2 * j, low), (2 * j + 1, jnp.logical_not(low))):
                h = hd[q]
                dw = _dot(_bf(jnp.where(msk, dyp, 0.0)), xp, NT)
                dcb = dcb + dw * h["seg"] * h["dtrow"]
                e_ = dw * h["w"]
                dcum_r = jnp.sum(e_.T, axis=0, keepdims=True) - jnp.sum(e_, axis=0, keepdims=True)
                ddt_r = jnp.sum(dw * cb * h["seg"], axis=0, keepdims=True)
                dcum = dcum + jnp.where(sub8 == q, dcum_r, 0.0)
                ddt = ddt + jnp.where(sub8 == q, ddt_r, 0.0)
        dxs = dxs + jnp.concatenate(dxs_pairs, axis=1)
        dcbb = _bf(dcb)
        dcm = dcm + _dot(dcbb, bmb)
        dbm = _dot(dcbb, cmb, TN)
        dsn = ds_s[...]
        dsb = _bf(dsn)
        dec = co["dec"]
        dbm = dbm + _dot(_bf(xs * dec), dsb, NT)
        dxd = _dot(bmb, dsb)
        dxs = dxs + dxd * dec
        ddec = head_rows(dxd * xs)
        last = cum[:, CHUNK - 1:CHUNK]
        erow = jnp.exp(last - cum)
        ddt = ddt + ddec * erow
        dla = ddec * erow * dt
        dlast = (jnp.sum(dla, axis=1, keepdims=True)
                 + head_col(jnp.sum(dsn * state, axis=0, keepdims=True)) * jnp.exp(last))
        dcum = dcum - dla + jnp.where(lane8 == CHUNK - 1, dlast, 0.0)
        ds_s[...] = co["elast"] * dsn + dstate_out
        dda = _lane_cumsum(dcum, lane8, rev=True)
        ddt = jnp.where(co["valid"], ddt + dda * a, 0.0)
        ddt_raw = ddt * jax.nn.sigmoid(co["pre"])
        ddt_ref[...] = ddt_raw
        dbias_ref[...] += jnp.sum(ddt_raw, axis=1, keepdims=True)
        dal_ref[...] += jnp.sum(dda * dt, axis=1, keepdims=True) * a
        dxs_ref[...] = dxs * _dsilu(xs_p)
        dbm_ref[...] = dbm * _dsilu(bm_p)
        dcm_ref[...] = dcm * _dsilu(cm_p)

    st = jax.ShapeDtypeStruct
    hacc = st((SSD_HEADS, CHUNK), F32)
    return _pc(body, name="ssd_chunk_bwd", grid=(SSD_GROUPS, nc),
               in_specs=[sp["wide"], sp["wide"], sp["narrow"], sp["narrow"], sp["dtT"], sp["hcol"], sp["hcol"],
                         sp["gvec"], sp["wide"], sp["gvec"], sp["wide"], sp["state"]],
               out_specs=[sp["wide"], sp["narrow"], sp["narrow"], sp["wide"], sp["dtT"], sp["gvec"], sp["hacc"],
                          sp["hacc"], sp["hacc"]],
               out_shape=[st((bsz, tp, SSD_INNER), F32), st((bsz, tp, SSD_BC), F32), st((bsz, tp, SSD_BC), F32),
                          st((bsz, tp, SSD_INNER), BF16), st((bsz, SSD_HEADS, tp), F32), st((1, SSD_INNER), F32),
                          hacc, hacc, hacc],
               scratch=[pltpu.VMEM((bsz, SSD_N, SSD_GW), F32)], vmem=20 << 20,
               )(dyn, xs_pre, bm_pre, cm_pre, dt_raw, dt_bias, a_log, d_exp, z, gnorm, y, states)


SSD_BC = SSD_GROUPS * SSD_N


def _ssd_weights(conv_w, conv_b, dt_bias, a_log, d, gnorm):
    cuts = (0, SSD_INNER, SSD_INNER + SSD_BC, SSD_INNER + 2 * SSD_BC)
    return dict(
        conv_w=[conv_w[:, cuts[i]:cuts[i + 1]] for i in range(3)],
        conv_b=[conv_b[cuts[i]:cuts[i + 1]] for i in range(3)],
        dt_bias=dt_bias.reshape(SSD_HEADS, 1), a_log=a_log.reshape(SSD_HEADS, 1),
        d_exp=jnp.repeat(d.reshape(SSD_HEADS), SSD_P).reshape(1, SSD_INNER), gnorm=gnorm.reshape(1, SSD_INNER))


def _ssd_layer_fwd(z, xs_in, bm_in, cm_in, dt_rows, w, bsz, nc):
    pres = [_conv_fwd(a, w["conv_w"][i], w["conv_b"][i], bsz, nc, name=f"ssd_conv{i}")
            for i, a in enumerate((xs_in, bm_in, cm_in))]
    def seq(a):
        return a.reshape(bsz, nc * CHUNK, a.shape[-1])

    dt_t = jnp.swapaxes(seq(dt_rows)[:, :, :SSD_HEADS], 1, 2)
    y, yn, states = _ssd_chunk_fwd(seq(pres[0]), seq(pres[1]), seq(pres[2]), dt_t, w["dt_bias"], w["a_log"],
                                   w["d_exp"], seq(z), w["gnorm"], bsz, nc)
    return yn.reshape(-1, SSD_INNER), dict(pres=pres, dt_t=dt_t, y=y, states=states)


def _ssd_layer_bwd(dyn, z, xs_in, bm_in, cm_in, sv, w, bsz, nc):
    pres = sv["pres"]

    def seq(a):
        return a.reshape(bsz, nc * CHUNK, a.shape[-1])

    def rows(a):
        return a.reshape(-1, a.shape[-1])

    dxs_p, dbm_p, dcm_p, dz, ddt_t, dgn, dd, dbias, dal = _ssd_chunk_bwd(
        seq(dyn), seq(pres[0]), seq(pres[1]), seq(pres[2]), sv["dt_t"], w["dt_bias"], w["a_log"], w["d_exp"], seq(z),
        w["gnorm"], sv["y"], sv["states"], bsz, nc)
    dz = rows(dz)
    outs = [_conv_bwd(rows(dp), a, w["conv_w"][i], bsz, nc, name=f"ssd_conv_bwd{i}")
            for i, (dp, a) in enumerate(((dxs_p, xs_in), (dbm_p, bm_in), (dcm_p, cm_in)))]
    ddt = _bf(_pad_lanes(rows(jnp.swapaxes(ddt_t, 1, 2))))
    grads = dict(
        ssd_conv_w=jnp.concatenate([o[1] for o in outs], axis=1),
        ssd_conv_b=jnp.concatenate([o[2] for o in outs], axis=1),
        ssd_dt_bias=dbias[:, 0], ssd_a_log=dal[:, 0], ssd_d=dd[:, 0], ssd_gnorm=dgn)
    return dz, outs[0][0], outs[1][0], outs[2][0], ddt, grads


WNAMES = ("meta_tokens", "ab_norm", "ab_w_in", "s5_lambda_re", "s5_lambda_im", "s5_log_dt", "s5_b_re", "s5_b_im",
          "s5_c_re", "s5_c_im", "s5_d", "s5_glu_w", "s5_glu_b", "ml_conv_w", "ml_conv_b", "ml_wq", "ml_wk", "ml_wv",
          "ml_w_gate", "ml_b_gate", "ml_norm", "ml_skip", "ab_w_out", "ssd_norm", "ssd_w_in", "ssd_conv_w",
          "ssd_conv_b", "ssd_dt_bias", "ssd_a_log", "ssd_d", "ssd_gnorm", "ssd_w_out", "final_norm")
SHARD_AXIS = dict(meta_tokens=1, ab_w_in=2, s5_glu_w=1, ml_conv_w=2, ml_wq=1, ml_wk=1, ml_wv=1, ml_w_gate=1,
                  ab_w_out=1, ssd_norm=1, ssd_w_in=2, ssd_conv_w=2, ssd_conv_b=1, ssd_gnorm=1, ssd_w_out=1)
BIG = ("ab_w_in", "s5_glu_w", "ab_w_out", "ssd_w_in", "ssd_w_out")
SMALL = tuple(n for n in WNAMES if n in SHARD_AXIS and n not in BIG)
REPL = tuple(n for n in WNAMES if n not in SHARD_AXIS)
PACK_ALIGN = 8 * 128


def _pack(arrs):
    lead = arrs[0][1]
    parts = []
    for a, nlead in arrs:
        f = a.reshape(a.shape[:nlead] + (-1,))
        f = jnp.pad(f, [(0, 0)] * nlead + [(0, (-f.shape[-1]) % PACK_ALIGN)])
        parts.append(f.reshape(f.shape[:nlead] + (-1, 128)))
    return jnp.concatenate(parts, axis=lead)


def _unpack(p, shapes):
    out, r0 = [], 0
    lead = p.shape[:-2]
    for s in shapes:
        n = math.prod(s)
        rows = -(-n // PACK_ALIGN) * 8
        seg = p[..., r0:r0 + rows, :].reshape(lead + (rows * 128,))[..., :n]
        out.append(seg.reshape(lead + tuple(s)))
        r0 += rows
    return out


def _assemble(g, axis):
    m = jnp.moveaxis(g, 0, axis)
    return m.reshape(m.shape[:axis] + (m.shape[axis] * m.shape[axis + 1],) + m.shape[axis + 2:])


def _split(full, axis):
    s = full.shape
    m = full.reshape(s[:axis] + (N_DEV, s[axis] // N_DEV) + s[axis + 1:])
    return jnp.moveaxis(m, axis, 0)


def kernel(x, *rest):
    nw = len(WNAMES)
    w = dict(zip(WNAMES, rest[:nw]))
    loss_target = rest[nw]
    mom = dict(zip(WNAMES, rest[nw + 1:2 * nw + 1]))
    var = dict(zip(WNAMES, rest[2 * nw + 1:3 * nw + 1]))
    bsz = x.shape[0]
    nc = 1 + SEQ // CHUNK
    tp = nc * CHUNK

    local = {n: _bf(w[n][0]) for n in BIG}
    small_local = _pack([(w[n], 0) for n in SMALL])
    gs = _exchange_start([small_local], ["ag"], name="gather_s")
    ga = _exchange_start([local["ab_w_in"]], ["ag"], name="gather_a", dep=gs["token"])
    got_s = _exchange_wait(gs, ga["token"])

    def assemble_big(n, got):
        return _assemble(got[:, None], SHARD_AXIS[n])[0]

    full = {}
    for n, g in zip(SMALL, _unpack(got_s[0], [w[n].shape for n in SMALL])):
        full[n] = _assemble(g, SHARD_AXIS[n])[0] if n != "meta_tokens" else _assemble(g, SHARD_AXIS[n])
    for n in REPL:
        full[n] = w[n][0] if n != "final_norm" else w[n]
    glu_b = full["s5_glu_b"].reshape(1, S5_WIDTH)
    meta = jnp.broadcast_to(full["meta_tokens"][None], (bsz, N_META, D_MODEL))
    h0 = jnp.concatenate([jnp.zeros((bsz, PAD_ROWS, D_MODEL), F32), meta, x], axis=1).reshape(bsz * tp, D_MODEL)
    xn0 = _rms_fwd(h0, full["ab_norm"], name="rms0")
    s5p, s5_vjp = _s5_tables(*[full[n] for n in ("s5_lambda_re", "s5_lambda_im", "s5_log_dt", "s5_b_re", "s5_b_im",
                                                   "s5_c_re", "s5_c_im", "s5_d")])
    mlw = _ml_weights(*[full[n] for n in ("ml_conv_w", "ml_conv_b", "ml_wq", "ml_wk", "ml_wv", "ml_w_gate",
                                           "ml_b_gate", "ml_norm", "ml_skip")])
    got_a = _exchange_wait(ga, [xn0, s5p["wbr"], s5p["wcr"], s5p["qr"], s5p["sr"], mlw["wq"], mlw["wk"], mlw["wv"],
                                mlw["wgq"]])
    gb = _exchange_start([local["s5_glu_w"], local["ab_w_out"]], ["ag", "ag"], name="gather_b", dep=got_a[0])
    gc = _exchange_start([local["ssd_w_in"], local["ssd_w_out"]], ["ag", "ag"], name="gather_c", dep=gb["token"])
    full["ab_w_in"] = assemble_big("ab_w_in", got_a[0])
    cuts0 = (0, S5_WIDTH, 2 * S5_WIDTH, 2 * S5_WIDTH + ML_WIDTH, 2 * (S5_WIDTH + ML_WIDTH))
    w_in0 = [full["ab_w_in"][:, cuts0[i]:cuts0[i + 1]] for i in range(4)]

    u, za, xb, zb = [_mm(xn0, wi, "NN", name=f"in0_{i}") for i, wi in enumerate(w_in0)]
    got_b = []

    def glu_w_after(scan_out):
        got_b.extend(_exchange_wait(gb, scan_out))
        return assemble_big("s5_glu_w", got_b[0])

    sv5 = _s5_layer_fwd(u, s5p, glu_w_after, bsz, nc)
    glu_w = assemble_big("s5_glu_w", got_b[0])
    w_out0 = assemble_big("ab_w_out", got_b[1])
    w_out0 = [w_out0[:S5_WIDTH], w_out0[S5_WIDTH:]]
    ya = _s5_post(sv5["y1"], sv5["glu_pre"], glu_b, za)
    yb, svm = _ml_layer_fwd(xb, zb, mlw, bsz, nc)
    h1 = _mm(ya, w_out0[0], "NN", name="out0_a", add=h0)
    h1 = _mm(yb, w_out0[1], "NN", name="out0_b", add=h1)
    got_c = _exchange_wait(gc, h1)
    w_in1, w_out1 = assemble_big("ssd_w_in", got_c[0]), assemble_big("ssd_w_out", got_c[1])
    cuts1 = (0, SSD_INNER, 2 * SSD_INNER, 2 * SSD_INNER + SSD_BC, 2 * SSD_INNER + 2 * SSD_BC)
    w_in1 = [w_in1[:, cuts1[i]:cuts1[i + 1]] for i in range(4)] + [_pad_lanes(w_in1[:, cuts1[4]:])]
    xn1 = _rms_fwd(h1, full["ssd_norm"], name="rms1")
    z1, xs_in, bm_in, cm_in, dt_rows = [_mm(xn1, wi, "NN", name=f"in1_{i}") for i, wi in enumerate(w_in1)]
    ssdw = _ssd_weights(*[full[n] for n in ("ssd_conv_w", "ssd_conv_b", "ssd_dt_bias", "ssd_a_log", "ssd_d",
                                             "ssd_gnorm")])
    yn, svs = _ssd_layer_fwd(z1, xs_in, bm_in, cm_in, dt_rows, ssdw, bsz, nc)
    h2 = _mm(yn, w_out1, "NN", name="out1", add=h1)
    loss_part, dh2, dfinal = _final_loss(h2, full["final_norm"], loss_target, bsz, nc)
    loss = lax.psum(loss_part[0, 0], ("x", "y", "c"))

    g = {"final_norm": dfinal}
    dyn = _mm(dh2, w_out1, "NT", name="d_out1")
    g["ssd_w_out"] = _mm(yn, dh2, "TN", name="dw_out1", out_dtype=BF16)
    dz1, dxs, dbm, dcm, ddt, gs = _ssd_layer_bwd(dyn, z1, xs_in, bm_in, cm_in, svs, ssdw, bsz, nc)
    g.update(gs)
    dps1 = (dz1, dxs, dbm, dcm, ddt)
    dxn1 = None
    for i, (dp, wi) in enumerate(zip(dps1, w_in1)):
        dxn1 = _mm(dp, wi, "NT", name=f"d_in1_{i}", add=dxn1)
    dw1 = [_mm(xn1, dp, "TN", name=f"dw_in1_{i}", out_dtype=BF16) for i, dp in enumerate(dps1)]
    g["ssd_w_in"] = jnp.concatenate(dw1[:4] + [dw1[4][:, :SSD_HEADS]], axis=1)

    def local_shape(n):
        return w[n].shape

    def slabs(n):
        gf = g[n].reshape((1,) + tuple(g[n].shape)) if n != "meta_tokens" else g[n]
        full_shape = tuple(d * (N_DEV if i == SHARD_AXIS[n] else 1) for i, d in enumerate(local_shape(n)))
        return _split(gf.reshape(full_shape), SHARD_AXIS[n])

    x1 = _exchange_start([slabs("ssd_w_in")[:, 0], slabs("ssd_w_out")[:, 0]], ["a2a", "a2a"], name="grads_1")
    dh1, g["ssd_norm"] = _rms_bwd(h1, full["ssd_norm"], dxn1, dh2, name="rms1_bwd", dep=x1["token"])
    dya = _mm(dh1, w_out0[0], "NT", name="d_out0_a")
    dyb = _mm(dh1, w_out0[1], "NT", name="d_out0_b")
    g["ab_w_out"] = jnp.concatenate([_mm(ya, dh1, "TN", name="dw_out0_a", out_dtype=BF16),
                                     _mm(yb, dh1, "TN", name="dw_out0_b", out_dtype=BF16)], axis=0)
    du, dza, g5 = _s5_layer_bwd(dya, u, za, sv5, s5p, s5_vjp, glu_w, glu_b, bsz, nc)
    g.update(g5)
    x2 = _exchange_start([slabs("ab_w_out")[:, 0], _bf(slabs("s5_glu_w")[:, 0])], ["a2a", "a2a"], name="grads_2")
    dxb, dzb, gm = _ml_layer_bwd(dyb, xb, zb, svm, mlw, bsz, nc, dep=x2["token"])
    g.update(gm)
    dps0 = (du, dza, dxb, dzb)
    dxn0 = None
    for i, (dp, wi) in enumerate(zip(dps0, w_in0)):
        dxn0 = _mm(dp, wi, "NT", name=f"d_in0_{i}", add=dxn0)
    dw0 = [_mm(xn0, dp, "TN", name=f"dw_in0_{i}", out_dtype=BF16, tn=S5_WIDTH, slabs=True) for i, dp in enumerate(dps0)]
    dw_in0_slabs = jnp.concatenate(dw0, axis=0)
    dh0, g["ab_norm"] = _rms_bwd(h0, full["ab_norm"], dxn0, dh1, name="rms0_bwd")
    dh0 = dh0.reshape(bsz, tp, D_MODEL)
    grad_x = dh0[:, CHUNK:]
    g["meta_tokens"] = jnp.sum(dh0[:, PAD_ROWS:CHUNK], axis=0)

    small_g = _pack([(slabs(n), 1) for n in SMALL])
    repl_g = _pack([(g[n], 0) for n in REPL])
    x3 = _exchange_start([dw_in0_slabs, small_g, repl_g], ["a2a", "a2a", "ag"], name="grads_3")

    def update_big(n, gp):
        return _adamw(w[n][0], mom[n][0], var[n][0], gp, name=f"adamw_{n}")

    res = {}
    ex1 = _exchange_wait(x1, x3["token"])
    res["ssd_w_in"], res["ssd_w_out"] = update_big("ssd_w_in", ex1[0]), update_big("ssd_w_out", ex1[1])
    ex2 = _exchange_wait(x2, res["ssd_w_out"][0])
    res["ab_w_out"], res["s5_glu_w"] = update_big("ab_w_out", ex2[0]), update_big("s5_glu_w", ex2[1])
    ex3 = _exchange_wait(x3, [res[n][0] for n in ("ssd_w_in", "ssd_w_out", "ab_w_out", "s5_glu_w")])
    res["ab_w_in"] = update_big("ab_w_in", ex3[0])
    for names, gp, tag in ((SMALL, ex3[1], "small"), (REPL, ex3[2], "repl")):
        shapes = [local_shape(n) for n in names]
        packs = [_pack([(d[n], 0) for n in names]) for d in (w, mom, var)]
        outs = _adamw(packs[0], packs[1], packs[2], gp, name=f"adamw_{tag}")
        for k, o in enumerate(outs):
            for n, a in zip(names, _unpack(o, shapes)):
                res.setdefault(n, [None] * 4)[k] = a
    outs = [loss, grad_x]
    for k in range(4):
        outs += [res[n][k].reshape(local_shape(n)) for n in WNAMES]
    return tuple(outs)
```

```python
import functools
import math

import jax
import jax.numpy as jnp
from jax import lax
from jax.experimental import pallas as pl
from jax.experimental.pallas import tpu as pltpu

F32 = jnp.float32
BF16 = jnp.bfloat16

D_MODEL = 2048
SEQ = 2048
N_META = 16
CHUNK = 128
PAD_ROWS = CHUNK - N_META
NORM_EPS = 1e-6
HEAD_NORM_EPS = 1e-5
S5_WIDTH = 1024
S5_GROUPS = 64
S5_GROUP_SIZE = 16
S5_STATE = 64
S5_GB = 8
S5_LANES = S5_GB * S5_STATE
ML_WIDTH = 3072
ML_HEADS = 8
ML_DH = 384
ML_CONV = 4
QKV_BLOCK = 4
SSD_INNER = 4096
SSD_HEADS = 64
SSD_P = 64
SSD_N = 128
SSD_GROUPS = 8
SSD_HPG = 8
SSD_GW = SSD_HPG * SSD_P
N_DEV = 8
ADAM_LR, ADAM_B1, ADAM_B2, ADAM_EPS, ADAM_WD, ADAM_STEP = 0.001, 0.9, 0.999, 1e-08, 0.01, 10
NEG = -1e30
VMEM_CAP = 60 * 1024 * 1024
MM_BLOCK_BUDGET = 22 * 1024 * 1024
MESH = pl.DeviceIdType.MESH

NN = (((1,), (0,)), ((), ()))
NT = (((1,), (1,)), ((), ()))
TN = (((0,), (0,)), ((), ()))


def _dot(a, b, dims=NN):
    return lax.dot_general(a, b, dims, preferred_element_type=F32)


def _bf(x):
    return x.astype(BF16)


def _pick(n, cands):
    for c in cands:
        if n % c == 0:
            return c
    return n


def _nbytes(shape, dtype):
    return math.prod(shape) * jnp.dtype(dtype).itemsize


ANY_SPEC = pl.BlockSpec(memory_space=pl.ANY)


def _pc(body, *, name, grid, in_specs, out_specs, out_shape, scratch=(), vmem=None, dep=None):
    limit = None if vmem is None else int(min(VMEM_CAP, max(32 * 1024 * 1024, 2 * vmem + (8 << 20))))
    n_in = len(in_specs)
    if dep is not None:
        inner = body

        def body(*refs):
            inner(*refs[:n_in], *refs[n_in + 1:])

        in_specs = list(in_specs) + [ANY_SPEC]
    call = pl.pallas_call(
        body, name=name, grid=grid, in_specs=in_specs, out_specs=out_specs, out_shape=out_shape,
        scratch_shapes=list(scratch),
        compiler_params=pltpu.CompilerParams(dimension_semantics=("arbitrary",) * len(grid), vmem_limit_bytes=limit))
    return call if dep is None else (lambda *args: call(*args, dep))


def _silu(x):
    return x * jax.nn.sigmoid(x)


def _dsilu(x):
    s = jax.nn.sigmoid(x)
    return s * (1.0 + x * (1.0 - s))


def _gelu_and_grad(x):
    c0 = math.sqrt(2.0 / math.pi)
    inner = c0 * (x + 0.044715 * x * x * x)
    t = jnp.tanh(inner)
    g = 0.5 * x * (1.0 + t)
    dg = 0.5 * (1.0 + t) + 0.5 * x * (1.0 - t * t) * c0 * (1.0 + 3 * 0.044715 * x * x)
    return g, dg


def _mm(a, b, mode, *, name, add=None, out_dtype=F32, tn=None, slabs=False, dep=None):
    if mode == "NN":
        (m, k), (k2, n) = a.shape, b.shape
    elif mode == "NT":
        (m, k), (n, k2) = a.shape, b.shape
    else:
        (k, m), (k2, n) = a.shape, b.shape
    assert k == k2, (a.shape, b.shape, mode)
    tm = _pick(m, (1088, 1024, 768, 512, 384, 256, 128))
    tn = tn or _pick(n, (512, 384, 256, 128))

    def block_bytes(tk):
        return (_nbytes((tm, tk), a.dtype) + _nbytes((tk, tn), b.dtype) + _nbytes((tm, tn), out_dtype)
                + (_nbytes((tm, tn), F32) if add is not None else 0))

    budget = MM_BLOCK_BUDGET // 2 if mode == "TN" else MM_BLOCK_BUDGET
    tk = k if block_bytes(k) <= budget else _pick(k, (2176, 2048, 1088, 1024, 768, 512, 384, 256, 128))
    nk = k // tk
    dims = {"NN": NN, "NT": NT, "TN": TN}[mode]

    def body(*refs):
        a_ref, b_ref = refs[0], refs[1]
        add_ref = refs[2] if add is not None else None
        o_ref = refs[3] if add is not None else refs[2]

        def finish(r):
            if add_ref is not None:
                r = r + add_ref[...]
            o_ref[...] = r.reshape(o_ref.shape).astype(o_ref.dtype)

        prod = _dot(_bf(a_ref[...]), _bf(b_ref[...]), dims)
        if nk == 1:
            finish(prod)
            return
        acc_ref = refs[-1]
        kk = pl.program_id(2)

        @pl.when(kk == 0)
        def _():
            acc_ref[...] = prod

        @pl.when(kk > 0)
        def _():
            acc_ref[...] += prod

        @pl.when(kk == nk - 1)
        def _():
            finish(acc_ref[...])

    if mode == "NN":
        a_spec = pl.BlockSpec((tm, tk), lambda i, j, kk: (i, kk))
        b_spec = pl.BlockSpec((tk, tn), lambda i, j, kk: (kk, j))
    elif mode == "NT":
        a_spec = pl.BlockSpec((tm, tk), lambda i, j, kk: (i, kk))
        b_spec = pl.BlockSpec((tn, tk), lambda i, j, kk: (j, kk))
    else:
        a_spec = pl.BlockSpec((tk, tm), lambda i, j, kk: (kk, i))
        b_spec = pl.BlockSpec((tk, tn), lambda i, j, kk: (kk, j))
    in_specs = [a_spec, b_spec]
    args = [a, b]
    if add is not None:
        in_specs.append(pl.BlockSpec((tm, tn), lambda i, j, kk: (i, j)))
        args.append(add)
    if slabs:
        out_shape = jax.ShapeDtypeStruct((n // tn, m, tn), out_dtype)
        out_spec = pl.BlockSpec((1, tm, tn), lambda i, j, kk: (j, i, 0))
    else:
        out_shape = jax.ShapeDtypeStruct((m, n), out_dtype)
        out_spec = pl.BlockSpec((tm, tn), lambda i, j, kk: (i, j))
    return _pc(body, name=name, grid=(m // tm, n // tn, nk), in_specs=in_specs, out_specs=out_spec,
               out_shape=out_shape, scratch=[] if nk == 1 else [pltpu.VMEM((tm, tn), F32)],
               vmem=block_bytes(tk) + (0 if nk == 1 else _nbytes((tm, tn), F32) // 2), dep=dep)(*args)


def _rms_fwd(x, g, *, name):
    r, d = x.shape
    tm = _pick(r, (256, 128))

    def body(x_ref, g_ref, o_ref):
        xv = x_ref[...]
        rstd = lax.rsqrt(jnp.mean(xv * xv, axis=1, keepdims=True) + NORM_EPS)
        o_ref[...] = (xv * rstd * g_ref[...]).astype(o_ref.dtype)

    return _pc(body, name=name, grid=(r // tm,),
               in_specs=[pl.BlockSpec((tm, d), lambda i: (i, 0)), pl.BlockSpec((1, d), lambda i: (0, 0))],
               out_specs=pl.BlockSpec((tm, d), lambda i: (i, 0)), out_shape=jax.ShapeDtypeStruct((r, d), BF16),
               vmem=tm * d * 6)(x, g.reshape(1, d))


def _rms_bwd(x, g, dxn, dres, *, name, dep=None):
    r, d = x.shape
    tm = _pick(r, (256, 128))

    def body(x_ref, g_ref, dxn_ref, dres_ref, dx_ref, dg_ref):
        @pl.when(pl.program_id(0) == 0)
        def _():
            dg_ref[...] = jnp.zeros_like(dg_ref)

        xv = x_ref[...]
        rstd = lax.rsqrt(jnp.mean(xv * xv, axis=1, keepdims=True) + NORM_EPS)
        xh = xv * rstd
        dy = dxn_ref[...]
        dg_ref[...] += jnp.sum(dy * xh, axis=0, keepdims=True)
        dyg = dy * g_ref[...]
        dx_ref[...] = dres_ref[...] + rstd * (dyg - xh * jnp.mean(dyg * xh, axis=1, keepdims=True))

    row = pl.BlockSpec((tm, d), lambda i: (i, 0))
    vec = pl.BlockSpec((1, d), lambda i: (0, 0))
    return _pc(body, name=name, grid=(r // tm,), in_specs=[row, vec, row, row], out_specs=[row, vec],
               out_shape=[jax.ShapeDtypeStruct((r, d), F32), jax.ShapeDtypeStruct((1, d), F32)],
               vmem=tm * d * 16, dep=dep)(x, g.reshape(1, d), dxn, dres)


def _final_loss(h, g, target, bsz, nc):
    d = h.shape[1]

    def body(h_ref, g_ref, t_ref, loss_ref, dh_ref, dg_ref):
        b, c = pl.program_id(0), pl.program_id(1)

        @pl.when((b == 0) & (c == 0))
        def _():
            loss_ref[...] = jnp.zeros_like(loss_ref)
            dg_ref[...] = jnp.zeros_like(dg_ref)

        @pl.when(c == 0)
        def _():
            dh_ref[...] = jnp.zeros_like(dh_ref)

        @pl.when(c > 0)
        def _():
            xv = h_ref[...]
            rstd = lax.rsqrt(jnp.mean(xv * xv, axis=1, keepdims=True) + NORM_EPS)
            xh = xv * rstd
            gv = g_ref[...]
            err = xh * gv - t_ref[0]
            loss_ref[...] += 0.5 * jnp.sum(jnp.mean(err * err, axis=1, keepdims=True))
            dy = err * (1.0 / d)
            dg_ref[...] += jnp.sum(dy * xh, axis=0, keepdims=True)
            dyg = dy * gv
            dh_ref[...] = rstd * (dyg - xh * jnp.mean(dyg * xh, axis=1, keepdims=True))

    row = pl.BlockSpec((CHUNK, d), lambda b, c: (b * nc + c, 0))
    vec = pl.BlockSpec((1, d), lambda b, c: (0, 0))
    return _pc(body, name="final_loss", grid=(bsz, nc),
               in_specs=[row, vec, pl.BlockSpec((1, CHUNK, d), lambda b, c: (b, jnp.maximum(c - 1, 0), 0))],
               out_specs=[pl.BlockSpec((8, 128), lambda b, c: (0, 0)), row, vec],
               out_shape=[jax.ShapeDtypeStruct((8, 128), F32), jax.ShapeDtypeStruct(h.shape, F32),
                          jax.ShapeDtypeStruct((1, d), F32)],
               vmem=CHUNK * d * 16)(h, g.reshape(1, d), target)


def _adamw(w, m, v, gparts, *, name):
    r, c = w.shape
    tr = _pick(r, (256, 128)) if r * c * 4 > (1 << 20) else r

    def body(w_ref, m_ref, v_ref, gp_ref, g_ref, d_ref, nm_ref, nv_ref):
        g = gp_ref[0].astype(F32)
        for j in range(1, N_DEV):
            g = g + gp_ref[j].astype(F32)
        mm = ADAM_B1 * m_ref[...] + (1.0 - ADAM_B1) * g
        vv = ADAM_B2 * v_ref[...] + (1.0 - ADAM_B2) * (g * g)
        m_hat = mm / (1.0 - ADAM_B1 ** ADAM_STEP)
        v_hat = vv / (1.0 - ADAM_B2 ** ADAM_STEP)
        g_ref[...] = g
        d_ref[...] = -ADAM_LR * (m_hat / (jnp.sqrt(v_hat) + ADAM_EPS) + ADAM_WD * w_ref[...])
        nm_ref[...] = mm
        nv_ref[...] = vv

    blk = pl.BlockSpec((tr, c), lambda i: (i, 0))
    out = jax.ShapeDtypeStruct((r, c), F32)
    return _pc(body, name=name, grid=(r // tr,),
               in_specs=[blk, blk, blk, pl.BlockSpec((N_DEV, tr, c), lambda i: (0, i, 0))],
               out_specs=[blk, blk, blk, blk], out_shape=[out, out, out, out],
               vmem=tr * c * (4 * 7 + N_DEV * jnp.dtype(gparts.dtype).itemsize))(w, m, v, gparts)


PEERS = (1, 2, 4, 6, 3, 5, 7)
HBM_SPEC = pl.BlockSpec(memory_space=pltpu.HBM)
SEM_SPEC = pl.BlockSpec(memory_space=pltpu.SEMAPHORE)
SIDE_EFFECT = pltpu.SideEffectType.DATAFLOW_SIDE_EFFECTING


def _peer(p):
    x, y, c = lax.axis_index("x"), lax.axis_index("y"), lax.axis_index("c")
    tx, ty, tc = x ^ ((p >> 2) & 1), y ^ ((p >> 1) & 1), c ^ (p & 1)
    return (tx, ty, tc), 4 * tx + 2 * ty + tc


def _place_own(a, kind, *, name):
    rows, cols = a.shape[-2:]
    tr = _pick(rows, (512, 256, 128, 64, 32, 16))
    me = (4 * lax.axis_index("x") + 2 * lax.axis_index("y") + lax.axis_index("c")).astype(jnp.int32).reshape(1)

    def body(me_ref, in_ref, out_ref):
        out_ref[...] = in_ref[...].reshape(out_ref.shape)

    if kind == "a2a":
        in_spec = pl.BlockSpec((1, tr, cols), lambda i, me_ref: (me_ref[0], i, 0))
    else:
        in_spec = pl.BlockSpec((tr, cols), lambda i, me_ref: (i, 0))
    return pl.pallas_call(
        body, name=name, out_shape=jax.ShapeDtypeStruct((N_DEV, rows, cols), a.dtype),
        grid_spec=pltpu.PrefetchScalarGridSpec(
            num_scalar_prefetch=1, grid=(rows // tr,), in_specs=[in_spec],
            out_specs=pl.BlockSpec((1, tr, cols), lambda i, me_ref: (me_ref[0], i, 0))))(me, a)


def _exchange_copies(ins, lands, send_sems, recv_sems, kinds, incoming):
    me = 4 * lax.axis_index("x") + 2 * lax.axis_index("y") + lax.axis_index("c")
    copies = []
    for i, kind in enumerate(kinds):
        for p in PEERS:
            dev, tgt = _peer(p)
            k = i * (N_DEV - 1) + p - 1
            copies.append(pltpu.make_async_remote_copy(
                src_ref=ins[i].at[tgt] if kind == "a2a" else ins[i], dst_ref=lands[i].at[tgt if incoming else me],
                send_sem=send_sems.at[k], recv_sem=recv_sems.at[k], device_id=dev, device_id_type=MESH))
    return copies


def _exchange_start(arrays, kinds, *, name, dep=None):
    n = len(arrays)
    lands = [_place_own(a, k, name=f"{name}_own{i}") for i, (a, k) in enumerate(zip(arrays, kinds))]
    extra = [] if dep is None else [dep]

    def body(*refs):
        ins, lnd = refs[:n], refs[n:2 * n]
        send_sems, recv_sems = refs[2 * n + len(extra)], refs[2 * n + len(extra) + 1]
        token = refs[-1]
        for cp in _exchange_copies(ins, lnd, send_sems, recv_sems, kinds, False):
            cp.start()
        token[...] = jnp.zeros_like(token)

    sem = pltpu.SemaphoreType.DMA((n * (N_DEV - 1),))
    outs = pl.pallas_call(
        body, name=name, in_specs=[HBM_SPEC] * (2 * n) + [ANY_SPEC] * len(extra),
        out_specs=[SEM_SPEC, SEM_SPEC] + [HBM_SPEC] * (2 * n) + [pl.BlockSpec(memory_space=pltpu.VMEM)],
        out_shape=[sem, sem] + [pltpu.HBM(a.shape, a.dtype) for a in arrays + lands]
        + [jax.ShapeDtypeStruct((8, 128), F32)],
        input_output_aliases={i: 2 + i for i in range(2 * n)},
        compiler_params=pltpu.CompilerParams(has_side_effects=SIDE_EFFECT),
    )(*[pltpu.with_memory_space_constraint(a, pltpu.HBM) for a in arrays + lands], *extra)
    return dict(send=outs[0], recv=outs[1], ins=list(outs[2:2 + n]), lands=list(outs[2 + n:2 + 2 * n]),
                token=outs[-1], kinds=kinds, name=name)


def _exchange_wait(h, after):
    n = len(h["ins"])
    kinds = h["kinds"]

    def body(*refs):
        ins, lnd = refs[:n], refs[n:2 * n]
        send_sems, recv_sems = refs[2 * n], refs[2 * n + 1]
        copies = _exchange_copies(ins, lnd, send_sems, recv_sems, kinds, True)
        for cp in copies:
            cp.wait_recv()
        for cp in copies:
            cp.wait_send()

    arrs = h["ins"] + h["lands"]
    after = list(after) if isinstance(after, (list, tuple)) else [after]
    outs = pl.pallas_call(
        body, name=h["name"] + "_wait", in_specs=[HBM_SPEC] * (2 * n) + [SEM_SPEC, SEM_SPEC] + [ANY_SPEC] * len(after),
        out_specs=[HBM_SPEC] * (2 * n), out_shape=[pltpu.HBM(a.shape, a.dtype) for a in arrs],
        input_output_aliases={i: i for i in range(2 * n)},
        compiler_params=pltpu.CompilerParams(has_side_effects=SIDE_EFFECT),
    )(*arrs, h["send"], h["recv"], *after)
    return list(outs[n:])


def _s5_params(lam_re, lam_im, log_dt, b_re, b_im):
    dt = jnp.exp(log_dt)[:, None]
    mag = jnp.exp(lam_re * dt)
    ar, ai = mag * jnp.cos(lam_im * dt), mag * jnp.sin(lam_im * dt)
    den = lam_re * lam_re + lam_im * lam_im
    qr = ((ar - 1.0) * lam_re + ai * lam_im) / den
    qi = (ai * lam_re - (ar - 1.0) * lam_im) / den
    bbr = qr[..., None] * b_re - qi[..., None] * b_im
    bbi = qr[..., None] * b_im + qi[..., None] * b_re
    return ar, ai, bbr, bbi


def _s5_power_table(ar, ai):
    pr, pi = ar.reshape(1, -1), ai.reshape(1, -1)
    while pr.shape[0] < CHUNK:
        sr, si = pr[-1:], pi[-1:]
        pr, pi = (jnp.concatenate([pr, pr * sr - pi * si], axis=0), jnp.concatenate([pi, pr * si + pi * sr], axis=0))
    return pr, pi


def _blockdiag(w, rows, cols):
    w = w.reshape(S5_GB, S5_GB, rows, cols)
    eye = jnp.eye(S5_GB, dtype=w.dtype)
    return jnp.einsum("abrc,bd->abrdc", w, eye).reshape(S5_GB, S5_GB * rows, S5_GB * cols)


def _blockdiag_extract(w, rows, cols):
    w = w.reshape(S5_GB, S5_GB, rows, S5_GB, cols)
    return jnp.einsum("abrbc->abrc", w).reshape(S5_GROUPS, rows, cols)


def _s5_scan_specs(bsz, nc, rev):
    def chunk(b, c):
        return b * nc + ((nc - 1 - c) if rev else c)

    return dict(
        u=pl.BlockSpec((CHUNK, CHUNK), lambda g, b, c: (chunk(b, c), g)),
        x=pl.BlockSpec((CHUNK, S5_LANES), lambda g, b, c: (chunk(b, c), g)),
        wb=pl.BlockSpec((1, CHUNK, S5_LANES), lambda g, b, c: (g, 0, 0)),
        wc=pl.BlockSpec((1, S5_LANES, CHUNK), lambda g, b, c: (g, 0, 0)),
        tab=pl.BlockSpec((CHUNK, S5_LANES), lambda g, b, c: (0, g)),
        step=pl.BlockSpec((8, S5_LANES), lambda g, b, c: (0, g)),
        d=pl.BlockSpec((1, CHUNK), lambda g, b, c: (0, g)),
        lane=pl.BlockSpec((1, S5_LANES), lambda g, b, c: (0, g)),
        xprev=pl.BlockSpec((8, S5_LANES), lambda g, b, c: (jnp.maximum(chunk(b, c) * (CHUNK // 8) - 1, 0), g)),
    )


def _s5_fwd(u, wbr, wbi, pr, pi, sr, si, wcr, wci, d, bsz, nc):
    r = u.shape[0]
    sp = _s5_scan_specs(bsz, nc, False)

    def body(u_ref, wbr_ref, wbi_ref, pr_ref, pi_ref, sr_ref, si_ref, wcr_ref, wci_ref, d_ref,
             xr_ref, xi_ref, y1_ref, g_ref, cr_s, ci_s):
        @pl.when(pl.program_id(2) == 0)
        def _():
            cr_s[...] = jnp.zeros_like(cr_s)
            ci_s[...] = jnp.zeros_like(ci_s)

        uv = u_ref[...]
        ub = _bf(uv)
        xr, xi = _dot(ub, wbr_ref[0]), _dot(ub, wbi_ref[0])
        sub = lax.broadcasted_iota(jnp.int32, (CHUNK, S5_LANES), 0) % 8
        for k in range(3):
            s = 1 << k
            ar, ai = sr_ref[k:k + 1, :], si_ref[k:k + 1, :]
            hr = jnp.where(sub >= s, pltpu.roll(xr, s, 0), 0.0)
            hi = jnp.where(sub >= s, pltpu.roll(xi, s, 0), 0.0)
            xr, xi = xr + (ar * hr - ai * hi), xi + (ar * hi + ai * hr)
        cr, ci = cr_s[...], ci_s[...]
        tr, ti = pr_ref[0:8, :], pi_ref[0:8, :]
        outr, outi = [], []
        for g8 in range(CHUNK // 8):
            br, bi = xr[8 * g8:8 * g8 + 8, :], xi[8 * g8:8 * g8 + 8, :]
            br, bi = br + (tr * cr - ti * ci), bi + (tr * ci + ti * cr)
            cr, ci = br[7:8, :], bi[7:8, :]
            outr.append(br)
            outi.append(bi)
        xr, xi = jnp.concatenate(outr, axis=0), jnp.concatenate(outi, axis=0)
        cr_s[...] = cr
        ci_s[...] = ci
        xr_ref[...] = xr
        xi_ref[...] = xi
        y = _dot(_bf(xr), wcr_ref[0]) - _dot(_bf(xi), wci_ref[0]) + d_ref[...] * uv
        y1_ref[...] = y
        g_ref[...] = _bf(_gelu_and_grad(y)[0])

    return _pc(body, name="s5_fwd", grid=(S5_GB, bsz, nc),
               in_specs=[sp["u"], sp["wb"], sp["wb"], sp["tab"], sp["tab"], sp["step"], sp["step"], sp["wc"], sp["wc"],
                         sp["d"]],
               out_specs=[sp["x"], sp["x"], sp["u"], sp["u"]],
               out_shape=[jax.ShapeDtypeStruct((r, S5_GROUPS * S5_STATE), F32)] * 2
               + [jax.ShapeDtypeStruct((r, S5_WIDTH), F32), jax.ShapeDtypeStruct((r, S5_WIDTH), BF16)],
               scratch=[pltpu.VMEM((1, S5_LANES), F32)] * 2, vmem=4 << 20,
               )(u, wbr, wbi, pr, pi, sr, si, wcr, wci, d)


def _s5_post(y1, glu_pre, glu_b, z):
    r, w = y1.shape
    tm = _pick(r, (256, 128))

    def body(y_ref, p_ref, b_ref, z_ref, o_ref):
        g = _gelu_and_grad(y_ref[...])[0]
        o_ref[...] = _bf(g * jax.nn.sigmoid(p_ref[...] + b_ref[...]) * _silu(z_ref[...]))

    row = pl.BlockSpec((tm, w), lambda i: (i, 0))
    return _pc(body, name="s5_post", grid=(r // tm,), in_specs=[row, row, pl.BlockSpec((1, w), lambda i: (0, 0)), row],
               out_specs=row, out_shape=jax.ShapeDtypeStruct((r, w), BF16), vmem=tm * w * 16)(y1, glu_pre, glu_b, z)


def _s5_post_bwd(dya, y1, glu_pre, glu_b, z):
    r, w = y1.shape
    tm = _pick(r, (256, 128))

    def body(dy_ref, y_ref, p_ref, b_ref, z_ref, dz_ref, dp_ref, dg_ref, db_ref):
        @pl.when(pl.program_id(0) == 0)
        def _():
            db_ref[...] = jnp.zeros_like(db_ref)

        g = _gelu_and_grad(y_ref[...])[0]
        s = jax.nn.sigmoid(p_ref[...] + b_ref[...])
        zv = z_ref[...]
        dy = dy_ref[...]
        do = dy * _silu(zv)
        dz_ref[...] = _bf(dy * g * s * _dsilu(zv))
        dp = do * g * s * (1.0 - s)
        dp_ref[...] = _bf(dp)
        db_ref[...] += jnp.sum(dp, axis=0, keepdims=True)
        dg_ref[...] = do * s

    row = pl.BlockSpec((tm, w), lambda i: (i, 0))
    vec = pl.BlockSpec((1, w), lambda i: (0, 0))
    return _pc(body, name="s5_post_bwd", grid=(r // tm,), in_specs=[row, row, row, vec, row],
               out_specs=[row, row, row, vec],
               out_shape=[jax.ShapeDtypeStruct((r, w), BF16), jax.ShapeDtypeStruct((r, w), BF16),
                          jax.ShapeDtypeStruct((r, w), F32), jax.ShapeDtypeStruct((1, w), F32)],
               vmem=tm * w * 24)(dya, y1, glu_pre, glu_b, z)


def _s5_bwd(dg, y1, u, xr, xi, wbr, wbi, qr, qi, sr, si, wcr, wci, d, bsz, nc):
    r = u.shape[0]
    sp = _s5_scan_specs(bsz, nc, True)

    def body(dg_ref, y1_ref, u_ref, xr_ref, xi_ref, xpr_ref, xpi_ref, wbr_ref, wbi_ref, qr_ref, qi_ref, sr_ref, si_ref,
             wcr_ref, wci_ref, d_ref, du_ref, dd_ref, dwcr_ref, dwci_ref, dwbr_ref, dwbi_ref, dar_ref, dai_ref,
             cr_s, ci_s):
        b, c = pl.program_id(1), pl.program_id(2)

        @pl.when((b == 0) & (c == 0))
        def _():
            for ref in (dd_ref, dwcr_ref, dwci_ref, dwbr_ref, dwbi_ref, dar_ref, dai_ref):
                ref[...] = jnp.zeros_like(ref)

        @pl.when(c == 0)
        def _():
            cr_s[...] = jnp.zeros_like(cr_s)
            ci_s[...] = jnp.zeros_like(ci_s)

        uv = u_ref[...]
        ub = _bf(uv)
        dy = dg_ref[...] * _gelu_and_grad(y1_ref[...])[1]
        dd_ref[...] += jnp.sum(dy * uv, axis=0, keepdims=True)
        dyb = _bf(dy)
        xr, xi = xr_ref[...], xi_ref[...]
        dwcr_ref[0] += _dot(_bf(xr), dyb, TN)
        dwci_ref[0] -= _dot(_bf(xi), dyb, TN)
        lr, li = _dot(dyb, wcr_ref[0], NT), -_dot(dyb, wci_ref[0], NT)
        row = lax.broadcasted_iota(jnp.int32, (CHUNK, S5_LANES), 0)
        sub = row % 8
        for k in range(3):
            s = 1 << k
            ar, ai = sr_ref[k:k + 1, :], si_ref[k:k + 1, :]
            hr = jnp.where(sub < 8 - s, pltpu.roll(lr, CHUNK - s, 0), 0.0)
            hi = jnp.where(sub < 8 - s, pltpu.roll(li, CHUNK - s, 0), 0.0)
            lr, li = lr + (ar * hr + ai * hi), li + (ar * hi - ai * hr)
        cr, ci = cr_s[...], ci_s[...]
        tr, ti = qr_ref[CHUNK - 8:CHUNK, :], qi_ref[CHUNK - 8:CHUNK, :]
        outr, outi = [], []
        for g8 in reversed(range(CHUNK // 8)):
            br, bi = lr[8 * g8:8 * g8 + 8, :], li[8 * g8:8 * g8 + 8, :]
            br, bi = br + (tr * cr + ti * ci), bi + (tr * ci - ti * cr)
            cr, ci = br[0:1, :], bi[0:1, :]
            outr.append(br)
            outi.append(bi)
        lr, li = jnp.concatenate(outr[::-1], axis=0), jnp.concatenate(outi[::-1], axis=0)
        cr_s[...] = cr
        ci_s[...] = ci
        lrb, lib = _bf(lr), _bf(li)
        du_ref[...] = _bf(_dot(lrb, wbr_ref[0], NT) + _dot(lib, wbi_ref[0], NT) + dy * d_ref[...])
        dwbr_ref[0] += _dot(ub, lrb, TN)
        dwbi_ref[0] += _dot(ub, lib, TN)
        first = c == nc - 1
        pr0 = jnp.where(first, 0.0, xpr_ref[7:8, :])
        pi0 = jnp.where(first, 0.0, xpi_ref[7:8, :])
        xpr = jnp.where(row == 0, pr0, pltpu.roll(xr, 1, 0))
        xpi = jnp.where(row == 0, pi0, pltpu.roll(xi, 1, 0))
        dar_ref[...] += jnp.sum(lr * xpr + li * xpi, axis=0, keepdims=True)
        dai_ref[...] += jnp.sum(li * xpr - lr * xpi, axis=0, keepdims=True)

    st = jax.ShapeDtypeStruct
    return _pc(body, name="s5_bwd", grid=(S5_GB, bsz, nc),
               in_specs=[sp["u"], sp["u"], sp["u"], sp["x"], sp["x"], sp["xprev"], sp["xprev"], sp["wb"], sp["wb"],
                         sp["tab"], sp["tab"], sp["step"], sp["step"], sp["wc"], sp["wc"], sp["d"]],
               out_specs=[sp["u"], sp["d"], sp["wc"], sp["wc"], sp["wb"], sp["wb"], sp["lane"], sp["lane"]],
               out_shape=[st((r, S5_WIDTH), BF16), st((1, S5_WIDTH), F32),
                          st((S5_GB, S5_LANES, CHUNK), F32), st((S5_GB, S5_LANES, CHUNK), F32),
                          st((S5_GB, CHUNK, S5_LANES), F32), st((S5_GB, CHUNK, S5_LANES), F32),
                          st((1, S5_GROUPS * S5_STATE), F32), st((1, S5_GROUPS * S5_STATE), F32)],
               scratch=[pltpu.VMEM((1, S5_LANES), F32)] * 2, vmem=6 << 20,
               )(dg, y1, u, xr, xi, xr, xi, wbr, wbi, qr, qi, sr, si, wcr, wci, d)


def _s5_layer_fwd(u, prm, glu_w, bsz, nc):
    xr, xi, y1, g = _s5_fwd(u, prm["wbr"], prm["wbi"], prm["pr"], prm["pi"], prm["sr"], prm["si"], prm["wcr"],
                            prm["wci"], prm["d"], bsz, nc)
    glu_pre = _mm(g, glu_w(y1) if callable(glu_w) else glu_w, "NN", name="s5_glu")
    return dict(xr=xr, xi=xi, y1=y1, g=g, glu_pre=glu_pre)


def _s5_layer_bwd(dya, u, z, sv, prm, pvjp, glu_w, glu_b, bsz, nc):
    dz, dglu, dg_direct, dglu_b = _s5_post_bwd(dya, sv["y1"], sv["glu_pre"], glu_b, z)
    dg = _mm(dglu, glu_w, "NT", name="s5_dg", add=dg_direct)
    dglu_w = _mm(sv["g"], dglu, "TN", name="s5_dglu_w")
    du, dd, dwcr, dwci, dwbr, dwbi, dar, dai = _s5_bwd(
        dg, sv["y1"], u, sv["xr"], sv["xi"], prm["wbr"], prm["wbi"], prm["qr"], prm["qi"], prm["sr"], prm["si"],
        prm["wcr"], prm["wci"], prm["d"], bsz, nc)
    dbbr = jnp.swapaxes(_blockdiag_extract(dwbr, S5_GROUP_SIZE, S5_STATE), 1, 2)
    dbbi = jnp.swapaxes(_blockdiag_extract(dwbi, S5_GROUP_SIZE, S5_STATE), 1, 2)
    dlr, dli, dldt, dbr, dbi = pvjp((dar.reshape(S5_GROUPS, S5_STATE), dai.reshape(S5_GROUPS, S5_STATE), dbbr, dbbi))
    grads = dict(
        s5_lambda_re=dlr, s5_lambda_im=dli, s5_log_dt=dldt, s5_b_re=dbr, s5_b_im=dbi,
        s5_c_re=jnp.swapaxes(_blockdiag_extract(dwcr, S5_STATE, S5_GROUP_SIZE), 1, 2),
        s5_c_im=jnp.swapaxes(_blockdiag_extract(dwci, S5_STATE, S5_GROUP_SIZE), 1, 2),
        s5_d=dd, s5_glu_w=dglu_w, s5_glu_b=dglu_b)
    return du, dz, grads


def _s5_tables(lam_re, lam_im, log_dt, b_re, b_im, c_re, c_im, d):
    (ar, ai, bbr, bbi), vjp = jax.vjp(_s5_params, lam_re, lam_im, log_dt, b_re, b_im)
    pr, pi = _s5_power_table(lax.stop_gradient(ar), lax.stop_gradient(ai))
    steps = [(1 << k) - 1 for k in range(8)]
    prm = dict(
        wbr=_bf(_blockdiag(jnp.swapaxes(bbr, 1, 2), S5_GROUP_SIZE, S5_STATE)),
        wbi=_bf(_blockdiag(jnp.swapaxes(bbi, 1, 2), S5_GROUP_SIZE, S5_STATE)),
        wcr=_bf(_blockdiag(jnp.swapaxes(c_re, 1, 2), S5_STATE, S5_GROUP_SIZE)),
        wci=_bf(_blockdiag(jnp.swapaxes(c_im, 1, 2), S5_STATE, S5_GROUP_SIZE)),
        pr=pr, pi=pi, qr=pr[::-1], qi=pi[::-1],
        sr=jnp.concatenate([pr[i:i + 1] for i in steps], axis=0),
        si=jnp.concatenate([pi[i:i + 1] for i in steps], axis=0), d=d.reshape(1, S5_WIDTH))
    return prm, vjp


def _tile16(p8):
    return jnp.concatenate([p8] * (CHUNK // 8), axis=0)


def _shift_down(x, halo, s, row):
    return jnp.where(row >= s, pltpu.roll(x, s, 0), pltpu.roll(halo, s, 0))


def _shift_up(x, halo, s, row):
    return jnp.where(row < CHUNK - s, pltpu.roll(x, CHUNK - s, 0), pltpu.roll(halo, CHUNK - s, 0))


def _conv_specs(nc, tw):
    def chunk(b, c):
        return b * nc + c

    return dict(
        x=pl.BlockSpec((CHUNK, tw), lambda j, b, c: (chunk(b, c), j)),
        prev=pl.BlockSpec((8, tw), lambda j, b, c: (jnp.maximum(chunk(b, c) * (CHUNK // 8) - 1, 0), j)),
        nxt=pl.BlockSpec((8, tw), lambda j, b, c: ((b * nc + jnp.minimum(c + 1, nc - 1)) * (CHUNK // 8), j)),
        w=pl.BlockSpec((ML_CONV, tw), lambda j, b, c: (0, j)),
        vec=pl.BlockSpec((1, tw), lambda j, b, c: (0, j)),
    )


def _conv_fwd(x, w, bias, bsz, nc, *, name):
    r, wd = x.shape
    tw = _pick(wd, (2048, 1536, 1024, 512, 384, 256, 128))
    sp = _conv_specs(nc, tw)

    def body(x_ref, p_ref, w_ref, b_ref, o_ref):
        c = pl.program_id(2)
        xv = x_ref[...]
        row = lax.broadcasted_iota(jnp.int32, xv.shape, 0)
        halo = jnp.where(c == 0, 0.0, _tile16(p_ref[...]))
        acc = b_ref[...] + w_ref[3:4, :] * xv
        for s in (1, 2, 3):
            acc = acc + w_ref[3 - s:4 - s, :] * _shift_down(xv, halo, s, row)
        o_ref[...] = acc

    return _pc(body, name=name, grid=(wd // tw, bsz, nc), in_specs=[sp["x"], sp["prev"], sp["w"], sp["vec"]],
               out_specs=sp["x"], out_shape=jax.ShapeDtypeStruct((r, wd), F32), vmem=CHUNK * tw * 16,
               )(x, x, w, bias.reshape(1, wd))


def _conv_bwd(dpre, x, w, bsz, nc, *, name, add=None):
    r, wd = x.shape
    tw = _pick(wd, (2048, 1536, 1024, 512, 384, 256, 128))
    sp = _conv_specs(nc, tw)

    def body(*refs):
        d_ref, n_ref, x_ref, p_ref, w_ref = refs[:5]
        add_ref = refs[5] if add is not None else None
        dx_ref, dw_ref, db_ref = refs[-3:]
        b, c = pl.program_id(1), pl.program_id(2)

        @pl.when((b == 0) & (c == 0))
        def _():
            dw_ref[...] = jnp.zeros_like(dw_ref)
            db_ref[...] = jnp.zeros_like(db_ref)

        dv, xv = d_ref[...], x_ref[...]
        row = lax.broadcasted_iota(jnp.int32, xv.shape, 0)
        dhalo = jnp.where(c == nc - 1, 0.0, _tile16(n_ref[...]))
        xhalo = jnp.where(c == 0, 0.0, _tile16(p_ref[...]))
        dx = w_ref[3:4, :] * dv
        for s in (1, 2, 3):
            dx = dx + w_ref[3 - s:4 - s, :] * _shift_up(dv, dhalo, s, row)
        if add_ref is not None:
            dx = dx + add_ref[...]
        dx_ref[...] = _bf(dx)
        db_ref[...] += jnp.sum(dv, axis=0, keepdims=True)
        dw_ref[3:4, :] += jnp.sum(dv * xv, axis=0, keepdims=True)
        for s in (1, 2, 3):
            dw_ref[3 - s:4 - s, :] += jnp.sum(dv * _shift_down(xv, xhalo, s, row), axis=0, keepdims=True)

    ins = [dpre, dpre, x, x, w] + ([add] if add is not None else [])
    specs = [sp["x"], sp["nxt"], sp["x"], sp["prev"], sp["w"]] + ([sp["x"]] if add is not None else [])
    return _pc(body, name=name, grid=(wd // tw, bsz, nc), in_specs=specs, out_specs=[sp["x"], sp["w"], sp["vec"]],
               out_shape=[jax.ShapeDtypeStruct((r, wd), BF16), jax.ShapeDtypeStruct((ML_CONV, wd), F32),
                          jax.ShapeDtypeStruct((1, wd), F32)], vmem=CHUNK * tw * 24)(*ins)


ML_SCALE = ML_DH ** -0.5


def _headwise_expand(w):
    tiled = jnp.tile(w.reshape(ML_HEADS, ML_DH, QKV_BLOCK), (1, 1, ML_DH // QKV_BLOCK))
    blk = jnp.arange(ML_DH) // QKV_BLOCK
    return jnp.where(blk[:, None] == blk[None, :], tiled, 0.0)


def _headwise_extract(w):
    return w[:, :, :QKV_BLOCK].reshape(ML_HEADS * ML_DH // QKV_BLOCK, QKV_BLOCK, QKV_BLOCK)


def _ml_pre(pre, x, wq, wk, wv, wgq, wgk, wgv, bsz, nc):
    r = x.shape[0]
    tr = _pick(r, (256, 128))
    hrow = pl.BlockSpec((tr, ML_DH), lambda h, i: (i, h))
    wexp = pl.BlockSpec((1, ML_DH, ML_DH), lambda h, i: (h, 0, 0))
    wg = pl.BlockSpec((ML_DH, CHUNK), lambda h, i: (h, 0))

    def body(pre_ref, x_ref, wq_ref, wk_ref, wv_ref, gq_ref, gk_ref, gv_ref, q_ref, qs_ref, k_ref, v_ref, gt_ref):
        xcb = _bf(_silu(pre_ref[...]))
        q = _dot(xcb, wq_ref[0])
        k = _dot(xcb, wk_ref[0])
        v = _dot(_bf(x_ref[...]), wv_ref[0])
        qb, kb, vb = _bf(q), _bf(k), _bf(v)
        q_ref[...] = qb
        qs_ref[...] = _bf(q * ML_SCALE)
        k_ref[...] = kb
        v_ref[...] = vb
        gt_ref[0] = _dot(qb, gq_ref[...]) + _dot(kb, gk_ref[...]) + _dot(vb, gv_ref[...])

    o = jax.ShapeDtypeStruct((r, ML_WIDTH), BF16)
    q, qs, k, v, gates8 = _pc(
        body, name="ml_pre", grid=(ML_HEADS, r // tr),
        in_specs=[hrow, hrow, wexp, wexp, wexp, wg, wg, wg],
        out_specs=[hrow, hrow, hrow, hrow, pl.BlockSpec((1, tr, CHUNK), lambda h, i: (h, i, 0))],
        out_shape=[o, o, o, o, jax.ShapeDtypeStruct((ML_HEADS, r, CHUNK), F32)], vmem=6 << 20,
    )(pre, x, wq, wk, wv, wgq, wgk, wgv)

    def sum_body(g_ref, o_ref):
        acc = g_ref[0]
        for j in range(1, ML_HEADS):
            acc = acc + g_ref[j]
        o_ref[...] = acc

    gates = _pc(sum_body, name="ml_gates_sum", grid=(r // tr,),
                in_specs=[pl.BlockSpec((ML_HEADS, tr, CHUNK), lambda i: (0, i, 0))],
                out_specs=pl.BlockSpec((tr, CHUNK), lambda i: (i, 0)),
                out_shape=jax.ShapeDtypeStruct((r, CHUNK), F32), vmem=2 << 20)(gates8)
    return q, qs, k, v, gates


def _tri(rev):
    r = lax.broadcasted_iota(jnp.int32, (CHUNK, CHUNK), 0)
    c = lax.broadcasted_iota(jnp.int32, (CHUNK, CHUNK), 1)
    return jnp.where((c >= r) if rev else (c <= r), 1.0, 0.0).astype(F32)


def _cumsum_rows(x, row, rev=False):
    for k in range(7):
        s = 1 << k
        if rev:
            x = x + jnp.where(row < CHUNK - s, pltpu.roll(x, CHUNK - s, 0), 0.0)
        else:
            x = x + jnp.where(row >= s, pltpu.roll(x, s, 0), 0.0)
    return x


def _log_sigmoid(x):
    return jnp.minimum(x, 0.0) - jnp.log(1.0 + jnp.exp(-jnp.abs(x)))


def _ml_core(gates, hd, first, m, qs, k, v, cmat, nvec):
    sq = (CHUNK, CHUNK)
    lane = lax.broadcasted_iota(jnp.int32, sq, 1)
    row = lax.broadcasted_iota(jnp.int32, sq, 0)
    igc = jnp.sum(jnp.where(lane == hd, gates, 0.0), axis=1, keepdims=True)
    fpc = jnp.sum(jnp.where(lane == hd + ML_HEADS, gates, 0.0), axis=1, keepdims=True)
    valid = jnp.logical_or(jnp.logical_not(first), row[:, :1] >= PAD_ROWS)
    igc = jnp.where(valid, igc, NEG)
    lfc = jnp.where(valid, _log_sigmoid(fpc), 0.0)
    bcb = _cumsum_rows(jnp.broadcast_to(lfc, sq), row)
    igb = jnp.broadcast_to(igc, sq)
    dm = jnp.where(lane <= row, bcb - (bcb - igb).T, NEG)
    bc = bcb[:, :1]
    inter = bc + m
    mt = jnp.maximum(inter, jnp.max(dm, axis=1, keepdims=True))
    wt = jnp.exp(dm - mt)
    wprev = jnp.exp(inter - mt)
    s0 = _dot(qs, k, NT)
    s = s0 * wt
    cb = _bf(cmat)
    qc = _dot(qs, cb)
    qf = qs.astype(F32)
    qn = jnp.sum(qf * nvec, axis=1, keepdims=True)
    num = _dot(_bf(s), v) + wprev * qc
    den = jnp.sum(s, axis=1, keepdims=True) + wprev * qn
    emt = jnp.exp(-mt)
    dd = jnp.maximum(jnp.abs(den), emt)
    blast = bcb[CHUNK - 1:CHUNK, :1]
    g = blast - bc + igc
    m_new = jnp.maximum(blast + m, jnp.max(g, axis=0, keepdims=True))
    decay = jnp.exp(blast + m - m_new)
    e = jnp.exp(g - m_new)
    kf = k.astype(F32)
    wk = e * kf
    return dict(lane=lane, row=row, fpc=fpc, valid=valid, wt=wt, wprev=wprev, s=s, cb=cb, qc=qc, qf=qf, qn=qn,
                num=num, den=den, emt=emt, dd=dd, m_new=m_new, decay=decay, e=e, kf=kf, wk=wk)


def _ml_headnorm(h):
    mu = jnp.mean(h, axis=1, keepdims=True)
    hc = h - mu
    rstd = lax.rsqrt(jnp.mean(hc * hc, axis=1, keepdims=True) + HEAD_NORM_EPS)
    return hc * rstd, rstd


def _ml_chunk_specs(nc, rev, bsz):
    def cc(c):
        return (nc - 1 - c) if rev else c

    return dict(
        hrow=pl.BlockSpec((bsz, CHUNK, ML_DH), lambda hd, c: (0, cc(c), hd)),
        gates=pl.BlockSpec((bsz, CHUNK, CHUNK), lambda hd, c: (0, cc(c), 0)),
        bias=pl.BlockSpec((1, CHUNK), lambda hd, c: (0, 0)),
        hvec=pl.BlockSpec((1, ML_DH), lambda hd, c: (0, hd)),
        cs=pl.BlockSpec((bsz, 1, ML_DH, ML_DH), lambda hd, c: (0, hd * nc + cc(c), 0, 0)),
        ns=pl.BlockSpec((bsz, 1, 1, ML_DH), lambda hd, c: (0, hd * nc + cc(c), 0, 0)),
        ms=pl.BlockSpec((bsz, 1, 1, CHUNK), lambda hd, c: (0, hd * nc + cc(c), 0, 0)),
        dgates=pl.BlockSpec((1, bsz, CHUNK, CHUNK), lambda hd, c: (hd, 0, cc(c), 0)),
    )


def _seq(a, bsz):
    return a.reshape(bsz, a.shape[0] // bsz, a.shape[1])


def _ml_chunk_fwd(qs, k, v, gates, b_gate, pre, z, nw, sk, bsz, nc):
    r = qs.shape[0]
    tp = r // bsz
    sp = _ml_chunk_specs(nc, False, bsz)

    def body(qs_all, k_all, v_all, gt_all, bg_ref, pre_all, z_all, nw_ref, sk_ref,
             h_all, yb_all, cs_all, ns_all, ms_all, c_sall, n_sall, m_sall):
        hd, c = pl.program_id(0), pl.program_id(1)

        @pl.when(c == 0)
        def _():
            c_sall[...] = jnp.zeros_like(c_sall)
            n_sall[...] = jnp.zeros_like(n_sall)
            m_sall[...] = jnp.zeros_like(m_sall)

        for bi in range(bsz):
            one(hd, c, qs_all.at[bi], k_all.at[bi], v_all.at[bi], gt_all.at[bi], bg_ref, pre_all.at[bi], z_all.at[bi],
                nw_ref, sk_ref, h_all.at[bi], yb_all.at[bi], cs_all.at[bi], ns_all.at[bi], ms_all.at[bi],
                c_sall.at[bi], n_sall.at[bi], m_sall.at[bi])

    def one(hd, c, qs_ref, k_ref, v_ref, gt_ref, bg_ref, pre_ref, z_ref, nw_ref, sk_ref,
            h_ref, yb_ref, cs_ref, ns_ref, ms_ref, c_s, n_s, m_s):
        cmat, nvec, m = c_s[...], n_s[...], m_s[...]
        cs_ref[0] = cmat
        ns_ref[0] = nvec
        ms_ref[0] = jnp.broadcast_to(m, (1, CHUNK))
        v_ = v_ref[...]
        co = _ml_core(gt_ref[...] + bg_ref[...], hd, c == 0, m, qs_ref[...], k_ref[...], v_, cmat, nvec)
        h = co["num"] / co["dd"]
        h_ref[...] = h
        hn, _ = _ml_headnorm(h)
        yb_ref[...] = _bf((hn * nw_ref[...] + sk_ref[...] * _silu(pre_ref[...])) * _silu(z_ref[...]))
        c_s[...] = co["decay"] * cmat + _dot(_bf(co["wk"]), v_, TN)
        n_s[...] = co["decay"] * nvec + jnp.sum(co["wk"], axis=0, keepdims=True)
        m_s[...] = co["m_new"]

    nst = ML_HEADS * nc
    h, yb, cs, ns, ms = _pc(
        body, name="ml_chunk_fwd", grid=(ML_HEADS, nc),
        in_specs=[sp["hrow"]] * 3 + [sp["gates"], sp["bias"], sp["hrow"], sp["hrow"], sp["hvec"], sp["hvec"]],
        out_specs=[sp["hrow"], sp["hrow"], sp["cs"], sp["ns"], sp["ms"]],
        out_shape=[jax.ShapeDtypeStruct((bsz, tp, ML_WIDTH), F32), jax.ShapeDtypeStruct((bsz, tp, ML_WIDTH), BF16),
                   jax.ShapeDtypeStruct((bsz, nst, ML_DH, ML_DH), F32),
                   jax.ShapeDtypeStruct((bsz, nst, 1, ML_DH), F32), jax.ShapeDtypeStruct((bsz, nst, 1, CHUNK), F32)],
        scratch=[pltpu.VMEM((bsz, ML_DH, ML_DH), F32), pltpu.VMEM((bsz, 1, ML_DH), F32),
                 pltpu.VMEM((bsz, 1, 1), F32)],
        vmem=12 << 20)(*[_seq(a, bsz) for a in (qs, k, v, gates)], b_gate, _seq(pre, bsz), _seq(z, bsz), nw, sk)
    return h.reshape(r, ML_WIDTH), yb.reshape(r, ML_WIDTH), cs, ns, ms


def _ml_chunk_bwd(dyb, qs, k, v, gates, b_gate, pre, z, nw, sk, h, cs, ns, ms, bsz, nc, dep=None):
    r = qs.shape[0]
    tp = r // bsz
    sp = _ml_chunk_specs(nc, True, bsz)

    def body(dy_all, qs_all, k_all, v_all, gt_all, bg_ref, pre_all, z_all, nw_ref, sk_ref, h_all, cs_all, ns_all,
             ms_all, dq_all, dk_all, dv_all, dz_all, dxc_all, dgt_all, dnw_ref, dsk_ref, dc_sall, dn_sall):
        hd, c = pl.program_id(0), pl.program_id(1)

        @pl.when(c == 0)
        def _():
            for ref in (dnw_ref, dsk_ref, dc_sall, dn_sall):
                ref[...] = jnp.zeros_like(ref)

        for bi in range(bsz):
            one(hd, c, dy_all.at[bi], qs_all.at[bi], k_all.at[bi], v_all.at[bi], gt_all.at[bi], bg_ref,
                pre_all.at[bi], z_all.at[bi], nw_ref, sk_ref, h_all.at[bi], cs_all.at[bi], ns_all.at[bi],
                ms_all.at[bi], dq_all.at[bi], dk_all.at[bi], dv_all.at[bi], dz_all.at[bi], dxc_all.at[bi],
                dgt_all.at[0, bi], dnw_ref, dsk_ref, dc_sall.at[bi], dn_sall.at[bi])

    def one(hd, c, dy_ref, qs_ref, k_ref, v_ref, gt_ref, bg_ref, pre_ref, z_ref, nw_ref, sk_ref, h_ref, cs_ref, ns_ref,
            ms_ref, dq_ref, dk_ref, dv_ref, dz_ref, dxc_ref, dgt_ref, dnw_ref, dsk_ref, dc_s, dn_s):

        qs, k, v = qs_ref[...], k_ref[...], v_ref[...]
        cmat, nvec, m = cs_ref[0], ns_ref[0], ms_ref[0][:, :1]
        co = _ml_core(gt_ref[...] + bg_ref[...], hd, c == nc - 1, m, qs, k, v, cmat, nvec)
        lane, row = co["lane"], co["row"]
        wt, wprev, s, cb, qf = co["wt"], co["wprev"], co["s"], co["cb"], co["qf"]
        h = h_ref[...]
        hn, rstd = _ml_headnorm(h)
        xc = _silu(pre_ref[...])
        zv = z_ref[...]
        nw, sk = nw_ref[...], sk_ref[...]
        dy = dy_ref[...]
        dz_ref[...] = _bf(dy * (hn * nw + sk * xc) * _dsilu(zv))
        do = dy * _silu(zv)
        dsk_ref[...] += jnp.sum(do * xc, axis=0, keepdims=True)
        dnw_ref[...] += jnp.sum(do * hn, axis=0, keepdims=True)
        dxc_ref[...] = do * sk
        dhn = do * nw
        dh = rstd * (dhn - jnp.mean(dhn, axis=1, keepdims=True) - hn * jnp.mean(dhn * hn, axis=1, keepdims=True))
        rinv = 1.0 / co["dd"]
        dnum = dh * rinv
        ddd = -jnp.sum(dh * h, axis=1, keepdims=True) * rinv
        den = co["den"]
        dden = jnp.where(jnp.abs(den) >= co["emt"], ddd * jnp.sign(den), 0.0)
        dnb = _bf(dnum)
        ds = _dot(dnb, v, NT) + dden
        dv = _dot(_bf(s), dnb, TN)
        dnw_ = _bf(dnum * wprev)
        dwn = dden * wprev
        dqs = _dot(dnw_, cb, NT) + dwn * nvec
        dc_out = _dot(qs, dnw_, TN)
        dn_out = jnp.sum(dwn * qf, axis=0, keepdims=True)
        dwprev = jnp.sum(dnum * co["qc"], axis=1, keepdims=True) + dden * co["qn"]
        ds0 = _bf(ds * wt)
        ddm = ds * s
        dqs = dqs + _dot(ds0, k)
        dk = _dot(ds0, qs, TN)
        colc = jnp.sum(ddm.T, axis=1, keepdims=True)
        dbc = dwprev * wprev + jnp.sum(ddm, axis=1, keepdims=True) - colc
        dig = colc
        dcn, dnn = dc_s[...], dn_s[...]
        dcb = _bf(dcn)
        decay, e, kf, wk = co["decay"], co["e"], co["kf"], co["wk"]
        ddecay = (jnp.sum(jnp.sum(dcn * cmat, axis=1, keepdims=True), axis=0, keepdims=True)
                  + jnp.sum(dnn * nvec, axis=1, keepdims=True))
        dwk = _dot(v, dcb, NT) + dnn
        dv = dv + _dot(_bf(wk), dcb)
        dk = dk + e * dwk
        dg = jnp.sum(dwk * kf, axis=1, keepdims=True) * e
        dblast = ddecay * decay + jnp.sum(dg, axis=0, keepdims=True)
        dbc = dbc - dg + jnp.where(row[:, :1] == CHUNK - 1, dblast, 0.0)
        dig = dig + dg
        dc_s[...] = decay * dcn + dc_out
        dn_s[...] = decay * dnn + dn_out
        dlf = _cumsum_rows(jnp.broadcast_to(dbc, (CHUNK, CHUNK)), row, rev=True)[:, :1]
        dfp = dlf * (1.0 - jax.nn.sigmoid(co["fpc"]))
        dig = jnp.where(co["valid"], dig, 0.0)
        dfp = jnp.where(co["valid"], dfp, 0.0)
        dgt_ref[...] = jnp.where(lane == hd, dig, 0.0) + jnp.where(lane == hd + ML_HEADS, dfp, 0.0)
        dq_ref[...] = _bf(dqs * ML_SCALE)
        dk_ref[...] = _bf(dk)
        dv_ref[...] = _bf(dv)

    ob = jax.ShapeDtypeStruct((bsz, tp, ML_WIDTH), BF16)
    dq, dk, dv, dz, dxc, dgt, dnw, dsk = _pc(
        body, name="ml_chunk_bwd", grid=(ML_HEADS, nc),
        in_specs=[sp["hrow"]] * 4 + [sp["gates"], sp["bias"], sp["hrow"], sp["hrow"], sp["hvec"], sp["hvec"],
                                     sp["hrow"], sp["cs"], sp["ns"], sp["ms"]],
        out_specs=[sp["hrow"]] * 5 + [sp["dgates"], sp["hvec"], sp["hvec"]],
        out_shape=[ob, ob, ob, ob, jax.ShapeDtypeStruct((bsz, tp, ML_WIDTH), F32),
                   jax.ShapeDtypeStruct((ML_HEADS, bsz, tp, CHUNK), F32),
                   jax.ShapeDtypeStruct((1, ML_WIDTH), F32), jax.ShapeDtypeStruct((1, ML_WIDTH), F32)],
        scratch=[pltpu.VMEM((bsz, ML_DH, ML_DH), F32), pltpu.VMEM((bsz, 1, ML_DH), F32)], vmem=16 << 20, dep=dep,
    )(*[_seq(a, bsz) for a in (dyb, qs, k, v, gates)], b_gate, _seq(pre, bsz), _seq(z, bsz), nw, sk, _seq(h, bsz),
      cs, ns, ms)
    return (dq.reshape(r, ML_WIDTH), dk.reshape(r, ML_WIDTH), dv.reshape(r, ML_WIDTH), dz.reshape(r, ML_WIDTH),
            dxc.reshape(r, ML_WIDTH), dgt.reshape(ML_HEADS, r, CHUNK), dnw, dsk)


def _ml_pre_bwd(dq, dk, dv, dgates, dxc_skip, pre, x, q, k, v, wq, wk, wv, wgq, wgk, wgv, bsz, nc):
    r = x.shape[0]
    tr = _pick(r, (256, 128))
    nt = r // tr
    hrow = pl.BlockSpec((tr, ML_DH), lambda h, i: (i, h))
    wexp = pl.BlockSpec((1, ML_DH, ML_DH), lambda h, i: (h, 0, 0))
    wcmp = pl.BlockSpec((1, ML_DH, CHUNK), lambda h, i: (h, 0, 0))
    wg = pl.BlockSpec((ML_DH, CHUNK), lambda h, i: (h, 0))
    dgs = pl.BlockSpec((ML_HEADS, tr, CHUNK), lambda h, i: (0, i, 0))
    bgs = pl.BlockSpec((1, 1, CHUNK), lambda h, i: (h, 0, 0))

    def body(dq_ref, dk_ref, dv_ref, dg_ref, dxs_ref, pre_ref, x_ref, q_ref, k_ref, v_ref, wq_ref, wk_ref, wv_ref,
             gq_ref, gk_ref, gv_ref, dpre_ref, dxv_ref, cq_ref, ck_ref, cv_ref, dgq_ref, dgk_ref, dgv_ref, dbg_ref,
             dwq_ref, dwk_ref, dwv_ref):
        i = pl.program_id(1)

        @pl.when(i == 0)
        def _():
            for ref in (dwq_ref, dwk_ref, dwv_ref, dgq_ref, dgk_ref, dgv_ref, dbg_ref):
                ref[...] = jnp.zeros_like(ref)

        dgt = dg_ref[0]
        for j in range(1, ML_HEADS):
            dgt = dgt + dg_ref[j]
        dbg_ref[0] += jnp.sum(dgt, axis=0, keepdims=True)
        dgb = _bf(dgt)
        dqt = _bf(dq_ref[...].astype(F32) + _dot(dgb, gq_ref[...], NT))
        dkt = _bf(dk_ref[...].astype(F32) + _dot(dgb, gk_ref[...], NT))
        dvt = _bf(dv_ref[...].astype(F32) + _dot(dgb, gv_ref[...], NT))
        dgq_ref[...] += _dot(q_ref[...], dgb, TN)
        dgk_ref[...] += _dot(k_ref[...], dgb, TN)
        dgv_ref[...] += _dot(v_ref[...], dgb, TN)
        prev = pre_ref[...]
        xcb = _bf(_silu(prev))
        xb = _bf(x_ref[...])
        dwq_ref[...] += _dot(xcb, dqt, TN)
        dwk_ref[...] += _dot(xcb, dkt, TN)
        dwv_ref[...] += _dot(xb, dvt, TN)
        dxc = _dot(dqt, wq_ref[0], NT) + _dot(dkt, wk_ref[0], NT) + dxs_ref[...]
        dpre_ref[...] = dxc * _dsilu(prev)
        dxv_ref[...] = _dot(dvt, wv_ref[0], NT)

        @pl.when(i == nt - 1)
        def _():
            rr = lax.broadcasted_iota(jnp.int32, (ML_DH, ML_DH), 0)
            cc = lax.broadcasted_iota(jnp.int32, (ML_DH, ML_DH), 1)
            diag = rr // QKV_BLOCK == cc // QKV_BLOCK
            fc = lax.broadcasted_iota(jnp.int32, (ML_DH, CHUNK), 0)
            fo = lax.broadcasted_iota(jnp.int32, (ML_DH, CHUNK), 1)
            fold = jnp.where(fc % QKV_BLOCK == fo, 1.0, 0.0).astype(F32)
            for src, dst in ((dwq_ref, cq_ref), (dwk_ref, ck_ref), (dwv_ref, cv_ref)):
                dst[0] = jnp.dot(jnp.where(diag, src[...], 0.0), fold, precision=HI, preferred_element_type=F32)

    f = jax.ShapeDtypeStruct((r, ML_WIDTH), F32)
    wc = jax.ShapeDtypeStruct((ML_HEADS, ML_DH, CHUNK), F32)
    wgs = jax.ShapeDtypeStruct((ML_WIDTH, CHUNK), F32)
    return _pc(body, name="ml_pre_bwd", grid=(ML_HEADS, nt),
               in_specs=[hrow, hrow, hrow, dgs, hrow, hrow, hrow, hrow, hrow, hrow, wexp, wexp, wexp, wg, wg, wg],
               out_specs=[hrow, hrow, wcmp, wcmp, wcmp, wg, wg, wg, bgs],
               out_shape=[f, f, wc, wc, wc, wgs, wgs, wgs, jax.ShapeDtypeStruct((ML_HEADS, 1, CHUNK), F32)],
               scratch=[pltpu.VMEM((ML_DH, ML_DH), F32)] * 3,
               vmem=8 << 20)(dq, dk, dv, dgates, dxc_skip, pre, x, q, k, v, wq, wk, wv, wgq, wgk, wgv)


def _pad_lanes(w):
    return jnp.pad(w, ((0, 0), (0, CHUNK - w.shape[1])))


def _ml_weights(conv_w, conv_b, wq, wk, wv, w_gate, b_gate, norm_w, skip):
    return dict(
        conv_w=conv_w, conv_b=conv_b,
        wq=_bf(_headwise_expand(wq)), wk=_bf(_headwise_expand(wk)), wv=_bf(_headwise_expand(wv)),
        wgq=_bf(_pad_lanes(w_gate[:ML_WIDTH])), wgk=_bf(_pad_lanes(w_gate[ML_WIDTH:2 * ML_WIDTH])),
        wgv=_bf(_pad_lanes(w_gate[2 * ML_WIDTH:])), b_gate=_pad_lanes(b_gate.reshape(1, -1)),
        norm=norm_w.reshape(1, ML_WIDTH), skip=skip.reshape(1, ML_WIDTH))


def _ml_layer_fwd(x, z, w, bsz, nc):
    pre = _conv_fwd(x, w["conv_w"], w["conv_b"], bsz, nc, name="ml_conv")
    q, qs, k, v, gates = _ml_pre(pre, x, w["wq"], w["wk"], w["wv"], w["wgq"], w["wgk"], w["wgv"], bsz, nc)
    h, yb, cs, ns, ms = _ml_chunk_fwd(qs, k, v, gates, w["b_gate"], pre, z, w["norm"], w["skip"], bsz, nc)
    return yb, dict(pre=pre, q=q, qs=qs, k=k, v=v, gates=gates, h=h, cs=cs, ns=ns, ms=ms)


def _ml_layer_bwd(dyb, x, z, sv, w, bsz, nc, dep=None):
    dq, dk, dv, dz, dxc, dgates, dnw, dsk = _ml_chunk_bwd(
        dyb, sv["qs"], sv["k"], sv["v"], sv["gates"], w["b_gate"], sv["pre"], z, w["norm"], w["skip"], sv["h"],
        sv["cs"], sv["ns"], sv["ms"], bsz, nc, dep=dep)
    dpre, dxv, dwq, dwk, dwv, dgq, dgk, dgv, dbg = _ml_pre_bwd(
        dq, dk, dv, dgates, dxc, sv["pre"], x, sv["q"], sv["k"], sv["v"], w["wq"], w["wk"], w["wv"], w["wgq"],
        w["wgk"], w["wgv"], bsz, nc)
    dx, dcw, dcb = _conv_bwd(dpre, x, w["conv_w"], bsz, nc, name="ml_conv_bwd", add=dxv)
    ng = 2 * ML_HEADS
    grads = dict(
        ml_conv_w=dcw, ml_conv_b=dcb, ml_wq=_headwise_extract(dwq), ml_wk=_headwise_extract(dwk),
        ml_wv=_headwise_extract(dwv), ml_w_gate=jnp.concatenate([dgq[:, :ng], dgk[:, :ng], dgv[:, :ng]], axis=0),
        ml_b_gate=dbg[0][:, :ng], ml_norm=dnw, ml_skip=dsk)
    return dx, dz, grads


HI = lax.Precision.HIGHEST


def _softplus(x):
    return jnp.maximum(x, 0.0) + jnp.log(1.0 + jnp.exp(-jnp.abs(x)))


def _lane_cumsum(x, lane, rev=False):
    del lane
    return jnp.dot(x, _tri(not rev), precision=lax.Precision.HIGHEST, preferred_element_type=F32)


def _head_sum_matrix():
    r = lax.broadcasted_iota(jnp.int32, (SSD_HPG, SSD_GW), 0)
    l = lax.broadcasted_iota(jnp.int32, (SSD_HPG, SSD_GW), 1)
    return jnp.where(l // SSD_P == r, 1.0, 0.0).astype(F32)


def _ssd_core(xs, bm, cm, dt_raw, dt_bias, a_log, first):
    sq = (CHUNK, CHUNK)
    lane8 = lax.broadcasted_iota(jnp.int32, (SSD_HPG, CHUNK), 1)
    lane = lax.broadcasted_iota(jnp.int32, sq, 1)
    row = lax.broadcasted_iota(jnp.int32, sq, 0)
    low = lane < SSD_P
    valid = jnp.logical_or(jnp.logical_not(first), lane8 >= PAD_ROWS)
    pre = dt_raw + dt_bias
    dt = jnp.where(valid, _softplus(pre), 0.0)
    a = -jnp.exp(a_log)
    cum = _lane_cumsum(dt * a, lane8)
    cb = _dot(_bf(cm), _bf(bm), NT)
    heads = []
    for r in range(SSD_HPG):
        rowb = jnp.broadcast_to(cum[r:r + 1, :], sq)
        colb = rowb.T
        seg = jnp.exp(jnp.where(lane <= row, colb - rowb, NEG))
        dtrow = jnp.broadcast_to(dt[r:r + 1, :], sq)
        lastb = colb[CHUNK - 1:CHUNK, :]
        heads.append(dict(seg=seg, dtrow=dtrow, w=cb * seg * dtrow, ecol=jnp.exp(colb),
                          dec=jnp.exp(lastb - colb) * dtrow.T, elast=jnp.exp(lastb)))

    def pairs(key):
        return jnp.concatenate([jnp.where(low[:heads[0][key].shape[0]], heads[2 * j][key], heads[2 * j + 1][key])
                                for j in range(SSD_HPG // 2)], axis=1)

    return dict(lane8=lane8, low=low, valid=valid, pre=pre, dt=dt, a=a, cum=cum, cb=cb, heads=heads,
                expc=pairs("ecol"), dec=pairs("dec"), elast=pairs("elast"))


def _ssd_specs(nc, rev, bsz):
    def cc(c):
        return (nc - 1 - c) if rev else c

    return dict(
        wide=pl.BlockSpec((bsz, CHUNK, SSD_GW), lambda g, c: (0, cc(c), g)),
        narrow=pl.BlockSpec((bsz, CHUNK, SSD_N), lambda g, c: (0, cc(c), g)),
        dtT=pl.BlockSpec((bsz, SSD_HPG, CHUNK), lambda g, c: (0, g, cc(c))),
        hcol=pl.BlockSpec((SSD_HPG, 1), lambda g, c: (g, 0)),
        hacc=pl.BlockSpec((SSD_HPG, CHUNK), lambda g, c: (g, 0)),
        gvec=pl.BlockSpec((1, SSD_GW), lambda g, c: (0, g)),
        state=pl.BlockSpec((bsz, 1, SSD_N, SSD_GW), lambda g, c: (0, g * nc + cc(c), 0, 0)),
    )


def _ssd_chunk_fwd(xs_pre, bm_pre, cm_pre, dt_raw, dt_bias, a_log, d_exp, z, gnorm, bsz, nc):
    tp = xs_pre.shape[1]
    sp = _ssd_specs(nc, False, bsz)

    def body(xs_all, bm_all, cm_all, dt_all, db_ref, al_ref, d_ref, z_all, gn_ref, y_all, yn_all, st_all, st_sall):
        c = pl.program_id(1)

        @pl.when(c == 0)
        def _():
            st_sall[...] = jnp.zeros_like(st_sall)

        for bi in range(bsz):
            one(c, xs_all.at[bi], bm_all.at[bi], cm_all.at[bi], dt_all.at[bi], db_ref, al_ref, d_ref, z_all.at[bi],
                gn_ref, y_all.at[bi], yn_all.at[bi], st_all.at[bi], st_sall.at[bi])

    def one(c, xs_ref, bm_ref, cm_ref, dt_ref, db_ref, al_ref, d_ref, z_ref, gn_ref, y_ref, yn_ref, st_ref, st_s):
        state = st_s[...]
        st_ref[0] = state
        xs, bm, cm = _silu(xs_ref[...]), _silu(bm_ref[...]), _silu(cm_ref[...])
        co = _ssd_core(xs, bm, cm, dt_ref[...], db_ref[...], al_ref[...], c == 0)
        low, hd = co["low"], co["heads"]
        ys = []
        for j in range(SSD_HPG // 2):
            xp = xs[:, j * CHUNK:(j + 1) * CHUNK]
            lhs = jnp.concatenate([hd[2 * j]["w"], hd[2 * j + 1]["w"]], axis=1)
            rhs = jnp.concatenate([jnp.where(low, xp, 0.0), jnp.where(low, 0.0, xp)], axis=0)
            ys.append(_dot(_bf(lhs), _bf(rhs)))
        cmb = _bf(cm)
        y = jnp.concatenate(ys, axis=1) + co["expc"] * _dot(cmb, _bf(state)) + d_ref[...] * xs
        y_ref[...] = y
        yg = y * _silu(z_ref[...])
        rstd = lax.rsqrt(jnp.mean(yg * yg, axis=1, keepdims=True) + NORM_EPS)
        yn_ref[...] = _bf(yg * rstd * gn_ref[...])
        st_s[...] = co["elast"] * state + _dot(_bf(bm), _bf(xs * co["dec"]), TN)

    return _pc(body, name="ssd_chunk_fwd", grid=(SSD_GROUPS, nc),
               in_specs=[sp["wide"], sp["narrow"], sp["narrow"], sp["dtT"], sp["hcol"], sp["hcol"], sp["gvec"],
                         sp["wide"], sp["gvec"]],
               out_specs=[sp["wide"], sp["wide"], sp["state"]],
               out_shape=[jax.ShapeDtypeStruct((bsz, tp, SSD_INNER), F32),
                          jax.ShapeDtypeStruct((bsz, tp, SSD_INNER), BF16),
                          jax.ShapeDtypeStruct((bsz, SSD_GROUPS * nc, SSD_N, SSD_GW), F32)],
               scratch=[pltpu.VMEM((bsz, SSD_N, SSD_GW), F32)], vmem=12 << 20,
               )(xs_pre, bm_pre, cm_pre, dt_raw, dt_bias, a_log, d_exp, z, gnorm)


def _ssd_chunk_bwd(dyn, xs_pre, bm_pre, cm_pre, dt_raw, dt_bias, a_log, d_exp, z, gnorm, y, states, bsz, nc):
    tp = xs_pre.shape[1]
    sp = _ssd_specs(nc, True, bsz)

    def body(dyn_all, xs_all, bm_all, cm_all, dt_all, db_ref, al_ref, d_ref, z_all, gn_ref, y_all, st_all,
             dxs_all, dbm_all, dcm_all, dz_all, ddt_all, dgn_ref, dd_ref, dbias_ref, dal_ref, ds_sall):
        c = pl.program_id(1)

        @pl.when(c == 0)
        def _():
            for ref in (dgn_ref, dd_ref, dbias_ref, dal_ref, ds_sall):
                ref[...] = jnp.zeros_like(ref)

        for bi in range(bsz):
            one(c, dyn_all.at[bi], xs_all.at[bi], bm_all.at[bi], cm_all.at[bi], dt_all.at[bi], db_ref, al_ref, d_ref,
                z_all.at[bi], gn_ref, y_all.at[bi], st_all.at[bi], dxs_all.at[bi], dbm_all.at[bi], dcm_all.at[bi],
                dz_all.at[bi], ddt_all.at[bi], dgn_ref, dd_ref, dbias_ref, dal_ref, ds_sall.at[bi])

    def one(c, dyn_ref, xs_ref, bm_ref, cm_ref, dt_ref, db_ref, al_ref, d_ref, z_ref, gn_ref, y_ref, st_ref,
            dxs_ref, dbm_ref, dcm_ref, dz_ref, ddt_ref, dgn_ref, dd_ref, dbias_ref, dal_ref, ds_s):
        xs_p, bm_p, cm_p = xs_ref[...], bm_ref[...], cm_ref[...]
        xs, bm, cm = _silu(xs_p), _silu(bm_p), _silu(cm_p)
        state = st_ref[0]
        co = _ssd_core(xs, bm, cm, dt_ref[...], db_ref[...], al_ref[...], c == nc - 1)
        low, hd, lane8, cb = co["low"], co["heads"], co["lane8"], co["cb"]
        dt, a, cum = co["dt"], co["a"], co["cum"]
        sub8 = lax.broadcasted_iota(jnp.int32, (SSD_HPG, CHUNK), 0)
        eh = _head_sum_matrix()

        def head_rows(full):
            return lax.dot_general(eh, full, NT, precision=HI, preferred_element_type=F32)

        def head_col(vec):
            return jnp.sum(eh * vec, axis=1, keepdims=True)

        yv, zv, gn = y_ref[...], z_ref[...], gn_ref[...]
        sz = _silu(zv)
        yg = yv * sz
        rstd = lax.rsqrt(jnp.mean(yg * yg, axis=1, keepdims=True) + NORM_EPS)
        yh = yg * rstd
        dyn = dyn_ref[...]
        dgn_ref[...] += jnp.sum(dyn * yh, axis=0, keepdims=True)
        dyh = dyn * gn
        dyg = rstd * (dyh - yh * jnp.mean(dyh * yh, axis=1, keepdims=True))
        dz_ref[...] = _bf(dyg * yv * _dsilu(zv))
        dy = dyg * sz
        dxs = dy * d_ref[...]
        dd_ref[...] += head_col(jnp.sum(dy * xs, axis=0, keepdims=True))
        cmb, bmb, stb = _bf(cm), _bf(bm), _bf(state)
        ysv = _dot(cmb, stb)
        expc = co["expc"]
        dys = _bf(dy * expc)
        dcum = head_rows(dy * ysv * expc)
        dcm = _dot(dys, stb, NT)
        dstate_out = _dot(cmb, dys, TN)
        dcb = jnp.zeros((CHUNK, CHUNK), F32)
        ddt = jnp.zeros((SSD_HPG, CHUNK), F32)
        dxs_pairs = []
        for j in range(SSD_HPG // 2):
            sl = slice(j * CHUNK, (j + 1) * CHUNK)
            dyp, xp = dy[:, sl], _bf(xs[:, sl])
            lhs = _bf(jnp.concatenate([hd[2 * j]["w"], hd[2 * j + 1]["w"]], axis=1))
            both = _dot(lhs, _bf(dyp), TN)
            dxs_pairs.append(jnp.where(low, both[:CHUNK], both[CHUNK:]))
            for q, msk in ((2 * j, low), (2 * j + 1, jnp.logical_not(low))):
                h = hd[q]
                dw = _dot(_bf(jnp.where(msk, dyp, 0.0)), xp, NT)
                dcb = dcb + dw * h["seg"] * h["dtrow"]
                e_ = dw * h["w"]
                dcum_r = jnp.sum(e_.T, axis=0, keepdims=True) - jnp.sum(e_, axis=0, keepdims=True)
                ddt_r = jnp.sum(dw * cb * h["seg"], axis=0, keepdims=True)
                dcum = dcum + jnp.where(sub8 == q, dcum_r, 0.0)
                ddt = ddt + jnp.where(sub8 == q, ddt_r, 0.0)
        dxs = dxs + jnp.concatenate(dxs_pairs, axis=1)
        dcbb = _bf(dcb)
        dcm = dcm + _dot(dcbb, bmb)
        dbm = _dot(dcbb, cmb, TN)
        dsn = ds_s[...]
        dsb = _bf(dsn)
        dec = co["dec"]
        dbm = dbm + _dot(_bf(xs * dec), dsb, NT)
        dxd = _dot(bmb, dsb)
        dxs = dxs + dxd * dec
        ddec = head_rows(dxd * xs)
        last = cum[:, CHUNK - 1:CHUNK]
        erow = jnp.exp(last - cum)
        ddt = ddt + ddec * erow
        dla = ddec * erow * dt
        dlast = (jnp.sum(dla, axis=1, keepdims=True)
                 + head_col(jnp.sum(dsn * state, axis=0, keepdims=True)) * jnp.exp(last))
        dcum = dcum - dla + jnp.where(lane8 == CHUNK - 1, dlast, 0.0)
        ds_s[...] = co["elast"] * dsn + dstate_out
        dda = _lane_cumsum(dcum, lane8, rev=True)
        ddt = jnp.where(co["valid"], ddt + dda * a, 0.0)
        ddt_raw = ddt * jax.nn.sigmoid(co["pre"])
        ddt_ref[...] = ddt_raw
        dbias_ref[...] += jnp.sum(ddt_raw, axis=1, keepdims=True)
        dal_ref[...] += jnp.sum(dda * dt, axis=1, keepdims=True) * a
        dxs_ref[...] = dxs * _dsilu(xs_p)
        dbm_ref[...] = dbm * _dsilu(bm_p)
        dcm_ref[...] = dcm * _dsilu(cm_p)

    st = jax.ShapeDtypeStruct
    hacc = st((SSD_HEADS, CHUNK), F32)
    return _pc(body, name="ssd_chunk_bwd", grid=(SSD_GROUPS, nc),
               in_specs=[sp["wide"], sp["wide"], sp["narrow"], sp["narrow"], sp["dtT"], sp["hcol"], sp["hcol"],
                         sp["gvec"], sp["wide"], sp["gvec"], sp["wide"], sp["state"]],
               out_specs=[sp["wide"], sp["narrow"], sp["narrow"], sp["wide"], sp["dtT"], sp["gvec"], sp["hacc"],
                          sp["hacc"], sp["hacc"]],
               out_shape=[st((bsz, tp, SSD_INNER), F32), st((bsz, tp, SSD_BC), F32), st((bsz, tp, SSD_BC), F32),
                          st((bsz, tp, SSD_INNER), BF16), st((bsz, SSD_HEADS, tp), F32), st((1, SSD_INNER), F32),
                          hacc, hacc, hacc],
               scratch=[pltpu.VMEM((bsz, SSD_N, SSD_GW), F32)], vmem=20 << 20,
               )(dyn, xs_pre, bm_pre, cm_pre, dt_raw, dt_bias, a_log, d_exp, z, gnorm, y, states)


SSD_BC = SSD_GROUPS * SSD_N


def _ssd_weights(conv_w, conv_b, dt_bias, a_log, d, gnorm):
    cuts = (0, SSD_INNER, SSD_INNER + SSD_BC, SSD_INNER + 2 * SSD_BC)
    return dict(
        conv_w=[conv_w[:, cuts[i]:cuts[i + 1]] for i in range(3)],
        conv_b=[conv_b[cuts[i]:cuts[i + 1]] for i in range(3)],
        dt_bias=dt_bias.reshape(SSD_HEADS, 1), a_log=a_log.reshape(SSD_HEADS, 1),
        d_exp=jnp.repeat(d.reshape(SSD_HEADS), SSD_P).reshape(1, SSD_INNER), gnorm=gnorm.reshape(1, SSD_INNER))


def _ssd_layer_fwd(z, xs_in, bm_in, cm_in, dt_rows, w, bsz, nc):
    pres = [_conv_fwd(a, w["conv_w"][i], w["conv_b"][i], bsz, nc, name=f"ssd_conv{i}")
            for i, a in enumerate((xs_in, bm_in, cm_in))]
    def seq(a):
        return a.reshape(bsz, nc * CHUNK, a.shape[-1])

    dt_t = jnp.swapaxes(seq(dt_rows)[:, :, :SSD_HEADS], 1, 2)
    y, yn, states = _ssd_chunk_fwd(seq(pres[0]), seq(pres[1]), seq(pres[2]), dt_t, w["dt_bias"], w["a_log"],
                                   w["d_exp"], seq(z), w["gnorm"], bsz, nc)
    return yn.reshape(-1, SSD_INNER), dict(pres=pres, dt_t=dt_t, y=y, states=states)


def _ssd_layer_bwd(dyn, z, xs_in, bm_in, cm_in, sv, w, bsz, nc):
    pres = sv["pres"]

    def seq(a):
        return a.reshape(bsz, nc * CHUNK, a.shape[-1])

    def rows(a):
        return a.reshape(-1, a.shape[-1])

    dxs_p, dbm_p, dcm_p, dz, ddt_t, dgn, dd, dbias, dal = _ssd_chunk_bwd(
        seq(dyn), seq(pres[0]), seq(pres[1]), seq(pres[2]), sv["dt_t"], w["dt_bias"], w["a_log"], w["d_exp"], seq(z),
        w["gnorm"], sv["y"], sv["states"], bsz, nc)
    dz = rows(dz)
    outs = [_conv_bwd(rows(dp), a, w["conv_w"][i], bsz, nc, name=f"ssd_conv_bwd{i}")
            for i, (dp, a) in enumerate(((dxs_p, xs_in), (dbm_p, bm_in), (dcm_p, cm_in)))]
    ddt = _bf(_pad_lanes(rows(jnp.swapaxes(ddt_t, 1, 2))))
    grads = dict(
        ssd_conv_w=jnp.concatenate([o[1] for o in outs], axis=1),
        ssd_conv_b=jnp.concatenate([o[2] for o in outs], axis=1),
        ssd_dt_bias=dbias[:, 0], ssd_a_log=dal[:, 0], ssd_d=dd[:, 0], ssd_gnorm=dgn)
    return dz, outs[0][0], outs[1][0], outs[2][0], ddt, grads


WNAMES = ("meta_tokens", "ab_norm", "ab_w_in", "s5_lambda_re", "s5_lambda_im", "s5_log_dt", "s5_b_re", "s5_b_im",
          "s5_c_re", "s5_c_im", "s5_d", "s5_glu_w", "s5_glu_b", "ml_conv_w", "ml_conv_b", "ml_wq", "ml_wk", "ml_wv",
          "ml_w_gate", "ml_b_gate", "ml_norm", "ml_skip", "ab_w_out", "ssd_norm", "ssd_w_in", "ssd_conv_w",
          "ssd_conv_b", "ssd_dt_bias", "ssd_a_log", "ssd_d", "ssd_gnorm", "ssd_w_out", "final_norm")
SHARD_AXIS = dict(meta_tokens=1, ab_w_in=2, s5_glu_w=1, ml_conv_w=2, ml_wq=1, ml_wk=1, ml_wv=1, ml_w_gate=1,
                  ab_w_out=1, ssd_norm=1, ssd_w_in=2, ssd_conv_w=2, ssd_conv_b=1, ssd_gnorm=1, ssd_w_out=1)
BIG = ("ab_w_in", "s5_glu_w", "ab_w_out", "ssd_w_in", "ssd_w_out")
SMALL = tuple(n for n in WNAMES if n in SHARD_AXIS and n not in BIG)
REPL = tuple(n for n in WNAMES if n not in SHARD_AXIS)
PACK_ALIGN = 8 * 128


def _pack(arrs):
    lead = arrs[0][1]
    parts = []
    for a, nlead in arrs:
        f = a.reshape(a.shape[:nlead] + (-1,))
        f = jnp.pad(f, [(0, 0)] * nlead + [(0, (-f.shape[-1]) % PACK_ALIGN)])
        parts.append(f.reshape(f.shape[:nlead] + (-1, 128)))
    return jnp.concatenate(parts, axis=lead)


def _unpack(p, shapes):
    out, r0 = [], 0
    lead = p.shape[:-2]
    for s in shapes:
        n = math.prod(s)
        rows = -(-n // PACK_ALIGN) * 8
        seg = p[..., r0:r0 + rows, :].reshape(lead + (rows * 128,))[..., :n]
        out.append(seg.reshape(lead + tuple(s)))
        r0 += rows
    return out


def _assemble(g, axis):
    m = jnp.moveaxis(g, 0, axis)
    return m.reshape(m.shape[:axis] + (m.shape[axis] * m.shape[axis + 1],) + m.shape[axis + 2:])


def _split(full, axis):
    s = full.shape
    m = full.reshape(s[:axis] + (N_DEV, s[axis] // N_DEV) + s[axis + 1:])
    return jnp.moveaxis(m, axis, 0)


def kernel(x, *rest):
    nw = len(WNAMES)
    w = dict(zip(WNAMES, rest[:nw]))
    loss_target = rest[nw]
    mom = dict(zip(WNAMES, rest[nw + 1:2 * nw + 1]))
    var = dict(zip(WNAMES, rest[2 * nw + 1:3 * nw + 1]))
    bsz = x.shape[0]
    nc = 1 + SEQ // CHUNK
    tp = nc * CHUNK

    local = {n: _bf(w[n][0]) for n in BIG}
    small_local = _pack([(w[n], 0) for n in SMALL])
    gs = _exchange_start([small_local], ["ag"], name="gather_s")
    ga = _exchange_start([local["ab_w_in"]], ["ag"], name="gather_a", dep=gs["token"])
    got_s = _exchange_wait(gs, ga["token"])

    def assemble_big(n, got):
        return _assemble(got[:, None], SHARD_AXIS[n])[0]

    full = {}
    for n, g in zip(SMALL, _unpack(got_s[0], [w[n].shape for n in SMALL])):
        full[n] = _assemble(g, SHARD_AXIS[n])[0] if n != "meta_tokens" else _assemble(g, SHARD_AXIS[n])
    for n in REPL:
        full[n] = w[n][0] if n != "final_norm" else w[n]
    glu_b = full["s5_glu_b"].reshape(1, S5_WIDTH)
    meta = jnp.broadcast_to(full["meta_tokens"][None], (bsz, N_META, D_MODEL))
    h0 = jnp.concatenate([jnp.zeros((bsz, PAD_ROWS, D_MODEL), F32), meta, x], axis=1).reshape(bsz * tp, D_MODEL)
    xn0 = _rms_fwd(h0, full["ab_norm"], name="rms0")
    s5p, s5_vjp = _s5_tables(*[full[n] for n in ("s5_lambda_re", "s5_lambda_im", "s5_log_dt", "s5_b_re", "s5_b_im",
                                                   "s5_c_re", "s5_c_im", "s5_d")])
    mlw = _ml_weights(*[full[n] for n in ("ml_conv_w", "ml_conv_b", "ml_wq", "ml_wk", "ml_wv", "ml_w_gate",
                                           "ml_b_gate", "ml_norm", "ml_skip")])
    got_a = _exchange_wait(ga, [xn0, s5p["wbr"], s5p["wcr"], s5p["qr"], s5p["sr"], mlw["wq"], mlw["wk"], mlw["wv"],
                                mlw["wgq"]])
    gb = _exchange_start([local["s5_glu_w"], local["ab_w_out"]], ["ag", "ag"], name="gather_b", dep=got_a[0])
    gc = _exchange_start([local["ssd_w_in"], local["ssd_w_out"]], ["ag", "ag"], name="gather_c", dep=gb["token"])
    full["ab_w_in"] = assemble_big("ab_w_in", got_a[0])
    cuts0 = (0, S5_WIDTH, 2 * S5_WIDTH, 2 * S5_WIDTH + ML_WIDTH, 2 * (S5_WIDTH + ML_WIDTH))
    w_in0 = [full["ab_w_in"][:, cuts0[i]:cuts0[i + 1]] for i in range(4)]

    u, za, xb, zb = [_mm(xn0, wi, "NN", name=f"in0_{i}") for i, wi in enumerate(w_in0)]
    got_b = []

    def glu_w_after(scan_out):
        got_b.extend(_exchange_wait(gb, scan_out))
        return assemble_big("s5_glu_w", got_b[0])

    sv5 = _s5_layer_fwd(u, s5p, glu_w_after, bsz, nc)
    glu_w = assemble_big("s5_glu_w", got_b[0])
    w_out0 = assemble_big("ab_w_out", got_b[1])
    w_out0 = [w_out0[:S5_WIDTH], w_out0[S5_WIDTH:]]
    ya = _s5_post(sv5["y1"], sv5["glu_pre"], glu_b, za)
    yb, svm = _ml_layer_fwd(xb, zb, mlw, bsz, nc)
    h1 = _mm(ya, w_out0[0], "NN", name="out0_a", add=h0)
    h1 = _mm(yb, w_out0[1], "NN", name="out0_b", add=h1)
    got_c = _exchange_wait(gc, h1)
    w_in1, w_out1 = assemble_big("ssd_w_in", got_c[0]), assemble_big("ssd_w_out", got_c[1])
    cuts1 = (0, SSD_INNER, 2 * SSD_INNER, 2 * SSD_INNER + SSD_BC, 2 * SSD_INNER + 2 * SSD_BC)
    w_in1 = [w_in1[:, cuts1[i]:cuts1[i + 1]] for i in range(4)] + [_pad_lanes(w_in1[:, cuts1[4]:])]
    xn1 = _rms_fwd(h1, full["ssd_norm"], name="rms1")
    z1, xs_in, bm_in, cm_in, dt_rows = [_mm(xn1, wi, "NN", name=f"in1_{i}") for i, wi in enumerate(w_in1)]
    ssdw = _ssd_weights(*[full[n] for n in ("ssd_conv_w", "ssd_conv_b", "ssd_dt_bias", "ssd_a_log", "ssd_d",
                                             "ssd_gnorm")])
    yn, svs = _ssd_layer_fwd(z1, xs_in, bm_in, cm_in, dt_rows, ssdw, bsz, nc)
    h2 = _mm(yn, w_out1, "NN", name="out1", add=h1)
    loss_part, dh2, dfinal = _final_loss(h2, full["final_norm"], loss_target, bsz, nc)
    loss = lax.psum(loss_part[0, 0], ("x", "y", "c"))

    g = {"final_norm": dfinal}
    dyn = _mm(dh2, w_out1, "NT", name="d_out1")
    g["ssd_w_out"] = _mm(yn, dh2, "TN", name="dw_out1", out_dtype=BF16)
    dz1, dxs, dbm, dcm, ddt, gs = _ssd_layer_bwd(dyn, z1, xs_in, bm_in, cm_in, svs, ssdw, bsz, nc)
    g.update(gs)
    dps1 = (dz1, dxs, dbm, dcm, ddt)
    dxn1 = None
    for i, (dp, wi) in enumerate(zip(dps1, w_in1)):
        dxn1 = _mm(dp, wi, "NT", name=f"d_in1_{i}", add=dxn1)
    dw1 = [_mm(xn1, dp, "TN", name=f"dw_in1_{i}", out_dtype=BF16) for i, dp in enumerate(dps1)]
    g["ssd_w_in"] = jnp.concatenate(dw1[:4] + [dw1[4][:, :SSD_HEADS]], axis=1)

    def local_shape(n):
        return w[n].shape

    def slabs(n):
        gf = g[n].reshape((1,) + tuple(g[n].shape)) if n != "meta_tokens" else g[n]
        full_shape = tuple(d * (N_DEV if i == SHARD_AXIS[n] else 1) for i, d in enumerate(local_shape(n)))
        return _split(gf.reshape(full_shape), SHARD_AXIS[n])

    x1 = _exchange_start([slabs("ssd_w_in")[:, 0], slabs("ssd_w_out")[:, 0]], ["a2a", "a2a"], name="grads_1")
    dh1, g["ssd_norm"] = _rms_bwd(h1, full["ssd_norm"], dxn1, dh2, name="rms1_bwd", dep=x1["token"])
    dya = _mm(dh1, w_out0[0], "NT", name="d_out0_a")
    dyb = _mm(dh1, w_out0[1], "NT", name="d_out0_b")
    g["ab_w_out"] = jnp.concatenate([_mm(ya, dh1, "TN", name="dw_out0_a", out_dtype=BF16),
                                     _mm(yb, dh1, "TN", name="dw_out0_b", out_dtype=BF16)], axis=0)
    du, dza, g5 = _s5_layer_bwd(dya, u, za, sv5, s5p, s5_vjp, glu_w, glu_b, bsz, nc)
    g.update(g5)
    x2 = _exchange_start([slabs("ab_w_out")[:, 0], _bf(slabs("s5_glu_w")[:, 0])], ["a2a", "a2a"], name="grads_2")
    dxb, dzb, gm = _ml_layer_bwd(dyb, xb, zb, svm, mlw, bsz, nc, dep=x2["token"])
    g.update(gm)
    dps0 = (du, dza, dxb, dzb)
    dw0 = [_mm(xn0, dp, "TN", name=f"dw_in0_{i}", out_dtype=BF16, tn=S5_WIDTH, slabs=True) for i, dp in enumerate(dps0)]
    dw_in0_slabs = jnp.concatenate(dw0, axis=0)
    x3 = _exchange_start([dw_in0_slabs], ["a2a"], name="grads_3")
    dxn0 = None
    for i, (dp, wi) in enumerate(zip(dps0, w_in0)):
        dxn0 = _mm(dp, wi, "NT", name=f"d_in0_{i}", add=dxn0, dep=x3["token"] if i == 0 else None)
    dh0, g["ab_norm"] = _rms_bwd(h0, full["ab_norm"], dxn0, dh1, name="rms0_bwd")
    dh0 = dh0.reshape(bsz, tp, D_MODEL)
    grad_x = dh0[:, CHUNK:]
    g["meta_tokens"] = jnp.sum(dh0[:, PAD_ROWS:CHUNK], axis=0)

    small_g = _pack([(slabs(n), 1) for n in SMALL])
    repl_g = _pack([(g[n], 0) for n in REPL])
    x4 = _exchange_start([small_g, repl_g], ["a2a", "ag"], name="grads_4")

    def update_big(n, gp):
        return _adamw(w[n][0], mom[n][0], var[n][0], gp, name=f"adamw_{n}")

    res = {}
    ex1 = _exchange_wait(x1, x4["token"])
    res["ssd_w_in"], res["ssd_w_out"] = update_big("ssd_w_in", ex1[0]), update_big("ssd_w_out", ex1[1])
    ex2 = _exchange_wait(x2, res["ssd_w_out"][0])
    res["ab_w_out"], res["s5_glu_w"] = update_big("ab_w_out", ex2[0]), update_big("s5_glu_w", ex2[1])
    ex3 = _exchange_wait(x3, [res[n][0] for n in ("ssd_w_in", "ssd_w_out", "ab_w_out", "s5_glu_w")])
    res["ab_w_in"] = update_big("ab_w_in", ex3[0])
    ex4 = _exchange_wait(x4, res["ab_w_in"][0])
    for names, gp, tag in ((SMALL, ex4[0], "small"), (REPL, ex4[1], "repl")):
        shapes = [local_shape(n) for n in names]
        packs = [_pack([(d[n], 0) for n in names]) for d in (w, mom, var)]
        outs = _adamw(packs[0], packs[1], packs[2], gp, name=f"adamw_{tag}")
        for k, o in enumerate(outs):
            for n, a in zip(names, _unpack(o, shapes)):
                res.setdefault(n, [None] * 4)[k] = a
    outs = [loss, grad_x]
    for k in range(4):
        outs += [res[n][k].reshape(local_shape(n)) for n in WNAMES]
    return tuple(outs)
```

```python
import functools
import math

import jax
import jax.numpy as jnp
from jax import lax
from jax.experimental import pallas as pl
from jax.experimental.pallas import tpu as pltpu

F32 = jnp.float32
BF16 = jnp.bfloat16

D_MODEL = 2048
SEQ = 2048
N_META = 16
CHUNK = 128
PAD_ROWS = CHUNK - N_META
NORM_EPS = 1e-6
HEAD_NORM_EPS = 1e-5
S5_WIDTH = 1024
S5_GROUPS = 64
S5_GROUP_SIZE = 16
S5_STATE = 64
S5_GB = 8
S5_LANES = S5_GB * S5_STATE
ML_WIDTH = 3072
ML_HEADS = 8
ML_DH = 384
ML_CONV = 4
QKV_BLOCK = 4
SSD_INNER = 4096
SSD_HEADS = 64
SSD_P = 64
SSD_N = 128
SSD_GROUPS = 8
SSD_HPG = 8
SSD_GW = SSD_HPG * SSD_P
N_DEV = 8
ADAM_LR, ADAM_B1, ADAM_B2, ADAM_EPS, ADAM_WD, ADAM_STEP = 0.001, 0.9, 0.999, 1e-08, 0.01, 10
NEG = -1e30
VMEM_CAP = 60 * 1024 * 1024
MM_BLOCK_BUDGET = 22 * 1024 * 1024
MESH = pl.DeviceIdType.MESH

NN = (((1,), (0,)), ((), ()))
NT = (((1,), (1,)), ((), ()))
TN = (((0,), (0,)), ((), ()))


def _dot(a, b, dims=NN):
    return lax.dot_general(a, b, dims, preferred_element_type=F32)


def _bf(x):
    return x.astype(BF16)


def _pick(n, cands):
    for c in cands:
        if n % c == 0:
            return c
    return n


def _nbytes(shape, dtype):
    return math.prod(shape) * jnp.dtype(dtype).itemsize


ANY_SPEC = pl.BlockSpec(memory_space=pl.ANY)


def _pc(body, *, name, grid, in_specs, out_specs, out_shape, scratch=(), vmem=None, dep=None):
    limit = None if vmem is None else int(min(VMEM_CAP, max(32 * 1024 * 1024, 2 * vmem + (8 << 20))))
    n_in = len(in_specs)
    if dep is not None:
        inner = body

        def body(*refs):
            inner(*refs[:n_in], *refs[n_in + 1:])

        in_specs = list(in_specs) + [ANY_SPEC]
    call = pl.pallas_call(
        body, name=name, grid=grid, in_specs=in_specs, out_specs=out_specs, out_shape=out_shape,
        scratch_shapes=list(scratch),
        compiler_params=pltpu.CompilerParams(dimension_semantics=("arbitrary",) * len(grid), vmem_limit_bytes=limit))
    return call if dep is None else (lambda *args: call(*args, dep))


def _silu(x):
    return x * jax.nn.sigmoid(x)


def _dsilu(x):
    s = jax.nn.sigmoid(x)
    return s * (1.0 + x * (1.0 - s))


def _gelu_and_grad(x):
    c0 = math.sqrt(2.0 / math.pi)
    inner = c0 * (x + 0.044715 * x * x * x)
    t = jnp.tanh(inner)
    g = 0.5 * x * (1.0 + t)
    dg = 0.5 * (1.0 + t) + 0.5 * x * (1.0 - t * t) * c0 * (1.0 + 3 * 0.044715 * x * x)
    return g, dg


def _mm(a, b, mode, *, name, add=None, out_dtype=F32, tn=None, slabs=False, dep=None):
    if mode == "NN":
        (m, k), (k2, n) = a.shape, b.shape
    elif mode == "NT":
        (m, k), (n, k2) = a.shape, b.shape
    else:
        (k, m), (k2, n) = a.shape, b.shape
    assert k == k2, (a.shape, b.shape, mode)
    tm = _pick(m, (1088, 1024, 768, 512, 384, 256, 128))
    tn = tn or _pick(n, (512, 384, 256, 128))

    def block_bytes(tk):
        return (_nbytes((tm, tk), a.dtype) + _nbytes((tk, tn), b.dtype) + _nbytes((tm, tn), out_dtype)
                + (_nbytes((tm, tn), F32) if add is not None else 0))

    budget = MM_BLOCK_BUDGET // 2 if mode == "TN" else MM_BLOCK_BUDGET
    tk = k if block_bytes(k) <= budget else _pick(k, (2176, 2048, 1088, 1024, 768, 512, 384, 256, 128))
    nk = k // tk
    dims = {"NN": NN, "NT": NT, "TN": TN}[mode]

    def body(*refs):
        a_ref, b_ref = refs[0], refs[1]
        add_ref = refs[2] if add is not None else None
        o_ref = refs[3] if add is not None else refs[2]

        def finish(r):
            if add_ref is not None:
                r = r + add_ref[...]
            o_ref[...] = r.reshape(o_ref.shape).astype(o_ref.dtype)

        prod = _dot(_bf(a_ref[...]), _bf(b_ref[...]), dims)
        if nk == 1:
            finish(prod)
            return
        acc_ref = refs[-1]
        kk = pl.program_id(2)

        @pl.when(kk == 0)
        def _():
            acc_ref[...] = prod

        @pl.when(kk > 0)
        def _():
            acc_ref[...] += prod

        @pl.when(kk == nk - 1)
        def _():
            finish(acc_ref[...])

    if mode == "NN":
        a_spec = pl.BlockSpec((tm, tk), lambda i, j, kk: (i, kk))
        b_spec = pl.BlockSpec((tk, tn), lambda i, j, kk: (kk, j))
    elif mode == "NT":
        a_spec = pl.BlockSpec((tm, tk), lambda i, j, kk: (i, kk))
        b_spec = pl.BlockSpec((tn, tk), lambda i, j, kk: (j, kk))
    else:
        a_spec = pl.BlockSpec((tk, tm), lambda i, j, kk: (kk, i))
        b_spec = pl.BlockSpec((tk, tn), lambda i, j, kk: (kk, j))
    in_specs = [a_spec, b_spec]
    args = [a, b]
    if add is not None:
        in_specs.append(pl.BlockSpec((tm, tn), lambda i, j, kk: (i, j)))
        args.append(add)
    if slabs:
        out_shape = jax.ShapeDtypeStruct((n // tn, m, tn), out_dtype)
        out_spec = pl.BlockSpec((1, tm, tn), lambda i, j, kk: (j, i, 0))
    else:
        out_shape = jax.ShapeDtypeStruct((m, n), out_dtype)
        out_spec = pl.BlockSpec((tm, tn), lambda i, j, kk: (i, j))
    return _pc(body, name=name, grid=(m // tm, n // tn, nk), in_specs=in_specs, out_specs=out_spec,
               out_shape=out_shape, scratch=[] if nk == 1 else [pltpu.VMEM((tm, tn), F32)],
               vmem=block_bytes(tk) + (0 if nk == 1 else _nbytes((tm, tn), F32) // 2), dep=dep)(*args)


def _rms_fwd(x, g, *, name):
    r, d = x.shape
    tm = _pick(r, (256, 128))

    def body(x_ref, g_ref, o_ref):
        xv = x_ref[...]
        rstd = lax.rsqrt(jnp.mean(xv * xv, axis=1, keepdims=True) + NORM_EPS)
        o_ref[...] = (xv * rstd * g_ref[...]).astype(o_ref.dtype)

    return _pc(body, name=name, grid=(r // tm,),
               in_specs=[pl.BlockSpec((tm, d), lambda i: (i, 0)), pl.BlockSpec((1, d), lambda i: (0, 0))],
               out_specs=pl.BlockSpec((tm, d), lambda i: (i, 0)), out_shape=jax.ShapeDtypeStruct((r, d), BF16),
               vmem=tm * d * 6)(x, g.reshape(1, d))


def _rms_bwd(x, g, dxn, dres, *, name, dep=None):
    r, d = x.shape
    tm = _pick(r, (256, 128))

    def body(x_ref, g_ref, dxn_ref, dres_ref, dx_ref, dg_ref):
        @pl.when(pl.program_id(0) == 0)
        def _():
            dg_ref[...] = jnp.zeros_like(dg_ref)

        xv = x_ref[...]
        rstd = lax.rsqrt(jnp.mean(xv * xv, axis=1, keepdims=True) + NORM_EPS)
        xh = xv * rstd
        dy = dxn_ref[...]
        dg_ref[...] += jnp.sum(dy * xh, axis=0, keepdims=True)
        dyg = dy * g_ref[...]
        dx_ref[...] = dres_ref[...] + rstd * (dyg - xh * jnp.mean(dyg * xh, axis=1, keepdims=True))

    row = pl.BlockSpec((tm, d), lambda i: (i, 0))
    vec = pl.BlockSpec((1, d), lambda i: (0, 0))
    return _pc(body, name=name, grid=(r // tm,), in_specs=[row, vec, row, row], out_specs=[row, vec],
               out_shape=[jax.ShapeDtypeStruct((r, d), F32), jax.ShapeDtypeStruct((1, d), F32)],
               vmem=tm * d * 16, dep=dep)(x, g.reshape(1, d), dxn, dres)


def _final_loss(h, g, target, bsz, nc):
    d = h.shape[1]

    def body(h_ref, g_ref, t_ref, loss_ref, dh_ref, dg_ref):
        b, c = pl.program_id(0), pl.program_id(1)

        @pl.when((b == 0) & (c == 0))
        def _():
            loss_ref[...] = jnp.zeros_like(loss_ref)
            dg_ref[...] = jnp.zeros_like(dg_ref)

        @pl.when(c == 0)
        def _():
            dh_ref[...] = jnp.zeros_like(dh_ref)

        @pl.when(c > 0)
        def _():
            xv = h_ref[...]
            rstd = lax.rsqrt(jnp.mean(xv * xv, axis=1, keepdims=True) + NORM_EPS)
            xh = xv * rstd
            gv = g_ref[...]
            err = xh * gv - t_ref[0]
            loss_ref[...] += 0.5 * jnp.sum(jnp.mean(err * err, axis=1, keepdims=True))
            dy = err * (1.0 / d)
            dg_ref[...] += jnp.sum(dy * xh, axis=0, keepdims=True)
            dyg = dy * gv
            dh_ref[...] = rstd * (dyg - xh * jnp.mean(dyg * xh, axis=1, keepdims=True))

    row = pl.BlockSpec((CHUNK, d), lambda b, c: (b * nc + c, 0))
    vec = pl.BlockSpec((1, d), lambda b, c: (0, 0))
    return _pc(body, name="final_loss", grid=(bsz, nc),
               in_specs=[row, vec, pl.BlockSpec((1, CHUNK, d), lambda b, c: (b, jnp.maximum(c - 1, 0), 0))],
               out_specs=[pl.BlockSpec((8, 128), lambda b, c: (0, 0)), row, vec],
               out_shape=[jax.ShapeDtypeStruct((8, 128), F32), jax.ShapeDtypeStruct(h.shape, F32),
                          jax.ShapeDtypeStruct((1, d), F32)],
               vmem=CHUNK * d * 16)(h, g.reshape(1, d), target)


def _adamw(w, m, v, gparts, *, name):
    r, c = w.shape
    tr = _pick(r, (256, 128)) if r * c * 4 > (1 << 20) else r

    def body(w_ref, m_ref, v_ref, gp_ref, g_ref, d_ref, nm_ref, nv_ref):
        g = gp_ref[0].astype(F32)
        for j in range(1, N_DEV):
            g = g + gp_ref[j].astype(F32)
        mm = ADAM_B1 * m_ref[...] + (1.0 - ADAM_B1) * g
        vv = ADAM_B2 * v_ref[...] + (1.0 - ADAM_B2) * (g * g)
        m_hat = mm / (1.0 - ADAM_B1 ** ADAM_STEP)
        v_hat = vv / (1.0 - ADAM_B2 ** ADAM_STEP)
        g_ref[...] = g
        d_ref[...] = -ADAM_LR * (m_hat / (jnp.sqrt(v_hat) + ADAM_EPS) + ADAM_WD * w_ref[...])
        nm_ref[...] = mm
        nv_ref[...] = vv

    blk = pl.BlockSpec((tr, c), lambda i: (i, 0))
    out = jax.ShapeDtypeStruct((r, c), F32)
    return _pc(body, name=name, grid=(r // tr,),
               in_specs=[blk, blk, blk, pl.BlockSpec((N_DEV, tr, c), lambda i: (0, i, 0))],
               out_specs=[blk, blk, blk, blk], out_shape=[out, out, out, out],
               vmem=tr * c * (4 * 7 + N_DEV * jnp.dtype(gparts.dtype).itemsize))(w, m, v, gparts)


PEERS = (1, 2, 4, 6, 3, 5, 7)
HBM_SPEC = pl.BlockSpec(memory_space=pltpu.HBM)
SEM_SPEC = pl.BlockSpec(memory_space=pltpu.SEMAPHORE)
SIDE_EFFECT = pltpu.SideEffectType.DATAFLOW_SIDE_EFFECTING


def _peer(p):
    x, y, c = lax.axis_index("x"), lax.axis_index("y"), lax.axis_index("c")
    tx, ty, tc = x ^ ((p >> 2) & 1), y ^ ((p >> 1) & 1), c ^ (p & 1)
    return (tx, ty, tc), 4 * tx + 2 * ty + tc


def _place_own(a, kind, *, name):
    rows, cols = a.shape[-2:]
    tr = _pick(rows, (512, 256, 128, 64, 32, 16))
    me = (4 * lax.axis_index("x") + 2 * lax.axis_index("y") + lax.axis_index("c")).astype(jnp.int32).reshape(1)

    def body(me_ref, in_ref, out_ref):
        out_ref[...] = in_ref[...].reshape(out_ref.shape)

    if kind == "a2a":
        in_spec = pl.BlockSpec((1, tr, cols), lambda i, me_ref: (me_ref[0], i, 0))
    else:
        in_spec = pl.BlockSpec((tr, cols), lambda i, me_ref: (i, 0))
    return pl.pallas_call(
        body, name=name, out_shape=jax.ShapeDtypeStruct((N_DEV, rows, cols), a.dtype),
        grid_spec=pltpu.PrefetchScalarGridSpec(
            num_scalar_prefetch=1, grid=(rows // tr,), in_specs=[in_spec],
            out_specs=pl.BlockSpec((1, tr, cols), lambda i, me_ref: (me_ref[0], i, 0))))(me, a)


def _exchange_copies(ins, lands, send_sems, recv_sems, kinds, incoming):
    me = 4 * lax.axis_index("x") + 2 * lax.axis_index("y") + lax.axis_index("c")
    copies = []
    for i, kind in enumerate(kinds):
        for p in PEERS:
            dev, tgt = _peer(p)
            k = i * (N_DEV - 1) + p - 1
            copies.append(pltpu.make_async_remote_copy(
                src_ref=ins[i].at[tgt] if kind == "a2a" else ins[i], dst_ref=lands[i].at[tgt if incoming else me],
                send_sem=send_sems.at[k], recv_sem=recv_sems.at[k], device_id=dev, device_id_type=MESH))
    return copies


def _exchange_start(arrays, kinds, *, name, dep=None):
    n = len(arrays)
    lands = [_place_own(a, k, name=f"{name}_own{i}") for i, (a, k) in enumerate(zip(arrays, kinds))]
    extra = [] if dep is None else [dep]

    def body(*refs):
        ins, lnd = refs[:n], refs[n:2 * n]
        send_sems, recv_sems = refs[2 * n + len(extra)], refs[2 * n + len(extra) + 1]
        token = refs[-1]
        for cp in _exchange_copies(ins, lnd, send_sems, recv_sems, kinds, False):
            cp.start()
        token[...] = jnp.zeros_like(token)

    sem = pltpu.SemaphoreType.DMA((n * (N_DEV - 1),))
    outs = pl.pallas_call(
        body, name=name, in_specs=[HBM_SPEC] * (2 * n) + [ANY_SPEC] * len(extra),
        out_specs=[SEM_SPEC, SEM_SPEC] + [HBM_SPEC] * (2 * n) + [pl.BlockSpec(memory_space=pltpu.VMEM)],
        out_shape=[sem, sem] + [pltpu.HBM(a.shape, a.dtype) for a in arrays + lands]
        + [jax.ShapeDtypeStruct((8, 128), F32)],
        input_output_aliases={i: 2 + i for i in range(2 * n)},
        compiler_params=pltpu.CompilerParams(has_side_effects=SIDE_EFFECT),
    )(*[pltpu.with_memory_space_constraint(a, pltpu.HBM) for a in arrays + lands], *extra)
    return dict(send=outs[0], recv=outs[1], ins=list(outs[2:2 + n]), lands=list(outs[2 + n:2 + 2 * n]),
                token=outs[-1], kinds=kinds, name=name)


def _exchange_wait(h, after):
    n = len(h["ins"])
    kinds = h["kinds"]

    def body(*refs):
        ins, lnd = refs[:n], refs[n:2 * n]
        send_sems, recv_sems = refs[2 * n], refs[2 * n + 1]
        copies = _exchange_copies(ins, lnd, send_sems, recv_sems, kinds, True)
        for cp in copies:
            cp.wait_recv()
        for cp in copies:
            cp.wait_send()

    arrs = h["ins"] + h["lands"]
    after = list(after) if isinstance(after, (list, tuple)) else [after]
    outs = pl.pallas_call(
        body, name=h["name"] + "_wait", in_specs=[HBM_SPEC] * (2 * n) + [SEM_SPEC, SEM_SPEC] + [ANY_SPEC] * len(after),
        out_specs=[HBM_SPEC] * (2 * n), out_shape=[pltpu.HBM(a.shape, a.dtype) for a in arrs],
        input_output_aliases={i: i for i in range(2 * n)},
        compiler_params=pltpu.CompilerParams(has_side_effects=SIDE_EFFECT),
    )(*arrs, h["send"], h["recv"], *after)
    return list(outs[n:])


def _gather_two_level(x, *, name, deps=()):
    def body(*refs):
        x_ref, out_ref = refs[0], refs[1 + len(deps)]
        send_sems, recv_sems, local_sem = refs[2 + len(deps):]
        ax, ay, ac = lax.axis_index("x"), lax.axis_index("y"), lax.axis_index("c")
        me, sibling = (ax, ay, ac), (ax, ay, 1 - ac)
        chips = [(1 - ax, ay), (ax, 1 - ay), (1 - ax, 1 - ay)]

        def slab(px, py, pc):
            return out_ref.at[4 * px + 2 * py + pc]

        def copy(k, block, to, src=None):
            return pltpu.make_async_remote_copy(
                src_ref=slab(*block) if src is None else src, dst_ref=slab(*block),
                send_sem=send_sems.at[k], recv_sem=recv_sems.at[k], device_id=to, device_id_type=MESH)

        mine = pltpu.make_async_copy(x_ref, slab(*me), local_sem)
        mine.start()
        first = [copy(0, me, sibling, src=x_ref)]
        first += [copy(1 + j, me, (*chip, ac), src=x_ref) for j, chip in enumerate(chips)]
        for cp in first:
            cp.start()
        passed = [copy(4 + j, (*chip, ac), sibling) for j, chip in enumerate(chips)]
        for j, chip in enumerate(chips):
            copy(1 + j, (*chip, ac), me).wait_recv()
            passed[j].start()
        copy(0, sibling, me).wait_recv()
        for j, chip in enumerate(chips):
            copy(4 + j, (*chip, 1 - ac), me).wait_recv()
        for cp in first + passed:
            cp.wait_send()
        mine.wait()

    return pl.pallas_call(
        body, name=name, in_specs=[HBM_SPEC] + [ANY_SPEC] * len(deps), out_specs=HBM_SPEC,
        out_shape=jax.ShapeDtypeStruct((N_DEV,) + x.shape, x.dtype),
        scratch_shapes=[pltpu.SemaphoreType.DMA((N_DEV - 1,)), pltpu.SemaphoreType.DMA((N_DEV - 1,)),
                        pltpu.SemaphoreType.DMA(())],
    )(x, *deps)


def _s5_params(lam_re, lam_im, log_dt, b_re, b_im):
    dt = jnp.exp(log_dt)[:, None]
    mag = jnp.exp(lam_re * dt)
    ar, ai = mag * jnp.cos(lam_im * dt), mag * jnp.sin(lam_im * dt)
    den = lam_re * lam_re + lam_im * lam_im
    qr = ((ar - 1.0) * lam_re + ai * lam_im) / den
    qi = (ai * lam_re - (ar - 1.0) * lam_im) / den
    bbr = qr[..., None] * b_re - qi[..., None] * b_im
    bbi = qr[..., None] * b_im + qi[..., None] * b_re
    return ar, ai, bbr, bbi


def _s5_power_table(ar, ai):
    pr, pi = ar.reshape(1, -1), ai.reshape(1, -1)
    while pr.shape[0] < 8:
        sr, si = pr[-1:], pi[-1:]
        pr, pi = (jnp.concatenate([pr, pr * sr - pi * si], axis=0), jnp.concatenate([pi, pr * si + pi * sr], axis=0))
    return pr, pi


def _blockdiag(w, rows, cols):
    w = w.reshape(S5_GB, S5_GB, rows, cols)
    eye = jnp.eye(S5_GB, dtype=w.dtype)
    return jnp.einsum("abrc,bd->abrdc", w, eye).reshape(S5_GB, S5_GB * rows, S5_GB * cols)


def _blockdiag_extract(w, rows, cols):
    w = w.reshape(S5_GB, S5_GB, rows, S5_GB, cols)
    return jnp.einsum("abrbc->abrc", w).reshape(S5_GROUPS, rows, cols)


def _s5_scan_specs(bsz, nc, rev):
    def cc(c):
        return (nc - 1 - c) if rev else c

    return dict(
        u=pl.BlockSpec((bsz, CHUNK, CHUNK), lambda g, c: (0, cc(c), g)),
        x=pl.BlockSpec((bsz, CHUNK, S5_LANES), lambda g, c: (0, cc(c), g)),
        wb=pl.BlockSpec((1, CHUNK, S5_LANES), lambda g, c: (g, 0, 0)),
        wc=pl.BlockSpec((1, S5_LANES, CHUNK), lambda g, c: (g, 0, 0)),
        tab=pl.BlockSpec((8, S5_LANES), lambda g, c: (0, g)),
        step=pl.BlockSpec((8, S5_LANES), lambda g, c: (0, g)),
        d=pl.BlockSpec((1, CHUNK), lambda g, c: (0, g)),
        lane=pl.BlockSpec((1, S5_LANES), lambda g, c: (0, g)),
        xprev=pl.BlockSpec((bsz, 8, S5_LANES), lambda g, c: (0, jnp.maximum(cc(c) * (CHUNK // 8) - 1, 0), g)),
    )


def _s5_fwd(u, wbr, wbi, pr, pi, sr, si, wcr, wci, d, bsz, nc):
    r = u.shape[0]
    tp = r // bsz
    sp = _s5_scan_specs(bsz, nc, False)

    def body(u_all, wbr_ref, wbi_ref, pr_ref, pi_ref, sr_ref, si_ref, wcr_ref, wci_ref, d_ref,
             xr_all, xi_all, y1_all, g_all, cr_sall, ci_sall):
        @pl.when(pl.program_id(1) == 0)
        def _():
            cr_sall[...] = jnp.zeros_like(cr_sall)
            ci_sall[...] = jnp.zeros_like(ci_sall)

        for bi in range(bsz):
            one(u_all.at[bi], wbr_ref, wbi_ref, pr_ref, pi_ref, sr_ref, si_ref, wcr_ref, wci_ref, d_ref,
                xr_all.at[bi], xi_all.at[bi], y1_all.at[bi], g_all.at[bi], cr_sall.at[bi], ci_sall.at[bi])

    def one(u_ref, wbr_ref, wbi_ref, pr_ref, pi_ref, sr_ref, si_ref, wcr_ref, wci_ref, d_ref,
            xr_ref, xi_ref, y1_ref, g_ref, cr_s, ci_s):
        uv = u_ref[...]
        ub = _bf(uv)
        xr, xi = _dot(ub, wbr_ref[0]), _dot(ub, wbi_ref[0])
        sub = lax.broadcasted_iota(jnp.int32, (CHUNK, S5_LANES), 0) % 8
        for k in range(3):
            s = 1 << k
            ar, ai = sr_ref[k:k + 1, :], si_ref[k:k + 1, :]
            hr = jnp.where(sub >= s, pltpu.roll(xr, s, 0), 0.0)
            hi = jnp.where(sub >= s, pltpu.roll(xi, s, 0), 0.0)
            xr, xi = xr + (ar * hr - ai * hi), xi + (ar * hi + ai * hr)
        cr, ci = cr_s[...], ci_s[...]
        tr, ti = pr_ref[...], pi_ref[...]
        outr, outi = [], []
        for g8 in range(CHUNK // 8):
            br, bi = xr[8 * g8:8 * g8 + 8, :], xi[8 * g8:8 * g8 + 8, :]
            br, bi = br + (tr * cr - ti * ci), bi + (tr * ci + ti * cr)
            cr, ci = br[7:8, :], bi[7:8, :]
            outr.append(br)
            outi.append(bi)
        xr, xi = jnp.concatenate(outr, axis=0), jnp.concatenate(outi, axis=0)
        cr_s[...] = cr
        ci_s[...] = ci
        xr_ref[...] = xr
        xi_ref[...] = xi
        y = _dot(_bf(xr), wcr_ref[0]) - _dot(_bf(xi), wci_ref[0]) + d_ref[...] * uv
        y1_ref[...] = y
        g_ref[...] = _bf(_gelu_and_grad(y)[0])

    ns = S5_GROUPS * S5_STATE
    xr, xi, y1, g = _pc(
        body, name="s5_fwd", grid=(S5_GB, nc),
        in_specs=[sp["u"], sp["wb"], sp["wb"], sp["tab"], sp["tab"], sp["step"], sp["step"], sp["wc"], sp["wc"],
                  sp["d"]],
        out_specs=[sp["x"], sp["x"], sp["u"], sp["u"]],
        out_shape=[jax.ShapeDtypeStruct((bsz, tp, ns), F32)] * 2
        + [jax.ShapeDtypeStruct((bsz, tp, S5_WIDTH), F32), jax.ShapeDtypeStruct((bsz, tp, S5_WIDTH), BF16)],
        scratch=[pltpu.VMEM((bsz, 1, S5_LANES), F32)] * 2, vmem=8 << 20,
    )(_seq(u, bsz), wbr, wbi, pr, pi, sr, si, wcr, wci, d)
    return xr.reshape(r, ns), xi.reshape(r, ns), y1.reshape(r, S5_WIDTH), g.reshape(r, S5_WIDTH)


def _s5_post(y1, glu_pre, glu_b, z):
    r, w = y1.shape
    tm = _pick(r, (256, 128))

    def body(y_ref, p_ref, b_ref, z_ref, o_ref):
        g = _gelu_and_grad(y_ref[...])[0]
        o_ref[...] = _bf(g * jax.nn.sigmoid(p_ref[...] + b_ref[...]) * _silu(z_ref[...]))

    row = pl.BlockSpec((tm, w), lambda i: (i, 0))
    return _pc(body, name="s5_post", grid=(r // tm,), in_specs=[row, row, pl.BlockSpec((1, w), lambda i: (0, 0)), row],
               out_specs=row, out_shape=jax.ShapeDtypeStruct((r, w), BF16), vmem=tm * w * 16)(y1, glu_pre, glu_b, z)


def _s5_post_bwd(dya, y1, glu_pre, glu_b, z):
    r, w = y1.shape
    tm = _pick(r, (256, 128))

    def body(dy_ref, y_ref, p_ref, b_ref, z_ref, dz_ref, dp_ref, dg_ref, db_ref):
        @pl.when(pl.program_id(0) == 0)
        def _():
            db_ref[...] = jnp.zeros_like(db_ref)

        g = _gelu_and_grad(y_ref[...])[0]
        s = jax.nn.sigmoid(p_ref[...] + b_ref[...])
        zv = z_ref[...]
        dy = dy_ref[...]
        do = dy * _silu(zv)
        dz_ref[...] = _bf(dy * g * s * _dsilu(zv))
        dp = do * g * s * (1.0 - s)
        dp_ref[...] = _bf(dp)
        db_ref[...] += jnp.sum(dp, axis=0, keepdims=True)
        dg_ref[...] = do * s

    row = pl.BlockSpec((tm, w), lambda i: (i, 0))
    vec = pl.BlockSpec((1, w), lambda i: (0, 0))
    return _pc(body, name="s5_post_bwd", grid=(r // tm,), in_specs=[row, row, row, vec, row],
               out_specs=[row, row, row, vec],
               out_shape=[jax.ShapeDtypeStruct((r, w), BF16), jax.ShapeDtypeStruct((r, w), BF16),
                          jax.ShapeDtypeStruct((r, w), F32), jax.ShapeDtypeStruct((1, w), F32)],
               vmem=tm * w * 24)(dya, y1, glu_pre, glu_b, z)


def _s5_bwd(dg, y1, u, xr, xi, wbr, wbi, qr, qi, sr, si, wcr, wci, d, bsz, nc):
    r = u.shape[0]
    tp = r // bsz
    sp = _s5_scan_specs(bsz, nc, True)

    def body(dg_all, y1_all, u_all, xr_all, xi_all, xpr_all, xpi_all, wbr_ref, wbi_ref, qr_ref, qi_ref, sr_ref, si_ref,
             wcr_ref, wci_ref, d_ref, du_all, dd_ref, dwcr_ref, dwci_ref, dwbr_ref, dwbi_ref, dar_ref, dai_ref,
             cr_sall, ci_sall):
        c = pl.program_id(1)

        @pl.when(c == 0)
        def _():
            for ref in (dd_ref, dwcr_ref, dwci_ref, dwbr_ref, dwbi_ref, dar_ref, dai_ref, cr_sall, ci_sall):
                ref[...] = jnp.zeros_like(ref)

        for bi in range(bsz):
            one(c, dg_all.at[bi], y1_all.at[bi], u_all.at[bi], xr_all.at[bi], xi_all.at[bi], xpr_all.at[bi],
                xpi_all.at[bi], wbr_ref, wbi_ref, qr_ref, qi_ref, sr_ref, si_ref, wcr_ref, wci_ref, d_ref,
                du_all.at[bi], dd_ref, dwcr_ref, dwci_ref, dwbr_ref, dwbi_ref, dar_ref, dai_ref, cr_sall.at[bi],
                ci_sall.at[bi])

    def one(c, dg_ref, y1_ref, u_ref, xr_ref, xi_ref, xpr_ref, xpi_ref, wbr_ref, wbi_ref, qr_ref, qi_ref, sr_ref, si_ref,
            wcr_ref, wci_ref, d_ref, du_ref, dd_ref, dwcr_ref, dwci_ref, dwbr_ref, dwbi_ref, dar_ref, dai_ref,
            cr_s, ci_s):
        uv = u_ref[...]
        ub = _bf(uv)
        dy = dg_ref[...] * _gelu_and_grad(y1_ref[...])[1]
        dd_ref[...] += jnp.sum(dy * uv, axis=0, keepdims=True)
        dyb = _bf(dy)
        xr, xi = xr_ref[...], xi_ref[...]
        dwcr_ref[0] += _dot(_bf(xr), dyb, TN)
        dwci_ref[0] -= _dot(_bf(xi), dyb, TN)
        lr, li = _dot(dyb, wcr_ref[0], NT), -_dot(dyb, wci_ref[0], NT)
        row = lax.broadcasted_iota(jnp.int32, (CHUNK, S5_LANES), 0)
        sub = row % 8
        for k in range(3):
            s = 1 << k
            ar, ai = sr_ref[k:k + 1, :], si_ref[k:k + 1, :]
            hr = jnp.where(sub < 8 - s, pltpu.roll(lr, CHUNK - s, 0), 0.0)
            hi = jnp.where(sub < 8 - s, pltpu.roll(li, CHUNK - s, 0), 0.0)
            lr, li = lr + (ar * hr + ai * hi), li + (ar * hi - ai * hr)
        cr, ci = cr_s[...], ci_s[...]
        tr, ti = qr_ref[...], qi_ref[...]
        outr, outi = [], []
        for g8 in reversed(range(CHUNK // 8)):
            br, bi = lr[8 * g8:8 * g8 + 8, :], li[8 * g8:8 * g8 + 8, :]
            br, bi = br + (tr * cr + ti * ci), bi + (tr * ci - ti * cr)
            cr, ci = br[0:1, :], bi[0:1, :]
            outr.append(br)
            outi.append(bi)
        lr, li = jnp.concatenate(outr[::-1], axis=0), jnp.concatenate(outi[::-1], axis=0)
        cr_s[...] = cr
        ci_s[...] = ci
        lrb, lib = _bf(lr), _bf(li)
        du_ref[...] = _bf(_dot(lrb, wbr_ref[0], NT) + _dot(lib, wbi_ref[0], NT) + dy * d_ref[...])
        dwbr_ref[0] += _dot(ub, lrb, TN)
        dwbi_ref[0] += _dot(ub, lib, TN)
        first = c == nc - 1
        pr0 = jnp.where(first, 0.0, xpr_ref[7:8, :])
        pi0 = jnp.where(first, 0.0, xpi_ref[7:8, :])
        xpr = jnp.where(row == 0, pr0, pltpu.roll(xr, 1, 0))
        xpi = jnp.where(row == 0, pi0, pltpu.roll(xi, 1, 0))
        dar_ref[...] += jnp.sum(lr * xpr + li * xpi, axis=0, keepdims=True)
        dai_ref[...] += jnp.sum(li * xpr - lr * xpi, axis=0, keepdims=True)

    st = jax.ShapeDtypeStruct
    xr3, xi3 = _seq(xr, bsz), _seq(xi, bsz)
    outs = _pc(body, name="s5_bwd", grid=(S5_GB, nc),
               in_specs=[sp["u"], sp["u"], sp["u"], sp["x"], sp["x"], sp["xprev"], sp["xprev"], sp["wb"], sp["wb"],
                         sp["tab"], sp["tab"], sp["step"], sp["step"], sp["wc"], sp["wc"], sp["d"]],
               out_specs=[sp["u"], sp["d"], sp["wc"], sp["wc"], sp["wb"], sp["wb"], sp["lane"], sp["lane"]],
               out_shape=[st((bsz, tp, S5_WIDTH), BF16), st((1, S5_WIDTH), F32),
                          st((S5_GB, S5_LANES, CHUNK), F32), st((S5_GB, S5_LANES, CHUNK), F32),
                          st((S5_GB, CHUNK, S5_LANES), F32), st((S5_GB, CHUNK, S5_LANES), F32),
                          st((1, S5_GROUPS * S5_STATE), F32), st((1, S5_GROUPS * S5_STATE), F32)],
               scratch=[pltpu.VMEM((bsz, 1, S5_LANES), F32)] * 2, vmem=12 << 20,
               )(_seq(dg, bsz), _seq(y1, bsz), _seq(u, bsz), xr3, xi3, xr3, xi3, wbr, wbi, qr, qi, sr, si, wcr, wci, d)
    return (outs[0].reshape(r, S5_WIDTH),) + tuple(outs[1:])


def _s5_layer_fwd(u, prm, glu_w, bsz, nc):
    xr, xi, y1, g = _s5_fwd(u, prm["wbr"], prm["wbi"], prm["pr"], prm["pi"], prm["sr"], prm["si"], prm["wcr"],
                            prm["wci"], prm["d"], bsz, nc)
    glu_pre = _mm(g, glu_w(y1) if callable(glu_w) else glu_w, "NN", name="s5_glu")
    return dict(xr=xr, xi=xi, y1=y1, g=g, glu_pre=glu_pre)


def _s5_layer_bwd(dya, u, z, sv, prm, pvjp, glu_w, glu_b, bsz, nc):
    dz, dglu, dg_direct, dglu_b = _s5_post_bwd(dya, sv["y1"], sv["glu_pre"], glu_b, z)
    dg = _mm(dglu, glu_w, "NT", name="s5_dg", add=dg_direct)
    dglu_w = _mm(sv["g"], dglu, "TN", name="s5_dglu_w")
    du, dd, dwcr, dwci, dwbr, dwbi, dar, dai = _s5_bwd(
        dg, sv["y1"], u, sv["xr"], sv["xi"], prm["wbr"], prm["wbi"], prm["qr"], prm["qi"], prm["sr"], prm["si"],
        prm["wcr"], prm["wci"], prm["d"], bsz, nc)
    dbbr = jnp.swapaxes(_blockdiag_extract(dwbr, S5_GROUP_SIZE, S5_STATE), 1, 2)
    dbbi = jnp.swapaxes(_blockdiag_extract(dwbi, S5_GROUP_SIZE, S5_STATE), 1, 2)
    dlr, dli, dldt, dbr, dbi = pvjp((dar.reshape(S5_GROUPS, S5_STATE), dai.reshape(S5_GROUPS, S5_STATE), dbbr, dbbi))
    grads = dict(
        s5_lambda_re=dlr, s5_lambda_im=dli, s5_log_dt=dldt, s5_b_re=dbr, s5_b_im=dbi,
        s5_c_re=jnp.swapaxes(_blockdiag_extract(dwcr, S5_STATE, S5_GROUP_SIZE), 1, 2),
        s5_c_im=jnp.swapaxes(_blockdiag_extract(dwci, S5_STATE, S5_GROUP_SIZE), 1, 2),
        s5_d=dd, s5_glu_w=dglu_w, s5_glu_b=dglu_b)
    return du, dz, grads


def _s5_tables(lam_re, lam_im, log_dt, b_re, b_im, c_re, c_im, d):
    (ar, ai, bbr, bbi), vjp = jax.vjp(_s5_params, lam_re, lam_im, log_dt, b_re, b_im)
    pr, pi = _s5_power_table(lax.stop_gradient(ar), lax.stop_gradient(ai))
    steps = [0, 1, 3, 7, 7, 7, 7, 7]
    prm = dict(
        wbr=_bf(_blockdiag(jnp.swapaxes(bbr, 1, 2), S5_GROUP_SIZE, S5_STATE)),
        wbi=_bf(_blockdiag(jnp.swapaxes(bbi, 1, 2), S5_GROUP_SIZE, S5_STATE)),
        wcr=_bf(_blockdiag(jnp.swapaxes(c_re, 1, 2), S5_STATE, S5_GROUP_SIZE)),
        wci=_bf(_blockdiag(jnp.swapaxes(c_im, 1, 2), S5_STATE, S5_GROUP_SIZE)),
        pr=pr, pi=pi, qr=pr[::-1], qi=pi[::-1],
        sr=jnp.concatenate([pr[i:i + 1] for i in steps], axis=0),
        si=jnp.concatenate([pi[i:i + 1] for i in steps], axis=0), d=d.reshape(1, S5_WIDTH))
    return prm, vjp


def _tile16(p8):
    return jnp.concatenate([p8] * (CHUNK // 8), axis=0)


def _shift_down(x, halo, s, row):
    return jnp.where(row >= s, pltpu.roll(x, s, 0), pltpu.roll(halo, s, 0))


def _shift_up(x, halo, s, row):
    return jnp.where(row < CHUNK - s, pltpu.roll(x, CHUNK - s, 0), pltpu.roll(halo, CHUNK - s, 0))


def _conv_specs(nc, tw):
    def chunk(b, c):
        return b * nc + c

    return dict(
        x=pl.BlockSpec((CHUNK, tw), lambda j, b, c: (chunk(b, c), j)),
        prev=pl.BlockSpec((8, tw), lambda j, b, c: (jnp.maximum(chunk(b, c) * (CHUNK // 8) - 1, 0), j)),
        nxt=pl.BlockSpec((8, tw), lambda j, b, c: ((b * nc + jnp.minimum(c + 1, nc - 1)) * (CHUNK // 8), j)),
        w=pl.BlockSpec((ML_CONV, tw), lambda j, b, c: (0, j)),
        vec=pl.BlockSpec((1, tw), lambda j, b, c: (0, j)),
    )


def _conv_fwd(x, w, bias, bsz, nc, *, name):
    r, wd = x.shape
    tw = _pick(wd, (2048, 1536, 1024, 512, 384, 256, 128))
    sp = _conv_specs(nc, tw)

    def body(x_ref, p_ref, w_ref, b_ref, o_ref):
        c = pl.program_id(2)
        xv = x_ref[...]
        row = lax.broadcasted_iota(jnp.int32, xv.shape, 0)
        halo = jnp.where(c == 0, 0.0, _tile16(p_ref[...]))
        acc = b_ref[...] + w_ref[3:4, :] * xv
        for s in (1, 2, 3):
            acc = acc + w_ref[3 - s:4 - s, :] * _shift_down(xv, halo, s, row)
        o_ref[...] = acc

    return _pc(body, name=name, grid=(wd // tw, bsz, nc), in_specs=[sp["x"], sp["prev"], sp["w"], sp["vec"]],
               out_specs=sp["x"], out_shape=jax.ShapeDtypeStruct((r, wd), F32), vmem=CHUNK * tw * 16,
               )(x, x, w, bias.reshape(1, wd))


def _conv_bwd(dpre, x, w, bsz, nc, *, name, add=None):
    r, wd = x.shape
    tw = _pick(wd, (2048, 1536, 1024, 512, 384, 256, 128))
    sp = _conv_specs(nc, tw)

    def body(*refs):
        d_ref, n_ref, x_ref, p_ref, w_ref = refs[:5]
        add_ref = refs[5] if add is not None else None
        dx_ref, dw_ref, db_ref = refs[-3:]
        b, c = pl.program_id(1), pl.program_id(2)

        @pl.when((b == 0) & (c == 0))
        def _():
            dw_ref[...] = jnp.zeros_like(dw_ref)
            db_ref[...] = jnp.zeros_like(db_ref)

        dv, xv = d_ref[...], x_ref[...]
        row = lax.broadcasted_iota(jnp.int32, xv.shape, 0)
        dhalo = jnp.where(c == nc - 1, 0.0, _tile16(n_ref[...]))
        xhalo = jnp.where(c == 0, 0.0, _tile16(p_ref[...]))
        dx = w_ref[3:4, :] * dv
        for s in (1, 2, 3):
            dx = dx + w_ref[3 - s:4 - s, :] * _shift_up(dv, dhalo, s, row)
        if add_ref is not None:
            dx = dx + add_ref[...]
        dx_ref[...] = _bf(dx)
        db_ref[...] += jnp.sum(dv, axis=0, keepdims=True)
        dw_ref[3:4, :] += jnp.sum(dv * xv, axis=0, keepdims=True)
        for s in (1, 2, 3):
            dw_ref[3 - s:4 - s, :] += jnp.sum(dv * _shift_down(xv, xhalo, s, row), axis=0, keepdims=True)

    ins = [dpre, dpre, x, x, w] + ([add] if add is not None else [])
    specs = [sp["x"], sp["nxt"], sp["x"], sp["prev"], sp["w"]] + ([sp["x"]] if add is not None else [])
    return _pc(body, name=name, grid=(wd // tw, bsz, nc), in_specs=specs, out_specs=[sp["x"], sp["w"], sp["vec"]],
               out_shape=[jax.ShapeDtypeStruct((r, wd), BF16), jax.ShapeDtypeStruct((ML_CONV, wd), F32),
                          jax.ShapeDtypeStruct((1, wd), F32)], vmem=CHUNK * tw * 24)(*ins)


ML_SCALE = ML_DH ** -0.5


def _headwise_expand(w):
    tiled = jnp.tile(w.reshape(ML_HEADS, ML_DH, QKV_BLOCK), (1, 1, ML_DH // QKV_BLOCK))
    blk = jnp.arange(ML_DH) // QKV_BLOCK
    return jnp.where(blk[:, None] == blk[None, :], tiled, 0.0)


def _headwise_extract(w):
    return w[:, :, :QKV_BLOCK].reshape(ML_HEADS * ML_DH // QKV_BLOCK, QKV_BLOCK, QKV_BLOCK)


def _ml_pre(pre, x, wq, wk, wv, wgq, wgk, wgv, bsz, nc):
    r = x.shape[0]
    tr = _pick(r, (256, 128))
    hrow = pl.BlockSpec((tr, ML_DH), lambda h, i: (i, h))
    wexp = pl.BlockSpec((1, ML_DH, ML_DH), lambda h, i: (h, 0, 0))
    wg = pl.BlockSpec((ML_DH, CHUNK), lambda h, i: (h, 0))

    def body(pre_ref, x_ref, wq_ref, wk_ref, wv_ref, gq_ref, gk_ref, gv_ref, q_ref, qs_ref, k_ref, v_ref, gt_ref):
        xcb = _bf(_silu(pre_ref[...]))
        q = _dot(xcb, wq_ref[0])
        k = _dot(xcb, wk_ref[0])
        v = _dot(_bf(x_ref[...]), wv_ref[0])
        qb, kb, vb = _bf(q), _bf(k), _bf(v)
        q_ref[...] = qb
        qs_ref[...] = _bf(q * ML_SCALE)
        k_ref[...] = kb
        v_ref[...] = vb
        gt_ref[0] = _dot(qb, gq_ref[...]) + _dot(kb, gk_ref[...]) + _dot(vb, gv_ref[...])

    o = jax.ShapeDtypeStruct((r, ML_WIDTH), BF16)
    q, qs, k, v, gates8 = _pc(
        body, name="ml_pre", grid=(ML_HEADS, r // tr),
        in_specs=[hrow, hrow, wexp, wexp, wexp, wg, wg, wg],
        out_specs=[hrow, hrow, hrow, hrow, pl.BlockSpec((1, tr, CHUNK), lambda h, i: (h, i, 0))],
        out_shape=[o, o, o, o, jax.ShapeDtypeStruct((ML_HEADS, r, CHUNK), F32)], vmem=6 << 20,
    )(pre, x, wq, wk, wv, wgq, wgk, wgv)

    def sum_body(g_ref, o_ref):
        acc = g_ref[0]
        for j in range(1, ML_HEADS):
            acc = acc + g_ref[j]
        o_ref[...] = acc

    gates = _pc(sum_body, name="ml_gates_sum", grid=(r // tr,),
                in_specs=[pl.BlockSpec((ML_HEADS, tr, CHUNK), lambda i: (0, i, 0))],
                out_specs=pl.BlockSpec((tr, CHUNK), lambda i: (i, 0)),
                out_shape=jax.ShapeDtypeStruct((r, CHUNK), F32), vmem=2 << 20)(gates8)
    return q, qs, k, v, gates


def _tri(rev):
    r = lax.broadcasted_iota(jnp.int32, (CHUNK, CHUNK), 0)
    c = lax.broadcasted_iota(jnp.int32, (CHUNK, CHUNK), 1)
    return jnp.where((c >= r) if rev else (c <= r), 1.0, 0.0).astype(F32)


def _cumsum_rows(x, row, rev=False):
    for k in range(7):
        s = 1 << k
        if rev:
            x = x + jnp.where(row < CHUNK - s, pltpu.roll(x, CHUNK - s, 0), 0.0)
        else:
            x = x + jnp.where(row >= s, pltpu.roll(x, s, 0), 0.0)
    return x


def _log_sigmoid(x):
    return jnp.minimum(x, 0.0) - jnp.log(1.0 + jnp.exp(-jnp.abs(x)))


def _ml_core(gates, hd, first, m, qs, k, v, cmat, nvec):
    sq = (CHUNK, CHUNK)
    lane = lax.broadcasted_iota(jnp.int32, sq, 1)
    row = lax.broadcasted_iota(jnp.int32, sq, 0)
    igc = jnp.sum(jnp.where(lane == hd, gates, 0.0), axis=1, keepdims=True)
    fpc = jnp.sum(jnp.where(lane == hd + ML_HEADS, gates, 0.0), axis=1, keepdims=True)
    valid = jnp.logical_or(jnp.logical_not(first), row[:, :1] >= PAD_ROWS)
    igc = jnp.where(valid, igc, NEG)
    lfc = jnp.where(valid, _log_sigmoid(fpc), 0.0)
    bcb = _cumsum_rows(jnp.broadcast_to(lfc, sq), row)
    igb = jnp.broadcast_to(igc, sq)
    dm = jnp.where(lane <= row, bcb - (bcb - igb).T, NEG)
    bc = bcb[:, :1]
    inter = bc + m
    mt = jnp.maximum(inter, jnp.max(dm, axis=1, keepdims=True))
    wt = jnp.exp(dm - mt)
    wprev = jnp.exp(inter - mt)
    s0 = _dot(qs, k, NT)
    s = s0 * wt
    cb = _bf(cmat)
    qc = _dot(qs, cb)
    qf = qs.astype(F32)
    qn = jnp.sum(qf * nvec, axis=1, keepdims=True)
    num = _dot(_bf(s), v) + wprev * qc
    den = jnp.sum(s, axis=1, keepdims=True) + wprev * qn
    emt = jnp.exp(-mt)
    dd = jnp.maximum(jnp.abs(den), emt)
    blast = bcb[CHUNK - 1:CHUNK, :1]
    g = blast - bc + igc
    m_new = jnp.maximum(blast + m, jnp.max(g, axis=0, keepdims=True))
    decay = jnp.exp(blast + m - m_new)
    e = jnp.exp(g - m_new)
    kf = k.astype(F32)
    wk = e * kf
    return dict(lane=lane, row=row, fpc=fpc, valid=valid, wt=wt, wprev=wprev, s=s, cb=cb, qc=qc, qf=qf, qn=qn,
                num=num, den=den, emt=emt, dd=dd, m_new=m_new, decay=decay, e=e, kf=kf, wk=wk)


def _ml_headnorm(h):
    mu = jnp.mean(h, axis=1, keepdims=True)
    hc = h - mu
    rstd = lax.rsqrt(jnp.mean(hc * hc, axis=1, keepdims=True) + HEAD_NORM_EPS)
    return hc * rstd, rstd


def _ml_chunk_specs(nc, rev, bsz):
    def cc(c):
        return (nc - 1 - c) if rev else c

    return dict(
        hrow=pl.BlockSpec((bsz, CHUNK, ML_DH), lambda hd, c: (0, cc(c), hd)),
        gates=pl.BlockSpec((bsz, CHUNK, CHUNK), lambda hd, c: (0, cc(c), 0)),
        bias=pl.BlockSpec((1, CHUNK), lambda hd, c: (0, 0)),
        hvec=pl.BlockSpec((1, ML_DH), lambda hd, c: (0, hd)),
        cs=pl.BlockSpec((bsz, 1, ML_DH, ML_DH), lambda hd, c: (0, hd * nc + cc(c), 0, 0)),
        ns=pl.BlockSpec((bsz, 1, 1, ML_DH), lambda hd, c: (0, hd * nc + cc(c), 0, 0)),
        ms=pl.BlockSpec((bsz, 1, 1, CHUNK), lambda hd, c: (0, hd * nc + cc(c), 0, 0)),
        dgates=pl.BlockSpec((1, bsz, CHUNK, CHUNK), lambda hd, c: (hd, 0, cc(c), 0)),
    )


def _seq(a, bsz):
    return a.reshape(bsz, a.shape[0] // bsz, a.shape[1])


def _ml_chunk_fwd(qs, k, v, gates, b_gate, pre, z, nw, sk, bsz, nc):
    r = qs.shape[0]
    tp = r // bsz
    sp = _ml_chunk_specs(nc, False, bsz)

    def body(qs_all, k_all, v_all, gt_all, bg_ref, pre_all, z_all, nw_ref, sk_ref,
             h_all, yb_all, cs_all, ns_all, ms_all, c_sall, n_sall, m_sall):
        hd, c = pl.program_id(0), pl.program_id(1)

        @pl.when(c == 0)
        def _():
            c_sall[...] = jnp.zeros_like(c_sall)
            n_sall[...] = jnp.zeros_like(n_sall)
            m_sall[...] = jnp.zeros_like(m_sall)

        for bi in range(bsz):
            one(hd, c, qs_all.at[bi], k_all.at[bi], v_all.at[bi], gt_all.at[bi], bg_ref, pre_all.at[bi], z_all.at[bi],
                nw_ref, sk_ref, h_all.at[bi], yb_all.at[bi], cs_all.at[bi], ns_all.at[bi], ms_all.at[bi],
                c_sall.at[bi], n_sall.at[bi], m_sall.at[bi])

    def one(hd, c, qs_ref, k_ref, v_ref, gt_ref, bg_ref, pre_ref, z_ref, nw_ref, sk_ref,
            h_ref, yb_ref, cs_ref, ns_ref, ms_ref, c_s, n_s, m_s):
        cmat, nvec, m = c_s[...], n_s[...], m_s[...]
        cs_ref[0] = cmat
        ns_ref[0] = nvec
        ms_ref[0] = jnp.broadcast_to(m, (1, CHUNK))
        v_ = v_ref[...]
        co = _ml_core(gt_ref[...] + bg_ref[...], hd, c == 0, m, qs_ref[...], k_ref[...], v_, cmat, nvec)
        h = co["num"] / co["dd"]
        h_ref[...] = h
        hn, _ = _ml_headnorm(h)
        yb_ref[...] = _bf((hn * nw_ref[...] + sk_ref[...] * _silu(pre_ref[...])) * _silu(z_ref[...]))
        c_s[...] = co["decay"] * cmat + _dot(_bf(co["wk"]), v_, TN)
        n_s[...] = co["decay"] * nvec + jnp.sum(co["wk"], axis=0, keepdims=True)
        m_s[...] = co["m_new"]

    nst = ML_HEADS * nc
    h, yb, cs, ns, ms = _pc(
        body, name="ml_chunk_fwd", grid=(ML_HEADS, nc),
        in_specs=[sp["hrow"]] * 3 + [sp["gates"], sp["bias"], sp["hrow"], sp["hrow"], sp["hvec"], sp["hvec"]],
        out_specs=[sp["hrow"], sp["hrow"], sp["cs"], sp["ns"], sp["ms"]],
        out_shape=[jax.ShapeDtypeStruct((bsz, tp, ML_WIDTH), F32), jax.ShapeDtypeStruct((bsz, tp, ML_WIDTH), BF16),
                   jax.ShapeDtypeStruct((bsz, nst, ML_DH, ML_DH), F32),
                   jax.ShapeDtypeStruct((bsz, nst, 1, ML_DH), F32), jax.ShapeDtypeStruct((bsz, nst, 1, CHUNK), F32)],
        scratch=[pltpu.VMEM((bsz, ML_DH, ML_DH), F32), pltpu.VMEM((bsz, 1, ML_DH), F32),
                 pltpu.VMEM((bsz, 1, 1), F32)],
        vmem=12 << 20)(*[_seq(a, bsz) for a in (qs, k, v, gates)], b_gate, _seq(pre, bsz), _seq(z, bsz), nw, sk)
    return h.reshape(r, ML_WIDTH), yb.reshape(r, ML_WIDTH), cs, ns, ms


def _ml_chunk_bwd(dyb, qs, k, v, gates, b_gate, pre, z, nw, sk, h, cs, ns, ms, bsz, nc, dep=None):
    r = qs.shape[0]
    tp = r // bsz
    sp = _ml_chunk_specs(nc, True, bsz)

    def body(dy_all, qs_all, k_all, v_all, gt_all, bg_ref, pre_all, z_all, nw_ref, sk_ref, h_all, cs_all, ns_all,
             ms_all, dq_all, dk_all, dv_all, dz_all, dxc_all, dgt_all, dnw_ref, dsk_ref, dc_sall, dn_sall):
        hd, c = pl.program_id(0), pl.program_id(1)

        @pl.when(c == 0)
        def _():
            for ref in (dnw_ref, dsk_ref, dc_sall, dn_sall):
                ref[...] = jnp.zeros_like(ref)

        for bi in range(bsz):
            one(hd, c, dy_all.at[bi], qs_all.at[bi], k_all.at[bi], v_all.at[bi], gt_all.at[bi], bg_ref,
                pre_all.at[bi], z_all.at[bi], nw_ref, sk_ref, h_all.at[bi], cs_all.at[bi], ns_all.at[bi],
                ms_all.at[bi], dq_all.at[bi], dk_all.at[bi], dv_all.at[bi], dz_all.at[bi], dxc_all.at[bi],
                dgt_all.at[0, bi], dnw_ref, dsk_ref, dc_sall.at[bi], dn_sall.at[bi])

    def one(hd, c, dy_ref, qs_ref, k_ref, v_ref, gt_ref, bg_ref, pre_ref, z_ref, nw_ref, sk_ref, h_ref, cs_ref, ns_ref,
            ms_ref, dq_ref, dk_ref, dv_ref, dz_ref, dxc_ref, dgt_ref, dnw_ref, dsk_ref, dc_s, dn_s):

        qs, k, v = qs_ref[...], k_ref[...], v_ref[...]
        cmat, nvec, m = cs_ref[0], ns_ref[0], ms_ref[0][:, :1]
        co = _ml_core(gt_ref[...] + bg_ref[...], hd, c == nc - 1, m, qs, k, v, cmat, nvec)
        lane, row = co["lane"], co["row"]
        wt, wprev, s, cb, qf = co["wt"], co["wprev"], co["s"], co["cb"], co["qf"]
        h = h_ref[...]
        hn, rstd = _ml_headnorm(h)
        xc = _silu(pre_ref[...])
        zv = z_ref[...]
        nw, sk = nw_ref[...], sk_ref[...]
        dy = dy_ref[...]
        dz_ref[...] = _bf(dy * (hn * nw + sk * xc) * _dsilu(zv))
        do = dy * _silu(zv)
        dsk_ref[...] += jnp.sum(do * xc, axis=0, keepdims=True)
        dnw_ref[...] += jnp.sum(do * hn, axis=0, keepdims=True)
        dxc_ref[...] = do * sk
        dhn = do * nw
        dh = rstd * (dhn - jnp.mean(dhn, axis=1, keepdims=True) - hn * jnp.mean(dhn * hn, axis=1, keepdims=True))
        rinv = 1.0 / co["dd"]
        dnum = dh * rinv
        ddd = -jnp.sum(dh * h, axis=1, keepdims=True) * rinv
        den = co["den"]
        dden = jnp.where(jnp.abs(den) >= co["emt"], ddd * jnp.sign(den), 0.0)
        dnb = _bf(dnum)
        ds = _dot(dnb, v, NT) + dden
        dv = _dot(_bf(s), dnb, TN)
        dnw_ = _bf(dnum * wprev)
        dwn = dden * wprev
        dqs = _dot(dnw_, cb, NT) + dwn * nvec
        dc_out = _dot(qs, dnw_, TN)
        dn_out = jnp.sum(dwn * qf, axis=0, keepdims=True)
        dwprev = jnp.sum(dnum * co["qc"], axis=1, keepdims=True) + dden * co["qn"]
        ds0 = _bf(ds * wt)
        ddm = ds * s
        dqs = dqs + _dot(ds0, k)
        dk = _dot(ds0, qs, TN)
        colc = jnp.sum(ddm.T, axis=1, keepdims=True)
        dbc = dwprev * wprev + jnp.sum(ddm, axis=1, keepdims=True) - colc
        dig = colc
        dcn, dnn = dc_s[...], dn_s[...]
        dcb = _bf(dcn)
        decay, e, kf, wk = co["decay"], co["e"], co["kf"], co["wk"]
        ddecay = (jnp.sum(jnp.sum(dcn * cmat, axis=1, keepdims=True), axis=0, keepdims=True)
                  + jnp.sum(dnn * nvec, axis=1, keepdims=True))
        dwk = _dot(v, dcb, NT) + dnn
        dv = dv + _dot(_bf(wk), dcb)
        dk = dk + e * dwk
        dg = jnp.sum(dwk * kf, axis=1, keepdims=True) * e
        dblast = ddecay * decay + jnp.sum(dg, axis=0, keepdims=True)
        dbc = dbc - dg + jnp.where(row[:, :1] == CHUNK - 1, dblast, 0.0)
        dig = dig + dg
        dc_s[...] = decay * dcn + dc_out
        dn_s[...] = decay * dnn + dn_out
        dlf = _cumsum_rows(jnp.broadcast_to(dbc, (CHUNK, CHUNK)), row, rev=True)[:, :1]
        dfp = dlf * (1.0 - jax.nn.sigmoid(co["fpc"]))
        dig = jnp.where(co["valid"], dig, 0.0)
        dfp = jnp.where(co["valid"], dfp, 0.0)
        dgt_ref[...] = jnp.where(lane == hd, dig, 0.0) + jnp.where(lane == hd + ML_HEADS, dfp, 0.0)
        dq_ref[...] = _bf(dqs * ML_SCALE)
        dk_ref[...] = _bf(dk)
        dv_ref[...] = _bf(dv)

    ob = jax.ShapeDtypeStruct((bsz, tp, ML_WIDTH), BF16)
    dq, dk, dv, dz, dxc, dgt, dnw, dsk = _pc(
        body, name="ml_chunk_bwd", grid=(ML_HEADS, nc),
        in_specs=[sp["hrow"]] * 4 + [sp["gates"], sp["bias"], sp["hrow"], sp["hrow"], sp["hvec"], sp["hvec"],
                                     sp["hrow"], sp["cs"], sp["ns"], sp["ms"]],
        out_specs=[sp["hrow"]] * 5 + [sp["dgates"], sp["hvec"], sp["hvec"]],
        out_shape=[ob, ob, ob, ob, jax.ShapeDtypeStruct((bsz, tp, ML_WIDTH), F32),
                   jax.ShapeDtypeStruct((ML_HEADS, bsz, tp, CHUNK), F32),
                   jax.ShapeDtypeStruct((1, ML_WIDTH), F32), jax.ShapeDtypeStruct((1, ML_WIDTH), F32)],
        scratch=[pltpu.VMEM((bsz, ML_DH, ML_DH), F32), pltpu.VMEM((bsz, 1, ML_DH), F32)], vmem=16 << 20, dep=dep,
    )(*[_seq(a, bsz) for a in (dyb, qs, k, v, gates)], b_gate, _seq(pre, bsz), _seq(z, bsz), nw, sk, _seq(h, bsz),
      cs, ns, ms)
    return (dq.reshape(r, ML_WIDTH), dk.reshape(r, ML_WIDTH), dv.reshape(r, ML_WIDTH), dz.reshape(r, ML_WIDTH),
            dxc.reshape(r, ML_WIDTH), dgt.reshape(ML_HEADS, r, CHUNK), dnw, dsk)


def _ml_pre_bwd(dq, dk, dv, dgates, dxc_skip, pre, x, q, k, v, wq, wk, wv, wgq, wgk, wgv, bsz, nc):
    r = x.shape[0]
    tr = _pick(r, (256, 128))
    nt = r // tr
    hrow = pl.BlockSpec((tr, ML_DH), lambda h, i: (i, h))
    wexp = pl.BlockSpec((1, ML_DH, ML_DH), lambda h, i: (h, 0, 0))
    wcmp = pl.BlockSpec((1, ML_DH, CHUNK), lambda h, i: (h, 0, 0))
    wg = pl.BlockSpec((ML_DH, CHUNK), lambda h, i: (h, 0))
    dgs = pl.BlockSpec((ML_HEADS, tr, CHUNK), lambda h, i: (0, i, 0))
    bgs = pl.BlockSpec((1, 1, CHUNK), lambda h, i: (h, 0, 0))

    def body(dq_ref, dk_ref, dv_ref, dg_ref, dxs_ref, pre_ref, x_ref, q_ref, k_ref, v_ref, wq_ref, wk_ref, wv_ref,
             gq_ref, gk_ref, gv_ref, dpre_ref, dxv_ref, cq_ref, ck_ref, cv_ref, dgq_ref, dgk_ref, dgv_ref, dbg_ref,
             dwq_ref, dwk_ref, dwv_ref):
        i = pl.program_id(1)

        @pl.when(i == 0)
        def _():
            for ref in (dwq_ref, dwk_ref, dwv_ref, dgq_ref, dgk_ref, dgv_ref, dbg_ref):
                ref[...] = jnp.zeros_like(ref)

        dgt = dg_ref[0]
        for j in range(1, ML_HEADS):
            dgt = dgt + dg_ref[j]
        dbg_ref[0] += jnp.sum(dgt, axis=0, keepdims=True)
        dgb = _bf(dgt)
        dqt = _bf(dq_ref[...].astype(F32) + _dot(dgb, gq_ref[...], NT))
        dkt = _bf(dk_ref[...].astype(F32) + _dot(dgb, gk_ref[...], NT))
        dvt = _bf(dv_ref[...].astype(F32) + _dot(dgb, gv_ref[...], NT))
        dgq_ref[...] += _dot(q_ref[...], dgb, TN)
        dgk_ref[...] += _dot(k_ref[...], dgb, TN)
        dgv_ref[...] += _dot(v_ref[...], dgb, TN)
        prev = pre_ref[...]
        xcb = _bf(_silu(prev))
        xb = _bf(x_ref[...])
        dwq_ref[...] += _dot(xcb, dqt, TN)
        dwk_ref[...] += _dot(xcb, dkt, TN)
        dwv_ref[...] += _dot(xb, dvt, TN)
        dxc = _dot(dqt, wq_ref[0], NT) + _dot(dkt, wk_ref[0], NT) + dxs_ref[...]
        dpre_ref[...] = dxc * _dsilu(prev)
        dxv_ref[...] = _dot(dvt, wv_ref[0], NT)

        @pl.when(i == nt - 1)
        def _():
            rr = lax.broadcasted_iota(jnp.int32, (ML_DH, ML_DH), 0)
            cc = lax.broadcasted_iota(jnp.int32, (ML_DH, ML_DH), 1)
            diag = rr // QKV_BLOCK == cc // QKV_BLOCK
            fc = lax.broadcasted_iota(jnp.int32, (ML_DH, CHUNK), 0)
            fo = lax.broadcasted_iota(jnp.int32, (ML_DH, CHUNK), 1)
            fold = jnp.where(fc % QKV_BLOCK == fo, 1.0, 0.0).astype(F32)
            for src, dst in ((dwq_ref, cq_ref), (dwk_ref, ck_ref), (dwv_ref, cv_ref)):
                dst[0] = jnp.dot(jnp.where(diag, src[...], 0.0), fold, precision=HI, preferred_element_type=F32)

    f = jax.ShapeDtypeStruct((r, ML_WIDTH), F32)
    wc = jax.ShapeDtypeStruct((ML_HEADS, ML_DH, CHUNK), F32)
    wgs = jax.ShapeDtypeStruct((ML_WIDTH, CHUNK), F32)
    return _pc(body, name="ml_pre_bwd", grid=(ML_HEADS, nt),
               in_specs=[hrow, hrow, hrow, dgs, hrow, hrow, hrow, hrow, hrow, hrow, wexp, wexp, wexp, wg, wg, wg],
               out_specs=[hrow, hrow, wcmp, wcmp, wcmp, wg, wg, wg, bgs],
               out_shape=[f, f, wc, wc, wc, wgs, wgs, wgs, jax.ShapeDtypeStruct((ML_HEADS, 1, CHUNK), F32)],
               scratch=[pltpu.VMEM((ML_DH, ML_DH), F32)] * 3,
               vmem=8 << 20)(dq, dk, dv, dgates, dxc_skip, pre, x, q, k, v, wq, wk, wv, wgq, wgk, wgv)


def _pad_lanes(w):
    return jnp.pad(w, ((0, 0), (0, CHUNK - w.shape[1])))


def _ml_weights(conv_w, conv_b, wq, wk, wv, w_gate, b_gate, norm_w, skip):
    return dict(
        conv_w=conv_w, conv_b=conv_b,
        wq=_bf(_headwise_expand(wq)), wk=_bf(_headwise_expand(wk)), wv=_bf(_headwise_expand(wv)),
        wgq=_bf(_pad_lanes(w_gate[:ML_WIDTH])), wgk=_bf(_pad_lanes(w_gate[ML_WIDTH:2 * ML_WIDTH])),
        wgv=_bf(_pad_lanes(w_gate[2 * ML_WIDTH:])), b_gate=_pad_lanes(b_gate.reshape(1, -1)),
        norm=norm_w.reshape(1, ML_WIDTH), skip=skip.reshape(1, ML_WIDTH))


def _ml_layer_fwd(x, z, w, bsz, nc):
    pre = _conv_fwd(x, w["conv_w"], w["conv_b"], bsz, nc, name="ml_conv")
    q, qs, k, v, gates = _ml_pre(pre, x, w["wq"], w["wk"], w["wv"], w["wgq"], w["wgk"], w["wgv"], bsz, nc)
    h, yb, cs, ns, ms = _ml_chunk_fwd(qs, k, v, gates, w["b_gate"], pre, z, w["norm"], w["skip"], bsz, nc)
    return yb, dict(pre=pre, q=q, qs=qs, k=k, v=v, gates=gates, h=h, cs=cs, ns=ns, ms=ms)


def _ml_layer_bwd(dyb, x, z, sv, w, bsz, nc, dep=None):
    dq, dk, dv, dz, dxc, dgates, dnw, dsk = _ml_chunk_bwd(
        dyb, sv["qs"], sv["k"], sv["v"], sv["gates"], w["b_gate"], sv["pre"], z, w["norm"], w["skip"], sv["h"],
        sv["cs"], sv["ns"], sv["ms"], bsz, nc, dep=dep)
    dpre, dxv, dwq, dwk, dwv, dgq, dgk, dgv, dbg = _ml_pre_bwd(
        dq, dk, dv, dgates, dxc, sv["pre"], x, sv["q"], sv["k"], sv["v"], w["wq"], w["wk"], w["wv"], w["wgq"],
        w["wgk"], w["wgv"], bsz, nc)
    dx, dcw, dcb = _conv_bwd(dpre, x, w["conv_w"], bsz, nc, name="ml_conv_bwd", add=dxv)
    ng = 2 * ML_HEADS
    grads = dict(
        ml_conv_w=dcw, ml_conv_b=dcb, ml_wq=_headwise_extract(dwq), ml_wk=_headwise_extract(dwk),
        ml_wv=_headwise_extract(dwv), ml_w_gate=jnp.concatenate([dgq[:, :ng], dgk[:, :ng], dgv[:, :ng]], axis=0),
        ml_b_gate=dbg[0][:, :ng], ml_norm=dnw, ml_skip=dsk)
    return dx, dz, grads


HI = lax.Precision.HIGHEST


def _softplus(x):
    return jnp.maximum(x, 0.0) + jnp.log(1.0 + jnp.exp(-jnp.abs(x)))


def _lane_cumsum(x, lane, rev=False):
    del lane
    return jnp.dot(x, _tri(not rev), precision=lax.Precision.HIGHEST, preferred_element_type=F32)


def _head_sum_matrix():
    r = lax.broadcasted_iota(jnp.int32, (SSD_HPG, SSD_GW), 0)
    l = lax.broadcasted_iota(jnp.int32, (SSD_HPG, SSD_GW), 1)
    return jnp.where(l // SSD_P == r, 1.0, 0.0).astype(F32)


def _ssd_core(xs, bm, cm, dt_raw, dt_bias, a_log, first):
    sq = (CHUNK, CHUNK)
    lane8 = lax.broadcasted_iota(jnp.int32, (SSD_HPG, CHUNK), 1)
    lane = lax.broadcasted_iota(jnp.int32, sq, 1)
    row = lax.broadcasted_iota(jnp.int32, sq, 0)
    low = lane < SSD_P
    valid = jnp.logical_or(jnp.logical_not(first), lane8 >= PAD_ROWS)
    pre = dt_raw + dt_bias
    dt = jnp.where(valid, _softplus(pre), 0.0)
    a = -jnp.exp(a_log)
    cum = _lane_cumsum(dt * a, lane8)
    cb = _dot(_bf(cm), _bf(bm), NT)
    heads = []
    for r in range(SSD_HPG):
        rowb = jnp.broadcast_to(cum[r:r + 1, :], sq)
        colb = rowb.T
        seg = jnp.exp(jnp.where(lane <= row, colb - rowb, NEG))
        dtrow = jnp.broadcast_to(dt[r:r + 1, :], sq)
        lastb = colb[CHUNK - 1:CHUNK, :]
        heads.append(dict(seg=seg, dtrow=dtrow, w=cb * seg * dtrow, ecol=jnp.exp(colb),
                          dec=jnp.exp(lastb - colb) * dtrow.T, elast=jnp.exp(lastb)))

    def pairs(key):
        return jnp.concatenate([jnp.where(low[:heads[0][key].shape[0]], heads[2 * j][key], heads[2 * j + 1][key])
                                for j in range(SSD_HPG // 2)], axis=1)

    return dict(lane8=lane8, low=low, valid=valid, pre=pre, dt=dt, a=a, cum=cum, cb=cb, heads=heads,
                expc=pairs("ecol"), dec=pairs("dec"), elast=pairs("elast"))


def _ssd_specs(nc, rev, bsz):
    def cc(c):
        return (nc - 1 - c) if rev else c

    return dict(
        wide=pl.BlockSpec((bsz, CHUNK, SSD_GW), lambda g, c: (0, cc(c), g)),
        narrow=pl.BlockSpec((bsz, CHUNK, SSD_N), lambda g, c: (0, cc(c), g)),
        dtT=pl.BlockSpec((bsz, SSD_HPG, CHUNK), lambda g, c: (0, g, cc(c))),
        hcol=pl.BlockSpec((SSD_HPG, 1), lambda g, c: (g, 0)),
        hacc=pl.BlockSpec((SSD_HPG, CHUNK), lambda g, c: (g, 0)),
        gvec=pl.BlockSpec((1, SSD_GW), lambda g, c: (0, g)),
        state=pl.BlockSpec((bsz, 1, SSD_N, SSD_GW), lambda g, c: (0, g * nc + cc(c), 0, 0)),
    )


def _ssd_chunk_fwd(xs_pre, bm_pre, cm_pre, dt_raw, dt_bias, a_log, d_exp, z, gnorm, bsz, nc):
    tp = xs_pre.shape[1]
    sp = _ssd_specs(nc, False, bsz)

    def body(xs_all, bm_all, cm_all, dt_all, db_ref, al_ref, d_ref, z_all, gn_ref, y_all, yn_all, st_all, st_sall):
        c = pl.program_id(1)

        @pl.when(c == 0)
        def _():
            st_sall[...] = jnp.zeros_like(st_sall)

        for bi in range(bsz):
            one(c, xs_all.at[bi], bm_all.at[bi], cm_all.at[bi], dt_all.at[bi], db_ref, al_ref, d_ref, z_all.at[bi],
                gn_ref, y_all.at[bi], yn_all.at[bi], st_all.at[bi], st_sall.at[bi])

    def one(c, xs_ref, bm_ref, cm_ref, dt_ref, db_ref, al_ref, d_ref, z_ref, gn_ref, y_ref, yn_ref, st_ref, st_s):
        state = st_s[...]
        st_ref[0] = state
        xs, bm, cm = _silu(xs_ref[...]), _silu(bm_ref[...]), _silu(cm_ref[...])
        co = _ssd_core(xs, bm, cm, dt_ref[...], db_ref[...], al_ref[...], c == 0)
        low, hd = co["low"], co["heads"]
        ys = []
        for j in range(SSD_HPG // 2):
            xp = xs[:, j * CHUNK:(j + 1) * CHUNK]
            lhs = jnp.concatenate([hd[2 * j]["w"], hd[2 * j + 1]["w"]], axis=1)
            rhs = jnp.concatenate([jnp.where(low, xp, 0.0), jnp.where(low, 0.0, xp)], axis=0)
            ys.append(_dot(_bf(lhs), _bf(rhs)))
        cmb = _bf(cm)
        y = jnp.concatenate(ys, axis=1) + co["expc"] * _dot(cmb, _bf(state)) + d_ref[...] * xs
        y_ref[...] = y
        yg = y * _silu(z_ref[...])
        rstd = lax.rsqrt(jnp.mean(yg * yg, axis=1, keepdims=True) + NORM_EPS)
        yn_ref[...] = _bf(yg * rstd * gn_ref[...])
        st_s[...] = co["elast"] * state + _dot(_bf(bm), _bf(xs * co["dec"]), TN)

    return _pc(body, name="ssd_chunk_fwd", grid=(SSD_GROUPS, nc),
               in_specs=[sp["wide"], sp["narrow"], sp["narrow"], sp["dtT"], sp["hcol"], sp["hcol"], sp["gvec"],
                         sp["wide"], sp["gvec"]],
               out_specs=[sp["wide"], sp["wide"], sp["state"]],
               out_shape=[jax.ShapeDtypeStruct((bsz, tp, SSD_INNER), F32),
                          jax.ShapeDtypeStruct((bsz, tp, SSD_INNER), BF16),
                          jax.ShapeDtypeStruct((bsz, SSD_GROUPS * nc, SSD_N, SSD_GW), F32)],
               scratch=[pltpu.VMEM((bsz, SSD_N, SSD_GW), F32)], vmem=12 << 20,
               )(xs_pre, bm_pre, cm_pre, dt_raw, dt_bias, a_log, d_exp, z, gnorm)


def _ssd_chunk_bwd(dyn, xs_pre, bm_pre, cm_pre, dt_raw, dt_bias, a_log, d_exp, z, gnorm, y, states, bsz, nc):
    tp = xs_pre.shape[1]
    sp = _ssd_specs(nc, True, bsz)

    def body(dyn_all, xs_all, bm_all, cm_all, dt_all, db_ref, al_ref, d_ref, z_all, gn_ref, y_all, st_all,
             dxs_all, dbm_all, dcm_all, dz_all, ddt_all, dgn_ref, dd_ref, dbias_ref, dal_ref, ds_sall):
        c = pl.program_id(1)

        @pl.when(c == 0)
        def _():
            for ref in (dgn_ref, dd_ref, dbias_ref, dal_ref, ds_sall):
                ref[...] = jnp.zeros_like(ref)

        for bi in range(bsz):
            one(c, dyn_all.at[bi], xs_all.at[bi], bm_all.at[bi], cm_all.at[bi], dt_all.at[bi], db_ref, al_ref, d_ref,
                z_all.at[bi], gn_ref, y_all.at[bi], st_all.at[bi], dxs_all.at[bi], dbm_all.at[bi], dcm_all.at[bi],
                dz_all.at[bi], ddt_all.at[bi], dgn_ref, dd_ref, dbias_ref, dal_ref, ds_sall.at[bi])

    def one(c, dyn_ref, xs_ref, bm_ref, cm_ref, dt_ref, db_ref, al_ref, d_ref, z_ref, gn_ref, y_ref, st_ref,
            dxs_ref, dbm_ref, dcm_ref, dz_ref, ddt_ref, dgn_ref, dd_ref, dbias_ref, dal_ref, ds_s):
        xs_p, bm_p, cm_p = xs_ref[...], bm_ref[...], cm_ref[...]
        xs, bm, cm = _silu(xs_p), _silu(bm_p), _silu(cm_p)
        state = st_ref[0]
        co = _ssd_core(xs, bm, cm, dt_ref[...], db_ref[...], al_ref[...], c == nc - 1)
        low, hd, lane8, cb = co["low"], co["heads"], co["lane8"], co["cb"]
        dt, a, cum = co["dt"], co["a"], co["cum"]
        sub8 = lax.broadcasted_iota(jnp.int32, (SSD_HPG, CHUNK), 0)
        eh = _head_sum_matrix()

        def head_rows(full):
            return lax.dot_general(eh, full, NT, precision=HI, preferred_element_type=F32)

        def head_col(vec):
            return jnp.sum(eh * vec, axis=1, keepdims=True)

        yv, zv, gn = y_ref[...], z_ref[...], gn_ref[...]
        sz = _silu(zv)
        yg = yv * sz
        rstd = lax.rsqrt(jnp.mean(yg * yg, axis=1, keepdims=True) + NORM_EPS)
        yh = yg * rstd
        dyn = dyn_ref[...]
        dgn_ref[...] += jnp.sum(dyn * yh, axis=0, keepdims=True)
        dyh = dyn * gn
        dyg = rstd * (dyh - yh * jnp.mean(dyh * yh, axis=1, keepdims=True))
        dz_ref[...] = _bf(dyg * yv * _dsilu(zv))
        dy = dyg * sz
        dxs = dy * d_ref[...]
        dd_ref[...] += head_col(jnp.sum(dy * xs, axis=0, keepdims=True))
        cmb, bmb, stb = _bf(cm), _bf(bm), _bf(state)
        ysv = _dot(cmb, stb)
        expc = co["expc"]
        dys = _bf(dy * expc)
        dcum = head_rows(dy * ysv * expc)
        dcm = _dot(dys, stb, NT)
        dstate_out = _dot(cmb, dys, TN)
        dcb = jnp.zeros((CHUNK, CHUNK), F32)
        ddt = jnp.zeros((SSD_HPG, CHUNK), F32)
        dxs_pairs = []
        for j in range(SSD_HPG // 2):
            sl = slice(j * CHUNK, (j + 1) * CHUNK)
            dyp, xp = dy[:, sl], _bf(xs[:, sl])
            lhs = _bf(jnp.concatenate([hd[2 * j]["w"], hd[2 * j + 1]["w"]], axis=1))
            both = _dot(lhs, _bf(dyp), TN)
            dxs_pairs.append(jnp.where(low, both[:CHUNK], both[CHUNK:]))
            for q, msk in ((2 * j, low), (2 * j + 1, jnp.logical_not(low))):
                h = hd[q]
                dw = _dot(_bf(jnp.where(msk, dyp, 0.0)), xp, NT)
                dcb = dcb + dw * h["seg"] * h["dtrow"]
                e_ = dw * h["w"]
                dcum_r = jnp.sum(e_.T, axis=0, keepdims=True) - jnp.sum(e_, axis=0, keepdims=True)
                ddt_r = jnp.sum(dw * cb * h["seg"], axis=0, keepdims=True)
                dcum = dcum + jnp.where(sub8 == q, dcum_r, 0.0)
                ddt = ddt + jnp.where(sub8 == q, ddt_r, 0.0)
        dxs = dxs + jnp.concatenate(dxs_pairs, axis=1)
        dcbb = _bf(dcb)
        dcm = dcm + _dot(dcbb, bmb)
        dbm = _dot(dcbb, cmb, TN)
        dsn = ds_s[...]
        dsb = _bf(dsn)
        dec = co["dec"]
        dbm = dbm + _dot(_bf(xs * dec), dsb, NT)
        dxd = _dot(bmb, dsb)
        dxs = dxs + dxd * dec
        ddec = head_rows(dxd * xs)
        last = cum[:, CHUNK - 1:CHUNK]
        erow = jnp.exp(last - cum)
        ddt = ddt + ddec * erow
        dla = ddec * erow * dt
        dlast = (jnp.sum(dla, axis=1, keepdims=True)
                 + head_col(jnp.sum(dsn * state, axis=0, keepdims=True)) * jnp.exp(last))
        dcum = dcum - dla + jnp.where(lane8 == CHUNK - 1, dlast, 0.0)
        ds_s[...] = co["elast"] * dsn + dstate_out
        dda = _lane_cumsum(dcum, lane8, rev=True)
        ddt = jnp.where(co["valid"], ddt + dda * a, 0.0)
        ddt_raw = ddt * jax.nn.sigmoid(co["pre"])
        ddt_ref[...] = ddt_raw
        dbias_ref[...] += jnp.sum(ddt_raw, axis=1, keepdims=True)
        dal_ref[...] += jnp.sum(dda * dt, axis=1, keepdims=True) * a
        dxs_ref[...] = dxs * _dsilu(xs_p)
        dbm_ref[...] = dbm * _dsilu(bm_p)
        dcm_ref[...] = dcm * _dsilu(cm_p)

    st = jax.ShapeDtypeStruct
    hacc = st((SSD_HEADS, CHUNK), F32)
    return _pc(body, name="ssd_chunk_bwd", grid=(SSD_GROUPS, nc),
               in_specs=[sp["wide"], sp["wide"], sp["narrow"], sp["narrow"], sp["dtT"], sp["hcol"], sp["hcol"],
                         sp["gvec"], sp["wide"], sp["gvec"], sp["wide"], sp["state"]],
               out_specs=[sp["wide"], sp["narrow"], sp["narrow"], sp["wide"], sp["dtT"], sp["gvec"], sp["hacc"],
                          sp["hacc"], sp["hacc"]],
               out_shape=[st((bsz, tp, SSD_INNER), F32), st((bsz, tp, SSD_BC), F32), st((bsz, tp, SSD_BC), F32),
                          st((bsz, tp, SSD_INNER), BF16), st((bsz, SSD_HEADS, tp), F32), st((1, SSD_INNER), F32),
                          hacc, hacc, hacc],
               scratch=[pltpu.VMEM((bsz, SSD_N, SSD_GW), F32)], vmem=20 << 20,
               )(dyn, xs_pre, bm_pre, cm_pre, dt_raw, dt_bias, a_log, d_exp, z, gnorm, y, states)


SSD_BC = SSD_GROUPS * SSD_N


def _ssd_weights(conv_w, conv_b, dt_bias, a_log, d, gnorm):
    cuts = (0, SSD_INNER, SSD_INNER + SSD_BC, SSD_INNER + 2 * SSD_BC)
    return dict(
        conv_w=[conv_w[:, cuts[i]:cuts[i + 1]] for i in range(3)],
        conv_b=[conv_b[cuts[i]:cuts[i + 1]] for i in range(3)],
        dt_bias=dt_bias.reshape(SSD_HEADS, 1), a_log=a_log.reshape(SSD_HEADS, 1),
        d_exp=jnp.repeat(d.reshape(SSD_HEADS), SSD_P).reshape(1, SSD_INNER), gnorm=gnorm.reshape(1, SSD_INNER))


def _ssd_layer_fwd(z, xs_in, bm_in, cm_in, dt_rows, w, bsz, nc):
    pres = [_conv_fwd(a, w["conv_w"][i], w["conv_b"][i], bsz, nc, name=f"ssd_conv{i}")
            for i, a in enumerate((xs_in, bm_in, cm_in))]
    def seq(a):
        return a.reshape(bsz, nc * CHUNK, a.shape[-1])

    dt_t = jnp.swapaxes(seq(dt_rows)[:, :, :SSD_HEADS], 1, 2)
    y, yn, states = _ssd_chunk_fwd(seq(pres[0]), seq(pres[1]), seq(pres[2]), dt_t, w["dt_bias"], w["a_log"],
                                   w["d_exp"], seq(z), w["gnorm"], bsz, nc)
    return yn.reshape(-1, SSD_INNER), dict(pres=pres, dt_t=dt_t, y=y, states=states)


def _ssd_layer_bwd(dyn, z, xs_in, bm_in, cm_in, sv, w, bsz, nc):
    pres = sv["pres"]

    def seq(a):
        return a.reshape(bsz, nc * CHUNK, a.shape[-1])

    def rows(a):
        return a.reshape(-1, a.shape[-1])

    dxs_p, dbm_p, dcm_p, dz, ddt_t, dgn, dd, dbias, dal = _ssd_chunk_bwd(
        seq(dyn), seq(pres[0]), seq(pres[1]), seq(pres[2]), sv["dt_t"], w["dt_bias"], w["a_log"], w["d_exp"], seq(z),
        w["gnorm"], sv["y"], sv["states"], bsz, nc)
    dz = rows(dz)
    outs = [_conv_bwd(rows(dp), a, w["conv_w"][i], bsz, nc, name=f"ssd_conv_bwd{i}")
            for i, (dp, a) in enumerate(((dxs_p, xs_in), (dbm_p, bm_in), (dcm_p, cm_in)))]
    ddt = _bf(_pad_lanes(rows(jnp.swapaxes(ddt_t, 1, 2))))
    grads = dict(
        ssd_conv_w=jnp.concatenate([o[1] for o in outs], axis=1),
        ssd_conv_b=jnp.concatenate([o[2] for o in outs], axis=1),
        ssd_dt_bias=dbias[:, 0], ssd_a_log=dal[:, 0], ssd_d=dd[:, 0], ssd_gnorm=dgn)
    return dz, outs[0][0], outs[1][0], outs[2][0], ddt, grads


WNAMES = ("meta_tokens", "ab_norm", "ab_w_in", "s5_lambda_re", "s5_lambda_im", "s5_log_dt", "s5_b_re", "s5_b_im",
          "s5_c_re", "s5_c_im", "s5_d", "s5_glu_w", "s5_glu_b", "ml_conv_w", "ml_conv_b", "ml_wq", "ml_wk", "ml_wv",
          "ml_w_gate", "ml_b_gate", "ml_norm", "ml_skip", "ab_w_out", "ssd_norm", "ssd_w_in", "ssd_conv_w",
          "ssd_conv_b", "ssd_dt_bias", "ssd_a_log", "ssd_d", "ssd_gnorm", "ssd_w_out", "final_norm")
SHARD_AXIS = dict(meta_tokens=1, ab_w_in=2, s5_glu_w=1, ml_conv_w=2, ml_wq=1, ml_wk=1, ml_wv=1, ml_w_gate=1,
                  ab_w_out=1, ssd_norm=1, ssd_w_in=2, ssd_conv_w=2, ssd_conv_b=1, ssd_gnorm=1, ssd_w_out=1)
BIG = ("ab_w_in", "s5_glu_w", "ab_w_out", "ssd_w_in", "ssd_w_out")
SMALL = tuple(n for n in WNAMES if n in SHARD_AXIS and n not in BIG)
REPL = tuple(n for n in WNAMES if n not in SHARD_AXIS)
PACK_ALIGN = 8 * 128


def _pack(arrs):
    lead = arrs[0][1]
    parts = []
    for a, nlead in arrs:
        f = a.reshape(a.shape[:nlead] + (-1,))
        f = jnp.pad(f, [(0, 0)] * nlead + [(0, (-f.shape[-1]) % PACK_ALIGN)])
        parts.append(f.reshape(f.shape[:nlead] + (-1, 128)))
    return jnp.concatenate(parts, axis=lead)


def _unpack(p, shapes):
    out, r0 = [], 0
    lead = p.shape[:-2]
    for s in shapes:
        n = math.prod(s)
        rows = -(-n // PACK_ALIGN) * 8
        seg = p[..., r0:r0 + rows, :].reshape(lead + (rows * 128,))[..., :n]
        out.append(seg.reshape(lead + tuple(s)))
        r0 += rows
    return out


def _assemble(g, axis):
    m = jnp.moveaxis(g, 0, axis)
    return m.reshape(m.shape[:axis] + (m.shape[axis] * m.shape[axis + 1],) + m.shape[axis + 2:])


def _split(full, axis):
    s = full.shape
    m = full.reshape(s[:axis] + (N_DEV, s[axis] // N_DEV) + s[axis + 1:])
    return jnp.moveaxis(m, axis, 0)


def kernel(x, *rest):
    nw = len(WNAMES)
    w = dict(zip(WNAMES, rest[:nw]))
    loss_target = rest[nw]
    mom = dict(zip(WNAMES, rest[nw + 1:2 * nw + 1]))
    var = dict(zip(WNAMES, rest[2 * nw + 1:3 * nw + 1]))
    bsz = x.shape[0]
    nc = 1 + SEQ // CHUNK
    tp = nc * CHUNK

    local = {n: _bf(w[n][0]) for n in BIG}
    small_local = _pack([(w[n], 0) for n in SMALL])
    gs = _exchange_start([small_local], ["ag"], name="gather_s")
    got_s = _exchange_wait(gs, gs["token"])

    def assemble_big(n, got):
        return _assemble(got[:, None], SHARD_AXIS[n])[0]

    full = {}
    for n, g in zip(SMALL, _unpack(got_s[0], [w[n].shape for n in SMALL])):
        full[n] = _assemble(g, SHARD_AXIS[n])[0] if n != "meta_tokens" else _assemble(g, SHARD_AXIS[n])
    for n in REPL:
        full[n] = w[n][0] if n != "final_norm" else w[n]
    glu_b = full["s5_glu_b"].reshape(1, S5_WIDTH)
    meta = jnp.broadcast_to(full["meta_tokens"][None], (bsz, N_META, D_MODEL))
    h0 = jnp.concatenate([jnp.zeros((bsz, PAD_ROWS, D_MODEL), F32), meta, x], axis=1).reshape(bsz * tp, D_MODEL)
    xn0 = _rms_fwd(h0, full["ab_norm"], name="rms0")
    s5p, s5_vjp = _s5_tables(*[full[n] for n in ("s5_lambda_re", "s5_lambda_im", "s5_log_dt", "s5_b_re", "s5_b_im",
                                                   "s5_c_re", "s5_c_im", "s5_d")])
    mlw = _ml_weights(*[full[n] for n in ("ml_conv_w", "ml_conv_b", "ml_wq", "ml_wk", "ml_wv", "ml_w_gate",
                                           "ml_b_gate", "ml_norm", "ml_skip")])
    got_a = [_gather_two_level(local["ab_w_in"], name="gather_a")]
    gb = _exchange_start([local["s5_glu_w"], local["ab_w_out"]], ["ag", "ag"], name="gather_b", dep=got_a[0])
    gc = _exchange_start([local["ssd_w_in"], local["ssd_w_out"]], ["ag", "ag"], name="gather_c", dep=gb["token"])
    full["ab_w_in"] = assemble_big("ab_w_in", got_a[0])
    cuts0 = (0, S5_WIDTH, 2 * S5_WIDTH, 2 * S5_WIDTH + ML_WIDTH, 2 * (S5_WIDTH + ML_WIDTH))
    w_in0 = [full["ab_w_in"][:, cuts0[i]:cuts0[i + 1]] for i in range(4)]

    u, za, xb, zb = [_mm(xn0, wi, "NN", name=f"in0_{i}") for i, wi in enumerate(w_in0)]
    got_b = []

    def glu_w_after(scan_out):
        got_b.extend(_exchange_wait(gb, scan_out))
        return assemble_big("s5_glu_w", got_b[0])

    sv5 = _s5_layer_fwd(u, s5p, glu_w_after, bsz, nc)
    glu_w = assemble_big("s5_glu_w", got_b[0])
    w_out0 = assemble_big("ab_w_out", got_b[1])
    w_out0 = [w_out0[:S5_WIDTH], w_out0[S5_WIDTH:]]
    ya = _s5_post(sv5["y1"], sv5["glu_pre"], glu_b, za)
    yb, svm = _ml_layer_fwd(xb, zb, mlw, bsz, nc)
    h1 = _mm(ya, w_out0[0], "NN", name="out0_a", add=h0)
    h1 = _mm(yb, w_out0[1], "NN", name="out0_b", add=h1)
    got_c = _exchange_wait(gc, h1)
    w_in1, w_out1 = assemble_big("ssd_w_in", got_c[0]), assemble_big("ssd_w_out", got_c[1])
    cuts1 = (0, SSD_INNER, 2 * SSD_INNER, 2 * SSD_INNER + SSD_BC, 2 * SSD_INNER + 2 * SSD_BC)
    w_in1 = [w_in1[:, cuts1[i]:cuts1[i + 1]] for i in range(4)] + [_pad_lanes(w_in1[:, cuts1[4]:])]
    xn1 = _rms_fwd(h1, full["ssd_norm"], name="rms1")
    z1, xs_in, bm_in, cm_in, dt_rows = [_mm(xn1, wi, "NN", name=f"in1_{i}") for i, wi in enumerate(w_in1)]
    ssdw = _ssd_weights(*[full[n] for n in ("ssd_conv_w", "ssd_conv_b", "ssd_dt_bias", "ssd_a_log", "ssd_d",
                                             "ssd_gnorm")])
    yn, svs = _ssd_layer_fwd(z1, xs_in, bm_in, cm_in, dt_rows, ssdw, bsz, nc)
    h2 = _mm(yn, w_out1, "NN", name="out1", add=h1)
    loss_part, dh2, dfinal = _final_loss(h2, full["final_norm"], loss_target, bsz, nc)
    loss = lax.psum(loss_part[0, 0], ("x", "y", "c"))

    g = {"final_norm": dfinal}
    dyn = _mm(dh2, w_out1, "NT", name="d_out1")
    g["ssd_w_out"] = _mm(yn, dh2, "TN", name="dw_out1", out_dtype=BF16)
    dz1, dxs, dbm, dcm, ddt, gs = _ssd_layer_bwd(dyn, z1, xs_in, bm_in, cm_in, svs, ssdw, bsz, nc)
    g.update(gs)
    dps1 = (dz1, dxs, dbm, dcm, ddt)
    dxn1 = None
    for i, (dp, wi) in enumerate(zip(dps1, w_in1)):
        dxn1 = _mm(dp, wi, "NT", name=f"d_in1_{i}", add=dxn1)
    dw1 = [_mm(xn1, dp, "TN", name=f"dw_in1_{i}", out_dtype=BF16) for i, dp in enumerate(dps1)]
    g["ssd_w_in"] = jnp.concatenate(dw1[:4] + [dw1[4][:, :SSD_HEADS]], axis=1)

    def local_shape(n):
        return w[n].shape

    def slabs(n):
        gf = g[n].reshape((1,) + tuple(g[n].shape)) if n != "meta_tokens" else g[n]
        full_shape = tuple(d * (N_DEV if i == SHARD_AXIS[n] else 1) for i, d in enumerate(local_shape(n)))
        return _split(gf.reshape(full_shape), SHARD_AXIS[n])

    x1 = _exchange_start([slabs("ssd_w_in")[:, 0], slabs("ssd_w_out")[:, 0]], ["a2a", "a2a"], name="grads_1")
    dh1, g["ssd_norm"] = _rms_bwd(h1, full["ssd_norm"], dxn1, dh2, name="rms1_bwd", dep=x1["token"])
    dya = _mm(dh1, w_out0[0], "NT", name="d_out0_a")
    dyb = _mm(dh1, w_out0[1], "NT", name="d_out0_b")
    g["ab_w_out"] = jnp.concatenate([_mm(ya, dh1, "TN", name="dw_out0_a", out_dtype=BF16),
                                     _mm(yb, dh1, "TN", name="dw_out0_b", out_dtype=BF16)], axis=0)
    du, dza, g5 = _s5_layer_bwd(dya, u, za, sv5, s5p, s5_vjp, glu_w, glu_b, bsz, nc)
    g.update(g5)
    x2 = _exchange_start([slabs("ab_w_out")[:, 0], _bf(slabs("s5_glu_w")[:, 0])], ["a2a", "a2a"], name="grads_2")
    dxb, dzb, gm = _ml_layer_bwd(dyb, xb, zb, svm, mlw, bsz, nc, dep=x2["token"])
    g.update(gm)
    dps0 = (du, dza, dxb, dzb)
    dw0 = [_mm(xn0, dp, "TN", name=f"dw_in0_{i}", out_dtype=BF16, tn=S5_WIDTH, slabs=True) for i, dp in enumerate(dps0)]
    dw_in0_slabs = jnp.concatenate(dw0, axis=0)
    x3 = _exchange_start([dw_in0_slabs], ["a2a"], name="grads_3")
    dxn0 = None
    for i, (dp, wi) in enumerate(zip(dps0, w_in0)):
        dxn0 = _mm(dp, wi, "NT", name=f"d_in0_{i}", add=dxn0, dep=x3["token"] if i == 0 else None)
    dh0, g["ab_norm"] = _rms_bwd(h0, full["ab_norm"], dxn0, dh1, name="rms0_bwd")
    dh0 = dh0.reshape(bsz, tp, D_MODEL)
    grad_x = dh0[:, CHUNK:]
    g["meta_tokens"] = jnp.sum(dh0[:, PAD_ROWS:CHUNK], axis=0)

    small_g = _pack([(slabs(n), 1) for n in SMALL])
    repl_g = _pack([(g[n], 0) for n in REPL])
    x4 = _exchange_start([small_g, repl_g], ["a2a", "ag"], name="grads_4")

    def update_big(n, gp):
        return _adamw(w[n][0], mom[n][0], var[n][0], gp, name=f"adamw_{n}")

    res = {}
    ex1 = _exchange_wait(x1, x4["token"])
    res["ssd_w_in"], res["ssd_w_out"] = update_big("ssd_w_in", ex1[0]), update_big("ssd_w_out", ex1[1])
    ex2 = _exchange_wait(x2, res["ssd_w_out"][0])
    res["ab_w_out"], res["s5_glu_w"] = update_big("ab_w_out", ex2[0]), update_big("s5_glu_w", ex2[1])
    ex3 = _exchange_wait(x3, [res[n][0] for n in ("ssd_w_in", "ssd_w_out", "ab_w_out", "s5_glu_w")])
    res["ab_w_in"] = update_big("ab_w_in", ex3[0])
    ex4 = _exchange_wait(x4, res["ab_w_in"][0])
    for names, gp, tag in ((SMALL, ex4[0], "small"), (REPL, ex4[1], "repl")):
        shapes = [local_shape(n) for n in names]
        packs = [_pack([(d[n], 0) for n in names]) for d in (w, mom, var)]
        outs = _adamw(packs[0], packs[1], packs[2], gp, name=f"adamw_{tag}")
        for k, o in enumerate(outs):
            for n, a in zip(names, _unpack(o, shapes)):
                res.setdefault(n, [None] * 4)[k] = a
    outs = [loss, grad_x]
    for k in range(4):
        outs += [res[n][k].reshape(local_shape(n)) for n in WNAMES]
    return tuple(outs)
```

```python
import functools
import math

import jax
import jax.numpy as jnp
from jax import lax
from jax.experimental import pallas as pl
from jax.experimental.pallas import tpu as pltpu

F32 = jnp.float32
BF16 = jnp.bfloat16

D_MODEL = 2048
SEQ = 2048
N_META = 16
CHUNK = 128
PAD_ROWS = CHUNK - N_META
NORM_EPS = 1e-6
HEAD_NORM_EPS = 1e-5
S5_WIDTH = 1024
S5_GROUPS = 64
S5_GROUP_SIZE = 16
S5_STATE = 64
S5_GB = 8
S5_LANES = S5_GB * S5_STATE
ML_WIDTH = 3072
ML_HEADS = 8
ML_DH = 384
ML_CONV = 4
QKV_BLOCK = 4
SSD_INNER = 4096
SSD_HEADS = 64
SSD_P = 64
SSD_N = 128
SSD_GROUPS = 8
SSD_HPG = 8
SSD_GW = SSD_HPG * SSD_P
N_DEV = 8
ADAM_LR, ADAM_B1, ADAM_B2, ADAM_EPS, ADAM_WD, ADAM_STEP = 0.001, 0.9, 0.999, 1e-08, 0.01, 10
NEG = -1e30
VMEM_CAP = 60 * 1024 * 1024
MM_BLOCK_BUDGET = 22 * 1024 * 1024
MESH = pl.DeviceIdType.MESH

NN = (((1,), (0,)), ((), ()))
NT = (((1,), (1,)), ((), ()))
TN = (((0,), (0,)), ((), ()))


def _dot(a, b, dims=NN):
    return lax.dot_general(a, b, dims, preferred_element_type=F32)


def _bf(x):
    return x.astype(BF16)


def _pick(n, cands):
    for c in cands:
        if n % c == 0:
            return c
    return n


def _nbytes(shape, dtype):
    return math.prod(shape) * jnp.dtype(dtype).itemsize


ANY_SPEC = pl.BlockSpec(memory_space=pl.ANY)


def _pc(body, *, name, grid, in_specs, out_specs, out_shape, scratch=(), vmem=None, dep=None):
    limit = None if vmem is None else int(min(VMEM_CAP, max(32 * 1024 * 1024, 2 * vmem + (8 << 20))))
    n_in = len(in_specs)
    if dep is not None:
        inner = body

        def body(*refs):
            inner(*refs[:n_in], *refs[n_in + 1:])

        in_specs = list(in_specs) + [ANY_SPEC]
    call = pl.pallas_call(
        body, name=name, grid=grid, in_specs=in_specs, out_specs=out_specs, out_shape=out_shape,
        scratch_shapes=list(scratch),
        compiler_params=pltpu.CompilerParams(dimension_semantics=("arbitrary",) * len(grid), vmem_limit_bytes=limit))
    return call if dep is None else (lambda *args: call(*args, dep))


def _silu(x):
    return x * jax.nn.sigmoid(x)


def _dsilu(x):
    s = jax.nn.sigmoid(x)
    return s * (1.0 + x * (1.0 - s))


def _gelu_and_grad(x):
    c0 = math.sqrt(2.0 / math.pi)
    inner = c0 * (x + 0.044715 * x * x * x)
    t = jnp.tanh(inner)
    g = 0.5 * x * (1.0 + t)
    dg = 0.5 * (1.0 + t) + 0.5 * x * (1.0 - t * t) * c0 * (1.0 + 3 * 0.044715 * x * x)
    return g, dg


def _mm(a, b, mode, *, name, add=None, out_dtype=F32, tn=None, slabs=False, dep=None):
    if mode == "NN":
        (m, k), (k2, n) = a.shape, b.shape
    elif mode == "NT":
        (m, k), (n, k2) = a.shape, b.shape
    else:
        (k, m), (k2, n) = a.shape, b.shape
    assert k == k2, (a.shape, b.shape, mode)
    tm = _pick(m, (1088, 1024, 768, 512, 384, 256, 128))
    tn = tn or _pick(n, (512, 384, 256, 128))

    def block_bytes(tk):
        return (_nbytes((tm, tk), a.dtype) + _nbytes((tk, tn), b.dtype) + _nbytes((tm, tn), out_dtype)
                + (_nbytes((tm, tn), F32) if add is not None else 0))

    budget = MM_BLOCK_BUDGET // 2 if mode == "TN" else MM_BLOCK_BUDGET
    tk = k if block_bytes(k) <= budget else _pick(k, (2176, 2048, 1088, 1024, 768, 512, 384, 256, 128))
    nk = k // tk
    dims = {"NN": NN, "NT": NT, "TN": TN}[mode]

    def body(*refs):
        a_ref, b_ref = refs[0], refs[1]
        add_ref = refs[2] if add is not None else None
        o_ref = refs[3] if add is not None else refs[2]

        def finish(r):
            if add_ref is not None:
                r = r + add_ref[...]
            o_ref[...] = r.reshape(o_ref.shape).astype(o_ref.dtype)

        prod = _dot(_bf(a_ref[...]), _bf(b_ref[...]), dims)
        if nk == 1:
            finish(prod)
            return
        acc_ref = refs[-1]
        kk = pl.program_id(2)

        @pl.when(kk == 0)
        def _():
            acc_ref[...] = prod

        @pl.when(kk > 0)
        def _():
            acc_ref[...] += prod

        @pl.when(kk == nk - 1)
        def _():
            finish(acc_ref[...])

    if mode == "NN":
        a_spec = pl.BlockSpec((tm, tk), lambda i, j, kk: (i, kk))
        b_spec = pl.BlockSpec((tk, tn), lambda i, j, kk: (kk, j))
    elif mode == "NT":
        a_spec = pl.BlockSpec((tm, tk), lambda i, j, kk: (i, kk))
        b_spec = pl.BlockSpec((tn, tk), lambda i, j, kk: (j, kk))
    else:
        a_spec = pl.BlockSpec((tk, tm), lambda i, j, kk: (kk, i))
        b_spec = pl.BlockSpec((tk, tn), lambda i, j, kk: (kk, j))
    in_specs = [a_spec, b_spec]
    args = [a, b]
    if add is not None:
        in_specs.append(pl.BlockSpec((tm, tn), lambda i, j, kk: (i, j)))
        args.append(add)
    if slabs:
        out_shape = jax.ShapeDtypeStruct((n // tn, m, tn), out_dtype)
        out_spec = pl.BlockSpec((1, tm, tn), lambda i, j, kk: (j, i, 0))
    else:
        out_shape = jax.ShapeDtypeStruct((m, n), out_dtype)
        out_spec = pl.BlockSpec((tm, tn), lambda i, j, kk: (i, j))
    return _pc(body, name=name, grid=(m // tm, n // tn, nk), in_specs=in_specs, out_specs=out_spec,
               out_shape=out_shape, scratch=[] if nk == 1 else [pltpu.VMEM((tm, tn), F32)],
               vmem=block_bytes(tk) + (0 if nk == 1 else _nbytes((tm, tn), F32) // 2), dep=dep)(*args)


def _rms_fwd(x, g, *, name):
    r, d = x.shape
    tm = _pick(r, (256, 128))

    def body(x_ref, g_ref, o_ref):
        xv = x_ref[...]
        rstd = lax.rsqrt(jnp.mean(xv * xv, axis=1, keepdims=True) + NORM_EPS)
        o_ref[...] = (xv * rstd * g_ref[...]).astype(o_ref.dtype)

    return _pc(body, name=name, grid=(r // tm,),
               in_specs=[pl.BlockSpec((tm, d), lambda i: (i, 0)), pl.BlockSpec((1, d), lambda i: (0, 0))],
               out_specs=pl.BlockSpec((tm, d), lambda i: (i, 0)), out_shape=jax.ShapeDtypeStruct((r, d), BF16),
               vmem=tm * d * 6)(x, g.reshape(1, d))


def _rms_bwd(x, g, dxn, dres, *, name, dep=None):
    r, d = x.shape
    tm = _pick(r, (256, 128))

    def body(x_ref, g_ref, dxn_ref, dres_ref, dx_ref, dg_ref):
        @pl.when(pl.program_id(0) == 0)
        def _():
            dg_ref[...] = jnp.zeros_like(dg_ref)

        xv = x_ref[...]
        rstd = lax.rsqrt(jnp.mean(xv * xv, axis=1, keepdims=True) + NORM_EPS)
        xh = xv * rstd
        dy = dxn_ref[...]
        dg_ref[...] += jnp.sum(dy * xh, axis=0, keepdims=True)
        dyg = dy * g_ref[...]
        dx_ref[...] = dres_ref[...] + rstd * (dyg - xh * jnp.mean(dyg * xh, axis=1, keepdims=True))

    row = pl.BlockSpec((tm, d), lambda i: (i, 0))
    vec = pl.BlockSpec((1, d), lambda i: (0, 0))
    return _pc(body, name=name, grid=(r // tm,), in_specs=[row, vec, row, row], out_specs=[row, vec],
               out_shape=[jax.ShapeDtypeStruct((r, d), F32), jax.ShapeDtypeStruct((1, d), F32)],
               vmem=tm * d * 16, dep=dep)(x, g.reshape(1, d), dxn, dres)


def _final_loss(h, g, target, bsz, nc):
    d = h.shape[1]

    def body(h_ref, g_ref, t_ref, loss_ref, dh_ref, dg_ref):
        b, c = pl.program_id(0), pl.program_id(1)

        @pl.when((b == 0) & (c == 0))
        def _():
            loss_ref[...] = jnp.zeros_like(loss_ref)
            dg_ref[...] = jnp.zeros_like(dg_ref)

        @pl.when(c == 0)
        def _():
            dh_ref[...] = jnp.zeros_like(dh_ref)

        @pl.when(c > 0)
        def _():
            xv = h_ref[...]
            rstd = lax.rsqrt(jnp.mean(xv * xv, axis=1, keepdims=True) + NORM_EPS)
            xh = xv * rstd
            gv = g_ref[...]
            err = xh * gv - t_ref[0]
            loss_ref[...] += 0.5 * jnp.sum(jnp.mean(err * err, axis=1, keepdims=True))
            dy = err * (1.0 / d)
            dg_ref[...] += jnp.sum(dy * xh, axis=0, keepdims=True)
            dyg = dy * gv
            dh_ref[...] = rstd * (dyg - xh * jnp.mean(dyg * xh, axis=1, keepdims=True))

    row = pl.BlockSpec((CHUNK, d), lambda b, c: (b * nc + c, 0))
    vec = pl.BlockSpec((1, d), lambda b, c: (0, 0))
    return _pc(body, name="final_loss", grid=(bsz, nc),
               in_specs=[row, vec, pl.BlockSpec((1, CHUNK, d), lambda b, c: (b, jnp.maximum(c - 1, 0), 0))],
               out_specs=[pl.BlockSpec((8, 128), lambda b, c: (0, 0)), row, vec],
               out_shape=[jax.ShapeDtypeStruct((8, 128), F32), jax.ShapeDtypeStruct(h.shape, F32),
                          jax.ShapeDtypeStruct((1, d), F32)],
               vmem=CHUNK * d * 16)(h, g.reshape(1, d), target)


def _adamw(w, m, v, gparts, *, name):
    r, c = w.shape
    tr = _pick(r, (256, 128)) if r * c * 4 > (1 << 20) else r

    def body(w_ref, m_ref, v_ref, gp_ref, g_ref, d_ref, nm_ref, nv_ref):
        g = gp_ref[0].astype(F32)
        for j in range(1, N_DEV):
            g = g + gp_ref[j].astype(F32)
        mm = ADAM_B1 * m_ref[...] + (1.0 - ADAM_B1) * g
        vv = ADAM_B2 * v_ref[...] + (1.0 - ADAM_B2) * (g * g)
        m_hat = mm / (1.0 - ADAM_B1 ** ADAM_STEP)
        v_hat = vv / (1.0 - ADAM_B2 ** ADAM_STEP)
        g_ref[...] = g
        d_ref[...] = -ADAM_LR * (m_hat / (jnp.sqrt(v_hat) + ADAM_EPS) + ADAM_WD * w_ref[...])
        nm_ref[...] = mm
        nv_ref[...] = vv

    blk = pl.BlockSpec((tr, c), lambda i: (i, 0))
    out = jax.ShapeDtypeStruct((r, c), F32)
    return _pc(body, name=name, grid=(r // tr,),
               in_specs=[blk, blk, blk, pl.BlockSpec((N_DEV, tr, c), lambda i: (0, i, 0))],
               out_specs=[blk, blk, blk, blk], out_shape=[out, out, out, out],
               vmem=tr * c * (4 * 7 + N_DEV * jnp.dtype(gparts.dtype).itemsize))(w, m, v, gparts)


PEERS = (1, 2, 4, 6, 3, 5, 7)
HBM_SPEC = pl.BlockSpec(memory_space=pltpu.HBM)
SEM_SPEC = pl.BlockSpec(memory_space=pltpu.SEMAPHORE)
SIDE_EFFECT = pltpu.SideEffectType.DATAFLOW_SIDE_EFFECTING


def _peer(p):
    x, y, c = lax.axis_index("x"), lax.axis_index("y"), lax.axis_index("c")
    tx, ty, tc = x ^ ((p >> 2) & 1), y ^ ((p >> 1) & 1), c ^ (p & 1)
    return (tx, ty, tc), 4 * tx + 2 * ty + tc


def _place_own(a, kind, *, name):
    rows, cols = a.shape[-2:]
    tr = _pick(rows, (512, 256, 128, 64, 32, 16))
    me = (4 * lax.axis_index("x") + 2 * lax.axis_index("y") + lax.axis_index("c")).astype(jnp.int32).reshape(1)

    def body(me_ref, in_ref, out_ref):
        out_ref[...] = in_ref[...].reshape(out_ref.shape)

    if kind == "a2a":
        in_spec = pl.BlockSpec((1, tr, cols), lambda i, me_ref: (me_ref[0], i, 0))
    else:
        in_spec = pl.BlockSpec((tr, cols), lambda i, me_ref: (i, 0))
    return pl.pallas_call(
        body, name=name, out_shape=jax.ShapeDtypeStruct((N_DEV, rows, cols), a.dtype),
        grid_spec=pltpu.PrefetchScalarGridSpec(
            num_scalar_prefetch=1, grid=(rows // tr,), in_specs=[in_spec],
            out_specs=pl.BlockSpec((1, tr, cols), lambda i, me_ref: (me_ref[0], i, 0))))(me, a)


def _exchange_copies(ins, lands, send_sems, recv_sems, kinds, incoming, peers=PEERS):
    me = 4 * lax.axis_index("x") + 2 * lax.axis_index("y") + lax.axis_index("c")
    copies = []
    for i, kind in enumerate(kinds):
        for p in peers:
            dev, tgt = _peer(p)
            k = i * (N_DEV - 1) + p - 1
            copies.append(pltpu.make_async_remote_copy(
                src_ref=ins[i].at[tgt] if kind == "a2a" else ins[i], dst_ref=lands[i].at[tgt if incoming else me],
                send_sem=send_sems.at[k], recv_sem=recv_sems.at[k], device_id=dev, device_id_type=MESH))
    return copies


def _exchange_start(arrays, kinds, *, name, dep=None, peers=PEERS):
    n = len(arrays)
    lands = [_place_own(a, k, name=f"{name}_own{i}") for i, (a, k) in enumerate(zip(arrays, kinds))]
    extra = [] if dep is None else [dep]

    def body(*refs):
        ins, lnd = refs[:n], refs[n:2 * n]
        send_sems, recv_sems = refs[2 * n + len(extra)], refs[2 * n + len(extra) + 1]
        token = refs[-1]
        for cp in _exchange_copies(ins, lnd, send_sems, recv_sems, kinds, False, peers):
            cp.start()
        token[...] = jnp.zeros_like(token)

    sem = pltpu.SemaphoreType.DMA((n * (N_DEV - 1),))
    outs = pl.pallas_call(
        body, name=name, in_specs=[HBM_SPEC] * (2 * n) + [ANY_SPEC] * len(extra),
        out_specs=[SEM_SPEC, SEM_SPEC] + [HBM_SPEC] * (2 * n) + [pl.BlockSpec(memory_space=pltpu.VMEM)],
        out_shape=[sem, sem] + [pltpu.HBM(a.shape, a.dtype) for a in arrays + lands]
        + [jax.ShapeDtypeStruct((8, 128), F32)],
        input_output_aliases={i: 2 + i for i in range(2 * n)},
        compiler_params=pltpu.CompilerParams(has_side_effects=SIDE_EFFECT),
    )(*[pltpu.with_memory_space_constraint(a, pltpu.HBM) for a in arrays + lands], *extra)
    return dict(send=outs[0], recv=outs[1], ins=list(outs[2:2 + n]), lands=list(outs[2 + n:2 + 2 * n]),
                token=outs[-1], kinds=kinds, name=name, peers=peers)


def _exchange_wait(h, after):
    n = len(h["ins"])
    kinds = h["kinds"]

    def body(*refs):
        ins, lnd = refs[:n], refs[n:2 * n]
        send_sems, recv_sems = refs[2 * n], refs[2 * n + 1]
        copies = _exchange_copies(ins, lnd, send_sems, recv_sems, kinds, True, h["peers"])
        for cp in copies:
            cp.wait_recv()
        for cp in copies:
            cp.wait_send()

    arrs = h["ins"] + h["lands"]
    after = list(after) if isinstance(after, (list, tuple)) else [after]
    outs = pl.pallas_call(
        body, name=h["name"] + "_wait", in_specs=[HBM_SPEC] * (2 * n) + [SEM_SPEC, SEM_SPEC] + [ANY_SPEC] * len(after),
        out_specs=[HBM_SPEC] * (2 * n), out_shape=[pltpu.HBM(a.shape, a.dtype) for a in arrs],
        input_output_aliases={i: i for i in range(2 * n)},
        compiler_params=pltpu.CompilerParams(has_side_effects=SIDE_EFFECT),
    )(*arrs, h["send"], h["recv"], *after)
    return list(outs[n:])


SAME_CORE = (0, 2, 4, 6)


def _forward_copies(land, send_sems, recv_sems, incoming):
    me = 4 * lax.axis_index("x") + 2 * lax.axis_index("y") + lax.axis_index("c")
    dev, sibling = _peer(1)
    return [pltpu.make_async_remote_copy(
        src_ref=land.at[me ^ q], dst_ref=land.at[(sibling if incoming else me) ^ q],
        send_sem=send_sems.at[j], recv_sem=recv_sems.at[j], device_id=dev, device_id_type=MESH)
        for j, q in enumerate(SAME_CORE)]


def _sibling_forward_start(land, *, name, dep=None):
    extra = [] if dep is None else [dep]

    def body(*refs):
        land_ref, send_sems, recv_sems, token = refs[0], refs[1 + len(extra)], refs[2 + len(extra)], refs[-1]
        for cp in _forward_copies(land_ref, send_sems, recv_sems, False):
            cp.start()
        token[...] = jnp.zeros_like(token)

    sem = pltpu.SemaphoreType.DMA((len(SAME_CORE),))
    outs = pl.pallas_call(
        body, name=name, in_specs=[HBM_SPEC] + [ANY_SPEC] * len(extra),
        out_specs=[SEM_SPEC, SEM_SPEC, HBM_SPEC, pl.BlockSpec(memory_space=pltpu.VMEM)],
        out_shape=[sem, sem, pltpu.HBM(land.shape, land.dtype), jax.ShapeDtypeStruct((8, 128), F32)],
        input_output_aliases={0: 2}, compiler_params=pltpu.CompilerParams(has_side_effects=SIDE_EFFECT),
    )(pltpu.with_memory_space_constraint(land, pltpu.HBM), *extra)
    return dict(send=outs[0], recv=outs[1], land=outs[2], token=outs[3], name=name)


def _sibling_forward_wait(h, after):
    def body(*refs):
        copies = _forward_copies(refs[0], refs[1], refs[2], True)
        for cp in copies:
            cp.wait_recv()
        for cp in copies:
            cp.wait_send()

    return pl.pallas_call(
        body, name=h["name"] + "_wait", in_specs=[HBM_SPEC, SEM_SPEC, SEM_SPEC, ANY_SPEC], out_specs=HBM_SPEC,
        out_shape=pltpu.HBM(h["land"].shape, h["land"].dtype), input_output_aliases={0: 0},
        compiler_params=pltpu.CompilerParams(has_side_effects=SIDE_EFFECT),
    )(h["land"], h["send"], h["recv"], after)


def _s5_params(lam_re, lam_im, log_dt, b_re, b_im):
    dt = jnp.exp(log_dt)[:, None]
    mag = jnp.exp(lam_re * dt)
    ar, ai = mag * jnp.cos(lam_im * dt), mag * jnp.sin(lam_im * dt)
    den = lam_re * lam_re + lam_im * lam_im
    qr = ((ar - 1.0) * lam_re + ai * lam_im) / den
    qi = (ai * lam_re - (ar - 1.0) * lam_im) / den
    bbr = qr[..., None] * b_re - qi[..., None] * b_im
    bbi = qr[..., None] * b_im + qi[..., None] * b_re
    return ar, ai, bbr, bbi


def _s5_power_table(ar, ai):
    pr, pi = ar.reshape(1, -1), ai.reshape(1, -1)
    while pr.shape[0] < 8:
        sr, si = pr[-1:], pi[-1:]
        pr, pi = (jnp.concatenate([pr, pr * sr - pi * si], axis=0), jnp.concatenate([pi, pr * si + pi * sr], axis=0))
    return pr, pi


def _blockdiag(w, rows, cols):
    w = w.reshape(S5_GB, S5_GB, rows, cols)
    eye = jnp.eye(S5_GB, dtype=w.dtype)
    return jnp.einsum("abrc,bd->abrdc", w, eye).reshape(S5_GB, S5_GB * rows, S5_GB * cols)


def _blockdiag_extract(w, rows, cols):
    w = w.reshape(S5_GB, S5_GB, rows, S5_GB, cols)
    return jnp.einsum("abrbc->abrc", w).reshape(S5_GROUPS, rows, cols)


def _s5_scan_specs(bsz, nc, rev):
    def cc(c):
        return (nc - 1 - c) if rev else c

    return dict(
        u=pl.BlockSpec((bsz, CHUNK, CHUNK), lambda g, c: (0, cc(c), g)),
        x=pl.BlockSpec((bsz, CHUNK, S5_LANES), lambda g, c: (0, cc(c), g)),
        wb=pl.BlockSpec((1, CHUNK, S5_LANES), lambda g, c: (g, 0, 0)),
        wc=pl.BlockSpec((1, S5_LANES, CHUNK), lambda g, c: (g, 0, 0)),
        tab=pl.BlockSpec((8, S5_LANES), lambda g, c: (0, g)),
        step=pl.BlockSpec((8, S5_LANES), lambda g, c: (0, g)),
        d=pl.BlockSpec((1, CHUNK), lambda g, c: (0, g)),
        lane=pl.BlockSpec((1, S5_LANES), lambda g, c: (0, g)),
        xprev=pl.BlockSpec((bsz, 8, S5_LANES), lambda g, c: (0, jnp.maximum(cc(c) * (CHUNK // 8) - 1, 0), g)),
    )


def _s5_fwd(u, wbr, wbi, pr, pi, sr, si, wcr, wci, d, bsz, nc):
    r = u.shape[0]
    tp = r // bsz
    sp = _s5_scan_specs(bsz, nc, False)

    def body(u_all, wbr_ref, wbi_ref, pr_ref, pi_ref, sr_ref, si_ref, wcr_ref, wci_ref, d_ref,
             xr_all, xi_all, y1_all, g_all, cr_sall, ci_sall):
        @pl.when(pl.program_id(1) == 0)
        def _():
            cr_sall[...] = jnp.zeros_like(cr_sall)
            ci_sall[...] = jnp.zeros_like(ci_sall)

        for bi in range(bsz):
            one(u_all.at[bi], wbr_ref, wbi_ref, pr_ref, pi_ref, sr_ref, si_ref, wcr_ref, wci_ref, d_ref,
                xr_all.at[bi], xi_all.at[bi], y1_all.at[bi], g_all.at[bi], cr_sall.at[bi], ci_sall.at[bi])

    def one(u_ref, wbr_ref, wbi_ref, pr_ref, pi_ref, sr_ref, si_ref, wcr_ref, wci_ref, d_ref,
            xr_ref, xi_ref, y1_ref, g_ref, cr_s, ci_s):
        uv = u_ref[...]
        ub = _bf(uv)
        xr, xi = _dot(ub, wbr_ref[0]), _dot(ub, wbi_ref[0])
        sub = lax.broadcasted_iota(jnp.int32, (CHUNK, S5_LANES), 0) % 8
        for k in range(3):
            s = 1 << k
            ar, ai = sr_ref[k:k + 1, :], si_ref[k:k + 1, :]
            hr = jnp.where(sub >= s, pltpu.roll(xr, s, 0), 0.0)
            hi = jnp.where(sub >= s, pltpu.roll(xi, s, 0), 0.0)
            xr, xi = xr + (ar * hr - ai * hi), xi + (ar * hi + ai * hr)
        cr, ci = cr_s[...], ci_s[...]
        tr, ti = pr_ref[...], pi_ref[...]
        outr, outi = [], []
        for g8 in range(CHUNK // 8):
            br, bi = xr[8 * g8:8 * g8 + 8, :], xi[8 * g8:8 * g8 + 8, :]
            br, bi = br + (tr * cr - ti * ci), bi + (tr * ci + ti * cr)
            cr, ci = br[7:8, :], bi[7:8, :]
            outr.append(br)
            outi.append(bi)
        xr, xi = jnp.concatenate(outr, axis=0), jnp.concatenate(outi, axis=0)
        cr_s[...] = cr
        ci_s[...] = ci
        xr_ref[...] = xr
        xi_ref[...] = xi
        y = _dot(_bf(xr), wcr_ref[0]) - _dot(_bf(xi), wci_ref[0]) + d_ref[...] * uv
        y1_ref[...] = y
        g_ref[...] = _bf(_gelu_and_grad(y)[0])

    ns = S5_GROUPS * S5_STATE
    xr, xi, y1, g = _pc(
        body, name="s5_fwd", grid=(S5_GB, nc),
        in_specs=[sp["u"], sp["wb"], sp["wb"], sp["tab"], sp["tab"], sp["step"], sp["step"], sp["wc"], sp["wc"],
                  sp["d"]],
        out_specs=[sp["x"], sp["x"], sp["u"], sp["u"]],
        out_shape=[jax.ShapeDtypeStruct((bsz, tp, ns), F32)] * 2
        + [jax.ShapeDtypeStruct((bsz, tp, S5_WIDTH), F32), jax.ShapeDtypeStruct((bsz, tp, S5_WIDTH), BF16)],
        scratch=[pltpu.VMEM((bsz, 1, S5_LANES), F32)] * 2, vmem=8 << 20,
    )(_seq(u, bsz), wbr, wbi, pr, pi, sr, si, wcr, wci, d)
    return xr.reshape(r, ns), xi.reshape(r, ns), y1.reshape(r, S5_WIDTH), g.reshape(r, S5_WIDTH)


def _s5_post(y1, glu_pre, glu_b, z):
    r, w = y1.shape
    tm = _pick(r, (256, 128))

    def body(y_ref, p_ref, b_ref, z_ref, o_ref):
        g = _gelu_and_grad(y_ref[...])[0]
        o_ref[...] = _bf(g * jax.nn.sigmoid(p_ref[...] + b_ref[...]) * _silu(z_ref[...]))

    row = pl.BlockSpec((tm, w), lambda i: (i, 0))
    return _pc(body, name="s5_post", grid=(r // tm,), in_specs=[row, row, pl.BlockSpec((1, w), lambda i: (0, 0)), row],
               out_specs=row, out_shape=jax.ShapeDtypeStruct((r, w), BF16), vmem=tm * w * 16)(y1, glu_pre, glu_b, z)


def _s5_post_bwd(dya, y1, glu_pre, glu_b, z):
    r, w = y1.shape
    tm = _pick(r, (256, 128))

    def body(dy_ref, y_ref, p_ref, b_ref, z_ref, dz_ref, dp_ref, dg_ref, db_ref):
        @pl.when(pl.program_id(0) == 0)
        def _():
            db_ref[...] = jnp.zeros_like(db_ref)

        g = _gelu_and_grad(y_ref[...])[0]
        s = jax.nn.sigmoid(p_ref[...] + b_ref[...])
        zv = z_ref[...]
        dy = dy_ref[...]
        do = dy * _silu(zv)
        dz_ref[...] = _bf(dy * g * s * _dsilu(zv))
        dp = do * g * s * (1.0 - s)
        dp_ref[...] = _bf(dp)
        db_ref[...] += jnp.sum(dp, axis=0, keepdims=True)
        dg_ref[...] = do * s

    row = pl.BlockSpec((tm, w), lambda i: (i, 0))
    vec = pl.BlockSpec((1, w), lambda i: (0, 0))
    return _pc(body, name="s5_post_bwd", grid=(r // tm,), in_specs=[row, row, row, vec, row],
               out_specs=[row, row, row, vec],
               out_shape=[jax.ShapeDtypeStruct((r, w), BF16), jax.ShapeDtypeStruct((r, w), BF16),
                          jax.ShapeDtypeStruct((r, w), F32), jax.ShapeDtypeStruct((1, w), F32)],
               vmem=tm * w * 24)(dya, y1, glu_pre, glu_b, z)


def _s5_bwd(dg, y1, u, xr, xi, wbr, wbi, qr, qi, sr, si, wcr, wci, d, bsz, nc):
    r = u.shape[0]
    tp = r // bsz
    sp = _s5_scan_specs(bsz, nc, True)

    def body(dg_all, y1_all, u_all, xr_all, xi_all, xpr_all, xpi_all, wbr_ref, wbi_ref, qr_ref, qi_ref, sr_ref, si_ref,
             wcr_ref, wci_ref, d_ref, du_all, dd_ref, dwcr_ref, dwci_ref, dwbr_ref, dwbi_ref, dar_ref, dai_ref,
             cr_sall, ci_sall):
        c = pl.program_id(1)

        @pl.when(c == 0)
        def _():
            for ref in (dd_ref, dwcr_ref, dwci_ref, dwbr_ref, dwbi_ref, dar_ref, dai_ref, cr_sall, ci_sall):
                ref[...] = jnp.zeros_like(ref)

        for bi in range(bsz):
            one(c, dg_all.at[bi], y1_all.at[bi], u_all.at[bi], xr_all.at[bi], xi_all.at[bi], xpr_all.at[bi],
                xpi_all.at[bi], wbr_ref, wbi_ref, qr_ref, qi_ref, sr_ref, si_ref, wcr_ref, wci_ref, d_ref,
                du_all.at[bi], dd_ref, dwcr_ref, dwci_ref, dwbr_ref, dwbi_ref, dar_ref, dai_ref, cr_sall.at[bi],
                ci_sall.at[bi])

    def one(c, dg_ref, y1_ref, u_ref, xr_ref, xi_ref, xpr_ref, xpi_ref, wbr_ref, wbi_ref, qr_ref, qi_ref, sr_ref, si_ref,
            wcr_ref, wci_ref, d_ref, du_ref, dd_ref, dwcr_ref, dwci_ref, dwbr_ref, dwbi_ref, dar_ref, dai_ref,
            cr_s, ci_s):
        uv = u_ref[...]
        ub = _bf(uv)
        dy = dg_ref[...] * _gelu_and_grad(y1_ref[...])[1]
        dd_ref[...] += jnp.sum(dy * uv, axis=0, keepdims=True)
        dyb = _bf(dy)
        xr, xi = xr_ref[...], xi_ref[...]
        dwcr_ref[0] += _dot(_bf(xr), dyb, TN)
        dwci_ref[0] -= _dot(_bf(xi), dyb, TN)
        lr, li = _dot(dyb, wcr_ref[0], NT), -_dot(dyb, wci_ref[0], NT)
        row = lax.broadcasted_iota(jnp.int32, (CHUNK, S5_LANES), 0)
        sub = row % 8
        for k in range(3):
            s = 1 << k
            ar, ai = sr_ref[k:k + 1, :], si_ref[k:k + 1, :]
            hr = jnp.where(sub < 8 - s, pltpu.roll(lr, CHUNK - s, 0), 0.0)
            hi = jnp.where(sub < 8 - s, pltpu.roll(li, CHUNK - s, 0), 0.0)
            lr, li = lr + (ar * hr + ai * hi), li + (ar * hi - ai * hr)
        cr, ci = cr_s[...], ci_s[...]
        tr, ti = qr_ref[...], qi_ref[...]
        outr, outi = [], []
        for g8 in reversed(range(CHUNK // 8)):
            br, bi = lr[8 * g8:8 * g8 + 8, :], li[8 * g8:8 * g8 + 8, :]
            br, bi = br + (tr * cr + ti * ci), bi + (tr * ci - ti * cr)
            cr, ci = br[0:1, :], bi[0:1, :]
            outr.append(br)
            outi.append(bi)
        lr, li = jnp.concatenate(outr[::-1], axis=0), jnp.concatenate(outi[::-1], axis=0)
        cr_s[...] = cr
        ci_s[...] = ci
        lrb, lib = _bf(lr), _bf(li)
        du_ref[...] = _bf(_dot(lrb, wbr_ref[0], NT) + _dot(lib, wbi_ref[0], NT) + dy * d_ref[...])
        dwbr_ref[0] += _dot(ub, lrb, TN)
        dwbi_ref[0] += _dot(ub, lib, TN)
        first = c == nc - 1
        pr0 = jnp.where(first, 0.0, xpr_ref[7:8, :])
        pi0 = jnp.where(first, 0.0, xpi_ref[7:8, :])
        xpr = jnp.where(row == 0, pr0, pltpu.roll(xr, 1, 0))
        xpi = jnp.where(row == 0, pi0, pltpu.roll(xi, 1, 0))
        dar_ref[...] += jnp.sum(lr * xpr + li * xpi, axis=0, keepdims=True)
        dai_ref[...] += jnp.sum(li * xpr - lr * xpi, axis=0, keepdims=True)

    st = jax.ShapeDtypeStruct
    xr3, xi3 = _seq(xr, bsz), _seq(xi, bsz)
    outs = _pc(body, name="s5_bwd", grid=(S5_GB, nc),
               in_specs=[sp["u"], sp["u"], sp["u"], sp["x"], sp["x"], sp["xprev"], sp["xprev"], sp["wb"], sp["wb"],
                         sp["tab"], sp["tab"], sp["step"], sp["step"], sp["wc"], sp["wc"], sp["d"]],
               out_specs=[sp["u"], sp["d"], sp["wc"], sp["wc"], sp["wb"], sp["wb"], sp["lane"], sp["lane"]],
               out_shape=[st((bsz, tp, S5_WIDTH), BF16), st((1, S5_WIDTH), F32),
                          st((S5_GB, S5_LANES, CHUNK), F32), st((S5_GB, S5_LANES, CHUNK), F32),
                          st((S5_GB, CHUNK, S5_LANES), F32), st((S5_GB, CHUNK, S5_LANES), F32),
                          st((1, S5_GROUPS * S5_STATE), F32), st((1, S5_GROUPS * S5_STATE), F32)],
               scratch=[pltpu.VMEM((bsz, 1, S5_LANES), F32)] * 2, vmem=12 << 20,
               )(_seq(dg, bsz), _seq(y1, bsz), _seq(u, bsz), xr3, xi3, xr3, xi3, wbr, wbi, qr, qi, sr, si, wcr, wci, d)
    return (outs[0].reshape(r, S5_WIDTH),) + tuple(outs[1:])


def _s5_layer_fwd(u, prm, glu_w, bsz, nc):
    xr, xi, y1, g = _s5_fwd(u, prm["wbr"], prm["wbi"], prm["pr"], prm["pi"], prm["sr"], prm["si"], prm["wcr"],
                            prm["wci"], prm["d"], bsz, nc)
    glu_pre = _mm(g, glu_w(y1) if callable(glu_w) else glu_w, "NN", name="s5_glu")
    return dict(xr=xr, xi=xi, y1=y1, g=g, glu_pre=glu_pre)


def _s5_layer_bwd(dya, u, z, sv, prm, pvjp, glu_w, glu_b, bsz, nc):
    dz, dglu, dg_direct, dglu_b = _s5_post_bwd(dya, sv["y1"], sv["glu_pre"], glu_b, z)
    dg = _mm(dglu, glu_w, "NT", name="s5_dg", add=dg_direct)
    dglu_w = _mm(sv["g"], dglu, "TN", name="s5_dglu_w")
    du, dd, dwcr, dwci, dwbr, dwbi, dar, dai = _s5_bwd(
        dg, sv["y1"], u, sv["xr"], sv["xi"], prm["wbr"], prm["wbi"], prm["qr"], prm["qi"], prm["sr"], prm["si"],
        prm["wcr"], prm["wci"], prm["d"], bsz, nc)
    dbbr = jnp.swapaxes(_blockdiag_extract(dwbr, S5_GROUP_SIZE, S5_STATE), 1, 2)
    dbbi = jnp.swapaxes(_blockdiag_extract(dwbi, S5_GROUP_SIZE, S5_STATE), 1, 2)
    dlr, dli, dldt, dbr, dbi = pvjp((dar.reshape(S5_GROUPS, S5_STATE), dai.reshape(S5_GROUPS, S5_STATE), dbbr, dbbi))
    grads = dict(
        s5_lambda_re=dlr, s5_lambda_im=dli, s5_log_dt=dldt, s5_b_re=dbr, s5_b_im=dbi,
        s5_c_re=jnp.swapaxes(_blockdiag_extract(dwcr, S5_STATE, S5_GROUP_SIZE), 1, 2),
        s5_c_im=jnp.swapaxes(_blockdiag_extract(dwci, S5_STATE, S5_GROUP_SIZE), 1, 2),
        s5_d=dd, s5_glu_w=dglu_w, s5_glu_b=dglu_b)
    return du, dz, grads


def _s5_tables(lam_re, lam_im, log_dt, b_re, b_im, c_re, c_im, d):
    (ar, ai, bbr, bbi), vjp = jax.vjp(_s5_params, lam_re, lam_im, log_dt, b_re, b_im)
    pr, pi = _s5_power_table(lax.stop_gradient(ar), lax.stop_gradient(ai))
    steps = [0, 1, 3, 7, 7, 7, 7, 7]
    prm = dict(
        wbr=_bf(_blockdiag(jnp.swapaxes(bbr, 1, 2), S5_GROUP_SIZE, S5_STATE)),
        wbi=_bf(_blockdiag(jnp.swapaxes(bbi, 1, 2), S5_GROUP_SIZE, S5_STATE)),
        wcr=_bf(_blockdiag(jnp.swapaxes(c_re, 1, 2), S5_STATE, S5_GROUP_SIZE)),
        wci=_bf(_blockdiag(jnp.swapaxes(c_im, 1, 2), S5_STATE, S5_GROUP_SIZE)),
        pr=pr, pi=pi, qr=pr[::-1], qi=pi[::-1],
        sr=jnp.concatenate([pr[i:i + 1] for i in steps], axis=0),
        si=jnp.concatenate([pi[i:i + 1] for i in steps], axis=0), d=d.reshape(1, S5_WIDTH))
    return prm, vjp


def _tile16(p8):
    return jnp.concatenate([p8] * (CHUNK // 8), axis=0)


def _shift_down(x, halo, s, row):
    return jnp.where(row >= s, pltpu.roll(x, s, 0), pltpu.roll(halo, s, 0))


def _shift_up(x, halo, s, row):
    return jnp.where(row < CHUNK - s, pltpu.roll(x, CHUNK - s, 0), pltpu.roll(halo, CHUNK - s, 0))


def _conv_specs(nc, tw):
    def chunk(b, c):
        return b * nc + c

    return dict(
        x=pl.BlockSpec((CHUNK, tw), lambda j, b, c: (chunk(b, c), j)),
        prev=pl.BlockSpec((8, tw), lambda j, b, c: (jnp.maximum(chunk(b, c) * (CHUNK // 8) - 1, 0), j)),
        nxt=pl.BlockSpec((8, tw), lambda j, b, c: ((b * nc + jnp.minimum(c + 1, nc - 1)) * (CHUNK // 8), j)),
        w=pl.BlockSpec((ML_CONV, tw), lambda j, b, c: (0, j)),
        vec=pl.BlockSpec((1, tw), lambda j, b, c: (0, j)),
    )


def _conv_fwd(x, w, bias, bsz, nc, *, name):
    r, wd = x.shape
    tw = _pick(wd, (2048, 1536, 1024, 512, 384, 256, 128))
    sp = _conv_specs(nc, tw)

    def body(x_ref, p_ref, w_ref, b_ref, o_ref):
        c = pl.program_id(2)
        xv = x_ref[...]
        row = lax.broadcasted_iota(jnp.int32, xv.shape, 0)
        halo = jnp.where(c == 0, 0.0, _tile16(p_ref[...]))
        acc = b_ref[...] + w_ref[3:4, :] * xv
        for s in (1, 2, 3):
            acc = acc + w_ref[3 - s:4 - s, :] * _shift_down(xv, halo, s, row)
        o_ref[...] = acc

    return _pc(body, name=name, grid=(wd // tw, bsz, nc), in_specs=[sp["x"], sp["prev"], sp["w"], sp["vec"]],
               out_specs=sp["x"], out_shape=jax.ShapeDtypeStruct((r, wd), F32), vmem=CHUNK * tw * 16,
               )(x, x, w, bias.reshape(1, wd))


def _conv_bwd(dpre, x, w, bsz, nc, *, name, add=None):
    r, wd = x.shape
    tw = _pick(wd, (2048, 1536, 1024, 512, 384, 256, 128))
    sp = _conv_specs(nc, tw)

    def body(*refs):
        d_ref, n_ref, x_ref, p_ref, w_ref = refs[:5]
        add_ref = refs[5] if add is not None else None
        dx_ref, dw_ref, db_ref = refs[-3:]
        b, c = pl.program_id(1), pl.program_id(2)

        @pl.when((b == 0) & (c == 0))
        def _():
            dw_ref[...] = jnp.zeros_like(dw_ref)
            db_ref[...] = jnp.zeros_like(db_ref)

        dv, xv = d_ref[...], x_ref[...]
        row = lax.broadcasted_iota(jnp.int32, xv.shape, 0)
        dhalo = jnp.where(c == nc - 1, 0.0, _tile16(n_ref[...]))
        xhalo = jnp.where(c == 0, 0.0, _tile16(p_ref[...]))
        dx = w_ref[3:4, :] * dv
        for s in (1, 2, 3):
            dx = dx + w_ref[3 - s:4 - s, :] * _shift_up(dv, dhalo, s, row)
        if add_ref is not None:
            dx = dx + add_ref[...]
        dx_ref[...] = _bf(dx)
        db_ref[...] += jnp.sum(dv, axis=0, keepdims=True)
        dw_ref[3:4, :] += jnp.sum(dv * xv, axis=0, keepdims=True)
        for s in (1, 2, 3):
            dw_ref[3 - s:4 - s, :] += jnp.sum(dv * _shift_down(xv, xhalo, s, row), axis=0, keepdims=True)

    ins = [dpre, dpre, x, x, w] + ([add] if add is not None else [])
    specs = [sp["x"], sp["nxt"], sp["x"], sp["prev"], sp["w"]] + ([sp["x"]] if add is not None else [])
    return _pc(body, name=name, grid=(wd // tw, bsz, nc), in_specs=specs, out_specs=[sp["x"], sp["w"], sp["vec"]],
               out_shape=[jax.ShapeDtypeStruct((r, wd), BF16), jax.ShapeDtypeStruct((ML_CONV, wd), F32),
                          jax.ShapeDtypeStruct((1, wd), F32)], vmem=CHUNK * tw * 24)(*ins)


ML_SCALE = ML_DH ** -0.5


def _headwise_expand(w):
    tiled = jnp.tile(w.reshape(ML_HEADS, ML_DH, QKV_BLOCK), (1, 1, ML_DH // QKV_BLOCK))
    blk = jnp.arange(ML_DH) // QKV_BLOCK
    return jnp.where(blk[:, None] == blk[None, :], tiled, 0.0)


def _headwise_extract(w):
    return w[:, :, :QKV_BLOCK].reshape(ML_HEADS * ML_DH // QKV_BLOCK, QKV_BLOCK, QKV_BLOCK)


def _ml_pre(pre, x, wq, wk, wv, wgq, wgk, wgv, bsz, nc):
    r = x.shape[0]
    tr = _pick(r, (256, 128))
    hrow = pl.BlockSpec((tr, ML_DH), lambda h, i: (i, h))
    wexp = pl.BlockSpec((1, ML_DH, ML_DH), lambda h, i: (h, 0, 0))
    wg = pl.BlockSpec((ML_DH, CHUNK), lambda h, i: (h, 0))

    def body(pre_ref, x_ref, wq_ref, wk_ref, wv_ref, gq_ref, gk_ref, gv_ref, q_ref, qs_ref, k_ref, v_ref, gt_ref):
        xcb = _bf(_silu(pre_ref[...]))
        q = _dot(xcb, wq_ref[0])
        k = _dot(xcb, wk_ref[0])
        v = _dot(_bf(x_ref[...]), wv_ref[0])
        qb, kb, vb = _bf(q), _bf(k), _bf(v)
        q_ref[...] = qb
        qs_ref[...] = _bf(q * ML_SCALE)
        k_ref[...] = kb
        v_ref[...] = vb
        gt_ref[0] = _dot(qb, gq_ref[...]) + _dot(kb, gk_ref[...]) + _dot(vb, gv_ref[...])

    o = jax.ShapeDtypeStruct((r, ML_WIDTH), BF16)
    q, qs, k, v, gates8 = _pc(
        body, name="ml_pre", grid=(ML_HEADS, r // tr),
        in_specs=[hrow, hrow, wexp, wexp, wexp, wg, wg, wg],
        out_specs=[hrow, hrow, hrow, hrow, pl.BlockSpec((1, tr, CHUNK), lambda h, i: (h, i, 0))],
        out_shape=[o, o, o, o, jax.ShapeDtypeStruct((ML_HEADS, r, CHUNK), F32)], vmem=6 << 20,
    )(pre, x, wq, wk, wv, wgq, wgk, wgv)

    def sum_body(g_ref, o_ref):
        acc = g_ref[0]
        for j in range(1, ML_HEADS):
            acc = acc + g_ref[j]
        o_ref[...] = acc

    gates = _pc(sum_body, name="ml_gates_sum", grid=(r // tr,),
                in_specs=[pl.BlockSpec((ML_HEADS, tr, CHUNK), lambda i: (0, i, 0))],
                out_specs=pl.BlockSpec((tr, CHUNK), lambda i: (i, 0)),
                out_shape=jax.ShapeDtypeStruct((r, CHUNK), F32), vmem=2 << 20)(gates8)
    return q, qs, k, v, gates


def _tri(rev):
    r = lax.broadcasted_iota(jnp.int32, (CHUNK, CHUNK), 0)
    c = lax.broadcasted_iota(jnp.int32, (CHUNK, CHUNK), 1)
    return jnp.where((c >= r) if rev else (c <= r), 1.0, 0.0).astype(F32)


def _cumsum_rows(x, row, rev=False):
    for k in range(7):
        s = 1 << k
        if rev:
            x = x + jnp.where(row < CHUNK - s, pltpu.roll(x, CHUNK - s, 0), 0.0)
        else:
            x = x + jnp.where(row >= s, pltpu.roll(x, s, 0), 0.0)
    return x


def _log_sigmoid(x):
    return jnp.minimum(x, 0.0) - jnp.log(1.0 + jnp.exp(-jnp.abs(x)))


def _ml_core(gates, hd, first, m, qs, k, v, cmat, nvec):
    sq = (CHUNK, CHUNK)
    lane = lax.broadcasted_iota(jnp.int32, sq, 1)
    row = lax.broadcasted_iota(jnp.int32, sq, 0)
    igc = jnp.sum(jnp.where(lane == hd, gates, 0.0), axis=1, keepdims=True)
    fpc = jnp.sum(jnp.where(lane == hd + ML_HEADS, gates, 0.0), axis=1, keepdims=True)
    valid = jnp.logical_or(jnp.logical_not(first), row[:, :1] >= PAD_ROWS)
    igc = jnp.where(valid, igc, NEG)
    lfc = jnp.where(valid, _log_sigmoid(fpc), 0.0)
    bcb = _cumsum_rows(jnp.broadcast_to(lfc, sq), row)
    igb = jnp.broadcast_to(igc, sq)
    dm = jnp.where(lane <= row, bcb - (bcb - igb).T, NEG)
    bc = bcb[:, :1]
    inter = bc + m
    mt = jnp.maximum(inter, jnp.max(dm, axis=1, keepdims=True))
    wt = jnp.exp(dm - mt)
    wprev = jnp.exp(inter - mt)
    s0 = _dot(qs, k, NT)
    s = s0 * wt
    cb = _bf(cmat)
    qc = _dot(qs, cb)
    qf = qs.astype(F32)
    qn = jnp.sum(qf * nvec, axis=1, keepdims=True)
    num = _dot(_bf(s), v) + wprev * qc
    den = jnp.sum(s, axis=1, keepdims=True) + wprev * qn
    emt = jnp.exp(-mt)
    dd = jnp.maximum(jnp.abs(den), emt)
    blast = bcb[CHUNK - 1:CHUNK, :1]
    g = blast - bc + igc
    m_new = jnp.maximum(blast + m, jnp.max(g, axis=0, keepdims=True))
    decay = jnp.exp(blast + m - m_new)
    e = jnp.exp(g - m_new)
    kf = k.astype(F32)
    wk = e * kf
    return dict(lane=lane, row=row, fpc=fpc, valid=valid, wt=wt, wprev=wprev, s=s, cb=cb, qc=qc, qf=qf, qn=qn,
                num=num, den=den, emt=emt, dd=dd, m_new=m_new, decay=decay, e=e, kf=kf, wk=wk)


def _ml_headnorm(h):
    mu = jnp.mean(h, axis=1, keepdims=True)
    hc = h - mu
    rstd = lax.rsqrt(jnp.mean(hc * hc, axis=1, keepdims=True) + HEAD_NORM_EPS)
    return hc * rstd, rstd


def _ml_chunk_specs(nc, rev, bsz):
    def cc(c):
        return (nc - 1 - c) if rev else c

    return dict(
        hrow=pl.BlockSpec((bsz, CHUNK, ML_DH), lambda hd, c: (0, cc(c), hd)),
        gates=pl.BlockSpec((bsz, CHUNK, CHUNK), lambda hd, c: (0, cc(c), 0)),
        bias=pl.BlockSpec((1, CHUNK), lambda hd, c: (0, 0)),
        hvec=pl.BlockSpec((1, ML_DH), lambda hd, c: (0, hd)),
        cs=pl.BlockSpec((bsz, 1, ML_DH, ML_DH), lambda hd, c: (0, hd * nc + cc(c), 0, 0)),
        ns=pl.BlockSpec((bsz, 1, 1, ML_DH), lambda hd, c: (0, hd * nc + cc(c), 0, 0)),
        ms=pl.BlockSpec((bsz, 1, 1, CHUNK), lambda hd, c: (0, hd * nc + cc(c), 0, 0)),
        dgates=pl.BlockSpec((1, bsz, CHUNK, CHUNK), lambda hd, c: (hd, 0, cc(c), 0)),
    )


def _seq(a, bsz):
    return a.reshape(bsz, a.shape[0] // bsz, a.shape[1])


def _ml_chunk_fwd(qs, k, v, gates, b_gate, pre, z, nw, sk, bsz, nc):
    r = qs.shape[0]
    tp = r // bsz
    sp = _ml_chunk_specs(nc, False, bsz)

    def body(qs_all, k_all, v_all, gt_all, bg_ref, pre_all, z_all, nw_ref, sk_ref,
             h_all, yb_all, cs_all, ns_all, ms_all, c_sall, n_sall, m_sall):
        hd, c = pl.program_id(0), pl.program_id(1)

        @pl.when(c == 0)
        def _():
            c_sall[...] = jnp.zeros_like(c_sall)
            n_sall[...] = jnp.zeros_like(n_sall)
            m_sall[...] = jnp.zeros_like(m_sall)

        for bi in range(bsz):
            one(hd, c, qs_all.at[bi], k_all.at[bi], v_all.at[bi], gt_all.at[bi], bg_ref, pre_all.at[bi], z_all.at[bi],
                nw_ref, sk_ref, h_all.at[bi], yb_all.at[bi], cs_all.at[bi], ns_all.at[bi], ms_all.at[bi],
                c_sall.at[bi], n_sall.at[bi], m_sall.at[bi])

    def one(hd, c, qs_ref, k_ref, v_ref, gt_ref, bg_ref, pre_ref, z_ref, nw_ref, sk_ref,
            h_ref, yb_ref, cs_ref, ns_ref, ms_ref, c_s, n_s, m_s):
        cmat, nvec, m = c_s[...], n_s[...], m_s[...]
        cs_ref[0] = cmat
        ns_ref[0] = nvec
        ms_ref[0] = jnp.broadcast_to(m, (1, CHUNK))
        v_ = v_ref[...]
        co = _ml_core(gt_ref[...] + bg_ref[...], hd, c == 0, m, qs_ref[...], k_ref[...], v_, cmat, nvec)
        h = co["num"] / co["dd"]
        h_ref[...] = h
        hn, _ = _ml_headnorm(h)
        yb_ref[...] = _bf((hn * nw_ref[...] + sk_ref[...] * _silu(pre_ref[...])) * _silu(z_ref[...]))
        c_s[...] = co["decay"] * cmat + _dot(_bf(co["wk"]), v_, TN)
        n_s[...] = co["decay"] * nvec + jnp.sum(co["wk"], axis=0, keepdims=True)
        m_s[...] = co["m_new"]

    nst = ML_HEADS * nc
    h, yb, cs, ns, ms = _pc(
        body, name="ml_chunk_fwd", grid=(ML_HEADS, nc),
        in_specs=[sp["hrow"]] * 3 + [sp["gates"], sp["bias"], sp["hrow"], sp["hrow"], sp["hvec"], sp["hvec"]],
        out_specs=[sp["hrow"], sp["hrow"], sp["cs"], sp["ns"], sp["ms"]],
        out_shape=[jax.ShapeDtypeStruct((bsz, tp, ML_WIDTH), F32), jax.ShapeDtypeStruct((bsz, tp, ML_WIDTH), BF16),
                   jax.ShapeDtypeStruct((bsz, nst, ML_DH, ML_DH), F32),
                   jax.ShapeDtypeStruct((bsz, nst, 1, ML_DH), F32), jax.ShapeDtypeStruct((bsz, nst, 1, CHUNK), F32)],
        scratch=[pltpu.VMEM((bsz, ML_DH, ML_DH), F32), pltpu.VMEM((bsz, 1, ML_DH), F32),
                 pltpu.VMEM((bsz, 1, 1), F32)],
        vmem=12 << 20)(*[_seq(a, bsz) for a in (qs, k, v, gates)], b_gate, _seq(pre, bsz), _seq(z, bsz), nw, sk)
    return h.reshape(r, ML_WIDTH), yb.reshape(r, ML_WIDTH), cs, ns, ms


def _ml_chunk_bwd(dyb, qs, k, v, gates, b_gate, pre, z, nw, sk, h, cs, ns, ms, bsz, nc, dep=None):
    r = qs.shape[0]
    tp = r // bsz
    sp = _ml_chunk_specs(nc, True, bsz)

    def body(dy_all, qs_all, k_all, v_all, gt_all, bg_ref, pre_all, z_all, nw_ref, sk_ref, h_all, cs_all, ns_all,
             ms_all, dq_all, dk_all, dv_all, dz_all, dxc_all, dgt_all, dnw_ref, dsk_ref, dc_sall, dn_sall):
        hd, c = pl.program_id(0), pl.program_id(1)

        @pl.when(c == 0)
        def _():
            for ref in (dnw_ref, dsk_ref, dc_sall, dn_sall):
                ref[...] = jnp.zeros_like(ref)

        for bi in range(bsz):
            one(hd, c, dy_all.at[bi], qs_all.at[bi], k_all.at[bi], v_all.at[bi], gt_all.at[bi], bg_ref,
                pre_all.at[bi], z_all.at[bi], nw_ref, sk_ref, h_all.at[bi], cs_all.at[bi], ns_all.at[bi],
                ms_all.at[bi], dq_all.at[bi], dk_all.at[bi], dv_all.at[bi], dz_all.at[bi], dxc_all.at[bi],
                dgt_all.at[0, bi], dnw_ref, dsk_ref, dc_sall.at[bi], dn_sall.at[bi])

    def one(hd, c, dy_ref, qs_ref, k_ref, v_ref, gt_ref, bg_ref, pre_ref, z_ref, nw_ref, sk_ref, h_ref, cs_ref, ns_ref,
            ms_ref, dq_ref, dk_ref, dv_ref, dz_ref, dxc_ref, dgt_ref, dnw_ref, dsk_ref, dc_s, dn_s):

        qs, k, v = qs_ref[...], k_ref[...], v_ref[...]
        cmat, nvec, m = cs_ref[0], ns_ref[0], ms_ref[0][:, :1]
        co = _ml_core(gt_ref[...] + bg_ref[...], hd, c == nc - 1, m, qs, k, v, cmat, nvec)
        lane, row = co["lane"], co["row"]
        wt, wprev, s, cb, qf = co["wt"], co["wprev"], co["s"], co["cb"], co["qf"]
        h = h_ref[...]
        hn, rstd = _ml_headnorm(h)
        xc = _silu(pre_ref[...])
        zv = z_ref[...]
        nw, sk = nw_ref[...], sk_ref[...]
        dy = dy_ref[...]
        dz_ref[...] = _bf(dy * (hn * nw + sk * xc) * _dsilu(zv))
        do = dy * _silu(zv)
        dsk_ref[...] += jnp.sum(do * xc, axis=0, keepdims=True)
        dnw_ref[...] += jnp.sum(do * hn, axis=0, keepdims=True)
        dxc_ref[...] = do * sk
        dhn = do * nw
        dh = rstd * (dhn - jnp.mean(dhn, axis=1, keepdims=True) - hn * jnp.mean(dhn * hn, axis=1, keepdims=True))
        rinv = 1.0 / co["dd"]
        dnum = dh * rinv
        ddd = -jnp.sum(dh * h, axis=1, keepdims=True) * rinv
        den = co["den"]
        dden = jnp.where(jnp.abs(den) >= co["emt"], ddd * jnp.sign(den), 0.0)
        dnb = _bf(dnum)
        ds = _dot(dnb, v, NT) + dden
        dv = _dot(_bf(s), dnb, TN)
        dnw_ = _bf(dnum * wprev)
        dwn = dden * wprev
        dqs = _dot(dnw_, cb, NT) + dwn * nvec
        dc_out = _dot(qs, dnw_, TN)
        dn_out = jnp.sum(dwn * qf, axis=0, keepdims=True)
        dwprev = jnp.sum(dnum * co["qc"], axis=1, keepdims=True) + dden * co["qn"]
        ds0 = _bf(ds * wt)
        ddm = ds * s
        dqs = dqs + _dot(ds0, k)
        dk = _dot(ds0, qs, TN)
        colc = jnp.sum(ddm.T, axis=1, keepdims=True)
        dbc = dwprev * wprev + jnp.sum(ddm, axis=1, keepdims=True) - colc
        dig = colc
        dcn, dnn = dc_s[...], dn_s[...]
        dcb = _bf(dcn)
        decay, e, kf, wk = co["decay"], co["e"], co["kf"], co["wk"]
        ddecay = (jnp.sum(jnp.sum(dcn * cmat, axis=1, keepdims=True), axis=0, keepdims=True)
                  + jnp.sum(dnn * nvec, axis=1, keepdims=True))
        dwk = _dot(v, dcb, NT) + dnn
        dv = dv + _dot(_bf(wk), dcb)
        dk = dk + e * dwk
        dg = jnp.sum(dwk * kf, axis=1, keepdims=True) * e
        dblast = ddecay * decay + jnp.sum(dg, axis=0, keepdims=True)
        dbc = dbc - dg + jnp.where(row[:, :1] == CHUNK - 1, dblast, 0.0)
        dig = dig + dg
        dc_s[...] = decay * dcn + dc_out
        dn_s[...] = decay * dnn + dn_out
        dlf = _cumsum_rows(jnp.broadcast_to(dbc, (CHUNK, CHUNK)), row, rev=True)[:, :1]
        dfp = dlf * (1.0 - jax.nn.sigmoid(co["fpc"]))
        dig = jnp.where(co["valid"], dig, 0.0)
        dfp = jnp.where(co["valid"], dfp, 0.0)
        dgt_ref[...] = jnp.where(lane == hd, dig, 0.0) + jnp.where(lane == hd + ML_HEADS, dfp, 0.0)
        dq_ref[...] = _bf(dqs * ML_SCALE)
        dk_ref[...] = _bf(dk)
        dv_ref[...] = _bf(dv)

    ob = jax.ShapeDtypeStruct((bsz, tp, ML_WIDTH), BF16)
    dq, dk, dv, dz, dxc, dgt, dnw, dsk = _pc(
        body, name="ml_chunk_bwd", grid=(ML_HEADS, nc),
        in_specs=[sp["hrow"]] * 4 + [sp["gates"], sp["bias"], sp["hrow"], sp["hrow"], sp["hvec"], sp["hvec"],
                                     sp["hrow"], sp["cs"], sp["ns"], sp["ms"]],
        out_specs=[sp["hrow"]] * 5 + [sp["dgates"], sp["hvec"], sp["hvec"]],
        out_shape=[ob, ob, ob, ob, jax.ShapeDtypeStruct((bsz, tp, ML_WIDTH), F32),
                   jax.ShapeDtypeStruct((ML_HEADS, bsz, tp, CHUNK), F32),
                   jax.ShapeDtypeStruct((1, ML_WIDTH), F32), jax.ShapeDtypeStruct((1, ML_WIDTH), F32)],
        scratch=[pltpu.VMEM((bsz, ML_DH, ML_DH), F32), pltpu.VMEM((bsz, 1, ML_DH), F32)], vmem=16 << 20, dep=dep,
    )(*[_seq(a, bsz) for a in (dyb, qs, k, v, gates)], b_gate, _seq(pre, bsz), _seq(z, bsz), nw, sk, _seq(h, bsz),
      cs, ns, ms)
    return (dq.reshape(r, ML_WIDTH), dk.reshape(r, ML_WIDTH), dv.reshape(r, ML_WIDTH), dz.reshape(r, ML_WIDTH),
            dxc.reshape(r, ML_WIDTH), dgt.reshape(ML_HEADS, r, CHUNK), dnw, dsk)


def _ml_pre_bwd(dq, dk, dv, dgates, dxc_skip, pre, x, q, k, v, wq, wk, wv, wgq, wgk, wgv, bsz, nc):
    r = x.shape[0]
    tr = _pick(r, (256, 128))
    nt = r // tr
    hrow = pl.BlockSpec((tr, ML_DH), lambda h, i: (i, h))
    wexp = pl.BlockSpec((1, ML_DH, ML_DH), lambda h, i: (h, 0, 0))
    wcmp = pl.BlockSpec((1, ML_DH, CHUNK), lambda h, i: (h, 0, 0))
    wg = pl.BlockSpec((ML_DH, CHUNK), lambda h, i: (h, 0))
    dgs = pl.BlockSpec((ML_HEADS, tr, CHUNK), lambda h, i: (0, i, 0))
    bgs = pl.BlockSpec((1, 1, CHUNK), lambda h, i: (h, 0, 0))

    def body(dq_ref, dk_ref, dv_ref, dg_ref, dxs_ref, pre_ref, x_ref, q_ref, k_ref, v_ref, wq_ref, wk_ref, wv_ref,
             gq_ref, gk_ref, gv_ref, dpre_ref, dxv_ref, cq_ref, ck_ref, cv_ref, dgq_ref, dgk_ref, dgv_ref, dbg_ref,
             dwq_ref, dwk_ref, dwv_ref):
        i = pl.program_id(1)

        @pl.when(i == 0)
        def _():
            for ref in (dwq_ref, dwk_ref, dwv_ref, dgq_ref, dgk_ref, dgv_ref, dbg_ref):
                ref[...] = jnp.zeros_like(ref)

        dgt = dg_ref[0]
        for j in range(1, ML_HEADS):
            dgt = dgt + dg_ref[j]
        dbg_ref[0] += jnp.sum(dgt, axis=0, keepdims=True)
        dgb = _bf(dgt)
        dqt = _bf(dq_ref[...].astype(F32) + _dot(dgb, gq_ref[...], NT))
        dkt = _bf(dk_ref[...].astype(F32) + _dot(dgb, gk_ref[...], NT))
        dvt = _bf(dv_ref[...].astype(F32) + _dot(dgb, gv_ref[...], NT))
        dgq_ref[...] += _dot(q_ref[...], dgb, TN)
        dgk_ref[...] += _dot(k_ref[...], dgb, TN)
        dgv_ref[...] += _dot(v_ref[...], dgb, TN)
        prev = pre_ref[...]
        xcb = _bf(_silu(prev))
        xb = _bf(x_ref[...])
        dwq_ref[...] += _dot(xcb, dqt, TN)
        dwk_ref[...] += _dot(xcb, dkt, TN)
        dwv_ref[...] += _dot(xb, dvt, TN)
        dxc = _dot(dqt, wq_ref[0], NT) + _dot(dkt, wk_ref[0], NT) + dxs_ref[...]
        dpre_ref[...] = dxc * _dsilu(prev)
        dxv_ref[...] = _dot(dvt, wv_ref[0], NT)

        @pl.when(i == nt - 1)
        def _():
            rr = lax.broadcasted_iota(jnp.int32, (ML_DH, ML_DH), 0)
            cc = lax.broadcasted_iota(jnp.int32, (ML_DH, ML_DH), 1)
            diag = rr // QKV_BLOCK == cc // QKV_BLOCK
            fc = lax.broadcasted_iota(jnp.int32, (ML_DH, CHUNK), 0)
            fo = lax.broadcasted_iota(jnp.int32, (ML_DH, CHUNK), 1)
            fold = jnp.where(fc % QKV_BLOCK == fo, 1.0, 0.0).astype(F32)
            for src, dst in ((dwq_ref, cq_ref), (dwk_ref, ck_ref), (dwv_ref, cv_ref)):
                dst[0] = jnp.dot(jnp.where(diag, src[...], 0.0), fold, precision=HI, preferred_element_type=F32)

    f = jax.ShapeDtypeStruct((r, ML_WIDTH), F32)
    wc = jax.ShapeDtypeStruct((ML_HEADS, ML_DH, CHUNK), F32)
    wgs = jax.ShapeDtypeStruct((ML_WIDTH, CHUNK), F32)
    return _pc(body, name="ml_pre_bwd", grid=(ML_HEADS, nt),
               in_specs=[hrow, hrow, hrow, dgs, hrow, hrow, hrow, hrow, hrow, hrow, wexp, wexp, wexp, wg, wg, wg],
               out_specs=[hrow, hrow, wcmp, wcmp, wcmp, wg, wg, wg, bgs],
               out_shape=[f, f, wc, wc, wc, wgs, wgs, wgs, jax.ShapeDtypeStruct((ML_HEADS, 1, CHUNK), F32)],
               scratch=[pltpu.VMEM((ML_DH, ML_DH), F32)] * 3,
               vmem=8 << 20)(dq, dk, dv, dgates, dxc_skip, pre, x, q, k, v, wq, wk, wv, wgq, wgk, wgv)


def _pad_lanes(w):
    return jnp.pad(w, ((0, 0), (0, CHUNK - w.shape[1])))


def _ml_weights(conv_w, conv_b, wq, wk, wv, w_gate, b_gate, norm_w, skip):
    return dict(
        conv_w=conv_w, conv_b=conv_b,
        wq=_bf(_headwise_expand(wq)), wk=_bf(_headwise_expand(wk)), wv=_bf(_headwise_expand(wv)),
        wgq=_bf(_pad_lanes(w_gate[:ML_WIDTH])), wgk=_bf(_pad_lanes(w_gate[ML_WIDTH:2 * ML_WIDTH])),
        wgv=_bf(_pad_lanes(w_gate[2 * ML_WIDTH:])), b_gate=_pad_lanes(b_gate.reshape(1, -1)),
        norm=norm_w.reshape(1, ML_WIDTH), skip=skip.reshape(1, ML_WIDTH))


def _ml_layer_fwd(x, z, w, bsz, nc):
    pre = _conv_fwd(x, w["conv_w"], w["conv_b"], bsz, nc, name="ml_conv")
    q, qs, k, v, gates = _ml_pre(pre, x, w["wq"], w["wk"], w["wv"], w["wgq"], w["wgk"], w["wgv"], bsz, nc)
    h, yb, cs, ns, ms = _ml_chunk_fwd(qs, k, v, gates, w["b_gate"], pre, z, w["norm"], w["skip"], bsz, nc)
    return yb, dict(pre=pre, q=q, qs=qs, k=k, v=v, gates=gates, h=h, cs=cs, ns=ns, ms=ms)


def _ml_layer_bwd(dyb, x, z, sv, w, bsz, nc, dep=None):
    dq, dk, dv, dz, dxc, dgates, dnw, dsk = _ml_chunk_bwd(
        dyb, sv["qs"], sv["k"], sv["v"], sv["gates"], w["b_gate"], sv["pre"], z, w["norm"], w["skip"], sv["h"],
        sv["cs"], sv["ns"], sv["ms"], bsz, nc, dep=dep)
    dpre, dxv, dwq, dwk, dwv, dgq, dgk, dgv, dbg = _ml_pre_bwd(
        dq, dk, dv, dgates, dxc, sv["pre"], x, sv["q"], sv["k"], sv["v"], w["wq"], w["wk"], w["wv"], w["wgq"],
        w["wgk"], w["wgv"], bsz, nc)
    dx, dcw, dcb = _conv_bwd(dpre, x, w["conv_w"], bsz, nc, name="ml_conv_bwd", add=dxv)
    ng = 2 * ML_HEADS
    grads = dict(
        ml_conv_w=dcw, ml_conv_b=dcb, ml_wq=_headwise_extract(dwq), ml_wk=_headwise_extract(dwk),
        ml_wv=_headwise_extract(dwv), ml_w_gate=jnp.concatenate([dgq[:, :ng], dgk[:, :ng], dgv[:, :ng]], axis=0),
        ml_b_gate=dbg[0][:, :ng], ml_norm=dnw, ml_skip=dsk)
    return dx, dz, grads


HI = lax.Precision.HIGHEST


def _softplus(x):
    return jnp.maximum(x, 0.0) + jnp.log(1.0 + jnp.exp(-jnp.abs(x)))


def _lane_cumsum(x, lane, rev=False):
    del lane
    return _dot_terms(x, _tri(not rev), NN, exact_rhs=True, terms=3)


def _dot_terms(lhs, rhs, dims, *, exact_rhs, terms):
    x = lhs if exact_rhs else rhs
    sel = _bf(rhs if exact_rhs else lhs)
    acc = None
    for _ in range(terms):
        piece = _bf(x)
        part = _dot(piece, sel, dims) if exact_rhs else _dot(sel, piece, dims)
        acc = part if acc is None else acc + part
        x = x - piece.astype(F32)
    return acc


def _head_sum_matrix():
    r = lax.broadcasted_iota(jnp.int32, (SSD_HPG, SSD_GW), 0)
    l = lax.broadcasted_iota(jnp.int32, (SSD_HPG, SSD_GW), 1)
    return jnp.where(l // SSD_P == r, 1.0, 0.0).astype(F32)


def _ssd_core(xs, bm, cm, dt_raw, dt_bias, a_log, first):
    sq = (CHUNK, CHUNK)
    lane8 = lax.broadcasted_iota(jnp.int32, (SSD_HPG, CHUNK), 1)
    lane = lax.broadcasted_iota(jnp.int32, sq, 1)
    row = lax.broadcasted_iota(jnp.int32, sq, 0)
    low = lane < SSD_P
    valid = jnp.logical_or(jnp.logical_not(first), lane8 >= PAD_ROWS)
    pre = dt_raw + dt_bias
    dt = jnp.where(valid, _softplus(pre), 0.0)
    a = -jnp.exp(a_log)
    cum = _lane_cumsum(dt * a, lane8)
    cb = _dot(_bf(cm), _bf(bm), NT)
    heads = []
    for r in range(SSD_HPG):
        rowb = jnp.broadcast_to(cum[r:r + 1, :], sq)
        colb = rowb.T
        seg = jnp.exp(jnp.where(lane <= row, colb - rowb, NEG))
        dtrow = jnp.broadcast_to(dt[r:r + 1, :], sq)
        lastb = colb[CHUNK - 1:CHUNK, :]
        heads.append(dict(seg=seg, dtrow=dtrow, w=cb * seg * dtrow, ecol=jnp.exp(colb),
                          dec=jnp.exp(lastb - colb) * dtrow.T, elast=jnp.exp(lastb)))

    def pairs(key):
        return jnp.concatenate([jnp.where(low[:heads[0][key].shape[0]], heads[2 * j][key], heads[2 * j + 1][key])
                                for j in range(SSD_HPG // 2)], axis=1)

    return dict(lane8=lane8, low=low, valid=valid, pre=pre, dt=dt, a=a, cum=cum, cb=cb, heads=heads,
                expc=pairs("ecol"), dec=pairs("dec"), elast=pairs("elast"))


def _ssd_specs(nc, rev, bsz):
    def cc(c):
        return (nc - 1 - c) if rev else c

    return dict(
        wide=pl.BlockSpec((bsz, CHUNK, SSD_GW), lambda g, c: (0, cc(c), g)),
        narrow=pl.BlockSpec((bsz, CHUNK, SSD_N), lambda g, c: (0, cc(c), g)),
        dtT=pl.BlockSpec((bsz, SSD_HPG, CHUNK), lambda g, c: (0, g, cc(c))),
        hcol=pl.BlockSpec((SSD_HPG, 1), lambda g, c: (g, 0)),
        hacc=pl.BlockSpec((SSD_HPG, CHUNK), lambda g, c: (g, 0)),
        gvec=pl.BlockSpec((1, SSD_GW), lambda g, c: (0, g)),
        state=pl.BlockSpec((bsz, 1, SSD_N, SSD_GW), lambda g, c: (0, g * nc + cc(c), 0, 0)),
    )


def _ssd_chunk_fwd(xs_pre, bm_pre, cm_pre, dt_raw, dt_bias, a_log, d_exp, z, gnorm, bsz, nc):
    tp = xs_pre.shape[1]
    sp = _ssd_specs(nc, False, bsz)

    def body(xs_all, bm_all, cm_all, dt_all, db_ref, al_ref, d_ref, z_all, gn_ref, y_all, yn_all, st_all, st_sall):
        c = pl.program_id(1)

        @pl.when(c == 0)
        def _():
            st_sall[...] = jnp.zeros_like(st_sall)

        for bi in range(bsz):
            one(c, xs_all.at[bi], bm_all.at[bi], cm_all.at[bi], dt_all.at[bi], db_ref, al_ref, d_ref, z_all.at[bi],
                gn_ref, y_all.at[bi], yn_all.at[bi], st_all.at[bi], st_sall.at[bi])

    def one(c, xs_ref, bm_ref, cm_ref, dt_ref, db_ref, al_ref, d_ref, z_ref, gn_ref, y_ref, yn_ref, st_ref, st_s):
        state = st_s[...]
        st_ref[0] = state
        xs, bm, cm = _silu(xs_ref[...]), _silu(bm_ref[...]), _silu(cm_ref[...])
        co = _ssd_core(xs, bm, cm, dt_ref[...], db_ref[...], al_ref[...], c == 0)
        low, hd = co["low"], co["heads"]
        ys = []
        for j in range(SSD_HPG // 2):
            xp = xs[:, j * CHUNK:(j + 1) * CHUNK]
            lhs = jnp.concatenate([hd[2 * j]["w"], hd[2 * j + 1]["w"]], axis=1)
            rhs = jnp.concatenate([jnp.where(low, xp, 0.0), jnp.where(low, 0.0, xp)], axis=0)
            ys.append(_dot(_bf(lhs), _bf(rhs)))
        cmb = _bf(cm)
        y = jnp.concatenate(ys, axis=1) + co["expc"] * _dot(cmb, _bf(state)) + d_ref[...] * xs
        y_ref[...] = y
        yg = y * _silu(z_ref[...])
        rstd = lax.rsqrt(jnp.mean(yg * yg, axis=1, keepdims=True) + NORM_EPS)
        yn_ref[...] = _bf(yg * rstd * gn_ref[...])
        st_s[...] = co["elast"] * state + _dot(_bf(bm), _bf(xs * co["dec"]), TN)

    return _pc(body, name="ssd_chunk_fwd", grid=(SSD_GROUPS, nc),
               in_specs=[sp["wide"], sp["narrow"], sp["narrow"], sp["dtT"], sp["hcol"], sp["hcol"], sp["gvec"],
                         sp["wide"], sp["gvec"]],
               out_specs=[sp["wide"], sp["wide"], sp["state"]],
               out_shape=[jax.ShapeDtypeStruct((bsz, tp, SSD_INNER), F32),
                          jax.ShapeDtypeStruct((bsz, tp, SSD_INNER), BF16),
                          jax.ShapeDtypeStruct((bsz, SSD_GROUPS * nc, SSD_N, SSD_GW), F32)],
               scratch=[pltpu.VMEM((bsz, SSD_N, SSD_GW), F32)], vmem=12 << 20,
               )(xs_pre, bm_pre, cm_pre, dt_raw, dt_bias, a_log, d_exp, z, gnorm)


def _ssd_chunk_bwd(dyn, xs_pre, bm_pre, cm_pre, dt_raw, dt_bias, a_log, d_exp, z, gnorm, y, states, bsz, nc):
    tp = xs_pre.shape[1]
    sp = _ssd_specs(nc, True, bsz)

    def body(dyn_all, xs_all, bm_all, cm_all, dt_all, db_ref, al_ref, d_ref, z_all, gn_ref, y_all, st_all,
             dxs_all, dbm_all, dcm_all, dz_all, ddt_all, dgn_ref, dd_ref, dbias_ref, dal_ref, ds_sall):
        c = pl.program_id(1)

        @pl.when(c == 0)
        def _():
            for ref in (dgn_ref, dd_ref, dbias_ref, dal_ref, ds_sall):
                ref[...] = jnp.zeros_like(ref)

        for bi in range(bsz):
            one(c, dyn_all.at[bi], xs_all.at[bi], bm_all.at[bi], cm_all.at[bi], dt_all.at[bi], db_ref, al_ref, d_ref,
                z_all.at[bi], gn_ref, y_all.at[bi], st_all.at[bi], dxs_all.at[bi], dbm_all.at[bi], dcm_all.at[bi],
                dz_all.at[bi], ddt_all.at[bi], dgn_ref, dd_ref, dbias_ref, dal_ref, ds_sall.at[bi])

    def one(c, dyn_ref, xs_ref, bm_ref, cm_ref, dt_ref, db_ref, al_ref, d_ref, z_ref, gn_ref, y_ref, st_ref,
            dxs_ref, dbm_ref, dcm_ref, dz_ref, ddt_ref, dgn_ref, dd_ref, dbias_ref, dal_ref, ds_s):
        xs_p, bm_p, cm_p = xs_ref[...], bm_ref[...], cm_ref[...]
        xs, bm, cm = _silu(xs_p), _silu(bm_p), _silu(cm_p)
        state = st_ref[0]
        co = _ssd_core(xs, bm, cm, dt_ref[...], db_ref[...], al_ref[...], c == nc - 1)
        low, hd, lane8, cb = co["low"], co["heads"], co["lane8"], co["cb"]
        dt, a, cum = co["dt"], co["a"], co["cum"]
        sub8 = lax.broadcasted_iota(jnp.int32, (SSD_HPG, CHUNK), 0)
        eh = _head_sum_matrix()

        def head_rows(full):
            return _dot_terms(eh, full, NT, exact_rhs=False, terms=2)

        def head_col(vec):
            return jnp.sum(eh * vec, axis=1, keepdims=True)

        yv, zv, gn = y_ref[...], z_ref[...], gn_ref[...]
        sz = _silu(zv)
        yg = yv * sz
        rstd = lax.rsqrt(jnp.mean(yg * yg, axis=1, keepdims=True) + NORM_EPS)
        yh = yg * rstd
        dyn = dyn_ref[...]
        dgn_ref[...] += jnp.sum(dyn * yh, axis=0, keepdims=True)
        dyh = dyn * gn
        dyg = rstd * (dyh - yh * jnp.mean(dyh * yh, axis=1, keepdims=True))
        dz_ref[...] = _bf(dyg * yv * _dsilu(zv))
        dy = dyg * sz
        dxs = dy * d_ref[...]
        dd_ref[...] += head_col(jnp.sum(dy * xs, axis=0, keepdims=True))
        cmb, bmb, stb = _bf(cm), _bf(bm), _bf(state)
        ysv = _dot(cmb, stb)
        expc = co["expc"]
        dys = _bf(dy * expc)
        dcum = head_rows(dy * ysv * expc)
        dcm = _dot(dys, stb, NT)
        dstate_out = _dot(cmb, dys, TN)
        dcb = jnp.zeros((CHUNK, CHUNK), F32)
        ddt = jnp.zeros((SSD_HPG, CHUNK), F32)
        dxs_pairs = []
        for j in range(SSD_HPG // 2):
            sl = slice(j * CHUNK, (j + 1) * CHUNK)
            dyp, xp = dy[:, sl], _bf(xs[:, sl])
            lhs = _bf(jnp.concatenate([hd[2 * j]["w"], hd[2 * j + 1]["w"]], axis=1))
            both = _dot(lhs, _bf(dyp), TN)
            dxs_pairs.append(jnp.where(low, both[:CHUNK], both[CHUNK:]))
            for q, msk in ((2 * j, low), (2 * j + 1, jnp.logical_not(low))):
                h = hd[q]
                dw = _dot(_bf(jnp.where(msk, dyp, 0.0)), xp, NT)
                dcb = dcb + dw * h["seg"] * h["dtrow"]
                e_ = dw * h["w"]
                dcum_r = jnp.sum(e_.T, axis=0, keepdims=True) - jnp.sum(e_, axis=0, keepdims=True)
                ddt_r = jnp.sum(dw * cb * h["seg"], axis=0, keepdims=True)
                dcum = dcum + jnp.where(sub8 == q, dcum_r, 0.0)
                ddt = ddt + jnp.where(sub8 == q, ddt_r, 0.0)
        dxs = dxs + jnp.concatenate(dxs_pairs, axis=1)
        dcbb = _bf(dcb)
        dcm = dcm + _dot(dcbb, bmb)
        dbm = _dot(dcbb, cmb, TN)
        dsn = ds_s[...]
        dsb = _bf(dsn)
        dec = co["dec"]
        dbm = dbm + _dot(_bf(xs * dec), dsb, NT)
        dxd = _dot(bmb, dsb)
        dxs = dxs + dxd * dec
        ddec = head_rows(dxd * xs)
        last = cum[:, CHUNK - 1:CHUNK]
        erow = jnp.exp(last - cum)
        ddt = ddt + ddec * erow
        dla = ddec * erow * dt
        dlast = (jnp.sum(dla, axis=1, keepdims=True)
                 + head_col(jnp.sum(dsn * state, axis=0, keepdims=True)) * jnp.exp(last))
        dcum = dcum - dla + jnp.where(lane8 == CHUNK - 1, dlast, 0.0)
        ds_s[...] = co["elast"] * dsn + dstate_out
        dda = _lane_cumsum(dcum, lane8, rev=True)
        ddt = jnp.where(co["valid"], ddt + dda * a, 0.0)
        ddt_raw = ddt * jax.nn.sigmoid(co["pre"])
        ddt_ref[...] = ddt_raw
        dbias_ref[...] += jnp.sum(ddt_raw, axis=1, keepdims=True)
        dal_ref[...] += jnp.sum(dda * dt, axis=1, keepdims=True) * a
        dxs_ref[...] = dxs * _dsilu(xs_p)
        dbm_ref[...] = dbm * _dsilu(bm_p)
        dcm_ref[...] = dcm * _dsilu(cm_p)

    st = jax.ShapeDtypeStruct
    hacc = st((SSD_HEADS, CHUNK), F32)
    return _pc(body, name="ssd_chunk_bwd", grid=(SSD_GROUPS, nc),
               in_specs=[sp["wide"], sp["wide"], sp["narrow"], sp["narrow"], sp["dtT"], sp["hcol"], sp["hcol"],
                         sp["gvec"], sp["wide"], sp["gvec"], sp["wide"], sp["state"]],
               out_specs=[sp["wide"], sp["narrow"], sp["narrow"], sp["wide"], sp["dtT"], sp["gvec"], sp["hacc"],
                          sp["hacc"], sp["hacc"]],
               out_shape=[st((bsz, tp, SSD_INNER), F32), st((bsz, tp, SSD_BC), F32), st((bsz, tp, SSD_BC), F32),
                          st((bsz, tp, SSD_INNER), BF16), st((bsz, SSD_HEADS, tp), F32), st((1, SSD_INNER), F32),
                          hacc, hacc, hacc],
               scratch=[pltpu.VMEM((bsz, SSD_N, SSD_GW), F32)], vmem=20 << 20,
               )(dyn, xs_pre, bm_pre, cm_pre, dt_raw, dt_bias, a_log, d_exp, z, gnorm, y, states)


SSD_BC = SSD_GROUPS * SSD_N


def _ssd_weights(conv_w, conv_b, dt_bias, a_log, d, gnorm):
    cuts = (0, SSD_INNER, SSD_INNER + SSD_BC, SSD_INNER + 2 * SSD_BC)
    return dict(
        conv_w=[conv_w[:, cuts[i]:cuts[i + 1]] for i in range(3)],
        conv_b=[conv_b[cuts[i]:cuts[i + 1]] for i in range(3)],
        dt_bias=dt_bias.reshape(SSD_HEADS, 1), a_log=a_log.reshape(SSD_HEADS, 1),
        d_exp=jnp.repeat(d.reshape(SSD_HEADS), SSD_P).reshape(1, SSD_INNER), gnorm=gnorm.reshape(1, SSD_INNER))


def _ssd_layer_fwd(z, xs_in, bm_in, cm_in, dt_rows, w, bsz, nc):
    pres = [_conv_fwd(a, w["conv_w"][i], w["conv_b"][i], bsz, nc, name=f"ssd_conv{i}")
            for i, a in enumerate((xs_in, bm_in, cm_in))]
    def seq(a):
        return a.reshape(bsz, nc * CHUNK, a.shape[-1])

    dt_t = jnp.swapaxes(seq(dt_rows)[:, :, :SSD_HEADS], 1, 2)
    y, yn, states = _ssd_chunk_fwd(seq(pres[0]), seq(pres[1]), seq(pres[2]), dt_t, w["dt_bias"], w["a_log"],
                                   w["d_exp"], seq(z), w["gnorm"], bsz, nc)
    return yn.reshape(-1, SSD_INNER), dict(pres=pres, dt_t=dt_t, y=y, states=states)


def _ssd_layer_bwd(dyn, z, xs_in, bm_in, cm_in, sv, w, bsz, nc):
    pres = sv["pres"]

    def seq(a):
        return a.reshape(bsz, nc * CHUNK, a.shape[-1])

    def rows(a):
        return a.reshape(-1, a.shape[-1])

    dxs_p, dbm_p, dcm_p, dz, ddt_t, dgn, dd, dbias, dal = _ssd_chunk_bwd(
        seq(dyn), seq(pres[0]), seq(pres[1]), seq(pres[2]), sv["dt_t"], w["dt_bias"], w["a_log"], w["d_exp"], seq(z),
        w["gnorm"], sv["y"], sv["states"], bsz, nc)
    dz = rows(dz)
    outs = [_conv_bwd(rows(dp), a, w["conv_w"][i], bsz, nc, name=f"ssd_conv_bwd{i}")
            for i, (dp, a) in enumerate(((dxs_p, xs_in), (dbm_p, bm_in), (dcm_p, cm_in)))]
    ddt = _bf(_pad_lanes(rows(jnp.swapaxes(ddt_t, 1, 2))))
    grads = dict(
        ssd_conv_w=jnp.concatenate([o[1] for o in outs], axis=1),
        ssd_conv_b=jnp.concatenate([o[2] for o in outs], axis=1),
        ssd_dt_bias=dbias[:, 0], ssd_a_log=dal[:, 0], ssd_d=dd[:, 0], ssd_gnorm=dgn)
    return dz, outs[0][0], outs[1][0], outs[2][0], ddt, grads


WNAMES = ("meta_tokens", "ab_norm", "ab_w_in", "s5_lambda_re", "s5_lambda_im", "s5_log_dt", "s5_b_re", "s5_b_im",
          "s5_c_re", "s5_c_im", "s5_d", "s5_glu_w", "s5_glu_b", "ml_conv_w", "ml_conv_b", "ml_wq", "ml_wk", "ml_wv",
          "ml_w_gate", "ml_b_gate", "ml_norm", "ml_skip", "ab_w_out", "ssd_norm", "ssd_w_in", "ssd_conv_w",
          "ssd_conv_b", "ssd_dt_bias", "ssd_a_log", "ssd_d", "ssd_gnorm", "ssd_w_out", "final_norm")
SHARD_AXIS = dict(meta_tokens=1, ab_w_in=2, s5_glu_w=1, ml_conv_w=2, ml_wq=1, ml_wk=1, ml_wv=1, ml_w_gate=1,
                  ab_w_out=1, ssd_norm=1, ssd_w_in=2, ssd_conv_w=2, ssd_conv_b=1, ssd_gnorm=1, ssd_w_out=1)
BIG = ("ab_w_in", "s5_glu_w", "ab_w_out", "ssd_w_in", "ssd_w_out")
SMALL = tuple(n for n in WNAMES if n in SHARD_AXIS and n not in BIG)
REPL = tuple(n for n in WNAMES if n not in SHARD_AXIS)
PACK_ALIGN = 8 * 128


def _pack(arrs):
    lead = arrs[0][1]
    parts = []
    for a, nlead in arrs:
        f = a.reshape(a.shape[:nlead] + (-1,))
        f = jnp.pad(f, [(0, 0)] * nlead + [(0, (-f.shape[-1]) % PACK_ALIGN)])
        parts.append(f.reshape(f.shape[:nlead] + (-1, 128)))
    return jnp.concatenate(parts, axis=lead)


def _unpack(p, shapes):
    out, r0 = [], 0
    lead = p.shape[:-2]
    for s in shapes:
        n = math.prod(s)
        rows = -(-n // PACK_ALIGN) * 8
        seg = p[..., r0:r0 + rows, :].reshape(lead + (rows * 128,))[..., :n]
        out.append(seg.reshape(lead + tuple(s)))
        r0 += rows
    return out


def _assemble(g, axis):
    m = jnp.moveaxis(g, 0, axis)
    return m.reshape(m.shape[:axis] + (m.shape[axis] * m.shape[axis + 1],) + m.shape[axis + 2:])


def _split(full, axis):
    s = full.shape
    m = full.reshape(s[:axis] + (N_DEV, s[axis] // N_DEV) + s[axis + 1:])
    return jnp.moveaxis(m, axis, 0)


def kernel(x, *rest):
    nw = len(WNAMES)
    w = dict(zip(WNAMES, rest[:nw]))
    loss_target = rest[nw]
    mom = dict(zip(WNAMES, rest[nw + 1:2 * nw + 1]))
    var = dict(zip(WNAMES, rest[2 * nw + 1:3 * nw + 1]))
    bsz = x.shape[0]
    nc = 1 + SEQ // CHUNK
    tp = nc * CHUNK

    local = {n: _bf(w[n][0]) for n in BIG}
    small_local = _pack([(w[n], 0) for n in SMALL])
    gs = _exchange_start([small_local], ["ag"], name="gather_s")
    ga = _exchange_start([local["ab_w_in"]], ["ag"], name="gather_a", dep=gs["token"], peers=SAME_CORE[1:])
    got_s = _exchange_wait(gs, ga["token"])

    def assemble_big(n, got):
        return _assemble(got[:, None], SHARD_AXIS[n])[0]

    full = {}
    for n, g in zip(SMALL, _unpack(got_s[0], [w[n].shape for n in SMALL])):
        full[n] = _assemble(g, SHARD_AXIS[n])[0] if n != "meta_tokens" else _assemble(g, SHARD_AXIS[n])
    for n in REPL:
        full[n] = w[n][0] if n != "final_norm" else w[n]
    glu_b = full["s5_glu_b"].reshape(1, S5_WIDTH)
    meta = jnp.broadcast_to(full["meta_tokens"][None], (bsz, N_META, D_MODEL))
    h0 = jnp.concatenate([jnp.zeros((bsz, PAD_ROWS, D_MODEL), F32), meta, x], axis=1).reshape(bsz * tp, D_MODEL)
    xn0 = _rms_fwd(h0, full["ab_norm"], name="rms0")
    s5p, s5_vjp = _s5_tables(*[full[n] for n in ("s5_lambda_re", "s5_lambda_im", "s5_log_dt", "s5_b_re", "s5_b_im",
                                                   "s5_c_re", "s5_c_im", "s5_d")])
    mlw = _ml_weights(*[full[n] for n in ("ml_conv_w", "ml_conv_b", "ml_wq", "ml_wk", "ml_wv", "ml_w_gate",
                                           "ml_b_gate", "ml_norm", "ml_skip")])
    got_a = _exchange_wait(ga, [xn0, s5p["wbr"], s5p["wcr"], s5p["pr"], mlw["wq"], mlw["wk"], mlw["wv"], mlw["wgq"]])
    fwd_a = _sibling_forward_start(got_a[0], name="gather_a2")
    got_a = [_sibling_forward_wait(fwd_a, fwd_a["token"])]
    gb = _exchange_start([local["s5_glu_w"], local["ab_w_out"]], ["ag", "ag"], name="gather_b", dep=got_a[0])
    gc = _exchange_start([local["ssd_w_in"], local["ssd_w_out"]], ["ag", "ag"], name="gather_c", dep=gb["token"])
    full["ab_w_in"] = assemble_big("ab_w_in", got_a[0])
    cuts0 = (0, S5_WIDTH, 2 * S5_WIDTH, 2 * S5_WIDTH + ML_WIDTH, 2 * (S5_WIDTH + ML_WIDTH))
    w_in0 = [full["ab_w_in"][:, cuts0[i]:cuts0[i + 1]] for i in range(4)]

    u, za, xb, zb = [_mm(xn0, wi, "NN", name=f"in0_{i}") for i, wi in enumerate(w_in0)]
    got_b = []

    def glu_w_after(scan_out):
        got_b.extend(_exchange_wait(gb, scan_out))
        return assemble_big("s5_glu_w", got_b[0])

    sv5 = _s5_layer_fwd(u, s5p, glu_w_after, bsz, nc)
    glu_w = assemble_big("s5_glu_w", got_b[0])
    w_out0 = assemble_big("ab_w_out", got_b[1])
    w_out0 = [w_out0[:S5_WIDTH], w_out0[S5_WIDTH:]]
    ya = _s5_post(sv5["y1"], sv5["glu_pre"], glu_b, za)
    yb, svm = _ml_layer_fwd(xb, zb, mlw, bsz, nc)
    h1 = _mm(ya, w_out0[0], "NN", name="out0_a", add=h0)
    h1 = _mm(yb, w_out0[1], "NN", name="out0_b", add=h1)
    got_c = _exchange_wait(gc, h1)
    w_in1, w_out1 = assemble_big("ssd_w_in", got_c[0]), assemble_big("ssd_w_out", got_c[1])
    cuts1 = (0, SSD_INNER, 2 * SSD_INNER, 2 * SSD_INNER + SSD_BC, 2 * SSD_INNER + 2 * SSD_BC)
    w_in1 = [w_in1[:, cuts1[i]:cuts1[i + 1]] for i in range(4)] + [_pad_lanes(w_in1[:, cuts1[4]:])]
    xn1 = _rms_fwd(h1, full["ssd_norm"], name="rms1")
    z1, xs_in, bm_in, cm_in, dt_rows = [_mm(xn1, wi, "NN", name=f"in1_{i}") for i, wi in enumerate(w_in1)]
    ssdw = _ssd_weights(*[full[n] for n in ("ssd_conv_w", "ssd_conv_b", "ssd_dt_bias", "ssd_a_log", "ssd_d",
                                             "ssd_gnorm")])
    yn, svs = _ssd_layer_fwd(z1, xs_in, bm_in, cm_in, dt_rows, ssdw, bsz, nc)
    h2 = _mm(yn, w_out1, "NN", name="out1", add=h1)
    loss_part, dh2, dfinal = _final_loss(h2, full["final_norm"], loss_target, bsz, nc)
    loss = lax.psum(loss_part[0, 0], ("x", "y", "c"))

    g = {"final_norm": dfinal}
    dyn = _mm(dh2, w_out1, "NT", name="d_out1")
    g["ssd_w_out"] = _mm(yn, dh2, "TN", name="dw_out1", out_dtype=BF16)
    dz1, dxs, dbm, dcm, ddt, gs = _ssd_layer_bwd(dyn, z1, xs_in, bm_in, cm_in, svs, ssdw, bsz, nc)
    g.update(gs)
    dps1 = (dz1, dxs, dbm, dcm, ddt)
    dxn1 = None
    for i, (dp, wi) in enumerate(zip(dps1, w_in1)):
        dxn1 = _mm(dp, wi, "NT", name=f"d_in1_{i}", add=dxn1)
    dw1 = [_mm(xn1, dp, "TN", name=f"dw_in1_{i}", out_dtype=BF16) for i, dp in enumerate(dps1)]
    g["ssd_w_in"] = jnp.concatenate(dw1[:4] + [dw1[4][:, :SSD_HEADS]], axis=1)

    def local_shape(n):
        return w[n].shape

    def slabs(n):
        gf = g[n].reshape((1,) + tuple(g[n].shape)) if n != "meta_tokens" else g[n]
        full_shape = tuple(d * (N_DEV if i == SHARD_AXIS[n] else 1) for i, d in enumerate(local_shape(n)))
        return _split(gf.reshape(full_shape), SHARD_AXIS[n])

    x1 = _exchange_start([slabs("ssd_w_in")[:, 0], slabs("ssd_w_out")[:, 0]], ["a2a", "a2a"], name="grads_1")
    dh1, g["ssd_norm"] = _rms_bwd(h1, full["ssd_norm"], dxn1, dh2, name="rms1_bwd", dep=x1["token"])
    dya = _mm(dh1, w_out0[0], "NT", name="d_out0_a")
    dyb = _mm(dh1, w_out0[1], "NT", name="d_out0_b")
    g["ab_w_out"] = jnp.concatenate([_mm(ya, dh1, "TN", name="dw_out0_a", out_dtype=BF16),
                                     _mm(yb, dh1, "TN", name="dw_out0_b", out_dtype=BF16)], axis=0)
    du, dza, g5 = _s5_layer_bwd(dya, u, za, sv5, s5p, s5_vjp, glu_w, glu_b, bsz, nc)
    g.update(g5)
    x2 = _exchange_start([slabs("ab_w_out")[:, 0], _bf(slabs("s5_glu_w")[:, 0])], ["a2a", "a2a"], name="grads_2")
    dxb, dzb, gm = _ml_layer_bwd(dyb, xb, zb, svm, mlw, bsz, nc, dep=x2["token"])
    g.update(gm)
    dps0 = (du, dza, dxb, dzb)
    dw0 = [_mm(xn0, dp, "TN", name=f"dw_in0_{i}", out_dtype=BF16, tn=S5_WIDTH, slabs=True) for i, dp in enumerate(dps0)]
    dw_in0_slabs = jnp.concatenate(dw0, axis=0)
    x3 = _exchange_start([dw_in0_slabs], ["a2a"], name="grads_3")
    dxn0 = None
    for i, (dp, wi) in enumerate(zip(dps0, w_in0)):
        dxn0 = _mm(dp, wi, "NT", name=f"d_in0_{i}", add=dxn0, dep=x3["token"] if i == 0 else None)
    dh0, g["ab_norm"] = _rms_bwd(h0, full["ab_norm"], dxn0, dh1, name="rms0_bwd")
    dh0 = dh0.reshape(bsz, tp, D_MODEL)
    grad_x = dh0[:, CHUNK:]
    g["meta_tokens"] = jnp.sum(dh0[:, PAD_ROWS:CHUNK], axis=0)

    small_g = _pack([(slabs(n), 1) for n in SMALL])
    repl_g = _pack([(g[n], 0) for n in REPL])
    x4 = _exchange_start([small_g, repl_g], ["a2a", "ag"], name="grads_4")

    def update_big(n, gp):
        return _adamw(w[n][0], mom[n][0], var[n][0], gp, name=f"adamw_{n}")

    res = {}
    ex1 = _exchange_wait(x1, x4["token"])
    res["ssd_w_in"], res["ssd_w_out"] = update_big("ssd_w_in", ex1[0]), update_big("ssd_w_out", ex1[1])
    ex2 = _exchange_wait(x2, res["ssd_w_out"][0])
    res["ab_w_out"], res["s5_glu_w"] = update_big("ab_w_out", ex2[0]), update_big("s5_glu_w", ex2[1])
    ex3 = _exchange_wait(x3, [res[n][0] for n in ("ssd_w_in", "ssd_w_out", "ab_w_out", "s5_glu_w")])
    res["ab_w_in"] = update_big("ab_w_in", ex3[0])
    ex4 = _exchange_wait(x4, res["ab_w_in"][0])
    for names, gp, tag in ((SMALL, ex4[0], "small"), (REPL, ex4[1], "repl")):
        shapes = [local_shape(n) for n in names]
        packs = [_pack([(d[n], 0) for n in names]) for d in (w, mom, var)]
        outs = _adamw(packs[0], packs[1], packs[2], gp, name=f"adamw_{tag}")
        for k, o in enumerate(outs):
            for n, a in zip(names, _unpack(o, shapes)):
                res.setdefault(n, [None] * 4)[k] = a
    outs = [loss, grad_x]
    for k in range(4):
        outs += [res[n][k].reshape(local_shape(n)) for n in WNAMES]
    return tuple(outs)
```

```python
import functools
import math

import jax
import jax.numpy as jnp
from jax import lax
from jax.experimental import pallas as pl
from jax.experimental.pallas import tpu as pltpu

F32 = jnp.float32
BF16 = jnp.bfloat16

D_MODEL = 2048
SEQ = 2048
N_META = 16
CHUNK = 128
PAD_ROWS = CHUNK - N_META
NORM_EPS = 1e-6
HEAD_NORM_EPS = 1e-5
S5_WIDTH = 1024
S5_GROUPS = 64
S5_GROUP_SIZE = 16
S5_STATE = 64
S5_GB = 8
S5_LANES = S5_GB * S5_STATE
ML_WIDTH = 3072
ML_HEADS = 8
ML_DH = 384
ML_CONV = 4
QKV_BLOCK = 4
SSD_INNER = 4096
SSD_HEADS = 64
SSD_P = 64
SSD_N = 128
SSD_GROUPS = 8
SSD_HPG = 8
SSD_GW = SSD_HPG * SSD_P
N_DEV = 8
ADAM_LR, ADAM_B1, ADAM_B2, ADAM_EPS, ADAM_WD, ADAM_STEP = 0.001, 0.9, 0.999, 1e-08, 0.01, 10
NEG = -1e30
VMEM_CAP = 60 * 1024 * 1024
MM_BLOCK_BUDGET = 22 * 1024 * 1024
MESH = pl.DeviceIdType.MESH

NN = (((1,), (0,)), ((), ()))
NT = (((1,), (1,)), ((), ()))
TN = (((0,), (0,)), ((), ()))


def _dot(a, b, dims=NN):
    return lax.dot_general(a, b, dims, preferred_element_type=F32)


def _bf(x):
    return x.astype(BF16)


def _pick(n, cands):
    for c in cands:
        if n % c == 0:
            return c
    return n


def _nbytes(shape, dtype):
    return math.prod(shape) * jnp.dtype(dtype).itemsize


ANY_SPEC = pl.BlockSpec(memory_space=pl.ANY)


def _pc(body, *, name, grid, in_specs, out_specs, out_shape, scratch=(), vmem=None, dep=None):
    limit = None if vmem is None else int(min(VMEM_CAP, max(32 * 1024 * 1024, 2 * vmem + (8 << 20))))
    n_in = len(in_specs)
    if dep is not None:
        inner = body

        def body(*refs):
            inner(*refs[:n_in], *refs[n_in + 1:])

        in_specs = list(in_specs) + [ANY_SPEC]
    call = pl.pallas_call(
        body, name=name, grid=grid, in_specs=in_specs, out_specs=out_specs, out_shape=out_shape,
        scratch_shapes=list(scratch),
        compiler_params=pltpu.CompilerParams(dimension_semantics=("arbitrary",) * len(grid), vmem_limit_bytes=limit))
    return call if dep is None else (lambda *args: call(*args, dep))


def _silu(x):
    return x * jax.nn.sigmoid(x)


def _dsilu(x):
    s = jax.nn.sigmoid(x)
    return s * (1.0 + x * (1.0 - s))


def _gelu_and_grad(x):
    c0 = math.sqrt(2.0 / math.pi)
    inner = c0 * (x + 0.044715 * x * x * x)
    t = jnp.tanh(inner)
    g = 0.5 * x * (1.0 + t)
    dg = 0.5 * (1.0 + t) + 0.5 * x * (1.0 - t * t) * c0 * (1.0 + 3 * 0.044715 * x * x)
    return g, dg


def _mm(a, b, mode, *, name, add=None, out_dtype=F32, tn=None, slabs=False, dep=None):
    if mode == "NN":
        (m, k), (k2, n) = a.shape, b.shape
    elif mode == "NT":
        (m, k), (n, k2) = a.shape, b.shape
    else:
        (k, m), (k2, n) = a.shape, b.shape
    assert k == k2, (a.shape, b.shape, mode)
    tm = _pick(m, (1088, 1024, 768, 512, 384, 256, 128))
    tn = tn or _pick(n, (512, 384, 256, 128))

    def block_bytes(tk):
        return (_nbytes((tm, tk), a.dtype) + _nbytes((tk, tn), b.dtype) + _nbytes((tm, tn), out_dtype)
                + (_nbytes((tm, tn), F32) if add is not None else 0))

    budget = MM_BLOCK_BUDGET // 2 if mode == "TN" else MM_BLOCK_BUDGET
    tk = k if block_bytes(k) <= budget else _pick(k, (2176, 2048, 1088, 1024, 768, 512, 384, 256, 128))
    nk = k // tk
    dims = {"NN": NN, "NT": NT, "TN": TN}[mode]

    def body(*refs):
        a_ref, b_ref = refs[0], refs[1]
        add_ref = refs[2] if add is not None else None
        o_ref = refs[3] if add is not None else refs[2]

        def finish(r):
            if add_ref is not None:
                r = r + add_ref[...]
            o_ref[...] = r.reshape(o_ref.shape).astype(o_ref.dtype)

        prod = _dot(_bf(a_ref[...]), _bf(b_ref[...]), dims)
        if nk == 1:
            finish(prod)
            return
        acc_ref = refs[-1]
        kk = pl.program_id(2)

        @pl.when(kk == 0)
        def _():
            acc_ref[...] = prod

        @pl.when(kk > 0)
        def _():
            acc_ref[...] += prod

        @pl.when(kk == nk - 1)
        def _():
            finish(acc_ref[...])

    if mode == "NN":
        a_spec = pl.BlockSpec((tm, tk), lambda i, j, kk: (i, kk))
        b_spec = pl.BlockSpec((tk, tn), lambda i, j, kk: (kk, j))
    elif mode == "NT":
        a_spec = pl.BlockSpec((tm, tk), lambda i, j, kk: (i, kk))
        b_spec = pl.BlockSpec((tn, tk), lambda i, j, kk: (j, kk))
    else:
        a_spec = pl.BlockSpec((tk, tm), lambda i, j, kk: (kk, i))
        b_spec = pl.BlockSpec((tk, tn), lambda i, j, kk: (kk, j))
    in_specs = [a_spec, b_spec]
    args = [a, b]
    if add is not None:
        in_specs.append(pl.BlockSpec((tm, tn), lambda i, j, kk: (i, j)))
        args.append(add)
    if slabs:
        out_shape = jax.ShapeDtypeStruct((n // tn, m, tn), out_dtype)
        out_spec = pl.BlockSpec((1, tm, tn), lambda i, j, kk: (j, i, 0))
    else:
        out_shape = jax.ShapeDtypeStruct((m, n), out_dtype)
        out_spec = pl.BlockSpec((tm, tn), lambda i, j, kk: (i, j))
    return _pc(body, name=name, grid=(m // tm, n // tn, nk), in_specs=in_specs, out_specs=out_spec,
               out_shape=out_shape, scratch=[] if nk == 1 else [pltpu.VMEM((tm, tn), F32)],
               vmem=block_bytes(tk) + (0 if nk == 1 else _nbytes((tm, tn), F32) // 2), dep=dep)(*args)


def _rms_fwd(x, g, *, name):
    r, d = x.shape
    tm = _pick(r, (256, 128))

    def body(x_ref, g_ref, o_ref):
        xv = x_ref[...]
        rstd = lax.rsqrt(jnp.mean(xv * xv, axis=1, keepdims=True) + NORM_EPS)
        o_ref[...] = (xv * rstd * g_ref[...]).astype(o_ref.dtype)

    return _pc(body, name=name, grid=(r // tm,),
               in_specs=[pl.BlockSpec((tm, d), lambda i: (i, 0)), pl.BlockSpec((1, d), lambda i: (0, 0))],
               out_specs=pl.BlockSpec((tm, d), lambda i: (i, 0)), out_shape=jax.ShapeDtypeStruct((r, d), BF16),
               vmem=tm * d * 6)(x, g.reshape(1, d))


def _rms_bwd(x, g, dxn, dres, *, name, dep=None):
    r, d = x.shape
    tm = _pick(r, (256, 128))

    def body(x_ref, g_ref, dxn_ref, dres_ref, dx_ref, dg_ref, db_ref):
        @pl.when(pl.program_id(0) == 0)
        def _():
            dg_ref[...] = jnp.zeros_like(dg_ref)

        xv = x_ref[...]
        rstd = lax.rsqrt(jnp.mean(xv * xv, axis=1, keepdims=True) + NORM_EPS)
        xh = xv * rstd
        dy = dxn_ref[...]
        dg_ref[...] += jnp.sum(dy * xh, axis=0, keepdims=True)
        dyg = dy * g_ref[...]
        dx_ref[...] = dres_ref[...] + rstd * (dyg - xh * jnp.mean(dyg * xh, axis=1, keepdims=True))

        db_ref[...] = _bf(dx_ref[...])

    row = pl.BlockSpec((tm, d), lambda i: (i, 0))
    vec = pl.BlockSpec((1, d), lambda i: (0, 0))
    return _pc(body, name=name, grid=(r // tm,), in_specs=[row, vec, row, row], out_specs=[row, vec, row],
               out_shape=[jax.ShapeDtypeStruct((r, d), F32), jax.ShapeDtypeStruct((1, d), F32),
                          jax.ShapeDtypeStruct((r, d), BF16)],
               vmem=tm * d * 18, dep=dep)(x, g.reshape(1, d), dxn, dres)


def _rms_bwd_first(x, g, dxn, dres, bsz, nc, *, name):
    d = x.shape[1]

    def body(x_ref, g_ref, dxn_ref, dres_ref, gx_ref, d0_ref, dg_ref):
        b, c = pl.program_id(0), pl.program_id(1)

        @pl.when((b == 0) & (c == 0))
        def _():
            dg_ref[...] = jnp.zeros_like(dg_ref)

        xv = x_ref[...]
        rstd = lax.rsqrt(jnp.mean(xv * xv, axis=1, keepdims=True) + NORM_EPS)
        xh = xv * rstd
        dy = dxn_ref[...]
        dg_ref[...] += jnp.sum(dy * xh, axis=0, keepdims=True)
        dyg = dy * g_ref[...]
        dx = dres_ref[...] + rstd * (dyg - xh * jnp.mean(dyg * xh, axis=1, keepdims=True))

        @pl.when(c == 0)
        def _():
            d0_ref[0] = dx

        @pl.when(c > 0)
        def _():
            gx_ref[0] = dx

    row = pl.BlockSpec((CHUNK, d), lambda b, c: (b * nc + c, 0))
    vec = pl.BlockSpec((1, d), lambda b, c: (0, 0))
    return _pc(body, name=name, grid=(bsz, nc), in_specs=[row, vec, row, row],
               out_specs=[pl.BlockSpec((1, CHUNK, d), lambda b, c: (b, jnp.maximum(c - 1, 0), 0)),
                          pl.BlockSpec((1, CHUNK, d), lambda b, c: (b, 0, 0)), vec],
               out_shape=[jax.ShapeDtypeStruct((bsz, (nc - 1) * CHUNK, d), F32),
                          jax.ShapeDtypeStruct((bsz, CHUNK, d), F32), jax.ShapeDtypeStruct((1, d), F32)],
               vmem=CHUNK * d * 24)(x, g.reshape(1, d), dxn, dres)


def _final_loss(h, g, target, bsz, nc):
    d = h.shape[1]

    def body(h_ref, g_ref, t_ref, loss_ref, dh_ref, dg_ref, db_ref):
        b, c = pl.program_id(0), pl.program_id(1)

        @pl.when((b == 0) & (c == 0))
        def _():
            loss_ref[...] = jnp.zeros_like(loss_ref)
            dg_ref[...] = jnp.zeros_like(dg_ref)

        @pl.when(c == 0)
        def _():
            dh_ref[...] = jnp.zeros_like(dh_ref)
            db_ref[...] = jnp.zeros_like(db_ref)

        @pl.when(c > 0)
        def _():
            xv = h_ref[...]
            rstd = lax.rsqrt(jnp.mean(xv * xv, axis=1, keepdims=True) + NORM_EPS)
            xh = xv * rstd
            gv = g_ref[...]
            err = xh * gv - t_ref[0]
            loss_ref[...] += 0.5 * jnp.sum(jnp.mean(err * err, axis=1, keepdims=True))
            dy = err * (1.0 / d)
            dg_ref[...] += jnp.sum(dy * xh, axis=0, keepdims=True)
            dyg = dy * gv
            dh = rstd * (dyg - xh * jnp.mean(dyg * xh, axis=1, keepdims=True))
            dh_ref[...] = dh
            db_ref[...] = _bf(dh)

    row = pl.BlockSpec((CHUNK, d), lambda b, c: (b * nc + c, 0))
    vec = pl.BlockSpec((1, d), lambda b, c: (0, 0))
    return _pc(body, name="final_loss", grid=(bsz, nc),
               in_specs=[row, vec, pl.BlockSpec((1, CHUNK, d), lambda b, c: (b, jnp.maximum(c - 1, 0), 0))],
               out_specs=[pl.BlockSpec((8, 128), lambda b, c: (0, 0)), row, vec, row],
               out_shape=[jax.ShapeDtypeStruct((8, 128), F32), jax.ShapeDtypeStruct(h.shape, F32),
                          jax.ShapeDtypeStruct((1, d), F32), jax.ShapeDtypeStruct(h.shape, BF16)],
               vmem=CHUNK * d * 18)(h, g.reshape(1, d), target)


def _adamw(w, m, v, gparts, *, name):
    r, c = w.shape
    tr = _pick(r, (256, 128)) if r * c * 4 > (1 << 20) else r

    def body(w_ref, m_ref, v_ref, gp_ref, g_ref, d_ref, nm_ref, nv_ref):
        g = gp_ref[0].astype(F32)
        for j in range(1, N_DEV):
            g = g + gp_ref[j].astype(F32)
        mm = ADAM_B1 * m_ref[...] + (1.0 - ADAM_B1) * g
        vv = ADAM_B2 * v_ref[...] + (1.0 - ADAM_B2) * (g * g)
        m_hat = mm / (1.0 - ADAM_B1 ** ADAM_STEP)
        v_hat = vv / (1.0 - ADAM_B2 ** ADAM_STEP)
        g_ref[...] = g
        d_ref[...] = -ADAM_LR * (m_hat / (jnp.sqrt(v_hat) + ADAM_EPS) + ADAM_WD * w_ref[...])
        nm_ref[...] = mm
        nv_ref[...] = vv

    blk = pl.BlockSpec((tr, c), lambda i: (i, 0))
    out = jax.ShapeDtypeStruct((r, c), F32)
    return _pc(body, name=name, grid=(r // tr,),
               in_specs=[blk, blk, blk, pl.BlockSpec((N_DEV, tr, c), lambda i: (0, i, 0))],
               out_specs=[blk, blk, blk, blk], out_shape=[out, out, out, out],
               vmem=tr * c * (4 * 7 + N_DEV * jnp.dtype(gparts.dtype).itemsize))(w, m, v, gparts)


PEERS = (1, 2, 4, 6, 3, 5, 7)
HBM_SPEC = pl.BlockSpec(memory_space=pltpu.HBM)
SEM_SPEC = pl.BlockSpec(memory_space=pltpu.SEMAPHORE)
SIDE_EFFECT = pltpu.SideEffectType.DATAFLOW_SIDE_EFFECTING


def _peer(p):
    x, y, c = lax.axis_index("x"), lax.axis_index("y"), lax.axis_index("c")
    tx, ty, tc = x ^ ((p >> 2) & 1), y ^ ((p >> 1) & 1), c ^ (p & 1)
    return (tx, ty, tc), 4 * tx + 2 * ty + tc


def _place_own(a, kind, *, name):
    rows, cols = a.shape[-2:]
    small = _nbytes((rows, cols), a.dtype) <= (2 << 20)
    tr = rows if small else _pick(rows, (512, 256, 128, 64, 32, 16))
    me = (4 * lax.axis_index("x") + 2 * lax.axis_index("y") + lax.axis_index("c")).astype(jnp.int32).reshape(1)

    def body(me_ref, in_ref, out_ref):
        out_ref[...] = in_ref[...].reshape(out_ref.shape)

    if kind == "a2a":
        in_spec = pl.BlockSpec((1, tr, cols), lambda i, me_ref: (me_ref[0], i, 0))
    else:
        in_spec = pl.BlockSpec((tr, cols), lambda i, me_ref: (i, 0))
    return pl.pallas_call(
        body, name=name, out_shape=jax.ShapeDtypeStruct((N_DEV, rows, cols), a.dtype),
        grid_spec=pltpu.PrefetchScalarGridSpec(
            num_scalar_prefetch=1, grid=(rows // tr,), in_specs=[in_spec],
            out_specs=pl.BlockSpec((1, tr, cols), lambda i, me_ref: (me_ref[0], i, 0))))(me, a)


def _exchange_copies(ins, lands, send_sems, recv_sems, kinds, incoming, peers=PEERS):
    me = 4 * lax.axis_index("x") + 2 * lax.axis_index("y") + lax.axis_index("c")
    copies = []
    for i, kind in enumerate(kinds):
        for p in peers:
            dev, tgt = _peer(p)
            k = i * (N_DEV - 1) + p - 1
            copies.append(pltpu.make_async_remote_copy(
                src_ref=ins[i].at[tgt] if kind == "a2a" else ins[i], dst_ref=lands[i].at[tgt if incoming else me],
                send_sem=send_sems.at[k], recv_sem=recv_sems.at[k], device_id=dev, device_id_type=MESH))
    return copies


def _exchange_start(arrays, kinds, *, name, dep=None, peers=PEERS):
    n = len(arrays)
    lands = [_place_own(a, k, name=f"{name}_own{i}") for i, (a, k) in enumerate(zip(arrays, kinds))]
    extra = [] if dep is None else [dep]

    def body(*refs):
        ins, lnd = refs[:n], refs[n:2 * n]
        send_sems, recv_sems = refs[2 * n + len(extra)], refs[2 * n + len(extra) + 1]
        token = refs[-1]
        for cp in _exchange_copies(ins, lnd, send_sems, recv_sems, kinds, False, peers):
            cp.start()
        token[...] = jnp.zeros_like(token)

    sem = pltpu.SemaphoreType.DMA((n * (N_DEV - 1),))
    outs = pl.pallas_call(
        body, name=name, in_specs=[HBM_SPEC] * (2 * n) + [ANY_SPEC] * len(extra),
        out_specs=[SEM_SPEC, SEM_SPEC] + [HBM_SPEC] * (2 * n) + [pl.BlockSpec(memory_space=pltpu.VMEM)],
        out_shape=[sem, sem] + [pltpu.HBM(a.shape, a.dtype) for a in arrays + lands]
        + [jax.ShapeDtypeStruct((8, 128), F32)],
        input_output_aliases={i: 2 + i for i in range(2 * n)},
        compiler_params=pltpu.CompilerParams(has_side_effects=SIDE_EFFECT),
    )(*[pltpu.with_memory_space_constraint(a, pltpu.HBM) for a in arrays + lands], *extra)
    return dict(send=outs[0], recv=outs[1], ins=list(outs[2:2 + n]), lands=list(outs[2 + n:2 + 2 * n]),
                token=outs[-1], kinds=kinds, name=name, peers=peers)


def _exchange_wait(h, after):
    n = len(h["ins"])
    kinds = h["kinds"]

    def body(*refs):
        ins, lnd = refs[:n], refs[n:2 * n]
        send_sems, recv_sems = refs[2 * n], refs[2 * n + 1]
        copies = _exchange_copies(ins, lnd, send_sems, recv_sems, kinds, True, h["peers"])
        for cp in copies:
            cp.wait_recv()
        for cp in copies:
            cp.wait_send()

    arrs = h["ins"] + h["lands"]
    after = list(after) if isinstance(after, (list, tuple)) else [after]
    outs = pl.pallas_call(
        body, name=h["name"] + "_wait", in_specs=[HBM_SPEC] * (2 * n) + [SEM_SPEC, SEM_SPEC] + [ANY_SPEC] * len(after),
        out_specs=[HBM_SPEC] * (2 * n), out_shape=[pltpu.HBM(a.shape, a.dtype) for a in arrs],
        input_output_aliases={i: i for i in range(2 * n)},
        compiler_params=pltpu.CompilerParams(has_side_effects=SIDE_EFFECT),
    )(*arrs, h["send"], h["recv"], *after)
    return list(outs[n:])


SAME_CORE = (0, 2, 4, 6)


def _forward_copies(land, send_sems, recv_sems, incoming):
    me = 4 * lax.axis_index("x") + 2 * lax.axis_index("y") + lax.axis_index("c")
    dev, sibling = _peer(1)
    return [pltpu.make_async_remote_copy(
        src_ref=land.at[me ^ q], dst_ref=land.at[(sibling if incoming else me) ^ q],
        send_sem=send_sems.at[j], recv_sem=recv_sems.at[j], device_id=dev, device_id_type=MESH)
        for j, q in enumerate(SAME_CORE)]


def _sibling_forward_start(land, *, name, dep=None):
    extra = [] if dep is None else [dep]

    def body(*refs):
        land_ref, send_sems, recv_sems, token = refs[0], refs[1 + len(extra)], refs[2 + len(extra)], refs[-1]
        for cp in _forward_copies(land_ref, send_sems, recv_sems, False):
            cp.start()
        token[...] = jnp.zeros_like(token)

    sem = pltpu.SemaphoreType.DMA((len(SAME_CORE),))
    outs = pl.pallas_call(
        body, name=name, in_specs=[HBM_SPEC] + [ANY_SPEC] * len(extra),
        out_specs=[SEM_SPEC, SEM_SPEC, HBM_SPEC, pl.BlockSpec(memory_space=pltpu.VMEM)],
        out_shape=[sem, sem, pltpu.HBM(land.shape, land.dtype), jax.ShapeDtypeStruct((8, 128), F32)],
        input_output_aliases={0: 2}, compiler_params=pltpu.CompilerParams(has_side_effects=SIDE_EFFECT),
    )(pltpu.with_memory_space_constraint(land, pltpu.HBM), *extra)
    return dict(send=outs[0], recv=outs[1], land=outs[2], token=outs[3], name=name)


def _sibling_forward_wait(h, after):
    def body(*refs):
        copies = _forward_copies(refs[0], refs[1], refs[2], True)
        for cp in copies:
            cp.wait_recv()
        for cp in copies:
            cp.wait_send()

    return pl.pallas_call(
        body, name=h["name"] + "_wait", in_specs=[HBM_SPEC, SEM_SPEC, SEM_SPEC, ANY_SPEC], out_specs=HBM_SPEC,
        out_shape=pltpu.HBM(h["land"].shape, h["land"].dtype), input_output_aliases={0: 0},
        compiler_params=pltpu.CompilerParams(has_side_effects=SIDE_EFFECT),
    )(h["land"], h["send"], h["recv"], after)


def _s5_params(lam_re, lam_im, log_dt, b_re, b_im):
    dt = jnp.exp(log_dt)[:, None]
    mag = jnp.exp(lam_re * dt)
    ar, ai = mag * jnp.cos(lam_im * dt), mag * jnp.sin(lam_im * dt)
    den = lam_re * lam_re + lam_im * lam_im
    qr = ((ar - 1.0) * lam_re + ai * lam_im) / den
    qi = (ai * lam_re - (ar - 1.0) * lam_im) / den
    bbr = qr[..., None] * b_re - qi[..., None] * b_im
    bbi = qr[..., None] * b_im + qi[..., None] * b_re
    return ar, ai, bbr, bbi


def _s5_power_table(ar, ai):
    pr, pi = ar.reshape(1, -1), ai.reshape(1, -1)
    while pr.shape[0] < 8:
        sr, si = pr[-1:], pi[-1:]
        pr, pi = (jnp.concatenate([pr, pr * sr - pi * si], axis=0), jnp.concatenate([pi, pr * si + pi * sr], axis=0))
    return pr, pi


def _blockdiag(w, rows, cols):
    w = w.reshape(S5_GB, S5_GB, rows, cols)
    eye = jnp.eye(S5_GB, dtype=w.dtype)
    return jnp.einsum("abrc,bd->abrdc", w, eye).reshape(S5_GB, S5_GB * rows, S5_GB * cols)


def _blockdiag_extract(w, rows, cols):
    w = w.reshape(S5_GB, S5_GB, rows, S5_GB, cols)
    return jnp.einsum("abrbc->abrc", w).reshape(S5_GROUPS, rows, cols)


def _s5_scan_specs(bsz, nc, rev):
    def cc(c):
        return (nc - 1 - c) if rev else c

    return dict(
        u=pl.BlockSpec((bsz, CHUNK, CHUNK), lambda g, c: (0, cc(c), g)),
        x=pl.BlockSpec((bsz, CHUNK, S5_LANES), lambda g, c: (0, cc(c), g)),
        wb=pl.BlockSpec((1, CHUNK, S5_LANES), lambda g, c: (g, 0, 0)),
        wc=pl.BlockSpec((1, S5_LANES, CHUNK), lambda g, c: (g, 0, 0)),
        tab=pl.BlockSpec((8, S5_LANES), lambda g, c: (0, g)),
        step=pl.BlockSpec((8, S5_LANES), lambda g, c: (0, g)),
        d=pl.BlockSpec((1, CHUNK), lambda g, c: (0, g)),
        lane=pl.BlockSpec((1, S5_LANES), lambda g, c: (0, g)),
        xprev=pl.BlockSpec((bsz, 8, S5_LANES), lambda g, c: (0, jnp.maximum(cc(c) * (CHUNK // 8) - 1, 0), g)),
    )


def _s5_fwd(u, wbr, wbi, pr, pi, sr, si, wcr, wci, d, bsz, nc):
    r = u.shape[0]
    tp = r // bsz
    sp = _s5_scan_specs(bsz, nc, False)

    def body(u_all, wbr_ref, wbi_ref, pr_ref, pi_ref, sr_ref, si_ref, wcr_ref, wci_ref, d_ref,
             xr_all, xi_all, y1_all, g_all, cr_sall, ci_sall):
        @pl.when(pl.program_id(1) == 0)
        def _():
            cr_sall[...] = jnp.zeros_like(cr_sall)
            ci_sall[...] = jnp.zeros_like(ci_sall)

        for bi in range(bsz):
            one(u_all.at[bi], wbr_ref, wbi_ref, pr_ref, pi_ref, sr_ref, si_ref, wcr_ref, wci_ref, d_ref,
                xr_all.at[bi], xi_all.at[bi], y1_all.at[bi], g_all.at[bi], cr_sall.at[bi], ci_sall.at[bi])

    def one(u_ref, wbr_ref, wbi_ref, pr_ref, pi_ref, sr_ref, si_ref, wcr_ref, wci_ref, d_ref,
            xr_ref, xi_ref, y1_ref, g_ref, cr_s, ci_s):
        uv = u_ref[...]
        ub = _bf(uv)
        xr, xi = _dot(ub, wbr_ref[0]), _dot(ub, wbi_ref[0])
        sub = lax.broadcasted_iota(jnp.int32, (CHUNK, S5_LANES), 0) % 8
        for k in range(3):
            s = 1 << k
            ar, ai = sr_ref[k:k + 1, :], si_ref[k:k + 1, :]
            hr = jnp.where(sub >= s, pltpu.roll(xr, s, 0), 0.0)
            hi = jnp.where(sub >= s, pltpu.roll(xi, s, 0), 0.0)
            xr, xi = xr + (ar * hr - ai * hi), xi + (ar * hi + ai * hr)
        cr, ci = cr_s[...], ci_s[...]
        tr, ti = pr_ref[...], pi_ref[...]
        outr, outi = [], []
        for g8 in range(CHUNK // 8):
            br, bi = xr[8 * g8:8 * g8 + 8, :], xi[8 * g8:8 * g8 + 8, :]
            br, bi = br + (tr * cr - ti * ci), bi + (tr * ci + ti * cr)
            cr, ci = br[7:8, :], bi[7:8, :]
            outr.append(br)
            outi.append(bi)
        xr, xi = jnp.concatenate(outr, axis=0), jnp.concatenate(outi, axis=0)
        cr_s[...] = cr
        ci_s[...] = ci
        xr_ref[...] = xr
        xi_ref[...] = xi
        y = _dot(_bf(xr), wcr_ref[0]) - _dot(_bf(xi), wci_ref[0]) + d_ref[...] * uv
        y1_ref[...] = y
        g_ref[...] = _bf(_gelu_and_grad(y)[0])

    ns = S5_GROUPS * S5_STATE
    xr, xi, y1, g = _pc(
        body, name="s5_fwd", grid=(S5_GB, nc),
        in_specs=[sp["u"], sp["wb"], sp["wb"], sp["tab"], sp["tab"], sp["step"], sp["step"], sp["wc"], sp["wc"],
                  sp["d"]],
        out_specs=[sp["x"], sp["x"], sp["u"], sp["u"]],
        out_shape=[jax.ShapeDtypeStruct((bsz, tp, ns), F32)] * 2
        + [jax.ShapeDtypeStruct((bsz, tp, S5_WIDTH), F32), jax.ShapeDtypeStruct((bsz, tp, S5_WIDTH), BF16)],
        scratch=[pltpu.VMEM((bsz, 1, S5_LANES), F32)] * 2, vmem=8 << 20,
    )(_seq(u, bsz), wbr, wbi, pr, pi, sr, si, wcr, wci, d)
    return xr.reshape(r, ns), xi.reshape(r, ns), y1.reshape(r, S5_WIDTH), g.reshape(r, S5_WIDTH)


def _s5_post(y1, glu_pre, glu_b, z):
    r, w = y1.shape
    tm = _pick(r, (256, 128))

    def body(y_ref, p_ref, b_ref, z_ref, o_ref):
        g = _gelu_and_grad(y_ref[...])[0]
        o_ref[...] = _bf(g * jax.nn.sigmoid(p_ref[...] + b_ref[...]) * _silu(z_ref[...]))

    row = pl.BlockSpec((tm, w), lambda i: (i, 0))
    return _pc(body, name="s5_post", grid=(r // tm,), in_specs=[row, row, pl.BlockSpec((1, w), lambda i: (0, 0)), row],
               out_specs=row, out_shape=jax.ShapeDtypeStruct((r, w), BF16), vmem=tm * w * 16)(y1, glu_pre, glu_b, z)


def _s5_post_bwd(dya, y1, glu_pre, glu_b, z):
    r, w = y1.shape
    tm = _pick(r, (256, 128))

    def body(dy_ref, y_ref, p_ref, b_ref, z_ref, dz_ref, dp_ref, dg_ref, db_ref):
        @pl.when(pl.program_id(0) == 0)
        def _():
            db_ref[...] = jnp.zeros_like(db_ref)

        g = _gelu_and_grad(y_ref[...])[0]
        s = jax.nn.sigmoid(p_ref[...] + b_ref[...])
        zv = z_ref[...]
        dy = dy_ref[...]
        do = dy * _silu(zv)
        dz_ref[...] = _bf(dy * g * s * _dsilu(zv))
        dp = do * g * s * (1.0 - s)
        dp_ref[...] = _bf(dp)
        db_ref[...] += jnp.sum(dp, axis=0, keepdims=True)
        dg_ref[...] = do * s

    row = pl.BlockSpec((tm, w), lambda i: (i, 0))
    vec = pl.BlockSpec((1, w), lambda i: (0, 0))
    return _pc(body, name="s5_post_bwd", grid=(r // tm,), in_specs=[row, row, row, vec, row],
               out_specs=[row, row, row, vec],
               out_shape=[jax.ShapeDtypeStruct((r, w), BF16), jax.ShapeDtypeStruct((r, w), BF16),
                          jax.ShapeDtypeStruct((r, w), F32), jax.ShapeDtypeStruct((1, w), F32)],
               vmem=tm * w * 24)(dya, y1, glu_pre, glu_b, z)


def _s5_bwd(dg, y1, u, xr, xi, wbr, wbi, qr, qi, sr, si, wcr, wci, d, bsz, nc):
    r = u.shape[0]
    tp = r // bsz
    sp = _s5_scan_specs(bsz, nc, True)

    def body(dg_all, y1_all, u_all, xr_all, xi_all, xpr_all, xpi_all, wbr_ref, wbi_ref, qr_ref, qi_ref, sr_ref, si_ref,
             wcr_ref, wci_ref, d_ref, du_all, dd_ref, dwcr_ref, dwci_ref, dwbr_ref, dwbi_ref, dar_ref, dai_ref,
             cr_sall, ci_sall):
        c = pl.program_id(1)

        @pl.when(c == 0)
        def _():
            for ref in (dd_ref, dwcr_ref, dwci_ref, dwbr_ref, dwbi_ref, dar_ref, dai_ref, cr_sall, ci_sall):
                ref[...] = jnp.zeros_like(ref)

        for bi in range(bsz):
            one(c, dg_all.at[bi], y1_all.at[bi], u_all.at[bi], xr_all.at[bi], xi_all.at[bi], xpr_all.at[bi],
                xpi_all.at[bi], wbr_ref, wbi_ref, qr_ref, qi_ref, sr_ref, si_ref, wcr_ref, wci_ref, d_ref,
                du_all.at[bi], dd_ref, dwcr_ref, dwci_ref, dwbr_ref, dwbi_ref, dar_ref, dai_ref, cr_sall.at[bi],
                ci_sall.at[bi])

    def one(c, dg_ref, y1_ref, u_ref, xr_ref, xi_ref, xpr_ref, xpi_ref, wbr_ref, wbi_ref, qr_ref, qi_ref, sr_ref, si_ref,
            wcr_ref, wci_ref, d_ref, du_ref, dd_ref, dwcr_ref, dwci_ref, dwbr_ref, dwbi_ref, dar_ref, dai_ref,
            cr_s, ci_s):
        uv = u_ref[...]
        ub = _bf(uv)
        dy = dg_ref[...] * _gelu_and_grad(y1_ref[...])[1]
        dd_ref[...] += jnp.sum(dy * uv, axis=0, keepdims=True)
        dyb = _bf(dy)
        xr, xi = xr_ref[...], xi_ref[...]
        dwcr_ref[0] += _dot(_bf(xr), dyb, TN)
        dwci_ref[0] -= _dot(_bf(xi), dyb, TN)
        lr, li = _dot(dyb, wcr_ref[0], NT), -_dot(dyb, wci_ref[0], NT)
        row = lax.broadcasted_iota(jnp.int32, (CHUNK, S5_LANES), 0)
        sub = row % 8
        for k in range(3):
            s = 1 << k
            ar, ai = sr_ref[k:k + 1, :], si_ref[k:k + 1, :]
            hr = jnp.where(sub < 8 - s, pltpu.roll(lr, CHUNK - s, 0), 0.0)
            hi = jnp.where(sub < 8 - s, pltpu.roll(li, CHUNK - s, 0), 0.0)
            lr, li = lr + (ar * hr + ai * hi), li + (ar * hi - ai * hr)
        cr, ci = cr_s[...], ci_s[...]
        tr, ti = qr_ref[...], qi_ref[...]
        outr, outi = [], []
        for g8 in reversed(range(CHUNK // 8)):
            br, bi = lr[8 * g8:8 * g8 + 8, :], li[8 * g8:8 * g8 + 8, :]
            br, bi = br + (tr * cr + ti * ci), bi + (tr * ci - ti * cr)
            cr, ci = br[0:1, :], bi[0:1, :]
            outr.append(br)
            outi.append(bi)
        lr, li = jnp.concatenate(outr[::-1], axis=0), jnp.concatenate(outi[::-1], axis=0)
        cr_s[...] = cr
        ci_s[...] = ci
        lrb, lib = _bf(lr), _bf(li)
        du_ref[...] = _bf(_dot(lrb, wbr_ref[0], NT) + _dot(lib, wbi_ref[0], NT) + dy * d_ref[...])
        dwbr_ref[0] += _dot(ub, lrb, TN)
        dwbi_ref[0] += _dot(ub, lib, TN)
        first = c == nc - 1
        pr0 = jnp.where(first, 0.0, xpr_ref[7:8, :])
        pi0 = jnp.where(first, 0.0, xpi_ref[7:8, :])
        xpr = jnp.where(row == 0, pr0, pltpu.roll(xr, 1, 0))
        xpi = jnp.where(row == 0, pi0, pltpu.roll(xi, 1, 0))
        dar_ref[...] += jnp.sum(lr * xpr + li * xpi, axis=0, keepdims=True)
        dai_ref[...] += jnp.sum(li * xpr - lr * xpi, axis=0, keepdims=True)

    st = jax.ShapeDtypeStruct
    xr3, xi3 = _seq(xr, bsz), _seq(xi, bsz)
    outs = _pc(body, name="s5_bwd", grid=(S5_GB, nc),
               in_specs=[sp["u"], sp["u"], sp["u"], sp["x"], sp["x"], sp["xprev"], sp["xprev"], sp["wb"], sp["wb"],
                         sp["tab"], sp["tab"], sp["step"], sp["step"], sp["wc"], sp["wc"], sp["d"]],
               out_specs=[sp["u"], sp["d"], sp["wc"], sp["wc"], sp["wb"], sp["wb"], sp["lane"], sp["lane"]],
               out_shape=[st((bsz, tp, S5_WIDTH), BF16), st((1, S5_WIDTH), F32),
                          st((S5_GB, S5_LANES, CHUNK), F32), st((S5_GB, S5_LANES, CHUNK), F32),
                          st((S5_GB, CHUNK, S5_LANES), F32), st((S5_GB, CHUNK, S5_LANES), F32),
                          st((1, S5_GROUPS * S5_STATE), F32), st((1, S5_GROUPS * S5_STATE), F32)],
               scratch=[pltpu.VMEM((bsz, 1, S5_LANES), F32)] * 2, vmem=12 << 20,
               )(_seq(dg, bsz), _seq(y1, bsz), _seq(u, bsz), xr3, xi3, xr3, xi3, wbr, wbi, qr, qi, sr, si, wcr, wci, d)
    return (outs[0].reshape(r, S5_WIDTH),) + tuple(outs[1:])


def _s5_layer_fwd(u, prm, glu_w, bsz, nc):
    xr, xi, y1, g = _s5_fwd(u, prm["wbr"], prm["wbi"], prm["pr"], prm["pi"], prm["sr"], prm["si"], prm["wcr"],
                            prm["wci"], prm["d"], bsz, nc)
    glu_pre = _mm(g, glu_w(y1) if callable(glu_w) else glu_w, "NN", name="s5_glu")
    return dict(xr=xr, xi=xi, y1=y1, g=g, glu_pre=glu_pre)


def _s5_layer_bwd(dya, u, z, sv, prm, pvjp, glu_w, glu_b, bsz, nc):
    dz, dglu, dg_direct, dglu_b = _s5_post_bwd(dya, sv["y1"], sv["glu_pre"], glu_b, z)
    dg = _mm(dglu, glu_w, "NT", name="s5_dg", add=dg_direct)
    dglu_w = _mm(sv["g"], dglu, "TN", name="s5_dglu_w")
    du, dd, dwcr, dwci, dwbr, dwbi, dar, dai = _s5_bwd(
        dg, sv["y1"], u, sv["xr"], sv["xi"], prm["wbr"], prm["wbi"], prm["qr"], prm["qi"], prm["sr"], prm["si"],
        prm["wcr"], prm["wci"], prm["d"], bsz, nc)
    dbbr = jnp.swapaxes(_blockdiag_extract(dwbr, S5_GROUP_SIZE, S5_STATE), 1, 2)
    dbbi = jnp.swapaxes(_blockdiag_extract(dwbi, S5_GROUP_SIZE, S5_STATE), 1, 2)
    dlr, dli, dldt, dbr, dbi = pvjp((dar.reshape(S5_GROUPS, S5_STATE), dai.reshape(S5_GROUPS, S5_STATE), dbbr, dbbi))
    grads = dict(
        s5_lambda_re=dlr, s5_lambda_im=dli, s5_log_dt=dldt, s5_b_re=dbr, s5_b_im=dbi,
        s5_c_re=jnp.swapaxes(_blockdiag_extract(dwcr, S5_STATE, S5_GROUP_SIZE), 1, 2),
        s5_c_im=jnp.swapaxes(_blockdiag_extract(dwci, S5_STATE, S5_GROUP_SIZE), 1, 2),
        s5_d=dd, s5_glu_w=dglu_w, s5_glu_b=dglu_b)
    return du, dz, grads


def _s5_tables(lam_re, lam_im, log_dt, b_re, b_im, c_re, c_im, d):
    (ar, ai, bbr, bbi), vjp = jax.vjp(_s5_params, lam_re, lam_im, log_dt, b_re, b_im)
    pr, pi = _s5_power_table(lax.stop_gradient(ar), lax.stop_gradient(ai))
    steps = [0, 1, 3, 7, 7, 7, 7, 7]
    prm = dict(
        wbr=_bf(_blockdiag(jnp.swapaxes(bbr, 1, 2), S5_GROUP_SIZE, S5_STATE)),
        wbi=_bf(_blockdiag(jnp.swapaxes(bbi, 1, 2), S5_GROUP_SIZE, S5_STATE)),
        wcr=_bf(_blockdiag(jnp.swapaxes(c_re, 1, 2), S5_STATE, S5_GROUP_SIZE)),
        wci=_bf(_blockdiag(jnp.swapaxes(c_im, 1, 2), S5_STATE, S5_GROUP_SIZE)),
        pr=pr, pi=pi, qr=pr[::-1], qi=pi[::-1],
        sr=jnp.concatenate([pr[i:i + 1] for i in steps], axis=0),
        si=jnp.concatenate([pi[i:i + 1] for i in steps], axis=0), d=d.reshape(1, S5_WIDTH))
    return prm, vjp


def _tile16(p8):
    return jnp.concatenate([p8] * (CHUNK // 8), axis=0)


def _shift_down(x, halo, s, row):
    return jnp.where(row >= s, pltpu.roll(x, s, 0), pltpu.roll(halo, s, 0))


def _shift_up(x, halo, s, row):
    return jnp.where(row < CHUNK - s, pltpu.roll(x, CHUNK - s, 0), pltpu.roll(halo, CHUNK - s, 0))


def _conv_specs(nc, tw):
    def chunk(b, c):
        return b * nc + c

    return dict(
        x=pl.BlockSpec((CHUNK, tw), lambda j, b, c: (chunk(b, c), j)),
        prev=pl.BlockSpec((8, tw), lambda j, b, c: (jnp.maximum(chunk(b, c) * (CHUNK // 8) - 1, 0), j)),
        nxt=pl.BlockSpec((8, tw), lambda j, b, c: ((b * nc + jnp.minimum(c + 1, nc - 1)) * (CHUNK // 8), j)),
        w=pl.BlockSpec((ML_CONV, tw), lambda j, b, c: (0, j)),
        vec=pl.BlockSpec((1, tw), lambda j, b, c: (0, j)),
    )


def _conv_fwd(x, w, bias, bsz, nc, *, name):
    r, wd = x.shape
    tw = _pick(wd, (2048, 1536, 1024, 512, 384, 256, 128))
    sp = _conv_specs(nc, tw)

    def body(x_ref, p_ref, w_ref, b_ref, o_ref):
        c = pl.program_id(2)
        xv = x_ref[...]
        row = lax.broadcasted_iota(jnp.int32, xv.shape, 0)
        halo = jnp.where(c == 0, 0.0, _tile16(p_ref[...]))
        acc = b_ref[...] + w_ref[3:4, :] * xv
        for s in (1, 2, 3):
            acc = acc + w_ref[3 - s:4 - s, :] * _shift_down(xv, halo, s, row)
        o_ref[...] = acc

    return _pc(body, name=name, grid=(wd // tw, bsz, nc), in_specs=[sp["x"], sp["prev"], sp["w"], sp["vec"]],
               out_specs=sp["x"], out_shape=jax.ShapeDtypeStruct((r, wd), F32), vmem=CHUNK * tw * 16,
               )(x, x, w, bias.reshape(1, wd))


def _conv_bwd(dpre, x, w, bsz, nc, *, name, add=None):
    r, wd = x.shape
    tw = _pick(wd, (2048, 1536, 1024, 512, 384, 256, 128))
    sp = _conv_specs(nc, tw)

    def body(*refs):
        d_ref, n_ref, x_ref, p_ref, w_ref = refs[:5]
        add_ref = refs[5] if add is not None else None
        dx_ref, dw_ref, db_ref = refs[-3:]
        b, c = pl.program_id(1), pl.program_id(2)

        @pl.when((b == 0) & (c == 0))
        def _():
            dw_ref[...] = jnp.zeros_like(dw_ref)
            db_ref[...] = jnp.zeros_like(db_ref)

        dv, xv = d_ref[...], x_ref[...]
        row = lax.broadcasted_iota(jnp.int32, xv.shape, 0)
        dhalo = jnp.where(c == nc - 1, 0.0, _tile16(n_ref[...]))
        xhalo = jnp.where(c == 0, 0.0, _tile16(p_ref[...]))
        dx = w_ref[3:4, :] * dv
        for s in (1, 2, 3):
            dx = dx + w_ref[3 - s:4 - s, :] * _shift_up(dv, dhalo, s, row)
        if add_ref is not None:
            dx = dx + add_ref[...]
        dx_ref[...] = _bf(dx)
        db_ref[...] += jnp.sum(dv, axis=0, keepdims=True)
        dw_ref[3:4, :] += jnp.sum(dv * xv, axis=0, keepdims=True)
        for s in (1, 2, 3):
            dw_ref[3 - s:4 - s, :] += jnp.sum(dv * _shift_down(xv, xhalo, s, row), axis=0, keepdims=True)

    ins = [dpre, dpre, x, x, w] + ([add] if add is not None else [])
    specs = [sp["x"], sp["nxt"], sp["x"], sp["prev"], sp["w"]] + ([sp["x"]] if add is not None else [])
    return _pc(body, name=name, grid=(wd // tw, bsz, nc), in_specs=specs, out_specs=[sp["x"], sp["w"], sp["vec"]],
               out_shape=[jax.ShapeDtypeStruct((r, wd), BF16), jax.ShapeDtypeStruct((ML_CONV, wd), F32),
                          jax.ShapeDtypeStruct((1, wd), F32)], vmem=CHUNK * tw * 24)(*ins)


ML_SCALE = ML_DH ** -0.5


def _headwise_expand(w):
    tiled = jnp.tile(w.reshape(ML_HEADS, ML_DH, QKV_BLOCK), (1, 1, ML_DH // QKV_BLOCK))
    blk = jnp.arange(ML_DH) // QKV_BLOCK
    return jnp.where(blk[:, None] == blk[None, :], tiled, 0.0)


def _headwise_extract(w):
    return w[:, :, :QKV_BLOCK].reshape(ML_HEADS * ML_DH // QKV_BLOCK, QKV_BLOCK, QKV_BLOCK)


def _ml_pre(pre, x, wq, wk, wv, wgq, wgk, wgv, bsz, nc):
    r = x.shape[0]
    tr = _pick(r, (256, 128))
    hrow = pl.BlockSpec((tr, ML_DH), lambda h, i: (i, h))
    wexp = pl.BlockSpec((1, ML_DH, ML_DH), lambda h, i: (h, 0, 0))
    wg = pl.BlockSpec((ML_DH, CHUNK), lambda h, i: (h, 0))

    def body(pre_ref, x_ref, wq_ref, wk_ref, wv_ref, gq_ref, gk_ref, gv_ref, q_ref, qs_ref, k_ref, v_ref, gt_ref):
        xcb = _bf(_silu(pre_ref[...]))
        q = _dot(xcb, wq_ref[0])
        k = _dot(xcb, wk_ref[0])
        v = _dot(_bf(x_ref[...]), wv_ref[0])
        qb, kb, vb = _bf(q), _bf(k), _bf(v)
        q_ref[...] = qb
        qs_ref[...] = _bf(q * ML_SCALE)
        k_ref[...] = kb
        v_ref[...] = vb
        gt_ref[0] = _dot(qb, gq_ref[...]) + _dot(kb, gk_ref[...]) + _dot(vb, gv_ref[...])

    o = jax.ShapeDtypeStruct((r, ML_WIDTH), BF16)
    q, qs, k, v, gates8 = _pc(
        body, name="ml_pre", grid=(ML_HEADS, r // tr),
        in_specs=[hrow, hrow, wexp, wexp, wexp, wg, wg, wg],
        out_specs=[hrow, hrow, hrow, hrow, pl.BlockSpec((1, tr, CHUNK), lambda h, i: (h, i, 0))],
        out_shape=[o, o, o, o, jax.ShapeDtypeStruct((ML_HEADS, r, CHUNK), F32)], vmem=6 << 20,
    )(pre, x, wq, wk, wv, wgq, wgk, wgv)

    def sum_body(g_ref, o_ref):
        acc = g_ref[0]
        for j in range(1, ML_HEADS):
            acc = acc + g_ref[j]
        o_ref[...] = acc

    gates = _pc(sum_body, name="ml_gates_sum", grid=(r // tr,),
                in_specs=[pl.BlockSpec((ML_HEADS, tr, CHUNK), lambda i: (0, i, 0))],
                out_specs=pl.BlockSpec((tr, CHUNK), lambda i: (i, 0)),
                out_shape=jax.ShapeDtypeStruct((r, CHUNK), F32), vmem=2 << 20)(gates8)
    return q, qs, k, v, gates


def _tri(rev):
    r = lax.broadcasted_iota(jnp.int32, (CHUNK, CHUNK), 0)
    c = lax.broadcasted_iota(jnp.int32, (CHUNK, CHUNK), 1)
    return jnp.where((c >= r) if rev else (c <= r), 1.0, 0.0).astype(F32)


def _cumsum_rows(x, row, rev=False):
    for k in range(7):
        s = 1 << k
        if rev:
            x = x + jnp.where(row < CHUNK - s, pltpu.roll(x, CHUNK - s, 0), 0.0)
        else:
            x = x + jnp.where(row >= s, pltpu.roll(x, s, 0), 0.0)
    return x


def _log_sigmoid(x):
    return jnp.minimum(x, 0.0) - jnp.log(1.0 + jnp.exp(-jnp.abs(x)))


def _ml_core(gates, hd, first, m, qs, k, v, cmat, nvec):
    sq = (CHUNK, CHUNK)
    lane = lax.broadcasted_iota(jnp.int32, sq, 1)
    row = lax.broadcasted_iota(jnp.int32, sq, 0)
    igc = jnp.sum(jnp.where(lane == hd, gates, 0.0), axis=1, keepdims=True)
    fpc = jnp.sum(jnp.where(lane == hd + ML_HEADS, gates, 0.0), axis=1, keepdims=True)
    valid = jnp.logical_or(jnp.logical_not(first), row[:, :1] >= PAD_ROWS)
    igc = jnp.where(valid, igc, NEG)
    lfc = jnp.where(valid, _log_sigmoid(fpc), 0.0)
    bcb = _cumsum_rows(jnp.broadcast_to(lfc, sq), row)
    igb = jnp.broadcast_to(igc, sq)
    dm = jnp.where(lane <= row, bcb - (bcb - igb).T, NEG)
    bc = bcb[:, :1]
    inter = bc + m
    mt = jnp.maximum(inter, jnp.max(dm, axis=1, keepdims=True))
    wt = jnp.exp(dm - mt)
    wprev = jnp.exp(inter - mt)
    s0 = _dot(qs, k, NT)
    s = s0 * wt
    cb = _bf(cmat)
    qc = _dot(qs, cb)
    qf = qs.astype(F32)
    qn = jnp.sum(qf * nvec, axis=1, keepdims=True)
    num = _dot(_bf(s), v) + wprev * qc
    den = jnp.sum(s, axis=1, keepdims=True) + wprev * qn
    emt = jnp.exp(-mt)
    dd = jnp.maximum(jnp.abs(den), emt)
    blast = bcb[CHUNK - 1:CHUNK, :1]
    g = blast - bc + igc
    m_new = jnp.maximum(blast + m, jnp.max(g, axis=0, keepdims=True))
    decay = jnp.exp(blast + m - m_new)
    e = jnp.exp(g - m_new)
    kf = k.astype(F32)
    wk = e * kf
    return dict(lane=lane, row=row, fpc=fpc, valid=valid, wt=wt, wprev=wprev, s=s, cb=cb, qc=qc, qf=qf, qn=qn,
                num=num, den=den, emt=emt, dd=dd, m_new=m_new, decay=decay, e=e, kf=kf, wk=wk)


def _ml_headnorm(h):
    mu = jnp.mean(h, axis=1, keepdims=True)
    hc = h - mu
    rstd = lax.rsqrt(jnp.mean(hc * hc, axis=1, keepdims=True) + HEAD_NORM_EPS)
    return hc * rstd, rstd


def _ml_chunk_specs(nc, rev, bsz):
    def cc(c):
        return (nc - 1 - c) if rev else c

    return dict(
        hrow=pl.BlockSpec((bsz, CHUNK, ML_DH), lambda hd, c: (0, cc(c), hd)),
        gates=pl.BlockSpec((bsz, CHUNK, CHUNK), lambda hd, c: (0, cc(c), 0)),
        bias=pl.BlockSpec((1, CHUNK), lambda hd, c: (0, 0)),
        hvec=pl.BlockSpec((1, ML_DH), lambda hd, c: (0, hd)),
        cs=pl.BlockSpec((bsz, 1, ML_DH, ML_DH), lambda hd, c: (0, hd * nc + cc(c), 0, 0)),
        ns=pl.BlockSpec((bsz, 1, 1, ML_DH), lambda hd, c: (0, hd * nc + cc(c), 0, 0)),
        ms=pl.BlockSpec((bsz, 1, 1, CHUNK), lambda hd, c: (0, hd * nc + cc(c), 0, 0)),
        dgates=pl.BlockSpec((1, bsz, CHUNK, CHUNK), lambda hd, c: (hd, 0, cc(c), 0)),
    )


def _seq(a, bsz):
    return a.reshape(bsz, a.shape[0] // bsz, a.shape[1])


def _ml_chunk_fwd(qs, k, v, gates, b_gate, pre, z, nw, sk, bsz, nc):
    r = qs.shape[0]
    tp = r // bsz
    sp = _ml_chunk_specs(nc, False, bsz)

    def body(qs_all, k_all, v_all, gt_all, bg_ref, pre_all, z_all, nw_ref, sk_ref,
             h_all, yb_all, cs_all, ns_all, ms_all, c_sall, n_sall, m_sall):
        hd, c = pl.program_id(0), pl.program_id(1)

        @pl.when(c == 0)
        def _():
            c_sall[...] = jnp.zeros_like(c_sall)
            n_sall[...] = jnp.zeros_like(n_sall)
            m_sall[...] = jnp.zeros_like(m_sall)

        for bi in range(bsz):
            one(hd, c, qs_all.at[bi], k_all.at[bi], v_all.at[bi], gt_all.at[bi], bg_ref, pre_all.at[bi], z_all.at[bi],
                nw_ref, sk_ref, h_all.at[bi], yb_all.at[bi], cs_all.at[bi], ns_all.at[bi], ms_all.at[bi],
                c_sall.at[bi], n_sall.at[bi], m_sall.at[bi])

    def one(hd, c, qs_ref, k_ref, v_ref, gt_ref, bg_ref, pre_ref, z_ref, nw_ref, sk_ref,
            h_ref, yb_ref, cs_ref, ns_ref, ms_ref, c_s, n_s, m_s):
        cmat, nvec, m = c_s[...], n_s[...], m_s[...]
        cs_ref[0] = cmat
        ns_ref[0] = nvec
        ms_ref[0] = jnp.broadcast_to(m, (1, CHUNK))
        v_ = v_ref[...]
        co = _ml_core(gt_ref[...] + bg_ref[...], hd, c == 0, m, qs_ref[...], k_ref[...], v_, cmat, nvec)
        h = co["num"] / co["dd"]
        h_ref[...] = h
        hn, _ = _ml_headnorm(h)
        yb_ref[...] = _bf((hn * nw_ref[...] + sk_ref[...] * _silu(pre_ref[...])) * _silu(z_ref[...]))
        c_s[...] = co["decay"] * cmat + _dot(_bf(co["wk"]), v_, TN)
        n_s[...] = co["decay"] * nvec + jnp.sum(co["wk"], axis=0, keepdims=True)
        m_s[...] = co["m_new"]

    nst = ML_HEADS * nc
    h, yb, cs, ns, ms = _pc(
        body, name="ml_chunk_fwd", grid=(ML_HEADS, nc),
        in_specs=[sp["hrow"]] * 3 + [sp["gates"], sp["bias"], sp["hrow"], sp["hrow"], sp["hvec"], sp["hvec"]],
        out_specs=[sp["hrow"], sp["hrow"], sp["cs"], sp["ns"], sp["ms"]],
        out_shape=[jax.ShapeDtypeStruct((bsz, tp, ML_WIDTH), F32), jax.ShapeDtypeStruct((bsz, tp, ML_WIDTH), BF16),
                   jax.ShapeDtypeStruct((bsz, nst, ML_DH, ML_DH), F32),
                   jax.ShapeDtypeStruct((bsz, nst, 1, ML_DH), F32), jax.ShapeDtypeStruct((bsz, nst, 1, CHUNK), F32)],
        scratch=[pltpu.VMEM((bsz, ML_DH, ML_DH), F32), pltpu.VMEM((bsz, 1, ML_DH), F32),
                 pltpu.VMEM((bsz, 1, 1), F32)],
        vmem=12 << 20)(*[_seq(a, bsz) for a in (qs, k, v, gates)], b_gate, _seq(pre, bsz), _seq(z, bsz), nw, sk)
    return h.reshape(r, ML_WIDTH), yb.reshape(r, ML_WIDTH), cs, ns, ms


def _ml_chunk_bwd(dyb, qs, k, v, gates, b_gate, pre, z, nw, sk, h, cs, ns, ms, bsz, nc, dep=None):
    r = qs.shape[0]
    tp = r // bsz
    sp = _ml_chunk_specs(nc, True, bsz)

    def body(dy_all, qs_all, k_all, v_all, gt_all, bg_ref, pre_all, z_all, nw_ref, sk_ref, h_all, cs_all, ns_all,
             ms_all, dq_all, dk_all, dv_all, dz_all, dxc_all, dgt_all, dnw_ref, dsk_ref, dc_sall, dn_sall):
        hd, c = pl.program_id(0), pl.program_id(1)

        @pl.when(c == 0)
        def _():
            for ref in (dnw_ref, dsk_ref, dc_sall, dn_sall):
                ref[...] = jnp.zeros_like(ref)

        for bi in range(bsz):
            one(hd, c, dy_all.at[bi], qs_all.at[bi], k_all.at[bi], v_all.at[bi], gt_all.at[bi], bg_ref,
                pre_all.at[bi], z_all.at[bi], nw_ref, sk_ref, h_all.at[bi], cs_all.at[bi], ns_all.at[bi],
                ms_all.at[bi], dq_all.at[bi], dk_all.at[bi], dv_all.at[bi], dz_all.at[bi], dxc_all.at[bi],
                dgt_all.at[0, bi], dnw_ref, dsk_ref, dc_sall.at[bi], dn_sall.at[bi])

    def one(hd, c, dy_ref, qs_ref, k_ref, v_ref, gt_ref, bg_ref, pre_ref, z_ref, nw_ref, sk_ref, h_ref, cs_ref, ns_ref,
            ms_ref, dq_ref, dk_ref, dv_ref, dz_ref, dxc_ref, dgt_ref, dnw_ref, dsk_ref, dc_s, dn_s):

        qs, k, v = qs_ref[...], k_ref[...], v_ref[...]
        cmat, nvec, m = cs_ref[0], ns_ref[0], ms_ref[0][:, :1]
        co = _ml_core(gt_ref[...] + bg_ref[...], hd, c == nc - 1, m, qs, k, v, cmat, nvec)
        lane, row = co["lane"], co["row"]
        wt, wprev, s, cb, qf = co["wt"], co["wprev"], co["s"], co["cb"], co["qf"]
        h = h_ref[...]
        hn, rstd = _ml_headnorm(h)
        xc = _silu(pre_ref[...])
        zv = z_ref[...]
        nw, sk = nw_ref[...], sk_ref[...]
        dy = dy_ref[...]
        dz_ref[...] = _bf(dy * (hn * nw + sk * xc) * _dsilu(zv))
        do = dy * _silu(zv)
        dsk_ref[...] += jnp.sum(do * xc, axis=0, keepdims=True)
        dnw_ref[...] += jnp.sum(do * hn, axis=0, keepdims=True)
        dxc_ref[...] = do * sk
        dhn = do * nw
        dh = rstd * (dhn - jnp.mean(dhn, axis=1, keepdims=True) - hn * jnp.mean(dhn * hn, axis=1, keepdims=True))
        rinv = 1.0 / co["dd"]
        dnum = dh * rinv
        ddd = -jnp.sum(dh * h, axis=1, keepdims=True) * rinv
        den = co["den"]
        dden = jnp.where(jnp.abs(den) >= co["emt"], ddd * jnp.sign(den), 0.0)
        dnb = _bf(dnum)
        ds = _dot(dnb, v, NT) + dden
        dv = _dot(_bf(s), dnb, TN)
        dnw_ = _bf(dnum * wprev)
        dwn = dden * wprev
        dqs = _dot(dnw_, cb, NT) + dwn * nvec
        dc_out = _dot(qs, dnw_, TN)
        dn_out = jnp.sum(dwn * qf, axis=0, keepdims=True)
        dwprev = jnp.sum(dnum * co["qc"], axis=1, keepdims=True) + dden * co["qn"]
        ds0 = _bf(ds * wt)
        ddm = ds * s
        dqs = dqs + _dot(ds0, k)
        dk = _dot(ds0, qs, TN)
        colc = jnp.sum(ddm.T, axis=1, keepdims=True)
        dbc = dwprev * wprev + jnp.sum(ddm, axis=1, keepdims=True) - colc
        dig = colc
        dcn, dnn = dc_s[...], dn_s[...]
        dcb = _bf(dcn)
        decay, e, kf, wk = co["decay"], co["e"], co["kf"], co["wk"]
        ddecay = (jnp.sum(jnp.sum(dcn * cmat, axis=1, keepdims=True), axis=0, keepdims=True)
                  + jnp.sum(dnn * nvec, axis=1, keepdims=True))
        dwk = _dot(v, dcb, NT) + dnn
        dv = dv + _dot(_bf(wk), dcb)
        dk = dk + e * dwk
        dg = jnp.sum(dwk * kf, axis=1, keepdims=True) * e
        dblast = ddecay * decay + jnp.sum(dg, axis=0, keepdims=True)
        dbc = dbc - dg + jnp.where(row[:, :1] == CHUNK - 1, dblast, 0.0)
        dig = dig + dg
        dc_s[...] = decay * dcn + dc_out
        dn_s[...] = decay * dnn + dn_out
        dlf = _cumsum_rows(jnp.broadcast_to(dbc, (CHUNK, CHUNK)), row, rev=True)[:, :1]
        dfp = dlf * (1.0 - jax.nn.sigmoid(co["fpc"]))
        dig = jnp.where(co["valid"], dig, 0.0)
        dfp = jnp.where(co["valid"], dfp, 0.0)
        dgt_ref[...] = jnp.where(lane == hd, dig, 0.0) + jnp.where(lane == hd + ML_HEADS, dfp, 0.0)
        dq_ref[...] = _bf(dqs * ML_SCALE)
        dk_ref[...] = _bf(dk)
        dv_ref[...] = _bf(dv)

    ob = jax.ShapeDtypeStruct((bsz, tp, ML_WIDTH), BF16)
    dq, dk, dv, dz, dxc, dgt, dnw, dsk = _pc(
        body, name="ml_chunk_bwd", grid=(ML_HEADS, nc),
        in_specs=[sp["hrow"]] * 4 + [sp["gates"], sp["bias"], sp["hrow"], sp["hrow"], sp["hvec"], sp["hvec"],
                                     sp["hrow"], sp["cs"], sp["ns"], sp["ms"]],
        out_specs=[sp["hrow"]] * 5 + [sp["dgates"], sp["hvec"], sp["hvec"]],
        out_shape=[ob, ob, ob, ob, jax.ShapeDtypeStruct((bsz, tp, ML_WIDTH), F32),
                   jax.ShapeDtypeStruct((ML_HEADS, bsz, tp, CHUNK), F32),
                   jax.ShapeDtypeStruct((1, ML_WIDTH), F32), jax.ShapeDtypeStruct((1, ML_WIDTH), F32)],
        scratch=[pltpu.VMEM((bsz, ML_DH, ML_DH), F32), pltpu.VMEM((bsz, 1, ML_DH), F32)], vmem=16 << 20, dep=dep,
    )(*[_seq(a, bsz) for a in (dyb, qs, k, v, gates)], b_gate, _seq(pre, bsz), _seq(z, bsz), nw, sk, _seq(h, bsz),
      cs, ns, ms)
    return (dq.reshape(r, ML_WIDTH), dk.reshape(r, ML_WIDTH), dv.reshape(r, ML_WIDTH), dz.reshape(r, ML_WIDTH),
            dxc.reshape(r, ML_WIDTH), dgt.reshape(ML_HEADS, r, CHUNK), dnw, dsk)


def _ml_pre_bwd(dq, dk, dv, dgates, dxc_skip, pre, x, q, k, v, wq, wk, wv, wgq, wgk, wgv, bsz, nc):
    r = x.shape[0]
    tr = _pick(r, (256, 128))
    nt = r // tr
    hrow = pl.BlockSpec((tr, ML_DH), lambda h, i: (i, h))
    wexp = pl.BlockSpec((1, ML_DH, ML_DH), lambda h, i: (h, 0, 0))
    wcmp = pl.BlockSpec((1, ML_DH, CHUNK), lambda h, i: (h, 0, 0))
    wg = pl.BlockSpec((ML_DH, CHUNK), lambda h, i: (h, 0))
    dgs = pl.BlockSpec((ML_HEADS, tr, CHUNK), lambda h, i: (0, i, 0))
    bgs = pl.BlockSpec((1, 1, CHUNK), lambda h, i: (h, 0, 0))

    def body(dq_ref, dk_ref, dv_ref, dg_ref, dxs_ref, pre_ref, x_ref, q_ref, k_ref, v_ref, wq_ref, wk_ref, wv_ref,
             gq_ref, gk_ref, gv_ref, dpre_ref, dxv_ref, cq_ref, ck_ref, cv_ref, dgq_ref, dgk_ref, dgv_ref, dbg_ref,
             dwq_ref, dwk_ref, dwv_ref):
        i = pl.program_id(1)

        @pl.when(i == 0)
        def _():
            for ref in (dwq_ref, dwk_ref, dwv_ref, dgq_ref, dgk_ref, dgv_ref, dbg_ref):
                ref[...] = jnp.zeros_like(ref)

        dgt = dg_ref[0]
        for j in range(1, ML_HEADS):
            dgt = dgt + dg_ref[j]
        dbg_ref[0] += jnp.sum(dgt, axis=0, keepdims=True)
        dgb = _bf(dgt)
        dqt = _bf(dq_ref[...].astype(F32) + _dot(dgb, gq_ref[...], NT))
        dkt = _bf(dk_ref[...].astype(F32) + _dot(dgb, gk_ref[...], NT))
        dvt = _bf(dv_ref[...].astype(F32) + _dot(dgb, gv_ref[...], NT))
        dgq_ref[...] += _dot(q_ref[...], dgb, TN)
        dgk_ref[...] += _dot(k_ref[...], dgb, TN)
        dgv_ref[...] += _dot(v_ref[...], dgb, TN)
        prev = pre_ref[...]
        xcb = _bf(_silu(prev))
        xb = _bf(x_ref[...])
        dwq_ref[...] += _dot(xcb, dqt, TN)
        dwk_ref[...] += _dot(xcb, dkt, TN)
        dwv_ref[...] += _dot(xb, dvt, TN)
        dxc = _dot(dqt, wq_ref[0], NT) + _dot(dkt, wk_ref[0], NT) + dxs_ref[...]
        dpre_ref[...] = dxc * _dsilu(prev)
        dxv_ref[...] = _dot(dvt, wv_ref[0], NT)

        @pl.when(i == nt - 1)
        def _():
            rr = lax.broadcasted_iota(jnp.int32, (ML_DH, ML_DH), 0)
            cc = lax.broadcasted_iota(jnp.int32, (ML_DH, ML_DH), 1)
            diag = rr // QKV_BLOCK == cc // QKV_BLOCK
            fc = lax.broadcasted_iota(jnp.int32, (ML_DH, CHUNK), 0)
            fo = lax.broadcasted_iota(jnp.int32, (ML_DH, CHUNK), 1)
            fold = jnp.where(fc % QKV_BLOCK == fo, 1.0, 0.0).astype(F32)
            for src, dst in ((dwq_ref, cq_ref), (dwk_ref, ck_ref), (dwv_ref, cv_ref)):
                dst[0] = jnp.dot(jnp.where(diag, src[...], 0.0), fold, precision=HI, preferred_element_type=F32)

    f = jax.ShapeDtypeStruct((r, ML_WIDTH), F32)
    wc = jax.ShapeDtypeStruct((ML_HEADS, ML_DH, CHUNK), F32)
    wgs = jax.ShapeDtypeStruct((ML_WIDTH, CHUNK), F32)
    return _pc(body, name="ml_pre_bwd", grid=(ML_HEADS, nt),
               in_specs=[hrow, hrow, hrow, dgs, hrow, hrow, hrow, hrow, hrow, hrow, wexp, wexp, wexp, wg, wg, wg],
               out_specs=[hrow, hrow, wcmp, wcmp, wcmp, wg, wg, wg, bgs],
               out_shape=[f, f, wc, wc, wc, wgs, wgs, wgs, jax.ShapeDtypeStruct((ML_HEADS, 1, CHUNK), F32)],
               scratch=[pltpu.VMEM((ML_DH, ML_DH), F32)] * 3,
               vmem=8 << 20)(dq, dk, dv, dgates, dxc_skip, pre, x, q, k, v, wq, wk, wv, wgq, wgk, wgv)


def _pad_lanes(w):
    return jnp.pad(w, ((0, 0), (0, CHUNK - w.shape[1])))


def _ml_weights(conv_w, conv_b, wq, wk, wv, w_gate, b_gate, norm_w, skip):
    return dict(
        conv_w=conv_w, conv_b=conv_b,
        wq=_bf(_headwise_expand(wq)), wk=_bf(_headwise_expand(wk)), wv=_bf(_headwise_expand(wv)),
        wgq=_bf(_pad_lanes(w_gate[:ML_WIDTH])), wgk=_bf(_pad_lanes(w_gate[ML_WIDTH:2 * ML_WIDTH])),
        wgv=_bf(_pad_lanes(w_gate[2 * ML_WIDTH:])), b_gate=_pad_lanes(b_gate.reshape(1, -1)),
        norm=norm_w.reshape(1, ML_WIDTH), skip=skip.reshape(1, ML_WIDTH))


def _ml_layer_fwd(x, z, w, bsz, nc):
    pre = _conv_fwd(x, w["conv_w"], w["conv_b"], bsz, nc, name="ml_conv")
    q, qs, k, v, gates = _ml_pre(pre, x, w["wq"], w["wk"], w["wv"], w["wgq"], w["wgk"], w["wgv"], bsz, nc)
    h, yb, cs, ns, ms = _ml_chunk_fwd(qs, k, v, gates, w["b_gate"], pre, z, w["norm"], w["skip"], bsz, nc)
    return yb, dict(pre=pre, q=q, qs=qs, k=k, v=v, gates=gates, h=h, cs=cs, ns=ns, ms=ms)


def _ml_layer_bwd(dyb, x, z, sv, w, bsz, nc, dep=None):
    dq, dk, dv, dz, dxc, dgates, dnw, dsk = _ml_chunk_bwd(
        dyb, sv["qs"], sv["k"], sv["v"], sv["gates"], w["b_gate"], sv["pre"], z, w["norm"], w["skip"], sv["h"],
        sv["cs"], sv["ns"], sv["ms"], bsz, nc, dep=dep)
    dpre, dxv, dwq, dwk, dwv, dgq, dgk, dgv, dbg = _ml_pre_bwd(
        dq, dk, dv, dgates, dxc, sv["pre"], x, sv["q"], sv["k"], sv["v"], w["wq"], w["wk"], w["wv"], w["wgq"],
        w["wgk"], w["wgv"], bsz, nc)
    dx, dcw, dcb = _conv_bwd(dpre, x, w["conv_w"], bsz, nc, name="ml_conv_bwd", add=dxv)
    ng = 2 * ML_HEADS
    grads = dict(
        ml_conv_w=dcw, ml_conv_b=dcb, ml_wq=_headwise_extract(dwq), ml_wk=_headwise_extract(dwk),
        ml_wv=_headwise_extract(dwv), ml_w_gate=jnp.concatenate([dgq[:, :ng], dgk[:, :ng], dgv[:, :ng]], axis=0),
        ml_b_gate=dbg[0][:, :ng], ml_norm=dnw, ml_skip=dsk)
    return dx, dz, grads


HI = lax.Precision.HIGHEST


def _softplus(x):
    return jnp.maximum(x, 0.0) + jnp.log(1.0 + jnp.exp(-jnp.abs(x)))


def _lane_cumsum(x, lane, rev=False):
    del lane
    return _dot_terms(x, _tri(not rev), NN, exact_rhs=True, terms=3)


def _dot_terms(lhs, rhs, dims, *, exact_rhs, terms):
    x = lhs if exact_rhs else rhs
    sel = _bf(rhs if exact_rhs else lhs)
    acc = None
    for _ in range(terms):
        piece = _bf(x)
        part = _dot(piece, sel, dims) if exact_rhs else _dot(sel, piece, dims)
        acc = part if acc is None else acc + part
        x = x - piece.astype(F32)
    return acc


def _head_sum_matrix():
    r = lax.broadcasted_iota(jnp.int32, (SSD_HPG, SSD_GW), 0)
    l = lax.broadcasted_iota(jnp.int32, (SSD_HPG, SSD_GW), 1)
    return jnp.where(l // SSD_P == r, 1.0, 0.0).astype(F32)


def _ssd_core(xs, bm, cm, dt_raw, dt_bias, a_log, first):
    sq = (CHUNK, CHUNK)
    lane8 = lax.broadcasted_iota(jnp.int32, (SSD_HPG, CHUNK), 1)
    lane = lax.broadcasted_iota(jnp.int32, sq, 1)
    row = lax.broadcasted_iota(jnp.int32, sq, 0)
    low = lane < SSD_P
    valid = jnp.logical_or(jnp.logical_not(first), lane8 >= PAD_ROWS)
    pre = dt_raw + dt_bias
    dt = jnp.where(valid, _softplus(pre), 0.0)
    a = -jnp.exp(a_log)
    cum = _lane_cumsum(dt * a, lane8)
    cb = _dot(_bf(cm), _bf(bm), NT)
    heads = []
    for r in range(SSD_HPG):
        rowb = jnp.broadcast_to(cum[r:r + 1, :], sq)
        colb = rowb.T
        seg = jnp.exp(jnp.where(lane <= row, colb - rowb, NEG))
        dtrow = jnp.broadcast_to(dt[r:r + 1, :], sq)
        lastb = colb[CHUNK - 1:CHUNK, :]
        heads.append(dict(seg=seg, dtrow=dtrow, w=cb * seg * dtrow, ecol=jnp.exp(colb),
                          dec=jnp.exp(lastb - colb) * dtrow.T, elast=jnp.exp(lastb)))

    def pairs(key):
        return jnp.concatenate([jnp.where(low[:heads[0][key].shape[0]], heads[2 * j][key], heads[2 * j + 1][key])
                                for j in range(SSD_HPG // 2)], axis=1)

    return dict(lane8=lane8, low=low, valid=valid, pre=pre, dt=dt, a=a, cum=cum, cb=cb, heads=heads,
                expc=pairs("ecol"), dec=pairs("dec"), elast=pairs("elast"))


def _ssd_specs(nc, rev, bsz):
    def cc(c):
        return (nc - 1 - c) if rev else c

    return dict(
        wide=pl.BlockSpec((bsz, CHUNK, SSD_GW), lambda g, c: (0, cc(c), g)),
        narrow=pl.BlockSpec((bsz, CHUNK, SSD_N), lambda g, c: (0, cc(c), g)),
        dtT=pl.BlockSpec((bsz, SSD_HPG, CHUNK), lambda g, c: (0, g, cc(c))),
        hcol=pl.BlockSpec((SSD_HPG, 1), lambda g, c: (g, 0)),
        hacc=pl.BlockSpec((SSD_HPG, CHUNK), lambda g, c: (g, 0)),
        gvec=pl.BlockSpec((1, SSD_GW), lambda g, c: (0, g)),
        state=pl.BlockSpec((bsz, 1, SSD_N, SSD_GW), lambda g, c: (0, g * nc + cc(c), 0, 0)),
    )


def _ssd_chunk_fwd(xs_pre, bm_pre, cm_pre, dt_raw, dt_bias, a_log, d_exp, z, gnorm, bsz, nc):
    tp = xs_pre.shape[1]
    sp = _ssd_specs(nc, False, bsz)

    def body(xs_all, bm_all, cm_all, dt_all, db_ref, al_ref, d_ref, z_all, gn_ref, y_all, yn_all, st_all, st_sall):
        c = pl.program_id(1)

        @pl.when(c == 0)
        def _():
            st_sall[...] = jnp.zeros_like(st_sall)

        for bi in range(bsz):
            one(c, xs_all.at[bi], bm_all.at[bi], cm_all.at[bi], dt_all.at[bi], db_ref, al_ref, d_ref, z_all.at[bi],
                gn_ref, y_all.at[bi], yn_all.at[bi], st_all.at[bi], st_sall.at[bi])

    def one(c, xs_ref, bm_ref, cm_ref, dt_ref, db_ref, al_ref, d_ref, z_ref, gn_ref, y_ref, yn_ref, st_ref, st_s):
        state = st_s[...]
        st_ref[0] = state
        xs, bm, cm = _silu(xs_ref[...]), _silu(bm_ref[...]), _silu(cm_ref[...])
        co = _ssd_core(xs, bm, cm, dt_ref[...], db_ref[...], al_ref[...], c == 0)
        low, hd = co["low"], co["heads"]
        ys = []
        for j in range(SSD_HPG // 2):
            xp = xs[:, j * CHUNK:(j + 1) * CHUNK]
            lhs = jnp.concatenate([hd[2 * j]["w"], hd[2 * j + 1]["w"]], axis=1)
            rhs = jnp.concatenate([jnp.where(low, xp, 0.0), jnp.where(low, 0.0, xp)], axis=0)
            ys.append(_dot(_bf(lhs), _bf(rhs)))
        cmb = _bf(cm)
        y = jnp.concatenate(ys, axis=1) + co["expc"] * _dot(cmb, _bf(state)) + d_ref[...] * xs
        y_ref[...] = y
        yg = y * _silu(z_ref[...])
        rstd = lax.rsqrt(jnp.mean(yg * yg, axis=1, keepdims=True) + NORM_EPS)
        yn_ref[...] = _bf(yg * rstd * gn_ref[...])
        st_s[...] = co["elast"] * state + _dot(_bf(bm), _bf(xs * co["dec"]), TN)

    return _pc(body, name="ssd_chunk_fwd", grid=(SSD_GROUPS, nc),
               in_specs=[sp["wide"], sp["narrow"], sp["narrow"], sp["dtT"], sp["hcol"], sp["hcol"], sp["gvec"],
                         sp["wide"], sp["gvec"]],
               out_specs=[sp["wide"], sp["wide"], sp["state"]],
               out_shape=[jax.ShapeDtypeStruct((bsz, tp, SSD_INNER), F32),
                          jax.ShapeDtypeStruct((bsz, tp, SSD_INNER), BF16),
                          jax.ShapeDtypeStruct((bsz, SSD_GROUPS * nc, SSD_N, SSD_GW), F32)],
               scratch=[pltpu.VMEM((bsz, SSD_N, SSD_GW), F32)], vmem=12 << 20,
               )(xs_pre, bm_pre, cm_pre, dt_raw, dt_bias, a_log, d_exp, z, gnorm)


def _ssd_chunk_bwd(dyn, xs_pre, bm_pre, cm_pre, dt_raw, dt_bias, a_log, d_exp, z, gnorm, y, states, bsz, nc):
    tp = xs_pre.shape[1]
    sp = _ssd_specs(nc, True, bsz)

    def body(dyn_all, xs_all, bm_all, cm_all, dt_all, db_ref, al_ref, d_ref, z_all, gn_ref, y_all, st_all,
             dxs_all, dbm_all, dcm_all, dz_all, ddt_all, dgn_ref, dd_ref, dbias_ref, dal_ref, ds_sall):
        c = pl.program_id(1)

        @pl.when(c == 0)
        def _():
            for ref in (dgn_ref, dd_ref, dbias_ref, dal_ref, ds_sall):
                ref[...] = jnp.zeros_like(ref)

        for bi in range(bsz):
            one(c, dyn_all.at[bi], xs_all.at[bi], bm_all.at[bi], cm_all.at[bi], dt_all.at[bi], db_ref, al_ref, d_ref,
                z_all.at[bi], gn_ref, y_all.at[bi], st_all.at[bi], dxs_all.at[bi], dbm_all.at[bi], dcm_all.at[bi],
                dz_all.at[bi], ddt_all.at[bi], dgn_ref, dd_ref, dbias_ref, dal_ref, ds_sall.at[bi])

    def one(c, dyn_ref, xs_ref, bm_ref, cm_ref, dt_ref, db_ref, al_ref, d_ref, z_ref, gn_ref, y_ref, st_ref,
            dxs_ref, dbm_ref, dcm_ref, dz_ref, ddt_ref, dgn_ref, dd_ref, dbias_ref, dal_ref, ds_s):
        xs_p, bm_p, cm_p = xs_ref[...], bm_ref[...], cm_ref[...]
        xs, bm, cm = _silu(xs_p), _silu(bm_p), _silu(cm_p)
        state = st_ref[0]
        co = _ssd_core(xs, bm, cm, dt_ref[...], db_ref[...], al_ref[...], c == nc - 1)
        low, hd, lane8, cb = co["low"], co["heads"], co["lane8"], co["cb"]
        dt, a, cum = co["dt"], co["a"], co["cum"]
        sub8 = lax.broadcasted_iota(jnp.int32, (SSD_HPG, CHUNK), 0)
        eh = _head_sum_matrix()

        def head_rows(full):
            return _dot_terms(eh, full, NT, exact_rhs=False, terms=2)

        def head_col(vec):
            return jnp.sum(eh * vec, axis=1, keepdims=True)

        yv, zv, gn = y_ref[...], z_ref[...], gn_ref[...]
        sz = _silu(zv)
        yg = yv * sz
        rstd = lax.rsqrt(jnp.mean(yg * yg, axis=1, keepdims=True) + NORM_EPS)
        yh = yg * rstd
        dyn = dyn_ref[...]
        dgn_ref[...] += jnp.sum(dyn * yh, axis=0, keepdims=True)
        dyh = dyn * gn
        dyg = rstd * (dyh - yh * jnp.mean(dyh * yh, axis=1, keepdims=True))
        dz_ref[...] = _bf(dyg * yv * _dsilu(zv))
        dy = dyg * sz
        dxs = dy * d_ref[...]
        dd_ref[...] += head_col(jnp.sum(dy * xs, axis=0, keepdims=True))
        cmb, bmb, stb = _bf(cm), _bf(bm), _bf(state)
        ysv = _dot(cmb, stb)
        expc = co["expc"]
        dys = _bf(dy * expc)
        dcum = head_rows(dy * ysv * expc)
        dcm = _dot(dys, stb, NT)
        dstate_out = _dot(cmb, dys, TN)
        dcb = jnp.zeros((CHUNK, CHUNK), F32)
        ddt = jnp.zeros((SSD_HPG, CHUNK), F32)
        dxs_pairs = []
        for j in range(SSD_HPG // 2):
            sl = slice(j * CHUNK, (j + 1) * CHUNK)
            dyp, xp = dy[:, sl], _bf(xs[:, sl])
            lhs = _bf(jnp.concatenate([hd[2 * j]["w"], hd[2 * j + 1]["w"]], axis=1))
            both = _dot(lhs, _bf(dyp), TN)
            dxs_pairs.append(jnp.where(low, both[:CHUNK], both[CHUNK:]))
            for q, msk in ((2 * j, low), (2 * j + 1, jnp.logical_not(low))):
                h = hd[q]
                dw = _dot(_bf(jnp.where(msk, dyp, 0.0)), xp, NT)
                dcb = dcb + dw * h["seg"] * h["dtrow"]
                e_ = dw * h["w"]
                dcum_r = jnp.sum(e_.T, axis=0, keepdims=True) - jnp.sum(e_, axis=0, keepdims=True)
                ddt_r = jnp.sum(dw * cb * h["seg"], axis=0, keepdims=True)
                dcum = dcum + jnp.where(sub8 == q, dcum_r, 0.0)
                ddt = ddt + jnp.where(sub8 == q, ddt_r, 0.0)
        dxs = dxs + jnp.concatenate(dxs_pairs, axis=1)
        dcbb = _bf(dcb)
        dcm = dcm + _dot(dcbb, bmb)
        dbm = _dot(dcbb, cmb, TN)
        dsn = ds_s[...]
        dsb = _bf(dsn)
        dec = co["dec"]
        dbm = dbm + _dot(_bf(xs * dec), dsb, NT)
        dxd = _dot(bmb, dsb)
        dxs = dxs + dxd * dec
        ddec = head_rows(dxd * xs)
        last = cum[:, CHUNK - 1:CHUNK]
        erow = jnp.exp(last - cum)
        ddt = ddt + ddec * erow
        dla = ddec * erow * dt
        dlast = (jnp.sum(dla, axis=1, keepdims=True)
                 + head_col(jnp.sum(dsn * state, axis=0, keepdims=True)) * jnp.exp(last))
        dcum = dcum - dla + jnp.where(lane8 == CHUNK - 1, dlast, 0.0)
        ds_s[...] = co["elast"] * dsn + dstate_out
        dda = _lane_cumsum(dcum, lane8, rev=True)
        ddt = jnp.where(co["valid"], ddt + dda * a, 0.0)
        ddt_raw = ddt * jax.nn.sigmoid(co["pre"])
        ddt_ref[...] = ddt_raw
        dbias_ref[...] += jnp.sum(ddt_raw, axis=1, keepdims=True)
        dal_ref[...] += jnp.sum(dda * dt, axis=1, keepdims=True) * a
        dxs_ref[...] = dxs * _dsilu(xs_p)
        dbm_ref[...] = dbm * _dsilu(bm_p)
        dcm_ref[...] = dcm * _dsilu(cm_p)

    st = jax.ShapeDtypeStruct
    hacc = st((SSD_HEADS, CHUNK), F32)
    return _pc(body, name="ssd_chunk_bwd", grid=(SSD_GROUPS, nc),
               in_specs=[sp["wide"], sp["wide"], sp["narrow"], sp["narrow"], sp["dtT"], sp["hcol"], sp["hcol"],
                         sp["gvec"], sp["wide"], sp["gvec"], sp["wide"], sp["state"]],
               out_specs=[sp["wide"], sp["narrow"], sp["narrow"], sp["wide"], sp["dtT"], sp["gvec"], sp["hacc"],
                          sp["hacc"], sp["hacc"]],
               out_shape=[st((bsz, tp, SSD_INNER), F32), st((bsz, tp, SSD_BC), F32), st((bsz, tp, SSD_BC), F32),
                          st((bsz, tp, SSD_INNER), BF16), st((bsz, SSD_HEADS, tp), F32), st((1, SSD_INNER), F32),
                          hacc, hacc, hacc],
               scratch=[pltpu.VMEM((bsz, SSD_N, SSD_GW), F32)], vmem=20 << 20,
               )(dyn, xs_pre, bm_pre, cm_pre, dt_raw, dt_bias, a_log, d_exp, z, gnorm, y, states)


SSD_BC = SSD_GROUPS * SSD_N


def _ssd_weights(conv_w, conv_b, dt_bias, a_log, d, gnorm):
    cuts = (0, SSD_INNER, SSD_INNER + SSD_BC, SSD_INNER + 2 * SSD_BC)
    return dict(
        conv_w=[conv_w[:, cuts[i]:cuts[i + 1]] for i in range(3)],
        conv_b=[conv_b[cuts[i]:cuts[i + 1]] for i in range(3)],
        dt_bias=dt_bias.reshape(SSD_HEADS, 1), a_log=a_log.reshape(SSD_HEADS, 1),
        d_exp=jnp.repeat(d.reshape(SSD_HEADS), SSD_P).reshape(1, SSD_INNER), gnorm=gnorm.reshape(1, SSD_INNER))


def _ssd_layer_fwd(z, xs_in, bm_in, cm_in, dt_rows, w, bsz, nc):
    pres = [_conv_fwd(a, w["conv_w"][i], w["conv_b"][i], bsz, nc, name=f"ssd_conv{i}")
            for i, a in enumerate((xs_in, bm_in, cm_in))]
    def seq(a):
        return a.reshape(bsz, nc * CHUNK, a.shape[-1])

    dt_t = jnp.swapaxes(seq(dt_rows)[:, :, :SSD_HEADS], 1, 2)
    y, yn, states = _ssd_chunk_fwd(seq(pres[0]), seq(pres[1]), seq(pres[2]), dt_t, w["dt_bias"], w["a_log"],
                                   w["d_exp"], seq(z), w["gnorm"], bsz, nc)
    return yn.reshape(-1, SSD_INNER), dict(pres=pres, dt_t=dt_t, y=y, states=states)


def _ssd_layer_bwd(dyn, z, xs_in, bm_in, cm_in, sv, w, bsz, nc):
    pres = sv["pres"]

    def seq(a):
        return a.reshape(bsz, nc * CHUNK, a.shape[-1])

    def rows(a):
        return a.reshape(-1, a.shape[-1])

    dxs_p, dbm_p, dcm_p, dz, ddt_t, dgn, dd, dbias, dal = _ssd_chunk_bwd(
        seq(dyn), seq(pres[0]), seq(pres[1]), seq(pres[2]), sv["dt_t"], w["dt_bias"], w["a_log"], w["d_exp"], seq(z),
        w["gnorm"], sv["y"], sv["states"], bsz, nc)
    dz = rows(dz)
    outs = [_conv_bwd(rows(dp), a, w["conv_w"][i], bsz, nc, name=f"ssd_conv_bwd{i}")
            for i, (dp, a) in enumerate(((dxs_p, xs_in), (dbm_p, bm_in), (dcm_p, cm_in)))]
    ddt = _bf(_pad_lanes(rows(jnp.swapaxes(ddt_t, 1, 2))))
    grads = dict(
        ssd_conv_w=jnp.concatenate([o[1] for o in outs], axis=1),
        ssd_conv_b=jnp.concatenate([o[2] for o in outs], axis=1),
        ssd_dt_bias=dbias[:, 0], ssd_a_log=dal[:, 0], ssd_d=dd[:, 0], ssd_gnorm=dgn)
    return dz, outs[0][0], outs[1][0], outs[2][0], ddt, grads


WNAMES = ("meta_tokens", "ab_norm", "ab_w_in", "s5_lambda_re", "s5_lambda_im", "s5_log_dt", "s5_b_re", "s5_b_im",
          "s5_c_re", "s5_c_im", "s5_d", "s5_glu_w", "s5_glu_b", "ml_conv_w", "ml_conv_b", "ml_wq", "ml_wk", "ml_wv",
          "ml_w_gate", "ml_b_gate", "ml_norm", "ml_skip", "ab_w_out", "ssd_norm", "ssd_w_in", "ssd_conv_w",
          "ssd_conv_b", "ssd_dt_bias", "ssd_a_log", "ssd_d", "ssd_gnorm", "ssd_w_out", "final_norm")
SHARD_AXIS = dict(meta_tokens=1, ab_w_in=2, s5_glu_w=1, ml_conv_w=2, ml_wq=1, ml_wk=1, ml_wv=1, ml_w_gate=1,
                  ab_w_out=1, ssd_norm=1, ssd_w_in=2, ssd_conv_w=2, ssd_conv_b=1, ssd_gnorm=1, ssd_w_out=1)
BIG = ("ab_w_in", "s5_glu_w", "ab_w_out", "ssd_w_in", "ssd_w_out")
SMALL = tuple(n for n in WNAMES if n in SHARD_AXIS and n not in BIG)
REPL = tuple(n for n in WNAMES if n not in SHARD_AXIS)
PACK_ALIGN = 8 * 128


def _pack(arrs):
    lead = arrs[0][1]
    parts = []
    for a, nlead in arrs:
        f = a.reshape(a.shape[:nlead] + (-1,))
        f = jnp.pad(f, [(0, 0)] * nlead + [(0, (-f.shape[-1]) % PACK_ALIGN)])
        parts.append(f.reshape(f.shape[:nlead] + (-1, 128)))
    return jnp.concatenate(parts, axis=lead)


def _unpack(p, shapes):
    out, r0 = [], 0
    lead = p.shape[:-2]
    for s in shapes:
        n = math.prod(s)
        rows = -(-n // PACK_ALIGN) * 8
        seg = p[..., r0:r0 + rows, :].reshape(lead + (rows * 128,))[..., :n]
        out.append(seg.reshape(lead + tuple(s)))
        r0 += rows
    return out


def _assemble(g, axis):
    m = jnp.moveaxis(g, 0, axis)
    return m.reshape(m.shape[:axis] + (m.shape[axis] * m.shape[axis + 1],) + m.shape[axis + 2:])


def _split(full, axis):
    s = full.shape
    m = full.reshape(s[:axis] + (N_DEV, s[axis] // N_DEV) + s[axis + 1:])
    return jnp.moveaxis(m, axis, 0)


def kernel(x, *rest):
    nw = len(WNAMES)
    w = dict(zip(WNAMES, rest[:nw]))
    loss_target = rest[nw]
    mom = dict(zip(WNAMES, rest[nw + 1:2 * nw + 1]))
    var = dict(zip(WNAMES, rest[2 * nw + 1:3 * nw + 1]))
    bsz = x.shape[0]
    nc = 1 + SEQ // CHUNK
    tp = nc * CHUNK

    local = {n: _bf(w[n][0]) for n in BIG}
    small_local = _pack([(w[n], 0) for n in SMALL])
    gs = _exchange_start([small_local], ["ag"], name="gather_s")
    ga = _exchange_start([local["ab_w_in"]], ["ag"], name="gather_a", dep=gs["token"], peers=SAME_CORE[1:])
    got_s = _exchange_wait(gs, ga["token"])

    def assemble_big(n, got):
        return _assemble(got[:, None], SHARD_AXIS[n])[0]

    full = {}
    for n, g in zip(SMALL, _unpack(got_s[0], [w[n].shape for n in SMALL])):
        full[n] = _assemble(g, SHARD_AXIS[n])[0] if n != "meta_tokens" else _assemble(g, SHARD_AXIS[n])
    for n in REPL:
        full[n] = w[n][0] if n != "final_norm" else w[n]
    glu_b = full["s5_glu_b"].reshape(1, S5_WIDTH)
    meta = jnp.broadcast_to(full["meta_tokens"][None], (bsz, N_META, D_MODEL))
    h0 = jnp.concatenate([jnp.zeros((bsz, PAD_ROWS, D_MODEL), F32), meta, x], axis=1).reshape(bsz * tp, D_MODEL)
    xn0 = _rms_fwd(h0, full["ab_norm"], name="rms0")
    s5p, s5_vjp = _s5_tables(*[full[n] for n in ("s5_lambda_re", "s5_lambda_im", "s5_log_dt", "s5_b_re", "s5_b_im",
                                                   "s5_c_re", "s5_c_im", "s5_d")])
    mlw = _ml_weights(*[full[n] for n in ("ml_conv_w", "ml_conv_b", "ml_wq", "ml_wk", "ml_wv", "ml_w_gate",
                                           "ml_b_gate", "ml_norm", "ml_skip")])
    got_a = _exchange_wait(ga, [xn0, s5p["wbr"], s5p["wcr"], s5p["pr"], mlw["wq"], mlw["wk"], mlw["wv"], mlw["wgq"]])
    fwd_a = _sibling_forward_start(got_a[0], name="gather_a2")
    got_a = [_sibling_forward_wait(fwd_a, fwd_a["token"])]
    gb = _exchange_start([local["s5_glu_w"], local["ab_w_out"]], ["ag", "ag"], name="gather_b", dep=got_a[0])
    gc = _exchange_start([local["ssd_w_in"], local["ssd_w_out"]], ["ag", "ag"], name="gather_c", dep=gb["token"])
    full["ab_w_in"] = assemble_big("ab_w_in", got_a[0])
    cuts0 = (0, S5_WIDTH, 2 * S5_WIDTH, 2 * S5_WIDTH + ML_WIDTH, 2 * (S5_WIDTH + ML_WIDTH))
    w_in0 = [full["ab_w_in"][:, cuts0[i]:cuts0[i + 1]] for i in range(4)]

    u, za, xb, zb = [_mm(xn0, wi, "NN", name=f"in0_{i}") for i, wi in enumerate(w_in0)]
    got_b = []

    def glu_w_after(scan_out):
        got_b.extend(_exchange_wait(gb, scan_out))
        return assemble_big("s5_glu_w", got_b[0])

    sv5 = _s5_layer_fwd(u, s5p, glu_w_after, bsz, nc)
    glu_w = assemble_big("s5_glu_w", got_b[0])
    w_out0 = assemble_big("ab_w_out", got_b[1])
    w_out0 = [w_out0[:S5_WIDTH], w_out0[S5_WIDTH:]]
    ya = _s5_post(sv5["y1"], sv5["glu_pre"], glu_b, za)
    yb, svm = _ml_layer_fwd(xb, zb, mlw, bsz, nc)
    h1 = _mm(ya, w_out0[0], "NN", name="out0_a", add=h0)
    h1 = _mm(yb, w_out0[1], "NN", name="out0_b", add=h1)
    got_c = _exchange_wait(gc, h1)
    w_in1, w_out1 = assemble_big("ssd_w_in", got_c[0]), assemble_big("ssd_w_out", got_c[1])
    cuts1 = (0, SSD_INNER, 2 * SSD_INNER, 2 * SSD_INNER + SSD_BC, 2 * SSD_INNER + 2 * SSD_BC)
    w_in1 = [w_in1[:, cuts1[i]:cuts1[i + 1]] for i in range(4)] + [_pad_lanes(w_in1[:, cuts1[4]:])]
    xn1 = _rms_fwd(h1, full["ssd_norm"], name="rms1")
    z1, xs_in, bm_in, cm_in, dt_rows = [_mm(xn1, wi, "NN", name=f"in1_{i}") for i, wi in enumerate(w_in1)]
    ssdw = _ssd_weights(*[full[n] for n in ("ssd_conv_w", "ssd_conv_b", "ssd_dt_bias", "ssd_a_log", "ssd_d",
                                             "ssd_gnorm")])
    yn, svs = _ssd_layer_fwd(z1, xs_in, bm_in, cm_in, dt_rows, ssdw, bsz, nc)
    h2 = _mm(yn, w_out1, "NN", name="out1", add=h1)
    loss_part, dh2, dfinal, dh2_b = _final_loss(h2, full["final_norm"], loss_target, bsz, nc)

    g = {"final_norm": dfinal}
    dyn = _mm(dh2_b, w_out1, "NT", name="d_out1")
    g["ssd_w_out"] = _mm(yn, dh2_b, "TN", name="dw_out1", out_dtype=BF16)
    dz1, dxs, dbm, dcm, ddt, gs = _ssd_layer_bwd(dyn, z1, xs_in, bm_in, cm_in, svs, ssdw, bsz, nc)
    g.update(gs)
    dps1 = (dz1, dxs, dbm, dcm, ddt)
    dxn1 = None
    for i, (dp, wi) in enumerate(zip(dps1, w_in1)):
        dxn1 = _mm(dp, wi, "NT", name=f"d_in1_{i}", add=dxn1)
    dw1 = [_mm(xn1, dp, "TN", name=f"dw_in1_{i}", out_dtype=BF16) for i, dp in enumerate(dps1)]
    g["ssd_w_in"] = jnp.concatenate(dw1[:4] + [dw1[4][:, :SSD_HEADS]], axis=1)

    def local_shape(n):
        return w[n].shape

    def slabs(n):
        gf = g[n].reshape((1,) + tuple(g[n].shape)) if n != "meta_tokens" else g[n]
        full_shape = tuple(d * (N_DEV if i == SHARD_AXIS[n] else 1) for i, d in enumerate(local_shape(n)))
        return _split(gf.reshape(full_shape), SHARD_AXIS[n])

    x1 = _exchange_start([slabs("ssd_w_in")[:, 0], slabs("ssd_w_out")[:, 0]], ["a2a", "a2a"], name="grads_1")
    dh1, g["ssd_norm"], dh1_b = _rms_bwd(h1, full["ssd_norm"], dxn1, dh2, name="rms1_bwd", dep=x1["token"])
    dya = _mm(dh1_b, w_out0[0], "NT", name="d_out0_a")
    dyb = _mm(dh1_b, w_out0[1], "NT", name="d_out0_b")
    g["ab_w_out"] = jnp.concatenate([_mm(ya, dh1_b, "TN", name="dw_out0_a", out_dtype=BF16),
                                     _mm(yb, dh1_b, "TN", name="dw_out0_b", out_dtype=BF16)], axis=0)
    du, dza, g5 = _s5_layer_bwd(dya, u, za, sv5, s5p, s5_vjp, glu_w, glu_b, bsz, nc)
    g.update(g5)
    x2 = _exchange_start([slabs("ab_w_out")[:, 0], _bf(slabs("s5_glu_w")[:, 0])], ["a2a", "a2a"], name="grads_2")
    dxb, dzb, gm = _ml_layer_bwd(dyb, xb, zb, svm, mlw, bsz, nc, dep=x2["token"])
    g.update(gm)
    dps0 = (du, dza, dxb, dzb)
    dw0 = [_mm(xn0, dp, "TN", name=f"dw_in0_{i}", out_dtype=BF16, tn=S5_WIDTH, slabs=True) for i, dp in enumerate(dps0)]
    dw_in0_slabs = jnp.concatenate(dw0, axis=0)
    x3 = _exchange_start([dw_in0_slabs], ["a2a"], name="grads_3")
    dxn0 = None
    for i, (dp, wi) in enumerate(zip(dps0, w_in0)):
        dxn0 = _mm(dp, wi, "NT", name=f"d_in0_{i}", add=dxn0, dep=x3["token"] if i == 0 else None)
    grad_x, d_chunk0, g["ab_norm"] = _rms_bwd_first(h0, full["ab_norm"], dxn0, dh1, bsz, nc, name="rms0_bwd")
    g["meta_tokens"] = jnp.sum(d_chunk0[:, PAD_ROWS:], axis=0)

    small_g = _pack([(slabs(n), 1) for n in SMALL])
    repl_g = _pack([(g[n], 0) for n in REPL])
    x4 = _exchange_start([small_g, repl_g, loss_part], ["a2a", "ag", "ag"], name="grads_4")

    def update_big(n, gp):
        return _adamw(w[n][0], mom[n][0], var[n][0], gp, name=f"adamw_{n}")

    res = {}
    ex1 = _exchange_wait(x1, x4["token"])
    res["ssd_w_in"], res["ssd_w_out"] = update_big("ssd_w_in", ex1[0]), update_big("ssd_w_out", ex1[1])
    ex2 = _exchange_wait(x2, res["ssd_w_out"][0])
    res["ab_w_out"], res["s5_glu_w"] = update_big("ab_w_out", ex2[0]), update_big("s5_glu_w", ex2[1])
    ex3 = _exchange_wait(x3, [res[n][0] for n in ("ssd_w_in", "ssd_w_out", "ab_w_out", "s5_glu_w")])
    res["ab_w_in"] = update_big("ab_w_in", ex3[0])
    ex4 = _exchange_wait(x4, res["ab_w_in"][0])
    loss = jnp.sum(ex4[2][:, 0, 0])
    for names, gp, tag in ((SMALL, ex4[0], "small"), (REPL, ex4[1], "repl")):
        shapes = [local_shape(n) for n in names]
        packs = [_pack([(d[n], 0) for n in names]) for d in (w, mom, var)]
        outs = _adamw(packs[0], packs[1], packs[2], gp, name=f"adamw_{tag}")
        for k, o in enumerate(outs):
            for n, a in zip(names, _unpack(o, shapes)):
                res.setdefault(n, [None] * 4)[k] = a
    outs = [loss, grad_x]
    for k in range(4):
        outs += [res[n][k].reshape(local_shape(n)) for n in WNAMES]
    return tuple(outs)
```

```python
import functools
import math

import jax
import jax.numpy as jnp
from jax import lax
from jax.experimental import pallas as pl
from jax.experimental.pallas import tpu as pltpu

F32 = jnp.float32
BF16 = jnp.bfloat16

D_MODEL = 2048
SEQ = 2048
N_META = 16
CHUNK = 128
PAD_ROWS = CHUNK - N_META
NORM_EPS = 1e-6
HEAD_NORM_EPS = 1e-5
S5_WIDTH = 1024
S5_GROUPS = 64
S5_GROUP_SIZE = 16
S5_STATE = 64
S5_GB = 8
S5_LANES = S5_GB * S5_STATE
ML_WIDTH = 3072
ML_HEADS = 8
ML_DH = 384
ML_CONV = 4
QKV_BLOCK = 4
SSD_INNER = 4096
SSD_HEADS = 64
SSD_P = 64
SSD_N = 128
SSD_GROUPS = 8
SSD_HPG = 8
SSD_GW = SSD_HPG * SSD_P
N_DEV = 8
ADAM_LR, ADAM_B1, ADAM_B2, ADAM_EPS, ADAM_WD, ADAM_STEP = 0.001, 0.9, 0.999, 1e-08, 0.01, 10
NEG = -1e30
VMEM_CAP = 60 * 1024 * 1024
MM_BLOCK_BUDGET = 22 * 1024 * 1024
MESH = pl.DeviceIdType.MESH

NN = (((1,), (0,)), ((), ()))
NT = (((1,), (1,)), ((), ()))
TN = (((0,), (0,)), ((), ()))


def _dot(a, b, dims=NN):
    return lax.dot_general(a, b, dims, preferred_element_type=F32)


def _bf(x):
    return x.astype(BF16)


def _pick(n, cands):
    for c in cands:
        if n % c == 0:
            return c
    return n


def _nbytes(shape, dtype):
    return math.prod(shape) * jnp.dtype(dtype).itemsize


ANY_SPEC = pl.BlockSpec(memory_space=pl.ANY)


def _pc(body, *, name, grid, in_specs, out_specs, out_shape, scratch=(), vmem=None, dep=None):
    limit = None if vmem is None else int(min(VMEM_CAP, max(32 * 1024 * 1024, 2 * vmem + (8 << 20))))
    n_in = len(in_specs)
    if dep is not None:
        inner = body

        def body(*refs):
            inner(*refs[:n_in], *refs[n_in + 1:])

        in_specs = list(in_specs) + [ANY_SPEC]
    call = pl.pallas_call(
        body, name=name, grid=grid, in_specs=in_specs, out_specs=out_specs, out_shape=out_shape,
        scratch_shapes=list(scratch),
        compiler_params=pltpu.CompilerParams(dimension_semantics=("arbitrary",) * len(grid), vmem_limit_bytes=limit))
    return call if dep is None else (lambda *args: call(*args, dep))


def _silu(x):
    return x * jax.nn.sigmoid(x)


def _dsilu(x):
    s = jax.nn.sigmoid(x)
    return s * (1.0 + x * (1.0 - s))


def _gelu_and_grad(x):
    c0 = math.sqrt(2.0 / math.pi)
    inner = c0 * (x + 0.044715 * x * x * x)
    t = jnp.tanh(inner)
    g = 0.5 * x * (1.0 + t)
    dg = 0.5 * (1.0 + t) + 0.5 * x * (1.0 - t * t) * c0 * (1.0 + 3 * 0.044715 * x * x)
    return g, dg


def _mm(a, b, mode, *, name, add=None, out_dtype=F32, tn=None, slabs=False, dep=None):
    if mode == "NN":
        (m, k), (k2, n) = a.shape, b.shape
    elif mode == "NT":
        (m, k), (n, k2) = a.shape, b.shape
    else:
        (k, m), (k2, n) = a.shape, b.shape
    assert k == k2, (a.shape, b.shape, mode)
    tm = _pick(m, (1088, 1024, 768, 512, 384, 256, 128))
    tn = tn or _pick(n, (512, 384, 256, 128))

    def block_bytes(tk):
        return (_nbytes((tm, tk), a.dtype) + _nbytes((tk, tn), b.dtype) + _nbytes((tm, tn), out_dtype)
                + (_nbytes((tm, tn), F32) if add is not None else 0))

    budget = MM_BLOCK_BUDGET // 2 if mode == "TN" else MM_BLOCK_BUDGET
    tk = k if block_bytes(k) <= budget else _pick(k, (2176, 2048, 1088, 1024, 768, 512, 384, 256, 128))
    nk = k // tk
    dims = {"NN": NN, "NT": NT, "TN": TN}[mode]

    def body(*refs):
        a_ref, b_ref = refs[0], refs[1]
        add_ref = refs[2] if add is not None else None
        o_ref = refs[3] if add is not None else refs[2]

        def finish(r):
            if add_ref is not None:
                r = r + add_ref[...]
            o_ref[...] = r.reshape(o_ref.shape).astype(o_ref.dtype)

        prod = _dot(_bf(a_ref[...]), _bf(b_ref[...]), dims)
        if nk == 1:
            finish(prod)
            return
        acc_ref = refs[-1]
        kk = pl.program_id(2)

        @pl.when(kk == 0)
        def _():
            acc_ref[...] = prod

        @pl.when(kk > 0)
        def _():
            acc_ref[...] += prod

        @pl.when(kk == nk - 1)
        def _():
            finish(acc_ref[...])

    if mode == "NN":
        a_spec = pl.BlockSpec((tm, tk), lambda i, j, kk: (i, kk))
        b_spec = pl.BlockSpec((tk, tn), lambda i, j, kk: (kk, j))
    elif mode == "NT":
        a_spec = pl.BlockSpec((tm, tk), lambda i, j, kk: (i, kk))
        b_spec = pl.BlockSpec((tn, tk), lambda i, j, kk: (j, kk))
    else:
        a_spec = pl.BlockSpec((tk, tm), lambda i, j, kk: (kk, i))
        b_spec = pl.BlockSpec((tk, tn), lambda i, j, kk: (kk, j))
    in_specs = [a_spec, b_spec]
    args = [a, b]
    if add is not None:
        in_specs.append(pl.BlockSpec((tm, tn), lambda i, j, kk: (i, j)))
        args.append(add)
    if slabs:
        out_shape = jax.ShapeDtypeStruct((n // tn, m, tn), out_dtype)
        out_spec = pl.BlockSpec((1, tm, tn), lambda i, j, kk: (j, i, 0))
    else:
        out_shape = jax.ShapeDtypeStruct((m, n), out_dtype)
        out_spec = pl.BlockSpec((tm, tn), lambda i, j, kk: (i, j))
    return _pc(body, name=name, grid=(m // tm, n // tn, nk), in_specs=in_specs, out_specs=out_spec,
               out_shape=out_shape, scratch=[] if nk == 1 else [pltpu.VMEM((tm, tn), F32)],
               vmem=block_bytes(tk) + (0 if nk == 1 else _nbytes((tm, tn), F32) // 2), dep=dep)(*args)


def _rms_fwd(x, g, *, name):
    r, d = x.shape
    tm = _pick(r, (256, 128))

    def body(x_ref, g_ref, o_ref):
        xv = x_ref[...]
        rstd = lax.rsqrt(jnp.mean(xv * xv, axis=1, keepdims=True) + NORM_EPS)
        o_ref[...] = (xv * rstd * g_ref[...]).astype(o_ref.dtype)

    return _pc(body, name=name, grid=(r // tm,),
               in_specs=[pl.BlockSpec((tm, d), lambda i: (i, 0)), pl.BlockSpec((1, d), lambda i: (0, 0))],
               out_specs=pl.BlockSpec((tm, d), lambda i: (i, 0)), out_shape=jax.ShapeDtypeStruct((r, d), BF16),
               vmem=tm * d * 6)(x, g.reshape(1, d))


def _rms_bwd(x, g, dxn, dres, *, name, dep=None):
    r, d = x.shape
    tm = _pick(r, (256, 128))

    def body(x_ref, g_ref, dxn_ref, dres_ref, dx_ref, dg_ref, db_ref):
        @pl.when(pl.program_id(0) == 0)
        def _():
            dg_ref[...] = jnp.zeros_like(dg_ref)

        xv = x_ref[...]
        rstd = lax.rsqrt(jnp.mean(xv * xv, axis=1, keepdims=True) + NORM_EPS)
        xh = xv * rstd
        dy = dxn_ref[...]
        dg_ref[...] += jnp.sum(dy * xh, axis=0, keepdims=True)
        dyg = dy * g_ref[...]
        dx_ref[...] = dres_ref[...] + rstd * (dyg - xh * jnp.mean(dyg * xh, axis=1, keepdims=True))

        db_ref[...] = _bf(dx_ref[...])

    row = pl.BlockSpec((tm, d), lambda i: (i, 0))
    vec = pl.BlockSpec((1, d), lambda i: (0, 0))
    return _pc(body, name=name, grid=(r // tm,), in_specs=[row, vec, row, row], out_specs=[row, vec, row],
               out_shape=[jax.ShapeDtypeStruct((r, d), F32), jax.ShapeDtypeStruct((1, d), F32),
                          jax.ShapeDtypeStruct((r, d), BF16)],
               vmem=tm * d * 18, dep=dep)(x, g.reshape(1, d), dxn, dres)


def _rms_bwd_first(x, g, dxn, dres, bsz, nc, *, name):
    d = x.shape[1]

    def body(x_ref, g_ref, dxn_ref, dres_ref, gx_ref, d0_ref, dg_ref):
        b, c = pl.program_id(0), pl.program_id(1)

        @pl.when((b == 0) & (c == 0))
        def _():
            dg_ref[...] = jnp.zeros_like(dg_ref)

        xv = x_ref[...]
        rstd = lax.rsqrt(jnp.mean(xv * xv, axis=1, keepdims=True) + NORM_EPS)
        xh = xv * rstd
        dy = dxn_ref[...]
        dg_ref[...] += jnp.sum(dy * xh, axis=0, keepdims=True)
        dyg = dy * g_ref[...]
        dx = dres_ref[...] + rstd * (dyg - xh * jnp.mean(dyg * xh, axis=1, keepdims=True))

        @pl.when(c == 0)
        def _():
            d0_ref[0] = dx

        @pl.when(c > 0)
        def _():
            gx_ref[0] = dx

    row = pl.BlockSpec((CHUNK, d), lambda b, c: (b * nc + c, 0))
    vec = pl.BlockSpec((1, d), lambda b, c: (0, 0))
    return _pc(body, name=name, grid=(bsz, nc), in_specs=[row, vec, row, row],
               out_specs=[pl.BlockSpec((1, CHUNK, d), lambda b, c: (b, jnp.maximum(c - 1, 0), 0)),
                          pl.BlockSpec((1, CHUNK, d), lambda b, c: (b, 0, 0)), vec],
               out_shape=[jax.ShapeDtypeStruct((bsz, (nc - 1) * CHUNK, d), F32),
                          jax.ShapeDtypeStruct((bsz, CHUNK, d), F32), jax.ShapeDtypeStruct((1, d), F32)],
               vmem=CHUNK * d * 24)(x, g.reshape(1, d), dxn, dres)


def _final_loss(h, g, target, bsz, nc):
    d = h.shape[1]

    def body(h_ref, g_ref, t_ref, loss_ref, dh_ref, dg_ref, db_ref):
        b, c = pl.program_id(0), pl.program_id(1)

        @pl.when((b == 0) & (c == 0))
        def _():
            loss_ref[...] = jnp.zeros_like(loss_ref)
            dg_ref[...] = jnp.zeros_like(dg_ref)

        @pl.when(c == 0)
        def _():
            dh_ref[...] = jnp.zeros_like(dh_ref)
            db_ref[...] = jnp.zeros_like(db_ref)

        @pl.when(c > 0)
        def _():
            xv = h_ref[...]
            rstd = lax.rsqrt(jnp.mean(xv * xv, axis=1, keepdims=True) + NORM_EPS)
            xh = xv * rstd
            gv = g_ref[...]
            err = xh * gv - t_ref[0]
            loss_ref[...] += 0.5 * jnp.sum(jnp.mean(err * err, axis=1, keepdims=True))
            dy = err * (1.0 / d)
            dg_ref[...] += jnp.sum(dy * xh, axis=0, keepdims=True)
            dyg = dy * gv
            dh = rstd * (dyg - xh * jnp.mean(dyg * xh, axis=1, keepdims=True))
            dh_ref[...] = dh
            db_ref[...] = _bf(dh)

    row = pl.BlockSpec((CHUNK, d), lambda b, c: (b * nc + c, 0))
    vec = pl.BlockSpec((1, d), lambda b, c: (0, 0))
    return _pc(body, name="final_loss", grid=(bsz, nc),
               in_specs=[row, vec, pl.BlockSpec((1, CHUNK, d), lambda b, c: (b, jnp.maximum(c - 1, 0), 0))],
               out_specs=[pl.BlockSpec((8, 128), lambda b, c: (0, 0)), row, vec, row],
               out_shape=[jax.ShapeDtypeStruct((8, 128), F32), jax.ShapeDtypeStruct(h.shape, F32),
                          jax.ShapeDtypeStruct((1, d), F32), jax.ShapeDtypeStruct(h.shape, BF16)],
               vmem=CHUNK * d * 18)(h, g.reshape(1, d), target)


def _adamw(w, m, v, gparts, *, name):
    r, c = w.shape
    tr = _pick(r, (256, 128)) if r * c * 4 > (1 << 20) else r

    def body(w_ref, m_ref, v_ref, gp_ref, g_ref, d_ref, nm_ref, nv_ref):
        g = gp_ref[0].astype(F32)
        for j in range(1, N_DEV):
            g = g + gp_ref[j].astype(F32)
        mm = ADAM_B1 * m_ref[...] + (1.0 - ADAM_B1) * g
        vv = ADAM_B2 * v_ref[...] + (1.0 - ADAM_B2) * (g * g)
        m_hat = mm / (1.0 - ADAM_B1 ** ADAM_STEP)
        v_hat = vv / (1.0 - ADAM_B2 ** ADAM_STEP)
        g_ref[...] = g
        d_ref[...] = -ADAM_LR * (m_hat / (jnp.sqrt(v_hat) + ADAM_EPS) + ADAM_WD * w_ref[...])
        nm_ref[...] = mm
        nv_ref[...] = vv

    blk = pl.BlockSpec((tr, c), lambda i: (i, 0))
    out = jax.ShapeDtypeStruct((r, c), F32)
    return _pc(body, name=name, grid=(r // tr,),
               in_specs=[blk, blk, blk, pl.BlockSpec((N_DEV, tr, c), lambda i: (0, i, 0))],
               out_specs=[blk, blk, blk, blk], out_shape=[out, out, out, out],
               vmem=tr * c * (4 * 7 + N_DEV * jnp.dtype(gparts.dtype).itemsize))(w, m, v, gparts)


PEERS = (1, 2, 4, 6, 3, 5, 7)
HBM_SPEC = pl.BlockSpec(memory_space=pltpu.HBM)
SEM_SPEC = pl.BlockSpec(memory_space=pltpu.SEMAPHORE)
SIDE_EFFECT = pltpu.SideEffectType.DATAFLOW_SIDE_EFFECTING


def _peer(p):
    x, y, c = lax.axis_index("x"), lax.axis_index("y"), lax.axis_index("c")
    tx, ty, tc = x ^ ((p >> 2) & 1), y ^ ((p >> 1) & 1), c ^ (p & 1)
    return (tx, ty, tc), 4 * tx + 2 * ty + tc


def _place_own(a, kind, *, name):
    rows, cols = a.shape[-2:]
    small = _nbytes((rows, cols), a.dtype) <= (2 << 20)
    tr = rows if small else _pick(rows, (512, 256, 128, 64, 32, 16))
    me = (4 * lax.axis_index("x") + 2 * lax.axis_index("y") + lax.axis_index("c")).astype(jnp.int32).reshape(1)

    def body(me_ref, in_ref, out_ref):
        out_ref[...] = in_ref[...].reshape(out_ref.shape)

    if kind == "a2a":
        in_spec = pl.BlockSpec((1, tr, cols), lambda i, me_ref: (me_ref[0], i, 0))
    else:
        in_spec = pl.BlockSpec((tr, cols), lambda i, me_ref: (i, 0))
    return pl.pallas_call(
        body, name=name, out_shape=jax.ShapeDtypeStruct((N_DEV, rows, cols), a.dtype),
        grid_spec=pltpu.PrefetchScalarGridSpec(
            num_scalar_prefetch=1, grid=(rows // tr,), in_specs=[in_spec],
            out_specs=pl.BlockSpec((1, tr, cols), lambda i, me_ref: (me_ref[0], i, 0))))(me, a)


def _exchange_copies(ins, lands, send_sems, recv_sems, kinds, incoming, peers=PEERS):
    me = 4 * lax.axis_index("x") + 2 * lax.axis_index("y") + lax.axis_index("c")
    copies = []
    for i, kind in enumerate(kinds):
        for p in peers:
            dev, tgt = _peer(p)
            k = i * (N_DEV - 1) + p - 1
            copies.append(pltpu.make_async_remote_copy(
                src_ref=ins[i].at[tgt] if kind == "a2a" else ins[i], dst_ref=lands[i].at[tgt if incoming else me],
                send_sem=send_sems.at[k], recv_sem=recv_sems.at[k], device_id=dev, device_id_type=MESH))
    return copies


def _exchange_start(arrays, kinds, *, name, dep=None, peers=PEERS):
    n = len(arrays)
    lands = [_place_own(a, k, name=f"{name}_own{i}") for i, (a, k) in enumerate(zip(arrays, kinds))]
    extra = [] if dep is None else [dep]

    def body(*refs):
        ins, lnd = refs[:n], refs[n:2 * n]
        send_sems, recv_sems = refs[2 * n + len(extra)], refs[2 * n + len(extra) + 1]
        token = refs[-1]
        for cp in _exchange_copies(ins, lnd, send_sems, recv_sems, kinds, False, peers):
            cp.start()
        token[...] = jnp.zeros_like(token)

    sem = pltpu.SemaphoreType.DMA((n * (N_DEV - 1),))
    outs = pl.pallas_call(
        body, name=name, in_specs=[HBM_SPEC] * (2 * n) + [ANY_SPEC] * len(extra),
        out_specs=[SEM_SPEC, SEM_SPEC] + [HBM_SPEC] * (2 * n) + [pl.BlockSpec(memory_space=pltpu.VMEM)],
        out_shape=[sem, sem] + [pltpu.HBM(a.shape, a.dtype) for a in arrays + lands]
        + [jax.ShapeDtypeStruct((8, 128), F32)],
        input_output_aliases={i: 2 + i for i in range(2 * n)},
        compiler_params=pltpu.CompilerParams(has_side_effects=SIDE_EFFECT),
    )(*[pltpu.with_memory_space_constraint(a, pltpu.HBM) for a in arrays + lands], *extra)
    return dict(send=outs[0], recv=outs[1], ins=list(outs[2:2 + n]), lands=list(outs[2 + n:2 + 2 * n]),
                token=outs[-1], kinds=kinds, name=name, peers=peers)


def _exchange_wait(h, after):
    n = len(h["ins"])
    kinds = h["kinds"]

    def body(*refs):
        ins, lnd = refs[:n], refs[n:2 * n]
        send_sems, recv_sems = refs[2 * n], refs[2 * n + 1]
        copies = _exchange_copies(ins, lnd, send_sems, recv_sems, kinds, True, h["peers"])
        for cp in copies:
            cp.wait_recv()
        for cp in copies:
            cp.wait_send()

    arrs = h["ins"] + h["lands"]
    after = list(after) if isinstance(after, (list, tuple)) else [after]
    outs = pl.pallas_call(
        body, name=h["name"] + "_wait", in_specs=[HBM_SPEC] * (2 * n) + [SEM_SPEC, SEM_SPEC] + [ANY_SPEC] * len(after),
        out_specs=[HBM_SPEC] * (2 * n), out_shape=[pltpu.HBM(a.shape, a.dtype) for a in arrs],
        input_output_aliases={i: i for i in range(2 * n)},
        compiler_params=pltpu.CompilerParams(has_side_effects=SIDE_EFFECT),
    )(*arrs, h["send"], h["recv"], *after)
    return list(outs[n:])


SAME_CORE = (0, 2, 4, 6)


def _forward_copies(land, send_sems, recv_sems, incoming):
    me = 4 * lax.axis_index("x") + 2 * lax.axis_index("y") + lax.axis_index("c")
    dev, sibling = _peer(1)
    return [pltpu.make_async_remote_copy(
        src_ref=land.at[me ^ q], dst_ref=land.at[(sibling if incoming else me) ^ q],
        send_sem=send_sems.at[j], recv_sem=recv_sems.at[j], device_id=dev, device_id_type=MESH)
        for j, q in enumerate(SAME_CORE)]


def _sibling_forward_start(land, *, name, dep=None):
    extra = [] if dep is None else [dep]

    def body(*refs):
        land_ref, send_sems, recv_sems, token = refs[0], refs[1 + len(extra)], refs[2 + len(extra)], refs[-1]
        for cp in _forward_copies(land_ref, send_sems, recv_sems, False):
            cp.start()
        token[...] = jnp.zeros_like(token)

    sem = pltpu.SemaphoreType.DMA((len(SAME_CORE),))
    outs = pl.pallas_call(
        body, name=name, in_specs=[HBM_SPEC] + [ANY_SPEC] * len(extra),
        out_specs=[SEM_SPEC, SEM_SPEC, HBM_SPEC, pl.BlockSpec(memory_space=pltpu.VMEM)],
        out_shape=[sem, sem, pltpu.HBM(land.shape, land.dtype), jax.ShapeDtypeStruct((8, 128), F32)],
        input_output_aliases={0: 2}, compiler_params=pltpu.CompilerParams(has_side_effects=SIDE_EFFECT),
    )(pltpu.with_memory_space_constraint(land, pltpu.HBM), *extra)
    return dict(send=outs[0], recv=outs[1], land=outs[2], token=outs[3], name=name)


def _sibling_forward_wait(h, after):
    def body(*refs):
        copies = _forward_copies(refs[0], refs[1], refs[2], True)
        for cp in copies:
            cp.wait_recv()
        for cp in copies:
            cp.wait_send()

    return pl.pallas_call(
        body, name=h["name"] + "_wait", in_specs=[HBM_SPEC, SEM_SPEC, SEM_SPEC, ANY_SPEC], out_specs=HBM_SPEC,
        out_shape=pltpu.HBM(h["land"].shape, h["land"].dtype), input_output_aliases={0: 0},
        compiler_params=pltpu.CompilerParams(has_side_effects=SIDE_EFFECT),
    )(h["land"], h["send"], h["recv"], after)


def _s5_params(lam_re, lam_im, log_dt, b_re, b_im):
    dt = jnp.exp(log_dt)[:, None]
    mag = jnp.exp(lam_re * dt)
    ar, ai = mag * jnp.cos(lam_im * dt), mag * jnp.sin(lam_im * dt)
    den = lam_re * lam_re + lam_im * lam_im
    qr = ((ar - 1.0) * lam_re + ai * lam_im) / den
    qi = (ai * lam_re - (ar - 1.0) * lam_im) / den
    bbr = qr[..., None] * b_re - qi[..., None] * b_im
    bbi = qr[..., None] * b_im + qi[..., None] * b_re
    return ar, ai, bbr, bbi


def _s5_power_table(ar, ai):
    pr, pi = ar.reshape(1, -1), ai.reshape(1, -1)
    while pr.shape[0] < 8:
        sr, si = pr[-1:], pi[-1:]
        pr, pi = (jnp.concatenate([pr, pr * sr - pi * si], axis=0), jnp.concatenate([pi, pr * si + pi * sr], axis=0))
    return pr, pi


def _blockdiag(w, rows, cols):
    w = w.reshape(S5_GB, S5_GB, rows, cols)
    eye = jnp.eye(S5_GB, dtype=w.dtype)
    return jnp.einsum("abrc,bd->abrdc", w, eye).reshape(S5_GB, S5_GB * rows, S5_GB * cols)


def _blockdiag_extract(w, rows, cols):
    w = w.reshape(S5_GB, S5_GB, rows, S5_GB, cols)
    return jnp.einsum("abrbc->abrc", w).reshape(S5_GROUPS, rows, cols)


def _s5_scan_specs(bsz, nc, rev):
    def cc(c):
        return (nc - 1 - c) if rev else c

    return dict(
        u=pl.BlockSpec((bsz, CHUNK, CHUNK), lambda g, c: (0, cc(c), g)),
        x=pl.BlockSpec((bsz, CHUNK, S5_LANES), lambda g, c: (0, cc(c), g)),
        wb=pl.BlockSpec((1, CHUNK, S5_LANES), lambda g, c: (g, 0, 0)),
        wc=pl.BlockSpec((1, S5_LANES, CHUNK), lambda g, c: (g, 0, 0)),
        tab=pl.BlockSpec((8, S5_LANES), lambda g, c: (0, g)),
        step=pl.BlockSpec((8, S5_LANES), lambda g, c: (0, g)),
        d=pl.BlockSpec((1, CHUNK), lambda g, c: (0, g)),
        lane=pl.BlockSpec((1, S5_LANES), lambda g, c: (0, g)),
        xprev=pl.BlockSpec((bsz, 8, S5_LANES), lambda g, c: (0, jnp.maximum(cc(c) * (CHUNK // 8) - 1, 0), g)),
    )


def _s5_fwd(u, wbr, wbi, pr, pi, sr, si, wcr, wci, d, bsz, nc):
    r = u.shape[0]
    tp = r // bsz
    sp = _s5_scan_specs(bsz, nc, False)

    def body(u_all, wbr_ref, wbi_ref, pr_ref, pi_ref, sr_ref, si_ref, wcr_ref, wci_ref, d_ref,
             xr_all, xi_all, y1_all, g_all, cr_sall, ci_sall):
        @pl.when(pl.program_id(1) == 0)
        def _():
            cr_sall[...] = jnp.zeros_like(cr_sall)
            ci_sall[...] = jnp.zeros_like(ci_sall)

        for bi in range(bsz):
            one(u_all.at[bi], wbr_ref, wbi_ref, pr_ref, pi_ref, sr_ref, si_ref, wcr_ref, wci_ref, d_ref,
                xr_all.at[bi], xi_all.at[bi], y1_all.at[bi], g_all.at[bi], cr_sall.at[bi], ci_sall.at[bi])

    def one(u_ref, wbr_ref, wbi_ref, pr_ref, pi_ref, sr_ref, si_ref, wcr_ref, wci_ref, d_ref,
            xr_ref, xi_ref, y1_ref, g_ref, cr_s, ci_s):
        uv = u_ref[...]
        ub = _bf(uv)
        xr, xi = _dot(ub, wbr_ref[0]), _dot(ub, wbi_ref[0])
        sub = lax.broadcasted_iota(jnp.int32, (CHUNK, S5_LANES), 0) % 8
        for k in range(3):
            s = 1 << k
            ar, ai = sr_ref[k:k + 1, :], si_ref[k:k + 1, :]
            hr = jnp.where(sub >= s, pltpu.roll(xr, s, 0), 0.0)
            hi = jnp.where(sub >= s, pltpu.roll(xi, s, 0), 0.0)
            xr, xi = xr + (ar * hr - ai * hi), xi + (ar * hi + ai * hr)
        cr, ci = cr_s[...], ci_s[...]
        tr, ti = pr_ref[...], pi_ref[...]
        outr, outi = [], []
        for g8 in range(CHUNK // 8):
            br, bi = xr[8 * g8:8 * g8 + 8, :], xi[8 * g8:8 * g8 + 8, :]
            br, bi = br + (tr * cr - ti * ci), bi + (tr * ci + ti * cr)
            cr, ci = br[7:8, :], bi[7:8, :]
            outr.append(br)
            outi.append(bi)
        xr, xi = jnp.concatenate(outr, axis=0), jnp.concatenate(outi, axis=0)
        cr_s[...] = cr
        ci_s[...] = ci
        xr_ref[...] = xr
        xi_ref[...] = xi
        y = _dot(_bf(xr), wcr_ref[0]) - _dot(_bf(xi), wci_ref[0]) + d_ref[...] * uv
        y1_ref[...] = y
        g_ref[...] = _bf(_gelu_and_grad(y)[0])

    ns = S5_GROUPS * S5_STATE
    xr, xi, y1, g = _pc(
        body, name="s5_fwd", grid=(S5_GB, nc),
        in_specs=[sp["u"], sp["wb"], sp["wb"], sp["tab"], sp["tab"], sp["step"], sp["step"], sp["wc"], sp["wc"],
                  sp["d"]],
        out_specs=[sp["x"], sp["x"], sp["u"], sp["u"]],
        out_shape=[jax.ShapeDtypeStruct((bsz, tp, ns), F32)] * 2
        + [jax.ShapeDtypeStruct((bsz, tp, S5_WIDTH), F32), jax.ShapeDtypeStruct((bsz, tp, S5_WIDTH), BF16)],
        scratch=[pltpu.VMEM((bsz, 1, S5_LANES), F32)] * 2, vmem=8 << 20,
    )(_seq(u, bsz), wbr, wbi, pr, pi, sr, si, wcr, wci, d)
    return xr.reshape(r, ns), xi.reshape(r, ns), y1.reshape(r, S5_WIDTH), g.reshape(r, S5_WIDTH)


def _s5_post(y1, glu_pre, glu_b, z):
    r, w = y1.shape
    tm = _pick(r, (256, 128))

    def body(y_ref, p_ref, b_ref, z_ref, o_ref):
        g = _gelu_and_grad(y_ref[...])[0]
        o_ref[...] = _bf(g * jax.nn.sigmoid(p_ref[...] + b_ref[...]) * _silu(z_ref[...]))

    row = pl.BlockSpec((tm, w), lambda i: (i, 0))
    return _pc(body, name="s5_post", grid=(r // tm,), in_specs=[row, row, pl.BlockSpec((1, w), lambda i: (0, 0)), row],
               out_specs=row, out_shape=jax.ShapeDtypeStruct((r, w), BF16), vmem=tm * w * 16)(y1, glu_pre, glu_b, z)


def _s5_post_bwd(dya, y1, glu_pre, glu_b, z):
    r, w = y1.shape
    tm = _pick(r, (256, 128))

    def body(dy_ref, y_ref, p_ref, b_ref, z_ref, dz_ref, dp_ref, dg_ref, db_ref):
        @pl.when(pl.program_id(0) == 0)
        def _():
            db_ref[...] = jnp.zeros_like(db_ref)

        g = _gelu_and_grad(y_ref[...])[0]
        s = jax.nn.sigmoid(p_ref[...] + b_ref[...])
        zv = z_ref[...]
        dy = dy_ref[...]
        do = dy * _silu(zv)
        dz_ref[...] = _bf(dy * g * s * _dsilu(zv))
        dp = do * g * s * (1.0 - s)
        dp_ref[...] = _bf(dp)
        db_ref[...] += jnp.sum(dp, axis=0, keepdims=True)
        dg_ref[...] = do * s

    row = pl.BlockSpec((tm, w), lambda i: (i, 0))
    vec = pl.BlockSpec((1, w), lambda i: (0, 0))
    return _pc(body, name="s5_post_bwd", grid=(r // tm,), in_specs=[row, row, row, vec, row],
               out_specs=[row, row, row, vec],
               out_shape=[jax.ShapeDtypeStruct((r, w), BF16), jax.ShapeDtypeStruct((r, w), BF16),
                          jax.ShapeDtypeStruct((r, w), F32), jax.ShapeDtypeStruct((1, w), F32)],
               vmem=tm * w * 24)(dya, y1, glu_pre, glu_b, z)


def _s5_bwd(dg, y1, u, xr, xi, wbr, wbi, qr, qi, sr, si, wcr, wci, d, bsz, nc):
    r = u.shape[0]
    tp = r // bsz
    sp = _s5_scan_specs(bsz, nc, True)

    def body(dg_all, y1_all, u_all, xr_all, xi_all, xpr_all, xpi_all, wbr_ref, wbi_ref, qr_ref, qi_ref, sr_ref, si_ref,
             wcr_ref, wci_ref, d_ref, du_all, dd_ref, dwcr_ref, dwci_ref, dwbr_ref, dwbi_ref, dar_ref, dai_ref,
             cr_sall, ci_sall):
        c = pl.program_id(1)

        @pl.when(c == 0)
        def _():
            for ref in (dd_ref, dwcr_ref, dwci_ref, dwbr_ref, dwbi_ref, dar_ref, dai_ref, cr_sall, ci_sall):
                ref[...] = jnp.zeros_like(ref)

        for bi in range(bsz):
            one(c, dg_all.at[bi], y1_all.at[bi], u_all.at[bi], xr_all.at[bi], xi_all.at[bi], xpr_all.at[bi],
                xpi_all.at[bi], wbr_ref, wbi_ref, qr_ref, qi_ref, sr_ref, si_ref, wcr_ref, wci_ref, d_ref,
                du_all.at[bi], dd_ref, dwcr_ref, dwci_ref, dwbr_ref, dwbi_ref, dar_ref, dai_ref, cr_sall.at[bi],
                ci_sall.at[bi])

    def one(c, dg_ref, y1_ref, u_ref, xr_ref, xi_ref, xpr_ref, xpi_ref, wbr_ref, wbi_ref, qr_ref, qi_ref, sr_ref, si_ref,
            wcr_ref, wci_ref, d_ref, du_ref, dd_ref, dwcr_ref, dwci_ref, dwbr_ref, dwbi_ref, dar_ref, dai_ref,
            cr_s, ci_s):
        uv = u_ref[...]
        ub = _bf(uv)
        dy = dg_ref[...] * _gelu_and_grad(y1_ref[...])[1]
        dd_ref[...] += jnp.sum(dy * uv, axis=0, keepdims=True)
        dyb = _bf(dy)
        xr, xi = xr_ref[...], xi_ref[...]
        dwcr_ref[0] += _dot(_bf(xr), dyb, TN)
        dwci_ref[0] -= _dot(_bf(xi), dyb, TN)
        lr, li = _dot(dyb, wcr_ref[0], NT), -_dot(dyb, wci_ref[0], NT)
        row = lax.broadcasted_iota(jnp.int32, (CHUNK, S5_LANES), 0)
        sub = row % 8
        for k in range(3):
            s = 1 << k
            ar, ai = sr_ref[k:k + 1, :], si_ref[k:k + 1, :]
            hr = jnp.where(sub < 8 - s, pltpu.roll(lr, CHUNK - s, 0), 0.0)
            hi = jnp.where(sub < 8 - s, pltpu.roll(li, CHUNK - s, 0), 0.0)
            lr, li = lr + (ar * hr + ai * hi), li + (ar * hi - ai * hr)
        cr, ci = cr_s[...], ci_s[...]
        tr, ti = qr_ref[...], qi_ref[...]
        outr, outi = [], []
        for g8 in reversed(range(CHUNK // 8)):
            br, bi = lr[8 * g8:8 * g8 + 8, :], li[8 * g8:8 * g8 + 8, :]
            br, bi = br + (tr * cr + ti * ci), bi + (tr * ci - ti * cr)
            cr, ci = br[0:1, :], bi[0:1, :]
            outr.append(br)
            outi.append(bi)
        lr, li = jnp.concatenate(outr[::-1], axis=0), jnp.concatenate(outi[::-1], axis=0)
        cr_s[...] = cr
        ci_s[...] = ci
        lrb, lib = _bf(lr), _bf(li)
        du_ref[...] = _bf(_dot(lrb, wbr_ref[0], NT) + _dot(lib, wbi_ref[0], NT) + dy * d_ref[...])
        dwbr_ref[0] += _dot(ub, lrb, TN)
        dwbi_ref[0] += _dot(ub, lib, TN)
        first = c == nc - 1
        pr0 = jnp.where(first, 0.0, xpr_ref[7:8, :])
        pi0 = jnp.where(first, 0.0, xpi_ref[7:8, :])
        xpr = jnp.where(row == 0, pr0, pltpu.roll(xr, 1, 0))
        xpi = jnp.where(row == 0, pi0, pltpu.roll(xi, 1, 0))
        dar_ref[...] += jnp.sum(lr * xpr + li * xpi, axis=0, keepdims=True)
        dai_ref[...] += jnp.sum(li * xpr - lr * xpi, axis=0, keepdims=True)

    st = jax.ShapeDtypeStruct
    xr3, xi3 = _seq(xr, bsz), _seq(xi, bsz)
    outs = _pc(body, name="s5_bwd", grid=(S5_GB, nc),
               in_specs=[sp["u"], sp["u"], sp["u"], sp["x"], sp["x"], sp["xprev"], sp["xprev"], sp["wb"], sp["wb"],
                         sp["tab"], sp["tab"], sp["step"], sp["step"], sp["wc"], sp["wc"], sp["d"]],
               out_specs=[sp["u"], sp["d"], sp["wc"], sp["wc"], sp["wb"], sp["wb"], sp["lane"], sp["lane"]],
               out_shape=[st((bsz, tp, S5_WIDTH), BF16), st((1, S5_WIDTH), F32),
                          st((S5_GB, S5_LANES, CHUNK), F32), st((S5_GB, S5_LANES, CHUNK), F32),
                          st((S5_GB, CHUNK, S5_LANES), F32), st((S5_GB, CHUNK, S5_LANES), F32),
                          st((1, S5_GROUPS * S5_STATE), F32), st((1, S5_GROUPS * S5_STATE), F32)],
               scratch=[pltpu.VMEM((bsz, 1, S5_LANES), F32)] * 2, vmem=12 << 20,
               )(_seq(dg, bsz), _seq(y1, bsz), _seq(u, bsz), xr3, xi3, xr3, xi3, wbr, wbi, qr, qi, sr, si, wcr, wci, d)
    return (outs[0].reshape(r, S5_WIDTH),) + tuple(outs[1:])


def _s5_layer_fwd(u, prm, glu_w, bsz, nc):
    xr, xi, y1, g = _s5_fwd(u, prm["wbr"], prm["wbi"], prm["pr"], prm["pi"], prm["sr"], prm["si"], prm["wcr"],
                            prm["wci"], prm["d"], bsz, nc)
    glu_pre = _mm(g, glu_w(y1) if callable(glu_w) else glu_w, "NN", name="s5_glu")
    return dict(xr=xr, xi=xi, y1=y1, g=g, glu_pre=glu_pre)


def _s5_layer_bwd(dya, u, z, sv, prm, pvjp, glu_w, glu_b, bsz, nc):
    dz, dglu, dg_direct, dglu_b = _s5_post_bwd(dya, sv["y1"], sv["glu_pre"], glu_b, z)
    dg = _mm(dglu, glu_w, "NT", name="s5_dg", add=dg_direct)
    dglu_w = _mm(sv["g"], dglu, "TN", name="s5_dglu_w")
    du, dd, dwcr, dwci, dwbr, dwbi, dar, dai = _s5_bwd(
        dg, sv["y1"], u, sv["xr"], sv["xi"], prm["wbr"], prm["wbi"], prm["qr"], prm["qi"], prm["sr"], prm["si"],
        prm["wcr"], prm["wci"], prm["d"], bsz, nc)
    dbbr = jnp.swapaxes(_blockdiag_extract(dwbr, S5_GROUP_SIZE, S5_STATE), 1, 2)
    dbbi = jnp.swapaxes(_blockdiag_extract(dwbi, S5_GROUP_SIZE, S5_STATE), 1, 2)
    dlr, dli, dldt, dbr, dbi = pvjp((dar.reshape(S5_GROUPS, S5_STATE), dai.reshape(S5_GROUPS, S5_STATE), dbbr, dbbi))
    grads = dict(
        s5_lambda_re=dlr, s5_lambda_im=dli, s5_log_dt=dldt, s5_b_re=dbr, s5_b_im=dbi,
        s5_c_re=jnp.swapaxes(_blockdiag_extract(dwcr, S5_STATE, S5_GROUP_SIZE), 1, 2),
        s5_c_im=jnp.swapaxes(_blockdiag_extract(dwci, S5_STATE, S5_GROUP_SIZE), 1, 2),
        s5_d=dd, s5_glu_w=dglu_w, s5_glu_b=dglu_b)
    return du, dz, grads


def _s5_tables(lam_re, lam_im, log_dt, b_re, b_im, c_re, c_im, d):
    (ar, ai, bbr, bbi), vjp = jax.vjp(_s5_params, lam_re, lam_im, log_dt, b_re, b_im)
    pr, pi = _s5_power_table(lax.stop_gradient(ar), lax.stop_gradient(ai))
    steps = [0, 1, 3, 7, 7, 7, 7, 7]
    flip8 = (jnp.arange(8)[:, None] + jnp.arange(8)[None, :] == 7).astype(F32)
    prm = dict(
        wbr=_bf(_blockdiag(jnp.swapaxes(bbr, 1, 2), S5_GROUP_SIZE, S5_STATE)),
        wbi=_bf(_blockdiag(jnp.swapaxes(bbi, 1, 2), S5_GROUP_SIZE, S5_STATE)),
        wcr=_bf(_blockdiag(jnp.swapaxes(c_re, 1, 2), S5_STATE, S5_GROUP_SIZE)),
        wci=_bf(_blockdiag(jnp.swapaxes(c_im, 1, 2), S5_STATE, S5_GROUP_SIZE)),
        pr=pr, pi=pi, qr=jnp.dot(flip8, pr, precision=lax.Precision.HIGHEST),
        qi=jnp.dot(flip8, pi, precision=lax.Precision.HIGHEST),
        sr=jnp.concatenate([pr[i:i + 1] for i in steps], axis=0),
        si=jnp.concatenate([pi[i:i + 1] for i in steps], axis=0), d=d.reshape(1, S5_WIDTH))
    return prm, vjp


def _tile16(p8):
    return jnp.concatenate([p8] * (CHUNK // 8), axis=0)


def _shift_down(x, halo, s, row):
    return jnp.where(row >= s, pltpu.roll(x, s, 0), pltpu.roll(halo, s, 0))


def _shift_up(x, halo, s, row):
    return jnp.where(row < CHUNK - s, pltpu.roll(x, CHUNK - s, 0), pltpu.roll(halo, CHUNK - s, 0))


def _conv_specs(nc, tw):
    def chunk(b, c):
        return b * nc + c

    return dict(
        x=pl.BlockSpec((CHUNK, tw), lambda j, b, c: (chunk(b, c), j)),
        prev=pl.BlockSpec((8, tw), lambda j, b, c: (jnp.maximum(chunk(b, c) * (CHUNK // 8) - 1, 0), j)),
        nxt=pl.BlockSpec((8, tw), lambda j, b, c: ((b * nc + jnp.minimum(c + 1, nc - 1)) * (CHUNK // 8), j)),
        w=pl.BlockSpec((ML_CONV, tw), lambda j, b, c: (0, j)),
        vec=pl.BlockSpec((1, tw), lambda j, b, c: (0, j)),
    )


def _conv_fwd(x, w, bias, bsz, nc, *, name):
    r, wd = x.shape
    tw = _pick(wd, (2048, 1536, 1024, 512, 384, 256, 128))
    sp = _conv_specs(nc, tw)

    def body(x_ref, p_ref, w_ref, b_ref, o_ref):
        c = pl.program_id(2)
        xv = x_ref[...]
        row = lax.broadcasted_iota(jnp.int32, xv.shape, 0)
        halo = jnp.where(c == 0, 0.0, _tile16(p_ref[...]))
        acc = b_ref[...] + w_ref[3:4, :] * xv
        for s in (1, 2, 3):
            acc = acc + w_ref[3 - s:4 - s, :] * _shift_down(xv, halo, s, row)
        o_ref[...] = acc

    return _pc(body, name=name, grid=(wd // tw, bsz, nc), in_specs=[sp["x"], sp["prev"], sp["w"], sp["vec"]],
               out_specs=sp["x"], out_shape=jax.ShapeDtypeStruct((r, wd), F32), vmem=CHUNK * tw * 16,
               )(x, x, w, bias.reshape(1, wd))


def _conv_bwd(dpre, x, w, bsz, nc, *, name, add=None):
    r, wd = x.shape
    tw = _pick(wd, (2048, 1536, 1024, 512, 384, 256, 128))
    sp = _conv_specs(nc, tw)

    def body(*refs):
        d_ref, n_ref, x_ref, p_ref, w_ref = refs[:5]
        add_ref = refs[5] if add is not None else None
        dx_ref, dw_ref, db_ref = refs[-3:]
        b, c = pl.program_id(1), pl.program_id(2)

        @pl.when((b == 0) & (c == 0))
        def _():
            dw_ref[...] = jnp.zeros_like(dw_ref)
            db_ref[...] = jnp.zeros_like(db_ref)

        dv, xv = d_ref[...], x_ref[...]
        row = lax.broadcasted_iota(jnp.int32, xv.shape, 0)
        dhalo = jnp.where(c == nc - 1, 0.0, _tile16(n_ref[...]))
        xhalo = jnp.where(c == 0, 0.0, _tile16(p_ref[...]))
        dx = w_ref[3:4, :] * dv
        for s in (1, 2, 3):
            dx = dx + w_ref[3 - s:4 - s, :] * _shift_up(dv, dhalo, s, row)
        if add_ref is not None:
            dx = dx + add_ref[...]
        dx_ref[...] = _bf(dx)
        db_ref[...] += jnp.sum(dv, axis=0, keepdims=True)
        dw_ref[3:4, :] += jnp.sum(dv * xv, axis=0, keepdims=True)
        for s in (1, 2, 3):
            dw_ref[3 - s:4 - s, :] += jnp.sum(dv * _shift_down(xv, xhalo, s, row), axis=0, keepdims=True)

    ins = [dpre, dpre, x, x, w] + ([add] if add is not None else [])
    specs = [sp["x"], sp["nxt"], sp["x"], sp["prev"], sp["w"]] + ([sp["x"]] if add is not None else [])
    return _pc(body, name=name, grid=(wd // tw, bsz, nc), in_specs=specs, out_specs=[sp["x"], sp["w"], sp["vec"]],
               out_shape=[jax.ShapeDtypeStruct((r, wd), BF16), jax.ShapeDtypeStruct((ML_CONV, wd), F32),
                          jax.ShapeDtypeStruct((1, wd), F32)], vmem=CHUNK * tw * 24)(*ins)


ML_SCALE = ML_DH ** -0.5


def _headwise_expand(w):
    tiled = jnp.tile(w.reshape(ML_HEADS, ML_DH, QKV_BLOCK), (1, 1, ML_DH // QKV_BLOCK))
    blk = jnp.arange(ML_DH) // QKV_BLOCK
    return jnp.where(blk[:, None] == blk[None, :], tiled, 0.0)


def _headwise_extract(w):
    return w[:, :, :QKV_BLOCK].reshape(ML_HEADS * ML_DH // QKV_BLOCK, QKV_BLOCK, QKV_BLOCK)


def _ml_pre(pre, x, wq, wk, wv, wgq, wgk, wgv, bsz, nc):
    r = x.shape[0]
    tr = _pick(r, (256, 128))
    hrow = pl.BlockSpec((tr, ML_DH), lambda h, i: (i, h))
    wexp = pl.BlockSpec((1, ML_DH, ML_DH), lambda h, i: (h, 0, 0))
    wg = pl.BlockSpec((ML_DH, CHUNK), lambda h, i: (h, 0))

    def body(pre_ref, x_ref, wq_ref, wk_ref, wv_ref, gq_ref, gk_ref, gv_ref, q_ref, qs_ref, k_ref, v_ref, gt_ref):
        xcb = _bf(_silu(pre_ref[...]))
        q = _dot(xcb, wq_ref[0])
        k = _dot(xcb, wk_ref[0])
        v = _dot(_bf(x_ref[...]), wv_ref[0])
        qb, kb, vb = _bf(q), _bf(k), _bf(v)
        q_ref[...] = qb
        qs_ref[...] = _bf(q * ML_SCALE)
        k_ref[...] = kb
        v_ref[...] = vb
        gt_ref[0] = _dot(qb, gq_ref[...]) + _dot(kb, gk_ref[...]) + _dot(vb, gv_ref[...])

    o = jax.ShapeDtypeStruct((r, ML_WIDTH), BF16)
    q, qs, k, v, gates8 = _pc(
        body, name="ml_pre", grid=(ML_HEADS, r // tr),
        in_specs=[hrow, hrow, wexp, wexp, wexp, wg, wg, wg],
        out_specs=[hrow, hrow, hrow, hrow, pl.BlockSpec((1, tr, CHUNK), lambda h, i: (h, i, 0))],
        out_shape=[o, o, o, o, jax.ShapeDtypeStruct((ML_HEADS, r, CHUNK), F32)], vmem=6 << 20,
    )(pre, x, wq, wk, wv, wgq, wgk, wgv)

    def sum_body(g_ref, o_ref):
        acc = g_ref[0]
        for j in range(1, ML_HEADS):
            acc = acc + g_ref[j]
        o_ref[...] = acc

    gates = _pc(sum_body, name="ml_gates_sum", grid=(r // tr,),
                in_specs=[pl.BlockSpec((ML_HEADS, tr, CHUNK), lambda i: (0, i, 0))],
                out_specs=pl.BlockSpec((tr, CHUNK), lambda i: (i, 0)),
                out_shape=jax.ShapeDtypeStruct((r, CHUNK), F32), vmem=2 << 20)(gates8)
    return q, qs, k, v, gates


def _tri(rev):
    r = lax.broadcasted_iota(jnp.int32, (CHUNK, CHUNK), 0)
    c = lax.broadcasted_iota(jnp.int32, (CHUNK, CHUNK), 1)
    return jnp.where((c >= r) if rev else (c <= r), 1.0, 0.0).astype(F32)


def _cumsum_rows(x, row, rev=False):
    for k in range(7):
        s = 1 << k
        if rev:
            x = x + jnp.where(row < CHUNK - s, pltpu.roll(x, CHUNK - s, 0), 0.0)
        else:
            x = x + jnp.where(row >= s, pltpu.roll(x, s, 0), 0.0)
    return x


def _log_sigmoid(x):
    return jnp.minimum(x, 0.0) - jnp.log(1.0 + jnp.exp(-jnp.abs(x)))


def _ml_core(gates, hd, first, m, qs, k, v, cmat, nvec):
    sq = (CHUNK, CHUNK)
    lane = lax.broadcasted_iota(jnp.int32, sq, 1)
    row = lax.broadcasted_iota(jnp.int32, sq, 0)
    igc = jnp.sum(jnp.where(lane == hd, gates, 0.0), axis=1, keepdims=True)
    fpc = jnp.sum(jnp.where(lane == hd + ML_HEADS, gates, 0.0), axis=1, keepdims=True)
    valid = jnp.logical_or(jnp.logical_not(first), row[:, :1] >= PAD_ROWS)
    igc = jnp.where(valid, igc, NEG)
    lfc = jnp.where(valid, _log_sigmoid(fpc), 0.0)
    bcb = _cumsum_rows(jnp.broadcast_to(lfc, sq), row)
    igb = jnp.broadcast_to(igc, sq)
    dm = jnp.where(lane <= row, bcb - (bcb - igb).T, NEG)
    bc = bcb[:, :1]
    inter = bc + m
    mt = jnp.maximum(inter, jnp.max(dm, axis=1, keepdims=True))
    wt = jnp.exp(dm - mt)
    wprev = jnp.exp(inter - mt)
    s0 = _dot(qs, k, NT)
    s = s0 * wt
    cb = _bf(cmat)
    qc = _dot(qs, cb)
    qf = qs.astype(F32)
    qn = jnp.sum(qf * nvec, axis=1, keepdims=True)
    num = _dot(_bf(s), v) + wprev * qc
    den = jnp.sum(s, axis=1, keepdims=True) + wprev * qn
    emt = jnp.exp(-mt)
    dd = jnp.maximum(jnp.abs(den), emt)
    blast = bcb[CHUNK - 1:CHUNK, :1]
    g = blast - bc + igc
    m_new = jnp.maximum(blast + m, jnp.max(g, axis=0, keepdims=True))
    decay = jnp.exp(blast + m - m_new)
    e = jnp.exp(g - m_new)
    kf = k.astype(F32)
    wk = e * kf
    return dict(lane=lane, row=row, fpc=fpc, valid=valid, wt=wt, wprev=wprev, s=s, cb=cb, qc=qc, qf=qf, qn=qn,
                num=num, den=den, emt=emt, dd=dd, m_new=m_new, decay=decay, e=e, kf=kf, wk=wk)


def _ml_headnorm(h):
    mu = jnp.mean(h, axis=1, keepdims=True)
    hc = h - mu
    rstd = lax.rsqrt(jnp.mean(hc * hc, axis=1, keepdims=True) + HEAD_NORM_EPS)
    return hc * rstd, rstd


def _ml_chunk_specs(nc, rev, bsz):
    def cc(c):
        return (nc - 1 - c) if rev else c

    return dict(
        hrow=pl.BlockSpec((bsz, CHUNK, ML_DH), lambda hd, c: (0, cc(c), hd)),
        gates=pl.BlockSpec((bsz, CHUNK, CHUNK), lambda hd, c: (0, cc(c), 0)),
        bias=pl.BlockSpec((1, CHUNK), lambda hd, c: (0, 0)),
        hvec=pl.BlockSpec((1, ML_DH), lambda hd, c: (0, hd)),
        cs=pl.BlockSpec((bsz, 1, ML_DH, ML_DH), lambda hd, c: (0, hd * nc + cc(c), 0, 0)),
        ns=pl.BlockSpec((bsz, 1, 1, ML_DH), lambda hd, c: (0, hd * nc + cc(c), 0, 0)),
        ms=pl.BlockSpec((bsz, 1, 1, CHUNK), lambda hd, c: (0, hd * nc + cc(c), 0, 0)),
        dgates=pl.BlockSpec((1, bsz, CHUNK, CHUNK), lambda hd, c: (hd, 0, cc(c), 0)),
    )


def _seq(a, bsz):
    return a.reshape(bsz, a.shape[0] // bsz, a.shape[1])


def _ml_chunk_fwd(qs, k, v, gates, b_gate, pre, z, nw, sk, bsz, nc):
    r = qs.shape[0]
    tp = r // bsz
    sp = _ml_chunk_specs(nc, False, bsz)

    def body(qs_all, k_all, v_all, gt_all, bg_ref, pre_all, z_all, nw_ref, sk_ref,
             h_all, yb_all, cs_all, ns_all, ms_all, c_sall, n_sall, m_sall):
        hd, c = pl.program_id(0), pl.program_id(1)

        @pl.when(c == 0)
        def _():
            c_sall[...] = jnp.zeros_like(c_sall)
            n_sall[...] = jnp.zeros_like(n_sall)
            m_sall[...] = jnp.zeros_like(m_sall)

        for bi in range(bsz):
            one(hd, c, qs_all.at[bi], k_all.at[bi], v_all.at[bi], gt_all.at[bi], bg_ref, pre_all.at[bi], z_all.at[bi],
                nw_ref, sk_ref, h_all.at[bi], yb_all.at[bi], cs_all.at[bi], ns_all.at[bi], ms_all.at[bi],
                c_sall.at[bi], n_sall.at[bi], m_sall.at[bi])

    def one(hd, c, qs_ref, k_ref, v_ref, gt_ref, bg_ref, pre_ref, z_ref, nw_ref, sk_ref,
            h_ref, yb_ref, cs_ref, ns_ref, ms_ref, c_s, n_s, m_s):
        cmat, nvec, m = c_s[...], n_s[...], m_s[...]
        cs_ref[0] = cmat
        ns_ref[0] = nvec
        ms_ref[0] = jnp.broadcast_to(m, (1, CHUNK))
        v_ = v_ref[...]
        co = _ml_core(gt_ref[...] + bg_ref[...], hd, c == 0, m, qs_ref[...], k_ref[...], v_, cmat, nvec)
        h = co["num"] / co["dd"]
        h_ref[...] = h
        hn, _ = _ml_headnorm(h)
        yb_ref[...] = _bf((hn * nw_ref[...] + sk_ref[...] * _silu(pre_ref[...])) * _silu(z_ref[...]))
        c_s[...] = co["decay"] * cmat + _dot(_bf(co["wk"]), v_, TN)
        n_s[...] = co["decay"] * nvec + jnp.sum(co["wk"], axis=0, keepdims=True)
        m_s[...] = co["m_new"]

    nst = ML_HEADS * nc
    h, yb, cs, ns, ms = _pc(
        body, name="ml_chunk_fwd", grid=(ML_HEADS, nc),
        in_specs=[sp["hrow"]] * 3 + [sp["gates"], sp["bias"], sp["hrow"], sp["hrow"], sp["hvec"], sp["hvec"]],
        out_specs=[sp["hrow"], sp["hrow"], sp["cs"], sp["ns"], sp["ms"]],
        out_shape=[jax.ShapeDtypeStruct((bsz, tp, ML_WIDTH), F32), jax.ShapeDtypeStruct((bsz, tp, ML_WIDTH), BF16),
                   jax.ShapeDtypeStruct((bsz, nst, ML_DH, ML_DH), F32),
                   jax.ShapeDtypeStruct((bsz, nst, 1, ML_DH), F32), jax.ShapeDtypeStruct((bsz, nst, 1, CHUNK), F32)],
        scratch=[pltpu.VMEM((bsz, ML_DH, ML_DH), F32), pltpu.VMEM((bsz, 1, ML_DH), F32),
                 pltpu.VMEM((bsz, 1, 1), F32)],
        vmem=12 << 20)(*[_seq(a, bsz) for a in (qs, k, v, gates)], b_gate, _seq(pre, bsz), _seq(z, bsz), nw, sk)
    return h.reshape(r, ML_WIDTH), yb.reshape(r, ML_WIDTH), cs, ns, ms


def _ml_chunk_bwd(dyb, qs, k, v, gates, b_gate, pre, z, nw, sk, h, cs, ns, ms, bsz, nc, dep=None):
    r = qs.shape[0]
    tp = r // bsz
    sp = _ml_chunk_specs(nc, True, bsz)

    def body(dy_all, qs_all, k_all, v_all, gt_all, bg_ref, pre_all, z_all, nw_ref, sk_ref, h_all, cs_all, ns_all,
             ms_all, dq_all, dk_all, dv_all, dz_all, dxc_all, dgt_all, dnw_ref, dsk_ref, dc_sall, dn_sall):
        hd, c = pl.program_id(0), pl.program_id(1)

        @pl.when(c == 0)
        def _():
            for ref in (dnw_ref, dsk_ref, dc_sall, dn_sall):
                ref[...] = jnp.zeros_like(ref)

        for bi in range(bsz):
            one(hd, c, dy_all.at[bi], qs_all.at[bi], k_all.at[bi], v_all.at[bi], gt_all.at[bi], bg_ref,
                pre_all.at[bi], z_all.at[bi], nw_ref, sk_ref, h_all.at[bi], cs_all.at[bi], ns_all.at[bi],
                ms_all.at[bi], dq_all.at[bi], dk_all.at[bi], dv_all.at[bi], dz_all.at[bi], dxc_all.at[bi],
                dgt_all.at[0, bi], dnw_ref, dsk_ref, dc_sall.at[bi], dn_sall.at[bi])

    def one(hd, c, dy_ref, qs_ref, k_ref, v_ref, gt_ref, bg_ref, pre_ref, z_ref, nw_ref, sk_ref, h_ref, cs_ref, ns_ref,
            ms_ref, dq_ref, dk_ref, dv_ref, dz_ref, dxc_ref, dgt_ref, dnw_ref, dsk_ref, dc_s, dn_s):

        qs, k, v = qs_ref[...], k_ref[...], v_ref[...]
        cmat, nvec, m = cs_ref[0], ns_ref[0], ms_ref[0][:, :1]
        co = _ml_core(gt_ref[...] + bg_ref[...], hd, c == nc - 1, m, qs, k, v, cmat, nvec)
        lane, row = co["lane"], co["row"]
        wt, wprev, s, cb, qf = co["wt"], co["wprev"], co["s"], co["cb"], co["qf"]
        h = h_ref[...]
        hn, rstd = _ml_headnorm(h)
        xc = _silu(pre_ref[...])
        zv = z_ref[...]
        nw, sk = nw_ref[...], sk_ref[...]
        dy = dy_ref[...]
        dz_ref[...] = _bf(dy * (hn * nw + sk * xc) * _dsilu(zv))
        do = dy * _silu(zv)
        dsk_ref[...] += jnp.sum(do * xc, axis=0, keepdims=True)
        dnw_ref[...] += jnp.sum(do * hn, axis=0, keepdims=True)
        dxc_ref[...] = do * sk
        dhn = do * nw
        dh = rstd * (dhn - jnp.mean(dhn, axis=1, keepdims=True) - hn * jnp.mean(dhn * hn, axis=1, keepdims=True))
        rinv = 1.0 / co["dd"]
        dnum = dh * rinv
        ddd = -jnp.sum(dh * h, axis=1, keepdims=True) * rinv
        den = co["den"]
        dden = jnp.where(jnp.abs(den) >= co["emt"], ddd * jnp.sign(den), 0.0)
        dnb = _bf(dnum)
        ds = _dot(dnb, v, NT) + dden
        dv = _dot(_bf(s), dnb, TN)
        dnw_ = _bf(dnum * wprev)
        dwn = dden * wprev
        dqs = _dot(dnw_, cb, NT) + dwn * nvec
        dc_out = _dot(qs, dnw_, TN)
        dn_out = jnp.sum(dwn * qf, axis=0, keepdims=True)
        dwprev = jnp.sum(dnum * co["qc"], axis=1, keepdims=True) + dden * co["qn"]
        ds0 = _bf(ds * wt)
        ddm = ds * s
        dqs = dqs + _dot(ds0, k)
        dk = _dot(ds0, qs, TN)
        colc = jnp.sum(ddm.T, axis=1, keepdims=True)
        dbc = dwprev * wprev + jnp.sum(ddm, axis=1, keepdims=True) - colc
        dig = colc
        dcn, dnn = dc_s[...], dn_s[...]
        dcb = _bf(dcn)
        decay, e, kf, wk = co["decay"], co["e"], co["kf"], co["wk"]
        ddecay = (jnp.sum(jnp.sum(dcn * cmat, axis=1, keepdims=True), axis=0, keepdims=True)
                  + jnp.sum(dnn * nvec, axis=1, keepdims=True))
        dwk = _dot(v, dcb, NT) + dnn
        dv = dv + _dot(_bf(wk), dcb)
        dk = dk + e * dwk
        dg = jnp.sum(dwk * kf, axis=1, keepdims=True) * e
        dblast = ddecay * decay + jnp.sum(dg, axis=0, keepdims=True)
        dbc = dbc - dg + jnp.where(row[:, :1] == CHUNK - 1, dblast, 0.0)
        dig = dig + dg
        dc_s[...] = decay * dcn + dc_out
        dn_s[...] = decay * dnn + dn_out
        dlf = _cumsum_rows(jnp.broadcast_to(dbc, (CHUNK, CHUNK)), row, rev=True)[:, :1]
        dfp = dlf * (1.0 - jax.nn.sigmoid(co["fpc"]))
        dig = jnp.where(co["valid"], dig, 0.0)
        dfp = jnp.where(co["valid"], dfp, 0.0)
        dgt_ref[...] = jnp.where(lane == hd, dig, 0.0) + jnp.where(lane == hd + ML_HEADS, dfp, 0.0)
        dq_ref[...] = _bf(dqs * ML_SCALE)
        dk_ref[...] = _bf(dk)
        dv_ref[...] = _bf(dv)

    ob = jax.ShapeDtypeStruct((bsz, tp, ML_WIDTH), BF16)
    dq, dk, dv, dz, dxc, dgt, dnw, dsk = _pc(
        body, name="ml_chunk_bwd", grid=(ML_HEADS, nc),
        in_specs=[sp["hrow"]] * 4 + [sp["gates"], sp["bias"], sp["hrow"], sp["hrow"], sp["hvec"], sp["hvec"],
                                     sp["hrow"], sp["cs"], sp["ns"], sp["ms"]],
        out_specs=[sp["hrow"]] * 5 + [sp["dgates"], sp["hvec"], sp["hvec"]],
        out_shape=[ob, ob, ob, ob, jax.ShapeDtypeStruct((bsz, tp, ML_WIDTH), F32),
                   jax.ShapeDtypeStruct((ML_HEADS, bsz, tp, CHUNK), F32),
                   jax.ShapeDtypeStruct((1, ML_WIDTH), F32), jax.ShapeDtypeStruct((1, ML_WIDTH), F32)],
        scratch=[pltpu.VMEM((bsz, ML_DH, ML_DH), F32), pltpu.VMEM((bsz, 1, ML_DH), F32)], vmem=16 << 20, dep=dep,
    )(*[_seq(a, bsz) for a in (dyb, qs, k, v, gates)], b_gate, _seq(pre, bsz), _seq(z, bsz), nw, sk, _seq(h, bsz),
      cs, ns, ms)
    return (dq.reshape(r, ML_WIDTH), dk.reshape(r, ML_WIDTH), dv.reshape(r, ML_WIDTH), dz.reshape(r, ML_WIDTH),
            dxc.reshape(r, ML_WIDTH), dgt.reshape(ML_HEADS, r, CHUNK), dnw, dsk)


def _ml_pre_bwd(dq, dk, dv, dgates, dxc_skip, pre, x, q, k, v, wq, wk, wv, wgq, wgk, wgv, bsz, nc):
    r = x.shape[0]
    tr = _pick(r, (256, 128))
    nt = r // tr
    hrow = pl.BlockSpec((tr, ML_DH), lambda h, i: (i, h))
    wexp = pl.BlockSpec((1, ML_DH, ML_DH), lambda h, i: (h, 0, 0))
    wcmp = pl.BlockSpec((1, ML_DH, CHUNK), lambda h, i: (h, 0, 0))
    wg = pl.BlockSpec((ML_DH, CHUNK), lambda h, i: (h, 0))
    dgs = pl.BlockSpec((ML_HEADS, tr, CHUNK), lambda h, i: (0, i, 0))
    bgs = pl.BlockSpec((1, 1, CHUNK), lambda h, i: (h, 0, 0))

    def body(dq_ref, dk_ref, dv_ref, dg_ref, dxs_ref, pre_ref, x_ref, q_ref, k_ref, v_ref, wq_ref, wk_ref, wv_ref,
             gq_ref, gk_ref, gv_ref, dpre_ref, dxv_ref, cq_ref, ck_ref, cv_ref, dgq_ref, dgk_ref, dgv_ref, dbg_ref,
             dwq_ref, dwk_ref, dwv_ref):
        i = pl.program_id(1)

        @pl.when(i == 0)
        def _():
            for ref in (dwq_ref, dwk_ref, dwv_ref, dgq_ref, dgk_ref, dgv_ref, dbg_ref):
                ref[...] = jnp.zeros_like(ref)

        dgt = dg_ref[0]
        for j in range(1, ML_HEADS):
            dgt = dgt + dg_ref[j]
        dbg_ref[0] += jnp.sum(dgt, axis=0, keepdims=True)
        dgb = _bf(dgt)
        dqt = _bf(dq_ref[...].astype(F32) + _dot(dgb, gq_ref[...], NT))
        dkt = _bf(dk_ref[...].astype(F32) + _dot(dgb, gk_ref[...], NT))
        dvt = _bf(dv_ref[...].astype(F32) + _dot(dgb, gv_ref[...], NT))
        dgq_ref[...] += _dot(q_ref[...], dgb, TN)
        dgk_ref[...] += _dot(k_ref[...], dgb, TN)
        dgv_ref[...] += _dot(v_ref[...], dgb, TN)
        prev = pre_ref[...]
        xcb = _bf(_silu(prev))
        xb = _bf(x_ref[...])
        dwq_ref[...] += _dot(xcb, dqt, TN)
        dwk_ref[...] += _dot(xcb, dkt, TN)
        dwv_ref[...] += _dot(xb, dvt, TN)
        dxc = _dot(dqt, wq_ref[0], NT) + _dot(dkt, wk_ref[0], NT) + dxs_ref[...]
        dpre_ref[...] = dxc * _dsilu(prev)
        dxv_ref[...] = _dot(dvt, wv_ref[0], NT)

        @pl.when(i == nt - 1)
        def _():
            rr = lax.broadcasted_iota(jnp.int32, (ML_DH, ML_DH), 0)
            cc = lax.broadcasted_iota(jnp.int32, (ML_DH, ML_DH), 1)
            diag = rr // QKV_BLOCK == cc // QKV_BLOCK
            fc = lax.broadcasted_iota(jnp.int32, (ML_DH, CHUNK), 0)
            fo = lax.broadcasted_iota(jnp.int32, (ML_DH, CHUNK), 1)
            fold = jnp.where(fc % QKV_BLOCK == fo, 1.0, 0.0).astype(F32)
            for src, dst in ((dwq_ref, cq_ref), (dwk_ref, ck_ref), (dwv_ref, cv_ref)):
                dst[0] = jnp.dot(jnp.where(diag, src[...], 0.0), fold, precision=HI, preferred_element_type=F32)

    f = jax.ShapeDtypeStruct((r, ML_WIDTH), F32)
    wc = jax.ShapeDtypeStruct((ML_HEADS, ML_DH, CHUNK), F32)
    wgs = jax.ShapeDtypeStruct((ML_WIDTH, CHUNK), F32)
    return _pc(body, name="ml_pre_bwd", grid=(ML_HEADS, nt),
               in_specs=[hrow, hrow, hrow, dgs, hrow, hrow, hrow, hrow, hrow, hrow, wexp, wexp, wexp, wg, wg, wg],
               out_specs=[hrow, hrow, wcmp, wcmp, wcmp, wg, wg, wg, bgs],
               out_shape=[f, f, wc, wc, wc, wgs, wgs, wgs, jax.ShapeDtypeStruct((ML_HEADS, 1, CHUNK), F32)],
               scratch=[pltpu.VMEM((ML_DH, ML_DH), F32)] * 3,
               vmem=8 << 20)(dq, dk, dv, dgates, dxc_skip, pre, x, q, k, v, wq, wk, wv, wgq, wgk, wgv)


def _pad_lanes(w):
    return jnp.pad(w, ((0, 0), (0, CHUNK - w.shape[1])))


def _ml_weights(conv_w, conv_b, wq, wk, wv, w_gate, b_gate, norm_w, skip):
    return dict(
        conv_w=conv_w, conv_b=conv_b,
        wq=_bf(_headwise_expand(wq)), wk=_bf(_headwise_expand(wk)), wv=_bf(_headwise_expand(wv)),
        wgq=_bf(_pad_lanes(w_gate[:ML_WIDTH])), wgk=_bf(_pad_lanes(w_gate[ML_WIDTH:2 * ML_WIDTH])),
        wgv=_bf(_pad_lanes(w_gate[2 * ML_WIDTH:])), b_gate=_pad_lanes(b_gate.reshape(1, -1)),
        norm=norm_w.reshape(1, ML_WIDTH), skip=skip.reshape(1, ML_WIDTH))


def _ml_layer_fwd(x, z, w, bsz, nc):
    pre = _conv_fwd(x, w["conv_w"], w["conv_b"], bsz, nc, name="ml_conv")
    q, qs, k, v, gates = _ml_pre(pre, x, w["wq"], w["wk"], w["wv"], w["wgq"], w["wgk"], w["wgv"], bsz, nc)
    h, yb, cs, ns, ms = _ml_chunk_fwd(qs, k, v, gates, w["b_gate"], pre, z, w["norm"], w["skip"], bsz, nc)
    return yb, dict(pre=pre, q=q, qs=qs, k=k, v=v, gates=gates, h=h, cs=cs, ns=ns, ms=ms)


def _ml_layer_bwd(dyb, x, z, sv, w, bsz, nc, dep=None):
    dq, dk, dv, dz, dxc, dgates, dnw, dsk = _ml_chunk_bwd(
        dyb, sv["qs"], sv["k"], sv["v"], sv["gates"], w["b_gate"], sv["pre"], z, w["norm"], w["skip"], sv["h"],
        sv["cs"], sv["ns"], sv["ms"], bsz, nc, dep=dep)
    dpre, dxv, dwq, dwk, dwv, dgq, dgk, dgv, dbg = _ml_pre_bwd(
        dq, dk, dv, dgates, dxc, sv["pre"], x, sv["q"], sv["k"], sv["v"], w["wq"], w["wk"], w["wv"], w["wgq"],
        w["wgk"], w["wgv"], bsz, nc)
    dx, dcw, dcb = _conv_bwd(dpre, x, w["conv_w"], bsz, nc, name="ml_conv_bwd", add=dxv)
    ng = 2 * ML_HEADS
    grads = dict(
        ml_conv_w=dcw, ml_conv_b=dcb, ml_wq=_headwise_extract(dwq), ml_wk=_headwise_extract(dwk),
        ml_wv=_headwise_extract(dwv), ml_w_gate=jnp.concatenate([dgq[:, :ng], dgk[:, :ng], dgv[:, :ng]], axis=0),
        ml_b_gate=dbg[0][:, :ng], ml_norm=dnw, ml_skip=dsk)
    return dx, dz, grads


HI = lax.Precision.HIGHEST


def _softplus(x):
    return jnp.maximum(x, 0.0) + jnp.log(1.0 + jnp.exp(-jnp.abs(x)))


def _lane_cumsum(x, lane, rev=False):
    del lane
    return _dot_terms(x, _tri(not rev), NN, exact_rhs=True, terms=3)


def _dot_terms(lhs, rhs, dims, *, exact_rhs, terms):
    x = lhs if exact_rhs else rhs
    sel = _bf(rhs if exact_rhs else lhs)
    acc = None
    for _ in range(terms):
        piece = _bf(x)
        part = _dot(piece, sel, dims) if exact_rhs else _dot(sel, piece, dims)
        acc = part if acc is None else acc + part
        x = x - piece.astype(F32)
    return acc


def _head_sum_matrix():
    r = lax.broadcasted_iota(jnp.int32, (SSD_HPG, SSD_GW), 0)
    l = lax.broadcasted_iota(jnp.int32, (SSD_HPG, SSD_GW), 1)
    return jnp.where(l // SSD_P == r, 1.0, 0.0).astype(F32)


def _ssd_core(xs, bm, cm, dt_raw, dt_bias, a_log, first):
    sq = (CHUNK, CHUNK)
    lane8 = lax.broadcasted_iota(jnp.int32, (SSD_HPG, CHUNK), 1)
    lane = lax.broadcasted_iota(jnp.int32, sq, 1)
    row = lax.broadcasted_iota(jnp.int32, sq, 0)
    low = lane < SSD_P
    valid = jnp.logical_or(jnp.logical_not(first), lane8 >= PAD_ROWS)
    pre = dt_raw + dt_bias
    dt = jnp.where(valid, _softplus(pre), 0.0)
    a = -jnp.exp(a_log)
    cum = _lane_cumsum(dt * a, lane8)
    cb = _dot(_bf(cm), _bf(bm), NT)
    heads = []
    for r in range(SSD_HPG):
        rowb = jnp.broadcast_to(cum[r:r + 1, :], sq)
        colb = rowb.T
        seg = jnp.exp(jnp.where(lane <= row, colb - rowb, NEG))
        dtrow = jnp.broadcast_to(dt[r:r + 1, :], sq)
        lastb = colb[CHUNK - 1:CHUNK, :]
        heads.append(dict(seg=seg, dtrow=dtrow, w=cb * seg * dtrow, ecol=jnp.exp(colb),
                          dec=jnp.exp(lastb - colb) * dtrow.T, elast=jnp.exp(lastb)))

    def pairs(key):
        return jnp.concatenate([jnp.where(low[:heads[0][key].shape[0]], heads[2 * j][key], heads[2 * j + 1][key])
                                for j in range(SSD_HPG // 2)], axis=1)

    return dict(lane8=lane8, low=low, valid=valid, pre=pre, dt=dt, a=a, cum=cum, cb=cb, heads=heads,
                expc=pairs("ecol"), dec=pairs("dec"), elast=pairs("elast"))


def _ssd_specs(nc, rev, bsz):
    def cc(c):
        return (nc - 1 - c) if rev else c

    return dict(
        wide=pl.BlockSpec((bsz, CHUNK, SSD_GW), lambda g, c: (0, cc(c), g)),
        narrow=pl.BlockSpec((bsz, CHUNK, SSD_N), lambda g, c: (0, cc(c), g)),
        dtT=pl.BlockSpec((bsz, SSD_HPG, CHUNK), lambda g, c: (0, g, cc(c))),
        hcol=pl.BlockSpec((SSD_HPG, 1), lambda g, c: (g, 0)),
        hacc=pl.BlockSpec((SSD_HPG, CHUNK), lambda g, c: (g, 0)),
        gvec=pl.BlockSpec((1, SSD_GW), lambda g, c: (0, g)),
        state=pl.BlockSpec((bsz, 1, SSD_N, SSD_GW), lambda g, c: (0, g * nc + cc(c), 0, 0)),
    )


def _ssd_chunk_fwd(xs_pre, bm_pre, cm_pre, dt_raw, dt_bias, a_log, d_exp, z, gnorm, bsz, nc):
    tp = xs_pre.shape[1]
    sp = _ssd_specs(nc, False, bsz)

    def body(xs_all, bm_all, cm_all, dt_all, db_ref, al_ref, d_ref, z_all, gn_ref, y_all, yn_all, st_all, st_sall):
        c = pl.program_id(1)

        @pl.when(c == 0)
        def _():
            st_sall[...] = jnp.zeros_like(st_sall)

        for bi in range(bsz):
            one(c, xs_all.at[bi], bm_all.at[bi], cm_all.at[bi], dt_all.at[bi], db_ref, al_ref, d_ref, z_all.at[bi],
                gn_ref, y_all.at[bi], yn_all.at[bi], st_all.at[bi], st_sall.at[bi])

    def one(c, xs_ref, bm_ref, cm_ref, dt_ref, db_ref, al_ref, d_ref, z_ref, gn_ref, y_ref, yn_ref, st_ref, st_s):
        state = st_s[...]
        st_ref[0] = state
        xs, bm, cm = _silu(xs_ref[...]), _silu(bm_ref[...]), _silu(cm_ref[...])
        co = _ssd_core(xs, bm, cm, dt_ref[...], db_ref[...], al_ref[...], c == 0)
        low, hd = co["low"], co["heads"]
        ys = []
        for j in range(SSD_HPG // 2):
            xp = xs[:, j * CHUNK:(j + 1) * CHUNK]
            lhs = jnp.concatenate([hd[2 * j]["w"], hd[2 * j + 1]["w"]], axis=1)
            rhs = jnp.concatenate([jnp.where(low, xp, 0.0), jnp.where(low, 0.0, xp)], axis=0)
            ys.append(_dot(_bf(lhs), _bf(rhs)))
        cmb = _bf(cm)
        y = jnp.concatenate(ys, axis=1) + co["expc"] * _dot(cmb, _bf(state)) + d_ref[...] * xs
        y_ref[...] = y
        yg = y * _silu(z_ref[...])
        rstd = lax.rsqrt(jnp.mean(yg * yg, axis=1, keepdims=True) + NORM_EPS)
        yn_ref[...] = _bf(yg * rstd * gn_ref[...])
        st_s[...] = co["elast"] * state + _dot(_bf(bm), _bf(xs * co["dec"]), TN)

    return _pc(body, name="ssd_chunk_fwd", grid=(SSD_GROUPS, nc),
               in_specs=[sp["wide"], sp["narrow"], sp["narrow"], sp["dtT"], sp["hcol"], sp["hcol"], sp["gvec"],
                         sp["wide"], sp["gvec"]],
               out_specs=[sp["wide"], sp["wide"], sp["state"]],
               out_shape=[jax.ShapeDtypeStruct((bsz, tp, SSD_INNER), F32),
                          jax.ShapeDtypeStruct((bsz, tp, SSD_INNER), BF16),
                          jax.ShapeDtypeStruct((bsz, SSD_GROUPS * nc, SSD_N, SSD_GW), F32)],
               scratch=[pltpu.VMEM((bsz, SSD_N, SSD_GW), F32)], vmem=12 << 20,
               )(xs_pre, bm_pre, cm_pre, dt_raw, dt_bias, a_log, d_exp, z, gnorm)


def _ssd_chunk_bwd(dyn, xs_pre, bm_pre, cm_pre, dt_raw, dt_bias, a_log, d_exp, z, gnorm, y, states, bsz, nc):
    tp = xs_pre.shape[1]
    sp = _ssd_specs(nc, True, bsz)

    def body(dyn_all, xs_all, bm_all, cm_all, dt_all, db_ref, al_ref, d_ref, z_all, gn_ref, y_all, st_all,
             dxs_all, dbm_all, dcm_all, dz_all, ddt_all, dgn_ref, dd_ref, dbias_ref, dal_ref, ds_sall):
        c = pl.program_id(1)

        @pl.when(c == 0)
        def _():
            for ref in (dgn_ref, dd_ref, dbias_ref, dal_ref, ds_sall):
                ref[...] = jnp.zeros_like(ref)

        for bi in range(bsz):
            one(c, dyn_all.at[bi], xs_all.at[bi], bm_all.at[bi], cm_all.at[bi], dt_all.at[bi], db_ref, al_ref, d_ref,
                z_all.at[bi], gn_ref, y_all.at[bi], st_all.at[bi], dxs_all.at[bi], dbm_all.at[bi], dcm_all.at[bi],
                dz_all.at[bi], ddt_all.at[bi], dgn_ref, dd_ref, dbias_ref, dal_ref, ds_sall.at[bi])

    def one(c, dyn_ref, xs_ref, bm_ref, cm_ref, dt_ref, db_ref, al_ref, d_ref, z_ref, gn_ref, y_ref, st_ref,
            dxs_ref, dbm_ref, dcm_ref, dz_ref, ddt_ref, dgn_ref, dd_ref, dbias_ref, dal_ref, ds_s):
        xs_p, bm_p, cm_p = xs_ref[...], bm_ref[...], cm_ref[...]
        xs, bm, cm = _silu(xs_p), _silu(bm_p), _silu(cm_p)
        state = st_ref[0]
        co = _ssd_core(xs, bm, cm, dt_ref[...], db_ref[...], al_ref[...], c == nc - 1)
        low, hd, lane8, cb = co["low"], co["heads"], co["lane8"], co["cb"]
        dt, a, cum = co["dt"], co["a"], co["cum"]
        sub8 = lax.broadcasted_iota(jnp.int32, (SSD_HPG, CHUNK), 0)
        eh = _head_sum_matrix()

        def head_rows(full):
            return _dot_terms(eh, full, NT, exact_rhs=False, terms=2)

        def head_col(vec):
            return jnp.sum(eh * vec, axis=1, keepdims=True)

        yv, zv, gn = y_ref[...], z_ref[...], gn_ref[...]
        sz = _silu(zv)
        yg = yv * sz
        rstd = lax.rsqrt(jnp.mean(yg * yg, axis=1, keepdims=True) + NORM_EPS)
        yh = yg * rstd
        dyn = dyn_ref[...]
        dgn_ref[...] += jnp.sum(dyn * yh, axis=0, keepdims=True)
        dyh = dyn * gn
        dyg = rstd * (dyh - yh * jnp.mean(dyh * yh, axis=1, keepdims=True))
        dz_ref[...] = _bf(dyg * yv * _dsilu(zv))
        dy = dyg * sz
        dxs = dy * d_ref[...]
        dd_ref[...] += head_col(jnp.sum(dy * xs, axis=0, keepdims=True))
        cmb, bmb, stb = _bf(cm), _bf(bm), _bf(state)
        ysv = _dot(cmb, stb)
        expc = co["expc"]
        dys = _bf(dy * expc)
        dcum = head_rows(dy * ysv * expc)
        dcm = _dot(dys, stb, NT)
        dstate_out = _dot(cmb, dys, TN)
        dcb = jnp.zeros((CHUNK, CHUNK), F32)
        ddt = jnp.zeros((SSD_HPG, CHUNK), F32)
        dxs_pairs = []
        for j in range(SSD_HPG // 2):
            sl = slice(j * CHUNK, (j + 1) * CHUNK)
            dyp, xp = dy[:, sl], _bf(xs[:, sl])
            lhs = _bf(jnp.concatenate([hd[2 * j]["w"], hd[2 * j + 1]["w"]], axis=1))
            both = _dot(lhs, _bf(dyp), TN)
            dxs_pairs.append(jnp.where(low, both[:CHUNK], both[CHUNK:]))
            for q, msk in ((2 * j, low), (2 * j + 1, jnp.logical_not(low))):
                h = hd[q]
                dw = _dot(_bf(jnp.where(msk, dyp, 0.0)), xp, NT)
                dcb = dcb + dw * h["seg"] * h["dtrow"]
                e_ = dw * h["w"]
                dcum_r = jnp.sum(e_.T, axis=0, keepdims=True) - jnp.sum(e_, axis=0, keepdims=True)
                ddt_r = jnp.sum(dw * cb * h["seg"], axis=0, keepdims=True)
                dcum = dcum + jnp.where(sub8 == q, dcum_r, 0.0)
                ddt = ddt + jnp.where(sub8 == q, ddt_r, 0.0)
        dxs = dxs + jnp.concatenate(dxs_pairs, axis=1)
        dcbb = _bf(dcb)
        dcm = dcm + _dot(dcbb, bmb)
        dbm = _dot(dcbb, cmb, TN)
        dsn = ds_s[...]
        dsb = _bf(dsn)
        dec = co["dec"]
        dbm = dbm + _dot(_bf(xs * dec), dsb, NT)
        dxd = _dot(bmb, dsb)
        dxs = dxs + dxd * dec
        ddec = head_rows(dxd * xs)
        last = cum[:, CHUNK - 1:CHUNK]
        erow = jnp.exp(last - cum)
        ddt = ddt + ddec * erow
        dla = ddec * erow * dt
        dlast = (jnp.sum(dla, axis=1, keepdims=True)
                 + head_col(jnp.sum(dsn * state, axis=0, keepdims=True)) * jnp.exp(last))
        dcum = dcum - dla + jnp.where(lane8 == CHUNK - 1, dlast, 0.0)
        ds_s[...] = co["elast"] * dsn + dstate_out
        dda = _lane_cumsum(dcum, lane8, rev=True)
        ddt = jnp.where(co["valid"], ddt + dda * a, 0.0)
        ddt_raw = ddt * jax.nn.sigmoid(co["pre"])
        ddt_ref[...] = ddt_raw
        dbias_ref[...] += jnp.sum(ddt_raw, axis=1, keepdims=True)
        dal_ref[...] += jnp.sum(dda * dt, axis=1, keepdims=True) * a
        dxs_ref[...] = dxs * _dsilu(xs_p)
        dbm_ref[...] = dbm * _dsilu(bm_p)
        dcm_ref[...] = dcm * _dsilu(cm_p)

    st = jax.ShapeDtypeStruct
    hacc = st((SSD_HEADS, CHUNK), F32)
    return _pc(body, name="ssd_chunk_bwd", grid=(SSD_GROUPS, nc),
               in_specs=[sp["wide"], sp["wide"], sp["narrow"], sp["narrow"], sp["dtT"], sp["hcol"], sp["hcol"],
                         sp["gvec"], sp["wide"], sp["gvec"], sp["wide"], sp["state"]],
               out_specs=[sp["wide"], sp["narrow"], sp["narrow"], sp["wide"], sp["dtT"], sp["gvec"], sp["hacc"],
                          sp["hacc"], sp["hacc"]],
               out_shape=[st((bsz, tp, SSD_INNER), F32), st((bsz, tp, SSD_BC), F32), st((bsz, tp, SSD_BC), F32),
                          st((bsz, tp, SSD_INNER), BF16), st((bsz, SSD_HEADS, tp), F32), st((1, SSD_INNER), F32),
                          hacc, hacc, hacc],
               scratch=[pltpu.VMEM((bsz, SSD_N, SSD_GW), F32)], vmem=20 << 20,
               )(dyn, xs_pre, bm_pre, cm_pre, dt_raw, dt_bias, a_log, d_exp, z, gnorm, y, states)


SSD_BC = SSD_GROUPS * SSD_N


def _ssd_weights(conv_w, conv_b, dt_bias, a_log, d, gnorm):
    cuts = (0, SSD_INNER, SSD_INNER + SSD_BC, SSD_INNER + 2 * SSD_BC)
    return dict(
        conv_w=[conv_w[:, cuts[i]:cuts[i + 1]] for i in range(3)],
        conv_b=[conv_b[cuts[i]:cuts[i + 1]] for i in range(3)],
        dt_bias=dt_bias.reshape(SSD_HEADS, 1), a_log=a_log.reshape(SSD_HEADS, 1),
        d_exp=jnp.repeat(d.reshape(SSD_HEADS), SSD_P).reshape(1, SSD_INNER), gnorm=gnorm.reshape(1, SSD_INNER))


def _ssd_layer_fwd(z, xs_in, bm_in, cm_in, dt_rows, w, bsz, nc):
    pres = [_conv_fwd(a, w["conv_w"][i], w["conv_b"][i], bsz, nc, name=f"ssd_conv{i}")
            for i, a in enumerate((xs_in, bm_in, cm_in))]
    def seq(a):
        return a.reshape(bsz, nc * CHUNK, a.shape[-1])

    dt_t = jnp.swapaxes(seq(dt_rows)[:, :, :SSD_HEADS], 1, 2)
    y, yn, states = _ssd_chunk_fwd(seq(pres[0]), seq(pres[1]), seq(pres[2]), dt_t, w["dt_bias"], w["a_log"],
                                   w["d_exp"], seq(z), w["gnorm"], bsz, nc)
    return yn.reshape(-1, SSD_INNER), dict(pres=pres, dt_t=dt_t, y=y, states=states)


def _ssd_layer_bwd(dyn, z, xs_in, bm_in, cm_in, sv, w, bsz, nc):
    pres = sv["pres"]

    def seq(a):
        return a.reshape(bsz, nc * CHUNK, a.shape[-1])

    def rows(a):
        return a.reshape(-1, a.shape[-1])

    dxs_p, dbm_p, dcm_p, dz, ddt_t, dgn, dd, dbias, dal = _ssd_chunk_bwd(
        seq(dyn), seq(pres[0]), seq(pres[1]), seq(pres[2]), sv["dt_t"], w["dt_bias"], w["a_log"], w["d_exp"], seq(z),
        w["gnorm"], sv["y"], sv["states"], bsz, nc)
    dz = rows(dz)
    outs = [_conv_bwd(rows(dp), a, w["conv_w"][i], bsz, nc, name=f"ssd_conv_bwd{i}")
            for i, (dp, a) in enumerate(((dxs_p, xs_in), (dbm_p, bm_in), (dcm_p, cm_in)))]
    ddt = _bf(_pad_lanes(rows(jnp.swapaxes(ddt_t, 1, 2))))
    grads = dict(
        ssd_conv_w=jnp.concatenate([o[1] for o in outs], axis=1),
        ssd_conv_b=jnp.concatenate([o[2] for o in outs], axis=1),
        ssd_dt_bias=dbias[:, 0], ssd_a_log=dal[:, 0], ssd_d=dd[:, 0], ssd_gnorm=dgn)
    return dz, outs[0][0], outs[1][0], outs[2][0], ddt, grads


WNAMES = ("meta_tokens", "ab_norm", "ab_w_in", "s5_lambda_re", "s5_lambda_im", "s5_log_dt", "s5_b_re", "s5_b_im",
          "s5_c_re", "s5_c_im", "s5_d", "s5_glu_w", "s5_glu_b", "ml_conv_w", "ml_conv_b", "ml_wq", "ml_wk", "ml_wv",
          "ml_w_gate", "ml_b_gate", "ml_norm", "ml_skip", "ab_w_out", "ssd_norm", "ssd_w_in", "ssd_conv_w",
          "ssd_conv_b", "ssd_dt_bias", "ssd_a_log", "ssd_d", "ssd_gnorm", "ssd_w_out", "final_norm")
SHARD_AXIS = dict(meta_tokens=1, ab_w_in=2, s5_glu_w=1, ml_conv_w=2, ml_wq=1, ml_wk=1, ml_wv=1, ml_w_gate=1,
                  ab_w_out=1, ssd_norm=1, ssd_w_in=2, ssd_conv_w=2, ssd_conv_b=1, ssd_gnorm=1, ssd_w_out=1)
BIG = ("ab_w_in", "s5_glu_w", "ab_w_out", "ssd_w_in", "ssd_w_out")
SMALL = tuple(n for n in WNAMES if n in SHARD_AXIS and n not in BIG)
REPL = tuple(n for n in WNAMES if n not in SHARD_AXIS)
PACK_ALIGN = 8 * 128


def _pack(arrs):
    lead = arrs[0][1]
    parts = []
    for a, nlead in arrs:
        f = a.reshape(a.shape[:nlead] + (-1,))
        parts.append(jnp.pad(f, [(0, 0)] * nlead + [(0, (-f.shape[-1]) % PACK_ALIGN)]))
    flat = jnp.concatenate(parts, axis=lead)
    return flat.reshape(flat.shape[:lead] + (-1, 128))


def _unpack(p, shapes):
    out, off = [], 0
    lead = p.shape[:-2]
    flat = p.reshape(lead + (-1,))
    for s in shapes:
        n = math.prod(s)
        out.append(flat[..., off:off + n].reshape(lead + tuple(s)))
        off += -(-n // PACK_ALIGN) * PACK_ALIGN
    return out


def _assemble(g, axis):
    m = jnp.moveaxis(g, 0, axis)
    return m.reshape(m.shape[:axis] + (m.shape[axis] * m.shape[axis + 1],) + m.shape[axis + 2:])


def _split(full, axis):
    s = full.shape
    m = full.reshape(s[:axis] + (N_DEV, s[axis] // N_DEV) + s[axis + 1:])
    return jnp.moveaxis(m, axis, 0)


def kernel(x, *rest):
    nw = len(WNAMES)
    w = dict(zip(WNAMES, rest[:nw]))
    loss_target = rest[nw]
    mom = dict(zip(WNAMES, rest[nw + 1:2 * nw + 1]))
    var = dict(zip(WNAMES, rest[2 * nw + 1:3 * nw + 1]))
    bsz = x.shape[0]
    nc = 1 + SEQ // CHUNK
    tp = nc * CHUNK

    local = {n: _bf(w[n][0]) for n in BIG}
    small_local = _pack([(w[n], 0) for n in SMALL])
    gs = _exchange_start([small_local], ["ag"], name="gather_s")
    ga = _exchange_start([local["ab_w_in"]], ["ag"], name="gather_a", dep=gs["token"], peers=SAME_CORE[1:])
    got_s = _exchange_wait(gs, ga["token"])

    def assemble_big(n, got):
        return _assemble(got[:, None], SHARD_AXIS[n])[0]

    full = {}
    for n, g in zip(SMALL, _unpack(got_s[0], [w[n].shape for n in SMALL])):
        full[n] = _assemble(g, SHARD_AXIS[n])[0] if n != "meta_tokens" else _assemble(g, SHARD_AXIS[n])
    for n in REPL:
        full[n] = w[n][0] if n != "final_norm" else w[n]
    glu_b = full["s5_glu_b"].reshape(1, S5_WIDTH)
    meta = jnp.broadcast_to(full["meta_tokens"][None], (bsz, N_META, D_MODEL))
    h0 = jnp.concatenate([jnp.zeros((bsz, PAD_ROWS, D_MODEL), F32), meta, x], axis=1).reshape(bsz * tp, D_MODEL)
    xn0 = _rms_fwd(h0, full["ab_norm"], name="rms0")
    s5p, s5_vjp = _s5_tables(*[full[n] for n in ("s5_lambda_re", "s5_lambda_im", "s5_log_dt", "s5_b_re", "s5_b_im",
                                                   "s5_c_re", "s5_c_im", "s5_d")])
    mlw = _ml_weights(*[full[n] for n in ("ml_conv_w", "ml_conv_b", "ml_wq", "ml_wk", "ml_wv", "ml_w_gate",
                                           "ml_b_gate", "ml_norm", "ml_skip")])
    got_a = _exchange_wait(ga, [xn0, s5p["wbr"], s5p["wcr"], s5p["pr"], mlw["wq"], mlw["wk"], mlw["wv"], mlw["wgq"]])
    fwd_a = _sibling_forward_start(got_a[0], name="gather_a2")
    got_a = [_sibling_forward_wait(fwd_a, fwd_a["token"])]
    gb = _exchange_start([local["s5_glu_w"], local["ab_w_out"]], ["ag", "ag"], name="gather_b", dep=got_a[0])
    gc = _exchange_start([local["ssd_w_in"], local["ssd_w_out"]], ["ag", "ag"], name="gather_c", dep=gb["token"])
    full["ab_w_in"] = assemble_big("ab_w_in", got_a[0])
    cuts0 = (0, S5_WIDTH, 2 * S5_WIDTH, 2 * S5_WIDTH + ML_WIDTH, 2 * (S5_WIDTH + ML_WIDTH))
    w_in0 = [full["ab_w_in"][:, cuts0[i]:cuts0[i + 1]] for i in range(4)]

    u, za, xb, zb = [_mm(xn0, wi, "NN", name=f"in0_{i}") for i, wi in enumerate(w_in0)]
    got_b = []

    def glu_w_after(scan_out):
        got_b.extend(_exchange_wait(gb, scan_out))
        return assemble_big("s5_glu_w", got_b[0])

    sv5 = _s5_layer_fwd(u, s5p, glu_w_after, bsz, nc)
    glu_w = assemble_big("s5_glu_w", got_b[0])
    w_out0 = assemble_big("ab_w_out", got_b[1])
    w_out0 = [w_out0[:S5_WIDTH], w_out0[S5_WIDTH:]]
    ya = _s5_post(sv5["y1"], sv5["glu_pre"], glu_b, za)
    yb, svm = _ml_layer_fwd(xb, zb, mlw, bsz, nc)
    h1 = _mm(ya, w_out0[0], "NN", name="out0_a", add=h0)
    h1 = _mm(yb, w_out0[1], "NN", name="out0_b", add=h1)
    got_c = _exchange_wait(gc, h1)
    w_in1, w_out1 = assemble_big("ssd_w_in", got_c[0]), assemble_big("ssd_w_out", got_c[1])
    cuts1 = (0, SSD_INNER, 2 * SSD_INNER, 2 * SSD_INNER + SSD_BC, 2 * SSD_INNER + 2 * SSD_BC)
    w_in1 = [w_in1[:, cuts1[i]:cuts1[i + 1]] for i in range(4)] + [_pad_lanes(w_in1[:, cuts1[4]:])]
    xn1 = _rms_fwd(h1, full["ssd_norm"], name="rms1")
    z1, xs_in, bm_in, cm_in, dt_rows = [_mm(xn1, wi, "NN", name=f"in1_{i}") for i, wi in enumerate(w_in1)]
    ssdw = _ssd_weights(*[full[n] for n in ("ssd_conv_w", "ssd_conv_b", "ssd_dt_bias", "ssd_a_log", "ssd_d",
                                             "ssd_gnorm")])
    yn, svs = _ssd_layer_fwd(z1, xs_in, bm_in, cm_in, dt_rows, ssdw, bsz, nc)
    h2 = _mm(yn, w_out1, "NN", name="out1", add=h1)
    loss_part, dh2, dfinal, dh2_b = _final_loss(h2, full["final_norm"], loss_target, bsz, nc)

    g = {"final_norm": dfinal}
    dyn = _mm(dh2_b, w_out1, "NT", name="d_out1")
    g["ssd_w_out"] = _mm(yn, dh2_b, "TN", name="dw_out1", out_dtype=BF16)
    dz1, dxs, dbm, dcm, ddt, gs = _ssd_layer_bwd(dyn, z1, xs_in, bm_in, cm_in, svs, ssdw, bsz, nc)
    g.update(gs)
    dps1 = (dz1, dxs, dbm, dcm, ddt)
    dxn1 = None
    for i, (dp, wi) in enumerate(zip(dps1, w_in1)):
        dxn1 = _mm(dp, wi, "NT", name=f"d_in1_{i}", add=dxn1)
    dw1 = [_mm(xn1, dp, "TN", name=f"dw_in1_{i}", out_dtype=BF16) for i, dp in enumerate(dps1)]
    g["ssd_w_in"] = jnp.concatenate(dw1[:4] + [dw1[4][:, :SSD_HEADS]], axis=1)

    def local_shape(n):
        return w[n].shape

    def slabs(n):
        gf = g[n].reshape((1,) + tuple(g[n].shape)) if n != "meta_tokens" else g[n]
        full_shape = tuple(d * (N_DEV if i == SHARD_AXIS[n] else 1) for i, d in enumerate(local_shape(n)))
        return _split(gf.reshape(full_shape), SHARD_AXIS[n])

    x1 = _exchange_start([slabs("ssd_w_in")[:, 0], slabs("ssd_w_out")[:, 0]], ["a2a", "a2a"], name="grads_1")
    dh1, g["ssd_norm"], dh1_b = _rms_bwd(h1, full["ssd_norm"], dxn1, dh2, name="rms1_bwd", dep=x1["token"])
    dya = _mm(dh1_b, w_out0[0], "NT", name="d_out0_a")
    dyb = _mm(dh1_b, w_out0[1], "NT", name="d_out0_b")
    g["ab_w_out"] = jnp.concatenate([_mm(ya, dh1_b, "TN", name="dw_out0_a", out_dtype=BF16),
                                     _mm(yb, dh1_b, "TN", name="dw_out0_b", out_dtype=BF16)], axis=0)
    du, dza, g5 = _s5_layer_bwd(dya, u, za, sv5, s5p, s5_vjp, glu_w, glu_b, bsz, nc)
    g.update(g5)
    x2 = _exchange_start([slabs("ab_w_out")[:, 0], _bf(slabs("s5_glu_w")[:, 0])], ["a2a", "a2a"], name="grads_2")
    dxb, dzb, gm = _ml_layer_bwd(dyb, xb, zb, svm, mlw, bsz, nc, dep=x2["token"])
    g.update(gm)
    dps0 = (du, dza, dxb, dzb)
    dw0 = [_mm(xn0, dp, "TN", name=f"dw_in0_{i}", out_dtype=BF16, tn=S5_WIDTH, slabs=True) for i, dp in enumerate(dps0)]
    dw_in0_slabs = jnp.concatenate(dw0, axis=0)
    x3 = _exchange_start([dw_in0_slabs], ["a2a"], name="grads_3")
    dxn0 = None
    for i, (dp, wi) in enumerate(zip(dps0, w_in0)):
        dxn0 = _mm(dp, wi, "NT", name=f"d_in0_{i}", add=dxn0, dep=x3["token"] if i == 0 else None)
    grad_x, d_chunk0, g["ab_norm"] = _rms_bwd_first(h0, full["ab_norm"], dxn0, dh1, bsz, nc, name="rms0_bwd")
    g["meta_tokens"] = jnp.sum(d_chunk0[:, PAD_ROWS:], axis=0)

    small_g = _pack([(slabs(n), 1) for n in SMALL])
    repl_g = _pack([(g[n], 0) for n in REPL])
    x4 = _exchange_start([small_g, repl_g, loss_part], ["a2a", "ag", "ag"], name="grads_4")

    def update_big(n, gp):
        return _adamw(w[n][0], mom[n][0], var[n][0], gp, name=f"adamw_{n}")

    res = {}
    ex1 = _exchange_wait(x1, x4["token"])
    res["ssd_w_in"], res["ssd_w_out"] = update_big("ssd_w_in", ex1[0]), update_big("ssd_w_out", ex1[1])
    ex2 = _exchange_wait(x2, res["ssd_w_out"][0])
    res["ab_w_out"], res["s5_glu_w"] = update_big("ab_w_out", ex2[0]), update_big("s5_glu_w", ex2[1])
    ex3 = _exchange_wait(x3, [res[n][0] for n in ("ssd_w_in", "ssd_w_out", "ab_w_out", "s5_glu_w")])
    res["ab_w_in"] = update_big("ab_w_in", ex3[0])
    ex4 = _exchange_wait(x4, res["ab_w_in"][0])
    loss = jnp.sum(ex4[2][:, 0, 0])
    for names, gp, tag in ((SMALL, ex4[0], "small"), (REPL, ex4[1], "repl")):
        shapes = [local_shape(n) for n in names]
        packs = [_pack([(d[n], 0) for n in names]) for d in (w, mom, var)]
        outs = _adamw(packs[0], packs[1], packs[2], gp, name=f"adamw_{tag}")
        for k, o in enumerate(outs):
            for n, a in zip(names, _unpack(o, shapes)):
                res.setdefault(n, [None] * 4)[k] = a
    outs = [loss, grad_x]
    for k in range(4):
        outs += [res[n][k].reshape(local_shape(n)) for n in WNAMES]
    return tuple(outs)
```

```python
import functools
import math

import jax
import jax.numpy as jnp
from jax import lax
from jax.experimental import pallas as pl
from jax.experimental.pallas import tpu as pltpu

F32 = jnp.float32
BF16 = jnp.bfloat16

D_MODEL = 2048
SEQ = 2048
N_META = 16
CHUNK = 128
PAD_ROWS = CHUNK - N_META
NORM_EPS = 1e-6
HEAD_NORM_EPS = 1e-5
S5_WIDTH = 1024
S5_GROUPS = 64
S5_GROUP_SIZE = 16
S5_STATE = 64
S5_GB = 8
S5_LANES = S5_GB * S5_STATE
ML_WIDTH = 3072
ML_HEADS = 8
ML_DH = 384
ML_CONV = 4
QKV_BLOCK = 4
SSD_INNER = 4096
SSD_HEADS = 64
SSD_P = 64
SSD_N = 128
SSD_GROUPS = 8
SSD_HPG = 8
SSD_GW = SSD_HPG * SSD_P
N_DEV = 8
ADAM_LR, ADAM_B1, ADAM_B2, ADAM_EPS, ADAM_WD, ADAM_STEP = 0.001, 0.9, 0.999, 1e-08, 0.01, 10
NEG = -1e30
VMEM_CAP = 60 * 1024 * 1024
MM_BLOCK_BUDGET = 22 * 1024 * 1024
MESH = pl.DeviceIdType.MESH

NN = (((1,), (0,)), ((), ()))
NT = (((1,), (1,)), ((), ()))
TN = (((0,), (0,)), ((), ()))


def _dot(a, b, dims=NN):
    return lax.dot_general(a, b, dims, preferred_element_type=F32)


def _bf(x):
    return x.astype(BF16)


def _pick(n, cands):
    for c in cands:
        if n % c == 0:
            return c
    return n


def _nbytes(shape, dtype):
    return math.prod(shape) * jnp.dtype(dtype).itemsize


ANY_SPEC = pl.BlockSpec(memory_space=pl.ANY)


def _pc(body, *, name, grid, in_specs, out_specs, out_shape, scratch=(), vmem=None, dep=None):
    limit = None if vmem is None else int(min(VMEM_CAP, max(32 * 1024 * 1024, 2 * vmem + (8 << 20))))
    n_in = len(in_specs)
    if dep is not None:
        inner = body

        def body(*refs):
            inner(*refs[:n_in], *refs[n_in + 1:])

        in_specs = list(in_specs) + [ANY_SPEC]
    call = pl.pallas_call(
        body, name=name, grid=grid, in_specs=in_specs, out_specs=out_specs, out_shape=out_shape,
        scratch_shapes=list(scratch),
        compiler_params=pltpu.CompilerParams(dimension_semantics=("arbitrary",) * len(grid), vmem_limit_bytes=limit))
    return call if dep is None else (lambda *args: call(*args, dep))


def _silu(x):
    return x * jax.nn.sigmoid(x)


def _dsilu(x):
    s = jax.nn.sigmoid(x)
    return s * (1.0 + x * (1.0 - s))


def _gelu_and_grad(x):
    c0 = math.sqrt(2.0 / math.pi)
    inner = c0 * (x + 0.044715 * x * x * x)
    t = jnp.tanh(inner)
    g = 0.5 * x * (1.0 + t)
    dg = 0.5 * (1.0 + t) + 0.5 * x * (1.0 - t * t) * c0 * (1.0 + 3 * 0.044715 * x * x)
    return g, dg


def _mm(a, b, mode, *, name, add=None, out_dtype=F32, tn=None, slabs=False, dep=None):
    if mode == "NN":
        (m, k), (k2, n) = a.shape, b.shape
    elif mode == "NT":
        (m, k), (n, k2) = a.shape, b.shape
    else:
        (k, m), (k2, n) = a.shape, b.shape
    assert k == k2, (a.shape, b.shape, mode)
    tm = _pick(m, (1088, 1024, 768, 512, 384, 256, 128))
    def block_bytes(tk, tn_):
        return (_nbytes((tm, tk), a.dtype) + _nbytes((tk, tn_), b.dtype) + _nbytes((tm, tn_), out_dtype)
                + (_nbytes((tm, tn_), F32) if add is not None else 0))

    budget = MM_BLOCK_BUDGET // 2 if mode == "TN" else MM_BLOCK_BUDGET
    if tn is None:
        tn = _pick(n, (512, 384, 256, 128))
        if mode != "TN" and n % 1024 == 0 and block_bytes(k, 1024) <= (2 * budget) // 3:
            tn = 1024
    tk = k if block_bytes(k, tn) <= budget else _pick(k, (2176, 2048, 1088, 1024, 768, 512, 384, 256, 128))
    nk = k // tk
    dims = {"NN": NN, "NT": NT, "TN": TN}[mode]

    def body(*refs):
        a_ref, b_ref = refs[0], refs[1]
        add_ref = refs[2] if add is not None else None
        o_ref = refs[3] if add is not None else refs[2]

        def finish(r):
            if add_ref is not None:
                r = r + add_ref[...]
            o_ref[...] = r.reshape(o_ref.shape).astype(o_ref.dtype)

        prod = _dot(_bf(a_ref[...]), _bf(b_ref[...]), dims)
        if nk == 1:
            finish(prod)
            return
        acc_ref = refs[-1]
        kk = pl.program_id(2)

        @pl.when(kk == 0)
        def _():
            acc_ref[...] = prod

        @pl.when(kk > 0)
        def _():
            acc_ref[...] += prod

        @pl.when(kk == nk - 1)
        def _():
            finish(acc_ref[...])

    if mode == "NN":
        a_spec = pl.BlockSpec((tm, tk), lambda i, j, kk: (i, kk))
        b_spec = pl.BlockSpec((tk, tn), lambda i, j, kk: (kk, j))
    elif mode == "NT":
        a_spec = pl.BlockSpec((tm, tk), lambda i, j, kk: (i, kk))
        b_spec = pl.BlockSpec((tn, tk), lambda i, j, kk: (j, kk))
    else:
        a_spec = pl.BlockSpec((tk, tm), lambda i, j, kk: (kk, i))
        b_spec = pl.BlockSpec((tk, tn), lambda i, j, kk: (kk, j))
    in_specs = [a_spec, b_spec]
    args = [a, b]
    if add is not None:
        in_specs.append(pl.BlockSpec((tm, tn), lambda i, j, kk: (i, j)))
        args.append(add)
    if slabs:
        out_shape = jax.ShapeDtypeStruct((n // tn, m, tn), out_dtype)
        out_spec = pl.BlockSpec((1, tm, tn), lambda i, j, kk: (j, i, 0))
    else:
        out_shape = jax.ShapeDtypeStruct((m, n), out_dtype)
        out_spec = pl.BlockSpec((tm, tn), lambda i, j, kk: (i, j))
    return _pc(body, name=name, grid=(m // tm, n // tn, nk), in_specs=in_specs, out_specs=out_spec,
               out_shape=out_shape, scratch=[] if nk == 1 else [pltpu.VMEM((tm, tn), F32)],
               vmem=block_bytes(tk, tn) + (0 if nk == 1 else _nbytes((tm, tn), F32) // 2), dep=dep)(*args)


def _rms_fwd(x, g, *, name):
    r, d = x.shape
    tm = _pick(r, (256, 128))

    def body(x_ref, g_ref, o_ref):
        xv = x_ref[...]
        rstd = lax.rsqrt(jnp.mean(xv * xv, axis=1, keepdims=True) + NORM_EPS)
        o_ref[...] = (xv * rstd * g_ref[...]).astype(o_ref.dtype)

    return _pc(body, name=name, grid=(r // tm,),
               in_specs=[pl.BlockSpec((tm, d), lambda i: (i, 0)), pl.BlockSpec((1, d), lambda i: (0, 0))],
               out_specs=pl.BlockSpec((tm, d), lambda i: (i, 0)), out_shape=jax.ShapeDtypeStruct((r, d), BF16),
               vmem=tm * d * 6)(x, g.reshape(1, d))


def _rms_bwd(x, g, dxn, dres, *, name, dep=None):
    r, d = x.shape
    tm = _pick(r, (256, 128))

    def body(x_ref, g_ref, dxn_ref, dres_ref, dx_ref, dg_ref, db_ref):
        @pl.when(pl.program_id(0) == 0)
        def _():
            dg_ref[...] = jnp.zeros_like(dg_ref)

        xv = x_ref[...]
        rstd = lax.rsqrt(jnp.mean(xv * xv, axis=1, keepdims=True) + NORM_EPS)
        xh = xv * rstd
        dy = dxn_ref[...]
        dg_ref[...] += jnp.sum(dy * xh, axis=0, keepdims=True)
        dyg = dy * g_ref[...]
        dx_ref[...] = dres_ref[...] + rstd * (dyg - xh * jnp.mean(dyg * xh, axis=1, keepdims=True))

        db_ref[...] = _bf(dx_ref[...])

    row = pl.BlockSpec((tm, d), lambda i: (i, 0))
    vec = pl.BlockSpec((1, d), lambda i: (0, 0))
    return _pc(body, name=name, grid=(r // tm,), in_specs=[row, vec, row, row], out_specs=[row, vec, row],
               out_shape=[jax.ShapeDtypeStruct((r, d), F32), jax.ShapeDtypeStruct((1, d), F32),
                          jax.ShapeDtypeStruct((r, d), BF16)],
               vmem=tm * d * 18, dep=dep)(x, g.reshape(1, d), dxn, dres)


def _rms_bwd_first(x, g, dxn, dres, bsz, nc, *, name):
    d = x.shape[1]

    def body(x_ref, g_ref, dxn_ref, dres_ref, gx_ref, d0_ref, dg_ref):
        b, c = pl.program_id(0), pl.program_id(1)

        @pl.when((b == 0) & (c == 0))
        def _():
            dg_ref[...] = jnp.zeros_like(dg_ref)

        xv = x_ref[...]
        rstd = lax.rsqrt(jnp.mean(xv * xv, axis=1, keepdims=True) + NORM_EPS)
        xh = xv * rstd
        dy = dxn_ref[...]
        dg_ref[...] += jnp.sum(dy * xh, axis=0, keepdims=True)
        dyg = dy * g_ref[...]
        dx = dres_ref[...] + rstd * (dyg - xh * jnp.mean(dyg * xh, axis=1, keepdims=True))

        @pl.when(c == 0)
        def _():
            d0_ref[0] = dx

        @pl.when(c > 0)
        def _():
            gx_ref[0] = dx

    row = pl.BlockSpec((CHUNK, d), lambda b, c: (b * nc + c, 0))
    vec = pl.BlockSpec((1, d), lambda b, c: (0, 0))
    return _pc(body, name=name, grid=(bsz, nc), in_specs=[row, vec, row, row],
               out_specs=[pl.BlockSpec((1, CHUNK, d), lambda b, c: (b, jnp.maximum(c - 1, 0), 0)),
                          pl.BlockSpec((1, CHUNK, d), lambda b, c: (b, 0, 0)), vec],
               out_shape=[jax.ShapeDtypeStruct((bsz, (nc - 1) * CHUNK, d), F32),
                          jax.ShapeDtypeStruct((bsz, CHUNK, d), F32), jax.ShapeDtypeStruct((1, d), F32)],
               vmem=CHUNK * d * 24)(x, g.reshape(1, d), dxn, dres)


def _final_loss(h, g, target, bsz, nc):
    d = h.shape[1]

    def body(h_ref, g_ref, t_ref, loss_ref, dh_ref, dg_ref, db_ref):
        b, c = pl.program_id(0), pl.program_id(1)

        @pl.when((b == 0) & (c == 0))
        def _():
            loss_ref[...] = jnp.zeros_like(loss_ref)
            dg_ref[...] = jnp.zeros_like(dg_ref)

        @pl.when(c == 0)
        def _():
            dh_ref[...] = jnp.zeros_like(dh_ref)
            db_ref[...] = jnp.zeros_like(db_ref)

        @pl.when(c > 0)
        def _():
            xv = h_ref[...]
            rstd = lax.rsqrt(jnp.mean(xv * xv, axis=1, keepdims=True) + NORM_EPS)
            xh = xv * rstd
            gv = g_ref[...]
            err = xh * gv - t_ref[0]
            loss_ref[...] += 0.5 * jnp.sum(jnp.mean(err * err, axis=1, keepdims=True))
            dy = err * (1.0 / d)
            dg_ref[...] += jnp.sum(dy * xh, axis=0, keepdims=True)
            dyg = dy * gv
            dh = rstd * (dyg - xh * jnp.mean(dyg * xh, axis=1, keepdims=True))
            dh_ref[...] = dh
            db_ref[...] = _bf(dh)

    row = pl.BlockSpec((CHUNK, d), lambda b, c: (b * nc + c, 0))
    vec = pl.BlockSpec((1, d), lambda b, c: (0, 0))
    return _pc(body, name="final_loss", grid=(bsz, nc),
               in_specs=[row, vec, pl.BlockSpec((1, CHUNK, d), lambda b, c: (b, jnp.maximum(c - 1, 0), 0))],
               out_specs=[pl.BlockSpec((8, 128), lambda b, c: (0, 0)), row, vec, row],
               out_shape=[jax.ShapeDtypeStruct((8, 128), F32), jax.ShapeDtypeStruct(h.shape, F32),
                          jax.ShapeDtypeStruct((1, d), F32), jax.ShapeDtypeStruct(h.shape, BF16)],
               vmem=CHUNK * d * 18)(h, g.reshape(1, d), target)


def _adamw(w, m, v, gparts, *, name):
    r, c = w.shape
    tr = _pick(r, (256, 128)) if r * c * 4 > (1 << 20) else r

    def body(w_ref, m_ref, v_ref, gp_ref, g_ref, d_ref, nm_ref, nv_ref):
        g = gp_ref[0].astype(F32)
        for j in range(1, N_DEV):
            g = g + gp_ref[j].astype(F32)
        mm = ADAM_B1 * m_ref[...] + (1.0 - ADAM_B1) * g
        vv = ADAM_B2 * v_ref[...] + (1.0 - ADAM_B2) * (g * g)
        m_hat = mm / (1.0 - ADAM_B1 ** ADAM_STEP)
        v_hat = vv / (1.0 - ADAM_B2 ** ADAM_STEP)
        g_ref[...] = g
        d_ref[...] = -ADAM_LR * (m_hat / (jnp.sqrt(v_hat) + ADAM_EPS) + ADAM_WD * w_ref[...])
        nm_ref[...] = mm
        nv_ref[...] = vv

    blk = pl.BlockSpec((tr, c), lambda i: (i, 0))
    out = jax.ShapeDtypeStruct((r, c), F32)
    return _pc(body, name=name, grid=(r // tr,),
               in_specs=[blk, blk, blk, pl.BlockSpec((N_DEV, tr, c), lambda i: (0, i, 0))],
               out_specs=[blk, blk, blk, blk], out_shape=[out, out, out, out],
               vmem=tr * c * (4 * 7 + N_DEV * jnp.dtype(gparts.dtype).itemsize))(w, m, v, gparts)


PEERS = (1, 2, 4, 6, 3, 5, 7)
HBM_SPEC = pl.BlockSpec(memory_space=pltpu.HBM)
SEM_SPEC = pl.BlockSpec(memory_space=pltpu.SEMAPHORE)
SIDE_EFFECT = pltpu.SideEffectType.DATAFLOW_SIDE_EFFECTING


def _peer(p):
    x, y, c = lax.axis_index("x"), lax.axis_index("y"), lax.axis_index("c")
    tx, ty, tc = x ^ ((p >> 2) & 1), y ^ ((p >> 1) & 1), c ^ (p & 1)
    return (tx, ty, tc), 4 * tx + 2 * ty + tc


def _place_own(a, kind, *, name):
    rows, cols = a.shape[-2:]
    small = _nbytes((rows, cols), a.dtype) <= (2 << 20)
    tr = rows if small else _pick(rows, (512, 256, 128, 64, 32, 16))
    me = (4 * lax.axis_index("x") + 2 * lax.axis_index("y") + lax.axis_index("c")).astype(jnp.int32).reshape(1)

    def body(me_ref, in_ref, out_ref):
        out_ref[...] = in_ref[...].reshape(out_ref.shape)

    if kind == "a2a":
        in_spec = pl.BlockSpec((1, tr, cols), lambda i, me_ref: (me_ref[0], i, 0))
    else:
        in_spec = pl.BlockSpec((tr, cols), lambda i, me_ref: (i, 0))
    return pl.pallas_call(
        body, name=name, out_shape=jax.ShapeDtypeStruct((N_DEV, rows, cols), a.dtype),
        grid_spec=pltpu.PrefetchScalarGridSpec(
            num_scalar_prefetch=1, grid=(rows // tr,), in_specs=[in_spec],
            out_specs=pl.BlockSpec((1, tr, cols), lambda i, me_ref: (me_ref[0], i, 0))))(me, a)


def _exchange_copies(ins, lands, send_sems, recv_sems, kinds, incoming, peers=PEERS):
    me = 4 * lax.axis_index("x") + 2 * lax.axis_index("y") + lax.axis_index("c")
    copies = []
    for i, kind in enumerate(kinds):
        for p in peers:
            dev, tgt = _peer(p)
            k = i * (N_DEV - 1) + p - 1
            copies.append(pltpu.make_async_remote_copy(
                src_ref=ins[i].at[tgt] if kind == "a2a" else ins[i], dst_ref=lands[i].at[tgt if incoming else me],
                send_sem=send_sems.at[k], recv_sem=recv_sems.at[k], device_id=dev, device_id_type=MESH))
    return copies


def _exchange_start(arrays, kinds, *, name, dep=None, peers=PEERS):
    n = len(arrays)
    lands = [_place_own(a, k, name=f"{name}_own{i}") for i, (a, k) in enumerate(zip(arrays, kinds))]
    extra = [] if dep is None else [dep]

    def body(*refs):
        ins, lnd = refs[:n], refs[n:2 * n]
        send_sems, recv_sems = refs[2 * n + len(extra)], refs[2 * n + len(extra) + 1]
        token = refs[-1]
        for cp in _exchange_copies(ins, lnd, send_sems, recv_sems, kinds, False, peers):
            cp.start()
        token[...] = jnp.zeros_like(token)

    sem = pltpu.SemaphoreType.DMA((n * (N_DEV - 1),))
    outs = pl.pallas_call(
        body, name=name, in_specs=[HBM_SPEC] * (2 * n) + [ANY_SPEC] * len(extra),
        out_specs=[SEM_SPEC, SEM_SPEC] + [HBM_SPEC] * (2 * n) + [pl.BlockSpec(memory_space=pltpu.VMEM)],
        out_shape=[sem, sem] + [pltpu.HBM(a.shape, a.dtype) for a in arrays + lands]
        + [jax.ShapeDtypeStruct((8, 128), F32)],
        input_output_aliases={i: 2 + i for i in range(2 * n)},
        compiler_params=pltpu.CompilerParams(has_side_effects=SIDE_EFFECT),
    )(*[pltpu.with_memory_space_constraint(a, pltpu.HBM) for a in arrays + lands], *extra)
    return dict(send=outs[0], recv=outs[1], ins=list(outs[2:2 + n]), lands=list(outs[2 + n:2 + 2 * n]),
                token=outs[-1], kinds=kinds, name=name, peers=peers)


def _exchange_wait(h, after):
    n = len(h["ins"])
    kinds = h["kinds"]

    def body(*refs):
        ins, lnd = refs[:n], refs[n:2 * n]
        send_sems, recv_sems = refs[2 * n], refs[2 * n + 1]
        copies = _exchange_copies(ins, lnd, send_sems, recv_sems, kinds, True, h["peers"])
        for cp in copies:
            cp.wait_recv()
        for cp in copies:
            cp.wait_send()

    arrs = h["ins"] + h["lands"]
    after = list(after) if isinstance(after, (list, tuple)) else [after]
    outs = pl.pallas_call(
        body, name=h["name"] + "_wait", in_specs=[HBM_SPEC] * (2 * n) + [SEM_SPEC, SEM_SPEC] + [ANY_SPEC] * len(after),
        out_specs=[HBM_SPEC] * (2 * n), out_shape=[pltpu.HBM(a.shape, a.dtype) for a in arrs],
        input_output_aliases={i: i for i in range(2 * n)},
        compiler_params=pltpu.CompilerParams(has_side_effects=SIDE_EFFECT),
    )(*arrs, h["send"], h["recv"], *after)
    return list(outs[n:])


SAME_CORE = (0, 2, 4, 6)


def _forward_copies(land, send_sems, recv_sems, incoming):
    me = 4 * lax.axis_index("x") + 2 * lax.axis_index("y") + lax.axis_index("c")
    dev, sibling = _peer(1)
    return [pltpu.make_async_remote_copy(
        src_ref=land.at[me ^ q], dst_ref=land.at[(sibling if incoming else me) ^ q],
        send_sem=send_sems.at[j], recv_sem=recv_sems.at[j], device_id=dev, device_id_type=MESH)
        for j, q in enumerate(SAME_CORE)]


def _sibling_forward_start(land, *, name, dep=None):
    extra = [] if dep is None else [dep]

    def body(*refs):
        land_ref, send_sems, recv_sems, token = refs[0], refs[1 + len(extra)], refs[2 + len(extra)], refs[-1]
        for cp in _forward_copies(land_ref, send_sems, recv_sems, False):
            cp.start()
        token[...] = jnp.zeros_like(token)

    sem = pltpu.SemaphoreType.DMA((len(SAME_CORE),))
    outs = pl.pallas_call(
        body, name=name, in_specs=[HBM_SPEC] + [ANY_SPEC] * len(extra),
        out_specs=[SEM_SPEC, SEM_SPEC, HBM_SPEC, pl.BlockSpec(memory_space=pltpu.VMEM)],
        out_shape=[sem, sem, pltpu.HBM(land.shape, land.dtype), jax.ShapeDtypeStruct((8, 128), F32)],
        input_output_aliases={0: 2}, compiler_params=pltpu.CompilerParams(has_side_effects=SIDE_EFFECT),
    )(pltpu.with_memory_space_constraint(land, pltpu.HBM), *extra)
    return dict(send=outs[0], recv=outs[1], land=outs[2], token=outs[3], name=name)


def _sibling_forward_wait(h, after):
    def body(*refs):
        copies = _forward_copies(refs[0], refs[1], refs[2], True)
        for cp in copies:
            cp.wait_recv()
        for cp in copies:
            cp.wait_send()

    return pl.pallas_call(
        body, name=h["name"] + "_wait", in_specs=[HBM_SPEC, SEM_SPEC, SEM_SPEC, ANY_SPEC], out_specs=HBM_SPEC,
        out_shape=pltpu.HBM(h["land"].shape, h["land"].dtype), input_output_aliases={0: 0},
        compiler_params=pltpu.CompilerParams(has_side_effects=SIDE_EFFECT),
    )(h["land"], h["send"], h["recv"], after)


def _s5_params(lam_re, lam_im, log_dt, b_re, b_im):
    dt = jnp.exp(log_dt)[:, None]
    mag = jnp.exp(lam_re * dt)
    ar, ai = mag * jnp.cos(lam_im * dt), mag * jnp.sin(lam_im * dt)
    den = lam_re * lam_re + lam_im * lam_im
    qr = ((ar - 1.0) * lam_re + ai * lam_im) / den
    qi = (ai * lam_re - (ar - 1.0) * lam_im) / den
    bbr = qr[..., None] * b_re - qi[..., None] * b_im
    bbi = qr[..., None] * b_im + qi[..., None] * b_re
    return ar, ai, bbr, bbi


def _s5_power_table(ar, ai):
    pr, pi = ar.reshape(1, -1), ai.reshape(1, -1)
    while pr.shape[0] < 8:
        sr, si = pr[-1:], pi[-1:]
        pr, pi = (jnp.concatenate([pr, pr * sr - pi * si], axis=0), jnp.concatenate([pi, pr * si + pi * sr], axis=0))
    return pr, pi


def _blockdiag(w, rows, cols):
    w = w.reshape(S5_GB, S5_GB, rows, cols)
    eye = jnp.eye(S5_GB, dtype=w.dtype)
    return jnp.einsum("abrc,bd->abrdc", w, eye).reshape(S5_GB, S5_GB * rows, S5_GB * cols)


def _blockdiag_extract(w, rows, cols):
    w = w.reshape(S5_GB, S5_GB, rows, S5_GB, cols)
    return jnp.einsum("abrbc->abrc", w).reshape(S5_GROUPS, rows, cols)


def _s5_scan_specs(bsz, nc, rev):
    def cc(c):
        return (nc - 1 - c) if rev else c

    return dict(
        u=pl.BlockSpec((bsz, CHUNK, CHUNK), lambda g, c: (0, cc(c), g)),
        x=pl.BlockSpec((bsz, CHUNK, S5_LANES), lambda g, c: (0, cc(c), g)),
        wb=pl.BlockSpec((1, CHUNK, S5_LANES), lambda g, c: (g, 0, 0)),
        wc=pl.BlockSpec((1, S5_LANES, CHUNK), lambda g, c: (g, 0, 0)),
        tab=pl.BlockSpec((8, S5_LANES), lambda g, c: (0, g)),
        step=pl.BlockSpec((8, S5_LANES), lambda g, c: (0, g)),
        d=pl.BlockSpec((1, CHUNK), lambda g, c: (0, g)),
        lane=pl.BlockSpec((1, S5_LANES), lambda g, c: (0, g)),
        xprev=pl.BlockSpec((bsz, 8, S5_LANES), lambda g, c: (0, jnp.maximum(cc(c) * (CHUNK // 8) - 1, 0), g)),
    )


def _s5_fwd(u, wbr, wbi, pr, pi, sr, si, wcr, wci, d, bsz, nc):
    r = u.shape[0]
    tp = r // bsz
    sp = _s5_scan_specs(bsz, nc, False)

    def body(u_all, wbr_ref, wbi_ref, pr_ref, pi_ref, sr_ref, si_ref, wcr_ref, wci_ref, d_ref,
             xr_all, xi_all, y1_all, g_all, cr_sall, ci_sall):
        @pl.when(pl.program_id(1) == 0)
        def _():
            cr_sall[...] = jnp.zeros_like(cr_sall)
            ci_sall[...] = jnp.zeros_like(ci_sall)

        for bi in range(bsz):
            one(u_all.at[bi], wbr_ref, wbi_ref, pr_ref, pi_ref, sr_ref, si_ref, wcr_ref, wci_ref, d_ref,
                xr_all.at[bi], xi_all.at[bi], y1_all.at[bi], g_all.at[bi], cr_sall.at[bi], ci_sall.at[bi])

    def one(u_ref, wbr_ref, wbi_ref, pr_ref, pi_ref, sr_ref, si_ref, wcr_ref, wci_ref, d_ref,
            xr_ref, xi_ref, y1_ref, g_ref, cr_s, ci_s):
        uv = u_ref[...]
        ub = _bf(uv)
        xr, xi = _dot(ub, wbr_ref[0]), _dot(ub, wbi_ref[0])
        sub = lax.broadcasted_iota(jnp.int32, (CHUNK, S5_LANES), 0) % 8
        for k in range(3):
            s = 1 << k
            ar, ai = sr_ref[k:k + 1, :], si_ref[k:k + 1, :]
            hr = jnp.where(sub >= s, pltpu.roll(xr, s, 0), 0.0)
            hi = jnp.where(sub >= s, pltpu.roll(xi, s, 0), 0.0)
            xr, xi = xr + (ar * hr - ai * hi), xi + (ar * hi + ai * hr)
        cr, ci = cr_s[...], ci_s[...]
        tr, ti = pr_ref[...], pi_ref[...]
        outr, outi = [], []
        for g8 in range(CHUNK // 8):
            br, bi = xr[8 * g8:8 * g8 + 8, :], xi[8 * g8:8 * g8 + 8, :]
            br, bi = br + (tr * cr - ti * ci), bi + (tr * ci + ti * cr)
            cr, ci = br[7:8, :], bi[7:8, :]
            outr.append(br)
            outi.append(bi)
        xr, xi = jnp.concatenate(outr, axis=0), jnp.concatenate(outi, axis=0)
        cr_s[...] = cr
        ci_s[...] = ci
        xr_ref[...] = xr
        xi_ref[...] = xi
        y = _dot(_bf(xr), wcr_ref[0]) - _dot(_bf(xi), wci_ref[0]) + d_ref[...] * uv
        y1_ref[...] = y
        g_ref[...] = _bf(_gelu_and_grad(y)[0])

    ns = S5_GROUPS * S5_STATE
    xr, xi, y1, g = _pc(
        body, name="s5_fwd", grid=(S5_GB, nc),
        in_specs=[sp["u"], sp["wb"], sp["wb"], sp["tab"], sp["tab"], sp["step"], sp["step"], sp["wc"], sp["wc"],
                  sp["d"]],
        out_specs=[sp["x"], sp["x"], sp["u"], sp["u"]],
        out_shape=[jax.ShapeDtypeStruct((bsz, tp, ns), F32)] * 2
        + [jax.ShapeDtypeStruct((bsz, tp, S5_WIDTH), F32), jax.ShapeDtypeStruct((bsz, tp, S5_WIDTH), BF16)],
        scratch=[pltpu.VMEM((bsz, 1, S5_LANES), F32)] * 2, vmem=8 << 20,
    )(_seq(u, bsz), wbr, wbi, pr, pi, sr, si, wcr, wci, d)
    return xr.reshape(r, ns), xi.reshape(r, ns), y1.reshape(r, S5_WIDTH), g.reshape(r, S5_WIDTH)


def _s5_post(y1, glu_pre, glu_b, z):
    r, w = y1.shape
    tm = _pick(r, (256, 128))

    def body(y_ref, p_ref, b_ref, z_ref, o_ref):
        g = _gelu_and_grad(y_ref[...])[0]
        o_ref[...] = _bf(g * jax.nn.sigmoid(p_ref[...] + b_ref[...]) * _silu(z_ref[...]))

    row = pl.BlockSpec((tm, w), lambda i: (i, 0))
    return _pc(body, name="s5_post", grid=(r // tm,), in_specs=[row, row, pl.BlockSpec((1, w), lambda i: (0, 0)), row],
               out_specs=row, out_shape=jax.ShapeDtypeStruct((r, w), BF16), vmem=tm * w * 16)(y1, glu_pre, glu_b, z)


def _s5_post_bwd(dya, y1, glu_pre, glu_b, z):
    r, w = y1.shape
    tm = _pick(r, (256, 128))

    def body(dy_ref, y_ref, p_ref, b_ref, z_ref, dz_ref, dp_ref, dg_ref, db_ref):
        @pl.when(pl.program_id(0) == 0)
        def _():
            db_ref[...] = jnp.zeros_like(db_ref)

        g = _gelu_and_grad(y_ref[...])[0]
        s = jax.nn.sigmoid(p_ref[...] + b_ref[...])
        zv = z_ref[...]
        dy = dy_ref[...]
        do = dy * _silu(zv)
        dz_ref[...] = _bf(dy * g * s * _dsilu(zv))
        dp = do * g * s * (1.0 - s)
        dp_ref[...] = _bf(dp)
        db_ref[...] += jnp.sum(dp, axis=0, keepdims=True)
        dg_ref[...] = do * s

    row = pl.BlockSpec((tm, w), lambda i: (i, 0))
    vec = pl.BlockSpec((1, w), lambda i: (0, 0))
    return _pc(body, name="s5_post_bwd", grid=(r // tm,), in_specs=[row, row, row, vec, row],
               out_specs=[row, row, row, vec],
               out_shape=[jax.ShapeDtypeStruct((r, w), BF16), jax.ShapeDtypeStruct((r, w), BF16),
                          jax.ShapeDtypeStruct((r, w), F32), jax.ShapeDtypeStruct((1, w), F32)],
               vmem=tm * w * 24)(dya, y1, glu_pre, glu_b, z)


def _s5_bwd(dg, y1, u, xr, xi, wbr, wbi, qr, qi, sr, si, wcr, wci, d, bsz, nc):
    r = u.shape[0]
    tp = r // bsz
    sp = _s5_scan_specs(bsz, nc, True)

    def body(dg_all, y1_all, u_all, xr_all, xi_all, xpr_all, xpi_all, wbr_ref, wbi_ref, qr_ref, qi_ref, sr_ref, si_ref,
             wcr_ref, wci_ref, d_ref, du_all, dd_ref, dwcr_ref, dwci_ref, dwbr_ref, dwbi_ref, dar_ref, dai_ref,
             cr_sall, ci_sall):
        c = pl.program_id(1)

        @pl.when(c == 0)
        def _():
            for ref in (dd_ref, dwcr_ref, dwci_ref, dwbr_ref, dwbi_ref, dar_ref, dai_ref, cr_sall, ci_sall):
                ref[...] = jnp.zeros_like(ref)

        for bi in range(bsz):
            one(c, dg_all.at[bi], y1_all.at[bi], u_all.at[bi], xr_all.at[bi], xi_all.at[bi], xpr_all.at[bi],
                xpi_all.at[bi], wbr_ref, wbi_ref, qr_ref, qi_ref, sr_ref, si_ref, wcr_ref, wci_ref, d_ref,
                du_all.at[bi], dd_ref, dwcr_ref, dwci_ref, dwbr_ref, dwbi_ref, dar_ref, dai_ref, cr_sall.at[bi],
                ci_sall.at[bi])

    def one(c, dg_ref, y1_ref, u_ref, xr_ref, xi_ref, xpr_ref, xpi_ref, wbr_ref, wbi_ref, qr_ref, qi_ref, sr_ref, si_ref,
            wcr_ref, wci_ref, d_ref, du_ref, dd_ref, dwcr_ref, dwci_ref, dwbr_ref, dwbi_ref, dar_ref, dai_ref,
            cr_s, ci_s):
        uv = u_ref[...]
        ub = _bf(uv)
        dy = dg_ref[...] * _gelu_and_grad(y1_ref[...])[1]
        dd_ref[...] += jnp.sum(dy * uv, axis=0, keepdims=True)
        dyb = _bf(dy)
        xr, xi = xr_ref[...], xi_ref[...]
        dwcr_ref[0] += _dot(_bf(xr), dyb, TN)
        dwci_ref[0] -= _dot(_bf(xi), dyb, TN)
        lr, li = _dot(dyb, wcr_ref[0], NT), -_dot(dyb, wci_ref[0], NT)
        row = lax.broadcasted_iota(jnp.int32, (CHUNK, S5_LANES), 0)
        sub = row % 8
        for k in range(3):
            s = 1 << k
            ar, ai = sr_ref[k:k + 1, :], si_ref[k:k + 1, :]
            hr = jnp.where(sub < 8 - s, pltpu.roll(lr, CHUNK - s, 0), 0.0)
            hi = jnp.where(sub < 8 - s, pltpu.roll(li, CHUNK - s, 0), 0.0)
            lr, li = lr + (ar * hr + ai * hi), li + (ar * hi - ai * hr)
        cr, ci = cr_s[...], ci_s[...]
        tr, ti = qr_ref[...], qi_ref[...]
        outr, outi = [], []
        for g8 in reversed(range(CHUNK // 8)):
            br, bi = lr[8 * g8:8 * g8 + 8, :], li[8 * g8:8 * g8 + 8, :]
            br, bi = br + (tr * cr + ti * ci), bi + (tr * ci - ti * cr)
            cr, ci = br[0:1, :], bi[0:1, :]
            outr.append(br)
            outi.append(bi)
        lr, li = jnp.concatenate(outr[::-1], axis=0), jnp.concatenate(outi[::-1], axis=0)
        cr_s[...] = cr
        ci_s[...] = ci
        lrb, lib = _bf(lr), _bf(li)
        du_ref[...] = _bf(_dot(lrb, wbr_ref[0], NT) + _dot(lib, wbi_ref[0], NT) + dy * d_ref[...])
        dwbr_ref[0] += _dot(ub, lrb, TN)
        dwbi_ref[0] += _dot(ub, lib, TN)
        first = c == nc - 1
        pr0 = jnp.where(first, 0.0, xpr_ref[7:8, :])
        pi0 = jnp.where(first, 0.0, xpi_ref[7:8, :])
        xpr = jnp.where(row == 0, pr0, pltpu.roll(xr, 1, 0))
        xpi = jnp.where(row == 0, pi0, pltpu.roll(xi, 1, 0))
        dar_ref[...] += jnp.sum(lr * xpr + li * xpi, axis=0, keepdims=True)
        dai_ref[...] += jnp.sum(li * xpr - lr * xpi, axis=0, keepdims=True)

    st = jax.ShapeDtypeStruct
    xr3, xi3 = _seq(xr, bsz), _seq(xi, bsz)
    outs = _pc(body, name="s5_bwd", grid=(S5_GB, nc),
               in_specs=[sp["u"], sp["u"], sp["u"], sp["x"], sp["x"], sp["xprev"], sp["xprev"], sp["wb"], sp["wb"],
                         sp["tab"], sp["tab"], sp["step"], sp["step"], sp["wc"], sp["wc"], sp["d"]],
               out_specs=[sp["u"], sp["d"], sp["wc"], sp["wc"], sp["wb"], sp["wb"], sp["lane"], sp["lane"]],
               out_shape=[st((bsz, tp, S5_WIDTH), BF16), st((1, S5_WIDTH), F32),
                          st((S5_GB, S5_LANES, CHUNK), F32), st((S5_GB, S5_LANES, CHUNK), F32),
                          st((S5_GB, CHUNK, S5_LANES), F32), st((S5_GB, CHUNK, S5_LANES), F32),
                          st((1, S5_GROUPS * S5_STATE), F32), st((1, S5_GROUPS * S5_STATE), F32)],
               scratch=[pltpu.VMEM((bsz, 1, S5_LANES), F32)] * 2, vmem=12 << 20,
               )(_seq(dg, bsz), _seq(y1, bsz), _seq(u, bsz), xr3, xi3, xr3, xi3, wbr, wbi, qr, qi, sr, si, wcr, wci, d)
    return (outs[0].reshape(r, S5_WIDTH),) + tuple(outs[1:])


def _s5_layer_fwd(u, prm, glu_w, bsz, nc):
    xr, xi, y1, g = _s5_fwd(u, prm["wbr"], prm["wbi"], prm["pr"], prm["pi"], prm["sr"], prm["si"], prm["wcr"],
                            prm["wci"], prm["d"], bsz, nc)
    glu_pre = _mm(g, glu_w(y1) if callable(glu_w) else glu_w, "NN", name="s5_glu")
    return dict(xr=xr, xi=xi, y1=y1, g=g, glu_pre=glu_pre)


def _s5_layer_bwd(dya, u, z, sv, prm, pvjp, glu_w, glu_b, bsz, nc):
    dz, dglu, dg_direct, dglu_b = _s5_post_bwd(dya, sv["y1"], sv["glu_pre"], glu_b, z)
    dg = _mm(dglu, glu_w, "NT", name="s5_dg", add=dg_direct)
    dglu_w = _mm(sv["g"], dglu, "TN", name="s5_dglu_w")
    du, dd, dwcr, dwci, dwbr, dwbi, dar, dai = _s5_bwd(
        dg, sv["y1"], u, sv["xr"], sv["xi"], prm["wbr"], prm["wbi"], prm["qr"], prm["qi"], prm["sr"], prm["si"],
        prm["wcr"], prm["wci"], prm["d"], bsz, nc)
    dbbr = jnp.swapaxes(_blockdiag_extract(dwbr, S5_GROUP_SIZE, S5_STATE), 1, 2)
    dbbi = jnp.swapaxes(_blockdiag_extract(dwbi, S5_GROUP_SIZE, S5_STATE), 1, 2)
    dlr, dli, dldt, dbr, dbi = pvjp((dar.reshape(S5_GROUPS, S5_STATE), dai.reshape(S5_GROUPS, S5_STATE), dbbr, dbbi))
    grads = dict(
        s5_lambda_re=dlr, s5_lambda_im=dli, s5_log_dt=dldt, s5_b_re=dbr, s5_b_im=dbi,
        s5_c_re=jnp.swapaxes(_blockdiag_extract(dwcr, S5_STATE, S5_GROUP_SIZE), 1, 2),
        s5_c_im=jnp.swapaxes(_blockdiag_extract(dwci, S5_STATE, S5_GROUP_SIZE), 1, 2),
        s5_d=dd, s5_glu_w=dglu_w, s5_glu_b=dglu_b)
    return du, dz, grads


def _s5_tables(lam_re, lam_im, log_dt, b_re, b_im, c_re, c_im, d):
    (ar, ai, bbr, bbi), vjp = jax.vjp(_s5_params, lam_re, lam_im, log_dt, b_re, b_im)
    pr, pi = _s5_power_table(lax.stop_gradient(ar), lax.stop_gradient(ai))
    steps = [0, 1, 3, 7, 7, 7, 7, 7]
    flip8 = (jnp.arange(8)[:, None] + jnp.arange(8)[None, :] == 7).astype(F32)
    prm = dict(
        wbr=_bf(_blockdiag(jnp.swapaxes(bbr, 1, 2), S5_GROUP_SIZE, S5_STATE)),
        wbi=_bf(_blockdiag(jnp.swapaxes(bbi, 1, 2), S5_GROUP_SIZE, S5_STATE)),
        wcr=_bf(_blockdiag(jnp.swapaxes(c_re, 1, 2), S5_STATE, S5_GROUP_SIZE)),
        wci=_bf(_blockdiag(jnp.swapaxes(c_im, 1, 2), S5_STATE, S5_GROUP_SIZE)),
        pr=pr, pi=pi, qr=jnp.dot(flip8, pr, precision=lax.Precision.HIGHEST),
        qi=jnp.dot(flip8, pi, precision=lax.Precision.HIGHEST),
        sr=jnp.concatenate([pr[i:i + 1] for i in steps], axis=0),
        si=jnp.concatenate([pi[i:i + 1] for i in steps], axis=0), d=d.reshape(1, S5_WIDTH))
    return prm, vjp


def _tile16(p8):
    return jnp.concatenate([p8] * (CHUNK // 8), axis=0)


def _shift_down(x, halo, s, row):
    return jnp.where(row >= s, pltpu.roll(x, s, 0), pltpu.roll(halo, s, 0))


def _shift_up(x, halo, s, row):
    return jnp.where(row < CHUNK - s, pltpu.roll(x, CHUNK - s, 0), pltpu.roll(halo, CHUNK - s, 0))


def _conv_specs(nc, tw):
    def chunk(b, c):
        return b * nc + c

    return dict(
        x=pl.BlockSpec((CHUNK, tw), lambda j, b, c: (chunk(b, c), j)),
        prev=pl.BlockSpec((8, tw), lambda j, b, c: (jnp.maximum(chunk(b, c) * (CHUNK // 8) - 1, 0), j)),
        nxt=pl.BlockSpec((8, tw), lambda j, b, c: ((b * nc + jnp.minimum(c + 1, nc - 1)) * (CHUNK // 8), j)),
        w=pl.BlockSpec((ML_CONV, tw), lambda j, b, c: (0, j)),
        vec=pl.BlockSpec((1, tw), lambda j, b, c: (0, j)),
    )


def _conv_fwd(x, w, bias, bsz, nc, *, name):
    r, wd = x.shape
    tw = _pick(wd, (2048, 1536, 1024, 512, 384, 256, 128))
    sp = _conv_specs(nc, tw)

    def body(x_ref, p_ref, w_ref, b_ref, o_ref):
        c = pl.program_id(2)
        xv = x_ref[...]
        row = lax.broadcasted_iota(jnp.int32, xv.shape, 0)
        halo = jnp.where(c == 0, 0.0, _tile16(p_ref[...]))
        acc = b_ref[...] + w_ref[3:4, :] * xv
        for s in (1, 2, 3):
            acc = acc + w_ref[3 - s:4 - s, :] * _shift_down(xv, halo, s, row)
        o_ref[...] = acc

    return _pc(body, name=name, grid=(wd // tw, bsz, nc), in_specs=[sp["x"], sp["prev"], sp["w"], sp["vec"]],
               out_specs=sp["x"], out_shape=jax.ShapeDtypeStruct((r, wd), F32), vmem=CHUNK * tw * 16,
               )(x, x, w, bias.reshape(1, wd))


def _conv_bwd(dpre, x, w, bsz, nc, *, name, add=None):
    r, wd = x.shape
    tw = _pick(wd, (2048, 1536, 1024, 512, 384, 256, 128))
    sp = _conv_specs(nc, tw)

    def body(*refs):
        d_ref, n_ref, x_ref, p_ref, w_ref = refs[:5]
        add_ref = refs[5] if add is not None else None
        dx_ref, dw_ref, db_ref = refs[-3:]
        b, c = pl.program_id(1), pl.program_id(2)

        @pl.when((b == 0) & (c == 0))
        def _():
            dw_ref[...] = jnp.zeros_like(dw_ref)
            db_ref[...] = jnp.zeros_like(db_ref)

        dv, xv = d_ref[...], x_ref[...]
        row = lax.broadcasted_iota(jnp.int32, xv.shape, 0)
        dhalo = jnp.where(c == nc - 1, 0.0, _tile16(n_ref[...]))
        xhalo = jnp.where(c == 0, 0.0, _tile16(p_ref[...]))
        dx = w_ref[3:4, :] * dv
        for s in (1, 2, 3):
            dx = dx + w_ref[3 - s:4 - s, :] * _shift_up(dv, dhalo, s, row)
        if add_ref is not None:
            dx = dx + add_ref[...]
        dx_ref[...] = _bf(dx)
        db_ref[...] += jnp.sum(dv, axis=0, keepdims=True)
        dw_ref[3:4, :] += jnp.sum(dv * xv, axis=0, keepdims=True)
        for s in (1, 2, 3):
            dw_ref[3 - s:4 - s, :] += jnp.sum(dv * _shift_down(xv, xhalo, s, row), axis=0, keepdims=True)

    ins = [dpre, dpre, x, x, w] + ([add] if add is not None else [])
    specs = [sp["x"], sp["nxt"], sp["x"], sp["prev"], sp["w"]] + ([sp["x"]] if add is not None else [])
    return _pc(body, name=name, grid=(wd // tw, bsz, nc), in_specs=specs, out_specs=[sp["x"], sp["w"], sp["vec"]],
               out_shape=[jax.ShapeDtypeStruct((r, wd), BF16), jax.ShapeDtypeStruct((ML_CONV, wd), F32),
                          jax.ShapeDtypeStruct((1, wd), F32)], vmem=CHUNK * tw * 24)(*ins)


ML_SCALE = ML_DH ** -0.5


def _headwise_expand(w):
    tiled = jnp.tile(w.reshape(ML_HEADS, ML_DH, QKV_BLOCK), (1, 1, ML_DH // QKV_BLOCK))
    blk = jnp.arange(ML_DH) // QKV_BLOCK
    return jnp.where(blk[:, None] == blk[None, :], tiled, 0.0)


def _headwise_extract(w):
    return w[:, :, :QKV_BLOCK].reshape(ML_HEADS * ML_DH // QKV_BLOCK, QKV_BLOCK, QKV_BLOCK)


def _ml_pre(pre, x, wq, wk, wv, wgq, wgk, wgv, bsz, nc):
    r = x.shape[0]
    tr = _pick(r, (256, 128))
    hrow = pl.BlockSpec((tr, ML_DH), lambda h, i: (i, h))
    wexp = pl.BlockSpec((1, ML_DH, ML_DH), lambda h, i: (h, 0, 0))
    wg = pl.BlockSpec((ML_DH, CHUNK), lambda h, i: (h, 0))

    def body(pre_ref, x_ref, wq_ref, wk_ref, wv_ref, gq_ref, gk_ref, gv_ref, q_ref, qs_ref, k_ref, v_ref, gt_ref):
        xcb = _bf(_silu(pre_ref[...]))
        q = _dot(xcb, wq_ref[0])
        k = _dot(xcb, wk_ref[0])
        v = _dot(_bf(x_ref[...]), wv_ref[0])
        qb, kb, vb = _bf(q), _bf(k), _bf(v)
        q_ref[...] = qb
        qs_ref[...] = _bf(q * ML_SCALE)
        k_ref[...] = kb
        v_ref[...] = vb
        gt_ref[0] = _dot(qb, gq_ref[...]) + _dot(kb, gk_ref[...]) + _dot(vb, gv_ref[...])

    o = jax.ShapeDtypeStruct((r, ML_WIDTH), BF16)
    q, qs, k, v, gates8 = _pc(
        body, name="ml_pre", grid=(ML_HEADS, r // tr),
        in_specs=[hrow, hrow, wexp, wexp, wexp, wg, wg, wg],
        out_specs=[hrow, hrow, hrow, hrow, pl.BlockSpec((1, tr, CHUNK), lambda h, i: (h, i, 0))],
        out_shape=[o, o, o, o, jax.ShapeDtypeStruct((ML_HEADS, r, CHUNK), F32)], vmem=6 << 20,
    )(pre, x, wq, wk, wv, wgq, wgk, wgv)

    def sum_body(g_ref, o_ref):
        acc = g_ref[0]
        for j in range(1, ML_HEADS):
            acc = acc + g_ref[j]
        o_ref[...] = acc

    gates = _pc(sum_body, name="ml_gates_sum", grid=(r // tr,),
                in_specs=[pl.BlockSpec((ML_HEADS, tr, CHUNK), lambda i: (0, i, 0))],
                out_specs=pl.BlockSpec((tr, CHUNK), lambda i: (i, 0)),
                out_shape=jax.ShapeDtypeStruct((r, CHUNK), F32), vmem=2 << 20)(gates8)
    return q, qs, k, v, gates


def _tri(rev):
    r = lax.broadcasted_iota(jnp.int32, (CHUNK, CHUNK), 0)
    c = lax.broadcasted_iota(jnp.int32, (CHUNK, CHUNK), 1)
    return jnp.where((c >= r) if rev else (c <= r), 1.0, 0.0).astype(F32)


def _cumsum_rows(x, row, rev=False):
    for k in range(7):
        s = 1 << k
        if rev:
            x = x + jnp.where(row < CHUNK - s, pltpu.roll(x, CHUNK - s, 0), 0.0)
        else:
            x = x + jnp.where(row >= s, pltpu.roll(x, s, 0), 0.0)
    return x


def _log_sigmoid(x):
    return jnp.minimum(x, 0.0) - jnp.log(1.0 + jnp.exp(-jnp.abs(x)))


def _ml_core(gates, hd, first, m, qs, k, v, cmat, nvec):
    sq = (CHUNK, CHUNK)
    lane = lax.broadcasted_iota(jnp.int32, sq, 1)
    row = lax.broadcasted_iota(jnp.int32, sq, 0)
    igc = jnp.sum(jnp.where(lane == hd, gates, 0.0), axis=1, keepdims=True)
    fpc = jnp.sum(jnp.where(lane == hd + ML_HEADS, gates, 0.0), axis=1, keepdims=True)
    valid = jnp.logical_or(jnp.logical_not(first), row[:, :1] >= PAD_ROWS)
    igc = jnp.where(valid, igc, NEG)
    lfc = jnp.where(valid, _log_sigmoid(fpc), 0.0)
    bcb = _cumsum_rows(jnp.broadcast_to(lfc, sq), row)
    igb = jnp.broadcast_to(igc, sq)
    dm = jnp.where(lane <= row, bcb - (bcb - igb).T, NEG)
    bc = bcb[:, :1]
    inter = bc + m
    mt = jnp.maximum(inter, jnp.max(dm, axis=1, keepdims=True))
    wt = jnp.exp(dm - mt)
    wprev = jnp.exp(inter - mt)
    s0 = _dot(qs, k, NT)
    s = s0 * wt
    cb = _bf(cmat)
    qc = _dot(qs, cb)
    qf = qs.astype(F32)
    qn = jnp.sum(qf * nvec, axis=1, keepdims=True)
    num = _dot(_bf(s), v) + wprev * qc
    den = jnp.sum(s, axis=1, keepdims=True) + wprev * qn
    emt = jnp.exp(-mt)
    dd = jnp.maximum(jnp.abs(den), emt)
    blast = bcb[CHUNK - 1:CHUNK, :1]
    g = blast - bc + igc
    m_new = jnp.maximum(blast + m, jnp.max(g, axis=0, keepdims=True))
    decay = jnp.exp(blast + m - m_new)
    e = jnp.exp(g - m_new)
    kf = k.astype(F32)
    wk = e * kf
    return dict(lane=lane, row=row, fpc=fpc, valid=valid, wt=wt, wprev=wprev, s=s, cb=cb, qc=qc, qf=qf, qn=qn,
                num=num, den=den, emt=emt, dd=dd, m_new=m_new, decay=decay, e=e, kf=kf, wk=wk)


def _ml_headnorm(h):
    mu = jnp.mean(h, axis=1, keepdims=True)
    hc = h - mu
    rstd = lax.rsqrt(jnp.mean(hc * hc, axis=1, keepdims=True) + HEAD_NORM_EPS)
    return hc * rstd, rstd


def _ml_chunk_specs(nc, rev, bsz):
    def cc(c):
        return (nc - 1 - c) if rev else c

    return dict(
        hrow=pl.BlockSpec((bsz, CHUNK, ML_DH), lambda hd, c: (0, cc(c), hd)),
        gates=pl.BlockSpec((bsz, CHUNK, CHUNK), lambda hd, c: (0, cc(c), 0)),
        bias=pl.BlockSpec((1, CHUNK), lambda hd, c: (0, 0)),
        hvec=pl.BlockSpec((1, ML_DH), lambda hd, c: (0, hd)),
        cs=pl.BlockSpec((bsz, 1, ML_DH, ML_DH), lambda hd, c: (0, hd * nc + cc(c), 0, 0)),
        ns=pl.BlockSpec((bsz, 1, 1, ML_DH), lambda hd, c: (0, hd * nc + cc(c), 0, 0)),
        ms=pl.BlockSpec((bsz, 1, 1, CHUNK), lambda hd, c: (0, hd * nc + cc(c), 0, 0)),
        dgates=pl.BlockSpec((1, bsz, CHUNK, CHUNK), lambda hd, c: (hd, 0, cc(c), 0)),
    )


def _seq(a, bsz):
    return a.reshape(bsz, a.shape[0] // bsz, a.shape[1])


def _ml_chunk_fwd(qs, k, v, gates, b_gate, pre, z, nw, sk, bsz, nc):
    r = qs.shape[0]
    tp = r // bsz
    sp = _ml_chunk_specs(nc, False, bsz)

    def body(qs_all, k_all, v_all, gt_all, bg_ref, pre_all, z_all, nw_ref, sk_ref,
             h_all, yb_all, cs_all, ns_all, ms_all, c_sall, n_sall, m_sall):
        hd, c = pl.program_id(0), pl.program_id(1)

        @pl.when(c == 0)
        def _():
            c_sall[...] = jnp.zeros_like(c_sall)
            n_sall[...] = jnp.zeros_like(n_sall)
            m_sall[...] = jnp.zeros_like(m_sall)

        for bi in range(bsz):
            one(hd, c, qs_all.at[bi], k_all.at[bi], v_all.at[bi], gt_all.at[bi], bg_ref, pre_all.at[bi], z_all.at[bi],
                nw_ref, sk_ref, h_all.at[bi], yb_all.at[bi], cs_all.at[bi], ns_all.at[bi], ms_all.at[bi],
                c_sall.at[bi], n_sall.at[bi], m_sall.at[bi])

    def one(hd, c, qs_ref, k_ref, v_ref, gt_ref, bg_ref, pre_ref, z_ref, nw_ref, sk_ref,
            h_ref, yb_ref, cs_ref, ns_ref, ms_ref, c_s, n_s, m_s):
        cmat, nvec, m = c_s[...], n_s[...], m_s[...]
        cs_ref[0] = cmat
        ns_ref[0] = nvec
        ms_ref[0] = jnp.broadcast_to(m, (1, CHUNK))
        v_ = v_ref[...]
        co = _ml_core(gt_ref[...] + bg_ref[...], hd, c == 0, m, qs_ref[...], k_ref[...], v_, cmat, nvec)
        h = co["num"] / co["dd"]
        h_ref[...] = h
        hn, _ = _ml_headnorm(h)
        yb_ref[...] = _bf((hn * nw_ref[...] + sk_ref[...] * _silu(pre_ref[...])) * _silu(z_ref[...]))
        c_s[...] = co["decay"] * cmat + _dot(_bf(co["wk"]), v_, TN)
        n_s[...] = co["decay"] * nvec + jnp.sum(co["wk"], axis=0, keepdims=True)
        m_s[...] = co["m_new"]

    nst = ML_HEADS * nc
    h, yb, cs, ns, ms = _pc(
        body, name="ml_chunk_fwd", grid=(ML_HEADS, nc),
        in_specs=[sp["hrow"]] * 3 + [sp["gates"], sp["bias"], sp["hrow"], sp["hrow"], sp["hvec"], sp["hvec"]],
        out_specs=[sp["hrow"], sp["hrow"], sp["cs"], sp["ns"], sp["ms"]],
        out_shape=[jax.ShapeDtypeStruct((bsz, tp, ML_WIDTH), F32), jax.ShapeDtypeStruct((bsz, tp, ML_WIDTH), BF16),
                   jax.ShapeDtypeStruct((bsz, nst, ML_DH, ML_DH), F32),
                   jax.ShapeDtypeStruct((bsz, nst, 1, ML_DH), F32), jax.ShapeDtypeStruct((bsz, nst, 1, CHUNK), F32)],
        scratch=[pltpu.VMEM((bsz, ML_DH, ML_DH), F32), pltpu.VMEM((bsz, 1, ML_DH), F32),
                 pltpu.VMEM((bsz, 1, 1), F32)],
        vmem=12 << 20)(*[_seq(a, bsz) for a in (qs, k, v, gates)], b_gate, _seq(pre, bsz), _seq(z, bsz), nw, sk)
    return h.reshape(r, ML_WIDTH), yb.reshape(r, ML_WIDTH), cs, ns, ms


def _ml_chunk_bwd(dyb, qs, k, v, gates, b_gate, pre, z, nw, sk, h, cs, ns, ms, bsz, nc, dep=None):
    r = qs.shape[0]
    tp = r // bsz
    sp = _ml_chunk_specs(nc, True, bsz)

    def body(dy_all, qs_all, k_all, v_all, gt_all, bg_ref, pre_all, z_all, nw_ref, sk_ref, h_all, cs_all, ns_all,
             ms_all, dq_all, dk_all, dv_all, dz_all, dxc_all, dgt_all, dnw_ref, dsk_ref, dc_sall, dn_sall):
        hd, c = pl.program_id(0), pl.program_id(1)

        @pl.when(c == 0)
        def _():
            for ref in (dnw_ref, dsk_ref, dc_sall, dn_sall):
                ref[...] = jnp.zeros_like(ref)

        for bi in range(bsz):
            one(hd, c, dy_all.at[bi], qs_all.at[bi], k_all.at[bi], v_all.at[bi], gt_all.at[bi], bg_ref,
                pre_all.at[bi], z_all.at[bi], nw_ref, sk_ref, h_all.at[bi], cs_all.at[bi], ns_all.at[bi],
                ms_all.at[bi], dq_all.at[bi], dk_all.at[bi], dv_all.at[bi], dz_all.at[bi], dxc_all.at[bi],
                dgt_all.at[0, bi], dnw_ref, dsk_ref, dc_sall.at[bi], dn_sall.at[bi])

    def one(hd, c, dy_ref, qs_ref, k_ref, v_ref, gt_ref, bg_ref, pre_ref, z_ref, nw_ref, sk_ref, h_ref, cs_ref, ns_ref,
            ms_ref, dq_ref, dk_ref, dv_ref, dz_ref, dxc_ref, dgt_ref, dnw_ref, dsk_ref, dc_s, dn_s):

        qs, k, v = qs_ref[...], k_ref[...], v_ref[...]
        cmat, nvec, m = cs_ref[0], ns_ref[0], ms_ref[0][:, :1]
        co = _ml_core(gt_ref[...] + bg_ref[...], hd, c == nc - 1, m, qs, k, v, cmat, nvec)
        lane, row = co["lane"], co["row"]
        wt, wprev, s, cb, qf = co["wt"], co["wprev"], co["s"], co["cb"], co["qf"]
        h = h_ref[...]
        hn, rstd = _ml_headnorm(h)
        xc = _silu(pre_ref[...])
        zv = z_ref[...]
        nw, sk = nw_ref[...], sk_ref[...]
        dy = dy_ref[...]
        dz_ref[...] = _bf(dy * (hn * nw + sk * xc) * _dsilu(zv))
        do = dy * _silu(zv)
        dsk_ref[...] += jnp.sum(do * xc, axis=0, keepdims=True)
        dnw_ref[...] += jnp.sum(do * hn, axis=0, keepdims=True)
        dxc_ref[...] = do * sk
        dhn = do * nw
        dh = rstd * (dhn - jnp.mean(dhn, axis=1, keepdims=True) - hn * jnp.mean(dhn * hn, axis=1, keepdims=True))
        rinv = 1.0 / co["dd"]
        dnum = dh * rinv
        ddd = -jnp.sum(dh * h, axis=1, keepdims=True) * rinv
        den = co["den"]
        dden = jnp.where(jnp.abs(den) >= co["emt"], ddd * jnp.sign(den), 0.0)
        dnb = _bf(dnum)
        ds = _dot(dnb, v, NT) + dden
        dv = _dot(_bf(s), dnb, TN)
        dnw_ = _bf(dnum * wprev)
        dwn = dden * wprev
        dqs = _dot(dnw_, cb, NT) + dwn * nvec
        dc_out = _dot(qs, dnw_, TN)
        dn_out = jnp.sum(dwn * qf, axis=0, keepdims=True)
        dwprev = jnp.sum(dnum * co["qc"], axis=1, keepdims=True) + dden * co["qn"]
        ds0 = _bf(ds * wt)
        ddm = ds * s
        dqs = dqs + _dot(ds0, k)
        dk = _dot(ds0, qs, TN)
        colc = jnp.sum(ddm.T, axis=1, keepdims=True)
        dbc = dwprev * wprev + jnp.sum(ddm, axis=1, keepdims=True) - colc
        dig = colc
        dcn, dnn = dc_s[...], dn_s[...]
        dcb = _bf(dcn)
        decay, e, kf, wk = co["decay"], co["e"], co["kf"], co["wk"]
        ddecay = (jnp.sum(jnp.sum(dcn * cmat, axis=1, keepdims=True), axis=0, keepdims=True)
                  + jnp.sum(dnn * nvec, axis=1, keepdims=True))
        dwk = _dot(v, dcb, NT) + dnn
        dv = dv + _dot(_bf(wk), dcb)
        dk = dk + e * dwk
        dg = jnp.sum(dwk * kf, axis=1, keepdims=True) * e
        dblast = ddecay * decay + jnp.sum(dg, axis=0, keepdims=True)
        dbc = dbc - dg + jnp.where(row[:, :1] == CHUNK - 1, dblast, 0.0)
        dig = dig + dg
        dc_s[...] = decay * dcn + dc_out
        dn_s[...] = decay * dnn + dn_out
        dlf = _cumsum_rows(jnp.broadcast_to(dbc, (CHUNK, CHUNK)), row, rev=True)[:, :1]
        dfp = dlf * (1.0 - jax.nn.sigmoid(co["fpc"]))
        dig = jnp.where(co["valid"], dig, 0.0)
        dfp = jnp.where(co["valid"], dfp, 0.0)
        dgt_ref[...] = jnp.where(lane == hd, dig, 0.0) + jnp.where(lane == hd + ML_HEADS, dfp, 0.0)
        dq_ref[...] = _bf(dqs * ML_SCALE)
        dk_ref[...] = _bf(dk)
        dv_ref[...] = _bf(dv)

    ob = jax.ShapeDtypeStruct((bsz, tp, ML_WIDTH), BF16)
    dq, dk, dv, dz, dxc, dgt, dnw, dsk = _pc(
        body, name="ml_chunk_bwd", grid=(ML_HEADS, nc),
        in_specs=[sp["hrow"]] * 4 + [sp["gates"], sp["bias"], sp["hrow"], sp["hrow"], sp["hvec"], sp["hvec"],
                                     sp["hrow"], sp["cs"], sp["ns"], sp["ms"]],
        out_specs=[sp["hrow"]] * 5 + [sp["dgates"], sp["hvec"], sp["hvec"]],
        out_shape=[ob, ob, ob, ob, jax.ShapeDtypeStruct((bsz, tp, ML_WIDTH), F32),
                   jax.ShapeDtypeStruct((ML_HEADS, bsz, tp, CHUNK), F32),
                   jax.ShapeDtypeStruct((1, ML_WIDTH), F32), jax.ShapeDtypeStruct((1, ML_WIDTH), F32)],
        scratch=[pltpu.VMEM((bsz, ML_DH, ML_DH), F32), pltpu.VMEM((bsz, 1, ML_DH), F32)], vmem=16 << 20, dep=dep,
    )(*[_seq(a, bsz) for a in (dyb, qs, k, v, gates)], b_gate, _seq(pre, bsz), _seq(z, bsz), nw, sk, _seq(h, bsz),
      cs, ns, ms)
    return (dq.reshape(r, ML_WIDTH), dk.reshape(r, ML_WIDTH), dv.reshape(r, ML_WIDTH), dz.reshape(r, ML_WIDTH),
            dxc.reshape(r, ML_WIDTH), dgt.reshape(ML_HEADS, r, CHUNK), dnw, dsk)


def _ml_pre_bwd(dq, dk, dv, dgates, dxc_skip, pre, x, q, k, v, wq, wk, wv, wgq, wgk, wgv, bsz, nc):
    r = x.shape[0]
    tr = _pick(r, (256, 128))
    nt = r // tr
    hrow = pl.BlockSpec((tr, ML_DH), lambda h, i: (i, h))
    wexp = pl.BlockSpec((1, ML_DH, ML_DH), lambda h, i: (h, 0, 0))
    wcmp = pl.BlockSpec((1, ML_DH, CHUNK), lambda h, i: (h, 0, 0))
    wg = pl.BlockSpec((ML_DH, CHUNK), lambda h, i: (h, 0))
    dgs = pl.BlockSpec((ML_HEADS, tr, CHUNK), lambda h, i: (0, i, 0))
    bgs = pl.BlockSpec((1, 1, CHUNK), lambda h, i: (h, 0, 0))

    def body(dq_ref, dk_ref, dv_ref, dg_ref, dxs_ref, pre_ref, x_ref, q_ref, k_ref, v_ref, wq_ref, wk_ref, wv_ref,
             gq_ref, gk_ref, gv_ref, dpre_ref, dxv_ref, cq_ref, ck_ref, cv_ref, dgq_ref, dgk_ref, dgv_ref, dbg_ref,
             dwq_ref, dwk_ref, dwv_ref):
        i = pl.program_id(1)

        @pl.when(i == 0)
        def _():
            for ref in (dwq_ref, dwk_ref, dwv_ref, dgq_ref, dgk_ref, dgv_ref, dbg_ref):
                ref[...] = jnp.zeros_like(ref)

        dgt = dg_ref[0]
        for j in range(1, ML_HEADS):
            dgt = dgt + dg_ref[j]
        dbg_ref[0] += jnp.sum(dgt, axis=0, keepdims=True)
        dgb = _bf(dgt)
        dqt = _bf(dq_ref[...].astype(F32) + _dot(dgb, gq_ref[...], NT))
        dkt = _bf(dk_ref[...].astype(F32) + _dot(dgb, gk_ref[...], NT))
        dvt = _bf(dv_ref[...].astype(F32) + _dot(dgb, gv_ref[...], NT))
        dgq_ref[...] += _dot(q_ref[...], dgb, TN)
        dgk_ref[...] += _dot(k_ref[...], dgb, TN)
        dgv_ref[...] += _dot(v_ref[...], dgb, TN)
        prev = pre_ref[...]
        xcb = _bf(_silu(prev))
        xb = _bf(x_ref[...])
        dwq_ref[...] += _dot(xcb, dqt, TN)
        dwk_ref[...] += _dot(xcb, dkt, TN)
        dwv_ref[...] += _dot(xb, dvt, TN)
        dxc = _dot(dqt, wq_ref[0], NT) + _dot(dkt, wk_ref[0], NT) + dxs_ref[...]
        dpre_ref[...] = dxc * _dsilu(prev)
        dxv_ref[...] = _dot(dvt, wv_ref[0], NT)

        @pl.when(i == nt - 1)
        def _():
            rr = lax.broadcasted_iota(jnp.int32, (ML_DH, ML_DH), 0)
            cc = lax.broadcasted_iota(jnp.int32, (ML_DH, ML_DH), 1)
            diag = rr // QKV_BLOCK == cc // QKV_BLOCK
            fc = lax.broadcasted_iota(jnp.int32, (ML_DH, CHUNK), 0)
            fo = lax.broadcasted_iota(jnp.int32, (ML_DH, CHUNK), 1)
            fold = jnp.where(fc % QKV_BLOCK == fo, 1.0, 0.0).astype(F32)
            for src, dst in ((dwq_ref, cq_ref), (dwk_ref, ck_ref), (dwv_ref, cv_ref)):
                dst[0] = jnp.dot(jnp.where(diag, src[...], 0.0), fold, precision=HI, preferred_element_type=F32)

    f = jax.ShapeDtypeStruct((r, ML_WIDTH), F32)
    wc = jax.ShapeDtypeStruct((ML_HEADS, ML_DH, CHUNK), F32)
    wgs = jax.ShapeDtypeStruct((ML_WIDTH, CHUNK), F32)
    return _pc(body, name="ml_pre_bwd", grid=(ML_HEADS, nt),
               in_specs=[hrow, hrow, hrow, dgs, hrow, hrow, hrow, hrow, hrow, hrow, wexp, wexp, wexp, wg, wg, wg],
               out_specs=[hrow, hrow, wcmp, wcmp, wcmp, wg, wg, wg, bgs],
               out_shape=[f, f, wc, wc, wc, wgs, wgs, wgs, jax.ShapeDtypeStruct((ML_HEADS, 1, CHUNK), F32)],
               scratch=[pltpu.VMEM((ML_DH, ML_DH), F32)] * 3,
               vmem=8 << 20)(dq, dk, dv, dgates, dxc_skip, pre, x, q, k, v, wq, wk, wv, wgq, wgk, wgv)


def _pad_lanes(w):
    return jnp.pad(w, ((0, 0), (0, CHUNK - w.shape[1])))


def _ml_weights(conv_w, conv_b, wq, wk, wv, w_gate, b_gate, norm_w, skip):
    return dict(
        conv_w=conv_w, conv_b=conv_b,
        wq=_bf(_headwise_expand(wq)), wk=_bf(_headwise_expand(wk)), wv=_bf(_headwise_expand(wv)),
        wgq=_bf(_pad_lanes(w_gate[:ML_WIDTH])), wgk=_bf(_pad_lanes(w_gate[ML_WIDTH:2 * ML_WIDTH])),
        wgv=_bf(_pad_lanes(w_gate[2 * ML_WIDTH:])), b_gate=_pad_lanes(b_gate.reshape(1, -1)),
        norm=norm_w.reshape(1, ML_WIDTH), skip=skip.reshape(1, ML_WIDTH))


def _ml_layer_fwd(x, z, w, bsz, nc):
    pre = _conv_fwd(x, w["conv_w"], w["conv_b"], bsz, nc, name="ml_conv")
    q, qs, k, v, gates = _ml_pre(pre, x, w["wq"], w["wk"], w["wv"], w["wgq"], w["wgk"], w["wgv"], bsz, nc)
    h, yb, cs, ns, ms = _ml_chunk_fwd(qs, k, v, gates, w["b_gate"], pre, z, w["norm"], w["skip"], bsz, nc)
    return yb, dict(pre=pre, q=q, qs=qs, k=k, v=v, gates=gates, h=h, cs=cs, ns=ns, ms=ms)


def _ml_layer_bwd(dyb, x, z, sv, w, bsz, nc, dep=None):
    dq, dk, dv, dz, dxc, dgates, dnw, dsk = _ml_chunk_bwd(
        dyb, sv["qs"], sv["k"], sv["v"], sv["gates"], w["b_gate"], sv["pre"], z, w["norm"], w["skip"], sv["h"],
        sv["cs"], sv["ns"], sv["ms"], bsz, nc, dep=dep)
    dpre, dxv, dwq, dwk, dwv, dgq, dgk, dgv, dbg = _ml_pre_bwd(
        dq, dk, dv, dgates, dxc, sv["pre"], x, sv["q"], sv["k"], sv["v"], w["wq"], w["wk"], w["wv"], w["wgq"],
        w["wgk"], w["wgv"], bsz, nc)
    dx, dcw, dcb = _conv_bwd(dpre, x, w["conv_w"], bsz, nc, name="ml_conv_bwd", add=dxv)
    ng = 2 * ML_HEADS
    grads = dict(
        ml_conv_w=dcw, ml_conv_b=dcb, ml_wq=_headwise_extract(dwq), ml_wk=_headwise_extract(dwk),
        ml_wv=_headwise_extract(dwv), ml_w_gate=jnp.concatenate([dgq[:, :ng], dgk[:, :ng], dgv[:, :ng]], axis=0),
        ml_b_gate=dbg[0][:, :ng], ml_norm=dnw, ml_skip=dsk)
    return dx, dz, grads


HI = lax.Precision.HIGHEST


def _softplus(x):
    return jnp.maximum(x, 0.0) + jnp.log(1.0 + jnp.exp(-jnp.abs(x)))


def _lane_cumsum(x, lane, rev=False):
    del lane
    return _dot_terms(x, _tri(not rev), NN, exact_rhs=True, terms=3)


def _dot_terms(lhs, rhs, dims, *, exact_rhs, terms):
    x = lhs if exact_rhs else rhs
    sel = _bf(rhs if exact_rhs else lhs)
    acc = None
    for _ in range(terms):
        piece = _bf(x)
        part = _dot(piece, sel, dims) if exact_rhs else _dot(sel, piece, dims)
        acc = part if acc is None else acc + part
        x = x - piece.astype(F32)
    return acc


def _head_sum_matrix():
    r = lax.broadcasted_iota(jnp.int32, (SSD_HPG, SSD_GW), 0)
    l = lax.broadcasted_iota(jnp.int32, (SSD_HPG, SSD_GW), 1)
    return jnp.where(l // SSD_P == r, 1.0, 0.0).astype(F32)


def _ssd_dt_specs(nc):
    return dict(rows=pl.BlockSpec((1, SSD_HEADS, CHUNK), lambda b, c: (b, 0, c)),
                col=pl.BlockSpec((SSD_HEADS, 1), lambda b, c: (0, 0)),
                acc=pl.BlockSpec((SSD_HEADS, CHUNK), lambda b, c: (0, 0)))


def _ssd_dt_valid(c):
    lane = lax.broadcasted_iota(jnp.int32, (SSD_HEADS, CHUNK), 1)
    return jnp.logical_or(c > 0, lane >= PAD_ROWS)


def _ssd_dt_prep(dt_raw, dt_bias, a_log, bsz, nc):
    sp = _ssd_dt_specs(nc)

    def body(raw_ref, b_ref, al_ref, dt_ref, cum_ref):
        dt = jnp.where(_ssd_dt_valid(pl.program_id(1)), _softplus(raw_ref[0] + b_ref[...]), 0.0)
        dt_ref[0] = dt
        cum_ref[0] = _lane_cumsum(dt * -jnp.exp(al_ref[...]), None)

    o = jax.ShapeDtypeStruct(dt_raw.shape, F32)
    return _pc(body, name="ssd_dt_prep", grid=(bsz, nc), in_specs=[sp["rows"], sp["col"], sp["col"]],
               out_specs=[sp["rows"], sp["rows"]], out_shape=[o, o], vmem=1 << 20)(dt_raw, dt_bias, a_log)


def _ssd_dt_post(dcum, ddt, dt_raw, dt_bias, a_log, bsz, nc):
    sp = _ssd_dt_specs(nc)

    def body(dcum_ref, ddt_ref, raw_ref, b_ref, al_ref, out_ref, dbias_ref, dal_ref):
        b, c = pl.program_id(0), pl.program_id(1)

        @pl.when((b == 0) & (c == 0))
        def _():
            dbias_ref[...] = jnp.zeros_like(dbias_ref)
            dal_ref[...] = jnp.zeros_like(dal_ref)

        valid = _ssd_dt_valid(c)
        pre = raw_ref[0] + b_ref[...]
        dt = jnp.where(valid, _softplus(pre), 0.0)
        a = -jnp.exp(al_ref[...])
        dda = _lane_cumsum(dcum_ref[0], None, rev=True)
        ddt_raw = jnp.where(valid, ddt_ref[0] + dda * a, 0.0) * jax.nn.sigmoid(pre)
        out_ref[0] = ddt_raw
        dbias_ref[...] += jnp.sum(ddt_raw, axis=1, keepdims=True)
        dal_ref[...] += jnp.sum(dda * dt, axis=1, keepdims=True) * a

    acc = jax.ShapeDtypeStruct((SSD_HEADS, CHUNK), F32)
    return _pc(body, name="ssd_dt_post", grid=(bsz, nc),
               in_specs=[sp["rows"], sp["rows"], sp["rows"], sp["col"], sp["col"]],
               out_specs=[sp["rows"], sp["acc"], sp["acc"]],
               out_shape=[jax.ShapeDtypeStruct(dt_raw.shape, F32), acc, acc], vmem=1 << 20,
               )(dcum, ddt, dt_raw, dt_bias, a_log)


def _ssd_core(xs, bm, cm, dt, cum):
    sq = (CHUNK, CHUNK)
    lane8 = lax.broadcasted_iota(jnp.int32, (SSD_HPG, CHUNK), 1)
    lane = lax.broadcasted_iota(jnp.int32, sq, 1)
    row = lax.broadcasted_iota(jnp.int32, sq, 0)
    low = lane < SSD_P
    cb = _dot(_bf(cm), _bf(bm), NT)
    heads = []
    for r in range(SSD_HPG):
        rowb = jnp.broadcast_to(cum[r:r + 1, :], sq)
        colb = rowb.T
        seg = jnp.exp(jnp.where(lane <= row, colb - rowb, NEG))
        dtrow = jnp.broadcast_to(dt[r:r + 1, :], sq)
        lastb = colb[CHUNK - 1:CHUNK, :]
        heads.append(dict(seg=seg, dtrow=dtrow, w=cb * seg * dtrow, ecol=jnp.exp(colb),
                          dec=jnp.exp(lastb - colb) * dtrow.T, elast=jnp.exp(lastb)))

    def pairs(key):
        return jnp.concatenate([jnp.where(low[:heads[0][key].shape[0]], heads[2 * j][key], heads[2 * j + 1][key])
                                for j in range(SSD_HPG // 2)], axis=1)

    return dict(lane8=lane8, low=low, dt=dt, cum=cum, cb=cb, heads=heads,
                expc=pairs("ecol"), dec=pairs("dec"), elast=pairs("elast"))


def _ssd_specs(nc, rev, bsz):
    def cc(c):
        return (nc - 1 - c) if rev else c

    return dict(
        wide=pl.BlockSpec((bsz, CHUNK, SSD_GW), lambda g, c: (0, cc(c), g)),
        narrow=pl.BlockSpec((bsz, CHUNK, SSD_N), lambda g, c: (0, cc(c), g)),
        dtT=pl.BlockSpec((bsz, SSD_HPG, CHUNK), lambda g, c: (0, g, cc(c))),
        hcol=pl.BlockSpec((SSD_HPG, 1), lambda g, c: (g, 0)),
        hacc=pl.BlockSpec((SSD_HPG, CHUNK), lambda g, c: (g, 0)),
        gvec=pl.BlockSpec((1, SSD_GW), lambda g, c: (0, g)),
        state=pl.BlockSpec((bsz, 1, SSD_N, SSD_GW), lambda g, c: (0, g * nc + cc(c), 0, 0)),
    )


def _ssd_chunk_fwd(xs_pre, bm_pre, cm_pre, dt, cum, d_exp, z, gnorm, bsz, nc):
    tp = xs_pre.shape[1]
    sp = _ssd_specs(nc, False, bsz)

    def body(xs_all, bm_all, cm_all, dt_all, cum_all, d_ref, z_all, gn_ref, y_all, yn_all, st_all, st_sall):
        @pl.when(pl.program_id(1) == 0)
        def _():
            st_sall[...] = jnp.zeros_like(st_sall)

        for bi in range(bsz):
            one(xs_all.at[bi], bm_all.at[bi], cm_all.at[bi], dt_all.at[bi], cum_all.at[bi], d_ref, z_all.at[bi],
                gn_ref, y_all.at[bi], yn_all.at[bi], st_all.at[bi], st_sall.at[bi])

    def one(xs_ref, bm_ref, cm_ref, dt_ref, cum_ref, d_ref, z_ref, gn_ref, y_ref, yn_ref, st_ref, st_s):
        state = st_s[...]
        st_ref[0] = state
        xs, bm, cm = _silu(xs_ref[...]), _silu(bm_ref[...]), _silu(cm_ref[...])
        co = _ssd_core(xs, bm, cm, dt_ref[...], cum_ref[...])
        low, hd = co["low"], co["heads"]
        ys = []
        for j in range(SSD_HPG // 2):
            xp = xs[:, j * CHUNK:(j + 1) * CHUNK]
            lhs = jnp.concatenate([hd[2 * j]["w"], hd[2 * j + 1]["w"]], axis=1)
            rhs = jnp.concatenate([jnp.where(low, xp, 0.0), jnp.where(low, 0.0, xp)], axis=0)
            ys.append(_dot(_bf(lhs), _bf(rhs)))
        cmb = _bf(cm)
        y = jnp.concatenate(ys, axis=1) + co["expc"] * _dot(cmb, _bf(state)) + d_ref[...] * xs
        y_ref[...] = y
        yg = y * _silu(z_ref[...])
        rstd = lax.rsqrt(jnp.mean(yg * yg, axis=1, keepdims=True) + NORM_EPS)
        yn_ref[...] = _bf(yg * rstd * gn_ref[...])
        st_s[...] = co["elast"] * state + _dot(_bf(bm), _bf(xs * co["dec"]), TN)

    return _pc(body, name="ssd_chunk_fwd", grid=(SSD_GROUPS, nc),
               in_specs=[sp["wide"], sp["narrow"], sp["narrow"], sp["dtT"], sp["dtT"], sp["gvec"], sp["wide"],
                         sp["gvec"]],
               out_specs=[sp["wide"], sp["wide"], sp["state"]],
               out_shape=[jax.ShapeDtypeStruct((bsz, tp, SSD_INNER), F32),
                          jax.ShapeDtypeStruct((bsz, tp, SSD_INNER), BF16),
                          jax.ShapeDtypeStruct((bsz, SSD_GROUPS * nc, SSD_N, SSD_GW), F32)],
               scratch=[pltpu.VMEM((bsz, SSD_N, SSD_GW), F32)], vmem=12 << 20,
               )(xs_pre, bm_pre, cm_pre, dt, cum, d_exp, z, gnorm)


def _ssd_chunk_bwd(dyn, xs_pre, bm_pre, cm_pre, dt, cum, d_exp, z, gnorm, y, states, bsz, nc):
    tp = xs_pre.shape[1]
    sp = _ssd_specs(nc, True, bsz)

    def body(dyn_all, xs_all, bm_all, cm_all, dt_all, cum_all, d_ref, z_all, gn_ref, y_all, st_all,
             dxs_all, dbm_all, dcm_all, dz_all, dcum_all, ddt_all, dgn_ref, dd_ref, ds_sall):
        @pl.when(pl.program_id(1) == 0)
        def _():
            for ref in (dgn_ref, dd_ref, ds_sall):
                ref[...] = jnp.zeros_like(ref)

        for bi in range(bsz):
            one(dyn_all.at[bi], xs_all.at[bi], bm_all.at[bi], cm_all.at[bi], dt_all.at[bi], cum_all.at[bi], d_ref,
                z_all.at[bi], gn_ref, y_all.at[bi], st_all.at[bi], dxs_all.at[bi], dbm_all.at[bi], dcm_all.at[bi],
                dz_all.at[bi], dcum_all.at[bi], ddt_all.at[bi], dgn_ref, dd_ref, ds_sall.at[bi])

    def one(dyn_ref, xs_ref, bm_ref, cm_ref, dt_ref, cum_ref, d_ref, z_ref, gn_ref, y_ref, st_ref,
            dxs_ref, dbm_ref, dcm_ref, dz_ref, dcum_ref, ddt_ref, dgn_ref, dd_ref, ds_s):
        xs_p, bm_p, cm_p = xs_ref[...], bm_ref[...], cm_ref[...]
        xs, bm, cm = _silu(xs_p), _silu(bm_p), _silu(cm_p)
        state = st_ref[0]
        co = _ssd_core(xs, bm, cm, dt_ref[...], cum_ref[...])
        low, hd, lane8, cb = co["low"], co["heads"], co["lane8"], co["cb"]
        dt, cum = co["dt"], co["cum"]
        sub8 = lax.broadcasted_iota(jnp.int32, (SSD_HPG, CHUNK), 0)
        eh = _head_sum_matrix()

        def head_rows(full):
            return _dot_terms(eh, full, NT, exact_rhs=False, terms=2)

        def head_col(vec):
            return jnp.sum(eh * vec, axis=1, keepdims=True)

        yv, zv, gn = y_ref[...], z_ref[...], gn_ref[...]
        sz = _silu(zv)
        yg = yv * sz
        rstd = lax.rsqrt(jnp.mean(yg * yg, axis=1, keepdims=True) + NORM_EPS)
        yh = yg * rstd
        dyn = dyn_ref[...]
        dgn_ref[...] += jnp.sum(dyn * yh, axis=0, keepdims=True)
        dyh = dyn * gn
        dyg = rstd * (dyh - yh * jnp.mean(dyh * yh, axis=1, keepdims=True))
        dz_ref[...] = _bf(dyg * yv * _dsilu(zv))
        dy = dyg * sz
        dxs = dy * d_ref[...]
        dd_ref[...] += head_col(jnp.sum(dy * xs, axis=0, keepdims=True))
        cmb, bmb, stb = _bf(cm), _bf(bm), _bf(state)
        ysv = _dot(cmb, stb)
        expc = co["expc"]
        dys = _bf(dy * expc)
        dcum = head_rows(dy * ysv * expc)
        dcm = _dot(dys, stb, NT)
        dstate_out = _dot(cmb, dys, TN)
        dcb = jnp.zeros((CHUNK, CHUNK), F32)
        ddt = jnp.zeros((SSD_HPG, CHUNK), F32)
        dxs_pairs = []
        for j in range(SSD_HPG // 2):
            sl = slice(j * CHUNK, (j + 1) * CHUNK)
            dyp, xp = dy[:, sl], _bf(xs[:, sl])
            lhs = _bf(jnp.concatenate([hd[2 * j]["w"], hd[2 * j + 1]["w"]], axis=1))
            both = _dot(lhs, _bf(dyp), TN)
            dxs_pairs.append(jnp.where(low, both[:CHUNK], both[CHUNK:]))
            for q, msk in ((2 * j, low), (2 * j + 1, jnp.logical_not(low))):
                h = hd[q]
                dw = _dot(_bf(jnp.where(msk, dyp, 0.0)), xp, NT)
                dcb = dcb + dw * h["seg"] * h["dtrow"]
                e_ = dw * h["w"]
                dcum_r = jnp.sum(e_.T, axis=0, keepdims=True) - jnp.sum(e_, axis=0, keepdims=True)
                ddt_r = jnp.sum(dw * cb * h["seg"], axis=0, keepdims=True)
                dcum = dcum + jnp.where(sub8 == q, dcum_r, 0.0)
                ddt = ddt + jnp.where(sub8 == q, ddt_r, 0.0)
        dxs = dxs + jnp.concatenate(dxs_pairs, axis=1)
        dcbb = _bf(dcb)
        dcm = dcm + _dot(dcbb, bmb)
        dbm = _dot(dcbb, cmb, TN)
        dsn = ds_s[...]
        dsb = _bf(dsn)
        dec = co["dec"]
        dbm = dbm + _dot(_bf(xs * dec), dsb, NT)
        dxd = _dot(bmb, dsb)
        dxs = dxs + dxd * dec
        ddec = head_rows(dxd * xs)
        last = cum[:, CHUNK - 1:CHUNK]
        erow = jnp.exp(last - cum)
        ddt = ddt + ddec * erow
        dla = ddec * erow * dt
        dlast = (jnp.sum(dla, axis=1, keepdims=True)
                 + head_col(jnp.sum(dsn * state, axis=0, keepdims=True)) * jnp.exp(last))
        dcum_ref[...] = dcum - dla + jnp.where(lane8 == CHUNK - 1, dlast, 0.0)
        ddt_ref[...] = ddt
        ds_s[...] = co["elast"] * dsn + dstate_out
        dxs_ref[...] = dxs * _dsilu(xs_p)
        dbm_ref[...] = dbm * _dsilu(bm_p)
        dcm_ref[...] = dcm * _dsilu(cm_p)

    st = jax.ShapeDtypeStruct
    hacc = st((SSD_HEADS, CHUNK), F32)
    return _pc(body, name="ssd_chunk_bwd", grid=(SSD_GROUPS, nc),
               in_specs=[sp["wide"], sp["wide"], sp["narrow"], sp["narrow"], sp["dtT"], sp["dtT"], sp["gvec"],
                         sp["wide"], sp["gvec"], sp["wide"], sp["state"]],
               out_specs=[sp["wide"], sp["narrow"], sp["narrow"], sp["wide"], sp["dtT"], sp["dtT"], sp["gvec"],
                          sp["hacc"]],
               out_shape=[st((bsz, tp, SSD_INNER), F32), st((bsz, tp, SSD_BC), F32), st((bsz, tp, SSD_BC), F32),
                          st((bsz, tp, SSD_INNER), BF16), st((bsz, SSD_HEADS, tp), F32),
                          st((bsz, SSD_HEADS, tp), F32), st((1, SSD_INNER), F32), hacc],
               scratch=[pltpu.VMEM((bsz, SSD_N, SSD_GW), F32)], vmem=20 << 20,
               )(dyn, xs_pre, bm_pre, cm_pre, dt, cum, d_exp, z, gnorm, y, states)


SSD_BC = SSD_GROUPS * SSD_N


def _ssd_weights(conv_w, conv_b, dt_bias, a_log, d, gnorm):
    cuts = (0, SSD_INNER, SSD_INNER + SSD_BC, SSD_INNER + 2 * SSD_BC)
    return dict(
        conv_w=[conv_w[:, cuts[i]:cuts[i + 1]] for i in range(3)],
        conv_b=[conv_b[cuts[i]:cuts[i + 1]] for i in range(3)],
        dt_bias=dt_bias.reshape(SSD_HEADS, 1), a_log=a_log.reshape(SSD_HEADS, 1),
        d_exp=jnp.repeat(d.reshape(SSD_HEADS), SSD_P).reshape(1, SSD_INNER), gnorm=gnorm.reshape(1, SSD_INNER))


def _ssd_layer_fwd(z, xs_in, bm_in, cm_in, dt_rows, w, bsz, nc):
    pres = [_conv_fwd(a, w["conv_w"][i], w["conv_b"][i], bsz, nc, name=f"ssd_conv{i}")
            for i, a in enumerate((xs_in, bm_in, cm_in))]
    def seq(a):
        return a.reshape(bsz, nc * CHUNK, a.shape[-1])

    dt_t = jnp.swapaxes(seq(dt_rows)[:, :, :SSD_HEADS], 1, 2)
    dt, cum = _ssd_dt_prep(dt_t, w["dt_bias"], w["a_log"], bsz, nc)
    y, yn, states = _ssd_chunk_fwd(seq(pres[0]), seq(pres[1]), seq(pres[2]), dt, cum, w["d_exp"], seq(z),
                                   w["gnorm"], bsz, nc)
    return yn.reshape(-1, SSD_INNER), dict(pres=pres, dt_t=dt_t, dt=dt, cum=cum, y=y, states=states)


def _ssd_layer_bwd(dyn, z, xs_in, bm_in, cm_in, sv, w, bsz, nc):
    pres = sv["pres"]

    def seq(a):
        return a.reshape(bsz, nc * CHUNK, a.shape[-1])

    def rows(a):
        return a.reshape(-1, a.shape[-1])

    dxs_p, dbm_p, dcm_p, dz, dcum, ddt_direct, dgn, dd = _ssd_chunk_bwd(
        seq(dyn), seq(pres[0]), seq(pres[1]), seq(pres[2]), sv["dt"], sv["cum"], w["d_exp"], seq(z), w["gnorm"],
        sv["y"], sv["states"], bsz, nc)
    ddt_t, dbias, dal = _ssd_dt_post(dcum, ddt_direct, sv["dt_t"], w["dt_bias"], w["a_log"], bsz, nc)
    dz = rows(dz)
    outs = [_conv_bwd(rows(dp), a, w["conv_w"][i], bsz, nc, name=f"ssd_conv_bwd{i}")
            for i, (dp, a) in enumerate(((dxs_p, xs_in), (dbm_p, bm_in), (dcm_p, cm_in)))]
    ddt = _bf(_pad_lanes(rows(jnp.swapaxes(ddt_t, 1, 2))))
    grads = dict(
        ssd_conv_w=jnp.concatenate([o[1] for o in outs], axis=1),
        ssd_conv_b=jnp.concatenate([o[2] for o in outs], axis=1),
        ssd_dt_bias=dbias[:, 0], ssd_a_log=dal[:, 0], ssd_d=dd[:, 0], ssd_gnorm=dgn)
    return dz, outs[0][0], outs[1][0], outs[2][0], ddt, grads


WNAMES = ("meta_tokens", "ab_norm", "ab_w_in", "s5_lambda_re", "s5_lambda_im", "s5_log_dt", "s5_b_re", "s5_b_im",
          "s5_c_re", "s5_c_im", "s5_d", "s5_glu_w", "s5_glu_b", "ml_conv_w", "ml_conv_b", "ml_wq", "ml_wk", "ml_wv",
          "ml_w_gate", "ml_b_gate", "ml_norm", "ml_skip", "ab_w_out", "ssd_norm", "ssd_w_in", "ssd_conv_w",
          "ssd_conv_b", "ssd_dt_bias", "ssd_a_log", "ssd_d", "ssd_gnorm", "ssd_w_out", "final_norm")
SHARD_AXIS = dict(meta_tokens=1, ab_w_in=2, s5_glu_w=1, ml_conv_w=2, ml_wq=1, ml_wk=1, ml_wv=1, ml_w_gate=1,
                  ab_w_out=1, ssd_norm=1, ssd_w_in=2, ssd_conv_w=2, ssd_conv_b=1, ssd_gnorm=1, ssd_w_out=1)
BIG = ("ab_w_in", "s5_glu_w", "ab_w_out", "ssd_w_in", "ssd_w_out")
SMALL = tuple(n for n in WNAMES if n in SHARD_AXIS and n not in BIG)
REPL = tuple(n for n in WNAMES if n not in SHARD_AXIS)
PACK_ALIGN = 8 * 128


def _pack(arrs):
    lead = arrs[0][1]
    parts = []
    for a, nlead in arrs:
        f = a.reshape(a.shape[:nlead] + (-1,))
        parts.append(jnp.pad(f, [(0, 0)] * nlead + [(0, (-f.shape[-1]) % PACK_ALIGN)]))
    flat = jnp.concatenate(parts, axis=lead)
    return flat.reshape(flat.shape[:lead] + (-1, 128))


def _unpack(p, shapes):
    out, off = [], 0
    lead = p.shape[:-2]
    flat = p.reshape(lead + (-1,))
    for s in shapes:
        n = math.prod(s)
        out.append(flat[..., off:off + n].reshape(lead + tuple(s)))
        off += -(-n // PACK_ALIGN) * PACK_ALIGN
    return out


def _assemble(g, axis):
    m = jnp.moveaxis(g, 0, axis)
    return m.reshape(m.shape[:axis] + (m.shape[axis] * m.shape[axis + 1],) + m.shape[axis + 2:])


def _split(full, axis):
    s = full.shape
    m = full.reshape(s[:axis] + (N_DEV, s[axis] // N_DEV) + s[axis + 1:])
    return jnp.moveaxis(m, axis, 0)


def kernel(x, *rest):
    nw = len(WNAMES)
    w = dict(zip(WNAMES, rest[:nw]))
    loss_target = rest[nw]
    mom = dict(zip(WNAMES, rest[nw + 1:2 * nw + 1]))
    var = dict(zip(WNAMES, rest[2 * nw + 1:3 * nw + 1]))
    bsz = x.shape[0]
    nc = 1 + SEQ // CHUNK
    tp = nc * CHUNK

    local = {n: _bf(w[n][0]) for n in BIG}
    small_local = _pack([(w[n], 0) for n in SMALL])
    gs = _exchange_start([small_local], ["ag"], name="gather_s")
    ga = _exchange_start([local["ab_w_in"]], ["ag"], name="gather_a", dep=gs["token"], peers=SAME_CORE[1:])
    got_s = _exchange_wait(gs, ga["token"])

    def assemble_big(n, got):
        return _assemble(got[:, None], SHARD_AXIS[n])[0]

    full = {}
    for n, g in zip(SMALL, _unpack(got_s[0], [w[n].shape for n in SMALL])):
        full[n] = _assemble(g, SHARD_AXIS[n])[0] if n != "meta_tokens" else _assemble(g, SHARD_AXIS[n])
    for n in REPL:
        full[n] = w[n][0] if n != "final_norm" else w[n]
    glu_b = full["s5_glu_b"].reshape(1, S5_WIDTH)
    meta = jnp.broadcast_to(full["meta_tokens"][None], (bsz, N_META, D_MODEL))
    h0 = jnp.concatenate([jnp.zeros((bsz, PAD_ROWS, D_MODEL), F32), meta, x], axis=1).reshape(bsz * tp, D_MODEL)
    xn0 = _rms_fwd(h0, full["ab_norm"], name="rms0")
    s5p, s5_vjp = _s5_tables(*[full[n] for n in ("s5_lambda_re", "s5_lambda_im", "s5_log_dt", "s5_b_re", "s5_b_im",
                                                   "s5_c_re", "s5_c_im", "s5_d")])
    mlw = _ml_weights(*[full[n] for n in ("ml_conv_w", "ml_conv_b", "ml_wq", "ml_wk", "ml_wv", "ml_w_gate",
                                           "ml_b_gate", "ml_norm", "ml_skip")])
    got_a = _exchange_wait(ga, [xn0, s5p["wbr"], s5p["wcr"], s5p["pr"], mlw["wq"], mlw["wk"], mlw["wv"], mlw["wgq"]])
    fwd_a = _sibling_forward_start(got_a[0], name="gather_a2")
    got_a = [_sibling_forward_wait(fwd_a, fwd_a["token"])]
    gb = _exchange_start([local["s5_glu_w"], local["ab_w_out"]], ["ag", "ag"], name="gather_b", dep=got_a[0])
    gc = _exchange_start([local["ssd_w_in"], local["ssd_w_out"]], ["ag", "ag"], name="gather_c", dep=gb["token"])
    full["ab_w_in"] = assemble_big("ab_w_in", got_a[0])
    cuts0 = (0, S5_WIDTH, 2 * S5_WIDTH, 2 * S5_WIDTH + ML_WIDTH, 2 * (S5_WIDTH + ML_WIDTH))
    w_in0 = [full["ab_w_in"][:, cuts0[i]:cuts0[i + 1]] for i in range(4)]

    u, za, xb, zb = [_mm(xn0, wi, "NN", name=f"in0_{i}") for i, wi in enumerate(w_in0)]
    got_b = []

    def glu_w_after(scan_out):
        got_b.extend(_exchange_wait(gb, scan_out))
        return assemble_big("s5_glu_w", got_b[0])

    sv5 = _s5_layer_fwd(u, s5p, glu_w_after, bsz, nc)
    glu_w = assemble_big("s5_glu_w", got_b[0])
    w_out0 = assemble_big("ab_w_out", got_b[1])
    w_out0 = [w_out0[:S5_WIDTH], w_out0[S5_WIDTH:]]
    ya = _s5_post(sv5["y1"], sv5["glu_pre"], glu_b, za)
    yb, svm = _ml_layer_fwd(xb, zb, mlw, bsz, nc)
    h1 = _mm(ya, w_out0[0], "NN", name="out0_a", add=h0)
    h1 = _mm(yb, w_out0[1], "NN", name="out0_b", add=h1)
    got_c = _exchange_wait(gc, h1)
    w_in1, w_out1 = assemble_big("ssd_w_in", got_c[0]), assemble_big("ssd_w_out", got_c[1])
    cuts1 = (0, SSD_INNER, 2 * SSD_INNER, 2 * SSD_INNER + SSD_BC, 2 * SSD_INNER + 2 * SSD_BC)
    w_in1 = [w_in1[:, cuts1[i]:cuts1[i + 1]] for i in range(4)] + [_pad_lanes(w_in1[:, cuts1[4]:])]
    xn1 = _rms_fwd(h1, full["ssd_norm"], name="rms1")
    z1, xs_in, bm_in, cm_in, dt_rows = [_mm(xn1, wi, "NN", name=f"in1_{i}") for i, wi in enumerate(w_in1)]
    ssdw = _ssd_weights(*[full[n] for n in ("ssd_conv_w", "ssd_conv_b", "ssd_dt_bias", "ssd_a_log", "ssd_d",
                                             "ssd_gnorm")])
    yn, svs = _ssd_layer_fwd(z1, xs_in, bm_in, cm_in, dt_rows, ssdw, bsz, nc)
    h2 = _mm(yn, w_out1, "NN", name="out1", add=h1)
    loss_part, dh2, dfinal, dh2_b = _final_loss(h2, full["final_norm"], loss_target, bsz, nc)

    g = {"final_norm": dfinal}
    dyn = _mm(dh2_b, w_out1, "NT", name="d_out1")
    g["ssd_w_out"] = _mm(yn, dh2_b, "TN", name="dw_out1", out_dtype=BF16)
    dz1, dxs, dbm, dcm, ddt, gs = _ssd_layer_bwd(dyn, z1, xs_in, bm_in, cm_in, svs, ssdw, bsz, nc)
    g.update(gs)
    dps1 = (dz1, dxs, dbm, dcm, ddt)
    dxn1 = None
    for i, (dp, wi) in enumerate(zip(dps1, w_in1)):
        dxn1 = _mm(dp, wi, "NT", name=f"d_in1_{i}", add=dxn1)
    dw1 = [_mm(xn1, dp, "TN", name=f"dw_in1_{i}", out_dtype=BF16) for i, dp in enumerate(dps1)]
    g["ssd_w_in"] = jnp.concatenate(dw1[:4] + [dw1[4][:, :SSD_HEADS]], axis=1)

    def local_shape(n):
        return w[n].shape

    def slabs(n):
        gf = g[n].reshape((1,) + tuple(g[n].shape)) if n != "meta_tokens" else g[n]
        full_shape = tuple(d * (N_DEV if i == SHARD_AXIS[n] else 1) for i, d in enumerate(local_shape(n)))
        return _split(gf.reshape(full_shape), SHARD_AXIS[n])

    x1 = _exchange_start([slabs("ssd_w_in")[:, 0], slabs("ssd_w_out")[:, 0]], ["a2a", "a2a"], name="grads_1")
    dh1, g["ssd_norm"], dh1_b = _rms_bwd(h1, full["ssd_norm"], dxn1, dh2, name="rms1_bwd", dep=x1["token"])
    dya = _mm(dh1_b, w_out0[0], "NT", name="d_out0_a")
    dyb = _mm(dh1_b, w_out0[1], "NT", name="d_out0_b")
    g["ab_w_out"] = jnp.concatenate([_mm(ya, dh1_b, "TN", name="dw_out0_a", out_dtype=BF16),
                                     _mm(yb, dh1_b, "TN", name="dw_out0_b", out_dtype=BF16)], axis=0)
    du, dza, g5 = _s5_layer_bwd(dya, u, za, sv5, s5p, s5_vjp, glu_w, glu_b, bsz, nc)
    g.update(g5)
    x2 = _exchange_start([slabs("ab_w_out")[:, 0], _bf(slabs("s5_glu_w")[:, 0])], ["a2a", "a2a"], name="grads_2")
    dxb, dzb, gm = _ml_layer_bwd(dyb, xb, zb, svm, mlw, bsz, nc, dep=x2["token"])
    g.update(gm)
    dps0 = (du, dza, dxb, dzb)
    dw0 = [_mm(xn0, dp, "TN", name=f"dw_in0_{i}", out_dtype=BF16, tn=S5_WIDTH, slabs=True) for i, dp in enumerate(dps0)]
    dw_in0_slabs = jnp.concatenate(dw0, axis=0)
    x3 = _exchange_start([dw_in0_slabs], ["a2a"], name="grads_3")
    dxn0 = None
    for i, (dp, wi) in enumerate(zip(dps0, w_in0)):
        dxn0 = _mm(dp, wi, "NT", name=f"d_in0_{i}", add=dxn0, dep=x3["token"] if i == 0 else None)
    grad_x, d_chunk0, g["ab_norm"] = _rms_bwd_first(h0, full["ab_norm"], dxn0, dh1, bsz, nc, name="rms0_bwd")
    g["meta_tokens"] = jnp.sum(d_chunk0[:, PAD_ROWS:], axis=0)

    small_g = _pack([(slabs(n), 1) for n in SMALL])
    repl_g = _pack([(g[n], 0) for n in REPL])
    x4 = _exchange_start([small_g, repl_g, loss_part], ["a2a", "ag", "ag"], name="grads_4")

    def update_big(n, gp):
        return _adamw(w[n][0], mom[n][0], var[n][0], gp, name=f"adamw_{n}")

    res = {}
    ex1 = _exchange_wait(x1, x4["token"])
    res["ssd_w_in"], res["ssd_w_out"] = update_big("ssd_w_in", ex1[0]), update_big("ssd_w_out", ex1[1])
    ex2 = _exchange_wait(x2, res["ssd_w_out"][0])
    res["ab_w_out"], res["s5_glu_w"] = update_big("ab_w_out", ex2[0]), update_big("s5_glu_w", ex2[1])
    ex3 = _exchange_wait(x3, [res[n][0] for n in ("ssd_w_in", "ssd_w_out", "ab_w_out", "s5_glu_w")])
    res["ab_w_in"] = update_big("ab_w_in", ex3[0])
    ex4 = _exchange_wait(x4, res["ab_w_in"][0])
    loss = jnp.sum(ex4[2][:, 0, 0])
    for names, gp, tag in ((SMALL, ex4[0], "small"), (REPL, ex4[1], "repl")):
        shapes = [local_shape(n) for n in names]
        packs = [_pack([(d[n], 0) for n in names]) for d in (w, mom, var)]
        outs = _adamw(packs[0], packs[1], packs[2], gp, name=f"adamw_{tag}")
        for k, o in enumerate(outs):
            for n, a in zip(names, _unpack(o, shapes)):
                res.setdefault(n, [None] * 4)[k] = a
    outs = [loss, grad_x]
    for k in range(4):
        outs += [res[n][k].reshape(local_shape(n)) for n in WNAMES]
    return tuple(outs)
```

```python
import functools
import math

import jax
import jax.numpy as jnp
from jax import lax
from jax.experimental import pallas as pl
from jax.experimental.pallas import tpu as pltpu

F32 = jnp.float32
BF16 = jnp.bfloat16

D_MODEL = 2048
SEQ = 2048
N_META = 16
CHUNK = 128
PAD_ROWS = CHUNK - N_META
NORM_EPS = 1e-6
HEAD_NORM_EPS = 1e-5
S5_WIDTH = 1024
S5_GROUPS = 64
S5_GROUP_SIZE = 16
S5_STATE = 64
S5_GB = 8
S5_LANES = S5_GB * S5_STATE
ML_WIDTH = 3072
ML_HEADS = 8
ML_DH = 384
ML_CONV = 4
QKV_BLOCK = 4
SSD_INNER = 4096
SSD_HEADS = 64
SSD_P = 64
SSD_N = 128
SSD_GROUPS = 8
SSD_HPG = 8
SSD_GW = SSD_HPG * SSD_P
N_DEV = 8
ADAM_LR, ADAM_B1, ADAM_B2, ADAM_EPS, ADAM_WD, ADAM_STEP = 0.001, 0.9, 0.999, 1e-08, 0.01, 10
NEG = -1e30
VMEM_CAP = 60 * 1024 * 1024
MM_BLOCK_BUDGET = 22 * 1024 * 1024
MESH = pl.DeviceIdType.MESH

NN = (((1,), (0,)), ((), ()))
NT = (((1,), (1,)), ((), ()))
TN = (((0,), (0,)), ((), ()))


def _dot(a, b, dims=NN):
    return lax.dot_general(a, b, dims, preferred_element_type=F32)


def _bf(x):
    return x.astype(BF16)


def _pick(n, cands):
    for c in cands:
        if n % c == 0:
            return c
    return n


def _nbytes(shape, dtype):
    return math.prod(shape) * jnp.dtype(dtype).itemsize


ANY_SPEC = pl.BlockSpec(memory_space=pl.ANY)


def _pc(body, *, name, grid, in_specs, out_specs, out_shape, scratch=(), vmem=None, dep=None):
    limit = None if vmem is None else int(min(VMEM_CAP, max(32 * 1024 * 1024, 2 * vmem + (8 << 20))))
    n_in = len(in_specs)
    if dep is not None:
        inner = body

        def body(*refs):
            inner(*refs[:n_in], *refs[n_in + 1:])

        in_specs = list(in_specs) + [ANY_SPEC]
    call = pl.pallas_call(
        body, name=name, grid=grid, in_specs=in_specs, out_specs=out_specs, out_shape=out_shape,
        scratch_shapes=list(scratch),
        compiler_params=pltpu.CompilerParams(dimension_semantics=("arbitrary",) * len(grid), vmem_limit_bytes=limit))
    return call if dep is None else (lambda *args: call(*args, dep))


def _silu(x):
    return x * jax.nn.sigmoid(x)


def _dsilu(x):
    s = jax.nn.sigmoid(x)
    return s * (1.0 + x * (1.0 - s))


def _gelu_and_grad(x):
    c0 = math.sqrt(2.0 / math.pi)
    inner = c0 * (x + 0.044715 * x * x * x)
    t = jnp.tanh(inner)
    g = 0.5 * x * (1.0 + t)
    dg = 0.5 * (1.0 + t) + 0.5 * x * (1.0 - t * t) * c0 * (1.0 + 3 * 0.044715 * x * x)
    return g, dg


def _mm(a, b, mode, *, name, add=None, out_dtype=F32, tn=None, slabs=False, dep=None):
    if mode == "NN":
        (m, k), (k2, n) = a.shape, b.shape
    elif mode == "NT":
        (m, k), (n, k2) = a.shape, b.shape
    else:
        (k, m), (k2, n) = a.shape, b.shape
    assert k == k2, (a.shape, b.shape, mode)
    tm = _pick(m, (1088, 1024, 768, 512, 384, 256, 128))
    def block_bytes(tk, tn_):
        return (_nbytes((tm, tk), a.dtype) + _nbytes((tk, tn_), b.dtype) + _nbytes((tm, tn_), out_dtype)
                + (_nbytes((tm, tn_), F32) if add is not None else 0))

    budget = MM_BLOCK_BUDGET // 2 if mode == "TN" else MM_BLOCK_BUDGET
    if tn is None:
        tn = _pick(n, (512, 384, 256, 128))
        if mode != "TN" and n % 1024 == 0 and block_bytes(k, 1024) <= (2 * budget) // 3:
            tn = 1024
    tk = k if block_bytes(k, tn) <= budget else _pick(k, (2176, 2048, 1088, 1024, 768, 512, 384, 256, 128))
    nk = k // tk
    dims = {"NN": NN, "NT": NT, "TN": TN}[mode]

    def body(*refs):
        a_ref, b_ref = refs[0], refs[1]
        add_ref = refs[2] if add is not None else None
        o_ref = refs[3] if add is not None else refs[2]

        def finish(r):
            if add_ref is not None:
                r = r + add_ref[...]
            o_ref[...] = r.reshape(o_ref.shape).astype(o_ref.dtype)

        prod = _dot(_bf(a_ref[...]), _bf(b_ref[...]), dims)
        if nk == 1:
            finish(prod)
            return
        acc_ref = refs[-1]
        kk = pl.program_id(2)

        @pl.when(kk == 0)
        def _():
            acc_ref[...] = prod

        @pl.when(kk > 0)
        def _():
            acc_ref[...] += prod

        @pl.when(kk == nk - 1)
        def _():
            finish(acc_ref[...])

    if mode == "NN":
        a_spec = pl.BlockSpec((tm, tk), lambda i, j, kk: (i, kk))
        b_spec = pl.BlockSpec((tk, tn), lambda i, j, kk: (kk, j))
    elif mode == "NT":
        a_spec = pl.BlockSpec((tm, tk), lambda i, j, kk: (i, kk))
        b_spec = pl.BlockSpec((tn, tk), lambda i, j, kk: (j, kk))
    else:
        a_spec = pl.BlockSpec((tk, tm), lambda i, j, kk: (kk, i))
        b_spec = pl.BlockSpec((tk, tn), lambda i, j, kk: (kk, j))
    in_specs = [a_spec, b_spec]
    args = [a, b]
    if add is not None:
        in_specs.append(pl.BlockSpec((tm, tn), lambda i, j, kk: (i, j)))
        args.append(add)
    if slabs:
        out_shape = jax.ShapeDtypeStruct((n // tn, m, tn), out_dtype)
        out_spec = pl.BlockSpec((1, tm, tn), lambda i, j, kk: (j, i, 0))
    else:
        out_shape = jax.ShapeDtypeStruct((m, n), out_dtype)
        out_spec = pl.BlockSpec((tm, tn), lambda i, j, kk: (i, j))
    return _pc(body, name=name, grid=(m // tm, n // tn, nk), in_specs=in_specs, out_specs=out_spec,
               out_shape=out_shape, scratch=[] if nk == 1 else [pltpu.VMEM((tm, tn), F32)],
               vmem=block_bytes(tk, tn) + (0 if nk == 1 else _nbytes((tm, tn), F32) // 2), dep=dep)(*args)


def _rms_fwd(x, g, *, name):
    r, d = x.shape
    tm = _pick(r, (256, 128))

    def body(x_ref, g_ref, o_ref):
        xv = x_ref[...]
        rstd = lax.rsqrt(jnp.mean(xv * xv, axis=1, keepdims=True) + NORM_EPS)
        o_ref[...] = (xv * rstd * g_ref[...]).astype(o_ref.dtype)

    return _pc(body, name=name, grid=(r // tm,),
               in_specs=[pl.BlockSpec((tm, d), lambda i: (i, 0)), pl.BlockSpec((1, d), lambda i: (0, 0))],
               out_specs=pl.BlockSpec((tm, d), lambda i: (i, 0)), out_shape=jax.ShapeDtypeStruct((r, d), BF16),
               vmem=tm * d * 6)(x, g.reshape(1, d))


def _rms_bwd(x, g, dxn, dres, *, name, dep=None):
    r, d = x.shape
    tm = _pick(r, (256, 128))

    def body(x_ref, g_ref, dxn_ref, dres_ref, dx_ref, dg_ref, db_ref):
        @pl.when(pl.program_id(0) == 0)
        def _():
            dg_ref[...] = jnp.zeros_like(dg_ref)

        xv = x_ref[...]
        rstd = lax.rsqrt(jnp.mean(xv * xv, axis=1, keepdims=True) + NORM_EPS)
        xh = xv * rstd
        dy = dxn_ref[...]
        dg_ref[...] += jnp.sum(dy * xh, axis=0, keepdims=True)
        dyg = dy * g_ref[...]
        dx_ref[...] = dres_ref[...] + rstd * (dyg - xh * jnp.mean(dyg * xh, axis=1, keepdims=True))

        db_ref[...] = _bf(dx_ref[...])

    row = pl.BlockSpec((tm, d), lambda i: (i, 0))
    vec = pl.BlockSpec((1, d), lambda i: (0, 0))
    return _pc(body, name=name, grid=(r // tm,), in_specs=[row, vec, row, row], out_specs=[row, vec, row],
               out_shape=[jax.ShapeDtypeStruct((r, d), F32), jax.ShapeDtypeStruct((1, d), F32),
                          jax.ShapeDtypeStruct((r, d), BF16)],
               vmem=tm * d * 18, dep=dep)(x, g.reshape(1, d), dxn, dres)


def _rms_bwd_first(x, g, dxn, dres, bsz, nc, *, name):
    d = x.shape[1]

    def body(x_ref, g_ref, dxn_ref, dres_ref, gx_ref, d0_ref, dg_ref):
        b, c = pl.program_id(0), pl.program_id(1)

        @pl.when((b == 0) & (c == 0))
        def _():
            dg_ref[...] = jnp.zeros_like(dg_ref)

        xv = x_ref[...]
        rstd = lax.rsqrt(jnp.mean(xv * xv, axis=1, keepdims=True) + NORM_EPS)
        xh = xv * rstd
        dy = dxn_ref[...]
        dg_ref[...] += jnp.sum(dy * xh, axis=0, keepdims=True)
        dyg = dy * g_ref[...]
        dx = dres_ref[...] + rstd * (dyg - xh * jnp.mean(dyg * xh, axis=1, keepdims=True))

        @pl.when(c == 0)
        def _():
            d0_ref[0] = dx

        @pl.when(c > 0)
        def _():
            gx_ref[0] = dx

    row = pl.BlockSpec((CHUNK, d), lambda b, c: (b * nc + c, 0))
    vec = pl.BlockSpec((1, d), lambda b, c: (0, 0))
    return _pc(body, name=name, grid=(bsz, nc), in_specs=[row, vec, row, row],
               out_specs=[pl.BlockSpec((1, CHUNK, d), lambda b, c: (b, jnp.maximum(c - 1, 0), 0)),
                          pl.BlockSpec((1, CHUNK, d), lambda b, c: (b, 0, 0)), vec],
               out_shape=[jax.ShapeDtypeStruct((bsz, (nc - 1) * CHUNK, d), F32),
                          jax.ShapeDtypeStruct((bsz, CHUNK, d), F32), jax.ShapeDtypeStruct((1, d), F32)],
               vmem=CHUNK * d * 24)(x, g.reshape(1, d), dxn, dres)


def _final_loss(h, g, target, bsz, nc):
    d = h.shape[1]

    def body(h_ref, g_ref, t_ref, loss_ref, dh_ref, dg_ref, db_ref):
        b, c = pl.program_id(0), pl.program_id(1)

        @pl.when((b == 0) & (c == 0))
        def _():
            loss_ref[...] = jnp.zeros_like(loss_ref)
            dg_ref[...] = jnp.zeros_like(dg_ref)

        @pl.when(c == 0)
        def _():
            dh_ref[...] = jnp.zeros_like(dh_ref)
            db_ref[...] = jnp.zeros_like(db_ref)

        @pl.when(c > 0)
        def _():
            xv = h_ref[...]
            rstd = lax.rsqrt(jnp.mean(xv * xv, axis=1, keepdims=True) + NORM_EPS)
            xh = xv * rstd
            gv = g_ref[...]
            err = xh * gv - t_ref[0]
            loss_ref[...] += 0.5 * jnp.sum(jnp.mean(err * err, axis=1, keepdims=True))
            dy = err * (1.0 / d)
            dg_ref[...] += jnp.sum(dy * xh, axis=0, keepdims=True)
            dyg = dy * gv
            dh = rstd * (dyg - xh * jnp.mean(dyg * xh, axis=1, keepdims=True))
            dh_ref[...] = dh
            db_ref[...] = _bf(dh)

    row = pl.BlockSpec((CHUNK, d), lambda b, c: (b * nc + c, 0))
    vec = pl.BlockSpec((1, d), lambda b, c: (0, 0))
    return _pc(body, name="final_loss", grid=(bsz, nc),
               in_specs=[row, vec, pl.BlockSpec((1, CHUNK, d), lambda b, c: (b, jnp.maximum(c - 1, 0), 0))],
               out_specs=[pl.BlockSpec((8, 128), lambda b, c: (0, 0)), row, vec, row],
               out_shape=[jax.ShapeDtypeStruct((8, 128), F32), jax.ShapeDtypeStruct(h.shape, F32),
                          jax.ShapeDtypeStruct((1, d), F32), jax.ShapeDtypeStruct(h.shape, BF16)],
               vmem=CHUNK * d * 18)(h, g.reshape(1, d), target)


def _adamw(w, m, v, gparts, *, name):
    r, c = w.shape
    tr = _pick(r, (256, 128)) if r * c * 4 > (1 << 20) else r

    def body(w_ref, m_ref, v_ref, gp_ref, g_ref, d_ref, nm_ref, nv_ref):
        g = gp_ref[0].astype(F32)
        for j in range(1, N_DEV):
            g = g + gp_ref[j].astype(F32)
        mm = ADAM_B1 * m_ref[...] + (1.0 - ADAM_B1) * g
        vv = ADAM_B2 * v_ref[...] + (1.0 - ADAM_B2) * (g * g)
        m_hat = mm / (1.0 - ADAM_B1 ** ADAM_STEP)
        v_hat = vv / (1.0 - ADAM_B2 ** ADAM_STEP)
        g_ref[...] = g
        d_ref[...] = -ADAM_LR * (m_hat / (jnp.sqrt(v_hat) + ADAM_EPS) + ADAM_WD * w_ref[...])
        nm_ref[...] = mm
        nv_ref[...] = vv

    blk = pl.BlockSpec((tr, c), lambda i: (i, 0))
    out = jax.ShapeDtypeStruct((r, c), F32)
    return _pc(body, name=name, grid=(r // tr,),
               in_specs=[blk, blk, blk, pl.BlockSpec((N_DEV, tr, c), lambda i: (0, i, 0))],
               out_specs=[blk, blk, blk, blk], out_shape=[out, out, out, out],
               vmem=tr * c * (4 * 7 + N_DEV * jnp.dtype(gparts.dtype).itemsize))(w, m, v, gparts)


PEERS = (1, 2, 4, 6, 3, 5, 7)
HBM_SPEC = pl.BlockSpec(memory_space=pltpu.HBM)
SEM_SPEC = pl.BlockSpec(memory_space=pltpu.SEMAPHORE)
SIDE_EFFECT = pltpu.SideEffectType.DATAFLOW_SIDE_EFFECTING


def _peer(p):
    x, y, c = lax.axis_index("x"), lax.axis_index("y"), lax.axis_index("c")
    tx, ty, tc = x ^ ((p >> 2) & 1), y ^ ((p >> 1) & 1), c ^ (p & 1)
    return (tx, ty, tc), 4 * tx + 2 * ty + tc


def _place_own(a, kind, *, name):
    rows, cols = a.shape[-2:]
    small = _nbytes((rows, cols), a.dtype) <= (2 << 20)
    tr = rows if small else _pick(rows, (512, 256, 128, 64, 32, 16))
    me = (4 * lax.axis_index("x") + 2 * lax.axis_index("y") + lax.axis_index("c")).astype(jnp.int32).reshape(1)

    def body(me_ref, in_ref, out_ref):
        out_ref[...] = in_ref[...].reshape(out_ref.shape)

    if kind == "a2a":
        in_spec = pl.BlockSpec((1, tr, cols), lambda i, me_ref: (me_ref[0], i, 0))
    else:
        in_spec = pl.BlockSpec((tr, cols), lambda i, me_ref: (i, 0))
    return pl.pallas_call(
        body, name=name, out_shape=jax.ShapeDtypeStruct((N_DEV, rows, cols), a.dtype),
        grid_spec=pltpu.PrefetchScalarGridSpec(
            num_scalar_prefetch=1, grid=(rows // tr,), in_specs=[in_spec],
            out_specs=pl.BlockSpec((1, tr, cols), lambda i, me_ref: (me_ref[0], i, 0))))(me, a)


def _exchange_copies(ins, lands, send_sems, recv_sems, kinds, incoming, peers=PEERS):
    me = 4 * lax.axis_index("x") + 2 * lax.axis_index("y") + lax.axis_index("c")
    copies = []
    for i, kind in enumerate(kinds):
        for p in peers:
            dev, tgt = _peer(p)
            k = i * (N_DEV - 1) + p - 1
            copies.append(pltpu.make_async_remote_copy(
                src_ref=ins[i].at[tgt] if kind == "a2a" else ins[i], dst_ref=lands[i].at[tgt if incoming else me],
                send_sem=send_sems.at[k], recv_sem=recv_sems.at[k], device_id=dev, device_id_type=MESH))
    return copies


def _exchange_start(arrays, kinds, *, name, dep=None, peers=PEERS):
    n = len(arrays)
    lands = [_place_own(a, k, name=f"{name}_own{i}") for i, (a, k) in enumerate(zip(arrays, kinds))]
    extra = [] if dep is None else [dep]

    def body(*refs):
        ins, lnd = refs[:n], refs[n:2 * n]
        send_sems, recv_sems = refs[2 * n + len(extra)], refs[2 * n + len(extra) + 1]
        token = refs[-1]
        for cp in _exchange_copies(ins, lnd, send_sems, recv_sems, kinds, False, peers):
            cp.start()
        token[...] = jnp.zeros_like(token)

    sem = pltpu.SemaphoreType.DMA((n * (N_DEV - 1),))
    outs = pl.pallas_call(
        body, name=name, in_specs=[HBM_SPEC] * (2 * n) + [ANY_SPEC] * len(extra),
        out_specs=[SEM_SPEC, SEM_SPEC] + [HBM_SPEC] * (2 * n) + [pl.BlockSpec(memory_space=pltpu.VMEM)],
        out_shape=[sem, sem] + [pltpu.HBM(a.shape, a.dtype) for a in arrays + lands]
        + [jax.ShapeDtypeStruct((8, 128), F32)],
        input_output_aliases={i: 2 + i for i in range(2 * n)},
        compiler_params=pltpu.CompilerParams(has_side_effects=SIDE_EFFECT),
    )(*[pltpu.with_memory_space_constraint(a, pltpu.HBM) for a in arrays + lands], *extra)
    return dict(send=outs[0], recv=outs[1], ins=list(outs[2:2 + n]), lands=list(outs[2 + n:2 + 2 * n]),
                token=outs[-1], kinds=kinds, name=name, peers=peers)


def _exchange_wait(h, after):
    n = len(h["ins"])
    kinds = h["kinds"]

    def body(*refs):
        ins, lnd = refs[:n], refs[n:2 * n]
        send_sems, recv_sems = refs[2 * n], refs[2 * n + 1]
        copies = _exchange_copies(ins, lnd, send_sems, recv_sems, kinds, True, h["peers"])
        for cp in copies:
            cp.wait_recv()
        for cp in copies:
            cp.wait_send()

    arrs = h["ins"] + h["lands"]
    after = list(after) if isinstance(after, (list, tuple)) else [after]
    outs = pl.pallas_call(
        body, name=h["name"] + "_wait", in_specs=[HBM_SPEC] * (2 * n) + [SEM_SPEC, SEM_SPEC] + [ANY_SPEC] * len(after),
        out_specs=[HBM_SPEC] * (2 * n), out_shape=[pltpu.HBM(a.shape, a.dtype) for a in arrs],
        input_output_aliases={i: i for i in range(2 * n)},
        compiler_params=pltpu.CompilerParams(has_side_effects=SIDE_EFFECT),
    )(*arrs, h["send"], h["recv"], *after)
    return list(outs[n:])


SAME_CORE = (0, 2, 4, 6)


def _forward_copies(land, send_sems, recv_sems, incoming):
    me = 4 * lax.axis_index("x") + 2 * lax.axis_index("y") + lax.axis_index("c")
    dev, sibling = _peer(1)
    return [pltpu.make_async_remote_copy(
        src_ref=land.at[me ^ q], dst_ref=land.at[(sibling if incoming else me) ^ q],
        send_sem=send_sems.at[j], recv_sem=recv_sems.at[j], device_id=dev, device_id_type=MESH)
        for j, q in enumerate(SAME_CORE)]


def _sibling_forward_start(land, *, name, dep=None):
    extra = [] if dep is None else [dep]

    def body(*refs):
        land_ref, send_sems, recv_sems, token = refs[0], refs[1 + len(extra)], refs[2 + len(extra)], refs[-1]
        for cp in _forward_copies(land_ref, send_sems, recv_sems, False):
            cp.start()
        token[...] = jnp.zeros_like(token)

    sem = pltpu.SemaphoreType.DMA((len(SAME_CORE),))
    outs = pl.pallas_call(
        body, name=name, in_specs=[HBM_SPEC] + [ANY_SPEC] * len(extra),
        out_specs=[SEM_SPEC, SEM_SPEC, HBM_SPEC, pl.BlockSpec(memory_space=pltpu.VMEM)],
        out_shape=[sem, sem, pltpu.HBM(land.shape, land.dtype), jax.ShapeDtypeStruct((8, 128), F32)],
        input_output_aliases={0: 2}, compiler_params=pltpu.CompilerParams(has_side_effects=SIDE_EFFECT),
    )(pltpu.with_memory_space_constraint(land, pltpu.HBM), *extra)
    return dict(send=outs[0], recv=outs[1], land=outs[2], token=outs[3], name=name)


def _sibling_forward_wait(h, after):
    def body(*refs):
        copies = _forward_copies(refs[0], refs[1], refs[2], True)
        for cp in copies:
            cp.wait_recv()
        for cp in copies:
            cp.wait_send()

    return pl.pallas_call(
        body, name=h["name"] + "_wait", in_specs=[HBM_SPEC, SEM_SPEC, SEM_SPEC, ANY_SPEC], out_specs=HBM_SPEC,
        out_shape=pltpu.HBM(h["land"].shape, h["land"].dtype), input_output_aliases={0: 0},
        compiler_params=pltpu.CompilerParams(has_side_effects=SIDE_EFFECT),
    )(h["land"], h["send"], h["recv"], after)


def _s5_params(lam_re, lam_im, log_dt, b_re, b_im):
    dt = jnp.exp(log_dt)[:, None]
    mag = jnp.exp(lam_re * dt)
    ar, ai = mag * jnp.cos(lam_im * dt), mag * jnp.sin(lam_im * dt)
    den = lam_re * lam_re + lam_im * lam_im
    qr = ((ar - 1.0) * lam_re + ai * lam_im) / den
    qi = (ai * lam_re - (ar - 1.0) * lam_im) / den
    bbr = qr[..., None] * b_re - qi[..., None] * b_im
    bbi = qr[..., None] * b_im + qi[..., None] * b_re
    return ar, ai, bbr, bbi


def _s5_power_table(ar, ai):
    pr, pi = ar.reshape(1, -1), ai.reshape(1, -1)
    while pr.shape[0] < 8:
        sr, si = pr[-1:], pi[-1:]
        pr, pi = (jnp.concatenate([pr, pr * sr - pi * si], axis=0), jnp.concatenate([pi, pr * si + pi * sr], axis=0))
    return pr, pi


def _blockdiag(w, rows, cols):
    w = w.reshape(S5_GB, S5_GB, rows, cols)
    eye = jnp.eye(S5_GB, dtype=w.dtype)
    return jnp.einsum("abrc,bd->abrdc", w, eye).reshape(S5_GB, S5_GB * rows, S5_GB * cols)


def _blockdiag_extract(w, rows, cols):
    w = w.reshape(S5_GB, S5_GB, rows, S5_GB, cols)
    return jnp.einsum("abrbc->abrc", w).reshape(S5_GROUPS, rows, cols)


def _s5_scan_specs(bsz, nc, rev):
    def cc(c):
        return (nc - 1 - c) if rev else c

    return dict(
        u=pl.BlockSpec((bsz, CHUNK, CHUNK), lambda g, c: (0, cc(c), g)),
        x=pl.BlockSpec((bsz, CHUNK, S5_LANES), lambda g, c: (0, cc(c), g)),
        wb=pl.BlockSpec((1, CHUNK, S5_LANES), lambda g, c: (g, 0, 0)),
        wc=pl.BlockSpec((1, S5_LANES, CHUNK), lambda g, c: (g, 0, 0)),
        tab=pl.BlockSpec((8, S5_LANES), lambda g, c: (0, g)),
        step=pl.BlockSpec((8, S5_LANES), lambda g, c: (0, g)),
        d=pl.BlockSpec((1, CHUNK), lambda g, c: (0, g)),
        lane=pl.BlockSpec((1, S5_LANES), lambda g, c: (0, g)),
        xprev=pl.BlockSpec((bsz, 8, S5_LANES), lambda g, c: (0, jnp.maximum(cc(c) * (CHUNK // 8) - 1, 0), g)),
    )


def _s5_fwd(u, wbr, wbi, pr, pi, sr, si, wcr, wci, d, bsz, nc):
    r = u.shape[0]
    tp = r // bsz
    sp = _s5_scan_specs(bsz, nc, False)

    def body(u_all, wbr_ref, wbi_ref, pr_ref, pi_ref, sr_ref, si_ref, wcr_ref, wci_ref, d_ref,
             xr_all, xi_all, y1_all, g_all, cr_sall, ci_sall):
        @pl.when(pl.program_id(1) == 0)
        def _():
            cr_sall[...] = jnp.zeros_like(cr_sall)
            ci_sall[...] = jnp.zeros_like(ci_sall)

        for bi in range(bsz):
            one(u_all.at[bi], wbr_ref, wbi_ref, pr_ref, pi_ref, sr_ref, si_ref, wcr_ref, wci_ref, d_ref,
                xr_all.at[bi], xi_all.at[bi], y1_all.at[bi], g_all.at[bi], cr_sall.at[bi], ci_sall.at[bi])

    def one(u_ref, wbr_ref, wbi_ref, pr_ref, pi_ref, sr_ref, si_ref, wcr_ref, wci_ref, d_ref,
            xr_ref, xi_ref, y1_ref, g_ref, cr_s, ci_s):
        uv = u_ref[...]
        ub = _bf(uv)
        xr, xi = _dot(ub, wbr_ref[0]), _dot(ub, wbi_ref[0])
        sub = lax.broadcasted_iota(jnp.int32, (CHUNK, S5_LANES), 0) % 8
        for k in range(3):
            s = 1 << k
            ar, ai = sr_ref[k:k + 1, :], si_ref[k:k + 1, :]
            hr = jnp.where(sub >= s, pltpu.roll(xr, s, 0), 0.0)
            hi = jnp.where(sub >= s, pltpu.roll(xi, s, 0), 0.0)
            xr, xi = xr + (ar * hr - ai * hi), xi + (ar * hi + ai * hr)
        cr, ci = cr_s[...], ci_s[...]
        tr, ti = pr_ref[...], pi_ref[...]
        outr, outi = [], []
        for g8 in range(CHUNK // 8):
            br, bi = xr[8 * g8:8 * g8 + 8, :], xi[8 * g8:8 * g8 + 8, :]
            br, bi = br + (tr * cr - ti * ci), bi + (tr * ci + ti * cr)
            cr, ci = br[7:8, :], bi[7:8, :]
            outr.append(br)
            outi.append(bi)
        xr, xi = jnp.concatenate(outr, axis=0), jnp.concatenate(outi, axis=0)
        cr_s[...] = cr
        ci_s[...] = ci
        xr_ref[...] = xr
        xi_ref[...] = xi
        y = _dot(_bf(xr), wcr_ref[0]) - _dot(_bf(xi), wci_ref[0]) + d_ref[...] * uv
        y1_ref[...] = y
        g_ref[...] = _bf(_gelu_and_grad(y)[0])

    ns = S5_GROUPS * S5_STATE
    xr, xi, y1, g = _pc(
        body, name="s5_fwd", grid=(S5_GB, nc),
        in_specs=[sp["u"], sp["wb"], sp["wb"], sp["tab"], sp["tab"], sp["step"], sp["step"], sp["wc"], sp["wc"],
                  sp["d"]],
        out_specs=[sp["x"], sp["x"], sp["u"], sp["u"]],
        out_shape=[jax.ShapeDtypeStruct((bsz, tp, ns), F32)] * 2
        + [jax.ShapeDtypeStruct((bsz, tp, S5_WIDTH), F32), jax.ShapeDtypeStruct((bsz, tp, S5_WIDTH), BF16)],
        scratch=[pltpu.VMEM((bsz, 1, S5_LANES), F32)] * 2, vmem=8 << 20,
    )(_seq(u, bsz), wbr, wbi, pr, pi, sr, si, wcr, wci, d)
    return xr.reshape(r, ns), xi.reshape(r, ns), y1.reshape(r, S5_WIDTH), g.reshape(r, S5_WIDTH)


def _s5_post(y1, glu_pre, glu_b, z):
    r, w = y1.shape
    tm = _pick(r, (256, 128))

    def body(y_ref, p_ref, b_ref, z_ref, o_ref):
        g = _gelu_and_grad(y_ref[...])[0]
        o_ref[...] = _bf(g * jax.nn.sigmoid(p_ref[...] + b_ref[...]) * _silu(z_ref[...]))

    row = pl.BlockSpec((tm, w), lambda i: (i, 0))
    return _pc(body, name="s5_post", grid=(r // tm,), in_specs=[row, row, pl.BlockSpec((1, w), lambda i: (0, 0)), row],
               out_specs=row, out_shape=jax.ShapeDtypeStruct((r, w), BF16), vmem=tm * w * 16)(y1, glu_pre, glu_b, z)


def _s5_post_bwd(dya, y1, glu_pre, glu_b, z):
    r, w = y1.shape
    tm = _pick(r, (256, 128))

    def body(dy_ref, y_ref, p_ref, b_ref, z_ref, dz_ref, dp_ref, dg_ref, db_ref):
        @pl.when(pl.program_id(0) == 0)
        def _():
            db_ref[...] = jnp.zeros_like(db_ref)

        g = _gelu_and_grad(y_ref[...])[0]
        s = jax.nn.sigmoid(p_ref[...] + b_ref[...])
        zv = z_ref[...]
        dy = dy_ref[...]
        do = dy * _silu(zv)
        dz_ref[...] = _bf(dy * g * s * _dsilu(zv))
        dp = do * g * s * (1.0 - s)
        dp_ref[...] = _bf(dp)
        db_ref[...] += jnp.sum(dp, axis=0, keepdims=True)
        dg_ref[...] = do * s

    row = pl.BlockSpec((tm, w), lambda i: (i, 0))
    vec = pl.BlockSpec((1, w), lambda i: (0, 0))
    return _pc(body, name="s5_post_bwd", grid=(r // tm,), in_specs=[row, row, row, vec, row],
               out_specs=[row, row, row, vec],
               out_shape=[jax.ShapeDtypeStruct((r, w), BF16), jax.ShapeDtypeStruct((r, w), BF16),
                          jax.ShapeDtypeStruct((r, w), F32), jax.ShapeDtypeStruct((1, w), F32)],
               vmem=tm * w * 24)(dya, y1, glu_pre, glu_b, z)


def _s5_bwd(dg, y1, u, xr, xi, wbr, wbi, qr, qi, sr, si, wcr, wci, d, bsz, nc):
    r = u.shape[0]
    tp = r // bsz
    sp = _s5_scan_specs(bsz, nc, True)

    def body(dg_all, y1_all, u_all, xr_all, xi_all, xpr_all, xpi_all, wbr_ref, wbi_ref, qr_ref, qi_ref, sr_ref, si_ref,
             wcr_ref, wci_ref, d_ref, du_all, dd_ref, dwcr_ref, dwci_ref, dwbr_ref, dwbi_ref, dar_ref, dai_ref,
             cr_sall, ci_sall):
        c = pl.program_id(1)

        @pl.when(c == 0)
        def _():
            for ref in (dd_ref, dwcr_ref, dwci_ref, dwbr_ref, dwbi_ref, dar_ref, dai_ref, cr_sall, ci_sall):
                ref[...] = jnp.zeros_like(ref)

        for bi in range(bsz):
            one(c, dg_all.at[bi], y1_all.at[bi], u_all.at[bi], xr_all.at[bi], xi_all.at[bi], xpr_all.at[bi],
                xpi_all.at[bi], wbr_ref, wbi_ref, qr_ref, qi_ref, sr_ref, si_ref, wcr_ref, wci_ref, d_ref,
                du_all.at[bi], dd_ref, dwcr_ref, dwci_ref, dwbr_ref, dwbi_ref, dar_ref, dai_ref, cr_sall.at[bi],
                ci_sall.at[bi])

    def one(c, dg_ref, y1_ref, u_ref, xr_ref, xi_ref, xpr_ref, xpi_ref, wbr_ref, wbi_ref, qr_ref, qi_ref, sr_ref, si_ref,
            wcr_ref, wci_ref, d_ref, du_ref, dd_ref, dwcr_ref, dwci_ref, dwbr_ref, dwbi_ref, dar_ref, dai_ref,
            cr_s, ci_s):
        uv = u_ref[...]
        ub = _bf(uv)
        dy = dg_ref[...] * _gelu_and_grad(y1_ref[...])[1]
        dd_ref[...] += jnp.sum(dy * uv, axis=0, keepdims=True)
        dyb = _bf(dy)
        xr, xi = xr_ref[...], xi_ref[...]
        dwcr_ref[0] += _dot(_bf(xr), dyb, TN)
        dwci_ref[0] -= _dot(_bf(xi), dyb, TN)
        lr, li = _dot(dyb, wcr_ref[0], NT), -_dot(dyb, wci_ref[0], NT)
        row = lax.broadcasted_iota(jnp.int32, (CHUNK, S5_LANES), 0)
        sub = row % 8
        for k in range(3):
            s = 1 << k
            ar, ai = sr_ref[k:k + 1, :], si_ref[k:k + 1, :]
            hr = jnp.where(sub < 8 - s, pltpu.roll(lr, CHUNK - s, 0), 0.0)
            hi = jnp.where(sub < 8 - s, pltpu.roll(li, CHUNK - s, 0), 0.0)
            lr, li = lr + (ar * hr + ai * hi), li + (ar * hi - ai * hr)
        cr, ci = cr_s[...], ci_s[...]
        tr, ti = qr_ref[...], qi_ref[...]
        outr, outi = [], []
        for g8 in reversed(range(CHUNK // 8)):
            br, bi = lr[8 * g8:8 * g8 + 8, :], li[8 * g8:8 * g8 + 8, :]
            br, bi = br + (tr * cr + ti * ci), bi + (tr * ci - ti * cr)
            cr, ci = br[0:1, :], bi[0:1, :]
            outr.append(br)
            outi.append(bi)
        lr, li = jnp.concatenate(outr[::-1], axis=0), jnp.concatenate(outi[::-1], axis=0)
        cr_s[...] = cr
        ci_s[...] = ci
        lrb, lib = _bf(lr), _bf(li)
        du_ref[...] = _bf(_dot(lrb, wbr_ref[0], NT) + _dot(lib, wbi_ref[0], NT) + dy * d_ref[...])
        dwbr_ref[0] += _dot(ub, lrb, TN)
        dwbi_ref[0] += _dot(ub, lib, TN)
        first = c == nc - 1
        pr0 = jnp.where(first, 0.0, xpr_ref[7:8, :])
        pi0 = jnp.where(first, 0.0, xpi_ref[7:8, :])
        xpr = jnp.where(row == 0, pr0, pltpu.roll(xr, 1, 0))
        xpi = jnp.where(row == 0, pi0, pltpu.roll(xi, 1, 0))
        dar_ref[...] += jnp.sum(lr * xpr + li * xpi, axis=0, keepdims=True)
        dai_ref[...] += jnp.sum(li * xpr - lr * xpi, axis=0, keepdims=True)

    st = jax.ShapeDtypeStruct
    xr3, xi3 = _seq(xr, bsz), _seq(xi, bsz)
    outs = _pc(body, name="s5_bwd", grid=(S5_GB, nc),
               in_specs=[sp["u"], sp["u"], sp["u"], sp["x"], sp["x"], sp["xprev"], sp["xprev"], sp["wb"], sp["wb"],
                         sp["tab"], sp["tab"], sp["step"], sp["step"], sp["wc"], sp["wc"], sp["d"]],
               out_specs=[sp["u"], sp["d"], sp["wc"], sp["wc"], sp["wb"], sp["wb"], sp["lane"], sp["lane"]],
               out_shape=[st((bsz, tp, S5_WIDTH), BF16), st((1, S5_WIDTH), F32),
                          st((S5_GB, S5_LANES, CHUNK), F32), st((S5_GB, S5_LANES, CHUNK), F32),
                          st((S5_GB, CHUNK, S5_LANES), F32), st((S5_GB, CHUNK, S5_LANES), F32),
                          st((1, S5_GROUPS * S5_STATE), F32), st((1, S5_GROUPS * S5_STATE), F32)],
               scratch=[pltpu.VMEM((bsz, 1, S5_LANES), F32)] * 2, vmem=12 << 20,
               )(_seq(dg, bsz), _seq(y1, bsz), _seq(u, bsz), xr3, xi3, xr3, xi3, wbr, wbi, qr, qi, sr, si, wcr, wci, d)
    return (outs[0].reshape(r, S5_WIDTH),) + tuple(outs[1:])


def _s5_layer_fwd(u, prm, glu_w, bsz, nc):
    xr, xi, y1, g = _s5_fwd(u, prm["wbr"], prm["wbi"], prm["pr"], prm["pi"], prm["sr"], prm["si"], prm["wcr"],
                            prm["wci"], prm["d"], bsz, nc)
    glu_pre = _mm(g, glu_w(y1) if callable(glu_w) else glu_w, "NN", name="s5_glu")
    return dict(xr=xr, xi=xi, y1=y1, g=g, glu_pre=glu_pre)


def _s5_layer_bwd(dya, u, z, sv, prm, pvjp, glu_w, glu_b, bsz, nc):
    dz, dglu, dg_direct, dglu_b = _s5_post_bwd(dya, sv["y1"], sv["glu_pre"], glu_b, z)
    dg = _mm(dglu, glu_w, "NT", name="s5_dg", add=dg_direct)
    dglu_w = _mm(sv["g"], dglu, "TN", name="s5_dglu_w")
    du, dd, dwcr, dwci, dwbr, dwbi, dar, dai = _s5_bwd(
        dg, sv["y1"], u, sv["xr"], sv["xi"], prm["wbr"], prm["wbi"], prm["qr"], prm["qi"], prm["sr"], prm["si"],
        prm["wcr"], prm["wci"], prm["d"], bsz, nc)
    dbbr = jnp.swapaxes(_blockdiag_extract(dwbr, S5_GROUP_SIZE, S5_STATE), 1, 2)
    dbbi = jnp.swapaxes(_blockdiag_extract(dwbi, S5_GROUP_SIZE, S5_STATE), 1, 2)
    dlr, dli, dldt, dbr, dbi = pvjp((dar.reshape(S5_GROUPS, S5_STATE), dai.reshape(S5_GROUPS, S5_STATE), dbbr, dbbi))
    grads = dict(
        s5_lambda_re=dlr, s5_lambda_im=dli, s5_log_dt=dldt, s5_b_re=dbr, s5_b_im=dbi,
        s5_c_re=jnp.swapaxes(_blockdiag_extract(dwcr, S5_STATE, S5_GROUP_SIZE), 1, 2),
        s5_c_im=jnp.swapaxes(_blockdiag_extract(dwci, S5_STATE, S5_GROUP_SIZE), 1, 2),
        s5_d=dd, s5_glu_w=dglu_w, s5_glu_b=dglu_b)
    return du, dz, grads


def _s5_tables(lam_re, lam_im, log_dt, b_re, b_im, c_re, c_im, d):
    (ar, ai, bbr, bbi), vjp = jax.vjp(_s5_params, lam_re, lam_im, log_dt, b_re, b_im)
    pr, pi = _s5_power_table(lax.stop_gradient(ar), lax.stop_gradient(ai))
    steps = [0, 1, 3, 7, 7, 7, 7, 7]
    flip8 = (jnp.arange(8)[:, None] + jnp.arange(8)[None, :] == 7).astype(F32)
    prm = dict(
        wbr=_bf(_blockdiag(jnp.swapaxes(bbr, 1, 2), S5_GROUP_SIZE, S5_STATE)),
        wbi=_bf(_blockdiag(jnp.swapaxes(bbi, 1, 2), S5_GROUP_SIZE, S5_STATE)),
        wcr=_bf(_blockdiag(jnp.swapaxes(c_re, 1, 2), S5_STATE, S5_GROUP_SIZE)),
        wci=_bf(_blockdiag(jnp.swapaxes(c_im, 1, 2), S5_STATE, S5_GROUP_SIZE)),
        pr=pr, pi=pi, qr=jnp.dot(flip8, pr, precision=lax.Precision.HIGHEST),
        qi=jnp.dot(flip8, pi, precision=lax.Precision.HIGHEST),
        sr=jnp.concatenate([pr[i:i + 1] for i in steps], axis=0),
        si=jnp.concatenate([pi[i:i + 1] for i in steps], axis=0), d=d.reshape(1, S5_WIDTH))
    return prm, vjp


def _shift_down(x, halo8, s):
    sh = pltpu.roll(x, s, 0)
    r8 = lax.broadcasted_iota(jnp.int32, halo8.shape, 0)
    first = jnp.where(r8 < s, pltpu.roll(halo8, s, 0), sh[:8])
    return jnp.concatenate([first, sh[8:]], axis=0)


def _shift_up(x, halo8, s):
    sh = pltpu.roll(x, CHUNK - s, 0)
    r8 = lax.broadcasted_iota(jnp.int32, halo8.shape, 0)
    last = jnp.where(r8 >= 8 - s, pltpu.roll(halo8, 8 - s, 0), sh[CHUNK - 8:])
    return jnp.concatenate([sh[:CHUNK - 8], last], axis=0)


def _conv_specs(nc, tw):
    def chunk(b, c):
        return b * nc + c

    return dict(
        x=pl.BlockSpec((CHUNK, tw), lambda j, b, c: (chunk(b, c), j)),
        prev=pl.BlockSpec((8, tw), lambda j, b, c: (jnp.maximum(chunk(b, c) * (CHUNK // 8) - 1, 0), j)),
        nxt=pl.BlockSpec((8, tw), lambda j, b, c: ((b * nc + jnp.minimum(c + 1, nc - 1)) * (CHUNK // 8), j)),
        w=pl.BlockSpec((ML_CONV, tw), lambda j, b, c: (0, j)),
        vec=pl.BlockSpec((1, tw), lambda j, b, c: (0, j)),
    )


def _conv_fwd(x, w, bias, bsz, nc, *, name):
    r, wd = x.shape
    tw = _pick(wd, (2048, 1536, 1024, 512, 384, 256, 128))
    sp = _conv_specs(nc, tw)

    def body(x_ref, p_ref, w_ref, b_ref, o_ref):
        c = pl.program_id(2)
        xv = x_ref[...]
        halo = jnp.where(c == 0, 0.0, p_ref[...])
        acc = b_ref[...] + w_ref[3:4, :] * xv
        for s in (1, 2, 3):
            acc = acc + w_ref[3 - s:4 - s, :] * _shift_down(xv, halo, s)
        o_ref[...] = acc

    return _pc(body, name=name, grid=(wd // tw, bsz, nc), in_specs=[sp["x"], sp["prev"], sp["w"], sp["vec"]],
               out_specs=sp["x"], out_shape=jax.ShapeDtypeStruct((r, wd), F32), vmem=CHUNK * tw * 16,
               )(x, x, w, bias.reshape(1, wd))


def _conv_bwd(dpre, x, w, bsz, nc, *, name, add=None):
    r, wd = x.shape
    tw = _pick(wd, (2048, 1536, 1024, 512, 384, 256, 128))
    sp = _conv_specs(nc, tw)

    def body(*refs):
        d_ref, n_ref, x_ref, p_ref, w_ref = refs[:5]
        add_ref = refs[5] if add is not None else None
        dx_ref, dw_ref, db_ref = refs[-3:]
        b, c = pl.program_id(1), pl.program_id(2)

        @pl.when((b == 0) & (c == 0))
        def _():
            dw_ref[...] = jnp.zeros_like(dw_ref)
            db_ref[...] = jnp.zeros_like(db_ref)

        dv, xv = d_ref[...], x_ref[...]
        dhalo = jnp.where(c == nc - 1, 0.0, n_ref[...])
        xhalo = jnp.where(c == 0, 0.0, p_ref[...])
        dx = w_ref[3:4, :] * dv
        for s in (1, 2, 3):
            dx = dx + w_ref[3 - s:4 - s, :] * _shift_up(dv, dhalo, s)
        if add_ref is not None:
            dx = dx + add_ref[...]
        dx_ref[...] = _bf(dx)
        db_ref[...] += jnp.sum(dv, axis=0, keepdims=True)
        dw_ref[3:4, :] += jnp.sum(dv * xv, axis=0, keepdims=True)
        for s in (1, 2, 3):
            dw_ref[3 - s:4 - s, :] += jnp.sum(dv * _shift_down(xv, xhalo, s), axis=0, keepdims=True)

    ins = [dpre, dpre, x, x, w] + ([add] if add is not None else [])
    specs = [sp["x"], sp["nxt"], sp["x"], sp["prev"], sp["w"]] + ([sp["x"]] if add is not None else [])
    return _pc(body, name=name, grid=(wd // tw, bsz, nc), in_specs=specs, out_specs=[sp["x"], sp["w"], sp["vec"]],
               out_shape=[jax.ShapeDtypeStruct((r, wd), BF16), jax.ShapeDtypeStruct((ML_CONV, wd), F32),
                          jax.ShapeDtypeStruct((1, wd), F32)], vmem=CHUNK * tw * 24)(*ins)


ML_SCALE = ML_DH ** -0.5


ML_LB = ML_DH // CHUNK


def _headwise_expand(w):
    tiled = jnp.tile(w.reshape(ML_HEADS, ML_DH, QKV_BLOCK), (1, 1, CHUNK // QKV_BLOCK))
    rblk = (jnp.arange(ML_DH) % CHUNK) // QKV_BLOCK
    cblk = jnp.arange(CHUNK) // QKV_BLOCK
    return jnp.where(rblk[:, None] == cblk[None, :], tiled, 0.0).reshape(ML_HEADS, ML_LB, CHUNK, CHUNK)


def _headwise_dot(x, w_ref, dims=NN):
    return jnp.concatenate([_dot(x[:, j * CHUNK:(j + 1) * CHUNK], w_ref[0, j], dims) for j in range(ML_LB)], axis=1)


def _headwise_extract(w):
    return w[:, :, :QKV_BLOCK].reshape(ML_HEADS * ML_DH // QKV_BLOCK, QKV_BLOCK, QKV_BLOCK)


def _ml_pre(pre, x, wq, wk, wv, wgq, wgk, wgv, bsz, nc):
    r = x.shape[0]
    tr = _pick(r, (256, 128))
    hrow = pl.BlockSpec((tr, ML_DH), lambda h, i: (i, h))
    wexp = pl.BlockSpec((1, ML_LB, CHUNK, CHUNK), lambda h, i: (h, 0, 0, 0))
    wg = pl.BlockSpec((ML_DH, CHUNK), lambda h, i: (h, 0))

    def body(pre_ref, x_ref, wq_ref, wk_ref, wv_ref, gq_ref, gk_ref, gv_ref, q_ref, qs_ref, k_ref, v_ref, gt_ref):
        xcb = _bf(_silu(pre_ref[...]))
        q = _headwise_dot(xcb, wq_ref)
        k = _headwise_dot(xcb, wk_ref)
        v = _headwise_dot(_bf(x_ref[...]), wv_ref)
        qb, kb, vb = _bf(q), _bf(k), _bf(v)
        q_ref[...] = qb
        qs_ref[...] = _bf(q * ML_SCALE)
        k_ref[...] = kb
        v_ref[...] = vb
        gt_ref[0] = _dot(qb, gq_ref[...]) + _dot(kb, gk_ref[...]) + _dot(vb, gv_ref[...])

    o = jax.ShapeDtypeStruct((r, ML_WIDTH), BF16)
    q, qs, k, v, gates8 = _pc(
        body, name="ml_pre", grid=(ML_HEADS, r // tr),
        in_specs=[hrow, hrow, wexp, wexp, wexp, wg, wg, wg],
        out_specs=[hrow, hrow, hrow, hrow, pl.BlockSpec((1, tr, CHUNK), lambda h, i: (h, i, 0))],
        out_shape=[o, o, o, o, jax.ShapeDtypeStruct((ML_HEADS, r, CHUNK), F32)], vmem=6 << 20,
    )(pre, x, wq, wk, wv, wgq, wgk, wgv)

    def sum_body(g_ref, o_ref):
        acc = g_ref[0]
        for j in range(1, ML_HEADS):
            acc = acc + g_ref[j]
        o_ref[...] = acc

    gates = _pc(sum_body, name="ml_gates_sum", grid=(r // tr,),
                in_specs=[pl.BlockSpec((ML_HEADS, tr, CHUNK), lambda i: (0, i, 0))],
                out_specs=pl.BlockSpec((tr, CHUNK), lambda i: (i, 0)),
                out_shape=jax.ShapeDtypeStruct((r, CHUNK), F32), vmem=2 << 20)(gates8)
    return q, qs, k, v, gates


def _tri(rev):
    r = lax.broadcasted_iota(jnp.int32, (CHUNK, CHUNK), 0)
    c = lax.broadcasted_iota(jnp.int32, (CHUNK, CHUNK), 1)
    return jnp.where((c >= r) if rev else (c <= r), 1.0, 0.0).astype(F32)


def _cumsum_rows(x, row, rev=False):
    for k in range(7):
        s = 1 << k
        if rev:
            x = x + jnp.where(row < CHUNK - s, pltpu.roll(x, CHUNK - s, 0), 0.0)
        else:
            x = x + jnp.where(row >= s, pltpu.roll(x, s, 0), 0.0)
    return x


def _log_sigmoid(x):
    return jnp.minimum(x, 0.0) - jnp.log(1.0 + jnp.exp(-jnp.abs(x)))


def _ml_core(gates, hd, first, m, qs, k, v, cmat, nvec):
    sq = (CHUNK, CHUNK)
    lane = lax.broadcasted_iota(jnp.int32, sq, 1)
    row = lax.broadcasted_iota(jnp.int32, sq, 0)
    igc = jnp.sum(jnp.where(lane == hd, gates, 0.0), axis=1, keepdims=True)
    fpc = jnp.sum(jnp.where(lane == hd + ML_HEADS, gates, 0.0), axis=1, keepdims=True)
    valid = jnp.logical_or(jnp.logical_not(first), row[:, :1] >= PAD_ROWS)
    igc = jnp.where(valid, igc, NEG)
    lfc = jnp.where(valid, _log_sigmoid(fpc), 0.0)
    bcb = _cumsum_rows(jnp.broadcast_to(lfc, sq), row)
    igb = jnp.broadcast_to(igc, sq)
    dm = jnp.where(lane <= row, bcb - (bcb - igb).T, NEG)
    bc = bcb[:, :1]
    inter = bc + m
    mt = jnp.maximum(inter, jnp.max(dm, axis=1, keepdims=True))
    wt = jnp.exp(dm - mt)
    wprev = jnp.exp(inter - mt)
    s0 = _dot(qs, k, NT)
    s = s0 * wt
    cb = _bf(cmat)
    qc = _dot(qs, cb)
    qf = qs.astype(F32)
    qn = jnp.sum(qf * nvec, axis=1, keepdims=True)
    num = _dot(_bf(s), v) + wprev * qc
    den = jnp.sum(s, axis=1, keepdims=True) + wprev * qn
    emt = jnp.exp(-mt)
    dd = jnp.maximum(jnp.abs(den), emt)
    blast = bcb[CHUNK - 1:CHUNK, :1]
    g = blast - bc + igc
    m_new = jnp.maximum(blast + m, jnp.max(g, axis=0, keepdims=True))
    decay = jnp.exp(blast + m - m_new)
    e = jnp.exp(g - m_new)
    kf = k.astype(F32)
    wk = e * kf
    return dict(lane=lane, row=row, fpc=fpc, valid=valid, wt=wt, wprev=wprev, s=s, cb=cb, qc=qc, qf=qf, qn=qn,
                num=num, den=den, emt=emt, dd=dd, m_new=m_new, decay=decay, e=e, kf=kf, wk=wk)


def _ml_headnorm(h):
    mu = jnp.mean(h, axis=1, keepdims=True)
    hc = h - mu
    rstd = lax.rsqrt(jnp.mean(hc * hc, axis=1, keepdims=True) + HEAD_NORM_EPS)
    return hc * rstd, rstd


def _ml_chunk_specs(nc, rev, bsz):
    def cc(c):
        return (nc - 1 - c) if rev else c

    return dict(
        hrow=pl.BlockSpec((bsz, CHUNK, ML_DH), lambda hd, c: (0, cc(c), hd)),
        gates=pl.BlockSpec((bsz, CHUNK, CHUNK), lambda hd, c: (0, cc(c), 0)),
        bias=pl.BlockSpec((1, CHUNK), lambda hd, c: (0, 0)),
        hvec=pl.BlockSpec((1, ML_DH), lambda hd, c: (0, hd)),
        cs=pl.BlockSpec((bsz, 1, ML_DH, ML_DH), lambda hd, c: (0, hd * nc + cc(c), 0, 0)),
        ns=pl.BlockSpec((bsz, 1, 1, ML_DH), lambda hd, c: (0, hd * nc + cc(c), 0, 0)),
        ms=pl.BlockSpec((bsz, 1, 1, CHUNK), lambda hd, c: (0, hd * nc + cc(c), 0, 0)),
        dgates=pl.BlockSpec((1, bsz, CHUNK, CHUNK), lambda hd, c: (hd, 0, cc(c), 0)),
    )


def _seq(a, bsz):
    return a.reshape(bsz, a.shape[0] // bsz, a.shape[1])


def _ml_chunk_fwd(qs, k, v, gates, b_gate, pre, z, nw, sk, bsz, nc):
    r = qs.shape[0]
    tp = r // bsz
    sp = _ml_chunk_specs(nc, False, bsz)

    def body(qs_all, k_all, v_all, gt_all, bg_ref, pre_all, z_all, nw_ref, sk_ref,
             h_all, yb_all, cs_all, ns_all, ms_all, c_sall, n_sall, m_sall):
        hd, c = pl.program_id(0), pl.program_id(1)

        @pl.when(c == 0)
        def _():
            c_sall[...] = jnp.zeros_like(c_sall)
            n_sall[...] = jnp.zeros_like(n_sall)
            m_sall[...] = jnp.zeros_like(m_sall)

        for bi in range(bsz):
            one(hd, c, qs_all.at[bi], k_all.at[bi], v_all.at[bi], gt_all.at[bi], bg_ref, pre_all.at[bi], z_all.at[bi],
                nw_ref, sk_ref, h_all.at[bi], yb_all.at[bi], cs_all.at[bi], ns_all.at[bi], ms_all.at[bi],
                c_sall.at[bi], n_sall.at[bi], m_sall.at[bi])

    def one(hd, c, qs_ref, k_ref, v_ref, gt_ref, bg_ref, pre_ref, z_ref, nw_ref, sk_ref,
            h_ref, yb_ref, cs_ref, ns_ref, ms_ref, c_s, n_s, m_s):
        cmat, nvec, m = c_s[...], n_s[...], m_s[...]
        cs_ref[0] = cmat
        ns_ref[0] = nvec
        ms_ref[0] = jnp.broadcast_to(m, (1, CHUNK))
        v_ = v_ref[...]
        co = _ml_core(gt_ref[...] + bg_ref[...], hd, c == 0, m, qs_ref[...], k_ref[...], v_, cmat, nvec)
        h = co["num"] / co["dd"]
        h_ref[...] = h
        hn, _ = _ml_headnorm(h)
        yb_ref[...] = _bf((hn * nw_ref[...] + sk_ref[...] * _silu(pre_ref[...])) * _silu(z_ref[...]))
        c_s[...] = co["decay"] * cmat + _dot(_bf(co["wk"]), v_, TN)
        n_s[...] = co["decay"] * nvec + jnp.sum(co["wk"], axis=0, keepdims=True)
        m_s[...] = co["m_new"]

    nst = ML_HEADS * nc
    h, yb, cs, ns, ms = _pc(
        body, name="ml_chunk_fwd", grid=(ML_HEADS, nc),
        in_specs=[sp["hrow"]] * 3 + [sp["gates"], sp["bias"], sp["hrow"], sp["hrow"], sp["hvec"], sp["hvec"]],
        out_specs=[sp["hrow"], sp["hrow"], sp["cs"], sp["ns"], sp["ms"]],
        out_shape=[jax.ShapeDtypeStruct((bsz, tp, ML_WIDTH), F32), jax.ShapeDtypeStruct((bsz, tp, ML_WIDTH), BF16),
                   jax.ShapeDtypeStruct((bsz, nst, ML_DH, ML_DH), F32),
                   jax.ShapeDtypeStruct((bsz, nst, 1, ML_DH), F32), jax.ShapeDtypeStruct((bsz, nst, 1, CHUNK), F32)],
        scratch=[pltpu.VMEM((bsz, ML_DH, ML_DH), F32), pltpu.VMEM((bsz, 1, ML_DH), F32),
                 pltpu.VMEM((bsz, 1, 1), F32)],
        vmem=12 << 20)(*[_seq(a, bsz) for a in (qs, k, v, gates)], b_gate, _seq(pre, bsz), _seq(z, bsz), nw, sk)
    return h.reshape(r, ML_WIDTH), yb.reshape(r, ML_WIDTH), cs, ns, ms


def _ml_chunk_bwd(dyb, qs, k, v, gates, b_gate, pre, z, nw, sk, h, cs, ns, ms, bsz, nc, dep=None):
    r = qs.shape[0]
    tp = r // bsz
    sp = _ml_chunk_specs(nc, True, bsz)

    def body(dy_all, qs_all, k_all, v_all, gt_all, bg_ref, pre_all, z_all, nw_ref, sk_ref, h_all, cs_all, ns_all,
             ms_all, dq_all, dk_all, dv_all, dz_all, dxc_all, dgt_all, dnw_ref, dsk_ref, dc_sall, dn_sall):
        hd, c = pl.program_id(0), pl.program_id(1)

        @pl.when(c == 0)
        def _():
            for ref in (dnw_ref, dsk_ref, dc_sall, dn_sall):
                ref[...] = jnp.zeros_like(ref)

        for bi in range(bsz):
            one(hd, c, dy_all.at[bi], qs_all.at[bi], k_all.at[bi], v_all.at[bi], gt_all.at[bi], bg_ref,
                pre_all.at[bi], z_all.at[bi], nw_ref, sk_ref, h_all.at[bi], cs_all.at[bi], ns_all.at[bi],
                ms_all.at[bi], dq_all.at[bi], dk_all.at[bi], dv_all.at[bi], dz_all.at[bi], dxc_all.at[bi],
                dgt_all.at[0, bi], dnw_ref, dsk_ref, dc_sall.at[bi], dn_sall.at[bi])

    def one(hd, c, dy_ref, qs_ref, k_ref, v_ref, gt_ref, bg_ref, pre_ref, z_ref, nw_ref, sk_ref, h_ref, cs_ref, ns_ref,
            ms_ref, dq_ref, dk_ref, dv_ref, dz_ref, dxc_ref, dgt_ref, dnw_ref, dsk_ref, dc_s, dn_s):

        qs, k, v = qs_ref[...], k_ref[...], v_ref[...]
        cmat, nvec, m = cs_ref[0], ns_ref[0], ms_ref[0][:, :1]
        co = _ml_core(gt_ref[...] + bg_ref[...], hd, c == nc - 1, m, qs, k, v, cmat, nvec)
        lane, row = co["lane"], co["row"]
        wt, wprev, s, cb, qf = co["wt"], co["wprev"], co["s"], co["cb"], co["qf"]
        h = h_ref[...]
        hn, rstd = _ml_headnorm(h)
        xc = _silu(pre_ref[...])
        zv = z_ref[...]
        nw, sk = nw_ref[...], sk_ref[...]
        dy = dy_ref[...]
        dz_ref[...] = _bf(dy * (hn * nw + sk * xc) * _dsilu(zv))
        do = dy * _silu(zv)
        dsk_ref[...] += jnp.sum(do * xc, axis=0, keepdims=True)
        dnw_ref[...] += jnp.sum(do * hn, axis=0, keepdims=True)
        dxc_ref[...] = do * sk
        dhn = do * nw
        dh = rstd * (dhn - jnp.mean(dhn, axis=1, keepdims=True) - hn * jnp.mean(dhn * hn, axis=1, keepdims=True))
        rinv = 1.0 / co["dd"]
        dnum = dh * rinv
        ddd = -jnp.sum(dh * h, axis=1, keepdims=True) * rinv
        den = co["den"]
        dden = jnp.where(jnp.abs(den) >= co["emt"], ddd * jnp.sign(den), 0.0)
        dnb = _bf(dnum)
        ds = _dot(dnb, v, NT) + dden
        dv = _dot(_bf(s), dnb, TN)
        dnw_ = _bf(dnum * wprev)
        dwn = dden * wprev
        dqs = _dot(dnw_, cb, NT) + dwn * nvec
        dc_out = _dot(qs, dnw_, TN)
        dn_out = jnp.sum(dwn * qf, axis=0, keepdims=True)
        dwprev = jnp.sum(dnum * co["qc"], axis=1, keepdims=True) + dden * co["qn"]
        ds0 = _bf(ds * wt)
        ddm = ds * s
        dqs = dqs + _dot(ds0, k)
        dk = _dot(ds0, qs, TN)
        colc = jnp.sum(ddm.T, axis=1, keepdims=True)
        dbc = dwprev * wprev + jnp.sum(ddm, axis=1, keepdims=True) - colc
        dig = colc
        dcn, dnn = dc_s[...], dn_s[...]
        dcb = _bf(dcn)
        decay, e, kf, wk = co["decay"], co["e"], co["kf"], co["wk"]
        ddecay = (jnp.sum(jnp.sum(dcn * cmat, axis=1, keepdims=True), axis=0, keepdims=True)
                  + jnp.sum(dnn * nvec, axis=1, keepdims=True))
        dwk = _dot(v, dcb, NT) + dnn
        dv = dv + _dot(_bf(wk), dcb)
        dk = dk + e * dwk
        dg = jnp.sum(dwk * kf, axis=1, keepdims=True) * e
        dblast = ddecay * decay + jnp.sum(dg, axis=0, keepdims=True)
        dbc = dbc - dg + jnp.where(row[:, :1] == CHUNK - 1, dblast, 0.0)
        dig = dig + dg
        dc_s[...] = decay * dcn + dc_out
        dn_s[...] = decay * dnn + dn_out
        dlf = _cumsum_rows(jnp.broadcast_to(dbc, (CHUNK, CHUNK)), row, rev=True)[:, :1]
        dfp = dlf * (1.0 - jax.nn.sigmoid(co["fpc"]))
        dig = jnp.where(co["valid"], dig, 0.0)
        dfp = jnp.where(co["valid"], dfp, 0.0)
        dgt_ref[...] = jnp.where(lane == hd, dig, 0.0) + jnp.where(lane == hd + ML_HEADS, dfp, 0.0)
        dq_ref[...] = _bf(dqs * ML_SCALE)
        dk_ref[...] = _bf(dk)
        dv_ref[...] = _bf(dv)

    ob = jax.ShapeDtypeStruct((bsz, tp, ML_WIDTH), BF16)
    dq, dk, dv, dz, dxc, dgt, dnw, dsk = _pc(
        body, name="ml_chunk_bwd", grid=(ML_HEADS, nc),
        in_specs=[sp["hrow"]] * 4 + [sp["gates"], sp["bias"], sp["hrow"], sp["hrow"], sp["hvec"], sp["hvec"],
                                     sp["hrow"], sp["cs"], sp["ns"], sp["ms"]],
        out_specs=[sp["hrow"]] * 5 + [sp["dgates"], sp["hvec"], sp["hvec"]],
        out_shape=[ob, ob, ob, ob, jax.ShapeDtypeStruct((bsz, tp, ML_WIDTH), F32),
                   jax.ShapeDtypeStruct((ML_HEADS, bsz, tp, CHUNK), F32),
                   jax.ShapeDtypeStruct((1, ML_WIDTH), F32), jax.ShapeDtypeStruct((1, ML_WIDTH), F32)],
        scratch=[pltpu.VMEM((bsz, ML_DH, ML_DH), F32), pltpu.VMEM((bsz, 1, ML_DH), F32)], vmem=16 << 20, dep=dep,
    )(*[_seq(a, bsz) for a in (dyb, qs, k, v, gates)], b_gate, _seq(pre, bsz), _seq(z, bsz), nw, sk, _seq(h, bsz),
      cs, ns, ms)
    return (dq.reshape(r, ML_WIDTH), dk.reshape(r, ML_WIDTH), dv.reshape(r, ML_WIDTH), dz.reshape(r, ML_WIDTH),
            dxc.reshape(r, ML_WIDTH), dgt.reshape(ML_HEADS, r, CHUNK), dnw, dsk)


def _ml_pre_bwd(dq, dk, dv, dgates, dxc_skip, pre, x, q, k, v, wq, wk, wv, wgq, wgk, wgv, bsz, nc):
    r = x.shape[0]
    tr = _pick(r, (256, 128))
    nt = r // tr
    hrow = pl.BlockSpec((tr, ML_DH), lambda h, i: (i, h))
    wexp = pl.BlockSpec((1, ML_LB, CHUNK, CHUNK), lambda h, i: (h, 0, 0, 0))
    wcmp = pl.BlockSpec((1, ML_DH, CHUNK), lambda h, i: (h, 0, 0))
    wg = pl.BlockSpec((ML_DH, CHUNK), lambda h, i: (h, 0))
    dgs = pl.BlockSpec((ML_HEADS, tr, CHUNK), lambda h, i: (0, i, 0))
    bgs = pl.BlockSpec((1, 1, CHUNK), lambda h, i: (h, 0, 0))

    def body(dq_ref, dk_ref, dv_ref, dg_ref, dxs_ref, pre_ref, x_ref, q_ref, k_ref, v_ref, wq_ref, wk_ref, wv_ref,
             gq_ref, gk_ref, gv_ref, dpre_ref, dxv_ref, cq_ref, ck_ref, cv_ref, dgq_ref, dgk_ref, dgv_ref, dbg_ref,
             dwq_ref, dwk_ref, dwv_ref):
        i = pl.program_id(1)

        @pl.when(i == 0)
        def _():
            for ref in (dwq_ref, dwk_ref, dwv_ref, dgq_ref, dgk_ref, dgv_ref, dbg_ref):
                ref[...] = jnp.zeros_like(ref)

        dgt = dg_ref[0]
        for j in range(1, ML_HEADS):
            dgt = dgt + dg_ref[j]
        dbg_ref[0] += jnp.sum(dgt, axis=0, keepdims=True)
        dgb = _bf(dgt)
        dqt = _bf(dq_ref[...].astype(F32) + _dot(dgb, gq_ref[...], NT))
        dkt = _bf(dk_ref[...].astype(F32) + _dot(dgb, gk_ref[...], NT))
        dvt = _bf(dv_ref[...].astype(F32) + _dot(dgb, gv_ref[...], NT))
        dgq_ref[...] += _dot(q_ref[...], dgb, TN)
        dgk_ref[...] += _dot(k_ref[...], dgb, TN)
        dgv_ref[...] += _dot(v_ref[...], dgb, TN)
        prev = pre_ref[...]
        xcb = _bf(_silu(prev))
        xb = _bf(x_ref[...])
        for j in range(ML_LB):
            sl = slice(j * CHUNK, (j + 1) * CHUNK)
            dwq_ref[j] += _dot(xcb[:, sl], dqt[:, sl], TN)
            dwk_ref[j] += _dot(xcb[:, sl], dkt[:, sl], TN)
            dwv_ref[j] += _dot(xb[:, sl], dvt[:, sl], TN)
        dxc = _headwise_dot(dqt, wq_ref, NT) + _headwise_dot(dkt, wk_ref, NT) + dxs_ref[...]
        dpre_ref[...] = dxc * _dsilu(prev)
        dxv_ref[...] = _headwise_dot(dvt, wv_ref, NT)

        @pl.when(i == nt - 1)
        def _():
            rr = lax.broadcasted_iota(jnp.int32, (CHUNK, CHUNK), 0)
            cc = lax.broadcasted_iota(jnp.int32, (CHUNK, CHUNK), 1)
            diag = rr // QKV_BLOCK == cc // QKV_BLOCK
            fold = jnp.where(rr % QKV_BLOCK == cc, 1.0, 0.0).astype(F32)
            for src, dst in ((dwq_ref, cq_ref), (dwk_ref, ck_ref), (dwv_ref, cv_ref)):
                for j in range(ML_LB):
                    dst[0, j * CHUNK:(j + 1) * CHUNK, :] = jnp.dot(
                        jnp.where(diag, src[j], 0.0), fold, precision=HI, preferred_element_type=F32)

    f = jax.ShapeDtypeStruct((r, ML_WIDTH), F32)
    wc = jax.ShapeDtypeStruct((ML_HEADS, ML_DH, CHUNK), F32)
    wgs = jax.ShapeDtypeStruct((ML_WIDTH, CHUNK), F32)
    return _pc(body, name="ml_pre_bwd", grid=(ML_HEADS, nt),
               in_specs=[hrow, hrow, hrow, dgs, hrow, hrow, hrow, hrow, hrow, hrow, wexp, wexp, wexp, wg, wg, wg],
               out_specs=[hrow, hrow, wcmp, wcmp, wcmp, wg, wg, wg, bgs],
               out_shape=[f, f, wc, wc, wc, wgs, wgs, wgs, jax.ShapeDtypeStruct((ML_HEADS, 1, CHUNK), F32)],
               scratch=[pltpu.VMEM((ML_LB, CHUNK, CHUNK), F32)] * 3,
               vmem=8 << 20)(dq, dk, dv, dgates, dxc_skip, pre, x, q, k, v, wq, wk, wv, wgq, wgk, wgv)


def _pad_lanes(w):
    return jnp.pad(w, ((0, 0), (0, CHUNK - w.shape[1])))


def _ml_weights(conv_w, conv_b, wq, wk, wv, w_gate, b_gate, norm_w, skip):
    return dict(
        conv_w=conv_w, conv_b=conv_b,
        wq=_bf(_headwise_expand(wq)), wk=_bf(_headwise_expand(wk)), wv=_bf(_headwise_expand(wv)),
        wgq=_bf(_pad_lanes(w_gate[:ML_WIDTH])), wgk=_bf(_pad_lanes(w_gate[ML_WIDTH:2 * ML_WIDTH])),
        wgv=_bf(_pad_lanes(w_gate[2 * ML_WIDTH:])), b_gate=_pad_lanes(b_gate.reshape(1, -1)),
        norm=norm_w.reshape(1, ML_WIDTH), skip=skip.reshape(1, ML_WIDTH))


def _ml_layer_fwd(x, z, w, bsz, nc):
    pre = _conv_fwd(x, w["conv_w"], w["conv_b"], bsz, nc, name="ml_conv")
    q, qs, k, v, gates = _ml_pre(pre, x, w["wq"], w["wk"], w["wv"], w["wgq"], w["wgk"], w["wgv"], bsz, nc)
    h, yb, cs, ns, ms = _ml_chunk_fwd(qs, k, v, gates, w["b_gate"], pre, z, w["norm"], w["skip"], bsz, nc)
    return yb, dict(pre=pre, q=q, qs=qs, k=k, v=v, gates=gates, h=h, cs=cs, ns=ns, ms=ms)


def _ml_layer_bwd(dyb, x, z, sv, w, bsz, nc, dep=None):
    dq, dk, dv, dz, dxc, dgates, dnw, dsk = _ml_chunk_bwd(
        dyb, sv["qs"], sv["k"], sv["v"], sv["gates"], w["b_gate"], sv["pre"], z, w["norm"], w["skip"], sv["h"],
        sv["cs"], sv["ns"], sv["ms"], bsz, nc, dep=dep)
    dpre, dxv, dwq, dwk, dwv, dgq, dgk, dgv, dbg = _ml_pre_bwd(
        dq, dk, dv, dgates, dxc, sv["pre"], x, sv["q"], sv["k"], sv["v"], w["wq"], w["wk"], w["wv"], w["wgq"],
        w["wgk"], w["wgv"], bsz, nc)
    dx, dcw, dcb = _conv_bwd(dpre, x, w["conv_w"], bsz, nc, name="ml_conv_bwd", add=dxv)
    ng = 2 * ML_HEADS
    grads = dict(
        ml_conv_w=dcw, ml_conv_b=dcb, ml_wq=_headwise_extract(dwq), ml_wk=_headwise_extract(dwk),
        ml_wv=_headwise_extract(dwv), ml_w_gate=jnp.concatenate([dgq[:, :ng], dgk[:, :ng], dgv[:, :ng]], axis=0),
        ml_b_gate=dbg[0][:, :ng], ml_norm=dnw, ml_skip=dsk)
    return dx, dz, grads


HI = lax.Precision.HIGHEST


def _softplus(x):
    return jnp.maximum(x, 0.0) + jnp.log(1.0 + jnp.exp(-jnp.abs(x)))


def _lane_cumsum(x, lane, rev=False):
    del lane
    return _dot_terms(x, _tri(not rev), NN, exact_rhs=True, terms=3)


def _dot_terms(lhs, rhs, dims, *, exact_rhs, terms):
    x = lhs if exact_rhs else rhs
    sel = _bf(rhs if exact_rhs else lhs)
    acc = None
    for _ in range(terms):
        piece = _bf(x)
        part = _dot(piece, sel, dims) if exact_rhs else _dot(sel, piece, dims)
        acc = part if acc is None else acc + part
        x = x - piece.astype(F32)
    return acc


def _head_sum_matrix():
    r = lax.broadcasted_iota(jnp.int32, (SSD_HPG, SSD_GW), 0)
    l = lax.broadcasted_iota(jnp.int32, (SSD_HPG, SSD_GW), 1)
    return jnp.where(l // SSD_P == r, 1.0, 0.0).astype(F32)


def _ssd_dt_specs(nc):
    return dict(rows=pl.BlockSpec((1, SSD_HEADS, CHUNK), lambda b, c: (b, 0, c)),
                col=pl.BlockSpec((SSD_HEADS, 1), lambda b, c: (0, 0)),
                acc=pl.BlockSpec((SSD_HEADS, CHUNK), lambda b, c: (0, 0)))


def _ssd_dt_valid(c):
    lane = lax.broadcasted_iota(jnp.int32, (SSD_HEADS, CHUNK), 1)
    return jnp.logical_or(c > 0, lane >= PAD_ROWS)


def _ssd_dt_prep(dt_raw, dt_bias, a_log, bsz, nc):
    sp = _ssd_dt_specs(nc)

    def body(raw_ref, b_ref, al_ref, dt_ref, cum_ref):
        dt = jnp.where(_ssd_dt_valid(pl.program_id(1)), _softplus(raw_ref[0] + b_ref[...]), 0.0)
        dt_ref[0] = dt
        cum_ref[0] = _lane_cumsum(dt * -jnp.exp(al_ref[...]), None)

    o = jax.ShapeDtypeStruct(dt_raw.shape, F32)
    return _pc(body, name="ssd_dt_prep", grid=(bsz, nc), in_specs=[sp["rows"], sp["col"], sp["col"]],
               out_specs=[sp["rows"], sp["rows"]], out_shape=[o, o], vmem=1 << 20)(dt_raw, dt_bias, a_log)


def _ssd_dt_post(dcum, ddt, dt_raw, dt_bias, a_log, bsz, nc):
    sp = _ssd_dt_specs(nc)

    def body(dcum_ref, ddt_ref, raw_ref, b_ref, al_ref, out_ref, dbias_ref, dal_ref):
        b, c = pl.program_id(0), pl.program_id(1)

        @pl.when((b == 0) & (c == 0))
        def _():
            dbias_ref[...] = jnp.zeros_like(dbias_ref)
            dal_ref[...] = jnp.zeros_like(dal_ref)

        valid = _ssd_dt_valid(c)
        pre = raw_ref[0] + b_ref[...]
        dt = jnp.where(valid, _softplus(pre), 0.0)
        a = -jnp.exp(al_ref[...])
        dda = _lane_cumsum(dcum_ref[0], None, rev=True)
        ddt_raw = jnp.where(valid, ddt_ref[0] + dda * a, 0.0) * jax.nn.sigmoid(pre)
        out_ref[0] = ddt_raw
        dbias_ref[...] += jnp.sum(ddt_raw, axis=1, keepdims=True)
        dal_ref[...] += jnp.sum(dda * dt, axis=1, keepdims=True) * a

    acc = jax.ShapeDtypeStruct((SSD_HEADS, CHUNK), F32)
    return _pc(body, name="ssd_dt_post", grid=(bsz, nc),
               in_specs=[sp["rows"], sp["rows"], sp["rows"], sp["col"], sp["col"]],
               out_specs=[sp["rows"], sp["acc"], sp["acc"]],
               out_shape=[jax.ShapeDtypeStruct(dt_raw.shape, F32), acc, acc], vmem=1 << 20,
               )(dcum, ddt, dt_raw, dt_bias, a_log)


def _ssd_core(xs, bm, cm, dt, cum):
    sq = (CHUNK, CHUNK)
    lane8 = lax.broadcasted_iota(jnp.int32, (SSD_HPG, CHUNK), 1)
    lane = lax.broadcasted_iota(jnp.int32, sq, 1)
    row = lax.broadcasted_iota(jnp.int32, sq, 0)
    low = lane < SSD_P
    cb = _dot(_bf(cm), _bf(bm), NT)
    heads = []
    for r in range(SSD_HPG):
        rowb = jnp.broadcast_to(cum[r:r + 1, :], sq)
        colb = rowb.T
        seg = jnp.exp(jnp.where(lane <= row, colb - rowb, NEG))
        dtrow = jnp.broadcast_to(dt[r:r + 1, :], sq)
        lastb = colb[CHUNK - 1:CHUNK, :]
        heads.append(dict(seg=seg, dtrow=dtrow, w=cb * seg * dtrow, ecol=jnp.exp(colb),
                          dec=jnp.exp(lastb - colb) * dtrow.T, elast=jnp.exp(lastb)))

    def pairs(key):
        return jnp.concatenate([jnp.where(low[:heads[0][key].shape[0]], heads[2 * j][key], heads[2 * j + 1][key])
                                for j in range(SSD_HPG // 2)], axis=1)

    return dict(lane8=lane8, low=low, dt=dt, cum=cum, cb=cb, heads=heads,
                expc=pairs("ecol"), dec=pairs("dec"), elast=pairs("elast"))


def _ssd_specs(nc, rev, bsz):
    def cc(c):
        return (nc - 1 - c) if rev else c

    return dict(
        wide=pl.BlockSpec((bsz, CHUNK, SSD_GW), lambda g, c: (0, cc(c), g)),
        narrow=pl.BlockSpec((bsz, CHUNK, SSD_N), lambda g, c: (0, cc(c), g)),
        dtT=pl.BlockSpec((bsz, SSD_HPG, CHUNK), lambda g, c: (0, g, cc(c))),
        hcol=pl.BlockSpec((SSD_HPG, 1), lambda g, c: (g, 0)),
        hacc=pl.BlockSpec((SSD_HPG, CHUNK), lambda g, c: (g, 0)),
        gvec=pl.BlockSpec((1, SSD_GW), lambda g, c: (0, g)),
        state=pl.BlockSpec((bsz, 1, SSD_N, SSD_GW), lambda g, c: (0, g * nc + cc(c), 0, 0)),
    )


def _ssd_chunk_fwd(xs_pre, bm_pre, cm_pre, dt, cum, d_exp, z, gnorm, bsz, nc):
    tp = xs_pre.shape[1]
    sp = _ssd_specs(nc, False, bsz)

    def body(xs_all, bm_all, cm_all, dt_all, cum_all, d_ref, z_all, gn_ref, y_all, yn_all, st_all, st_sall):
        @pl.when(pl.program_id(1) == 0)
        def _():
            st_sall[...] = jnp.zeros_like(st_sall)

        for bi in range(bsz):
            one(xs_all.at[bi], bm_all.at[bi], cm_all.at[bi], dt_all.at[bi], cum_all.at[bi], d_ref, z_all.at[bi],
                gn_ref, y_all.at[bi], yn_all.at[bi], st_all.at[bi], st_sall.at[bi])

    def one(xs_ref, bm_ref, cm_ref, dt_ref, cum_ref, d_ref, z_ref, gn_ref, y_ref, yn_ref, st_ref, st_s):
        state = st_s[...]
        st_ref[0] = state
        xs, bm, cm = _silu(xs_ref[...]), _silu(bm_ref[...]), _silu(cm_ref[...])
        co = _ssd_core(xs, bm, cm, dt_ref[...], cum_ref[...])
        low, hd = co["low"], co["heads"]
        ys = []
        for j in range(SSD_HPG // 2):
            xp = xs[:, j * CHUNK:(j + 1) * CHUNK]
            lhs = jnp.concatenate([hd[2 * j]["w"], hd[2 * j + 1]["w"]], axis=1)
            rhs = jnp.concatenate([jnp.where(low, xp, 0.0), jnp.where(low, 0.0, xp)], axis=0)
            ys.append(_dot(_bf(lhs), _bf(rhs)))
        cmb = _bf(cm)
        y = jnp.concatenate(ys, axis=1) + co["expc"] * _dot(cmb, _bf(state)) + d_ref[...] * xs
        y_ref[...] = y
        yg = y * _silu(z_ref[...])
        rstd = lax.rsqrt(jnp.mean(yg * yg, axis=1, keepdims=True) + NORM_EPS)
        yn_ref[...] = _bf(yg * rstd * gn_ref[...])
        st_s[...] = co["elast"] * state + _dot(_bf(bm), _bf(xs * co["dec"]), TN)

    return _pc(body, name="ssd_chunk_fwd", grid=(SSD_GROUPS, nc),
               in_specs=[sp["wide"], sp["narrow"], sp["narrow"], sp["dtT"], sp["dtT"], sp["gvec"], sp["wide"],
                         sp["gvec"]],
               out_specs=[sp["wide"], sp["wide"], sp["state"]],
               out_shape=[jax.ShapeDtypeStruct((bsz, tp, SSD_INNER), F32),
                          jax.ShapeDtypeStruct((bsz, tp, SSD_INNER), BF16),
                          jax.ShapeDtypeStruct((bsz, SSD_GROUPS * nc, SSD_N, SSD_GW), F32)],
               scratch=[pltpu.VMEM((bsz, SSD_N, SSD_GW), F32)], vmem=12 << 20,
               )(xs_pre, bm_pre, cm_pre, dt, cum, d_exp, z, gnorm)


def _ssd_chunk_bwd(dyn, xs_pre, bm_pre, cm_pre, dt, cum, d_exp, z, gnorm, y, states, bsz, nc):
    tp = xs_pre.shape[1]
    sp = _ssd_specs(nc, True, bsz)

    def body(dyn_all, xs_all, bm_all, cm_all, dt_all, cum_all, d_ref, z_all, gn_ref, y_all, st_all,
             dxs_all, dbm_all, dcm_all, dz_all, dcum_all, ddt_all, dgn_ref, dd_ref, ds_sall):
        @pl.when(pl.program_id(1) == 0)
        def _():
            for ref in (dgn_ref, dd_ref, ds_sall):
                ref[...] = jnp.zeros_like(ref)

        for bi in range(bsz):
            one(dyn_all.at[bi], xs_all.at[bi], bm_all.at[bi], cm_all.at[bi], dt_all.at[bi], cum_all.at[bi], d_ref,
                z_all.at[bi], gn_ref, y_all.at[bi], st_all.at[bi], dxs_all.at[bi], dbm_all.at[bi], dcm_all.at[bi],
                dz_all.at[bi], dcum_all.at[bi], ddt_all.at[bi], dgn_ref, dd_ref, ds_sall.at[bi])

    def one(dyn_ref, xs_ref, bm_ref, cm_ref, dt_ref, cum_ref, d_ref, z_ref, gn_ref, y_ref, st_ref,
            dxs_ref, dbm_ref, dcm_ref, dz_ref, dcum_ref, ddt_ref, dgn_ref, dd_ref, ds_s):
        xs_p, bm_p, cm_p = xs_ref[...], bm_ref[...], cm_ref[...]
        xs, bm, cm = _silu(xs_p), _silu(bm_p), _silu(cm_p)
        state = st_ref[0]
        co = _ssd_core(xs, bm, cm, dt_ref[...], cum_ref[...])
        low, hd, lane8, cb = co["low"], co["heads"], co["lane8"], co["cb"]
        dt, cum = co["dt"], co["cum"]
        sub8 = lax.broadcasted_iota(jnp.int32, (SSD_HPG, CHUNK), 0)
        eh = _head_sum_matrix()

        def head_rows(full):
            return _dot_terms(eh, full, NT, exact_rhs=False, terms=2)

        def head_col(vec):
            return jnp.sum(eh * vec, axis=1, keepdims=True)

        yv, zv, gn = y_ref[...], z_ref[...], gn_ref[...]
        sz = _silu(zv)
        yg = yv * sz
        rstd = lax.rsqrt(jnp.mean(yg * yg, axis=1, keepdims=True) + NORM_EPS)
        yh = yg * rstd
        dyn = dyn_ref[...]
        dgn_ref[...] += jnp.sum(dyn * yh, axis=0, keepdims=True)
        dyh = dyn * gn
        dyg = rstd * (dyh - yh * jnp.mean(dyh * yh, axis=1, keepdims=True))
        dz_ref[...] = _bf(dyg * yv * _dsilu(zv))
        dy = dyg * sz
        dxs = dy * d_ref[...]
        dd_ref[...] += head_col(jnp.sum(dy * xs, axis=0, keepdims=True))
        cmb, bmb, stb = _bf(cm), _bf(bm), _bf(state)
        ysv = _dot(cmb, stb)
        expc = co["expc"]
        dys = _bf(dy * expc)
        dcum = head_rows(dy * ysv * expc)
        dcm = _dot(dys, stb, NT)
        dstate_out = _dot(cmb, dys, TN)
        dcb = jnp.zeros((CHUNK, CHUNK), F32)
        ddt = jnp.zeros((SSD_HPG, CHUNK), F32)
        dxs_pairs = []
        for j in range(SSD_HPG // 2):
            sl = slice(j * CHUNK, (j + 1) * CHUNK)
            dyp, xp = dy[:, sl], _bf(xs[:, sl])
            lhs = _bf(jnp.concatenate([hd[2 * j]["w"], hd[2 * j + 1]["w"]], axis=1))
            both = _dot(lhs, _bf(dyp), TN)
            dxs_pairs.append(jnp.where(low, both[:CHUNK], both[CHUNK:]))
            for q, msk in ((2 * j, low), (2 * j + 1, jnp.logical_not(low))):
                h = hd[q]
                dw = _dot(_bf(jnp.where(msk, dyp, 0.0)), xp, NT)
                dcb = dcb + dw * h["seg"] * h["dtrow"]
                e_ = dw * h["w"]
                dcum_r = jnp.sum(e_.T, axis=0, keepdims=True) - jnp.sum(e_, axis=0, keepdims=True)
                ddt_r = jnp.sum(dw * cb * h["seg"], axis=0, keepdims=True)
                dcum = dcum + jnp.where(sub8 == q, dcum_r, 0.0)
                ddt = ddt + jnp.where(sub8 == q, ddt_r, 0.0)
        dxs = dxs + jnp.concatenate(dxs_pairs, axis=1)
        dcbb = _bf(dcb)
        dcm = dcm + _dot(dcbb, bmb)
        dbm = _dot(dcbb, cmb, TN)
        dsn = ds_s[...]
        dsb = _bf(dsn)
        dec = co["dec"]
        dbm = dbm + _dot(_bf(xs * dec), dsb, NT)
        dxd = _dot(bmb, dsb)
        dxs = dxs + dxd * dec
        ddec = head_rows(dxd * xs)
        last = cum[:, CHUNK - 1:CHUNK]
        erow = jnp.exp(last - cum)
        ddt = ddt + ddec * erow
        dla = ddec * erow * dt
        dlast = (jnp.sum(dla, axis=1, keepdims=True)
                 + head_col(jnp.sum(dsn * state, axis=0, keepdims=True)) * jnp.exp(last))
        dcum_ref[...] = dcum - dla + jnp.where(lane8 == CHUNK - 1, dlast, 0.0)
        ddt_ref[...] = ddt
        ds_s[...] = co["elast"] * dsn + dstate_out
        dxs_ref[...] = dxs * _dsilu(xs_p)
        dbm_ref[...] = dbm * _dsilu(bm_p)
        dcm_ref[...] = dcm * _dsilu(cm_p)

    st = jax.ShapeDtypeStruct
    hacc = st((SSD_HEADS, CHUNK), F32)
    return _pc(body, name="ssd_chunk_bwd", grid=(SSD_GROUPS, nc),
               in_specs=[sp["wide"], sp["wide"], sp["narrow"], sp["narrow"], sp["dtT"], sp["dtT"], sp["gvec"],
                         sp["wide"], sp["gvec"], sp["wide"], sp["state"]],
               out_specs=[sp["wide"], sp["narrow"], sp["narrow"], sp["wide"], sp["dtT"], sp["dtT"], sp["gvec"],
                          sp["hacc"]],
               out_shape=[st((bsz, tp, SSD_INNER), F32), st((bsz, tp, SSD_BC), F32), st((bsz, tp, SSD_BC), F32),
                          st((bsz, tp, SSD_INNER), BF16), st((bsz, SSD_HEADS, tp), F32),
                          st((bsz, SSD_HEADS, tp), F32), st((1, SSD_INNER), F32), hacc],
               scratch=[pltpu.VMEM((bsz, SSD_N, SSD_GW), F32)], vmem=20 << 20,
               )(dyn, xs_pre, bm_pre, cm_pre, dt, cum, d_exp, z, gnorm, y, states)


SSD_BC = SSD_GROUPS * SSD_N


def _ssd_weights(conv_w, conv_b, dt_bias, a_log, d, gnorm):
    cuts = (0, SSD_INNER, SSD_INNER + SSD_BC, SSD_INNER + 2 * SSD_BC)
    return dict(
        conv_w=[conv_w[:, cuts[i]:cuts[i + 1]] for i in range(3)],
        conv_b=[conv_b[cuts[i]:cuts[i + 1]] for i in range(3)],
        dt_bias=dt_bias.reshape(SSD_HEADS, 1), a_log=a_log.reshape(SSD_HEADS, 1),
        d_exp=jnp.repeat(d.reshape(SSD_HEADS), SSD_P).reshape(1, SSD_INNER), gnorm=gnorm.reshape(1, SSD_INNER))


def _ssd_layer_fwd(z, xs_in, bm_in, cm_in, dt_rows, w, bsz, nc):
    pres = [_conv_fwd(a, w["conv_w"][i], w["conv_b"][i], bsz, nc, name=f"ssd_conv{i}")
            for i, a in enumerate((xs_in, bm_in, cm_in))]
    def seq(a):
        return a.reshape(bsz, nc * CHUNK, a.shape[-1])

    dt_t = jnp.swapaxes(seq(dt_rows)[:, :, :SSD_HEADS], 1, 2)
    dt, cum = _ssd_dt_prep(dt_t, w["dt_bias"], w["a_log"], bsz, nc)
    y, yn, states = _ssd_chunk_fwd(seq(pres[0]), seq(pres[1]), seq(pres[2]), dt, cum, w["d_exp"], seq(z),
                                   w["gnorm"], bsz, nc)
    return yn.reshape(-1, SSD_INNER), dict(pres=pres, dt_t=dt_t, dt=dt, cum=cum, y=y, states=states)


def _ssd_layer_bwd(dyn, z, xs_in, bm_in, cm_in, sv, w, bsz, nc):
    pres = sv["pres"]

    def seq(a):
        return a.reshape(bsz, nc * CHUNK, a.shape[-1])

    def rows(a):
        return a.reshape(-1, a.shape[-1])

    dxs_p, dbm_p, dcm_p, dz, dcum, ddt_direct, dgn, dd = _ssd_chunk_bwd(
        seq(dyn), seq(pres[0]), seq(pres[1]), seq(pres[2]), sv["dt"], sv["cum"], w["d_exp"], seq(z), w["gnorm"],
        sv["y"], sv["states"], bsz, nc)
    ddt_t, dbias, dal = _ssd_dt_post(dcum, ddt_direct, sv["dt_t"], w["dt_bias"], w["a_log"], bsz, nc)
    dz = rows(dz)
    outs = [_conv_bwd(rows(dp), a, w["conv_w"][i], bsz, nc, name=f"ssd_conv_bwd{i}")
            for i, (dp, a) in enumerate(((dxs_p, xs_in), (dbm_p, bm_in), (dcm_p, cm_in)))]
    ddt = _bf(_pad_lanes(rows(jnp.swapaxes(ddt_t, 1, 2))))
    grads = dict(
        ssd_conv_w=jnp.concatenate([o[1] for o in outs], axis=1),
        ssd_conv_b=jnp.concatenate([o[2] for o in outs], axis=1),
        ssd_dt_bias=dbias[:, 0], ssd_a_log=dal[:, 0], ssd_d=dd[:, 0], ssd_gnorm=dgn)
    return dz, outs[0][0], outs[1][0], outs[2][0], ddt, grads


WNAMES = ("meta_tokens", "ab_norm", "ab_w_in", "s5_lambda_re", "s5_lambda_im", "s5_log_dt", "s5_b_re", "s5_b_im",
          "s5_c_re", "s5_c_im", "s5_d", "s5_glu_w", "s5_glu_b", "ml_conv_w", "ml_conv_b", "ml_wq", "ml_wk", "ml_wv",
          "ml_w_gate", "ml_b_gate", "ml_norm", "ml_skip", "ab_w_out", "ssd_norm", "ssd_w_in", "ssd_conv_w",
          "ssd_conv_b", "ssd_dt_bias", "ssd_a_log", "ssd_d", "ssd_gnorm", "ssd_w_out", "final_norm")
SHARD_AXIS = dict(meta_tokens=1, ab_w_in=2, s5_glu_w=1, ml_conv_w=2, ml_wq=1, ml_wk=1, ml_wv=1, ml_w_gate=1,
                  ab_w_out=1, ssd_norm=1, ssd_w_in=2, ssd_conv_w=2, ssd_conv_b=1, ssd_gnorm=1, ssd_w_out=1)
BIG = ("ab_w_in", "s5_glu_w", "ab_w_out", "ssd_w_in", "ssd_w_out")
SMALL = tuple(n for n in WNAMES if n in SHARD_AXIS and n not in BIG)
REPL = tuple(n for n in WNAMES if n not in SHARD_AXIS)
PACK_ALIGN = 8 * 128


def _pack(arrs):
    lead = arrs[0][1]
    parts = []
    for a, nlead in arrs:
        f = a.reshape(a.shape[:nlead] + (-1,))
        parts.append(jnp.pad(f, [(0, 0)] * nlead + [(0, (-f.shape[-1]) % PACK_ALIGN)]))
    flat = jnp.concatenate(parts, axis=lead)
    return flat.reshape(flat.shape[:lead] + (-1, 128))


def _unpack(p, shapes):
    out, off = [], 0
    lead = p.shape[:-2]
    flat = p.reshape(lead + (-1,))
    for s in shapes:
        n = math.prod(s)
        out.append(flat[..., off:off + n].reshape(lead + tuple(s)))
        off += -(-n // PACK_ALIGN) * PACK_ALIGN
    return out


def _assemble(g, axis):
    m = jnp.moveaxis(g, 0, axis)
    return m.reshape(m.shape[:axis] + (m.shape[axis] * m.shape[axis + 1],) + m.shape[axis + 2:])


def _split(full, axis):
    s = full.shape
    m = full.reshape(s[:axis] + (N_DEV, s[axis] // N_DEV) + s[axis + 1:])
    return jnp.moveaxis(m, axis, 0)


def kernel(x, *rest):
    nw = len(WNAMES)
    w = dict(zip(WNAMES, rest[:nw]))
    loss_target = rest[nw]
    mom = dict(zip(WNAMES, rest[nw + 1:2 * nw + 1]))
    var = dict(zip(WNAMES, rest[2 * nw + 1:3 * nw + 1]))
    bsz = x.shape[0]
    nc = 1 + SEQ // CHUNK
    tp = nc * CHUNK

    local = {n: _bf(w[n][0]) for n in BIG}
    small_local = _pack([(w[n], 0) for n in SMALL])
    gs = _exchange_start([small_local], ["ag"], name="gather_s")
    ga = _exchange_start([local["ab_w_in"]], ["ag"], name="gather_a", dep=gs["token"], peers=SAME_CORE[1:])
    got_s = _exchange_wait(gs, ga["token"])

    def assemble_big(n, got):
        return _assemble(got[:, None], SHARD_AXIS[n])[0]

    full = {}
    for n, g in zip(SMALL, _unpack(got_s[0], [w[n].shape for n in SMALL])):
        full[n] = _assemble(g, SHARD_AXIS[n])[0] if n != "meta_tokens" else _assemble(g, SHARD_AXIS[n])
    for n in REPL:
        full[n] = w[n][0] if n != "final_norm" else w[n]
    glu_b = full["s5_glu_b"].reshape(1, S5_WIDTH)
    meta = jnp.broadcast_to(full["meta_tokens"][None], (bsz, N_META, D_MODEL))
    h0 = jnp.concatenate([jnp.zeros((bsz, PAD_ROWS, D_MODEL), F32), meta, x], axis=1).reshape(bsz * tp, D_MODEL)
    xn0 = _rms_fwd(h0, full["ab_norm"], name="rms0")
    s5p, s5_vjp = _s5_tables(*[full[n] for n in ("s5_lambda_re", "s5_lambda_im", "s5_log_dt", "s5_b_re", "s5_b_im",
                                                   "s5_c_re", "s5_c_im", "s5_d")])
    mlw = _ml_weights(*[full[n] for n in ("ml_conv_w", "ml_conv_b", "ml_wq", "ml_wk", "ml_wv", "ml_w_gate",
                                           "ml_b_gate", "ml_norm", "ml_skip")])
    got_a = _exchange_wait(ga, [xn0, s5p["wbr"], s5p["wcr"], s5p["pr"], mlw["wq"], mlw["wk"], mlw["wv"], mlw["wgq"]])
    fwd_a = _sibling_forward_start(got_a[0], name="gather_a2")
    got_a = [_sibling_forward_wait(fwd_a, fwd_a["token"])]
    gb = _exchange_start([local["s5_glu_w"], local["ab_w_out"]], ["ag", "ag"], name="gather_b", dep=got_a[0])
    gc = _exchange_start([local["ssd_w_in"], local["ssd_w_out"]], ["ag", "ag"], name="gather_c", dep=gb["token"])
    full["ab_w_in"] = assemble_big("ab_w_in", got_a[0])
    cuts0 = (0, S5_WIDTH, 2 * S5_WIDTH, 2 * S5_WIDTH + ML_WIDTH, 2 * (S5_WIDTH + ML_WIDTH))
    w_in0 = [full["ab_w_in"][:, cuts0[i]:cuts0[i + 1]] for i in range(4)]

    u, za, xb, zb = [_mm(xn0, wi, "NN", name=f"in0_{i}") for i, wi in enumerate(w_in0)]
    got_b = []

    def glu_w_after(scan_out):
        got_b.extend(_exchange_wait(gb, scan_out))
        return assemble_big("s5_glu_w", got_b[0])

    sv5 = _s5_layer_fwd(u, s5p, glu_w_after, bsz, nc)
    glu_w = assemble_big("s5_glu_w", got_b[0])
    w_out0 = assemble_big("ab_w_out", got_b[1])
    w_out0 = [w_out0[:S5_WIDTH], w_out0[S5_WIDTH:]]
    ya = _s5_post(sv5["y1"], sv5["glu_pre"], glu_b, za)
    yb, svm = _ml_layer_fwd(xb, zb, mlw, bsz, nc)
    h1 = _mm(ya, w_out0[0], "NN", name="out0_a", add=h0)
    h1 = _mm(yb, w_out0[1], "NN", name="out0_b", add=h1)
    got_c = _exchange_wait(gc, h1)
    w_in1, w_out1 = assemble_big("ssd_w_in", got_c[0]), assemble_big("ssd_w_out", got_c[1])
    cuts1 = (0, SSD_INNER, 2 * SSD_INNER, 2 * SSD_INNER + SSD_BC, 2 * SSD_INNER + 2 * SSD_BC)
    w_in1 = [w_in1[:, cuts1[i]:cuts1[i + 1]] for i in range(4)] + [_pad_lanes(w_in1[:, cuts1[4]:])]
    xn1 = _rms_fwd(h1, full["ssd_norm"], name="rms1")
    z1, xs_in, bm_in, cm_in, dt_rows = [_mm(xn1, wi, "NN", name=f"in1_{i}") for i, wi in enumerate(w_in1)]
    ssdw = _ssd_weights(*[full[n] for n in ("ssd_conv_w", "ssd_conv_b", "ssd_dt_bias", "ssd_a_log", "ssd_d",
                                             "ssd_gnorm")])
    yn, svs = _ssd_layer_fwd(z1, xs_in, bm_in, cm_in, dt_rows, ssdw, bsz, nc)
    h2 = _mm(yn, w_out1, "NN", name="out1", add=h1)
    loss_part, dh2, dfinal, dh2_b = _final_loss(h2, full["final_norm"], loss_target, bsz, nc)

    g = {"final_norm": dfinal}
    dyn = _mm(dh2_b, w_out1, "NT", name="d_out1")
    g["ssd_w_out"] = _mm(yn, dh2_b, "TN", name="dw_out1", out_dtype=BF16)
    dz1, dxs, dbm, dcm, ddt, gs = _ssd_layer_bwd(dyn, z1, xs_in, bm_in, cm_in, svs, ssdw, bsz, nc)
    g.update(gs)
    dps1 = (dz1, dxs, dbm, dcm, ddt)
    dxn1 = None
    for i, (dp, wi) in enumerate(zip(dps1, w_in1)):
        dxn1 = _mm(dp, wi, "NT", name=f"d_in1_{i}", add=dxn1)
    dw1 = [_mm(xn1, dp, "TN", name=f"dw_in1_{i}", out_dtype=BF16) for i, dp in enumerate(dps1)]
    g["ssd_w_in"] = jnp.concatenate(dw1[:4] + [dw1[4][:, :SSD_HEADS]], axis=1)

    def local_shape(n):
        return w[n].shape

    def slabs(n):
        gf = g[n].reshape((1,) + tuple(g[n].shape)) if n != "meta_tokens" else g[n]
        full_shape = tuple(d * (N_DEV if i == SHARD_AXIS[n] else 1) for i, d in enumerate(local_shape(n)))
        return _split(gf.reshape(full_shape), SHARD_AXIS[n])

    x1 = _exchange_start([slabs("ssd_w_in")[:, 0], slabs("ssd_w_out")[:, 0]], ["a2a", "a2a"], name="grads_1")
    dh1, g["ssd_norm"], dh1_b = _rms_bwd(h1, full["ssd_norm"], dxn1, dh2, name="rms1_bwd", dep=x1["token"])
    dya = _mm(dh1_b, w_out0[0], "NT", name="d_out0_a")
    dyb = _mm(dh1_b, w_out0[1], "NT", name="d_out0_b")
    g["ab_w_out"] = jnp.concatenate([_mm(ya, dh1_b, "TN", name="dw_out0_a", out_dtype=BF16),
                                     _mm(yb, dh1_b, "TN", name="dw_out0_b", out_dtype=BF16)], axis=0)
    du, dza, g5 = _s5_layer_bwd(dya, u, za, sv5, s5p, s5_vjp, glu_w, glu_b, bsz, nc)
    g.update(g5)
    x2 = _exchange_start([slabs("ab_w_out")[:, 0], _bf(slabs("s5_glu_w")[:, 0])], ["a2a", "a2a"], name="grads_2")
    dxb, dzb, gm = _ml_layer_bwd(dyb, xb, zb, svm, mlw, bsz, nc, dep=x2["token"])
    g.update(gm)
    dps0 = (du, dza, dxb, dzb)
    dw0 = [_mm(xn0, dp, "TN", name=f"dw_in0_{i}", out_dtype=BF16, tn=S5_WIDTH, slabs=True) for i, dp in enumerate(dps0)]
    dw_in0_slabs = jnp.concatenate(dw0, axis=0)
    x3 = _exchange_start([dw_in0_slabs], ["a2a"], name="grads_3")
    dxn0 = None
    for i, (dp, wi) in enumerate(zip(dps0, w_in0)):
        dxn0 = _mm(dp, wi, "NT", name=f"d_in0_{i}", add=dxn0, dep=x3["token"] if i == 0 else None)
    grad_x, d_chunk0, g["ab_norm"] = _rms_bwd_first(h0, full["ab_norm"], dxn0, dh1, bsz, nc, name="rms0_bwd")
    g["meta_tokens"] = jnp.sum(d_chunk0[:, PAD_ROWS:], axis=0)

    small_g = _pack([(slabs(n), 1) for n in SMALL])
    repl_g = _pack([(g[n], 0) for n in REPL])
    x4 = _exchange_start([small_g, repl_g, loss_part], ["a2a", "ag", "ag"], name="grads_4")

    def update_big(n, gp):
        return _adamw(w[n][0], mom[n][0], var[n][0], gp, name=f"adamw_{n}")

    res = {}
    ex1 = _exchange_wait(x1, x4["token"])
    res["ssd_w_in"], res["ssd_w_out"] = update_big("ssd_w_in", ex1[0]), update_big("ssd_w_out", ex1[1])
    ex2 = _exchange_wait(x2, res["ssd_w_out"][0])
    res["ab_w_out"], res["s5_glu_w"] = update_big("ab_w_out", ex2[0]), update_big("s5_glu_w", ex2[1])
    ex3 = _exchange_wait(x3, [res[n][0] for n in ("ssd_w_in", "ssd_w_out", "ab_w_out", "s5_glu_w")])
    res["ab_w_in"] = update_big("ab_w_in", ex3[0])
    ex4 = _exchange_wait(x4, res["ab_w_in"][0])
    loss = jnp.sum(ex4[2][:, 0, 0])
    for names, gp, tag in ((SMALL, ex4[0], "small"), (REPL, ex4[1], "repl")):
        shapes = [local_shape(n) for n in names]
        packs = [_pack([(d[n], 0) for n in names]) for d in (w, mom, var)]
        outs = _adamw(packs[0], packs[1], packs[2], gp, name=f"adamw_{tag}")
        for k, o in enumerate(outs):
            for n, a in zip(names, _unpack(o, shapes)):
                res.setdefault(n, [None] * 4)[k] = a
    outs = [loss, grad_x]
    for k in range(4):
        outs += [res[n][k].reshape(local_shape(n)) for n in WNAMES]
    return tuple(outs)
```

```python
import functools
import math

import jax
import jax.numpy as jnp
from jax import lax
from jax.experimental import pallas as pl
from jax.experimental.pallas import tpu as pltpu

F32 = jnp.float32
BF16 = jnp.bfloat16

D_MODEL = 2048
SEQ = 2048
N_META = 16
CHUNK = 128
PAD_ROWS = CHUNK - N_META
NORM_EPS = 1e-6
HEAD_NORM_EPS = 1e-5
S5_WIDTH = 1024
S5_GROUPS = 64
S5_GROUP_SIZE = 16
S5_STATE = 64
S5_GB = 8
S5_LANES = S5_GB * S5_STATE
ML_WIDTH = 3072
ML_HEADS = 8
ML_DH = 384
ML_CONV = 4
QKV_BLOCK = 4
SSD_INNER = 4096
SSD_HEADS = 64
SSD_P = 64
SSD_N = 128
SSD_GROUPS = 8
SSD_HPG = 8
SSD_GW = SSD_HPG * SSD_P
N_DEV = 8
ADAM_LR, ADAM_B1, ADAM_B2, ADAM_EPS, ADAM_WD, ADAM_STEP = 0.001, 0.9, 0.999, 1e-08, 0.01, 10
NEG = -1e30
VMEM_CAP = 60 * 1024 * 1024
MM_BLOCK_BUDGET = 22 * 1024 * 1024
MESH = pl.DeviceIdType.MESH

NN = (((1,), (0,)), ((), ()))
NT = (((1,), (1,)), ((), ()))
TN = (((0,), (0,)), ((), ()))


def _dot(a, b, dims=NN):
    return lax.dot_general(a, b, dims, preferred_element_type=F32)


def _bf(x):
    return x.astype(BF16)


def _pick(n, cands):
    for c in cands:
        if n % c == 0:
            return c
    return n


def _nbytes(shape, dtype):
    return math.prod(shape) * jnp.dtype(dtype).itemsize


ANY_SPEC = pl.BlockSpec(memory_space=pl.ANY)


def _pc(body, *, name, grid, in_specs, out_specs, out_shape, scratch=(), vmem=None, dep=None):
    limit = None if vmem is None else int(min(VMEM_CAP, max(32 * 1024 * 1024, 2 * vmem + (8 << 20))))
    n_in = len(in_specs)
    if dep is not None:
        inner = body

        def body(*refs):
            inner(*refs[:n_in], *refs[n_in + 1:])

        in_specs = list(in_specs) + [ANY_SPEC]
    call = pl.pallas_call(
        body, name=name, grid=grid, in_specs=in_specs, out_specs=out_specs, out_shape=out_shape,
        scratch_shapes=list(scratch),
        compiler_params=pltpu.CompilerParams(dimension_semantics=("arbitrary",) * len(grid), vmem_limit_bytes=limit))
    return call if dep is None else (lambda *args: call(*args, dep))


def _silu(x):
    return x * jax.nn.sigmoid(x)


def _dsilu(x):
    s = jax.nn.sigmoid(x)
    return s * (1.0 + x * (1.0 - s))


def _gelu_and_grad(x):
    c0 = math.sqrt(2.0 / math.pi)
    inner = c0 * (x + 0.044715 * x * x * x)
    t = jnp.tanh(inner)
    g = 0.5 * x * (1.0 + t)
    dg = 0.5 * (1.0 + t) + 0.5 * x * (1.0 - t * t) * c0 * (1.0 + 3 * 0.044715 * x * x)
    return g, dg


def _mm(a, b, mode, *, name, add=None, out_dtype=F32, tn=None, slabs=False, dep=None):
    if mode == "NN":
        (m, k), (k2, n) = a.shape, b.shape
    elif mode == "NT":
        (m, k), (n, k2) = a.shape, b.shape
    else:
        (k, m), (k2, n) = a.shape, b.shape
    assert k == k2, (a.shape, b.shape, mode)
    tm = _pick(m, (1088, 1024, 768, 512, 384, 256, 128))
    def block_bytes(tk, tn_):
        return (_nbytes((tm, tk), a.dtype) + _nbytes((tk, tn_), b.dtype) + _nbytes((tm, tn_), out_dtype)
                + (_nbytes((tm, tn_), F32) if add is not None else 0))

    budget = MM_BLOCK_BUDGET // 2 if mode == "TN" else MM_BLOCK_BUDGET
    if tn is None:
        tn = _pick(n, (512, 384, 256, 128))
        if mode != "TN" and n % 1024 == 0 and block_bytes(k, 1024) <= (2 * budget) // 3:
            tn = 1024
    tk = k if block_bytes(k, tn) <= budget else _pick(k, (2176, 2048, 1088, 1024, 768, 512, 384, 256, 128))
    nk = k // tk
    dims = {"NN": NN, "NT": NT, "TN": TN}[mode]

    def body(*refs):
        a_ref, b_ref = refs[0], refs[1]
        add_ref = refs[2] if add is not None else None
        o_ref = refs[3] if add is not None else refs[2]

        def finish(r):
            if add_ref is not None:
                r = r + add_ref[...]
            o_ref[...] = r.reshape(o_ref.shape).astype(o_ref.dtype)

        prod = _dot(_bf(a_ref[...]), _bf(b_ref[...]), dims)
        if nk == 1:
            finish(prod)
            return
        acc_ref = refs[-1]
        kk = pl.program_id(2)

        @pl.when(kk == 0)
        def _():
            acc_ref[...] = prod

        @pl.when(kk > 0)
        def _():
            acc_ref[...] += prod

        @pl.when(kk == nk - 1)
        def _():
            finish(acc_ref[...])

    if mode == "NN":
        a_spec = pl.BlockSpec((tm, tk), lambda i, j, kk: (i, kk))
        b_spec = pl.BlockSpec((tk, tn), lambda i, j, kk: (kk, j))
    elif mode == "NT":
        a_spec = pl.BlockSpec((tm, tk), lambda i, j, kk: (i, kk))
        b_spec = pl.BlockSpec((tn, tk), lambda i, j, kk: (j, kk))
    else:
        a_spec = pl.BlockSpec((tk, tm), lambda i, j, kk: (kk, i))
        b_spec = pl.BlockSpec((tk, tn), lambda i, j, kk: (kk, j))
    in_specs = [a_spec, b_spec]
    args = [a, b]
    if add is not None:
        in_specs.append(pl.BlockSpec((tm, tn), lambda i, j, kk: (i, j)))
        args.append(add)
    if slabs:
        out_shape = jax.ShapeDtypeStruct((n // tn, m, tn), out_dtype)
        out_spec = pl.BlockSpec((1, tm, tn), lambda i, j, kk: (j, i, 0))
    else:
        out_shape = jax.ShapeDtypeStruct((m, n), out_dtype)
        out_spec = pl.BlockSpec((tm, tn), lambda i, j, kk: (i, j))
    return _pc(body, name=name, grid=(m // tm, n // tn, nk), in_specs=in_specs, out_specs=out_spec,
               out_shape=out_shape, scratch=[] if nk == 1 else [pltpu.VMEM((tm, tn), F32)],
               vmem=block_bytes(tk, tn) + (0 if nk == 1 else _nbytes((tm, tn), F32) // 2), dep=dep)(*args)


def _rms_fwd(x, g, *, name):
    r, d = x.shape
    tm = _pick(r, (256, 128))

    def body(x_ref, g_ref, o_ref):
        xv = x_ref[...]
        rstd = lax.rsqrt(jnp.mean(xv * xv, axis=1, keepdims=True) + NORM_EPS)
        o_ref[...] = (xv * rstd * g_ref[...]).astype(o_ref.dtype)

    return _pc(body, name=name, grid=(r // tm,),
               in_specs=[pl.BlockSpec((tm, d), lambda i: (i, 0)), pl.BlockSpec((1, d), lambda i: (0, 0))],
               out_specs=pl.BlockSpec((tm, d), lambda i: (i, 0)), out_shape=jax.ShapeDtypeStruct((r, d), BF16),
               vmem=tm * d * 6)(x, g.reshape(1, d))


def _rms_bwd(x, g, dxn, dres, *, name, dep=None):
    r, d = x.shape
    tm = _pick(r, (256, 128))

    def body(x_ref, g_ref, dxn_ref, dres_ref, dx_ref, dg_ref, db_ref):
        @pl.when(pl.program_id(0) == 0)
        def _():
            dg_ref[...] = jnp.zeros_like(dg_ref)

        xv = x_ref[...]
        rstd = lax.rsqrt(jnp.mean(xv * xv, axis=1, keepdims=True) + NORM_EPS)
        xh = xv * rstd
        dy = dxn_ref[...]
        dg_ref[...] += jnp.sum(dy * xh, axis=0, keepdims=True)
        dyg = dy * g_ref[...]
        dx_ref[...] = dres_ref[...] + rstd * (dyg - xh * jnp.mean(dyg * xh, axis=1, keepdims=True))

        db_ref[...] = _bf(dx_ref[...])

    row = pl.BlockSpec((tm, d), lambda i: (i, 0))
    vec = pl.BlockSpec((1, d), lambda i: (0, 0))
    return _pc(body, name=name, grid=(r // tm,), in_specs=[row, vec, row, row], out_specs=[row, vec, row],
               out_shape=[jax.ShapeDtypeStruct((r, d), F32), jax.ShapeDtypeStruct((1, d), F32),
                          jax.ShapeDtypeStruct((r, d), BF16)],
               vmem=tm * d * 18, dep=dep)(x, g.reshape(1, d), dxn, dres)


def _rms_bwd_first(x, g, dxn, dres, bsz, nc, *, name):
    d = x.shape[1]

    def body(x_ref, g_ref, dxn_ref, dres_ref, gx_ref, d0_ref, dg_ref):
        b, c = pl.program_id(0), pl.program_id(1)

        @pl.when((b == 0) & (c == 0))
        def _():
            dg_ref[...] = jnp.zeros_like(dg_ref)

        xv = x_ref[...]
        rstd = lax.rsqrt(jnp.mean(xv * xv, axis=1, keepdims=True) + NORM_EPS)
        xh = xv * rstd
        dy = dxn_ref[...]
        dg_ref[...] += jnp.sum(dy * xh, axis=0, keepdims=True)
        dyg = dy * g_ref[...]
        dx = dres_ref[...] + rstd * (dyg - xh * jnp.mean(dyg * xh, axis=1, keepdims=True))

        @pl.when(c == 0)
        def _():
            d0_ref[0] = dx

        @pl.when(c > 0)
        def _():
            gx_ref[0] = dx

    row = pl.BlockSpec((CHUNK, d), lambda b, c: (b * nc + c, 0))
    vec = pl.BlockSpec((1, d), lambda b, c: (0, 0))
    return _pc(body, name=name, grid=(bsz, nc), in_specs=[row, vec, row, row],
               out_specs=[pl.BlockSpec((1, CHUNK, d), lambda b, c: (b, jnp.maximum(c - 1, 0), 0)),
                          pl.BlockSpec((1, CHUNK, d), lambda b, c: (b, 0, 0)), vec],
               out_shape=[jax.ShapeDtypeStruct((bsz, (nc - 1) * CHUNK, d), F32),
                          jax.ShapeDtypeStruct((bsz, CHUNK, d), F32), jax.ShapeDtypeStruct((1, d), F32)],
               vmem=CHUNK * d * 24)(x, g.reshape(1, d), dxn, dres)


def _final_loss(h, g, target, bsz, nc):
    d = h.shape[1]

    def body(h_ref, g_ref, t_ref, loss_ref, dh_ref, dg_ref, db_ref):
        b, c = pl.program_id(0), pl.program_id(1)

        @pl.when((b == 0) & (c == 0))
        def _():
            loss_ref[...] = jnp.zeros_like(loss_ref)
            dg_ref[...] = jnp.zeros_like(dg_ref)

        @pl.when(c == 0)
        def _():
            dh_ref[...] = jnp.zeros_like(dh_ref)
            db_ref[...] = jnp.zeros_like(db_ref)

        @pl.when(c > 0)
        def _():
            xv = h_ref[...]
            rstd = lax.rsqrt(jnp.mean(xv * xv, axis=1, keepdims=True) + NORM_EPS)
            xh = xv * rstd
            gv = g_ref[...]
            err = xh * gv - t_ref[0]
            loss_ref[...] += 0.5 * jnp.sum(jnp.mean(err * err, axis=1, keepdims=True))
            dy = err * (1.0 / d)
            dg_ref[...] += jnp.sum(dy * xh, axis=0, keepdims=True)
            dyg = dy * gv
            dh = rstd * (dyg - xh * jnp.mean(dyg * xh, axis=1, keepdims=True))
            dh_ref[...] = dh
            db_ref[...] = _bf(dh)

    row = pl.BlockSpec((CHUNK, d), lambda b, c: (b * nc + c, 0))
    vec = pl.BlockSpec((1, d), lambda b, c: (0, 0))
    return _pc(body, name="final_loss", grid=(bsz, nc),
               in_specs=[row, vec, pl.BlockSpec((1, CHUNK, d), lambda b, c: (b, jnp.maximum(c - 1, 0), 0))],
               out_specs=[pl.BlockSpec((8, 128), lambda b, c: (0, 0)), row, vec, row],
               out_shape=[jax.ShapeDtypeStruct((8, 128), F32), jax.ShapeDtypeStruct(h.shape, F32),
                          jax.ShapeDtypeStruct((1, d), F32), jax.ShapeDtypeStruct(h.shape, BF16)],
               vmem=CHUNK * d * 18)(h, g.reshape(1, d), target)


def _adamw(w, m, v, gparts, *, name):
    r, c = w.shape
    tr = _pick(r, (256, 128)) if r * c * 4 > (1 << 20) else r

    def body(w_ref, m_ref, v_ref, gp_ref, g_ref, d_ref, nm_ref, nv_ref):
        g = gp_ref[0].astype(F32)
        for j in range(1, N_DEV):
            g = g + gp_ref[j].astype(F32)
        mm = ADAM_B1 * m_ref[...] + (1.0 - ADAM_B1) * g
        vv = ADAM_B2 * v_ref[...] + (1.0 - ADAM_B2) * (g * g)
        m_hat = mm / (1.0 - ADAM_B1 ** ADAM_STEP)
        v_hat = vv / (1.0 - ADAM_B2 ** ADAM_STEP)
        g_ref[...] = g
        d_ref[...] = -ADAM_LR * (m_hat / (jnp.sqrt(v_hat) + ADAM_EPS) + ADAM_WD * w_ref[...])
        nm_ref[...] = mm
        nv_ref[...] = vv

    blk = pl.BlockSpec((tr, c), lambda i: (i, 0))
    out = jax.ShapeDtypeStruct((r, c), F32)
    return _pc(body, name=name, grid=(r // tr,),
               in_specs=[blk, blk, blk, pl.BlockSpec((N_DEV, tr, c), lambda i: (0, i, 0))],
               out_specs=[blk, blk, blk, blk], out_shape=[out, out, out, out],
               vmem=tr * c * (4 * 7 + N_DEV * jnp.dtype(gparts.dtype).itemsize))(w, m, v, gparts)


PEERS = (1, 2, 4, 6, 3, 5, 7)
HBM_SPEC = pl.BlockSpec(memory_space=pltpu.HBM)
SEM_SPEC = pl.BlockSpec(memory_space=pltpu.SEMAPHORE)
SIDE_EFFECT = pltpu.SideEffectType.DATAFLOW_SIDE_EFFECTING


def _peer(p):
    x, y, c = lax.axis_index("x"), lax.axis_index("y"), lax.axis_index("c")
    tx, ty, tc = x ^ ((p >> 2) & 1), y ^ ((p >> 1) & 1), c ^ (p & 1)
    return (tx, ty, tc), 4 * tx + 2 * ty + tc


def _place_own(a, kind, *, name):
    rows, cols = a.shape[-2:]
    small = _nbytes((rows, cols), a.dtype) <= (2 << 20)
    tr = rows if small else _pick(rows, (512, 256, 128, 64, 32, 16))
    me = (4 * lax.axis_index("x") + 2 * lax.axis_index("y") + lax.axis_index("c")).astype(jnp.int32).reshape(1)

    def body(me_ref, in_ref, out_ref):
        out_ref[...] = in_ref[...].reshape(out_ref.shape)

    if kind == "a2a":
        in_spec = pl.BlockSpec((1, tr, cols), lambda i, me_ref: (me_ref[0], i, 0))
    else:
        in_spec = pl.BlockSpec((tr, cols), lambda i, me_ref: (i, 0))
    return pl.pallas_call(
        body, name=name, out_shape=jax.ShapeDtypeStruct((N_DEV, rows, cols), a.dtype),
        grid_spec=pltpu.PrefetchScalarGridSpec(
            num_scalar_prefetch=1, grid=(rows // tr,), in_specs=[in_spec],
            out_specs=pl.BlockSpec((1, tr, cols), lambda i, me_ref: (me_ref[0], i, 0))))(me, a)


def _exchange_copies(ins, lands, send_sems, recv_sems, kinds, incoming, peers=PEERS):
    me = 4 * lax.axis_index("x") + 2 * lax.axis_index("y") + lax.axis_index("c")
    copies = []
    for i, kind in enumerate(kinds):
        for p in peers:
            dev, tgt = _peer(p)
            k = i * (N_DEV - 1) + p - 1
            copies.append(pltpu.make_async_remote_copy(
                src_ref=ins[i].at[tgt] if kind == "a2a" else ins[i], dst_ref=lands[i].at[tgt if incoming else me],
                send_sem=send_sems.at[k], recv_sem=recv_sems.at[k], device_id=dev, device_id_type=MESH))
    return copies


def _exchange_start(arrays, kinds, *, name, dep=None, peers=PEERS):
    n = len(arrays)
    lands = [_place_own(a, k, name=f"{name}_own{i}") for i, (a, k) in enumerate(zip(arrays, kinds))]
    extra = [] if dep is None else [dep]

    def body(*refs):
        ins, lnd = refs[:n], refs[n:2 * n]
        send_sems, recv_sems = refs[2 * n + len(extra)], refs[2 * n + len(extra) + 1]
        token = refs[-1]
        for cp in _exchange_copies(ins, lnd, send_sems, recv_sems, kinds, False, peers):
            cp.start()
        token[...] = jnp.zeros_like(token)

    sem = pltpu.SemaphoreType.DMA((n * (N_DEV - 1),))
    outs = pl.pallas_call(
        body, name=name, in_specs=[HBM_SPEC] * (2 * n) + [ANY_SPEC] * len(extra),
        out_specs=[SEM_SPEC, SEM_SPEC] + [HBM_SPEC] * (2 * n) + [pl.BlockSpec(memory_space=pltpu.VMEM)],
        out_shape=[sem, sem] + [pltpu.HBM(a.shape, a.dtype) for a in arrays + lands]
        + [jax.ShapeDtypeStruct((8, 128), F32)],
        input_output_aliases={i: 2 + i for i in range(2 * n)},
        compiler_params=pltpu.CompilerParams(has_side_effects=SIDE_EFFECT),
    )(*[pltpu.with_memory_space_constraint(a, pltpu.HBM) for a in arrays + lands], *extra)
    return dict(send=outs[0], recv=outs[1], ins=list(outs[2:2 + n]), lands=list(outs[2 + n:2 + 2 * n]),
                token=outs[-1], kinds=kinds, name=name, peers=peers)


def _exchange_wait(h, after):
    n = len(h["ins"])
    kinds = h["kinds"]

    def body(*refs):
        ins, lnd = refs[:n], refs[n:2 * n]
        send_sems, recv_sems = refs[2 * n], refs[2 * n + 1]
        copies = _exchange_copies(ins, lnd, send_sems, recv_sems, kinds, True, h["peers"])
        for cp in copies:
            cp.wait_recv()
        for cp in copies:
            cp.wait_send()

    arrs = h["ins"] + h["lands"]
    after = list(after) if isinstance(after, (list, tuple)) else [after]
    outs = pl.pallas_call(
        body, name=h["name"] + "_wait", in_specs=[HBM_SPEC] * (2 * n) + [SEM_SPEC, SEM_SPEC] + [ANY_SPEC] * len(after),
        out_specs=[HBM_SPEC] * (2 * n), out_shape=[pltpu.HBM(a.shape, a.dtype) for a in arrs],
        input_output_aliases={i: i for i in range(2 * n)},
        compiler_params=pltpu.CompilerParams(has_side_effects=SIDE_EFFECT),
    )(*arrs, h["send"], h["recv"], *after)
    return list(outs[n:])


SAME_CORE = (0, 2, 4, 6)


def _forward_copies(land, send_sems, recv_sems, incoming):
    me = 4 * lax.axis_index("x") + 2 * lax.axis_index("y") + lax.axis_index("c")
    dev, sibling = _peer(1)
    return [pltpu.make_async_remote_copy(
        src_ref=land.at[me ^ q], dst_ref=land.at[(sibling if incoming else me) ^ q],
        send_sem=send_sems.at[j], recv_sem=recv_sems.at[j], device_id=dev, device_id_type=MESH)
        for j, q in enumerate(SAME_CORE)]


def _sibling_forward_start(land, *, name, dep=None):
    extra = [] if dep is None else [dep]

    def body(*refs):
        land_ref, send_sems, recv_sems, token = refs[0], refs[1 + len(extra)], refs[2 + len(extra)], refs[-1]
        for cp in _forward_copies(land_ref, send_sems, recv_sems, False):
            cp.start()
        token[...] = jnp.zeros_like(token)

    sem = pltpu.SemaphoreType.DMA((len(SAME_CORE),))
    outs = pl.pallas_call(
        body, name=name, in_specs=[HBM_SPEC] + [ANY_SPEC] * len(extra),
        out_specs=[SEM_SPEC, SEM_SPEC, HBM_SPEC, pl.BlockSpec(memory_space=pltpu.VMEM)],
        out_shape=[sem, sem, pltpu.HBM(land.shape, land.dtype), jax.ShapeDtypeStruct((8, 128), F32)],
        input_output_aliases={0: 2}, compiler_params=pltpu.CompilerParams(has_side_effects=SIDE_EFFECT),
    )(pltpu.with_memory_space_constraint(land, pltpu.HBM), *extra)
    return dict(send=outs[0], recv=outs[1], land=outs[2], token=outs[3], name=name)


def _sibling_forward_wait(h, after):
    def body(*refs):
        copies = _forward_copies(refs[0], refs[1], refs[2], True)
        for cp in copies:
            cp.wait_recv()
        for cp in copies:
            cp.wait_send()

    return pl.pallas_call(
        body, name=h["name"] + "_wait", in_specs=[HBM_SPEC, SEM_SPEC, SEM_SPEC, ANY_SPEC], out_specs=HBM_SPEC,
        out_shape=pltpu.HBM(h["land"].shape, h["land"].dtype), input_output_aliases={0: 0},
        compiler_params=pltpu.CompilerParams(has_side_effects=SIDE_EFFECT),
    )(h["land"], h["send"], h["recv"], after)


def _s5_params(lam_re, lam_im, log_dt, b_re, b_im):
    dt = jnp.exp(log_dt)[:, None]
    mag = jnp.exp(lam_re * dt)
    ar, ai = mag * jnp.cos(lam_im * dt), mag * jnp.sin(lam_im * dt)
    den = lam_re * lam_re + lam_im * lam_im
    qr = ((ar - 1.0) * lam_re + ai * lam_im) / den
    qi = (ai * lam_re - (ar - 1.0) * lam_im) / den
    bbr = qr[..., None] * b_re - qi[..., None] * b_im
    bbi = qr[..., None] * b_im + qi[..., None] * b_re
    return ar, ai, bbr, bbi


def _s5_power_table(ar, ai):
    pr, pi = ar.reshape(1, -1), ai.reshape(1, -1)
    while pr.shape[0] < 8:
        sr, si = pr[-1:], pi[-1:]
        pr, pi = (jnp.concatenate([pr, pr * sr - pi * si], axis=0), jnp.concatenate([pi, pr * si + pi * sr], axis=0))
    return pr, pi


def _blockdiag(w, rows, cols):
    w = w.reshape(S5_GB, S5_GB, rows, cols)
    eye = jnp.eye(S5_GB, dtype=w.dtype)
    return jnp.einsum("abrc,bd->abrdc", w, eye).reshape(S5_GB, S5_GB * rows, S5_GB * cols)


def _blockdiag_extract(w, rows, cols):
    w = w.reshape(S5_GB, S5_GB, rows, S5_GB, cols)
    return jnp.einsum("abrbc->abrc", w).reshape(S5_GROUPS, rows, cols)


def _s5_scan_specs(bsz, nc, rev):
    def cc(c):
        return (nc - 1 - c) if rev else c

    return dict(
        u=pl.BlockSpec((bsz, CHUNK, CHUNK), lambda g, c: (0, cc(c), g)),
        x=pl.BlockSpec((bsz, CHUNK, S5_LANES), lambda g, c: (0, cc(c), g)),
        wb=pl.BlockSpec((1, CHUNK, S5_LANES), lambda g, c: (g, 0, 0)),
        wc=pl.BlockSpec((1, S5_LANES, CHUNK), lambda g, c: (g, 0, 0)),
        tab=pl.BlockSpec((8, S5_LANES), lambda g, c: (0, g)),
        step=pl.BlockSpec((8, S5_LANES), lambda g, c: (0, g)),
        d=pl.BlockSpec((1, CHUNK), lambda g, c: (0, g)),
        lane=pl.BlockSpec((1, S5_LANES), lambda g, c: (0, g)),
        xprev=pl.BlockSpec((bsz, 8, S5_LANES), lambda g, c: (0, jnp.maximum(cc(c) * (CHUNK // 8) - 1, 0), g)),
    )


def _s5_fwd(u, wbr, wbi, pr, pi, sr, si, wcr, wci, d, bsz, nc):
    r = u.shape[0]
    tp = r // bsz
    sp = _s5_scan_specs(bsz, nc, False)

    def body(u_all, wbr_ref, wbi_ref, pr_ref, pi_ref, sr_ref, si_ref, wcr_ref, wci_ref, d_ref,
             xr_all, xi_all, y1_all, g_all, cr_sall, ci_sall):
        @pl.when(pl.program_id(1) == 0)
        def _():
            cr_sall[...] = jnp.zeros_like(cr_sall)
            ci_sall[...] = jnp.zeros_like(ci_sall)

        for bi in range(bsz):
            one(u_all.at[bi], wbr_ref, wbi_ref, pr_ref, pi_ref, sr_ref, si_ref, wcr_ref, wci_ref, d_ref,
                xr_all.at[bi], xi_all.at[bi], y1_all.at[bi], g_all.at[bi], cr_sall.at[bi], ci_sall.at[bi])

    def one(u_ref, wbr_ref, wbi_ref, pr_ref, pi_ref, sr_ref, si_ref, wcr_ref, wci_ref, d_ref,
            xr_ref, xi_ref, y1_ref, g_ref, cr_s, ci_s):
        uv = u_ref[...]
        ub = _bf(uv)
        xr, xi = _dot(ub, wbr_ref[0]), _dot(ub, wbi_ref[0])
        sub = lax.broadcasted_iota(jnp.int32, (CHUNK, S5_LANES), 0) % 8
        for k in range(3):
            s = 1 << k
            ar, ai = sr_ref[k:k + 1, :], si_ref[k:k + 1, :]
            hr = jnp.where(sub >= s, pltpu.roll(xr, s, 0), 0.0)
            hi = jnp.where(sub >= s, pltpu.roll(xi, s, 0), 0.0)
            xr, xi = xr + (ar * hr - ai * hi), xi + (ar * hi + ai * hr)
        cr, ci = cr_s[...], ci_s[...]
        tr, ti = pr_ref[...], pi_ref[...]
        outr, outi = [], []
        for g8 in range(CHUNK // 8):
            br, bi = xr[8 * g8:8 * g8 + 8, :], xi[8 * g8:8 * g8 + 8, :]
            br, bi = br + (tr * cr - ti * ci), bi + (tr * ci + ti * cr)
            cr, ci = br[7:8, :], bi[7:8, :]
            outr.append(br)
            outi.append(bi)
        xr, xi = jnp.concatenate(outr, axis=0), jnp.concatenate(outi, axis=0)
        cr_s[...] = cr
        ci_s[...] = ci
        xr_ref[...] = xr
        xi_ref[...] = xi
        y = _dot(_bf(xr), wcr_ref[0]) - _dot(_bf(xi), wci_ref[0]) + d_ref[...] * uv
        y1_ref[...] = y
        g_ref[...] = _bf(_gelu_and_grad(y)[0])

    ns = S5_GROUPS * S5_STATE
    xr, xi, y1, g = _pc(
        body, name="s5_fwd", grid=(S5_GB, nc),
        in_specs=[sp["u"], sp["wb"], sp["wb"], sp["tab"], sp["tab"], sp["step"], sp["step"], sp["wc"], sp["wc"],
                  sp["d"]],
        out_specs=[sp["x"], sp["x"], sp["u"], sp["u"]],
        out_shape=[jax.ShapeDtypeStruct((bsz, tp, ns), F32)] * 2
        + [jax.ShapeDtypeStruct((bsz, tp, S5_WIDTH), F32), jax.ShapeDtypeStruct((bsz, tp, S5_WIDTH), BF16)],
        scratch=[pltpu.VMEM((bsz, 1, S5_LANES), F32)] * 2, vmem=8 << 20,
    )(_seq(u, bsz), wbr, wbi, pr, pi, sr, si, wcr, wci, d)
    return xr.reshape(r, ns), xi.reshape(r, ns), y1.reshape(r, S5_WIDTH), g.reshape(r, S5_WIDTH)


def _s5_post(y1, glu_pre, glu_b, z):
    r, w = y1.shape
    tm = _pick(r, (256, 128))

    def body(y_ref, p_ref, b_ref, z_ref, o_ref):
        g = _gelu_and_grad(y_ref[...])[0]
        o_ref[...] = _bf(g * jax.nn.sigmoid(p_ref[...] + b_ref[...]) * _silu(z_ref[...]))

    row = pl.BlockSpec((tm, w), lambda i: (i, 0))
    return _pc(body, name="s5_post", grid=(r // tm,), in_specs=[row, row, pl.BlockSpec((1, w), lambda i: (0, 0)), row],
               out_specs=row, out_shape=jax.ShapeDtypeStruct((r, w), BF16), vmem=tm * w * 16)(y1, glu_pre, glu_b, z)


def _s5_post_bwd(dya, y1, glu_pre, glu_b, z):
    r, w = y1.shape
    tm = _pick(r, (256, 128))

    def body(dy_ref, y_ref, p_ref, b_ref, z_ref, dz_ref, dp_ref, dg_ref, db_ref):
        @pl.when(pl.program_id(0) == 0)
        def _():
            db_ref[...] = jnp.zeros_like(db_ref)

        g = _gelu_and_grad(y_ref[...])[0]
        s = jax.nn.sigmoid(p_ref[...] + b_ref[...])
        zv = z_ref[...]
        dy = dy_ref[...]
        do = dy * _silu(zv)
        dz_ref[...] = _bf(dy * g * s * _dsilu(zv))
        dp = do * g * s * (1.0 - s)
        dp_ref[...] = _bf(dp)
        db_ref[...] += jnp.sum(dp, axis=0, keepdims=True)
        dg_ref[...] = do * s

    row = pl.BlockSpec((tm, w), lambda i: (i, 0))
    vec = pl.BlockSpec((1, w), lambda i: (0, 0))
    return _pc(body, name="s5_post_bwd", grid=(r // tm,), in_specs=[row, row, row, vec, row],
               out_specs=[row, row, row, vec],
               out_shape=[jax.ShapeDtypeStruct((r, w), BF16), jax.ShapeDtypeStruct((r, w), BF16),
                          jax.ShapeDtypeStruct((r, w), F32), jax.ShapeDtypeStruct((1, w), F32)],
               vmem=tm * w * 24)(dya, y1, glu_pre, glu_b, z)


def _s5_bwd(dg, y1, u, xr, xi, wbr, wbi, qr, qi, sr, si, wcr, wci, d, bsz, nc):
    r = u.shape[0]
    tp = r // bsz
    sp = _s5_scan_specs(bsz, nc, True)

    def body(dg_all, y1_all, u_all, xr_all, xi_all, xpr_all, xpi_all, wbr_ref, wbi_ref, qr_ref, qi_ref, sr_ref, si_ref,
             wcr_ref, wci_ref, d_ref, du_all, dd_ref, dwcr_ref, dwci_ref, dwbr_ref, dwbi_ref, dar_ref, dai_ref,
             cr_sall, ci_sall):
        c = pl.program_id(1)

        @pl.when(c == 0)
        def _():
            for ref in (dd_ref, dwcr_ref, dwci_ref, dwbr_ref, dwbi_ref, dar_ref, dai_ref, cr_sall, ci_sall):
                ref[...] = jnp.zeros_like(ref)

        for bi in range(bsz):
            one(c, dg_all.at[bi], y1_all.at[bi], u_all.at[bi], xr_all.at[bi], xi_all.at[bi], xpr_all.at[bi],
                xpi_all.at[bi], wbr_ref, wbi_ref, qr_ref, qi_ref, sr_ref, si_ref, wcr_ref, wci_ref, d_ref,
                du_all.at[bi], dd_ref, dwcr_ref, dwci_ref, dwbr_ref, dwbi_ref, dar_ref, dai_ref, cr_sall.at[bi],
                ci_sall.at[bi])

    def one(c, dg_ref, y1_ref, u_ref, xr_ref, xi_ref, xpr_ref, xpi_ref, wbr_ref, wbi_ref, qr_ref, qi_ref, sr_ref, si_ref,
            wcr_ref, wci_ref, d_ref, du_ref, dd_ref, dwcr_ref, dwci_ref, dwbr_ref, dwbi_ref, dar_ref, dai_ref,
            cr_s, ci_s):
        uv = u_ref[...]
        ub = _bf(uv)
        dy = dg_ref[...] * _gelu_and_grad(y1_ref[...])[1]
        dd_ref[...] += jnp.sum(dy * uv, axis=0, keepdims=True)
        dyb = _bf(dy)
        xr, xi = xr_ref[...], xi_ref[...]
        dwcr_ref[0] += _dot(_bf(xr), dyb, TN)
        dwci_ref[0] -= _dot(_bf(xi), dyb, TN)
        lr, li = _dot(dyb, wcr_ref[0], NT), -_dot(dyb, wci_ref[0], NT)
        row = lax.broadcasted_iota(jnp.int32, (CHUNK, S5_LANES), 0)
        sub = row % 8
        for k in range(3):
            s = 1 << k
            ar, ai = sr_ref[k:k + 1, :], si_ref[k:k + 1, :]
            hr = jnp.where(sub < 8 - s, pltpu.roll(lr, CHUNK - s, 0), 0.0)
            hi = jnp.where(sub < 8 - s, pltpu.roll(li, CHUNK - s, 0), 0.0)
            lr, li = lr + (ar * hr + ai * hi), li + (ar * hi - ai * hr)
        cr, ci = cr_s[...], ci_s[...]
        tr, ti = qr_ref[...], qi_ref[...]
        outr, outi = [], []
        for g8 in reversed(range(CHUNK // 8)):
            br, bi = lr[8 * g8:8 * g8 + 8, :], li[8 * g8:8 * g8 + 8, :]
            br, bi = br + (tr * cr + ti * ci), bi + (tr * ci - ti * cr)
            cr, ci = br[0:1, :], bi[0:1, :]
            outr.append(br)
            outi.append(bi)
        lr, li = jnp.concatenate(outr[::-1], axis=0), jnp.concatenate(outi[::-1], axis=0)
        cr_s[...] = cr
        ci_s[...] = ci
        lrb, lib = _bf(lr), _bf(li)
        du_ref[...] = _bf(_dot(lrb, wbr_ref[0], NT) + _dot(lib, wbi_ref[0], NT) + dy * d_ref[...])
        dwbr_ref[0] += _dot(ub, lrb, TN)
        dwbi_ref[0] += _dot(ub, lib, TN)
        first = c == nc - 1
        pr0 = jnp.where(first, 0.0, xpr_ref[7:8, :])
        pi0 = jnp.where(first, 0.0, xpi_ref[7:8, :])
        xpr = jnp.where(row == 0, pr0, pltpu.roll(xr, 1, 0))
        xpi = jnp.where(row == 0, pi0, pltpu.roll(xi, 1, 0))
        dar_ref[...] += jnp.sum(lr * xpr + li * xpi, axis=0, keepdims=True)
        dai_ref[...] += jnp.sum(li * xpr - lr * xpi, axis=0, keepdims=True)

    st = jax.ShapeDtypeStruct
    xr3, xi3 = _seq(xr, bsz), _seq(xi, bsz)
    outs = _pc(body, name="s5_bwd", grid=(S5_GB, nc),
               in_specs=[sp["u"], sp["u"], sp["u"], sp["x"], sp["x"], sp["xprev"], sp["xprev"], sp["wb"], sp["wb"],
                         sp["tab"], sp["tab"], sp["step"], sp["step"], sp["wc"], sp["wc"], sp["d"]],
               out_specs=[sp["u"], sp["d"], sp["wc"], sp["wc"], sp["wb"], sp["wb"], sp["lane"], sp["lane"]],
               out_shape=[st((bsz, tp, S5_WIDTH), BF16), st((1, S5_WIDTH), F32),
                          st((S5_GB, S5_LANES, CHUNK), F32), st((S5_GB, S5_LANES, CHUNK), F32),
                          st((S5_GB, CHUNK, S5_LANES), F32), st((S5_GB, CHUNK, S5_LANES), F32),
                          st((1, S5_GROUPS * S5_STATE), F32), st((1, S5_GROUPS * S5_STATE), F32)],
               scratch=[pltpu.VMEM((bsz, 1, S5_LANES), F32)] * 2, vmem=12 << 20,
               )(_seq(dg, bsz), _seq(y1, bsz), _seq(u, bsz), xr3, xi3, xr3, xi3, wbr, wbi, qr, qi, sr, si, wcr, wci, d)
    return (outs[0].reshape(r, S5_WIDTH),) + tuple(outs[1:])


def _s5_layer_fwd(u, prm, glu_w, bsz, nc):
    xr, xi, y1, g = _s5_fwd(u, prm["wbr"], prm["wbi"], prm["pr"], prm["pi"], prm["sr"], prm["si"], prm["wcr"],
                            prm["wci"], prm["d"], bsz, nc)
    glu_pre = _mm(g, glu_w(y1) if callable(glu_w) else glu_w, "NN", name="s5_glu")
    return dict(xr=xr, xi=xi, y1=y1, g=g, glu_pre=glu_pre)


def _s5_layer_bwd(dya, u, z, sv, prm, pvjp, glu_w, glu_b, bsz, nc):
    dz, dglu, dg_direct, dglu_b = _s5_post_bwd(dya, sv["y1"], sv["glu_pre"], glu_b, z)
    dg = _mm(dglu, glu_w, "NT", name="s5_dg", add=dg_direct)
    dglu_w = _mm(sv["g"], dglu, "TN", name="s5_dglu_w")
    du, dd, dwcr, dwci, dwbr, dwbi, dar, dai = _s5_bwd(
        dg, sv["y1"], u, sv["xr"], sv["xi"], prm["wbr"], prm["wbi"], prm["qr"], prm["qi"], prm["sr"], prm["si"],
        prm["wcr"], prm["wci"], prm["d"], bsz, nc)
    dbbr = jnp.swapaxes(_blockdiag_extract(dwbr, S5_GROUP_SIZE, S5_STATE), 1, 2)
    dbbi = jnp.swapaxes(_blockdiag_extract(dwbi, S5_GROUP_SIZE, S5_STATE), 1, 2)
    dlr, dli, dldt, dbr, dbi = pvjp((dar.reshape(S5_GROUPS, S5_STATE), dai.reshape(S5_GROUPS, S5_STATE), dbbr, dbbi))
    grads = dict(
        s5_lambda_re=dlr, s5_lambda_im=dli, s5_log_dt=dldt, s5_b_re=dbr, s5_b_im=dbi,
        s5_c_re=jnp.swapaxes(_blockdiag_extract(dwcr, S5_STATE, S5_GROUP_SIZE), 1, 2),
        s5_c_im=jnp.swapaxes(_blockdiag_extract(dwci, S5_STATE, S5_GROUP_SIZE), 1, 2),
        s5_d=dd, s5_glu_w=dglu_w, s5_glu_b=dglu_b)
    return du, dz, grads


def _s5_tables(lam_re, lam_im, log_dt, b_re, b_im, c_re, c_im, d):
    (ar, ai, bbr, bbi), vjp = jax.vjp(_s5_params, lam_re, lam_im, log_dt, b_re, b_im)
    pr, pi = _s5_power_table(lax.stop_gradient(ar), lax.stop_gradient(ai))
    steps = [0, 1, 3, 7, 7, 7, 7, 7]
    flip8 = (jnp.arange(8)[:, None] + jnp.arange(8)[None, :] == 7).astype(F32)
    prm = dict(
        wbr=_bf(_blockdiag(jnp.swapaxes(bbr, 1, 2), S5_GROUP_SIZE, S5_STATE)),
        wbi=_bf(_blockdiag(jnp.swapaxes(bbi, 1, 2), S5_GROUP_SIZE, S5_STATE)),
        wcr=_bf(_blockdiag(jnp.swapaxes(c_re, 1, 2), S5_STATE, S5_GROUP_SIZE)),
        wci=_bf(_blockdiag(jnp.swapaxes(c_im, 1, 2), S5_STATE, S5_GROUP_SIZE)),
        pr=pr, pi=pi, qr=jnp.dot(flip8, pr, precision=lax.Precision.HIGHEST),
        qi=jnp.dot(flip8, pi, precision=lax.Precision.HIGHEST),
        sr=jnp.concatenate([pr[i:i + 1] for i in steps], axis=0),
        si=jnp.concatenate([pi[i:i + 1] for i in steps], axis=0), d=d.reshape(1, S5_WIDTH))
    return prm, vjp


def _shift_down(x, halo8, s):
    sh = pltpu.roll(x, s, 0)
    r8 = lax.broadcasted_iota(jnp.int32, halo8.shape, 0)
    first = jnp.where(r8 < s, pltpu.roll(halo8, s, 0), sh[:8])
    return jnp.concatenate([first, sh[8:]], axis=0)


def _shift_up(x, halo8, s):
    sh = pltpu.roll(x, CHUNK - s, 0)
    r8 = lax.broadcasted_iota(jnp.int32, halo8.shape, 0)
    last = jnp.where(r8 >= 8 - s, pltpu.roll(halo8, 8 - s, 0), sh[CHUNK - 8:])
    return jnp.concatenate([sh[:CHUNK - 8], last], axis=0)


def _conv_specs(nc, tw):
    def chunk(b, c):
        return b * nc + c

    return dict(
        x=pl.BlockSpec((CHUNK, tw), lambda j, b, c: (chunk(b, c), j)),
        prev=pl.BlockSpec((8, tw), lambda j, b, c: (jnp.maximum(chunk(b, c) * (CHUNK // 8) - 1, 0), j)),
        nxt=pl.BlockSpec((8, tw), lambda j, b, c: ((b * nc + jnp.minimum(c + 1, nc - 1)) * (CHUNK // 8), j)),
        w=pl.BlockSpec((ML_CONV, tw), lambda j, b, c: (0, j)),
        vec=pl.BlockSpec((1, tw), lambda j, b, c: (0, j)),
    )


def _conv_fwd(x, w, bias, bsz, nc, *, name):
    r, wd = x.shape
    tw = _pick(wd, (2048, 1536, 1024, 512, 384, 256, 128))
    sp = _conv_specs(nc, tw)

    def body(x_ref, p_ref, w_ref, b_ref, o_ref):
        c = pl.program_id(2)
        xv = x_ref[...]
        halo = jnp.where(c == 0, 0.0, p_ref[...])
        acc = b_ref[...] + w_ref[3:4, :] * xv
        for s in (1, 2, 3):
            acc = acc + w_ref[3 - s:4 - s, :] * _shift_down(xv, halo, s)
        o_ref[...] = acc

    return _pc(body, name=name, grid=(wd // tw, bsz, nc), in_specs=[sp["x"], sp["prev"], sp["w"], sp["vec"]],
               out_specs=sp["x"], out_shape=jax.ShapeDtypeStruct((r, wd), F32), vmem=CHUNK * tw * 16,
               )(x, x, w, bias.reshape(1, wd))


def _conv_bwd(dpre, x, w, bsz, nc, *, name, add=None):
    r, wd = x.shape
    tw = _pick(wd, (2048, 1536, 1024, 512, 384, 256, 128))
    sp = _conv_specs(nc, tw)

    def body(*refs):
        d_ref, n_ref, x_ref, p_ref, w_ref = refs[:5]
        add_ref = refs[5] if add is not None else None
        dx_ref, dw_ref, db_ref = refs[-3:]
        b, c = pl.program_id(1), pl.program_id(2)

        @pl.when((b == 0) & (c == 0))
        def _():
            dw_ref[...] = jnp.zeros_like(dw_ref)
            db_ref[...] = jnp.zeros_like(db_ref)

        dv, xv = d_ref[...], x_ref[...]
        dhalo = jnp.where(c == nc - 1, 0.0, n_ref[...])
        xhalo = jnp.where(c == 0, 0.0, p_ref[...])
        dx = w_ref[3:4, :] * dv
        for s in (1, 2, 3):
            dx = dx + w_ref[3 - s:4 - s, :] * _shift_up(dv, dhalo, s)
        if add_ref is not None:
            dx = dx + add_ref[...]
        dx_ref[...] = _bf(dx)
        db_ref[...] += jnp.sum(dv, axis=0, keepdims=True)
        dw_ref[3:4, :] += jnp.sum(dv * xv, axis=0, keepdims=True)
        for s in (1, 2, 3):
            dw_ref[3 - s:4 - s, :] += jnp.sum(dv * _shift_down(xv, xhalo, s), axis=0, keepdims=True)

    ins = [dpre, dpre, x, x, w] + ([add] if add is not None else [])
    specs = [sp["x"], sp["nxt"], sp["x"], sp["prev"], sp["w"]] + ([sp["x"]] if add is not None else [])
    return _pc(body, name=name, grid=(wd // tw, bsz, nc), in_specs=specs, out_specs=[sp["x"], sp["w"], sp["vec"]],
               out_shape=[jax.ShapeDtypeStruct((r, wd), BF16), jax.ShapeDtypeStruct((ML_CONV, wd), F32),
                          jax.ShapeDtypeStruct((1, wd), F32)], vmem=CHUNK * tw * 24)(*ins)


ML_SCALE = ML_DH ** -0.5


ML_LB = ML_DH // CHUNK


def _headwise_expand(w):
    tiled = jnp.tile(w.reshape(ML_HEADS, ML_DH, QKV_BLOCK), (1, 1, CHUNK // QKV_BLOCK))
    rblk = (jnp.arange(ML_DH) % CHUNK) // QKV_BLOCK
    cblk = jnp.arange(CHUNK) // QKV_BLOCK
    return jnp.where(rblk[:, None] == cblk[None, :], tiled, 0.0).reshape(ML_HEADS, ML_LB, CHUNK, CHUNK)


def _headwise_dot(x, w_ref, dims=NN):
    return jnp.concatenate([_dot(x[:, j * CHUNK:(j + 1) * CHUNK], w_ref[0, j], dims) for j in range(ML_LB)], axis=1)


def _headwise_extract(w):
    return w[:, :, :QKV_BLOCK].reshape(ML_HEADS * ML_DH // QKV_BLOCK, QKV_BLOCK, QKV_BLOCK)


def _ml_pre(pre, x, wq, wk, wv, wgq, wgk, wgv, bsz, nc):
    r = x.shape[0]
    tr = _pick(r, (256, 128))
    hrow = pl.BlockSpec((tr, ML_DH), lambda h, i: (i, h))
    wexp = pl.BlockSpec((1, ML_LB, CHUNK, CHUNK), lambda h, i: (h, 0, 0, 0))
    wg = pl.BlockSpec((ML_DH, CHUNK), lambda h, i: (h, 0))

    def body(pre_ref, x_ref, wq_ref, wk_ref, wv_ref, gq_ref, gk_ref, gv_ref, q_ref, qs_ref, k_ref, v_ref, gt_ref):
        xcb = _bf(_silu(pre_ref[...]))
        q = _headwise_dot(xcb, wq_ref)
        k = _headwise_dot(xcb, wk_ref)
        v = _headwise_dot(_bf(x_ref[...]), wv_ref)
        qb, kb, vb = _bf(q), _bf(k), _bf(v)
        q_ref[...] = qb
        qs_ref[...] = _bf(q * ML_SCALE)
        k_ref[...] = kb
        v_ref[...] = vb
        gt_ref[0] = _dot(qb, gq_ref[...]) + _dot(kb, gk_ref[...]) + _dot(vb, gv_ref[...])

    o = jax.ShapeDtypeStruct((r, ML_WIDTH), BF16)
    q, qs, k, v, gates8 = _pc(
        body, name="ml_pre", grid=(ML_HEADS, r // tr),
        in_specs=[hrow, hrow, wexp, wexp, wexp, wg, wg, wg],
        out_specs=[hrow, hrow, hrow, hrow, pl.BlockSpec((1, tr, CHUNK), lambda h, i: (h, i, 0))],
        out_shape=[o, o, o, o, jax.ShapeDtypeStruct((ML_HEADS, r, CHUNK), F32)], vmem=6 << 20,
    )(pre, x, wq, wk, wv, wgq, wgk, wgv)

    return q, qs, k, v, _sum_heads(gates8, name="ml_gates_sum")


def _sum_heads(g8, *, name):
    r = g8.shape[1]
    tr = _pick(r, (256, 128))

    def body(g_ref, o_ref):
        acc = g_ref[0]
        for j in range(1, ML_HEADS):
            acc = acc + g_ref[j]
        o_ref[...] = acc

    return _pc(body, name=name, grid=(r // tr,),
               in_specs=[pl.BlockSpec((ML_HEADS, tr, CHUNK), lambda i: (0, i, 0))],
               out_specs=pl.BlockSpec((tr, CHUNK), lambda i: (i, 0)),
               out_shape=jax.ShapeDtypeStruct((r, CHUNK), F32), vmem=2 << 20)(g8)


def _tri(rev):
    r = lax.broadcasted_iota(jnp.int32, (CHUNK, CHUNK), 0)
    c = lax.broadcasted_iota(jnp.int32, (CHUNK, CHUNK), 1)
    return jnp.where((c >= r) if rev else (c <= r), 1.0, 0.0).astype(F32)


def _cumsum_rows(x, row, rev=False):
    for k in range(7):
        s = 1 << k
        if rev:
            x = x + jnp.where(row < CHUNK - s, pltpu.roll(x, CHUNK - s, 0), 0.0)
        else:
            x = x + jnp.where(row >= s, pltpu.roll(x, s, 0), 0.0)
    return x


def _log_sigmoid(x):
    return jnp.minimum(x, 0.0) - jnp.log(1.0 + jnp.exp(-jnp.abs(x)))


def _ml_core(gates, hd, first, m, qs, k, v, cmat, nvec):
    sq = (CHUNK, CHUNK)
    lane = lax.broadcasted_iota(jnp.int32, sq, 1)
    row = lax.broadcasted_iota(jnp.int32, sq, 0)
    igc = jnp.sum(jnp.where(lane == hd, gates, 0.0), axis=1, keepdims=True)
    fpc = jnp.sum(jnp.where(lane == hd + ML_HEADS, gates, 0.0), axis=1, keepdims=True)
    valid = jnp.logical_or(jnp.logical_not(first), row[:, :1] >= PAD_ROWS)
    igc = jnp.where(valid, igc, NEG)
    lfc = jnp.where(valid, _log_sigmoid(fpc), 0.0)
    bcb = _cumsum_rows(jnp.broadcast_to(lfc, sq), row)
    igb = jnp.broadcast_to(igc, sq)
    dm = jnp.where(lane <= row, bcb - (bcb - igb).T, NEG)
    bc = bcb[:, :1]
    inter = bc + m
    mt = jnp.maximum(inter, jnp.max(dm, axis=1, keepdims=True))
    wt = jnp.exp(dm - mt)
    wprev = jnp.exp(inter - mt)
    s0 = _dot(qs, k, NT)
    s = s0 * wt
    cb = _bf(cmat)
    qc = _dot(qs, cb)
    qf = qs.astype(F32)
    qn = jnp.sum(qf * nvec, axis=1, keepdims=True)
    num = _dot(_bf(s), v) + wprev * qc
    den = jnp.sum(s, axis=1, keepdims=True) + wprev * qn
    emt = jnp.exp(-mt)
    dd = jnp.maximum(jnp.abs(den), emt)
    blast = bcb[CHUNK - 1:CHUNK, :1]
    g = blast - bc + igc
    m_new = jnp.maximum(blast + m, jnp.max(g, axis=0, keepdims=True))
    decay = jnp.exp(blast + m - m_new)
    e = jnp.exp(g - m_new)
    kf = k.astype(F32)
    wk = e * kf
    return dict(lane=lane, row=row, fpc=fpc, valid=valid, wt=wt, wprev=wprev, s=s, cb=cb, qc=qc, qf=qf, qn=qn,
                num=num, den=den, emt=emt, dd=dd, m_new=m_new, decay=decay, e=e, kf=kf, wk=wk)


def _ml_headnorm(h):
    mu = jnp.mean(h, axis=1, keepdims=True)
    hc = h - mu
    rstd = lax.rsqrt(jnp.mean(hc * hc, axis=1, keepdims=True) + HEAD_NORM_EPS)
    return hc * rstd, rstd


def _ml_chunk_specs(nc, rev, bsz):
    def cc(c):
        return (nc - 1 - c) if rev else c

    return dict(
        hrow=pl.BlockSpec((bsz, CHUNK, ML_DH), lambda hd, c: (0, cc(c), hd)),
        gates=pl.BlockSpec((bsz, CHUNK, CHUNK), lambda hd, c: (0, cc(c), 0)),
        bias=pl.BlockSpec((1, CHUNK), lambda hd, c: (0, 0)),
        hvec=pl.BlockSpec((1, ML_DH), lambda hd, c: (0, hd)),
        cs=pl.BlockSpec((bsz, 1, ML_DH, ML_DH), lambda hd, c: (0, hd * nc + cc(c), 0, 0)),
        ns=pl.BlockSpec((bsz, 1, 1, ML_DH), lambda hd, c: (0, hd * nc + cc(c), 0, 0)),
        ms=pl.BlockSpec((bsz, 1, 1, CHUNK), lambda hd, c: (0, hd * nc + cc(c), 0, 0)),
        dgates=pl.BlockSpec((1, bsz, CHUNK, CHUNK), lambda hd, c: (hd, 0, cc(c), 0)),
    )


def _seq(a, bsz):
    return a.reshape(bsz, a.shape[0] // bsz, a.shape[1])


def _ml_chunk_fwd(qs, k, v, gates, b_gate, pre, z, nw, sk, bsz, nc):
    r = qs.shape[0]
    tp = r // bsz
    sp = _ml_chunk_specs(nc, False, bsz)

    def body(qs_all, k_all, v_all, gt_all, bg_ref, pre_all, z_all, nw_ref, sk_ref,
             h_all, yb_all, cs_all, ns_all, ms_all, c_sall, n_sall, m_sall):
        hd, c = pl.program_id(0), pl.program_id(1)

        @pl.when(c == 0)
        def _():
            c_sall[...] = jnp.zeros_like(c_sall)
            n_sall[...] = jnp.zeros_like(n_sall)
            m_sall[...] = jnp.zeros_like(m_sall)

        for bi in range(bsz):
            one(hd, c, qs_all.at[bi], k_all.at[bi], v_all.at[bi], gt_all.at[bi], bg_ref, pre_all.at[bi], z_all.at[bi],
                nw_ref, sk_ref, h_all.at[bi], yb_all.at[bi], cs_all.at[bi], ns_all.at[bi], ms_all.at[bi],
                c_sall.at[bi], n_sall.at[bi], m_sall.at[bi])

    def one(hd, c, qs_ref, k_ref, v_ref, gt_ref, bg_ref, pre_ref, z_ref, nw_ref, sk_ref,
            h_ref, yb_ref, cs_ref, ns_ref, ms_ref, c_s, n_s, m_s):
        cmat, nvec, m = c_s[...], n_s[...], m_s[...]
        cs_ref[0] = cmat
        ns_ref[0] = nvec
        ms_ref[0] = jnp.broadcast_to(m, (1, CHUNK))
        v_ = v_ref[...]
        co = _ml_core(gt_ref[...] + bg_ref[...], hd, c == 0, m, qs_ref[...], k_ref[...], v_, cmat, nvec)
        h = co["num"] / co["dd"]
        h_ref[...] = h
        hn, _ = _ml_headnorm(h)
        yb_ref[...] = _bf((hn * nw_ref[...] + sk_ref[...] * _silu(pre_ref[...])) * _silu(z_ref[...]))
        c_s[...] = co["decay"] * cmat + _dot(_bf(co["wk"]), v_, TN)
        n_s[...] = co["decay"] * nvec + jnp.sum(co["wk"], axis=0, keepdims=True)
        m_s[...] = co["m_new"]

    nst = ML_HEADS * nc
    h, yb, cs, ns, ms = _pc(
        body, name="ml_chunk_fwd", grid=(ML_HEADS, nc),
        in_specs=[sp["hrow"]] * 3 + [sp["gates"], sp["bias"], sp["hrow"], sp["hrow"], sp["hvec"], sp["hvec"]],
        out_specs=[sp["hrow"], sp["hrow"], sp["cs"], sp["ns"], sp["ms"]],
        out_shape=[jax.ShapeDtypeStruct((bsz, tp, ML_WIDTH), F32), jax.ShapeDtypeStruct((bsz, tp, ML_WIDTH), BF16),
                   jax.ShapeDtypeStruct((bsz, nst, ML_DH, ML_DH), F32),
                   jax.ShapeDtypeStruct((bsz, nst, 1, ML_DH), F32), jax.ShapeDtypeStruct((bsz, nst, 1, CHUNK), F32)],
        scratch=[pltpu.VMEM((bsz, ML_DH, ML_DH), F32), pltpu.VMEM((bsz, 1, ML_DH), F32),
                 pltpu.VMEM((bsz, 1, 1), F32)],
        vmem=12 << 20)(*[_seq(a, bsz) for a in (qs, k, v, gates)], b_gate, _seq(pre, bsz), _seq(z, bsz), nw, sk)
    return h.reshape(r, ML_WIDTH), yb.reshape(r, ML_WIDTH), cs, ns, ms


def _ml_chunk_bwd(dyb, qs, k, v, gates, b_gate, pre, z, nw, sk, h, cs, ns, ms, bsz, nc, dep=None):
    r = qs.shape[0]
    tp = r // bsz
    sp = _ml_chunk_specs(nc, True, bsz)

    def body(dy_all, qs_all, k_all, v_all, gt_all, bg_ref, pre_all, z_all, nw_ref, sk_ref, h_all, cs_all, ns_all,
             ms_all, dq_all, dk_all, dv_all, dz_all, dxc_all, dgt_all, dnw_ref, dsk_ref, dc_sall, dn_sall):
        hd, c = pl.program_id(0), pl.program_id(1)

        @pl.when(c == 0)
        def _():
            for ref in (dnw_ref, dsk_ref, dc_sall, dn_sall):
                ref[...] = jnp.zeros_like(ref)

        for bi in range(bsz):
            one(hd, c, dy_all.at[bi], qs_all.at[bi], k_all.at[bi], v_all.at[bi], gt_all.at[bi], bg_ref,
                pre_all.at[bi], z_all.at[bi], nw_ref, sk_ref, h_all.at[bi], cs_all.at[bi], ns_all.at[bi],
                ms_all.at[bi], dq_all.at[bi], dk_all.at[bi], dv_all.at[bi], dz_all.at[bi], dxc_all.at[bi],
                dgt_all.at[0, bi], dnw_ref, dsk_ref, dc_sall.at[bi], dn_sall.at[bi])

    def one(hd, c, dy_ref, qs_ref, k_ref, v_ref, gt_ref, bg_ref, pre_ref, z_ref, nw_ref, sk_ref, h_ref, cs_ref, ns_ref,
            ms_ref, dq_ref, dk_ref, dv_ref, dz_ref, dxc_ref, dgt_ref, dnw_ref, dsk_ref, dc_s, dn_s):

        qs, k, v = qs_ref[...], k_ref[...], v_ref[...]
        cmat, nvec, m = cs_ref[0], ns_ref[0], ms_ref[0][:, :1]
        co = _ml_core(gt_ref[...] + bg_ref[...], hd, c == nc - 1, m, qs, k, v, cmat, nvec)
        lane, row = co["lane"], co["row"]
        wt, wprev, s, cb, qf = co["wt"], co["wprev"], co["s"], co["cb"], co["qf"]
        h = h_ref[...]
        hn, rstd = _ml_headnorm(h)
        xc = _silu(pre_ref[...])
        zv = z_ref[...]
        nw, sk = nw_ref[...], sk_ref[...]
        dy = dy_ref[...]
        dz_ref[...] = _bf(dy * (hn * nw + sk * xc) * _dsilu(zv))
        do = dy * _silu(zv)
        dsk_ref[...] += jnp.sum(do * xc, axis=0, keepdims=True)
        dnw_ref[...] += jnp.sum(do * hn, axis=0, keepdims=True)
        dxc_ref[...] = do * sk
        dhn = do * nw
        dh = rstd * (dhn - jnp.mean(dhn, axis=1, keepdims=True) - hn * jnp.mean(dhn * hn, axis=1, keepdims=True))
        rinv = 1.0 / co["dd"]
        dnum = dh * rinv
        ddd = -jnp.sum(dh * h, axis=1, keepdims=True) * rinv
        den = co["den"]
        dden = jnp.where(jnp.abs(den) >= co["emt"], ddd * jnp.sign(den), 0.0)
        dnb = _bf(dnum)
        ds = _dot(dnb, v, NT) + dden
        dv = _dot(_bf(s), dnb, TN)
        dnw_ = _bf(dnum * wprev)
        dwn = dden * wprev
        dqs = _dot(dnw_, cb, NT) + dwn * nvec
        dc_out = _dot(qs, dnw_, TN)
        dn_out = jnp.sum(dwn * qf, axis=0, keepdims=True)
        dwprev = jnp.sum(dnum * co["qc"], axis=1, keepdims=True) + dden * co["qn"]
        ds0 = _bf(ds * wt)
        ddm = ds * s
        dqs = dqs + _dot(ds0, k)
        dk = _dot(ds0, qs, TN)
        colc = jnp.sum(ddm.T, axis=1, keepdims=True)
        dbc = dwprev * wprev + jnp.sum(ddm, axis=1, keepdims=True) - colc
        dig = colc
        dcn, dnn = dc_s[...], dn_s[...]
        dcb = _bf(dcn)
        decay, e, kf, wk = co["decay"], co["e"], co["kf"], co["wk"]
        ddecay = (jnp.sum(jnp.sum(dcn * cmat, axis=1, keepdims=True), axis=0, keepdims=True)
                  + jnp.sum(dnn * nvec, axis=1, keepdims=True))
        dwk = _dot(v, dcb, NT) + dnn
        dv = dv + _dot(_bf(wk), dcb)
        dk = dk + e * dwk
        dg = jnp.sum(dwk * kf, axis=1, keepdims=True) * e
        dblast = ddecay * decay + jnp.sum(dg, axis=0, keepdims=True)
        dbc = dbc - dg + jnp.where(row[:, :1] == CHUNK - 1, dblast, 0.0)
        dig = dig + dg
        dc_s[...] = decay * dcn + dc_out
        dn_s[...] = decay * dnn + dn_out
        dlf = _cumsum_rows(jnp.broadcast_to(dbc, (CHUNK, CHUNK)), row, rev=True)[:, :1]
        dfp = dlf * (1.0 - jax.nn.sigmoid(co["fpc"]))
        dig = jnp.where(co["valid"], dig, 0.0)
        dfp = jnp.where(co["valid"], dfp, 0.0)
        dgt_ref[...] = jnp.where(lane == hd, dig, 0.0) + jnp.where(lane == hd + ML_HEADS, dfp, 0.0)
        dq_ref[...] = _bf(dqs * ML_SCALE)
        dk_ref[...] = _bf(dk)
        dv_ref[...] = _bf(dv)

    ob = jax.ShapeDtypeStruct((bsz, tp, ML_WIDTH), BF16)
    dq, dk, dv, dz, dxc, dgt, dnw, dsk = _pc(
        body, name="ml_chunk_bwd", grid=(ML_HEADS, nc),
        in_specs=[sp["hrow"]] * 4 + [sp["gates"], sp["bias"], sp["hrow"], sp["hrow"], sp["hvec"], sp["hvec"],
                                     sp["hrow"], sp["cs"], sp["ns"], sp["ms"]],
        out_specs=[sp["hrow"]] * 5 + [sp["dgates"], sp["hvec"], sp["hvec"]],
        out_shape=[ob, ob, ob, ob, jax.ShapeDtypeStruct((bsz, tp, ML_WIDTH), F32),
                   jax.ShapeDtypeStruct((ML_HEADS, bsz, tp, CHUNK), F32),
                   jax.ShapeDtypeStruct((1, ML_WIDTH), F32), jax.ShapeDtypeStruct((1, ML_WIDTH), F32)],
        scratch=[pltpu.VMEM((bsz, ML_DH, ML_DH), F32), pltpu.VMEM((bsz, 1, ML_DH), F32)], vmem=16 << 20, dep=dep,
    )(*[_seq(a, bsz) for a in (dyb, qs, k, v, gates)], b_gate, _seq(pre, bsz), _seq(z, bsz), nw, sk, _seq(h, bsz),
      cs, ns, ms)
    return (dq.reshape(r, ML_WIDTH), dk.reshape(r, ML_WIDTH), dv.reshape(r, ML_WIDTH), dz.reshape(r, ML_WIDTH),
            dxc.reshape(r, ML_WIDTH), dgt.reshape(ML_HEADS, r, CHUNK), dnw, dsk)


def _ml_pre_bwd(dq, dk, dv, dgates, dxc_skip, pre, x, q, k, v, wq, wk, wv, wgq, wgk, wgv, bsz, nc):
    r = x.shape[0]
    tr = _pick(r, (256, 128))
    nt = r // tr
    hrow = pl.BlockSpec((tr, ML_DH), lambda h, i: (i, h))
    wexp = pl.BlockSpec((1, ML_LB, CHUNK, CHUNK), lambda h, i: (h, 0, 0, 0))
    wcmp = pl.BlockSpec((1, ML_DH, CHUNK), lambda h, i: (h, 0, 0))
    wg = pl.BlockSpec((ML_DH, CHUNK), lambda h, i: (h, 0))
    dgs = pl.BlockSpec((tr, CHUNK), lambda h, i: (i, 0))
    bgs = pl.BlockSpec((1, 1, CHUNK), lambda h, i: (h, 0, 0))

    def body(dq_ref, dk_ref, dv_ref, dg_ref, dxs_ref, pre_ref, x_ref, q_ref, k_ref, v_ref, wq_ref, wk_ref, wv_ref,
             gq_ref, gk_ref, gv_ref, dpre_ref, dxv_ref, cq_ref, ck_ref, cv_ref, dgq_ref, dgk_ref, dgv_ref, dbg_ref,
             dwq_ref, dwk_ref, dwv_ref):
        i = pl.program_id(1)

        @pl.when(i == 0)
        def _():
            for ref in (dwq_ref, dwk_ref, dwv_ref, dgq_ref, dgk_ref, dgv_ref, dbg_ref):
                ref[...] = jnp.zeros_like(ref)

        dgt = dg_ref[...]
        dbg_ref[0] += jnp.sum(dgt, axis=0, keepdims=True)
        dgb = _bf(dgt)
        dqt = _bf(dq_ref[...].astype(F32) + _dot(dgb, gq_ref[...], NT))
        dkt = _bf(dk_ref[...].astype(F32) + _dot(dgb, gk_ref[...], NT))
        dvt = _bf(dv_ref[...].astype(F32) + _dot(dgb, gv_ref[...], NT))
        dgq_ref[...] += _dot(q_ref[...], dgb, TN)
        dgk_ref[...] += _dot(k_ref[...], dgb, TN)
        dgv_ref[...] += _dot(v_ref[...], dgb, TN)
        prev = pre_ref[...]
        xcb = _bf(_silu(prev))
        xb = _bf(x_ref[...])
        for j in range(ML_LB):
            sl = slice(j * CHUNK, (j + 1) * CHUNK)
            dwq_ref[j] += _dot(xcb[:, sl], dqt[:, sl], TN)
            dwk_ref[j] += _dot(xcb[:, sl], dkt[:, sl], TN)
            dwv_ref[j] += _dot(xb[:, sl], dvt[:, sl], TN)
        dxc = _headwise_dot(dqt, wq_ref, NT) + _headwise_dot(dkt, wk_ref, NT) + dxs_ref[...]
        dpre_ref[...] = dxc * _dsilu(prev)
        dxv_ref[...] = _headwise_dot(dvt, wv_ref, NT)

        @pl.when(i == nt - 1)
        def _():
            rr = lax.broadcasted_iota(jnp.int32, (CHUNK, CHUNK), 0)
            cc = lax.broadcasted_iota(jnp.int32, (CHUNK, CHUNK), 1)
            diag = rr // QKV_BLOCK == cc // QKV_BLOCK
            fold = jnp.where(rr % QKV_BLOCK == cc, 1.0, 0.0).astype(F32)
            for src, dst in ((dwq_ref, cq_ref), (dwk_ref, ck_ref), (dwv_ref, cv_ref)):
                for j in range(ML_LB):
                    dst[0, j * CHUNK:(j + 1) * CHUNK, :] = jnp.dot(
                        jnp.where(diag, src[j], 0.0), fold, precision=HI, preferred_element_type=F32)

    f = jax.ShapeDtypeStruct((r, ML_WIDTH), F32)
    wc = jax.ShapeDtypeStruct((ML_HEADS, ML_DH, CHUNK), F32)
    wgs = jax.ShapeDtypeStruct((ML_WIDTH, CHUNK), F32)
    return _pc(body, name="ml_pre_bwd", grid=(ML_HEADS, nt),
               in_specs=[hrow, hrow, hrow, dgs, hrow, hrow, hrow, hrow, hrow, hrow, wexp, wexp, wexp, wg, wg, wg],
               out_specs=[hrow, hrow, wcmp, wcmp, wcmp, wg, wg, wg, bgs],
               out_shape=[f, f, wc, wc, wc, wgs, wgs, wgs, jax.ShapeDtypeStruct((ML_HEADS, 1, CHUNK), F32)],
               scratch=[pltpu.VMEM((ML_LB, CHUNK, CHUNK), F32)] * 3,
               vmem=8 << 20)(dq, dk, dv, dgates, dxc_skip, pre, x, q, k, v, wq, wk, wv, wgq, wgk, wgv)


def _pad_lanes(w):
    return jnp.pad(w, ((0, 0), (0, CHUNK - w.shape[1])))


def _ml_weights(conv_w, conv_b, wq, wk, wv, w_gate, b_gate, norm_w, skip):
    return dict(
        conv_w=conv_w, conv_b=conv_b,
        wq=_bf(_headwise_expand(wq)), wk=_bf(_headwise_expand(wk)), wv=_bf(_headwise_expand(wv)),
        wgq=_bf(_pad_lanes(w_gate[:ML_WIDTH])), wgk=_bf(_pad_lanes(w_gate[ML_WIDTH:2 * ML_WIDTH])),
        wgv=_bf(_pad_lanes(w_gate[2 * ML_WIDTH:])), b_gate=_pad_lanes(b_gate.reshape(1, -1)),
        norm=norm_w.reshape(1, ML_WIDTH), skip=skip.reshape(1, ML_WIDTH))


def _ml_layer_fwd(x, z, w, bsz, nc):
    pre = _conv_fwd(x, w["conv_w"], w["conv_b"], bsz, nc, name="ml_conv")
    q, qs, k, v, gates = _ml_pre(pre, x, w["wq"], w["wk"], w["wv"], w["wgq"], w["wgk"], w["wgv"], bsz, nc)
    h, yb, cs, ns, ms = _ml_chunk_fwd(qs, k, v, gates, w["b_gate"], pre, z, w["norm"], w["skip"], bsz, nc)
    return yb, dict(pre=pre, q=q, qs=qs, k=k, v=v, gates=gates, h=h, cs=cs, ns=ns, ms=ms)


def _ml_layer_bwd(dyb, x, z, sv, w, bsz, nc, dep=None):
    dq, dk, dv, dz, dxc, dgates, dnw, dsk = _ml_chunk_bwd(
        dyb, sv["qs"], sv["k"], sv["v"], sv["gates"], w["b_gate"], sv["pre"], z, w["norm"], w["skip"], sv["h"],
        sv["cs"], sv["ns"], sv["ms"], bsz, nc, dep=dep)
    dpre, dxv, dwq, dwk, dwv, dgq, dgk, dgv, dbg = _ml_pre_bwd(
        dq, dk, dv, _sum_heads(dgates, name="ml_dgates_sum"), dxc, sv["pre"], x, sv["q"], sv["k"], sv["v"], w["wq"],
        w["wk"], w["wv"], w["wgq"], w["wgk"], w["wgv"], bsz, nc)
    dx, dcw, dcb = _conv_bwd(dpre, x, w["conv_w"], bsz, nc, name="ml_conv_bwd", add=dxv)
    ng = 2 * ML_HEADS
    grads = dict(
        ml_conv_w=dcw, ml_conv_b=dcb, ml_wq=_headwise_extract(dwq), ml_wk=_headwise_extract(dwk),
        ml_wv=_headwise_extract(dwv), ml_w_gate=jnp.concatenate([dgq[:, :ng], dgk[:, :ng], dgv[:, :ng]], axis=0),
        ml_b_gate=dbg[0][:, :ng], ml_norm=dnw, ml_skip=dsk)
    return dx, dz, grads


HI = lax.Precision.HIGHEST


def _softplus(x):
    return jnp.maximum(x, 0.0) + jnp.log(1.0 + jnp.exp(-jnp.abs(x)))


def _lane_cumsum(x, lane, rev=False):
    del lane
    return _dot_terms(x, _tri(not rev), NN, exact_rhs=True, terms=3)


def _dot_terms(lhs, rhs, dims, *, exact_rhs, terms):
    x = lhs if exact_rhs else rhs
    sel = _bf(rhs if exact_rhs else lhs)
    acc = None
    for _ in range(terms):
        piece = _bf(x)
        part = _dot(piece, sel, dims) if exact_rhs else _dot(sel, piece, dims)
        acc = part if acc is None else acc + part
        x = x - piece.astype(F32)
    return acc


def _head_sum_matrix():
    r = lax.broadcasted_iota(jnp.int32, (SSD_HPG, SSD_GW), 0)
    l = lax.broadcasted_iota(jnp.int32, (SSD_HPG, SSD_GW), 1)
    return jnp.where(l // SSD_P == r, 1.0, 0.0).astype(F32)


def _ssd_dt_specs(nc):
    return dict(rows=pl.BlockSpec((1, SSD_HEADS, CHUNK), lambda b, c: (b, 0, c)),
                col=pl.BlockSpec((SSD_HEADS, 1), lambda b, c: (0, 0)),
                acc=pl.BlockSpec((SSD_HEADS, CHUNK), lambda b, c: (0, 0)))


def _ssd_dt_valid(c):
    lane = lax.broadcasted_iota(jnp.int32, (SSD_HEADS, CHUNK), 1)
    return jnp.logical_or(c > 0, lane >= PAD_ROWS)


def _ssd_dt_prep(dt_raw, dt_bias, a_log, bsz, nc):
    sp = _ssd_dt_specs(nc)

    def body(raw_ref, b_ref, al_ref, dt_ref, cum_ref):
        dt = jnp.where(_ssd_dt_valid(pl.program_id(1)), _softplus(raw_ref[0] + b_ref[...]), 0.0)
        dt_ref[0] = dt
        cum_ref[0] = _lane_cumsum(dt * -jnp.exp(al_ref[...]), None)

    o = jax.ShapeDtypeStruct(dt_raw.shape, F32)
    return _pc(body, name="ssd_dt_prep", grid=(bsz, nc), in_specs=[sp["rows"], sp["col"], sp["col"]],
               out_specs=[sp["rows"], sp["rows"]], out_shape=[o, o], vmem=1 << 20)(dt_raw, dt_bias, a_log)


def _ssd_dt_post(dcum, ddt, dt_raw, dt_bias, a_log, bsz, nc):
    sp = _ssd_dt_specs(nc)

    def body(dcum_ref, ddt_ref, raw_ref, b_ref, al_ref, out_ref, dbias_ref, dal_ref):
        b, c = pl.program_id(0), pl.program_id(1)

        @pl.when((b == 0) & (c == 0))
        def _():
            dbias_ref[...] = jnp.zeros_like(dbias_ref)
            dal_ref[...] = jnp.zeros_like(dal_ref)

        valid = _ssd_dt_valid(c)
        pre = raw_ref[0] + b_ref[...]
        dt = jnp.where(valid, _softplus(pre), 0.0)
        a = -jnp.exp(al_ref[...])
        dda = _lane_cumsum(dcum_ref[0], None, rev=True)
        ddt_raw = jnp.where(valid, ddt_ref[0] + dda * a, 0.0) * jax.nn.sigmoid(pre)
        out_ref[0] = ddt_raw
        dbias_ref[...] += jnp.sum(ddt_raw, axis=1, keepdims=True)
        dal_ref[...] += jnp.sum(dda * dt, axis=1, keepdims=True) * a

    acc = jax.ShapeDtypeStruct((SSD_HEADS, CHUNK), F32)
    return _pc(body, name="ssd_dt_post", grid=(bsz, nc),
               in_specs=[sp["rows"], sp["rows"], sp["rows"], sp["col"], sp["col"]],
               out_specs=[sp["rows"], sp["acc"], sp["acc"]],
               out_shape=[jax.ShapeDtypeStruct(dt_raw.shape, F32), acc, acc], vmem=1 << 20,
               )(dcum, ddt, dt_raw, dt_bias, a_log)


def _ssd_core(xs, bm, cm, dt, cum):
    sq = (CHUNK, CHUNK)
    lane8 = lax.broadcasted_iota(jnp.int32, (SSD_HPG, CHUNK), 1)
    lane = lax.broadcasted_iota(jnp.int32, sq, 1)
    row = lax.broadcasted_iota(jnp.int32, sq, 0)
    low = lane < SSD_P
    cb = _dot(_bf(cm), _bf(bm), NT)
    heads = []
    for r in range(SSD_HPG):
        rowb = jnp.broadcast_to(cum[r:r + 1, :], sq)
        colb = rowb.T
        seg = jnp.exp(jnp.where(lane <= row, colb - rowb, NEG))
        dtrow = jnp.broadcast_to(dt[r:r + 1, :], sq)
        lastb = colb[CHUNK - 1:CHUNK, :]
        heads.append(dict(seg=seg, dtrow=dtrow, w=cb * seg * dtrow, ecol=jnp.exp(colb),
                          dec=jnp.exp(lastb - colb) * dtrow.T, elast=jnp.exp(lastb)))

    def pairs(key):
        return jnp.concatenate([jnp.where(low[:heads[0][key].shape[0]], heads[2 * j][key], heads[2 * j + 1][key])
                                for j in range(SSD_HPG // 2)], axis=1)

    return dict(lane8=lane8, low=low, dt=dt, cum=cum, cb=cb, heads=heads,
                expc=pairs("ecol"), dec=pairs("dec"), elast=pairs("elast"))


def _ssd_specs(nc, rev, bsz):
    def cc(c):
        return (nc - 1 - c) if rev else c

    return dict(
        wide=pl.BlockSpec((bsz, CHUNK, SSD_GW), lambda g, c: (0, cc(c), g)),
        narrow=pl.BlockSpec((bsz, CHUNK, SSD_N), lambda g, c: (0, cc(c), g)),
        dtT=pl.BlockSpec((bsz, SSD_HPG, CHUNK), lambda g, c: (0, g, cc(c))),
        hcol=pl.BlockSpec((SSD_HPG, 1), lambda g, c: (g, 0)),
        hacc=pl.BlockSpec((SSD_HPG, CHUNK), lambda g, c: (g, 0)),
        gvec=pl.BlockSpec((1, SSD_GW), lambda g, c: (0, g)),
        state=pl.BlockSpec((bsz, 1, SSD_N, SSD_GW), lambda g, c: (0, g * nc + cc(c), 0, 0)),
    )


def _ssd_chunk_fwd(xs_pre, bm_pre, cm_pre, dt, cum, d_exp, z, gnorm, bsz, nc):
    tp = xs_pre.shape[1]
    sp = _ssd_specs(nc, False, bsz)

    def body(xs_all, bm_all, cm_all, dt_all, cum_all, d_ref, z_all, gn_ref, y_all, yn_all, st_all, st_sall):
        @pl.when(pl.program_id(1) == 0)
        def _():
            st_sall[...] = jnp.zeros_like(st_sall)

        for bi in range(bsz):
            one(xs_all.at[bi], bm_all.at[bi], cm_all.at[bi], dt_all.at[bi], cum_all.at[bi], d_ref, z_all.at[bi],
                gn_ref, y_all.at[bi], yn_all.at[bi], st_all.at[bi], st_sall.at[bi])

    def one(xs_ref, bm_ref, cm_ref, dt_ref, cum_ref, d_ref, z_ref, gn_ref, y_ref, yn_ref, st_ref, st_s):
        state = st_s[...]
        st_ref[0] = state
        xs, bm, cm = _silu(xs_ref[...]), _silu(bm_ref[...]), _silu(cm_ref[...])
        co = _ssd_core(xs, bm, cm, dt_ref[...], cum_ref[...])
        low, hd = co["low"], co["heads"]
        ys = []
        for j in range(SSD_HPG // 2):
            xp = xs[:, j * CHUNK:(j + 1) * CHUNK]
            lhs = jnp.concatenate([hd[2 * j]["w"], hd[2 * j + 1]["w"]], axis=1)
            rhs = jnp.concatenate([jnp.where(low, xp, 0.0), jnp.where(low, 0.0, xp)], axis=0)
            ys.append(_dot(_bf(lhs), _bf(rhs)))
        cmb = _bf(cm)
        y = jnp.concatenate(ys, axis=1) + co["expc"] * _dot(cmb, _bf(state)) + d_ref[...] * xs
        y_ref[...] = y
        yg = y * _silu(z_ref[...])
        rstd = lax.rsqrt(jnp.mean(yg * yg, axis=1, keepdims=True) + NORM_EPS)
        yn_ref[...] = _bf(yg * rstd * gn_ref[...])
        st_s[...] = co["elast"] * state + _dot(_bf(bm), _bf(xs * co["dec"]), TN)

    return _pc(body, name="ssd_chunk_fwd", grid=(SSD_GROUPS, nc),
               in_specs=[sp["wide"], sp["narrow"], sp["narrow"], sp["dtT"], sp["dtT"], sp["gvec"], sp["wide"],
                         sp["gvec"]],
               out_specs=[sp["wide"], sp["wide"], sp["state"]],
               out_shape=[jax.ShapeDtypeStruct((bsz, tp, SSD_INNER), F32),
                          jax.ShapeDtypeStruct((bsz, tp, SSD_INNER), BF16),
                          jax.ShapeDtypeStruct((bsz, SSD_GROUPS * nc, SSD_N, SSD_GW), F32)],
               scratch=[pltpu.VMEM((bsz, SSD_N, SSD_GW), F32)], vmem=12 << 20,
               )(xs_pre, bm_pre, cm_pre, dt, cum, d_exp, z, gnorm)


def _ssd_chunk_bwd(dyn, xs_pre, bm_pre, cm_pre, dt, cum, d_exp, z, gnorm, y, states, bsz, nc):
    tp = xs_pre.shape[1]
    sp = _ssd_specs(nc, True, bsz)

    def body(dyn_all, xs_all, bm_all, cm_all, dt_all, cum_all, d_ref, z_all, gn_ref, y_all, st_all,
             dxs_all, dbm_all, dcm_all, dz_all, dcum_all, ddt_all, dgn_ref, dd_ref, ds_sall):
        @pl.when(pl.program_id(1) == 0)
        def _():
            for ref in (dgn_ref, dd_ref, ds_sall):
                ref[...] = jnp.zeros_like(ref)

        for bi in range(bsz):
            one(dyn_all.at[bi], xs_all.at[bi], bm_all.at[bi], cm_all.at[bi], dt_all.at[bi], cum_all.at[bi], d_ref,
                z_all.at[bi], gn_ref, y_all.at[bi], st_all.at[bi], dxs_all.at[bi], dbm_all.at[bi], dcm_all.at[bi],
                dz_all.at[bi], dcum_all.at[bi], ddt_all.at[bi], dgn_ref, dd_ref, ds_sall.at[bi])

    def one(dyn_ref, xs_ref, bm_ref, cm_ref, dt_ref, cum_ref, d_ref, z_ref, gn_ref, y_ref, st_ref,
            dxs_ref, dbm_ref, dcm_ref, dz_ref, dcum_ref, ddt_ref, dgn_ref, dd_ref, ds_s):
        xs_p, bm_p, cm_p = xs_ref[...], bm_ref[...], cm_ref[...]
        xs, bm, cm = _silu(xs_p), _silu(bm_p), _silu(cm_p)
        state = st_ref[0]
        co = _ssd_core(xs, bm, cm, dt_ref[...], cum_ref[...])
        low, hd, lane8, cb = co["low"], co["heads"], co["lane8"], co["cb"]
        dt, cum = co["dt"], co["cum"]
        sub8 = lax.broadcasted_iota(jnp.int32, (SSD_HPG, CHUNK), 0)
        eh = _head_sum_matrix()

        def head_rows(full):
            return _dot_terms(eh, full, NT, exact_rhs=False, terms=2)

        def head_col(vec):
            return jnp.sum(eh * vec, axis=1, keepdims=True)

        yv, zv, gn = y_ref[...], z_ref[...], gn_ref[...]
        sz = _silu(zv)
        yg = yv * sz
        rstd = lax.rsqrt(jnp.mean(yg * yg, axis=1, keepdims=True) + NORM_EPS)
        yh = yg * rstd
        dyn = dyn_ref[...]
        dgn_ref[...] += jnp.sum(dyn * yh, axis=0, keepdims=True)
        dyh = dyn * gn
        dyg = rstd * (dyh - yh * jnp.mean(dyh * yh, axis=1, keepdims=True))
        dz_ref[...] = _bf(dyg * yv * _dsilu(zv))
        dy = dyg * sz
        dxs = dy * d_ref[...]
        dd_ref[...] += head_col(jnp.sum(dy * xs, axis=0, keepdims=True))
        cmb, bmb, stb = _bf(cm), _bf(bm), _bf(state)
        ysv = _dot(cmb, stb)
        expc = co["expc"]
        dys = _bf(dy * expc)
        dcum = head_rows(dy * ysv * expc)
        dcm = _dot(dys, stb, NT)
        dstate_out = _dot(cmb, dys, TN)
        dcb = jnp.zeros((CHUNK, CHUNK), F32)
        ddt = jnp.zeros((SSD_HPG, CHUNK), F32)
        dxs_pairs = []
        for j in range(SSD_HPG // 2):
            sl = slice(j * CHUNK, (j + 1) * CHUNK)
            dyp, xp = dy[:, sl], _bf(xs[:, sl])
            lhs = _bf(jnp.concatenate([hd[2 * j]["w"], hd[2 * j + 1]["w"]], axis=1))
            both = _dot(lhs, _bf(dyp), TN)
            dxs_pairs.append(jnp.where(low, both[:CHUNK], both[CHUNK:]))
            for q, msk in ((2 * j, low), (2 * j + 1, jnp.logical_not(low))):
                h = hd[q]
                dw = _dot(_bf(jnp.where(msk, dyp, 0.0)), xp, NT)
                dcb = dcb + dw * h["seg"] * h["dtrow"]
                e_ = dw * h["w"]
                dcum_r = jnp.sum(e_.T, axis=0, keepdims=True) - jnp.sum(e_, axis=0, keepdims=True)
                ddt_r = jnp.sum(dw * cb * h["seg"], axis=0, keepdims=True)
                dcum = dcum + jnp.where(sub8 == q, dcum_r, 0.0)
                ddt = ddt + jnp.where(sub8 == q, ddt_r, 0.0)
        dxs = dxs + jnp.concatenate(dxs_pairs, axis=1)
        dcbb = _bf(dcb)
        dcm = dcm + _dot(dcbb, bmb)
        dbm = _dot(dcbb, cmb, TN)
        dsn = ds_s[...]
        dsb = _bf(dsn)
        dec = co["dec"]
        dbm = dbm + _dot(_bf(xs * dec), dsb, NT)
        dxd = _dot(bmb, dsb)
        dxs = dxs + dxd * dec
        ddec = head_rows(dxd * xs)
        last = cum[:, CHUNK - 1:CHUNK]
        erow = jnp.exp(last - cum)
        ddt = ddt + ddec * erow
        dla = ddec * erow * dt
        dlast = (jnp.sum(dla, axis=1, keepdims=True)
                 + head_col(jnp.sum(dsn * state, axis=0, keepdims=True)) * jnp.exp(last))
        dcum_ref[...] = dcum - dla + jnp.where(lane8 == CHUNK - 1, dlast, 0.0)
        ddt_ref[...] = ddt
        ds_s[...] = co["elast"] * dsn + dstate_out
        dxs_ref[...] = dxs * _dsilu(xs_p)
        dbm_ref[...] = dbm * _dsilu(bm_p)
        dcm_ref[...] = dcm * _dsilu(cm_p)

    st = jax.ShapeDtypeStruct
    hacc = st((SSD_HEADS, CHUNK), F32)
    return _pc(body, name="ssd_chunk_bwd", grid=(SSD_GROUPS, nc),
               in_specs=[sp["wide"], sp["wide"], sp["narrow"], sp["narrow"], sp["dtT"], sp["dtT"], sp["gvec"],
                         sp["wide"], sp["gvec"], sp["wide"], sp["state"]],
               out_specs=[sp["wide"], sp["narrow"], sp["narrow"], sp["wide"], sp["dtT"], sp["dtT"], sp["gvec"],
                          sp["hacc"]],
               out_shape=[st((bsz, tp, SSD_INNER), F32), st((bsz, tp, SSD_BC), F32), st((bsz, tp, SSD_BC), F32),
                          st((bsz, tp, SSD_INNER), BF16), st((bsz, SSD_HEADS, tp), F32),
                          st((bsz, SSD_HEADS, tp), F32), st((1, SSD_INNER), F32), hacc],
               scratch=[pltpu.VMEM((bsz, SSD_N, SSD_GW), F32)], vmem=20 << 20,
               )(dyn, xs_pre, bm_pre, cm_pre, dt, cum, d_exp, z, gnorm, y, states)


SSD_BC = SSD_GROUPS * SSD_N


def _ssd_weights(conv_w, conv_b, dt_bias, a_log, d, gnorm):
    cuts = (0, SSD_INNER, SSD_INNER + SSD_BC, SSD_INNER + 2 * SSD_BC)
    return dict(
        conv_w=[conv_w[:, cuts[i]:cuts[i + 1]] for i in range(3)],
        conv_b=[conv_b[cuts[i]:cuts[i + 1]] for i in range(3)],
        dt_bias=dt_bias.reshape(SSD_HEADS, 1), a_log=a_log.reshape(SSD_HEADS, 1),
        d_exp=jnp.repeat(d.reshape(SSD_HEADS), SSD_P).reshape(1, SSD_INNER), gnorm=gnorm.reshape(1, SSD_INNER))


def _ssd_layer_fwd(z, xs_in, bm_in, cm_in, dt_rows, w, bsz, nc):
    pres = [_conv_fwd(a, w["conv_w"][i], w["conv_b"][i], bsz, nc, name=f"ssd_conv{i}")
            for i, a in enumerate((xs_in, bm_in, cm_in))]
    def seq(a):
        return a.reshape(bsz, nc * CHUNK, a.shape[-1])

    dt_t = jnp.swapaxes(seq(dt_rows)[:, :, :SSD_HEADS], 1, 2)
    dt, cum = _ssd_dt_prep(dt_t, w["dt_bias"], w["a_log"], bsz, nc)
    y, yn, states = _ssd_chunk_fwd(seq(pres[0]), seq(pres[1]), seq(pres[2]), dt, cum, w["d_exp"], seq(z),
                                   w["gnorm"], bsz, nc)
    return yn.reshape(-1, SSD_INNER), dict(pres=pres, dt_t=dt_t, dt=dt, cum=cum, y=y, states=states)


def _ssd_layer_bwd(dyn, z, xs_in, bm_in, cm_in, sv, w, bsz, nc):
    pres = sv["pres"]

    def seq(a):
        return a.reshape(bsz, nc * CHUNK, a.shape[-1])

    def rows(a):
        return a.reshape(-1, a.shape[-1])

    dxs_p, dbm_p, dcm_p, dz, dcum, ddt_direct, dgn, dd = _ssd_chunk_bwd(
        seq(dyn), seq(pres[0]), seq(pres[1]), seq(pres[2]), sv["dt"], sv["cum"], w["d_exp"], seq(z), w["gnorm"],
        sv["y"], sv["states"], bsz, nc)
    ddt_t, dbias, dal = _ssd_dt_post(dcum, ddt_direct, sv["dt_t"], w["dt_bias"], w["a_log"], bsz, nc)
    dz = rows(dz)
    outs = [_conv_bwd(rows(dp), a, w["conv_w"][i], bsz, nc, name=f"ssd_conv_bwd{i}")
            for i, (dp, a) in enumerate(((dxs_p, xs_in), (dbm_p, bm_in), (dcm_p, cm_in)))]
    ddt = _bf(_pad_lanes(rows(jnp.swapaxes(ddt_t, 1, 2))))
    grads = dict(
        ssd_conv_w=jnp.concatenate([o[1] for o in outs], axis=1),
        ssd_conv_b=jnp.concatenate([o[2] for o in outs], axis=1),
        ssd_dt_bias=dbias[:, 0], ssd_a_log=dal[:, 0], ssd_d=dd[:, 0], ssd_gnorm=dgn)
    return dz, outs[0][0], outs[1][0], outs[2][0], ddt, grads


WNAMES = ("meta_tokens", "ab_norm", "ab_w_in", "s5_lambda_re", "s5_lambda_im", "s5_log_dt", "s5_b_re", "s5_b_im",
          "s5_c_re", "s5_c_im", "s5_d", "s5_glu_w", "s5_glu_b", "ml_conv_w", "ml_conv_b", "ml_wq", "ml_wk", "ml_wv",
          "ml_w_gate", "ml_b_gate", "ml_norm", "ml_skip", "ab_w_out", "ssd_norm", "ssd_w_in", "ssd_conv_w",
          "ssd_conv_b", "ssd_dt_bias", "ssd_a_log", "ssd_d", "ssd_gnorm", "ssd_w_out", "final_norm")
SHARD_AXIS = dict(meta_tokens=1, ab_w_in=2, s5_glu_w=1, ml_conv_w=2, ml_wq=1, ml_wk=1, ml_wv=1, ml_w_gate=1,
                  ab_w_out=1, ssd_norm=1, ssd_w_in=2, ssd_conv_w=2, ssd_conv_b=1, ssd_gnorm=1, ssd_w_out=1)
BIG = ("ab_w_in", "s5_glu_w", "ab_w_out", "ssd_w_in", "ssd_w_out")
SMALL = tuple(n for n in WNAMES if n in SHARD_AXIS and n not in BIG)
REPL = tuple(n for n in WNAMES if n not in SHARD_AXIS)
PACK_ALIGN = 8 * 128


def _pack(arrs):
    lead = arrs[0][1]
    parts = []
    for a, nlead in arrs:
        f = a.reshape(a.shape[:nlead] + (-1,))
        parts.append(jnp.pad(f, [(0, 0)] * nlead + [(0, (-f.shape[-1]) % PACK_ALIGN)]))
    flat = jnp.concatenate(parts, axis=lead)
    return flat.reshape(flat.shape[:lead] + (-1, 128))


def _unpack(p, shapes):
    out, off = [], 0
    lead = p.shape[:-2]
    flat = p.reshape(lead + (-1,))
    for s in shapes:
        n = math.prod(s)
        out.append(flat[..., off:off + n].reshape(lead + tuple(s)))
        off += -(-n // PACK_ALIGN) * PACK_ALIGN
    return out


def _assemble(g, axis):
    m = jnp.moveaxis(g, 0, axis)
    return m.reshape(m.shape[:axis] + (m.shape[axis] * m.shape[axis + 1],) + m.shape[axis + 2:])


def _split(full, axis):
    s = full.shape
    m = full.reshape(s[:axis] + (N_DEV, s[axis] // N_DEV) + s[axis + 1:])
    return jnp.moveaxis(m, axis, 0)


def kernel(x, *rest):
    nw = len(WNAMES)
    w = dict(zip(WNAMES, rest[:nw]))
    loss_target = rest[nw]
    mom = dict(zip(WNAMES, rest[nw + 1:2 * nw + 1]))
    var = dict(zip(WNAMES, rest[2 * nw + 1:3 * nw + 1]))
    bsz = x.shape[0]
    nc = 1 + SEQ // CHUNK
    tp = nc * CHUNK

    local = {n: _bf(w[n][0]) for n in BIG}
    small_local = _pack([(w[n], 0) for n in SMALL])
    gs = _exchange_start([small_local], ["ag"], name="gather_s")
    ga = _exchange_start([local["ab_w_in"]], ["ag"], name="gather_a", dep=gs["token"], peers=SAME_CORE[1:])
    got_s = _exchange_wait(gs, ga["token"])

    def assemble_big(n, got):
        return _assemble(got[:, None], SHARD_AXIS[n])[0]

    full = {}
    for n, g in zip(SMALL, _unpack(got_s[0], [w[n].shape for n in SMALL])):
        full[n] = _assemble(g, SHARD_AXIS[n])[0] if n != "meta_tokens" else _assemble(g, SHARD_AXIS[n])
    for n in REPL:
        full[n] = w[n][0] if n != "final_norm" else w[n]
    glu_b = full["s5_glu_b"].reshape(1, S5_WIDTH)
    meta = jnp.broadcast_to(full["meta_tokens"][None], (bsz, N_META, D_MODEL))
    h0 = jnp.concatenate([jnp.zeros((bsz, PAD_ROWS, D_MODEL), F32), meta, x], axis=1).reshape(bsz * tp, D_MODEL)
    xn0 = _rms_fwd(h0, full["ab_norm"], name="rms0")
    s5p, s5_vjp = _s5_tables(*[full[n] for n in ("s5_lambda_re", "s5_lambda_im", "s5_log_dt", "s5_b_re", "s5_b_im",
                                                   "s5_c_re", "s5_c_im", "s5_d")])
    mlw = _ml_weights(*[full[n] for n in ("ml_conv_w", "ml_conv_b", "ml_wq", "ml_wk", "ml_wv", "ml_w_gate",
                                           "ml_b_gate", "ml_norm", "ml_skip")])
    got_a = _exchange_wait(ga, [xn0, s5p["wbr"], s5p["wcr"], s5p["pr"], mlw["wq"], mlw["wk"], mlw["wv"], mlw["wgq"]])
    fwd_a = _sibling_forward_start(got_a[0], name="gather_a2")
    got_a = [_sibling_forward_wait(fwd_a, fwd_a["token"])]
    gb = _exchange_start([local["s5_glu_w"], local["ab_w_out"]], ["ag", "ag"], name="gather_b", dep=got_a[0])
    gc = _exchange_start([local["ssd_w_in"], local["ssd_w_out"]], ["ag", "ag"], name="gather_c", dep=gb["token"])
    full["ab_w_in"] = assemble_big("ab_w_in", got_a[0])
    cuts0 = (0, S5_WIDTH, 2 * S5_WIDTH, 2 * S5_WIDTH + ML_WIDTH, 2 * (S5_WIDTH + ML_WIDTH))
    w_in0 = [full["ab_w_in"][:, cuts0[i]:cuts0[i + 1]] for i in range(4)]

    u, za, xb, zb = [_mm(xn0, wi, "NN", name=f"in0_{i}") for i, wi in enumerate(w_in0)]
    got_b = []

    def glu_w_after(scan_out):
        got_b.extend(_exchange_wait(gb, scan_out))
        return assemble_big("s5_glu_w", got_b[0])

    sv5 = _s5_layer_fwd(u, s5p, glu_w_after, bsz, nc)
    glu_w = assemble_big("s5_glu_w", got_b[0])
    w_out0 = assemble_big("ab_w_out", got_b[1])
    w_out0 = [w_out0[:S5_WIDTH], w_out0[S5_WIDTH:]]
    ya = _s5_post(sv5["y1"], sv5["glu_pre"], glu_b, za)
    yb, svm = _ml_layer_fwd(xb, zb, mlw, bsz, nc)
    h1 = _mm(ya, w_out0[0], "NN", name="out0_a", add=h0)
    h1 = _mm(yb, w_out0[1], "NN", name="out0_b", add=h1)
    got_c = _exchange_wait(gc, h1)
    w_in1, w_out1 = assemble_big("ssd_w_in", got_c[0]), assemble_big("ssd_w_out", got_c[1])
    cuts1 = (0, SSD_INNER, 2 * SSD_INNER, 2 * SSD_INNER + SSD_BC, 2 * SSD_INNER + 2 * SSD_BC)
    w_in1 = [w_in1[:, cuts1[i]:cuts1[i + 1]] for i in range(4)] + [_pad_lanes(w_in1[:, cuts1[4]:])]
    xn1 = _rms_fwd(h1, full["ssd_norm"], name="rms1")
    z1, xs_in, bm_in, cm_in, dt_rows = [_mm(xn1, wi, "NN", name=f"in1_{i}") for i, wi in enumerate(w_in1)]
    ssdw = _ssd_weights(*[full[n] for n in ("ssd_conv_w", "ssd_conv_b", "ssd_dt_bias", "ssd_a_log", "ssd_d",
                                             "ssd_gnorm")])
    yn, svs = _ssd_layer_fwd(z1, xs_in, bm_in, cm_in, dt_rows, ssdw, bsz, nc)
    h2 = _mm(yn, w_out1, "NN", name="out1", add=h1)
    loss_part, dh2, dfinal, dh2_b = _final_loss(h2, full["final_norm"], loss_target, bsz, nc)

    g = {"final_norm": dfinal}
    dyn = _mm(dh2_b, w_out1, "NT", name="d_out1")
    g["ssd_w_out"] = _mm(yn, dh2_b, "TN", name="dw_out1", out_dtype=BF16)
    dz1, dxs, dbm, dcm, ddt, gs = _ssd_layer_bwd(dyn, z1, xs_in, bm_in, cm_in, svs, ssdw, bsz, nc)
    g.update(gs)
    dps1 = (dz1, dxs, dbm, dcm, ddt)
    dxn1 = None
    for i, (dp, wi) in enumerate(zip(dps1, w_in1)):
        dxn1 = _mm(dp, wi, "NT", name=f"d_in1_{i}", add=dxn1)
    dw1 = [_mm(xn1, dp, "TN", name=f"dw_in1_{i}", out_dtype=BF16) for i, dp in enumerate(dps1)]
    g["ssd_w_in"] = jnp.concatenate(dw1[:4] + [dw1[4][:, :SSD_HEADS]], axis=1)

    def local_shape(n):
        return w[n].shape

    def slabs(n):
        gf = g[n].reshape((1,) + tuple(g[n].shape)) if n != "meta_tokens" else g[n]
        full_shape = tuple(d * (N_DEV if i == SHARD_AXIS[n] else 1) for i, d in enumerate(local_shape(n)))
        return _split(gf.reshape(full_shape), SHARD_AXIS[n])

    x1 = _exchange_start([slabs("ssd_w_in")[:, 0], slabs("ssd_w_out")[:, 0]], ["a2a", "a2a"], name="grads_1")
    dh1, g["ssd_norm"], dh1_b = _rms_bwd(h1, full["ssd_norm"], dxn1, dh2, name="rms1_bwd", dep=x1["token"])
    dya = _mm(dh1_b, w_out0[0], "NT", name="d_out0_a")
    dyb = _mm(dh1_b, w_out0[1], "NT", name="d_out0_b")
    g["ab_w_out"] = jnp.concatenate([_mm(ya, dh1_b, "TN", name="dw_out0_a", out_dtype=BF16),
                                     _mm(yb, dh1_b, "TN", name="dw_out0_b", out_dtype=BF16)], axis=0)
    du, dza, g5 = _s5_layer_bwd(dya, u, za, sv5, s5p, s5_vjp, glu_w, glu_b, bsz, nc)
    g.update(g5)
    x2 = _exchange_start([slabs("ab_w_out")[:, 0], _bf(slabs("s5_glu_w")[:, 0])], ["a2a", "a2a"], name="grads_2")
    dxb, dzb, gm = _ml_layer_bwd(dyb, xb, zb, svm, mlw, bsz, nc, dep=x2["token"])
    g.update(gm)
    dps0 = (du, dza, dxb, dzb)
    dw0 = [_mm(xn0, dp, "TN", name=f"dw_in0_{i}", out_dtype=BF16, tn=S5_WIDTH, slabs=True) for i, dp in enumerate(dps0)]
    dw_in0_slabs = jnp.concatenate(dw0, axis=0)
    x3 = _exchange_start([dw_in0_slabs], ["a2a"], name="grads_3")
    dxn0 = None
    for i, (dp, wi) in enumerate(zip(dps0, w_in0)):
        dxn0 = _mm(dp, wi, "NT", name=f"d_in0_{i}", add=dxn0, dep=x3["token"] if i == 0 else None)
    grad_x, d_chunk0, g["ab_norm"] = _rms_bwd_first(h0, full["ab_norm"], dxn0, dh1, bsz, nc, name="rms0_bwd")
    g["meta_tokens"] = jnp.sum(d_chunk0[:, PAD_ROWS:], axis=0)

    small_g = _pack([(slabs(n), 1) for n in SMALL])
    repl_g = _pack([(g[n], 0) for n in REPL])
    x4 = _exchange_start([small_g, repl_g, loss_part], ["a2a", "ag", "ag"], name="grads_4")

    def update_big(n, gp):
        return _adamw(w[n][0], mom[n][0], var[n][0], gp, name=f"adamw_{n}")

    res = {}
    ex1 = _exchange_wait(x1, x4["token"])
    res["ssd_w_in"], res["ssd_w_out"] = update_big("ssd_w_in", ex1[0]), update_big("ssd_w_out", ex1[1])
    ex2 = _exchange_wait(x2, res["ssd_w_out"][0])
    res["ab_w_out"], res["s5_glu_w"] = update_big("ab_w_out", ex2[0]), update_big("s5_glu_w", ex2[1])
    ex3 = _exchange_wait(x3, [res[n][0] for n in ("ssd_w_in", "ssd_w_out", "ab_w_out", "s5_glu_w")])
    res["ab_w_in"] = update_big("ab_w_in", ex3[0])
    ex4 = _exchange_wait(x4, res["ab_w_in"][0])
    loss = jnp.sum(ex4[2][:, 0, 0])
    for names, gp, tag in ((SMALL, ex4[0], "small"), (REPL, ex4[1], "repl")):
        shapes = [local_shape(n) for n in names]
        packs = [_pack([(d[n], 0) for n in names]) for d in (w, mom, var)]
        outs = _adamw(packs[0], packs[1], packs[2], gp, name=f"adamw_{tag}")
        for k, o in enumerate(outs):
            for n, a in zip(names, _unpack(o, shapes)):
                res.setdefault(n, [None] * 4)[k] = a
    outs = [loss, grad_x]
    for k in range(4):
        outs += [res[n][k].reshape(local_shape(n)) for n in WNAMES]
    return tuple(outs)
```

```python
import functools
import math

import jax
import jax.numpy as jnp
from jax import lax
from jax.experimental import pallas as pl
from jax.experimental.pallas import tpu as pltpu

F32 = jnp.float32
BF16 = jnp.bfloat16

D_MODEL = 2048
SEQ = 2048
N_META = 16
CHUNK = 128
PAD_ROWS = CHUNK - N_META
NORM_EPS = 1e-6
HEAD_NORM_EPS = 1e-5
S5_WIDTH = 1024
S5_GROUPS = 64
S5_GROUP_SIZE = 16
S5_STATE = 64
S5_GB = 8
S5_LANES = S5_GB * S5_STATE
ML_WIDTH = 3072
ML_HEADS = 8
ML_DH = 384
ML_CONV = 4
QKV_BLOCK = 4
SSD_INNER = 4096
SSD_HEADS = 64
SSD_P = 64
SSD_N = 128
SSD_GROUPS = 8
SSD_HPG = 8
SSD_GW = SSD_HPG * SSD_P
N_DEV = 8
ADAM_LR, ADAM_B1, ADAM_B2, ADAM_EPS, ADAM_WD, ADAM_STEP = 0.001, 0.9, 0.999, 1e-08, 0.01, 10
NEG = -1e30
VMEM_CAP = 60 * 1024 * 1024
MM_BLOCK_BUDGET = 22 * 1024 * 1024
MESH = pl.DeviceIdType.MESH

NN = (((1,), (0,)), ((), ()))
NT = (((1,), (1,)), ((), ()))
TN = (((0,), (0,)), ((), ()))


def _dot(a, b, dims=NN):
    return lax.dot_general(a, b, dims, preferred_element_type=F32)


def _bf(x):
    return x.astype(BF16)


def _pick(n, cands):
    for c in cands:
        if n % c == 0:
            return c
    return n


def _nbytes(shape, dtype):
    return math.prod(shape) * jnp.dtype(dtype).itemsize


ANY_SPEC = pl.BlockSpec(memory_space=pl.ANY)


def _pc(body, *, name, grid, in_specs, out_specs, out_shape, scratch=(), vmem=None, dep=None):
    limit = None if vmem is None else int(min(VMEM_CAP, max(32 * 1024 * 1024, 2 * vmem + (8 << 20))))
    n_in = len(in_specs)
    if dep is not None:
        inner = body

        def body(*refs):
            inner(*refs[:n_in], *refs[n_in + 1:])

        in_specs = list(in_specs) + [ANY_SPEC]
    call = pl.pallas_call(
        body, name=name, grid=grid, in_specs=in_specs, out_specs=out_specs, out_shape=out_shape,
        scratch_shapes=list(scratch),
        compiler_params=pltpu.CompilerParams(dimension_semantics=("arbitrary",) * len(grid), vmem_limit_bytes=limit))
    return call if dep is None else (lambda *args: call(*args, dep))


def _silu(x):
    return x * jax.nn.sigmoid(x)


def _dsilu(x):
    s = jax.nn.sigmoid(x)
    return s * (1.0 + x * (1.0 - s))


def _gelu_and_grad(x):
    c0 = math.sqrt(2.0 / math.pi)
    inner = c0 * (x + 0.044715 * x * x * x)
    t = jnp.tanh(inner)
    g = 0.5 * x * (1.0 + t)
    dg = 0.5 * (1.0 + t) + 0.5 * x * (1.0 - t * t) * c0 * (1.0 + 3 * 0.044715 * x * x)
    return g, dg


def _mm(a, b, mode, *, name, add=None, out_dtype=F32, tn=None, slabs=False, dep=None):
    if mode == "NN":
        (m, k), (k2, n) = a.shape, b.shape
    elif mode == "NT":
        (m, k), (n, k2) = a.shape, b.shape
    else:
        (k, m), (k2, n) = a.shape, b.shape
    assert k == k2, (a.shape, b.shape, mode)
    tm = _pick(m, (1088, 1024, 768, 512, 384, 256, 128))
    def block_bytes(tk, tn_):
        return (_nbytes((tm, tk), a.dtype) + _nbytes((tk, tn_), b.dtype) + _nbytes((tm, tn_), out_dtype)
                + (_nbytes((tm, tn_), F32) if add is not None else 0))

    budget = MM_BLOCK_BUDGET // 2 if mode == "TN" else MM_BLOCK_BUDGET
    if tn is None:
        tn = _pick(n, (512, 384, 256, 128))
        if mode != "TN" and n % 1024 == 0 and block_bytes(k, 1024) <= (2 * budget) // 3:
            tn = 1024
    tk = k if block_bytes(k, tn) <= budget else _pick(k, (2176, 2048, 1088, 1024, 768, 512, 384, 256, 128))
    nk = k // tk
    dims = {"NN": NN, "NT": NT, "TN": TN}[mode]

    def body(*refs):
        a_ref, b_ref = refs[0], refs[1]
        add_ref = refs[2] if add is not None else None
        o_ref = refs[3] if add is not None else refs[2]

        def finish(r):
            if add_ref is not None:
                r = r + add_ref[...]
            o_ref[...] = r.reshape(o_ref.shape).astype(o_ref.dtype)

        prod = _dot(_bf(a_ref[...]), _bf(b_ref[...]), dims)
        if nk == 1:
            finish(prod)
            return
        acc_ref = refs[-1]
        kk = pl.program_id(2)

        @pl.when(kk == 0)
        def _():
            acc_ref[...] = prod

        @pl.when(kk > 0)
        def _():
            acc_ref[...] += prod

        @pl.when(kk == nk - 1)
        def _():
            finish(acc_ref[...])

    if mode == "NN":
        a_spec = pl.BlockSpec((tm, tk), lambda i, j, kk: (i, kk))
        b_spec = pl.BlockSpec((tk, tn), lambda i, j, kk: (kk, j))
    elif mode == "NT":
        a_spec = pl.BlockSpec((tm, tk), lambda i, j, kk: (i, kk))
        b_spec = pl.BlockSpec((tn, tk), lambda i, j, kk: (j, kk))
    else:
        a_spec = pl.BlockSpec((tk, tm), lambda i, j, kk: (kk, i))
        b_spec = pl.BlockSpec((tk, tn), lambda i, j, kk: (kk, j))
    in_specs = [a_spec, b_spec]
    args = [a, b]
    if add is not None:
        in_specs.append(pl.BlockSpec((tm, tn), lambda i, j, kk: (i, j)))
        args.append(add)
    if slabs:
        out_shape = jax.ShapeDtypeStruct((n // tn, m, tn), out_dtype)
        out_spec = pl.BlockSpec((1, tm, tn), lambda i, j, kk: (j, i, 0))
    else:
        out_shape = jax.ShapeDtypeStruct((m, n), out_dtype)
        out_spec = pl.BlockSpec((tm, tn), lambda i, j, kk: (i, j))
    return _pc(body, name=name, grid=(m // tm, n // tn, nk), in_specs=in_specs, out_specs=out_spec,
               out_shape=out_shape, scratch=[] if nk == 1 else [pltpu.VMEM((tm, tn), F32)],
               vmem=block_bytes(tk, tn) + (0 if nk == 1 else _nbytes((tm, tn), F32) // 2), dep=dep)(*args)


def _rms_fwd(x, g, *, name):
    r, d = x.shape
    tm = _pick(r, (256, 128))

    def body(x_ref, g_ref, o_ref):
        xv = x_ref[...]
        rstd = lax.rsqrt(jnp.mean(xv * xv, axis=1, keepdims=True) + NORM_EPS)
        o_ref[...] = (xv * rstd * g_ref[...]).astype(o_ref.dtype)

    return _pc(body, name=name, grid=(r // tm,),
               in_specs=[pl.BlockSpec((tm, d), lambda i: (i, 0)), pl.BlockSpec((1, d), lambda i: (0, 0))],
               out_specs=pl.BlockSpec((tm, d), lambda i: (i, 0)), out_shape=jax.ShapeDtypeStruct((r, d), BF16),
               vmem=tm * d * 6)(x, g.reshape(1, d))


def _rms_bwd(x, g, dxn, dres, *, name, dep=None):
    r, d = x.shape
    tm = _pick(r, (256, 128))

    def body(x_ref, g_ref, dxn_ref, dres_ref, dx_ref, dg_ref, db_ref):
        @pl.when(pl.program_id(0) == 0)
        def _():
            dg_ref[...] = jnp.zeros_like(dg_ref)

        xv = x_ref[...]
        rstd = lax.rsqrt(jnp.mean(xv * xv, axis=1, keepdims=True) + NORM_EPS)
        xh = xv * rstd
        dy = dxn_ref[...]
        dg_ref[...] += jnp.sum(dy * xh, axis=0, keepdims=True)
        dyg = dy * g_ref[...]
        dx_ref[...] = dres_ref[...] + rstd * (dyg - xh * jnp.mean(dyg * xh, axis=1, keepdims=True))

        db_ref[...] = _bf(dx_ref[...])

    row = pl.BlockSpec((tm, d), lambda i: (i, 0))
    vec = pl.BlockSpec((1, d), lambda i: (0, 0))
    return _pc(body, name=name, grid=(r // tm,), in_specs=[row, vec, row, row], out_specs=[row, vec, row],
               out_shape=[jax.ShapeDtypeStruct((r, d), F32), jax.ShapeDtypeStruct((1, d), F32),
                          jax.ShapeDtypeStruct((r, d), BF16)],
               vmem=tm * d * 18, dep=dep)(x, g.reshape(1, d), dxn, dres)


def _rms_bwd_first(x, g, dxn, dres, bsz, nc, *, name):
    d = x.shape[1]

    def body(x_ref, g_ref, dxn_ref, dres_ref, gx_ref, d0_ref, dg_ref):
        b, c = pl.program_id(0), pl.program_id(1)

        @pl.when((b == 0) & (c == 0))
        def _():
            dg_ref[...] = jnp.zeros_like(dg_ref)

        xv = x_ref[...]
        rstd = lax.rsqrt(jnp.mean(xv * xv, axis=1, keepdims=True) + NORM_EPS)
        xh = xv * rstd
        dy = dxn_ref[...]
        dg_ref[...] += jnp.sum(dy * xh, axis=0, keepdims=True)
        dyg = dy * g_ref[...]
        dx = dres_ref[...] + rstd * (dyg - xh * jnp.mean(dyg * xh, axis=1, keepdims=True))

        @pl.when(c == 0)
        def _():
            d0_ref[0] = dx

        @pl.when(c > 0)
        def _():
            gx_ref[0] = dx

    row = pl.BlockSpec((CHUNK, d), lambda b, c: (b * nc + c, 0))
    vec = pl.BlockSpec((1, d), lambda b, c: (0, 0))
    return _pc(body, name=name, grid=(bsz, nc), in_specs=[row, vec, row, row],
               out_specs=[pl.BlockSpec((1, CHUNK, d), lambda b, c: (b, jnp.maximum(c - 1, 0), 0)),
                          pl.BlockSpec((1, CHUNK, d), lambda b, c: (b, 0, 0)), vec],
               out_shape=[jax.ShapeDtypeStruct((bsz, (nc - 1) * CHUNK, d), F32),
                          jax.ShapeDtypeStruct((bsz, CHUNK, d), F32), jax.ShapeDtypeStruct((1, d), F32)],
               vmem=CHUNK * d * 24)(x, g.reshape(1, d), dxn, dres)


def _final_loss(h, g, target, bsz, nc):
    d = h.shape[1]

    def body(h_ref, g_ref, t_ref, loss_ref, dh_ref, dg_ref, db_ref):
        b, c = pl.program_id(0), pl.program_id(1)

        @pl.when((b == 0) & (c == 0))
        def _():
            loss_ref[...] = jnp.zeros_like(loss_ref)
            dg_ref[...] = jnp.zeros_like(dg_ref)

        @pl.when(c == 0)
        def _():
            dh_ref[...] = jnp.zeros_like(dh_ref)
            db_ref[...] = jnp.zeros_like(db_ref)

        @pl.when(c > 0)
        def _():
            xv = h_ref[...]
            rstd = lax.rsqrt(jnp.mean(xv * xv, axis=1, keepdims=True) + NORM_EPS)
            xh = xv * rstd
            gv = g_ref[...]
            err = xh * gv - t_ref[0]
            loss_ref[...] += 0.5 * jnp.sum(jnp.mean(err * err, axis=1, keepdims=True))
            dy = err * (1.0 / d)
            dg_ref[...] += jnp.sum(dy * xh, axis=0, keepdims=True)
            dyg = dy * gv
            dh = rstd * (dyg - xh * jnp.mean(dyg * xh, axis=1, keepdims=True))
            dh_ref[...] = dh
            db_ref[...] = _bf(dh)

    row = pl.BlockSpec((CHUNK, d), lambda b, c: (b * nc + c, 0))
    vec = pl.BlockSpec((1, d), lambda b, c: (0, 0))
    return _pc(body, name="final_loss", grid=(bsz, nc),
               in_specs=[row, vec, pl.BlockSpec((1, CHUNK, d), lambda b, c: (b, jnp.maximum(c - 1, 0), 0))],
               out_specs=[pl.BlockSpec((8, 128), lambda b, c: (0, 0)), row, vec, row],
               out_shape=[jax.ShapeDtypeStruct((8, 128), F32), jax.ShapeDtypeStruct(h.shape, F32),
                          jax.ShapeDtypeStruct((1, d), F32), jax.ShapeDtypeStruct(h.shape, BF16)],
               vmem=CHUNK * d * 18)(h, g.reshape(1, d), target)


def _adamw(w, m, v, gparts, *, name):
    r, c = w.shape
    tr = _pick(r, (256, 128)) if r * c * 4 > (1 << 20) else r

    def body(w_ref, m_ref, v_ref, gp_ref, g_ref, d_ref, nm_ref, nv_ref):
        g = gp_ref[0].astype(F32)
        for j in range(1, N_DEV):
            g = g + gp_ref[j].astype(F32)
        mm = ADAM_B1 * m_ref[...] + (1.0 - ADAM_B1) * g
        vv = ADAM_B2 * v_ref[...] + (1.0 - ADAM_B2) * (g * g)
        m_hat = mm / (1.0 - ADAM_B1 ** ADAM_STEP)
        v_hat = vv / (1.0 - ADAM_B2 ** ADAM_STEP)
        g_ref[...] = g
        d_ref[...] = -ADAM_LR * (m_hat / (jnp.sqrt(v_hat) + ADAM_EPS) + ADAM_WD * w_ref[...])
        nm_ref[...] = mm
        nv_ref[...] = vv

    blk = pl.BlockSpec((tr, c), lambda i: (i, 0))
    out = jax.ShapeDtypeStruct((r, c), F32)
    return _pc(body, name=name, grid=(r // tr,),
               in_specs=[blk, blk, blk, pl.BlockSpec((N_DEV, tr, c), lambda i: (0, i, 0))],
               out_specs=[blk, blk, blk, blk], out_shape=[out, out, out, out],
               vmem=tr * c * (4 * 7 + N_DEV * jnp.dtype(gparts.dtype).itemsize))(w, m, v, gparts)


PEERS = (1, 2, 4, 6, 3, 5, 7)
HBM_SPEC = pl.BlockSpec(memory_space=pltpu.HBM)
SEM_SPEC = pl.BlockSpec(memory_space=pltpu.SEMAPHORE)
SIDE_EFFECT = pltpu.SideEffectType.DATAFLOW_SIDE_EFFECTING


def _peer(p):
    x, y, c = lax.axis_index("x"), lax.axis_index("y"), lax.axis_index("c")
    tx, ty, tc = x ^ ((p >> 2) & 1), y ^ ((p >> 1) & 1), c ^ (p & 1)
    return (tx, ty, tc), 4 * tx + 2 * ty + tc


def _place_own(a, kind, *, name):
    rows, cols = a.shape[-2:]
    small = _nbytes((rows, cols), a.dtype) <= (2 << 20)
    tr = rows if small else _pick(rows, (512, 256, 128, 64, 32, 16))
    me = (4 * lax.axis_index("x") + 2 * lax.axis_index("y") + lax.axis_index("c")).astype(jnp.int32).reshape(1)

    def body(me_ref, in_ref, out_ref):
        out_ref[...] = in_ref[...].reshape(out_ref.shape)

    if kind == "a2a":
        in_spec = pl.BlockSpec((1, tr, cols), lambda i, me_ref: (me_ref[0], i, 0))
    else:
        in_spec = pl.BlockSpec((tr, cols), lambda i, me_ref: (i, 0))
    return pl.pallas_call(
        body, name=name, out_shape=jax.ShapeDtypeStruct((N_DEV, rows, cols), a.dtype),
        grid_spec=pltpu.PrefetchScalarGridSpec(
            num_scalar_prefetch=1, grid=(rows // tr,), in_specs=[in_spec],
            out_specs=pl.BlockSpec((1, tr, cols), lambda i, me_ref: (me_ref[0], i, 0))))(me, a)


def _exchange_copies(ins, lands, send_sems, recv_sems, kinds, incoming, peers=PEERS):
    me = 4 * lax.axis_index("x") + 2 * lax.axis_index("y") + lax.axis_index("c")
    copies = []
    for i, kind in enumerate(kinds):
        for p in peers:
            dev, tgt = _peer(p)
            k = i * (N_DEV - 1) + p - 1
            copies.append(pltpu.make_async_remote_copy(
                src_ref=ins[i].at[tgt] if kind == "a2a" else ins[i], dst_ref=lands[i].at[tgt if incoming else me],
                send_sem=send_sems.at[k], recv_sem=recv_sems.at[k], device_id=dev, device_id_type=MESH))
    return copies


def _exchange_start(arrays, kinds, *, name, dep=None, peers=PEERS):
    n = len(arrays)
    lands = [_place_own(a, k, name=f"{name}_own{i}") for i, (a, k) in enumerate(zip(arrays, kinds))]
    extra = [] if dep is None else [dep]

    def body(*refs):
        ins, lnd = refs[:n], refs[n:2 * n]
        send_sems, recv_sems = refs[2 * n + len(extra)], refs[2 * n + len(extra) + 1]
        token = refs[-1]
        for cp in _exchange_copies(ins, lnd, send_sems, recv_sems, kinds, False, peers):
            cp.start()
        token[...] = jnp.zeros_like(token)

    sem = pltpu.SemaphoreType.DMA((n * (N_DEV - 1),))
    outs = pl.pallas_call(
        body, name=name, in_specs=[HBM_SPEC] * (2 * n) + [ANY_SPEC] * len(extra),
        out_specs=[SEM_SPEC, SEM_SPEC] + [HBM_SPEC] * (2 * n) + [pl.BlockSpec(memory_space=pltpu.VMEM)],
        out_shape=[sem, sem] + [pltpu.HBM(a.shape, a.dtype) for a in arrays + lands]
        + [jax.ShapeDtypeStruct((8, 128), F32)],
        input_output_aliases={i: 2 + i for i in range(2 * n)},
        compiler_params=pltpu.CompilerParams(has_side_effects=SIDE_EFFECT),
    )(*[pltpu.with_memory_space_constraint(a, pltpu.HBM) for a in arrays + lands], *extra)
    return dict(send=outs[0], recv=outs[1], ins=list(outs[2:2 + n]), lands=list(outs[2 + n:2 + 2 * n]),
                token=outs[-1], kinds=kinds, name=name, peers=peers)


def _exchange_wait(h, after):
    n = len(h["ins"])
    kinds = h["kinds"]

    def body(*refs):
        ins, lnd = refs[:n], refs[n:2 * n]
        send_sems, recv_sems = refs[2 * n], refs[2 * n + 1]
        copies = _exchange_copies(ins, lnd, send_sems, recv_sems, kinds, True, h["peers"])
        for cp in copies:
            cp.wait_recv()
        for cp in copies:
            cp.wait_send()

    arrs = h["ins"] + h["lands"]
    after = list(after) if isinstance(after, (list, tuple)) else [after]
    outs = pl.pallas_call(
        body, name=h["name"] + "_wait", in_specs=[HBM_SPEC] * (2 * n) + [SEM_SPEC, SEM_SPEC] + [ANY_SPEC] * len(after),
        out_specs=[HBM_SPEC] * (2 * n), out_shape=[pltpu.HBM(a.shape, a.dtype) for a in arrs],
        input_output_aliases={i: i for i in range(2 * n)},
        compiler_params=pltpu.CompilerParams(has_side_effects=SIDE_EFFECT),
    )(*arrs, h["send"], h["recv"], *after)
    return list(outs[n:])


SAME_CORE = (0, 2, 4, 6)


def _forward_copies(land, send_sems, recv_sems, incoming):
    me = 4 * lax.axis_index("x") + 2 * lax.axis_index("y") + lax.axis_index("c")
    dev, sibling = _peer(1)
    return [pltpu.make_async_remote_copy(
        src_ref=land.at[me ^ q], dst_ref=land.at[(sibling if incoming else me) ^ q],
        send_sem=send_sems.at[j], recv_sem=recv_sems.at[j], device_id=dev, device_id_type=MESH)
        for j, q in enumerate(SAME_CORE)]


def _sibling_forward_start(land, *, name, dep=None):
    extra = [] if dep is None else [dep]

    def body(*refs):
        land_ref, send_sems, recv_sems, token = refs[0], refs[1 + len(extra)], refs[2 + len(extra)], refs[-1]
        for cp in _forward_copies(land_ref, send_sems, recv_sems, False):
            cp.start()
        token[...] = jnp.zeros_like(token)

    sem = pltpu.SemaphoreType.DMA((len(SAME_CORE),))
    outs = pl.pallas_call(
        body, name=name, in_specs=[HBM_SPEC] + [ANY_SPEC] * len(extra),
        out_specs=[SEM_SPEC, SEM_SPEC, HBM_SPEC, pl.BlockSpec(memory_space=pltpu.VMEM)],
        out_shape=[sem, sem, pltpu.HBM(land.shape, land.dtype), jax.ShapeDtypeStruct((8, 128), F32)],
        input_output_aliases={0: 2}, compiler_params=pltpu.CompilerParams(has_side_effects=SIDE_EFFECT),
    )(pltpu.with_memory_space_constraint(land, pltpu.HBM), *extra)
    return dict(send=outs[0], recv=outs[1], land=outs[2], token=outs[3], name=name)


def _sibling_forward_wait(h, after):
    def body(*refs):
        copies = _forward_copies(refs[0], refs[1], refs[2], True)
        for cp in copies:
            cp.wait_recv()
        for cp in copies:
            cp.wait_send()

    return pl.pallas_call(
        body, name=h["name"] + "_wait", in_specs=[HBM_SPEC, SEM_SPEC, SEM_SPEC, ANY_SPEC], out_specs=HBM_SPEC,
        out_shape=pltpu.HBM(h["land"].shape, h["land"].dtype), input_output_aliases={0: 0},
        compiler_params=pltpu.CompilerParams(has_side_effects=SIDE_EFFECT),
    )(h["land"], h["send"], h["recv"], after)


def _s5_params(lam_re, lam_im, log_dt, b_re, b_im):
    dt = jnp.exp(log_dt)[:, None]
    mag = jnp.exp(lam_re * dt)
    ar, ai = mag * jnp.cos(lam_im * dt), mag * jnp.sin(lam_im * dt)
    den = lam_re * lam_re + lam_im * lam_im
    qr = ((ar - 1.0) * lam_re + ai * lam_im) / den
    qi = (ai * lam_re - (ar - 1.0) * lam_im) / den
    bbr = qr[..., None] * b_re - qi[..., None] * b_im
    bbi = qr[..., None] * b_im + qi[..., None] * b_re
    return ar, ai, bbr, bbi


def _s5_power_table(ar, ai):
    pr, pi = ar.reshape(1, -1), ai.reshape(1, -1)
    while pr.shape[0] < 8:
        sr, si = pr[-1:], pi[-1:]
        pr, pi = (jnp.concatenate([pr, pr * sr - pi * si], axis=0), jnp.concatenate([pi, pr * si + pi * sr], axis=0))
    return pr, pi


def _blockdiag(w, rows, cols):
    w = w.reshape(S5_GB, S5_GB, rows, cols)
    eye = jnp.eye(S5_GB, dtype=w.dtype)
    return jnp.einsum("abrc,bd->abrdc", w, eye).reshape(S5_GB, S5_GB * rows, S5_GB * cols)


def _blockdiag_extract(w, rows, cols):
    w = w.reshape(S5_GB, S5_GB, rows, S5_GB, cols)
    return jnp.einsum("abrbc->abrc", w).reshape(S5_GROUPS, rows, cols)


def _s5_scan_specs(bsz, nc, rev):
    def cc(c):
        return (nc - 1 - c) if rev else c

    return dict(
        u=pl.BlockSpec((bsz, CHUNK, CHUNK), lambda g, c: (0, cc(c), g)),
        x=pl.BlockSpec((bsz, CHUNK, S5_LANES), lambda g, c: (0, cc(c), g)),
        wb=pl.BlockSpec((1, CHUNK, S5_LANES), lambda g, c: (g, 0, 0)),
        wc=pl.BlockSpec((1, S5_LANES, CHUNK), lambda g, c: (g, 0, 0)),
        tab=pl.BlockSpec((8, S5_LANES), lambda g, c: (0, g)),
        step=pl.BlockSpec((8, S5_LANES), lambda g, c: (0, g)),
        d=pl.BlockSpec((1, CHUNK), lambda g, c: (0, g)),
        lane=pl.BlockSpec((1, S5_LANES), lambda g, c: (0, g)),
        xprev=pl.BlockSpec((bsz, 8, S5_LANES), lambda g, c: (0, jnp.maximum(cc(c) * (CHUNK // 8) - 1, 0), g)),
    )


def _s5_fwd(u, wbr, wbi, pr, pi, sr, si, wcr, wci, d, bsz, nc):
    r = u.shape[0]
    tp = r // bsz
    sp = _s5_scan_specs(bsz, nc, False)

    def body(u_all, wbr_ref, wbi_ref, pr_ref, pi_ref, sr_ref, si_ref, wcr_ref, wci_ref, d_ref,
             xr_all, xi_all, y1_all, g_all, cr_sall, ci_sall):
        @pl.when(pl.program_id(1) == 0)
        def _():
            cr_sall[...] = jnp.zeros_like(cr_sall)
            ci_sall[...] = jnp.zeros_like(ci_sall)

        for bi in range(bsz):
            one(u_all.at[bi], wbr_ref, wbi_ref, pr_ref, pi_ref, sr_ref, si_ref, wcr_ref, wci_ref, d_ref,
                xr_all.at[bi], xi_all.at[bi], y1_all.at[bi], g_all.at[bi], cr_sall.at[bi], ci_sall.at[bi])

    def one(u_ref, wbr_ref, wbi_ref, pr_ref, pi_ref, sr_ref, si_ref, wcr_ref, wci_ref, d_ref,
            xr_ref, xi_ref, y1_ref, g_ref, cr_s, ci_s):
        uv = u_ref[...]
        ub = _bf(uv)
        xr, xi = _dot(ub, wbr_ref[0]), _dot(ub, wbi_ref[0])
        sub = lax.broadcasted_iota(jnp.int32, (CHUNK, S5_LANES), 0) % 8
        for k in range(3):
            s = 1 << k
            ar, ai = sr_ref[k:k + 1, :], si_ref[k:k + 1, :]
            hr = jnp.where(sub >= s, pltpu.roll(xr, s, 0), 0.0)
            hi = jnp.where(sub >= s, pltpu.roll(xi, s, 0), 0.0)
            xr, xi = xr + (ar * hr - ai * hi), xi + (ar * hi + ai * hr)
        cr, ci = cr_s[...], ci_s[...]
        tr, ti = pr_ref[...], pi_ref[...]
        outr, outi = [], []
        for g8 in range(CHUNK // 8):
            br, bi = xr[8 * g8:8 * g8 + 8, :], xi[8 * g8:8 * g8 + 8, :]
            br, bi = br + (tr * cr - ti * ci), bi + (tr * ci + ti * cr)
            cr, ci = br[7:8, :], bi[7:8, :]
            outr.append(br)
            outi.append(bi)
        xr, xi = jnp.concatenate(outr, axis=0), jnp.concatenate(outi, axis=0)
        cr_s[...] = cr
        ci_s[...] = ci
        xr_ref[...] = xr
        xi_ref[...] = xi
        y = _dot(_bf(xr), wcr_ref[0]) - _dot(_bf(xi), wci_ref[0]) + d_ref[...] * uv
        y1_ref[...] = y
        g_ref[...] = _bf(_gelu_and_grad(y)[0])

    ns = S5_GROUPS * S5_STATE
    xr, xi, y1, g = _pc(
        body, name="s5_fwd", grid=(S5_GB, nc),
        in_specs=[sp["u"], sp["wb"], sp["wb"], sp["tab"], sp["tab"], sp["step"], sp["step"], sp["wc"], sp["wc"],
                  sp["d"]],
        out_specs=[sp["x"], sp["x"], sp["u"], sp["u"]],
        out_shape=[jax.ShapeDtypeStruct((bsz, tp, ns), F32)] * 2
        + [jax.ShapeDtypeStruct((bsz, tp, S5_WIDTH), F32), jax.ShapeDtypeStruct((bsz, tp, S5_WIDTH), BF16)],
        scratch=[pltpu.VMEM((bsz, 1, S5_LANES), F32)] * 2, vmem=8 << 20,
    )(_seq(u, bsz), wbr, wbi, pr, pi, sr, si, wcr, wci, d)
    return xr.reshape(r, ns), xi.reshape(r, ns), y1.reshape(r, S5_WIDTH), g.reshape(r, S5_WIDTH)


def _s5_post(y1, glu_pre, glu_b, z):
    r, w = y1.shape
    tm = _pick(r, (256, 128))

    def body(y_ref, p_ref, b_ref, z_ref, o_ref):
        g = _gelu_and_grad(y_ref[...])[0]
        o_ref[...] = _bf(g * jax.nn.sigmoid(p_ref[...] + b_ref[...]) * _silu(z_ref[...]))

    row = pl.BlockSpec((tm, w), lambda i: (i, 0))
    return _pc(body, name="s5_post", grid=(r // tm,), in_specs=[row, row, pl.BlockSpec((1, w), lambda i: (0, 0)), row],
               out_specs=row, out_shape=jax.ShapeDtypeStruct((r, w), BF16), vmem=tm * w * 16)(y1, glu_pre, glu_b, z)


def _s5_post_bwd(dya, y1, glu_pre, glu_b, z):
    r, w = y1.shape
    tm = _pick(r, (256, 128))

    def body(dy_ref, y_ref, p_ref, b_ref, z_ref, dz_ref, dp_ref, dg_ref, db_ref):
        @pl.when(pl.program_id(0) == 0)
        def _():
            db_ref[...] = jnp.zeros_like(db_ref)

        g = _gelu_and_grad(y_ref[...])[0]
        s = jax.nn.sigmoid(p_ref[...] + b_ref[...])
        zv = z_ref[...]
        dy = dy_ref[...]
        do = dy * _silu(zv)
        dz_ref[...] = _bf(dy * g * s * _dsilu(zv))
        dp = do * g * s * (1.0 - s)
        dp_ref[...] = _bf(dp)
        db_ref[...] += jnp.sum(dp, axis=0, keepdims=True)
        dg_ref[...] = do * s

    row = pl.BlockSpec((tm, w), lambda i: (i, 0))
    vec = pl.BlockSpec((1, w), lambda i: (0, 0))
    return _pc(body, name="s5_post_bwd", grid=(r // tm,), in_specs=[row, row, row, vec, row],
               out_specs=[row, row, row, vec],
               out_shape=[jax.ShapeDtypeStruct((r, w), BF16), jax.ShapeDtypeStruct((r, w), BF16),
                          jax.ShapeDtypeStruct((r, w), F32), jax.ShapeDtypeStruct((1, w), F32)],
               vmem=tm * w * 24)(dya, y1, glu_pre, glu_b, z)


def _s5_bwd(dg, y1, u, xr, xi, wbr, wbi, qr, qi, sr, si, wcr, wci, d, bsz, nc):
    r = u.shape[0]
    tp = r // bsz
    sp = _s5_scan_specs(bsz, nc, True)

    def body(dg_all, y1_all, u_all, xr_all, xi_all, xpr_all, xpi_all, wbr_ref, wbi_ref, qr_ref, qi_ref, sr_ref, si_ref,
             wcr_ref, wci_ref, d_ref, du_all, dd_ref, dwcr_ref, dwci_ref, dwbr_ref, dwbi_ref, dar_ref, dai_ref,
             cr_sall, ci_sall):
        c = pl.program_id(1)

        @pl.when(c == 0)
        def _():
            for ref in (dd_ref, dwcr_ref, dwci_ref, dwbr_ref, dwbi_ref, dar_ref, dai_ref, cr_sall, ci_sall):
                ref[...] = jnp.zeros_like(ref)

        for bi in range(bsz):
            one(c, dg_all.at[bi], y1_all.at[bi], u_all.at[bi], xr_all.at[bi], xi_all.at[bi], xpr_all.at[bi],
                xpi_all.at[bi], wbr_ref, wbi_ref, qr_ref, qi_ref, sr_ref, si_ref, wcr_ref, wci_ref, d_ref,
                du_all.at[bi], dd_ref, dwcr_ref, dwci_ref, dwbr_ref, dwbi_ref, dar_ref, dai_ref, cr_sall.at[bi],
                ci_sall.at[bi])

    def one(c, dg_ref, y1_ref, u_ref, xr_ref, xi_ref, xpr_ref, xpi_ref, wbr_ref, wbi_ref, qr_ref, qi_ref, sr_ref, si_ref,
            wcr_ref, wci_ref, d_ref, du_ref, dd_ref, dwcr_ref, dwci_ref, dwbr_ref, dwbi_ref, dar_ref, dai_ref,
            cr_s, ci_s):
        uv = u_ref[...]
        ub = _bf(uv)
        dy = dg_ref[...] * _gelu_and_grad(y1_ref[...])[1]
        dd_ref[...] += jnp.sum(dy * uv, axis=0, keepdims=True)
        dyb = _bf(dy)
        xr, xi = xr_ref[...], xi_ref[...]
        dwcr_ref[0] += _dot(_bf(xr), dyb, TN)
        dwci_ref[0] -= _dot(_bf(xi), dyb, TN)
        lr, li = _dot(dyb, wcr_ref[0], NT), -_dot(dyb, wci_ref[0], NT)
        row = lax.broadcasted_iota(jnp.int32, (CHUNK, S5_LANES), 0)
        sub = row % 8
        for k in range(3):
            s = 1 << k
            ar, ai = sr_ref[k:k + 1, :], si_ref[k:k + 1, :]
            hr = jnp.where(sub < 8 - s, pltpu.roll(lr, CHUNK - s, 0), 0.0)
            hi = jnp.where(sub < 8 - s, pltpu.roll(li, CHUNK - s, 0), 0.0)
            lr, li = lr + (ar * hr + ai * hi), li + (ar * hi - ai * hr)
        cr, ci = cr_s[...], ci_s[...]
        tr, ti = qr_ref[...], qi_ref[...]
        outr, outi = [], []
        for g8 in reversed(range(CHUNK // 8)):
            br, bi = lr[8 * g8:8 * g8 + 8, :], li[8 * g8:8 * g8 + 8, :]
            br, bi = br + (tr * cr + ti * ci), bi + (tr * ci - ti * cr)
            cr, ci = br[0:1, :], bi[0:1, :]
            outr.append(br)
            outi.append(bi)
        lr, li = jnp.concatenate(outr[::-1], axis=0), jnp.concatenate(outi[::-1], axis=0)
        cr_s[...] = cr
        ci_s[...] = ci
        lrb, lib = _bf(lr), _bf(li)
        du_ref[...] = _bf(_dot(lrb, wbr_ref[0], NT) + _dot(lib, wbi_ref[0], NT) + dy * d_ref[...])
        dwbr_ref[0] += _dot(ub, lrb, TN)
        dwbi_ref[0] += _dot(ub, lib, TN)
        first = c == nc - 1
        pr0 = jnp.where(first, 0.0, xpr_ref[7:8, :])
        pi0 = jnp.where(first, 0.0, xpi_ref[7:8, :])
        xpr = jnp.where(row == 0, pr0, pltpu.roll(xr, 1, 0))
        xpi = jnp.where(row == 0, pi0, pltpu.roll(xi, 1, 0))
        dar_ref[...] += jnp.sum(lr * xpr + li * xpi, axis=0, keepdims=True)
        dai_ref[...] += jnp.sum(li * xpr - lr * xpi, axis=0, keepdims=True)

    st = jax.ShapeDtypeStruct
    xr3, xi3 = _seq(xr, bsz), _seq(xi, bsz)
    outs = _pc(body, name="s5_bwd", grid=(S5_GB, nc),
               in_specs=[sp["u"], sp["u"], sp["u"], sp["x"], sp["x"], sp["xprev"], sp["xprev"], sp["wb"], sp["wb"],
                         sp["tab"], sp["tab"], sp["step"], sp["step"], sp["wc"], sp["wc"], sp["d"]],
               out_specs=[sp["u"], sp["d"], sp["wc"], sp["wc"], sp["wb"], sp["wb"], sp["lane"], sp["lane"]],
               out_shape=[st((bsz, tp, S5_WIDTH), BF16), st((1, S5_WIDTH), F32),
                          st((S5_GB, S5_LANES, CHUNK), F32), st((S5_GB, S5_LANES, CHUNK), F32),
                          st((S5_GB, CHUNK, S5_LANES), F32), st((S5_GB, CHUNK, S5_LANES), F32),
                          st((1, S5_GROUPS * S5_STATE), F32), st((1, S5_GROUPS * S5_STATE), F32)],
               scratch=[pltpu.VMEM((bsz, 1, S5_LANES), F32)] * 2, vmem=12 << 20,
               )(_seq(dg, bsz), _seq(y1, bsz), _seq(u, bsz), xr3, xi3, xr3, xi3, wbr, wbi, qr, qi, sr, si, wcr, wci, d)
    return (outs[0].reshape(r, S5_WIDTH),) + tuple(outs[1:])


def _s5_layer_fwd(u, prm, glu_w, bsz, nc):
    xr, xi, y1, g = _s5_fwd(u, prm["wbr"], prm["wbi"], prm["pr"], prm["pi"], prm["sr"], prm["si"], prm["wcr"],
                            prm["wci"], prm["d"], bsz, nc)
    glu_pre = _mm(g, glu_w(y1) if callable(glu_w) else glu_w, "NN", name="s5_glu")
    return dict(xr=xr, xi=xi, y1=y1, g=g, glu_pre=glu_pre)


def _s5_layer_bwd(dya, u, z, sv, prm, pvjp, glu_w, glu_b, bsz, nc):
    dz, dglu, dg_direct, dglu_b = _s5_post_bwd(dya, sv["y1"], sv["glu_pre"], glu_b, z)
    dg = _mm(dglu, glu_w, "NT", name="s5_dg", add=dg_direct)
    dglu_w = _mm(sv["g"], dglu, "TN", name="s5_dglu_w")
    du, dd, dwcr, dwci, dwbr, dwbi, dar, dai = _s5_bwd(
        dg, sv["y1"], u, sv["xr"], sv["xi"], prm["wbr"], prm["wbi"], prm["qr"], prm["qi"], prm["sr"], prm["si"],
        prm["wcr"], prm["wci"], prm["d"], bsz, nc)
    dbbr = jnp.swapaxes(_blockdiag_extract(dwbr, S5_GROUP_SIZE, S5_STATE), 1, 2)
    dbbi = jnp.swapaxes(_blockdiag_extract(dwbi, S5_GROUP_SIZE, S5_STATE), 1, 2)
    dlr, dli, dldt, dbr, dbi = pvjp((dar.reshape(S5_GROUPS, S5_STATE), dai.reshape(S5_GROUPS, S5_STATE), dbbr, dbbi))
    grads = dict(
        s5_lambda_re=dlr, s5_lambda_im=dli, s5_log_dt=dldt, s5_b_re=dbr, s5_b_im=dbi,
        s5_c_re=jnp.swapaxes(_blockdiag_extract(dwcr, S5_STATE, S5_GROUP_SIZE), 1, 2),
        s5_c_im=jnp.swapaxes(_blockdiag_extract(dwci, S5_STATE, S5_GROUP_SIZE), 1, 2),
        s5_d=dd, s5_glu_w=dglu_w, s5_glu_b=dglu_b)
    return du, dz, grads


def _s5_tables(lam_re, lam_im, log_dt, b_re, b_im, c_re, c_im, d):
    (ar, ai, bbr, bbi), vjp = jax.vjp(_s5_params, lam_re, lam_im, log_dt, b_re, b_im)
    pr, pi = _s5_power_table(lax.stop_gradient(ar), lax.stop_gradient(ai))
    steps = [0, 1, 3, 7, 7, 7, 7, 7]
    flip8 = (jnp.arange(8)[:, None] + jnp.arange(8)[None, :] == 7).astype(F32)
    prm = dict(
        wbr=_bf(_blockdiag(jnp.swapaxes(bbr, 1, 2), S5_GROUP_SIZE, S5_STATE)),
        wbi=_bf(_blockdiag(jnp.swapaxes(bbi, 1, 2), S5_GROUP_SIZE, S5_STATE)),
        wcr=_bf(_blockdiag(jnp.swapaxes(c_re, 1, 2), S5_STATE, S5_GROUP_SIZE)),
        wci=_bf(_blockdiag(jnp.swapaxes(c_im, 1, 2), S5_STATE, S5_GROUP_SIZE)),
        pr=pr, pi=pi, qr=jnp.dot(flip8, pr, precision=lax.Precision.HIGHEST),
        qi=jnp.dot(flip8, pi, precision=lax.Precision.HIGHEST),
        sr=jnp.concatenate([pr[i:i + 1] for i in steps], axis=0),
        si=jnp.concatenate([pi[i:i + 1] for i in steps], axis=0), d=d.reshape(1, S5_WIDTH))
    return prm, vjp


def _shift_down(x, halo8, s):
    sh = pltpu.roll(x, s, 0)
    r8 = lax.broadcasted_iota(jnp.int32, halo8.shape, 0)
    first = jnp.where(r8 < s, pltpu.roll(halo8, s, 0), sh[:8])
    return jnp.concatenate([first, sh[8:]], axis=0)


def _shift_up(x, halo8, s):
    sh = pltpu.roll(x, CHUNK - s, 0)
    r8 = lax.broadcasted_iota(jnp.int32, halo8.shape, 0)
    last = jnp.where(r8 >= 8 - s, pltpu.roll(halo8, 8 - s, 0), sh[CHUNK - 8:])
    return jnp.concatenate([sh[:CHUNK - 8], last], axis=0)


def _conv_specs(nc, tw):
    def chunk(b, c):
        return b * nc + c

    return dict(
        x=pl.BlockSpec((CHUNK, tw), lambda j, b, c: (chunk(b, c), j)),
        prev=pl.BlockSpec((8, tw), lambda j, b, c: (jnp.maximum(chunk(b, c) * (CHUNK // 8) - 1, 0), j)),
        nxt=pl.BlockSpec((8, tw), lambda j, b, c: ((b * nc + jnp.minimum(c + 1, nc - 1)) * (CHUNK // 8), j)),
        w=pl.BlockSpec((ML_CONV, tw), lambda j, b, c: (0, j)),
        vec=pl.BlockSpec((1, tw), lambda j, b, c: (0, j)),
    )


def _conv_fwd(x, w, bias, bsz, nc, *, name):
    r, wd = x.shape
    tw = _pick(wd, (2048, 1536, 1024, 512, 384, 256, 128))
    sp = _conv_specs(nc, tw)

    def body(x_ref, p_ref, w_ref, b_ref, o_ref):
        c = pl.program_id(2)
        xv = x_ref[...]
        halo = jnp.where(c == 0, 0.0, p_ref[...])
        acc = b_ref[...] + w_ref[3:4, :] * xv
        for s in (1, 2, 3):
            acc = acc + w_ref[3 - s:4 - s, :] * _shift_down(xv, halo, s)
        o_ref[...] = acc

    return _pc(body, name=name, grid=(wd // tw, bsz, nc), in_specs=[sp["x"], sp["prev"], sp["w"], sp["vec"]],
               out_specs=sp["x"], out_shape=jax.ShapeDtypeStruct((r, wd), F32), vmem=CHUNK * tw * 16,
               )(x, x, w, bias.reshape(1, wd))


def _conv_bwd(dpre, x, w, bsz, nc, *, name, add=None):
    r, wd = x.shape
    tw = _pick(wd, (2048, 1536, 1024, 512, 384, 256, 128))
    sp = _conv_specs(nc, tw)

    def body(*refs):
        d_ref, n_ref, x_ref, p_ref, w_ref = refs[:5]
        add_ref = refs[5] if add is not None else None
        dx_ref, dw_ref, db_ref = refs[-3:]
        b, c = pl.program_id(1), pl.program_id(2)

        @pl.when((b == 0) & (c == 0))
        def _():
            dw_ref[...] = jnp.zeros_like(dw_ref)
            db_ref[...] = jnp.zeros_like(db_ref)

        dv, xv = d_ref[...], x_ref[...]
        dhalo = jnp.where(c == nc - 1, 0.0, n_ref[...])
        xhalo = jnp.where(c == 0, 0.0, p_ref[...])
        dx = w_ref[3:4, :] * dv
        for s in (1, 2, 3):
            dx = dx + w_ref[3 - s:4 - s, :] * _shift_up(dv, dhalo, s)
        if add_ref is not None:
            dx = dx + add_ref[...]
        dx_ref[...] = _bf(dx)
        db_ref[...] += jnp.sum(dv, axis=0, keepdims=True)
        dw_ref[3:4, :] += jnp.sum(dv * xv, axis=0, keepdims=True)
        for s in (1, 2, 3):
            dw_ref[3 - s:4 - s, :] += jnp.sum(dv * _shift_down(xv, xhalo, s), axis=0, keepdims=True)

    ins = [dpre, dpre, x, x, w] + ([add] if add is not None else [])
    specs = [sp["x"], sp["nxt"], sp["x"], sp["prev"], sp["w"]] + ([sp["x"]] if add is not None else [])
    return _pc(body, name=name, grid=(wd // tw, bsz, nc), in_specs=specs, out_specs=[sp["x"], sp["w"], sp["vec"]],
               out_shape=[jax.ShapeDtypeStruct((r, wd), BF16), jax.ShapeDtypeStruct((ML_CONV, wd), F32),
                          jax.ShapeDtypeStruct((1, wd), F32)], vmem=CHUNK * tw * 24)(*ins)


ML_SCALE = ML_DH ** -0.5


ML_LB = ML_DH // CHUNK


def _headwise_expand(w):
    tiled = jnp.tile(w.reshape(ML_HEADS, ML_DH, QKV_BLOCK), (1, 1, CHUNK // QKV_BLOCK))
    rblk = (jnp.arange(ML_DH) % CHUNK) // QKV_BLOCK
    cblk = jnp.arange(CHUNK) // QKV_BLOCK
    return jnp.where(rblk[:, None] == cblk[None, :], tiled, 0.0).reshape(ML_HEADS, ML_LB, CHUNK, CHUNK)


def _headwise_dot(x, w_ref, dims=NN):
    return jnp.concatenate([_dot(x[:, j * CHUNK:(j + 1) * CHUNK], w_ref[0, j], dims) for j in range(ML_LB)], axis=1)


def _headwise_extract(w):
    return w[:, :, :QKV_BLOCK].reshape(ML_HEADS * ML_DH // QKV_BLOCK, QKV_BLOCK, QKV_BLOCK)


def _ml_pre(pre, x, wq, wk, wv, wgq, wgk, wgv, bsz, nc):
    r = x.shape[0]
    tr = _pick(r, (256, 128))
    hrow = pl.BlockSpec((tr, ML_DH), lambda h, i: (i, h))
    wexp = pl.BlockSpec((1, ML_LB, CHUNK, CHUNK), lambda h, i: (h, 0, 0, 0))
    wg = pl.BlockSpec((ML_DH, CHUNK), lambda h, i: (h, 0))

    def body(pre_ref, x_ref, wq_ref, wk_ref, wv_ref, gq_ref, gk_ref, gv_ref, qs_ref, k_ref, v_ref, gt_ref):
        xcb = _bf(_silu(pre_ref[...]))
        q = _headwise_dot(xcb, wq_ref)
        k = _headwise_dot(xcb, wk_ref)
        v = _headwise_dot(_bf(x_ref[...]), wv_ref)
        qb, kb, vb = _bf(q), _bf(k), _bf(v)
        qs_ref[...] = _bf(q * ML_SCALE)
        k_ref[...] = kb
        v_ref[...] = vb
        gt_ref[0] = _dot(qb, gq_ref[...]) + _dot(kb, gk_ref[...]) + _dot(vb, gv_ref[...])

    o = jax.ShapeDtypeStruct((r, ML_WIDTH), BF16)
    qs, k, v, gates8 = _pc(
        body, name="ml_pre", grid=(ML_HEADS, r // tr),
        in_specs=[hrow, hrow, wexp, wexp, wexp, wg, wg, wg],
        out_specs=[hrow, hrow, hrow, pl.BlockSpec((1, tr, CHUNK), lambda h, i: (h, i, 0))],
        out_shape=[o, o, o, jax.ShapeDtypeStruct((ML_HEADS, r, CHUNK), F32)], vmem=6 << 20,
    )(pre, x, wq, wk, wv, wgq, wgk, wgv)

    return qs, k, v, _sum_heads(gates8, name="ml_gates_sum")


def _sum_heads(g8, *, name):
    r = g8.shape[1]
    tr = _pick(r, (256, 128))

    def body(g_ref, o_ref):
        acc = g_ref[0]
        for j in range(1, ML_HEADS):
            acc = acc + g_ref[j]
        o_ref[...] = acc

    return _pc(body, name=name, grid=(r // tr,),
               in_specs=[pl.BlockSpec((ML_HEADS, tr, CHUNK), lambda i: (0, i, 0))],
               out_specs=pl.BlockSpec((tr, CHUNK), lambda i: (i, 0)),
               out_shape=jax.ShapeDtypeStruct((r, CHUNK), F32), vmem=2 << 20)(g8)


def _tri(rev):
    r = lax.broadcasted_iota(jnp.int32, (CHUNK, CHUNK), 0)
    c = lax.broadcasted_iota(jnp.int32, (CHUNK, CHUNK), 1)
    return jnp.where((c >= r) if rev else (c <= r), 1.0, 0.0).astype(F32)


def _cumsum_rows(x, row, rev=False):
    for k in range(7):
        s = 1 << k
        if rev:
            x = x + jnp.where(row < CHUNK - s, pltpu.roll(x, CHUNK - s, 0), 0.0)
        else:
            x = x + jnp.where(row >= s, pltpu.roll(x, s, 0), 0.0)
    return x


def _log_sigmoid(x):
    return jnp.minimum(x, 0.0) - jnp.log(1.0 + jnp.exp(-jnp.abs(x)))


def _ml_core(gates, hd, first, m, qs, k, v, cmat, nvec):
    sq = (CHUNK, CHUNK)
    lane = lax.broadcasted_iota(jnp.int32, sq, 1)
    row = lax.broadcasted_iota(jnp.int32, sq, 0)
    igc = jnp.sum(jnp.where(lane == hd, gates, 0.0), axis=1, keepdims=True)
    fpc = jnp.sum(jnp.where(lane == hd + ML_HEADS, gates, 0.0), axis=1, keepdims=True)
    valid = jnp.logical_or(jnp.logical_not(first), row[:, :1] >= PAD_ROWS)
    igc = jnp.where(valid, igc, NEG)
    lfc = jnp.where(valid, _log_sigmoid(fpc), 0.0)
    bcb = _cumsum_rows(jnp.broadcast_to(lfc, sq), row)
    igb = jnp.broadcast_to(igc, sq)
    dm = jnp.where(lane <= row, bcb - (bcb - igb).T, NEG)
    bc = bcb[:, :1]
    inter = bc + m
    mt = jnp.maximum(inter, jnp.max(dm, axis=1, keepdims=True))
    wt = jnp.exp(dm - mt)
    wprev = jnp.exp(inter - mt)
    s0 = _dot(qs, k, NT)
    s = s0 * wt
    cb = _bf(cmat)
    qc = _dot(qs, cb)
    qf = qs.astype(F32)
    qn = jnp.sum(qf * nvec, axis=1, keepdims=True)
    num = _dot(_bf(s), v) + wprev * qc
    den = jnp.sum(s, axis=1, keepdims=True) + wprev * qn
    emt = jnp.exp(-mt)
    dd = jnp.maximum(jnp.abs(den), emt)
    blast = bcb[CHUNK - 1:CHUNK, :1]
    g = blast - bc + igc
    m_new = jnp.maximum(blast + m, jnp.max(g, axis=0, keepdims=True))
    decay = jnp.exp(blast + m - m_new)
    e = jnp.exp(g - m_new)
    kf = k.astype(F32)
    wk = e * kf
    return dict(lane=lane, row=row, fpc=fpc, valid=valid, wt=wt, wprev=wprev, s=s, cb=cb, qc=qc, qf=qf, qn=qn,
                num=num, den=den, emt=emt, dd=dd, m_new=m_new, decay=decay, e=e, kf=kf, wk=wk)


def _ml_headnorm(h):
    mu = jnp.mean(h, axis=1, keepdims=True)
    hc = h - mu
    rstd = lax.rsqrt(jnp.mean(hc * hc, axis=1, keepdims=True) + HEAD_NORM_EPS)
    return hc * rstd, rstd


def _ml_chunk_specs(nc, rev, bsz):
    def cc(c):
        return (nc - 1 - c) if rev else c

    return dict(
        hrow=pl.BlockSpec((bsz, CHUNK, ML_DH), lambda hd, c: (0, cc(c), hd)),
        gates=pl.BlockSpec((bsz, CHUNK, CHUNK), lambda hd, c: (0, cc(c), 0)),
        bias=pl.BlockSpec((1, CHUNK), lambda hd, c: (0, 0)),
        hvec=pl.BlockSpec((1, ML_DH), lambda hd, c: (0, hd)),
        cs=pl.BlockSpec((bsz, 1, ML_DH, ML_DH), lambda hd, c: (0, hd * nc + cc(c), 0, 0)),
        ns=pl.BlockSpec((bsz, 1, 1, ML_DH), lambda hd, c: (0, hd * nc + cc(c), 0, 0)),
        ms=pl.BlockSpec((bsz, 1, 1, CHUNK), lambda hd, c: (0, hd * nc + cc(c), 0, 0)),
        dgates=pl.BlockSpec((1, bsz, CHUNK, CHUNK), lambda hd, c: (hd, 0, cc(c), 0)),
    )


def _seq(a, bsz):
    return a.reshape(bsz, a.shape[0] // bsz, a.shape[1])


def _ml_chunk_fwd(qs, k, v, gates, b_gate, pre, z, nw, sk, bsz, nc):
    r = qs.shape[0]
    tp = r // bsz
    sp = _ml_chunk_specs(nc, False, bsz)

    def body(qs_all, k_all, v_all, gt_all, bg_ref, pre_all, z_all, nw_ref, sk_ref,
             h_all, yb_all, cs_all, ns_all, ms_all, c_sall, n_sall, m_sall):
        hd, c = pl.program_id(0), pl.program_id(1)

        @pl.when(c == 0)
        def _():
            c_sall[...] = jnp.zeros_like(c_sall)
            n_sall[...] = jnp.zeros_like(n_sall)
            m_sall[...] = jnp.zeros_like(m_sall)

        for bi in range(bsz):
            one(hd, c, qs_all.at[bi], k_all.at[bi], v_all.at[bi], gt_all.at[bi], bg_ref, pre_all.at[bi], z_all.at[bi],
                nw_ref, sk_ref, h_all.at[bi], yb_all.at[bi], cs_all.at[bi], ns_all.at[bi], ms_all.at[bi],
                c_sall.at[bi], n_sall.at[bi], m_sall.at[bi])

    def one(hd, c, qs_ref, k_ref, v_ref, gt_ref, bg_ref, pre_ref, z_ref, nw_ref, sk_ref,
            h_ref, yb_ref, cs_ref, ns_ref, ms_ref, c_s, n_s, m_s):
        cmat, nvec, m = c_s[...], n_s[...], m_s[...]
        cs_ref[0] = cmat
        ns_ref[0] = nvec
        ms_ref[0] = jnp.broadcast_to(m, (1, CHUNK))
        v_ = v_ref[...]
        co = _ml_core(gt_ref[...] + bg_ref[...], hd, c == 0, m, qs_ref[...], k_ref[...], v_, cmat, nvec)
        h = co["num"] / co["dd"]
        h_ref[...] = h
        hn, _ = _ml_headnorm(h)
        yb_ref[...] = _bf((hn * nw_ref[...] + sk_ref[...] * _silu(pre_ref[...])) * _silu(z_ref[...]))
        c_s[...] = co["decay"] * cmat + _dot(_bf(co["wk"]), v_, TN)
        n_s[...] = co["decay"] * nvec + jnp.sum(co["wk"], axis=0, keepdims=True)
        m_s[...] = co["m_new"]

    nst = ML_HEADS * nc
    h, yb, cs, ns, ms = _pc(
        body, name="ml_chunk_fwd", grid=(ML_HEADS, nc),
        in_specs=[sp["hrow"]] * 3 + [sp["gates"], sp["bias"], sp["hrow"], sp["hrow"], sp["hvec"], sp["hvec"]],
        out_specs=[sp["hrow"], sp["hrow"], sp["cs"], sp["ns"], sp["ms"]],
        out_shape=[jax.ShapeDtypeStruct((bsz, tp, ML_WIDTH), F32), jax.ShapeDtypeStruct((bsz, tp, ML_WIDTH), BF16),
                   jax.ShapeDtypeStruct((bsz, nst, ML_DH, ML_DH), F32),
                   jax.ShapeDtypeStruct((bsz, nst, 1, ML_DH), F32), jax.ShapeDtypeStruct((bsz, nst, 1, CHUNK), F32)],
        scratch=[pltpu.VMEM((bsz, ML_DH, ML_DH), F32), pltpu.VMEM((bsz, 1, ML_DH), F32),
                 pltpu.VMEM((bsz, 1, 1), F32)],
        vmem=12 << 20)(*[_seq(a, bsz) for a in (qs, k, v, gates)], b_gate, _seq(pre, bsz), _seq(z, bsz), nw, sk)
    return h.reshape(r, ML_WIDTH), yb.reshape(r, ML_WIDTH), cs, ns, ms


def _ml_chunk_bwd(dyb, qs, k, v, gates, b_gate, pre, z, nw, sk, h, cs, ns, ms, bsz, nc, dep=None):
    r = qs.shape[0]
    tp = r // bsz
    sp = _ml_chunk_specs(nc, True, bsz)

    def body(dy_all, qs_all, k_all, v_all, gt_all, bg_ref, pre_all, z_all, nw_ref, sk_ref, h_all, cs_all, ns_all,
             ms_all, dq_all, dk_all, dv_all, dz_all, dxc_all, dgt_all, dnw_ref, dsk_ref, dc_sall, dn_sall):
        hd, c = pl.program_id(0), pl.program_id(1)

        @pl.when(c == 0)
        def _():
            for ref in (dnw_ref, dsk_ref, dc_sall, dn_sall):
                ref[...] = jnp.zeros_like(ref)

        for bi in range(bsz):
            one(hd, c, dy_all.at[bi], qs_all.at[bi], k_all.at[bi], v_all.at[bi], gt_all.at[bi], bg_ref,
                pre_all.at[bi], z_all.at[bi], nw_ref, sk_ref, h_all.at[bi], cs_all.at[bi], ns_all.at[bi],
                ms_all.at[bi], dq_all.at[bi], dk_all.at[bi], dv_all.at[bi], dz_all.at[bi], dxc_all.at[bi],
                dgt_all.at[0, bi], dnw_ref, dsk_ref, dc_sall.at[bi], dn_sall.at[bi])

    def one(hd, c, dy_ref, qs_ref, k_ref, v_ref, gt_ref, bg_ref, pre_ref, z_ref, nw_ref, sk_ref, h_ref, cs_ref, ns_ref,
            ms_ref, dq_ref, dk_ref, dv_ref, dz_ref, dxc_ref, dgt_ref, dnw_ref, dsk_ref, dc_s, dn_s):

        qs, k, v = qs_ref[...], k_ref[...], v_ref[...]
        cmat, nvec, m = cs_ref[0], ns_ref[0], ms_ref[0][:, :1]
        co = _ml_core(gt_ref[...] + bg_ref[...], hd, c == nc - 1, m, qs, k, v, cmat, nvec)
        lane, row = co["lane"], co["row"]
        wt, wprev, s, cb, qf = co["wt"], co["wprev"], co["s"], co["cb"], co["qf"]
        h = h_ref[...]
        hn, rstd = _ml_headnorm(h)
        xc = _silu(pre_ref[...])
        zv = z_ref[...]
        nw, sk = nw_ref[...], sk_ref[...]
        dy = dy_ref[...]
        dz_ref[...] = _bf(dy * (hn * nw + sk * xc) * _dsilu(zv))
        do = dy * _silu(zv)
        dsk_ref[...] += jnp.sum(do * xc, axis=0, keepdims=True)
        dnw_ref[...] += jnp.sum(do * hn, axis=0, keepdims=True)
        dxc_ref[...] = do * sk
        dhn = do * nw
        dh = rstd * (dhn - jnp.mean(dhn, axis=1, keepdims=True) - hn * jnp.mean(dhn * hn, axis=1, keepdims=True))
        rinv = 1.0 / co["dd"]
        dnum = dh * rinv
        ddd = -jnp.sum(dh * h, axis=1, keepdims=True) * rinv
        den = co["den"]
        dden = jnp.where(jnp.abs(den) >= co["emt"], ddd * jnp.sign(den), 0.0)
        dnb = _bf(dnum)
        ds = _dot(dnb, v, NT) + dden
        dv = _dot(_bf(s), dnb, TN)
        dnw_ = _bf(dnum * wprev)
        dwn = dden * wprev
        dqs = _dot(dnw_, cb, NT) + dwn * nvec
        dc_out = _dot(qs, dnw_, TN)
        dn_out = jnp.sum(dwn * qf, axis=0, keepdims=True)
        dwprev = jnp.sum(dnum * co["qc"], axis=1, keepdims=True) + dden * co["qn"]
        ds0 = _bf(ds * wt)
        ddm = ds * s
        dqs = dqs + _dot(ds0, k)
        dk = _dot(ds0, qs, TN)
        colc = jnp.sum(ddm.T, axis=1, keepdims=True)
        dbc = dwprev * wprev + jnp.sum(ddm, axis=1, keepdims=True) - colc
        dig = colc
        dcn, dnn = dc_s[...], dn_s[...]
        dcb = _bf(dcn)
        decay, e, kf, wk = co["decay"], co["e"], co["kf"], co["wk"]
        ddecay = (jnp.sum(jnp.sum(dcn * cmat, axis=1, keepdims=True), axis=0, keepdims=True)
                  + jnp.sum(dnn * nvec, axis=1, keepdims=True))
        dwk = _dot(v, dcb, NT) + dnn
        dv = dv + _dot(_bf(wk), dcb)
        dk = dk + e * dwk
        dg = jnp.sum(dwk * kf, axis=1, keepdims=True) * e
        dblast = ddecay * decay + jnp.sum(dg, axis=0, keepdims=True)
        dbc = dbc - dg + jnp.where(row[:, :1] == CHUNK - 1, dblast, 0.0)
        dig = dig + dg
        dc_s[...] = decay * dcn + dc_out
        dn_s[...] = decay * dnn + dn_out
        dlf = _cumsum_rows(jnp.broadcast_to(dbc, (CHUNK, CHUNK)), row, rev=True)[:, :1]
        dfp = dlf * (1.0 - jax.nn.sigmoid(co["fpc"]))
        dig = jnp.where(co["valid"], dig, 0.0)
        dfp = jnp.where(co["valid"], dfp, 0.0)
        dgt_ref[...] = jnp.where(lane == hd, dig, 0.0) + jnp.where(lane == hd + ML_HEADS, dfp, 0.0)
        dq_ref[...] = _bf(dqs * ML_SCALE)
        dk_ref[...] = _bf(dk)
        dv_ref[...] = _bf(dv)

    ob = jax.ShapeDtypeStruct((bsz, tp, ML_WIDTH), BF16)
    dq, dk, dv, dz, dxc, dgt, dnw, dsk = _pc(
        body, name="ml_chunk_bwd", grid=(ML_HEADS, nc),
        in_specs=[sp["hrow"]] * 4 + [sp["gates"], sp["bias"], sp["hrow"], sp["hrow"], sp["hvec"], sp["hvec"],
                                     sp["hrow"], sp["cs"], sp["ns"], sp["ms"]],
        out_specs=[sp["hrow"]] * 5 + [sp["dgates"], sp["hvec"], sp["hvec"]],
        out_shape=[ob, ob, ob, ob, jax.ShapeDtypeStruct((bsz, tp, ML_WIDTH), F32),
                   jax.ShapeDtypeStruct((ML_HEADS, bsz, tp, CHUNK), F32),
                   jax.ShapeDtypeStruct((1, ML_WIDTH), F32), jax.ShapeDtypeStruct((1, ML_WIDTH), F32)],
        scratch=[pltpu.VMEM((bsz, ML_DH, ML_DH), F32), pltpu.VMEM((bsz, 1, ML_DH), F32)], vmem=16 << 20, dep=dep,
    )(*[_seq(a, bsz) for a in (dyb, qs, k, v, gates)], b_gate, _seq(pre, bsz), _seq(z, bsz), nw, sk, _seq(h, bsz),
      cs, ns, ms)
    return (dq.reshape(r, ML_WIDTH), dk.reshape(r, ML_WIDTH), dv.reshape(r, ML_WIDTH), dz.reshape(r, ML_WIDTH),
            dxc.reshape(r, ML_WIDTH), dgt.reshape(ML_HEADS, r, CHUNK), dnw, dsk)


def _ml_pre_bwd(dq, dk, dv, dgates, dxc_skip, pre, x, q, k, v, wq, wk, wv, wgq, wgk, wgv, bsz, nc):
    r = x.shape[0]
    tr = _pick(r, (256, 128))
    nt = r // tr
    hrow = pl.BlockSpec((tr, ML_DH), lambda h, i: (i, h))
    wexp = pl.BlockSpec((1, ML_LB, CHUNK, CHUNK), lambda h, i: (h, 0, 0, 0))
    wcmp = pl.BlockSpec((1, ML_DH, CHUNK), lambda h, i: (h, 0, 0))
    wg = pl.BlockSpec((ML_DH, CHUNK), lambda h, i: (h, 0))
    dgs = pl.BlockSpec((tr, CHUNK), lambda h, i: (i, 0))
    bgs = pl.BlockSpec((1, 1, CHUNK), lambda h, i: (h, 0, 0))

    def body(dq_ref, dk_ref, dv_ref, dg_ref, dxs_ref, pre_ref, x_ref, q_ref, k_ref, v_ref, wq_ref, wk_ref, wv_ref,
             gq_ref, gk_ref, gv_ref, dpre_ref, dxv_ref, cq_ref, ck_ref, cv_ref, dgq_ref, dgk_ref, dgv_ref, dbg_ref,
             dwq_ref, dwk_ref, dwv_ref):
        i = pl.program_id(1)

        @pl.when(i == 0)
        def _():
            for ref in (dwq_ref, dwk_ref, dwv_ref, dgq_ref, dgk_ref, dgv_ref, dbg_ref):
                ref[...] = jnp.zeros_like(ref)

        dgt = dg_ref[...]
        dbg_ref[0] += jnp.sum(dgt, axis=0, keepdims=True)
        dgb = _bf(dgt)
        dqt = _bf(dq_ref[...].astype(F32) + _dot(dgb, gq_ref[...], NT))
        dkt = _bf(dk_ref[...].astype(F32) + _dot(dgb, gk_ref[...], NT))
        dvt = _bf(dv_ref[...].astype(F32) + _dot(dgb, gv_ref[...], NT))
        dgq_ref[...] += _dot(q_ref[...], dgb, TN)
        dgk_ref[...] += _dot(k_ref[...], dgb, TN)
        dgv_ref[...] += _dot(v_ref[...], dgb, TN)
        prev = pre_ref[...]
        xcb = _bf(_silu(prev))
        xb = _bf(x_ref[...])
        for j in range(ML_LB):
            sl = slice(j * CHUNK, (j + 1) * CHUNK)
            dwq_ref[j] += _dot(xcb[:, sl], dqt[:, sl], TN)
            dwk_ref[j] += _dot(xcb[:, sl], dkt[:, sl], TN)
            dwv_ref[j] += _dot(xb[:, sl], dvt[:, sl], TN)
        dxc = _headwise_dot(dqt, wq_ref, NT) + _headwise_dot(dkt, wk_ref, NT) + dxs_ref[...]
        dpre_ref[...] = dxc * _dsilu(prev)
        dxv_ref[...] = _headwise_dot(dvt, wv_ref, NT)

        @pl.when(i == nt - 1)
        def _():
            rr = lax.broadcasted_iota(jnp.int32, (CHUNK, CHUNK), 0)
            cc = lax.broadcasted_iota(jnp.int32, (CHUNK, CHUNK), 1)
            diag = rr // QKV_BLOCK == cc // QKV_BLOCK
            fold = jnp.where(rr % QKV_BLOCK == cc, 1.0, 0.0).astype(F32)
            for src, dst in ((dwq_ref, cq_ref), (dwk_ref, ck_ref), (dwv_ref, cv_ref)):
                for j in range(ML_LB):
                    dst[0, j * CHUNK:(j + 1) * CHUNK, :] = jnp.dot(
                        jnp.where(diag, src[j], 0.0), fold, precision=HI, preferred_element_type=F32)

    f = jax.ShapeDtypeStruct((r, ML_WIDTH), F32)
    wc = jax.ShapeDtypeStruct((ML_HEADS, ML_DH, CHUNK), F32)
    wgs = jax.ShapeDtypeStruct((ML_WIDTH, CHUNK), F32)
    return _pc(body, name="ml_pre_bwd", grid=(ML_HEADS, nt),
               in_specs=[hrow, hrow, hrow, dgs, hrow, hrow, hrow, hrow, hrow, hrow, wexp, wexp, wexp, wg, wg, wg],
               out_specs=[hrow, hrow, wcmp, wcmp, wcmp, wg, wg, wg, bgs],
               out_shape=[f, f, wc, wc, wc, wgs, wgs, wgs, jax.ShapeDtypeStruct((ML_HEADS, 1, CHUNK), F32)],
               scratch=[pltpu.VMEM((ML_LB, CHUNK, CHUNK), F32)] * 3,
               vmem=8 << 20)(dq, dk, dv, dgates, dxc_skip, pre, x, q, k, v, wq, wk, wv, wgq, wgk, wgv)


def _pad_lanes(w):
    return jnp.pad(w, ((0, 0), (0, CHUNK - w.shape[1])))


def _ml_weights(conv_w, conv_b, wq, wk, wv, w_gate, b_gate, norm_w, skip):
    return dict(
        conv_w=conv_w, conv_b=conv_b,
        wq=_bf(_headwise_expand(wq)), wk=_bf(_headwise_expand(wk)), wv=_bf(_headwise_expand(wv)),
        wgq=_bf(_pad_lanes(w_gate[:ML_WIDTH])), wgk=_bf(_pad_lanes(w_gate[ML_WIDTH:2 * ML_WIDTH])),
        wgv=_bf(_pad_lanes(w_gate[2 * ML_WIDTH:])), b_gate=_pad_lanes(b_gate.reshape(1, -1)),
        norm=norm_w.reshape(1, ML_WIDTH), skip=skip.reshape(1, ML_WIDTH))


def _ml_layer_fwd(x, z, w, bsz, nc):
    pre = _conv_fwd(x, w["conv_w"], w["conv_b"], bsz, nc, name="ml_conv")
    qs, k, v, gates = _ml_pre(pre, x, w["wq"], w["wk"], w["wv"], w["wgq"], w["wgk"], w["wgv"], bsz, nc)
    h, yb, cs, ns, ms = _ml_chunk_fwd(qs, k, v, gates, w["b_gate"], pre, z, w["norm"], w["skip"], bsz, nc)
    return yb, dict(pre=pre, qs=qs, k=k, v=v, gates=gates, h=h, cs=cs, ns=ns, ms=ms)


def _ml_layer_bwd(dyb, x, z, sv, w, bsz, nc, dep=None):
    dq, dk, dv, dz, dxc, dgates, dnw, dsk = _ml_chunk_bwd(
        dyb, sv["qs"], sv["k"], sv["v"], sv["gates"], w["b_gate"], sv["pre"], z, w["norm"], w["skip"], sv["h"],
        sv["cs"], sv["ns"], sv["ms"], bsz, nc, dep=dep)
    dpre, dxv, dwq, dwk, dwv, dgq, dgk, dgv, dbg = _ml_pre_bwd(
        dq, dk, dv, _sum_heads(dgates, name="ml_dgates_sum"), dxc, sv["pre"], x, sv["qs"], sv["k"], sv["v"], w["wq"],
        w["wk"], w["wv"], w["wgq"], w["wgk"], w["wgv"], bsz, nc)
    dx, dcw, dcb = _conv_bwd(dpre, x, w["conv_w"], bsz, nc, name="ml_conv_bwd", add=dxv)
    ng = 2 * ML_HEADS
    grads = dict(
        ml_conv_w=dcw, ml_conv_b=dcb, ml_wq=_headwise_extract(dwq), ml_wk=_headwise_extract(dwk),
        ml_wv=_headwise_extract(dwv),
        ml_w_gate=jnp.concatenate([dgq[:, :ng] * (1.0 / ML_SCALE), dgk[:, :ng], dgv[:, :ng]], axis=0),
        ml_b_gate=dbg[0][:, :ng], ml_norm=dnw, ml_skip=dsk)
    return dx, dz, grads


HI = lax.Precision.HIGHEST


def _softplus(x):
    return jnp.maximum(x, 0.0) + jnp.log(1.0 + jnp.exp(-jnp.abs(x)))


def _lane_cumsum(x, lane, rev=False):
    del lane
    return _dot_terms(x, _tri(not rev), NN, exact_rhs=True, terms=3)


def _dot_terms(lhs, rhs, dims, *, exact_rhs, terms):
    x = lhs if exact_rhs else rhs
    sel = _bf(rhs if exact_rhs else lhs)
    acc = None
    for _ in range(terms):
        piece = _bf(x)
        part = _dot(piece, sel, dims) if exact_rhs else _dot(sel, piece, dims)
        acc = part if acc is None else acc + part
        x = x - piece.astype(F32)
    return acc


def _head_sum_matrix():
    r = lax.broadcasted_iota(jnp.int32, (SSD_HPG, SSD_GW), 0)
    l = lax.broadcasted_iota(jnp.int32, (SSD_HPG, SSD_GW), 1)
    return jnp.where(l // SSD_P == r, 1.0, 0.0).astype(F32)


def _ssd_dt_specs(nc):
    return dict(rows=pl.BlockSpec((1, SSD_HEADS, CHUNK), lambda b, c: (b, 0, c)),
                col=pl.BlockSpec((SSD_HEADS, 1), lambda b, c: (0, 0)),
                acc=pl.BlockSpec((SSD_HEADS, CHUNK), lambda b, c: (0, 0)))


def _ssd_dt_valid(c):
    lane = lax.broadcasted_iota(jnp.int32, (SSD_HEADS, CHUNK), 1)
    return jnp.logical_or(c > 0, lane >= PAD_ROWS)


def _ssd_dt_prep(dt_raw, dt_bias, a_log, bsz, nc):
    sp = _ssd_dt_specs(nc)

    def body(raw_ref, b_ref, al_ref, dt_ref, cum_ref):
        dt = jnp.where(_ssd_dt_valid(pl.program_id(1)), _softplus(raw_ref[0] + b_ref[...]), 0.0)
        dt_ref[0] = dt
        cum_ref[0] = _lane_cumsum(dt * -jnp.exp(al_ref[...]), None)

    o = jax.ShapeDtypeStruct(dt_raw.shape, F32)
    return _pc(body, name="ssd_dt_prep", grid=(bsz, nc), in_specs=[sp["rows"], sp["col"], sp["col"]],
               out_specs=[sp["rows"], sp["rows"]], out_shape=[o, o], vmem=1 << 20)(dt_raw, dt_bias, a_log)


def _ssd_dt_post(dcum, ddt, dt_raw, dt_bias, a_log, bsz, nc):
    sp = _ssd_dt_specs(nc)

    def body(dcum_ref, ddt_ref, raw_ref, b_ref, al_ref, out_ref, dbias_ref, dal_ref):
        b, c = pl.program_id(0), pl.program_id(1)

        @pl.when((b == 0) & (c == 0))
        def _():
            dbias_ref[...] = jnp.zeros_like(dbias_ref)
            dal_ref[...] = jnp.zeros_like(dal_ref)

        valid = _ssd_dt_valid(c)
        pre = raw_ref[0] + b_ref[...]
        dt = jnp.where(valid, _softplus(pre), 0.0)
        a = -jnp.exp(al_ref[...])
        dda = _lane_cumsum(dcum_ref[0], None, rev=True)
        ddt_raw = jnp.where(valid, ddt_ref[0] + dda * a, 0.0) * jax.nn.sigmoid(pre)
        out_ref[0] = ddt_raw
        dbias_ref[...] += jnp.sum(ddt_raw, axis=1, keepdims=True)
        dal_ref[...] += jnp.sum(dda * dt, axis=1, keepdims=True) * a

    acc = jax.ShapeDtypeStruct((SSD_HEADS, CHUNK), F32)
    return _pc(body, name="ssd_dt_post", grid=(bsz, nc),
               in_specs=[sp["rows"], sp["rows"], sp["rows"], sp["col"], sp["col"]],
               out_specs=[sp["rows"], sp["acc"], sp["acc"]],
               out_shape=[jax.ShapeDtypeStruct(dt_raw.shape, F32), acc, acc], vmem=1 << 20,
               )(dcum, ddt, dt_raw, dt_bias, a_log)


def _ssd_core(xs, bm, cm, dt, cum):
    sq = (CHUNK, CHUNK)
    lane8 = lax.broadcasted_iota(jnp.int32, (SSD_HPG, CHUNK), 1)
    lane = lax.broadcasted_iota(jnp.int32, sq, 1)
    row = lax.broadcasted_iota(jnp.int32, sq, 0)
    low = lane < SSD_P
    cb = _dot(_bf(cm), _bf(bm), NT)
    heads = []
    for r in range(SSD_HPG):
        rowb = jnp.broadcast_to(cum[r:r + 1, :], sq)
        colb = rowb.T
        seg = jnp.exp(jnp.where(lane <= row, colb - rowb, NEG))
        dtrow = jnp.broadcast_to(dt[r:r + 1, :], sq)
        lastb = colb[CHUNK - 1:CHUNK, :]
        heads.append(dict(seg=seg, dtrow=dtrow, w=cb * seg * dtrow, ecol=jnp.exp(colb),
                          dec=jnp.exp(lastb - colb) * dtrow.T, elast=jnp.exp(lastb)))

    def pairs(key):
        return jnp.concatenate([jnp.where(low[:heads[0][key].shape[0]], heads[2 * j][key], heads[2 * j + 1][key])
                                for j in range(SSD_HPG // 2)], axis=1)

    return dict(lane8=lane8, low=low, dt=dt, cum=cum, cb=cb, heads=heads,
                expc=pairs("ecol"), dec=pairs("dec"), elast=pairs("elast"))


def _ssd_specs(nc, rev, bsz):
    def cc(c):
        return (nc - 1 - c) if rev else c

    return dict(
        wide=pl.BlockSpec((bsz, CHUNK, SSD_GW), lambda g, c: (0, cc(c), g)),
        narrow=pl.BlockSpec((bsz, CHUNK, SSD_N), lambda g, c: (0, cc(c), g)),
        dtT=pl.BlockSpec((bsz, SSD_HPG, CHUNK), lambda g, c: (0, g, cc(c))),
        hcol=pl.BlockSpec((SSD_HPG, 1), lambda g, c: (g, 0)),
        hacc=pl.BlockSpec((SSD_HPG, CHUNK), lambda g, c: (g, 0)),
        gvec=pl.BlockSpec((1, SSD_GW), lambda g, c: (0, g)),
        state=pl.BlockSpec((bsz, 1, SSD_N, SSD_GW), lambda g, c: (0, g * nc + cc(c), 0, 0)),
    )


def _ssd_chunk_fwd(xs_pre, bm_pre, cm_pre, dt, cum, d_exp, z, gnorm, bsz, nc):
    tp = xs_pre.shape[1]
    sp = _ssd_specs(nc, False, bsz)

    def body(xs_all, bm_all, cm_all, dt_all, cum_all, d_ref, z_all, gn_ref, y_all, yn_all, st_all, st_sall):
        @pl.when(pl.program_id(1) == 0)
        def _():
            st_sall[...] = jnp.zeros_like(st_sall)

        for bi in range(bsz):
            one(xs_all.at[bi], bm_all.at[bi], cm_all.at[bi], dt_all.at[bi], cum_all.at[bi], d_ref, z_all.at[bi],
                gn_ref, y_all.at[bi], yn_all.at[bi], st_all.at[bi], st_sall.at[bi])

    def one(xs_ref, bm_ref, cm_ref, dt_ref, cum_ref, d_ref, z_ref, gn_ref, y_ref, yn_ref, st_ref, st_s):
        state = st_s[...]
        st_ref[0] = state
        xs, bm, cm = _silu(xs_ref[...]), _silu(bm_ref[...]), _silu(cm_ref[...])
        co = _ssd_core(xs, bm, cm, dt_ref[...], cum_ref[...])
        low, hd = co["low"], co["heads"]
        ys = []
        for j in range(SSD_HPG // 2):
            xp = xs[:, j * CHUNK:(j + 1) * CHUNK]
            lhs = jnp.concatenate([hd[2 * j]["w"], hd[2 * j + 1]["w"]], axis=1)
            rhs = jnp.concatenate([jnp.where(low, xp, 0.0), jnp.where(low, 0.0, xp)], axis=0)
            ys.append(_dot(_bf(lhs), _bf(rhs)))
        cmb = _bf(cm)
        y = jnp.concatenate(ys, axis=1) + co["expc"] * _dot(cmb, _bf(state)) + d_ref[...] * xs
        y_ref[...] = y
        yg = y * _silu(z_ref[...])
        rstd = lax.rsqrt(jnp.mean(yg * yg, axis=1, keepdims=True) + NORM_EPS)
        yn_ref[...] = _bf(yg * rstd * gn_ref[...])
        st_s[...] = co["elast"] * state + _dot(_bf(bm), _bf(xs * co["dec"]), TN)

    return _pc(body, name="ssd_chunk_fwd", grid=(SSD_GROUPS, nc),
               in_specs=[sp["wide"], sp["narrow"], sp["narrow"], sp["dtT"], sp["dtT"], sp["gvec"], sp["wide"],
                         sp["gvec"]],
               out_specs=[sp["wide"], sp["wide"], sp["state"]],
               out_shape=[jax.ShapeDtypeStruct((bsz, tp, SSD_INNER), F32),
                          jax.ShapeDtypeStruct((bsz, tp, SSD_INNER), BF16),
                          jax.ShapeDtypeStruct((bsz, SSD_GROUPS * nc, SSD_N, SSD_GW), F32)],
               scratch=[pltpu.VMEM((bsz, SSD_N, SSD_GW), F32)], vmem=12 << 20,
               )(xs_pre, bm_pre, cm_pre, dt, cum, d_exp, z, gnorm)


def _ssd_chunk_bwd(dyn, xs_pre, bm_pre, cm_pre, dt, cum, d_exp, z, gnorm, y, states, bsz, nc):
    tp = xs_pre.shape[1]
    sp = _ssd_specs(nc, True, bsz)

    def body(dyn_all, xs_all, bm_all, cm_all, dt_all, cum_all, d_ref, z_all, gn_ref, y_all, st_all,
             dxs_all, dbm_all, dcm_all, dz_all, dcum_all, ddt_all, dgn_ref, dd_ref, ds_sall):
        @pl.when(pl.program_id(1) == 0)
        def _():
            for ref in (dgn_ref, dd_ref, ds_sall):
                ref[...] = jnp.zeros_like(ref)

        for bi in range(bsz):
            one(dyn_all.at[bi], xs_all.at[bi], bm_all.at[bi], cm_all.at[bi], dt_all.at[bi], cum_all.at[bi], d_ref,
                z_all.at[bi], gn_ref, y_all.at[bi], st_all.at[bi], dxs_all.at[bi], dbm_all.at[bi], dcm_all.at[bi],
                dz_all.at[bi], dcum_all.at[bi], ddt_all.at[bi], dgn_ref, dd_ref, ds_sall.at[bi])

    def one(dyn_ref, xs_ref, bm_ref, cm_ref, dt_ref, cum_ref, d_ref, z_ref, gn_ref, y_ref, st_ref,
            dxs_ref, dbm_ref, dcm_ref, dz_ref, dcum_ref, ddt_ref, dgn_ref, dd_ref, ds_s):
        xs_p, bm_p, cm_p = xs_ref[...], bm_ref[...], cm_ref[...]
        xs, bm, cm = _silu(xs_p), _silu(bm_p), _silu(cm_p)
        state = st_ref[0]
        co = _ssd_core(xs, bm, cm, dt_ref[...], cum_ref[...])
        low, hd, lane8, cb = co["low"], co["heads"], co["lane8"], co["cb"]
        dt, cum = co["dt"], co["cum"]
        sub8 = lax.broadcasted_iota(jnp.int32, (SSD_HPG, CHUNK), 0)
        eh = _head_sum_matrix()

        def head_rows(full):
            return _dot_terms(eh, full, NT, exact_rhs=False, terms=2)

        def head_col(vec):
            return jnp.sum(eh * vec, axis=1, keepdims=True)

        yv, zv, gn = y_ref[...], z_ref[...], gn_ref[...]
        sz = _silu(zv)
        yg = yv * sz
        rstd = lax.rsqrt(jnp.mean(yg * yg, axis=1, keepdims=True) + NORM_EPS)
        yh = yg * rstd
        dyn = dyn_ref[...]
        dgn_ref[...] += jnp.sum(dyn * yh, axis=0, keepdims=True)
        dyh = dyn * gn
        dyg = rstd * (dyh - yh * jnp.mean(dyh * yh, axis=1, keepdims=True))
        dz_ref[...] = _bf(dyg * yv * _dsilu(zv))
        dy = dyg * sz
        dxs = dy * d_ref[...]
        dd_ref[...] += head_col(jnp.sum(dy * xs, axis=0, keepdims=True))
        cmb, bmb, stb = _bf(cm), _bf(bm), _bf(state)
        ysv = _dot(cmb, stb)
        expc = co["expc"]
        dys = _bf(dy * expc)
        dcum = head_rows(dy * ysv * expc)
        dcm = _dot(dys, stb, NT)
        dstate_out = _dot(cmb, dys, TN)
        dcb = jnp.zeros((CHUNK, CHUNK), F32)
        ddt = jnp.zeros((SSD_HPG, CHUNK), F32)
        dxs_pairs = []
        for j in range(SSD_HPG // 2):
            sl = slice(j * CHUNK, (j + 1) * CHUNK)
            dyp, xp = dy[:, sl], _bf(xs[:, sl])
            lhs = _bf(jnp.concatenate([hd[2 * j]["w"], hd[2 * j + 1]["w"]], axis=1))
            both = _dot(lhs, _bf(dyp), TN)
            dxs_pairs.append(jnp.where(low, both[:CHUNK], both[CHUNK:]))
            for q, msk in ((2 * j, low), (2 * j + 1, jnp.logical_not(low))):
                h = hd[q]
                dw = _dot(_bf(jnp.where(msk, dyp, 0.0)), xp, NT)
                dcb = dcb + dw * h["seg"] * h["dtrow"]
                e_ = dw * h["w"]
                dcum_r = jnp.sum(e_.T, axis=0, keepdims=True) - jnp.sum(e_, axis=0, keepdims=True)
                ddt_r = jnp.sum(dw * cb * h["seg"], axis=0, keepdims=True)
                dcum = dcum + jnp.where(sub8 == q, dcum_r, 0.0)
                ddt = ddt + jnp.where(sub8 == q, ddt_r, 0.0)
        dxs = dxs + jnp.concatenate(dxs_pairs, axis=1)
        dcbb = _bf(dcb)
        dcm = dcm + _dot(dcbb, bmb)
        dbm = _dot(dcbb, cmb, TN)
        dsn = ds_s[...]
        dsb = _bf(dsn)
        dec = co["dec"]
        dbm = dbm + _dot(_bf(xs * dec), dsb, NT)
        dxd = _dot(bmb, dsb)
        dxs = dxs + dxd * dec
        ddec = head_rows(dxd * xs)
        last = cum[:, CHUNK - 1:CHUNK]
        erow = jnp.exp(last - cum)
        ddt = ddt + ddec * erow
        dla = ddec * erow * dt
        dlast = (jnp.sum(dla, axis=1, keepdims=True)
                 + head_col(jnp.sum(dsn * state, axis=0, keepdims=True)) * jnp.exp(last))
        dcum_ref[...] = dcum - dla + jnp.where(lane8 == CHUNK - 1, dlast, 0.0)
        ddt_ref[...] = ddt
        ds_s[...] = co["elast"] * dsn + dstate_out
        dxs_ref[...] = dxs * _dsilu(xs_p)
        dbm_ref[...] = dbm * _dsilu(bm_p)
        dcm_ref[...] = dcm * _dsilu(cm_p)

    st = jax.ShapeDtypeStruct
    hacc = st((SSD_HEADS, CHUNK), F32)
    return _pc(body, name="ssd_chunk_bwd", grid=(SSD_GROUPS, nc),
               in_specs=[sp["wide"], sp["wide"], sp["narrow"], sp["narrow"], sp["dtT"], sp["dtT"], sp["gvec"],
                         sp["wide"], sp["gvec"], sp["wide"], sp["state"]],
               out_specs=[sp["wide"], sp["narrow"], sp["narrow"], sp["wide"], sp["dtT"], sp["dtT"], sp["gvec"],
                          sp["hacc"]],
               out_shape=[st((bsz, tp, SSD_INNER), F32), st((bsz, tp, SSD_BC), F32), st((bsz, tp, SSD_BC), F32),
                          st((bsz, tp, SSD_INNER), BF16), st((bsz, SSD_HEADS, tp), F32),
                          st((bsz, SSD_HEADS, tp), F32), st((1, SSD_INNER), F32), hacc],
               scratch=[pltpu.VMEM((bsz, SSD_N, SSD_GW), F32)], vmem=20 << 20,
               )(dyn, xs_pre, bm_pre, cm_pre, dt, cum, d_exp, z, gnorm, y, states)


SSD_BC = SSD_GROUPS * SSD_N


def _ssd_weights(conv_w, conv_b, dt_bias, a_log, d, gnorm):
    cuts = (0, SSD_INNER, SSD_INNER + SSD_BC, SSD_INNER + 2 * SSD_BC)
    return dict(
        conv_w=[conv_w[:, cuts[i]:cuts[i + 1]] for i in range(3)],
        conv_b=[conv_b[cuts[i]:cuts[i + 1]] for i in range(3)],
        dt_bias=dt_bias.reshape(SSD_HEADS, 1), a_log=a_log.reshape(SSD_HEADS, 1),
        d_exp=jnp.repeat(d.reshape(SSD_HEADS), SSD_P).reshape(1, SSD_INNER), gnorm=gnorm.reshape(1, SSD_INNER))


def _ssd_layer_fwd(z, xs_in, bm_in, cm_in, dt_rows, w, bsz, nc):
    pres = [_conv_fwd(a, w["conv_w"][i], w["conv_b"][i], bsz, nc, name=f"ssd_conv{i}")
            for i, a in enumerate((xs_in, bm_in, cm_in))]
    def seq(a):
        return a.reshape(bsz, nc * CHUNK, a.shape[-1])

    dt_t = jnp.swapaxes(seq(dt_rows)[:, :, :SSD_HEADS], 1, 2)
    dt, cum = _ssd_dt_prep(dt_t, w["dt_bias"], w["a_log"], bsz, nc)
    y, yn, states = _ssd_chunk_fwd(seq(pres[0]), seq(pres[1]), seq(pres[2]), dt, cum, w["d_exp"], seq(z),
                                   w["gnorm"], bsz, nc)
    return yn.reshape(-1, SSD_INNER), dict(pres=pres, dt_t=dt_t, dt=dt, cum=cum, y=y, states=states)


def _ssd_layer_bwd(dyn, z, xs_in, bm_in, cm_in, sv, w, bsz, nc):
    pres = sv["pres"]

    def seq(a):
        return a.reshape(bsz, nc * CHUNK, a.shape[-1])

    def rows(a):
        return a.reshape(-1, a.shape[-1])

    dxs_p, dbm_p, dcm_p, dz, dcum, ddt_direct, dgn, dd = _ssd_chunk_bwd(
        seq(dyn), seq(pres[0]), seq(pres[1]), seq(pres[2]), sv["dt"], sv["cum"], w["d_exp"], seq(z), w["gnorm"],
        sv["y"], sv["states"], bsz, nc)
    ddt_t, dbias, dal = _ssd_dt_post(dcum, ddt_direct, sv["dt_t"], w["dt_bias"], w["a_log"], bsz, nc)
    dz = rows(dz)
    outs = [_conv_bwd(rows(dp), a, w["conv_w"][i], bsz, nc, name=f"ssd_conv_bwd{i}")
            for i, (dp, a) in enumerate(((dxs_p, xs_in), (dbm_p, bm_in), (dcm_p, cm_in)))]
    ddt = _bf(_pad_lanes(rows(jnp.swapaxes(ddt_t, 1, 2))))
    grads = dict(
        ssd_conv_w=jnp.concatenate([o[1] for o in outs], axis=1),
        ssd_conv_b=jnp.concatenate([o[2] for o in outs], axis=1),
        ssd_dt_bias=dbias[:, 0], ssd_a_log=dal[:, 0], ssd_d=dd[:, 0], ssd_gnorm=dgn)
    return dz, outs[0][0], outs[1][0], outs[2][0], ddt, grads


WNAMES = ("meta_tokens", "ab_norm", "ab_w_in", "s5_lambda_re", "s5_lambda_im", "s5_log_dt", "s5_b_re", "s5_b_im",
          "s5_c_re", "s5_c_im", "s5_d", "s5_glu_w", "s5_glu_b", "ml_conv_w", "ml_conv_b", "ml_wq", "ml_wk", "ml_wv",
          "ml_w_gate", "ml_b_gate", "ml_norm", "ml_skip", "ab_w_out", "ssd_norm", "ssd_w_in", "ssd_conv_w",
          "ssd_conv_b", "ssd_dt_bias", "ssd_a_log", "ssd_d", "ssd_gnorm", "ssd_w_out", "final_norm")
SHARD_AXIS = dict(meta_tokens=1, ab_w_in=2, s5_glu_w=1, ml_conv_w=2, ml_wq=1, ml_wk=1, ml_wv=1, ml_w_gate=1,
                  ab_w_out=1, ssd_norm=1, ssd_w_in=2, ssd_conv_w=2, ssd_conv_b=1, ssd_gnorm=1, ssd_w_out=1)
BIG = ("ab_w_in", "s5_glu_w", "ab_w_out", "ssd_w_in", "ssd_w_out")
SMALL = tuple(n for n in WNAMES if n in SHARD_AXIS and n not in BIG)
REPL = tuple(n for n in WNAMES if n not in SHARD_AXIS)
PACK_ALIGN = 8 * 128


def _pack(arrs):
    lead = arrs[0][1]
    parts = []
    for a, nlead in arrs:
        f = a.reshape(a.shape[:nlead] + (-1,))
        parts.append(jnp.pad(f, [(0, 0)] * nlead + [(0, (-f.shape[-1]) % PACK_ALIGN)]))
    flat = jnp.concatenate(parts, axis=lead)
    return flat.reshape(flat.shape[:lead] + (-1, 128))


def _unpack(p, shapes):
    out, off = [], 0
    lead = p.shape[:-2]
    flat = p.reshape(lead + (-1,))
    for s in shapes:
        n = math.prod(s)
        out.append(flat[..., off:off + n].reshape(lead + tuple(s)))
        off += -(-n // PACK_ALIGN) * PACK_ALIGN
    return out


def _assemble(g, axis):
    m = jnp.moveaxis(g, 0, axis)
    return m.reshape(m.shape[:axis] + (m.shape[axis] * m.shape[axis + 1],) + m.shape[axis + 2:])


def _split(full, axis):
    s = full.shape
    m = full.reshape(s[:axis] + (N_DEV, s[axis] // N_DEV) + s[axis + 1:])
    return jnp.moveaxis(m, axis, 0)


def kernel(x, *rest):
    nw = len(WNAMES)
    w = dict(zip(WNAMES, rest[:nw]))
    loss_target = rest[nw]
    mom = dict(zip(WNAMES, rest[nw + 1:2 * nw + 1]))
    var = dict(zip(WNAMES, rest[2 * nw + 1:3 * nw + 1]))
    bsz = x.shape[0]
    nc = 1 + SEQ // CHUNK
    tp = nc * CHUNK

    local = {n: _bf(w[n][0]) for n in BIG}
    small_local = _pack([(w[n], 0) for n in SMALL])
    gs = _exchange_start([small_local], ["ag"], name="gather_s")
    ga = _exchange_start([local["ab_w_in"]], ["ag"], name="gather_a", dep=gs["token"], peers=SAME_CORE[1:])
    got_s = _exchange_wait(gs, ga["token"])

    def assemble_big(n, got):
        return _assemble(got[:, None], SHARD_AXIS[n])[0]

    full = {}
    for n, g in zip(SMALL, _unpack(got_s[0], [w[n].shape for n in SMALL])):
        full[n] = _assemble(g, SHARD_AXIS[n])[0] if n != "meta_tokens" else _assemble(g, SHARD_AXIS[n])
    for n in REPL:
        full[n] = w[n][0] if n != "final_norm" else w[n]
    glu_b = full["s5_glu_b"].reshape(1, S5_WIDTH)
    meta = jnp.broadcast_to(full["meta_tokens"][None], (bsz, N_META, D_MODEL))
    h0 = jnp.concatenate([jnp.zeros((bsz, PAD_ROWS, D_MODEL), F32), meta, x], axis=1).reshape(bsz * tp, D_MODEL)
    xn0 = _rms_fwd(h0, full["ab_norm"], name="rms0")
    s5p, s5_vjp = _s5_tables(*[full[n] for n in ("s5_lambda_re", "s5_lambda_im", "s5_log_dt", "s5_b_re", "s5_b_im",
                                                   "s5_c_re", "s5_c_im", "s5_d")])
    mlw = _ml_weights(*[full[n] for n in ("ml_conv_w", "ml_conv_b", "ml_wq", "ml_wk", "ml_wv", "ml_w_gate",
                                           "ml_b_gate", "ml_norm", "ml_skip")])
    got_a = _exchange_wait(ga, [xn0, s5p["wbr"], s5p["wcr"], s5p["pr"], mlw["wq"], mlw["wk"], mlw["wv"], mlw["wgq"]])
    fwd_a = _sibling_forward_start(got_a[0], name="gather_a2")
    got_a = [_sibling_forward_wait(fwd_a, fwd_a["token"])]
    gb = _exchange_start([local["s5_glu_w"], local["ab_w_out"]], ["ag", "ag"], name="gather_b", dep=got_a[0])
    gc = _exchange_start([local["ssd_w_in"], local["ssd_w_out"]], ["ag", "ag"], name="gather_c", dep=gb["token"])
    full["ab_w_in"] = assemble_big("ab_w_in", got_a[0])
    cuts0 = (0, S5_WIDTH, 2 * S5_WIDTH, 2 * S5_WIDTH + ML_WIDTH, 2 * (S5_WIDTH + ML_WIDTH))
    w_in0 = [full["ab_w_in"][:, cuts0[i]:cuts0[i + 1]] for i in range(4)]

    u, za, xb, zb = [_mm(xn0, wi, "NN", name=f"in0_{i}") for i, wi in enumerate(w_in0)]
    got_b = []

    def glu_w_after(scan_out):
        got_b.extend(_exchange_wait(gb, scan_out))
        return assemble_big("s5_glu_w", got_b[0])

    sv5 = _s5_layer_fwd(u, s5p, glu_w_after, bsz, nc)
    glu_w = assemble_big("s5_glu_w", got_b[0])
    w_out0 = assemble_big("ab_w_out", got_b[1])
    w_out0 = [w_out0[:S5_WIDTH], w_out0[S5_WIDTH:]]
    ya = _s5_post(sv5["y1"], sv5["glu_pre"], glu_b, za)
    yb, svm = _ml_layer_fwd(xb, zb, mlw, bsz, nc)
    h1 = _mm(ya, w_out0[0], "NN", name="out0_a", add=h0)
    h1 = _mm(yb, w_out0[1], "NN", name="out0_b", add=h1)
    got_c = _exchange_wait(gc, h1)
    w_in1, w_out1 = assemble_big("ssd_w_in", got_c[0]), assemble_big("ssd_w_out", got_c[1])
    cuts1 = (0, SSD_INNER, 2 * SSD_INNER, 2 * SSD_INNER + SSD_BC, 2 * SSD_INNER + 2 * SSD_BC)
    w_tail1 = jnp.pad(w_in1[:, cuts1[2]:], ((0, 0), (0, CHUNK - SSD_HEADS)))
    w_in1 = [w_in1[:, cuts1[i]:cuts1[i + 1]] for i in range(4)] + [_pad_lanes(w_in1[:, cuts1[4]:])]
    xn1 = _rms_fwd(h1, full["ssd_norm"], name="rms1")
    z1, xs_in, bm_in, cm_in, dt_rows = [_mm(xn1, wi, "NN", name=f"in1_{i}") for i, wi in enumerate(w_in1)]
    ssdw = _ssd_weights(*[full[n] for n in ("ssd_conv_w", "ssd_conv_b", "ssd_dt_bias", "ssd_a_log", "ssd_d",
                                             "ssd_gnorm")])
    yn, svs = _ssd_layer_fwd(z1, xs_in, bm_in, cm_in, dt_rows, ssdw, bsz, nc)
    h2 = _mm(yn, w_out1, "NN", name="out1", add=h1)
    loss_part, dh2, dfinal, dh2_b = _final_loss(h2, full["final_norm"], loss_target, bsz, nc)

    g = {"final_norm": dfinal}
    dyn = _mm(dh2_b, w_out1, "NT", name="d_out1")
    g["ssd_w_out"] = _mm(yn, dh2_b, "TN", name="dw_out1", out_dtype=BF16)
    dz1, dxs, dbm, dcm, ddt, gs = _ssd_layer_bwd(dyn, z1, xs_in, bm_in, cm_in, svs, ssdw, bsz, nc)
    g.update(gs)
    dps1 = (dz1, dxs, dbm, dcm, ddt)
    dxn1 = None
    for i, (dp, wi) in enumerate(zip((dz1, dxs, jnp.concatenate([dbm, dcm, ddt], axis=1)),
                                     (w_in1[0], w_in1[1], w_tail1))):
        dxn1 = _mm(dp, wi, "NT", name=f"d_in1_{i}", add=dxn1)
    dw1 = [_mm(xn1, dp, "TN", name=f"dw_in1_{i}", out_dtype=BF16) for i, dp in enumerate(dps1)]
    g["ssd_w_in"] = jnp.concatenate(dw1[:4] + [dw1[4][:, :SSD_HEADS]], axis=1)

    def local_shape(n):
        return w[n].shape

    def slabs(n):
        gf = g[n].reshape((1,) + tuple(g[n].shape)) if n != "meta_tokens" else g[n]
        full_shape = tuple(d * (N_DEV if i == SHARD_AXIS[n] else 1) for i, d in enumerate(local_shape(n)))
        return _split(gf.reshape(full_shape), SHARD_AXIS[n])

    x1 = _exchange_start([slabs("ssd_w_in")[:, 0], slabs("ssd_w_out")[:, 0]], ["a2a", "a2a"], name="grads_1")
    dh1, g["ssd_norm"], dh1_b = _rms_bwd(h1, full["ssd_norm"], dxn1, dh2, name="rms1_bwd", dep=x1["token"])
    dya = _mm(dh1_b, w_out0[0], "NT", name="d_out0_a")
    dyb = _mm(dh1_b, w_out0[1], "NT", name="d_out0_b")
    g["ab_w_out"] = jnp.concatenate([_mm(ya, dh1_b, "TN", name="dw_out0_a", out_dtype=BF16),
                                     _mm(yb, dh1_b, "TN", name="dw_out0_b", out_dtype=BF16)], axis=0)
    du, dza, g5 = _s5_layer_bwd(dya, u, za, sv5, s5p, s5_vjp, glu_w, glu_b, bsz, nc)
    g.update(g5)
    x2 = _exchange_start([slabs("ab_w_out")[:, 0], _bf(slabs("s5_glu_w")[:, 0])], ["a2a", "a2a"], name="grads_2")
    dxb, dzb, gm = _ml_layer_bwd(dyb, xb, zb, svm, mlw, bsz, nc, dep=x2["token"])
    g.update(gm)
    dps0 = (du, dza, dxb, dzb)
    dw0 = [_mm(xn0, dp, "TN", name=f"dw_in0_{i}", out_dtype=BF16, tn=S5_WIDTH, slabs=True) for i, dp in enumerate(dps0)]
    dw_in0_slabs = jnp.concatenate(dw0, axis=0)
    x3 = _exchange_start([dw_in0_slabs], ["a2a"], name="grads_3")
    dxn0 = None
    for i, (dp, wi) in enumerate(zip(dps0, w_in0)):
        dxn0 = _mm(dp, wi, "NT", name=f"d_in0_{i}", add=dxn0, dep=x3["token"] if i == 0 else None)
    grad_x, d_chunk0, g["ab_norm"] = _rms_bwd_first(h0, full["ab_norm"], dxn0, dh1, bsz, nc, name="rms0_bwd")
    g["meta_tokens"] = jnp.sum(d_chunk0[:, PAD_ROWS:], axis=0)

    small_g = _pack([(slabs(n), 1) for n in SMALL])
    repl_g = _pack([(g[n], 0) for n in REPL])
    x4 = _exchange_start([small_g, repl_g, loss_part], ["a2a", "ag", "ag"], name="grads_4")

    def update_big(n, gp):
        return _adamw(w[n][0], mom[n][0], var[n][0], gp, name=f"adamw_{n}")

    res = {}
    ex1 = _exchange_wait(x1, x4["token"])
    res["ssd_w_in"], res["ssd_w_out"] = update_big("ssd_w_in", ex1[0]), update_big("ssd_w_out", ex1[1])
    ex2 = _exchange_wait(x2, res["ssd_w_out"][0])
    res["ab_w_out"], res["s5_glu_w"] = update_big("ab_w_out", ex2[0]), update_big("s5_glu_w", ex2[1])
    ex3 = _exchange_wait(x3, [res[n][0] for n in ("ssd_w_in", "ssd_w_out", "ab_w_out", "s5_glu_w")])
    res["ab_w_in"] = update_big("ab_w_in", ex3[0])
    ex4 = _exchange_wait(x4, res["ab_w_in"][0])
    loss = jnp.sum(ex4[2][:, 0, 0])
    for names, gp, tag in ((SMALL, ex4[0], "small"), (REPL, ex4[1], "repl")):
        shapes = [local_shape(n) for n in names]
        packs = [_pack([(d[n], 0) for n in names]) for d in (w, mom, var)]
        outs = _adamw(packs[0], packs[1], packs[2], gp, name=f"adamw_{tag}")
        for k, o in enumerate(outs):
            for n, a in zip(names, _unpack(o, shapes)):
                res.setdefault(n, [None] * 4)[k] = a
    outs = [loss, grad_x]
    for k in range(4):
        outs += [res[n][k].reshape(local_shape(n)) for n in WNAMES]
    return tuple(outs)
```

```python
import functools
import math

import jax
import jax.numpy as jnp
from jax import lax
from jax.experimental import pallas as pl
from jax.experimental.pallas import tpu as pltpu

F32 = jnp.float32
BF16 = jnp.bfloat16

D_MODEL = 2048
SEQ = 2048
N_META = 16
CHUNK = 128
PAD_ROWS = CHUNK - N_META
NORM_EPS = 1e-6
HEAD_NORM_EPS = 1e-5
S5_WIDTH = 1024
S5_GROUPS = 64
S5_GROUP_SIZE = 16
S5_STATE = 64
S5_GB = 8
S5_LANES = S5_GB * S5_STATE
ML_WIDTH = 3072
ML_HEADS = 8
ML_DH = 384
ML_CONV = 4
QKV_BLOCK = 4
SSD_INNER = 4096
SSD_HEADS = 64
SSD_P = 64
SSD_N = 128
SSD_GROUPS = 8
SSD_HPG = 8
SSD_GW = SSD_HPG * SSD_P
N_DEV = 8
ADAM_LR, ADAM_B1, ADAM_B2, ADAM_EPS, ADAM_WD, ADAM_STEP = 0.001, 0.9, 0.999, 1e-08, 0.01, 10
NEG = -1e30
VMEM_CAP = 60 * 1024 * 1024
MM_BLOCK_BUDGET = 22 * 1024 * 1024
MESH = pl.DeviceIdType.MESH

NN = (((1,), (0,)), ((), ()))
NT = (((1,), (1,)), ((), ()))
TN = (((0,), (0,)), ((), ()))


def _dot(a, b, dims=NN):
    return lax.dot_general(a, b, dims, preferred_element_type=F32)


def _bf(x):
    return x.astype(BF16)


def _pick(n, cands):
    for c in cands:
        if n % c == 0:
            return c
    return n


def _nbytes(shape, dtype):
    return math.prod(shape) * jnp.dtype(dtype).itemsize


ANY_SPEC = pl.BlockSpec(memory_space=pl.ANY)


def _pc(body, *, name, grid, in_specs, out_specs, out_shape, scratch=(), vmem=None, dep=None):
    limit = None if vmem is None else int(min(VMEM_CAP, max(32 * 1024 * 1024, 2 * vmem + (8 << 20))))
    n_in = len(in_specs)
    if dep is not None:
        inner = body

        def body(*refs):
            inner(*refs[:n_in], *refs[n_in + 1:])

        in_specs = list(in_specs) + [ANY_SPEC]
    call = pl.pallas_call(
        body, name=name, grid=grid, in_specs=in_specs, out_specs=out_specs, out_shape=out_shape,
        scratch_shapes=list(scratch),
        compiler_params=pltpu.CompilerParams(dimension_semantics=("arbitrary",) * len(grid), vmem_limit_bytes=limit))
    return call if dep is None else (lambda *args: call(*args, dep))


def _silu(x):
    return x * jax.nn.sigmoid(x)


def _dsilu(x):
    s = jax.nn.sigmoid(x)
    return s * (1.0 + x * (1.0 - s))


def _gelu_and_grad(x):
    c0 = math.sqrt(2.0 / math.pi)
    inner = c0 * (x + 0.044715 * x * x * x)
    t = jnp.tanh(inner)
    g = 0.5 * x * (1.0 + t)
    dg = 0.5 * (1.0 + t) + 0.5 * x * (1.0 - t * t) * c0 * (1.0 + 3 * 0.044715 * x * x)
    return g, dg


def _mm(a, b, mode, *, name, add=None, out_dtype=F32, tn=None, slabs=False, dep=None):
    if mode == "NN":
        (m, k), (k2, n) = a.shape, b.shape
    elif mode == "NT":
        (m, k), (n, k2) = a.shape, b.shape
    else:
        (k, m), (k2, n) = a.shape, b.shape
    assert k == k2, (a.shape, b.shape, mode)
    tm = _pick(m, (1088, 1024, 768, 512, 384, 256, 128))
    def block_bytes(tk, tn_):
        return (_nbytes((tm, tk), a.dtype) + _nbytes((tk, tn_), b.dtype) + _nbytes((tm, tn_), out_dtype)
                + (_nbytes((tm, tn_), F32) if add is not None else 0))

    budget = MM_BLOCK_BUDGET // 2 if mode == "TN" else MM_BLOCK_BUDGET
    if tn is None:
        tn = _pick(n, (512, 384, 256, 128))
        if mode != "TN" and n % 1024 == 0 and block_bytes(k, 1024) <= (2 * budget) // 3:
            tn = 1024
    if mode == "TN" and block_bytes(k, tn) > budget and m % 512 == 0 and tm > 512:
        wide, tm = tm, 512
        if block_bytes(k, tn) > budget:
            tm = wide
    tk = k if block_bytes(k, tn) <= budget else _pick(k, (2176, 2048, 1088, 1024, 768, 512, 384, 256, 128))
    nk = k // tk
    dims = {"NN": NN, "NT": NT, "TN": TN}[mode]

    def body(*refs):
        a_ref, b_ref = refs[0], refs[1]
        add_ref = refs[2] if add is not None else None
        o_ref = refs[3] if add is not None else refs[2]

        def finish(r):
            if add_ref is not None:
                r = r + add_ref[...]
            o_ref[...] = r.reshape(o_ref.shape).astype(o_ref.dtype)

        prod = _dot(_bf(a_ref[...]), _bf(b_ref[...]), dims)
        if nk == 1:
            finish(prod)
            return
        acc_ref = refs[-1]
        kk = pl.program_id(2)

        @pl.when(kk == 0)
        def _():
            acc_ref[...] = prod

        @pl.when(kk > 0)
        def _():
            acc_ref[...] += prod

        @pl.when(kk == nk - 1)
        def _():
            finish(acc_ref[...])

    if mode == "NN":
        a_spec = pl.BlockSpec((tm, tk), lambda i, j, kk: (i, kk))
        b_spec = pl.BlockSpec((tk, tn), lambda i, j, kk: (kk, j))
    elif mode == "NT":
        a_spec = pl.BlockSpec((tm, tk), lambda i, j, kk: (i, kk))
        b_spec = pl.BlockSpec((tn, tk), lambda i, j, kk: (j, kk))
    else:
        a_spec = pl.BlockSpec((tk, tm), lambda i, j, kk: (kk, i))
        b_spec = pl.BlockSpec((tk, tn), lambda i, j, kk: (kk, j))
    in_specs = [a_spec, b_spec]
    args = [a, b]
    if add is not None:
        in_specs.append(pl.BlockSpec((tm, tn), lambda i, j, kk: (i, j)))
        args.append(add)
    if slabs:
        out_shape = jax.ShapeDtypeStruct((n // tn, m, tn), out_dtype)
        out_spec = pl.BlockSpec((1, tm, tn), lambda i, j, kk: (j, i, 0))
    else:
        out_shape = jax.ShapeDtypeStruct((m, n), out_dtype)
        out_spec = pl.BlockSpec((tm, tn), lambda i, j, kk: (i, j))
    return _pc(body, name=name, grid=(m // tm, n // tn, nk), in_specs=in_specs, out_specs=out_spec,
               out_shape=out_shape, scratch=[] if nk == 1 else [pltpu.VMEM((tm, tn), F32)],
               vmem=block_bytes(tk, tn) + (0 if nk == 1 else _nbytes((tm, tn), F32) // 2), dep=dep)(*args)


def _rms_fwd(x, g, *, name):
    r, d = x.shape
    tm = _pick(r, (256, 128))

    def body(x_ref, g_ref, o_ref):
        xv = x_ref[...]
        rstd = lax.rsqrt(jnp.mean(xv * xv, axis=1, keepdims=True) + NORM_EPS)
        o_ref[...] = (xv * rstd * g_ref[...]).astype(o_ref.dtype)

    return _pc(body, name=name, grid=(r // tm,),
               in_specs=[pl.BlockSpec((tm, d), lambda i: (i, 0)), pl.BlockSpec((1, d), lambda i: (0, 0))],
               out_specs=pl.BlockSpec((tm, d), lambda i: (i, 0)), out_shape=jax.ShapeDtypeStruct((r, d), BF16),
               vmem=tm * d * 6)(x, g.reshape(1, d))


def _rms_bwd(x, g, dxn, dres, *, name, dep=None):
    r, d = x.shape
    tm = _pick(r, (256, 128))

    def body(x_ref, g_ref, dxn_ref, dres_ref, dx_ref, dg_ref, db_ref):
        @pl.when(pl.program_id(0) == 0)
        def _():
            dg_ref[...] = jnp.zeros_like(dg_ref)

        xv = x_ref[...]
        rstd = lax.rsqrt(jnp.mean(xv * xv, axis=1, keepdims=True) + NORM_EPS)
        xh = xv * rstd
        dy = dxn_ref[...]
        dg_ref[...] += jnp.sum(dy * xh, axis=0, keepdims=True)
        dyg = dy * g_ref[...]
        dx_ref[...] = dres_ref[...] + rstd * (dyg - xh * jnp.mean(dyg * xh, axis=1, keepdims=True))

        db_ref[...] = _bf(dx_ref[...])

    row = pl.BlockSpec((tm, d), lambda i: (i, 0))
    vec = pl.BlockSpec((1, d), lambda i: (0, 0))
    return _pc(body, name=name, grid=(r // tm,), in_specs=[row, vec, row, row], out_specs=[row, vec, row],
               out_shape=[jax.ShapeDtypeStruct((r, d), F32), jax.ShapeDtypeStruct((1, d), F32),
                          jax.ShapeDtypeStruct((r, d), BF16)],
               vmem=tm * d * 18, dep=dep)(x, g.reshape(1, d), dxn, dres)


def _rms_bwd_first(x, g, dxn, dres, bsz, nc, *, name):
    d = x.shape[1]

    def body(x_ref, g_ref, dxn_ref, dres_ref, gx_ref, d0_ref, dg_ref):
        b, c = pl.program_id(0), pl.program_id(1)

        @pl.when((b == 0) & (c == 0))
        def _():
            dg_ref[...] = jnp.zeros_like(dg_ref)

        xv = x_ref[...]
        rstd = lax.rsqrt(jnp.mean(xv * xv, axis=1, keepdims=True) + NORM_EPS)
        xh = xv * rstd
        dy = dxn_ref[...]
        dg_ref[...] += jnp.sum(dy * xh, axis=0, keepdims=True)
        dyg = dy * g_ref[...]
        dx = dres_ref[...] + rstd * (dyg - xh * jnp.mean(dyg * xh, axis=1, keepdims=True))

        @pl.when(c == 0)
        def _():
            d0_ref[0] = dx

        @pl.when(c > 0)
        def _():
            gx_ref[0] = dx

    row = pl.BlockSpec((CHUNK, d), lambda b, c: (b * nc + c, 0))
    vec = pl.BlockSpec((1, d), lambda b, c: (0, 0))
    return _pc(body, name=name, grid=(bsz, nc), in_specs=[row, vec, row, row],
               out_specs=[pl.BlockSpec((1, CHUNK, d), lambda b, c: (b, jnp.maximum(c - 1, 0), 0)),
                          pl.BlockSpec((1, CHUNK, d), lambda b, c: (b, 0, 0)), vec],
               out_shape=[jax.ShapeDtypeStruct((bsz, (nc - 1) * CHUNK, d), F32),
                          jax.ShapeDtypeStruct((bsz, CHUNK, d), F32), jax.ShapeDtypeStruct((1, d), F32)],
               vmem=CHUNK * d * 24)(x, g.reshape(1, d), dxn, dres)


def _final_loss(h, g, target, bsz, nc):
    d = h.shape[1]

    def body(h_ref, g_ref, t_ref, loss_ref, dh_ref, dg_ref, db_ref):
        b, c = pl.program_id(0), pl.program_id(1)

        @pl.when((b == 0) & (c == 0))
        def _():
            loss_ref[...] = jnp.zeros_like(loss_ref)
            dg_ref[...] = jnp.zeros_like(dg_ref)

        @pl.when(c == 0)
        def _():
            dh_ref[...] = jnp.zeros_like(dh_ref)
            db_ref[...] = jnp.zeros_like(db_ref)

        @pl.when(c > 0)
        def _():
            xv = h_ref[...]
            rstd = lax.rsqrt(jnp.mean(xv * xv, axis=1, keepdims=True) + NORM_EPS)
            xh = xv * rstd
            gv = g_ref[...]
            err = xh * gv - t_ref[0]
            loss_ref[...] += 0.5 * jnp.sum(jnp.mean(err * err, axis=1, keepdims=True))
            dy = err * (1.0 / d)
            dg_ref[...] += jnp.sum(dy * xh, axis=0, keepdims=True)
            dyg = dy * gv
            dh = rstd * (dyg - xh * jnp.mean(dyg * xh, axis=1, keepdims=True))
            dh_ref[...] = dh
            db_ref[...] = _bf(dh)

    row = pl.BlockSpec((CHUNK, d), lambda b, c: (b * nc + c, 0))
    vec = pl.BlockSpec((1, d), lambda b, c: (0, 0))
    return _pc(body, name="final_loss", grid=(bsz, nc),
               in_specs=[row, vec, pl.BlockSpec((1, CHUNK, d), lambda b, c: (b, jnp.maximum(c - 1, 0), 0))],
               out_specs=[pl.BlockSpec((8, 128), lambda b, c: (0, 0)), row, vec, row],
               out_shape=[jax.ShapeDtypeStruct((8, 128), F32), jax.ShapeDtypeStruct(h.shape, F32),
                          jax.ShapeDtypeStruct((1, d), F32), jax.ShapeDtypeStruct(h.shape, BF16)],
               vmem=CHUNK * d * 18)(h, g.reshape(1, d), target)


def _adamw(w, m, v, gparts, *, name):
    r, c = w.shape
    tr = _pick(r, (256, 128)) if r * c * 4 > (1 << 20) else r

    def body(w_ref, m_ref, v_ref, gp_ref, g_ref, d_ref, nm_ref, nv_ref):
        g = gp_ref[0].astype(F32)
        for j in range(1, N_DEV):
            g = g + gp_ref[j].astype(F32)
        mm = ADAM_B1 * m_ref[...] + (1.0 - ADAM_B1) * g
        vv = ADAM_B2 * v_ref[...] + (1.0 - ADAM_B2) * (g * g)
        m_hat = mm / (1.0 - ADAM_B1 ** ADAM_STEP)
        v_hat = vv / (1.0 - ADAM_B2 ** ADAM_STEP)
        g_ref[...] = g
        d_ref[...] = -ADAM_LR * (m_hat / (jnp.sqrt(v_hat) + ADAM_EPS) + ADAM_WD * w_ref[...])
        nm_ref[...] = mm
        nv_ref[...] = vv

    blk = pl.BlockSpec((tr, c), lambda i: (i, 0))
    out = jax.ShapeDtypeStruct((r, c), F32)
    return _pc(body, name=name, grid=(r // tr,),
               in_specs=[blk, blk, blk, pl.BlockSpec((N_DEV, tr, c), lambda i: (0, i, 0))],
               out_specs=[blk, blk, blk, blk], out_shape=[out, out, out, out],
               vmem=tr * c * (4 * 7 + N_DEV * jnp.dtype(gparts.dtype).itemsize))(w, m, v, gparts)


PEERS = (1, 2, 4, 6, 3, 5, 7)
HBM_SPEC = pl.BlockSpec(memory_space=pltpu.HBM)
SEM_SPEC = pl.BlockSpec(memory_space=pltpu.SEMAPHORE)
SIDE_EFFECT = pltpu.SideEffectType.DATAFLOW_SIDE_EFFECTING


def _peer(p):
    x, y, c = lax.axis_index("x"), lax.axis_index("y"), lax.axis_index("c")
    tx, ty, tc = x ^ ((p >> 2) & 1), y ^ ((p >> 1) & 1), c ^ (p & 1)
    return (tx, ty, tc), 4 * tx + 2 * ty + tc


def _place_own(a, kind, *, name):
    rows, cols = a.shape[-2:]
    small = _nbytes((rows, cols), a.dtype) <= (2 << 20)
    tr = rows if small else _pick(rows, (512, 256, 128, 64, 32, 16))
    me = (4 * lax.axis_index("x") + 2 * lax.axis_index("y") + lax.axis_index("c")).astype(jnp.int32).reshape(1)

    def body(me_ref, in_ref, out_ref):
        out_ref[...] = in_ref[...].reshape(out_ref.shape)

    if kind == "a2a":
        in_spec = pl.BlockSpec((1, tr, cols), lambda i, me_ref: (me_ref[0], i, 0))
    else:
        in_spec = pl.BlockSpec((tr, cols), lambda i, me_ref: (i, 0))
    return pl.pallas_call(
        body, name=name, out_shape=jax.ShapeDtypeStruct((N_DEV, rows, cols), a.dtype),
        grid_spec=pltpu.PrefetchScalarGridSpec(
            num_scalar_prefetch=1, grid=(rows // tr,), in_specs=[in_spec],
            out_specs=pl.BlockSpec((1, tr, cols), lambda i, me_ref: (me_ref[0], i, 0))))(me, a)


def _exchange_copies(ins, lands, send_sems, recv_sems, kinds, incoming, peers=PEERS):
    me = 4 * lax.axis_index("x") + 2 * lax.axis_index("y") + lax.axis_index("c")
    copies = []
    for i, kind in enumerate(kinds):
        for p in peers:
            dev, tgt = _peer(p)
            k = i * (N_DEV - 1) + p - 1
            copies.append(pltpu.make_async_remote_copy(
                src_ref=ins[i].at[tgt] if kind == "a2a" else ins[i], dst_ref=lands[i].at[tgt if incoming else me],
                send_sem=send_sems.at[k], recv_sem=recv_sems.at[k], device_id=dev, device_id_type=MESH))
    return copies


def _exchange_start(arrays, kinds, *, name, dep=None, peers=PEERS):
    n = len(arrays)
    lands = [_place_own(a, k, name=f"{name}_own{i}") for i, (a, k) in enumerate(zip(arrays, kinds))]
    extra = [] if dep is None else [dep]

    def body(*refs):
        ins, lnd = refs[:n], refs[n:2 * n]
        send_sems, recv_sems = refs[2 * n + len(extra)], refs[2 * n + len(extra) + 1]
        token = refs[-1]
        for cp in _exchange_copies(ins, lnd, send_sems, recv_sems, kinds, False, peers):
            cp.start()
        token[...] = jnp.zeros_like(token)

    sem = pltpu.SemaphoreType.DMA((n * (N_DEV - 1),))
    outs = pl.pallas_call(
        body, name=name, in_specs=[HBM_SPEC] * (2 * n) + [ANY_SPEC] * len(extra),
        out_specs=[SEM_SPEC, SEM_SPEC] + [HBM_SPEC] * (2 * n) + [pl.BlockSpec(memory_space=pltpu.VMEM)],
        out_shape=[sem, sem] + [pltpu.HBM(a.shape, a.dtype) for a in arrays + lands]
        + [jax.ShapeDtypeStruct((8, 128), F32)],
        input_output_aliases={i: 2 + i for i in range(2 * n)},
        compiler_params=pltpu.CompilerParams(has_side_effects=SIDE_EFFECT),
    )(*[pltpu.with_memory_space_constraint(a, pltpu.HBM) for a in arrays + lands], *extra)
    return dict(send=outs[0], recv=outs[1], ins=list(outs[2:2 + n]), lands=list(outs[2 + n:2 + 2 * n]),
                token=outs[-1], kinds=kinds, name=name, peers=peers)


def _exchange_wait(h, after):
    n = len(h["ins"])
    kinds = h["kinds"]

    def body(*refs):
        ins, lnd = refs[:n], refs[n:2 * n]
        send_sems, recv_sems = refs[2 * n], refs[2 * n + 1]
        copies = _exchange_copies(ins, lnd, send_sems, recv_sems, kinds, True, h["peers"])
        for cp in copies:
            cp.wait_recv()
        for cp in copies:
            cp.wait_send()

    arrs = h["ins"] + h["lands"]
    after = list(after) if isinstance(after, (list, tuple)) else [after]
    outs = pl.pallas_call(
        body, name=h["name"] + "_wait", in_specs=[HBM_SPEC] * (2 * n) + [SEM_SPEC, SEM_SPEC] + [ANY_SPEC] * len(after),
        out_specs=[HBM_SPEC] * (2 * n), out_shape=[pltpu.HBM(a.shape, a.dtype) for a in arrs],
        input_output_aliases={i: i for i in range(2 * n)},
        compiler_params=pltpu.CompilerParams(has_side_effects=SIDE_EFFECT),
    )(*arrs, h["send"], h["recv"], *after)
    return list(outs[n:])


SAME_CORE = (0, 2, 4, 6)


def _forward_copies(land, send_sems, recv_sems, incoming):
    me = 4 * lax.axis_index("x") + 2 * lax.axis_index("y") + lax.axis_index("c")
    dev, sibling = _peer(1)
    return [pltpu.make_async_remote_copy(
        src_ref=land.at[me ^ q], dst_ref=land.at[(sibling if incoming else me) ^ q],
        send_sem=send_sems.at[j], recv_sem=recv_sems.at[j], device_id=dev, device_id_type=MESH)
        for j, q in enumerate(SAME_CORE)]


def _sibling_forward_start(land, *, name, dep=None):
    extra = [] if dep is None else [dep]

    def body(*refs):
        land_ref, send_sems, recv_sems, token = refs[0], refs[1 + len(extra)], refs[2 + len(extra)], refs[-1]
        for cp in _forward_copies(land_ref, send_sems, recv_sems, False):
            cp.start()
        token[...] = jnp.zeros_like(token)

    sem = pltpu.SemaphoreType.DMA((len(SAME_CORE),))
    outs = pl.pallas_call(
        body, name=name, in_specs=[HBM_SPEC] + [ANY_SPEC] * len(extra),
        out_specs=[SEM_SPEC, SEM_SPEC, HBM_SPEC, pl.BlockSpec(memory_space=pltpu.VMEM)],
        out_shape=[sem, sem, pltpu.HBM(land.shape, land.dtype), jax.ShapeDtypeStruct((8, 128), F32)],
        input_output_aliases={0: 2}, compiler_params=pltpu.CompilerParams(has_side_effects=SIDE_EFFECT),
    )(pltpu.with_memory_space_constraint(land, pltpu.HBM), *extra)
    return dict(send=outs[0], recv=outs[1], land=outs[2], token=outs[3], name=name)


def _sibling_forward_wait(h, after):
    def body(*refs):
        copies = _forward_copies(refs[0], refs[1], refs[2], True)
        for cp in copies:
            cp.wait_recv()
        for cp in copies:
            cp.wait_send()

    return pl.pallas_call(
        body, name=h["name"] + "_wait", in_specs=[HBM_SPEC, SEM_SPEC, SEM_SPEC, ANY_SPEC], out_specs=HBM_SPEC,
        out_shape=pltpu.HBM(h["land"].shape, h["land"].dtype), input_output_aliases={0: 0},
        compiler_params=pltpu.CompilerParams(has_side_effects=SIDE_EFFECT),
    )(h["land"], h["send"], h["recv"], after)


def _s5_params(lam_re, lam_im, log_dt, b_re, b_im):
    dt = jnp.exp(log_dt)[:, None]
    mag = jnp.exp(lam_re * dt)
    ar, ai = mag * jnp.cos(lam_im * dt), mag * jnp.sin(lam_im * dt)
    den = lam_re * lam_re + lam_im * lam_im
    qr = ((ar - 1.0) * lam_re + ai * lam_im) / den
    qi = (ai * lam_re - (ar - 1.0) * lam_im) / den
    bbr = qr[..., None] * b_re - qi[..., None] * b_im
    bbi = qr[..., None] * b_im + qi[..., None] * b_re
    return ar, ai, bbr, bbi


def _s5_power_table(ar, ai):
    pr, pi = ar.reshape(1, -1), ai.reshape(1, -1)
    while pr.shape[0] < 8:
        sr, si = pr[-1:], pi[-1:]
        pr, pi = (jnp.concatenate([pr, pr * sr - pi * si], axis=0), jnp.concatenate([pi, pr * si + pi * sr], axis=0))
    return pr, pi


def _blockdiag(w, rows, cols):
    w = w.reshape(S5_GB, S5_GB, rows, cols)
    eye = jnp.eye(S5_GB, dtype=w.dtype)
    return jnp.einsum("abrc,bd->abrdc", w, eye).reshape(S5_GB, S5_GB * rows, S5_GB * cols)


def _blockdiag_extract(w, rows, cols):
    w = w.reshape(S5_GB, S5_GB, rows, S5_GB, cols)
    return jnp.einsum("abrbc->abrc", w).reshape(S5_GROUPS, rows, cols)


def _s5_scan_specs(bsz, nc, rev):
    def cc(c):
        return (nc - 1 - c) if rev else c

    return dict(
        u=pl.BlockSpec((bsz, CHUNK, CHUNK), lambda g, c: (0, cc(c), g)),
        x=pl.BlockSpec((bsz, CHUNK, S5_LANES), lambda g, c: (0, cc(c), g)),
        wb=pl.BlockSpec((1, CHUNK, S5_LANES), lambda g, c: (g, 0, 0)),
        wc=pl.BlockSpec((1, S5_LANES, CHUNK), lambda g, c: (g, 0, 0)),
        tab=pl.BlockSpec((8, S5_LANES), lambda g, c: (0, g)),
        step=pl.BlockSpec((8, S5_LANES), lambda g, c: (0, g)),
        d=pl.BlockSpec((1, CHUNK), lambda g, c: (0, g)),
        lane=pl.BlockSpec((1, S5_LANES), lambda g, c: (0, g)),
        xprev=pl.BlockSpec((bsz, 8, S5_LANES), lambda g, c: (0, jnp.maximum(cc(c) * (CHUNK // 8) - 1, 0), g)),
    )


def _s5_fwd(u, wbr, wbi, pr, pi, sr, si, wcr, wci, d, bsz, nc):
    r = u.shape[0]
    tp = r // bsz
    sp = _s5_scan_specs(bsz, nc, False)

    def body(u_all, wbr_ref, wbi_ref, pr_ref, pi_ref, sr_ref, si_ref, wcr_ref, wci_ref, d_ref,
             xr_all, xi_all, y1_all, g_all, cr_sall, ci_sall):
        @pl.when(pl.program_id(1) == 0)
        def _():
            cr_sall[...] = jnp.zeros_like(cr_sall)
            ci_sall[...] = jnp.zeros_like(ci_sall)

        for bi in range(bsz):
            one(u_all.at[bi], wbr_ref, wbi_ref, pr_ref, pi_ref, sr_ref, si_ref, wcr_ref, wci_ref, d_ref,
                xr_all.at[bi], xi_all.at[bi], y1_all.at[bi], g_all.at[bi], cr_sall.at[bi], ci_sall.at[bi])

    def one(u_ref, wbr_ref, wbi_ref, pr_ref, pi_ref, sr_ref, si_ref, wcr_ref, wci_ref, d_ref,
            xr_ref, xi_ref, y1_ref, g_ref, cr_s, ci_s):
        uv = u_ref[...]
        ub = _bf(uv)
        xr, xi = _dot(ub, wbr_ref[0]), _dot(ub, wbi_ref[0])
        sub = lax.broadcasted_iota(jnp.int32, (CHUNK, S5_LANES), 0) % 8
        for k in range(3):
            s = 1 << k
            ar, ai = sr_ref[k:k + 1, :], si_ref[k:k + 1, :]
            hr = jnp.where(sub >= s, pltpu.roll(xr, s, 0), 0.0)
            hi = jnp.where(sub >= s, pltpu.roll(xi, s, 0), 0.0)
            xr, xi = xr + (ar * hr - ai * hi), xi + (ar * hi + ai * hr)
        cr, ci = cr_s[...], ci_s[...]
        tr, ti = pr_ref[...], pi_ref[...]
        outr, outi = [], []
        for g8 in range(CHUNK // 8):
            br, bi = xr[8 * g8:8 * g8 + 8, :], xi[8 * g8:8 * g8 + 8, :]
            br, bi = br + (tr * cr - ti * ci), bi + (tr * ci + ti * cr)
            cr, ci = br[7:8, :], bi[7:8, :]
            outr.append(br)
            outi.append(bi)
        xr, xi = jnp.concatenate(outr, axis=0), jnp.concatenate(outi, axis=0)
        cr_s[...] = cr
        ci_s[...] = ci
        xr_ref[...] = xr
        xi_ref[...] = xi
        y = _dot(_bf(xr), wcr_ref[0]) - _dot(_bf(xi), wci_ref[0]) + d_ref[...] * uv
        y1_ref[...] = y
        g_ref[...] = _bf(_gelu_and_grad(y)[0])

    ns = S5_GROUPS * S5_STATE
    xr, xi, y1, g = _pc(
        body, name="s5_fwd", grid=(S5_GB, nc),
        in_specs=[sp["u"], sp["wb"], sp["wb"], sp["tab"], sp["tab"], sp["step"], sp["step"], sp["wc"], sp["wc"],
                  sp["d"]],
        out_specs=[sp["x"], sp["x"], sp["u"], sp["u"]],
        out_shape=[jax.ShapeDtypeStruct((bsz, tp, ns), F32)] * 2
        + [jax.ShapeDtypeStruct((bsz, tp, S5_WIDTH), F32), jax.ShapeDtypeStruct((bsz, tp, S5_WIDTH), BF16)],
        scratch=[pltpu.VMEM((bsz, 1, S5_LANES), F32)] * 2, vmem=8 << 20,
    )(_seq(u, bsz), wbr, wbi, pr, pi, sr, si, wcr, wci, d)
    return xr.reshape(r, ns), xi.reshape(r, ns), y1.reshape(r, S5_WIDTH), g.reshape(r, S5_WIDTH)


def _s5_post(y1, glu_pre, glu_b, z):
    r, w = y1.shape
    tm = _pick(r, (256, 128))

    def body(y_ref, p_ref, b_ref, z_ref, o_ref):
        g = _gelu_and_grad(y_ref[...])[0]
        o_ref[...] = _bf(g * jax.nn.sigmoid(p_ref[...] + b_ref[...]) * _silu(z_ref[...]))

    row = pl.BlockSpec((tm, w), lambda i: (i, 0))
    return _pc(body, name="s5_post", grid=(r // tm,), in_specs=[row, row, pl.BlockSpec((1, w), lambda i: (0, 0)), row],
               out_specs=row, out_shape=jax.ShapeDtypeStruct((r, w), BF16), vmem=tm * w * 16)(y1, glu_pre, glu_b, z)


def _s5_post_bwd(dya, y1, glu_pre, glu_b, z):
    r, w = y1.shape
    tm = _pick(r, (256, 128))

    def body(dy_ref, y_ref, p_ref, b_ref, z_ref, dz_ref, dp_ref, dg_ref, db_ref):
        @pl.when(pl.program_id(0) == 0)
        def _():
            db_ref[...] = jnp.zeros_like(db_ref)

        g = _gelu_and_grad(y_ref[...])[0]
        s = jax.nn.sigmoid(p_ref[...] + b_ref[...])
        zv = z_ref[...]
        dy = dy_ref[...]
        do = dy * _silu(zv)
        dz_ref[...] = _bf(dy * g * s * _dsilu(zv))
        dp = do * g * s * (1.0 - s)
        dp_ref[...] = _bf(dp)
        db_ref[...] += jnp.sum(dp, axis=0, keepdims=True)
        dg_ref[...] = do * s

    row = pl.BlockSpec((tm, w), lambda i: (i, 0))
    vec = pl.BlockSpec((1, w), lambda i: (0, 0))
    return _pc(body, name="s5_post_bwd", grid=(r // tm,), in_specs=[row, row, row, vec, row],
               out_specs=[row, row, row, vec],
               out_shape=[jax.ShapeDtypeStruct((r, w), BF16), jax.ShapeDtypeStruct((r, w), BF16),
                          jax.ShapeDtypeStruct((r, w), F32), jax.ShapeDtypeStruct((1, w), F32)],
               vmem=tm * w * 24)(dya, y1, glu_pre, glu_b, z)


def _s5_bwd(dg, y1, u, xr, xi, wbr, wbi, qr, qi, sr, si, wcr, wci, d, bsz, nc):
    r = u.shape[0]
    tp = r // bsz
    sp = _s5_scan_specs(bsz, nc, True)

    def body(dg_all, y1_all, u_all, xr_all, xi_all, xpr_all, xpi_all, wbr_ref, wbi_ref, qr_ref, qi_ref, sr_ref, si_ref,
             wcr_ref, wci_ref, d_ref, du_all, dd_ref, dwcr_ref, dwci_ref, dwbr_ref, dwbi_ref, dar_ref, dai_ref,
             cr_sall, ci_sall):
        c = pl.program_id(1)

        @pl.when(c == 0)
        def _():
            for ref in (dd_ref, dwcr_ref, dwci_ref, dwbr_ref, dwbi_ref, dar_ref, dai_ref, cr_sall, ci_sall):
                ref[...] = jnp.zeros_like(ref)

        for bi in range(bsz):
            one(c, dg_all.at[bi], y1_all.at[bi], u_all.at[bi], xr_all.at[bi], xi_all.at[bi], xpr_all.at[bi],
                xpi_all.at[bi], wbr_ref, wbi_ref, qr_ref, qi_ref, sr_ref, si_ref, wcr_ref, wci_ref, d_ref,
                du_all.at[bi], dd_ref, dwcr_ref, dwci_ref, dwbr_ref, dwbi_ref, dar_ref, dai_ref, cr_sall.at[bi],
                ci_sall.at[bi])

    def one(c, dg_ref, y1_ref, u_ref, xr_ref, xi_ref, xpr_ref, xpi_ref, wbr_ref, wbi_ref, qr_ref, qi_ref, sr_ref, si_ref,
            wcr_ref, wci_ref, d_ref, du_ref, dd_ref, dwcr_ref, dwci_ref, dwbr_ref, dwbi_ref, dar_ref, dai_ref,
            cr_s, ci_s):
        uv = u_ref[...]
        ub = _bf(uv)
        dy = dg_ref[...] * _gelu_and_grad(y1_ref[...])[1]
        dd_ref[...] += jnp.sum(dy * uv, axis=0, keepdims=True)
        dyb = _bf(dy)
        xr, xi = xr_ref[...], xi_ref[...]
        dwcr_ref[0] += _dot(_bf(xr), dyb, TN)
        dwci_ref[0] -= _dot(_bf(xi), dyb, TN)
        lr, li = _dot(dyb, wcr_ref[0], NT), -_dot(dyb, wci_ref[0], NT)
        row = lax.broadcasted_iota(jnp.int32, (CHUNK, S5_LANES), 0)
        sub = row % 8
        for k in range(3):
            s = 1 << k
            ar, ai = sr_ref[k:k + 1, :], si_ref[k:k + 1, :]
            hr = jnp.where(sub < 8 - s, pltpu.roll(lr, CHUNK - s, 0), 0.0)
            hi = jnp.where(sub < 8 - s, pltpu.roll(li, CHUNK - s, 0), 0.0)
            lr, li = lr + (ar * hr + ai * hi), li + (ar * hi - ai * hr)
        cr, ci = cr_s[...], ci_s[...]
        tr, ti = qr_ref[...], qi_ref[...]
        outr, outi = [], []
        for g8 in reversed(range(CHUNK // 8)):
            br, bi = lr[8 * g8:8 * g8 + 8, :], li[8 * g8:8 * g8 + 8, :]
            br, bi = br + (tr * cr + ti * ci), bi + (tr * ci - ti * cr)
            cr, ci = br[0:1, :], bi[0:1, :]
            outr.append(br)
            outi.append(bi)
        lr, li = jnp.concatenate(outr[::-1], axis=0), jnp.concatenate(outi[::-1], axis=0)
        cr_s[...] = cr
        ci_s[...] = ci
        lrb, lib = _bf(lr), _bf(li)
        du_ref[...] = _bf(_dot(lrb, wbr_ref[0], NT) + _dot(lib, wbi_ref[0], NT) + dy * d_ref[...])
        dwbr_ref[0] += _dot(ub, lrb, TN)
        dwbi_ref[0] += _dot(ub, lib, TN)
        first = c == nc - 1
        pr0 = jnp.where(first, 0.0, xpr_ref[7:8, :])
        pi0 = jnp.where(first, 0.0, xpi_ref[7:8, :])
        xpr = jnp.where(row == 0, pr0, pltpu.roll(xr, 1, 0))
        xpi = jnp.where(row == 0, pi0, pltpu.roll(xi, 1, 0))
        dar_ref[...] += jnp.sum(lr * xpr + li * xpi, axis=0, keepdims=True)
        dai_ref[...] += jnp.sum(li * xpr - lr * xpi, axis=0, keepdims=True)

    st = jax.ShapeDtypeStruct
    xr3, xi3 = _seq(xr, bsz), _seq(xi, bsz)
    outs = _pc(body, name="s5_bwd", grid=(S5_GB, nc),
               in_specs=[sp["u"], sp["u"], sp["u"], sp["x"], sp["x"], sp["xprev"], sp["xprev"], sp["wb"], sp["wb"],
                         sp["tab"], sp["tab"], sp["step"], sp["step"], sp["wc"], sp["wc"], sp["d"]],
               out_specs=[sp["u"], sp["d"], sp["wc"], sp["wc"], sp["wb"], sp["wb"], sp["lane"], sp["lane"]],
               out_shape=[st((bsz, tp, S5_WIDTH), BF16), st((1, S5_WIDTH), F32),
                          st((S5_GB, S5_LANES, CHUNK), F32), st((S5_GB, S5_LANES, CHUNK), F32),
                          st((S5_GB, CHUNK, S5_LANES), F32), st((S5_GB, CHUNK, S5_LANES), F32),
                          st((1, S5_GROUPS * S5_STATE), F32), st((1, S5_GROUPS * S5_STATE), F32)],
               scratch=[pltpu.VMEM((bsz, 1, S5_LANES), F32)] * 2, vmem=12 << 20,
               )(_seq(dg, bsz), _seq(y1, bsz), _seq(u, bsz), xr3, xi3, xr3, xi3, wbr, wbi, qr, qi, sr, si, wcr, wci, d)
    return (outs[0].reshape(r, S5_WIDTH),) + tuple(outs[1:])


def _s5_layer_fwd(u, prm, glu_w, bsz, nc):
    xr, xi, y1, g = _s5_fwd(u, prm["wbr"], prm["wbi"], prm["pr"], prm["pi"], prm["sr"], prm["si"], prm["wcr"],
                            prm["wci"], prm["d"], bsz, nc)
    glu_pre = _mm(g, glu_w(y1) if callable(glu_w) else glu_w, "NN", name="s5_glu")
    return dict(xr=xr, xi=xi, y1=y1, g=g, glu_pre=glu_pre)


def _s5_layer_bwd(dya, u, z, sv, prm, pvjp, glu_w, glu_b, bsz, nc):
    dz, dglu, dg_direct, dglu_b = _s5_post_bwd(dya, sv["y1"], sv["glu_pre"], glu_b, z)
    dg = _mm(dglu, glu_w, "NT", name="s5_dg", add=dg_direct)
    dglu_w = _mm(sv["g"], dglu, "TN", name="s5_dglu_w")
    du, dd, dwcr, dwci, dwbr, dwbi, dar, dai = _s5_bwd(
        dg, sv["y1"], u, sv["xr"], sv["xi"], prm["wbr"], prm["wbi"], prm["qr"], prm["qi"], prm["sr"], prm["si"],
        prm["wcr"], prm["wci"], prm["d"], bsz, nc)
    dbbr = jnp.swapaxes(_blockdiag_extract(dwbr, S5_GROUP_SIZE, S5_STATE), 1, 2)
    dbbi = jnp.swapaxes(_blockdiag_extract(dwbi, S5_GROUP_SIZE, S5_STATE), 1, 2)
    dlr, dli, dldt, dbr, dbi = pvjp((dar.reshape(S5_GROUPS, S5_STATE), dai.reshape(S5_GROUPS, S5_STATE), dbbr, dbbi))
    grads = dict(
        s5_lambda_re=dlr, s5_lambda_im=dli, s5_log_dt=dldt, s5_b_re=dbr, s5_b_im=dbi,
        s5_c_re=jnp.swapaxes(_blockdiag_extract(dwcr, S5_STATE, S5_GROUP_SIZE), 1, 2),
        s5_c_im=jnp.swapaxes(_blockdiag_extract(dwci, S5_STATE, S5_GROUP_SIZE), 1, 2),
        s5_d=dd, s5_glu_w=dglu_w, s5_glu_b=dglu_b)
    return du, dz, grads


def _s5_tables(lam_re, lam_im, log_dt, b_re, b_im, c_re, c_im, d):
    (ar, ai, bbr, bbi), vjp = jax.vjp(_s5_params, lam_re, lam_im, log_dt, b_re, b_im)
    pr, pi = _s5_power_table(lax.stop_gradient(ar), lax.stop_gradient(ai))
    steps = [0, 1, 3, 7, 7, 7, 7, 7]
    flip8 = (jnp.arange(8)[:, None] + jnp.arange(8)[None, :] == 7).astype(F32)
    prm = dict(
        wbr=_bf(_blockdiag(jnp.swapaxes(bbr, 1, 2), S5_GROUP_SIZE, S5_STATE)),
        wbi=_bf(_blockdiag(jnp.swapaxes(bbi, 1, 2), S5_GROUP_SIZE, S5_STATE)),
        wcr=_bf(_blockdiag(jnp.swapaxes(c_re, 1, 2), S5_STATE, S5_GROUP_SIZE)),
        wci=_bf(_blockdiag(jnp.swapaxes(c_im, 1, 2), S5_STATE, S5_GROUP_SIZE)),
        pr=pr, pi=pi, qr=jnp.dot(flip8, pr, precision=lax.Precision.HIGHEST),
        qi=jnp.dot(flip8, pi, precision=lax.Precision.HIGHEST),
        sr=jnp.concatenate([pr[i:i + 1] for i in steps], axis=0),
        si=jnp.concatenate([pi[i:i + 1] for i in steps], axis=0), d=d.reshape(1, S5_WIDTH))
    return prm, vjp


def _shift_down(x, halo8, s):
    sh = pltpu.roll(x, s, 0)
    r8 = lax.broadcasted_iota(jnp.int32, halo8.shape, 0)
    first = jnp.where(r8 < s, pltpu.roll(halo8, s, 0), sh[:8])
    return jnp.concatenate([first, sh[8:]], axis=0)


def _shift_up(x, halo8, s):
    sh = pltpu.roll(x, CHUNK - s, 0)
    r8 = lax.broadcasted_iota(jnp.int32, halo8.shape, 0)
    last = jnp.where(r8 >= 8 - s, pltpu.roll(halo8, 8 - s, 0), sh[CHUNK - 8:])
    return jnp.concatenate([sh[:CHUNK - 8], last], axis=0)


def _conv_specs(nc, tw):
    def chunk(b, c):
        return b * nc + c

    return dict(
        x=pl.BlockSpec((CHUNK, tw), lambda j, b, c: (chunk(b, c), j)),
        prev=pl.BlockSpec((8, tw), lambda j, b, c: (jnp.maximum(chunk(b, c) * (CHUNK // 8) - 1, 0), j)),
        nxt=pl.BlockSpec((8, tw), lambda j, b, c: ((b * nc + jnp.minimum(c + 1, nc - 1)) * (CHUNK // 8), j)),
        w=pl.BlockSpec((ML_CONV, tw), lambda j, b, c: (0, j)),
        vec=pl.BlockSpec((1, tw), lambda j, b, c: (0, j)),
    )


def _conv_fwd(x, w, bias, bsz, nc, *, name):
    r, wd = x.shape
    tw = _pick(wd, (2048, 1536, 1024, 512, 384, 256, 128))
    sp = _conv_specs(nc, tw)

    def body(x_ref, p_ref, w_ref, b_ref, o_ref):
        c = pl.program_id(2)
        xv = x_ref[...]
        halo = jnp.where(c == 0, 0.0, p_ref[...])
        acc = b_ref[...] + w_ref[3:4, :] * xv
        for s in (1, 2, 3):
            acc = acc + w_ref[3 - s:4 - s, :] * _shift_down(xv, halo, s)
        o_ref[...] = acc

    return _pc(body, name=name, grid=(wd // tw, bsz, nc), in_specs=[sp["x"], sp["prev"], sp["w"], sp["vec"]],
               out_specs=sp["x"], out_shape=jax.ShapeDtypeStruct((r, wd), F32), vmem=CHUNK * tw * 16,
               )(x, x, w, bias.reshape(1, wd))


def _conv_bwd(dpre, x, w, bsz, nc, *, name, add=None):
    r, wd = x.shape
    tw = _pick(wd, (2048, 1536, 1024, 512, 384, 256, 128))
    sp = _conv_specs(nc, tw)

    def body(*refs):
        d_ref, n_ref, x_ref, p_ref, w_ref = refs[:5]
        add_ref = refs[5] if add is not None else None
        dx_ref, dw_ref, db_ref = refs[-3:]
        b, c = pl.program_id(1), pl.program_id(2)

        @pl.when((b == 0) & (c == 0))
        def _():
            dw_ref[...] = jnp.zeros_like(dw_ref)
            db_ref[...] = jnp.zeros_like(db_ref)

        dv, xv = d_ref[...], x_ref[...]
        dhalo = jnp.where(c == nc - 1, 0.0, n_ref[...])
        xhalo = jnp.where(c == 0, 0.0, p_ref[...])
        dx = w_ref[3:4, :] * dv
        for s in (1, 2, 3):
            dx = dx + w_ref[3 - s:4 - s, :] * _shift_up(dv, dhalo, s)
        if add_ref is not None:
            dx = dx + add_ref[...]
        dx_ref[...] = _bf(dx)
        db_ref[...] += jnp.sum(dv, axis=0, keepdims=True)
        dw_ref[3:4, :] += jnp.sum(dv * xv, axis=0, keepdims=True)
        for s in (1, 2, 3):
            dw_ref[3 - s:4 - s, :] += jnp.sum(dv * _shift_down(xv, xhalo, s), axis=0, keepdims=True)

    ins = [dpre, dpre, x, x, w] + ([add] if add is not None else [])
    specs = [sp["x"], sp["nxt"], sp["x"], sp["prev"], sp["w"]] + ([sp["x"]] if add is not None else [])
    return _pc(body, name=name, grid=(wd // tw, bsz, nc), in_specs=specs, out_specs=[sp["x"], sp["w"], sp["vec"]],
               out_shape=[jax.ShapeDtypeStruct((r, wd), BF16), jax.ShapeDtypeStruct((ML_CONV, wd), F32),
                          jax.ShapeDtypeStruct((1, wd), F32)], vmem=CHUNK * tw * 24)(*ins)


ML_SCALE = ML_DH ** -0.5


ML_LB = ML_DH // CHUNK


def _headwise_expand(w):
    tiled = jnp.tile(w.reshape(ML_HEADS, ML_DH, QKV_BLOCK), (1, 1, CHUNK // QKV_BLOCK))
    rblk = (jnp.arange(ML_DH) % CHUNK) // QKV_BLOCK
    cblk = jnp.arange(CHUNK) // QKV_BLOCK
    return jnp.where(rblk[:, None] == cblk[None, :], tiled, 0.0).reshape(ML_HEADS, ML_LB, CHUNK, CHUNK)


def _headwise_dot(x, w_ref, dims=NN):
    return jnp.concatenate([_dot(x[:, j * CHUNK:(j + 1) * CHUNK], w_ref[0, j], dims) for j in range(ML_LB)], axis=1)


def _headwise_extract(w):
    return w[:, :, :QKV_BLOCK].reshape(ML_HEADS * ML_DH // QKV_BLOCK, QKV_BLOCK, QKV_BLOCK)


def _ml_pre(pre, x, wq, wk, wv, wgq, wgk, wgv, bsz, nc):
    r = x.shape[0]
    tr = _pick(r, (256, 128))
    hrow = pl.BlockSpec((tr, ML_DH), lambda h, i: (i, h))
    wexp = pl.BlockSpec((1, ML_LB, CHUNK, CHUNK), lambda h, i: (h, 0, 0, 0))
    wg = pl.BlockSpec((ML_DH, CHUNK), lambda h, i: (h, 0))

    def body(pre_ref, x_ref, wq_ref, wk_ref, wv_ref, gq_ref, gk_ref, gv_ref, qs_ref, k_ref, v_ref, gt_ref):
        xcb = _bf(_silu(pre_ref[...]))
        q = _headwise_dot(xcb, wq_ref)
        k = _headwise_dot(xcb, wk_ref)
        v = _headwise_dot(_bf(x_ref[...]), wv_ref)
        qb, kb, vb = _bf(q), _bf(k), _bf(v)
        qs_ref[...] = _bf(q * ML_SCALE)
        k_ref[...] = kb
        v_ref[...] = vb
        gt_ref[0] = _dot(qb, gq_ref[...]) + _dot(kb, gk_ref[...]) + _dot(vb, gv_ref[...])

    o = jax.ShapeDtypeStruct((r, ML_WIDTH), BF16)
    qs, k, v, gates8 = _pc(
        body, name="ml_pre", grid=(ML_HEADS, r // tr),
        in_specs=[hrow, hrow, wexp, wexp, wexp, wg, wg, wg],
        out_specs=[hrow, hrow, hrow, pl.BlockSpec((1, tr, CHUNK), lambda h, i: (h, i, 0))],
        out_shape=[o, o, o, jax.ShapeDtypeStruct((ML_HEADS, r, CHUNK), F32)], vmem=6 << 20,
    )(pre, x, wq, wk, wv, wgq, wgk, wgv)

    return qs, k, v, _sum_heads(gates8, name="ml_gates_sum")


def _sum_heads(g8, *, name):
    r = g8.shape[1]
    tr = _pick(r, (256, 128))

    def body(g_ref, o_ref):
        acc = g_ref[0]
        for j in range(1, ML_HEADS):
            acc = acc + g_ref[j]
        o_ref[...] = acc

    return _pc(body, name=name, grid=(r // tr,),
               in_specs=[pl.BlockSpec((ML_HEADS, tr, CHUNK), lambda i: (0, i, 0))],
               out_specs=pl.BlockSpec((tr, CHUNK), lambda i: (i, 0)),
               out_shape=jax.ShapeDtypeStruct((r, CHUNK), F32), vmem=2 << 20)(g8)


def _tri(rev):
    r = lax.broadcasted_iota(jnp.int32, (CHUNK, CHUNK), 0)
    c = lax.broadcasted_iota(jnp.int32, (CHUNK, CHUNK), 1)
    return jnp.where((c >= r) if rev else (c <= r), 1.0, 0.0).astype(F32)


def _cumsum_rows(x, row, rev=False):
    for k in range(7):
        s = 1 << k
        if rev:
            x = x + jnp.where(row < CHUNK - s, pltpu.roll(x, CHUNK - s, 0), 0.0)
        else:
            x = x + jnp.where(row >= s, pltpu.roll(x, s, 0), 0.0)
    return x


def _log_sigmoid(x):
    return jnp.minimum(x, 0.0) - jnp.log(1.0 + jnp.exp(-jnp.abs(x)))


def _ml_core(gates, hd, first, m, qs, k, v, cmat, nvec):
    sq = (CHUNK, CHUNK)
    lane = lax.broadcasted_iota(jnp.int32, sq, 1)
    row = lax.broadcasted_iota(jnp.int32, sq, 0)
    igc = jnp.sum(jnp.where(lane == hd, gates, 0.0), axis=1, keepdims=True)
    fpc = jnp.sum(jnp.where(lane == hd + ML_HEADS, gates, 0.0), axis=1, keepdims=True)
    valid = jnp.logical_or(jnp.logical_not(first), row[:, :1] >= PAD_ROWS)
    igc = jnp.where(valid, igc, NEG)
    lfc = jnp.where(valid, _log_sigmoid(fpc), 0.0)
    bcb = _cumsum_rows(jnp.broadcast_to(lfc, sq), row)
    igb = jnp.broadcast_to(igc, sq)
    dm = jnp.where(lane <= row, bcb - (bcb - igb).T, NEG)
    bc = bcb[:, :1]
    inter = bc + m
    mt = jnp.maximum(inter, jnp.max(dm, axis=1, keepdims=True))
    wt = jnp.exp(dm - mt)
    wprev = jnp.exp(inter - mt)
    s0 = _dot(qs, k, NT)
    s = s0 * wt
    cb = _bf(cmat)
    qc = _dot(qs, cb)
    qf = qs.astype(F32)
    qn = jnp.sum(qf * nvec, axis=1, keepdims=True)
    num = _dot(_bf(s), v) + wprev * qc
    den = jnp.sum(s, axis=1, keepdims=True) + wprev * qn
    emt = jnp.exp(-mt)
    dd = jnp.maximum(jnp.abs(den), emt)
    blast = bcb[CHUNK - 1:CHUNK, :1]
    g = blast - bc + igc
    m_new = jnp.maximum(blast + m, jnp.max(g, axis=0, keepdims=True))
    decay = jnp.exp(blast + m - m_new)
    e = jnp.exp(g - m_new)
    kf = k.astype(F32)
    wk = e * kf
    return dict(lane=lane, row=row, fpc=fpc, valid=valid, wt=wt, wprev=wprev, s=s, cb=cb, qc=qc, qf=qf, qn=qn,
                num=num, den=den, emt=emt, dd=dd, m_new=m_new, decay=decay, e=e, kf=kf, wk=wk)


def _ml_headnorm(h):
    mu = jnp.mean(h, axis=1, keepdims=True)
    hc = h - mu
    rstd = lax.rsqrt(jnp.mean(hc * hc, axis=1, keepdims=True) + HEAD_NORM_EPS)
    return hc * rstd, rstd


def _ml_chunk_specs(nc, rev, bsz):
    def cc(c):
        return (nc - 1 - c) if rev else c

    return dict(
        hrow=pl.BlockSpec((bsz, CHUNK, ML_DH), lambda hd, c: (0, cc(c), hd)),
        gates=pl.BlockSpec((bsz, CHUNK, CHUNK), lambda hd, c: (0, cc(c), 0)),
        bias=pl.BlockSpec((1, CHUNK), lambda hd, c: (0, 0)),
        hvec=pl.BlockSpec((1, ML_DH), lambda hd, c: (0, hd)),
        cs=pl.BlockSpec((bsz, 1, ML_DH, ML_DH), lambda hd, c: (0, hd * nc + cc(c), 0, 0)),
        ns=pl.BlockSpec((bsz, 1, 1, ML_DH), lambda hd, c: (0, hd * nc + cc(c), 0, 0)),
        ms=pl.BlockSpec((bsz, 1, 1, CHUNK), lambda hd, c: (0, hd * nc + cc(c), 0, 0)),
        dgates=pl.BlockSpec((1, bsz, CHUNK, CHUNK), lambda hd, c: (hd, 0, cc(c), 0)),
    )


def _seq(a, bsz):
    return a.reshape(bsz, a.shape[0] // bsz, a.shape[1])


def _ml_chunk_fwd(qs, k, v, gates, b_gate, pre, z, nw, sk, bsz, nc):
    r = qs.shape[0]
    tp = r // bsz
    sp = _ml_chunk_specs(nc, False, bsz)

    def body(qs_all, k_all, v_all, gt_all, bg_ref, pre_all, z_all, nw_ref, sk_ref,
             h_all, yb_all, cs_all, ns_all, ms_all, c_sall, n_sall, m_sall):
        hd, c = pl.program_id(0), pl.program_id(1)

        @pl.when(c == 0)
        def _():
            c_sall[...] = jnp.zeros_like(c_sall)
            n_sall[...] = jnp.zeros_like(n_sall)
            m_sall[...] = jnp.zeros_like(m_sall)

        for bi in range(bsz):
            one(hd, c, qs_all.at[bi], k_all.at[bi], v_all.at[bi], gt_all.at[bi], bg_ref, pre_all.at[bi], z_all.at[bi],
                nw_ref, sk_ref, h_all.at[bi], yb_all.at[bi], cs_all.at[bi], ns_all.at[bi], ms_all.at[bi],
                c_sall.at[bi], n_sall.at[bi], m_sall.at[bi])

    def one(hd, c, qs_ref, k_ref, v_ref, gt_ref, bg_ref, pre_ref, z_ref, nw_ref, sk_ref,
            h_ref, yb_ref, cs_ref, ns_ref, ms_ref, c_s, n_s, m_s):
        cmat, nvec, m = c_s[...], n_s[...], m_s[...]
        cs_ref[0] = cmat
        ns_ref[0] = nvec
        ms_ref[0] = jnp.broadcast_to(m, (1, CHUNK))
        v_ = v_ref[...]
        co = _ml_core(gt_ref[...] + bg_ref[...], hd, c == 0, m, qs_ref[...], k_ref[...], v_, cmat, nvec)
        h = co["num"] / co["dd"]
        h_ref[...] = h
        hn, _ = _ml_headnorm(h)
        yb_ref[...] = _bf((hn * nw_ref[...] + sk_ref[...] * _silu(pre_ref[...])) * _silu(z_ref[...]))
        c_s[...] = co["decay"] * cmat + _dot(_bf(co["wk"]), v_, TN)
        n_s[...] = co["decay"] * nvec + jnp.sum(co["wk"], axis=0, keepdims=True)
        m_s[...] = co["m_new"]

    nst = ML_HEADS * nc
    h, yb, cs, ns, ms = _pc(
        body, name="ml_chunk_fwd", grid=(ML_HEADS, nc),
        in_specs=[sp["hrow"]] * 3 + [sp["gates"], sp["bias"], sp["hrow"], sp["hrow"], sp["hvec"], sp["hvec"]],
        out_specs=[sp["hrow"], sp["hrow"], sp["cs"], sp["ns"], sp["ms"]],
        out_shape=[jax.ShapeDtypeStruct((bsz, tp, ML_WIDTH), F32), jax.ShapeDtypeStruct((bsz, tp, ML_WIDTH), BF16),
                   jax.ShapeDtypeStruct((bsz, nst, ML_DH, ML_DH), F32),
                   jax.ShapeDtypeStruct((bsz, nst, 1, ML_DH), F32), jax.ShapeDtypeStruct((bsz, nst, 1, CHUNK), F32)],
        scratch=[pltpu.VMEM((bsz, ML_DH, ML_DH), F32), pltpu.VMEM((bsz, 1, ML_DH), F32),
                 pltpu.VMEM((bsz, 1, 1), F32)],
        vmem=12 << 20)(*[_seq(a, bsz) for a in (qs, k, v, gates)], b_gate, _seq(pre, bsz), _seq(z, bsz), nw, sk)
    return h.reshape(r, ML_WIDTH), yb.reshape(r, ML_WIDTH), cs, ns, ms


def _ml_chunk_bwd(dyb, qs, k, v, gates, b_gate, pre, z, nw, sk, h, cs, ns, ms, bsz, nc, dep=None):
    r = qs.shape[0]
    tp = r // bsz
    sp = _ml_chunk_specs(nc, True, bsz)

    def body(dy_all, qs_all, k_all, v_all, gt_all, bg_ref, pre_all, z_all, nw_ref, sk_ref, h_all, cs_all, ns_all,
             ms_all, dq_all, dk_all, dv_all, dz_all, dxc_all, dgt_all, dnw_ref, dsk_ref, dc_sall, dn_sall):
        hd, c = pl.program_id(0), pl.program_id(1)

        @pl.when(c == 0)
        def _():
            for ref in (dnw_ref, dsk_ref, dc_sall, dn_sall):
                ref[...] = jnp.zeros_like(ref)

        for bi in range(bsz):
            one(hd, c, dy_all.at[bi], qs_all.at[bi], k_all.at[bi], v_all.at[bi], gt_all.at[bi], bg_ref,
                pre_all.at[bi], z_all.at[bi], nw_ref, sk_ref, h_all.at[bi], cs_all.at[bi], ns_all.at[bi],
                ms_all.at[bi], dq_all.at[bi], dk_all.at[bi], dv_all.at[bi], dz_all.at[bi], dxc_all.at[bi],
                dgt_all.at[0, bi], dnw_ref, dsk_ref, dc_sall.at[bi], dn_sall.at[bi])

    def one(hd, c, dy_ref, qs_ref, k_ref, v_ref, gt_ref, bg_ref, pre_ref, z_ref, nw_ref, sk_ref, h_ref, cs_ref, ns_ref,
            ms_ref, dq_ref, dk_ref, dv_ref, dz_ref, dxc_ref, dgt_ref, dnw_ref, dsk_ref, dc_s, dn_s):

        qs, k, v = qs_ref[...], k_ref[...], v_ref[...]
        cmat, nvec, m = cs_ref[0], ns_ref[0], ms_ref[0][:, :1]
        co = _ml_core(gt_ref[...] + bg_ref[...], hd, c == nc - 1, m, qs, k, v, cmat, nvec)
        lane, row = co["lane"], co["row"]
        wt, wprev, s, cb, qf = co["wt"], co["wprev"], co["s"], co["cb"], co["qf"]
        h = h_ref[...]
        hn, rstd = _ml_headnorm(h)
        xc = _silu(pre_ref[...])
        zv = z_ref[...]
        nw, sk = nw_ref[...], sk_ref[...]
        dy = dy_ref[...]
        dz_ref[...] = _bf(dy * (hn * nw + sk * xc) * _dsilu(zv))
        do = dy * _silu(zv)
        dsk_ref[...] += jnp.sum(do * xc, axis=0, keepdims=True)
        dnw_ref[...] += jnp.sum(do * hn, axis=0, keepdims=True)
        dxc_ref[...] = do * sk
        dhn = do * nw
        dh = rstd * (dhn - jnp.mean(dhn, axis=1, keepdims=True) - hn * jnp.mean(dhn * hn, axis=1, keepdims=True))
        rinv = 1.0 / co["dd"]
        dnum = dh * rinv
        ddd = -jnp.sum(dh * h, axis=1, keepdims=True) * rinv
        den = co["den"]
        dden = jnp.where(jnp.abs(den) >= co["emt"], ddd * jnp.sign(den), 0.0)
        dnb = _bf(dnum)
        ds = _dot(dnb, v, NT) + dden
        dv = _dot(_bf(s), dnb, TN)
        dnw_ = _bf(dnum * wprev)
        dwn = dden * wprev
        dqs = _dot(dnw_, cb, NT) + dwn * nvec
        dc_out = _dot(qs, dnw_, TN)
        dn_out = jnp.sum(dwn * qf, axis=0, keepdims=True)
        dwprev = jnp.sum(dnum * co["qc"], axis=1, keepdims=True) + dden * co["qn"]
        ds0 = _bf(ds * wt)
        ddm = ds * s
        dqs = dqs + _dot(ds0, k)
        dk = _dot(ds0, qs, TN)
        colc = jnp.sum(ddm.T, axis=1, keepdims=True)
        dbc = dwprev * wprev + jnp.sum(ddm, axis=1, keepdims=True) - colc
        dig = colc
        dcn, dnn = dc_s[...], dn_s[...]
        dcb = _bf(dcn)
        decay, e, kf, wk = co["decay"], co["e"], co["kf"], co["wk"]
        ddecay = (jnp.sum(jnp.sum(dcn * cmat, axis=1, keepdims=True), axis=0, keepdims=True)
                  + jnp.sum(dnn * nvec, axis=1, keepdims=True))
        dwk = _dot(v, dcb, NT) + dnn
        dv = dv + _dot(_bf(wk), dcb)
        dk = dk + e * dwk
        dg = jnp.sum(dwk * kf, axis=1, keepdims=True) * e
        dblast = ddecay * decay + jnp.sum(dg, axis=0, keepdims=True)
        dbc = dbc - dg + jnp.where(row[:, :1] == CHUNK - 1, dblast, 0.0)
        dig = dig + dg
        dc_s[...] = decay * dcn + dc_out
        dn_s[...] = decay * dnn + dn_out
        dlf = _cumsum_rows(jnp.broadcast_to(dbc, (CHUNK, CHUNK)), row, rev=True)[:, :1]
        dfp = dlf * (1.0 - jax.nn.sigmoid(co["fpc"]))
        dig = jnp.where(co["valid"], dig, 0.0)
        dfp = jnp.where(co["valid"], dfp, 0.0)
        dgt_ref[...] = jnp.where(lane == hd, dig, 0.0) + jnp.where(lane == hd + ML_HEADS, dfp, 0.0)
        dq_ref[...] = _bf(dqs * ML_SCALE)
        dk_ref[...] = _bf(dk)
        dv_ref[...] = _bf(dv)

    ob = jax.ShapeDtypeStruct((bsz, tp, ML_WIDTH), BF16)
    dq, dk, dv, dz, dxc, dgt, dnw, dsk = _pc(
        body, name="ml_chunk_bwd", grid=(ML_HEADS, nc),
        in_specs=[sp["hrow"]] * 4 + [sp["gates"], sp["bias"], sp["hrow"], sp["hrow"], sp["hvec"], sp["hvec"],
                                     sp["hrow"], sp["cs"], sp["ns"], sp["ms"]],
        out_specs=[sp["hrow"]] * 5 + [sp["dgates"], sp["hvec"], sp["hvec"]],
        out_shape=[ob, ob, ob, ob, jax.ShapeDtypeStruct((bsz, tp, ML_WIDTH), F32),
                   jax.ShapeDtypeStruct((ML_HEADS, bsz, tp, CHUNK), F32),
                   jax.ShapeDtypeStruct((1, ML_WIDTH), F32), jax.ShapeDtypeStruct((1, ML_WIDTH), F32)],
        scratch=[pltpu.VMEM((bsz, ML_DH, ML_DH), F32), pltpu.VMEM((bsz, 1, ML_DH), F32)], vmem=16 << 20, dep=dep,
    )(*[_seq(a, bsz) for a in (dyb, qs, k, v, gates)], b_gate, _seq(pre, bsz), _seq(z, bsz), nw, sk, _seq(h, bsz),
      cs, ns, ms)
    return (dq.reshape(r, ML_WIDTH), dk.reshape(r, ML_WIDTH), dv.reshape(r, ML_WIDTH), dz.reshape(r, ML_WIDTH),
            dxc.reshape(r, ML_WIDTH), dgt.reshape(ML_HEADS, r, CHUNK), dnw, dsk)


def _ml_pre_bwd(dq, dk, dv, dgates, dxc_skip, pre, x, q, k, v, wq, wk, wv, wgq, wgk, wgv, bsz, nc):
    r = x.shape[0]
    tr = _pick(r, (256, 128))
    nt = r // tr
    hrow = pl.BlockSpec((tr, ML_DH), lambda h, i: (i, h))
    wexp = pl.BlockSpec((1, ML_LB, CHUNK, CHUNK), lambda h, i: (h, 0, 0, 0))
    wcmp = pl.BlockSpec((1, ML_DH, CHUNK), lambda h, i: (h, 0, 0))
    wg = pl.BlockSpec((ML_DH, CHUNK), lambda h, i: (h, 0))
    dgs = pl.BlockSpec((tr, CHUNK), lambda h, i: (i, 0))
    bgs = pl.BlockSpec((1, 1, CHUNK), lambda h, i: (h, 0, 0))

    def body(dq_ref, dk_ref, dv_ref, dg_ref, dxs_ref, pre_ref, x_ref, q_ref, k_ref, v_ref, wq_ref, wk_ref, wv_ref,
             gq_ref, gk_ref, gv_ref, dpre_ref, dxv_ref, cq_ref, ck_ref, cv_ref, dgq_ref, dgk_ref, dgv_ref, dbg_ref,
             dwq_ref, dwk_ref, dwv_ref):
        i = pl.program_id(1)

        @pl.when(i == 0)
        def _():
            for ref in (dwq_ref, dwk_ref, dwv_ref, dgq_ref, dgk_ref, dgv_ref, dbg_ref):
                ref[...] = jnp.zeros_like(ref)

        dgt = dg_ref[...]
        dbg_ref[0] += jnp.sum(dgt, axis=0, keepdims=True)
        dgb = _bf(dgt)
        dqt = _bf(dq_ref[...].astype(F32) + _dot(dgb, gq_ref[...], NT))
        dkt = _bf(dk_ref[...].astype(F32) + _dot(dgb, gk_ref[...], NT))
        dvt = _bf(dv_ref[...].astype(F32) + _dot(dgb, gv_ref[...], NT))
        dgq_ref[...] += _dot(q_ref[...], dgb, TN)
        dgk_ref[...] += _dot(k_ref[...], dgb, TN)
        dgv_ref[...] += _dot(v_ref[...], dgb, TN)
        prev = pre_ref[...]
        xcb = _bf(_silu(prev))
        xb = _bf(x_ref[...])
        for j in range(ML_LB):
            sl = slice(j * CHUNK, (j + 1) * CHUNK)
            dwq_ref[j] += _dot(xcb[:, sl], dqt[:, sl], TN)
            dwk_ref[j] += _dot(xcb[:, sl], dkt[:, sl], TN)
            dwv_ref[j] += _dot(xb[:, sl], dvt[:, sl], TN)
        dxc = _headwise_dot(dqt, wq_ref, NT) + _headwise_dot(dkt, wk_ref, NT) + dxs_ref[...]
        dpre_ref[...] = dxc * _dsilu(prev)
        dxv_ref[...] = _headwise_dot(dvt, wv_ref, NT)

        @pl.when(i == nt - 1)
        def _():
            rr = lax.broadcasted_iota(jnp.int32, (CHUNK, CHUNK), 0)
            cc = lax.broadcasted_iota(jnp.int32, (CHUNK, CHUNK), 1)
            diag = rr // QKV_BLOCK == cc // QKV_BLOCK
            fold = jnp.where(rr % QKV_BLOCK == cc, 1.0, 0.0).astype(F32)
            for src, dst in ((dwq_ref, cq_ref), (dwk_ref, ck_ref), (dwv_ref, cv_ref)):
                for j in range(ML_LB):
                    dst[0, j * CHUNK:(j + 1) * CHUNK, :] = jnp.dot(
                        jnp.where(diag, src[j], 0.0), fold, precision=HI, preferred_element_type=F32)

    f = jax.ShapeDtypeStruct((r, ML_WIDTH), F32)
    wc = jax.ShapeDtypeStruct((ML_HEADS, ML_DH, CHUNK), F32)
    wgs = jax.ShapeDtypeStruct((ML_WIDTH, CHUNK), F32)
    return _pc(body, name="ml_pre_bwd", grid=(ML_HEADS, nt),
               in_specs=[hrow, hrow, hrow, dgs, hrow, hrow, hrow, hrow, hrow, hrow, wexp, wexp, wexp, wg, wg, wg],
               out_specs=[hrow, hrow, wcmp, wcmp, wcmp, wg, wg, wg, bgs],
               out_shape=[f, f, wc, wc, wc, wgs, wgs, wgs, jax.ShapeDtypeStruct((ML_HEADS, 1, CHUNK), F32)],
               scratch=[pltpu.VMEM((ML_LB, CHUNK, CHUNK), F32)] * 3,
               vmem=8 << 20)(dq, dk, dv, dgates, dxc_skip, pre, x, q, k, v, wq, wk, wv, wgq, wgk, wgv)


def _pad_lanes(w):
    return jnp.pad(w, ((0, 0), (0, CHUNK - w.shape[1])))


def _ml_weights(conv_w, conv_b, wq, wk, wv, w_gate, b_gate, norm_w, skip):
    return dict(
        conv_w=conv_w, conv_b=conv_b,
        wq=_bf(_headwise_expand(wq)), wk=_bf(_headwise_expand(wk)), wv=_bf(_headwise_expand(wv)),
        wgq=_bf(_pad_lanes(w_gate[:ML_WIDTH])), wgk=_bf(_pad_lanes(w_gate[ML_WIDTH:2 * ML_WIDTH])),
        wgv=_bf(_pad_lanes(w_gate[2 * ML_WIDTH:])), b_gate=_pad_lanes(b_gate.reshape(1, -1)),
        norm=norm_w.reshape(1, ML_WIDTH), skip=skip.reshape(1, ML_WIDTH))


def _ml_layer_fwd(x, z, w, bsz, nc):
    pre = _conv_fwd(x, w["conv_w"], w["conv_b"], bsz, nc, name="ml_conv")
    qs, k, v, gates = _ml_pre(pre, x, w["wq"], w["wk"], w["wv"], w["wgq"], w["wgk"], w["wgv"], bsz, nc)
    h, yb, cs, ns, ms = _ml_chunk_fwd(qs, k, v, gates, w["b_gate"], pre, z, w["norm"], w["skip"], bsz, nc)
    return yb, dict(pre=pre, qs=qs, k=k, v=v, gates=gates, h=h, cs=cs, ns=ns, ms=ms)


def _ml_layer_bwd(dyb, x, z, sv, w, bsz, nc, dep=None):
    dq, dk, dv, dz, dxc, dgates, dnw, dsk = _ml_chunk_bwd(
        dyb, sv["qs"], sv["k"], sv["v"], sv["gates"], w["b_gate"], sv["pre"], z, w["norm"], w["skip"], sv["h"],
        sv["cs"], sv["ns"], sv["ms"], bsz, nc, dep=dep)
    dpre, dxv, dwq, dwk, dwv, dgq, dgk, dgv, dbg = _ml_pre_bwd(
        dq, dk, dv, _sum_heads(dgates, name="ml_dgates_sum"), dxc, sv["pre"], x, sv["qs"], sv["k"], sv["v"], w["wq"],
        w["wk"], w["wv"], w["wgq"], w["wgk"], w["wgv"], bsz, nc)
    dx, dcw, dcb = _conv_bwd(dpre, x, w["conv_w"], bsz, nc, name="ml_conv_bwd", add=dxv)
    ng = 2 * ML_HEADS
    grads = dict(
        ml_conv_w=dcw, ml_conv_b=dcb, ml_wq=_headwise_extract(dwq), ml_wk=_headwise_extract(dwk),
        ml_wv=_headwise_extract(dwv),
        ml_w_gate=jnp.concatenate([dgq[:, :ng] * (1.0 / ML_SCALE), dgk[:, :ng], dgv[:, :ng]], axis=0),
        ml_b_gate=dbg[0][:, :ng], ml_norm=dnw, ml_skip=dsk)
    return dx, dz, grads


HI = lax.Precision.HIGHEST


def _softplus(x):
    return jnp.maximum(x, 0.0) + jnp.log(1.0 + jnp.exp(-jnp.abs(x)))


def _lane_cumsum(x, lane, rev=False):
    del lane
    return _dot_terms(x, _tri(not rev), NN, exact_rhs=True, terms=3)


def _dot_terms(lhs, rhs, dims, *, exact_rhs, terms):
    x = lhs if exact_rhs else rhs
    sel = _bf(rhs if exact_rhs else lhs)
    acc = None
    for _ in range(terms):
        piece = _bf(x)
        part = _dot(piece, sel, dims) if exact_rhs else _dot(sel, piece, dims)
        acc = part if acc is None else acc + part
        x = x - piece.astype(F32)
    return acc


def _head_sum_matrix():
    r = lax.broadcasted_iota(jnp.int32, (SSD_HPG, SSD_GW), 0)
    l = lax.broadcasted_iota(jnp.int32, (SSD_HPG, SSD_GW), 1)
    return jnp.where(l // SSD_P == r, 1.0, 0.0).astype(F32)


def _ssd_dt_specs(nc):
    return dict(rows=pl.BlockSpec((1, SSD_HEADS, CHUNK), lambda b, c: (b, 0, c)),
                col=pl.BlockSpec((SSD_HEADS, 1), lambda b, c: (0, 0)),
                acc=pl.BlockSpec((SSD_HEADS, CHUNK), lambda b, c: (0, 0)))


def _ssd_dt_valid(c):
    lane = lax.broadcasted_iota(jnp.int32, (SSD_HEADS, CHUNK), 1)
    return jnp.logical_or(c > 0, lane >= PAD_ROWS)


def _ssd_dt_prep(dt_raw, dt_bias, a_log, bsz, nc):
    sp = _ssd_dt_specs(nc)

    def body(raw_ref, b_ref, al_ref, dt_ref, cum_ref):
        dt = jnp.where(_ssd_dt_valid(pl.program_id(1)), _softplus(raw_ref[0] + b_ref[...]), 0.0)
        dt_ref[0] = dt
        cum_ref[0] = _lane_cumsum(dt * -jnp.exp(al_ref[...]), None)

    o = jax.ShapeDtypeStruct(dt_raw.shape, F32)
    return _pc(body, name="ssd_dt_prep", grid=(bsz, nc), in_specs=[sp["rows"], sp["col"], sp["col"]],
               out_specs=[sp["rows"], sp["rows"]], out_shape=[o, o], vmem=1 << 20)(dt_raw, dt_bias, a_log)


def _ssd_dt_post(dcum, ddt, dt_raw, dt_bias, a_log, bsz, nc):
    sp = _ssd_dt_specs(nc)

    def body(dcum_ref, ddt_ref, raw_ref, b_ref, al_ref, out_ref, dbias_ref, dal_ref):
        b, c = pl.program_id(0), pl.program_id(1)

        @pl.when((b == 0) & (c == 0))
        def _():
            dbias_ref[...] = jnp.zeros_like(dbias_ref)
            dal_ref[...] = jnp.zeros_like(dal_ref)

        valid = _ssd_dt_valid(c)
        pre = raw_ref[0] + b_ref[...]
        dt = jnp.where(valid, _softplus(pre), 0.0)
        a = -jnp.exp(al_ref[...])
        dda = _lane_cumsum(dcum_ref[0], None, rev=True)
        ddt_raw = jnp.where(valid, ddt_ref[0] + dda * a, 0.0) * jax.nn.sigmoid(pre)
        out_ref[0] = ddt_raw
        dbias_ref[...] += jnp.sum(ddt_raw, axis=1, keepdims=True)
        dal_ref[...] += jnp.sum(dda * dt, axis=1, keepdims=True) * a

    acc = jax.ShapeDtypeStruct((SSD_HEADS, CHUNK), F32)
    return _pc(body, name="ssd_dt_post", grid=(bsz, nc),
               in_specs=[sp["rows"], sp["rows"], sp["rows"], sp["col"], sp["col"]],
               out_specs=[sp["rows"], sp["acc"], sp["acc"]],
               out_shape=[jax.ShapeDtypeStruct(dt_raw.shape, F32), acc, acc], vmem=1 << 20,
               )(dcum, ddt, dt_raw, dt_bias, a_log)


def _ssd_core(xs, bm, cm, dt, cum):
    sq = (CHUNK, CHUNK)
    lane8 = lax.broadcasted_iota(jnp.int32, (SSD_HPG, CHUNK), 1)
    lane = lax.broadcasted_iota(jnp.int32, sq, 1)
    row = lax.broadcasted_iota(jnp.int32, sq, 0)
    low = lane < SSD_P
    cb = _dot(_bf(cm), _bf(bm), NT)
    heads = []
    for r in range(SSD_HPG):
        rowb = jnp.broadcast_to(cum[r:r + 1, :], sq)
        colb = rowb.T
        seg = jnp.exp(jnp.where(lane <= row, colb - rowb, NEG))
        dtrow = jnp.broadcast_to(dt[r:r + 1, :], sq)
        lastb = colb[CHUNK - 1:CHUNK, :]
        heads.append(dict(seg=seg, dtrow=dtrow, w=cb * seg * dtrow, ecol=jnp.exp(colb),
                          dec=jnp.exp(lastb - colb) * dtrow.T, elast=jnp.exp(lastb)))

    def pairs(key):
        return jnp.concatenate([jnp.where(low[:heads[0][key].shape[0]], heads[2 * j][key], heads[2 * j + 1][key])
                                for j in range(SSD_HPG // 2)], axis=1)

    return dict(lane8=lane8, low=low, dt=dt, cum=cum, cb=cb, heads=heads,
                expc=pairs("ecol"), dec=pairs("dec"), elast=pairs("elast"))


def _ssd_specs(nc, rev, bsz):
    def cc(c):
        return (nc - 1 - c) if rev else c

    return dict(
        wide=pl.BlockSpec((bsz, CHUNK, SSD_GW), lambda g, c: (0, cc(c), g)),
        narrow=pl.BlockSpec((bsz, CHUNK, SSD_N), lambda g, c: (0, cc(c), g)),
        dtT=pl.BlockSpec((bsz, SSD_HPG, CHUNK), lambda g, c: (0, g, cc(c))),
        hcol=pl.BlockSpec((SSD_HPG, 1), lambda g, c: (g, 0)),
        hacc=pl.BlockSpec((SSD_HPG, CHUNK), lambda g, c: (g, 0)),
        gvec=pl.BlockSpec((1, SSD_GW), lambda g, c: (0, g)),
        state=pl.BlockSpec((bsz, 1, SSD_N, SSD_GW), lambda g, c: (0, g * nc + cc(c), 0, 0)),
    )


def _ssd_chunk_fwd(xs_pre, bm_pre, cm_pre, dt, cum, d_exp, z, gnorm, bsz, nc):
    tp = xs_pre.shape[1]
    sp = _ssd_specs(nc, False, bsz)

    def body(xs_all, bm_all, cm_all, dt_all, cum_all, d_ref, z_all, gn_ref, y_all, yn_all, st_all, st_sall):
        @pl.when(pl.program_id(1) == 0)
        def _():
            st_sall[...] = jnp.zeros_like(st_sall)

        for bi in range(bsz):
            one(xs_all.at[bi], bm_all.at[bi], cm_all.at[bi], dt_all.at[bi], cum_all.at[bi], d_ref, z_all.at[bi],
                gn_ref, y_all.at[bi], yn_all.at[bi], st_all.at[bi], st_sall.at[bi])

    def one(xs_ref, bm_ref, cm_ref, dt_ref, cum_ref, d_ref, z_ref, gn_ref, y_ref, yn_ref, st_ref, st_s):
        state = st_s[...]
        st_ref[0] = state
        xs, bm, cm = _silu(xs_ref[...]), _silu(bm_ref[...]), _silu(cm_ref[...])
        co = _ssd_core(xs, bm, cm, dt_ref[...], cum_ref[...])
        low, hd = co["low"], co["heads"]
        ys = []
        for j in range(SSD_HPG // 2):
            xp = xs[:, j * CHUNK:(j + 1) * CHUNK]
            lhs = jnp.concatenate([hd[2 * j]["w"], hd[2 * j + 1]["w"]], axis=1)
            rhs = jnp.concatenate([jnp.where(low, xp, 0.0), jnp.where(low, 0.0, xp)], axis=0)
            ys.append(_dot(_bf(lhs), _bf(rhs)))
        cmb = _bf(cm)
        y = jnp.concatenate(ys, axis=1) + co["expc"] * _dot(cmb, _bf(state)) + d_ref[...] * xs
        y_ref[...] = y
        yg = y * _silu(z_ref[...])
        rstd = lax.rsqrt(jnp.mean(yg * yg, axis=1, keepdims=True) + NORM_EPS)
        yn_ref[...] = _bf(yg * rstd * gn_ref[...])
        st_s[...] = co["elast"] * state + _dot(_bf(bm), _bf(xs * co["dec"]), TN)

    return _pc(body, name="ssd_chunk_fwd", grid=(SSD_GROUPS, nc),
               in_specs=[sp["wide"], sp["narrow"], sp["narrow"], sp["dtT"], sp["dtT"], sp["gvec"], sp["wide"],
                         sp["gvec"]],
               out_specs=[sp["wide"], sp["wide"], sp["state"]],
               out_shape=[jax.ShapeDtypeStruct((bsz, tp, SSD_INNER), F32),
                          jax.ShapeDtypeStruct((bsz, tp, SSD_INNER), BF16),
                          jax.ShapeDtypeStruct((bsz, SSD_GROUPS * nc, SSD_N, SSD_GW), F32)],
               scratch=[pltpu.VMEM((bsz, SSD_N, SSD_GW), F32)], vmem=12 << 20,
               )(xs_pre, bm_pre, cm_pre, dt, cum, d_exp, z, gnorm)


def _ssd_chunk_bwd(dyn, xs_pre, bm_pre, cm_pre, dt, cum, d_exp, z, gnorm, y, states, bsz, nc):
    tp = xs_pre.shape[1]
    sp = _ssd_specs(nc, True, bsz)

    def body(dyn_all, xs_all, bm_all, cm_all, dt_all, cum_all, d_ref, z_all, gn_ref, y_all, st_all,
             dxs_all, dbm_all, dcm_all, dz_all, dcum_all, ddt_all, dgn_ref, dd_ref, ds_sall):
        @pl.when(pl.program_id(1) == 0)
        def _():
            for ref in (dgn_ref, dd_ref, ds_sall):
                ref[...] = jnp.zeros_like(ref)

        for bi in range(bsz):
            one(dyn_all.at[bi], xs_all.at[bi], bm_all.at[bi], cm_all.at[bi], dt_all.at[bi], cum_all.at[bi], d_ref,
                z_all.at[bi], gn_ref, y_all.at[bi], st_all.at[bi], dxs_all.at[bi], dbm_all.at[bi], dcm_all.at[bi],
                dz_all.at[bi], dcum_all.at[bi], ddt_all.at[bi], dgn_ref, dd_ref, ds_sall.at[bi])

    def one(dyn_ref, xs_ref, bm_ref, cm_ref, dt_ref, cum_ref, d_ref, z_ref, gn_ref, y_ref, st_ref,
            dxs_ref, dbm_ref, dcm_ref, dz_ref, dcum_ref, ddt_ref, dgn_ref, dd_ref, ds_s):
        xs_p, bm_p, cm_p = xs_ref[...], bm_ref[...], cm_ref[...]
        xs, bm, cm = _silu(xs_p), _silu(bm_p), _silu(cm_p)
        state = st_ref[0]
        co = _ssd_core(xs, bm, cm, dt_ref[...], cum_ref[...])
        low, hd, lane8, cb = co["low"], co["heads"], co["lane8"], co["cb"]
        dt, cum = co["dt"], co["cum"]
        sub8 = lax.broadcasted_iota(jnp.int32, (SSD_HPG, CHUNK), 0)
        eh = _head_sum_matrix()

        def head_rows(full):
            return _dot_terms(eh, full, NT, exact_rhs=False, terms=2)

        def head_col(vec):
            return jnp.sum(eh * vec, axis=1, keepdims=True)

        yv, zv, gn = y_ref[...], z_ref[...], gn_ref[...]
        sz = _silu(zv)
        yg = yv * sz
        rstd = lax.rsqrt(jnp.mean(yg * yg, axis=1, keepdims=True) + NORM_EPS)
        yh = yg * rstd
        dyn = dyn_ref[...]
        dgn_ref[...] += jnp.sum(dyn * yh, axis=0, keepdims=True)
        dyh = dyn * gn
        dyg = rstd * (dyh - yh * jnp.mean(dyh * yh, axis=1, keepdims=True))
        dz_ref[...] = _bf(dyg * yv * _dsilu(zv))
        dy = dyg * sz
        dxs = dy * d_ref[...]
        dd_ref[...] += head_col(jnp.sum(dy * xs, axis=0, keepdims=True))
        cmb, bmb, stb = _bf(cm), _bf(bm), _bf(state)
        ysv = _dot(cmb, stb)
        expc = co["expc"]
        dys = _bf(dy * expc)
        dcum = head_rows(dy * ysv * expc)
        dcm = _dot(dys, stb, NT)
        dstate_out = _dot(cmb, dys, TN)
        dcb = jnp.zeros((CHUNK, CHUNK), F32)
        ddt = jnp.zeros((SSD_HPG, CHUNK), F32)
        dxs_pairs = []
        for j in range(SSD_HPG // 2):
            sl = slice(j * CHUNK, (j + 1) * CHUNK)
            dyp, xp = dy[:, sl], _bf(xs[:, sl])
            lhs = _bf(jnp.concatenate([hd[2 * j]["w"], hd[2 * j + 1]["w"]], axis=1))
            both = _dot(lhs, _bf(dyp), TN)
            dxs_pairs.append(jnp.where(low, both[:CHUNK], both[CHUNK:]))
            for q, msk in ((2 * j, low), (2 * j + 1, jnp.logical_not(low))):
                h = hd[q]
                dw = _dot(_bf(jnp.where(msk, dyp, 0.0)), xp, NT)
                dcb = dcb + dw * h["seg"] * h["dtrow"]
                e_ = dw * h["w"]
                dcum_r = jnp.sum(e_.T, axis=0, keepdims=True) - jnp.sum(e_, axis=0, keepdims=True)
                ddt_r = jnp.sum(dw * cb * h["seg"], axis=0, keepdims=True)
                dcum = dcum + jnp.where(sub8 == q, dcum_r, 0.0)
                ddt = ddt + jnp.where(sub8 == q, ddt_r, 0.0)
        dxs = dxs + jnp.concatenate(dxs_pairs, axis=1)
        dcbb = _bf(dcb)
        dcm = dcm + _dot(dcbb, bmb)
        dbm = _dot(dcbb, cmb, TN)
        dsn = ds_s[...]
        dsb = _bf(dsn)
        dec = co["dec"]
        dbm = dbm + _dot(_bf(xs * dec), dsb, NT)
        dxd = _dot(bmb, dsb)
        dxs = dxs + dxd * dec
        ddec = head_rows(dxd * xs)
        last = cum[:, CHUNK - 1:CHUNK]
        erow = jnp.exp(last - cum)
        ddt = ddt + ddec * erow
        dla = ddec * erow * dt
        dlast = (jnp.sum(dla, axis=1, keepdims=True)
                 + head_col(jnp.sum(dsn * state, axis=0, keepdims=True)) * jnp.exp(last))
        dcum_ref[...] = dcum - dla + jnp.where(lane8 == CHUNK - 1, dlast, 0.0)
        ddt_ref[...] = ddt
        ds_s[...] = co["elast"] * dsn + dstate_out
        dxs_ref[...] = dxs * _dsilu(xs_p)
        dbm_ref[...] = dbm * _dsilu(bm_p)
        dcm_ref[...] = dcm * _dsilu(cm_p)

    st = jax.ShapeDtypeStruct
    hacc = st((SSD_HEADS, CHUNK), F32)
    return _pc(body, name="ssd_chunk_bwd", grid=(SSD_GROUPS, nc),
               in_specs=[sp["wide"], sp["wide"], sp["narrow"], sp["narrow"], sp["dtT"], sp["dtT"], sp["gvec"],
                         sp["wide"], sp["gvec"], sp["wide"], sp["state"]],
               out_specs=[sp["wide"], sp["narrow"], sp["narrow"], sp["wide"], sp["dtT"], sp["dtT"], sp["gvec"],
                          sp["hacc"]],
               out_shape=[st((bsz, tp, SSD_INNER), F32), st((bsz, tp, SSD_BC), F32), st((bsz, tp, SSD_BC), F32),
                          st((bsz, tp, SSD_INNER), BF16), st((bsz, SSD_HEADS, tp), F32),
                          st((bsz, SSD_HEADS, tp), F32), st((1, SSD_INNER), F32), hacc],
               scratch=[pltpu.VMEM((bsz, SSD_N, SSD_GW), F32)], vmem=20 << 20,
               )(dyn, xs_pre, bm_pre, cm_pre, dt, cum, d_exp, z, gnorm, y, states)


SSD_BC = SSD_GROUPS * SSD_N


def _ssd_weights(conv_w, conv_b, dt_bias, a_log, d, gnorm):
    cuts = (0, SSD_INNER, SSD_INNER + SSD_BC, SSD_INNER + 2 * SSD_BC)
    return dict(
        conv_w=[conv_w[:, cuts[i]:cuts[i + 1]] for i in range(3)],
        conv_b=[conv_b[cuts[i]:cuts[i + 1]] for i in range(3)],
        dt_bias=dt_bias.reshape(SSD_HEADS, 1), a_log=a_log.reshape(SSD_HEADS, 1),
        d_exp=jnp.repeat(d.reshape(SSD_HEADS), SSD_P).reshape(1, SSD_INNER), gnorm=gnorm.reshape(1, SSD_INNER))


def _ssd_layer_fwd(z, xs_in, bm_in, cm_in, dt_rows, w, bsz, nc):
    pres = [_conv_fwd(a, w["conv_w"][i], w["conv_b"][i], bsz, nc, name=f"ssd_conv{i}")
            for i, a in enumerate((xs_in, bm_in, cm_in))]
    def seq(a):
        return a.reshape(bsz, nc * CHUNK, a.shape[-1])

    dt_t = jnp.swapaxes(seq(dt_rows)[:, :, :SSD_HEADS], 1, 2)
    dt, cum = _ssd_dt_prep(dt_t, w["dt_bias"], w["a_log"], bsz, nc)
    y, yn, states = _ssd_chunk_fwd(seq(pres[0]), seq(pres[1]), seq(pres[2]), dt, cum, w["d_exp"], seq(z),
                                   w["gnorm"], bsz, nc)
    return yn.reshape(-1, SSD_INNER), dict(pres=pres, dt_t=dt_t, dt=dt, cum=cum, y=y, states=states)


def _ssd_layer_bwd(dyn, z, xs_in, bm_in, cm_in, sv, w, bsz, nc):
    pres = sv["pres"]

    def seq(a):
        return a.reshape(bsz, nc * CHUNK, a.shape[-1])

    def rows(a):
        return a.reshape(-1, a.shape[-1])

    dxs_p, dbm_p, dcm_p, dz, dcum, ddt_direct, dgn, dd = _ssd_chunk_bwd(
        seq(dyn), seq(pres[0]), seq(pres[1]), seq(pres[2]), sv["dt"], sv["cum"], w["d_exp"], seq(z), w["gnorm"],
        sv["y"], sv["states"], bsz, nc)
    ddt_t, dbias, dal = _ssd_dt_post(dcum, ddt_direct, sv["dt_t"], w["dt_bias"], w["a_log"], bsz, nc)
    dz = rows(dz)
    outs = [_conv_bwd(rows(dp), a, w["conv_w"][i], bsz, nc, name=f"ssd_conv_bwd{i}")
            for i, (dp, a) in enumerate(((dxs_p, xs_in), (dbm_p, bm_in), (dcm_p, cm_in)))]
    ddt = _bf(_pad_lanes(rows(jnp.swapaxes(ddt_t, 1, 2))))
    grads = dict(
        ssd_conv_w=jnp.concatenate([o[1] for o in outs], axis=1),
        ssd_conv_b=jnp.concatenate([o[2] for o in outs], axis=1),
        ssd_dt_bias=dbias[:, 0], ssd_a_log=dal[:, 0], ssd_d=dd[:, 0], ssd_gnorm=dgn)
    return dz, outs[0][0], outs[1][0], outs[2][0], ddt, grads


WNAMES = ("meta_tokens", "ab_norm", "ab_w_in", "s5_lambda_re", "s5_lambda_im", "s5_log_dt", "s5_b_re", "s5_b_im",
          "s5_c_re", "s5_c_im", "s5_d", "s5_glu_w", "s5_glu_b", "ml_conv_w", "ml_conv_b", "ml_wq", "ml_wk", "ml_wv",
          "ml_w_gate", "ml_b_gate", "ml_norm", "ml_skip", "ab_w_out", "ssd_norm", "ssd_w_in", "ssd_conv_w",
          "ssd_conv_b", "ssd_dt_bias", "ssd_a_log", "ssd_d", "ssd_gnorm", "ssd_w_out", "final_norm")
SHARD_AXIS = dict(meta_tokens=1, ab_w_in=2, s5_glu_w=1, ml_conv_w=2, ml_wq=1, ml_wk=1, ml_wv=1, ml_w_gate=1,
                  ab_w_out=1, ssd_norm=1, ssd_w_in=2, ssd_conv_w=2, ssd_conv_b=1, ssd_gnorm=1, ssd_w_out=1)
BIG = ("ab_w_in", "s5_glu_w", "ab_w_out", "ssd_w_in", "ssd_w_out")
SMALL = tuple(n for n in WNAMES if n in SHARD_AXIS and n not in BIG)
REPL = tuple(n for n in WNAMES if n not in SHARD_AXIS)
PACK_ALIGN = 8 * 128


def _pack(arrs):
    lead = arrs[0][1]
    parts = []
    for a, nlead in arrs:
        f = a.reshape(a.shape[:nlead] + (-1,))
        parts.append(jnp.pad(f, [(0, 0)] * nlead + [(0, (-f.shape[-1]) % PACK_ALIGN)]))
    flat = jnp.concatenate(parts, axis=lead)
    return flat.reshape(flat.shape[:lead] + (-1, 128))


def _unpack(p, shapes):
    out, off = [], 0
    lead = p.shape[:-2]
    flat = p.reshape(lead + (-1,))
    for s in shapes:
        n = math.prod(s)
        out.append(flat[..., off:off + n].reshape(lead + tuple(s)))
        off += -(-n // PACK_ALIGN) * PACK_ALIGN
    return out


def _assemble(g, axis):
    m = jnp.moveaxis(g, 0, axis)
    return m.reshape(m.shape[:axis] + (m.shape[axis] * m.shape[axis + 1],) + m.shape[axis + 2:])


def _split(full, axis):
    s = full.shape
    m = full.reshape(s[:axis] + (N_DEV, s[axis] // N_DEV) + s[axis + 1:])
    return jnp.moveaxis(m, axis, 0)


def kernel(x, *rest):
    nw = len(WNAMES)
    w = dict(zip(WNAMES, rest[:nw]))
    loss_target = rest[nw]
    mom = dict(zip(WNAMES, rest[nw + 1:2 * nw + 1]))
    var = dict(zip(WNAMES, rest[2 * nw + 1:3 * nw + 1]))
    bsz = x.shape[0]
    nc = 1 + SEQ // CHUNK
    tp = nc * CHUNK

    local = {n: _bf(w[n][0]) for n in BIG}
    small_local = _pack([(w[n], 0) for n in SMALL])
    gs = _exchange_start([small_local], ["ag"], name="gather_s")
    ga = _exchange_start([local["ab_w_in"]], ["ag"], name="gather_a", dep=gs["token"], peers=SAME_CORE[1:])
    got_s = _exchange_wait(gs, ga["token"])

    def assemble_big(n, got):
        return _assemble(got[:, None], SHARD_AXIS[n])[0]

    full = {}
    for n, g in zip(SMALL, _unpack(got_s[0], [w[n].shape for n in SMALL])):
        full[n] = _assemble(g, SHARD_AXIS[n])[0] if n != "meta_tokens" else _assemble(g, SHARD_AXIS[n])
    for n in REPL:
        full[n] = w[n][0] if n != "final_norm" else w[n]
    glu_b = full["s5_glu_b"].reshape(1, S5_WIDTH)
    meta = jnp.broadcast_to(full["meta_tokens"][None], (bsz, N_META, D_MODEL))
    h0 = jnp.concatenate([jnp.zeros((bsz, PAD_ROWS, D_MODEL), F32), meta, x], axis=1).reshape(bsz * tp, D_MODEL)
    xn0 = _rms_fwd(h0, full["ab_norm"], name="rms0")
    s5p, s5_vjp = _s5_tables(*[full[n] for n in ("s5_lambda_re", "s5_lambda_im", "s5_log_dt", "s5_b_re", "s5_b_im",
                                                   "s5_c_re", "s5_c_im", "s5_d")])
    mlw = _ml_weights(*[full[n] for n in ("ml_conv_w", "ml_conv_b", "ml_wq", "ml_wk", "ml_wv", "ml_w_gate",
                                           "ml_b_gate", "ml_norm", "ml_skip")])
    got_a = _exchange_wait(ga, [xn0, s5p["wbr"], s5p["wcr"], s5p["pr"], mlw["wq"], mlw["wk"], mlw["wv"], mlw["wgq"]])
    fwd_a = _sibling_forward_start(got_a[0], name="gather_a2")
    got_a = [_sibling_forward_wait(fwd_a, fwd_a["token"])]
    gb = _exchange_start([local["s5_glu_w"], local["ab_w_out"]], ["ag", "ag"], name="gather_b", dep=got_a[0])
    gc = _exchange_start([local["ssd_w_in"], local["ssd_w_out"]], ["ag", "ag"], name="gather_c", dep=gb["token"])
    full["ab_w_in"] = assemble_big("ab_w_in", got_a[0])
    cuts0 = (0, S5_WIDTH, 2 * S5_WIDTH, 2 * S5_WIDTH + ML_WIDTH, 2 * (S5_WIDTH + ML_WIDTH))
    w_in0 = [full["ab_w_in"][:, cuts0[i]:cuts0[i + 1]] for i in range(4)]

    u, za, xb, zb = [_mm(xn0, wi, "NN", name=f"in0_{i}") for i, wi in enumerate(w_in0)]
    got_b = []

    def glu_w_after(scan_out):
        got_b.extend(_exchange_wait(gb, scan_out))
        return assemble_big("s5_glu_w", got_b[0])

    sv5 = _s5_layer_fwd(u, s5p, glu_w_after, bsz, nc)
    glu_w = assemble_big("s5_glu_w", got_b[0])
    w_out0 = assemble_big("ab_w_out", got_b[1])
    w_out0 = [w_out0[:S5_WIDTH], w_out0[S5_WIDTH:]]
    ya = _s5_post(sv5["y1"], sv5["glu_pre"], glu_b, za)
    yb, svm = _ml_layer_fwd(xb, zb, mlw, bsz, nc)
    h1 = _mm(ya, w_out0[0], "NN", name="out0_a", add=h0)
    h1 = _mm(yb, w_out0[1], "NN", name="out0_b", add=h1)
    got_c = _exchange_wait(gc, h1)
    w_in1, w_out1 = assemble_big("ssd_w_in", got_c[0]), assemble_big("ssd_w_out", got_c[1])
    cuts1 = (0, SSD_INNER, 2 * SSD_INNER, 2 * SSD_INNER + SSD_BC, 2 * SSD_INNER + 2 * SSD_BC)
    w_in1 = [w_in1[:, cuts1[i]:cuts1[i + 1]] for i in range(4)] + [_pad_lanes(w_in1[:, cuts1[4]:])]
    xn1 = _rms_fwd(h1, full["ssd_norm"], name="rms1")
    z1, xs_in, bm_in, cm_in, dt_rows = [_mm(xn1, wi, "NN", name=f"in1_{i}") for i, wi in enumerate(w_in1)]
    ssdw = _ssd_weights(*[full[n] for n in ("ssd_conv_w", "ssd_conv_b", "ssd_dt_bias", "ssd_a_log", "ssd_d",
                                             "ssd_gnorm")])
    yn, svs = _ssd_layer_fwd(z1, xs_in, bm_in, cm_in, dt_rows, ssdw, bsz, nc)
    h2 = _mm(yn, w_out1, "NN", name="out1", add=h1)
    loss_part, dh2, dfinal, dh2_b = _final_loss(h2, full["final_norm"], loss_target, bsz, nc)

    g = {"final_norm": dfinal}
    dyn = _mm(dh2_b, w_out1, "NT", name="d_out1")
    g["ssd_w_out"] = _mm(yn, dh2_b, "TN", name="dw_out1", out_dtype=BF16)
    dz1, dxs, dbm, dcm, ddt, gs = _ssd_layer_bwd(dyn, z1, xs_in, bm_in, cm_in, svs, ssdw, bsz, nc)
    g.update(gs)
    dps1 = (dz1, dxs, dbm, dcm, ddt)
    dxn1 = None
    for i, (dp, wi) in enumerate(zip(dps1, w_in1)):
        dxn1 = _mm(dp, wi, "NT", name=f"d_in1_{i}", add=dxn1)
    dw1 = [_mm(xn1, dp, "TN", name=f"dw_in1_{i}", out_dtype=BF16) for i, dp in enumerate(dps1)]
    g["ssd_w_in"] = jnp.concatenate(dw1[:4] + [dw1[4][:, :SSD_HEADS]], axis=1)

    def local_shape(n):
        return w[n].shape

    def slabs(n):
        gf = g[n].reshape((1,) + tuple(g[n].shape)) if n != "meta_tokens" else g[n]
        full_shape = tuple(d * (N_DEV if i == SHARD_AXIS[n] else 1) for i, d in enumerate(local_shape(n)))
        return _split(gf.reshape(full_shape), SHARD_AXIS[n])

    x1 = _exchange_start([slabs("ssd_w_in")[:, 0], slabs("ssd_w_out")[:, 0]], ["a2a", "a2a"], name="grads_1")
    dh1, g["ssd_norm"], dh1_b = _rms_bwd(h1, full["ssd_norm"], dxn1, dh2, name="rms1_bwd", dep=x1["token"])
    dya = _mm(dh1_b, w_out0[0], "NT", name="d_out0_a")
    dyb = _mm(dh1_b, w_out0[1], "NT", name="d_out0_b")
    g["ab_w_out"] = jnp.concatenate([_mm(ya, dh1_b, "TN", name="dw_out0_a", out_dtype=BF16),
                                     _mm(yb, dh1_b, "TN", name="dw_out0_b", out_dtype=BF16)], axis=0)
    du, dza, g5 = _s5_layer_bwd(dya, u, za, sv5, s5p, s5_vjp, glu_w, glu_b, bsz, nc)
    g.update(g5)
    x2 = _exchange_start([slabs("ab_w_out")[:, 0], _bf(slabs("s5_glu_w")[:, 0])], ["a2a", "a2a"], name="grads_2")
    dxb, dzb, gm = _ml_layer_bwd(dyb, xb, zb, svm, mlw, bsz, nc, dep=x2["token"])
    g.update(gm)
    dps0 = (du, dza, dxb, dzb)
    dw0 = [_mm(xn0, dp, "TN", name=f"dw_in0_{i}", out_dtype=BF16, tn=S5_WIDTH, slabs=True) for i, dp in enumerate(dps0)]
    dw_in0_slabs = jnp.concatenate(dw0, axis=0)
    x3 = _exchange_start([dw_in0_slabs], ["a2a"], name="grads_3")
    dxn0 = None
    for i, (dp, wi) in enumerate(zip(dps0, w_in0)):
        dxn0 = _mm(dp, wi, "NT", name=f"d_in0_{i}", add=dxn0, dep=x3["token"] if i == 0 else None)
    grad_x, d_chunk0, g["ab_norm"] = _rms_bwd_first(h0, full["ab_norm"], dxn0, dh1, bsz, nc, name="rms0_bwd")
    g["meta_tokens"] = jnp.sum(d_chunk0[:, PAD_ROWS:], axis=0)

    small_g = _pack([(slabs(n), 1) for n in SMALL])
    repl_g = _pack([(g[n], 0) for n in REPL])
    x4 = _exchange_start([small_g, repl_g, loss_part], ["a2a", "ag", "ag"], name="grads_4")

    def update_big(n, gp):
        return _adamw(w[n][0], mom[n][0], var[n][0], gp, name=f"adamw_{n}")

    res = {}
    ex1 = _exchange_wait(x1, x4["token"])
    res["ssd_w_in"], res["ssd_w_out"] = update_big("ssd_w_in", ex1[0]), update_big("ssd_w_out", ex1[1])
    ex2 = _exchange_wait(x2, res["ssd_w_out"][0])
    res["ab_w_out"], res["s5_glu_w"] = update_big("ab_w_out", ex2[0]), update_big("s5_glu_w", ex2[1])
    ex3 = _exchange_wait(x3, [res[n][0] for n in ("ssd_w_in", "ssd_w_out", "ab_w_out", "s5_glu_w")])
    res["ab_w_in"] = update_big("ab_w_in", ex3[0])
    ex4 = _exchange_wait(x4, res["ab_w_in"][0])
    loss = jnp.sum(ex4[2][:, 0, 0])
    for names, gp, tag in ((SMALL, ex4[0], "small"), (REPL, ex4[1], "repl")):
        shapes = [local_shape(n) for n in names]
        packs = [_pack([(d[n], 0) for n in names]) for d in (w, mom, var)]
        outs = _adamw(packs[0], packs[1], packs[2], gp, name=f"adamw_{tag}")
        for k, o in enumerate(outs):
            for n, a in zip(names, _unpack(o, shapes)):
                res.setdefault(n, [None] * 4)[k] = a
    outs = [loss, grad_x]
    for k in range(4):
        outs += [res[n][k].reshape(local_shape(n)) for n in WNAMES]
    return tuple(outs)
```

```python
import functools
import math

import jax
import jax.numpy as jnp
from jax import lax
from jax.experimental import pallas as pl
from jax.experimental.pallas import tpu as pltpu

F32 = jnp.float32
BF16 = jnp.bfloat16

D_MODEL = 2048
SEQ = 2048
N_META = 16
CHUNK = 128
PAD_ROWS = CHUNK - N_META
NORM_EPS = 1e-6
HEAD_NORM_EPS = 1e-5
S5_WIDTH = 1024
S5_GROUPS = 64
S5_GROUP_SIZE = 16
S5_STATE = 64
S5_GB = 8
S5_LANES = S5_GB * S5_STATE
ML_WIDTH = 3072
ML_HEADS = 8
ML_DH = 384
ML_CONV = 4
QKV_BLOCK = 4
SSD_INNER = 4096
SSD_HEADS = 64
SSD_P = 64
SSD_N = 128
SSD_GROUPS = 8
SSD_HPG = 8
SSD_GW = SSD_HPG * SSD_P
N_DEV = 8
ADAM_LR, ADAM_B1, ADAM_B2, ADAM_EPS, ADAM_WD, ADAM_STEP = 0.001, 0.9, 0.999, 1e-08, 0.01, 10
NEG = -1e30
VMEM_CAP = 60 * 1024 * 1024
MM_BLOCK_BUDGET = 22 * 1024 * 1024
MESH = pl.DeviceIdType.MESH

NN = (((1,), (0,)), ((), ()))
NT = (((1,), (1,)), ((), ()))
TN = (((0,), (0,)), ((), ()))


def _dot(a, b, dims=NN):
    return lax.dot_general(a, b, dims, preferred_element_type=F32)


def _bf(x):
    return x.astype(BF16)


def _pick(n, cands):
    for c in cands:
        if n % c == 0:
            return c
    return n


def _nbytes(shape, dtype):
    return math.prod(shape) * jnp.dtype(dtype).itemsize


ANY_SPEC = pl.BlockSpec(memory_space=pl.ANY)


def _pc(body, *, name, grid, in_specs, out_specs, out_shape, scratch=(), vmem=None, dep=None):
    limit = None if vmem is None else int(min(VMEM_CAP, max(32 * 1024 * 1024, 2 * vmem + (8 << 20))))
    n_in = len(in_specs)
    if dep is not None:
        inner = body

        def body(*refs):
            inner(*refs[:n_in], *refs[n_in + 1:])

        in_specs = list(in_specs) + [ANY_SPEC]
    call = pl.pallas_call(
        body, name=name, grid=grid, in_specs=in_specs, out_specs=out_specs, out_shape=out_shape,
        scratch_shapes=list(scratch),
        compiler_params=pltpu.CompilerParams(dimension_semantics=("arbitrary",) * len(grid), vmem_limit_bytes=limit))
    return call if dep is None else (lambda *args: call(*args, dep))


def _silu(x):
    return x * jax.nn.sigmoid(x)


def _dsilu(x):
    s = jax.nn.sigmoid(x)
    return s * (1.0 + x * (1.0 - s))


def _gelu_and_grad(x):
    c0 = math.sqrt(2.0 / math.pi)
    inner = c0 * (x + 0.044715 * x * x * x)
    t = jnp.tanh(inner)
    g = 0.5 * x * (1.0 + t)
    dg = 0.5 * (1.0 + t) + 0.5 * x * (1.0 - t * t) * c0 * (1.0 + 3 * 0.044715 * x * x)
    return g, dg


def _mm(a, b, mode, *, name, add=None, out_dtype=F32, tn=None, slabs=False, dep=None):
    if mode == "NN":
        (m, k), (k2, n) = a.shape, b.shape
    elif mode == "NT":
        (m, k), (n, k2) = a.shape, b.shape
    else:
        (k, m), (k2, n) = a.shape, b.shape
    assert k == k2, (a.shape, b.shape, mode)
    tm = _pick(m, (1088, 1024, 768, 512, 384, 256, 128))
    def block_bytes(tk, tn_):
        return (_nbytes((tm, tk), a.dtype) + _nbytes((tk, tn_), b.dtype) + _nbytes((tm, tn_), out_dtype)
                + (_nbytes((tm, tn_), F32) if add is not None else 0))

    budget = MM_BLOCK_BUDGET // 2 if mode == "TN" else MM_BLOCK_BUDGET
    if tn is None:
        tn = _pick(n, (512, 384, 256, 128))
        if mode != "TN" and n % 1024 == 0 and block_bytes(k, 1024) <= (2 * budget) // 3:
            tn = 1024
    if mode == "TN" and block_bytes(k, tn) > budget and m % 512 == 0 and tm > 512:
        wide, tm = tm, 512
        if block_bytes(k, tn) > budget:
            tm = wide
    tk = k if block_bytes(k, tn) <= budget else _pick(k, (2176, 2048, 1088, 1024, 768, 512, 384, 256, 128))
    nk = k // tk
    dims = {"NN": NN, "NT": NT, "TN": TN}[mode]

    def body(*refs):
        a_ref, b_ref = refs[0], refs[1]
        add_ref = refs[2] if add is not None else None
        o_ref = refs[3] if add is not None else refs[2]

        def finish(r):
            if add_ref is not None:
                r = r + add_ref[...]
            o_ref[...] = r.reshape(o_ref.shape).astype(o_ref.dtype)

        prod = _dot(_bf(a_ref[...]), _bf(b_ref[...]), dims)
        if nk == 1:
            finish(prod)
            return
        acc_ref = refs[-1]
        kk = pl.program_id(2)

        @pl.when(kk == 0)
        def _():
            acc_ref[...] = prod

        @pl.when(kk > 0)
        def _():
            acc_ref[...] += prod

        @pl.when(kk == nk - 1)
        def _():
            finish(acc_ref[...])

    if mode == "NN":
        a_spec = pl.BlockSpec((tm, tk), lambda i, j, kk: (i, kk))
        b_spec = pl.BlockSpec((tk, tn), lambda i, j, kk: (kk, j))
    elif mode == "NT":
        a_spec = pl.BlockSpec((tm, tk), lambda i, j, kk: (i, kk))
        b_spec = pl.BlockSpec((tn, tk), lambda i, j, kk: (j, kk))
    else:
        a_spec = pl.BlockSpec((tk, tm), lambda i, j, kk: (kk, i))
        b_spec = pl.BlockSpec((tk, tn), lambda i, j, kk: (kk, j))
    in_specs = [a_spec, b_spec]
    args = [a, b]
    if add is not None:
        in_specs.append(pl.BlockSpec((tm, tn), lambda i, j, kk: (i, j)))
        args.append(add)
    if slabs:
        out_shape = jax.ShapeDtypeStruct((n // tn, m, tn), out_dtype)
        out_spec = pl.BlockSpec((1, tm, tn), lambda i, j, kk: (j, i, 0))
    else:
        out_shape = jax.ShapeDtypeStruct((m, n), out_dtype)
        out_spec = pl.BlockSpec((tm, tn), lambda i, j, kk: (i, j))
    return _pc(body, name=name, grid=(m // tm, n // tn, nk), in_specs=in_specs, out_specs=out_spec,
               out_shape=out_shape, scratch=[] if nk == 1 else [pltpu.VMEM((tm, tn), F32)],
               vmem=block_bytes(tk, tn) + (0 if nk == 1 else _nbytes((tm, tn), F32) // 2), dep=dep)(*args)


def _rms_fwd(x, g, *, name):
    r, d = x.shape
    tm = _pick(r, (256, 128))

    def body(x_ref, g_ref, o_ref):
        xv = x_ref[...]
        rstd = lax.rsqrt(jnp.mean(xv * xv, axis=1, keepdims=True) + NORM_EPS)
        o_ref[...] = (xv * rstd * g_ref[...]).astype(o_ref.dtype)

    return _pc(body, name=name, grid=(r // tm,),
               in_specs=[pl.BlockSpec((tm, d), lambda i: (i, 0)), pl.BlockSpec((1, d), lambda i: (0, 0))],
               out_specs=pl.BlockSpec((tm, d), lambda i: (i, 0)), out_shape=jax.ShapeDtypeStruct((r, d), BF16),
               vmem=tm * d * 6)(x, g.reshape(1, d))


def _rms_bwd(x, g, dxn, dres, *, name, dep=None):
    r, d = x.shape
    tm = _pick(r, (256, 128))

    def body(x_ref, g_ref, dxn_ref, dres_ref, dx_ref, dg_ref, db_ref):
        @pl.when(pl.program_id(0) == 0)
        def _():
            dg_ref[...] = jnp.zeros_like(dg_ref)

        xv = x_ref[...]
        rstd = lax.rsqrt(jnp.mean(xv * xv, axis=1, keepdims=True) + NORM_EPS)
        xh = xv * rstd
        dy = dxn_ref[...]
        dg_ref[...] += jnp.sum(dy * xh, axis=0, keepdims=True)
        dyg = dy * g_ref[...]
        dx_ref[...] = dres_ref[...] + rstd * (dyg - xh * jnp.mean(dyg * xh, axis=1, keepdims=True))

        db_ref[...] = _bf(dx_ref[...])

    row = pl.BlockSpec((tm, d), lambda i: (i, 0))
    vec = pl.BlockSpec((1, d), lambda i: (0, 0))
    return _pc(body, name=name, grid=(r // tm,), in_specs=[row, vec, row, row], out_specs=[row, vec, row],
               out_shape=[jax.ShapeDtypeStruct((r, d), F32), jax.ShapeDtypeStruct((1, d), F32),
                          jax.ShapeDtypeStruct((r, d), BF16)],
               vmem=tm * d * 18, dep=dep)(x, g.reshape(1, d), dxn, dres)


def _rms_bwd_first(x, g, dxn, dres, bsz, nc, *, name):
    d = x.shape[1]

    def body(x_ref, g_ref, dxn_ref, dres_ref, gx_ref, d0_ref, dg_ref):
        b, c = pl.program_id(0), pl.program_id(1)

        @pl.when((b == 0) & (c == 0))
        def _():
            dg_ref[...] = jnp.zeros_like(dg_ref)

        xv = x_ref[...]
        rstd = lax.rsqrt(jnp.mean(xv * xv, axis=1, keepdims=True) + NORM_EPS)
        xh = xv * rstd
        dy = dxn_ref[...]
        dg_ref[...] += jnp.sum(dy * xh, axis=0, keepdims=True)
        dyg = dy * g_ref[...]
        dx = dres_ref[...] + rstd * (dyg - xh * jnp.mean(dyg * xh, axis=1, keepdims=True))

        @pl.when(c == 0)
        def _():
            d0_ref[0] = dx

        @pl.when(c > 0)
        def _():
            gx_ref[0] = dx

    row = pl.BlockSpec((CHUNK, d), lambda b, c: (b * nc + c, 0))
    vec = pl.BlockSpec((1, d), lambda b, c: (0, 0))
    return _pc(body, name=name, grid=(bsz, nc), in_specs=[row, vec, row, row],
               out_specs=[pl.BlockSpec((1, CHUNK, d), lambda b, c: (b, jnp.maximum(c - 1, 0), 0)),
                          pl.BlockSpec((1, CHUNK, d), lambda b, c: (b, 0, 0)), vec],
               out_shape=[jax.ShapeDtypeStruct((bsz, (nc - 1) * CHUNK, d), F32),
                          jax.ShapeDtypeStruct((bsz, CHUNK, d), F32), jax.ShapeDtypeStruct((1, d), F32)],
               vmem=CHUNK * d * 24)(x, g.reshape(1, d), dxn, dres)


def _final_loss(h, g, target, bsz, nc):
    d = h.shape[1]

    def body(h_ref, g_ref, t_ref, loss_ref, dh_ref, dg_ref, db_ref):
        b, c = pl.program_id(0), pl.program_id(1)

        @pl.when((b == 0) & (c == 0))
        def _():
            loss_ref[...] = jnp.zeros_like(loss_ref)
            dg_ref[...] = jnp.zeros_like(dg_ref)

        @pl.when(c == 0)
        def _():
            dh_ref[...] = jnp.zeros_like(dh_ref)
            db_ref[...] = jnp.zeros_like(db_ref)

        @pl.when(c > 0)
        def _():
            xv = h_ref[...]
            rstd = lax.rsqrt(jnp.mean(xv * xv, axis=1, keepdims=True) + NORM_EPS)
            xh = xv * rstd
            gv = g_ref[...]
            err = xh * gv - t_ref[0]
            loss_ref[...] += 0.5 * jnp.sum(jnp.mean(err * err, axis=1, keepdims=True))
            dy = err * (1.0 / d)
            dg_ref[...] += jnp.sum(dy * xh, axis=0, keepdims=True)
            dyg = dy * gv
            dh = rstd * (dyg - xh * jnp.mean(dyg * xh, axis=1, keepdims=True))
            dh_ref[...] = dh
            db_ref[...] = _bf(dh)

    row = pl.BlockSpec((CHUNK, d), lambda b, c: (b * nc + c, 0))
    vec = pl.BlockSpec((1, d), lambda b, c: (0, 0))
    return _pc(body, name="final_loss", grid=(bsz, nc),
               in_specs=[row, vec, pl.BlockSpec((1, CHUNK, d), lambda b, c: (b, jnp.maximum(c - 1, 0), 0))],
               out_specs=[pl.BlockSpec((8, 128), lambda b, c: (0, 0)), row, vec, row],
               out_shape=[jax.ShapeDtypeStruct((8, 128), F32), jax.ShapeDtypeStruct(h.shape, F32),
                          jax.ShapeDtypeStruct((1, d), F32), jax.ShapeDtypeStruct(h.shape, BF16)],
               vmem=CHUNK * d * 18)(h, g.reshape(1, d), target)


def _adamw(w, m, v, gparts, *, name):
    r, c = w.shape
    tr = _pick(r, (256, 128)) if r * c * 4 > (1 << 20) else r

    def body(w_ref, m_ref, v_ref, gp_ref, g_ref, d_ref, nm_ref, nv_ref):
        g = gp_ref[0].astype(F32)
        for j in range(1, N_DEV):
            g = g + gp_ref[j].astype(F32)
        mm = ADAM_B1 * m_ref[...] + (1.0 - ADAM_B1) * g
        vv = ADAM_B2 * v_ref[...] + (1.0 - ADAM_B2) * (g * g)
        m_hat = mm / (1.0 - ADAM_B1 ** ADAM_STEP)
        v_hat = vv / (1.0 - ADAM_B2 ** ADAM_STEP)
        g_ref[...] = g
        d_ref[...] = -ADAM_LR * (m_hat / (jnp.sqrt(v_hat) + ADAM_EPS) + ADAM_WD * w_ref[...])
        nm_ref[...] = mm
        nv_ref[...] = vv

    blk = pl.BlockSpec((tr, c), lambda i: (i, 0))
    out = jax.ShapeDtypeStruct((r, c), F32)
    return _pc(body, name=name, grid=(r // tr,),
               in_specs=[blk, blk, blk, pl.BlockSpec((N_DEV, tr, c), lambda i: (0, i, 0))],
               out_specs=[blk, blk, blk, blk], out_shape=[out, out, out, out],
               vmem=tr * c * (4 * 7 + N_DEV * jnp.dtype(gparts.dtype).itemsize))(w, m, v, gparts)


PEERS = (1, 2, 4, 6, 3, 5, 7)
HBM_SPEC = pl.BlockSpec(memory_space=pltpu.HBM)
SEM_SPEC = pl.BlockSpec(memory_space=pltpu.SEMAPHORE)
SIDE_EFFECT = pltpu.SideEffectType.DATAFLOW_SIDE_EFFECTING


def _peer(p):
    x, y, c = lax.axis_index("x"), lax.axis_index("y"), lax.axis_index("c")
    tx, ty, tc = x ^ ((p >> 2) & 1), y ^ ((p >> 1) & 1), c ^ (p & 1)
    return (tx, ty, tc), 4 * tx + 2 * ty + tc


def _place_own(a, kind, *, name):
    rows, cols = a.shape[-2:]
    small = _nbytes((rows, cols), a.dtype) <= (2 << 20)
    tr = rows if small else _pick(rows, (512, 256, 128, 64, 32, 16))
    me = (4 * lax.axis_index("x") + 2 * lax.axis_index("y") + lax.axis_index("c")).astype(jnp.int32).reshape(1)

    def body(me_ref, in_ref, out_ref):
        out_ref[...] = in_ref[...].reshape(out_ref.shape)

    if kind == "a2a":
        in_spec = pl.BlockSpec((1, tr, cols), lambda i, me_ref: (me_ref[0], i, 0))
    else:
        in_spec = pl.BlockSpec((tr, cols), lambda i, me_ref: (i, 0))
    return pl.pallas_call(
        body, name=name, out_shape=jax.ShapeDtypeStruct((N_DEV, rows, cols), a.dtype),
        grid_spec=pltpu.PrefetchScalarGridSpec(
            num_scalar_prefetch=1, grid=(rows // tr,), in_specs=[in_spec],
            out_specs=pl.BlockSpec((1, tr, cols), lambda i, me_ref: (me_ref[0], i, 0))))(me, a)


def _exchange_copies(ins, lands, send_sems, recv_sems, kinds, incoming, peers=PEERS):
    me = 4 * lax.axis_index("x") + 2 * lax.axis_index("y") + lax.axis_index("c")
    copies = []
    for i, kind in enumerate(kinds):
        for p in peers:
            dev, tgt = _peer(p)
            k = i * (N_DEV - 1) + p - 1
            copies.append(pltpu.make_async_remote_copy(
                src_ref=ins[i].at[tgt] if kind == "a2a" else ins[i], dst_ref=lands[i].at[tgt if incoming else me],
                send_sem=send_sems.at[k], recv_sem=recv_sems.at[k], device_id=dev, device_id_type=MESH))
    return copies


def _exchange_start(arrays, kinds, *, name, dep=None, peers=PEERS):
    n = len(arrays)
    lands = [_place_own(a, k, name=f"{name}_own{i}") for i, (a, k) in enumerate(zip(arrays, kinds))]
    extra = [] if dep is None else [dep]

    def body(*refs):
        ins, lnd = refs[:n], refs[n:2 * n]
        send_sems, recv_sems = refs[2 * n + len(extra)], refs[2 * n + len(extra) + 1]
        token = refs[-1]
        for cp in _exchange_copies(ins, lnd, send_sems, recv_sems, kinds, False, peers):
            cp.start()
        token[...] = jnp.zeros_like(token)

    sem = pltpu.SemaphoreType.DMA((n * (N_DEV - 1),))
    outs = pl.pallas_call(
        body, name=name, in_specs=[HBM_SPEC] * (2 * n) + [ANY_SPEC] * len(extra),
        out_specs=[SEM_SPEC, SEM_SPEC] + [HBM_SPEC] * (2 * n) + [pl.BlockSpec(memory_space=pltpu.VMEM)],
        out_shape=[sem, sem] + [pltpu.HBM(a.shape, a.dtype) for a in arrays + lands]
        + [jax.ShapeDtypeStruct((8, 128), F32)],
        input_output_aliases={i: 2 + i for i in range(2 * n)},
        compiler_params=pltpu.CompilerParams(has_side_effects=SIDE_EFFECT),
    )(*[pltpu.with_memory_space_constraint(a, pltpu.HBM) for a in arrays + lands], *extra)
    return dict(send=outs[0], recv=outs[1], ins=list(outs[2:2 + n]), lands=list(outs[2 + n:2 + 2 * n]),
                token=outs[-1], kinds=kinds, name=name, peers=peers)


def _exchange_wait(h, after):
    n = len(h["ins"])
    kinds = h["kinds"]

    def body(*refs):
        ins, lnd = refs[:n], refs[n:2 * n]
        send_sems, recv_sems = refs[2 * n], refs[2 * n + 1]
        copies = _exchange_copies(ins, lnd, send_sems, recv_sems, kinds, True, h["peers"])
        for cp in copies:
            cp.wait_recv()
        for cp in copies:
            cp.wait_send()

    arrs = h["ins"] + h["lands"]
    after = list(after) if isinstance(after, (list, tuple)) else [after]
    outs = pl.pallas_call(
        body, name=h["name"] + "_wait", in_specs=[HBM_SPEC] * (2 * n) + [SEM_SPEC, SEM_SPEC] + [ANY_SPEC] * len(after),
        out_specs=[HBM_SPEC] * (2 * n), out_shape=[pltpu.HBM(a.shape, a.dtype) for a in arrs],
        input_output_aliases={i: i for i in range(2 * n)},
        compiler_params=pltpu.CompilerParams(has_side_effects=SIDE_EFFECT),
    )(*arrs, h["send"], h["recv"], *after)
    return list(outs[n:])


SAME_CORE = (0, 2, 4, 6)


def _forward_copies(land, send_sems, recv_sems, incoming):
    me = 4 * lax.axis_index("x") + 2 * lax.axis_index("y") + lax.axis_index("c")
    dev, sibling = _peer(1)
    return [pltpu.make_async_remote_copy(
        src_ref=land.at[me ^ q], dst_ref=land.at[(sibling if incoming else me) ^ q],
        send_sem=send_sems.at[j], recv_sem=recv_sems.at[j], device_id=dev, device_id_type=MESH)
        for j, q in enumerate(SAME_CORE)]


def _sibling_forward_start(land, *, name, dep=None):
    extra = [] if dep is None else [dep]

    def body(*refs):
        land_ref, send_sems, recv_sems, token = refs[0], refs[1 + len(extra)], refs[2 + len(extra)], refs[-1]
        for cp in _forward_copies(land_ref, send_sems, recv_sems, False):
            cp.start()
        token[...] = jnp.zeros_like(token)

    sem = pltpu.SemaphoreType.DMA((len(SAME_CORE),))
    outs = pl.pallas_call(
        body, name=name, in_specs=[HBM_SPEC] + [ANY_SPEC] * len(extra),
        out_specs=[SEM_SPEC, SEM_SPEC, HBM_SPEC, pl.BlockSpec(memory_space=pltpu.VMEM)],
        out_shape=[sem, sem, pltpu.HBM(land.shape, land.dtype), jax.ShapeDtypeStruct((8, 128), F32)],
        input_output_aliases={0: 2}, compiler_params=pltpu.CompilerParams(has_side_effects=SIDE_EFFECT),
    )(pltpu.with_memory_space_constraint(land, pltpu.HBM), *extra)
    return dict(send=outs[0], recv=outs[1], land=outs[2], token=outs[3], name=name)


def _sibling_forward_wait(h, after):
    def body(*refs):
        copies = _forward_copies(refs[0], refs[1], refs[2], True)
        for cp in copies:
            cp.wait_recv()
        for cp in copies:
            cp.wait_send()

    return pl.pallas_call(
        body, name=h["name"] + "_wait", in_specs=[HBM_SPEC, SEM_SPEC, SEM_SPEC, ANY_SPEC], out_specs=HBM_SPEC,
        out_shape=pltpu.HBM(h["land"].shape, h["land"].dtype), input_output_aliases={0: 0},
        compiler_params=pltpu.CompilerParams(has_side_effects=SIDE_EFFECT),
    )(h["land"], h["send"], h["recv"], after)


def _s5_params(lam_re, lam_im, log_dt, b_re, b_im):
    dt = jnp.exp(log_dt)[:, None]
    mag = jnp.exp(lam_re * dt)
    ar, ai = mag * jnp.cos(lam_im * dt), mag * jnp.sin(lam_im * dt)
    den = lam_re * lam_re + lam_im * lam_im
    qr = ((ar - 1.0) * lam_re + ai * lam_im) / den
    qi = (ai * lam_re - (ar - 1.0) * lam_im) / den
    bbr = qr[..., None] * b_re - qi[..., None] * b_im
    bbi = qr[..., None] * b_im + qi[..., None] * b_re
    return ar, ai, bbr, bbi


def _s5_power_table(ar, ai):
    pr, pi = ar.reshape(1, -1), ai.reshape(1, -1)
    while pr.shape[0] < 8:
        sr, si = pr[-1:], pi[-1:]
        pr, pi = (jnp.concatenate([pr, pr * sr - pi * si], axis=0), jnp.concatenate([pi, pr * si + pi * sr], axis=0))
    return pr, pi


def _blockdiag(w, rows, cols):
    w = w.reshape(S5_GB, S5_GB, rows, cols)
    eye = jnp.eye(S5_GB, dtype=w.dtype)
    return jnp.einsum("abrc,bd->abrdc", w, eye).reshape(S5_GB, S5_GB * rows, S5_GB * cols)


def _blockdiag_extract(w, rows, cols):
    w = w.reshape(S5_GB, S5_GB, rows, S5_GB, cols)
    return jnp.einsum("abrbc->abrc", w).reshape(S5_GROUPS, rows, cols)


def _s5_scan_specs(bsz, nc, rev):
    def cc(c):
        return (nc - 1 - c) if rev else c

    return dict(
        u=pl.BlockSpec((bsz, CHUNK, CHUNK), lambda g, c: (0, cc(c), g)),
        x=pl.BlockSpec((bsz, CHUNK, S5_LANES), lambda g, c: (0, cc(c), g)),
        wb=pl.BlockSpec((1, CHUNK, S5_LANES), lambda g, c: (g, 0, 0)),
        wc=pl.BlockSpec((1, S5_LANES, CHUNK), lambda g, c: (g, 0, 0)),
        tab=pl.BlockSpec((8, S5_LANES), lambda g, c: (0, g)),
        step=pl.BlockSpec((8, S5_LANES), lambda g, c: (0, g)),
        d=pl.BlockSpec((1, CHUNK), lambda g, c: (0, g)),
        lane=pl.BlockSpec((1, S5_LANES), lambda g, c: (0, g)),
        xprev=pl.BlockSpec((bsz, 8, S5_LANES), lambda g, c: (0, jnp.maximum(cc(c) * (CHUNK // 8) - 1, 0), g)),
    )


def _s5_fwd(u, wbr, wbi, pr, pi, sr, si, wcr, wci, d, bsz, nc):
    r = u.shape[0]
    tp = r // bsz
    sp = _s5_scan_specs(bsz, nc, False)

    def body(u_all, wbr_ref, wbi_ref, pr_ref, pi_ref, sr_ref, si_ref, wcr_ref, wci_ref, d_ref,
             xr_all, xi_all, y1_all, g_all, cr_sall, ci_sall):
        @pl.when(pl.program_id(1) == 0)
        def _():
            cr_sall[...] = jnp.zeros_like(cr_sall)
            ci_sall[...] = jnp.zeros_like(ci_sall)

        for bi in range(bsz):
            one(u_all.at[bi], wbr_ref, wbi_ref, pr_ref, pi_ref, sr_ref, si_ref, wcr_ref, wci_ref, d_ref,
                xr_all.at[bi], xi_all.at[bi], y1_all.at[bi], g_all.at[bi], cr_sall.at[bi], ci_sall.at[bi])

    def one(u_ref, wbr_ref, wbi_ref, pr_ref, pi_ref, sr_ref, si_ref, wcr_ref, wci_ref, d_ref,
            xr_ref, xi_ref, y1_ref, g_ref, cr_s, ci_s):
        uv = u_ref[...]
        ub = _bf(uv)
        xr, xi = _dot(ub, wbr_ref[0]), _dot(ub, wbi_ref[0])
        sub = lax.broadcasted_iota(jnp.int32, (CHUNK, S5_LANES), 0) % 8
        for k in range(3):
            s = 1 << k
            ar, ai = sr_ref[k:k + 1, :], si_ref[k:k + 1, :]
            hr = jnp.where(sub >= s, pltpu.roll(xr, s, 0), 0.0)
            hi = jnp.where(sub >= s, pltpu.roll(xi, s, 0), 0.0)
            xr, xi = xr + (ar * hr - ai * hi), xi + (ar * hi + ai * hr)
        cr, ci = cr_s[...], ci_s[...]
        tr, ti = pr_ref[...], pi_ref[...]
        outr, outi = [], []
        for g8 in range(CHUNK // 8):
            br, bi = xr[8 * g8:8 * g8 + 8, :], xi[8 * g8:8 * g8 + 8, :]
            br, bi = br + (tr * cr - ti * ci), bi + (tr * ci + ti * cr)
            cr, ci = br[7:8, :], bi[7:8, :]
            outr.append(br)
            outi.append(bi)
        xr, xi = jnp.concatenate(outr, axis=0), jnp.concatenate(outi, axis=0)
        cr_s[...] = cr
        ci_s[...] = ci
        xr_ref[...] = xr
        xi_ref[...] = xi
        y = _dot(_bf(xr), wcr_ref[0]) - _dot(_bf(xi), wci_ref[0]) + d_ref[...] * uv
        y1_ref[...] = y
        g_ref[...] = _bf(_gelu_and_grad(y)[0])

    ns = S5_GROUPS * S5_STATE
    xr, xi, y1, g = _pc(
        body, name="s5_fwd", grid=(S5_GB, nc),
        in_specs=[sp["u"], sp["wb"], sp["wb"], sp["tab"], sp["tab"], sp["step"], sp["step"], sp["wc"], sp["wc"],
                  sp["d"]],
        out_specs=[sp["x"], sp["x"], sp["u"], sp["u"]],
        out_shape=[jax.ShapeDtypeStruct((bsz, tp, ns), F32)] * 2
        + [jax.ShapeDtypeStruct((bsz, tp, S5_WIDTH), F32), jax.ShapeDtypeStruct((bsz, tp, S5_WIDTH), BF16)],
        scratch=[pltpu.VMEM((bsz, 1, S5_LANES), F32)] * 2, vmem=8 << 20,
    )(_seq(u, bsz), wbr, wbi, pr, pi, sr, si, wcr, wci, d)
    return xr.reshape(r, ns), xi.reshape(r, ns), y1.reshape(r, S5_WIDTH), g.reshape(r, S5_WIDTH)


def _s5_post(y1, glu_pre, glu_b, z):
    r, w = y1.shape
    tm = _pick(r, (256, 128))

    def body(y_ref, p_ref, b_ref, z_ref, o_ref):
        g = _gelu_and_grad(y_ref[...])[0]
        o_ref[...] = _bf(g * jax.nn.sigmoid(p_ref[...] + b_ref[...]) * _silu(z_ref[...]))

    row = pl.BlockSpec((tm, w), lambda i: (i, 0))
    return _pc(body, name="s5_post", grid=(r // tm,), in_specs=[row, row, pl.BlockSpec((1, w), lambda i: (0, 0)), row],
               out_specs=row, out_shape=jax.ShapeDtypeStruct((r, w), BF16), vmem=tm * w * 16)(y1, glu_pre, glu_b, z)


def _s5_post_bwd(dya, y1, glu_pre, glu_b, z):
    r, w = y1.shape
    tm = _pick(r, (256, 128))

    def body(dy_ref, y_ref, p_ref, b_ref, z_ref, dz_ref, dp_ref, dg_ref, db_ref):
        @pl.when(pl.program_id(0) == 0)
        def _():
            db_ref[...] = jnp.zeros_like(db_ref)

        g = _gelu_and_grad(y_ref[...])[0]
        s = jax.nn.sigmoid(p_ref[...] + b_ref[...])
        zv = z_ref[...]
        dy = dy_ref[...]
        do = dy * _silu(zv)
        dz_ref[...] = _bf(dy * g * s * _dsilu(zv))
        dp = do * g * s * (1.0 - s)
        dp_ref[...] = _bf(dp)
        db_ref[...] += jnp.sum(dp, axis=0, keepdims=True)
        dg_ref[...] = do * s

    row = pl.BlockSpec((tm, w), lambda i: (i, 0))
    vec = pl.BlockSpec((1, w), lambda i: (0, 0))
    return _pc(body, name="s5_post_bwd", grid=(r // tm,), in_specs=[row, row, row, vec, row],
               out_specs=[row, row, row, vec],
               out_shape=[jax.ShapeDtypeStruct((r, w), BF16), jax.ShapeDtypeStruct((r, w), BF16),
                          jax.ShapeDtypeStruct((r, w), F32), jax.ShapeDtypeStruct((1, w), F32)],
               vmem=tm * w * 24)(dya, y1, glu_pre, glu_b, z)


def _s5_bwd(dg, y1, u, xr, xi, wbr, wbi, qr, qi, sr, si, wcr, wci, d, bsz, nc):
    r = u.shape[0]
    tp = r // bsz
    sp = _s5_scan_specs(bsz, nc, True)

    def body(dg_all, y1_all, u_all, xr_all, xi_all, xpr_all, xpi_all, wbr_ref, wbi_ref, qr_ref, qi_ref, sr_ref, si_ref,
             wcr_ref, wci_ref, d_ref, du_all, dd_ref, dwcr_ref, dwci_ref, dwbr_ref, dwbi_ref, dar_ref, dai_ref,
             cr_sall, ci_sall):
        c = pl.program_id(1)

        @pl.when(c == 0)
        def _():
            for ref in (dd_ref, dwcr_ref, dwci_ref, dwbr_ref, dwbi_ref, dar_ref, dai_ref, cr_sall, ci_sall):
                ref[...] = jnp.zeros_like(ref)

        for bi in range(bsz):
            one(c, dg_all.at[bi], y1_all.at[bi], u_all.at[bi], xr_all.at[bi], xi_all.at[bi], xpr_all.at[bi],
                xpi_all.at[bi], wbr_ref, wbi_ref, qr_ref, qi_ref, sr_ref, si_ref, wcr_ref, wci_ref, d_ref,
                du_all.at[bi], dd_ref, dwcr_ref, dwci_ref, dwbr_ref, dwbi_ref, dar_ref, dai_ref, cr_sall.at[bi],
                ci_sall.at[bi])

    def one(c, dg_ref, y1_ref, u_ref, xr_ref, xi_ref, xpr_ref, xpi_ref, wbr_ref, wbi_ref, qr_ref, qi_ref, sr_ref, si_ref,
            wcr_ref, wci_ref, d_ref, du_ref, dd_ref, dwcr_ref, dwci_ref, dwbr_ref, dwbi_ref, dar_ref, dai_ref,
            cr_s, ci_s):
        uv = u_ref[...]
        ub = _bf(uv)
        dy = dg_ref[...] * _gelu_and_grad(y1_ref[...])[1]
        dd_ref[...] += jnp.sum(dy * uv, axis=0, keepdims=True)
        dyb = _bf(dy)
        xr, xi = xr_ref[...], xi_ref[...]
        dwcr_ref[0] += _dot(_bf(xr), dyb, TN)
        dwci_ref[0] -= _dot(_bf(xi), dyb, TN)
        lr, li = _dot(dyb, wcr_ref[0], NT), -_dot(dyb, wci_ref[0], NT)
        row = lax.broadcasted_iota(jnp.int32, (CHUNK, S5_LANES), 0)
        sub = row % 8
        for k in range(3):
            s = 1 << k
            ar, ai = sr_ref[k:k + 1, :], si_ref[k:k + 1, :]
            hr = jnp.where(sub < 8 - s, pltpu.roll(lr, CHUNK - s, 0), 0.0)
            hi = jnp.where(sub < 8 - s, pltpu.roll(li, CHUNK - s, 0), 0.0)
            lr, li = lr + (ar * hr + ai * hi), li + (ar * hi - ai * hr)
        cr, ci = cr_s[...], ci_s[...]
        tr, ti = qr_ref[...], qi_ref[...]
        outr, outi = [], []
        for g8 in reversed(range(CHUNK // 8)):
            br, bi = lr[8 * g8:8 * g8 + 8, :], li[8 * g8:8 * g8 + 8, :]
            br, bi = br + (tr * cr + ti * ci), bi + (tr * ci - ti * cr)
            cr, ci = br[0:1, :], bi[0:1, :]
            outr.append(br)
            outi.append(bi)
        lr, li = jnp.concatenate(outr[::-1], axis=0), jnp.concatenate(outi[::-1], axis=0)
        cr_s[...] = cr
        ci_s[...] = ci
        lrb, lib = _bf(lr), _bf(li)
        du_ref[...] = _bf(_dot(lrb, wbr_ref[0], NT) + _dot(lib, wbi_ref[0], NT) + dy * d_ref[...])
        dwbr_ref[0] += _dot(ub, lrb, TN)
        dwbi_ref[0] += _dot(ub, lib, TN)
        first = c == nc - 1
        pr0 = jnp.where(first, 0.0, xpr_ref[7:8, :])
        pi0 = jnp.where(first, 0.0, xpi_ref[7:8, :])
        xpr = jnp.where(row == 0, pr0, pltpu.roll(xr, 1, 0))
        xpi = jnp.where(row == 0, pi0, pltpu.roll(xi, 1, 0))
        dar_ref[...] += jnp.sum(lr * xpr + li * xpi, axis=0, keepdims=True)
        dai_ref[...] += jnp.sum(li * xpr - lr * xpi, axis=0, keepdims=True)

    st = jax.ShapeDtypeStruct
    xr3, xi3 = _seq(xr, bsz), _seq(xi, bsz)
    outs = _pc(body, name="s5_bwd", grid=(S5_GB, nc),
               in_specs=[sp["u"], sp["u"], sp["u"], sp["x"], sp["x"], sp["xprev"], sp["xprev"], sp["wb"], sp["wb"],
                         sp["tab"], sp["tab"], sp["step"], sp["step"], sp["wc"], sp["wc"], sp["d"]],
               out_specs=[sp["u"], sp["d"], sp["wc"], sp["wc"], sp["wb"], sp["wb"], sp["lane"], sp["lane"]],
               out_shape=[st((bsz, tp, S5_WIDTH), BF16), st((1, S5_WIDTH), F32),
                          st((S5_GB, S5_LANES, CHUNK), F32), st((S5_GB, S5_LANES, CHUNK), F32),
                          st((S5_GB, CHUNK, S5_LANES), F32), st((S5_GB, CHUNK, S5_LANES), F32),
                          st((1, S5_GROUPS * S5_STATE), F32), st((1, S5_GROUPS * S5_STATE), F32)],
               scratch=[pltpu.VMEM((bsz, 1, S5_LANES), F32)] * 2, vmem=12 << 20,
               )(_seq(dg, bsz), _seq(y1, bsz), _seq(u, bsz), xr3, xi3, xr3, xi3, wbr, wbi, qr, qi, sr, si, wcr, wci, d)
    return (outs[0].reshape(r, S5_WIDTH),) + tuple(outs[1:])


def _s5_layer_fwd(u, prm, glu_w, bsz, nc):
    xr, xi, y1, g = _s5_fwd(u, prm["wbr"], prm["wbi"], prm["pr"], prm["pi"], prm["sr"], prm["si"], prm["wcr"],
                            prm["wci"], prm["d"], bsz, nc)
    glu_pre = _mm(g, glu_w(y1) if callable(glu_w) else glu_w, "NN", name="s5_glu")
    return dict(xr=xr, xi=xi, y1=y1, g=g, glu_pre=glu_pre)


def _s5_layer_bwd(dya, u, z, sv, prm, pvjp, glu_w, glu_b, bsz, nc):
    dz, dglu, dg_direct, dglu_b = _s5_post_bwd(dya, sv["y1"], sv["glu_pre"], glu_b, z)
    dg = _mm(dglu, glu_w, "NT", name="s5_dg", add=dg_direct)
    dglu_w = _mm(sv["g"], dglu, "TN", name="s5_dglu_w")
    du, dd, dwcr, dwci, dwbr, dwbi, dar, dai = _s5_bwd(
        dg, sv["y1"], u, sv["xr"], sv["xi"], prm["wbr"], prm["wbi"], prm["qr"], prm["qi"], prm["sr"], prm["si"],
        prm["wcr"], prm["wci"], prm["d"], bsz, nc)
    dbbr = jnp.swapaxes(_blockdiag_extract(dwbr, S5_GROUP_SIZE, S5_STATE), 1, 2)
    dbbi = jnp.swapaxes(_blockdiag_extract(dwbi, S5_GROUP_SIZE, S5_STATE), 1, 2)
    dlr, dli, dldt, dbr, dbi = pvjp((dar.reshape(S5_GROUPS, S5_STATE), dai.reshape(S5_GROUPS, S5_STATE), dbbr, dbbi))
    grads = dict(
        s5_lambda_re=dlr, s5_lambda_im=dli, s5_log_dt=dldt, s5_b_re=dbr, s5_b_im=dbi,
        s5_c_re=jnp.swapaxes(_blockdiag_extract(dwcr, S5_STATE, S5_GROUP_SIZE), 1, 2),
        s5_c_im=jnp.swapaxes(_blockdiag_extract(dwci, S5_STATE, S5_GROUP_SIZE), 1, 2),
        s5_d=dd, s5_glu_w=dglu_w, s5_glu_b=dglu_b)
    return du, dz, grads


def _s5_tables(lam_re, lam_im, log_dt, b_re, b_im, c_re, c_im, d):
    (ar, ai, bbr, bbi), vjp = jax.vjp(_s5_params, lam_re, lam_im, log_dt, b_re, b_im)
    pr, pi = _s5_power_table(lax.stop_gradient(ar), lax.stop_gradient(ai))
    steps = [0, 1, 3, 7, 7, 7, 7, 7]
    flip8 = (jnp.arange(8)[:, None] + jnp.arange(8)[None, :] == 7).astype(F32)
    prm = dict(
        wbr=_bf(_blockdiag(jnp.swapaxes(bbr, 1, 2), S5_GROUP_SIZE, S5_STATE)),
        wbi=_bf(_blockdiag(jnp.swapaxes(bbi, 1, 2), S5_GROUP_SIZE, S5_STATE)),
        wcr=_bf(_blockdiag(jnp.swapaxes(c_re, 1, 2), S5_STATE, S5_GROUP_SIZE)),
        wci=_bf(_blockdiag(jnp.swapaxes(c_im, 1, 2), S5_STATE, S5_GROUP_SIZE)),
        pr=pr, pi=pi, qr=jnp.dot(flip8, pr, precision=lax.Precision.HIGHEST),
        qi=jnp.dot(flip8, pi, precision=lax.Precision.HIGHEST),
        sr=jnp.concatenate([pr[i:i + 1] for i in steps], axis=0),
        si=jnp.concatenate([pi[i:i + 1] for i in steps], axis=0), d=d.reshape(1, S5_WIDTH))
    return prm, vjp


def _shift_down(x, halo8, s):
    sh = pltpu.roll(x, s, 0)
    r8 = lax.broadcasted_iota(jnp.int32, halo8.shape, 0)
    first = jnp.where(r8 < s, pltpu.roll(halo8, s, 0), sh[:8])
    return jnp.concatenate([first, sh[8:]], axis=0)


def _shift_up(x, halo8, s):
    sh = pltpu.roll(x, CHUNK - s, 0)
    r8 = lax.broadcasted_iota(jnp.int32, halo8.shape, 0)
    last = jnp.where(r8 >= 8 - s, pltpu.roll(halo8, 8 - s, 0), sh[CHUNK - 8:])
    return jnp.concatenate([sh[:CHUNK - 8], last], axis=0)


def _conv_specs(nc, tw):
    def chunk(b, c):
        return b * nc + c

    return dict(
        x=pl.BlockSpec((CHUNK, tw), lambda j, b, c: (chunk(b, c), j)),
        prev=pl.BlockSpec((8, tw), lambda j, b, c: (jnp.maximum(chunk(b, c) * (CHUNK // 8) - 1, 0), j)),
        nxt=pl.BlockSpec((8, tw), lambda j, b, c: ((b * nc + jnp.minimum(c + 1, nc - 1)) * (CHUNK // 8), j)),
        w=pl.BlockSpec((ML_CONV, tw), lambda j, b, c: (0, j)),
        vec=pl.BlockSpec((1, tw), lambda j, b, c: (0, j)),
    )


def _conv_fwd(x, w, bias, bsz, nc, *, name):
    r, wd = x.shape
    tw = _pick(wd, (2048, 1536, 1024, 512, 384, 256, 128))
    sp = _conv_specs(nc, tw)

    def body(x_ref, p_ref, w_ref, b_ref, o_ref):
        c = pl.program_id(2)
        xv = x_ref[...]
        halo = jnp.where(c == 0, 0.0, p_ref[...])
        acc = b_ref[...] + w_ref[3:4, :] * xv
        for s in (1, 2, 3):
            acc = acc + w_ref[3 - s:4 - s, :] * _shift_down(xv, halo, s)
        o_ref[...] = acc

    return _pc(body, name=name, grid=(wd // tw, bsz, nc), in_specs=[sp["x"], sp["prev"], sp["w"], sp["vec"]],
               out_specs=sp["x"], out_shape=jax.ShapeDtypeStruct((r, wd), F32), vmem=CHUNK * tw * 16,
               )(x, x, w, bias.reshape(1, wd))


def _conv_bwd(dpre, x, w, bsz, nc, *, name, add=None):
    r, wd = x.shape
    tw = _pick(wd, (2048, 1536, 1024, 512, 384, 256, 128))
    sp = _conv_specs(nc, tw)

    def body(*refs):
        d_ref, n_ref, x_ref, p_ref, w_ref = refs[:5]
        add_ref = refs[5] if add is not None else None
        dx_ref, dw_ref, db_ref = refs[-3:]
        b, c = pl.program_id(1), pl.program_id(2)

        @pl.when((b == 0) & (c == 0))
        def _():
            dw_ref[...] = jnp.zeros_like(dw_ref)
            db_ref[...] = jnp.zeros_like(db_ref)

        dv, xv = d_ref[...], x_ref[...]
        dhalo = jnp.where(c == nc - 1, 0.0, n_ref[...])
        xhalo = jnp.where(c == 0, 0.0, p_ref[...])
        dx = w_ref[3:4, :] * dv
        for s in (1, 2, 3):
            dx = dx + w_ref[3 - s:4 - s, :] * _shift_up(dv, dhalo, s)
        if add_ref is not None:
            dx = dx + add_ref[...]
        dx_ref[...] = _bf(dx)
        db_ref[...] += jnp.sum(dv, axis=0, keepdims=True)
        dw_ref[3:4, :] += jnp.sum(dv * xv, axis=0, keepdims=True)
        for s in (1, 2, 3):
            dw_ref[3 - s:4 - s, :] += jnp.sum(dv * _shift_down(xv, xhalo, s), axis=0, keepdims=True)

    ins = [dpre, dpre, x, x, w] + ([add] if add is not None else [])
    specs = [sp["x"], sp["nxt"], sp["x"], sp["prev"], sp["w"]] + ([sp["x"]] if add is not None else [])
    return _pc(body, name=name, grid=(wd // tw, bsz, nc), in_specs=specs, out_specs=[sp["x"], sp["w"], sp["vec"]],
               out_shape=[jax.ShapeDtypeStruct((r, wd), BF16), jax.ShapeDtypeStruct((ML_CONV, wd), F32),
                          jax.ShapeDtypeStruct((1, wd), F32)], vmem=CHUNK * tw * 24)(*ins)


ML_SCALE = ML_DH ** -0.5


ML_LB = ML_DH // CHUNK


def _headwise_expand(w):
    tiled = jnp.tile(w.reshape(ML_HEADS, ML_DH, QKV_BLOCK), (1, 1, CHUNK // QKV_BLOCK))
    rblk = (jnp.arange(ML_DH) % CHUNK) // QKV_BLOCK
    cblk = jnp.arange(CHUNK) // QKV_BLOCK
    return jnp.where(rblk[:, None] == cblk[None, :], tiled, 0.0).reshape(ML_HEADS, ML_LB, CHUNK, CHUNK)


def _headwise_dot(x, w_ref, dims=NN):
    return jnp.concatenate([_dot(x[:, j * CHUNK:(j + 1) * CHUNK], w_ref[0, j], dims) for j in range(ML_LB)], axis=1)


def _headwise_extract(w):
    return w[:, :, :QKV_BLOCK].reshape(ML_HEADS * ML_DH // QKV_BLOCK, QKV_BLOCK, QKV_BLOCK)


def _ml_pre(pre, x, wq, wk, wv, wgq, wgk, wgv, bsz, nc):
    r = x.shape[0]
    tr = _pick(r, (256, 128))
    hrow = pl.BlockSpec((tr, ML_DH), lambda i, h: (i, h))
    wexp = pl.BlockSpec((ML_HEADS, ML_LB, CHUNK, CHUNK), lambda i, h: (0, 0, 0, 0))
    wg = pl.BlockSpec((ML_WIDTH, CHUNK), lambda i, h: (0, 0))

    def body(pre_ref, x_ref, wq_ref, wk_ref, wv_ref, gq_ref, gk_ref, gv_ref, qs_ref, k_ref, v_ref, gt_ref):
        h = pl.program_id(1)
        head = pl.ds(h, 1)
        rows = pl.ds(pl.multiple_of(h * ML_DH, ML_DH), ML_DH)
        xcb = _bf(_silu(pre_ref[...]))
        q = _headwise_dot(xcb, wq_ref.at[head])
        k = _headwise_dot(xcb, wk_ref.at[head])
        v = _headwise_dot(_bf(x_ref[...]), wv_ref.at[head])
        qb, kb, vb = _bf(q), _bf(k), _bf(v)
        qs_ref[...] = _bf(q * ML_SCALE)
        k_ref[...] = kb
        v_ref[...] = vb
        part = _dot(qb, gq_ref[rows, :]) + _dot(kb, gk_ref[rows, :]) + _dot(vb, gv_ref[rows, :])

        @pl.when(h == 0)
        def _():
            gt_ref[...] = part

        @pl.when(h > 0)
        def _():
            gt_ref[...] += part

    o = jax.ShapeDtypeStruct((r, ML_WIDTH), BF16)
    return _pc(
        body, name="ml_pre", grid=(r // tr, ML_HEADS),
        in_specs=[hrow, hrow, wexp, wexp, wexp, wg, wg, wg],
        out_specs=[hrow, hrow, hrow, pl.BlockSpec((tr, CHUNK), lambda i, h: (i, 0))],
        out_shape=[o, o, o, jax.ShapeDtypeStruct((r, CHUNK), F32)], vmem=12 << 20,
    )(pre, x, wq, wk, wv, wgq, wgk, wgv)


def _sum_heads(g8, *, name):
    r = g8.shape[1]
    tr = _pick(r, (256, 128))

    def body(g_ref, o_ref):
        acc = g_ref[0]
        for j in range(1, ML_HEADS):
            acc = acc + g_ref[j]
        o_ref[...] = acc

    return _pc(body, name=name, grid=(r // tr,),
               in_specs=[pl.BlockSpec((ML_HEADS, tr, CHUNK), lambda i: (0, i, 0))],
               out_specs=pl.BlockSpec((tr, CHUNK), lambda i: (i, 0)),
               out_shape=jax.ShapeDtypeStruct((r, CHUNK), F32), vmem=2 << 20)(g8)


def _tri(rev):
    r = lax.broadcasted_iota(jnp.int32, (CHUNK, CHUNK), 0)
    c = lax.broadcasted_iota(jnp.int32, (CHUNK, CHUNK), 1)
    return jnp.where((c >= r) if rev else (c <= r), 1.0, 0.0).astype(F32)


def _cumsum_rows(x, row, rev=False):
    for k in range(7):
        s = 1 << k
        if rev:
            x = x + jnp.where(row < CHUNK - s, pltpu.roll(x, CHUNK - s, 0), 0.0)
        else:
            x = x + jnp.where(row >= s, pltpu.roll(x, s, 0), 0.0)
    return x


def _log_sigmoid(x):
    return jnp.minimum(x, 0.0) - jnp.log(1.0 + jnp.exp(-jnp.abs(x)))


def _ml_core(gates, hd, first, m, qs, k, v, cmat, nvec):
    sq = (CHUNK, CHUNK)
    lane = lax.broadcasted_iota(jnp.int32, sq, 1)
    row = lax.broadcasted_iota(jnp.int32, sq, 0)
    igc = jnp.sum(jnp.where(lane == hd, gates, 0.0), axis=1, keepdims=True)
    fpc = jnp.sum(jnp.where(lane == hd + ML_HEADS, gates, 0.0), axis=1, keepdims=True)
    valid = jnp.logical_or(jnp.logical_not(first), row[:, :1] >= PAD_ROWS)
    igc = jnp.where(valid, igc, NEG)
    lfc = jnp.where(valid, _log_sigmoid(fpc), 0.0)
    bcb = _cumsum_rows(jnp.broadcast_to(lfc, sq), row)
    igb = jnp.broadcast_to(igc, sq)
    dm = jnp.where(lane <= row, bcb - (bcb - igb).T, NEG)
    bc = bcb[:, :1]
    inter = bc + m
    mt = jnp.maximum(inter, jnp.max(dm, axis=1, keepdims=True))
    wt = jnp.exp(dm - mt)
    wprev = jnp.exp(inter - mt)
    s0 = _dot(qs, k, NT)
    s = s0 * wt
    cb = _bf(cmat)
    qc = _dot(qs, cb)
    qf = qs.astype(F32)
    qn = jnp.sum(qf * nvec, axis=1, keepdims=True)
    num = _dot(_bf(s), v) + wprev * qc
    den = jnp.sum(s, axis=1, keepdims=True) + wprev * qn
    emt = jnp.exp(-mt)
    dd = jnp.maximum(jnp.abs(den), emt)
    blast = bcb[CHUNK - 1:CHUNK, :1]
    g = blast - bc + igc
    m_new = jnp.maximum(blast + m, jnp.max(g, axis=0, keepdims=True))
    decay = jnp.exp(blast + m - m_new)
    e = jnp.exp(g - m_new)
    kf = k.astype(F32)
    wk = e * kf
    return dict(lane=lane, row=row, fpc=fpc, valid=valid, wt=wt, wprev=wprev, s=s, cb=cb, qc=qc, qf=qf, qn=qn,
                num=num, den=den, emt=emt, dd=dd, m_new=m_new, decay=decay, e=e, kf=kf, wk=wk)


def _ml_headnorm(h):
    mu = jnp.mean(h, axis=1, keepdims=True)
    hc = h - mu
    rstd = lax.rsqrt(jnp.mean(hc * hc, axis=1, keepdims=True) + HEAD_NORM_EPS)
    return hc * rstd, rstd


def _ml_chunk_specs(nc, rev, bsz):
    def cc(c):
        return (nc - 1 - c) if rev else c

    return dict(
        hrow=pl.BlockSpec((bsz, CHUNK, ML_DH), lambda hd, c: (0, cc(c), hd)),
        gates=pl.BlockSpec((bsz, CHUNK, CHUNK), lambda hd, c: (0, cc(c), 0)),
        bias=pl.BlockSpec((1, CHUNK), lambda hd, c: (0, 0)),
        hvec=pl.BlockSpec((1, ML_DH), lambda hd, c: (0, hd)),
        cs=pl.BlockSpec((bsz, 1, ML_DH, ML_DH), lambda hd, c: (0, hd * nc + cc(c), 0, 0)),
        ns=pl.BlockSpec((bsz, 1, 1, ML_DH), lambda hd, c: (0, hd * nc + cc(c), 0, 0)),
        ms=pl.BlockSpec((bsz, 1, 1, CHUNK), lambda hd, c: (0, hd * nc + cc(c), 0, 0)),
        dgates=pl.BlockSpec((1, bsz, CHUNK, CHUNK), lambda hd, c: (hd, 0, cc(c), 0)),
    )


def _seq(a, bsz):
    return a.reshape(bsz, a.shape[0] // bsz, a.shape[1])


def _ml_chunk_fwd(qs, k, v, gates, b_gate, pre, z, nw, sk, bsz, nc):
    r = qs.shape[0]
    tp = r // bsz
    sp = _ml_chunk_specs(nc, False, bsz)

    def body(qs_all, k_all, v_all, gt_all, bg_ref, pre_all, z_all, nw_ref, sk_ref,
             h_all, yb_all, cs_all, ns_all, ms_all, c_sall, n_sall, m_sall):
        hd, c = pl.program_id(0), pl.program_id(1)

        @pl.when(c == 0)
        def _():
            c_sall[...] = jnp.zeros_like(c_sall)
            n_sall[...] = jnp.zeros_like(n_sall)
            m_sall[...] = jnp.zeros_like(m_sall)

        for bi in range(bsz):
            one(hd, c, qs_all.at[bi], k_all.at[bi], v_all.at[bi], gt_all.at[bi], bg_ref, pre_all.at[bi], z_all.at[bi],
                nw_ref, sk_ref, h_all.at[bi], yb_all.at[bi], cs_all.at[bi], ns_all.at[bi], ms_all.at[bi],
                c_sall.at[bi], n_sall.at[bi], m_sall.at[bi])

    def one(hd, c, qs_ref, k_ref, v_ref, gt_ref, bg_ref, pre_ref, z_ref, nw_ref, sk_ref,
            h_ref, yb_ref, cs_ref, ns_ref, ms_ref, c_s, n_s, m_s):
        cmat, nvec, m = c_s[...], n_s[...], m_s[...]
        cs_ref[0] = cmat
        ns_ref[0] = nvec
        ms_ref[0] = jnp.broadcast_to(m, (1, CHUNK))
        v_ = v_ref[...]
        co = _ml_core(gt_ref[...] + bg_ref[...], hd, c == 0, m, qs_ref[...], k_ref[...], v_, cmat, nvec)
        h = co["num"] / co["dd"]
        h_ref[...] = h
        hn, _ = _ml_headnorm(h)
        yb_ref[...] = _bf((hn * nw_ref[...] + sk_ref[...] * _silu(pre_ref[...])) * _silu(z_ref[...]))
        c_s[...] = co["decay"] * cmat + _dot(_bf(co["wk"]), v_, TN)
        n_s[...] = co["decay"] * nvec + jnp.sum(co["wk"], axis=0, keepdims=True)
        m_s[...] = co["m_new"]

    nst = ML_HEADS * nc
    h, yb, cs, ns, ms = _pc(
        body, name="ml_chunk_fwd", grid=(ML_HEADS, nc),
        in_specs=[sp["hrow"]] * 3 + [sp["gates"], sp["bias"], sp["hrow"], sp["hrow"], sp["hvec"], sp["hvec"]],
        out_specs=[sp["hrow"], sp["hrow"], sp["cs"], sp["ns"], sp["ms"]],
        out_shape=[jax.ShapeDtypeStruct((bsz, tp, ML_WIDTH), F32), jax.ShapeDtypeStruct((bsz, tp, ML_WIDTH), BF16),
                   jax.ShapeDtypeStruct((bsz, nst, ML_DH, ML_DH), F32),
                   jax.ShapeDtypeStruct((bsz, nst, 1, ML_DH), F32), jax.ShapeDtypeStruct((bsz, nst, 1, CHUNK), F32)],
        scratch=[pltpu.VMEM((bsz, ML_DH, ML_DH), F32), pltpu.VMEM((bsz, 1, ML_DH), F32),
                 pltpu.VMEM((bsz, 1, 1), F32)],
        vmem=12 << 20)(*[_seq(a, bsz) for a in (qs, k, v, gates)], b_gate, _seq(pre, bsz), _seq(z, bsz), nw, sk)
    return h.reshape(r, ML_WIDTH), yb.reshape(r, ML_WIDTH), cs, ns, ms


def _ml_chunk_bwd(dyb, qs, k, v, gates, b_gate, pre, z, nw, sk, h, cs, ns, ms, bsz, nc, dep=None):
    r = qs.shape[0]
    tp = r // bsz
    sp = _ml_chunk_specs(nc, True, bsz)

    def body(dy_all, qs_all, k_all, v_all, gt_all, bg_ref, pre_all, z_all, nw_ref, sk_ref, h_all, cs_all, ns_all,
             ms_all, dq_all, dk_all, dv_all, dz_all, dxc_all, dgt_all, dnw_ref, dsk_ref, dc_sall, dn_sall):
        hd, c = pl.program_id(0), pl.program_id(1)

        @pl.when(c == 0)
        def _():
            for ref in (dnw_ref, dsk_ref, dc_sall, dn_sall):
                ref[...] = jnp.zeros_like(ref)

        for bi in range(bsz):
            one(hd, c, dy_all.at[bi], qs_all.at[bi], k_all.at[bi], v_all.at[bi], gt_all.at[bi], bg_ref,
                pre_all.at[bi], z_all.at[bi], nw_ref, sk_ref, h_all.at[bi], cs_all.at[bi], ns_all.at[bi],
                ms_all.at[bi], dq_all.at[bi], dk_all.at[bi], dv_all.at[bi], dz_all.at[bi], dxc_all.at[bi],
                dgt_all.at[0, bi], dnw_ref, dsk_ref, dc_sall.at[bi], dn_sall.at[bi])

    def one(hd, c, dy_ref, qs_ref, k_ref, v_ref, gt_ref, bg_ref, pre_ref, z_ref, nw_ref, sk_ref, h_ref, cs_ref, ns_ref,
            ms_ref, dq_ref, dk_ref, dv_ref, dz_ref, dxc_ref, dgt_ref, dnw_ref, dsk_ref, dc_s, dn_s):

        qs, k, v = qs_ref[...], k_ref[...], v_ref[...]
        cmat, nvec, m = cs_ref[0], ns_ref[0], ms_ref[0][:, :1]
        co = _ml_core(gt_ref[...] + bg_ref[...], hd, c == nc - 1, m, qs, k, v, cmat, nvec)
        lane, row = co["lane"], co["row"]
        wt, wprev, s, cb, qf = co["wt"], co["wprev"], co["s"], co["cb"], co["qf"]
        h = h_ref[...]
        hn, rstd = _ml_headnorm(h)
        xc = _silu(pre_ref[...])
        zv = z_ref[...]
        nw, sk = nw_ref[...], sk_ref[...]
        dy = dy_ref[...]
        dz_ref[...] = _bf(dy * (hn * nw + sk * xc) * _dsilu(zv))
        do = dy * _silu(zv)
        dsk_ref[...] += jnp.sum(do * xc, axis=0, keepdims=True)
        dnw_ref[...] += jnp.sum(do * hn, axis=0, keepdims=True)
        dxc_ref[...] = do * sk
        dhn = do * nw
        dh = rstd * (dhn - jnp.mean(dhn, axis=1, keepdims=True) - hn * jnp.mean(dhn * hn, axis=1, keepdims=True))
        rinv = 1.0 / co["dd"]
        dnum = dh * rinv
        ddd = -jnp.sum(dh * h, axis=1, keepdims=True) * rinv
        den = co["den"]
        dden = jnp.where(jnp.abs(den) >= co["emt"], ddd * jnp.sign(den), 0.0)
        dnb = _bf(dnum)
        ds = _dot(dnb, v, NT) + dden
        dv = _dot(_bf(s), dnb, TN)
        dnw_ = _bf(dnum * wprev)
        dwn = dden * wprev
        dqs = _dot(dnw_, cb, NT) + dwn * nvec
        dc_out = _dot(qs, dnw_, TN)
        dn_out = jnp.sum(dwn * qf, axis=0, keepdims=True)
        dwprev = jnp.sum(dnum * co["qc"], axis=1, keepdims=True) + dden * co["qn"]
        ds0 = _bf(ds * wt)
        ddm = ds * s
        dqs = dqs + _dot(ds0, k)
        dk = _dot(ds0, qs, TN)
        colc = jnp.sum(ddm.T, axis=1, keepdims=True)
        dbc = dwprev * wprev + jnp.sum(ddm, axis=1, keepdims=True) - colc
        dig = colc
        dcn, dnn = dc_s[...], dn_s[...]
        dcb = _bf(dcn)
        decay, e, kf, wk = co["decay"], co["e"], co["kf"], co["wk"]
        ddecay = (jnp.sum(jnp.sum(dcn * cmat, axis=1, keepdims=True), axis=0, keepdims=True)
                  + jnp.sum(dnn * nvec, axis=1, keepdims=True))
        dwk = _dot(v, dcb, NT) + dnn
        dv = dv + _dot(_bf(wk), dcb)
        dk = dk + e * dwk
        dg = jnp.sum(dwk * kf, axis=1, keepdims=True) * e
        dblast = ddecay * decay + jnp.sum(dg, axis=0, keepdims=True)
        dbc = dbc - dg + jnp.where(row[:, :1] == CHUNK - 1, dblast, 0.0)
        dig = dig + dg
        dc_s[...] = decay * dcn + dc_out
        dn_s[...] = decay * dnn + dn_out
        dlf = _cumsum_rows(jnp.broadcast_to(dbc, (CHUNK, CHUNK)), row, rev=True)[:, :1]
        dfp = dlf * (1.0 - jax.nn.sigmoid(co["fpc"]))
        dig = jnp.where(co["valid"], dig, 0.0)
        dfp = jnp.where(co["valid"], dfp, 0.0)
        dgt_ref[...] = jnp.where(lane == hd, dig, 0.0) + jnp.where(lane == hd + ML_HEADS, dfp, 0.0)
        dq_ref[...] = _bf(dqs * ML_SCALE)
        dk_ref[...] = _bf(dk)
        dv_ref[...] = _bf(dv)

    ob = jax.ShapeDtypeStruct((bsz, tp, ML_WIDTH), BF16)
    dq, dk, dv, dz, dxc, dgt, dnw, dsk = _pc(
        body, name="ml_chunk_bwd", grid=(ML_HEADS, nc),
        in_specs=[sp["hrow"]] * 4 + [sp["gates"], sp["bias"], sp["hrow"], sp["hrow"], sp["hvec"], sp["hvec"],
                                     sp["hrow"], sp["cs"], sp["ns"], sp["ms"]],
        out_specs=[sp["hrow"]] * 5 + [sp["dgates"], sp["hvec"], sp["hvec"]],
        out_shape=[ob, ob, ob, ob, jax.ShapeDtypeStruct((bsz, tp, ML_WIDTH), F32),
                   jax.ShapeDtypeStruct((ML_HEADS, bsz, tp, CHUNK), F32),
                   jax.ShapeDtypeStruct((1, ML_WIDTH), F32), jax.ShapeDtypeStruct((1, ML_WIDTH), F32)],
        scratch=[pltpu.VMEM((bsz, ML_DH, ML_DH), F32), pltpu.VMEM((bsz, 1, ML_DH), F32)], vmem=16 << 20, dep=dep,
    )(*[_seq(a, bsz) for a in (dyb, qs, k, v, gates)], b_gate, _seq(pre, bsz), _seq(z, bsz), nw, sk, _seq(h, bsz),
      cs, ns, ms)
    return (dq.reshape(r, ML_WIDTH), dk.reshape(r, ML_WIDTH), dv.reshape(r, ML_WIDTH), dz.reshape(r, ML_WIDTH),
            dxc.reshape(r, ML_WIDTH), dgt.reshape(ML_HEADS, r, CHUNK), dnw, dsk)


def _ml_pre_bwd(dq, dk, dv, dgates, dxc_skip, pre, x, q, k, v, wq, wk, wv, wgq, wgk, wgv, bsz, nc):
    r = x.shape[0]
    tr = _pick(r, (256, 128))
    nt = r // tr
    hrow = pl.BlockSpec((tr, ML_DH), lambda h, i: (i, h))
    wexp = pl.BlockSpec((1, ML_LB, CHUNK, CHUNK), lambda h, i: (h, 0, 0, 0))
    wcmp = pl.BlockSpec((1, ML_DH, CHUNK), lambda h, i: (h, 0, 0))
    wg = pl.BlockSpec((ML_DH, CHUNK), lambda h, i: (h, 0))
    dgs = pl.BlockSpec((tr, CHUNK), lambda h, i: (i, 0))
    bgs = pl.BlockSpec((1, 1, CHUNK), lambda h, i: (h, 0, 0))

    def body(dq_ref, dk_ref, dv_ref, dg_ref, dxs_ref, pre_ref, x_ref, q_ref, k_ref, v_ref, wq_ref, wk_ref, wv_ref,
             gq_ref, gk_ref, gv_ref, dpre_ref, dxv_ref, cq_ref, ck_ref, cv_ref, dgq_ref, dgk_ref, dgv_ref, dbg_ref,
             dwq_ref, dwk_ref, dwv_ref):
        i = pl.program_id(1)

        @pl.when(i == 0)
        def _():
            for ref in (dwq_ref, dwk_ref, dwv_ref, dgq_ref, dgk_ref, dgv_ref, dbg_ref):
                ref[...] = jnp.zeros_like(ref)

        dgt = dg_ref[...]
        dbg_ref[0] += jnp.sum(dgt, axis=0, keepdims=True)
        dgb = _bf(dgt)
        dqt = _bf(dq_ref[...].astype(F32) + _dot(dgb, gq_ref[...], NT))
        dkt = _bf(dk_ref[...].astype(F32) + _dot(dgb, gk_ref[...], NT))
        dvt = _bf(dv_ref[...].astype(F32) + _dot(dgb, gv_ref[...], NT))
        dgq_ref[...] += _dot(q_ref[...], dgb, TN)
        dgk_ref[...] += _dot(k_ref[...], dgb, TN)
        dgv_ref[...] += _dot(v_ref[...], dgb, TN)
        prev = pre_ref[...]
        xcb = _bf(_silu(prev))
        xb = _bf(x_ref[...])
        for j in range(ML_LB):
            sl = slice(j * CHUNK, (j + 1) * CHUNK)
            dwq_ref[j] += _dot(xcb[:, sl], dqt[:, sl], TN)
            dwk_ref[j] += _dot(xcb[:, sl], dkt[:, sl], TN)
            dwv_ref[j] += _dot(xb[:, sl], dvt[:, sl], TN)
        dxc = _headwise_dot(dqt, wq_ref, NT) + _headwise_dot(dkt, wk_ref, NT) + dxs_ref[...]
        dpre_ref[...] = dxc * _dsilu(prev)
        dxv_ref[...] = _headwise_dot(dvt, wv_ref, NT)

        @pl.when(i == nt - 1)
        def _():
            rr = lax.broadcasted_iota(jnp.int32, (CHUNK, CHUNK), 0)
            cc = lax.broadcasted_iota(jnp.int32, (CHUNK, CHUNK), 1)
            diag = rr // QKV_BLOCK == cc // QKV_BLOCK
            fold = jnp.where(rr % QKV_BLOCK == cc, 1.0, 0.0).astype(F32)
            for src, dst in ((dwq_ref, cq_ref), (dwk_ref, ck_ref), (dwv_ref, cv_ref)):
                for j in range(ML_LB):
                    dst[0, j * CHUNK:(j + 1) * CHUNK, :] = jnp.dot(
                        jnp.where(diag, src[j], 0.0), fold, precision=HI, preferred_element_type=F32)

    f = jax.ShapeDtypeStruct((r, ML_WIDTH), F32)
    wc = jax.ShapeDtypeStruct((ML_HEADS, ML_DH, CHUNK), F32)
    wgs = jax.ShapeDtypeStruct((ML_WIDTH, CHUNK), F32)
    return _pc(body, name="ml_pre_bwd", grid=(ML_HEADS, nt),
               in_specs=[hrow, hrow, hrow, dgs, hrow, hrow, hrow, hrow, hrow, hrow, wexp, wexp, wexp, wg, wg, wg],
               out_specs=[hrow, hrow, wcmp, wcmp, wcmp, wg, wg, wg, bgs],
               out_shape=[f, f, wc, wc, wc, wgs, wgs, wgs, jax.ShapeDtypeStruct((ML_HEADS, 1, CHUNK), F32)],
               scratch=[pltpu.VMEM((ML_LB, CHUNK, CHUNK), F32)] * 3,
               vmem=8 << 20)(dq, dk, dv, dgates, dxc_skip, pre, x, q, k, v, wq, wk, wv, wgq, wgk, wgv)


def _pad_lanes(w):
    return jnp.pad(w, ((0, 0), (0, CHUNK - w.shape[1])))


def _ml_weights(conv_w, conv_b, wq, wk, wv, w_gate, b_gate, norm_w, skip):
    return dict(
        conv_w=conv_w, conv_b=conv_b,
        wq=_bf(_headwise_expand(wq)), wk=_bf(_headwise_expand(wk)), wv=_bf(_headwise_expand(wv)),
        wgq=_bf(_pad_lanes(w_gate[:ML_WIDTH])), wgk=_bf(_pad_lanes(w_gate[ML_WIDTH:2 * ML_WIDTH])),
        wgv=_bf(_pad_lanes(w_gate[2 * ML_WIDTH:])), b_gate=_pad_lanes(b_gate.reshape(1, -1)),
        norm=norm_w.reshape(1, ML_WIDTH), skip=skip.reshape(1, ML_WIDTH))


def _ml_layer_fwd(x, z, w, bsz, nc):
    pre = _conv_fwd(x, w["conv_w"], w["conv_b"], bsz, nc, name="ml_conv")
    qs, k, v, gates = _ml_pre(pre, x, w["wq"], w["wk"], w["wv"], w["wgq"], w["wgk"], w["wgv"], bsz, nc)
    h, yb, cs, ns, ms = _ml_chunk_fwd(qs, k, v, gates, w["b_gate"], pre, z, w["norm"], w["skip"], bsz, nc)
    return yb, dict(pre=pre, qs=qs, k=k, v=v, gates=gates, h=h, cs=cs, ns=ns, ms=ms)


def _ml_layer_bwd(dyb, x, z, sv, w, bsz, nc, dep=None):
    dq, dk, dv, dz, dxc, dgates, dnw, dsk = _ml_chunk_bwd(
        dyb, sv["qs"], sv["k"], sv["v"], sv["gates"], w["b_gate"], sv["pre"], z, w["norm"], w["skip"], sv["h"],
        sv["cs"], sv["ns"], sv["ms"], bsz, nc, dep=dep)
    dpre, dxv, dwq, dwk, dwv, dgq, dgk, dgv, dbg = _ml_pre_bwd(
        dq, dk, dv, _sum_heads(dgates, name="ml_dgates_sum"), dxc, sv["pre"], x, sv["qs"], sv["k"], sv["v"], w["wq"],
        w["wk"], w["wv"], w["wgq"], w["wgk"], w["wgv"], bsz, nc)
    dx, dcw, dcb = _conv_bwd(dpre, x, w["conv_w"], bsz, nc, name="ml_conv_bwd", add=dxv)
    ng = 2 * ML_HEADS
    grads = dict(
        ml_conv_w=dcw, ml_conv_b=dcb, ml_wq=_headwise_extract(dwq), ml_wk=_headwise_extract(dwk),
        ml_wv=_headwise_extract(dwv),
        ml_w_gate=jnp.concatenate([dgq[:, :ng] * (1.0 / ML_SCALE), dgk[:, :ng], dgv[:, :ng]], axis=0),
        ml_b_gate=dbg[0][:, :ng], ml_norm=dnw, ml_skip=dsk)
    return dx, dz, grads


HI = lax.Precision.HIGHEST


def _softplus(x):
    return jnp.maximum(x, 0.0) + jnp.log(1.0 + jnp.exp(-jnp.abs(x)))


def _lane_cumsum(x, lane, rev=False):
    del lane
    return _dot_terms(x, _tri(not rev), NN, exact_rhs=True, terms=3)


def _dot_terms(lhs, rhs, dims, *, exact_rhs, terms):
    x = lhs if exact_rhs else rhs
    sel = _bf(rhs if exact_rhs else lhs)
    acc = None
    for _ in range(terms):
        piece = _bf(x)
        part = _dot(piece, sel, dims) if exact_rhs else _dot(sel, piece, dims)
        acc = part if acc is None else acc + part
        x = x - piece.astype(F32)
    return acc


def _head_sum_matrix():
    r = lax.broadcasted_iota(jnp.int32, (SSD_HPG, SSD_GW), 0)
    l = lax.broadcasted_iota(jnp.int32, (SSD_HPG, SSD_GW), 1)
    return jnp.where(l // SSD_P == r, 1.0, 0.0).astype(F32)


def _ssd_dt_specs(nc):
    return dict(rows=pl.BlockSpec((1, SSD_HEADS, CHUNK), lambda b, c: (b, 0, c)),
                col=pl.BlockSpec((SSD_HEADS, 1), lambda b, c: (0, 0)),
                acc=pl.BlockSpec((SSD_HEADS, CHUNK), lambda b, c: (0, 0)))


def _ssd_dt_valid(c):
    lane = lax.broadcasted_iota(jnp.int32, (SSD_HEADS, CHUNK), 1)
    return jnp.logical_or(c > 0, lane >= PAD_ROWS)


def _ssd_dt_prep(dt_raw, dt_bias, a_log, bsz, nc):
    sp = _ssd_dt_specs(nc)

    def body(raw_ref, b_ref, al_ref, dt_ref, cum_ref):
        dt = jnp.where(_ssd_dt_valid(pl.program_id(1)), _softplus(raw_ref[0] + b_ref[...]), 0.0)
        dt_ref[0] = dt
        cum_ref[0] = _lane_cumsum(dt * -jnp.exp(al_ref[...]), None)

    o = jax.ShapeDtypeStruct(dt_raw.shape, F32)
    return _pc(body, name="ssd_dt_prep", grid=(bsz, nc), in_specs=[sp["rows"], sp["col"], sp["col"]],
               out_specs=[sp["rows"], sp["rows"]], out_shape=[o, o], vmem=1 << 20)(dt_raw, dt_bias, a_log)


def _ssd_dt_post(dcum, ddt, dt_raw, dt_bias, a_log, bsz, nc):
    sp = _ssd_dt_specs(nc)

    def body(dcum_ref, ddt_ref, raw_ref, b_ref, al_ref, out_ref, dbias_ref, dal_ref):
        b, c = pl.program_id(0), pl.program_id(1)

        @pl.when((b == 0) & (c == 0))
        def _():
            dbias_ref[...] = jnp.zeros_like(dbias_ref)
            dal_ref[...] = jnp.zeros_like(dal_ref)

        valid = _ssd_dt_valid(c)
        pre = raw_ref[0] + b_ref[...]
        dt = jnp.where(valid, _softplus(pre), 0.0)
        a = -jnp.exp(al_ref[...])
        dda = _lane_cumsum(dcum_ref[0], None, rev=True)
        ddt_raw = jnp.where(valid, ddt_ref[0] + dda * a, 0.0) * jax.nn.sigmoid(pre)
        out_ref[0] = ddt_raw
        dbias_ref[...] += jnp.sum(ddt_raw, axis=1, keepdims=True)
        dal_ref[...] += jnp.sum(dda * dt, axis=1, keepdims=True) * a

    acc = jax.ShapeDtypeStruct((SSD_HEADS, CHUNK), F32)
    return _pc(body, name="ssd_dt_post", grid=(bsz, nc),
               in_specs=[sp["rows"], sp["rows"], sp["rows"], sp["col"], sp["col"]],
               out_specs=[sp["rows"], sp["acc"], sp["acc"]],
               out_shape=[jax.ShapeDtypeStruct(dt_raw.shape, F32), acc, acc], vmem=1 << 20,
               )(dcum, ddt, dt_raw, dt_bias, a_log)


def _ssd_core(xs, bm, cm, dt, cum):
    sq = (CHUNK, CHUNK)
    lane8 = lax.broadcasted_iota(jnp.int32, (SSD_HPG, CHUNK), 1)
    lane = lax.broadcasted_iota(jnp.int32, sq, 1)
    row = lax.broadcasted_iota(jnp.int32, sq, 0)
    low = lane < SSD_P
    cb = _dot(_bf(cm), _bf(bm), NT)
    heads = []
    for r in range(SSD_HPG):
        rowb = jnp.broadcast_to(cum[r:r + 1, :], sq)
        colb = rowb.T
        seg = jnp.exp(jnp.where(lane <= row, colb - rowb, NEG))
        dtrow = jnp.broadcast_to(dt[r:r + 1, :], sq)
        lastb = colb[CHUNK - 1:CHUNK, :]
        heads.append(dict(seg=seg, dtrow=dtrow, w=cb * seg * dtrow, ecol=jnp.exp(colb),
                          dec=jnp.exp(lastb - colb) * dtrow.T, elast=jnp.exp(lastb)))

    def pairs(key):
        return jnp.concatenate([jnp.where(low[:heads[0][key].shape[0]], heads[2 * j][key], heads[2 * j + 1][key])
                                for j in range(SSD_HPG // 2)], axis=1)

    return dict(lane8=lane8, low=low, dt=dt, cum=cum, cb=cb, heads=heads,
                expc=pairs("ecol"), dec=pairs("dec"), elast=pairs("elast"))


def _ssd_specs(nc, rev, bsz):
    def cc(c):
        return (nc - 1 - c) if rev else c

    return dict(
        wide=pl.BlockSpec((bsz, CHUNK, SSD_GW), lambda g, c: (0, cc(c), g)),
        narrow=pl.BlockSpec((bsz, CHUNK, SSD_N), lambda g, c: (0, cc(c), g)),
        dtT=pl.BlockSpec((bsz, SSD_HPG, CHUNK), lambda g, c: (0, g, cc(c))),
        hcol=pl.BlockSpec((SSD_HPG, 1), lambda g, c: (g, 0)),
        hacc=pl.BlockSpec((SSD_HPG, CHUNK), lambda g, c: (g, 0)),
        gvec=pl.BlockSpec((1, SSD_GW), lambda g, c: (0, g)),
        state=pl.BlockSpec((bsz, 1, SSD_N, SSD_GW), lambda g, c: (0, g * nc + cc(c), 0, 0)),
    )


def _ssd_chunk_fwd(xs_pre, bm_pre, cm_pre, dt, cum, d_exp, z, gnorm, bsz, nc):
    tp = xs_pre.shape[1]
    sp = _ssd_specs(nc, False, bsz)

    def body(xs_all, bm_all, cm_all, dt_all, cum_all, d_ref, z_all, gn_ref, y_all, yn_all, st_all, st_sall):
        @pl.when(pl.program_id(1) == 0)
        def _():
            st_sall[...] = jnp.zeros_like(st_sall)

        for bi in range(bsz):
            one(xs_all.at[bi], bm_all.at[bi], cm_all.at[bi], dt_all.at[bi], cum_all.at[bi], d_ref, z_all.at[bi],
                gn_ref, y_all.at[bi], yn_all.at[bi], st_all.at[bi], st_sall.at[bi])

    def one(xs_ref, bm_ref, cm_ref, dt_ref, cum_ref, d_ref, z_ref, gn_ref, y_ref, yn_ref, st_ref, st_s):
        state = st_s[...]
        st_ref[0] = state
        xs, bm, cm = _silu(xs_ref[...]), _silu(bm_ref[...]), _silu(cm_ref[...])
        co = _ssd_core(xs, bm, cm, dt_ref[...], cum_ref[...])
        low, hd = co["low"], co["heads"]
        ys = []
        for j in range(SSD_HPG // 2):
            xp = xs[:, j * CHUNK:(j + 1) * CHUNK]
            lhs = jnp.concatenate([hd[2 * j]["w"], hd[2 * j + 1]["w"]], axis=1)
            rhs = jnp.concatenate([jnp.where(low, xp, 0.0), jnp.where(low, 0.0, xp)], axis=0)
            ys.append(_dot(_bf(lhs), _bf(rhs)))
        cmb = _bf(cm)
        y = jnp.concatenate(ys, axis=1) + co["expc"] * _dot(cmb, _bf(state)) + d_ref[...] * xs
        y_ref[...] = y
        yg = y * _silu(z_ref[...])
        rstd = lax.rsqrt(jnp.mean(yg * yg, axis=1, keepdims=True) + NORM_EPS)
        yn_ref[...] = _bf(yg * rstd * gn_ref[...])
        st_s[...] = co["elast"] * state + _dot(_bf(bm), _bf(xs * co["dec"]), TN)

    return _pc(body, name="ssd_chunk_fwd", grid=(SSD_GROUPS, nc),
               in_specs=[sp["wide"], sp["narrow"], sp["narrow"], sp["dtT"], sp["dtT"], sp["gvec"], sp["wide"],
                         sp["gvec"]],
               out_specs=[sp["wide"], sp["wide"], sp["state"]],
               out_shape=[jax.ShapeDtypeStruct((bsz, tp, SSD_INNER), F32),
                          jax.ShapeDtypeStruct((bsz, tp, SSD_INNER), BF16),
                          jax.ShapeDtypeStruct((bsz, SSD_GROUPS * nc, SSD_N, SSD_GW), F32)],
               scratch=[pltpu.VMEM((bsz, SSD_N, SSD_GW), F32)], vmem=12 << 20,
               )(xs_pre, bm_pre, cm_pre, dt, cum, d_exp, z, gnorm)


def _ssd_chunk_bwd(dyn, xs_pre, bm_pre, cm_pre, dt, cum, d_exp, z, gnorm, y, states, bsz, nc):
    tp = xs_pre.shape[1]
    sp = _ssd_specs(nc, True, bsz)

    def body(dyn_all, xs_all, bm_all, cm_all, dt_all, cum_all, d_ref, z_all, gn_ref, y_all, st_all,
             dxs_all, dbm_all, dcm_all, dz_all, dcum_all, ddt_all, dgn_ref, dd_ref, ds_sall):
        @pl.when(pl.program_id(1) == 0)
        def _():
            for ref in (dgn_ref, dd_ref, ds_sall):
                ref[...] = jnp.zeros_like(ref)

        for bi in range(bsz):
            one(dyn_all.at[bi], xs_all.at[bi], bm_all.at[bi], cm_all.at[bi], dt_all.at[bi], cum_all.at[bi], d_ref,
                z_all.at[bi], gn_ref, y_all.at[bi], st_all.at[bi], dxs_all.at[bi], dbm_all.at[bi], dcm_all.at[bi],
                dz_all.at[bi], dcum_all.at[bi], ddt_all.at[bi], dgn_ref, dd_ref, ds_sall.at[bi])

    def one(dyn_ref, xs_ref, bm_ref, cm_ref, dt_ref, cum_ref, d_ref, z_ref, gn_ref, y_ref, st_ref,
            dxs_ref, dbm_ref, dcm_ref, dz_ref, dcum_ref, ddt_ref, dgn_ref, dd_ref, ds_s):
        xs_p, bm_p, cm_p = xs_ref[...], bm_ref[...], cm_ref[...]
        xs, bm, cm = _silu(xs_p), _silu(bm_p), _silu(cm_p)
        state = st_ref[0]
        co = _ssd_core(xs, bm, cm, dt_ref[...], cum_ref[...])
        low, hd, lane8, cb = co["low"], co["heads"], co["lane8"], co["cb"]
        dt, cum = co["dt"], co["cum"]
        sub8 = lax.broadcasted_iota(jnp.int32, (SSD_HPG, CHUNK), 0)
        eh = _head_sum_matrix()

        def head_rows(full):
            return _dot_terms(eh, full, NT, exact_rhs=False, terms=2)

        def head_col(vec):
            return jnp.sum(eh * vec, axis=1, keepdims=True)

        yv, zv, gn = y_ref[...], z_ref[...], gn_ref[...]
        sz = _silu(zv)
        yg = yv * sz
        rstd = lax.rsqrt(jnp.mean(yg * yg, axis=1, keepdims=True) + NORM_EPS)
        yh = yg * rstd
        dyn = dyn_ref[...]
        dgn_ref[...] += jnp.sum(dyn * yh, axis=0, keepdims=True)
        dyh = dyn * gn
        dyg = rstd * (dyh - yh * jnp.mean(dyh * yh, axis=1, keepdims=True))
        dz_ref[...] = _bf(dyg * yv * _dsilu(zv))
        dy = dyg * sz
        dxs = dy * d_ref[...]
        dd_ref[...] += head_col(jnp.sum(dy * xs, axis=0, keepdims=True))
        cmb, bmb, stb = _bf(cm), _bf(bm), _bf(state)
        ysv = _dot(cmb, stb)
        expc = co["expc"]
        dys = _bf(dy * expc)
        dcum = head_rows(dy * ysv * expc)
        dcm = _dot(dys, stb, NT)
        dstate_out = _dot(cmb, dys, TN)
        dcb = jnp.zeros((CHUNK, CHUNK), F32)
        ddt = jnp.zeros((SSD_HPG, CHUNK), F32)
        dxs_pairs = []
        for j in range(SSD_HPG // 2):
            sl = slice(j * CHUNK, (j + 1) * CHUNK)
            dyp, xp = dy[:, sl], _bf(xs[:, sl])
            lhs = _bf(jnp.concatenate([hd[2 * j]["w"], hd[2 * j + 1]["w"]], axis=1))
            both = _dot(lhs, _bf(dyp), TN)
            dxs_pairs.append(jnp.where(low, both[:CHUNK], both[CHUNK:]))
            for q, msk in ((2 * j, low), (2 * j + 1, jnp.logical_not(low))):
                h = hd[q]
                dw = _dot(_bf(jnp.where(msk, dyp, 0.0)), xp, NT)
                dcb = dcb + dw * h["seg"] * h["dtrow"]
                e_ = dw * h["w"]
                dcum_r = jnp.sum(e_.T, axis=0, keepdims=True) - jnp.sum(e_, axis=0, keepdims=True)
                ddt_r = jnp.sum(dw * cb * h["seg"], axis=0, keepdims=True)
                dcum = dcum + jnp.where(sub8 == q, dcum_r, 0.0)
                ddt = ddt + jnp.where(sub8 == q, ddt_r, 0.0)
        dxs = dxs + jnp.concatenate(dxs_pairs, axis=1)
        dcbb = _bf(dcb)
        dcm = dcm + _dot(dcbb, bmb)
        dbm = _dot(dcbb, cmb, TN)
        dsn = ds_s[...]
        dsb = _bf(dsn)
        dec = co["dec"]
        dbm = dbm + _dot(_bf(xs * dec), dsb, NT)
        dxd = _dot(bmb, dsb)
        dxs = dxs + dxd * dec
        ddec = head_rows(dxd * xs)
        last = cum[:, CHUNK - 1:CHUNK]
        erow = jnp.exp(last - cum)
        ddt = ddt + ddec * erow
        dla = ddec * erow * dt
        dlast = (jnp.sum(dla, axis=1, keepdims=True)
                 + head_col(jnp.sum(dsn * state, axis=0, keepdims=True)) * jnp.exp(last))
        dcum_ref[...] = dcum - dla + jnp.where(lane8 == CHUNK - 1, dlast, 0.0)
        ddt_ref[...] = ddt
        ds_s[...] = co["elast"] * dsn + dstate_out
        dxs_ref[...] = dxs * _dsilu(xs_p)
        dbm_ref[...] = dbm * _dsilu(bm_p)
        dcm_ref[...] = dcm * _dsilu(cm_p)

    st = jax.ShapeDtypeStruct
    hacc = st((SSD_HEADS, CHUNK), F32)
    return _pc(body, name="ssd_chunk_bwd", grid=(SSD_GROUPS, nc),
               in_specs=[sp["wide"], sp["wide"], sp["narrow"], sp["narrow"], sp["dtT"], sp["dtT"], sp["gvec"],
                         sp["wide"], sp["gvec"], sp["wide"], sp["state"]],
               out_specs=[sp["wide"], sp["narrow"], sp["narrow"], sp["wide"], sp["dtT"], sp["dtT"], sp["gvec"],
                          sp["hacc"]],
               out_shape=[st((bsz, tp, SSD_INNER), F32), st((bsz, tp, SSD_BC), F32), st((bsz, tp, SSD_BC), F32),
                          st((bsz, tp, SSD_INNER), BF16), st((bsz, SSD_HEADS, tp), F32),
                          st((bsz, SSD_HEADS, tp), F32), st((1, SSD_INNER), F32), hacc],
               scratch=[pltpu.VMEM((bsz, SSD_N, SSD_GW), F32)], vmem=20 << 20,
               )(dyn, xs_pre, bm_pre, cm_pre, dt, cum, d_exp, z, gnorm, y, states)


SSD_BC = SSD_GROUPS * SSD_N


def _ssd_weights(conv_w, conv_b, dt_bias, a_log, d, gnorm):
    cuts = (0, SSD_INNER, SSD_INNER + SSD_BC, SSD_INNER + 2 * SSD_BC)
    return dict(
        conv_w=[conv_w[:, cuts[i]:cuts[i + 1]] for i in range(3)],
        conv_b=[conv_b[cuts[i]:cuts[i + 1]] for i in range(3)],
        dt_bias=dt_bias.reshape(SSD_HEADS, 1), a_log=a_log.reshape(SSD_HEADS, 1),
        d_exp=jnp.repeat(d.reshape(SSD_HEADS), SSD_P).reshape(1, SSD_INNER), gnorm=gnorm.reshape(1, SSD_INNER))


def _ssd_layer_fwd(z, xs_in, bm_in, cm_in, dt_rows, w, bsz, nc):
    pres = [_conv_fwd(a, w["conv_w"][i], w["conv_b"][i], bsz, nc, name=f"ssd_conv{i}")
            for i, a in enumerate((xs_in, bm_in, cm_in))]
    def seq(a):
        return a.reshape(bsz, nc * CHUNK, a.shape[-1])

    dt_t = jnp.swapaxes(seq(dt_rows)[:, :, :SSD_HEADS], 1, 2)
    dt, cum = _ssd_dt_prep(dt_t, w["dt_bias"], w["a_log"], bsz, nc)
    y, yn, states = _ssd_chunk_fwd(seq(pres[0]), seq(pres[1]), seq(pres[2]), dt, cum, w["d_exp"], seq(z),
                                   w["gnorm"], bsz, nc)
    return yn.reshape(-1, SSD_INNER), dict(pres=pres, dt_t=dt_t, dt=dt, cum=cum, y=y, states=states)


def _ssd_layer_bwd(dyn, z, xs_in, bm_in, cm_in, sv, w, bsz, nc):
    pres = sv["pres"]

    def seq(a):
        return a.reshape(bsz, nc * CHUNK, a.shape[-1])

    def rows(a):
        return a.reshape(-1, a.shape[-1])

    dxs_p, dbm_p, dcm_p, dz, dcum, ddt_direct, dgn, dd = _ssd_chunk_bwd(
        seq(dyn), seq(pres[0]), seq(pres[1]), seq(pres[2]), sv["dt"], sv["cum"], w["d_exp"], seq(z), w["gnorm"],
        sv["y"], sv["states"], bsz, nc)
    ddt_t, dbias, dal = _ssd_dt_post(dcum, ddt_direct, sv["dt_t"], w["dt_bias"], w["a_log"], bsz, nc)
    dz = rows(dz)
    outs = [_conv_bwd(rows(dp), a, w["conv_w"][i], bsz, nc, name=f"ssd_conv_bwd{i}")
            for i, (dp, a) in enumerate(((dxs_p, xs_in), (dbm_p, bm_in), (dcm_p, cm_in)))]
    ddt = _bf(_pad_lanes(rows(jnp.swapaxes(ddt_t, 1, 2))))
    grads = dict(
        ssd_conv_w=jnp.concatenate([o[1] for o in outs], axis=1),
        ssd_conv_b=jnp.concatenate([o[2] for o in outs], axis=1),
        ssd_dt_bias=dbias[:, 0], ssd_a_log=dal[:, 0], ssd_d=dd[:, 0], ssd_gnorm=dgn)
    return dz, outs[0][0], outs[1][0], outs[2][0], ddt, grads


WNAMES = ("meta_tokens", "ab_norm", "ab_w_in", "s5_lambda_re", "s5_lambda_im", "s5_log_dt", "s5_b_re", "s5_b_im",
          "s5_c_re", "s5_c_im", "s5_d", "s5_glu_w", "s5_glu_b", "ml_conv_w", "ml_conv_b", "ml_wq", "ml_wk", "ml_wv",
          "ml_w_gate", "ml_b_gate", "ml_norm", "ml_skip", "ab_w_out", "ssd_norm", "ssd_w_in", "ssd_conv_w",
          "ssd_conv_b", "ssd_dt_bias", "ssd_a_log", "ssd_d", "ssd_gnorm", "ssd_w_out", "final_norm")
SHARD_AXIS = dict(meta_tokens=1, ab_w_in=2, s5_glu_w=1, ml_conv_w=2, ml_wq=1, ml_wk=1, ml_wv=1, ml_w_gate=1,
                  ab_w_out=1, ssd_norm=1, ssd_w_in=2, ssd_conv_w=2, ssd_conv_b=1, ssd_gnorm=1, ssd_w_out=1)
BIG = ("ab_w_in", "s5_glu_w", "ab_w_out", "ssd_w_in", "ssd_w_out")
SMALL = tuple(n for n in WNAMES if n in SHARD_AXIS and n not in BIG)
REPL = tuple(n for n in WNAMES if n not in SHARD_AXIS)
PACK_ALIGN = 8 * 128


def _pack(arrs):
    lead = arrs[0][1]
    parts = []
    for a, nlead in arrs:
        f = a.reshape(a.shape[:nlead] + (-1,))
        parts.append(jnp.pad(f, [(0, 0)] * nlead + [(0, (-f.shape[-1]) % PACK_ALIGN)]))
    flat = jnp.concatenate(parts, axis=lead)
    return flat.reshape(flat.shape[:lead] + (-1, 128))


def _unpack(p, shapes):
    out, off = [], 0
    lead = p.shape[:-2]
    flat = p.reshape(lead + (-1,))
    for s in shapes:
        n = math.prod(s)
        out.append(flat[..., off:off + n].reshape(lead + tuple(s)))
        off += -(-n // PACK_ALIGN) * PACK_ALIGN
    return out


def _assemble(g, axis):
    m = jnp.moveaxis(g, 0, axis)
    return m.reshape(m.shape[:axis] + (m.shape[axis] * m.shape[axis + 1],) + m.shape[axis + 2:])


def _split(full, axis):
    s = full.shape
    m = full.reshape(s[:axis] + (N_DEV, s[axis] // N_DEV) + s[axis + 1:])
    return jnp.moveaxis(m, axis, 0)


def kernel(x, *rest):
    nw = len(WNAMES)
    w = dict(zip(WNAMES, rest[:nw]))
    loss_target = rest[nw]
    mom = dict(zip(WNAMES, rest[nw + 1:2 * nw + 1]))
    var = dict(zip(WNAMES, rest[2 * nw + 1:3 * nw + 1]))
    bsz = x.shape[0]
    nc = 1 + SEQ // CHUNK
    tp = nc * CHUNK

    local = {n: _bf(w[n][0]) for n in BIG}
    small_local = _pack([(w[n], 0) for n in SMALL])
    gs = _exchange_start([small_local], ["ag"], name="gather_s")
    ga = _exchange_start([local["ab_w_in"]], ["ag"], name="gather_a", dep=gs["token"], peers=SAME_CORE[1:])
    got_s = _exchange_wait(gs, ga["token"])

    def assemble_big(n, got):
        return _assemble(got[:, None], SHARD_AXIS[n])[0]

    full = {}
    for n, g in zip(SMALL, _unpack(got_s[0], [w[n].shape for n in SMALL])):
        full[n] = _assemble(g, SHARD_AXIS[n])[0] if n != "meta_tokens" else _assemble(g, SHARD_AXIS[n])
    for n in REPL:
        full[n] = w[n][0] if n != "final_norm" else w[n]
    glu_b = full["s5_glu_b"].reshape(1, S5_WIDTH)
    meta = jnp.broadcast_to(full["meta_tokens"][None], (bsz, N_META, D_MODEL))
    h0 = jnp.concatenate([jnp.zeros((bsz, PAD_ROWS, D_MODEL), F32), meta, x], axis=1).reshape(bsz * tp, D_MODEL)
    xn0 = _rms_fwd(h0, full["ab_norm"], name="rms0")
    s5p, s5_vjp = _s5_tables(*[full[n] for n in ("s5_lambda_re", "s5_lambda_im", "s5_log_dt", "s5_b_re", "s5_b_im",
                                                   "s5_c_re", "s5_c_im", "s5_d")])
    mlw = _ml_weights(*[full[n] for n in ("ml_conv_w", "ml_conv_b", "ml_wq", "ml_wk", "ml_wv", "ml_w_gate",
                                           "ml_b_gate", "ml_norm", "ml_skip")])
    got_a = _exchange_wait(ga, [xn0, s5p["wbr"], s5p["wcr"], s5p["pr"], mlw["wq"], mlw["wk"], mlw["wv"], mlw["wgq"]])
    fwd_a = _sibling_forward_start(got_a[0], name="gather_a2")
    got_a = [_sibling_forward_wait(fwd_a, fwd_a["token"])]
    gb = _exchange_start([local["s5_glu_w"], local["ab_w_out"]], ["ag", "ag"], name="gather_b", dep=got_a[0])
    gc = _exchange_start([local["ssd_w_in"], local["ssd_w_out"]], ["ag", "ag"], name="gather_c", dep=gb["token"])
    full["ab_w_in"] = assemble_big("ab_w_in", got_a[0])
    cuts0 = (0, S5_WIDTH, 2 * S5_WIDTH, 2 * S5_WIDTH + ML_WIDTH, 2 * (S5_WIDTH + ML_WIDTH))
    w_in0 = [full["ab_w_in"][:, cuts0[i]:cuts0[i + 1]] for i in range(4)]

    u, za, xb, zb = [_mm(xn0, wi, "NN", name=f"in0_{i}") for i, wi in enumerate(w_in0)]
    got_b = []

    def glu_w_after(scan_out):
        got_b.extend(_exchange_wait(gb, scan_out))
        return assemble_big("s5_glu_w", got_b[0])

    sv5 = _s5_layer_fwd(u, s5p, glu_w_after, bsz, nc)
    glu_w = assemble_big("s5_glu_w", got_b[0])
    w_out0 = assemble_big("ab_w_out", got_b[1])
    w_out0 = [w_out0[:S5_WIDTH], w_out0[S5_WIDTH:]]
    ya = _s5_post(sv5["y1"], sv5["glu_pre"], glu_b, za)
    yb, svm = _ml_layer_fwd(xb, zb, mlw, bsz, nc)
    h1 = _mm(ya, w_out0[0], "NN", name="out0_a", add=h0)
    h1 = _mm(yb, w_out0[1], "NN", name="out0_b", add=h1)
    got_c = _exchange_wait(gc, h1)
    w_in1, w_out1 = assemble_big("ssd_w_in", got_c[0]), assemble_big("ssd_w_out", got_c[1])
    cuts1 = (0, SSD_INNER, 2 * SSD_INNER, 2 * SSD_INNER + SSD_BC, 2 * SSD_INNER + 2 * SSD_BC)
    w_in1 = [w_in1[:, cuts1[i]:cuts1[i + 1]] for i in range(4)] + [_pad_lanes(w_in1[:, cuts1[4]:])]
    xn1 = _rms_fwd(h1, full["ssd_norm"], name="rms1")
    z1, xs_in, bm_in, cm_in, dt_rows = [_mm(xn1, wi, "NN", name=f"in1_{i}") for i, wi in enumerate(w_in1)]
    ssdw = _ssd_weights(*[full[n] for n in ("ssd_conv_w", "ssd_conv_b", "ssd_dt_bias", "ssd_a_log", "ssd_d",
                                             "ssd_gnorm")])
    yn, svs = _ssd_layer_fwd(z1, xs_in, bm_in, cm_in, dt_rows, ssdw, bsz, nc)
    h2 = _mm(yn, w_out1, "NN", name="out1", add=h1)
    loss_part, dh2, dfinal, dh2_b = _final_loss(h2, full["final_norm"], loss_target, bsz, nc)

    g = {"final_norm": dfinal}
    dyn = _mm(dh2_b, w_out1, "NT", name="d_out1")
    g["ssd_w_out"] = _mm(yn, dh2_b, "TN", name="dw_out1", out_dtype=BF16)
    dz1, dxs, dbm, dcm, ddt, gs = _ssd_layer_bwd(dyn, z1, xs_in, bm_in, cm_in, svs, ssdw, bsz, nc)
    g.update(gs)
    dps1 = (dz1, dxs, dbm, dcm, ddt)
    dxn1 = None
    for i, (dp, wi) in enumerate(zip(dps1, w_in1)):
        dxn1 = _mm(dp, wi, "NT", name=f"d_in1_{i}", add=dxn1)
    dw1 = [_mm(xn1, dp, "TN", name=f"dw_in1_{i}", out_dtype=BF16) for i, dp in enumerate(dps1)]
    g["ssd_w_in"] = jnp.concatenate(dw1[:4] + [dw1[4][:, :SSD_HEADS]], axis=1)

    def local_shape(n):
        return w[n].shape

    def slabs(n):
        gf = g[n].reshape((1,) + tuple(g[n].shape)) if n != "meta_tokens" else g[n]
        full_shape = tuple(d * (N_DEV if i == SHARD_AXIS[n] else 1) for i, d in enumerate(local_shape(n)))
        return _split(gf.reshape(full_shape), SHARD_AXIS[n])

    x1 = _exchange_start([slabs("ssd_w_in")[:, 0], slabs("ssd_w_out")[:, 0]], ["a2a", "a2a"], name="grads_1")
    dh1, g["ssd_norm"], dh1_b = _rms_bwd(h1, full["ssd_norm"], dxn1, dh2, name="rms1_bwd", dep=x1["token"])
    dya = _mm(dh1_b, w_out0[0], "NT", name="d_out0_a")
    dyb = _mm(dh1_b, w_out0[1], "NT", name="d_out0_b")
    g["ab_w_out"] = jnp.concatenate([_mm(ya, dh1_b, "TN", name="dw_out0_a", out_dtype=BF16),
                                     _mm(yb, dh1_b, "TN", name="dw_out0_b", out_dtype=BF16)], axis=0)
    du, dza, g5 = _s5_layer_bwd(dya, u, za, sv5, s5p, s5_vjp, glu_w, glu_b, bsz, nc)
    g.update(g5)
    x2 = _exchange_start([slabs("ab_w_out")[:, 0], _bf(slabs("s5_glu_w")[:, 0])], ["a2a", "a2a"], name="grads_2")
    dxb, dzb, gm = _ml_layer_bwd(dyb, xb, zb, svm, mlw, bsz, nc, dep=x2["token"])
    g.update(gm)
    dps0 = (du, dza, dxb, dzb)
    dw0 = [_mm(xn0, dp, "TN", name=f"dw_in0_{i}", out_dtype=BF16, tn=S5_WIDTH, slabs=True) for i, dp in enumerate(dps0)]
    dw_in0_slabs = jnp.concatenate(dw0, axis=0)
    x3 = _exchange_start([dw_in0_slabs], ["a2a"], name="grads_3")
    dxn0 = None
    for i, (dp, wi) in enumerate(zip(dps0, w_in0)):
        dxn0 = _mm(dp, wi, "NT", name=f"d_in0_{i}", add=dxn0, dep=x3["token"] if i == 0 else None)
    grad_x, d_chunk0, g["ab_norm"] = _rms_bwd_first(h0, full["ab_norm"], dxn0, dh1, bsz, nc, name="rms0_bwd")
    g["meta_tokens"] = jnp.sum(d_chunk0[:, PAD_ROWS:], axis=0)

    small_g = _pack([(slabs(n), 1) for n in SMALL])
    repl_g = _pack([(g[n], 0) for n in REPL])
    x4 = _exchange_start([small_g, repl_g, loss_part], ["a2a", "ag", "ag"], name="grads_4")

    def update_big(n, gp):
        return _adamw(w[n][0], mom[n][0], var[n][0], gp, name=f"adamw_{n}")

    res = {}
    ex1 = _exchange_wait(x1, x4["token"])
    res["ssd_w_in"], res["ssd_w_out"] = update_big("ssd_w_in", ex1[0]), update_big("ssd_w_out", ex1[1])
    ex2 = _exchange_wait(x2, res["ssd_w_out"][0])
    res["ab_w_out"], res["s5_glu_w"] = update_big("ab_w_out", ex2[0]), update_big("s5_glu_w", ex2[1])
    ex3 = _exchange_wait(x3, [res[n][0] for n in ("ssd_w_in", "ssd_w_out", "ab_w_out", "s5_glu_w")])
    res["ab_w_in"] = update_big("ab_w_in", ex3[0])
    ex4 = _exchange_wait(x4, res["ab_w_in"][0])
    loss = jnp.sum(ex4[2][:, 0, 0])
    for names, gp, tag in ((SMALL, ex4[0], "small"), (REPL, ex4[1], "repl")):
        shapes = [local_shape(n) for n in names]
        packs = [_pack([(d[n], 0) for n in names]) for d in (w, mom, var)]
        outs = _adamw(packs[0], packs[1], packs[2], gp, name=f"adamw_{tag}")
        for k, o in enumerate(outs):
            for n, a in zip(names, _unpack(o, shapes)):
                res.setdefault(n, [None] * 4)[k] = a
    outs = [loss, grad_x]
    for k in range(4):
        outs += [res[n][k].reshape(local_shape(n)) for n in WNAMES]
    return tuple(outs)
```
